```python
import math
import jax
import jax.numpy as jnp
from jax import lax
import numpy as np

D_MODEL = 1024
BATCH = 8
SEQ = 4096
DEPTH = 2

GRID_W = 64
CTX_LEN = 256
N_MIXERS = 2
N_SSD_LAYERS = (DEPTH + N_MIXERS - 1) // N_MIXERS
N_GM_LAYERS = DEPTH // N_MIXERS
N_MOD = 9
MACARON_W = 0.5
EPS = 1e-6

FFN_DIM = 2816

SSD_INNER = 2 * D_MODEL
SSD_HEAD_DIM = 64
SSD_HEADS = SSD_INNER // SSD_HEAD_DIM
SSD_GROUPS = 8
SSD_HPG = SSD_HEADS // SSD_GROUPS
SSD_STATE = 128
SSD_CONV = 5
SSD_CHUNK = 128
SSD_CONV_DIM = SSD_INNER + 2 * SSD_GROUPS * SSD_STATE
SSD_PROJ = SSD_INNER + SSD_CONV_DIM + 2 * SSD_HEADS

GM_CHUNK = 128
GM_INNER = 2 * D_MODEL
GM_GROUPS = 8
GM_GROUP_DIM = GM_INNER // GM_GROUPS

kernel_name = 'hybrid_ssd_gmlp_macaron_dit_block'


def rms_norm(x, g):
    xf = x.astype(jnp.float32)
    y = xf * lax.rsqrt(jnp.mean(xf * xf, axis=-1, keepdims=True) + EPS)
    return (y * g.astype(jnp.float32)).astype(x.dtype)


def layer_norm(x, g, b):
    xf = x.astype(jnp.float32)
    mu = jnp.mean(xf, axis=-1, keepdims=True)
    var = jnp.mean(jnp.square(xf - mu), axis=-1, keepdims=True)
    y = (xf - mu) * lax.rsqrt(var + EPS) * g.astype(jnp.float32) + b.astype(jnp.float32)
    return y.astype(x.dtype)


def sublayer_in(h, g_pre, shift, scale):
    return rms_norm(h, g_pre) * (1 + scale) + shift


def sublayer_out(h, y, g_post, gate, weight):
    return h + weight * gate * rms_norm(y, g_post)


def swiglu(h, w_in, w_out):
    gate, up = jnp.split(h @ w_in, 2, axis=-1)
    return (jax.nn.silu(gate) * up) @ w_out


def depthwise_conv(u, w, b):
    y = lax.conv_general_dilated(
        u, w[:, None, :].astype(u.dtype), window_strides=(1,),
        padding=[(SSD_CONV // 2, SSD_CONV // 2)],
        dimension_numbers=('NWC', 'WIO', 'NWC'),
        feature_group_count=u.shape[-1])
    return y + b


def ssd_chunked(xh, dt, A, Bm, Cm, h0):
    b, l, g, k, p = xh.shape
    n = Bm.shape[-1]
    q = SSD_CHUNK
    nc = l // q
    xc = xh.reshape(b, nc, q, g, k, p)
    dtc = dt.reshape(b, nc, q, g, k)
    Bc = Bm.reshape(b, nc, q, g, n)
    Cc = Cm.reshape(b, nc, q, g, n)
    a_cum = jnp.cumsum(dtc * A, axis=2)
    xdt = xc * dtc[..., None]
    tri = jnp.tril(jnp.ones((q, q), dtype=bool))[None, None, :, :, None, None]
    seg = a_cum[:, :, :, None] - a_cum[:, :, None, :]
    decay = jnp.exp(jnp.where(tri, seg, -jnp.inf))
    cb = jnp.einsum('bcign,bcjgn->bcijg', Cc, Bc)
    y_diag = jnp.einsum('bcijg,bcijgk,bcjgkp->bcigkp', cb, decay, xdt)
    decay_end = jnp.exp(a_cum[:, :, -1:] - a_cum)
    states = jnp.einsum('bcqgn,bcqgk,bcqgkp->bcgkpn', Bc, decay_end, xdt)
    chunk_decay = jnp.exp(a_cum[:, :, -1])

    def step(h, inp):
        s, d = inp
        return h * d[..., None, None] + s, h

    h_last, h_start = lax.scan(step, h0, (jnp.moveaxis(states, 1, 0), jnp.moveaxis(chunk_decay, 1, 0)))
    h_start = jnp.moveaxis(h_start, 0, 1)
    y_off = jnp.einsum('bcign,bcgkpn,bcigk->bcigkp', Cc, h_start, jnp.exp(a_cum))
    return (y_diag + y_off).reshape(b, l, g, k, p), h_last


def ssd_branch(u, h0f, h0b, w_in, conv_w, conv_b, dt_bias, A_log, D_skip, norm_g, w_out):
    bsz, l, _ = u.shape
    proj = u @ w_in
    z = proj[..., :SSD_INNER]
    xbc = proj[..., SSD_INNER:SSD_INNER + SSD_CONV_DIM]
    dt_raw = proj[..., SSD_INNER + SSD_CONV_DIM:]
    xbc = jax.nn.silu(depthwise_conv(xbc, conv_w, conv_b)).astype(jnp.float32)
    gn = SSD_GROUPS * SSD_STATE
    xs = xbc[..., :SSD_INNER].reshape(bsz, l, SSD_GROUPS, SSD_HPG, SSD_HEAD_DIM)
    Bm = xbc[..., SSD_INNER:SSD_INNER + gn].reshape(bsz, l, SSD_GROUPS, SSD_STATE)
    Cm = xbc[..., SSD_INNER + gn:].reshape(bsz, l, SSD_GROUPS, SSD_STATE)
    dt = jax.nn.softplus(dt_raw.astype(jnp.float32).reshape(bsz, l, 2, SSD_GROUPS, SSD_HPG)
                         + dt_bias.astype(jnp.float32).reshape(2, SSD_GROUPS, SSD_HPG))
    A = -jnp.exp(A_log.astype(jnp.float32)).reshape(2, SSD_GROUPS, SSD_HPG)
    rev = lambda t: jnp.flip(t, axis=1)
    yf, hf = ssd_chunked(xs, dt[:, :, 0], A[0], Bm, Cm, h0f)
    yb, hb = ssd_chunked(rev(xs), rev(dt[:, :, 1]), A[1], rev(Bm), rev(Cm), h0b)
    y = yf + rev(yb) + D_skip.astype(jnp.float32).reshape(SSD_GROUPS, SSD_HPG, 1) * xs
    y = y.reshape(bsz, l, SSD_INNER) * jax.nn.silu(z.astype(jnp.float32))
    y = y.reshape(bsz, l, SSD_GROUPS, SSD_INNER // SSD_GROUPS)
    y = y * lax.rsqrt(jnp.mean(y * y, axis=-1, keepdims=True) + EPS)
    y = y.reshape(bsz, l, SSD_INNER) * norm_g.astype(jnp.float32)
    return y.astype(u.dtype) @ w_out, hf, hb


def gmlp_branch(u, w_in, v_g, v_b, w_s, b_s, w_out):
    bsz, l, _ = u.shape
    gu, gv = jnp.split(jax.nn.gelu(u @ w_in), 2, axis=-1)
    gv = layer_norm(gv, v_g, v_b).reshape(bsz, l // GM_CHUNK, GM_CHUNK, GM_GROUPS, GM_GROUP_DIM)
    s = jnp.einsum('gij,bcjgd->bcigd', w_s, gv) + b_s.T[:, :, None]
    return (gu * s.reshape(bsz, l, GM_INNER)) @ w_out


def _fwd_setup_inputs(seed: int = 0) -> dict:
    key = jax.random.key(seed)
    ks = jax.random.split(key, 24)
    D = D_MODEL

    def nrm(k, shape, scale):
        return jax.random.normal(k, shape, jnp.float32) * scale

    lo, hi = math.log(1e-3), math.log(1e-1)
    dt0 = jnp.exp(jax.random.uniform(ks[12], (N_SSD_LAYERS, 2, SSD_HEADS), jnp.float32, lo, hi))
    return {
        'x': nrm(ks[0], (BATCH, SEQ, D), 1.0),
        'c': nrm(ks[1], (BATCH, D), 1.0),
        'ctx': nrm(ks[2], (BATCH, CTX_LEN, D), 1.0),
        'c_ctx': nrm(ks[3], (D,), 1.0),
        'ada_w': nrm(ks[4], (DEPTH, D, N_MOD * D), 0.5 * D ** -0.5),
        'ada_b': nrm(ks[5], (DEPTH, N_MOD * D), 0.02),
        'norm_g': 1.0 + nrm(ks[6], (DEPTH, 6, D), 0.02),
        'ffn_w_in': nrm(ks[7], (DEPTH, 2, D, 2 * FFN_DIM), D ** -0.5),
        'ffn_w_out': nrm(ks[8], (DEPTH, 2, FFN_DIM, D), FFN_DIM ** -0.5),
        'ssd_w_in': nrm(ks[9], (N_SSD_LAYERS, D, SSD_PROJ), D ** -0.5),
        'ssd_conv_w': nrm(ks[10], (N_SSD_LAYERS, SSD_CONV, SSD_CONV_DIM), SSD_CONV ** -0.5),
        'ssd_conv_b': nrm(ks[11], (N_SSD_LAYERS, SSD_CONV_DIM), 0.02),
        'ssd_dt_bias': dt0 + jnp.log(-jnp.expm1(-dt0)),
        'ssd_A_log': jnp.log(jax.random.uniform(ks[13], (N_SSD_LAYERS, 2, SSD_HEADS), jnp.float32, 1.0, 16.0)),
        'ssd_D': 1.0 + nrm(ks[14], (N_SSD_LAYERS, SSD_HEADS), 0.02),
        'ssd_norm_g': 1.0 + nrm(ks[15], (N_SSD_LAYERS, SSD_INNER), 0.02),
        'ssd_w_out': nrm(ks[16], (N_SSD_LAYERS, SSD_INNER, D), SSD_INNER ** -0.5),
        'gm_w_in': nrm(ks[17], (N_GM_LAYERS, D, 2 * GM_INNER), D ** -0.5),
        'gm_v_g': 1.0 + nrm(ks[18], (N_GM_LAYERS, GM_INNER), 0.02),
        'gm_v_b': nrm(ks[19], (N_GM_LAYERS, GM_INNER), 0.02),
        'gm_w_s': nrm(ks[20], (N_GM_LAYERS, GM_GROUPS, GM_CHUNK, GM_CHUNK), GM_CHUNK ** -0.5),
        'gm_b_s': 1.0 + nrm(ks[21], (N_GM_LAYERS, GM_GROUPS, GM_CHUNK), 0.02),
        'gm_w_out': nrm(ks[22], (N_GM_LAYERS, GM_INNER, D), GM_INNER ** -0.5),
    }


def _fwd_reference(x, c, ctx, c_ctx, ada_w, ada_b, norm_g, ffn_w_in, ffn_w_out,
              ssd_w_in, ssd_conv_w, ssd_conv_b, ssd_dt_bias, ssd_A_log, ssd_D, ssd_norm_g, ssd_w_out,
              gm_w_in, gm_v_g, gm_v_b, gm_w_s, gm_b_s, gm_w_out):
    bsz = x.shape[0]
    silu_c = jax.nn.silu(c)
    silu_cc = jax.nn.silu(c_ctx)
    for i in range(DEPTH):
        use_ssd = (i % N_MIXERS) == 0
        j = i // N_MIXERS
        last = i == DEPTH - 1
        ctx_needed = (not last) or use_ssd
        ctx_full = not last
        mx = jnp.split((silu_c @ ada_w[i] + ada_b[i])[:, None, :], N_MOD, axis=-1)
        mc = jnp.split(silu_cc @ ada_w[i] + ada_b[i], N_MOD, axis=-1)
        g = norm_g[i]

        x = sublayer_out(x, swiglu(sublayer_in(x, g[0], mx[0], mx[1]), ffn_w_in[i, 0], ffn_w_out[i, 0]),
                         g[1], mx[2], MACARON_W)
        if ctx_needed:
            ctx = sublayer_out(ctx, swiglu(sublayer_in(ctx, g[0], mc[0], mc[1]), ffn_w_in[i, 0], ffn_w_out[i, 0]),
                               g[1], mc[2], MACARON_W)

        xm = sublayer_in(x, g[2], mx[3], mx[4])
        if use_ssd:
            cm = sublayer_in(ctx, g[2], mc[3], mc[4])
            h0 = jnp.zeros((bsz, SSD_GROUPS, SSD_HPG, SSD_HEAD_DIM, SSD_STATE), jnp.float32)
            y_ctx, hf, hb = ssd_branch(cm, h0, h0, ssd_w_in[j], ssd_conv_w[j], ssd_conv_b[j], ssd_dt_bias[j],
                                       ssd_A_log[j], ssd_D[j], ssd_norm_g[j], ssd_w_out[j])
            y_x, _, _ = ssd_branch(xm, hf, hb, ssd_w_in[j], ssd_conv_w[j], ssd_conv_b[j], ssd_dt_bias[j],
                                   ssd_A_log[j], ssd_D[j], ssd_norm_g[j], ssd_w_out[j])
        else:
            y_x = gmlp_branch(xm, gm_w_in[j], gm_v_g[j], gm_v_b[j], gm_w_s[j], gm_b_s[j], gm_w_out[j])
            if ctx_full:
                cm = sublayer_in(ctx, g[2], mc[3], mc[4])
                y_ctx = gmlp_branch(cm, gm_w_in[j], gm_v_g[j], gm_v_b[j], gm_w_s[j], gm_b_s[j], gm_w_out[j])
        x = sublayer_out(x, y_x, g[3], mx[5], 1.0)

        x = sublayer_out(x, swiglu(sublayer_in(x, g[4], mx[6], mx[7]), ffn_w_in[i, 1], ffn_w_out[i, 1]),
                         g[5], mx[8], MACARON_W)
        if ctx_full:
            ctx = sublayer_out(ctx, y_ctx, g[3], mc[5], 1.0)
            ctx = sublayer_out(ctx, swiglu(sublayer_in(ctx, g[4], mc[6], mc[7]), ffn_w_in[i, 1], ffn_w_out[i, 1]),
                               g[5], mc[8], MACARON_W)
    return x


import jax as _jax
import jax.numpy as _jnp

TWIN_FORMAT = 'train_step'
FWD_PARAMS = ['x', 'c', 'ctx', 'c_ctx', 'ada_w', 'ada_b', 'norm_g', 'ffn_w_in', 'ffn_w_out', 'ssd_w_in', 'ssd_conv_w', 'ssd_conv_b', 'ssd_dt_bias', 'ssd_A_log', 'ssd_D', 'ssd_norm_g', 'ssd_w_out', 'gm_w_in', 'gm_v_g', 'gm_v_b', 'gm_w_s', 'gm_b_s', 'gm_w_out']
TWIN_WEIGHTS = ['c_ctx', 'ada_w', 'ada_b', 'norm_g', 'ffn_w_in', 'ffn_w_out', 'ssd_w_in', 'ssd_conv_w', 'ssd_conv_b', 'ssd_dt_bias', 'ssd_A_log', 'ssd_D', 'ssd_norm_g', 'ssd_w_out', 'gm_w_in', 'gm_v_g', 'gm_v_b', 'gm_w_s', 'gm_b_s', 'gm_w_out']
TWIN_DIFF_INPUT = 'x'
TWIN_INPUTS = ['x', 'c', 'ctx', 'c_ctx', 'ada_w', 'ada_b', 'norm_g', 'ffn_w_in', 'ffn_w_out', 'ssd_w_in', 'ssd_conv_w', 'ssd_conv_b', 'ssd_dt_bias', 'ssd_A_log', 'ssd_D', 'ssd_norm_g', 'ssd_w_out', 'gm_w_in', 'gm_v_g', 'gm_v_b', 'gm_w_s', 'gm_b_s', 'gm_w_out', 'loss_target', 'm_c_ctx', 'm_ada_w', 'm_ada_b', 'm_norm_g', 'm_ffn_w_in', 'm_ffn_w_out', 'm_ssd_w_in', 'm_ssd_conv_w', 'm_ssd_conv_b', 'm_ssd_dt_bias', 'm_ssd_A_log', 'm_ssd_D', 'm_ssd_norm_g', 'm_ssd_w_out', 'm_gm_w_in', 'm_gm_v_g', 'm_gm_v_b', 'm_gm_w_s', 'm_gm_b_s', 'm_gm_w_out', 'v_c_ctx', 'v_ada_w', 'v_ada_b', 'v_norm_g', 'v_ffn_w_in', 'v_ffn_w_out', 'v_ssd_w_in', 'v_ssd_conv_w', 'v_ssd_conv_b', 'v_ssd_dt_bias', 'v_ssd_A_log', 'v_ssd_D', 'v_ssd_norm_g', 'v_ssd_w_out', 'v_gm_w_in', 'v_gm_v_g', 'v_gm_v_b', 'v_gm_w_s', 'v_gm_b_s', 'v_gm_w_out']
TWIN_OUTPUTS = ['loss', 'grad_x', 'grad_c_ctx', 'grad_ada_w', 'grad_ada_b', 'grad_norm_g', 'grad_ffn_w_in', 'grad_ffn_w_out', 'grad_ssd_w_in', 'grad_ssd_conv_w', 'grad_ssd_conv_b', 'grad_ssd_dt_bias', 'grad_ssd_A_log', 'grad_ssd_D', 'grad_ssd_norm_g', 'grad_ssd_w_out', 'grad_gm_w_in', 'grad_gm_v_g', 'grad_gm_v_b', 'grad_gm_w_s', 'grad_gm_b_s', 'grad_gm_w_out', 'delta_c_ctx', 'delta_ada_w', 'delta_ada_b', 'delta_norm_g', 'delta_ffn_w_in', 'delta_ffn_w_out', 'delta_ssd_w_in', 'delta_ssd_conv_w', 'delta_ssd_conv_b', 'delta_ssd_dt_bias', 'delta_ssd_A_log', 'delta_ssd_D', 'delta_ssd_norm_g', 'delta_ssd_w_out', 'delta_gm_w_in', 'delta_gm_v_g', 'delta_gm_v_b', 'delta_gm_w_s', 'delta_gm_b_s', 'delta_gm_w_out', 'new_m_c_ctx', 'new_m_ada_w', 'new_m_ada_b', 'new_m_norm_g', 'new_m_ffn_w_in', 'new_m_ffn_w_out', 'new_m_ssd_w_in', 'new_m_ssd_conv_w', 'new_m_ssd_conv_b', 'new_m_ssd_dt_bias', 'new_m_ssd_A_log', 'new_m_ssd_D', 'new_m_ssd_norm_g', 'new_m_ssd_w_out', 'new_m_gm_w_in', 'new_m_gm_v_g', 'new_m_gm_v_b', 'new_m_gm_w_s', 'new_m_gm_b_s', 'new_m_gm_w_out', 'new_v_c_ctx', 'new_v_ada_w', 'new_v_ada_b', 'new_v_norm_g', 'new_v_ffn_w_in', 'new_v_ffn_w_out', 'new_v_ssd_w_in', 'new_v_ssd_conv_w', 'new_v_ssd_conv_b', 'new_v_ssd_dt_bias', 'new_v_ssd_A_log', 'new_v_ssd_D', 'new_v_ssd_norm_g', 'new_v_ssd_w_out', 'new_v_gm_w_in', 'new_v_gm_v_g', 'new_v_gm_v_b', 'new_v_gm_w_s', 'new_v_gm_b_s', 'new_v_gm_w_out']
TWIN_LEAF_KINDS = {'loss': 'loss', 'grad_x': 'grad_x', 'grad_c_ctx': 'grad_w', 'grad_ada_w': 'grad_w', 'grad_ada_b': 'grad_w', 'grad_norm_g': 'grad_w', 'grad_ffn_w_in': 'grad_w', 'grad_ffn_w_out': 'grad_w', 'grad_ssd_w_in': 'grad_w', 'grad_ssd_conv_w': 'grad_w', 'grad_ssd_conv_b': 'grad_w', 'grad_ssd_dt_bias': 'grad_w', 'grad_ssd_A_log': 'grad_w', 'grad_ssd_D': 'grad_w', 'grad_ssd_norm_g': 'grad_w', 'grad_ssd_w_out': 'grad_w', 'grad_gm_w_in': 'grad_w', 'grad_gm_v_g': 'grad_w', 'grad_gm_v_b': 'grad_w', 'grad_gm_w_s': 'grad_w', 'grad_gm_b_s': 'grad_w', 'grad_gm_w_out': 'grad_w', 'delta_c_ctx': 'delta_w', 'delta_ada_w': 'delta_w', 'delta_ada_b': 'delta_w', 'delta_norm_g': 'delta_w', 'delta_ffn_w_in': 'delta_w', 'delta_ffn_w_out': 'delta_w', 'delta_ssd_w_in': 'delta_w', 'delta_ssd_conv_w': 'delta_w', 'delta_ssd_conv_b': 'delta_w', 'delta_ssd_dt_bias': 'delta_w', 'delta_ssd_A_log': 'delta_w', 'delta_ssd_D': 'delta_w', 'delta_ssd_norm_g': 'delta_w', 'delta_ssd_w_out': 'delta_w', 'delta_gm_w_in': 'delta_w', 'delta_gm_v_g': 'delta_w', 'delta_gm_v_b': 'delta_w', 'delta_gm_w_s': 'delta_w', 'delta_gm_b_s': 'delta_w', 'delta_gm_w_out': 'delta_w', 'new_m_c_ctx': 'new_m', 'new_m_ada_w': 'new_m', 'new_m_ada_b': 'new_m', 'new_m_norm_g': 'new_m', 'new_m_ffn_w_in': 'new_m', 'new_m_ffn_w_out': 'new_m', 'new_m_ssd_w_in': 'new_m', 'new_m_ssd_conv_w': 'new_m', 'new_m_ssd_conv_b': 'new_m', 'new_m_ssd_dt_bias': 'new_m', 'new_m_ssd_A_log': 'new_m', 'new_m_ssd_D': 'new_m', 'new_m_ssd_norm_g': 'new_m', 'new_m_ssd_w_out': 'new_m', 'new_m_gm_w_in': 'new_m', 'new_m_gm_v_g': 'new_m', 'new_m_gm_v_b': 'new_m', 'new_m_gm_w_s': 'new_m', 'new_m_gm_b_s': 'new_m', 'new_m_gm_w_out': 'new_m', 'new_v_c_ctx': 'new_v', 'new_v_ada_w': 'new_v', 'new_v_ada_b': 'new_v', 'new_v_norm_g': 'new_v', 'new_v_ffn_w_in': 'new_v', 'new_v_ffn_w_out': 'new_v', 'new_v_ssd_w_in': 'new_v', 'new_v_ssd_conv_w': 'new_v', 'new_v_ssd_conv_b': 'new_v', 'new_v_ssd_dt_bias': 'new_v', 'new_v_ssd_A_log': 'new_v', 'new_v_ssd_D': 'new_v', 'new_v_ssd_norm_g': 'new_v', 'new_v_ssd_w_out': 'new_v', 'new_v_gm_w_in': 'new_v', 'new_v_gm_v_g': 'new_v', 'new_v_gm_v_b': 'new_v', 'new_v_gm_w_s': 'new_v', 'new_v_gm_b_s': 'new_v', 'new_v_gm_w_out': 'new_v'}


def _forward(args):
    return _fwd_reference(*[args[k] for k in FWD_PARAMS])


def _output_shape():
    out = _jax.eval_shape(lambda: _forward(_fwd_setup_inputs(0)))
    return out.shape, out.dtype

N_MICROBATCH = 1
ADAM_LR = 0.001
ADAM_B1 = 0.9
ADAM_B2 = 0.999
ADAM_EPS = 1e-08
ADAM_WD = 0.01
ADAM_STEP = 10
PER_EXAMPLE_BATCH_AXIS = {'x': 0, 'c': 0, 'ctx': 0, 'loss_target': 0}
SHARED_INPUTS = []
_WEIGHT_DTYPES = {'c_ctx': _jnp.float32, 'ada_w': _jnp.float32, 'ada_b': _jnp.float32, 'norm_g': _jnp.float32, 'ffn_w_in': _jnp.float32, 'ffn_w_out': _jnp.float32, 'ssd_w_in': _jnp.float32, 'ssd_conv_w': _jnp.float32, 'ssd_conv_b': _jnp.float32, 'ssd_dt_bias': _jnp.float32, 'ssd_A_log': _jnp.float32, 'ssd_D': _jnp.float32, 'ssd_norm_g': _jnp.float32, 'ssd_w_out': _jnp.float32, 'gm_w_in': _jnp.float32, 'gm_v_g': _jnp.float32, 'gm_v_b': _jnp.float32, 'gm_w_s': _jnp.float32, 'gm_b_s': _jnp.float32, 'gm_w_out': _jnp.float32}
MOMENT_SCALE = {'c_ctx': 5.695082e-03, 'ada_w': 1.059623e+00, 'ada_b': 1.999848e+00, 'norm_g': 1.593885e+00, 'ffn_w_in': 3.358760e-02, 'ffn_w_out': 5.926569e-02, 'ssd_w_in': 7.893752e-02, 'ssd_conv_w': 8.756855e-02, 'ssd_conv_b': 1.838247e-01, 'ssd_dt_bias': 1.048654e-01, 'ssd_A_log': 2.293689e-01, 'ssd_D': 2.420566e-01, 'ssd_norm_g': 1.541952e-01, 'ssd_w_out': 2.032974e-01, 'gm_w_in': 7.782944e-02, 'gm_v_g': 5.003402e-02, 'gm_v_b': 4.981029e-02, 'gm_w_s': 6.146208e-02, 'gm_b_s': 6.074697e-02, 'gm_w_out': 2.542183e-01}


def _to_microbatches(a, axis):
    t = _jnp.moveaxis(a, axis, 0)
    t = t.reshape((N_MICROBATCH, t.shape[0] // N_MICROBATCH) + t.shape[1:])
    return _jnp.moveaxis(t, 1, axis + 1)


def setup_inputs(seed: int = 0) -> dict:
    inp = _fwd_setup_inputs(seed)
    key = _jax.random.fold_in(_jax.random.key(seed), 7919)
    shape, _ = _output_shape()
    out = dict(inp)
    out["loss_target"] = _jax.random.normal(_jax.random.fold_in(key, 0), shape, _jnp.float32)
    for i, name in enumerate(TWIN_WEIGHTS):
        w = inp[name].astype(_jnp.float32)
        if MOMENT_SCALE is None:
            s = _jnp.sqrt(_jnp.mean(_jnp.square(w)) + 1e-30)
        else:
            s = MOMENT_SCALE[name]
        km, kv = _jax.random.split(_jax.random.fold_in(key, i + 1))
        out[name] = w
        out["m_" + name] = s * _jax.random.normal(km, w.shape, _jnp.float32)
        out["v_" + name] = (s * s) * _jax.random.uniform(kv, w.shape, _jnp.float32, 0.5, 1.5)
    if N_MICROBATCH > 1:
        for name, axis in PER_EXAMPLE_BATCH_AXIS.items():
            out[name] = _to_microbatches(out[name], axis)
    return {'x': out['x'], 'c': out['c'], 'ctx': out['ctx'], 'c_ctx': out['c_ctx'], 'ada_w': out['ada_w'], 'ada_b': out['ada_b'], 'norm_g': out['norm_g'], 'ffn_w_in': out['ffn_w_in'], 'ffn_w_out': out['ffn_w_out'], 'ssd_w_in': out['ssd_w_in'], 'ssd_conv_w': out['ssd_conv_w'], 'ssd_conv_b': out['ssd_conv_b'], 'ssd_dt_bias': out['ssd_dt_bias'], 'ssd_A_log': out['ssd_A_log'], 'ssd_D': out['ssd_D'], 'ssd_norm_g': out['ssd_norm_g'], 'ssd_w_out': out['ssd_w_out'], 'gm_w_in': out['gm_w_in'], 'gm_v_g': out['gm_v_g'], 'gm_v_b': out['gm_v_b'], 'gm_w_s': out['gm_w_s'], 'gm_b_s': out['gm_b_s'], 'gm_w_out': out['gm_w_out'], 'loss_target': out['loss_target'], 'm_c_ctx': out['m_c_ctx'], 'm_ada_w': out['m_ada_w'], 'm_ada_b': out['m_ada_b'], 'm_norm_g': out['m_norm_g'], 'm_ffn_w_in': out['m_ffn_w_in'], 'm_ffn_w_out': out['m_ffn_w_out'], 'm_ssd_w_in': out['m_ssd_w_in'], 'm_ssd_conv_w': out['m_ssd_conv_w'], 'm_ssd_conv_b': out['m_ssd_conv_b'], 'm_ssd_dt_bias': out['m_ssd_dt_bias'], 'm_ssd_A_log': out['m_ssd_A_log'], 'm_ssd_D': out['m_ssd_D'], 'm_ssd_norm_g': out['m_ssd_norm_g'], 'm_ssd_w_out': out['m_ssd_w_out'], 'm_gm_w_in': out['m_gm_w_in'], 'm_gm_v_g': out['m_gm_v_g'], 'm_gm_v_b': out['m_gm_v_b'], 'm_gm_w_s': out['m_gm_w_s'], 'm_gm_b_s': out['m_gm_b_s'], 'm_gm_w_out': out['m_gm_w_out'], 'v_c_ctx': out['v_c_ctx'], 'v_ada_w': out['v_ada_w'], 'v_ada_b': out['v_ada_b'], 'v_norm_g': out['v_norm_g'], 'v_ffn_w_in': out['v_ffn_w_in'], 'v_ffn_w_out': out['v_ffn_w_out'], 'v_ssd_w_in': out['v_ssd_w_in'], 'v_ssd_conv_w': out['v_ssd_conv_w'], 'v_ssd_conv_b': out['v_ssd_conv_b'], 'v_ssd_dt_bias': out['v_ssd_dt_bias'], 'v_ssd_A_log': out['v_ssd_A_log'], 'v_ssd_D': out['v_ssd_D'], 'v_ssd_norm_g': out['v_ssd_norm_g'], 'v_ssd_w_out': out['v_ssd_w_out'], 'v_gm_w_in': out['v_gm_w_in'], 'v_gm_v_g': out['v_gm_v_g'], 'v_gm_v_b': out['v_gm_v_b'], 'v_gm_w_s': out['v_gm_w_s'], 'v_gm_b_s': out['v_gm_b_s'], 'v_gm_w_out': out['v_gm_w_out']}


def _loss(weights, diff, rest, loss_target):
    with _jax.named_scope("forward"):
        args = {**rest, TWIN_DIFF_INPUT: diff, **{k: w.astype(_WEIGHT_DTYPES[k]) for k, w in weights.items()}}
        y = _forward(args)
    with _jax.named_scope("loss_head"):
        err = _jnp.square(y.astype(_jnp.float32) - loss_target)
        return 0.5 * _jnp.sum(_jnp.mean(err, axis=-1)) if err.ndim else 0.5 * err


def _adamw(w, g, m, v):
    m = ADAM_B1 * m + (1.0 - ADAM_B1) * g
    v = ADAM_B2 * v + (1.0 - ADAM_B2) * _jnp.square(g)
    m_hat = m / (1.0 - ADAM_B1 ** ADAM_STEP)
    v_hat = v / (1.0 - ADAM_B2 ** ADAM_STEP)
    delta = -ADAM_LR * (m_hat / (_jnp.sqrt(v_hat) + ADAM_EPS) + ADAM_WD * w)
    return delta, m, v


def reference(x, c, ctx, c_ctx, ada_w, ada_b, norm_g, ffn_w_in, ffn_w_out, ssd_w_in, ssd_conv_w, ssd_conv_b, ssd_dt_bias, ssd_A_log, ssd_D, ssd_norm_g, ssd_w_out, gm_w_in, gm_v_g, gm_v_b, gm_w_s, gm_b_s, gm_w_out, loss_target, m_c_ctx, m_ada_w, m_ada_b, m_norm_g, m_ffn_w_in, m_ffn_w_out, m_ssd_w_in, m_ssd_conv_w, m_ssd_conv_b, m_ssd_dt_bias, m_ssd_A_log, m_ssd_D, m_ssd_norm_g, m_ssd_w_out, m_gm_w_in, m_gm_v_g, m_gm_v_b, m_gm_w_s, m_gm_b_s, m_gm_w_out, v_c_ctx, v_ada_w, v_ada_b, v_norm_g, v_ffn_w_in, v_ffn_w_out, v_ssd_w_in, v_ssd_conv_w, v_ssd_conv_b, v_ssd_dt_bias, v_ssd_A_log, v_ssd_D, v_ssd_norm_g, v_ssd_w_out, v_gm_w_in, v_gm_v_g, v_gm_v_b, v_gm_w_s, v_gm_b_s, v_gm_w_out):
    given = dict(x=x, c=c, ctx=ctx, c_ctx=c_ctx, ada_w=ada_w, ada_b=ada_b, norm_g=norm_g, ffn_w_in=ffn_w_in, ffn_w_out=ffn_w_out, ssd_w_in=ssd_w_in, ssd_conv_w=ssd_conv_w, ssd_conv_b=ssd_conv_b, ssd_dt_bias=ssd_dt_bias, ssd_A_log=ssd_A_log, ssd_D=ssd_D, ssd_norm_g=ssd_norm_g, ssd_w_out=ssd_w_out, gm_w_in=gm_w_in, gm_v_g=gm_v_g, gm_v_b=gm_v_b, gm_w_s=gm_w_s, gm_b_s=gm_b_s, gm_w_out=gm_w_out, loss_target=loss_target, m_c_ctx=m_c_ctx, m_ada_w=m_ada_w, m_ada_b=m_ada_b, m_norm_g=m_norm_g, m_ffn_w_in=m_ffn_w_in, m_ffn_w_out=m_ffn_w_out, m_ssd_w_in=m_ssd_w_in, m_ssd_conv_w=m_ssd_conv_w, m_ssd_conv_b=m_ssd_conv_b, m_ssd_dt_bias=m_ssd_dt_bias, m_ssd_A_log=m_ssd_A_log, m_ssd_D=m_ssd_D, m_ssd_norm_g=m_ssd_norm_g, m_ssd_w_out=m_ssd_w_out, m_gm_w_in=m_gm_w_in, m_gm_v_g=m_gm_v_g, m_gm_v_b=m_gm_v_b, m_gm_w_s=m_gm_w_s, m_gm_b_s=m_gm_b_s, m_gm_w_out=m_gm_w_out, v_c_ctx=v_c_ctx, v_ada_w=v_ada_w, v_ada_b=v_ada_b, v_norm_g=v_norm_g, v_ffn_w_in=v_ffn_w_in, v_ffn_w_out=v_ffn_w_out, v_ssd_w_in=v_ssd_w_in, v_ssd_conv_w=v_ssd_conv_w, v_ssd_conv_b=v_ssd_conv_b, v_ssd_dt_bias=v_ssd_dt_bias, v_ssd_A_log=v_ssd_A_log, v_ssd_D=v_ssd_D, v_ssd_norm_g=v_ssd_norm_g, v_ssd_w_out=v_ssd_w_out, v_gm_w_in=v_gm_w_in, v_gm_v_g=v_gm_v_g, v_gm_v_b=v_gm_v_b, v_gm_w_s=v_gm_w_s, v_gm_b_s=v_gm_b_s, v_gm_w_out=v_gm_w_out)
    weights = {n: given[n] for n in TWIN_WEIGHTS}
    shared = {n: given[n] for n in SHARED_INPUTS}
    per_example = {n: given[n] for n in ['x', 'c', 'ctx']}
    grad_fn = _jax.value_and_grad(_loss, argnums=(0, 1))

    def one_microbatch(ex, loss_target):
        ex = dict(ex)
        diff = ex.pop(TWIN_DIFF_INPUT)
        return grad_fn(weights, diff, {**shared, **ex}, loss_target)

    if N_MICROBATCH == 1:
        loss, (grad_w, grad_x) = one_microbatch(per_example, given["loss_target"])
    else:
        def body(carry, xs):
            loss_sum, grad_sum = carry
            l_k, (gw_k, gx_k) = one_microbatch(xs[0], xs[1])
            with _jax.named_scope("update"):
                return (loss_sum + l_k, _jax.tree.map(_jnp.add, grad_sum, gw_k)), gx_k

        init = (_jnp.zeros((), _jnp.float32), _jax.tree.map(_jnp.zeros_like, weights))
        (loss, grad_w), grad_x = _jax.lax.scan(body, init, (per_example, given["loss_target"]))
    with _jax.named_scope("update"):
        delta_w, new_m, new_v = {}, {}, {}
        for n in TWIN_WEIGHTS:
            delta_w[n], new_m[n], new_v[n] = _adamw(weights[n], grad_w[n], given["m_" + n], given["v_" + n])
    return (loss, grad_x, *[grad_w[n] for n in TWIN_WEIGHTS], *[delta_w[n] for n in TWIN_WEIGHTS],
            *[new_m[n] for n in TWIN_WEIGHTS], *[new_v[n] for n in TWIN_WEIGHTS])
```

```python
import functools
import math

import jax
import jax.numpy as jnp
from jax import lax
from jax.experimental import pallas as pl
from jax.experimental.pallas import tpu as pltpu

F32 = jnp.float32
BF16 = jnp.bfloat16

NDEV = 8
D_MODEL = 1024
FFN_DIM = 2816
N_MOD = 9
EPS = 1e-6
SSD_INNER = 2048
SSD_HEADS = 32
SSD_HEAD_DIM = 64
SSD_GROUPS = 8
SSD_HPG = 4
SSD_STATE = 128
SSD_CONV = 5
SSD_CONV_DIM = 4096
CHUNK = 128
GM_INNER = 2048
GM_GROUPS = 8
GM_GROUP_DIM = 256
ADAM_LR = 0.001
ADAM_B1 = 0.9
ADAM_B2 = 0.999
ADAM_EPS = 1e-08
ADAM_WD = 0.01
ADAM_STEP = 10
NEG_BIG = -1e30
VMEM_LIMIT_BYTES = 56 * 1024 * 1024
HI = lax.Precision.HIGHEST


def _params(*sem):
    return pltpu.CompilerParams(dimension_semantics=sem, vmem_limit_bytes=VMEM_LIMIT_BYTES)


def _pick(n, target, mult=16):
    if n <= target:
        return n
    for t in range(target - target % mult, 0, -mult):
        if n % t == 0:
            return t
    raise ValueError((n, target, mult))


def _sig(x):
    return 1.0 / (1.0 + jnp.exp(-x))


def _silu(x):
    return x * _sig(x)


def _dsilu(x):
    s = _sig(x)
    return s * (1.0 + x * (1.0 - s))


_GELU_C = math.sqrt(2.0 / math.pi)


def _gelu(x):
    return 0.5 * x * (1.0 + jnp.tanh(_GELU_C * (x + 0.044715 * x * x * x)))


def _dgelu(x):
    t = jnp.tanh(_GELU_C * (x + 0.044715 * x * x * x))
    return 0.5 * (1.0 + t) + 0.5 * x * (1.0 - t * t) * _GELU_C * (1.0 + 3.0 * 0.044715 * x * x)


def _softplus(x):
    return jnp.maximum(x, 0.0) + jnp.log1p(jnp.exp(-jnp.abs(x)))


def _sum0(v):
    return jnp.sum(v, axis=0, keepdims=True)


def _rms(h):
    r = lax.rsqrt(jnp.mean(h * h, axis=-1, keepdims=True) + EPS)
    return h * r, r


def _dot(a, b, dims=((1,), (0,)), precision=None):
    return lax.dot_general(a, b, (dims, ((), ())), preferred_element_type=F32, precision=precision)


_NT = ((1,), (1,))
_TN = ((0,), (0,))


def _rowwise(name, fn, n_rows, rows, consts, outs, accs=(), *, tm, nseg=1, seg_blocks=0):
    assert n_rows % tm == 0
    if nseg == 2:
        assert seg_blocks > 0
        seg = lambda i: jnp.where(i < seg_blocks, 0, 1)
    else:
        seg = lambda i: 0
    in_specs, args = [], []
    for r in rows:
        arr, width, cb, off = r if isinstance(r, tuple) else (r, r.shape[1], 0, 0)
        in_specs.append(pl.BlockSpec((tm, width), lambda i, cb=cb, off=off: (i + off, cb)))
        args.append(arr)
    for kind, arr in consts:
        if kind == "seg":
            assert arr.shape[0] == nseg and arr.shape[1] == 1, arr.shape
            in_specs.append(pl.BlockSpec((None, 1, arr.shape[2]), lambda i: (seg(i), 0, 0)))
        else:
            in_specs.append(pl.BlockSpec(arr.shape, lambda i: (0, 0)))
        args.append(arr)
    out_shape = [jax.ShapeDtypeStruct((n_rows, w), dt) for w, dt in outs]
    out_specs = [pl.BlockSpec((tm, w), lambda i: (i, 0)) for w, _ in outs]
    out_shape += [jax.ShapeDtypeStruct((nseg, 1, w), F32) for w in accs]
    out_specs += [pl.BlockSpec((None, 1, w), lambda i: (seg(i), 0, 0)) for w in accs]
    n_in, n_out, n_acc = len(args), len(outs), len(accs)

    def kern(*refs):
        ins = [r[...] for r in refs[:n_in]]
        res, sums = fn(*ins)
        for ref, v in zip(refs[n_in:n_in + n_out], res):
            ref[...] = v.astype(ref.dtype)
        if n_acc:
            i = pl.program_id(0)
            first = (i == 0) | (i == seg_blocks) if nseg == 2 else (i == 0)
            acc_refs = refs[n_in + n_out:]

            @pl.when(first)
            def _():
                for ref, v in zip(acc_refs, sums):
                    ref[...] = v

            @pl.when(jnp.logical_not(first))
            def _():
                for ref, v in zip(acc_refs, sums):
                    ref[...] += v

    res = pl.pallas_call(
        kern, name=name, grid=(n_rows // tm,), in_specs=in_specs, out_specs=out_specs, out_shape=out_shape,
        compiler_params=_params("arbitrary"),
    )(*args)
    return res


def _pre_fwd_fn(h, g, shift, scale):
    hh, _ = _rms(h)
    return (hh * g * (1.0 + scale) + shift,), ()


def _pre_bwd_fn(du, h, dres, g, scale):
    hh, r = _rms(h)
    n = hh * g
    dn = du * (1.0 + scale)
    dhh = dn * g
    dh = dres + r * (dhh - hh * jnp.mean(dhh * hh, axis=-1, keepdims=True))
    return (dh,), (_sum0(du), _sum0(du * n), _sum0(dn * hh))


def _post_fwd_fn(weight, h, y, g, gate):
    yh, _ = _rms(y)
    return (h + weight * gate * (yh * g),), ()


def _post_bwd_fn(weight, dh, y, g, gate):
    yh, r = _rms(y)
    dr = dh * weight
    dyh = dr * gate * g
    dy = r * (dyh - yh * jnp.mean(dyh * yh, axis=-1, keepdims=True))
    return (dy,), (_sum0(dr * yh * g), _sum0(dr * gate * yh))


def _glu_bwd_fn(ds, a, b):
    a = a.astype(F32)
    b = b.astype(F32)
    sg = _sig(a)
    da = ds * b * (sg * (1.0 + a * (1.0 - sg)))
    db = ds * (a * sg)
    return (jnp.concatenate([da, db], axis=1),), ()


def _loss_fn(y, t):
    diff = y - t
    return (diff * (1.0 / D_MODEL),), (_sum0(diff * diff),)


def _ssd_y(yf, yb, xs, z, dvec):
    y = yf + yb + dvec * xs
    return y, y * _silu(z)


def _ssdgate_fwd_fn(yf, yb, xs, z, dvec, ng):
    _, yg = _ssd_y(yf, yb, xs, z, dvec)
    parts = []
    for g in range(SSD_GROUPS):
        sl = slice(g * 256, (g + 1) * 256)
        parts.append(_rms(yg[:, sl])[0])
    return (jnp.concatenate(parts, axis=1) * ng,), ()


def _ssdgate_bwd_fn(dyn, yf, yb, xs, z, dvec, ng):
    y, yg = _ssd_y(yf, yb, xs, z, dvec)
    dyg_parts, ygh_parts = [], []
    for g in range(SSD_GROUPS):
        sl = slice(g * 256, (g + 1) * 256)
        ygh, r = _rms(yg[:, sl])
        d = dyn[:, sl] * ng[:, sl]
        dyg_parts.append(r * (d - ygh * jnp.mean(d * ygh, axis=-1, keepdims=True)))
        ygh_parts.append(ygh)
    dyg = jnp.concatenate(dyg_parts, axis=1)
    ygh = jnp.concatenate(ygh_parts, axis=1)
    dy = dyg * _silu(z)
    dz = dyg * y * _dsilu(z)
    return (dy, dz), (_sum0(dyn * ygh), _sum0(dy * xs))


def _ln_stats(v):
    mu = jnp.mean(v, axis=-1, keepdims=True)
    vc = v - mu
    r = lax.rsqrt(jnp.mean(vc * vc, axis=-1, keepdims=True) + EPS)
    return vc * r, r


def _gm_act_fwd_fn(p, vg, vb):
    gu = _gelu(p[:, :GM_INNER])
    gvh, _ = _ln_stats(_gelu(p[:, GM_INNER:]))
    return (gu, gvh * vg + vb), ()


def _gm_act_bwd_fn(p, dgu, dgvn, vg):
    pu = p[:, :GM_INNER]
    pv = p[:, GM_INNER:]
    gvh, r = _ln_stats(_gelu(pv))
    dgvh = dgvn * vg
    dgv = r * (dgvh - jnp.mean(dgvh, axis=-1, keepdims=True) - gvh * jnp.mean(dgvh * gvh, axis=-1, keepdims=True))
    dp = jnp.concatenate([dgu * _dgelu(pu), dgv * _dgelu(pv)], axis=1)
    return (dp,), (_sum0(dgvn * gvh), _sum0(dgvn))


def _mm(a, b, *, out_dtype, name, tm=640, tn=512, tk=1024, add=None):
    m, k = a.shape
    k2, n = b.shape
    assert k == k2
    tm, tn, tk = _pick(m, tm), _pick(n, tn, 128), _pick(k, tk, 128)
    nk = k // tk

    def kern(*refs):
        if add is None:
            a_ref, b_ref, o_ref, acc_ref = refs
        else:
            a_ref, b_ref, add_ref, o_ref, acc_ref = refs
        kk = pl.program_id(2)

        @pl.when(kk == 0)
        def _():
            acc_ref[...] = jnp.zeros_like(acc_ref)

        acc_ref[...] += jnp.dot(a_ref[...], b_ref[...], preferred_element_type=F32)

        @pl.when(kk == nk - 1)
        def _():
            r = acc_ref[...]
            if add is not None:
                r = r + add_ref[...]
            o_ref[...] = r.astype(o_ref.dtype)

    in_specs = [pl.BlockSpec((tm, tk), lambda i, j, kk: (i, kk)), pl.BlockSpec((tk, tn), lambda i, j, kk: (kk, j))]
    args = [a, b]
    if add is not None:
        in_specs.append(pl.BlockSpec((tm, tn), lambda i, j, kk: (i, j)))
        args.append(add)
    return pl.pallas_call(
        kern, name=name, grid=(m // tm, n // tn, nk), in_specs=in_specs,
        out_specs=pl.BlockSpec((tm, tn), lambda i, j, kk: (i, j)),
        out_shape=jax.ShapeDtypeStruct((m, n), out_dtype),
        scratch_shapes=[pltpu.VMEM((tm, tn), F32)],
        compiler_params=_params("parallel", "parallel", "arbitrary"),
    )(*args)


def _mm_glu(u, wa, wb, *, name, tm=640, tn=256):
    m, k = u.shape
    n = wa.shape[1]
    tm, tn = _pick(m, tm), _pick(n, tn, 128)

    def kern(u_ref, wa_ref, wb_ref, s_ref, a_ref, b_ref):
        uu = u_ref[...]
        a = jnp.dot(uu, wa_ref[...], preferred_element_type=F32)
        b = jnp.dot(uu, wb_ref[...], preferred_element_type=F32)
        s_ref[...] = (_silu(a) * b).astype(BF16)
        a_ref[...] = a.astype(BF16)
        b_ref[...] = b.astype(BF16)

    ospec = pl.BlockSpec((tm, tn), lambda i, j: (i, j))
    return pl.pallas_call(
        kern, name=name, grid=(m // tm, n // tn),
        in_specs=[pl.BlockSpec((tm, k), lambda i, j: (i, 0)), pl.BlockSpec((k, tn), lambda i, j: (0, j)),
                  pl.BlockSpec((k, tn), lambda i, j: (0, j))],
        out_specs=[ospec, ospec, ospec],
        out_shape=[jax.ShapeDtypeStruct((m, n), BF16)] * 3,
        compiler_params=_params("parallel", "parallel"),
    )(u, wa, wb)


def _mm_tn(a, b, *, name, tm=512, tn=512, tk=512):
    t, m = a.shape
    t2, n = b.shape
    assert t == t2
    tm, tn, tk = _pick(m, tm, 128), _pick(n, tn, 128), _pick(t, tk)
    nk = t // tk

    def kern(a_ref, b_ref, o_ref):
        kk = pl.program_id(2)

        @pl.when(kk == 0)
        def _():
            o_ref[...] = jnp.zeros_like(o_ref)

        o_ref[...] += _dot(a_ref[...], b_ref[...], _TN)

    return pl.pallas_call(
        kern, name=name, grid=(m // tm, n // tn, nk),
        in_specs=[pl.BlockSpec((tk, tm), lambda i, j, kk: (kk, i)), pl.BlockSpec((tk, tn), lambda i, j, kk: (kk, j))],
        out_specs=pl.BlockSpec((tm, tn), lambda i, j, kk: (i, j)),
        out_shape=jax.ShapeDtypeStruct((m, n), F32),
        compiler_params=_params("parallel", "parallel", "arbitrary"),
    )(a, b)


def _mm_f32(a, b, *, name, silu_a=False, bias=None):
    m, k = a.shape
    n = b.shape[1]

    def kern(*refs):
        if bias is None:
            a_ref, b_ref, o_ref = refs
        else:
            a_ref, b_ref, bias_ref, o_ref = refs
        av = a_ref[...]
        if silu_a:
            av = _silu(av)
        r = jnp.dot(av, b_ref[...], preferred_element_type=F32, precision=HI)
        if bias is not None:
            r = r + bias_ref[...]
        o_ref[...] = r

    args = [a, b] + ([] if bias is None else [bias])
    return pl.pallas_call(kern, name=name, out_shape=jax.ShapeDtypeStruct((m, n), F32),
                          compiler_params=pltpu.CompilerParams(vmem_limit_bytes=VMEM_LIMIT_BYTES))(*args)


def _shifted(v, s, t, lo, hi):
    n = v.shape[0]
    r = v if s == 0 else pltpu.roll(v, (-s) % n, 0)
    ok = (t + s >= lo) & (t + s < hi)
    return jnp.where(ok, r, 0.0)


def _seg_bounds(n, n_ctx):
    t = lax.broadcasted_iota(jnp.int32, (n, 1), 0)
    lo = jnp.where(t < n_ctx, 0, n_ctx)
    hi = jnp.where(t < n_ctx, n_ctx, n)
    return t, lo, hi


def _conv_fwd(xp, w8, b, *, n_ctx, name, cb=256):
    n, c = xp.shape

    def kern(x_ref, w_ref, b_ref, cpre_ref, act_ref):
        x = x_ref[...]
        t, lo, hi = _seg_bounds(n, n_ctx)
        acc = jnp.zeros_like(x) + b_ref[...]
        for k in range(SSD_CONV):
            acc = acc + _shifted(x, k - SSD_CONV // 2, t, lo, hi) * w_ref[k:k + 1, :]
        cpre_ref[...] = acc
        act_ref[...] = _silu(acc)

    spec = pl.BlockSpec((n, cb), lambda j: (0, j))
    return pl.pallas_call(
        kern, name=name, grid=(c // cb,),
        in_specs=[spec, pl.BlockSpec((8, cb), lambda j: (0, j)), pl.BlockSpec((1, cb), lambda j: (0, j))],
        out_specs=[spec, spec], out_shape=[jax.ShapeDtypeStruct((n, c), F32)] * 2,
        compiler_params=_params("parallel"),
    )(xp, w8, b)


def _conv_bwd(d1, d2, cpre, xp, w8, *, n_ctx, name, cb=128):
    n, c = xp.shape

    def kern(d1_ref, d2_ref, cpre_ref, x_ref, w_ref, dx_ref, dw_ref, db_ref):
        g = (d1_ref[...] + d2_ref[...]) * _dsilu(cpre_ref[...])
        x = x_ref[...]
        t, lo, hi = _seg_bounds(n, n_ctx)
        dx = jnp.zeros_like(g)
        dw_ref[...] = jnp.zeros_like(dw_ref)
        for k in range(SSD_CONV):
            s = k - SSD_CONV // 2
            dx = dx + _shifted(g, -s, t, lo, hi) * w_ref[k:k + 1, :]
            dw_ref[k:k + 1, :] = _sum0(g * _shifted(x, s, t, lo, hi))
        dx_ref[...] = dx.astype(BF16)
        db_ref[...] = _sum0(g)

    spec = pl.BlockSpec((n, cb), lambda j: (0, j))
    return pl.pallas_call(
        kern, name=name, grid=(c // cb,),
        in_specs=[spec, spec, spec, spec, pl.BlockSpec((8, cb), lambda j: (0, j))],
        out_specs=[spec, pl.BlockSpec((8, cb), lambda j: (0, j)), pl.BlockSpec((1, cb), lambda j: (0, j))],
        out_shape=[jax.ShapeDtypeStruct((n, c), BF16), jax.ShapeDtypeStruct((8, c), F32),
                   jax.ShapeDtypeStruct((1, c), F32)],
        compiler_params=_params("parallel"),
    )(d1, d2, cpre, xp, w8)


def _chunk_of(s, nc, n_ctx_chunks, rev):
    if not rev:
        return s
    return jnp.where(s < n_ctx_chunks, n_ctx_chunks - 1 - s, nc - 1 - (s - n_ctx_chunks))


def _scan_common(dt_raw, dtT_raw, bias_r, bias_c, alog_r, alog_c, rev):
    ii = lax.broadcasted_iota(jnp.int32, (CHUNK, CHUNK), 0)
    jj = lax.broadcasted_iota(jnp.int32, (CHUNK, CHUNK), 1)
    tri = (jj >= ii) if rev else (jj <= ii)
    tri_t = (ii >= jj) if rev else (ii <= jj)
    a_r = -jnp.exp(alog_r)
    a_c = -jnp.exp(alog_c)
    dt = _softplus(dt_raw + bias_r)
    dt_t = _softplus(dtT_raw + bias_c)
    al = dt * a_r
    acum = _dot(tri.astype(F32), al, precision=HI)
    acum_t = _dot(dt_t * a_c, tri_t.astype(F32), precision=HI)
    atot = _sum0(al)
    return tri, tri_t, a_r, dt, acum, acum_t, atot


def _ssd_scan_fwd(xbc, dt_raw, dtT_raw, bias_r, bias_c, alog_r, alog_c, *, rev, n_ctx_chunks, name):
    n = xbc.shape[0]
    nc = n // CHUNK
    cidx = functools.partial(_chunk_of, nc=nc, n_ctx_chunks=n_ctx_chunks, rev=rev)

    def kern(xs_ref, b_ref, c_ref, dt_ref, dtT_ref, br_ref, bc_ref, ar_ref, ac_ref, y_ref, hs_ref, h_scr):
        @pl.when(pl.program_id(0) == 0)
        def _():
            h_scr[...] = jnp.zeros_like(h_scr)

        tri, _, _, dt, acum, acum_t, atot = _scan_common(
            dt_ref[...], dtT_ref[...], br_ref[...], bc_ref[...], ar_ref[...], ac_ref[...], rev)
        ea = jnp.exp(acum)
        dec_end = jnp.exp(atot - acum)
        etot = jnp.exp(atot)
        hs_ref[...] = h_scr[...]
        for g in range(SSD_GROUPS):
            bg = b_ref[:, g * SSD_STATE:(g + 1) * SSD_STATE].astype(BF16)
            cg = c_ref[:, g * SSD_STATE:(g + 1) * SSD_STATE].astype(BF16)
            cb = _dot(cg, bg, _NT)
            ys = []
            for k in range(SSD_HPG):
                h = g * SSD_HPG + k
                lmat = jnp.exp(jnp.where(tri, acum[:, h:h + 1] - acum_t[h:h + 1, :], NEG_BIG))
                m = (cb * lmat).astype(BF16)
                xdt = xs_ref[:, h * SSD_HEAD_DIM:(h + 1) * SSD_HEAD_DIM] * dt[:, h:h + 1]
                hh = h_scr[h]
                y = _dot(m, xdt.astype(BF16)) + _dot(cg, hh.astype(BF16), _NT) * ea[:, h:h + 1]
                ys.append(y)
                xdw = (xdt * dec_end[:, h:h + 1]).astype(BF16)
                h_scr[h] = hh * etot[:, h:h + 1] + _dot(xdw, bg, _TN)
            y_ref[:, g * 256:(g + 1) * 256] = jnp.concatenate(ys, axis=1)

    nh = SSD_HEADS
    small = lambda shape: pl.BlockSpec(shape, lambda s: (0, 0))
    return pl.pallas_call(
        kern, name=name, grid=(nc,),
        in_specs=[pl.BlockSpec((CHUNK, SSD_INNER), lambda s: (cidx(s), 0)),
                  pl.BlockSpec((CHUNK, 1024), lambda s: (cidx(s), 2)),
                  pl.BlockSpec((CHUNK, 1024), lambda s: (cidx(s), 3)),
                  pl.BlockSpec((CHUNK, nh), lambda s: (cidx(s), 0)),
                  pl.BlockSpec((nh, CHUNK), lambda s: (0, cidx(s))),
                  small((1, nh)), small((nh, 1)), small((1, nh)), small((nh, 1))],
        out_specs=[pl.BlockSpec((CHUNK, SSD_INNER), lambda s: (cidx(s), 0)),
                   pl.BlockSpec((None, nh, SSD_HEAD_DIM, SSD_STATE), lambda s: (s, 0, 0, 0))],
        out_shape=[jax.ShapeDtypeStruct((n, SSD_INNER), F32),
                   jax.ShapeDtypeStruct((nc, nh, SSD_HEAD_DIM, SSD_STATE), F32)],
        scratch_shapes=[pltpu.VMEM((nh, SSD_HEAD_DIM, SSD_STATE), F32)],
        compiler_params=_params("arbitrary"),
    )(xbc, xbc, xbc, dt_raw, dtT_raw, bias_r, bias_c, alog_r, alog_c)


def _ssd_scan_bwd(dy, xbc, hs, dt_raw, dtT_raw, bias_r, bias_c, alog_r, alog_c, dvec, *, rev, n_ctx_chunks,
                  direct, name):
    n = xbc.shape[0]
    nc = n // CHUNK
    nh = SSD_HEADS
    step_of = lambda r: nc - 1 - r
    cidx = lambda r: _chunk_of(step_of(r), nc, n_ctx_chunks, rev)

    def kern(dy_ref, xs_ref, b_ref, c_ref, hs_ref, dt_ref, dtT_ref, br_ref, bc_ref, ar_ref, ac_ref, dv_ref,
             dx_ref, ddt_ref, dal_ref, dbias_ref, dh_scr):
        @pl.when(pl.program_id(0) == 0)
        def _():
            dh_scr[...] = jnp.zeros_like(dh_scr)
            dal_ref[...] = jnp.zeros_like(dal_ref)
            dbias_ref[...] = jnp.zeros_like(dbias_ref)

        tri, tri_t, a_r, dt, acum, acum_t, atot = _scan_common(
            dt_ref[...], dtT_ref[...], br_ref[...], bc_ref[...], ar_ref[...], ac_ref[...], rev)
        ea = jnp.exp(acum)
        dec_end = jnp.exp(atot - acum)
        etot = jnp.exp(atot)
        ii = lax.broadcasted_iota(jnp.int32, (CHUNK, CHUNK), 0)
        jj = lax.broadcasted_iota(jnp.int32, (CHUNK, CHUNK), 1)
        strict = tri & (ii != jj)
        tri_t_bf = tri_t.astype(BF16)
        lane = lax.broadcasted_iota(jnp.int32, (CHUNK, nh), 1)
        lane1 = lax.broadcasted_iota(jnp.int32, (1, nh), 1)
        ycol = jnp.zeros((CHUNK, nh), F32)
        dal = jnp.zeros((CHUNK, nh), F32)
        ddt = jnp.zeros((CHUNK, nh), F32)
        dtot = jnp.zeros((1, nh), F32)
        for g in range(SSD_GROUPS):
            bg = b_ref[:, g * SSD_STATE:(g + 1) * SSD_STATE].astype(BF16)
            cg = c_ref[:, g * SSD_STATE:(g + 1) * SSD_STATE].astype(BF16)
            cb = _dot(cg, bg, _NT)
            dcb = jnp.zeros((CHUNK, CHUNK), F32)
            dbg = jnp.zeros((CHUNK, SSD_STATE), F32)
            dcg = jnp.zeros((CHUNK, SSD_STATE), F32)
            dxs = []
            for k in range(SSD_HPG):
                h = g * SSD_HPG + k
                hsl = slice(h * SSD_HEAD_DIM, (h + 1) * SSD_HEAD_DIM)
                sel = lane == h
                lmat = jnp.exp(jnp.where(tri, acum[:, h:h + 1] - acum_t[h:h + 1, :], NEG_BIG))
                mf = cb * lmat
                xh = xs_ref[:, hsl]
                dtc = dt[:, h:h + 1]
                xdt = xh * dtc
                dyh = dy_ref[:, hsl]
                dyh_bf = dyh.astype(BF16)
                hst = hs_ref[h]
                hst_bf = hst.astype(BF16)
                dh = dh_scr[h]
                dh_bf = dh.astype(BF16)
                eac = ea[:, h:h + 1]
                dec = dec_end[:, h:h + 1]
                et = etot[:, h:h + 1]
                yoff = _dot(cg, hst_bf, _NT) * eac
                dyo_bf = (dyh * eac).astype(BF16)
                dcg = dcg + _dot(dyo_bf, hst_bf)
                dh_scr[h] = dh * et + _dot(dyo_bf, cg, _TN)
                col = jnp.sum(dyh * yoff, axis=1, keepdims=True)
                bdh = _dot(bg, dh_bf, _NT)
                dxdt = _dot(mf.astype(BF16), dyh_bf, _TN) + bdh * dec
                e = jnp.sum(xdt * bdh, axis=1, keepdims=True) * dec
                col = col - e
                dtot_h = _sum0(e) + _sum0(jnp.sum(dh * hst, axis=1, keepdims=True)) * et
                dbg = dbg + _dot((xdt * dec).astype(BF16), dh_bf)
                dm = _dot(dyh_bf, xdt.astype(BF16), _NT)
                dcb = dcb + dm * lmat
                xmat = _dot(tri_t_bf, (dm * mf).astype(BF16))
                dal_h = jnp.sum(jnp.where(strict, xmat, 0.0), axis=1, keepdims=True)
                ycol = ycol + jnp.where(sel, col, 0.0)
                dal = dal + jnp.where(sel, dal_h, 0.0)
                ddt = ddt + jnp.where(sel, jnp.sum(dxdt * xh, axis=1, keepdims=True), 0.0)
                dtot = dtot + jnp.where(lane1 == h, dtot_h, 0.0)
                dxh = dxdt * dtc
                if direct:
                    dxh = dxh + dyh * dv_ref[:, hsl]
                dxs.append(dxh)
            dcb_bf = dcb.astype(BF16)
            dcg = dcg + _dot(dcb_bf, bg)
            dbg = dbg + _dot(dcb_bf, cg, _TN)
            dx_ref[:, g * 256:(g + 1) * 256] = jnp.concatenate(dxs, axis=1)
            dx_ref[:, SSD_INNER + g * SSD_STATE:SSD_INNER + (g + 1) * SSD_STATE] = dbg
            dx_ref[:, SSD_INNER + 1024 + g * SSD_STATE:SSD_INNER + 1024 + (g + 1) * SSD_STATE] = dcg
        dal = dal + _dot(tri_t.astype(F32), ycol, precision=HI) + dtot
        ddt = ddt + dal * a_r
        ddt_raw = ddt * _sig(dt_ref[...] + br_ref[...])
        ddt_ref[...] = ddt_raw
        dal_ref[...] += _sum0(dal * dt) * a_r
        dbias_ref[...] += _sum0(ddt_raw)

    small = lambda shape: pl.BlockSpec(shape, lambda r: (0, 0))
    return pl.pallas_call(
        kern, name=name, grid=(nc,),
        in_specs=[pl.BlockSpec((CHUNK, SSD_INNER), lambda r: (cidx(r), 0)),
                  pl.BlockSpec((CHUNK, SSD_INNER), lambda r: (cidx(r), 0)),
                  pl.BlockSpec((CHUNK, 1024), lambda r: (cidx(r), 2)),
                  pl.BlockSpec((CHUNK, 1024), lambda r: (cidx(r), 3)),
                  pl.BlockSpec((None, nh, SSD_HEAD_DIM, SSD_STATE), lambda r: (step_of(r), 0, 0, 0)),
                  pl.BlockSpec((CHUNK, nh), lambda r: (cidx(r), 0)),
                  pl.BlockSpec((nh, CHUNK), lambda r: (0, cidx(r))),
                  small((1, nh)), small((nh, 1)), small((1, nh)), small((nh, 1)), small((1, SSD_INNER))],
        out_specs=[pl.BlockSpec((CHUNK, SSD_CONV_DIM), lambda r: (cidx(r), 0)),
                   pl.BlockSpec((CHUNK, nh), lambda r: (cidx(r), 0)),
                   small((1, nh)), small((1, nh))],
        out_shape=[jax.ShapeDtypeStruct((n, SSD_CONV_DIM), F32), jax.ShapeDtypeStruct((n, nh), F32),
                   jax.ShapeDtypeStruct((1, nh), F32), jax.ShapeDtypeStruct((1, nh), F32)],
        scratch_shapes=[pltpu.VMEM((nh, SSD_HEAD_DIM, SSD_STATE), F32)],
        compiler_params=_params("arbitrary"),
    )(dy, xbc, xbc, xbc, hs, dt_raw, dtT_raw, bias_r, bias_c, alog_r, alog_c, dvec)


def _gm_spatial_fwd(gu, gvn, ws, bst, *, name):
    n = gu.shape[0]

    def kern(gu_ref, gv_ref, ws_ref, bs_ref, o_ref):
        for g in range(GM_GROUPS):
            sl = slice(g * GM_GROUP_DIM, (g + 1) * GM_GROUP_DIM)
            s = _dot(ws_ref[g], gv_ref[:, sl]) + bs_ref[:, g:g + 1]
            o_ref[:, sl] = (gu_ref[:, sl] * s).astype(BF16)

    spec = pl.BlockSpec((CHUNK, GM_INNER), lambda i: (i, 0))
    return pl.pallas_call(
        kern, name=name, grid=(n // CHUNK,),
        in_specs=[spec, spec, pl.BlockSpec(ws.shape, lambda i: (0, 0, 0)), pl.BlockSpec(bst.shape, lambda i: (0, 0))],
        out_specs=spec, out_shape=jax.ShapeDtypeStruct((n, GM_INNER), BF16),
        compiler_params=_params("parallel"),
    )(gu, gvn, ws, bst)


def _gm_spatial_bwd(dt, gu, gvn, ws, wst, bst, *, name):
    n = gu.shape[0]

    def kern(dt_ref, gu_ref, gv_ref, ws_ref, wst_ref, bs_ref, dgu_ref, dgv_ref, dws_ref, dbs_ref):
        @pl.when(pl.program_id(0) == 0)
        def _():
            dws_ref[...] = jnp.zeros_like(dws_ref)
            dbs_ref[...] = jnp.zeros_like(dbs_ref)

        lane = lax.broadcasted_iota(jnp.int32, (CHUNK, GM_GROUPS), 1)
        dbs = jnp.zeros((CHUNK, GM_GROUPS), F32)
        for g in range(GM_GROUPS):
            sl = slice(g * GM_GROUP_DIM, (g + 1) * GM_GROUP_DIM)
            gv = gv_ref[:, sl]
            s = _dot(ws_ref[g], gv) + bs_ref[:, g:g + 1]
            d = dt_ref[:, sl]
            dgu_ref[:, sl] = d * s
            ds = d * gu_ref[:, sl]
            ds_bf = ds.astype(BF16)
            dws_ref[g] += _dot(ds_bf, gv, _NT)
            dgv_ref[:, sl] = _dot(wst_ref[g], ds_bf)
            dbs = dbs + jnp.where(lane == g, jnp.sum(ds, axis=1, keepdims=True), 0.0)
        dbs_ref[...] += dbs

    spec = pl.BlockSpec((CHUNK, GM_INNER), lambda i: (i, 0))
    wspec = pl.BlockSpec(ws.shape, lambda i: (0, 0, 0))
    bspec = pl.BlockSpec(bst.shape, lambda i: (0, 0))
    return pl.pallas_call(
        kern, name=name, grid=(n // CHUNK,),
        in_specs=[spec, spec, spec, wspec, wspec, bspec],
        out_specs=[spec, spec, wspec, bspec],
        out_shape=[jax.ShapeDtypeStruct((n, GM_INNER), F32), jax.ShapeDtypeStruct((n, GM_INNER), F32),
                   jax.ShapeDtypeStruct(ws.shape, F32), jax.ShapeDtypeStruct(bst.shape, F32)],
        compiler_params=_params("arbitrary"),
    )(dt, gu, gvn, ws, wst, bst)


def _adamw(parts, w, m, v, *, name, tm=256):
    ns, r, wd = parts.shape
    tm = _pick(r, tm, 8)

    def kern(p_ref, w_ref, m_ref, v_ref, g_ref, d_ref, nm_ref, nv_ref):
        g = p_ref[0].astype(F32)
        for s in range(1, ns):
            g = g + p_ref[s].astype(F32)
        m2 = ADAM_B1 * m_ref[...] + (1.0 - ADAM_B1) * g
        v2 = ADAM_B2 * v_ref[...] + (1.0 - ADAM_B2) * (g * g)
        m_hat = m2 / (1.0 - ADAM_B1 ** ADAM_STEP)
        v_hat = v2 / (1.0 - ADAM_B2 ** ADAM_STEP)
        g_ref[...] = g
        d_ref[...] = -ADAM_LR * (m_hat / (jnp.sqrt(v_hat) + ADAM_EPS) + ADAM_WD * w_ref[...])
        nm_ref[...] = m2
        nv_ref[...] = v2

    spec = pl.BlockSpec((tm, wd), lambda i: (i, 0))
    return pl.pallas_call(
        kern, name=name, grid=(r // tm,),
        in_specs=[pl.BlockSpec((ns, tm, wd), lambda i: (0, i, 0)), spec, spec, spec],
        out_specs=[spec] * 4, out_shape=[jax.ShapeDtypeStruct((r, wd), F32)] * 4,
        compiler_params=_params("parallel"),
    )(parts, w, m, v)


def _sum_slots(parts, *, name, scale_by=None):
    ns, r, wd = parts.shape

    def kern(*refs):
        p_ref, o_ref = refs[0], refs[-1]
        g = p_ref[0]
        for s in range(1, ns):
            g = g + p_ref[s]
        if scale_by is not None:
            g = g * _dsilu(refs[1][...])
        o_ref[...] = g

    args = [parts] + ([] if scale_by is None else [scale_by])
    return pl.pallas_call(kern, name=name, out_shape=jax.ShapeDtypeStruct((r, wd), F32),
                          compiler_params=pltpu.CompilerParams(vmem_limit_bytes=VMEM_LIMIT_BYTES))(*args)


def _mesh_pos():
    x, y, c = lax.axis_index("x"), lax.axis_index("y"), lax.axis_index("c")
    return x, y, c, 4 * x + 2 * y + c


def _flip(x, y, c, f):
    fx, fy, fc = (f >> 2) & 1, (f >> 1) & 1, f & 1
    px = 1 - x if fx else x
    py = 1 - y if fy else y
    pc = 1 - c if fc else c
    return (px, py, pc), 4 * px + 2 * py + pc


_HBM_SPEC = pl.BlockSpec(memory_space=pltpu.HBM)


def _exchange(arrays, *, scatter, name):
    na = len(arrays)
    if scatter:
        out_shape = [jax.ShapeDtypeStruct(a.shape, a.dtype) for a in arrays]
    else:
        out_shape = [jax.ShapeDtypeStruct((NDEV,) + a.shape, a.dtype) for a in arrays]

    def body(*refs):
        ins, outs = refs[:na], refs[na:2 * na]
        send_sems, recv_sems, local_sems = refs[2 * na:]
        x, y, c, me = _mesh_pos()
        copies = []
        for i in range(na):
            src_own = ins[i].at[me] if scatter else ins[i]
            lc = pltpu.make_async_copy(src_own, outs[i].at[me], local_sems.at[i])
            lc.start()
            copies.append(lc)
        sends = []
        for f in range(1, NDEV):
            peer, pidx = _flip(x, y, c, f)
            for i in range(na):
                k = i * (NDEV - 1) + f - 1
                src = ins[i].at[pidx] if scatter else ins[i]
                cp = pltpu.make_async_remote_copy(
                    src_ref=src, dst_ref=outs[i].at[me], send_sem=send_sems.at[k], recv_sem=recv_sems.at[k],
                    device_id=peer, device_id_type=pl.DeviceIdType.MESH)
                cp.start()
                sends.append(cp)
        for f in range(1, NDEV):
            peer, pidx = _flip(x, y, c, f)
            for i in range(na):
                k = i * (NDEV - 1) + f - 1
                src = ins[i].at[pidx] if scatter else ins[i]
                pltpu.make_async_remote_copy(
                    src_ref=src, dst_ref=outs[i].at[pidx], send_sem=send_sems.at[k], recv_sem=recv_sems.at[k],
                    device_id=peer, device_id_type=pl.DeviceIdType.MESH).wait_recv()
        for cp in sends:
            cp.wait_send()
        for lc in copies:
            lc.wait()

    return pl.pallas_call(
        body, name=name, out_shape=out_shape, in_specs=[_HBM_SPEC] * na, out_specs=[_HBM_SPEC] * na,
        scratch_shapes=[pltpu.SemaphoreType.DMA((na * (NDEV - 1),)), pltpu.SemaphoreType.DMA((na * (NDEV - 1),)),
                        pltpu.SemaphoreType.DMA((na,))],
        compiler_params=pltpu.CompilerParams(has_side_effects=True),
    )(*arrays)


def _seg_kw(nseg, n_ctx, tm):
    return dict(nseg=nseg, seg_blocks=(n_ctx // tm if nseg == 2 else 0))


def _ffn_fwd(tag, h, gpre, gpost, shift, scale, gate, w, *, nseg, n_ctx, tm):
    n = h.shape[0]
    kw = _seg_kw(nseg, n_ctx, tm)
    (u,) = _rowwise(tag + "_pre", _pre_fwd_fn, n, [h], [("full", gpre), ("seg", shift), ("seg", scale)],
                    [(D_MODEL, BF16)], tm=tm, **kw)
    s, a, b = _mm_glu(u, w["wa"], w["wb"], name=tag + "_glu")
    y = _mm(s, w["wout"], out_dtype=F32, name=tag + "_out", tn=512, tk=FFN_DIM)
    (ho,) = _rowwise(tag + "_post", functools.partial(_post_fwd_fn, 0.5), n, [h, y], [("full", gpost), ("seg", gate)],
                     [(D_MODEL, F32)], tm=tm, **kw)
    return ho, dict(h=h, u=u, s=s, a=a, b=b, y=y)


def _ffn_bwd(tag, dho, sv, gpre, gpost, scale, gate, w, *, nseg, n_ctx, tm):
    n = dho.shape[0]
    kw = _seg_kw(nseg, n_ctx, tm)
    dy, dgate, dgpost = _rowwise(tag + "_postb", functools.partial(_post_bwd_fn, 0.5), n, [dho, sv["y"]],
                                 [("full", gpost), ("seg", gate)], [(D_MODEL, BF16)], [D_MODEL, D_MODEL], tm=tm, **kw)
    dwout = _mm_tn(sv["s"], dy, name=tag + "_dwout", tm=1408, tn=1024)
    ds = _mm(dy, w["wout_t"], out_dtype=F32, name=tag + "_ds", tn=704)
    (dp,) = _rowwise(tag + "_glub", _glu_bwd_fn, n, [ds, sv["a"], sv["b"]], [], [(2 * FFN_DIM, BF16)], tm=min(tm, 128))
    dwin = _mm_tn(sv["u"], dp, name=tag + "_dwin", tm=1024, tn=512)
    du = _mm(dp, w["win_t"], out_dtype=F32, name=tag + "_du", tn=1024, tk=512)
    dh, dshift, dscale, dgpre = _rowwise(tag + "_preb", _pre_bwd_fn, n, [du, sv["h"], dho],
                                         [("full", gpre), ("seg", scale)], [(D_MODEL, F32)],
                                         [D_MODEL, D_MODEL, D_MODEL], tm=tm, **kw)
    return dh, dwin, dwout, dict(shift=dshift, scale=dscale, gate=dgate, gpre=dgpre, gpost=dgpost)


def _local_step(x, ctx, target, mods, norm_g, wts, small):
    t_len, n_ctx = x.shape[0], ctx.shape[0]
    n0 = t_len + n_ctx
    tm0 = _pick(n_ctx, 256, 8)
    tm1 = _pick(t_len, 256, 8)
    ncc = n_ctx // CHUNK
    g = {}

    def modrow(i, k, nseg):
        mc, mx = mods[i]
        if nseg == 2:
            return jnp.stack([mc[k], mx[k]])[:, None, :]
        return mx[k][None, None, :]

    def gvec(i, k):
        return norm_g[i, k][None, :]

    xc = jnp.concatenate([ctx, x], axis=0)
    L0 = dict(nseg=2, n_ctx=n_ctx, tm=tm0)
    h1, sv_f01 = _ffn_fwd("l0f1", xc, gvec(0, 0), gvec(0, 1), modrow(0, 0, 2), modrow(0, 1, 2), modrow(0, 2, 2),
                          wts["ffn00"], **L0)
    kw0 = _seg_kw(2, n_ctx, tm0)
    (um0,) = _rowwise("l0m_pre", _pre_fwd_fn, n0, [h1], [("full", gvec(0, 2)), ("seg", modrow(0, 3, 2)),
                                                         ("seg", modrow(0, 4, 2))], [(D_MODEL, BF16)], tm=tm0, **kw0)
    z = _mm(um0, wts["ssd_wz"], out_dtype=F32, name="ssd_z", tm=544)
    xbc_pre = _mm(um0, wts["ssd_wxbc"], out_dtype=F32, name="ssd_xbc", tm=544)
    dtr = _mm(um0, wts["ssd_wdt"], out_dtype=F32, name="ssd_dt", tm=544)
    cpre, xbc = _conv_fwd(xbc_pre, small["conv_w8"], small["conv_b"], n_ctx=n_ctx, name="ssd_conv")
    nh = SSD_HEADS
    dt_dir = [dtr[:, :nh], dtr[:, nh:2 * nh]]
    dtT_dir = [d.T for d in dt_dir]
    bias_r = [small["dt_bias"][d][None, :] for d in range(2)]
    bias_c = [small["dt_bias"][d][:, None] for d in range(2)]
    alog_r = [small["a_log"][d][None, :] for d in range(2)]
    alog_c = [small["a_log"][d][:, None] for d in range(2)]
    ys, hss = [], []
    for d in range(2):
        yd, hsd = _ssd_scan_fwd(xbc, dt_dir[d], dtT_dir[d], bias_r[d], bias_c[d], alog_r[d], alog_c[d],
                                rev=(d == 1), n_ctx_chunks=ncc, name=f"ssd_scan{d}")
        ys.append(yd)
        hss.append(hsd)
    dvec = jnp.repeat(small["ssd_d"], SSD_HEAD_DIM)[None, :]
    ngv = small["ssd_norm_g"][None, :]
    gate_rows = [ys[0], ys[1], (xbc, SSD_INNER, 0, 0), z]
    (yn_all,) = _rowwise("ssd_gate", _ssdgate_fwd_fn, n0, gate_rows, [("full", dvec), ("full", ngv)],
                         [(SSD_INNER, BF16)], tm=128)
    yn = yn_all[n_ctx:]
    yo0 = _mm(yn, wts["ssd_wout"], out_dtype=F32, name="ssd_out", tn=1024, tk=1024)
    h1x = h1[n_ctx:]
    L1 = dict(nseg=1, n_ctx=0, tm=tm1)
    (h2,) = _rowwise("l0m_post", functools.partial(_post_fwd_fn, 1.0), t_len, [h1x, yo0],
                     [("full", gvec(0, 3)), ("seg", modrow(0, 5, 1))], [(D_MODEL, F32)], tm=tm1)
    h3, sv_f02 = _ffn_fwd("l0f2", h2, gvec(0, 4), gvec(0, 5), modrow(0, 6, 1), modrow(0, 7, 1), modrow(0, 8, 1),
                          wts["ffn01"], **L1)

    h4, sv_f11 = _ffn_fwd("l1f1", h3, gvec(1, 0), gvec(1, 1), modrow(1, 0, 1), modrow(1, 1, 1), modrow(1, 2, 1),
                          wts["ffn10"], **L1)
    (um1,) = _rowwise("l1m_pre", _pre_fwd_fn, t_len, [h4], [("full", gvec(1, 2)), ("seg", modrow(1, 3, 1)),
                                                            ("seg", modrow(1, 4, 1))], [(D_MODEL, BF16)], tm=tm1)
    p1 = _mm(um1, wts["gm_win"], out_dtype=F32, name="gm_in")
    vg = small["gm_v_g"][None, :]
    vb = small["gm_v_b"][None, :]
    gu, gvn = _rowwise("gm_act", _gm_act_fwd_fn, t_len, [p1], [("full", vg), ("full", vb)],
                       [(GM_INNER, F32), (GM_INNER, BF16)], tm=128)
    ws_bf = small["gm_w_s"].astype(BF16)
    wst_bf = jnp.swapaxes(small["gm_w_s"], 1, 2).astype(BF16)
    bst = small["gm_b_s"].T
    tgm = _gm_spatial_fwd(gu, gvn, ws_bf, bst, name="gm_spatial")
    yo1 = _mm(tgm, wts["gm_wout"], out_dtype=F32, name="gm_out", tn=1024, tk=1024)
    (h5,) = _rowwise("l1m_post", functools.partial(_post_fwd_fn, 1.0), t_len, [h4, yo1],
                     [("full", gvec(1, 3)), ("seg", modrow(1, 5, 1))], [(D_MODEL, F32)], tm=tm1)
    h6, sv_f12 = _ffn_fwd("l1f2", h5, gvec(1, 4), gvec(1, 5), modrow(1, 6, 1), modrow(1, 7, 1), modrow(1, 8, 1),
                          wts["ffn11"], **L1)

    dh, loss_parts = _rowwise("loss", _loss_fn, t_len, [h6, target], [], [(D_MODEL, F32)], [D_MODEL], tm=tm1)

    zero = jnp.zeros((D_MODEL,), F32)
    dmx = [[zero] * N_MOD for _ in range(2)]
    dmc = [[zero] * N_MOD for _ in range(2)]
    dng = [[zero] * 6 for _ in range(2)]

    def put_mod(i, k, acc):
        if acc.shape[0] == 2:
            dmc[i][k] = dmc[i][k] + acc[0, 0]
            dmx[i][k] = dmx[i][k] + acc[1, 0]
        else:
            dmx[i][k] = dmx[i][k] + acc[0, 0]

    def put_g(i, k, acc):
        dng[i][k] = dng[i][k] + jnp.sum(acc[:, 0], axis=0)

    def ffn_back(tag, i, j, dho, sv, w, lay):
        nseg = lay["nseg"]
        base = 0 if j == 0 else 6
        gi = 0 if j == 0 else 4
        dh_in, dwin, dwout, s = _ffn_bwd(tag, dho, sv, gvec(i, gi), gvec(i, gi + 1), modrow(i, base + 1, nseg),
                                         modrow(i, base + 2, nseg), w, **lay)
        put_mod(i, base, s["shift"])
        put_mod(i, base + 1, s["scale"])
        put_mod(i, base + 2, s["gate"])
        put_g(i, gi, s["gpre"])
        put_g(i, gi + 1, s["gpost"])
        g[f"ffn_w_in{i}{j}"] = dwin
        g[f"ffn_w_out{i}{j}"] = dwout
        return dh_in

    dh = ffn_back("l1f2", 1, 1, dh, sv_f12, wts["ffn11"], L1)
    dyo, dgate, dgp = _rowwise("l1m_postb", functools.partial(_post_bwd_fn, 1.0), t_len, [dh, yo1],
                               [("full", gvec(1, 3)), ("seg", modrow(1, 5, 1))], [(D_MODEL, BF16)],
                               [D_MODEL, D_MODEL], tm=tm1)
    put_mod(1, 5, dgate)
    put_g(1, 3, dgp)
    g["gm_w_out"] = _mm_tn(tgm, dyo, name="gm_dwout", tn=1024)
    dtg = _mm(dyo, wts["gm_wout_t"], out_dtype=F32, name="gm_dt")
    dgu, dgvn, dws, dbst = _gm_spatial_bwd(dtg, gu, gvn, ws_bf, wst_bf, bst, name="gm_spatialb")
    g["gm_w_s"] = dws
    g["gm_b_s"] = dbst.T
    dp1, dvg, dvb = _rowwise("gm_actb", _gm_act_bwd_fn, t_len, [p1, dgu, dgvn], [("full", vg)],
                             [(2 * GM_INNER, BF16)], [GM_INNER, GM_INNER], tm=128)
    g["gm_v_g"] = dvg[0, 0]
    g["gm_v_b"] = dvb[0, 0]
    g["gm_w_in"] = _mm_tn(um1, dp1, name="gm_dwin", tm=1024)
    dum1 = _mm(dp1, wts["gm_win_t"], out_dtype=F32, name="gm_dum", tn=1024, tk=512)
    dh, dsh, dsc, dgp = _rowwise("l1m_preb", _pre_bwd_fn, t_len, [dum1, h4, dh],
                                 [("full", gvec(1, 2)), ("seg", modrow(1, 4, 1))], [(D_MODEL, F32)],
                                 [D_MODEL, D_MODEL, D_MODEL], tm=tm1)
    put_mod(1, 3, dsh)
    put_mod(1, 4, dsc)
    put_g(1, 2, dgp)
    dh = ffn_back("l1f1", 1, 0, dh, sv_f11, wts["ffn10"], L1)

    dh = ffn_back("l0f2", 0, 1, dh, sv_f02, wts["ffn01"], L1)
    dyo, dgate, dgp = _rowwise("l0m_postb", functools.partial(_post_bwd_fn, 1.0), t_len, [dh, yo0],
                               [("full", gvec(0, 3)), ("seg", modrow(0, 5, 1))], [(D_MODEL, BF16)],
                               [D_MODEL, D_MODEL], tm=tm1)
    put_mod(0, 5, dgate)
    put_g(0, 3, dgp)
    g["ssd_w_out"] = _mm_tn(yn, dyo, name="ssd_dwout", tn=1024)
    dyn = _mm(dyo, wts["ssd_wout_t"], out_dtype=F32, name="ssd_dyn")
    dyn_all = jnp.concatenate([jnp.zeros((n_ctx, SSD_INNER), F32), dyn], axis=0)
    dy_ssd, dz, dngv, ddv = _rowwise("ssd_gateb", _ssdgate_bwd_fn, n0, [dyn_all] + gate_rows,
                                     [("full", dvec), ("full", ngv)], [(SSD_INNER, F32), (SSD_INNER, BF16)],
                                     [SSD_INNER, SSD_INNER], tm=128)
    g["ssd_norm_g"] = dngv[0, 0]
    g["ssd_D"] = jnp.sum(ddv[0, 0].reshape(SSD_HEADS, SSD_HEAD_DIM), axis=1)
    dxbcs, ddts, dalogs, dbiases = [], [], [], []
    for d in range(2):
        dxd, ddtd, dal, dbi = _ssd_scan_bwd(dy_ssd, xbc, hss[d], dt_dir[d], dtT_dir[d], bias_r[d], bias_c[d],
                                            alog_r[d], alog_c[d], dvec, rev=(d == 1), n_ctx_chunks=ncc,
                                            direct=(d == 0), name=f"ssd_scanb{d}")
        dxbcs.append(dxd)
        ddts.append(ddtd)
        dalogs.append(dal[0])
        dbiases.append(dbi[0])
    g["ssd_A_log"] = jnp.stack(dalogs)
    g["ssd_dt_bias"] = jnp.stack(dbiases)
    dxbc_pre, dcw8, dcb = _conv_bwd(dxbcs[0], dxbcs[1], cpre, xbc_pre, small["conv_w8"], n_ctx=n_ctx, name="ssd_convb")
    g["ssd_conv_w"] = dcw8[:SSD_CONV]
    g["ssd_conv_b"] = dcb[0]
    ddt_bf = jnp.concatenate([ddts[0], ddts[1], jnp.zeros((n0, 128 - 2 * nh), F32)], axis=1).astype(BF16)
    g["ssd_w_in"] = jnp.concatenate([
        _mm_tn(um0, dz, name="ssd_dwz", tm=1024),
        _mm_tn(um0, dxbc_pre, name="ssd_dwxbc", tm=1024),
        _mm_tn(um0, ddt_bf, name="ssd_dwdt", tm=1024)[:, :2 * nh]], axis=1)
    dum0 = _mm(dz, wts["ssd_wz_t"], out_dtype=F32, name="ssd_dum_z", tm=544, tn=1024, tk=512)
    dum0 = _mm(dxbc_pre, wts["ssd_wxbc_t"], out_dtype=F32, name="ssd_dum_x", tm=544, tn=1024, tk=512, add=dum0)
    dum0 = _mm(ddt_bf, wts["ssd_wdt_t"], out_dtype=F32, name="ssd_dum_dt", tm=544, tn=1024, add=dum0)
    dres = jnp.concatenate([jnp.zeros((n_ctx, D_MODEL), F32), dh], axis=0)
    dh0, dsh, dsc, dgp = _rowwise("l0m_preb", _pre_bwd_fn, n0, [dum0, h1, dres],
                                  [("full", gvec(0, 2)), ("seg", modrow(0, 4, 2))], [(D_MODEL, F32)],
                                  [D_MODEL, D_MODEL, D_MODEL], tm=tm0, **kw0)
    put_mod(0, 3, dsh)
    put_mod(0, 4, dsc)
    put_g(0, 2, dgp)
    dh0 = ffn_back("l0f1", 0, 0, dh0, sv_f01, wts["ffn00"], L0)
    grad_x = dh0[n_ctx:]
    g["norm_g"] = jnp.stack([jnp.stack(r) for r in dng])
    g["dmx"] = jnp.stack([jnp.concatenate(r) for r in dmx])
    g["dmc"] = jnp.stack([jnp.concatenate(r) for r in dmc])
    return loss_parts[0], grad_x, g


def _prep_weights(ffn_w_in, ffn_w_out, ssd_w_in, ssd_w_out, gm_w_in, gm_w_out):
    w = {}
    for i in range(2):
        for j in range(2):
            win = ffn_w_in[i, j]
            wout = ffn_w_out[i, j]
            w[f"ffn{i}{j}"] = dict(wa=win[:, :FFN_DIM], wb=win[:, FFN_DIM:], win_t=win.T, wout=wout, wout_t=wout.T)
    sw = ssd_w_in[0]
    nh2 = 2 * SSD_HEADS
    wdt = jnp.pad(sw[:, SSD_INNER + SSD_CONV_DIM:], ((0, 0), (0, 128 - nh2)))
    w["ssd_wz"] = sw[:, :SSD_INNER]
    w["ssd_wxbc"] = sw[:, SSD_INNER:SSD_INNER + SSD_CONV_DIM]
    w["ssd_wdt"] = wdt
    w["ssd_wz_t"] = w["ssd_wz"].T
    w["ssd_wxbc_t"] = w["ssd_wxbc"].T
    w["ssd_wdt_t"] = wdt.T
    w["ssd_wout"] = ssd_w_out[0]
    w["ssd_wout_t"] = ssd_w_out[0].T
    w["gm_win"] = gm_w_in[0]
    w["gm_win_t"] = gm_w_in[0].T
    w["gm_wout"] = gm_w_out[0]
    w["gm_wout_t"] = gm_w_out[0].T
    return w


def _cols_gathered(gat, axis):
    parts = [gat[d] for d in range(NDEV)]
    return jnp.concatenate(parts, axis=axis)


def _cols_scatter(full, axis):
    return jnp.stack(jnp.split(full, NDEV, axis=axis))


def kernel(x, c, ctx, c_ctx, ada_w, ada_b, norm_g, ffn_w_in, ffn_w_out, ssd_w_in, ssd_conv_w, ssd_conv_b, ssd_dt_bias, ssd_A_log, ssd_D, ssd_norm_g, ssd_w_out, gm_w_in, gm_v_g, gm_v_b, gm_w_s, gm_b_s, gm_w_out, loss_target, m_c_ctx, m_ada_w, m_ada_b, m_norm_g, m_ffn_w_in, m_ffn_w_out, m_ssd_w_in, m_ssd_conv_w, m_ssd_conv_b, m_ssd_dt_bias, m_ssd_A_log, m_ssd_D, m_ssd_norm_g, m_ssd_w_out, m_gm_w_in, m_gm_v_g, m_gm_v_b, m_gm_w_s, m_gm_b_s, m_gm_w_out, v_c_ctx, v_ada_w, v_ada_b, v_norm_g, v_ffn_w_in, v_ffn_w_out, v_ssd_w_in, v_ssd_conv_w, v_ssd_conv_b, v_ssd_dt_bias, v_ssd_A_log, v_ssd_D, v_ssd_norm_g, v_ssd_w_out, v_gm_w_in, v_gm_v_g, v_gm_v_b, v_gm_w_s, v_gm_b_s, v_gm_w_out):
    me = 4 * lax.axis_index("x") + 2 * lax.axis_index("y") + lax.axis_index("c")
    d = D_MODEL
    ncol = N_MOD * d // NDEV

    big = [ffn_w_in.astype(BF16), ffn_w_out.astype(BF16), ssd_w_in.astype(BF16), ssd_w_out.astype(BF16),
           gm_w_in.astype(BF16), gm_w_out.astype(BF16)]
    gat = _exchange(big, scatter=False, name="gather_weights")
    small_pack = jnp.concatenate([c.reshape(-1), norm_g.reshape(-1), ssd_conv_w.reshape(-1), gm_v_g.reshape(-1),
                                  gm_v_b.reshape(-1)])[None, :]
    (sp,) = _exchange([small_pack], scatter=False, name="gather_small")
    sp = sp[:, 0]
    o = 0
    c_all = sp[:, o:o + d]; o += d
    ng_all = sp[:, o:o + 2 * 6 * 128].reshape(NDEV, 2, 6, 128); o += 2 * 6 * 128
    cw_all = sp[:, o:o + SSD_CONV * 512].reshape(NDEV, SSD_CONV, 512); o += SSD_CONV * 512
    vg_all = sp[:, o:o + 256]; o += 256
    vb_all = sp[:, o:o + 256]; o += 256
    norm_g_full = jnp.transpose(ng_all, (1, 2, 0, 3)).reshape(2, 6, d)
    conv_w_full = jnp.transpose(cw_all, (1, 0, 2)).reshape(SSD_CONV, SSD_CONV_DIM)
    gm_v_g_full = vg_all.reshape(-1)
    gm_v_b_full = vb_all.reshape(-1)

    wts = _prep_weights(_cols_gathered(gat[0], 3), _cols_gathered(gat[1], 2), _cols_gathered(gat[2], 2),
                        _cols_gathered(gat[3], 1), _cols_gathered(gat[4], 2), _cols_gathered(gat[5], 1))

    c16 = jnp.concatenate([c_all, jnp.broadcast_to(c_ctx[None, :], (NDEV, d))], axis=0)
    ada_b_loc = lax.dynamic_slice_in_dim(ada_b, me * ncol, ncol, axis=1)
    mods_loc = jnp.stack([_mm_f32(c16, ada_w[i], name=f"ada_mod{i}", silu_a=True, bias=ada_b_loc[i][None, :])
                          for i in range(2)])
    (mods_all,) = _exchange([mods_loc], scatter=False, name="gather_mods")
    mods_rows = jnp.transpose(mods_all, (1, 2, 0, 3)).reshape(2, 2 * NDEV, N_MOD * d)
    mx = lax.dynamic_index_in_dim(mods_rows, me, axis=1, keepdims=False).reshape(2, N_MOD, d)
    mc = mods_rows[:, NDEV].reshape(2, N_MOD, d)
    mods = [(mc[i], mx[i]) for i in range(2)]

    small = dict(conv_w8=jnp.pad(conv_w_full, ((0, 8 - SSD_CONV), (0, 0))), conv_b=ssd_conv_b, dt_bias=ssd_dt_bias[0],
                 a_log=ssd_A_log[0], ssd_d=ssd_D[0], ssd_norm_g=ssd_norm_g[0], gm_v_g=gm_v_g_full,
                 gm_v_b=gm_v_b_full, gm_w_s=gm_w_s[0], gm_b_s=gm_b_s[0])
    loss_parts, grad_x, g = _local_step(x[0], ctx[0], loss_target[0], mods, norm_g_full, wts, small)
    loss = lax.psum(0.5 / d * jnp.sum(loss_parts), ("x", "y", "c"))

    def stack4(fmt):
        return jnp.stack([jnp.stack([g[fmt.format(i, j)] for j in range(2)]) for i in range(2)])

    gfull = [stack4("ffn_w_in{}{}"), stack4("ffn_w_out{}{}"), g["ssd_w_in"][None], g["ssd_w_out"][None],
             g["gm_w_in"][None], g["gm_w_out"][None]]
    axes = [3, 2, 2, 1, 2, 1]
    parts = _exchange([_cols_scatter(a, ax).astype(BF16) for a, ax in zip(gfull, axes)], scatter=True,
                      name="scatter_grads")
    big_w = [ffn_w_in, ffn_w_out, ssd_w_in, ssd_w_out, gm_w_in, gm_w_out]
    big_m = [m_ffn_w_in, m_ffn_w_out, m_ssd_w_in, m_ssd_w_out, m_gm_w_in, m_gm_w_out]
    big_v = [v_ffn_w_in, v_ffn_w_out, v_ssd_w_in, v_ssd_w_out, v_gm_w_in, v_gm_w_out]
    big_names = ["ffn_w_in", "ffn_w_out", "ssd_w_in", "ssd_w_out", "gm_w_in", "gm_w_out"]
    res = {}
    for nm, p, w_, m_, v_ in zip(big_names, parts, big_w, big_m, big_v):
        shp = w_.shape
        wd = shp[-1]
        outs = _adamw(p.reshape(NDEV, -1, wd), w_.reshape(-1, wd), m_.reshape(-1, wd), v_.reshape(-1, wd),
                      name="adamw_" + nm)
        res[nm] = [o_.reshape(shp) for o_ in outs]

    sg_names = ["dmx", "dmc", "norm_g", "ssd_conv_w", "ssd_conv_b", "ssd_dt_bias", "ssd_A_log", "ssd_D", "ssd_norm_g",
                "gm_v_g", "gm_v_b", "gm_w_s", "gm_b_s"]
    sg_shapes = [g[n].shape for n in sg_names]
    flat = jnp.concatenate([g[n].reshape(-1) for n in sg_names])
    npack = flat.shape[0]
    pad = (-npack) % 1024
    flat = jnp.pad(flat, (0, pad)).reshape(-1, 128)
    (sg_all,) = _exchange([flat], scatter=False, name="gather_small_grads")
    sg_sum = _sum_slots(sg_all, name="sum_small_grads").reshape(-1)[:npack]
    sums = {}
    o = 0
    for n, shp in zip(sg_names, sg_shapes):
        sz = math.prod(shp)
        sums[n] = sg_sum[o:o + sz].reshape(shp)
        o += sz
    per_dev = sg_all.reshape(NDEV, -1)
    dmx_all = per_dev[:, :2 * N_MOD * d].reshape(NDEV, 2, N_MOD * d)
    dmc_all = per_dev[:, 2 * N_MOD * d:4 * N_MOD * d].reshape(NDEV, 2, N_MOD * d)

    (s16,) = _rowwise("ada_silu", lambda cc: ((_silu(cc),), ()), 2 * NDEV, [c16], [], [(d, F32)], tm=2 * NDEV)
    s16_t = s16.T
    g_ada_w, dcc_parts = [], []
    for i in range(2):
        rhs = jnp.concatenate([lax.dynamic_slice_in_dim(dmx_all[:, i], me * ncol, ncol, axis=1),
                               lax.dynamic_slice_in_dim(dmc_all[:, i], me * ncol, ncol, axis=1)], axis=0)
        g_ada_w.append(_mm_f32(s16_t, rhs, name=f"ada_dw{i}"))
        dmc_loc = lax.dynamic_slice_in_dim(sums["dmc"][i], me * ncol, ncol, axis=0)
        rhs_c = jnp.zeros((ncol, 128), F32).at[:, 0].set(dmc_loc)
        dcc_parts.append(_mm_f32(ada_w[i], rhs_c, name=f"ada_dcc{i}")[:, 0])
    g_ada_w = jnp.stack(g_ada_w)
    dcc_part = (dcc_parts[0] + dcc_parts[1]).reshape(8, 128)
    (dcc_all,) = _exchange([dcc_part], scatter=False, name="gather_dcc")
    g_c_ctx = _sum_slots(dcc_all, name="sum_dcc", scale_by=c_ctx.reshape(8, 128)).reshape(d)
    g_ada_b = sums["dmx"] + sums["dmc"]

    outs = _adamw(g_ada_w.reshape(1, -1, ncol), ada_w.reshape(-1, ncol), m_ada_w.reshape(-1, ncol),
                  v_ada_w.reshape(-1, ncol), name="adamw_ada_w")
    res["ada_w"] = [o_.reshape(ada_w.shape) for o_ in outs]

    loc = lambda a, ax, n: lax.dynamic_slice_in_dim(a, me * n, n, axis=ax)
    small_g = dict(c_ctx=g_c_ctx, ada_b=g_ada_b, norm_g=loc(sums["norm_g"], 2, 128),
                   ssd_conv_w=loc(sums["ssd_conv_w"], 1, 512)[None], ssd_conv_b=sums["ssd_conv_b"][None],
                   ssd_dt_bias=sums["ssd_dt_bias"][None], ssd_A_log=sums["ssd_A_log"][None], ssd_D=sums["ssd_D"][None],
                   ssd_norm_g=sums["ssd_norm_g"][None], gm_v_g=loc(sums["gm_v_g"], 0, 256)[None],
                   gm_v_b=loc(sums["gm_v_b"], 0, 256)[None], gm_w_s=sums["gm_w_s"][None], gm_b_s=sums["gm_b_s"][None])
    small_w = dict(c_ctx=(c_ctx, m_c_ctx, v_c_ctx), ada_b=(ada_b, m_ada_b, v_ada_b), norm_g=(norm_g, m_norm_g, v_norm_g),
                   ssd_conv_w=(ssd_conv_w, m_ssd_conv_w, v_ssd_conv_w), ssd_conv_b=(ssd_conv_b, m_ssd_conv_b, v_ssd_conv_b),
                   ssd_dt_bias=(ssd_dt_bias, m_ssd_dt_bias, v_ssd_dt_bias), ssd_A_log=(ssd_A_log, m_ssd_A_log, v_ssd_A_log),
                   ssd_D=(ssd_D, m_ssd_D, v_ssd_D), ssd_norm_g=(ssd_norm_g, m_ssd_norm_g, v_ssd_norm_g),
                   gm_v_g=(gm_v_g, m_gm_v_g, v_gm_v_g), gm_v_b=(gm_v_b, m_gm_v_b, v_gm_v_b),
                   gm_w_s=(gm_w_s, m_gm_w_s, v_gm_w_s), gm_b_s=(gm_b_s, m_gm_b_s, v_gm_b_s))
    sn = list(small_w)

    def pack(arrs):
        f = jnp.concatenate([a.reshape(-1) for a in arrs])
        return jnp.pad(f, (0, (-f.shape[0]) % 1024)).reshape(-1, 128)

    pg = pack([small_g[n].reshape(small_w[n][0].shape) for n in sn])
    outs = _adamw(pg[None], pack([small_w[n][0] for n in sn]), pack([small_w[n][1] for n in sn]),
                  pack([small_w[n][2] for n in sn]), name="adamw_small")
    flat_outs = [o_.reshape(-1) for o_ in outs]
    o = 0
    for n in sn:
        shp = small_w[n][0].shape
        sz = math.prod(shp)
        res[n] = [fo[o:o + sz].reshape(shp) for fo in flat_outs]
        o += sz

    order = ["c_ctx", "ada_w", "ada_b", "norm_g", "ffn_w_in", "ffn_w_out", "ssd_w_in", "ssd_conv_w", "ssd_conv_b",
             "ssd_dt_bias", "ssd_A_log", "ssd_D", "ssd_norm_g", "ssd_w_out", "gm_w_in", "gm_v_g", "gm_v_b", "gm_w_s",
             "gm_b_s", "gm_w_out"]
    result = [loss, grad_x[None]]
    for k in range(4):
        result += [res[n][k] for n in order]
    return tuple(result)
```

```python
import functools
import math

import jax
import jax.numpy as jnp
from jax import lax
from jax.experimental import pallas as pl
from jax.experimental.pallas import tpu as pltpu

F32 = jnp.float32
BF16 = jnp.bfloat16

NDEV = 8
D_MODEL = 1024
FFN_DIM = 2816
N_MOD = 9
EPS = 1e-6
SSD_INNER = 2048
SSD_HEADS = 32
SSD_HEAD_DIM = 64
SSD_GROUPS = 8
SSD_HPG = 4
SSD_STATE = 128
SSD_CONV = 5
SSD_CONV_DIM = 4096
CHUNK = 128
GM_INNER = 2048
GM_GROUPS = 8
GM_GROUP_DIM = 256
ADAM_LR = 0.001
ADAM_B1 = 0.9
ADAM_B2 = 0.999
ADAM_EPS = 1e-08
ADAM_WD = 0.01
ADAM_STEP = 10
NEG_BIG = -1e30
VMEM_LIMIT_BYTES = 56 * 1024 * 1024
HI = lax.Precision.HIGHEST


def _params(*sem):
    return pltpu.CompilerParams(dimension_semantics=sem, vmem_limit_bytes=VMEM_LIMIT_BYTES)


def _pick(n, target, mult=16):
    if n <= target:
        return n
    for t in range(target - target % mult, 0, -mult):
        if n % t == 0:
            return t
    raise ValueError((n, target, mult))


def _sig(x):
    return 1.0 / (1.0 + jnp.exp(-x))


def _silu(x):
    return x * _sig(x)


def _dsilu(x):
    s = _sig(x)
    return s * (1.0 + x * (1.0 - s))


_GELU_C = math.sqrt(2.0 / math.pi)


def _gelu(x):
    return 0.5 * x * (1.0 + jnp.tanh(_GELU_C * (x + 0.044715 * x * x * x)))


def _dgelu(x):
    t = jnp.tanh(_GELU_C * (x + 0.044715 * x * x * x))
    return 0.5 * (1.0 + t) + 0.5 * x * (1.0 - t * t) * _GELU_C * (1.0 + 3.0 * 0.044715 * x * x)


def _softplus(x):
    return jnp.maximum(x, 0.0) + jnp.log1p(jnp.exp(-jnp.abs(x)))


def _sum0(v):
    return jnp.sum(v, axis=0, keepdims=True)


def _rms(h):
    r = lax.rsqrt(jnp.mean(h * h, axis=-1, keepdims=True) + EPS)
    return h * r, r


def _dot(a, b, dims=((1,), (0,)), precision=None):
    return lax.dot_general(a, b, (dims, ((), ())), preferred_element_type=F32, precision=precision)


_NT = ((1,), (1,))
_TN = ((0,), (0,))


def _rowwise(name, fn, n_rows, rows, consts, outs, accs=(), *, tm, nseg=1, seg_blocks=0):
    assert n_rows % tm == 0
    if nseg == 2:
        assert seg_blocks > 0
        seg = lambda i: jnp.where(i < seg_blocks, 0, 1)
    else:
        seg = lambda i: 0
    in_specs, args = [], []
    for r in rows:
        arr, width, cb, off = r if isinstance(r, tuple) else (r, r.shape[1], 0, 0)
        in_specs.append(pl.BlockSpec((tm, width), lambda i, cb=cb, off=off: (i + off, cb)))
        args.append(arr)
    for kind, arr in consts:
        if kind == "seg":
            assert arr.shape[0] == nseg and arr.shape[1] == 1, arr.shape
            in_specs.append(pl.BlockSpec((None, 1, arr.shape[2]), lambda i: (seg(i), 0, 0)))
        else:
            in_specs.append(pl.BlockSpec(arr.shape, lambda i: (0, 0)))
        args.append(arr)
    out_shape = [jax.ShapeDtypeStruct((n_rows, w), dt) for w, dt in outs]
    out_specs = [pl.BlockSpec((tm, w), lambda i: (i, 0)) for w, _ in outs]
    out_shape += [jax.ShapeDtypeStruct((nseg, 1, w), F32) for w in accs]
    out_specs += [pl.BlockSpec((None, 1, w), lambda i: (seg(i), 0, 0)) for w in accs]
    n_in, n_out, n_acc = len(args), len(outs), len(accs)

    def kern(*refs):
        ins = [r[...] for r in refs[:n_in]]
        res, sums = fn(*ins)
        for ref, v in zip(refs[n_in:n_in + n_out], res):
            ref[...] = v.astype(ref.dtype)
        if n_acc:
            i = pl.program_id(0)
            first = (i == 0) | (i == seg_blocks) if nseg == 2 else (i == 0)
            acc_refs = refs[n_in + n_out:]

            @pl.when(first)
            def _():
                for ref, v in zip(acc_refs, sums):
                    ref[...] = v

            @pl.when(jnp.logical_not(first))
            def _():
                for ref, v in zip(acc_refs, sums):
                    ref[...] += v

    res = pl.pallas_call(
        kern, name=name, grid=(n_rows // tm,), in_specs=in_specs, out_specs=out_specs, out_shape=out_shape,
        compiler_params=_params("arbitrary"),
    )(*args)
    return res


def _pre_fwd_fn(h, g, shift, scale):
    hh, _ = _rms(h)
    return (hh * g * (1.0 + scale) + shift,), ()


def _pre_bwd_fn(du, h, dres, g, scale):
    hh, r = _rms(h)
    n = hh * g
    dn = du * (1.0 + scale)
    dhh = dn * g
    dh = dres + r * (dhh - hh * jnp.mean(dhh * hh, axis=-1, keepdims=True))
    return (dh,), (_sum0(du), _sum0(du * n), _sum0(dn * hh))


def _post_fwd_fn(weight, h, y, g, gate):
    yh, _ = _rms(y)
    return (h + weight * gate * (yh * g),), ()


def _post_bwd_fn(weight, dh, y, g, gate):
    yh, r = _rms(y)
    dr = dh * weight
    dyh = dr * gate * g
    dy = r * (dyh - yh * jnp.mean(dyh * yh, axis=-1, keepdims=True))
    return (dy,), (_sum0(dr * yh * g), _sum0(dr * gate * yh))


def _glu_bwd_fn(ds, a, b):
    a = a.astype(F32)
    b = b.astype(F32)
    sg = _sig(a)
    da = ds * b * (sg * (1.0 + a * (1.0 - sg)))
    db = ds * (a * sg)
    return (jnp.concatenate([da, db], axis=1),), ()


def _loss_fn(y, t):
    diff = y - t
    return (diff * (1.0 / D_MODEL),), (_sum0(diff * diff),)


def _ssd_y(yf, yb, xs, z, dvec):
    y = yf + yb + dvec * xs
    return y, y * _silu(z)


def _ssdgate_fwd_fn(yf, yb, xs, z, dvec, ng):
    _, yg = _ssd_y(yf, yb, xs, z, dvec)
    parts = []
    for g in range(SSD_GROUPS):
        sl = slice(g * 256, (g + 1) * 256)
        parts.append(_rms(yg[:, sl])[0])
    return (jnp.concatenate(parts, axis=1) * ng,), ()


def _ssdgate_bwd_fn(dyn, yf, yb, xs, z, dvec, ng):
    y, yg = _ssd_y(yf, yb, xs, z, dvec)
    dyg_parts, ygh_parts = [], []
    for g in range(SSD_GROUPS):
        sl = slice(g * 256, (g + 1) * 256)
        ygh, r = _rms(yg[:, sl])
        d = dyn[:, sl] * ng[:, sl]
        dyg_parts.append(r * (d - ygh * jnp.mean(d * ygh, axis=-1, keepdims=True)))
        ygh_parts.append(ygh)
    dyg = jnp.concatenate(dyg_parts, axis=1)
    ygh = jnp.concatenate(ygh_parts, axis=1)
    dy = dyg * _silu(z)
    dz = dyg * y * _dsilu(z)
    return (dy, dz), (_sum0(dyn * ygh), _sum0(dy * xs))


def _ln_stats(v):
    mu = jnp.mean(v, axis=-1, keepdims=True)
    vc = v - mu
    r = lax.rsqrt(jnp.mean(vc * vc, axis=-1, keepdims=True) + EPS)
    return vc * r, r


def _gm_act_fwd_fn(p, vg, vb):
    gu = _gelu(p[:, :GM_INNER])
    gvh, _ = _ln_stats(_gelu(p[:, GM_INNER:]))
    return (gu, gvh * vg + vb), ()


def _gm_act_bwd_fn(p, dgu, dgvn, vg):
    pu = p[:, :GM_INNER]
    pv = p[:, GM_INNER:]
    gvh, r = _ln_stats(_gelu(pv))
    dgvh = dgvn * vg
    dgv = r * (dgvh - jnp.mean(dgvh, axis=-1, keepdims=True) - gvh * jnp.mean(dgvh * gvh, axis=-1, keepdims=True))
    dp = jnp.concatenate([dgu * _dgelu(pu), dgv * _dgelu(pv)], axis=1)
    return (dp,), (_sum0(dgvn * gvh), _sum0(dgvn))


def _mm(a, b, *, out_dtype, name, tm=640, tn=512, tk=1024, add=None):
    m, k = a.shape
    k2, n = b.shape
    assert k == k2
    tm, tn, tk = _pick(m, tm), _pick(n, tn, 128), _pick(k, tk, 128)
    nk = k // tk

    def kern(*refs):
        if add is None:
            a_ref, b_ref, o_ref, acc_ref = refs
        else:
            a_ref, b_ref, add_ref, o_ref, acc_ref = refs
        kk = pl.program_id(2)

        @pl.when(kk == 0)
        def _():
            acc_ref[...] = jnp.zeros_like(acc_ref)

        acc_ref[...] += jnp.dot(a_ref[...], b_ref[...], preferred_element_type=F32)

        @pl.when(kk == nk - 1)
        def _():
            r = acc_ref[...]
            if add is not None:
                r = r + add_ref[...]
            o_ref[...] = r.astype(o_ref.dtype)

    in_specs = [pl.BlockSpec((tm, tk), lambda i, j, kk: (i, kk)), pl.BlockSpec((tk, tn), lambda i, j, kk: (kk, j))]
    args = [a, b]
    if add is not None:
        in_specs.append(pl.BlockSpec((tm, tn), lambda i, j, kk: (i, j)))
        args.append(add)
    return pl.pallas_call(
        kern, name=name, grid=(m // tm, n // tn, nk), in_specs=in_specs,
        out_specs=pl.BlockSpec((tm, tn), lambda i, j, kk: (i, j)),
        out_shape=jax.ShapeDtypeStruct((m, n), out_dtype),
        scratch_shapes=[pltpu.VMEM((tm, tn), F32)],
        compiler_params=_params("parallel", "parallel", "arbitrary"),
    )(*args)


def _mm_glu(u, wa, wb, *, name, tm=640, tn=256):
    m, k = u.shape
    n = wa.shape[1]
    tm, tn = _pick(m, tm), _pick(n, tn, 128)

    def kern(u_ref, wa_ref, wb_ref, s_ref, a_ref, b_ref):
        uu = u_ref[...]
        a = jnp.dot(uu, wa_ref[...], preferred_element_type=F32)
        b = jnp.dot(uu, wb_ref[...], preferred_element_type=F32)
        s_ref[...] = (_silu(a) * b).astype(BF16)
        a_ref[...] = a.astype(BF16)
        b_ref[...] = b.astype(BF16)

    ospec = pl.BlockSpec((tm, tn), lambda i, j: (i, j))
    return pl.pallas_call(
        kern, name=name, grid=(m // tm, n // tn),
        in_specs=[pl.BlockSpec((tm, k), lambda i, j: (i, 0)), pl.BlockSpec((k, tn), lambda i, j: (0, j)),
                  pl.BlockSpec((k, tn), lambda i, j: (0, j))],
        out_specs=[ospec, ospec, ospec],
        out_shape=[jax.ShapeDtypeStruct((m, n), BF16)] * 3,
        compiler_params=_params("parallel", "parallel"),
    )(u, wa, wb)


def _mm_tn(a, b, *, name, tm=512, tn=512, tk=512):
    t, m = a.shape
    t2, n = b.shape
    assert t == t2
    tm, tn, tk = _pick(m, tm, 128), _pick(n, tn, 128), _pick(t, tk)
    nk = t // tk

    def kern(a_ref, b_ref, o_ref):
        kk = pl.program_id(2)

        @pl.when(kk == 0)
        def _():
            o_ref[...] = jnp.zeros_like(o_ref)

        o_ref[...] += _dot(a_ref[...], b_ref[...], _TN)

    return pl.pallas_call(
        kern, name=name, grid=(m // tm, n // tn, nk),
        in_specs=[pl.BlockSpec((tk, tm), lambda i, j, kk: (kk, i)), pl.BlockSpec((tk, tn), lambda i, j, kk: (kk, j))],
        out_specs=pl.BlockSpec((tm, tn), lambda i, j, kk: (i, j)),
        out_shape=jax.ShapeDtypeStruct((m, n), F32),
        compiler_params=_params("parallel", "parallel", "arbitrary"),
    )(a, b)


def _mm_f32(a, b, *, name, silu_a=False, bias=None):
    m, k = a.shape
    n = b.shape[1]

    def kern(*refs):
        if bias is None:
            a_ref, b_ref, o_ref = refs
        else:
            a_ref, b_ref, bias_ref, o_ref = refs
        av = a_ref[...]
        if silu_a:
            av = _silu(av)
        r = jnp.dot(av, b_ref[...], preferred_element_type=F32, precision=HI)
        if bias is not None:
            r = r + bias_ref[...]
        o_ref[...] = r

    args = [a, b] + ([] if bias is None else [bias])
    return pl.pallas_call(kern, name=name, out_shape=jax.ShapeDtypeStruct((m, n), F32),
                          compiler_params=pltpu.CompilerParams(vmem_limit_bytes=VMEM_LIMIT_BYTES))(*args)


def _shifted(v, s, t, lo, hi):
    n = v.shape[0]
    r = v if s == 0 else pltpu.roll(v, (-s) % n, 0)
    ok = (t + s >= lo) & (t + s < hi)
    return jnp.where(ok, r, 0.0)


def _seg_bounds(n, n_ctx):
    t = lax.broadcasted_iota(jnp.int32, (n, 1), 0)
    lo = jnp.where(t < n_ctx, 0, n_ctx)
    hi = jnp.where(t < n_ctx, n_ctx, n)
    return t, lo, hi


def _conv_fwd(xp, w8, b, *, n_ctx, name, cb=256):
    n, c = xp.shape

    def kern(x_ref, w_ref, b_ref, cpre_ref, act_ref):
        x = x_ref[...]
        t, lo, hi = _seg_bounds(n, n_ctx)
        acc = jnp.zeros_like(x) + b_ref[...]
        for k in range(SSD_CONV):
            acc = acc + _shifted(x, k - SSD_CONV // 2, t, lo, hi) * w_ref[k:k + 1, :]
        cpre_ref[...] = acc
        act_ref[...] = _silu(acc)

    spec = pl.BlockSpec((n, cb), lambda j: (0, j))
    return pl.pallas_call(
        kern, name=name, grid=(c // cb,),
        in_specs=[spec, pl.BlockSpec((8, cb), lambda j: (0, j)), pl.BlockSpec((1, cb), lambda j: (0, j))],
        out_specs=[spec, spec], out_shape=[jax.ShapeDtypeStruct((n, c), F32)] * 2,
        compiler_params=_params("parallel"),
    )(xp, w8, b)


def _conv_bwd(d1, d2, cpre, xp, w8, *, n_ctx, name, cb=128):
    n, c = xp.shape

    def kern(d1_ref, d2_ref, cpre_ref, x_ref, w_ref, dx_ref, dw_ref, db_ref):
        g = (d1_ref[...] + d2_ref[...]) * _dsilu(cpre_ref[...])
        x = x_ref[...]
        t, lo, hi = _seg_bounds(n, n_ctx)
        dx = jnp.zeros_like(g)
        dw_ref[...] = jnp.zeros_like(dw_ref)
        for k in range(SSD_CONV):
            s = k - SSD_CONV // 2
            dx = dx + _shifted(g, -s, t, lo, hi) * w_ref[k:k + 1, :]
            dw_ref[k:k + 1, :] = _sum0(g * _shifted(x, s, t, lo, hi))
        dx_ref[...] = dx.astype(BF16)
        db_ref[...] = _sum0(g)

    spec = pl.BlockSpec((n, cb), lambda j: (0, j))
    return pl.pallas_call(
        kern, name=name, grid=(c // cb,),
        in_specs=[spec, spec, spec, spec, pl.BlockSpec((8, cb), lambda j: (0, j))],
        out_specs=[spec, pl.BlockSpec((8, cb), lambda j: (0, j)), pl.BlockSpec((1, cb), lambda j: (0, j))],
        out_shape=[jax.ShapeDtypeStruct((n, c), BF16), jax.ShapeDtypeStruct((8, c), F32),
                   jax.ShapeDtypeStruct((1, c), F32)],
        compiler_params=_params("parallel"),
    )(d1, d2, cpre, xp, w8)


def _chunk_of(s, nc, n_ctx_chunks, rev):
    if not rev:
        return s
    return jnp.where(s < n_ctx_chunks, n_ctx_chunks - 1 - s, nc - 1 - (s - n_ctx_chunks))


def _scan_common(dt_raw, dtT_raw, bias_r, bias_c, alog_r, alog_c, rev):
    ii = lax.broadcasted_iota(jnp.int32, (CHUNK, CHUNK), 0)
    jj = lax.broadcasted_iota(jnp.int32, (CHUNK, CHUNK), 1)
    tri = (jj >= ii) if rev else (jj <= ii)
    tri_t = (ii >= jj) if rev else (ii <= jj)
    a_r = -jnp.exp(alog_r)
    a_c = -jnp.exp(alog_c)
    dt = _softplus(dt_raw + bias_r)
    dt_t = _softplus(dtT_raw + bias_c)
    al = dt * a_r
    acum = _dot(tri.astype(F32), al, precision=HI)
    acum_t = _dot(dt_t * a_c, tri_t.astype(F32), precision=HI)
    atot = _sum0(al)
    return tri, tri_t, a_r, dt, acum, acum_t, atot


def _ssd_scan_fwd(xbc, dt_raw, dtT_raw, bias_r, bias_c, alog_r, alog_c, *, rev, n_ctx_chunks, name):
    n = xbc.shape[0]
    nc = n // CHUNK
    cidx = functools.partial(_chunk_of, nc=nc, n_ctx_chunks=n_ctx_chunks, rev=rev)

    def kern(xs_ref, b_ref, c_ref, dt_ref, dtT_ref, br_ref, bc_ref, ar_ref, ac_ref, y_ref, hs_ref, h_scr):
        @pl.when(pl.program_id(0) == 0)
        def _():
            h_scr[...] = jnp.zeros_like(h_scr)

        tri, _, _, dt, acum, acum_t, atot = _scan_common(
            dt_ref[...], dtT_ref[...], br_ref[...], bc_ref[...], ar_ref[...], ac_ref[...], rev)
        ea = jnp.exp(acum)
        dec_end = jnp.exp(atot - acum)
        etot = jnp.exp(atot)
        hs_ref[...] = h_scr[...]
        for g in range(SSD_GROUPS):
            bg = b_ref[:, g * SSD_STATE:(g + 1) * SSD_STATE].astype(BF16)
            cg = c_ref[:, g * SSD_STATE:(g + 1) * SSD_STATE].astype(BF16)
            cb = _dot(cg, bg, _NT)
            ys = []
            for k in range(SSD_HPG):
                h = g * SSD_HPG + k
                lmat = jnp.exp(jnp.where(tri, acum[:, h:h + 1] - acum_t[h:h + 1, :], NEG_BIG))
                m = (cb * lmat).astype(BF16)
                xdt = xs_ref[:, h * SSD_HEAD_DIM:(h + 1) * SSD_HEAD_DIM] * dt[:, h:h + 1]
                hh = h_scr[h]
                y = _dot(m, xdt.astype(BF16)) + _dot(cg, hh.astype(BF16), _NT) * ea[:, h:h + 1]
                ys.append(y)
                xdw = (xdt * dec_end[:, h:h + 1]).astype(BF16)
                h_scr[h] = hh * etot[:, h:h + 1] + _dot(xdw, bg, _TN)
            y_ref[:, g * 256:(g + 1) * 256] = jnp.concatenate(ys, axis=1)

    nh = SSD_HEADS
    small = lambda shape: pl.BlockSpec(shape, lambda s: (0, 0))
    return pl.pallas_call(
        kern, name=name, grid=(nc,),
        in_specs=[pl.BlockSpec((CHUNK, SSD_INNER), lambda s: (cidx(s), 0)),
                  pl.BlockSpec((CHUNK, 1024), lambda s: (cidx(s), 2)),
                  pl.BlockSpec((CHUNK, 1024), lambda s: (cidx(s), 3)),
                  pl.BlockSpec((CHUNK, nh), lambda s: (cidx(s), 0)),
                  pl.BlockSpec((nh, CHUNK), lambda s: (0, cidx(s))),
                  small((1, nh)), small((nh, 1)), small((1, nh)), small((nh, 1))],
        out_specs=[pl.BlockSpec((CHUNK, SSD_INNER), lambda s: (cidx(s), 0)),
                   pl.BlockSpec((None, nh, SSD_HEAD_DIM, SSD_STATE), lambda s: (s, 0, 0, 0))],
        out_shape=[jax.ShapeDtypeStruct((n, SSD_INNER), F32),
                   jax.ShapeDtypeStruct((nc, nh, SSD_HEAD_DIM, SSD_STATE), F32)],
        scratch_shapes=[pltpu.VMEM((nh, SSD_HEAD_DIM, SSD_STATE), F32)],
        compiler_params=_params("arbitrary"),
    )(xbc, xbc, xbc, dt_raw, dtT_raw, bias_r, bias_c, alog_r, alog_c)


def _ssd_scan_bwd(dy, xbc, hs, dt_raw, dtT_raw, bias_r, bias_c, alog_r, alog_c, dvec, *, rev, n_ctx_chunks,
                  direct, name):
    n = xbc.shape[0]
    nc = n // CHUNK
    nh = SSD_HEADS
    step_of = lambda r: nc - 1 - r
    cidx = lambda r: _chunk_of(step_of(r), nc, n_ctx_chunks, rev)

    def kern(dy_ref, xs_ref, b_ref, c_ref, hs_ref, dt_ref, dtT_ref, br_ref, bc_ref, ar_ref, ac_ref, dv_ref,
             dx_ref, ddt_ref, dal_ref, dbias_ref, dh_scr):
        @pl.when(pl.program_id(0) == 0)
        def _():
            dh_scr[...] = jnp.zeros_like(dh_scr)
            dal_ref[...] = jnp.zeros_like(dal_ref)
            dbias_ref[...] = jnp.zeros_like(dbias_ref)

        tri, tri_t, a_r, dt, acum, acum_t, atot = _scan_common(
            dt_ref[...], dtT_ref[...], br_ref[...], bc_ref[...], ar_ref[...], ac_ref[...], rev)
        ea = jnp.exp(acum)
        dec_end = jnp.exp(atot - acum)
        etot = jnp.exp(atot)
        ii = lax.broadcasted_iota(jnp.int32, (CHUNK, CHUNK), 0)
        jj = lax.broadcasted_iota(jnp.int32, (CHUNK, CHUNK), 1)
        strict = tri & (ii != jj)
        tri_t_bf = tri_t.astype(BF16)
        lane = lax.broadcasted_iota(jnp.int32, (CHUNK, nh), 1)
        lane1 = lax.broadcasted_iota(jnp.int32, (1, nh), 1)
        ycol = jnp.zeros((CHUNK, nh), F32)
        dal = jnp.zeros((CHUNK, nh), F32)
        ddt = jnp.zeros((CHUNK, nh), F32)
        dtot = jnp.zeros((1, nh), F32)
        for g in range(SSD_GROUPS):
            bg = b_ref[:, g * SSD_STATE:(g + 1) * SSD_STATE].astype(BF16)
            cg = c_ref[:, g * SSD_STATE:(g + 1) * SSD_STATE].astype(BF16)
            cb = _dot(cg, bg, _NT)
            dcb = jnp.zeros((CHUNK, CHUNK), F32)
            dbg = jnp.zeros((CHUNK, SSD_STATE), F32)
            dcg = jnp.zeros((CHUNK, SSD_STATE), F32)
            dxs = []
            for k in range(SSD_HPG):
                h = g * SSD_HPG + k
                hsl = slice(h * SSD_HEAD_DIM, (h + 1) * SSD_HEAD_DIM)
                sel = lane == h
                lmat = jnp.exp(jnp.where(tri, acum[:, h:h + 1] - acum_t[h:h + 1, :], NEG_BIG))
                mf = cb * lmat
                xh = xs_ref[:, hsl]
                dtc = dt[:, h:h + 1]
                xdt = xh * dtc
                dyh = dy_ref[:, hsl]
                dyh_bf = dyh.astype(BF16)
                hst = hs_ref[h]
                hst_bf = hst.astype(BF16)
                dh = dh_scr[h]
                dh_bf = dh.astype(BF16)
                eac = ea[:, h:h + 1]
                dec = dec_end[:, h:h + 1]
                et = etot[:, h:h + 1]
                yoff = _dot(cg, hst_bf, _NT) * eac
                dyo_bf = (dyh * eac).astype(BF16)
                dcg = dcg + _dot(dyo_bf, hst_bf)
                dh_scr[h] = dh * et + _dot(dyo_bf, cg, _TN)
                col = jnp.sum(dyh * yoff, axis=1, keepdims=True)
                bdh = _dot(bg, dh_bf, _NT)
                dxdt = _dot(mf.astype(BF16), dyh_bf, _TN) + bdh * dec
                e = jnp.sum(xdt * bdh, axis=1, keepdims=True) * dec
                col = col - e
                dtot_h = _sum0(e) + _sum0(jnp.sum(dh * hst, axis=1, keepdims=True)) * et
                dbg = dbg + _dot((xdt * dec).astype(BF16), dh_bf)
                dm = _dot(dyh_bf, xdt.astype(BF16), _NT)
                dcb = dcb + dm * lmat
                xmat = _dot(tri_t_bf, (dm * mf).astype(BF16))
                dal_h = jnp.sum(jnp.where(strict, xmat, 0.0), axis=1, keepdims=True)
                ycol = ycol + jnp.where(sel, col, 0.0)
                dal = dal + jnp.where(sel, dal_h, 0.0)
                ddt = ddt + jnp.where(sel, jnp.sum(dxdt * xh, axis=1, keepdims=True), 0.0)
                dtot = dtot + jnp.where(lane1 == h, dtot_h, 0.0)
                dxh = dxdt * dtc
                if direct:
                    dxh = dxh + dyh * dv_ref[:, hsl]
                dxs.append(dxh)
            dcb_bf = dcb.astype(BF16)
            dcg = dcg + _dot(dcb_bf, bg)
            dbg = dbg + _dot(dcb_bf, cg, _TN)
            dx_ref[:, g * 256:(g + 1) * 256] = jnp.concatenate(dxs, axis=1)
            dx_ref[:, SSD_INNER + g * SSD_STATE:SSD_INNER + (g + 1) * SSD_STATE] = dbg
            dx_ref[:, SSD_INNER + 1024 + g * SSD_STATE:SSD_INNER + 1024 + (g + 1) * SSD_STATE] = dcg
        dal = dal + _dot(tri_t.astype(F32), ycol, precision=HI) + dtot
        ddt = ddt + dal * a_r
        ddt_raw = ddt * _sig(dt_ref[...] + br_ref[...])
        ddt_ref[...] = ddt_raw
        dal_ref[...] += _sum0(dal * dt) * a_r
        dbias_ref[...] += _sum0(ddt_raw)

    small = lambda shape: pl.BlockSpec(shape, lambda r: (0, 0))
    return pl.pallas_call(
        kern, name=name, grid=(nc,),
        in_specs=[pl.BlockSpec((CHUNK, SSD_INNER), lambda r: (cidx(r), 0)),
                  pl.BlockSpec((CHUNK, SSD_INNER), lambda r: (cidx(r), 0)),
                  pl.BlockSpec((CHUNK, 1024), lambda r: (cidx(r), 2)),
                  pl.BlockSpec((CHUNK, 1024), lambda r: (cidx(r), 3)),
                  pl.BlockSpec((None, nh, SSD_HEAD_DIM, SSD_STATE), lambda r: (step_of(r), 0, 0, 0)),
                  pl.BlockSpec((CHUNK, nh), lambda r: (cidx(r), 0)),
                  pl.BlockSpec((nh, CHUNK), lambda r: (0, cidx(r))),
                  small((1, nh)), small((nh, 1)), small((1, nh)), small((nh, 1)), small((1, SSD_INNER))],
        out_specs=[pl.BlockSpec((CHUNK, SSD_CONV_DIM), lambda r: (cidx(r), 0)),
                   pl.BlockSpec((CHUNK, nh), lambda r: (cidx(r), 0)),
                   small((1, nh)), small((1, nh))],
        out_shape=[jax.ShapeDtypeStruct((n, SSD_CONV_DIM), F32), jax.ShapeDtypeStruct((n, nh), F32),
                   jax.ShapeDtypeStruct((1, nh), F32), jax.ShapeDtypeStruct((1, nh), F32)],
        scratch_shapes=[pltpu.VMEM((nh, SSD_HEAD_DIM, SSD_STATE), F32)],
        compiler_params=_params("arbitrary"),
    )(dy, xbc, xbc, xbc, hs, dt_raw, dtT_raw, bias_r, bias_c, alog_r, alog_c, dvec)


def _gm_spatial_fwd(gu, gvn, ws, bst, *, name):
    n = gu.shape[0]

    def kern(gu_ref, gv_ref, ws_ref, bs_ref, o_ref):
        for g in range(GM_GROUPS):
            sl = slice(g * GM_GROUP_DIM, (g + 1) * GM_GROUP_DIM)
            s = _dot(ws_ref[g], gv_ref[:, sl]) + bs_ref[:, g:g + 1]
            o_ref[:, sl] = (gu_ref[:, sl] * s).astype(BF16)

    spec = pl.BlockSpec((CHUNK, GM_INNER), lambda i: (i, 0))
    return pl.pallas_call(
        kern, name=name, grid=(n // CHUNK,),
        in_specs=[spec, spec, pl.BlockSpec(ws.shape, lambda i: (0, 0, 0)), pl.BlockSpec(bst.shape, lambda i: (0, 0))],
        out_specs=spec, out_shape=jax.ShapeDtypeStruct((n, GM_INNER), BF16),
        compiler_params=_params("parallel"),
    )(gu, gvn, ws, bst)


def _gm_spatial_bwd(dt, gu, gvn, ws, wst, bst, *, name):
    n = gu.shape[0]

    def kern(dt_ref, gu_ref, gv_ref, ws_ref, wst_ref, bs_ref, dgu_ref, dgv_ref, dws_ref, dbs_ref):
        @pl.when(pl.program_id(0) == 0)
        def _():
            dws_ref[...] = jnp.zeros_like(dws_ref)
            dbs_ref[...] = jnp.zeros_like(dbs_ref)

        lane = lax.broadcasted_iota(jnp.int32, (CHUNK, GM_GROUPS), 1)
        dbs = jnp.zeros((CHUNK, GM_GROUPS), F32)
        for g in range(GM_GROUPS):
            sl = slice(g * GM_GROUP_DIM, (g + 1) * GM_GROUP_DIM)
            gv = gv_ref[:, sl]
            s = _dot(ws_ref[g], gv) + bs_ref[:, g:g + 1]
            d = dt_ref[:, sl]
            dgu_ref[:, sl] = d * s
            ds = d * gu_ref[:, sl]
            ds_bf = ds.astype(BF16)
            dws_ref[g] += _dot(ds_bf, gv, _NT)
            dgv_ref[:, sl] = _dot(wst_ref[g], ds_bf)
            dbs = dbs + jnp.where(lane == g, jnp.sum(ds, axis=1, keepdims=True), 0.0)
        dbs_ref[...] += dbs

    spec = pl.BlockSpec((CHUNK, GM_INNER), lambda i: (i, 0))
    wspec = pl.BlockSpec(ws.shape, lambda i: (0, 0, 0))
    bspec = pl.BlockSpec(bst.shape, lambda i: (0, 0))
    return pl.pallas_call(
        kern, name=name, grid=(n // CHUNK,),
        in_specs=[spec, spec, spec, wspec, wspec, bspec],
        out_specs=[spec, spec, wspec, bspec],
        out_shape=[jax.ShapeDtypeStruct((n, GM_INNER), F32), jax.ShapeDtypeStruct((n, GM_INNER), F32),
                   jax.ShapeDtypeStruct(ws.shape, F32), jax.ShapeDtypeStruct(bst.shape, F32)],
        compiler_params=_params("arbitrary"),
    )(dt, gu, gvn, ws, wst, bst)


def _adamw(parts, w, m, v, *, name, tm=256):
    ns, r, wd = parts.shape
    tm = _pick(r, tm, 8)

    def kern(p_ref, w_ref, m_ref, v_ref, g_ref, d_ref, nm_ref, nv_ref):
        g = p_ref[0].astype(F32)
        for s in range(1, ns):
            g = g + p_ref[s].astype(F32)
        m2 = ADAM_B1 * m_ref[...] + (1.0 - ADAM_B1) * g
        v2 = ADAM_B2 * v_ref[...] + (1.0 - ADAM_B2) * (g * g)
        m_hat = m2 / (1.0 - ADAM_B1 ** ADAM_STEP)
        v_hat = v2 / (1.0 - ADAM_B2 ** ADAM_STEP)
        g_ref[...] = g
        d_ref[...] = -ADAM_LR * (m_hat / (jnp.sqrt(v_hat) + ADAM_EPS) + ADAM_WD * w_ref[...])
        nm_ref[...] = m2
        nv_ref[...] = v2

    spec = pl.BlockSpec((tm, wd), lambda i: (i, 0))
    return pl.pallas_call(
        kern, name=name, grid=(r // tm,),
        in_specs=[pl.BlockSpec((ns, tm, wd), lambda i: (0, i, 0)), spec, spec, spec],
        out_specs=[spec] * 4, out_shape=[jax.ShapeDtypeStruct((r, wd), F32)] * 4,
        compiler_params=_params("parallel"),
    )(parts, w, m, v)


def _sum_slots(parts, *, name, scale_by=None):
    ns, r, wd = parts.shape

    def kern(*refs):
        p_ref, o_ref = refs[0], refs[-1]
        g = p_ref[0]
        for s in range(1, ns):
            g = g + p_ref[s]
        if scale_by is not None:
            g = g * _dsilu(refs[1][...])
        o_ref[...] = g

    args = [parts] + ([] if scale_by is None else [scale_by])
    return pl.pallas_call(kern, name=name, out_shape=jax.ShapeDtypeStruct((r, wd), F32),
                          compiler_params=pltpu.CompilerParams(vmem_limit_bytes=VMEM_LIMIT_BYTES))(*args)


def _mesh_pos():
    x, y, c = lax.axis_index("x"), lax.axis_index("y"), lax.axis_index("c")
    return x, y, c, 4 * x + 2 * y + c


def _flip(x, y, c, f):
    fx, fy, fc = (f >> 2) & 1, (f >> 1) & 1, f & 1
    px = 1 - x if fx else x
    py = 1 - y if fy else y
    pc = 1 - c if fc else c
    return (px, py, pc), 4 * px + 2 * py + pc


_HBM_SPEC = pl.BlockSpec(memory_space=pltpu.HBM)


def _exchange(arrays, *, scatter, name):
    na = len(arrays)
    if scatter:
        out_shape = [jax.ShapeDtypeStruct(a.shape, a.dtype) for a in arrays]
    else:
        out_shape = [jax.ShapeDtypeStruct((NDEV,) + a.shape, a.dtype) for a in arrays]

    def body(*refs):
        ins, outs = refs[:na], refs[na:2 * na]
        send_sems, recv_sems, local_sems = refs[2 * na:]
        x, y, c, me = _mesh_pos()
        copies = []
        for i in range(na):
            src_own = ins[i].at[me] if scatter else ins[i]
            lc = pltpu.make_async_copy(src_own, outs[i].at[me], local_sems.at[i])
            lc.start()
            copies.append(lc)
        sends = []
        for f in range(1, NDEV):
            peer, pidx = _flip(x, y, c, f)
            for i in range(na):
                k = i * (NDEV - 1) + f - 1
                src = ins[i].at[pidx] if scatter else ins[i]
                cp = pltpu.make_async_remote_copy(
                    src_ref=src, dst_ref=outs[i].at[me], send_sem=send_sems.at[k], recv_sem=recv_sems.at[k],
                    device_id=peer, device_id_type=pl.DeviceIdType.MESH)
                cp.start()
                sends.append(cp)
        for f in range(1, NDEV):
            peer, pidx = _flip(x, y, c, f)
            for i in range(na):
                k = i * (NDEV - 1) + f - 1
                src = ins[i].at[pidx] if scatter else ins[i]
                pltpu.make_async_remote_copy(
                    src_ref=src, dst_ref=outs[i].at[pidx], send_sem=send_sems.at[k], recv_sem=recv_sems.at[k],
                    device_id=peer, device_id_type=pl.DeviceIdType.MESH).wait_recv()
        for cp in sends:
            cp.wait_send()
        for lc in copies:
            lc.wait()

    return pl.pallas_call(
        body, name=name, out_shape=out_shape, in_specs=[_HBM_SPEC] * na, out_specs=[_HBM_SPEC] * na,
        scratch_shapes=[pltpu.SemaphoreType.DMA((na * (NDEV - 1),)), pltpu.SemaphoreType.DMA((na * (NDEV - 1),)),
                        pltpu.SemaphoreType.DMA((na,))],
        compiler_params=pltpu.CompilerParams(has_side_effects=True),
    )(*arrays)


_SEM_SPEC = pl.BlockSpec(memory_space=pltpu.SEMAPHORE)
_DATAFLOW = pltpu.SideEffectType.DATAFLOW_SIDE_EFFECTING


def _split_copies(srcs, lands, send_sems, recv_sems, scatter):
    x, y, c, me = _mesh_pos()
    pairs = []
    for f in range(1, NDEV):
        peer, pidx = _flip(x, y, c, f)
        for i in range(len(srcs)):
            k = i * (NDEV - 1) + f - 1
            src = srcs[i].at[pidx] if scatter else srcs[i]
            mk = lambda dst: pltpu.make_async_remote_copy(
                src_ref=src, dst_ref=dst, send_sem=send_sems.at[k], recv_sem=recv_sems.at[k],
                device_id=peer, device_id_type=pl.DeviceIdType.MESH)
            pairs.append((mk(lands[i].at[me]), mk(lands[i].at[pidx])))
    return pairs


def _exchange_start(srcs, lands, *, scatter, name):
    na = len(srcs)
    nsem = na * (NDEV - 1)

    def body(*refs):
        ins_src, ins_land = refs[:na], refs[na:2 * na]
        send_sems, recv_sems = refs[2 * na], refs[2 * na + 1]
        token = refs[-1]
        for start, _ in _split_copies(ins_src, ins_land, send_sems, recv_sems, scatter):
            start.start()
        token[...] = jnp.zeros_like(token)

    thru = [pltpu.HBM(a.shape, a.dtype) for a in list(srcs) + list(lands)]
    res = pl.pallas_call(
        body, name=name,
        out_shape=(pltpu.SemaphoreType.DMA((nsem,)), pltpu.SemaphoreType.DMA((nsem,)), *thru,
                   jax.ShapeDtypeStruct((8, 128), F32)),
        in_specs=[_HBM_SPEC] * (2 * na),
        out_specs=(_SEM_SPEC, _SEM_SPEC, *([_HBM_SPEC] * (2 * na)), pl.BlockSpec(memory_space=pltpu.VMEM)),
        input_output_aliases={i: 2 + i for i in range(2 * na)},
        compiler_params=pltpu.CompilerParams(has_side_effects=_DATAFLOW),
    )(*[pltpu.with_memory_space_constraint(a, pltpu.HBM) for a in list(srcs) + list(lands)])
    send_sems, recv_sems = res[0], res[1]
    return send_sems, recv_sems, res[2:2 + na], res[2 + na:2 + 2 * na], res[-1][0, 0]


def _exchange_wait(send_sems, recv_sems, srcs, lands, after, *, scatter, name):
    na = len(srcs)

    def body(*refs):
        ins_src, ins_land = refs[:na], refs[na:2 * na]
        s_sems, r_sems = refs[2 * na], refs[2 * na + 1]
        for sent, arrived in _split_copies(ins_src, ins_land, s_sems, r_sems, scatter):
            sent.wait_send()
            arrived.wait_recv()

    thru = [pltpu.HBM(a.shape, a.dtype) for a in list(srcs) + list(lands)]
    res = pl.pallas_call(
        body, name=name, out_shape=tuple(thru),
        in_specs=[_HBM_SPEC] * (2 * na) + [_SEM_SPEC, _SEM_SPEC, pl.BlockSpec(memory_space=pl.ANY)],
        out_specs=tuple([_HBM_SPEC] * (2 * na)),
        input_output_aliases={i: i for i in range(2 * na)},
        compiler_params=pltpu.CompilerParams(has_side_effects=_DATAFLOW),
    )(*srcs, *lands, send_sems, recv_sems, after)
    return res[na:]


def _landing(block, me):
    buf = lax.empty((NDEV,) + block.shape, block.dtype)
    return lax.dynamic_update_slice_in_dim(buf, block[None], me, axis=0)


def _seg_kw(nseg, n_ctx, tm):
    return dict(nseg=nseg, seg_blocks=(n_ctx // tm if nseg == 2 else 0))


def _ffn_fwd(tag, h, gpre, gpost, shift, scale, gate, w, *, nseg, n_ctx, tm):
    n = h.shape[0]
    kw = _seg_kw(nseg, n_ctx, tm)
    (u,) = _rowwise(tag + "_pre", _pre_fwd_fn, n, [h], [("full", gpre), ("seg", shift), ("seg", scale)],
                    [(D_MODEL, BF16)], tm=tm, **kw)
    s, a, b = _mm_glu(u, w["wa"], w["wb"], name=tag + "_glu")
    y = _mm(s, w["wout"], out_dtype=F32, name=tag + "_out", tn=512, tk=FFN_DIM)
    (ho,) = _rowwise(tag + "_post", functools.partial(_post_fwd_fn, 0.5), n, [h, y], [("full", gpost), ("seg", gate)],
                     [(D_MODEL, F32)], tm=tm, **kw)
    return ho, dict(h=h, u=u, s=s, a=a, b=b, y=y)


def _ffn_bwd(tag, dho, sv, gpre, gpost, scale, gate, w, *, nseg, n_ctx, tm):
    n = dho.shape[0]
    kw = _seg_kw(nseg, n_ctx, tm)
    dy, dgate, dgpost = _rowwise(tag + "_postb", functools.partial(_post_bwd_fn, 0.5), n, [dho, sv["y"]],
                                 [("full", gpost), ("seg", gate)], [(D_MODEL, BF16)], [D_MODEL, D_MODEL], tm=tm, **kw)
    dwout = _mm_tn(sv["s"], dy, name=tag + "_dwout", tm=1408, tn=1024)
    ds = _mm(dy, w["wout_t"], out_dtype=F32, name=tag + "_ds", tn=704)
    (dp,) = _rowwise(tag + "_glub", _glu_bwd_fn, n, [ds, sv["a"], sv["b"]], [], [(2 * FFN_DIM, BF16)], tm=min(tm, 128))
    dwin = _mm_tn(sv["u"], dp, name=tag + "_dwin", tm=1024, tn=512)
    du = _mm(dp, w["win_t"], out_dtype=F32, name=tag + "_du", tn=1024, tk=512)
    dh, dshift, dscale, dgpre = _rowwise(tag + "_preb", _pre_bwd_fn, n, [du, sv["h"], dho],
                                         [("full", gpre), ("seg", scale)], [(D_MODEL, F32)],
                                         [D_MODEL, D_MODEL, D_MODEL], tm=tm, **kw)
    return dh, dwin, dwout, dict(shift=dshift, scale=dscale, gate=dgate, gpre=dgpre, gpost=dgpost)


def _local_step(x, ctx, target, mods, norm_g, get_w, small, put_grad):
    t_len, n_ctx = x.shape[0], ctx.shape[0]
    n0 = t_len + n_ctx
    tm0 = _pick(n_ctx, 256, 8)
    tm1 = _pick(t_len, 256, 8)
    ncc = n_ctx // CHUNK
    g = {}

    def modrow(i, k, nseg):
        mc, mx = mods[i]
        if nseg == 2:
            return jnp.stack([mc[k], mx[k]])[:, None, :]
        return mx[k][None, None, :]

    pending = [None]

    def gvec(i, k):
        v = norm_g[i, k][None, :]
        if pending[0] is not None:
            v = v + pending[0]
            pending[0] = None
        return v

    xc = jnp.concatenate([ctx, x], axis=0)
    L0 = dict(nseg=2, n_ctx=n_ctx, tm=tm0)
    wts = dict(get_w("ffn00", xc))
    h1, sv_f01 = _ffn_fwd("l0f1", xc, gvec(0, 0), gvec(0, 1), modrow(0, 0, 2), modrow(0, 1, 2), modrow(0, 2, 2),
                          wts["ffn00"], **L0)
    kw0 = _seg_kw(2, n_ctx, tm0)
    (um0,) = _rowwise("l0m_pre", _pre_fwd_fn, n0, [h1], [("full", gvec(0, 2)), ("seg", modrow(0, 3, 2)),
                                                         ("seg", modrow(0, 4, 2))], [(D_MODEL, BF16)], tm=tm0, **kw0)
    wts.update(get_w("ssd", um0))
    z = _mm(um0, wts["ssd_wz"], out_dtype=F32, name="ssd_z", tm=544)
    xbc_pre = _mm(um0, wts["ssd_wxbc"], out_dtype=F32, name="ssd_xbc", tm=544)
    dtr = _mm(um0, wts["ssd_wdt"], out_dtype=F32, name="ssd_dt", tm=544)
    cpre, xbc = _conv_fwd(xbc_pre, small["conv_w8"], small["conv_b"], n_ctx=n_ctx, name="ssd_conv")
    nh = SSD_HEADS
    dt_dir = [dtr[:, :nh], dtr[:, nh:2 * nh]]
    dtT_dir = [d.T for d in dt_dir]
    bias_r = [small["dt_bias"][d][None, :] for d in range(2)]
    bias_c = [small["dt_bias"][d][:, None] for d in range(2)]
    alog_r = [small["a_log"][d][None, :] for d in range(2)]
    alog_c = [small["a_log"][d][:, None] for d in range(2)]
    ys, hss = [], []
    for d in range(2):
        yd, hsd = _ssd_scan_fwd(xbc, dt_dir[d], dtT_dir[d], bias_r[d], bias_c[d], alog_r[d], alog_c[d],
                                rev=(d == 1), n_ctx_chunks=ncc, name=f"ssd_scan{d}")
        ys.append(yd)
        hss.append(hsd)
    dvec = jnp.repeat(small["ssd_d"], SSD_HEAD_DIM)[None, :]
    ngv = small["ssd_norm_g"][None, :]
    gate_rows = [ys[0], ys[1], (xbc, SSD_INNER, 0, 0), z]
    (yn_all,) = _rowwise("ssd_gate", _ssdgate_fwd_fn, n0, gate_rows, [("full", dvec), ("full", ngv)],
                         [(SSD_INNER, BF16)], tm=128)
    yn = yn_all[n_ctx:]
    yo0 = _mm(yn, wts["ssd_wout"], out_dtype=F32, name="ssd_out", tn=1024, tk=1024)
    h1x = h1[n_ctx:]
    L1 = dict(nseg=1, n_ctx=0, tm=tm1)
    (h2,) = _rowwise("l0m_post", functools.partial(_post_fwd_fn, 1.0), t_len, [h1x, yo0],
                     [("full", gvec(0, 3)), ("seg", modrow(0, 5, 1))], [(D_MODEL, F32)], tm=tm1)
    wts.update(get_w("ffn01", h2))
    h3, sv_f02 = _ffn_fwd("l0f2", h2, gvec(0, 4), gvec(0, 5), modrow(0, 6, 1), modrow(0, 7, 1), modrow(0, 8, 1),
                          wts["ffn01"], **L1)

    wts.update(get_w("ffn10", h3))
    h4, sv_f11 = _ffn_fwd("l1f1", h3, gvec(1, 0), gvec(1, 1), modrow(1, 0, 1), modrow(1, 1, 1), modrow(1, 2, 1),
                          wts["ffn10"], **L1)
    (um1,) = _rowwise("l1m_pre", _pre_fwd_fn, t_len, [h4], [("full", gvec(1, 2)), ("seg", modrow(1, 3, 1)),
                                                            ("seg", modrow(1, 4, 1))], [(D_MODEL, BF16)], tm=tm1)
    wts.update(get_w("gm", um1))
    p1 = _mm(um1, wts["gm_win"], out_dtype=F32, name="gm_in")
    vg = small["gm_v_g"][None, :]
    vb = small["gm_v_b"][None, :]
    gu, gvn = _rowwise("gm_act", _gm_act_fwd_fn, t_len, [p1], [("full", vg), ("full", vb)],
                       [(GM_INNER, F32), (GM_INNER, BF16)], tm=128)
    ws_bf = small["gm_w_s"].astype(BF16)
    wst_bf = jnp.swapaxes(small["gm_w_s"], 1, 2).astype(BF16)
    bst = small["gm_b_s"].T
    tgm = _gm_spatial_fwd(gu, gvn, ws_bf, bst, name="gm_spatial")
    yo1 = _mm(tgm, wts["gm_wout"], out_dtype=F32, name="gm_out", tn=1024, tk=1024)
    (h5,) = _rowwise("l1m_post", functools.partial(_post_fwd_fn, 1.0), t_len, [h4, yo1],
                     [("full", gvec(1, 3)), ("seg", modrow(1, 5, 1))], [(D_MODEL, F32)], tm=tm1)
    wts.update(get_w("ffn11", h5))
    h6, sv_f12 = _ffn_fwd("l1f2", h5, gvec(1, 4), gvec(1, 5), modrow(1, 6, 1), modrow(1, 7, 1), modrow(1, 8, 1),
                          wts["ffn11"], **L1)

    dh, loss_parts = _rowwise("loss", _loss_fn, t_len, [h6, target], [], [(D_MODEL, F32)], [D_MODEL], tm=tm1)

    zero = jnp.zeros((D_MODEL,), F32)
    dmx = [[zero] * N_MOD for _ in range(2)]
    dmc = [[zero] * N_MOD for _ in range(2)]
    dng = [[zero] * 6 for _ in range(2)]

    def put_mod(i, k, acc):
        if acc.shape[0] == 2:
            dmc[i][k] = dmc[i][k] + acc[0, 0]
            dmx[i][k] = dmx[i][k] + acc[1, 0]
        else:
            dmx[i][k] = dmx[i][k] + acc[0, 0]

    def put_g(i, k, acc):
        dng[i][k] = dng[i][k] + jnp.sum(acc[:, 0], axis=0)

    def ffn_back(tag, i, j, dho, sv, w, lay):
        nseg = lay["nseg"]
        base = 0 if j == 0 else 6
        gi = 0 if j == 0 else 4
        dh_in, dwin, dwout, s = _ffn_bwd(tag, dho, sv, gvec(i, gi), gvec(i, gi + 1), modrow(i, base + 1, nseg),
                                         modrow(i, base + 2, nseg), w, **lay)
        put_mod(i, base, s["shift"])
        put_mod(i, base + 1, s["scale"])
        put_mod(i, base + 2, s["gate"])
        put_g(i, gi, s["gpre"])
        put_g(i, gi + 1, s["gpost"])
        pending[0] = put_grad(f"ffn{i}{j}", dict(w_in=dwin, w_out=dwout))
        return dh_in

    dh = ffn_back("l1f2", 1, 1, dh, sv_f12, wts["ffn11"], L1)
    dyo, dgate, dgp = _rowwise("l1m_postb", functools.partial(_post_bwd_fn, 1.0), t_len, [dh, yo1],
                               [("full", gvec(1, 3)), ("seg", modrow(1, 5, 1))], [(D_MODEL, BF16)],
                               [D_MODEL, D_MODEL], tm=tm1)
    put_mod(1, 5, dgate)
    put_g(1, 3, dgp)
    dw_gm_out = _mm_tn(tgm, dyo, name="gm_dwout", tn=1024)
    dtg = _mm(dyo, wts["gm_wout_t"], out_dtype=F32, name="gm_dt")
    dgu, dgvn, dws, dbst = _gm_spatial_bwd(dtg, gu, gvn, ws_bf, wst_bf, bst, name="gm_spatialb")
    g["gm_w_s"] = dws
    g["gm_b_s"] = dbst.T
    dp1, dvg, dvb = _rowwise("gm_actb", _gm_act_bwd_fn, t_len, [p1, dgu, dgvn], [("full", vg)],
                             [(2 * GM_INNER, BF16)], [GM_INNER, GM_INNER], tm=128)
    g["gm_v_g"] = dvg[0, 0]
    g["gm_v_b"] = dvb[0, 0]
    pending[0] = put_grad("gm", dict(w_in=_mm_tn(um1, dp1, name="gm_dwin", tm=1024), w_out=dw_gm_out))
    dum1 = _mm(dp1, wts["gm_win_t"], out_dtype=F32, name="gm_dum", tn=1024, tk=512)
    dh, dsh, dsc, dgp = _rowwise("l1m_preb", _pre_bwd_fn, t_len, [dum1, h4, dh],
                                 [("full", gvec(1, 2)), ("seg", modrow(1, 4, 1))], [(D_MODEL, F32)],
                                 [D_MODEL, D_MODEL, D_MODEL], tm=tm1)
    put_mod(1, 3, dsh)
    put_mod(1, 4, dsc)
    put_g(1, 2, dgp)
    dh = ffn_back("l1f1", 1, 0, dh, sv_f11, wts["ffn10"], L1)

    dh = ffn_back("l0f2", 0, 1, dh, sv_f02, wts["ffn01"], L1)
    dyo, dgate, dgp = _rowwise("l0m_postb", functools.partial(_post_bwd_fn, 1.0), t_len, [dh, yo0],
                               [("full", gvec(0, 3)), ("seg", modrow(0, 5, 1))], [(D_MODEL, BF16)],
                               [D_MODEL, D_MODEL], tm=tm1)
    put_mod(0, 5, dgate)
    put_g(0, 3, dgp)
    dw_ssd_out = _mm_tn(yn, dyo, name="ssd_dwout", tn=1024)
    dyn = _mm(dyo, wts["ssd_wout_t"], out_dtype=F32, name="ssd_dyn")
    dyn_all = jnp.concatenate([jnp.zeros((n_ctx, SSD_INNER), F32), dyn], axis=0)
    dy_ssd, dz, dngv, ddv = _rowwise("ssd_gateb", _ssdgate_bwd_fn, n0, [dyn_all] + gate_rows,
                                     [("full", dvec), ("full", ngv)], [(SSD_INNER, F32), (SSD_INNER, BF16)],
                                     [SSD_INNER, SSD_INNER], tm=128)
    g["ssd_norm_g"] = dngv[0, 0]
    g["ssd_D"] = jnp.sum(ddv[0, 0].reshape(SSD_HEADS, SSD_HEAD_DIM), axis=1)
    dxbcs, ddts, dalogs, dbiases = [], [], [], []
    for d in range(2):
        dxd, ddtd, dal, dbi = _ssd_scan_bwd(dy_ssd, xbc, hss[d], dt_dir[d], dtT_dir[d], bias_r[d], bias_c[d],
                                            alog_r[d], alog_c[d], dvec, rev=(d == 1), n_ctx_chunks=ncc,
                                            direct=(d == 0), name=f"ssd_scanb{d}")
        dxbcs.append(dxd)
        ddts.append(ddtd)
        dalogs.append(dal[0])
        dbiases.append(dbi[0])
    g["ssd_A_log"] = jnp.stack(dalogs)
    g["ssd_dt_bias"] = jnp.stack(dbiases)
    dxbc_pre, dcw8, dcb = _conv_bwd(dxbcs[0], dxbcs[1], cpre, xbc_pre, small["conv_w8"], n_ctx=n_ctx, name="ssd_convb")
    g["ssd_conv_w"] = dcw8[:SSD_CONV]
    g["ssd_conv_b"] = dcb[0]
    ddt_bf = jnp.concatenate([ddts[0], ddts[1], jnp.zeros((n0, 128 - 2 * nh), F32)], axis=1).astype(BF16)
    dw_ssd_in = jnp.concatenate([
        _mm_tn(um0, dz, name="ssd_dwz", tm=1024),
        _mm_tn(um0, dxbc_pre, name="ssd_dwxbc", tm=1024),
        _mm_tn(um0, ddt_bf, name="ssd_dwdt", tm=1024)[:, :2 * nh]], axis=1)
    pending[0] = put_grad("ssd", dict(w_in=dw_ssd_in, w_out=dw_ssd_out))
    dum0 = _mm(dz, wts["ssd_wz_t"], out_dtype=F32, name="ssd_dum_z", tm=544, tn=1024, tk=512)
    dum0 = _mm(dxbc_pre, wts["ssd_wxbc_t"], out_dtype=F32, name="ssd_dum_x", tm=544, tn=1024, tk=512, add=dum0)
    dum0 = _mm(ddt_bf, wts["ssd_wdt_t"], out_dtype=F32, name="ssd_dum_dt", tm=544, tn=1024, add=dum0)
    dres = jnp.concatenate([jnp.zeros((n_ctx, D_MODEL), F32), dh], axis=0)
    dh0, dsh, dsc, dgp = _rowwise("l0m_preb", _pre_bwd_fn, n0, [dum0, h1, dres],
                                  [("full", gvec(0, 2)), ("seg", modrow(0, 4, 2))], [(D_MODEL, F32)],
                                  [D_MODEL, D_MODEL, D_MODEL], tm=tm0, **kw0)
    put_mod(0, 3, dsh)
    put_mod(0, 4, dsc)
    put_g(0, 2, dgp)
    dh0 = ffn_back("l0f1", 0, 0, dh0, sv_f01, wts["ffn00"], L0)
    grad_x = dh0[n_ctx:]
    g["norm_g"] = jnp.stack([jnp.stack(r) for r in dng])
    g["dmx"] = jnp.stack([jnp.concatenate(r) for r in dmx])
    g["dmc"] = jnp.stack([jnp.concatenate(r) for r in dmc])
    return loss_parts[0], grad_x, g


GROUPS = ("ffn00", "ssd", "ffn01", "ffn10", "gm", "ffn11")


def _group_mats(group, lands):
    win_l, wout_l = lands
    k, nloc = win_l.shape[1], win_l.shape[2]
    win = jnp.transpose(win_l, (1, 0, 2)).reshape(k, NDEV * nloc)
    win_t = jnp.transpose(win_l, (0, 2, 1)).reshape(NDEV * nloc, k)
    wout = wout_l.reshape(-1, wout_l.shape[2])
    if group.startswith("ffn"):
        return {group: dict(wa=win[:, :FFN_DIM], wb=win[:, FFN_DIM:], win_t=win_t, wout=wout, wout_t=wout.T)}
    if group == "gm":
        return dict(gm_win=win, gm_win_t=win_t, gm_wout=wout, gm_wout_t=wout.T)
    assert group == "ssd"
    c0, c1 = SSD_INNER, SSD_INNER + SSD_CONV_DIM
    padc = 128 - 2 * SSD_HEADS
    return dict(ssd_wz=win[:, :c0], ssd_wxbc=win[:, c0:c1], ssd_wdt=jnp.pad(win[:, c1:], ((0, 0), (0, padc))),
                ssd_wz_t=win_t[:c0], ssd_wxbc_t=win_t[c0:c1], ssd_wdt_t=jnp.pad(win_t[c1:], ((0, padc), (0, 0))),
                ssd_wout=wout, ssd_wout_t=wout.T)


def _group_blocks(grads):
    gin, gout = grads["w_in"], grads["w_out"]
    k, n = gin.shape
    bin_ = jnp.transpose(gin.reshape(k, NDEV, n // NDEV), (1, 0, 2)).astype(BF16)
    bout = gout.reshape(NDEV, gout.shape[0] // NDEV, gout.shape[1]).astype(BF16)
    return [bin_, bout]


def kernel(x, c, ctx, c_ctx, ada_w, ada_b, norm_g, ffn_w_in, ffn_w_out, ssd_w_in, ssd_conv_w, ssd_conv_b, ssd_dt_bias, ssd_A_log, ssd_D, ssd_norm_g, ssd_w_out, gm_w_in, gm_v_g, gm_v_b, gm_w_s, gm_b_s, gm_w_out, loss_target, m_c_ctx, m_ada_w, m_ada_b, m_norm_g, m_ffn_w_in, m_ffn_w_out, m_ssd_w_in, m_ssd_conv_w, m_ssd_conv_b, m_ssd_dt_bias, m_ssd_A_log, m_ssd_D, m_ssd_norm_g, m_ssd_w_out, m_gm_w_in, m_gm_v_g, m_gm_v_b, m_gm_w_s, m_gm_b_s, m_gm_w_out, v_c_ctx, v_ada_w, v_ada_b, v_norm_g, v_ffn_w_in, v_ffn_w_out, v_ssd_w_in, v_ssd_conv_w, v_ssd_conv_b, v_ssd_dt_bias, v_ssd_A_log, v_ssd_D, v_ssd_norm_g, v_ssd_w_out, v_gm_w_in, v_gm_v_g, v_gm_v_b, v_gm_w_s, v_gm_b_s, v_gm_w_out):
    me = 4 * lax.axis_index("x") + 2 * lax.axis_index("y") + lax.axis_index("c")
    d = D_MODEL
    ncol = N_MOD * d // NDEV

    shard = {"ssd": (ssd_w_in[0], ssd_w_out[0]), "gm": (gm_w_in[0], gm_w_out[0])}
    moment = {"ssd": ((m_ssd_w_in[0], v_ssd_w_in[0]), (m_ssd_w_out[0], v_ssd_w_out[0])),
              "gm": ((m_gm_w_in[0], v_gm_w_in[0]), (m_gm_w_out[0], v_gm_w_out[0]))}
    for i in range(2):
        for j in range(2):
            shard[f"ffn{i}{j}"] = (ffn_w_in[i, j], ffn_w_out[i, j])
            moment[f"ffn{i}{j}"] = ((m_ffn_w_in[i, j], v_ffn_w_in[i, j]), (m_ffn_w_out[i, j], v_ffn_w_out[i, j]))
    gathers = {}
    started = jnp.zeros((), F32)
    for grp in GROUPS:
        srcs = [a.astype(BF16) for a in shard[grp]]
        st = _exchange_start(srcs, [_landing(s, me) for s in srcs], scatter=False, name="gather_start_" + grp)
        gathers[grp] = st[:4]
        started = started + st[4]

    def get_w(grp, after):
        lands = _exchange_wait(*gathers[grp], after, scatter=False, name="gather_wait_" + grp)
        return _group_mats(grp, lands)

    scatters = {}

    def put_grad(grp, grads):
        blocks = _group_blocks(grads)
        lands = [_landing(lax.dynamic_index_in_dim(b, me, axis=0, keepdims=False), me) for b in blocks]
        st = _exchange_start(blocks, lands, scatter=True, name="scatter_start_" + grp)
        scatters[grp] = st[:4]
        return st[4]

    small_pack = jnp.concatenate([c.reshape(-1) + started, norm_g.reshape(-1), ssd_conv_w.reshape(-1),
                                  gm_v_g.reshape(-1), gm_v_b.reshape(-1)])[None, :]
    (sp,) = _exchange([small_pack], scatter=False, name="gather_small")
    sp = sp[:, 0]
    o = 0
    c_all = sp[:, o:o + d]; o += d
    ng_all = sp[:, o:o + 2 * 6 * 128].reshape(NDEV, 2, 6, 128); o += 2 * 6 * 128
    cw_all = sp[:, o:o + SSD_CONV * 512].reshape(NDEV, SSD_CONV, 512); o += SSD_CONV * 512
    vg_all = sp[:, o:o + 256]; o += 256
    vb_all = sp[:, o:o + 256]; o += 256
    norm_g_full = jnp.transpose(ng_all, (1, 2, 0, 3)).reshape(2, 6, d)
    conv_w_full = jnp.transpose(cw_all, (1, 0, 2)).reshape(SSD_CONV, SSD_CONV_DIM)
    gm_v_g_full = vg_all.reshape(-1)
    gm_v_b_full = vb_all.reshape(-1)

    c16 = jnp.concatenate([c_all, jnp.broadcast_to(c_ctx[None, :], (NDEV, d))], axis=0)
    ada_b_loc = lax.dynamic_slice_in_dim(ada_b, me * ncol, ncol, axis=1)
    mods_loc = jnp.stack([_mm_f32(c16, ada_w[i], name=f"ada_mod{i}", silu_a=True, bias=ada_b_loc[i][None, :])
                          for i in range(2)])
    (mods_all,) = _exchange([mods_loc], scatter=False, name="gather_mods")
    mods_rows = jnp.transpose(mods_all, (1, 2, 0, 3)).reshape(2, 2 * NDEV, N_MOD * d)
    mx = lax.dynamic_index_in_dim(mods_rows, me, axis=1, keepdims=False).reshape(2, N_MOD, d)
    mc = mods_rows[:, NDEV].reshape(2, N_MOD, d)
    mods = [(mc[i], mx[i]) for i in range(2)]

    small = dict(conv_w8=jnp.pad(conv_w_full, ((0, 8 - SSD_CONV), (0, 0))), conv_b=ssd_conv_b, dt_bias=ssd_dt_bias[0],
                 a_log=ssd_A_log[0], ssd_d=ssd_D[0], ssd_norm_g=ssd_norm_g[0], gm_v_g=gm_v_g_full,
                 gm_v_b=gm_v_b_full, gm_w_s=gm_w_s[0], gm_b_s=gm_b_s[0])
    loss_parts, grad_x, g = _local_step(x[0], ctx[0], loss_target[0], mods, norm_g_full, get_w, small, put_grad)
    loss = lax.psum(0.5 / d * jnp.sum(loss_parts), ("x", "y", "c"))

    upd = {}
    for grp in reversed(GROUPS):
        parts = _exchange_wait(*scatters[grp], grad_x, scatter=True, name="scatter_wait_" + grp)
        for which, p, w_, (m_, v_) in zip(("in", "out"), parts, shard[grp], moment[grp]):
            upd[grp, which] = _adamw(p, w_, m_, v_, name=f"adamw_{grp}_{which}")
    res = {}
    for which in ("in", "out"):
        res["ffn_w_" + which] = [jnp.stack([jnp.stack([upd[f"ffn{i}{j}", which][k] for j in range(2)])
                                            for i in range(2)]) for k in range(4)]
        res["ssd_w_" + which] = [upd["ssd", which][k][None] for k in range(4)]
        res["gm_w_" + which] = [upd["gm", which][k][None] for k in range(4)]

    sg_names = ["dmx", "dmc", "norm_g", "ssd_conv_w", "ssd_conv_b", "ssd_dt_bias", "ssd_A_log", "ssd_D", "ssd_norm_g",
                "gm_v_g", "gm_v_b", "gm_w_s", "gm_b_s"]
    sg_shapes = [g[n].shape for n in sg_names]
    flat = jnp.concatenate([g[n].reshape(-1) for n in sg_names])
    npack = flat.shape[0]
    pad = (-npack) % 1024
    flat = jnp.pad(flat, (0, pad)).reshape(-1, 128)
    (sg_all,) = _exchange([flat], scatter=False, name="gather_small_grads")
    sg_sum = _sum_slots(sg_all, name="sum_small_grads").reshape(-1)[:npack]
    sums = {}
    o = 0
    for n, shp in zip(sg_names, sg_shapes):
        sz = math.prod(shp)
        sums[n] = sg_sum[o:o + sz].reshape(shp)
        o += sz
    per_dev = sg_all.reshape(NDEV, -1)
    dmx_all = per_dev[:, :2 * N_MOD * d].reshape(NDEV, 2, N_MOD * d)
    dmc_all = per_dev[:, 2 * N_MOD * d:4 * N_MOD * d].reshape(NDEV, 2, N_MOD * d)

    (s16,) = _rowwise("ada_silu", lambda cc: ((_silu(cc),), ()), 2 * NDEV, [c16], [], [(d, F32)], tm=2 * NDEV)
    s16_t = s16.T
    g_ada_w, dcc_parts = [], []
    for i in range(2):
        rhs = jnp.concatenate([lax.dynamic_slice_in_dim(dmx_all[:, i], me * ncol, ncol, axis=1),
                               lax.dynamic_slice_in_dim(dmc_all[:, i], me * ncol, ncol, axis=1)], axis=0)
        g_ada_w.append(_mm_f32(s16_t, rhs, name=f"ada_dw{i}"))
        dmc_loc = lax.dynamic_slice_in_dim(sums["dmc"][i], me * ncol, ncol, axis=0)
        rhs_c = jnp.zeros((ncol, 128), F32).at[:, 0].set(dmc_loc)
        dcc_parts.append(_mm_f32(ada_w[i], rhs_c, name=f"ada_dcc{i}")[:, 0])
    g_ada_w = jnp.stack(g_ada_w)
    dcc_part = (dcc_parts[0] + dcc_parts[1]).reshape(8, 128)
    (dcc_all,) = _exchange([dcc_part], scatter=False, name="gather_dcc")
    g_c_ctx = _sum_slots(dcc_all, name="sum_dcc", scale_by=c_ctx.reshape(8, 128)).reshape(d)
    g_ada_b = sums["dmx"] + sums["dmc"]

    outs = _adamw(g_ada_w.reshape(1, -1, ncol), ada_w.reshape(-1, ncol), m_ada_w.reshape(-1, ncol),
                  v_ada_w.reshape(-1, ncol), name="adamw_ada_w")
    res["ada_w"] = [o_.reshape(ada_w.shape) for o_ in outs]

    loc = lambda a, ax, n: lax.dynamic_slice_in_dim(a, me * n, n, axis=ax)
    small_g = dict(c_ctx=g_c_ctx, ada_b=g_ada_b, norm_g=loc(sums["norm_g"], 2, 128),
                   ssd_conv_w=loc(sums["ssd_conv_w"], 1, 512)[None], ssd_conv_b=sums["ssd_conv_b"][None],
                   ssd_dt_bias=sums["ssd_dt_bias"][None], ssd_A_log=sums["ssd_A_log"][None], ssd_D=sums["ssd_D"][None],
                   ssd_norm_g=sums["ssd_norm_g"][None], gm_v_g=loc(sums["gm_v_g"], 0, 256)[None],
                   gm_v_b=loc(sums["gm_v_b"], 0, 256)[None], gm_w_s=sums["gm_w_s"][None], gm_b_s=sums["gm_b_s"][None])
    small_w = dict(c_ctx=(c_ctx, m_c_ctx, v_c_ctx), ada_b=(ada_b, m_ada_b, v_ada_b), norm_g=(norm_g, m_norm_g, v_norm_g),
                   ssd_conv_w=(ssd_conv_w, m_ssd_conv_w, v_ssd_conv_w), ssd_conv_b=(ssd_conv_b, m_ssd_conv_b, v_ssd_conv_b),
                   ssd_dt_bias=(ssd_dt_bias, m_ssd_dt_bias, v_ssd_dt_bias), ssd_A_log=(ssd_A_log, m_ssd_A_log, v_ssd_A_log),
                   ssd_D=(ssd_D, m_ssd_D, v_ssd_D), ssd_norm_g=(ssd_norm_g, m_ssd_norm_g, v_ssd_norm_g),
                   gm_v_g=(gm_v_g, m_gm_v_g, v_gm_v_g), gm_v_b=(gm_v_b, m_gm_v_b, v_gm_v_b),
                   gm_w_s=(gm_w_s, m_gm_w_s, v_gm_w_s), gm_b_s=(gm_b_s, m_gm_b_s, v_gm_b_s))
    sn = list(small_w)

    def pack(arrs):
        f = jnp.concatenate([a.reshape(-1) for a in arrs])
        return jnp.pad(f, (0, (-f.shape[0]) % 1024)).reshape(-1, 128)

    pg = pack([small_g[n].reshape(small_w[n][0].shape) for n in sn])
    outs = _adamw(pg[None], pack([small_w[n][0] for n in sn]), pack([small_w[n][1] for n in sn]),
                  pack([small_w[n][2] for n in sn]), name="adamw_small")
    flat_outs = [o_.reshape(-1) for o_ in outs]
    o = 0
    for n in sn:
        shp = small_w[n][0].shape
        sz = math.prod(shp)
        res[n] = [fo[o:o + sz].reshape(shp) for fo in flat_outs]
        o += sz

    order = ["c_ctx", "ada_w", "ada_b", "norm_g", "ffn_w_in", "ffn_w_out", "ssd_w_in", "ssd_conv_w", "ssd_conv_b",
             "ssd_dt_bias", "ssd_A_log", "ssd_D", "ssd_norm_g", "ssd_w_out", "gm_w_in", "gm_v_g", "gm_v_b", "gm_w_s",
             "gm_b_s", "gm_w_out"]
    result = [loss, grad_x[None]]
    for k in range(4):
        result += [res[n][k] for n in order]
    return tuple(result)
```

```python
import functools
import math

import jax
import jax.numpy as jnp
from jax import lax
from jax.experimental import pallas as pl
from jax.experimental.pallas import tpu as pltpu

F32 = jnp.float32
BF16 = jnp.bfloat16

NDEV = 8
D_MODEL = 1024
FFN_DIM = 2816
N_MOD = 9
EPS = 1e-6
SSD_INNER = 2048
SSD_HEADS = 32
SSD_HEAD_DIM = 64
SSD_GROUPS = 8
SSD_HPG = 4
SSD_STATE = 128
SSD_CONV = 5
SSD_CONV_DIM = 4096
CHUNK = 128
GM_INNER = 2048
GM_GROUPS = 8
GM_GROUP_DIM = 256
ADAM_LR = 0.001
ADAM_B1 = 0.9
ADAM_B2 = 0.999
ADAM_EPS = 1e-08
ADAM_WD = 0.01
ADAM_STEP = 10
NEG_BIG = -1e30
VMEM_LIMIT_BYTES = 56 * 1024 * 1024
HI = lax.Precision.HIGHEST


def _params(*sem):
    return pltpu.CompilerParams(dimension_semantics=sem, vmem_limit_bytes=VMEM_LIMIT_BYTES)


def _pick(n, target, mult=16):
    if n <= target:
        return n
    for t in range(target - target % mult, 0, -mult):
        if n % t == 0:
            return t
    raise ValueError((n, target, mult))


def _sig(x):
    return 1.0 / (1.0 + jnp.exp(-x))


def _silu(x):
    return x * _sig(x)


def _dsilu(x):
    s = _sig(x)
    return s * (1.0 + x * (1.0 - s))


_GELU_C = math.sqrt(2.0 / math.pi)


def _gelu(x):
    return 0.5 * x * (1.0 + jnp.tanh(_GELU_C * (x + 0.044715 * x * x * x)))


def _dgelu(x):
    t = jnp.tanh(_GELU_C * (x + 0.044715 * x * x * x))
    return 0.5 * (1.0 + t) + 0.5 * x * (1.0 - t * t) * _GELU_C * (1.0 + 3.0 * 0.044715 * x * x)


def _softplus(x):
    return jnp.maximum(x, 0.0) + jnp.log1p(jnp.exp(-jnp.abs(x)))


def _sum0(v):
    return jnp.sum(v, axis=0, keepdims=True)


def _rms(h):
    r = lax.rsqrt(jnp.mean(h * h, axis=-1, keepdims=True) + EPS)
    return h * r, r


def _dot(a, b, dims=((1,), (0,)), precision=None):
    return lax.dot_general(a, b, (dims, ((), ())), preferred_element_type=F32, precision=precision)


_NT = ((1,), (1,))
_TN = ((0,), (0,))


def _rowwise(name, fn, n_rows, rows, consts, outs, accs=(), *, tm, nseg=1, seg_blocks=0):
    assert n_rows % tm == 0
    if nseg == 2:
        assert seg_blocks > 0
        seg = lambda i: jnp.where(i < seg_blocks, 0, 1)
    else:
        seg = lambda i: 0
    in_specs, args = [], []
    for r in rows:
        arr, width, cb, off = r if isinstance(r, tuple) else (r, r.shape[1], 0, 0)
        in_specs.append(pl.BlockSpec((tm, width), lambda i, cb=cb, off=off: (i + off, cb)))
        args.append(arr)
    for kind, arr in consts:
        if kind == "seg":
            assert arr.shape[0] == nseg and arr.shape[1] == 1, arr.shape
            in_specs.append(pl.BlockSpec((None, 1, arr.shape[2]), lambda i: (seg(i), 0, 0)))
        else:
            in_specs.append(pl.BlockSpec(arr.shape, lambda i: (0, 0)))
        args.append(arr)
    out_shape = [jax.ShapeDtypeStruct((n_rows, w), dt) for w, dt in outs]
    out_specs = [pl.BlockSpec((tm, w), lambda i: (i, 0)) for w, _ in outs]
    out_shape += [jax.ShapeDtypeStruct((nseg, 1, w), F32) for w in accs]
    out_specs += [pl.BlockSpec((None, 1, w), lambda i: (seg(i), 0, 0)) for w in accs]
    n_in, n_out, n_acc = len(args), len(outs), len(accs)

    def kern(*refs):
        ins = [r[...] for r in refs[:n_in]]
        res, sums = fn(*ins)
        for ref, v in zip(refs[n_in:n_in + n_out], res):
            ref[...] = v.astype(ref.dtype)
        if n_acc:
            i = pl.program_id(0)
            first = (i == 0) | (i == seg_blocks) if nseg == 2 else (i == 0)
            acc_refs = refs[n_in + n_out:]

            @pl.when(first)
            def _():
                for ref, v in zip(acc_refs, sums):
                    ref[...] = v

            @pl.when(jnp.logical_not(first))
            def _():
                for ref, v in zip(acc_refs, sums):
                    ref[...] += v

    res = pl.pallas_call(
        kern, name=name, grid=(n_rows // tm,), in_specs=in_specs, out_specs=out_specs, out_shape=out_shape,
        compiler_params=_params("arbitrary"),
    )(*args)
    return res


def _pre_fwd_fn(h, g, shift, scale):
    hh, _ = _rms(h)
    return (hh * g * (1.0 + scale) + shift,), ()


def _pre_bwd_fn(du, h, dres, g, scale):
    hh, r = _rms(h)
    n = hh * g
    dn = du * (1.0 + scale)
    dhh = dn * g
    dh = dres + r * (dhh - hh * jnp.mean(dhh * hh, axis=-1, keepdims=True))
    return (dh,), (_sum0(du), _sum0(du * n), _sum0(dn * hh))


def _post_fwd_fn(weight, h, y, g, gate):
    yh, _ = _rms(y)
    return (h + weight * gate * (yh * g),), ()


def _post_bwd_fn(weight, dh, y, g, gate):
    yh, r = _rms(y)
    dr = dh * weight
    dyh = dr * gate * g
    dy = r * (dyh - yh * jnp.mean(dyh * yh, axis=-1, keepdims=True))
    return (dy,), (_sum0(dr * yh * g), _sum0(dr * gate * yh))


def _glu_bwd_fn(ds, a, b):
    a = a.astype(F32)
    b = b.astype(F32)
    sg = _sig(a)
    da = ds * b * (sg * (1.0 + a * (1.0 - sg)))
    db = ds * (a * sg)
    return (jnp.concatenate([da, db], axis=1),), ()


def _loss_fn(y, t):
    diff = y - t
    return (diff * (1.0 / D_MODEL),), (_sum0(diff * diff),)


def _ssd_y(yf, yb, xs, z, dvec):
    y = yf + yb + dvec * xs
    return y, y * _silu(z)


def _ssdgate_fwd_fn(yf, yb, xs, z, dvec, ng):
    _, yg = _ssd_y(yf, yb, xs, z, dvec)
    parts = []
    for g in range(SSD_GROUPS):
        sl = slice(g * 256, (g + 1) * 256)
        parts.append(_rms(yg[:, sl])[0])
    return (jnp.concatenate(parts, axis=1) * ng,), ()


def _ssdgate_bwd_fn(dyn, yf, yb, xs, z, dvec, ng):
    y, yg = _ssd_y(yf, yb, xs, z, dvec)
    dyg_parts, ygh_parts = [], []
    for g in range(SSD_GROUPS):
        sl = slice(g * 256, (g + 1) * 256)
        ygh, r = _rms(yg[:, sl])
        d = dyn[:, sl] * ng[:, sl]
        dyg_parts.append(r * (d - ygh * jnp.mean(d * ygh, axis=-1, keepdims=True)))
        ygh_parts.append(ygh)
    dyg = jnp.concatenate(dyg_parts, axis=1)
    ygh = jnp.concatenate(ygh_parts, axis=1)
    dy = dyg * _silu(z)
    dz = dyg * y * _dsilu(z)
    return (dy, dz), (_sum0(dyn * ygh), _sum0(dy * xs))


def _ln_stats(v):
    mu = jnp.mean(v, axis=-1, keepdims=True)
    vc = v - mu
    r = lax.rsqrt(jnp.mean(vc * vc, axis=-1, keepdims=True) + EPS)
    return vc * r, r


def _gm_act_fwd_fn(p, vg, vb):
    gu = _gelu(p[:, :GM_INNER])
    gvh, _ = _ln_stats(_gelu(p[:, GM_INNER:]))
    return (gu, gvh * vg + vb), ()


def _gm_act_bwd_fn(p, dgu, dgvn, vg):
    pu = p[:, :GM_INNER]
    pv = p[:, GM_INNER:]
    gvh, r = _ln_stats(_gelu(pv))
    dgvh = dgvn * vg
    dgv = r * (dgvh - jnp.mean(dgvh, axis=-1, keepdims=True) - gvh * jnp.mean(dgvh * gvh, axis=-1, keepdims=True))
    dp = jnp.concatenate([dgu * _dgelu(pu), dgv * _dgelu(pv)], axis=1)
    return (dp,), (_sum0(dgvn * gvh), _sum0(dgvn))


def _mm(a, b, *, out_dtype, name, tm=640, tn=512, tk=1024, add=None):
    m, k = a.shape
    k2, n = b.shape
    assert k == k2
    tm, tn, tk = _pick(m, tm), _pick(n, tn, 128), _pick(k, tk, 128)
    nk = k // tk

    def kern(*refs):
        if add is None:
            a_ref, b_ref, o_ref, acc_ref = refs
        else:
            a_ref, b_ref, add_ref, o_ref, acc_ref = refs
        kk = pl.program_id(2)

        @pl.when(kk == 0)
        def _():
            acc_ref[...] = jnp.zeros_like(acc_ref)

        acc_ref[...] += jnp.dot(a_ref[...], b_ref[...], preferred_element_type=F32)

        @pl.when(kk == nk - 1)
        def _():
            r = acc_ref[...]
            if add is not None:
                r = r + add_ref[...]
            o_ref[...] = r.astype(o_ref.dtype)

    in_specs = [pl.BlockSpec((tm, tk), lambda i, j, kk: (i, kk)), pl.BlockSpec((tk, tn), lambda i, j, kk: (kk, j))]
    args = [a, b]
    if add is not None:
        in_specs.append(pl.BlockSpec((tm, tn), lambda i, j, kk: (i, j)))
        args.append(add)
    return pl.pallas_call(
        kern, name=name, grid=(m // tm, n // tn, nk), in_specs=in_specs,
        out_specs=pl.BlockSpec((tm, tn), lambda i, j, kk: (i, j)),
        out_shape=jax.ShapeDtypeStruct((m, n), out_dtype),
        scratch_shapes=[pltpu.VMEM((tm, tn), F32)],
        compiler_params=_params("parallel", "parallel", "arbitrary"),
    )(*args)


def _mm_glu(u, wa, wb, *, name, tm=640, tn=256):
    m, k = u.shape
    n = wa.shape[1]
    tm, tn = _pick(m, tm), _pick(n, tn, 128)

    def kern(u_ref, wa_ref, wb_ref, s_ref, a_ref, b_ref):
        uu = u_ref[...]
        a = jnp.dot(uu, wa_ref[...], preferred_element_type=F32)
        b = jnp.dot(uu, wb_ref[...], preferred_element_type=F32)
        s_ref[...] = (_silu(a) * b).astype(BF16)
        a_ref[...] = a.astype(BF16)
        b_ref[...] = b.astype(BF16)

    ospec = pl.BlockSpec((tm, tn), lambda i, j: (i, j))
    return pl.pallas_call(
        kern, name=name, grid=(m // tm, n // tn),
        in_specs=[pl.BlockSpec((tm, k), lambda i, j: (i, 0)), pl.BlockSpec((k, tn), lambda i, j: (0, j)),
                  pl.BlockSpec((k, tn), lambda i, j: (0, j))],
        out_specs=[ospec, ospec, ospec],
        out_shape=[jax.ShapeDtypeStruct((m, n), BF16)] * 3,
        compiler_params=_params("parallel", "parallel"),
    )(u, wa, wb)


def _mm_tn(a, b, *, name, tm=512, tn=512, tk=512):
    t, m = a.shape
    t2, n = b.shape
    assert t == t2
    tm, tn, tk = _pick(m, tm, 128), _pick(n, tn, 128), _pick(t, tk)
    nk = t // tk

    def kern(a_ref, b_ref, o_ref):
        kk = pl.program_id(2)

        @pl.when(kk == 0)
        def _():
            o_ref[...] = jnp.zeros_like(o_ref)

        o_ref[...] += _dot(a_ref[...], b_ref[...], _TN)

    return pl.pallas_call(
        kern, name=name, grid=(m // tm, n // tn, nk),
        in_specs=[pl.BlockSpec((tk, tm), lambda i, j, kk: (kk, i)), pl.BlockSpec((tk, tn), lambda i, j, kk: (kk, j))],
        out_specs=pl.BlockSpec((tm, tn), lambda i, j, kk: (i, j)),
        out_shape=jax.ShapeDtypeStruct((m, n), F32),
        compiler_params=_params("parallel", "parallel", "arbitrary"),
    )(a, b)


def _mm_f32(a, b, *, name, silu_a=False, bias=None):
    m, k = a.shape
    n = b.shape[1]

    def kern(*refs):
        if bias is None:
            a_ref, b_ref, o_ref = refs
        else:
            a_ref, b_ref, bias_ref, o_ref = refs
        av = a_ref[...]
        if silu_a:
            av = _silu(av)
        r = jnp.dot(av, b_ref[...], preferred_element_type=F32, precision=HI)
        if bias is not None:
            r = r + bias_ref[...]
        o_ref[...] = r

    args = [a, b] + ([] if bias is None else [bias])
    return pl.pallas_call(kern, name=name, out_shape=jax.ShapeDtypeStruct((m, n), F32),
                          compiler_params=pltpu.CompilerParams(vmem_limit_bytes=VMEM_LIMIT_BYTES))(*args)


def _shifted(v, s, t, lo, hi):
    n = v.shape[0]
    r = v if s == 0 else pltpu.roll(v, (-s) % n, 0)
    ok = (t + s >= lo) & (t + s < hi)
    return jnp.where(ok, r, 0.0)


def _seg_bounds(n, n_ctx):
    t = lax.broadcasted_iota(jnp.int32, (n, 1), 0)
    lo = jnp.where(t < n_ctx, 0, n_ctx)
    hi = jnp.where(t < n_ctx, n_ctx, n)
    return t, lo, hi


def _conv_fwd(xp, w8, b, *, n_ctx, name, cb=256):
    n, c = xp.shape

    def kern(x_ref, w_ref, b_ref, cpre_ref, act_ref):
        x = x_ref[...]
        t, lo, hi = _seg_bounds(n, n_ctx)
        acc = jnp.zeros_like(x) + b_ref[...]
        for k in range(SSD_CONV):
            acc = acc + _shifted(x, k - SSD_CONV // 2, t, lo, hi) * w_ref[k:k + 1, :]
        cpre_ref[...] = acc
        act_ref[...] = _silu(acc)

    spec = pl.BlockSpec((n, cb), lambda j: (0, j))
    return pl.pallas_call(
        kern, name=name, grid=(c // cb,),
        in_specs=[spec, pl.BlockSpec((8, cb), lambda j: (0, j)), pl.BlockSpec((1, cb), lambda j: (0, j))],
        out_specs=[spec, spec], out_shape=[jax.ShapeDtypeStruct((n, c), F32)] * 2,
        compiler_params=_params("parallel"),
    )(xp, w8, b)


def _conv_bwd(d1, d2, cpre, xp, w8, *, n_ctx, name, cb=128):
    n, c = xp.shape

    def kern(d1_ref, d2_ref, cpre_ref, x_ref, w_ref, dx_ref, dw_ref, db_ref):
        g = (d1_ref[...] + d2_ref[...]) * _dsilu(cpre_ref[...])
        x = x_ref[...]
        t, lo, hi = _seg_bounds(n, n_ctx)
        dx = jnp.zeros_like(g)
        dw_ref[...] = jnp.zeros_like(dw_ref)
        for k in range(SSD_CONV):
            s = k - SSD_CONV // 2
            dx = dx + _shifted(g, -s, t, lo, hi) * w_ref[k:k + 1, :]
            dw_ref[k:k + 1, :] = _sum0(g * _shifted(x, s, t, lo, hi))
        dx_ref[...] = dx.astype(BF16)
        db_ref[...] = _sum0(g)

    spec = pl.BlockSpec((n, cb), lambda j: (0, j))
    return pl.pallas_call(
        kern, name=name, grid=(c // cb,),
        in_specs=[spec, spec, spec, spec, pl.BlockSpec((8, cb), lambda j: (0, j))],
        out_specs=[spec, pl.BlockSpec((8, cb), lambda j: (0, j)), pl.BlockSpec((1, cb), lambda j: (0, j))],
        out_shape=[jax.ShapeDtypeStruct((n, c), BF16), jax.ShapeDtypeStruct((8, c), F32),
                   jax.ShapeDtypeStruct((1, c), F32)],
        compiler_params=_params("parallel"),
    )(d1, d2, cpre, xp, w8)


def _chunk_of(s, nc, n_ctx_chunks, rev):
    if not rev:
        return s
    return jnp.where(s < n_ctx_chunks, n_ctx_chunks - 1 - s, nc - 1 - (s - n_ctx_chunks))


def _scan_common(dt_raw, dtT_raw, bias_r, bias_c, alog_r, alog_c, rev):
    ii = lax.broadcasted_iota(jnp.int32, (CHUNK, CHUNK), 0)
    jj = lax.broadcasted_iota(jnp.int32, (CHUNK, CHUNK), 1)
    tri = (jj >= ii) if rev else (jj <= ii)
    tri_t = (ii >= jj) if rev else (ii <= jj)
    a_r = -jnp.exp(alog_r)
    a_c = -jnp.exp(alog_c)
    dt = _softplus(dt_raw + bias_r)
    dt_t = _softplus(dtT_raw + bias_c)
    al = dt * a_r
    acum = _dot(tri.astype(F32), al, precision=HI)
    acum_t = _dot(dt_t * a_c, tri_t.astype(F32), precision=HI)
    atot = _sum0(al)
    return tri, tri_t, a_r, dt, acum, acum_t, atot


def _ssd_scan_fwd(xbc, dt_raw, dtT_raw, bias_r, bias_c, alog_r, alog_c, *, rev, n_ctx_chunks, name):
    n = xbc.shape[0]
    nc = n // CHUNK
    cidx = functools.partial(_chunk_of, nc=nc, n_ctx_chunks=n_ctx_chunks, rev=rev)

    def kern(xs_ref, b_ref, c_ref, dt_ref, dtT_ref, br_ref, bc_ref, ar_ref, ac_ref, y_ref, hs_ref, h_scr):
        @pl.when(pl.program_id(0) == 0)
        def _():
            h_scr[...] = jnp.zeros_like(h_scr)

        tri, _, _, dt, acum, acum_t, atot = _scan_common(
            dt_ref[...], dtT_ref[...], br_ref[...], bc_ref[...], ar_ref[...], ac_ref[...], rev)
        ea = jnp.exp(acum)
        dec_end = jnp.exp(atot - acum)
        etot = jnp.exp(atot)
        hs_ref[...] = h_scr[...]
        for g in range(SSD_GROUPS):
            bg = b_ref[:, g * SSD_STATE:(g + 1) * SSD_STATE].astype(BF16)
            cg = c_ref[:, g * SSD_STATE:(g + 1) * SSD_STATE].astype(BF16)
            cb = _dot(cg, bg, _NT)
            ys = []
            for k in range(SSD_HPG):
                h = g * SSD_HPG + k
                lmat = jnp.exp(jnp.where(tri, acum[:, h:h + 1] - acum_t[h:h + 1, :], NEG_BIG))
                m = (cb * lmat).astype(BF16)
                xdt = xs_ref[:, h * SSD_HEAD_DIM:(h + 1) * SSD_HEAD_DIM] * dt[:, h:h + 1]
                hh = h_scr[h]
                y = _dot(m, xdt.astype(BF16)) + _dot(cg, hh.astype(BF16), _NT) * ea[:, h:h + 1]
                ys.append(y)
                xdw = (xdt * dec_end[:, h:h + 1]).astype(BF16)
                h_scr[h] = hh * etot[:, h:h + 1] + _dot(xdw, bg, _TN)
            y_ref[:, g * 256:(g + 1) * 256] = jnp.concatenate(ys, axis=1)

    nh = SSD_HEADS
    small = lambda shape: pl.BlockSpec(shape, lambda s: (0, 0))
    return pl.pallas_call(
        kern, name=name, grid=(nc,),
        in_specs=[pl.BlockSpec((CHUNK, SSD_INNER), lambda s: (cidx(s), 0)),
                  pl.BlockSpec((CHUNK, 1024), lambda s: (cidx(s), 2)),
                  pl.BlockSpec((CHUNK, 1024), lambda s: (cidx(s), 3)),
                  pl.BlockSpec((CHUNK, nh), lambda s: (cidx(s), 0)),
                  pl.BlockSpec((nh, CHUNK), lambda s: (0, cidx(s))),
                  small((1, nh)), small((nh, 1)), small((1, nh)), small((nh, 1))],
        out_specs=[pl.BlockSpec((CHUNK, SSD_INNER), lambda s: (cidx(s), 0)),
                   pl.BlockSpec((None, nh, SSD_HEAD_DIM, SSD_STATE), lambda s: (s, 0, 0, 0))],
        out_shape=[jax.ShapeDtypeStruct((n, SSD_INNER), F32),
                   jax.ShapeDtypeStruct((nc, nh, SSD_HEAD_DIM, SSD_STATE), F32)],
        scratch_shapes=[pltpu.VMEM((nh, SSD_HEAD_DIM, SSD_STATE), F32)],
        compiler_params=_params("arbitrary"),
    )(xbc, xbc, xbc, dt_raw, dtT_raw, bias_r, bias_c, alog_r, alog_c)


def _ssd_scan_bwd(dy, xbc, hs, dt_raw, dtT_raw, bias_r, bias_c, alog_r, alog_c, dvec, *, rev, n_ctx_chunks,
                  direct, name):
    n = xbc.shape[0]
    nc = n // CHUNK
    nh = SSD_HEADS
    step_of = lambda r: nc - 1 - r
    cidx = lambda r: _chunk_of(step_of(r), nc, n_ctx_chunks, rev)

    def kern(dy_ref, xs_ref, b_ref, c_ref, hs_ref, dt_ref, dtT_ref, br_ref, bc_ref, ar_ref, ac_ref, dv_ref,
             dx_ref, ddt_ref, dal_ref, dbias_ref, dh_scr):
        @pl.when(pl.program_id(0) == 0)
        def _():
            dh_scr[...] = jnp.zeros_like(dh_scr)
            dal_ref[...] = jnp.zeros_like(dal_ref)
            dbias_ref[...] = jnp.zeros_like(dbias_ref)

        tri, tri_t, a_r, dt, acum, acum_t, atot = _scan_common(
            dt_ref[...], dtT_ref[...], br_ref[...], bc_ref[...], ar_ref[...], ac_ref[...], rev)
        ea = jnp.exp(acum)
        dec_end = jnp.exp(atot - acum)
        etot = jnp.exp(atot)
        ii = lax.broadcasted_iota(jnp.int32, (CHUNK, CHUNK), 0)
        jj = lax.broadcasted_iota(jnp.int32, (CHUNK, CHUNK), 1)
        strict = tri & (ii != jj)
        tri_t_bf = tri_t.astype(BF16)
        lane = lax.broadcasted_iota(jnp.int32, (CHUNK, nh), 1)
        lane1 = lax.broadcasted_iota(jnp.int32, (1, nh), 1)
        ycol = jnp.zeros((CHUNK, nh), F32)
        dal = jnp.zeros((CHUNK, nh), F32)
        ddt = jnp.zeros((CHUNK, nh), F32)
        dtot = jnp.zeros((1, nh), F32)
        for g in range(SSD_GROUPS):
            bg = b_ref[:, g * SSD_STATE:(g + 1) * SSD_STATE].astype(BF16)
            cg = c_ref[:, g * SSD_STATE:(g + 1) * SSD_STATE].astype(BF16)
            cb = _dot(cg, bg, _NT)
            dcb = jnp.zeros((CHUNK, CHUNK), F32)
            dbg = jnp.zeros((CHUNK, SSD_STATE), F32)
            dcg = jnp.zeros((CHUNK, SSD_STATE), F32)
            dxs = []
            for k in range(SSD_HPG):
                h = g * SSD_HPG + k
                hsl = slice(h * SSD_HEAD_DIM, (h + 1) * SSD_HEAD_DIM)
                sel = lane == h
                lmat = jnp.exp(jnp.where(tri, acum[:, h:h + 1] - acum_t[h:h + 1, :], NEG_BIG))
                mf = cb * lmat
                xh = xs_ref[:, hsl]
                dtc = dt[:, h:h + 1]
                xdt = xh * dtc
                dyh = dy_ref[:, hsl]
                dyh_bf = dyh.astype(BF16)
                hst = hs_ref[h]
                hst_bf = hst.astype(BF16)
                dh = dh_scr[h]
                dh_bf = dh.astype(BF16)
                eac = ea[:, h:h + 1]
                dec = dec_end[:, h:h + 1]
                et = etot[:, h:h + 1]
                yoff = _dot(cg, hst_bf, _NT) * eac
                dyo_bf = (dyh * eac).astype(BF16)
                dcg = dcg + _dot(dyo_bf, hst_bf)
                dh_scr[h] = dh * et + _dot(dyo_bf, cg, _TN)
                col = jnp.sum(dyh * yoff, axis=1, keepdims=True)
                bdh = _dot(bg, dh_bf, _NT)
                dxdt = _dot(mf.astype(BF16), dyh_bf, _TN) + bdh * dec
                e = jnp.sum(xdt * bdh, axis=1, keepdims=True) * dec
                col = col - e
                dtot_h = _sum0(e) + _sum0(jnp.sum(dh * hst, axis=1, keepdims=True)) * et
                dbg = dbg + _dot((xdt * dec).astype(BF16), dh_bf)
                dm = _dot(dyh_bf, xdt.astype(BF16), _NT)
                dcb = dcb + dm * lmat
                xmat = _dot(tri_t_bf, (dm * mf).astype(BF16))
                dal_h = jnp.sum(jnp.where(strict, xmat, 0.0), axis=1, keepdims=True)
                ycol = ycol + jnp.where(sel, col, 0.0)
                dal = dal + jnp.where(sel, dal_h, 0.0)
                ddt = ddt + jnp.where(sel, jnp.sum(dxdt * xh, axis=1, keepdims=True), 0.0)
                dtot = dtot + jnp.where(lane1 == h, dtot_h, 0.0)
                dxh = dxdt * dtc
                if direct:
                    dxh = dxh + dyh * dv_ref[:, hsl]
                dxs.append(dxh)
            dcb_bf = dcb.astype(BF16)
            dcg = dcg + _dot(dcb_bf, bg)
            dbg = dbg + _dot(dcb_bf, cg, _TN)
            dx_ref[:, g * 256:(g + 1) * 256] = jnp.concatenate(dxs, axis=1)
            dx_ref[:, SSD_INNER + g * SSD_STATE:SSD_INNER + (g + 1) * SSD_STATE] = dbg
            dx_ref[:, SSD_INNER + 1024 + g * SSD_STATE:SSD_INNER + 1024 + (g + 1) * SSD_STATE] = dcg
        dal = dal + _dot(tri_t.astype(F32), ycol, precision=HI) + dtot
        ddt = ddt + dal * a_r
        ddt_raw = ddt * _sig(dt_ref[...] + br_ref[...])
        ddt_ref[...] = ddt_raw
        dal_ref[...] += _sum0(dal * dt) * a_r
        dbias_ref[...] += _sum0(ddt_raw)

    small = lambda shape: pl.BlockSpec(shape, lambda r: (0, 0))
    return pl.pallas_call(
        kern, name=name, grid=(nc,),
        in_specs=[pl.BlockSpec((CHUNK, SSD_INNER), lambda r: (cidx(r), 0)),
                  pl.BlockSpec((CHUNK, SSD_INNER), lambda r: (cidx(r), 0)),
                  pl.BlockSpec((CHUNK, 1024), lambda r: (cidx(r), 2)),
                  pl.BlockSpec((CHUNK, 1024), lambda r: (cidx(r), 3)),
                  pl.BlockSpec((None, nh, SSD_HEAD_DIM, SSD_STATE), lambda r: (step_of(r), 0, 0, 0)),
                  pl.BlockSpec((CHUNK, nh), lambda r: (cidx(r), 0)),
                  pl.BlockSpec((nh, CHUNK), lambda r: (0, cidx(r))),
                  small((1, nh)), small((nh, 1)), small((1, nh)), small((nh, 1)), small((1, SSD_INNER))],
        out_specs=[pl.BlockSpec((CHUNK, SSD_CONV_DIM), lambda r: (cidx(r), 0)),
                   pl.BlockSpec((CHUNK, nh), lambda r: (cidx(r), 0)),
                   small((1, nh)), small((1, nh))],
        out_shape=[jax.ShapeDtypeStruct((n, SSD_CONV_DIM), F32), jax.ShapeDtypeStruct((n, nh), F32),
                   jax.ShapeDtypeStruct((1, nh), F32), jax.ShapeDtypeStruct((1, nh), F32)],
        scratch_shapes=[pltpu.VMEM((nh, SSD_HEAD_DIM, SSD_STATE), F32)],
        compiler_params=_params("arbitrary"),
    )(dy, xbc, xbc, xbc, hs, dt_raw, dtT_raw, bias_r, bias_c, alog_r, alog_c, dvec)


def _gm_spatial_fwd(gu, gvn, ws, bst, *, name):
    n = gu.shape[0]

    def kern(gu_ref, gv_ref, ws_ref, bs_ref, o_ref):
        for g in range(GM_GROUPS):
            sl = slice(g * GM_GROUP_DIM, (g + 1) * GM_GROUP_DIM)
            s = _dot(ws_ref[g], gv_ref[:, sl]) + bs_ref[:, g:g + 1]
            o_ref[:, sl] = (gu_ref[:, sl] * s).astype(BF16)

    spec = pl.BlockSpec((CHUNK, GM_INNER), lambda i: (i, 0))
    return pl.pallas_call(
        kern, name=name, grid=(n // CHUNK,),
        in_specs=[spec, spec, pl.BlockSpec(ws.shape, lambda i: (0, 0, 0)), pl.BlockSpec(bst.shape, lambda i: (0, 0))],
        out_specs=spec, out_shape=jax.ShapeDtypeStruct((n, GM_INNER), BF16),
        compiler_params=_params("parallel"),
    )(gu, gvn, ws, bst)


def _gm_spatial_bwd(dt, gu, gvn, ws, wst, bst, *, name):
    n = gu.shape[0]

    def kern(dt_ref, gu_ref, gv_ref, ws_ref, wst_ref, bs_ref, dgu_ref, dgv_ref, dws_ref, dbs_ref):
        @pl.when(pl.program_id(0) == 0)
        def _():
            dws_ref[...] = jnp.zeros_like(dws_ref)
            dbs_ref[...] = jnp.zeros_like(dbs_ref)

        lane = lax.broadcasted_iota(jnp.int32, (CHUNK, GM_GROUPS), 1)
        dbs = jnp.zeros((CHUNK, GM_GROUPS), F32)
        for g in range(GM_GROUPS):
            sl = slice(g * GM_GROUP_DIM, (g + 1) * GM_GROUP_DIM)
            gv = gv_ref[:, sl]
            s = _dot(ws_ref[g], gv) + bs_ref[:, g:g + 1]
            d = dt_ref[:, sl]
            dgu_ref[:, sl] = d * s
            ds = d * gu_ref[:, sl]
            ds_bf = ds.astype(BF16)
            dws_ref[g] += _dot(ds_bf, gv, _NT)
            dgv_ref[:, sl] = _dot(wst_ref[g], ds_bf)
            dbs = dbs + jnp.where(lane == g, jnp.sum(ds, axis=1, keepdims=True), 0.0)
        dbs_ref[...] += dbs

    spec = pl.BlockSpec((CHUNK, GM_INNER), lambda i: (i, 0))
    wspec = pl.BlockSpec(ws.shape, lambda i: (0, 0, 0))
    bspec = pl.BlockSpec(bst.shape, lambda i: (0, 0))
    return pl.pallas_call(
        kern, name=name, grid=(n // CHUNK,),
        in_specs=[spec, spec, spec, wspec, wspec, bspec],
        out_specs=[spec, spec, wspec, bspec],
        out_shape=[jax.ShapeDtypeStruct((n, GM_INNER), F32), jax.ShapeDtypeStruct((n, GM_INNER), F32),
                   jax.ShapeDtypeStruct(ws.shape, F32), jax.ShapeDtypeStruct(bst.shape, F32)],
        compiler_params=_params("arbitrary"),
    )(dt, gu, gvn, ws, wst, bst)


def _adamw(parts, w, m, v, *, name, tm=256):
    ns, r, wd = parts.shape
    tm = _pick(r, tm, 8)

    def kern(p_ref, w_ref, m_ref, v_ref, g_ref, d_ref, nm_ref, nv_ref):
        g = p_ref[0].astype(F32)
        for s in range(1, ns):
            g = g + p_ref[s].astype(F32)
        m2 = ADAM_B1 * m_ref[...] + (1.0 - ADAM_B1) * g
        v2 = ADAM_B2 * v_ref[...] + (1.0 - ADAM_B2) * (g * g)
        m_hat = m2 / (1.0 - ADAM_B1 ** ADAM_STEP)
        v_hat = v2 / (1.0 - ADAM_B2 ** ADAM_STEP)
        g_ref[...] = g
        d_ref[...] = -ADAM_LR * (m_hat / (jnp.sqrt(v_hat) + ADAM_EPS) + ADAM_WD * w_ref[...])
        nm_ref[...] = m2
        nv_ref[...] = v2

    spec = pl.BlockSpec((tm, wd), lambda i: (i, 0))
    return pl.pallas_call(
        kern, name=name, grid=(r // tm,),
        in_specs=[pl.BlockSpec((ns, tm, wd), lambda i: (0, i, 0)), spec, spec, spec],
        out_specs=[spec] * 4, out_shape=[jax.ShapeDtypeStruct((r, wd), F32)] * 4,
        compiler_params=_params("parallel"),
    )(parts, w, m, v)


def _sum_slots(parts, *, name, scale_by=None):
    ns, r, wd = parts.shape

    def kern(*refs):
        p_ref, o_ref = refs[0], refs[-1]
        g = p_ref[0]
        for s in range(1, ns):
            g = g + p_ref[s]
        if scale_by is not None:
            g = g * _dsilu(refs[1][...])
        o_ref[...] = g

    args = [parts] + ([] if scale_by is None else [scale_by])
    return pl.pallas_call(kern, name=name, out_shape=jax.ShapeDtypeStruct((r, wd), F32),
                          compiler_params=pltpu.CompilerParams(vmem_limit_bytes=VMEM_LIMIT_BYTES))(*args)


def _mesh_pos():
    x, y, c = lax.axis_index("x"), lax.axis_index("y"), lax.axis_index("c")
    return x, y, c, 4 * x + 2 * y + c


def _flip(x, y, c, f):
    fx, fy, fc = (f >> 2) & 1, (f >> 1) & 1, f & 1
    px = 1 - x if fx else x
    py = 1 - y if fy else y
    pc = 1 - c if fc else c
    return (px, py, pc), 4 * px + 2 * py + pc


_HBM_SPEC = pl.BlockSpec(memory_space=pltpu.HBM)


def _exchange(arrays, *, scatter, name):
    na = len(arrays)
    if scatter:
        out_shape = [jax.ShapeDtypeStruct(a.shape, a.dtype) for a in arrays]
    else:
        out_shape = [jax.ShapeDtypeStruct((NDEV,) + a.shape, a.dtype) for a in arrays]

    out_shape.append(jax.ShapeDtypeStruct((8, 128), F32))

    def body(*refs):
        ins, outs = refs[:na], refs[na:2 * na]
        send_sems, recv_sems, local_sems = refs[2 * na + 1:]
        refs[2 * na][...] = jnp.zeros((8, 128), F32)
        x, y, c, me = _mesh_pos()
        copies = []
        for i in range(na):
            src_own = ins[i].at[me] if scatter else ins[i]
            lc = pltpu.make_async_copy(src_own, outs[i].at[me], local_sems.at[i])
            lc.start()
            copies.append(lc)
        sends = []
        for f in range(1, NDEV):
            peer, pidx = _flip(x, y, c, f)
            for i in range(na):
                k = i * (NDEV - 1) + f - 1
                src = ins[i].at[pidx] if scatter else ins[i]
                cp = pltpu.make_async_remote_copy(
                    src_ref=src, dst_ref=outs[i].at[me], send_sem=send_sems.at[k], recv_sem=recv_sems.at[k],
                    device_id=peer, device_id_type=pl.DeviceIdType.MESH)
                cp.start()
                sends.append(cp)
        for f in range(1, NDEV):
            peer, pidx = _flip(x, y, c, f)
            for i in range(na):
                k = i * (NDEV - 1) + f - 1
                src = ins[i].at[pidx] if scatter else ins[i]
                pltpu.make_async_remote_copy(
                    src_ref=src, dst_ref=outs[i].at[pidx], send_sem=send_sems.at[k], recv_sem=recv_sems.at[k],
                    device_id=peer, device_id_type=pl.DeviceIdType.MESH).wait_recv()
        for cp in sends:
            cp.wait_send()
        for lc in copies:
            lc.wait()

    res = pl.pallas_call(
        body, name=name, out_shape=out_shape, in_specs=[_HBM_SPEC] * na,
        out_specs=[_HBM_SPEC] * na + [pl.BlockSpec(memory_space=pltpu.VMEM)],
        scratch_shapes=[pltpu.SemaphoreType.DMA((na * (NDEV - 1),)), pltpu.SemaphoreType.DMA((na * (NDEV - 1),)),
                        pltpu.SemaphoreType.DMA((na,))],
        compiler_params=pltpu.CompilerParams(has_side_effects=True),
    )(*arrays)
    return res[:na], res[na][0, 0]


_SEM_SPEC = pl.BlockSpec(memory_space=pltpu.SEMAPHORE)
_DATAFLOW = pltpu.SideEffectType.DATAFLOW_SIDE_EFFECTING


def _split_copies(srcs, lands, send_sems, recv_sems, scatter, arriving):
    x, y, c, me = _mesh_pos()
    copies = []
    for i in range(len(srcs)):
        for f in range(1, NDEV):
            peer, pidx = _flip(x, y, c, f)
            k = i * (NDEV - 1) + f - 1
            copies.append(pltpu.make_async_remote_copy(
                src_ref=srcs[i].at[pidx] if scatter else srcs[i], dst_ref=lands[i].at[pidx if arriving else me],
                send_sem=send_sems.at[k], recv_sem=recv_sems.at[k], device_id=peer,
                device_id_type=pl.DeviceIdType.MESH))
    return copies


def _exchange_start(srcs, lands, *, scatter, name):
    na = len(srcs)
    nsem = na * (NDEV - 1)

    def body(*refs):
        ins_src, ins_land = refs[:na], refs[na:2 * na]
        send_sems, recv_sems = refs[2 * na], refs[2 * na + 1]
        token = refs[-1]
        for cp in _split_copies(ins_src, ins_land, send_sems, recv_sems, scatter, False):
            cp.start()
        token[...] = jnp.zeros_like(token)

    thru = [pltpu.HBM(a.shape, a.dtype) for a in list(srcs) + list(lands)]
    res = pl.pallas_call(
        body, name=name,
        out_shape=(pltpu.SemaphoreType.DMA((nsem,)), pltpu.SemaphoreType.DMA((nsem,)), *thru,
                   jax.ShapeDtypeStruct((8, 128), F32)),
        in_specs=[_HBM_SPEC] * (2 * na),
        out_specs=(_SEM_SPEC, _SEM_SPEC, *([_HBM_SPEC] * (2 * na)), pl.BlockSpec(memory_space=pltpu.VMEM)),
        input_output_aliases={i: 2 + i for i in range(2 * na)},
        compiler_params=pltpu.CompilerParams(has_side_effects=_DATAFLOW),
    )(*[pltpu.with_memory_space_constraint(a, pltpu.HBM) for a in list(srcs) + list(lands)])
    send_sems, recv_sems = res[0], res[1]
    return send_sems, recv_sems, res[2:2 + na], res[2 + na:2 + 2 * na], res[-1][0, 0]


def _exchange_wait(send_sems, recv_sems, srcs, lands, after, *, scatter, name):
    na = len(srcs)

    def body(*refs):
        ins_src, ins_land = refs[:na], refs[na:2 * na]
        s_sems, r_sems = refs[2 * na], refs[2 * na + 1]
        for cp in _split_copies(ins_src, ins_land, s_sems, r_sems, scatter, False):
            cp.wait_send()
        for cp in _split_copies(ins_src, ins_land, s_sems, r_sems, scatter, True):
            cp.wait_recv()

    thru = [pltpu.HBM(a.shape, a.dtype) for a in list(srcs) + list(lands)]
    res = pl.pallas_call(
        body, name=name, out_shape=tuple(thru),
        in_specs=[_HBM_SPEC] * (2 * na) + [_SEM_SPEC, _SEM_SPEC, pl.BlockSpec(memory_space=pl.ANY)],
        out_specs=tuple([_HBM_SPEC] * (2 * na)),
        input_output_aliases={i: i for i in range(2 * na)},
        compiler_params=pltpu.CompilerParams(has_side_effects=_DATAFLOW),
    )(*srcs, *lands, send_sems, recv_sems, after)
    return res[na:]


def _landing(block, me):
    buf = lax.empty((NDEV,) + block.shape, block.dtype)
    return lax.dynamic_update_slice_in_dim(buf, block[None], me, axis=0)


def _seg_kw(nseg, n_ctx, tm):
    return dict(nseg=nseg, seg_blocks=(n_ctx // tm if nseg == 2 else 0))


def _ffn_fwd(tag, h, gpre, gpost, shift, scale, gate, w, *, nseg, n_ctx, tm):
    n = h.shape[0]
    kw = _seg_kw(nseg, n_ctx, tm)
    (u,) = _rowwise(tag + "_pre", _pre_fwd_fn, n, [h], [("full", gpre), ("seg", shift), ("seg", scale)],
                    [(D_MODEL, BF16)], tm=tm, **kw)
    s, a, b = _mm_glu(u, w["wa"], w["wb"], name=tag + "_glu")
    if "late" in w:
        w.update(w.pop("late")(s))
    y = _mm(s, w["wout"], out_dtype=F32, name=tag + "_out", tn=512, tk=FFN_DIM)
    (ho,) = _rowwise(tag + "_post", functools.partial(_post_fwd_fn, 0.5), n, [h, y], [("full", gpost), ("seg", gate)],
                     [(D_MODEL, F32)], tm=tm, **kw)
    return ho, dict(h=h, u=u, s=s, a=a, b=b, y=y)


def _ffn_bwd(tag, dho, sv, gpre, gpost, scale, gate, w, put, *, nseg, n_ctx, tm):
    n = dho.shape[0]
    kw = _seg_kw(nseg, n_ctx, tm)
    dy, dgate, dgpost = _rowwise(tag + "_postb", functools.partial(_post_bwd_fn, 0.5), n, [dho, sv["y"]],
                                 [("full", gpost), ("seg", gate)], [(D_MODEL, BF16)], [D_MODEL, D_MODEL], tm=tm, **kw)
    tok = put("w_out", _mm_tn(sv["s"], dy, name=tag + "_dwout", tm=1408, tn=1024))
    ds = _mm(dy, w["wout_t"], out_dtype=F32, name=tag + "_ds", tn=704)
    (dp,) = _rowwise(tag + "_glub", _glu_bwd_fn, n, [ds, sv["a"], sv["b"]], [], [(2 * FFN_DIM, BF16)], tm=min(tm, 128))
    dwin = _mm_tn(sv["u"], dp, name=tag + "_dwin", tm=1024, tn=512)
    du = _mm(dp, w["win_t"], out_dtype=F32, name=tag + "_du", tn=1024, tk=512)
    if tok is not None:
        gpre = gpre + tok
    dh, dshift, dscale, dgpre = _rowwise(tag + "_preb", _pre_bwd_fn, n, [du, sv["h"], dho],
                                         [("full", gpre), ("seg", scale)], [(D_MODEL, F32)],
                                         [D_MODEL, D_MODEL, D_MODEL], tm=tm, **kw)
    return dh, put("w_in", dwin), dict(shift=dshift, scale=dscale, gate=dgate, gpre=dgpre, gpost=dgpost)


def _local_step(x, ctx, target, mods, norm_g, get_w, small, put_grad):
    t_len, n_ctx = x.shape[0], ctx.shape[0]
    n0 = t_len + n_ctx
    tm0 = _pick(n_ctx, 256, 8)
    tm1 = _pick(t_len, 256, 8)
    ncc = n_ctx // CHUNK
    g = {}

    def modrow(i, k, nseg):
        mc, mx = mods[i]
        if nseg == 2:
            return jnp.stack([mc[k], mx[k]])[:, None, :]
        return mx[k][None, None, :]

    pending = [None]

    def gvec(i, k):
        v = norm_g[i, k][None, :]
        if pending[0] is not None:
            v = v + pending[0]
            pending[0] = None
        return v

    xc = jnp.concatenate([ctx, x], axis=0)
    L0 = dict(nseg=2, n_ctx=n_ctx, tm=tm0)
    wts = dict(get_w("ffn00", xc))
    h1, sv_f01 = _ffn_fwd("l0f1", xc, gvec(0, 0), gvec(0, 1), modrow(0, 0, 2), modrow(0, 1, 2), modrow(0, 2, 2),
                          wts["ffn00"], **L0)
    kw0 = _seg_kw(2, n_ctx, tm0)
    (um0,) = _rowwise("l0m_pre", _pre_fwd_fn, n0, [h1], [("full", gvec(0, 2)), ("seg", modrow(0, 3, 2)),
                                                         ("seg", modrow(0, 4, 2))], [(D_MODEL, BF16)], tm=tm0, **kw0)
    wts.update(get_w("ssd", um0))
    z = _mm(um0, wts["ssd_wz"], out_dtype=F32, name="ssd_z", tm=544)
    xbc_pre = _mm(um0, wts["ssd_wxbc"], out_dtype=F32, name="ssd_xbc", tm=544)
    dtr = _mm(um0, wts["ssd_wdt"], out_dtype=F32, name="ssd_dt", tm=544)
    cpre, xbc = _conv_fwd(xbc_pre, small["conv_w8"], small["conv_b"], n_ctx=n_ctx, name="ssd_conv")
    nh = SSD_HEADS
    dt_dir = [dtr[:, :nh], dtr[:, nh:2 * nh]]
    dtT_dir = [d.T for d in dt_dir]
    bias_r = [small["dt_bias"][d][None, :] for d in range(2)]
    bias_c = [small["dt_bias"][d][:, None] for d in range(2)]
    alog_r = [small["a_log"][d][None, :] for d in range(2)]
    alog_c = [small["a_log"][d][:, None] for d in range(2)]
    ys, hss = [], []
    for d in range(2):
        yd, hsd = _ssd_scan_fwd(xbc, dt_dir[d], dtT_dir[d], bias_r[d], bias_c[d], alog_r[d], alog_c[d],
                                rev=(d == 1), n_ctx_chunks=ncc, name=f"ssd_scan{d}")
        ys.append(yd)
        hss.append(hsd)
    dvec = jnp.repeat(small["ssd_d"], SSD_HEAD_DIM)[None, :]
    ngv = small["ssd_norm_g"][None, :]
    gate_rows = [ys[0], ys[1], (xbc, SSD_INNER, 0, 0), z]
    (yn_all,) = _rowwise("ssd_gate", _ssdgate_fwd_fn, n0, gate_rows, [("full", dvec), ("full", ngv)],
                         [(SSD_INNER, BF16)], tm=128)
    yn = yn_all[n_ctx:]
    yo0 = _mm(yn, wts["ssd_wout"], out_dtype=F32, name="ssd_out", tn=1024, tk=1024)
    h1x = h1[n_ctx:]
    L1 = dict(nseg=1, n_ctx=0, tm=tm1)
    (h2,) = _rowwise("l0m_post", functools.partial(_post_fwd_fn, 1.0), t_len, [h1x, yo0],
                     [("full", gvec(0, 3)), ("seg", modrow(0, 5, 1))], [(D_MODEL, F32)], tm=tm1)
    wts.update(get_w("ffn01", h2))
    h3, sv_f02 = _ffn_fwd("l0f2", h2, gvec(0, 4), gvec(0, 5), modrow(0, 6, 1), modrow(0, 7, 1), modrow(0, 8, 1),
                          wts["ffn01"], **L1)

    wts.update(get_w("ffn10", h3))
    h4, sv_f11 = _ffn_fwd("l1f1", h3, gvec(1, 0), gvec(1, 1), modrow(1, 0, 1), modrow(1, 1, 1), modrow(1, 2, 1),
                          wts["ffn10"], **L1)
    (um1,) = _rowwise("l1m_pre", _pre_fwd_fn, t_len, [h4], [("full", gvec(1, 2)), ("seg", modrow(1, 3, 1)),
                                                            ("seg", modrow(1, 4, 1))], [(D_MODEL, BF16)], tm=tm1)
    wts.update(get_w("gm", um1))
    p1 = _mm(um1, wts["gm_win"], out_dtype=F32, name="gm_in")
    vg = small["gm_v_g"][None, :]
    vb = small["gm_v_b"][None, :]
    gu, gvn = _rowwise("gm_act", _gm_act_fwd_fn, t_len, [p1], [("full", vg), ("full", vb)],
                       [(GM_INNER, F32), (GM_INNER, BF16)], tm=128)
    ws_bf = small["gm_w_s"].astype(BF16)
    wst_bf = jnp.swapaxes(small["gm_w_s"], 1, 2).astype(BF16)
    bst = small["gm_b_s"].T
    tgm = _gm_spatial_fwd(gu, gvn, ws_bf, bst, name="gm_spatial")
    yo1 = _mm(tgm, wts["gm_wout"], out_dtype=F32, name="gm_out", tn=1024, tk=1024)
    (h5,) = _rowwise("l1m_post", functools.partial(_post_fwd_fn, 1.0), t_len, [h4, yo1],
                     [("full", gvec(1, 3)), ("seg", modrow(1, 5, 1))], [(D_MODEL, F32)], tm=tm1)
    wts.update(get_w("ffn11", h5))
    h6, sv_f12 = _ffn_fwd("l1f2", h5, gvec(1, 4), gvec(1, 5), modrow(1, 6, 1), modrow(1, 7, 1), modrow(1, 8, 1),
                          wts["ffn11"], **L1)

    dh, loss_parts = _rowwise("loss", _loss_fn, t_len, [h6, target], [], [(D_MODEL, F32)], [D_MODEL], tm=tm1)

    zero = jnp.zeros((D_MODEL,), F32)
    dmx = [[zero] * N_MOD for _ in range(2)]
    dmc = [[zero] * N_MOD for _ in range(2)]
    dng = [[zero] * 6 for _ in range(2)]

    def put_mod(i, k, acc):
        if acc.shape[0] == 2:
            dmc[i][k] = dmc[i][k] + acc[0, 0]
            dmx[i][k] = dmx[i][k] + acc[1, 0]
        else:
            dmx[i][k] = dmx[i][k] + acc[0, 0]

    def put_g(i, k, acc):
        dng[i][k] = dng[i][k] + jnp.sum(acc[:, 0], axis=0)

    def ffn_back(tag, i, j, dho, sv, w, lay):
        nseg = lay["nseg"]
        base = 0 if j == 0 else 6
        gi = 0 if j == 0 else 4
        dh_in, pending[0], s = _ffn_bwd(tag, dho, sv, gvec(i, gi), gvec(i, gi + 1), modrow(i, base + 1, nseg),
                                        modrow(i, base + 2, nseg), w, functools.partial(put_grad, f"ffn{i}{j}"), **lay)
        put_mod(i, base, s["shift"])
        put_mod(i, base + 1, s["scale"])
        put_mod(i, base + 2, s["gate"])
        put_g(i, gi, s["gpre"])
        put_g(i, gi + 1, s["gpost"])
        return dh_in

    dh = ffn_back("l1f2", 1, 1, dh, sv_f12, wts["ffn11"], L1)
    dyo, dgate, dgp = _rowwise("l1m_postb", functools.partial(_post_bwd_fn, 1.0), t_len, [dh, yo1],
                               [("full", gvec(1, 3)), ("seg", modrow(1, 5, 1))], [(D_MODEL, BF16)],
                               [D_MODEL, D_MODEL], tm=tm1)
    put_mod(1, 5, dgate)
    put_g(1, 3, dgp)
    put_grad("gm", "w_out", _mm_tn(tgm, dyo, name="gm_dwout", tn=1024))
    dtg = _mm(dyo, wts["gm_wout_t"], out_dtype=F32, name="gm_dt")
    dgu, dgvn, dws, dbst = _gm_spatial_bwd(dtg, gu, gvn, ws_bf, wst_bf, bst, name="gm_spatialb")
    g["gm_w_s"] = dws
    g["gm_b_s"] = dbst.T
    dp1, dvg, dvb = _rowwise("gm_actb", _gm_act_bwd_fn, t_len, [p1, dgu, dgvn], [("full", vg)],
                             [(2 * GM_INNER, BF16)], [GM_INNER, GM_INNER], tm=128)
    g["gm_v_g"] = dvg[0, 0]
    g["gm_v_b"] = dvb[0, 0]
    pending[0] = put_grad("gm", "w_in", _mm_tn(um1, dp1, name="gm_dwin", tm=1024))
    dum1 = _mm(dp1, wts["gm_win_t"], out_dtype=F32, name="gm_dum", tn=1024, tk=512)
    dh, dsh, dsc, dgp = _rowwise("l1m_preb", _pre_bwd_fn, t_len, [dum1, h4, dh],
                                 [("full", gvec(1, 2)), ("seg", modrow(1, 4, 1))], [(D_MODEL, F32)],
                                 [D_MODEL, D_MODEL, D_MODEL], tm=tm1)
    put_mod(1, 3, dsh)
    put_mod(1, 4, dsc)
    put_g(1, 2, dgp)
    dh = ffn_back("l1f1", 1, 0, dh, sv_f11, wts["ffn10"], L1)

    dh = ffn_back("l0f2", 0, 1, dh, sv_f02, wts["ffn01"], L1)
    dyo, dgate, dgp = _rowwise("l0m_postb", functools.partial(_post_bwd_fn, 1.0), t_len, [dh, yo0],
                               [("full", gvec(0, 3)), ("seg", modrow(0, 5, 1))], [(D_MODEL, BF16)],
                               [D_MODEL, D_MODEL], tm=tm1)
    put_mod(0, 5, dgate)
    put_g(0, 3, dgp)
    put_grad("ssd", "w_out", _mm_tn(yn, dyo, name="ssd_dwout", tn=1024))
    dyn = _mm(dyo, wts["ssd_wout_t"], out_dtype=F32, name="ssd_dyn")
    dyn_all = jnp.concatenate([jnp.zeros((n_ctx, SSD_INNER), F32), dyn], axis=0)
    dy_ssd, dz, dngv, ddv = _rowwise("ssd_gateb", _ssdgate_bwd_fn, n0, [dyn_all] + gate_rows,
                                     [("full", dvec), ("full", ngv)], [(SSD_INNER, F32), (SSD_INNER, BF16)],
                                     [SSD_INNER, SSD_INNER], tm=128)
    g["ssd_norm_g"] = dngv[0, 0]
    g["ssd_D"] = jnp.sum(ddv[0, 0].reshape(SSD_HEADS, SSD_HEAD_DIM), axis=1)
    dxbcs, ddts, dalogs, dbiases = [], [], [], []
    for d in range(2):
        dxd, ddtd, dal, dbi = _ssd_scan_bwd(dy_ssd, xbc, hss[d], dt_dir[d], dtT_dir[d], bias_r[d], bias_c[d],
                                            alog_r[d], alog_c[d], dvec, rev=(d == 1), n_ctx_chunks=ncc,
                                            direct=(d == 0), name=f"ssd_scanb{d}")
        dxbcs.append(dxd)
        ddts.append(ddtd)
        dalogs.append(dal[0])
        dbiases.append(dbi[0])
    g["ssd_A_log"] = jnp.stack(dalogs)
    g["ssd_dt_bias"] = jnp.stack(dbiases)
    dxbc_pre, dcw8, dcb = _conv_bwd(dxbcs[0], dxbcs[1], cpre, xbc_pre, small["conv_w8"], n_ctx=n_ctx, name="ssd_convb")
    g["ssd_conv_w"] = dcw8[:SSD_CONV]
    g["ssd_conv_b"] = dcb[0]
    ddt_bf = jnp.concatenate([ddts[0], ddts[1], jnp.zeros((n0, 128 - 2 * nh), F32)], axis=1).astype(BF16)
    dw_ssd_in = jnp.concatenate([
        _mm_tn(um0, dz, name="ssd_dwz", tm=1024),
        _mm_tn(um0, dxbc_pre, name="ssd_dwxbc", tm=1024),
        _mm_tn(um0, ddt_bf, name="ssd_dwdt", tm=1024)[:, :2 * nh]], axis=1)
    pending[0] = put_grad("ssd", "w_in", dw_ssd_in)
    dum0 = _mm(dz, wts["ssd_wz_t"], out_dtype=F32, name="ssd_dum_z", tm=544, tn=1024, tk=512)
    dum0 = _mm(dxbc_pre, wts["ssd_wxbc_t"], out_dtype=F32, name="ssd_dum_x", tm=544, tn=1024, tk=512, add=dum0)
    dum0 = _mm(ddt_bf, wts["ssd_wdt_t"], out_dtype=F32, name="ssd_dum_dt", tm=544, tn=1024, add=dum0)
    dres = jnp.concatenate([jnp.zeros((n_ctx, D_MODEL), F32), dh], axis=0)
    dh0, dsh, dsc, dgp = _rowwise("l0m_preb", _pre_bwd_fn, n0, [dum0, h1, dres],
                                  [("full", gvec(0, 2)), ("seg", modrow(0, 4, 2))], [(D_MODEL, F32)],
                                  [D_MODEL, D_MODEL, D_MODEL], tm=tm0, **kw0)
    put_mod(0, 3, dsh)
    put_mod(0, 4, dsc)
    put_g(0, 2, dgp)
    dh0 = ffn_back("l0f1", 0, 0, dh0, sv_f01, wts["ffn00"], L0)
    grad_x = dh0[n_ctx:]
    g["norm_g"] = jnp.stack([jnp.stack(r) for r in dng])
    g["dmx"] = jnp.stack([jnp.concatenate(r) for r in dmx])
    g["dmc"] = jnp.stack([jnp.concatenate(r) for r in dmc])
    return loss_parts[0], grad_x, g


GROUPS = ("ffn00", "ssd", "ffn01", "ffn10", "gm", "ffn11")


def _mats_in(group, win_l):
    k, nloc = win_l.shape[1], win_l.shape[2]
    win = jnp.transpose(win_l, (1, 0, 2)).reshape(k, NDEV * nloc)
    win_t = jnp.transpose(win_l, (0, 2, 1)).reshape(NDEV * nloc, k)
    if group.startswith("ffn"):
        return dict(wa=win[:, :FFN_DIM], wb=win[:, FFN_DIM:], win_t=win_t)
    if group == "gm":
        return dict(gm_win=win, gm_win_t=win_t)
    assert group == "ssd"
    c0, c1 = SSD_INNER, SSD_INNER + SSD_CONV_DIM
    padc = 128 - 2 * SSD_HEADS
    return dict(ssd_wz=win[:, :c0], ssd_wxbc=win[:, c0:c1], ssd_wdt=jnp.pad(win[:, c1:], ((0, 0), (0, padc))),
                ssd_wz_t=win_t[:c0], ssd_wxbc_t=win_t[c0:c1], ssd_wdt_t=jnp.pad(win_t[c1:], ((0, padc), (0, 0))))


def _mats_out(group, wout_l):
    wout = wout_l.reshape(-1, wout_l.shape[2])
    pre = "" if group.startswith("ffn") else group + "_"
    return {pre + "wout": wout, pre + "wout_t": wout.T}


def _group_mats(group, lands):
    m = {**_mats_in(group, lands[0]), **_mats_out(group, lands[1])}
    return {group: m} if group.startswith("ffn") else m


def _grad_blocks(which, grad):
    if which == "w_in":
        k, n = grad.shape
        return jnp.transpose(grad.reshape(k, NDEV, n // NDEV), (1, 0, 2)).astype(BF16)
    return grad.reshape(NDEV, grad.shape[0] // NDEV, grad.shape[1]).astype(BF16)


def kernel(x, c, ctx, c_ctx, ada_w, ada_b, norm_g, ffn_w_in, ffn_w_out, ssd_w_in, ssd_conv_w, ssd_conv_b, ssd_dt_bias, ssd_A_log, ssd_D, ssd_norm_g, ssd_w_out, gm_w_in, gm_v_g, gm_v_b, gm_w_s, gm_b_s, gm_w_out, loss_target, m_c_ctx, m_ada_w, m_ada_b, m_norm_g, m_ffn_w_in, m_ffn_w_out, m_ssd_w_in, m_ssd_conv_w, m_ssd_conv_b, m_ssd_dt_bias, m_ssd_A_log, m_ssd_D, m_ssd_norm_g, m_ssd_w_out, m_gm_w_in, m_gm_v_g, m_gm_v_b, m_gm_w_s, m_gm_b_s, m_gm_w_out, v_c_ctx, v_ada_w, v_ada_b, v_norm_g, v_ffn_w_in, v_ffn_w_out, v_ssd_w_in, v_ssd_conv_w, v_ssd_conv_b, v_ssd_dt_bias, v_ssd_A_log, v_ssd_D, v_ssd_norm_g, v_ssd_w_out, v_gm_w_in, v_gm_v_g, v_gm_v_b, v_gm_w_s, v_gm_b_s, v_gm_w_out):
    me = 4 * lax.axis_index("x") + 2 * lax.axis_index("y") + lax.axis_index("c")
    d = D_MODEL
    ncol = N_MOD * d // NDEV

    small_pack = jnp.concatenate([c.reshape(-1), norm_g.reshape(-1), ssd_conv_w.reshape(-1),
                                  gm_v_g.reshape(-1), gm_v_b.reshape(-1)])[None, :]
    (sp,), _ = _exchange([small_pack], scatter=False, name="gather_small")
    sp = sp[:, 0]
    o = 0
    c_all = sp[:, o:o + d]; o += d
    ng_all = sp[:, o:o + 2 * 6 * 128].reshape(NDEV, 2, 6, 128); o += 2 * 6 * 128
    cw_all = sp[:, o:o + SSD_CONV * 512].reshape(NDEV, SSD_CONV, 512); o += SSD_CONV * 512
    vg_all = sp[:, o:o + 256]; o += 256
    vb_all = sp[:, o:o + 256]; o += 256
    norm_g_full = jnp.transpose(ng_all, (1, 2, 0, 3)).reshape(2, 6, d)
    conv_w_full = jnp.transpose(cw_all, (1, 0, 2)).reshape(SSD_CONV, SSD_CONV_DIM)
    gm_v_g_full = vg_all.reshape(-1)
    gm_v_b_full = vb_all.reshape(-1)

    c16 = jnp.concatenate([c_all, jnp.broadcast_to(c_ctx[None, :], (NDEV, d))], axis=0)
    ada_b_loc = lax.dynamic_slice_in_dim(ada_b, me * ncol, ncol, axis=1)
    mods_loc = jnp.stack([_mm_f32(c16, ada_w[i], name=f"ada_mod{i}", silu_a=True, bias=ada_b_loc[i][None, :])
                          for i in range(2)])
    (mods_all,), mods_done = _exchange([mods_loc], scatter=False, name="gather_mods")

    shard = {"ssd": (ssd_w_in[0], ssd_w_out[0]), "gm": (gm_w_in[0], gm_w_out[0])}
    moment = {"ssd": ((m_ssd_w_in[0], v_ssd_w_in[0]), (m_ssd_w_out[0], v_ssd_w_out[0])),
              "gm": ((m_gm_w_in[0], v_gm_w_in[0]), (m_gm_w_out[0], v_gm_w_out[0]))}
    for i in range(2):
        for j in range(2):
            shard[f"ffn{i}{j}"] = (ffn_w_in[i, j], ffn_w_out[i, j])
            moment[f"ffn{i}{j}"] = ((m_ffn_w_in[i, j], v_ffn_w_in[i, j]), (m_ffn_w_out[i, j], v_ffn_w_out[i, j]))
    first = GROUPS[0]
    units = [(first + "_in", first, (0,)), (first + "_out", first, (1,))] + [(grp, grp, (0, 1)) for grp in GROUPS[1:]]
    gathers = {}
    for unit, grp, idx in units:
        srcs = [(shard[grp][k] + mods_done).astype(BF16) for k in idx]
        st = _exchange_start(srcs, [_landing(s, me) for s in srcs], scatter=False, name="gather_start_" + unit)
        gathers[unit] = st[:4]

    def fetch(unit, after):
        return _exchange_wait(*gathers[unit], after, scatter=False, name="gather_wait_" + unit)

    def get_w(grp, after):
        if grp != first:
            return _group_mats(grp, fetch(grp, after))
        late = lambda later: _mats_out(grp, fetch(grp + "_out", later)[0])
        return {grp: dict(_mats_in(grp, fetch(grp + "_in", after)[0]), late=late)}

    scatters = {}
    held = {}

    def put_grad(grp, which, grad):
        if grp == first:
            unit, blocks = grp + "_" + which[2:], [_grad_blocks(which, grad)]
        else:
            held[grp, which] = _grad_blocks(which, grad)
            if (grp, "w_in") not in held or (grp, "w_out") not in held:
                return None
            unit, blocks = grp, [held[grp, "w_in"], held[grp, "w_out"]]
        lands = [_landing(lax.dynamic_index_in_dim(b, me, axis=0, keepdims=False), me) for b in blocks]
        st = _exchange_start(blocks, lands, scatter=True, name="scatter_start_" + unit)
        scatters[unit] = st[:4]
        return st[4]

    mods_rows = jnp.transpose(mods_all, (1, 2, 0, 3)).reshape(2, 2 * NDEV, N_MOD * d)
    mx = lax.dynamic_index_in_dim(mods_rows, me, axis=1, keepdims=False).reshape(2, N_MOD, d)
    mc = mods_rows[:, NDEV].reshape(2, N_MOD, d)
    mods = [(mc[i], mx[i]) for i in range(2)]

    small = dict(conv_w8=jnp.pad(conv_w_full, ((0, 8 - SSD_CONV), (0, 0))), conv_b=ssd_conv_b, dt_bias=ssd_dt_bias[0],
                 a_log=ssd_A_log[0], ssd_d=ssd_D[0], ssd_norm_g=ssd_norm_g[0], gm_v_g=gm_v_g_full,
                 gm_v_b=gm_v_b_full, gm_w_s=gm_w_s[0], gm_b_s=gm_b_s[0])
    loss_parts, grad_x, g = _local_step(x[0], ctx[0], loss_target[0], mods, norm_g_full, get_w, small, put_grad)
    loss = lax.psum(0.5 / d * jnp.sum(loss_parts), ("x", "y", "c"))

    upd = {}
    after = grad_x
    for unit, grp, idx in reversed(units):
        parts = _exchange_wait(*scatters[unit], after, scatter=True, name="scatter_wait_" + unit)
        for k, p in zip(idx, parts):
            m_, v_ = moment[grp][k]
            which = ("in", "out")[k]
            upd[grp, which] = _adamw(p, shard[grp][k], m_, v_, name=f"adamw_{grp}_{which}")
            after = upd[grp, which][0]
    res = {}
    for which in ("in", "out"):
        res["ffn_w_" + which] = [jnp.stack([jnp.stack([upd[f"ffn{i}{j}", which][k] for j in range(2)])
                                            for i in range(2)]) for k in range(4)]
        res["ssd_w_" + which] = [upd["ssd", which][k][None] for k in range(4)]
        res["gm_w_" + which] = [upd["gm", which][k][None] for k in range(4)]

    sg_names = ["dmx", "dmc", "norm_g", "ssd_conv_w", "ssd_conv_b", "ssd_dt_bias", "ssd_A_log", "ssd_D", "ssd_norm_g",
                "gm_v_g", "gm_v_b", "gm_w_s", "gm_b_s"]
    sg_shapes = [g[n].shape for n in sg_names]
    flat = jnp.concatenate([g[n].reshape(-1) for n in sg_names])
    npack = flat.shape[0]
    pad = (-npack) % 1024
    flat = jnp.pad(flat, (0, pad)).reshape(-1, 128)
    (sg_all,), _ = _exchange([flat], scatter=False, name="gather_small_grads")
    sg_sum = _sum_slots(sg_all, name="sum_small_grads").reshape(-1)[:npack]
    sums = {}
    o = 0
    for n, shp in zip(sg_names, sg_shapes):
        sz = math.prod(shp)
        sums[n] = sg_sum[o:o + sz].reshape(shp)
        o += sz
    per_dev = sg_all.reshape(NDEV, -1)
    dmx_all = per_dev[:, :2 * N_MOD * d].reshape(NDEV, 2, N_MOD * d)
    dmc_all = per_dev[:, 2 * N_MOD * d:4 * N_MOD * d].reshape(NDEV, 2, N_MOD * d)

    (s16,) = _rowwise("ada_silu", lambda cc: ((_silu(cc),), ()), 2 * NDEV, [c16], [], [(d, F32)], tm=2 * NDEV)
    s16_t = s16.T
    g_ada_w, dcc_parts = [], []
    for i in range(2):
        rhs = jnp.concatenate([lax.dynamic_slice_in_dim(dmx_all[:, i], me * ncol, ncol, axis=1),
                               lax.dynamic_slice_in_dim(dmc_all[:, i], me * ncol, ncol, axis=1)], axis=0)
        g_ada_w.append(_mm_f32(s16_t, rhs, name=f"ada_dw{i}"))
        dmc_loc = lax.dynamic_slice_in_dim(sums["dmc"][i], me * ncol, ncol, axis=0)
        rhs_c = jnp.zeros((ncol, 128), F32).at[:, 0].set(dmc_loc)
        dcc_parts.append(_mm_f32(ada_w[i], rhs_c, name=f"ada_dcc{i}")[:, 0])
    g_ada_w = jnp.stack(g_ada_w)
    dcc_part = (dcc_parts[0] + dcc_parts[1]).reshape(8, 128)
    (dcc_all,), _ = _exchange([dcc_part], scatter=False, name="gather_dcc")
    g_c_ctx = _sum_slots(dcc_all, name="sum_dcc", scale_by=c_ctx.reshape(8, 128)).reshape(d)
    g_ada_b = sums["dmx"] + sums["dmc"]

    outs = _adamw(g_ada_w.reshape(1, -1, ncol), ada_w.reshape(-1, ncol), m_ada_w.reshape(-1, ncol),
                  v_ada_w.reshape(-1, ncol), name="adamw_ada_w")
    res["ada_w"] = [o_.reshape(ada_w.shape) for o_ in outs]

    loc = lambda a, ax, n: lax.dynamic_slice_in_dim(a, me * n, n, axis=ax)
    small_g = dict(c_ctx=g_c_ctx, ada_b=g_ada_b, norm_g=loc(sums["norm_g"], 2, 128),
                   ssd_conv_w=loc(sums["ssd_conv_w"], 1, 512)[None], ssd_conv_b=sums["ssd_conv_b"][None],
                   ssd_dt_bias=sums["ssd_dt_bias"][None], ssd_A_log=sums["ssd_A_log"][None], ssd_D=sums["ssd_D"][None],
                   ssd_norm_g=sums["ssd_norm_g"][None], gm_v_g=loc(sums["gm_v_g"], 0, 256)[None],
                   gm_v_b=loc(sums["gm_v_b"], 0, 256)[None], gm_w_s=sums["gm_w_s"][None], gm_b_s=sums["gm_b_s"][None])
    small_w = dict(c_ctx=(c_ctx, m_c_ctx, v_c_ctx), ada_b=(ada_b, m_ada_b, v_ada_b), norm_g=(norm_g, m_norm_g, v_norm_g),
                   ssd_conv_w=(ssd_conv_w, m_ssd_conv_w, v_ssd_conv_w), ssd_conv_b=(ssd_conv_b, m_ssd_conv_b, v_ssd_conv_b),
                   ssd_dt_bias=(ssd_dt_bias, m_ssd_dt_bias, v_ssd_dt_bias), ssd_A_log=(ssd_A_log, m_ssd_A_log, v_ssd_A_log),
                   ssd_D=(ssd_D, m_ssd_D, v_ssd_D), ssd_norm_g=(ssd_norm_g, m_ssd_norm_g, v_ssd_norm_g),
                   gm_v_g=(gm_v_g, m_gm_v_g, v_gm_v_g), gm_v_b=(gm_v_b, m_gm_v_b, v_gm_v_b),
                   gm_w_s=(gm_w_s, m_gm_w_s, v_gm_w_s), gm_b_s=(gm_b_s, m_gm_b_s, v_gm_b_s))
    sn = list(small_w)

    def pack(arrs):
        f = jnp.concatenate([a.reshape(-1) for a in arrs])
        return jnp.pad(f, (0, (-f.shape[0]) % 1024)).reshape(-1, 128)

    pg = pack([small_g[n].reshape(small_w[n][0].shape) for n in sn])
    outs = _adamw(pg[None], pack([small_w[n][0] for n in sn]), pack([small_w[n][1] for n in sn]),
                  pack([small_w[n][2] for n in sn]), name="adamw_small")
    flat_outs = [o_.reshape(-1) for o_ in outs]
    o = 0
    for n in sn:
        shp = small_w[n][0].shape
        sz = math.prod(shp)
        res[n] = [fo[o:o + sz].reshape(shp) for fo in flat_outs]
        o += sz

    order = ["c_ctx", "ada_w", "ada_b", "norm_g", "ffn_w_in", "ffn_w_out", "ssd_w_in", "ssd_conv_w", "ssd_conv_b",
             "ssd_dt_bias", "ssd_A_log", "ssd_D", "ssd_norm_g", "ssd_w_out", "gm_w_in", "gm_v_g", "gm_v_b", "gm_w_s",
             "gm_b_s", "gm_w_out"]
    result = [loss, grad_x[None]]
    for k in range(4):
        result += [res[n][k] for n in order]
    return tuple(result)
```

```python
import functools
import math

import jax
import jax.numpy as jnp
from jax import lax
from jax.experimental import pallas as pl
from jax.experimental.pallas import tpu as pltpu

F32 = jnp.float32
BF16 = jnp.bfloat16

NDEV = 8
D_MODEL = 1024
FFN_DIM = 2816
N_MOD = 9
EPS = 1e-6
SSD_INNER = 2048
SSD_HEADS = 32
SSD_HEAD_DIM = 64
SSD_GROUPS = 8
SSD_HPG = 4
SSD_STATE = 128
SSD_CONV = 5
SSD_CONV_DIM = 4096
CHUNK = 128
GM_INNER = 2048
GM_GROUPS = 8
GM_GROUP_DIM = 256
ADAM_LR = 0.001
ADAM_B1 = 0.9
ADAM_B2 = 0.999
ADAM_EPS = 1e-08
ADAM_WD = 0.01
ADAM_STEP = 10
NEG_BIG = -1e30
VMEM_LIMIT_BYTES = 56 * 1024 * 1024
HI = lax.Precision.HIGHEST


def _params(*sem):
    return pltpu.CompilerParams(dimension_semantics=sem, vmem_limit_bytes=VMEM_LIMIT_BYTES)


def _pick(n, target, mult=16):
    if n <= target:
        return n
    for t in range(target - target % mult, 0, -mult):
        if n % t == 0:
            return t
    raise ValueError((n, target, mult))


def _sig(x):
    return 1.0 / (1.0 + jnp.exp(-x))


def _silu(x):
    return x * _sig(x)


def _dsilu(x):
    s = _sig(x)
    return s * (1.0 + x * (1.0 - s))


_GELU_C = math.sqrt(2.0 / math.pi)


def _gelu(x):
    return 0.5 * x * (1.0 + jnp.tanh(_GELU_C * (x + 0.044715 * x * x * x)))


def _dgelu(x):
    t = jnp.tanh(_GELU_C * (x + 0.044715 * x * x * x))
    return 0.5 * (1.0 + t) + 0.5 * x * (1.0 - t * t) * _GELU_C * (1.0 + 3.0 * 0.044715 * x * x)


def _softplus(x):
    return jnp.maximum(x, 0.0) + jnp.log1p(jnp.exp(-jnp.abs(x)))


def _sum0(v):
    return jnp.sum(v, axis=0, keepdims=True)


def _rms(h):
    r = lax.rsqrt(jnp.mean(h * h, axis=-1, keepdims=True) + EPS)
    return h * r, r


def _dot(a, b, dims=((1,), (0,)), precision=None):
    return lax.dot_general(a, b, (dims, ((), ())), preferred_element_type=F32, precision=precision)


_NT = ((1,), (1,))
_TN = ((0,), (0,))


def _rowwise(name, fn, n_rows, rows, consts, outs, accs=(), *, tm, nseg=1, seg_blocks=0):
    assert n_rows % tm == 0
    if nseg == 2:
        assert seg_blocks > 0
        seg = lambda i: jnp.where(i < seg_blocks, 0, 1)
    else:
        seg = lambda i: 0
    in_specs, args = [], []
    for r in rows:
        arr, width, cb, off = r if isinstance(r, tuple) else (r, r.shape[1], 0, 0)
        in_specs.append(pl.BlockSpec((tm, width), lambda i, cb=cb, off=off: (i + off, cb)))
        args.append(arr)
    for kind, arr in consts:
        if kind == "seg":
            assert arr.shape[0] == nseg and arr.shape[1] == 1, arr.shape
            in_specs.append(pl.BlockSpec((None, 1, arr.shape[2]), lambda i: (seg(i), 0, 0)))
        else:
            in_specs.append(pl.BlockSpec(arr.shape, lambda i: (0, 0)))
        args.append(arr)
    out_shape = [jax.ShapeDtypeStruct((n_rows, w), dt) for w, dt in outs]
    out_specs = [pl.BlockSpec((tm, w), lambda i: (i, 0)) for w, _ in outs]
    out_shape += [jax.ShapeDtypeStruct((nseg, 1, w), F32) for w in accs]
    out_specs += [pl.BlockSpec((None, 1, w), lambda i: (seg(i), 0, 0)) for w in accs]
    n_in, n_out, n_acc = len(args), len(outs), len(accs)

    def kern(*refs):
        ins = [r[...] for r in refs[:n_in]]
        res, sums = fn(*ins)
        for ref, v in zip(refs[n_in:n_in + n_out], res):
            ref[...] = v.astype(ref.dtype)
        if n_acc:
            i = pl.program_id(0)
            first = (i == 0) | (i == seg_blocks) if nseg == 2 else (i == 0)
            acc_refs = refs[n_in + n_out:]

            @pl.when(first)
            def _():
                for ref, v in zip(acc_refs, sums):
                    ref[...] = v

            @pl.when(jnp.logical_not(first))
            def _():
                for ref, v in zip(acc_refs, sums):
                    ref[...] += v

    res = pl.pallas_call(
        kern, name=name, grid=(n_rows // tm,), in_specs=in_specs, out_specs=out_specs, out_shape=out_shape,
        compiler_params=_params("arbitrary"),
    )(*args)
    return res


def _pre_fwd_fn(h, g, shift, scale):
    hh, _ = _rms(h)
    return (hh * g * (1.0 + scale) + shift,), ()


def _pre_bwd_fn(du, h, dres, g, scale):
    hh, r = _rms(h)
    n = hh * g
    dn = du * (1.0 + scale)
    dhh = dn * g
    dh = dres + r * (dhh - hh * jnp.mean(dhh * hh, axis=-1, keepdims=True))
    return (dh,), (_sum0(du), _sum0(du * n), _sum0(dn * hh))


def _post_fwd_fn(weight, h, y, g, gate):
    yh, _ = _rms(y)
    return (h + weight * gate * (yh * g),), ()


def _post_bwd_fn(weight, dh, y, g, gate):
    yh, r = _rms(y)
    dr = dh * weight
    dyh = dr * gate * g
    dy = r * (dyh - yh * jnp.mean(dyh * yh, axis=-1, keepdims=True))
    return (dy,), (_sum0(dr * yh * g), _sum0(dr * gate * yh))


def _glu_bwd_fn(ds, a, b):
    a = a.astype(F32)
    b = b.astype(F32)
    sg = _sig(a)
    da = ds * b * (sg * (1.0 + a * (1.0 - sg)))
    db = ds * (a * sg)
    return (jnp.concatenate([da, db], axis=1),), ()


def _loss_fn(y, t):
    diff = y - t
    return (diff * (1.0 / D_MODEL),), (_sum0(diff * diff),)


def _ssd_y(yf, yb, xs, z, dvec):
    y = yf + yb + dvec * xs
    return y, y * _silu(z)


def _ssdgate_fwd_fn(yf, yb, xs, z, dvec, ng):
    _, yg = _ssd_y(yf, yb, xs, z, dvec)
    parts = []
    for g in range(SSD_GROUPS):
        sl = slice(g * 256, (g + 1) * 256)
        parts.append(_rms(yg[:, sl])[0])
    return (jnp.concatenate(parts, axis=1) * ng,), ()


def _ssdgate_bwd_fn(dyn, yf, yb, xs, z, dvec, ng):
    y, yg = _ssd_y(yf, yb, xs, z, dvec)
    dyg_parts, ygh_parts = [], []
    for g in range(SSD_GROUPS):
        sl = slice(g * 256, (g + 1) * 256)
        ygh, r = _rms(yg[:, sl])
        d = dyn[:, sl] * ng[:, sl]
        dyg_parts.append(r * (d - ygh * jnp.mean(d * ygh, axis=-1, keepdims=True)))
        ygh_parts.append(ygh)
    dyg = jnp.concatenate(dyg_parts, axis=1)
    ygh = jnp.concatenate(ygh_parts, axis=1)
    dy = dyg * _silu(z)
    dz = dyg * y * _dsilu(z)
    return (dy, dz), (_sum0(dyn * ygh), _sum0(dy * xs))


def _ln_stats(v):
    mu = jnp.mean(v, axis=-1, keepdims=True)
    vc = v - mu
    r = lax.rsqrt(jnp.mean(vc * vc, axis=-1, keepdims=True) + EPS)
    return vc * r, r


def _gm_act_fwd_fn(p, vg, vb):
    gu = _gelu(p[:, :GM_INNER])
    gvh, _ = _ln_stats(_gelu(p[:, GM_INNER:]))
    return (gu, gvh * vg + vb), ()


def _gm_act_bwd_fn(p, dgu, dgvn, vg):
    pu = p[:, :GM_INNER]
    pv = p[:, GM_INNER:]
    gvh, r = _ln_stats(_gelu(pv))
    dgvh = dgvn * vg
    dgv = r * (dgvh - jnp.mean(dgvh, axis=-1, keepdims=True) - gvh * jnp.mean(dgvh * gvh, axis=-1, keepdims=True))
    dp = jnp.concatenate([dgu * _dgelu(pu), dgv * _dgelu(pv)], axis=1)
    return (dp,), (_sum0(dgvn * gvh), _sum0(dgvn))


def _mm(a, b, *, out_dtype, name, tm=640, tn=512, tk=1024, add=None):
    m, k = a.shape
    k2, n = b.shape
    assert k == k2
    tm, tn, tk = _pick(m, tm), _pick(n, tn, 128), _pick(k, tk, 128)
    nk = k // tk

    def kern(*refs):
        if add is None:
            a_ref, b_ref, o_ref, acc_ref = refs
        else:
            a_ref, b_ref, add_ref, o_ref, acc_ref = refs
        kk = pl.program_id(2)

        @pl.when(kk == 0)
        def _():
            acc_ref[...] = jnp.zeros_like(acc_ref)

        acc_ref[...] += jnp.dot(a_ref[...], b_ref[...], preferred_element_type=F32)

        @pl.when(kk == nk - 1)
        def _():
            r = acc_ref[...]
            if add is not None:
                r = r + add_ref[...]
            o_ref[...] = r.astype(o_ref.dtype)

    in_specs = [pl.BlockSpec((tm, tk), lambda i, j, kk: (i, kk)), pl.BlockSpec((tk, tn), lambda i, j, kk: (kk, j))]
    args = [a, b]
    if add is not None:
        in_specs.append(pl.BlockSpec((tm, tn), lambda i, j, kk: (i, j)))
        args.append(add)
    return pl.pallas_call(
        kern, name=name, grid=(m // tm, n // tn, nk), in_specs=in_specs,
        out_specs=pl.BlockSpec((tm, tn), lambda i, j, kk: (i, j)),
        out_shape=jax.ShapeDtypeStruct((m, n), out_dtype),
        scratch_shapes=[pltpu.VMEM((tm, tn), F32)],
        compiler_params=_params("parallel", "parallel", "arbitrary"),
    )(*args)


def _mm_glu(u, wa, wb, *, name, tm=640, tn=256):
    m, k = u.shape
    n = wa.shape[1]
    tm, tn = _pick(m, tm), _pick(n, tn, 128)

    def kern(u_ref, wa_ref, wb_ref, s_ref, a_ref, b_ref):
        uu = u_ref[...]
        a = jnp.dot(uu, wa_ref[...], preferred_element_type=F32)
        b = jnp.dot(uu, wb_ref[...], preferred_element_type=F32)
        s_ref[...] = (_silu(a) * b).astype(BF16)
        a_ref[...] = a.astype(BF16)
        b_ref[...] = b.astype(BF16)

    ospec = pl.BlockSpec((tm, tn), lambda i, j: (i, j))
    return pl.pallas_call(
        kern, name=name, grid=(m // tm, n // tn),
        in_specs=[pl.BlockSpec((tm, k), lambda i, j: (i, 0)), pl.BlockSpec((k, tn), lambda i, j: (0, j)),
                  pl.BlockSpec((k, tn), lambda i, j: (0, j))],
        out_specs=[ospec, ospec, ospec],
        out_shape=[jax.ShapeDtypeStruct((m, n), BF16)] * 3,
        compiler_params=_params("parallel", "parallel"),
    )(u, wa, wb)


def _mm_tn(a, b, *, name, tm=512, tn=512, tk=512):
    t, m = a.shape
    t2, n = b.shape
    assert t == t2
    tm, tn, tk = _pick(m, tm, 128), _pick(n, tn, 128), _pick(t, tk)
    nk = t // tk

    def kern(a_ref, b_ref, o_ref):
        kk = pl.program_id(2)

        @pl.when(kk == 0)
        def _():
            o_ref[...] = jnp.zeros_like(o_ref)

        o_ref[...] += _dot(a_ref[...], b_ref[...], _TN)

    return pl.pallas_call(
        kern, name=name, grid=(m // tm, n // tn, nk),
        in_specs=[pl.BlockSpec((tk, tm), lambda i, j, kk: (kk, i)), pl.BlockSpec((tk, tn), lambda i, j, kk: (kk, j))],
        out_specs=pl.BlockSpec((tm, tn), lambda i, j, kk: (i, j)),
        out_shape=jax.ShapeDtypeStruct((m, n), F32),
        compiler_params=_params("parallel", "parallel", "arbitrary"),
    )(a, b)


def _mm_f32(a, b, *, name, silu_a=False, bias=None):
    m, k = a.shape
    n = b.shape[1]

    def kern(*refs):
        if bias is None:
            a_ref, b_ref, o_ref = refs
        else:
            a_ref, b_ref, bias_ref, o_ref = refs
        av = a_ref[...]
        if silu_a:
            av = _silu(av)
        r = jnp.dot(av, b_ref[...], preferred_element_type=F32, precision=HI)
        if bias is not None:
            r = r + bias_ref[...]
        o_ref[...] = r

    args = [a, b] + ([] if bias is None else [bias])
    return pl.pallas_call(kern, name=name, out_shape=jax.ShapeDtypeStruct((m, n), F32),
                          compiler_params=pltpu.CompilerParams(vmem_limit_bytes=VMEM_LIMIT_BYTES))(*args)


def _shifted(v, s, t, lo, hi):
    n = v.shape[0]
    r = v if s == 0 else pltpu.roll(v, (-s) % n, 0)
    ok = (t + s >= lo) & (t + s < hi)
    return jnp.where(ok, r, 0.0)


def _seg_bounds(n, n_ctx):
    t = lax.broadcasted_iota(jnp.int32, (n, 1), 0)
    lo = jnp.where(t < n_ctx, 0, n_ctx)
    hi = jnp.where(t < n_ctx, n_ctx, n)
    return t, lo, hi


def _conv_fwd(xp, w8, b, *, n_ctx, name, cb=256):
    n, c = xp.shape

    def kern(x_ref, w_ref, b_ref, cpre_ref, act_ref):
        x = x_ref[...]
        t, lo, hi = _seg_bounds(n, n_ctx)
        acc = jnp.zeros_like(x) + b_ref[...]
        for k in range(SSD_CONV):
            acc = acc + _shifted(x, k - SSD_CONV // 2, t, lo, hi) * w_ref[k:k + 1, :]
        cpre_ref[...] = acc
        act_ref[...] = _silu(acc)

    spec = pl.BlockSpec((n, cb), lambda j: (0, j))
    return pl.pallas_call(
        kern, name=name, grid=(c // cb,),
        in_specs=[spec, pl.BlockSpec((8, cb), lambda j: (0, j)), pl.BlockSpec((1, cb), lambda j: (0, j))],
        out_specs=[spec, spec], out_shape=[jax.ShapeDtypeStruct((n, c), F32)] * 2,
        compiler_params=_params("parallel"),
    )(xp, w8, b)


def _conv_bwd(d1, d2, cpre, xp, w8, *, n_ctx, name, cb=128):
    n, c = xp.shape

    def kern(d1_ref, d2_ref, cpre_ref, x_ref, w_ref, dx_ref, dw_ref, db_ref):
        g = (d1_ref[...] + d2_ref[...]) * _dsilu(cpre_ref[...])
        x = x_ref[...]
        t, lo, hi = _seg_bounds(n, n_ctx)
        dx = jnp.zeros_like(g)
        dw_ref[...] = jnp.zeros_like(dw_ref)
        for k in range(SSD_CONV):
            s = k - SSD_CONV // 2
            dx = dx + _shifted(g, -s, t, lo, hi) * w_ref[k:k + 1, :]
            dw_ref[k:k + 1, :] = _sum0(g * _shifted(x, s, t, lo, hi))
        dx_ref[...] = dx.astype(BF16)
        db_ref[...] = _sum0(g)

    spec = pl.BlockSpec((n, cb), lambda j: (0, j))
    return pl.pallas_call(
        kern, name=name, grid=(c // cb,),
        in_specs=[spec, spec, spec, spec, pl.BlockSpec((8, cb), lambda j: (0, j))],
        out_specs=[spec, pl.BlockSpec((8, cb), lambda j: (0, j)), pl.BlockSpec((1, cb), lambda j: (0, j))],
        out_shape=[jax.ShapeDtypeStruct((n, c), BF16), jax.ShapeDtypeStruct((8, c), F32),
                   jax.ShapeDtypeStruct((1, c), F32)],
        compiler_params=_params("parallel"),
    )(d1, d2, cpre, xp, w8)


def _chunk_of(s, nc, n_ctx_chunks, rev):
    if not rev:
        return s
    return jnp.where(s < n_ctx_chunks, n_ctx_chunks - 1 - s, nc - 1 - (s - n_ctx_chunks))


def _scan_common(dt_raw, dtT_raw, bias_r, bias_c, alog_r, alog_c, rev):
    ii = lax.broadcasted_iota(jnp.int32, (CHUNK, CHUNK), 0)
    jj = lax.broadcasted_iota(jnp.int32, (CHUNK, CHUNK), 1)
    tri = (jj >= ii) if rev else (jj <= ii)
    tri_t = (ii >= jj) if rev else (ii <= jj)
    a_r = -jnp.exp(alog_r)
    a_c = -jnp.exp(alog_c)
    dt = _softplus(dt_raw + bias_r)
    dt_t = _softplus(dtT_raw + bias_c)
    al = dt * a_r
    acum = _dot(tri.astype(F32), al, precision=HI)
    acum_t = _dot(dt_t * a_c, tri_t.astype(F32), precision=HI)
    atot = _sum0(al)
    return tri, tri_t, a_r, dt, acum, acum_t, atot


def _head_spread():
    return jnp.repeat(jnp.eye(SSD_HEADS, dtype=BF16), SSD_HEAD_DIM, axis=1)


def _dot_sel(v, sel):
    hi = v.astype(BF16)
    lo = (v - hi.astype(F32)).astype(BF16)
    return _dot(hi, sel) + _dot(lo, sel)


def _ssd_scan_fwd(xbc, dt_raw, dtT_raw, bias_r, bias_c, alog_r, alog_c, *, rev, n_ctx_chunks, name):
    n = xbc.shape[0]
    nc = n // CHUNK
    cidx = functools.partial(_chunk_of, nc=nc, n_ctx_chunks=n_ctx_chunks, rev=rev)

    def kern(xs_ref, b_ref, c_ref, dt_ref, dtT_ref, br_ref, bc_ref, ar_ref, ac_ref, e_ref, y_ref, hs_ref, h_scr):
        @pl.when(pl.program_id(0) == 0)
        def _():
            h_scr[...] = jnp.zeros_like(h_scr)

        tri, _, _, dt, acum, acum_t, atot = _scan_common(
            dt_ref[...], dtT_ref[...], br_ref[...], bc_ref[...], ar_ref[...], ac_ref[...], rev)
        etot = jnp.exp(atot)
        spread = lambda v: _dot_sel(v, e_ref[...])
        xdt_all = xs_ref[...] * spread(dt)
        eax = spread(jnp.exp(acum))
        xdw_all = xdt_all * spread(jnp.exp(atot - acum))
        hs_ref[...] = h_scr[...]
        for g in range(SSD_GROUPS):
            gs = slice(g * 256, (g + 1) * 256)
            bg = b_ref[:, g * SSD_STATE:(g + 1) * SSD_STATE].astype(BF16)
            cg = c_ref[:, g * SSD_STATE:(g + 1) * SSD_STATE].astype(BF16)
            cb = _dot(cg, bg, _NT)
            h4 = h_scr[gs, :]
            ys = []
            for k in range(SSD_HPG):
                h = g * SSD_HPG + k
                lmat = jnp.exp(jnp.where(tri, acum[:, h:h + 1] - acum_t[h:h + 1, :], NEG_BIG))
                xdt_h = xdt_all[:, h * SSD_HEAD_DIM:(h + 1) * SSD_HEAD_DIM].astype(BF16)
                ys.append(_dot((cb * lmat).astype(BF16), xdt_h))
            y_ref[:, gs] = jnp.concatenate(ys, axis=1) + _dot(cg, h4.astype(BF16), _NT) * eax[:, gs]
            s4 = _dot(xdw_all[:, gs].astype(BF16), bg, _TN)
            for k in range(SSD_HPG):
                h = g * SSD_HPG + k
                rs = slice(h * SSD_HEAD_DIM, (h + 1) * SSD_HEAD_DIM)
                h_scr[rs, :] = h4[k * SSD_HEAD_DIM:(k + 1) * SSD_HEAD_DIM] * etot[:, h:h + 1] + \
                    s4[k * SSD_HEAD_DIM:(k + 1) * SSD_HEAD_DIM]

    nh = SSD_HEADS
    small = lambda shape: pl.BlockSpec(shape, lambda s: (0, 0))
    return pl.pallas_call(
        kern, name=name, grid=(nc,),
        in_specs=[pl.BlockSpec((CHUNK, SSD_INNER), lambda s: (cidx(s), 0)),
                  pl.BlockSpec((CHUNK, 1024), lambda s: (cidx(s), 2)),
                  pl.BlockSpec((CHUNK, 1024), lambda s: (cidx(s), 3)),
                  pl.BlockSpec((CHUNK, nh), lambda s: (cidx(s), 0)),
                  pl.BlockSpec((nh, CHUNK), lambda s: (0, cidx(s))),
                  small((1, nh)), small((nh, 1)), small((1, nh)), small((nh, 1)), small((nh, SSD_INNER))],
        out_specs=[pl.BlockSpec((CHUNK, SSD_INNER), lambda s: (cidx(s), 0)),
                   pl.BlockSpec((None, SSD_INNER, SSD_STATE), lambda s: (s, 0, 0))],
        out_shape=[jax.ShapeDtypeStruct((n, SSD_INNER), F32),
                   jax.ShapeDtypeStruct((nc, SSD_INNER, SSD_STATE), F32)],
        scratch_shapes=[pltpu.VMEM((SSD_INNER, SSD_STATE), F32)],
        compiler_params=_params("arbitrary"),
    )(xbc, xbc, xbc, dt_raw, dtT_raw, bias_r, bias_c, alog_r, alog_c, _head_spread())


def _ssd_scan_bwd(dy, xbc, hs, dt_raw, dtT_raw, bias_r, bias_c, alog_r, alog_c, dvec, *, rev, n_ctx_chunks,
                  direct, name):
    n = xbc.shape[0]
    nc = n // CHUNK
    nh = SSD_HEADS
    step_of = lambda r: nc - 1 - r
    cidx = lambda r: _chunk_of(step_of(r), nc, n_ctx_chunks, rev)

    def kern(dy_ref, xs_ref, b_ref, c_ref, hs_ref, dt_ref, dtT_ref, br_ref, bc_ref, ar_ref, ac_ref, dv_ref,
             e_ref, et_ref, dx_ref, ddt_ref, dal_ref, dbias_ref, dh_scr):
        @pl.when(pl.program_id(0) == 0)
        def _():
            dh_scr[...] = jnp.zeros_like(dh_scr)
            dal_ref[...] = jnp.zeros_like(dal_ref)
            dbias_ref[...] = jnp.zeros_like(dbias_ref)

        tri, tri_t, a_r, dt, acum, acum_t, atot = _scan_common(
            dt_ref[...], dtT_ref[...], br_ref[...], bc_ref[...], ar_ref[...], ac_ref[...], rev)
        etot = jnp.exp(atot)
        spread = lambda v: _dot_sel(v, e_ref[...])
        gather = lambda v: _dot_sel(v, et_ref[...])
        xs_all = xs_ref[...]
        dy_all = dy_ref[...]
        dtx = spread(dt)
        eax = spread(jnp.exp(acum))
        decx = spread(jnp.exp(atot - acum))
        xdt_all = xs_all * dtx
        xdw_all = xdt_all * decx
        dyo_all = dy_all * eax
        lane = lax.broadcasted_iota(jnp.int32, (CHUNK, nh), 1)
        lane1 = lax.broadcasted_iota(jnp.int32, (1, nh), 1)
        sub = lax.broadcasted_iota(jnp.int32, (nh, CHUNK), 0)
        g_rows = jnp.zeros((CHUNK, nh), F32)
        g_cols = jnp.zeros((nh, CHUNK), F32)
        dtot = jnp.zeros((1, nh), F32)
        q_col, q_e, q_dt = [], [], []
        for g in range(SSD_GROUPS):
            gs = slice(g * 256, (g + 1) * 256)
            bg = b_ref[:, g * SSD_STATE:(g + 1) * SSD_STATE].astype(BF16)
            cg = c_ref[:, g * SSD_STATE:(g + 1) * SSD_STATE].astype(BF16)
            cb = _dot(cg, bg, _NT)
            hs4 = hs_ref[gs, :]
            dh4 = dh_scr[gs, :]
            hs4_bf = hs4.astype(BF16)
            dh4_bf = dh4.astype(BF16)
            dy4 = dy_all[:, gs]
            dy4_bf = dy4.astype(BF16)
            xdt4_bf = xdt_all[:, gs].astype(BF16)
            xdw4 = xdw_all[:, gs]
            xdw4_bf = xdw4.astype(BF16)
            dyo4_bf = dyo_all[:, gs].astype(BF16)
            yoff4 = _dot(cg, hs4_bf, _NT) * eax[:, gs]
            dcg = _dot(dyo4_bf, hs4_bf)
            dh_new4 = _dot(dyo4_bf, cg, _TN)
            bdh4 = _dot(bg, dh4_bf, _NT)
            dbg = _dot(xdw4_bf, dh4_bf)
            e4 = xdw4 * bdh4
            q_col.append(dy4 * yoff4 - e4)
            q_e.append(e4)
            hsum = jnp.sum(dh4 * hs4, axis=1, keepdims=True)
            dcb = jnp.zeros((CHUNK, CHUNK), F32)
            dxdts = []
            for k in range(SSD_HPG):
                h = g * SSD_HPG + k
                ks = slice(k * SSD_HEAD_DIM, (k + 1) * SSD_HEAD_DIM)
                lmat = jnp.exp(jnp.where(tri, acum[:, h:h + 1] - acum_t[h:h + 1, :], NEG_BIG))
                mf = cb * lmat
                dm = _dot(dy4_bf[:, ks], xdt4_bf[:, ks], _NT)
                dcb = dcb + dm * lmat
                gmat = dm * mf
                g_rows = g_rows + jnp.where(lane == h, jnp.sum(gmat, axis=1, keepdims=True), 0.0)
                g_cols = g_cols + jnp.where(sub == h, _sum0(gmat), 0.0)
                dxdts.append(_dot(mf.astype(BF16), dy4_bf[:, ks], _TN))
                et = etot[:, h:h + 1]
                dtot = dtot + jnp.where(lane1 == h, _sum0(hsum[ks]) * et, 0.0)
                dh_scr[h * SSD_HEAD_DIM:(h + 1) * SSD_HEAD_DIM, :] = dh4[ks] * et + dh_new4[ks]
            dxdt4 = jnp.concatenate(dxdts, axis=1) + bdh4 * decx[:, gs]
            q_dt.append(dxdt4 * xs_all[:, gs])
            dx4 = dxdt4 * dtx[:, gs]
            if direct:
                dx4 = dx4 + dy4 * dv_ref[:, gs]
            dcb_bf = dcb.astype(BF16)
            dx_ref[:, gs] = dx4
            dx_ref[:, SSD_INNER + g * SSD_STATE:SSD_INNER + (g + 1) * SSD_STATE] = dbg + _dot(dcb_bf, cg, _TN)
            dx_ref[:, SSD_INNER + 1024 + g * SSD_STATE:SSD_INNER + 1024 + (g + 1) * SSD_STATE] = \
                dcg + _dot(dcb_bf, bg)
        e_heads = gather(jnp.concatenate(q_e, axis=1))
        dacum = gather(jnp.concatenate(q_col, axis=1)) + g_rows - g_cols.T
        dal = _dot(tri_t.astype(F32), dacum, precision=HI) + dtot + _sum0(e_heads)
        ddt = gather(jnp.concatenate(q_dt, axis=1)) + dal * a_r
        ddt_raw = ddt * _sig(dt_ref[...] + br_ref[...])
        ddt_ref[...] = ddt_raw
        dal_ref[...] += _sum0(dal * dt) * a_r
        dbias_ref[...] += _sum0(ddt_raw)

    small = lambda shape: pl.BlockSpec(shape, lambda r: (0, 0))
    return pl.pallas_call(
        kern, name=name, grid=(nc,),
        in_specs=[pl.BlockSpec((CHUNK, SSD_INNER), lambda r: (cidx(r), 0)),
                  pl.BlockSpec((CHUNK, SSD_INNER), lambda r: (cidx(r), 0)),
                  pl.BlockSpec((CHUNK, 1024), lambda r: (cidx(r), 2)),
                  pl.BlockSpec((CHUNK, 1024), lambda r: (cidx(r), 3)),
                  pl.BlockSpec((None, SSD_INNER, SSD_STATE), lambda r: (step_of(r), 0, 0)),
                  pl.BlockSpec((CHUNK, nh), lambda r: (cidx(r), 0)),
                  pl.BlockSpec((nh, CHUNK), lambda r: (0, cidx(r))),
                  small((1, nh)), small((nh, 1)), small((1, nh)), small((nh, 1)), small((1, SSD_INNER)),
                  small((nh, SSD_INNER)), small((SSD_INNER, nh))],
        out_specs=[pl.BlockSpec((CHUNK, SSD_CONV_DIM), lambda r: (cidx(r), 0)),
                   pl.BlockSpec((CHUNK, nh), lambda r: (cidx(r), 0)),
                   small((1, nh)), small((1, nh))],
        out_shape=[jax.ShapeDtypeStruct((n, SSD_CONV_DIM), F32), jax.ShapeDtypeStruct((n, nh), F32),
                   jax.ShapeDtypeStruct((1, nh), F32), jax.ShapeDtypeStruct((1, nh), F32)],
        scratch_shapes=[pltpu.VMEM((SSD_INNER, SSD_STATE), F32)],
        compiler_params=_params("arbitrary"),
    )(dy, xbc, xbc, xbc, hs, dt_raw, dtT_raw, bias_r, bias_c, alog_r, alog_c, dvec, _head_spread(),
      _head_spread().T)


def _gm_spatial_fwd(gu, gvn, ws, bst, *, name):
    n = gu.shape[0]

    def kern(gu_ref, gv_ref, ws_ref, bs_ref, o_ref):
        for g in range(GM_GROUPS):
            sl = slice(g * GM_GROUP_DIM, (g + 1) * GM_GROUP_DIM)
            s = _dot(ws_ref[g], gv_ref[:, sl]) + bs_ref[:, g:g + 1]
            o_ref[:, sl] = (gu_ref[:, sl] * s).astype(BF16)

    spec = pl.BlockSpec((CHUNK, GM_INNER), lambda i: (i, 0))
    return pl.pallas_call(
        kern, name=name, grid=(n // CHUNK,),
        in_specs=[spec, spec, pl.BlockSpec(ws.shape, lambda i: (0, 0, 0)), pl.BlockSpec(bst.shape, lambda i: (0, 0))],
        out_specs=spec, out_shape=jax.ShapeDtypeStruct((n, GM_INNER), BF16),
        compiler_params=_params("parallel"),
    )(gu, gvn, ws, bst)


def _gm_spatial_bwd(dt, gu, gvn, ws, wst, bst, *, name):
    n = gu.shape[0]

    def kern(dt_ref, gu_ref, gv_ref, ws_ref, wst_ref, bs_ref, dgu_ref, dgv_ref, dws_ref, dbs_ref):
        @pl.when(pl.program_id(0) == 0)
        def _():
            dws_ref[...] = jnp.zeros_like(dws_ref)
            dbs_ref[...] = jnp.zeros_like(dbs_ref)

        lane = lax.broadcasted_iota(jnp.int32, (CHUNK, GM_GROUPS), 1)
        dbs = jnp.zeros((CHUNK, GM_GROUPS), F32)
        for g in range(GM_GROUPS):
            sl = slice(g * GM_GROUP_DIM, (g + 1) * GM_GROUP_DIM)
            gv = gv_ref[:, sl]
            s = _dot(ws_ref[g], gv) + bs_ref[:, g:g + 1]
            d = dt_ref[:, sl]
            dgu_ref[:, sl] = d * s
            ds = d * gu_ref[:, sl]
            ds_bf = ds.astype(BF16)
            dws_ref[g] += _dot(ds_bf, gv, _NT)
            dgv_ref[:, sl] = _dot(wst_ref[g], ds_bf)
            dbs = dbs + jnp.where(lane == g, jnp.sum(ds, axis=1, keepdims=True), 0.0)
        dbs_ref[...] += dbs

    spec = pl.BlockSpec((CHUNK, GM_INNER), lambda i: (i, 0))
    wspec = pl.BlockSpec(ws.shape, lambda i: (0, 0, 0))
    bspec = pl.BlockSpec(bst.shape, lambda i: (0, 0))
    return pl.pallas_call(
        kern, name=name, grid=(n // CHUNK,),
        in_specs=[spec, spec, spec, wspec, wspec, bspec],
        out_specs=[spec, spec, wspec, bspec],
        out_shape=[jax.ShapeDtypeStruct((n, GM_INNER), F32), jax.ShapeDtypeStruct((n, GM_INNER), F32),
                   jax.ShapeDtypeStruct(ws.shape, F32), jax.ShapeDtypeStruct(bst.shape, F32)],
        compiler_params=_params("arbitrary"),
    )(dt, gu, gvn, ws, wst, bst)


def _adamw(parts, w, m, v, *, name, tm=256):
    ns, r, wd = parts.shape
    tm = _pick(r, tm, 8)

    def kern(p_ref, w_ref, m_ref, v_ref, g_ref, d_ref, nm_ref, nv_ref):
        g = p_ref[0].astype(F32)
        for s in range(1, ns):
            g = g + p_ref[s].astype(F32)
        m2 = ADAM_B1 * m_ref[...] + (1.0 - ADAM_B1) * g
        v2 = ADAM_B2 * v_ref[...] + (1.0 - ADAM_B2) * (g * g)
        m_hat = m2 / (1.0 - ADAM_B1 ** ADAM_STEP)
        v_hat = v2 / (1.0 - ADAM_B2 ** ADAM_STEP)
        g_ref[...] = g
        d_ref[...] = -ADAM_LR * (m_hat / (jnp.sqrt(v_hat) + ADAM_EPS) + ADAM_WD * w_ref[...])
        nm_ref[...] = m2
        nv_ref[...] = v2

    spec = pl.BlockSpec((tm, wd), lambda i: (i, 0))
    return pl.pallas_call(
        kern, name=name, grid=(r // tm,),
        in_specs=[pl.BlockSpec((ns, tm, wd), lambda i: (0, i, 0)), spec, spec, spec],
        out_specs=[spec] * 4, out_shape=[jax.ShapeDtypeStruct((r, wd), F32)] * 4,
        compiler_params=_params("parallel"),
    )(parts, w, m, v)


def _sum_slots(parts, *, name, scale_by=None):
    ns, r, wd = parts.shape

    def kern(*refs):
        p_ref, o_ref = refs[0], refs[-1]
        g = p_ref[0]
        for s in range(1, ns):
            g = g + p_ref[s]
        if scale_by is not None:
            g = g * _dsilu(refs[1][...])
        o_ref[...] = g

    args = [parts] + ([] if scale_by is None else [scale_by])
    return pl.pallas_call(kern, name=name, out_shape=jax.ShapeDtypeStruct((r, wd), F32),
                          compiler_params=pltpu.CompilerParams(vmem_limit_bytes=VMEM_LIMIT_BYTES))(*args)


def _mesh_pos():
    x, y, c = lax.axis_index("x"), lax.axis_index("y"), lax.axis_index("c")
    return x, y, c, 4 * x + 2 * y + c


def _flip(x, y, c, f):
    fx, fy, fc = (f >> 2) & 1, (f >> 1) & 1, f & 1
    px = 1 - x if fx else x
    py = 1 - y if fy else y
    pc = 1 - c if fc else c
    return (px, py, pc), 4 * px + 2 * py + pc


_HBM_SPEC = pl.BlockSpec(memory_space=pltpu.HBM)


def _exchange(arrays, *, scatter, name):
    na = len(arrays)
    if scatter:
        out_shape = [jax.ShapeDtypeStruct(a.shape, a.dtype) for a in arrays]
    else:
        out_shape = [jax.ShapeDtypeStruct((NDEV,) + a.shape, a.dtype) for a in arrays]

    out_shape.append(jax.ShapeDtypeStruct((8, 128), F32))

    def body(*refs):
        ins, outs = refs[:na], refs[na:2 * na]
        send_sems, recv_sems, local_sems = refs[2 * na + 1:]
        refs[2 * na][...] = jnp.zeros((8, 128), F32)
        x, y, c, me = _mesh_pos()
        copies = []
        for i in range(na):
            src_own = ins[i].at[me] if scatter else ins[i]
            lc = pltpu.make_async_copy(src_own, outs[i].at[me], local_sems.at[i])
            lc.start()
            copies.append(lc)
        sends = []
        for f in range(1, NDEV):
            peer, pidx = _flip(x, y, c, f)
            for i in range(na):
                k = i * (NDEV - 1) + f - 1
                src = ins[i].at[pidx] if scatter else ins[i]
                cp = pltpu.make_async_remote_copy(
                    src_ref=src, dst_ref=outs[i].at[me], send_sem=send_sems.at[k], recv_sem=recv_sems.at[k],
                    device_id=peer, device_id_type=pl.DeviceIdType.MESH)
                cp.start()
                sends.append(cp)
        for f in range(1, NDEV):
            peer, pidx = _flip(x, y, c, f)
            for i in range(na):
                k = i * (NDEV - 1) + f - 1
                src = ins[i].at[pidx] if scatter else ins[i]
                pltpu.make_async_remote_copy(
                    src_ref=src, dst_ref=outs[i].at[pidx], send_sem=send_sems.at[k], recv_sem=recv_sems.at[k],
                    device_id=peer, device_id_type=pl.DeviceIdType.MESH).wait_recv()
        for cp in sends:
            cp.wait_send()
        for lc in copies:
            lc.wait()

    res = pl.pallas_call(
        body, name=name, out_shape=out_shape, in_specs=[_HBM_SPEC] * na,
        out_specs=[_HBM_SPEC] * na + [pl.BlockSpec(memory_space=pltpu.VMEM)],
        scratch_shapes=[pltpu.SemaphoreType.DMA((na * (NDEV - 1),)), pltpu.SemaphoreType.DMA((na * (NDEV - 1),)),
                        pltpu.SemaphoreType.DMA((na,))],
        compiler_params=pltpu.CompilerParams(has_side_effects=True),
    )(*arrays)
    return res[:na], res[na][0, 0]


_SEM_SPEC = pl.BlockSpec(memory_space=pltpu.SEMAPHORE)
_DATAFLOW = pltpu.SideEffectType.DATAFLOW_SIDE_EFFECTING


def _split_copies(srcs, lands, send_sems, recv_sems, scatter, arriving):
    x, y, c, me = _mesh_pos()
    copies = []
    for i in range(len(srcs)):
        for f in range(1, NDEV):
            peer, pidx = _flip(x, y, c, f)
            k = i * (NDEV - 1) + f - 1
            copies.append(pltpu.make_async_remote_copy(
                src_ref=srcs[i].at[pidx] if scatter else srcs[i], dst_ref=lands[i].at[pidx if arriving else me],
                send_sem=send_sems.at[k], recv_sem=recv_sems.at[k], device_id=peer,
                device_id_type=pl.DeviceIdType.MESH))
    return copies


def _exchange_start(srcs, lands, *, scatter, name):
    na = len(srcs)
    nsem = na * (NDEV - 1)

    def body(*refs):
        ins_src, ins_land = refs[:na], refs[na:2 * na]
        send_sems, recv_sems = refs[2 * na], refs[2 * na + 1]
        token = refs[-1]
        for cp in _split_copies(ins_src, ins_land, send_sems, recv_sems, scatter, False):
            cp.start()
        token[...] = jnp.zeros_like(token)

    thru = [pltpu.HBM(a.shape, a.dtype) for a in list(srcs) + list(lands)]
    res = pl.pallas_call(
        body, name=name,
        out_shape=(pltpu.SemaphoreType.DMA((nsem,)), pltpu.SemaphoreType.DMA((nsem,)), *thru,
                   jax.ShapeDtypeStruct((8, 128), F32)),
        in_specs=[_HBM_SPEC] * (2 * na),
        out_specs=(_SEM_SPEC, _SEM_SPEC, *([_HBM_SPEC] * (2 * na)), pl.BlockSpec(memory_space=pltpu.VMEM)),
        input_output_aliases={i: 2 + i for i in range(2 * na)},
        compiler_params=pltpu.CompilerParams(has_side_effects=_DATAFLOW),
    )(*[pltpu.with_memory_space_constraint(a, pltpu.HBM) for a in list(srcs) + list(lands)])
    send_sems, recv_sems = res[0], res[1]
    return send_sems, recv_sems, res[2:2 + na], res[2 + na:2 + 2 * na], res[-1][0, 0]


def _exchange_wait(send_sems, recv_sems, srcs, lands, after, *, scatter, name):
    na = len(srcs)

    def body(*refs):
        ins_src, ins_land = refs[:na], refs[na:2 * na]
        s_sems, r_sems = refs[2 * na], refs[2 * na + 1]
        for cp in _split_copies(ins_src, ins_land, s_sems, r_sems, scatter, False):
            cp.wait_send()
        for cp in _split_copies(ins_src, ins_land, s_sems, r_sems, scatter, True):
            cp.wait_recv()

    thru = [pltpu.HBM(a.shape, a.dtype) for a in list(srcs) + list(lands)]
    res = pl.pallas_call(
        body, name=name, out_shape=tuple(thru),
        in_specs=[_HBM_SPEC] * (2 * na) + [_SEM_SPEC, _SEM_SPEC, pl.BlockSpec(memory_space=pl.ANY)],
        out_specs=tuple([_HBM_SPEC] * (2 * na)),
        input_output_aliases={i: i for i in range(2 * na)},
        compiler_params=pltpu.CompilerParams(has_side_effects=_DATAFLOW),
    )(*srcs, *lands, send_sems, recv_sems, after)
    return res[na:]


def _landing(block, me):
    buf = lax.empty((NDEV,) + block.shape, block.dtype)
    return lax.dynamic_update_slice_in_dim(buf, block[None], me, axis=0)


def _seg_kw(nseg, n_ctx, tm):
    return dict(nseg=nseg, seg_blocks=(n_ctx // tm if nseg == 2 else 0))


def _ffn_fwd(tag, h, gpre, gpost, shift, scale, gate, w, *, nseg, n_ctx, tm):
    n = h.shape[0]
    kw = _seg_kw(nseg, n_ctx, tm)
    (u,) = _rowwise(tag + "_pre", _pre_fwd_fn, n, [h], [("full", gpre), ("seg", shift), ("seg", scale)],
                    [(D_MODEL, BF16)], tm=tm, **kw)
    s, a, b = _mm_glu(u, w["wa"], w["wb"], name=tag + "_glu")
    if "late" in w:
        w.update(w.pop("late")(s))
    y = _mm(s, w["wout"], out_dtype=F32, name=tag + "_out", tn=512, tk=FFN_DIM)
    (ho,) = _rowwise(tag + "_post", functools.partial(_post_fwd_fn, 0.5), n, [h, y], [("full", gpost), ("seg", gate)],
                     [(D_MODEL, F32)], tm=tm, **kw)
    return ho, dict(h=h, u=u, s=s, a=a, b=b, y=y)


def _ffn_bwd(tag, dho, sv, gpre, gpost, scale, gate, w, put, *, nseg, n_ctx, tm):
    n = dho.shape[0]
    kw = _seg_kw(nseg, n_ctx, tm)
    dy, dgate, dgpost = _rowwise(tag + "_postb", functools.partial(_post_bwd_fn, 0.5), n, [dho, sv["y"]],
                                 [("full", gpost), ("seg", gate)], [(D_MODEL, BF16)], [D_MODEL, D_MODEL], tm=tm, **kw)
    tok = put("w_out", _mm_tn(sv["s"], dy, name=tag + "_dwout", tm=1408, tn=1024))
    ds = _mm(dy, w["wout_t"], out_dtype=F32, name=tag + "_ds", tn=704)
    (dp,) = _rowwise(tag + "_glub", _glu_bwd_fn, n, [ds, sv["a"], sv["b"]], [], [(2 * FFN_DIM, BF16)], tm=min(tm, 128))
    dwin = _mm_tn(sv["u"], dp, name=tag + "_dwin", tm=1024, tn=512)
    du = _mm(dp, w["win_t"], out_dtype=F32, name=tag + "_du", tn=1024, tk=512)
    if tok is not None:
        gpre = gpre + tok
    dh, dshift, dscale, dgpre = _rowwise(tag + "_preb", _pre_bwd_fn, n, [du, sv["h"], dho],
                                         [("full", gpre), ("seg", scale)], [(D_MODEL, F32)],
                                         [D_MODEL, D_MODEL, D_MODEL], tm=tm, **kw)
    return dh, put("w_in", dwin), dict(shift=dshift, scale=dscale, gate=dgate, gpre=dgpre, gpost=dgpost)


def _local_step(x, ctx, target, mods, norm_g, get_w, small, put_grad):
    t_len, n_ctx = x.shape[0], ctx.shape[0]
    n0 = t_len + n_ctx
    tm0 = _pick(n_ctx, 256, 8)
    tm1 = _pick(t_len, 256, 8)
    ncc = n_ctx // CHUNK
    g = {}

    def modrow(i, k, nseg):
        mc, mx = mods[i]
        if nseg == 2:
            return jnp.stack([mc[k], mx[k]])[:, None, :]
        return mx[k][None, None, :]

    pending = [None]

    def gvec(i, k):
        v = norm_g[i, k][None, :]
        if pending[0] is not None:
            v = v + pending[0]
            pending[0] = None
        return v

    xc = jnp.concatenate([ctx, x], axis=0)
    L0 = dict(nseg=2, n_ctx=n_ctx, tm=tm0)
    wts = dict(get_w("ffn00", xc))
    h1, sv_f01 = _ffn_fwd("l0f1", xc, gvec(0, 0), gvec(0, 1), modrow(0, 0, 2), modrow(0, 1, 2), modrow(0, 2, 2),
                          wts["ffn00"], **L0)
    kw0 = _seg_kw(2, n_ctx, tm0)
    (um0,) = _rowwise("l0m_pre", _pre_fwd_fn, n0, [h1], [("full", gvec(0, 2)), ("seg", modrow(0, 3, 2)),
                                                         ("seg", modrow(0, 4, 2))], [(D_MODEL, BF16)], tm=tm0, **kw0)
    wts.update(get_w("ssd", um0))
    z = _mm(um0, wts["ssd_wz"], out_dtype=F32, name="ssd_z", tm=544)
    xbc_pre = _mm(um0, wts["ssd_wxbc"], out_dtype=F32, name="ssd_xbc", tm=544)
    dtr = _mm(um0, wts["ssd_wdt"], out_dtype=F32, name="ssd_dt", tm=544)
    cpre, xbc = _conv_fwd(xbc_pre, small["conv_w8"], small["conv_b"], n_ctx=n_ctx, name="ssd_conv")
    nh = SSD_HEADS
    dt_dir = [dtr[:, :nh], dtr[:, nh:2 * nh]]
    dtT_dir = [d.T for d in dt_dir]
    bias_r = [small["dt_bias"][d][None, :] for d in range(2)]
    bias_c = [small["dt_bias"][d][:, None] for d in range(2)]
    alog_r = [small["a_log"][d][None, :] for d in range(2)]
    alog_c = [small["a_log"][d][:, None] for d in range(2)]
    ys, hss = [], []
    for d in range(2):
        yd, hsd = _ssd_scan_fwd(xbc, dt_dir[d], dtT_dir[d], bias_r[d], bias_c[d], alog_r[d], alog_c[d],
                                rev=(d == 1), n_ctx_chunks=ncc, name=f"ssd_scan{d}")
        ys.append(yd)
        hss.append(hsd)
    dvec = jnp.repeat(small["ssd_d"], SSD_HEAD_DIM)[None, :]
    ngv = small["ssd_norm_g"][None, :]
    gate_rows = [ys[0], ys[1], (xbc, SSD_INNER, 0, 0), z]
    (yn_all,) = _rowwise("ssd_gate", _ssdgate_fwd_fn, n0, gate_rows, [("full", dvec), ("full", ngv)],
                         [(SSD_INNER, BF16)], tm=128)
    yn = yn_all[n_ctx:]
    yo0 = _mm(yn, wts["ssd_wout"], out_dtype=F32, name="ssd_out", tn=1024, tk=1024)
    h1x = h1[n_ctx:]
    L1 = dict(nseg=1, n_ctx=0, tm=tm1)
    (h2,) = _rowwise("l0m_post", functools.partial(_post_fwd_fn, 1.0), t_len, [h1x, yo0],
                     [("full", gvec(0, 3)), ("seg", modrow(0, 5, 1))], [(D_MODEL, F32)], tm=tm1)
    wts.update(get_w("ffn01", h2))
    h3, sv_f02 = _ffn_fwd("l0f2", h2, gvec(0, 4), gvec(0, 5), modrow(0, 6, 1), modrow(0, 7, 1), modrow(0, 8, 1),
                          wts["ffn01"], **L1)

    wts.update(get_w("ffn10", h3))
    h4, sv_f11 = _ffn_fwd("l1f1", h3, gvec(1, 0), gvec(1, 1), modrow(1, 0, 1), modrow(1, 1, 1), modrow(1, 2, 1),
                          wts["ffn10"], **L1)
    (um1,) = _rowwise("l1m_pre", _pre_fwd_fn, t_len, [h4], [("full", gvec(1, 2)), ("seg", modrow(1, 3, 1)),
                                                            ("seg", modrow(1, 4, 1))], [(D_MODEL, BF16)], tm=tm1)
    wts.update(get_w("gm", um1))
    p1 = _mm(um1, wts["gm_win"], out_dtype=F32, name="gm_in")
    vg = small["gm_v_g"][None, :]
    vb = small["gm_v_b"][None, :]
    gu, gvn = _rowwise("gm_act", _gm_act_fwd_fn, t_len, [p1], [("full", vg), ("full", vb)],
                       [(GM_INNER, F32), (GM_INNER, BF16)], tm=128)
    ws_bf = small["gm_w_s"].astype(BF16)
    wst_bf = jnp.swapaxes(small["gm_w_s"], 1, 2).astype(BF16)
    bst = small["gm_b_s"].T
    tgm = _gm_spatial_fwd(gu, gvn, ws_bf, bst, name="gm_spatial")
    yo1 = _mm(tgm, wts["gm_wout"], out_dtype=F32, name="gm_out", tn=1024, tk=1024)
    (h5,) = _rowwise("l1m_post", functools.partial(_post_fwd_fn, 1.0), t_len, [h4, yo1],
                     [("full", gvec(1, 3)), ("seg", modrow(1, 5, 1))], [(D_MODEL, F32)], tm=tm1)
    wts.update(get_w("ffn11", h5))
    h6, sv_f12 = _ffn_fwd("l1f2", h5, gvec(1, 4), gvec(1, 5), modrow(1, 6, 1), modrow(1, 7, 1), modrow(1, 8, 1),
                          wts["ffn11"], **L1)

    dh, loss_parts = _rowwise("loss", _loss_fn, t_len, [h6, target], [], [(D_MODEL, F32)], [D_MODEL], tm=tm1)

    zero = jnp.zeros((D_MODEL,), F32)
    dmx = [[zero] * N_MOD for _ in range(2)]
    dmc = [[zero] * N_MOD for _ in range(2)]
    dng = [[zero] * 6 for _ in range(2)]

    def put_mod(i, k, acc):
        if acc.shape[0] == 2:
            dmc[i][k] = dmc[i][k] + acc[0, 0]
            dmx[i][k] = dmx[i][k] + acc[1, 0]
        else:
            dmx[i][k] = dmx[i][k] + acc[0, 0]

    def put_g(i, k, acc):
        dng[i][k] = dng[i][k] + jnp.sum(acc[:, 0], axis=0)

    def ffn_back(tag, i, j, dho, sv, w, lay):
        nseg = lay["nseg"]
        base = 0 if j == 0 else 6
        gi = 0 if j == 0 else 4
        dh_in, pending[0], s = _ffn_bwd(tag, dho, sv, gvec(i, gi), gvec(i, gi + 1), modrow(i, base + 1, nseg),
                                        modrow(i, base + 2, nseg), w, functools.partial(put_grad, f"ffn{i}{j}"), **lay)
        put_mod(i, base, s["shift"])
        put_mod(i, base + 1, s["scale"])
        put_mod(i, base + 2, s["gate"])
        put_g(i, gi, s["gpre"])
        put_g(i, gi + 1, s["gpost"])
        return dh_in

    dh = ffn_back("l1f2", 1, 1, dh, sv_f12, wts["ffn11"], L1)
    dyo, dgate, dgp = _rowwise("l1m_postb", functools.partial(_post_bwd_fn, 1.0), t_len, [dh, yo1],
                               [("full", gvec(1, 3)), ("seg", modrow(1, 5, 1))], [(D_MODEL, BF16)],
                               [D_MODEL, D_MODEL], tm=tm1)
    put_mod(1, 5, dgate)
    put_g(1, 3, dgp)
    put_grad("gm", "w_out", _mm_tn(tgm, dyo, name="gm_dwout", tn=1024))
    dtg = _mm(dyo, wts["gm_wout_t"], out_dtype=F32, name="gm_dt")
    dgu, dgvn, dws, dbst = _gm_spatial_bwd(dtg, gu, gvn, ws_bf, wst_bf, bst, name="gm_spatialb")
    g["gm_w_s"] = dws
    g["gm_b_s"] = dbst.T
    dp1, dvg, dvb = _rowwise("gm_actb", _gm_act_bwd_fn, t_len, [p1, dgu, dgvn], [("full", vg)],
                             [(2 * GM_INNER, BF16)], [GM_INNER, GM_INNER], tm=128)
    g["gm_v_g"] = dvg[0, 0]
    g["gm_v_b"] = dvb[0, 0]
    pending[0] = put_grad("gm", "w_in", _mm_tn(um1, dp1, name="gm_dwin", tm=1024))
    dum1 = _mm(dp1, wts["gm_win_t"], out_dtype=F32, name="gm_dum", tn=1024, tk=512)
    dh, dsh, dsc, dgp = _rowwise("l1m_preb", _pre_bwd_fn, t_len, [dum1, h4, dh],
                                 [("full", gvec(1, 2)), ("seg", modrow(1, 4, 1))], [(D_MODEL, F32)],
                                 [D_MODEL, D_MODEL, D_MODEL], tm=tm1)
    put_mod(1, 3, dsh)
    put_mod(1, 4, dsc)
    put_g(1, 2, dgp)
    dh = ffn_back("l1f1", 1, 0, dh, sv_f11, wts["ffn10"], L1)

    dh = ffn_back("l0f2", 0, 1, dh, sv_f02, wts["ffn01"], L1)
    dyo, dgate, dgp = _rowwise("l0m_postb", functools.partial(_post_bwd_fn, 1.0), t_len, [dh, yo0],
                               [("full", gvec(0, 3)), ("seg", modrow(0, 5, 1))], [(D_MODEL, BF16)],
                               [D_MODEL, D_MODEL], tm=tm1)
    put_mod(0, 5, dgate)
    put_g(0, 3, dgp)
    put_grad("ssd", "w_out", _mm_tn(yn, dyo, name="ssd_dwout", tn=1024))
    dyn = _mm(dyo, wts["ssd_wout_t"], out_dtype=F32, name="ssd_dyn")
    dyn_all = jnp.concatenate([jnp.zeros((n_ctx, SSD_INNER), F32), dyn], axis=0)
    dy_ssd, dz, dngv, ddv = _rowwise("ssd_gateb", _ssdgate_bwd_fn, n0, [dyn_all] + gate_rows,
                                     [("full", dvec), ("full", ngv)], [(SSD_INNER, F32), (SSD_INNER, BF16)],
                                     [SSD_INNER, SSD_INNER], tm=128)
    g["ssd_norm_g"] = dngv[0, 0]
    g["ssd_D"] = jnp.sum(ddv[0, 0].reshape(SSD_HEADS, SSD_HEAD_DIM), axis=1)
    dxbcs, ddts, dalogs, dbiases = [], [], [], []
    for d in range(2):
        dxd, ddtd, dal, dbi = _ssd_scan_bwd(dy_ssd, xbc, hss[d], dt_dir[d], dtT_dir[d], bias_r[d], bias_c[d],
                                            alog_r[d], alog_c[d], dvec, rev=(d == 1), n_ctx_chunks=ncc,
                                            direct=(d == 0), name=f"ssd_scanb{d}")
        dxbcs.append(dxd)
        ddts.append(ddtd)
        dalogs.append(dal[0])
        dbiases.append(dbi[0])
    g["ssd_A_log"] = jnp.stack(dalogs)
    g["ssd_dt_bias"] = jnp.stack(dbiases)
    dxbc_pre, dcw8, dcb = _conv_bwd(dxbcs[0], dxbcs[1], cpre, xbc_pre, small["conv_w8"], n_ctx=n_ctx, name="ssd_convb")
    g["ssd_conv_w"] = dcw8[:SSD_CONV]
    g["ssd_conv_b"] = dcb[0]
    ddt_bf = jnp.concatenate([ddts[0], ddts[1], jnp.zeros((n0, 128 - 2 * nh), F32)], axis=1).astype(BF16)
    dw_ssd_in = jnp.concatenate([
        _mm_tn(um0, dz, name="ssd_dwz", tm=1024),
        _mm_tn(um0, dxbc_pre, name="ssd_dwxbc", tm=1024),
        _mm_tn(um0, ddt_bf, name="ssd_dwdt", tm=1024)[:, :2 * nh]], axis=1)
    pending[0] = put_grad("ssd", "w_in", dw_ssd_in)
    dum0 = _mm(dz, wts["ssd_wz_t"], out_dtype=F32, name="ssd_dum_z", tm=544, tn=1024, tk=512)
    dum0 = _mm(dxbc_pre, wts["ssd_wxbc_t"], out_dtype=F32, name="ssd_dum_x", tm=544, tn=1024, tk=512, add=dum0)
    dum0 = _mm(ddt_bf, wts["ssd_wdt_t"], out_dtype=F32, name="ssd_dum_dt", tm=544, tn=1024, add=dum0)
    dres = jnp.concatenate([jnp.zeros((n_ctx, D_MODEL), F32), dh], axis=0)
    dh0, dsh, dsc, dgp = _rowwise("l0m_preb", _pre_bwd_fn, n0, [dum0, h1, dres],
                                  [("full", gvec(0, 2)), ("seg", modrow(0, 4, 2))], [(D_MODEL, F32)],
                                  [D_MODEL, D_MODEL, D_MODEL], tm=tm0, **kw0)
    put_mod(0, 3, dsh)
    put_mod(0, 4, dsc)
    put_g(0, 2, dgp)
    dh0 = ffn_back("l0f1", 0, 0, dh0, sv_f01, wts["ffn00"], L0)
    grad_x = dh0[n_ctx:]
    g["norm_g"] = jnp.stack([jnp.stack(r) for r in dng])
    g["dmx"] = jnp.stack([jnp.concatenate(r) for r in dmx])
    g["dmc"] = jnp.stack([jnp.concatenate(r) for r in dmc])
    return loss_parts[0], grad_x, g


GROUPS = ("ffn00", "ssd", "ffn01", "ffn10", "gm", "ffn11")


def _mats_in(group, win_l):
    k, nloc = win_l.shape[1], win_l.shape[2]
    win = jnp.transpose(win_l, (1, 0, 2)).reshape(k, NDEV * nloc)
    win_t = jnp.transpose(win_l, (0, 2, 1)).reshape(NDEV * nloc, k)
    if group.startswith("ffn"):
        return dict(wa=win[:, :FFN_DIM], wb=win[:, FFN_DIM:], win_t=win_t)
    if group == "gm":
        return dict(gm_win=win, gm_win_t=win_t)
    assert group == "ssd"
    c0, c1 = SSD_INNER, SSD_INNER + SSD_CONV_DIM
    padc = 128 - 2 * SSD_HEADS
    return dict(ssd_wz=win[:, :c0], ssd_wxbc=win[:, c0:c1], ssd_wdt=jnp.pad(win[:, c1:], ((0, 0), (0, padc))),
                ssd_wz_t=win_t[:c0], ssd_wxbc_t=win_t[c0:c1], ssd_wdt_t=jnp.pad(win_t[c1:], ((0, padc), (0, 0))))


def _mats_out(group, wout_l):
    wout = wout_l.reshape(-1, wout_l.shape[2])
    pre = "" if group.startswith("ffn") else group + "_"
    return {pre + "wout": wout, pre + "wout_t": wout.T}


def _group_mats(group, lands):
    m = {**_mats_in(group, lands[0]), **_mats_out(group, lands[1])}
    return {group: m} if group.startswith("ffn") else m


def _grad_blocks(which, grad):
    if which == "w_in":
        k, n = grad.shape
        return jnp.transpose(grad.reshape(k, NDEV, n // NDEV), (1, 0, 2)).astype(BF16)
    return grad.reshape(NDEV, grad.shape[0] // NDEV, grad.shape[1]).astype(BF16)


def kernel(x, c, ctx, c_ctx, ada_w, ada_b, norm_g, ffn_w_in, ffn_w_out, ssd_w_in, ssd_conv_w, ssd_conv_b, ssd_dt_bias, ssd_A_log, ssd_D, ssd_norm_g, ssd_w_out, gm_w_in, gm_v_g, gm_v_b, gm_w_s, gm_b_s, gm_w_out, loss_target, m_c_ctx, m_ada_w, m_ada_b, m_norm_g, m_ffn_w_in, m_ffn_w_out, m_ssd_w_in, m_ssd_conv_w, m_ssd_conv_b, m_ssd_dt_bias, m_ssd_A_log, m_ssd_D, m_ssd_norm_g, m_ssd_w_out, m_gm_w_in, m_gm_v_g, m_gm_v_b, m_gm_w_s, m_gm_b_s, m_gm_w_out, v_c_ctx, v_ada_w, v_ada_b, v_norm_g, v_ffn_w_in, v_ffn_w_out, v_ssd_w_in, v_ssd_conv_w, v_ssd_conv_b, v_ssd_dt_bias, v_ssd_A_log, v_ssd_D, v_ssd_norm_g, v_ssd_w_out, v_gm_w_in, v_gm_v_g, v_gm_v_b, v_gm_w_s, v_gm_b_s, v_gm_w_out):
    me = 4 * lax.axis_index("x") + 2 * lax.axis_index("y") + lax.axis_index("c")
    d = D_MODEL
    ncol = N_MOD * d // NDEV

    small_pack = jnp.concatenate([c.reshape(-1), norm_g.reshape(-1), ssd_conv_w.reshape(-1),
                                  gm_v_g.reshape(-1), gm_v_b.reshape(-1)])[None, :]
    (sp,), _ = _exchange([small_pack], scatter=False, name="gather_small")
    sp = sp[:, 0]
    o = 0
    c_all = sp[:, o:o + d]; o += d
    ng_all = sp[:, o:o + 2 * 6 * 128].reshape(NDEV, 2, 6, 128); o += 2 * 6 * 128
    cw_all = sp[:, o:o + SSD_CONV * 512].reshape(NDEV, SSD_CONV, 512); o += SSD_CONV * 512
    vg_all = sp[:, o:o + 256]; o += 256
    vb_all = sp[:, o:o + 256]; o += 256
    norm_g_full = jnp.transpose(ng_all, (1, 2, 0, 3)).reshape(2, 6, d)
    conv_w_full = jnp.transpose(cw_all, (1, 0, 2)).reshape(SSD_CONV, SSD_CONV_DIM)
    gm_v_g_full = vg_all.reshape(-1)
    gm_v_b_full = vb_all.reshape(-1)

    c16 = jnp.concatenate([c_all, jnp.broadcast_to(c_ctx[None, :], (NDEV, d))], axis=0)
    ada_b_loc = lax.dynamic_slice_in_dim(ada_b, me * ncol, ncol, axis=1)
    mods_loc = jnp.stack([_mm_f32(c16, ada_w[i], name=f"ada_mod{i}", silu_a=True, bias=ada_b_loc[i][None, :])
                          for i in range(2)])
    (mods_all,), mods_done = _exchange([mods_loc], scatter=False, name="gather_mods")

    shard = {"ssd": (ssd_w_in[0], ssd_w_out[0]), "gm": (gm_w_in[0], gm_w_out[0])}
    moment = {"ssd": ((m_ssd_w_in[0], v_ssd_w_in[0]), (m_ssd_w_out[0], v_ssd_w_out[0])),
              "gm": ((m_gm_w_in[0], v_gm_w_in[0]), (m_gm_w_out[0], v_gm_w_out[0]))}
    for i in range(2):
        for j in range(2):
            shard[f"ffn{i}{j}"] = (ffn_w_in[i, j], ffn_w_out[i, j])
            moment[f"ffn{i}{j}"] = ((m_ffn_w_in[i, j], v_ffn_w_in[i, j]), (m_ffn_w_out[i, j], v_ffn_w_out[i, j]))
    first = GROUPS[0]
    units = [(first + "_in", first, (0,)), (first + "_out", first, (1,))] + [(grp, grp, (0, 1)) for grp in GROUPS[1:]]
    gathers = {}
    for unit, grp, idx in units:
        srcs = [(shard[grp][k] + mods_done).astype(BF16) for k in idx]
        st = _exchange_start(srcs, [_landing(s, me) for s in srcs], scatter=False, name="gather_start_" + unit)
        gathers[unit] = st[:4]

    def fetch(unit, after):
        return _exchange_wait(*gathers[unit], after, scatter=False, name="gather_wait_" + unit)

    def get_w(grp, after):
        if grp != first:
            return _group_mats(grp, fetch(grp, after))
        late = lambda later: _mats_out(grp, fetch(grp + "_out", later)[0])
        return {grp: dict(_mats_in(grp, fetch(grp + "_in", after)[0]), late=late)}

    scatters = {}
    held = {}

    def put_grad(grp, which, grad):
        if grp == first:
            unit, blocks = grp + "_" + which[2:], [_grad_blocks(which, grad)]
        else:
            held[grp, which] = _grad_blocks(which, grad)
            if (grp, "w_in") not in held or (grp, "w_out") not in held:
                return None
            unit, blocks = grp, [held[grp, "w_in"], held[grp, "w_out"]]
        lands = [_landing(lax.dynamic_index_in_dim(b, me, axis=0, keepdims=False), me) for b in blocks]
        st = _exchange_start(blocks, lands, scatter=True, name="scatter_start_" + unit)
        scatters[unit] = st[:4]
        return st[4]

    mods_rows = jnp.transpose(mods_all, (1, 2, 0, 3)).reshape(2, 2 * NDEV, N_MOD * d)
    mx = lax.dynamic_index_in_dim(mods_rows, me, axis=1, keepdims=False).reshape(2, N_MOD, d)
    mc = mods_rows[:, NDEV].reshape(2, N_MOD, d)
    mods = [(mc[i], mx[i]) for i in range(2)]

    small = dict(conv_w8=jnp.pad(conv_w_full, ((0, 8 - SSD_CONV), (0, 0))), conv_b=ssd_conv_b, dt_bias=ssd_dt_bias[0],
                 a_log=ssd_A_log[0], ssd_d=ssd_D[0], ssd_norm_g=ssd_norm_g[0], gm_v_g=gm_v_g_full,
                 gm_v_b=gm_v_b_full, gm_w_s=gm_w_s[0], gm_b_s=gm_b_s[0])
    loss_parts, grad_x, g = _local_step(x[0], ctx[0], loss_target[0], mods, norm_g_full, get_w, small, put_grad)
    loss = lax.psum(0.5 / d * jnp.sum(loss_parts), ("x", "y", "c"))

    upd = {}
    after = grad_x
    for unit, grp, idx in reversed(units):
        parts = _exchange_wait(*scatters[unit], after, scatter=True, name="scatter_wait_" + unit)
        for k, p in zip(idx, parts):
            m_, v_ = moment[grp][k]
            which = ("in", "out")[k]
            upd[grp, which] = _adamw(p, shard[grp][k], m_, v_, name=f"adamw_{grp}_{which}")
            after = upd[grp, which][0]
    res = {}
    for which in ("in", "out"):
        res["ffn_w_" + which] = [jnp.stack([jnp.stack([upd[f"ffn{i}{j}", which][k] for j in range(2)])
                                            for i in range(2)]) for k in range(4)]
        res["ssd_w_" + which] = [upd["ssd", which][k][None] for k in range(4)]
        res["gm_w_" + which] = [upd["gm", which][k][None] for k in range(4)]

    sg_names = ["dmx", "dmc", "norm_g", "ssd_conv_w", "ssd_conv_b", "ssd_dt_bias", "ssd_A_log", "ssd_D", "ssd_norm_g",
                "gm_v_g", "gm_v_b", "gm_w_s", "gm_b_s"]
    sg_shapes = [g[n].shape for n in sg_names]
    flat = jnp.concatenate([g[n].reshape(-1) for n in sg_names])
    npack = flat.shape[0]
    pad = (-npack) % 1024
    flat = jnp.pad(flat, (0, pad)).reshape(-1, 128)
    (sg_all,), _ = _exchange([flat], scatter=False, name="gather_small_grads")
    sg_sum = _sum_slots(sg_all, name="sum_small_grads").reshape(-1)[:npack]
    sums = {}
    o = 0
    for n, shp in zip(sg_names, sg_shapes):
        sz = math.prod(shp)
        sums[n] = sg_sum[o:o + sz].reshape(shp)
        o += sz
    per_dev = sg_all.reshape(NDEV, -1)
    dmx_all = per_dev[:, :2 * N_MOD * d].reshape(NDEV, 2, N_MOD * d)
    dmc_all = per_dev[:, 2 * N_MOD * d:4 * N_MOD * d].reshape(NDEV, 2, N_MOD * d)

    (s16,) = _rowwise("ada_silu", lambda cc: ((_silu(cc),), ()), 2 * NDEV, [c16], [], [(d, F32)], tm=2 * NDEV)
    s16_t = s16.T
    g_ada_w, dcc_parts = [], []
    for i in range(2):
        rhs = jnp.concatenate([lax.dynamic_slice_in_dim(dmx_all[:, i], me * ncol, ncol, axis=1),
                               lax.dynamic_slice_in_dim(dmc_all[:, i], me * ncol, ncol, axis=1)], axis=0)
        g_ada_w.append(_mm_f32(s16_t, rhs, name=f"ada_dw{i}"))
        dmc_loc = lax.dynamic_slice_in_dim(sums["dmc"][i], me * ncol, ncol, axis=0)
        rhs_c = jnp.zeros((ncol, 128), F32).at[:, 0].set(dmc_loc)
        dcc_parts.append(_mm_f32(ada_w[i], rhs_c, name=f"ada_dcc{i}")[:, 0])
    g_ada_w = jnp.stack(g_ada_w)
    dcc_part = (dcc_parts[0] + dcc_parts[1]).reshape(8, 128)
    (dcc_all,), _ = _exchange([dcc_part], scatter=False, name="gather_dcc")
    g_c_ctx = _sum_slots(dcc_all, name="sum_dcc", scale_by=c_ctx.reshape(8, 128)).reshape(d)
    g_ada_b = sums["dmx"] + sums["dmc"]

    outs = _adamw(g_ada_w.reshape(1, -1, ncol), ada_w.reshape(-1, ncol), m_ada_w.reshape(-1, ncol),
                  v_ada_w.reshape(-1, ncol), name="adamw_ada_w")
    res["ada_w"] = [o_.reshape(ada_w.shape) for o_ in outs]

    loc = lambda a, ax, n: lax.dynamic_slice_in_dim(a, me * n, n, axis=ax)
    small_g = dict(c_ctx=g_c_ctx, ada_b=g_ada_b, norm_g=loc(sums["norm_g"], 2, 128),
                   ssd_conv_w=loc(sums["ssd_conv_w"], 1, 512)[None], ssd_conv_b=sums["ssd_conv_b"][None],
                   ssd_dt_bias=sums["ssd_dt_bias"][None], ssd_A_log=sums["ssd_A_log"][None], ssd_D=sums["ssd_D"][None],
                   ssd_norm_g=sums["ssd_norm_g"][None], gm_v_g=loc(sums["gm_v_g"], 0, 256)[None],
                   gm_v_b=loc(sums["gm_v_b"], 0, 256)[None], gm_w_s=sums["gm_w_s"][None], gm_b_s=sums["gm_b_s"][None])
    small_w = dict(c_ctx=(c_ctx, m_c_ctx, v_c_ctx), ada_b=(ada_b, m_ada_b, v_ada_b), norm_g=(norm_g, m_norm_g, v_norm_g),
                   ssd_conv_w=(ssd_conv_w, m_ssd_conv_w, v_ssd_conv_w), ssd_conv_b=(ssd_conv_b, m_ssd_conv_b, v_ssd_conv_b),
                   ssd_dt_bias=(ssd_dt_bias, m_ssd_dt_bias, v_ssd_dt_bias), ssd_A_log=(ssd_A_log, m_ssd_A_log, v_ssd_A_log),
                   ssd_D=(ssd_D, m_ssd_D, v_ssd_D), ssd_norm_g=(ssd_norm_g, m_ssd_norm_g, v_ssd_norm_g),
                   gm_v_g=(gm_v_g, m_gm_v_g, v_gm_v_g), gm_v_b=(gm_v_b, m_gm_v_b, v_gm_v_b),
                   gm_w_s=(gm_w_s, m_gm_w_s, v_gm_w_s), gm_b_s=(gm_b_s, m_gm_b_s, v_gm_b_s))
    sn = list(small_w)

    def pack(arrs):
        f = jnp.concatenate([a.reshape(-1) for a in arrs])
        return jnp.pad(f, (0, (-f.shape[0]) % 1024)).reshape(-1, 128)

    pg = pack([small_g[n].reshape(small_w[n][0].shape) for n in sn])
    outs = _adamw(pg[None], pack([small_w[n][0] for n in sn]), pack([small_w[n][1] for n in sn]),
                  pack([small_w[n][2] for n in sn]), name="adamw_small")
    flat_outs = [o_.reshape(-1) for o_ in outs]
    o = 0
    for n in sn:
        shp = small_w[n][0].shape
        sz = math.prod(shp)
        res[n] = [fo[o:o + sz].reshape(shp) for fo in flat_outs]
        o += sz

    order = ["c_ctx", "ada_w", "ada_b", "norm_g", "ffn_w_in", "ffn_w_out", "ssd_w_in", "ssd_conv_w", "ssd_conv_b",
             "ssd_dt_bias", "ssd_A_log", "ssd_D", "ssd_norm_g", "ssd_w_out", "gm_w_in", "gm_v_g", "gm_v_b", "gm_w_s",
             "gm_b_s", "gm_w_out"]
    result = [loss, grad_x[None]]
    for k in range(4):
        result += [res[n][k] for n in order]
    return tuple(result)
```

```python
import functools
import math

import jax
import jax.numpy as jnp
from jax import lax
from jax.experimental import pallas as pl
from jax.experimental.pallas import tpu as pltpu

F32 = jnp.float32
BF16 = jnp.bfloat16

NDEV = 8
D_MODEL = 1024
FFN_DIM = 2816
N_MOD = 9
EPS = 1e-6
SSD_INNER = 2048
SSD_HEADS = 32
SSD_HEAD_DIM = 64
SSD_GROUPS = 8
SSD_HPG = 4
SSD_STATE = 128
SSD_CONV = 5
SSD_CONV_DIM = 4096
CHUNK = 128
GM_INNER = 2048
GM_GROUPS = 8
GM_GROUP_DIM = 256
ADAM_LR = 0.001
ADAM_B1 = 0.9
ADAM_B2 = 0.999
ADAM_EPS = 1e-08
ADAM_WD = 0.01
ADAM_STEP = 10
NEG_BIG = -1e30
VMEM_LIMIT_BYTES = 56 * 1024 * 1024
HI = lax.Precision.HIGHEST


def _params(*sem):
    return pltpu.CompilerParams(dimension_semantics=sem, vmem_limit_bytes=VMEM_LIMIT_BYTES)


def _pick(n, target, mult=16):
    if n <= target:
        return n
    for t in range(target - target % mult, 0, -mult):
        if n % t == 0:
            return t
    raise ValueError((n, target, mult))


def _sig(x):
    return 0.5 * jnp.tanh(0.5 * x) + 0.5


def _silu(x):
    return x * _sig(x)


def _dsilu(x):
    s = _sig(x)
    return s * (1.0 + x * (1.0 - s))


_GELU_C = math.sqrt(2.0 / math.pi)


def _gelu(x):
    return 0.5 * x * (1.0 + jnp.tanh(_GELU_C * (x + 0.044715 * x * x * x)))


def _dgelu(x):
    t = jnp.tanh(_GELU_C * (x + 0.044715 * x * x * x))
    return 0.5 * (1.0 + t) + 0.5 * x * (1.0 - t * t) * _GELU_C * (1.0 + 3.0 * 0.044715 * x * x)


def _softplus(x):
    return jnp.maximum(x, 0.0) + jnp.log1p(jnp.exp(-jnp.abs(x)))


def _sum0(v):
    return jnp.sum(v, axis=0, keepdims=True)


def _rms(h):
    r = lax.rsqrt(jnp.mean(h * h, axis=-1, keepdims=True) + EPS)
    return h * r, r


def _dot(a, b, dims=((1,), (0,)), precision=None):
    return lax.dot_general(a, b, (dims, ((), ())), preferred_element_type=F32, precision=precision)


_NT = ((1,), (1,))
_TN = ((0,), (0,))


def _rowwise(name, fn, n_rows, rows, consts, outs, accs=(), *, tm, nseg=1, seg_blocks=0):
    assert n_rows % tm == 0
    if nseg == 2:
        assert seg_blocks > 0
        seg = lambda i: jnp.where(i < seg_blocks, 0, 1)
    else:
        seg = lambda i: 0
    in_specs, args = [], []
    for r in rows:
        arr, width, cb, off = r if isinstance(r, tuple) else (r, r.shape[1], 0, 0)
        in_specs.append(pl.BlockSpec((tm, width), lambda i, cb=cb, off=off: (i + off, cb)))
        args.append(arr)
    for kind, arr in consts:
        if kind == "seg":
            assert arr.shape[0] == nseg and arr.shape[1] == 1, arr.shape
            in_specs.append(pl.BlockSpec((None, 1, arr.shape[2]), lambda i: (seg(i), 0, 0)))
        else:
            in_specs.append(pl.BlockSpec(arr.shape, lambda i: (0, 0)))
        args.append(arr)
    out_shape = [jax.ShapeDtypeStruct((n_rows, w), dt) for w, dt in outs]
    out_specs = [pl.BlockSpec((tm, w), lambda i: (i, 0)) for w, _ in outs]
    out_shape += [jax.ShapeDtypeStruct((nseg, 1, w), F32) for w in accs]
    out_specs += [pl.BlockSpec((None, 1, w), lambda i: (seg(i), 0, 0)) for w in accs]
    n_in, n_out, n_acc = len(args), len(outs), len(accs)

    def kern(*refs):
        ins = [r[...] for r in refs[:n_in]]
        res, sums = fn(*ins)
        for ref, v in zip(refs[n_in:n_in + n_out], res):
            ref[...] = v.astype(ref.dtype)
        if n_acc:
            i = pl.program_id(0)
            first = (i == 0) | (i == seg_blocks) if nseg == 2 else (i == 0)
            acc_refs = refs[n_in + n_out:]

            @pl.when(first)
            def _():
                for ref, v in zip(acc_refs, sums):
                    ref[...] = v

            @pl.when(jnp.logical_not(first))
            def _():
                for ref, v in zip(acc_refs, sums):
                    ref[...] += v

    res = pl.pallas_call(
        kern, name=name, grid=(n_rows // tm,), in_specs=in_specs, out_specs=out_specs, out_shape=out_shape,
        compiler_params=_params("arbitrary"),
    )(*args)
    return res


def _pre_fwd_fn(h, g, shift, scale):
    hh, _ = _rms(h)
    return (hh * g * (1.0 + scale) + shift,), ()


def _pre_bwd_fn(du, h, dres, g, scale):
    hh, r = _rms(h)
    n = hh * g
    dn = du * (1.0 + scale)
    dhh = dn * g
    dh = dres + r * (dhh - hh * jnp.mean(dhh * hh, axis=-1, keepdims=True))
    return (dh,), (_sum0(du), _sum0(du * n), _sum0(dn * hh))


def _post_fwd_fn(weight, h, y, g, gate):
    yh, _ = _rms(y)
    return (h + weight * gate * (yh * g),), ()


def _post_bwd_fn(weight, dh, y, g, gate):
    yh, r = _rms(y)
    dr = dh * weight
    dyh = dr * gate * g
    dy = r * (dyh - yh * jnp.mean(dyh * yh, axis=-1, keepdims=True))
    return (dy,), (_sum0(dr * yh * g), _sum0(dr * gate * yh))


def _glu_bwd_fn(ds, a, b):
    a = a.astype(F32)
    b = b.astype(F32)
    sg = _sig(a)
    da = ds * b * (sg * (1.0 + a * (1.0 - sg)))
    db = ds * (a * sg)
    return (jnp.concatenate([da, db], axis=1),), ()


def _loss_fn(y, t):
    diff = y - t
    return (diff * (1.0 / D_MODEL),), (_sum0(diff * diff),)


def _ssd_y(yf, yb, xs, z, dvec):
    y = yf + yb + dvec * xs
    return y, y * _silu(z)


def _ssdgate_fwd_fn(yf, yb, xs, z, dvec, ng):
    _, yg = _ssd_y(yf, yb, xs, z, dvec)
    parts = []
    for g in range(SSD_GROUPS):
        sl = slice(g * 256, (g + 1) * 256)
        parts.append(_rms(yg[:, sl])[0])
    return (jnp.concatenate(parts, axis=1) * ng,), ()


def _ssdgate_bwd_fn(dyn, yf, yb, xs, z, dvec, ng):
    y, yg = _ssd_y(yf, yb, xs, z, dvec)
    dyg_parts, ygh_parts = [], []
    for g in range(SSD_GROUPS):
        sl = slice(g * 256, (g + 1) * 256)
        ygh, r = _rms(yg[:, sl])
        d = dyn[:, sl] * ng[:, sl]
        dyg_parts.append(r * (d - ygh * jnp.mean(d * ygh, axis=-1, keepdims=True)))
        ygh_parts.append(ygh)
    dyg = jnp.concatenate(dyg_parts, axis=1)
    ygh = jnp.concatenate(ygh_parts, axis=1)
    dy = dyg * _silu(z)
    dz = dyg * y * _dsilu(z)
    return (dy, dz), (_sum0(dyn * ygh), _sum0(dy * xs))


def _ln_stats(v):
    mu = jnp.mean(v, axis=-1, keepdims=True)
    vc = v - mu
    r = lax.rsqrt(jnp.mean(vc * vc, axis=-1, keepdims=True) + EPS)
    return vc * r, r


def _gm_act_fwd_fn(p, vg, vb):
    gu = _gelu(p[:, :GM_INNER])
    gvh, _ = _ln_stats(_gelu(p[:, GM_INNER:]))
    return (gu, gvh * vg + vb), ()


def _gm_act_bwd_fn(p, dgu, dgvn, vg):
    pu = p[:, :GM_INNER]
    pv = p[:, GM_INNER:]
    gvh, r = _ln_stats(_gelu(pv))
    dgvh = dgvn * vg
    dgv = r * (dgvh - jnp.mean(dgvh, axis=-1, keepdims=True) - gvh * jnp.mean(dgvh * gvh, axis=-1, keepdims=True))
    dp = jnp.concatenate([dgu * _dgelu(pu), dgv * _dgelu(pv)], axis=1)
    return (dp,), (_sum0(dgvn * gvh), _sum0(dgvn))


def _mm(a, b, *, out_dtype, name, tm=1088, tn=1024, tk=1408, add=None, rhs_t=False, n=None, b_off=(0, 0)):
    m, k = a.shape
    if n is None:
        n, k2 = b.shape if rhs_t else b.shape[::-1]
        assert k == k2
    tm, tn, tk = _pick(m, tm), _pick(n, tn, 128), _pick(k, tk, 128)
    o0, o1 = b_off
    nk = k // tk
    dims = _NT if rhs_t else ((1,), (0,))

    def kern(*refs):
        a_ref, b_ref = refs[:2]
        add_ref = refs[2] if add is not None else None
        o_ref = refs[3] if add is not None else refs[2]

        def finish(r):
            if add is not None:
                r = r + add_ref[...]
            o_ref[...] = r.astype(o_ref.dtype)

        p = _dot(a_ref[...], b_ref[...], dims)
        if nk == 1:
            finish(p)
            return
        acc_ref = refs[-1]
        kk = pl.program_id(2)

        @pl.when(kk == 0)
        def _():
            acc_ref[...] = p

        @pl.when((kk > 0) & (kk < nk - 1))
        def _():
            acc_ref[...] += p

        @pl.when(kk == nk - 1)
        def _():
            finish(acc_ref[...] + p)

    if rhs_t:
        b_spec = pl.BlockSpec((tn, tk), lambda i, j, kk: (j + o0, kk + o1))
    else:
        b_spec = pl.BlockSpec((tk, tn), lambda i, j, kk: (kk + o0, j + o1))
    in_specs = [pl.BlockSpec((tm, tk), lambda i, j, kk: (i, kk)), b_spec]
    args = [a, b]
    if add is not None:
        in_specs.append(pl.BlockSpec((tm, tn), lambda i, j, kk: (i, j)))
        args.append(add)
    return pl.pallas_call(
        kern, name=name, grid=(m // tm, n // tn, nk), in_specs=in_specs,
        out_specs=pl.BlockSpec((tm, tn), lambda i, j, kk: (i, j)),
        out_shape=jax.ShapeDtypeStruct((m, n), out_dtype),
        scratch_shapes=[pltpu.VMEM((tm, tn), F32)] if nk > 1 else [],
        compiler_params=_params("parallel", "parallel", "arbitrary"),
    )(*args)


def _mm_glu(u, win, *, name, tm=2176, tn=256):
    m, k = u.shape
    n = win.shape[1] // 2
    tm, tn = _pick(m, tm), _pick(n, tn, 128)
    nj = n // tn

    def kern(u_ref, wa_ref, wb_ref, s_ref, a_ref, b_ref):
        uu = u_ref[...]
        a = jnp.dot(uu, wa_ref[...], preferred_element_type=F32)
        b = jnp.dot(uu, wb_ref[...], preferred_element_type=F32)
        s_ref[...] = (_silu(a) * b).astype(BF16)
        a_ref[...] = a.astype(BF16)
        b_ref[...] = b.astype(BF16)

    ospec = pl.BlockSpec((tm, tn), lambda i, j: (i, j))
    return pl.pallas_call(
        kern, name=name, grid=(m // tm, nj),
        in_specs=[pl.BlockSpec((tm, k), lambda i, j: (i, 0)), pl.BlockSpec((k, tn), lambda i, j: (0, j)),
                  pl.BlockSpec((k, tn), lambda i, j: (0, nj + j))],
        out_specs=[ospec, ospec, ospec],
        out_shape=[jax.ShapeDtypeStruct((m, n), BF16)] * 3,
        compiler_params=_params("parallel", "parallel"),
    )(u, win, win)


def _mm_tn(a, b, *, name, tm=1024, tn=1024, tk=1088):
    t, m = a.shape
    t2, n = b.shape
    assert t == t2
    tm, tn, tk = _pick(m, tm, 128), _pick(n, tn, 128), _pick(t, tk)
    nk = t // tk

    def kern(a_ref, b_ref, o_ref):
        kk = pl.program_id(2)

        @pl.when(kk == 0)
        def _():
            o_ref[...] = jnp.zeros_like(o_ref)

        o_ref[...] += _dot(a_ref[...], b_ref[...], _TN)

    return pl.pallas_call(
        kern, name=name, grid=(m // tm, n // tn, nk),
        in_specs=[pl.BlockSpec((tk, tm), lambda i, j, kk: (kk, i)), pl.BlockSpec((tk, tn), lambda i, j, kk: (kk, j))],
        out_specs=pl.BlockSpec((tm, tn), lambda i, j, kk: (i, j)),
        out_shape=jax.ShapeDtypeStruct((m, n), F32),
        compiler_params=_params("parallel", "parallel", "arbitrary"),
    )(a, b)


def _mm_f32(a, b, *, name, silu_a=False, bias=None):
    m, k = a.shape
    n = b.shape[1]

    def kern(*refs):
        if bias is None:
            a_ref, b_ref, o_ref = refs
        else:
            a_ref, b_ref, bias_ref, o_ref = refs
        av = a_ref[...]
        if silu_a:
            av = _silu(av)
        r = jnp.dot(av, b_ref[...], preferred_element_type=F32, precision=HI)
        if bias is not None:
            r = r + bias_ref[...]
        o_ref[...] = r

    args = [a, b] + ([] if bias is None else [bias])
    return pl.pallas_call(kern, name=name, out_shape=jax.ShapeDtypeStruct((m, n), F32),
                          compiler_params=pltpu.CompilerParams(vmem_limit_bytes=VMEM_LIMIT_BYTES))(*args)


def _shifted(v, s, t, lo, hi):
    n = v.shape[0]
    r = v if s == 0 else pltpu.roll(v, (-s) % n, 0)
    ok = (t + s >= lo) & (t + s < hi)
    return jnp.where(ok, r, 0.0)


def _seg_bounds(n, n_ctx):
    t = lax.broadcasted_iota(jnp.int32, (n, 1), 0)
    lo = jnp.where(t < n_ctx, 0, n_ctx)
    hi = jnp.where(t < n_ctx, n_ctx, n)
    return t, lo, hi


def _conv_fwd(xp, w8, b, *, n_ctx, name, cb=256):
    n, c = xp.shape

    def kern(x_ref, w_ref, b_ref, cpre_ref, act_ref):
        x = x_ref[...]
        t, lo, hi = _seg_bounds(n, n_ctx)
        acc = jnp.zeros_like(x) + b_ref[...]
        for k in range(SSD_CONV):
            acc = acc + _shifted(x, k - SSD_CONV // 2, t, lo, hi) * w_ref[k:k + 1, :]
        cpre_ref[...] = acc
        act_ref[...] = _silu(acc)

    spec = pl.BlockSpec((n, cb), lambda j: (0, j))
    return pl.pallas_call(
        kern, name=name, grid=(c // cb,),
        in_specs=[spec, pl.BlockSpec((8, cb), lambda j: (0, j)), pl.BlockSpec((1, cb), lambda j: (0, j))],
        out_specs=[spec, spec], out_shape=[jax.ShapeDtypeStruct((n, c), F32)] * 2,
        compiler_params=_params("parallel"),
    )(xp, w8, b)


def _conv_bwd(d1, d2, cpre, xp, w8, *, n_ctx, name, cb=128):
    n, c = xp.shape

    def kern(d1_ref, d2_ref, cpre_ref, x_ref, w_ref, dx_ref, dw_ref, db_ref):
        g = (d1_ref[...] + d2_ref[...]) * _dsilu(cpre_ref[...])
        x = x_ref[...]
        t, lo, hi = _seg_bounds(n, n_ctx)
        dx = jnp.zeros_like(g)
        dw_ref[...] = jnp.zeros_like(dw_ref)
        for k in range(SSD_CONV):
            s = k - SSD_CONV // 2
            dx = dx + _shifted(g, -s, t, lo, hi) * w_ref[k:k + 1, :]
            dw_ref[k:k + 1, :] = _sum0(g * _shifted(x, s, t, lo, hi))
        dx_ref[...] = dx.astype(BF16)
        db_ref[...] = _sum0(g)

    spec = pl.BlockSpec((n, cb), lambda j: (0, j))
    return pl.pallas_call(
        kern, name=name, grid=(c // cb,),
        in_specs=[spec, spec, spec, spec, pl.BlockSpec((8, cb), lambda j: (0, j))],
        out_specs=[spec, pl.BlockSpec((8, cb), lambda j: (0, j)), pl.BlockSpec((1, cb), lambda j: (0, j))],
        out_shape=[jax.ShapeDtypeStruct((n, c), BF16), jax.ShapeDtypeStruct((8, c), F32),
                   jax.ShapeDtypeStruct((1, c), F32)],
        compiler_params=_params("parallel"),
    )(d1, d2, cpre, xp, w8)


def _chunk_of(s, nc, n_ctx_chunks, rev):
    if not rev:
        return s
    return jnp.where(s < n_ctx_chunks, n_ctx_chunks - 1 - s, nc - 1 - (s - n_ctx_chunks))


def _scan_common(dt_raw, dtT_raw, bias_r, bias_c, alog_r, alog_c, rev):
    ii = lax.broadcasted_iota(jnp.int32, (CHUNK, CHUNK), 0)
    jj = lax.broadcasted_iota(jnp.int32, (CHUNK, CHUNK), 1)
    tri = (jj >= ii) if rev else (jj <= ii)
    tri_t = (ii >= jj) if rev else (ii <= jj)
    a_r = -jnp.exp(alog_r)
    a_c = -jnp.exp(alog_c)
    dt = _softplus(dt_raw + bias_r)
    dt_t = _softplus(dtT_raw + bias_c)
    al = dt * a_r
    acum = _dot(tri.astype(F32), al, precision=HI)
    acum_t = _dot(dt_t * a_c, tri_t.astype(F32), precision=HI)
    atot = _sum0(al)
    return tri, tri_t, a_r, dt, acum, acum_t, atot


def _head_spread():
    return jnp.repeat(jnp.eye(SSD_HEADS, dtype=BF16), SSD_HEAD_DIM, axis=1)


def _dot_sel(v, sel):
    hi = v.astype(BF16)
    lo = (v - hi.astype(F32)).astype(BF16)
    return _dot(hi, sel) + _dot(lo, sel)


def _ssd_scan_fwd(xbc, dt_raw, dtT_raw, bias_r, bias_c, alog_r, alog_c, *, rev, n_ctx_chunks, name):
    n = xbc.shape[0]
    nc = n // CHUNK
    cidx = functools.partial(_chunk_of, nc=nc, n_ctx_chunks=n_ctx_chunks, rev=rev)

    def kern(xs_ref, b_ref, c_ref, dt_ref, dtT_ref, br_ref, bc_ref, ar_ref, ac_ref, e_ref, y_ref, hs_ref, h_scr):
        @pl.when(pl.program_id(0) == 0)
        def _():
            h_scr[...] = jnp.zeros_like(h_scr)

        tri, _, _, dt, acum, acum_t, atot = _scan_common(
            dt_ref[...], dtT_ref[...], br_ref[...], bc_ref[...], ar_ref[...], ac_ref[...], rev)
        etot = jnp.exp(atot)
        spread = lambda v: _dot_sel(v, e_ref[...])
        xdt_all = xs_ref[...] * spread(dt)
        eax = spread(jnp.exp(acum))
        xdw_all = xdt_all * spread(jnp.exp(atot - acum))
        hs_ref[...] = h_scr[...]
        for g in range(SSD_GROUPS):
            gs = slice(g * 256, (g + 1) * 256)
            bg = b_ref[:, g * SSD_STATE:(g + 1) * SSD_STATE].astype(BF16)
            cg = c_ref[:, g * SSD_STATE:(g + 1) * SSD_STATE].astype(BF16)
            cb = _dot(cg, bg, _NT)
            h4 = h_scr[gs, :]
            ys = []
            for k in range(SSD_HPG):
                h = g * SSD_HPG + k
                lmat = jnp.exp(jnp.where(tri, acum[:, h:h + 1] - acum_t[h:h + 1, :], NEG_BIG))
                xdt_h = xdt_all[:, h * SSD_HEAD_DIM:(h + 1) * SSD_HEAD_DIM].astype(BF16)
                ys.append(_dot((cb * lmat).astype(BF16), xdt_h))
            y_ref[:, gs] = jnp.concatenate(ys, axis=1) + _dot(cg, h4.astype(BF16), _NT) * eax[:, gs]
            s4 = _dot(xdw_all[:, gs].astype(BF16), bg, _TN)
            for k in range(SSD_HPG):
                h = g * SSD_HPG + k
                rs = slice(h * SSD_HEAD_DIM, (h + 1) * SSD_HEAD_DIM)
                h_scr[rs, :] = h4[k * SSD_HEAD_DIM:(k + 1) * SSD_HEAD_DIM] * etot[:, h:h + 1] + \
                    s4[k * SSD_HEAD_DIM:(k + 1) * SSD_HEAD_DIM]

    nh = SSD_HEADS
    small = lambda shape: pl.BlockSpec(shape, lambda s: (0, 0))
    return pl.pallas_call(
        kern, name=name, grid=(nc,),
        in_specs=[pl.BlockSpec((CHUNK, SSD_INNER), lambda s: (cidx(s), 0)),
                  pl.BlockSpec((CHUNK, 1024), lambda s: (cidx(s), 2)),
                  pl.BlockSpec((CHUNK, 1024), lambda s: (cidx(s), 3)),
                  pl.BlockSpec((CHUNK, nh), lambda s: (cidx(s), 0)),
                  pl.BlockSpec((nh, CHUNK), lambda s: (0, cidx(s))),
                  small((1, nh)), small((nh, 1)), small((1, nh)), small((nh, 1)), small((nh, SSD_INNER))],
        out_specs=[pl.BlockSpec((CHUNK, SSD_INNER), lambda s: (cidx(s), 0)),
                   pl.BlockSpec((None, SSD_INNER, SSD_STATE), lambda s: (s, 0, 0))],
        out_shape=[jax.ShapeDtypeStruct((n, SSD_INNER), F32),
                   jax.ShapeDtypeStruct((nc, SSD_INNER, SSD_STATE), F32)],
        scratch_shapes=[pltpu.VMEM((SSD_INNER, SSD_STATE), F32)],
        compiler_params=_params("arbitrary"),
    )(xbc, xbc, xbc, dt_raw, dtT_raw, bias_r, bias_c, alog_r, alog_c, _head_spread())


def _ssd_scan_bwd(dy, xbc, hs, dt_raw, dtT_raw, bias_r, bias_c, alog_r, alog_c, dvec, *, rev, n_ctx_chunks,
                  direct, name):
    n = xbc.shape[0]
    nc = n // CHUNK
    nh = SSD_HEADS
    step_of = lambda r: nc - 1 - r
    cidx = lambda r: _chunk_of(step_of(r), nc, n_ctx_chunks, rev)

    def kern(dy_ref, xs_ref, b_ref, c_ref, hs_ref, dt_ref, dtT_ref, br_ref, bc_ref, ar_ref, ac_ref, dv_ref,
             e_ref, et_ref, dx_ref, ddt_ref, dal_ref, dbias_ref, dh_scr):
        @pl.when(pl.program_id(0) == 0)
        def _():
            dh_scr[...] = jnp.zeros_like(dh_scr)
            dal_ref[...] = jnp.zeros_like(dal_ref)
            dbias_ref[...] = jnp.zeros_like(dbias_ref)

        tri, tri_t, a_r, dt, acum, acum_t, atot = _scan_common(
            dt_ref[...], dtT_ref[...], br_ref[...], bc_ref[...], ar_ref[...], ac_ref[...], rev)
        etot = jnp.exp(atot)
        spread = lambda v: _dot_sel(v, e_ref[...])
        gather = lambda v: _dot_sel(v, et_ref[...])
        xs_all = xs_ref[...]
        dy_all = dy_ref[...]
        dtx = spread(dt)
        eax = spread(jnp.exp(acum))
        decx = spread(jnp.exp(atot - acum))
        xdt_all = xs_all * dtx
        xdw_all = xdt_all * decx
        dyo_all = dy_all * eax
        lane = lax.broadcasted_iota(jnp.int32, (CHUNK, nh), 1)
        lane1 = lax.broadcasted_iota(jnp.int32, (1, nh), 1)
        sub = lax.broadcasted_iota(jnp.int32, (nh, CHUNK), 0)
        g_rows = jnp.zeros((CHUNK, nh), F32)
        g_cols = jnp.zeros((nh, CHUNK), F32)
        dtot = jnp.zeros((1, nh), F32)
        q_col, q_e, q_dt = [], [], []
        for g in range(SSD_GROUPS):
            gs = slice(g * 256, (g + 1) * 256)
            bg = b_ref[:, g * SSD_STATE:(g + 1) * SSD_STATE].astype(BF16)
            cg = c_ref[:, g * SSD_STATE:(g + 1) * SSD_STATE].astype(BF16)
            cb = _dot(cg, bg, _NT)
            hs4 = hs_ref[gs, :]
            dh4 = dh_scr[gs, :]
            hs4_bf = hs4.astype(BF16)
            dh4_bf = dh4.astype(BF16)
            dy4 = dy_all[:, gs]
            dy4_bf = dy4.astype(BF16)
            xdt4_bf = xdt_all[:, gs].astype(BF16)
            xdw4 = xdw_all[:, gs]
            xdw4_bf = xdw4.astype(BF16)
            dyo4_bf = dyo_all[:, gs].astype(BF16)
            yoff4 = _dot(cg, hs4_bf, _NT) * eax[:, gs]
            dcg = _dot(dyo4_bf, hs4_bf)
            dh_new4 = _dot(dyo4_bf, cg, _TN)
            bdh4 = _dot(bg, dh4_bf, _NT)
            dbg = _dot(xdw4_bf, dh4_bf)
            e4 = xdw4 * bdh4
            q_col.append(dy4 * yoff4 - e4)
            q_e.append(e4)
            hsum = jnp.sum(dh4 * hs4, axis=1, keepdims=True)
            dcb = jnp.zeros((CHUNK, CHUNK), F32)
            dxdts = []
            for k in range(SSD_HPG):
                h = g * SSD_HPG + k
                ks = slice(k * SSD_HEAD_DIM, (k + 1) * SSD_HEAD_DIM)
                lmat = jnp.exp(jnp.where(tri, acum[:, h:h + 1] - acum_t[h:h + 1, :], NEG_BIG))
                mf = cb * lmat
                dm = _dot(dy4_bf[:, ks], xdt4_bf[:, ks], _NT)
                dcb = dcb + dm * lmat
                gmat = dm * mf
                g_rows = g_rows + jnp.where(lane == h, jnp.sum(gmat, axis=1, keepdims=True), 0.0)
                g_cols = g_cols + jnp.where(sub == h, _sum0(gmat), 0.0)
                dxdts.append(_dot(mf.astype(BF16), dy4_bf[:, ks], _TN))
                et = etot[:, h:h + 1]
                dtot = dtot + jnp.where(lane1 == h, _sum0(hsum[ks]) * et, 0.0)
                dh_scr[h * SSD_HEAD_DIM:(h + 1) * SSD_HEAD_DIM, :] = dh4[ks] * et + dh_new4[ks]
            dxdt4 = jnp.concatenate(dxdts, axis=1) + bdh4 * decx[:, gs]
            q_dt.append(dxdt4 * xs_all[:, gs])
            dx4 = dxdt4 * dtx[:, gs]
            if direct:
                dx4 = dx4 + dy4 * dv_ref[:, gs]
            dcb_bf = dcb.astype(BF16)
            dx_ref[:, gs] = dx4
            dx_ref[:, SSD_INNER + g * SSD_STATE:SSD_INNER + (g + 1) * SSD_STATE] = dbg + _dot(dcb_bf, cg, _TN)
            dx_ref[:, SSD_INNER + 1024 + g * SSD_STATE:SSD_INNER + 1024 + (g + 1) * SSD_STATE] = \
                dcg + _dot(dcb_bf, bg)
        e_heads = gather(jnp.concatenate(q_e, axis=1))
        dacum = gather(jnp.concatenate(q_col, axis=1)) + g_rows - g_cols.T
        dal = _dot(tri_t.astype(F32), dacum, precision=HI) + dtot + _sum0(e_heads)
        ddt = gather(jnp.concatenate(q_dt, axis=1)) + dal * a_r
        ddt_raw = ddt * _sig(dt_ref[...] + br_ref[...])
        ddt_ref[...] = ddt_raw
        dal_ref[...] += _sum0(dal * dt) * a_r
        dbias_ref[...] += _sum0(ddt_raw)

    small = lambda shape: pl.BlockSpec(shape, lambda r: (0, 0))
    return pl.pallas_call(
        kern, name=name, grid=(nc,),
        in_specs=[pl.BlockSpec((CHUNK, SSD_INNER), lambda r: (cidx(r), 0)),
                  pl.BlockSpec((CHUNK, SSD_INNER), lambda r: (cidx(r), 0)),
                  pl.BlockSpec((CHUNK, 1024), lambda r: (cidx(r), 2)),
                  pl.BlockSpec((CHUNK, 1024), lambda r: (cidx(r), 3)),
                  pl.BlockSpec((None, SSD_INNER, SSD_STATE), lambda r: (step_of(r), 0, 0)),
                  pl.BlockSpec((CHUNK, nh), lambda r: (cidx(r), 0)),
                  pl.BlockSpec((nh, CHUNK), lambda r: (0, cidx(r))),
                  small((1, nh)), small((nh, 1)), small((1, nh)), small((nh, 1)), small((1, SSD_INNER)),
                  small((nh, SSD_INNER)), small((SSD_INNER, nh))],
        out_specs=[pl.BlockSpec((CHUNK, SSD_CONV_DIM), lambda r: (cidx(r), 0)),
                   pl.BlockSpec((CHUNK, nh), lambda r: (cidx(r), 0)),
                   small((1, nh)), small((1, nh))],
        out_shape=[jax.ShapeDtypeStruct((n, SSD_CONV_DIM), F32), jax.ShapeDtypeStruct((n, nh), F32),
                   jax.ShapeDtypeStruct((1, nh), F32), jax.ShapeDtypeStruct((1, nh), F32)],
        scratch_shapes=[pltpu.VMEM((SSD_INNER, SSD_STATE), F32)],
        compiler_params=_params("arbitrary"),
    )(dy, xbc, xbc, xbc, hs, dt_raw, dtT_raw, bias_r, bias_c, alog_r, alog_c, dvec, _head_spread(),
      _head_spread().T)


def _gm_spatial_fwd(gu, gvn, ws, bst, *, name):
    n = gu.shape[0]

    def kern(gu_ref, gv_ref, ws_ref, bs_ref, o_ref):
        for g in range(GM_GROUPS):
            sl = slice(g * GM_GROUP_DIM, (g + 1) * GM_GROUP_DIM)
            s = _dot(ws_ref[g], gv_ref[:, sl]) + bs_ref[:, g:g + 1]
            o_ref[:, sl] = (gu_ref[:, sl] * s).astype(BF16)

    spec = pl.BlockSpec((CHUNK, GM_INNER), lambda i: (i, 0))
    return pl.pallas_call(
        kern, name=name, grid=(n // CHUNK,),
        in_specs=[spec, spec, pl.BlockSpec(ws.shape, lambda i: (0, 0, 0)), pl.BlockSpec(bst.shape, lambda i: (0, 0))],
        out_specs=spec, out_shape=jax.ShapeDtypeStruct((n, GM_INNER), BF16),
        compiler_params=_params("parallel"),
    )(gu, gvn, ws, bst)


def _gm_spatial_bwd(dt, gu, gvn, ws, wst, bst, *, name):
    n = gu.shape[0]

    def kern(dt_ref, gu_ref, gv_ref, ws_ref, wst_ref, bs_ref, dgu_ref, dgv_ref, dws_ref, dbs_ref):
        @pl.when(pl.program_id(0) == 0)
        def _():
            dws_ref[...] = jnp.zeros_like(dws_ref)
            dbs_ref[...] = jnp.zeros_like(dbs_ref)

        lane = lax.broadcasted_iota(jnp.int32, (CHUNK, GM_GROUPS), 1)
        dbs = jnp.zeros((CHUNK, GM_GROUPS), F32)
        for g in range(GM_GROUPS):
            sl = slice(g * GM_GROUP_DIM, (g + 1) * GM_GROUP_DIM)
            gv = gv_ref[:, sl]
            s = _dot(ws_ref[g], gv) + bs_ref[:, g:g + 1]
            d = dt_ref[:, sl]
            dgu_ref[:, sl] = d * s
            ds = d * gu_ref[:, sl]
            ds_bf = ds.astype(BF16)
            dws_ref[g] += _dot(ds_bf, gv, _NT)
            dgv_ref[:, sl] = _dot(wst_ref[g], ds_bf)
            dbs = dbs + jnp.where(lane == g, jnp.sum(ds, axis=1, keepdims=True), 0.0)
        dbs_ref[...] += dbs

    spec = pl.BlockSpec((CHUNK, GM_INNER), lambda i: (i, 0))
    wspec = pl.BlockSpec(ws.shape, lambda i: (0, 0, 0))
    bspec = pl.BlockSpec(bst.shape, lambda i: (0, 0))
    return pl.pallas_call(
        kern, name=name, grid=(n // CHUNK,),
        in_specs=[spec, spec, spec, wspec, wspec, bspec],
        out_specs=[spec, spec, wspec, bspec],
        out_shape=[jax.ShapeDtypeStruct((n, GM_INNER), F32), jax.ShapeDtypeStruct((n, GM_INNER), F32),
                   jax.ShapeDtypeStruct(ws.shape, F32), jax.ShapeDtypeStruct(bst.shape, F32)],
        compiler_params=_params("arbitrary"),
    )(dt, gu, gvn, ws, wst, bst)


def _adamw(parts, w, m, v, *, name, tm=256):
    ns, r, wd = parts.shape
    tm = _pick(r, tm, 8)

    def kern(p_ref, w_ref, m_ref, v_ref, g_ref, d_ref, nm_ref, nv_ref):
        g = p_ref[0].astype(F32)
        for s in range(1, ns):
            g = g + p_ref[s].astype(F32)
        m2 = ADAM_B1 * m_ref[...] + (1.0 - ADAM_B1) * g
        v2 = ADAM_B2 * v_ref[...] + (1.0 - ADAM_B2) * (g * g)
        m_hat = m2 / (1.0 - ADAM_B1 ** ADAM_STEP)
        v_hat = v2 / (1.0 - ADAM_B2 ** ADAM_STEP)
        g_ref[...] = g
        d_ref[...] = -ADAM_LR * (m_hat / (jnp.sqrt(v_hat) + ADAM_EPS) + ADAM_WD * w_ref[...])
        nm_ref[...] = m2
        nv_ref[...] = v2

    spec = pl.BlockSpec((tm, wd), lambda i: (i, 0))
    return pl.pallas_call(
        kern, name=name, grid=(r // tm,),
        in_specs=[pl.BlockSpec((ns, tm, wd), lambda i: (0, i, 0)), spec, spec, spec],
        out_specs=[spec] * 4, out_shape=[jax.ShapeDtypeStruct((r, wd), F32)] * 4,
        compiler_params=_params("parallel"),
    )(parts, w, m, v)


def _sum_slots(parts, *, name, scale_by=None):
    ns, r, wd = parts.shape

    def kern(*refs):
        p_ref, o_ref = refs[0], refs[-1]
        g = p_ref[0]
        for s in range(1, ns):
            g = g + p_ref[s]
        if scale_by is not None:
            g = g * _dsilu(refs[1][...])
        o_ref[...] = g

    args = [parts] + ([] if scale_by is None else [scale_by])
    return pl.pallas_call(kern, name=name, out_shape=jax.ShapeDtypeStruct((r, wd), F32),
                          compiler_params=pltpu.CompilerParams(vmem_limit_bytes=VMEM_LIMIT_BYTES))(*args)


def _mesh_pos():
    x, y, c = lax.axis_index("x"), lax.axis_index("y"), lax.axis_index("c")
    return x, y, c, 4 * x + 2 * y + c


def _flip(x, y, c, f):
    fx, fy, fc = (f >> 2) & 1, (f >> 1) & 1, f & 1
    px = 1 - x if fx else x
    py = 1 - y if fy else y
    pc = 1 - c if fc else c
    return (px, py, pc), 4 * px + 2 * py + pc


_HBM_SPEC = pl.BlockSpec(memory_space=pltpu.HBM)


def _exchange(arrays, *, scatter, name):
    na = len(arrays)
    if scatter:
        out_shape = [jax.ShapeDtypeStruct(a.shape, a.dtype) for a in arrays]
    else:
        out_shape = [jax.ShapeDtypeStruct((NDEV,) + a.shape, a.dtype) for a in arrays]

    out_shape.append(jax.ShapeDtypeStruct((8, 128), F32))

    def body(*refs):
        ins, outs = refs[:na], refs[na:2 * na]
        send_sems, recv_sems, local_sems = refs[2 * na + 1:]
        refs[2 * na][...] = jnp.zeros((8, 128), F32)
        x, y, c, me = _mesh_pos()
        copies = []
        for i in range(na):
            src_own = ins[i].at[me] if scatter else ins[i]
            lc = pltpu.make_async_copy(src_own, outs[i].at[me], local_sems.at[i])
            lc.start()
            copies.append(lc)
        sends = []
        for f in range(1, NDEV):
            peer, pidx = _flip(x, y, c, f)
            for i in range(na):
                k = i * (NDEV - 1) + f - 1
                src = ins[i].at[pidx] if scatter else ins[i]
                cp = pltpu.make_async_remote_copy(
                    src_ref=src, dst_ref=outs[i].at[me], send_sem=send_sems.at[k], recv_sem=recv_sems.at[k],
                    device_id=peer, device_id_type=pl.DeviceIdType.MESH)
                cp.start()
                sends.append(cp)
        for f in range(1, NDEV):
            peer, pidx = _flip(x, y, c, f)
            for i in range(na):
                k = i * (NDEV - 1) + f - 1
                src = ins[i].at[pidx] if scatter else ins[i]
                pltpu.make_async_remote_copy(
                    src_ref=src, dst_ref=outs[i].at[pidx], send_sem=send_sems.at[k], recv_sem=recv_sems.at[k],
                    device_id=peer, device_id_type=pl.DeviceIdType.MESH).wait_recv()
        for cp in sends:
            cp.wait_send()
        for lc in copies:
            lc.wait()

    res = pl.pallas_call(
        body, name=name, out_shape=out_shape, in_specs=[_HBM_SPEC] * na,
        out_specs=[_HBM_SPEC] * na + [pl.BlockSpec(memory_space=pltpu.VMEM)],
        scratch_shapes=[pltpu.SemaphoreType.DMA((na * (NDEV - 1),)), pltpu.SemaphoreType.DMA((na * (NDEV - 1),)),
                        pltpu.SemaphoreType.DMA((na,))],
        compiler_params=pltpu.CompilerParams(has_side_effects=True),
    )(*arrays)
    return res[:na], res[na][0, 0]


_SEM_SPEC = pl.BlockSpec(memory_space=pltpu.SEMAPHORE)
_DATAFLOW = pltpu.SideEffectType.DATAFLOW_SIDE_EFFECTING


def _split_copies(srcs, lands, send_sems, recv_sems, scatter, arriving):
    x, y, c, me = _mesh_pos()
    copies = []
    for i in range(len(srcs)):
        for f in range(1, NDEV):
            peer, pidx = _flip(x, y, c, f)
            k = i * (NDEV - 1) + f - 1
            copies.append(pltpu.make_async_remote_copy(
                src_ref=srcs[i].at[pidx] if scatter else srcs[i], dst_ref=lands[i].at[pidx if arriving else me],
                send_sem=send_sems.at[k], recv_sem=recv_sems.at[k], device_id=peer,
                device_id_type=pl.DeviceIdType.MESH))
    return copies


def _exchange_start(srcs, lands, *, scatter, name):
    na = len(srcs)
    nsem = na * (NDEV - 1)

    def body(*refs):
        ins_src, ins_land = refs[:na], refs[na:2 * na]
        send_sems, recv_sems = refs[2 * na], refs[2 * na + 1]
        token = refs[-1]
        for cp in _split_copies(ins_src, ins_land, send_sems, recv_sems, scatter, False):
            cp.start()
        token[...] = jnp.zeros_like(token)

    thru = [pltpu.HBM(a.shape, a.dtype) for a in list(srcs) + list(lands)]
    res = pl.pallas_call(
        body, name=name,
        out_shape=(pltpu.SemaphoreType.DMA((nsem,)), pltpu.SemaphoreType.DMA((nsem,)), *thru,
                   jax.ShapeDtypeStruct((8, 128), F32)),
        in_specs=[_HBM_SPEC] * (2 * na),
        out_specs=(_SEM_SPEC, _SEM_SPEC, *([_HBM_SPEC] * (2 * na)), pl.BlockSpec(memory_space=pltpu.VMEM)),
        input_output_aliases={i: 2 + i for i in range(2 * na)},
        compiler_params=pltpu.CompilerParams(has_side_effects=_DATAFLOW),
    )(*[pltpu.with_memory_space_constraint(a, pltpu.HBM) for a in list(srcs) + list(lands)])
    send_sems, recv_sems = res[0], res[1]
    return send_sems, recv_sems, res[2:2 + na], res[2 + na:2 + 2 * na], res[-1][0, 0]


def _exchange_wait(send_sems, recv_sems, srcs, lands, after, *, scatter, name):
    na = len(srcs)

    def body(*refs):
        ins_src, ins_land = refs[:na], refs[na:2 * na]
        s_sems, r_sems = refs[2 * na], refs[2 * na + 1]
        for cp in _split_copies(ins_src, ins_land, s_sems, r_sems, scatter, False):
            cp.wait_send()
        for cp in _split_copies(ins_src, ins_land, s_sems, r_sems, scatter, True):
            cp.wait_recv()

    thru = [pltpu.HBM(a.shape, a.dtype) for a in list(srcs) + list(lands)]
    res = pl.pallas_call(
        body, name=name, out_shape=tuple(thru),
        in_specs=[_HBM_SPEC] * (2 * na) + [_SEM_SPEC, _SEM_SPEC, pl.BlockSpec(memory_space=pl.ANY)],
        out_specs=tuple([_HBM_SPEC] * (2 * na)),
        input_output_aliases={i: i for i in range(2 * na)},
        compiler_params=pltpu.CompilerParams(has_side_effects=_DATAFLOW),
    )(*srcs, *lands, send_sems, recv_sems, after)
    return res[na:]


def _landing(block, me):
    buf = lax.empty((NDEV,) + block.shape, block.dtype)
    return lax.dynamic_update_slice_in_dim(buf, block[None], me, axis=0)


def _seg_kw(nseg, n_ctx, tm):
    return dict(nseg=nseg, seg_blocks=(n_ctx // tm if nseg == 2 else 0))


def _ffn_fwd(tag, h, gpre, gpost, shift, scale, gate, w, *, nseg, n_ctx, tm):
    n = h.shape[0]
    kw = _seg_kw(nseg, n_ctx, tm)
    (u,) = _rowwise(tag + "_pre", _pre_fwd_fn, n, [h], [("full", gpre), ("seg", shift), ("seg", scale)],
                    [(D_MODEL, BF16)], tm=tm, **kw)
    s, a, b = _mm_glu(u, w["win"], name=tag + "_glu")
    if "late" in w:
        w.update(w.pop("late")(s))
    y = _mm(s, w["wout"], out_dtype=F32, name=tag + "_out", tk=FFN_DIM)
    (ho,) = _rowwise(tag + "_post", functools.partial(_post_fwd_fn, 0.5), n, [h, y], [("full", gpost), ("seg", gate)],
                     [(D_MODEL, F32)], tm=tm, **kw)
    return ho, dict(h=h, u=u, s=s, a=a, b=b, y=y)


def _ffn_bwd(tag, dho, sv, gpre, gpost, scale, gate, w, put, *, nseg, n_ctx, tm):
    n = dho.shape[0]
    kw = _seg_kw(nseg, n_ctx, tm)
    dy, dgate, dgpost = _rowwise(tag + "_postb", functools.partial(_post_bwd_fn, 0.5), n, [dho, sv["y"]],
                                 [("full", gpost), ("seg", gate)], [(D_MODEL, BF16)], [D_MODEL, D_MODEL], tm=tm, **kw)
    tok = put("w_out", _mm_tn(sv["s"], dy, name=tag + "_dwout", tm=1408, tn=1024))
    ds = _mm(dy, w["wout"], out_dtype=F32, name=tag + "_ds", tn=1408, rhs_t=True)
    (dp,) = _rowwise(tag + "_glub", _glu_bwd_fn, n, [ds, sv["a"], sv["b"]], [], [(2 * FFN_DIM, BF16)], tm=min(tm, 128))
    dwin = _mm_tn(sv["u"], dp, name=tag + "_dwin", tn=1408)
    du = _mm(dp, w["win"], out_dtype=F32, name=tag + "_du", rhs_t=True)
    if tok is not None:
        gpre = gpre + tok
    dh, dshift, dscale, dgpre = _rowwise(tag + "_preb", _pre_bwd_fn, n, [du, sv["h"], dho],
                                         [("full", gpre), ("seg", scale)], [(D_MODEL, F32)],
                                         [D_MODEL, D_MODEL, D_MODEL], tm=tm, **kw)
    return dh, put("w_in", dwin), dict(shift=dshift, scale=dscale, gate=dgate, gpre=dgpre, gpost=dgpost)


def _local_step(x, ctx, target, mods, norm_g, get_w, small, put_grad):
    t_len, n_ctx = x.shape[0], ctx.shape[0]
    n0 = t_len + n_ctx
    tm0 = _pick(n_ctx, 256, 8)
    tm1 = _pick(t_len, 256, 8)
    ncc = n_ctx // CHUNK
    g = {}

    def modrow(i, k, nseg):
        mc, mx = mods[i]
        if nseg == 2:
            return jnp.stack([mc[k], mx[k]])[:, None, :]
        return mx[k][None, None, :]

    pending = [None]

    def gvec(i, k):
        v = norm_g[i, k][None, :]
        if pending[0] is not None:
            v = v + pending[0]
            pending[0] = None
        return v

    xc = jnp.concatenate([ctx, x], axis=0)
    L0 = dict(nseg=2, n_ctx=n_ctx, tm=tm0)
    wts = dict(get_w("ffn00", xc))
    h1, sv_f01 = _ffn_fwd("l0f1", xc, gvec(0, 0), gvec(0, 1), modrow(0, 0, 2), modrow(0, 1, 2), modrow(0, 2, 2),
                          wts["ffn00"], **L0)
    kw0 = _seg_kw(2, n_ctx, tm0)
    (um0,) = _rowwise("l0m_pre", _pre_fwd_fn, n0, [h1], [("full", gvec(0, 2)), ("seg", modrow(0, 3, 2)),
                                                         ("seg", modrow(0, 4, 2))], [(D_MODEL, BF16)], tm=tm0, **kw0)
    wts.update(get_w("ssd", um0))
    z = _mm(um0, wts["ssd_win"], out_dtype=F32, name="ssd_z", n=SSD_INNER)
    xbc_pre = _mm(um0, wts["ssd_win"], out_dtype=F32, name="ssd_xbc", n=SSD_CONV_DIM, b_off=(0, SSD_INNER // 1024))
    dtr = _mm(um0, wts["ssd_wdt"], out_dtype=F32, name="ssd_dt")
    cpre, xbc = _conv_fwd(xbc_pre, small["conv_w8"], small["conv_b"], n_ctx=n_ctx, name="ssd_conv")
    nh = SSD_HEADS
    dt_dir = [dtr[:, :nh], dtr[:, nh:2 * nh]]
    dtT_dir = [d.T for d in dt_dir]
    bias_r = [small["dt_bias"][d][None, :] for d in range(2)]
    bias_c = [small["dt_bias"][d][:, None] for d in range(2)]
    alog_r = [small["a_log"][d][None, :] for d in range(2)]
    alog_c = [small["a_log"][d][:, None] for d in range(2)]
    ys, hss = [], []
    for d in range(2):
        yd, hsd = _ssd_scan_fwd(xbc, dt_dir[d], dtT_dir[d], bias_r[d], bias_c[d], alog_r[d], alog_c[d],
                                rev=(d == 1), n_ctx_chunks=ncc, name=f"ssd_scan{d}")
        ys.append(yd)
        hss.append(hsd)
    dvec = jnp.repeat(small["ssd_d"], SSD_HEAD_DIM)[None, :]
    ngv = small["ssd_norm_g"][None, :]
    gate_rows = [ys[0], ys[1], (xbc, SSD_INNER, 0, 0), z]
    (yn_all,) = _rowwise("ssd_gate", _ssdgate_fwd_fn, n0, gate_rows, [("full", dvec), ("full", ngv)],
                         [(SSD_INNER, BF16)], tm=128)
    yn = yn_all[n_ctx:]
    yo0 = _mm(yn, wts["ssd_wout"], out_dtype=F32, name="ssd_out", tk=SSD_INNER)
    h1x = h1[n_ctx:]
    L1 = dict(nseg=1, n_ctx=0, tm=tm1)
    (h2,) = _rowwise("l0m_post", functools.partial(_post_fwd_fn, 1.0), t_len, [h1x, yo0],
                     [("full", gvec(0, 3)), ("seg", modrow(0, 5, 1))], [(D_MODEL, F32)], tm=tm1)
    wts.update(get_w("ffn01", h2))
    h3, sv_f02 = _ffn_fwd("l0f2", h2, gvec(0, 4), gvec(0, 5), modrow(0, 6, 1), modrow(0, 7, 1), modrow(0, 8, 1),
                          wts["ffn01"], **L1)

    wts.update(get_w("ffn10", h3))
    h4, sv_f11 = _ffn_fwd("l1f1", h3, gvec(1, 0), gvec(1, 1), modrow(1, 0, 1), modrow(1, 1, 1), modrow(1, 2, 1),
                          wts["ffn10"], **L1)
    (um1,) = _rowwise("l1m_pre", _pre_fwd_fn, t_len, [h4], [("full", gvec(1, 2)), ("seg", modrow(1, 3, 1)),
                                                            ("seg", modrow(1, 4, 1))], [(D_MODEL, BF16)], tm=tm1)
    wts.update(get_w("gm", um1))
    p1 = _mm(um1, wts["gm_win"], out_dtype=F32, name="gm_in")
    vg = small["gm_v_g"][None, :]
    vb = small["gm_v_b"][None, :]
    gu, gvn = _rowwise("gm_act", _gm_act_fwd_fn, t_len, [p1], [("full", vg), ("full", vb)],
                       [(GM_INNER, F32), (GM_INNER, BF16)], tm=128)
    ws_bf = small["gm_w_s"].astype(BF16)
    wst_bf = jnp.swapaxes(small["gm_w_s"], 1, 2).astype(BF16)
    bst = small["gm_b_s"].T
    tgm = _gm_spatial_fwd(gu, gvn, ws_bf, bst, name="gm_spatial")
    yo1 = _mm(tgm, wts["gm_wout"], out_dtype=F32, name="gm_out", tk=GM_INNER)
    (h5,) = _rowwise("l1m_post", functools.partial(_post_fwd_fn, 1.0), t_len, [h4, yo1],
                     [("full", gvec(1, 3)), ("seg", modrow(1, 5, 1))], [(D_MODEL, F32)], tm=tm1)
    wts.update(get_w("ffn11", h5))
    h6, sv_f12 = _ffn_fwd("l1f2", h5, gvec(1, 4), gvec(1, 5), modrow(1, 6, 1), modrow(1, 7, 1), modrow(1, 8, 1),
                          wts["ffn11"], **L1)

    dh, loss_parts = _rowwise("loss", _loss_fn, t_len, [h6, target], [], [(D_MODEL, F32)], [D_MODEL], tm=tm1)

    zero = jnp.zeros((D_MODEL,), F32)
    dmx = [[zero] * N_MOD for _ in range(2)]
    dmc = [[zero] * N_MOD for _ in range(2)]
    dng = [[zero] * 6 for _ in range(2)]

    def put_mod(i, k, acc):
        if acc.shape[0] == 2:
            dmc[i][k] = dmc[i][k] + acc[0, 0]
            dmx[i][k] = dmx[i][k] + acc[1, 0]
        else:
            dmx[i][k] = dmx[i][k] + acc[0, 0]

    def put_g(i, k, acc):
        dng[i][k] = dng[i][k] + jnp.sum(acc[:, 0], axis=0)

    def ffn_back(tag, i, j, dho, sv, w, lay):
        nseg = lay["nseg"]
        base = 0 if j == 0 else 6
        gi = 0 if j == 0 else 4
        dh_in, pending[0], s = _ffn_bwd(tag, dho, sv, gvec(i, gi), gvec(i, gi + 1), modrow(i, base + 1, nseg),
                                        modrow(i, base + 2, nseg), w, functools.partial(put_grad, f"ffn{i}{j}"), **lay)
        put_mod(i, base, s["shift"])
        put_mod(i, base + 1, s["scale"])
        put_mod(i, base + 2, s["gate"])
        put_g(i, gi, s["gpre"])
        put_g(i, gi + 1, s["gpost"])
        return dh_in

    dh = ffn_back("l1f2", 1, 1, dh, sv_f12, wts["ffn11"], L1)
    dyo, dgate, dgp = _rowwise("l1m_postb", functools.partial(_post_bwd_fn, 1.0), t_len, [dh, yo1],
                               [("full", gvec(1, 3)), ("seg", modrow(1, 5, 1))], [(D_MODEL, BF16)],
                               [D_MODEL, D_MODEL], tm=tm1)
    put_mod(1, 5, dgate)
    put_g(1, 3, dgp)
    put_grad("gm", "w_out", _mm_tn(tgm, dyo, name="gm_dwout", tn=1024))
    dtg = _mm(dyo, wts["gm_wout"], out_dtype=F32, name="gm_dt", rhs_t=True)
    dgu, dgvn, dws, dbst = _gm_spatial_bwd(dtg, gu, gvn, ws_bf, wst_bf, bst, name="gm_spatialb")
    g["gm_w_s"] = dws
    g["gm_b_s"] = dbst.T
    dp1, dvg, dvb = _rowwise("gm_actb", _gm_act_bwd_fn, t_len, [p1, dgu, dgvn], [("full", vg)],
                             [(2 * GM_INNER, BF16)], [GM_INNER, GM_INNER], tm=128)
    g["gm_v_g"] = dvg[0, 0]
    g["gm_v_b"] = dvb[0, 0]
    pending[0] = put_grad("gm", "w_in", _mm_tn(um1, dp1, name="gm_dwin", tm=1024))
    dum1 = _mm(dp1, wts["gm_win"], out_dtype=F32, name="gm_dum", tk=1024, rhs_t=True)
    dh, dsh, dsc, dgp = _rowwise("l1m_preb", _pre_bwd_fn, t_len, [dum1, h4, dh],
                                 [("full", gvec(1, 2)), ("seg", modrow(1, 4, 1))], [(D_MODEL, F32)],
                                 [D_MODEL, D_MODEL, D_MODEL], tm=tm1)
    put_mod(1, 3, dsh)
    put_mod(1, 4, dsc)
    put_g(1, 2, dgp)
    dh = ffn_back("l1f1", 1, 0, dh, sv_f11, wts["ffn10"], L1)

    dh = ffn_back("l0f2", 0, 1, dh, sv_f02, wts["ffn01"], L1)
    dyo, dgate, dgp = _rowwise("l0m_postb", functools.partial(_post_bwd_fn, 1.0), t_len, [dh, yo0],
                               [("full", gvec(0, 3)), ("seg", modrow(0, 5, 1))], [(D_MODEL, BF16)],
                               [D_MODEL, D_MODEL], tm=tm1)
    put_mod(0, 5, dgate)
    put_g(0, 3, dgp)
    put_grad("ssd", "w_out", _mm_tn(yn, dyo, name="ssd_dwout", tn=1024))
    dyn = _mm(dyo, wts["ssd_wout"], out_dtype=F32, name="ssd_dyn", rhs_t=True)
    dyn_all = jnp.concatenate([jnp.zeros((n_ctx, SSD_INNER), F32), dyn], axis=0)
    dy_ssd, dz, dngv, ddv = _rowwise("ssd_gateb", _ssdgate_bwd_fn, n0, [dyn_all] + gate_rows,
                                     [("full", dvec), ("full", ngv)], [(SSD_INNER, F32), (SSD_INNER, BF16)],
                                     [SSD_INNER, SSD_INNER], tm=128)
    g["ssd_norm_g"] = dngv[0, 0]
    g["ssd_D"] = jnp.sum(ddv[0, 0].reshape(SSD_HEADS, SSD_HEAD_DIM), axis=1)
    dxbcs, ddts, dalogs, dbiases = [], [], [], []
    for d in range(2):
        dxd, ddtd, dal, dbi = _ssd_scan_bwd(dy_ssd, xbc, hss[d], dt_dir[d], dtT_dir[d], bias_r[d], bias_c[d],
                                            alog_r[d], alog_c[d], dvec, rev=(d == 1), n_ctx_chunks=ncc,
                                            direct=(d == 0), name=f"ssd_scanb{d}")
        dxbcs.append(dxd)
        ddts.append(ddtd)
        dalogs.append(dal[0])
        dbiases.append(dbi[0])
    g["ssd_A_log"] = jnp.stack(dalogs)
    g["ssd_dt_bias"] = jnp.stack(dbiases)
    dxbc_pre, dcw8, dcb = _conv_bwd(dxbcs[0], dxbcs[1], cpre, xbc_pre, small["conv_w8"], n_ctx=n_ctx, name="ssd_convb")
    g["ssd_conv_w"] = dcw8[:SSD_CONV]
    g["ssd_conv_b"] = dcb[0]
    ddt_bf = jnp.concatenate([ddts[0], ddts[1], jnp.zeros((n0, 128 - 2 * nh), F32)], axis=1).astype(BF16)
    dw_ssd_in = jnp.concatenate([
        _mm_tn(um0, dz, name="ssd_dwz", tm=1024),
        _mm_tn(um0, dxbc_pre, name="ssd_dwxbc", tm=1024),
        _mm_tn(um0, ddt_bf, name="ssd_dwdt", tm=1024)[:, :2 * nh]], axis=1)
    pending[0] = put_grad("ssd", "w_in", dw_ssd_in)
    win_ssd = wts["ssd_win"]
    dum0 = _mm(dz, win_ssd, out_dtype=F32, name="ssd_dum_z", tk=1024, rhs_t=True, n=D_MODEL)
    dum0 = _mm(dxbc_pre, win_ssd, out_dtype=F32, name="ssd_dum_x", tk=1024, rhs_t=True, n=D_MODEL,
               b_off=(0, SSD_INNER // 1024), add=dum0)
    dum0 = _mm(ddt_bf, wts["ssd_wdt"], out_dtype=F32, name="ssd_dum_dt", rhs_t=True, add=dum0)
    dres = jnp.concatenate([jnp.zeros((n_ctx, D_MODEL), F32), dh], axis=0)
    dh0, dsh, dsc, dgp = _rowwise("l0m_preb", _pre_bwd_fn, n0, [dum0, h1, dres],
                                  [("full", gvec(0, 2)), ("seg", modrow(0, 4, 2))], [(D_MODEL, F32)],
                                  [D_MODEL, D_MODEL, D_MODEL], tm=tm0, **kw0)
    put_mod(0, 3, dsh)
    put_mod(0, 4, dsc)
    put_g(0, 2, dgp)
    dh0 = ffn_back("l0f1", 0, 0, dh0, sv_f01, wts["ffn00"], L0)
    grad_x = dh0[n_ctx:]
    g["norm_g"] = jnp.stack([jnp.stack(r) for r in dng])
    g["dmx"] = jnp.stack([jnp.concatenate(r) for r in dmx])
    g["dmc"] = jnp.stack([jnp.concatenate(r) for r in dmc])
    return loss_parts[0], grad_x, g


GROUPS = ("ffn00", "ssd", "ffn01", "ffn10", "gm", "ffn11")


def _mats_in(group, win_l):
    k, nloc = win_l.shape[1], win_l.shape[2]
    win = jnp.transpose(win_l, (1, 0, 2)).reshape(k, NDEV * nloc)
    if group.startswith("ffn"):
        return dict(win=win)
    if group == "gm":
        return dict(gm_win=win)
    assert group == "ssd"
    c1 = SSD_INNER + SSD_CONV_DIM
    return dict(ssd_win=win, ssd_wdt=jnp.pad(win[:, c1:], ((0, 0), (0, 128 - 2 * SSD_HEADS))))


def _mats_out(group, wout_l):
    pre = "" if group.startswith("ffn") else group + "_"
    return {pre + "wout": wout_l.reshape(-1, wout_l.shape[2])}


def _group_mats(group, lands):
    m = {**_mats_in(group, lands[0]), **_mats_out(group, lands[1])}
    return {group: m} if group.startswith("ffn") else m


def _grad_blocks(which, grad):
    if which == "w_in":
        k, n = grad.shape
        return jnp.transpose(grad.reshape(k, NDEV, n // NDEV), (1, 0, 2)).astype(BF16)
    return grad.reshape(NDEV, grad.shape[0] // NDEV, grad.shape[1]).astype(BF16)


def kernel(x, c, ctx, c_ctx, ada_w, ada_b, norm_g, ffn_w_in, ffn_w_out, ssd_w_in, ssd_conv_w, ssd_conv_b, ssd_dt_bias, ssd_A_log, ssd_D, ssd_norm_g, ssd_w_out, gm_w_in, gm_v_g, gm_v_b, gm_w_s, gm_b_s, gm_w_out, loss_target, m_c_ctx, m_ada_w, m_ada_b, m_norm_g, m_ffn_w_in, m_ffn_w_out, m_ssd_w_in, m_ssd_conv_w, m_ssd_conv_b, m_ssd_dt_bias, m_ssd_A_log, m_ssd_D, m_ssd_norm_g, m_ssd_w_out, m_gm_w_in, m_gm_v_g, m_gm_v_b, m_gm_w_s, m_gm_b_s, m_gm_w_out, v_c_ctx, v_ada_w, v_ada_b, v_norm_g, v_ffn_w_in, v_ffn_w_out, v_ssd_w_in, v_ssd_conv_w, v_ssd_conv_b, v_ssd_dt_bias, v_ssd_A_log, v_ssd_D, v_ssd_norm_g, v_ssd_w_out, v_gm_w_in, v_gm_v_g, v_gm_v_b, v_gm_w_s, v_gm_b_s, v_gm_w_out):
    me = 4 * lax.axis_index("x") + 2 * lax.axis_index("y") + lax.axis_index("c")
    d = D_MODEL
    ncol = N_MOD * d // NDEV

    small_pack = jnp.concatenate([c.reshape(-1), norm_g.reshape(-1), ssd_conv_w.reshape(-1),
                                  gm_v_g.reshape(-1), gm_v_b.reshape(-1)])[None, :]
    (sp,), _ = _exchange([small_pack], scatter=False, name="gather_small")
    sp = sp[:, 0]
    o = 0
    c_all = sp[:, o:o + d]; o += d
    ng_all = sp[:, o:o + 2 * 6 * 128].reshape(NDEV, 2, 6, 128); o += 2 * 6 * 128
    cw_all = sp[:, o:o + SSD_CONV * 512].reshape(NDEV, SSD_CONV, 512); o += SSD_CONV * 512
    vg_all = sp[:, o:o + 256]; o += 256
    vb_all = sp[:, o:o + 256]; o += 256
    norm_g_full = jnp.transpose(ng_all, (1, 2, 0, 3)).reshape(2, 6, d)
    conv_w_full = jnp.transpose(cw_all, (1, 0, 2)).reshape(SSD_CONV, SSD_CONV_DIM)
    gm_v_g_full = vg_all.reshape(-1)
    gm_v_b_full = vb_all.reshape(-1)

    c16 = jnp.concatenate([c_all, jnp.broadcast_to(c_ctx[None, :], (NDEV, d))], axis=0)
    ada_b_loc = lax.dynamic_slice_in_dim(ada_b, me * ncol, ncol, axis=1)
    mods_loc = jnp.stack([_mm_f32(c16, ada_w[i], name=f"ada_mod{i}", silu_a=True, bias=ada_b_loc[i][None, :])
                          for i in range(2)])
    (mods_all,), mods_done = _exchange([mods_loc], scatter=False, name="gather_mods")

    shard = {"ssd": (ssd_w_in[0], ssd_w_out[0]), "gm": (gm_w_in[0], gm_w_out[0])}
    moment = {"ssd": ((m_ssd_w_in[0], v_ssd_w_in[0]), (m_ssd_w_out[0], v_ssd_w_out[0])),
              "gm": ((m_gm_w_in[0], v_gm_w_in[0]), (m_gm_w_out[0], v_gm_w_out[0]))}
    for i in range(2):
        for j in range(2):
            shard[f"ffn{i}{j}"] = (ffn_w_in[i, j], ffn_w_out[i, j])
            moment[f"ffn{i}{j}"] = ((m_ffn_w_in[i, j], v_ffn_w_in[i, j]), (m_ffn_w_out[i, j], v_ffn_w_out[i, j]))
    first = GROUPS[0]
    units = [(first + "_in", first, (0,)), (first + "_out", first, (1,))] + [(grp, grp, (0, 1)) for grp in GROUPS[1:]]
    gathers = {}
    for unit, grp, idx in units:
        srcs = [(shard[grp][k] + mods_done).astype(BF16) for k in idx]
        st = _exchange_start(srcs, [_landing(s, me) for s in srcs], scatter=False, name="gather_start_" + unit)
        gathers[unit] = st[:4]

    def fetch(unit, after):
        return _exchange_wait(*gathers[unit], after, scatter=False, name="gather_wait_" + unit)

    def get_w(grp, after):
        if grp != first:
            return _group_mats(grp, fetch(grp, after))
        late = lambda later: _mats_out(grp, fetch(grp + "_out", later)[0])
        return {grp: dict(_mats_in(grp, fetch(grp + "_in", after)[0]), late=late)}

    scatters = {}
    held = {}

    def put_grad(grp, which, grad):
        if grp == first:
            unit, blocks = grp + "_" + which[2:], [_grad_blocks(which, grad)]
        else:
            held[grp, which] = _grad_blocks(which, grad)
            if (grp, "w_in") not in held or (grp, "w_out") not in held:
                return None
            unit, blocks = grp, [held[grp, "w_in"], held[grp, "w_out"]]
        lands = [_landing(lax.dynamic_index_in_dim(b, me, axis=0, keepdims=False), me) for b in blocks]
        st = _exchange_start(blocks, lands, scatter=True, name="scatter_start_" + unit)
        scatters[unit] = st[:4]
        return st[4]

    mods_rows = jnp.transpose(mods_all, (1, 2, 0, 3)).reshape(2, 2 * NDEV, N_MOD * d)
    mx = lax.dynamic_index_in_dim(mods_rows, me, axis=1, keepdims=False).reshape(2, N_MOD, d)
    mc = mods_rows[:, NDEV].reshape(2, N_MOD, d)
    mods = [(mc[i], mx[i]) for i in range(2)]

    small = dict(conv_w8=jnp.pad(conv_w_full, ((0, 8 - SSD_CONV), (0, 0))), conv_b=ssd_conv_b, dt_bias=ssd_dt_bias[0],
                 a_log=ssd_A_log[0], ssd_d=ssd_D[0], ssd_norm_g=ssd_norm_g[0], gm_v_g=gm_v_g_full,
                 gm_v_b=gm_v_b_full, gm_w_s=gm_w_s[0], gm_b_s=gm_b_s[0])
    loss_parts, grad_x, g = _local_step(x[0], ctx[0], loss_target[0], mods, norm_g_full, get_w, small, put_grad)
    loss = lax.psum(0.5 / d * jnp.sum(loss_parts), ("x", "y", "c"))

    upd = {}
    after = grad_x
    for unit, grp, idx in reversed(units):
        parts = _exchange_wait(*scatters[unit], after, scatter=True, name="scatter_wait_" + unit)
        for k, p in zip(idx, parts):
            m_, v_ = moment[grp][k]
            which = ("in", "out")[k]
            upd[grp, which] = _adamw(p, shard[grp][k], m_, v_, name=f"adamw_{grp}_{which}")
            after = upd[grp, which][0]
    res = {}
    for which in ("in", "out"):
        res["ffn_w_" + which] = [jnp.stack([jnp.stack([upd[f"ffn{i}{j}", which][k] for j in range(2)])
                                            for i in range(2)]) for k in range(4)]
        res["ssd_w_" + which] = [upd["ssd", which][k][None] for k in range(4)]
        res["gm_w_" + which] = [upd["gm", which][k][None] for k in range(4)]

    sg_names = ["dmx", "dmc", "norm_g", "ssd_conv_w", "ssd_conv_b", "ssd_dt_bias", "ssd_A_log", "ssd_D", "ssd_norm_g",
                "gm_v_g", "gm_v_b", "gm_w_s", "gm_b_s"]
    sg_shapes = [g[n].shape for n in sg_names]
    flat = jnp.concatenate([g[n].reshape(-1) for n in sg_names])
    npack = flat.shape[0]
    pad = (-npack) % 1024
    flat = jnp.pad(flat, (0, pad)).reshape(-1, 128)
    (sg_all,), _ = _exchange([flat], scatter=False, name="gather_small_grads")
    sg_sum = _sum_slots(sg_all, name="sum_small_grads").reshape(-1)[:npack]
    sums = {}
    o = 0
    for n, shp in zip(sg_names, sg_shapes):
        sz = math.prod(shp)
        sums[n] = sg_sum[o:o + sz].reshape(shp)
        o += sz
    per_dev = sg_all.reshape(NDEV, -1)
    dmx_all = per_dev[:, :2 * N_MOD * d].reshape(NDEV, 2, N_MOD * d)
    dmc_all = per_dev[:, 2 * N_MOD * d:4 * N_MOD * d].reshape(NDEV, 2, N_MOD * d)

    (s16,) = _rowwise("ada_silu", lambda cc: ((_silu(cc),), ()), 2 * NDEV, [c16], [], [(d, F32)], tm=2 * NDEV)
    s16_t = s16.T
    g_ada_w, dcc_parts = [], []
    for i in range(2):
        rhs = jnp.concatenate([lax.dynamic_slice_in_dim(dmx_all[:, i], me * ncol, ncol, axis=1),
                               lax.dynamic_slice_in_dim(dmc_all[:, i], me * ncol, ncol, axis=1)], axis=0)
        g_ada_w.append(_mm_f32(s16_t, rhs, name=f"ada_dw{i}"))
        dmc_loc = lax.dynamic_slice_in_dim(sums["dmc"][i], me * ncol, ncol, axis=0)
        rhs_c = jnp.zeros((ncol, 128), F32).at[:, 0].set(dmc_loc)
        dcc_parts.append(_mm_f32(ada_w[i], rhs_c, name=f"ada_dcc{i}")[:, 0])
    g_ada_w = jnp.stack(g_ada_w)
    dcc_part = (dcc_parts[0] + dcc_parts[1]).reshape(8, 128)
    (dcc_all,), _ = _exchange([dcc_part], scatter=False, name="gather_dcc")
    g_c_ctx = _sum_slots(dcc_all, name="sum_dcc", scale_by=c_ctx.reshape(8, 128)).reshape(d)
    g_ada_b = sums["dmx"] + sums["dmc"]

    outs = _adamw(g_ada_w.reshape(1, -1, ncol), ada_w.reshape(-1, ncol), m_ada_w.reshape(-1, ncol),
                  v_ada_w.reshape(-1, ncol), name="adamw_ada_w")
    res["ada_w"] = [o_.reshape(ada_w.shape) for o_ in outs]

    loc = lambda a, ax, n: lax.dynamic_slice_in_dim(a, me * n, n, axis=ax)
    small_g = dict(c_ctx=g_c_ctx, ada_b=g_ada_b, norm_g=loc(sums["norm_g"], 2, 128),
                   ssd_conv_w=loc(sums["ssd_conv_w"], 1, 512)[None], ssd_conv_b=sums["ssd_conv_b"][None],
                   ssd_dt_bias=sums["ssd_dt_bias"][None], ssd_A_log=sums["ssd_A_log"][None], ssd_D=sums["ssd_D"][None],
                   ssd_norm_g=sums["ssd_norm_g"][None], gm_v_g=loc(sums["gm_v_g"], 0, 256)[None],
                   gm_v_b=loc(sums["gm_v_b"], 0, 256)[None], gm_w_s=sums["gm_w_s"][None], gm_b_s=sums["gm_b_s"][None])
    small_w = dict(c_ctx=(c_ctx, m_c_ctx, v_c_ctx), ada_b=(ada_b, m_ada_b, v_ada_b), norm_g=(norm_g, m_norm_g, v_norm_g),
                   ssd_conv_w=(ssd_conv_w, m_ssd_conv_w, v_ssd_conv_w), ssd_conv_b=(ssd_conv_b, m_ssd_conv_b, v_ssd_conv_b),
                   ssd_dt_bias=(ssd_dt_bias, m_ssd_dt_bias, v_ssd_dt_bias), ssd_A_log=(ssd_A_log, m_ssd_A_log, v_ssd_A_log),
                   ssd_D=(ssd_D, m_ssd_D, v_ssd_D), ssd_norm_g=(ssd_norm_g, m_ssd_norm_g, v_ssd_norm_g),
                   gm_v_g=(gm_v_g, m_gm_v_g, v_gm_v_g), gm_v_b=(gm_v_b, m_gm_v_b, v_gm_v_b),
                   gm_w_s=(gm_w_s, m_gm_w_s, v_gm_w_s), gm_b_s=(gm_b_s, m_gm_b_s, v_gm_b_s))
    sn = list(small_w)

    def pack(arrs):
        f = jnp.concatenate([a.reshape(-1) for a in arrs])
        return jnp.pad(f, (0, (-f.shape[0]) % 1024)).reshape(-1, 128)

    pg = pack([small_g[n].reshape(small_w[n][0].shape) for n in sn])
    outs = _adamw(pg[None], pack([small_w[n][0] for n in sn]), pack([small_w[n][1] for n in sn]),
                  pack([small_w[n][2] for n in sn]), name="adamw_small")
    flat_outs = [o_.reshape(-1) for o_ in outs]
    o = 0
    for n in sn:
        shp = small_w[n][0].shape
        sz = math.prod(shp)
        res[n] = [fo[o:o + sz].reshape(shp) for fo in flat_outs]
        o += sz

    order = ["c_ctx", "ada_w", "ada_b", "norm_g", "ffn_w_in", "ffn_w_out", "ssd_w_in", "ssd_conv_w", "ssd_conv_b",
             "ssd_dt_bias", "ssd_A_log", "ssd_D", "ssd_norm_g", "ssd_w_out", "gm_w_in", "gm_v_g", "gm_v_b", "gm_w_s",
             "gm_b_s", "gm_w_out"]
    result = [loss, grad_x[None]]
    for k in range(4):
        result += [res[n][k] for n in order]
    return tuple(result)
```

```python
import functools
import math

import jax
import jax.numpy as jnp
from jax import lax
from jax.experimental import pallas as pl
from jax.experimental.pallas import tpu as pltpu

F32 = jnp.float32
BF16 = jnp.bfloat16

NDEV = 8
D_MODEL = 1024
FFN_DIM = 2816
N_MOD = 9
EPS = 1e-6
SSD_INNER = 2048
SSD_HEADS = 32
SSD_HEAD_DIM = 64
SSD_GROUPS = 8
SSD_HPG = 4
SSD_STATE = 128
SSD_CONV = 5
SSD_CONV_DIM = 4096
CHUNK = 128
GM_INNER = 2048
GM_GROUPS = 8
GM_GROUP_DIM = 256
ADAM_LR = 0.001
ADAM_B1 = 0.9
ADAM_B2 = 0.999
ADAM_EPS = 1e-08
ADAM_WD = 0.01
ADAM_STEP = 10
NEG_BIG = -1e30
VMEM_LIMIT_BYTES = 56 * 1024 * 1024
HI = lax.Precision.HIGHEST


def _params(*sem):
    return pltpu.CompilerParams(dimension_semantics=sem, vmem_limit_bytes=VMEM_LIMIT_BYTES)


def _pick(n, target, mult=16):
    if n <= target:
        return n
    for t in range(target - target % mult, 0, -mult):
        if n % t == 0:
            return t
    raise ValueError((n, target, mult))


def _sig(x):
    return 0.5 * jnp.tanh(0.5 * x) + 0.5


def _silu(x):
    return x * _sig(x)


def _dsilu(x):
    s = _sig(x)
    return s * (1.0 + x * (1.0 - s))


_GELU_C = math.sqrt(2.0 / math.pi)


def _gelu(x):
    return 0.5 * x * (1.0 + jnp.tanh(_GELU_C * (x + 0.044715 * x * x * x)))


def _dgelu(x):
    t = jnp.tanh(_GELU_C * (x + 0.044715 * x * x * x))
    return 0.5 * (1.0 + t) + 0.5 * x * (1.0 - t * t) * _GELU_C * (1.0 + 3.0 * 0.044715 * x * x)


def _softplus(x):
    return jnp.maximum(x, 0.0) + jnp.log1p(jnp.exp(-jnp.abs(x)))


def _sum0(v):
    return jnp.sum(v, axis=0, keepdims=True)


def _rms(h):
    r = lax.rsqrt(jnp.mean(h * h, axis=-1, keepdims=True) + EPS)
    return h * r, r


def _dot(a, b, dims=((1,), (0,)), precision=None):
    return lax.dot_general(a, b, (dims, ((), ())), preferred_element_type=F32, precision=precision)


_NT = ((1,), (1,))
_TN = ((0,), (0,))


def _rowwise(name, fn, n_rows, rows, consts, outs, accs=(), *, tm, nseg=1, seg_blocks=0):
    assert n_rows % tm == 0
    if nseg == 2:
        assert seg_blocks > 0
        seg = lambda i: jnp.where(i < seg_blocks, 0, 1)
    else:
        seg = lambda i: 0
    in_specs, args, lacking = [], [], []
    for r in rows:
        arr, width, cb, off = r if isinstance(r, tuple) else (r, r.shape[1], 0, 0)
        in_specs.append(pl.BlockSpec((tm, width), lambda i, cb=cb, off=off: (jnp.maximum(i + off, 0), cb)))
        args.append(arr)
        lacking.append(-off if off < 0 else 0)
    for kind, arr in consts:
        if kind == "seg":
            assert arr.shape[0] == nseg and arr.shape[1] == 1, arr.shape
            in_specs.append(pl.BlockSpec((None, 1, arr.shape[2]), lambda i: (seg(i), 0, 0)))
        else:
            in_specs.append(pl.BlockSpec(arr.shape, lambda i: (0, 0)))
        args.append(arr)
    out_shape = [jax.ShapeDtypeStruct((n_rows, w), dt) for w, dt in outs]
    out_specs = [pl.BlockSpec((tm, w), lambda i: (i, 0)) for w, _ in outs]
    out_shape += [jax.ShapeDtypeStruct((nseg, 1, w), F32) for w in accs]
    out_specs += [pl.BlockSpec((None, 1, w), lambda i: (seg(i), 0, 0)) for w in accs]
    n_in, n_out, n_acc = len(args), len(outs), len(accs)

    def kern(*refs):
        i = pl.program_id(0)
        ins = [r[...] for r in refs[:n_in]]
        for k, lack in enumerate(lacking):
            if lack:
                ins[k] = jnp.where(i >= lack, ins[k], jnp.zeros_like(ins[k]))
        res, terms = fn(*ins)
        for ref, v in zip(refs[n_in:n_in + n_out], res):
            ref[...] = v.astype(ref.dtype)
        if n_acc:
            sums = [_sum0(v) for v in terms]
            first = (i == 0) | (i == seg_blocks) if nseg == 2 else (i == 0)
            acc_refs = refs[n_in + n_out:]

            @pl.when(first)
            def _():
                for ref, v in zip(acc_refs, sums):
                    ref[...] = v

            @pl.when(jnp.logical_not(first))
            def _():
                for ref, v in zip(acc_refs, sums):
                    ref[...] += v

    res = pl.pallas_call(
        kern, name=name, grid=(n_rows // tm,), in_specs=in_specs, out_specs=out_specs, out_shape=out_shape,
        compiler_params=_params("arbitrary"),
    )(*args)
    return res


def _pre_fwd_fn(h, g, shift, scale):
    hh, _ = _rms(h)
    return (hh * g * (1.0 + scale) + shift,), ()


def _pre_bwd_fn(du, h, dres, g, scale):
    hh, r = _rms(h)
    n = hh * g
    dn = du * (1.0 + scale)
    dhh = dn * g
    dh = dres + r * (dhh - hh * jnp.mean(dhh * hh, axis=-1, keepdims=True))
    return (dh,), (du, du * n, dn * hh)


def _post_fwd_fn(weight, h, y, g, gate):
    yh, _ = _rms(y)
    return (h + weight * gate * (yh * g),), ()


def _out_post_fn(weight, y, h, g, gate):
    return (y,) + _post_fwd_fn(weight, h, y, g, gate)[0], ()


def _post_bwd_fn(weight, dh, y, g, gate):
    yh, r = _rms(y)
    dr = dh * weight
    dyh = dr * gate * g
    dy = r * (dyh - yh * jnp.mean(dyh * yh, axis=-1, keepdims=True))
    return (dy,), (dr * yh * g, dr * gate * yh)


def _glu_bwd_fn(ds, a, b):
    a = a.astype(F32)
    b = b.astype(F32)
    sg = _sig(a)
    da = ds * b * (sg * (1.0 + a * (1.0 - sg)))
    db = ds * (a * sg)
    return (jnp.concatenate([da, db], axis=1),), ()


def _loss_fn(y, t):
    diff = y - t
    return (diff * (1.0 / D_MODEL),), (diff * diff,)


def _ssd_y(yf, yb, xs, z, dvec):
    y = yf + yb + dvec * xs
    return y, y * _silu(z)


def _ssdgate_fwd_fn(yf, yb, xs, z, dvec, ng):
    _, yg = _ssd_y(yf, yb, xs, z, dvec)
    parts = []
    for g in range(SSD_GROUPS):
        sl = slice(g * 256, (g + 1) * 256)
        parts.append(_rms(yg[:, sl])[0])
    return (jnp.concatenate(parts, axis=1) * ng,), ()


def _ssdgate_bwd_fn(dyn, yf, yb, xs, z, dvec, ng):
    y, yg = _ssd_y(yf, yb, xs, z, dvec)
    dyg_parts, ygh_parts = [], []
    for g in range(SSD_GROUPS):
        sl = slice(g * 256, (g + 1) * 256)
        ygh, r = _rms(yg[:, sl])
        d = dyn[:, sl] * ng[:, sl]
        dyg_parts.append(r * (d - ygh * jnp.mean(d * ygh, axis=-1, keepdims=True)))
        ygh_parts.append(ygh)
    dyg = jnp.concatenate(dyg_parts, axis=1)
    ygh = jnp.concatenate(ygh_parts, axis=1)
    dy = dyg * _silu(z)
    dz = dyg * y * _dsilu(z)
    return (dy, dz), (dyn * ygh, dy * xs)


def _ln_stats(v):
    mu = jnp.mean(v, axis=-1, keepdims=True)
    vc = v - mu
    r = lax.rsqrt(jnp.mean(vc * vc, axis=-1, keepdims=True) + EPS)
    return vc * r, r


def _gm_act_fwd_fn(p, vg, vb):
    gu = _gelu(p[:, :GM_INNER])
    gvh, _ = _ln_stats(_gelu(p[:, GM_INNER:]))
    return (gu, gvh * vg + vb), ()


def _gm_act_bwd_fn(p, dgu, dgvn, vg):
    pu = p[:, :GM_INNER]
    pv = p[:, GM_INNER:]
    gvh, r = _ln_stats(_gelu(pv))
    dgvh = dgvn * vg
    dgv = r * (dgvh - jnp.mean(dgvh, axis=-1, keepdims=True) - gvh * jnp.mean(dgvh * gvh, axis=-1, keepdims=True))
    dp = jnp.concatenate([dgu * _dgelu(pu), dgv * _dgelu(pv)], axis=1)
    return (dp,), (dgvn * gvh, dgvn)


def _mm(a, b, *, out_dtype, name, tm=1088, tn=1024, tk=1408, add=None, rhs_t=False, n=None, b_off=(0, 0)):
    m, k = a.shape
    if n is None:
        n, k2 = b.shape if rhs_t else b.shape[::-1]
        assert k == k2
    tm, tn, tk = _pick(m, tm), _pick(n, tn, 128), _pick(k, tk, 128)
    o0, o1 = b_off
    nk = k // tk
    dims = _NT if rhs_t else ((1,), (0,))

    def kern(*refs):
        a_ref, b_ref = refs[:2]
        add_ref = refs[2] if add is not None else None
        o_ref = refs[3] if add is not None else refs[2]

        def finish(r):
            if add is not None:
                r = r + add_ref[...]
            o_ref[...] = r.astype(o_ref.dtype)

        p = _dot(a_ref[...], b_ref[...], dims)
        if nk == 1:
            finish(p)
            return
        acc_ref = refs[-1]
        kk = pl.program_id(2)

        @pl.when(kk == 0)
        def _():
            acc_ref[...] = p

        @pl.when((kk > 0) & (kk < nk - 1))
        def _():
            acc_ref[...] += p

        @pl.when(kk == nk - 1)
        def _():
            finish(acc_ref[...] + p)

    if rhs_t:
        b_spec = pl.BlockSpec((tn, tk), lambda i, j, kk: (j + o0, kk + o1))
    else:
        b_spec = pl.BlockSpec((tk, tn), lambda i, j, kk: (kk + o0, j + o1))
    in_specs = [pl.BlockSpec((tm, tk), lambda i, j, kk: (i, kk)), b_spec]
    args = [a, b]
    if add is not None:
        in_specs.append(pl.BlockSpec((tm, tn), lambda i, j, kk: (i, j)))
        args.append(add)
    return pl.pallas_call(
        kern, name=name, grid=(m // tm, n // tn, nk), in_specs=in_specs,
        out_specs=pl.BlockSpec((tm, tn), lambda i, j, kk: (i, j)),
        out_shape=jax.ShapeDtypeStruct((m, n), out_dtype),
        scratch_shapes=[pltpu.VMEM((tm, tn), F32)] if nk > 1 else [],
        compiler_params=_params("parallel", "parallel", "arbitrary"),
    )(*args)


def _mm_rows(a, b, fn, rows, consts, outs, accs=(), *, name, tm=544, tk=1408, rhs_t=False, n_ctx=0):
    m, k = a.shape
    n = b.shape[0] if rhs_t else b.shape[1]
    tm, tk = _pick(m, tm), _pick(k, tk, 128)
    nk = k // tk
    dims = _NT if rhs_t else ((1,), (0,))
    n_rows, n_const, n_out, n_acc = len(rows), len(consts), len(outs), len(accs)

    def kern(*refs):
        a_ref, b_ref = refs[:2]
        row_refs = refs[2:2 + n_rows]
        const_refs = refs[2 + n_rows:2 + n_rows + n_const]
        out_refs = refs[2 + n_rows + n_const:2 + n_rows + n_const + n_out]
        acc_refs = refs[2 + n_rows + n_const + n_out:2 + n_rows + n_const + n_out + n_acc]
        i, kk = pl.program_id(0), pl.program_id(1)

        def finish(p):
            is_ctx = (i * tm + lax.broadcasted_iota(jnp.int32, (tm, 1), 0)) < n_ctx
            cvals = []
            for (kind, arr), ref in zip(consts, const_refs):
                if kind == "seg":
                    cvals.append(jnp.where(is_ctx, ref[0], ref[1]) if arr.shape[0] == 2 else ref[0])
                else:
                    cvals.append(ref[...])
            res, terms = fn(p, *[r[...] for r in row_refs], *cvals)
            for ref, v in zip(out_refs, res):
                ref[...] = v.astype(ref.dtype)
            for ref, v in zip(acc_refs, terms):
                s_all = _sum0(v)
                s_ctx = _sum0(jnp.where(is_ctx, v, 0.0)) if n_ctx else jnp.zeros_like(s_all)
                both = jnp.concatenate([s_ctx, s_all - s_ctx], axis=0)[:, None, :]

                @pl.when(i == 0)
                def _():
                    ref[...] = both

                @pl.when(i > 0)
                def _():
                    ref[...] += both

        p = _dot(a_ref[...], b_ref[...], dims)
        if nk == 1:
            finish(p)
            return
        scr = refs[-1]

        @pl.when(kk == 0)
        def _():
            scr[...] = p

        @pl.when((kk > 0) & (kk < nk - 1))
        def _():
            scr[...] += p

        @pl.when(kk == nk - 1)
        def _():
            finish(scr[...] + p)

    b_spec = pl.BlockSpec((n, tk), lambda i, kk: (0, kk)) if rhs_t else pl.BlockSpec((tk, n), lambda i, kk: (kk, 0))
    in_specs = [pl.BlockSpec((tm, tk), lambda i, kk: (i, kk)), b_spec]
    in_specs += [pl.BlockSpec((tm, r.shape[1]), lambda i, kk: (i, 0)) for r in rows]
    for kind, arr in consts:
        in_specs.append(pl.BlockSpec(arr.shape, (lambda i, kk: (0, 0, 0)) if kind == "seg" else (lambda i, kk: (0, 0))))
    out_shape = [jax.ShapeDtypeStruct((m, w), dt) for w, dt in outs]
    out_specs = [pl.BlockSpec((tm, w), lambda i, kk: (i, 0)) for w, _ in outs]
    out_shape += [jax.ShapeDtypeStruct((2, 1, w), F32) for w in accs]
    out_specs += [pl.BlockSpec((2, 1, w), lambda i, kk: (0, 0, 0)) for w in accs]
    return pl.pallas_call(
        kern, name=name, grid=(m // tm, nk), in_specs=in_specs, out_specs=out_specs, out_shape=out_shape,
        scratch_shapes=[pltpu.VMEM((tm, n), F32)] if nk > 1 else [],
        compiler_params=_params("arbitrary", "arbitrary"),
    )(a, b, *rows, *[arr for _, arr in consts])


def _mm_glu(u, win, *, name, tm=2176, tn=256):
    m, k = u.shape
    n = win.shape[1] // 2
    tm, tn = _pick(m, tm), _pick(n, tn, 128)
    nj = n // tn

    def kern(u_ref, wa_ref, wb_ref, s_ref, a_ref, b_ref):
        uu = u_ref[...]
        a = jnp.dot(uu, wa_ref[...], preferred_element_type=F32)
        b = jnp.dot(uu, wb_ref[...], preferred_element_type=F32)
        s_ref[...] = (_silu(a) * b).astype(BF16)
        a_ref[...] = a.astype(BF16)
        b_ref[...] = b.astype(BF16)

    ospec = pl.BlockSpec((tm, tn), lambda i, j: (i, j))
    return pl.pallas_call(
        kern, name=name, grid=(m // tm, nj),
        in_specs=[pl.BlockSpec((tm, k), lambda i, j: (i, 0)), pl.BlockSpec((k, tn), lambda i, j: (0, j)),
                  pl.BlockSpec((k, tn), lambda i, j: (0, nj + j))],
        out_specs=[ospec, ospec, ospec],
        out_shape=[jax.ShapeDtypeStruct((m, n), BF16)] * 3,
        compiler_params=_params("parallel", "parallel"),
    )(u, win, win)


def _mm_tn(a, b, *, name, tm=1024, tn=1024, tk=1088):
    t, m = a.shape
    t2, n = b.shape
    assert t == t2
    tm, tn, tk = _pick(m, tm, 128), _pick(n, tn, 128), _pick(t, tk)
    nk = t // tk

    def kern(a_ref, b_ref, o_ref):
        kk = pl.program_id(2)

        @pl.when(kk == 0)
        def _():
            o_ref[...] = jnp.zeros_like(o_ref)

        o_ref[...] += _dot(a_ref[...], b_ref[...], _TN)

    return pl.pallas_call(
        kern, name=name, grid=(m // tm, n // tn, nk),
        in_specs=[pl.BlockSpec((tk, tm), lambda i, j, kk: (kk, i)), pl.BlockSpec((tk, tn), lambda i, j, kk: (kk, j))],
        out_specs=pl.BlockSpec((tm, tn), lambda i, j, kk: (i, j)),
        out_shape=jax.ShapeDtypeStruct((m, n), F32),
        compiler_params=_params("parallel", "parallel", "arbitrary"),
    )(a, b)


def _mm_f32(a, b, *, name, silu_a=False, bias=None):
    m, k = a.shape
    n = b.shape[1]

    def kern(*refs):
        if bias is None:
            a_ref, b_ref, o_ref = refs
        else:
            a_ref, b_ref, bias_ref, o_ref = refs
        av = a_ref[...]
        if silu_a:
            av = _silu(av)
        r = jnp.dot(av, b_ref[...], preferred_element_type=F32, precision=HI)
        if bias is not None:
            r = r + bias_ref[...]
        o_ref[...] = r

    args = [a, b] + ([] if bias is None else [bias])
    return pl.pallas_call(kern, name=name, out_shape=jax.ShapeDtypeStruct((m, n), F32),
                          compiler_params=pltpu.CompilerParams(vmem_limit_bytes=VMEM_LIMIT_BYTES))(*args)


CONV_WIN = 32


def _conv_windows(n, n_ctx):
    assert n_ctx % CONV_WIN == 0 and n_ctx >= CONV_WIN and n - n_ctx >= CONV_WIN
    return (0, n_ctx - CONV_WIN // 2, n - CONV_WIN)


def _tap_outside(r0, s, n, n_ctx):
    t = r0 + lax.broadcasted_iota(jnp.int32, (CONV_WIN, 1), 0)
    lo = jnp.where(t < n_ctx, 0, n_ctx)
    hi = jnp.where(t < n_ctx, n_ctx, n)
    return jnp.where((t + s >= lo) & (t + s < hi), 0.0, 1.0)


def _rolled(v, s):
    return v if s == 0 else pltpu.roll(v, (-s) % v.shape[0], 0)


def _conv_fwd(xp, w8, b, *, n_ctx, name, cb=256):
    n, c = xp.shape
    half = SSD_CONV // 2

    def kern(x_ref, w_ref, b_ref, cpre_ref, act_ref):
        x = x_ref[...]
        acc = jnp.zeros_like(x) + b_ref[...]
        rolled = {}
        for k in range(SSD_CONV):
            rolled[k] = _rolled(x, k - half)
            acc = acc + rolled[k] * w_ref[k:k + 1, :]
        cpre_ref[...] = acc
        act_ref[...] = _silu(acc)
        for r0 in _conv_windows(n, n_ctx):
            rows = slice(r0, r0 + CONV_WIN)
            fix = acc[rows]
            for k in range(SSD_CONV):
                if k != half:
                    fix = fix - rolled[k][rows] * w_ref[k:k + 1, :] * _tap_outside(r0, k - half, n, n_ctx)
            cpre_ref[rows, :] = fix
            act_ref[rows, :] = _silu(fix)

    spec = pl.BlockSpec((n, cb), lambda j: (0, j))
    return pl.pallas_call(
        kern, name=name, grid=(c // cb,),
        in_specs=[spec, pl.BlockSpec((8, cb), lambda j: (0, j)), pl.BlockSpec((1, cb), lambda j: (0, j))],
        out_specs=[spec, spec], out_shape=[jax.ShapeDtypeStruct((n, c), F32)] * 2,
        compiler_params=_params("parallel"),
    )(xp, w8, b)


def _conv_bwd(d1, d2, cpre, xp, w8, *, n_ctx, name, cb=128):
    n, c = xp.shape
    half = SSD_CONV // 2

    def kern(d1_ref, d2_ref, cpre_ref, x_ref, w_ref, dx_ref, dw_ref, db_ref):
        g = (d1_ref[...] + d2_ref[...]) * _dsilu(cpre_ref[...])
        x = x_ref[...]
        dx = jnp.zeros_like(g)
        dw_ref[...] = jnp.zeros_like(dw_ref)
        g_rolled = {}
        for k in range(SSD_CONV):
            s = k - half
            g_rolled[k] = _rolled(g, -s)
            dx = dx + g_rolled[k] * w_ref[k:k + 1, :]
            xr = _rolled(x, s)
            dw = _sum0(g * xr)
            if s != 0:
                for r0 in _conv_windows(n, n_ctx):
                    rows = slice(r0, r0 + CONV_WIN)
                    dw = dw - _sum0(g[rows] * xr[rows] * _tap_outside(r0, s, n, n_ctx))
            dw_ref[k:k + 1, :] = dw
        dx_ref[...] = dx.astype(BF16)
        for r0 in _conv_windows(n, n_ctx):
            rows = slice(r0, r0 + CONV_WIN)
            fix = dx[rows]
            for k in range(SSD_CONV):
                if k != half:
                    fix = fix - g_rolled[k][rows] * w_ref[k:k + 1, :] * _tap_outside(r0, half - k, n, n_ctx)
            dx_ref[rows, :] = fix.astype(BF16)
        db_ref[...] = _sum0(g)

    spec = pl.BlockSpec((n, cb), lambda j: (0, j))
    return pl.pallas_call(
        kern, name=name, grid=(c // cb,),
        in_specs=[spec, spec, spec, spec, pl.BlockSpec((8, cb), lambda j: (0, j))],
        out_specs=[spec, pl.BlockSpec((8, cb), lambda j: (0, j)), pl.BlockSpec((1, cb), lambda j: (0, j))],
        out_shape=[jax.ShapeDtypeStruct((n, c), BF16), jax.ShapeDtypeStruct((8, c), F32),
                   jax.ShapeDtypeStruct((1, c), F32)],
        compiler_params=_params("parallel"),
    )(d1, d2, cpre, xp, w8)


def _chunk_of(s, nc, n_ctx_chunks, rev):
    if not rev:
        return s
    return jnp.where(s < n_ctx_chunks, n_ctx_chunks - 1 - s, nc - 1 - (s - n_ctx_chunks))


def _scan_common(dt_raw, dtT_raw, bias_r, bias_c, alog_r, alog_c, rev):
    ii = lax.broadcasted_iota(jnp.int32, (CHUNK, CHUNK), 0)
    jj = lax.broadcasted_iota(jnp.int32, (CHUNK, CHUNK), 1)
    tri = (jj >= ii) if rev else (jj <= ii)
    tri_t = (ii >= jj) if rev else (ii <= jj)
    a_r = -jnp.exp(alog_r)
    a_c = -jnp.exp(alog_c)
    dt = _softplus(dt_raw + bias_r)
    dt_t = _softplus(dtT_raw + bias_c)
    al = dt * a_r
    acum = _dot(tri.astype(F32), al, precision=HI)
    acum_t = _dot(dt_t * a_c, tri_t.astype(F32), precision=HI)
    atot = _sum0(al)
    return tri, tri_t, a_r, dt, acum, acum_t, atot


def _head_spread():
    return jnp.repeat(jnp.eye(SSD_HEADS, dtype=BF16), SSD_HEAD_DIM, axis=1)


def _dot_sel(v, sel):
    hi = v.astype(BF16)
    lo = (v - hi.astype(F32)).astype(BF16)
    return _dot(hi, sel) + _dot(lo, sel)


def _ssd_scan_fwd(xbc, dt_raw, dtT_raw, bias_r, bias_c, alog_r, alog_c, *, rev, n_ctx_chunks, name):
    n = xbc.shape[0]
    nc = n // CHUNK
    cidx = functools.partial(_chunk_of, nc=nc, n_ctx_chunks=n_ctx_chunks, rev=rev)

    def kern(xs_ref, b_ref, c_ref, dt_ref, dtT_ref, br_ref, bc_ref, ar_ref, ac_ref, e_ref, y_ref, hs_ref, h_scr):
        @pl.when(pl.program_id(0) == 0)
        def _():
            h_scr[...] = jnp.zeros_like(h_scr)

        tri, _, _, dt, acum, acum_t, atot = _scan_common(
            dt_ref[...], dtT_ref[...], br_ref[...], bc_ref[...], ar_ref[...], ac_ref[...], rev)
        etot = jnp.exp(atot)
        spread = lambda v: _dot_sel(v, e_ref[...])
        xdt_all = xs_ref[...] * spread(dt)
        eax = spread(jnp.exp(acum))
        xdw_all = xdt_all * spread(jnp.exp(atot - acum))
        hs_ref[...] = h_scr[...]
        for g in range(SSD_GROUPS):
            gs = slice(g * 256, (g + 1) * 256)
            bg = b_ref[:, g * SSD_STATE:(g + 1) * SSD_STATE].astype(BF16)
            cg = c_ref[:, g * SSD_STATE:(g + 1) * SSD_STATE].astype(BF16)
            cb = _dot(cg, bg, _NT)
            h4 = h_scr[gs, :]
            ys = []
            for k in range(SSD_HPG):
                h = g * SSD_HPG + k
                lmat = jnp.exp(jnp.where(tri, acum[:, h:h + 1] - acum_t[h:h + 1, :], NEG_BIG))
                xdt_h = xdt_all[:, h * SSD_HEAD_DIM:(h + 1) * SSD_HEAD_DIM].astype(BF16)
                ys.append(_dot((cb * lmat).astype(BF16), xdt_h))
            y_ref[:, gs] = jnp.concatenate(ys, axis=1) + _dot(cg, h4.astype(BF16), _NT) * eax[:, gs]
            s4 = _dot(xdw_all[:, gs].astype(BF16), bg, _TN)
            for k in range(SSD_HPG):
                h = g * SSD_HPG + k
                rs = slice(h * SSD_HEAD_DIM, (h + 1) * SSD_HEAD_DIM)
                h_scr[rs, :] = h4[k * SSD_HEAD_DIM:(k + 1) * SSD_HEAD_DIM] * etot[:, h:h + 1] + \
                    s4[k * SSD_HEAD_DIM:(k + 1) * SSD_HEAD_DIM]

    nh = SSD_HEADS
    small = lambda shape: pl.BlockSpec(shape, lambda s: (0, 0))
    return pl.pallas_call(
        kern, name=name, grid=(nc,),
        in_specs=[pl.BlockSpec((CHUNK, SSD_INNER), lambda s: (cidx(s), 0)),
                  pl.BlockSpec((CHUNK, 1024), lambda s: (cidx(s), 2)),
                  pl.BlockSpec((CHUNK, 1024), lambda s: (cidx(s), 3)),
                  pl.BlockSpec((CHUNK, nh), lambda s: (cidx(s), 0)),
                  pl.BlockSpec((nh, CHUNK), lambda s: (0, cidx(s))),
                  small((1, nh)), small((nh, 1)), small((1, nh)), small((nh, 1)), small((nh, SSD_INNER))],
        out_specs=[pl.BlockSpec((CHUNK, SSD_INNER), lambda s: (cidx(s), 0)),
                   pl.BlockSpec((None, SSD_INNER, SSD_STATE), lambda s: (s, 0, 0))],
        out_shape=[jax.ShapeDtypeStruct((n, SSD_INNER), F32),
                   jax.ShapeDtypeStruct((nc, SSD_INNER, SSD_STATE), F32)],
        scratch_shapes=[pltpu.VMEM((SSD_INNER, SSD_STATE), F32)],
        compiler_params=_params("arbitrary"),
    )(xbc, xbc, xbc, dt_raw, dtT_raw, bias_r, bias_c, alog_r, alog_c, _head_spread())


def _ssd_scan_bwd(dy, xbc, hs, dt_raw, dtT_raw, bias_r, bias_c, alog_r, alog_c, dvec, *, rev, n_ctx_chunks,
                  direct, name):
    n = xbc.shape[0]
    nc = n // CHUNK
    nh = SSD_HEADS
    step_of = lambda r: nc - 1 - r
    cidx = lambda r: _chunk_of(step_of(r), nc, n_ctx_chunks, rev)

    def kern(dy_ref, xs_ref, b_ref, c_ref, hs_ref, dt_ref, dtT_ref, br_ref, bc_ref, ar_ref, ac_ref, dv_ref,
             e_ref, et_ref, dx_ref, ddt_ref, dal_ref, dbias_ref, dh_scr):
        @pl.when(pl.program_id(0) == 0)
        def _():
            dh_scr[...] = jnp.zeros_like(dh_scr)
            dal_ref[...] = jnp.zeros_like(dal_ref)
            dbias_ref[...] = jnp.zeros_like(dbias_ref)

        tri, tri_t, a_r, dt, acum, acum_t, atot = _scan_common(
            dt_ref[...], dtT_ref[...], br_ref[...], bc_ref[...], ar_ref[...], ac_ref[...], rev)
        etot = jnp.exp(atot)
        spread = lambda v: _dot_sel(v, e_ref[...])
        gather = lambda v: _dot_sel(v, et_ref[...])
        xs_all = xs_ref[...]
        dy_all = dy_ref[...]
        dtx = spread(dt)
        eax = spread(jnp.exp(acum))
        decx = spread(jnp.exp(atot - acum))
        xdt_all = xs_all * dtx
        xdw_all = xdt_all * decx
        dyo_all = dy_all * eax
        lane = lax.broadcasted_iota(jnp.int32, (CHUNK, nh), 1)
        lane1 = lax.broadcasted_iota(jnp.int32, (1, nh), 1)
        sub = lax.broadcasted_iota(jnp.int32, (nh, CHUNK), 0)
        g_rows = jnp.zeros((CHUNK, nh), F32)
        g_cols = jnp.zeros((nh, CHUNK), F32)
        dtot = jnp.zeros((1, nh), F32)
        q_col, q_e, q_dt = [], [], []
        for g in range(SSD_GROUPS):
            gs = slice(g * 256, (g + 1) * 256)
            bg = b_ref[:, g * SSD_STATE:(g + 1) * SSD_STATE].astype(BF16)
            cg = c_ref[:, g * SSD_STATE:(g + 1) * SSD_STATE].astype(BF16)
            cb = _dot(cg, bg, _NT)
            hs4 = hs_ref[gs, :]
            dh4 = dh_scr[gs, :]
            hs4_bf = hs4.astype(BF16)
            dh4_bf = dh4.astype(BF16)
            dy4 = dy_all[:, gs]
            dy4_bf = dy4.astype(BF16)
            xdt4_bf = xdt_all[:, gs].astype(BF16)
            xdw4 = xdw_all[:, gs]
            xdw4_bf = xdw4.astype(BF16)
            dyo4_bf = dyo_all[:, gs].astype(BF16)
            yoff4 = _dot(cg, hs4_bf, _NT) * eax[:, gs]
            dcg = _dot(dyo4_bf, hs4_bf)
            dh_new4 = _dot(dyo4_bf, cg, _TN)
            bdh4 = _dot(bg, dh4_bf, _NT)
            dbg = _dot(xdw4_bf, dh4_bf)
            e4 = xdw4 * bdh4
            q_col.append(dy4 * yoff4 - e4)
            q_e.append(e4)
            hsum = jnp.sum(dh4 * hs4, axis=1, keepdims=True)
            dcb = jnp.zeros((CHUNK, CHUNK), F32)
            dxdts = []
            for k in range(SSD_HPG):
                h = g * SSD_HPG + k
                ks = slice(k * SSD_HEAD_DIM, (k + 1) * SSD_HEAD_DIM)
                lmat = jnp.exp(jnp.where(tri, acum[:, h:h + 1] - acum_t[h:h + 1, :], NEG_BIG))
                mf = cb * lmat
                dm = _dot(dy4_bf[:, ks], xdt4_bf[:, ks], _NT)
                dcb = dcb + dm * lmat
                gmat = dm * mf
                g_rows = g_rows + jnp.where(lane == h, jnp.sum(gmat, axis=1, keepdims=True), 0.0)
                g_cols = g_cols + jnp.where(sub == h, _sum0(gmat), 0.0)
                dxdts.append(_dot(mf.astype(BF16), dy4_bf[:, ks], _TN))
                et = etot[:, h:h + 1]
                dtot = dtot + jnp.where(lane1 == h, _sum0(hsum[ks]) * et, 0.0)
                dh_scr[h * SSD_HEAD_DIM:(h + 1) * SSD_HEAD_DIM, :] = dh4[ks] * et + dh_new4[ks]
            dxdt4 = jnp.concatenate(dxdts, axis=1) + bdh4 * decx[:, gs]
            q_dt.append(dxdt4 * xs_all[:, gs])
            dx4 = dxdt4 * dtx[:, gs]
            if direct:
                dx4 = dx4 + dy4 * dv_ref[:, gs]
            dcb_bf = dcb.astype(BF16)
            dx_ref[:, gs] = dx4
            dx_ref[:, SSD_INNER + g * SSD_STATE:SSD_INNER + (g + 1) * SSD_STATE] = dbg + _dot(dcb_bf, cg, _TN)
            dx_ref[:, SSD_INNER + 1024 + g * SSD_STATE:SSD_INNER + 1024 + (g + 1) * SSD_STATE] = \
                dcg + _dot(dcb_bf, bg)
        e_heads = gather(jnp.concatenate(q_e, axis=1))
        dacum = gather(jnp.concatenate(q_col, axis=1)) + g_rows - g_cols.T
        dal = _dot(tri_t.astype(F32), dacum, precision=HI) + dtot + _sum0(e_heads)
        ddt = gather(jnp.concatenate(q_dt, axis=1)) + dal * a_r
        ddt_raw = ddt * _sig(dt_ref[...] + br_ref[...])
        ddt_ref[...] = ddt_raw
        dal_ref[...] += _sum0(dal * dt) * a_r
        dbias_ref[...] += _sum0(ddt_raw)

    small = lambda shape: pl.BlockSpec(shape, lambda r: (0, 0))
    return pl.pallas_call(
        kern, name=name, grid=(nc,),
        in_specs=[pl.BlockSpec((CHUNK, SSD_INNER), lambda r: (cidx(r), 0)),
                  pl.BlockSpec((CHUNK, SSD_INNER), lambda r: (cidx(r), 0)),
                  pl.BlockSpec((CHUNK, 1024), lambda r: (cidx(r), 2)),
                  pl.BlockSpec((CHUNK, 1024), lambda r: (cidx(r), 3)),
                  pl.BlockSpec((None, SSD_INNER, SSD_STATE), lambda r: (step_of(r), 0, 0)),
                  pl.BlockSpec((CHUNK, nh), lambda r: (cidx(r), 0)),
                  pl.BlockSpec((nh, CHUNK), lambda r: (0, cidx(r))),
                  small((1, nh)), small((nh, 1)), small((1, nh)), small((nh, 1)), small((1, SSD_INNER)),
                  small((nh, SSD_INNER)), small((SSD_INNER, nh))],
        out_specs=[pl.BlockSpec((CHUNK, SSD_CONV_DIM), lambda r: (cidx(r), 0)),
                   pl.BlockSpec((CHUNK, nh), lambda r: (cidx(r), 0)),
                   small((1, nh)), small((1, nh))],
        out_shape=[jax.ShapeDtypeStruct((n, SSD_CONV_DIM), F32), jax.ShapeDtypeStruct((n, nh), F32),
                   jax.ShapeDtypeStruct((1, nh), F32), jax.ShapeDtypeStruct((1, nh), F32)],
        scratch_shapes=[pltpu.VMEM((SSD_INNER, SSD_STATE), F32)],
        compiler_params=_params("arbitrary"),
    )(dy, xbc, xbc, xbc, hs, dt_raw, dtT_raw, bias_r, bias_c, alog_r, alog_c, dvec, _head_spread(),
      _head_spread().T)


def _gm_spatial_fwd(gu, gvn, ws, bst, *, name):
    n = gu.shape[0]

    def kern(gu_ref, gv_ref, ws_ref, bs_ref, o_ref):
        for g in range(GM_GROUPS):
            sl = slice(g * GM_GROUP_DIM, (g + 1) * GM_GROUP_DIM)
            s = _dot(ws_ref[g], gv_ref[:, sl]) + bs_ref[:, g:g + 1]
            o_ref[:, sl] = (gu_ref[:, sl] * s).astype(BF16)

    spec = pl.BlockSpec((CHUNK, GM_INNER), lambda i: (i, 0))
    return pl.pallas_call(
        kern, name=name, grid=(n // CHUNK,),
        in_specs=[spec, spec, pl.BlockSpec(ws.shape, lambda i: (0, 0, 0)), pl.BlockSpec(bst.shape, lambda i: (0, 0))],
        out_specs=spec, out_shape=jax.ShapeDtypeStruct((n, GM_INNER), BF16),
        compiler_params=_params("parallel"),
    )(gu, gvn, ws, bst)


def _gm_spatial_bwd(dt, gu, gvn, ws, wst, bst, *, name):
    n = gu.shape[0]

    def kern(dt_ref, gu_ref, gv_ref, ws_ref, wst_ref, bs_ref, dgu_ref, dgv_ref, dws_ref, dbs_ref):
        @pl.when(pl.program_id(0) == 0)
        def _():
            dws_ref[...] = jnp.zeros_like(dws_ref)
            dbs_ref[...] = jnp.zeros_like(dbs_ref)

        lane = lax.broadcasted_iota(jnp.int32, (CHUNK, GM_GROUPS), 1)
        dbs = jnp.zeros((CHUNK, GM_GROUPS), F32)
        for g in range(GM_GROUPS):
            sl = slice(g * GM_GROUP_DIM, (g + 1) * GM_GROUP_DIM)
            gv = gv_ref[:, sl]
            s = _dot(ws_ref[g], gv) + bs_ref[:, g:g + 1]
            d = dt_ref[:, sl]
            dgu_ref[:, sl] = d * s
            ds = d * gu_ref[:, sl]
            ds_bf = ds.astype(BF16)
            dws_ref[g] += _dot(ds_bf, gv, _NT)
            dgv_ref[:, sl] = _dot(wst_ref[g], ds_bf)
            dbs = dbs + jnp.where(lane == g, jnp.sum(ds, axis=1, keepdims=True), 0.0)
        dbs_ref[...] += dbs

    spec = pl.BlockSpec((CHUNK, GM_INNER), lambda i: (i, 0))
    wspec = pl.BlockSpec(ws.shape, lambda i: (0, 0, 0))
    bspec = pl.BlockSpec(bst.shape, lambda i: (0, 0))
    return pl.pallas_call(
        kern, name=name, grid=(n // CHUNK,),
        in_specs=[spec, spec, spec, wspec, wspec, bspec],
        out_specs=[spec, spec, wspec, bspec],
        out_shape=[jax.ShapeDtypeStruct((n, GM_INNER), F32), jax.ShapeDtypeStruct((n, GM_INNER), F32),
                   jax.ShapeDtypeStruct(ws.shape, F32), jax.ShapeDtypeStruct(bst.shape, F32)],
        compiler_params=_params("arbitrary"),
    )(dt, gu, gvn, ws, wst, bst)


def _adamw(parts, w, m, v, *, name, tm=256):
    ns, r, wd = parts.shape
    tm = _pick(r, tm, 8)

    def kern(p_ref, w_ref, m_ref, v_ref, g_ref, d_ref, nm_ref, nv_ref):
        g = p_ref[0].astype(F32)
        for s in range(1, ns):
            g = g + p_ref[s].astype(F32)
        m2 = ADAM_B1 * m_ref[...] + (1.0 - ADAM_B1) * g
        v2 = ADAM_B2 * v_ref[...] + (1.0 - ADAM_B2) * (g * g)
        m_hat = m2 / (1.0 - ADAM_B1 ** ADAM_STEP)
        v_hat = v2 / (1.0 - ADAM_B2 ** ADAM_STEP)
        g_ref[...] = g
        d_ref[...] = -ADAM_LR * (m_hat / (jnp.sqrt(v_hat) + ADAM_EPS) + ADAM_WD * w_ref[...])
        nm_ref[...] = m2
        nv_ref[...] = v2

    spec = pl.BlockSpec((tm, wd), lambda i: (i, 0))
    return pl.pallas_call(
        kern, name=name, grid=(r // tm,),
        in_specs=[pl.BlockSpec((ns, tm, wd), lambda i: (0, i, 0)), spec, spec, spec],
        out_specs=[spec] * 4, out_shape=[jax.ShapeDtypeStruct((r, wd), F32)] * 4,
        compiler_params=_params("parallel"),
    )(parts, w, m, v)


def _sum_slots(parts, *, name, scale_by=None):
    ns, r, wd = parts.shape

    def kern(*refs):
        p_ref, o_ref = refs[0], refs[-1]
        g = p_ref[0]
        for s in range(1, ns):
            g = g + p_ref[s]
        if scale_by is not None:
            g = g * _dsilu(refs[1][...])
        o_ref[...] = g

    args = [parts] + ([] if scale_by is None else [scale_by])
    return pl.pallas_call(kern, name=name, out_shape=jax.ShapeDtypeStruct((r, wd), F32),
                          compiler_params=pltpu.CompilerParams(vmem_limit_bytes=VMEM_LIMIT_BYTES))(*args)


def _mesh_pos():
    x, y, c = lax.axis_index("x"), lax.axis_index("y"), lax.axis_index("c")
    return x, y, c, 4 * x + 2 * y + c


def _flip(x, y, c, f):
    fx, fy, fc = (f >> 2) & 1, (f >> 1) & 1, f & 1
    px = 1 - x if fx else x
    py = 1 - y if fy else y
    pc = 1 - c if fc else c
    return (px, py, pc), 4 * px + 2 * py + pc


_HBM_SPEC = pl.BlockSpec(memory_space=pltpu.HBM)


def _exchange(arrays, *, scatter, name):
    na = len(arrays)
    if scatter:
        out_shape = [jax.ShapeDtypeStruct(a.shape, a.dtype) for a in arrays]
    else:
        out_shape = [jax.ShapeDtypeStruct((NDEV,) + a.shape, a.dtype) for a in arrays]

    out_shape.append(jax.ShapeDtypeStruct((8, 128), F32))

    def body(*refs):
        ins, outs = refs[:na], refs[na:2 * na]
        send_sems, recv_sems, local_sems = refs[2 * na + 1:]
        refs[2 * na][...] = jnp.zeros((8, 128), F32)
        x, y, c, me = _mesh_pos()
        copies = []
        for i in range(na):
            src_own = ins[i].at[me] if scatter else ins[i]
            lc = pltpu.make_async_copy(src_own, outs[i].at[me], local_sems.at[i])
            lc.start()
            copies.append(lc)
        sends = []
        for f in range(1, NDEV):
            peer, pidx = _flip(x, y, c, f)
            for i in range(na):
                k = i * (NDEV - 1) + f - 1
                src = ins[i].at[pidx] if scatter else ins[i]
                cp = pltpu.make_async_remote_copy(
                    src_ref=src, dst_ref=outs[i].at[me], send_sem=send_sems.at[k], recv_sem=recv_sems.at[k],
                    device_id=peer, device_id_type=pl.DeviceIdType.MESH)
                cp.start()
                sends.append(cp)
        for f in range(1, NDEV):
            peer, pidx = _flip(x, y, c, f)
            for i in range(na):
                k = i * (NDEV - 1) + f - 1
                src = ins[i].at[pidx] if scatter else ins[i]
                pltpu.make_async_remote_copy(
                    src_ref=src, dst_ref=outs[i].at[pidx], send_sem=send_sems.at[k], recv_sem=recv_sems.at[k],
                    device_id=peer, device_id_type=pl.DeviceIdType.MESH).wait_recv()
        for cp in sends:
            cp.wait_send()
        for lc in copies:
            lc.wait()

    res = pl.pallas_call(
        body, name=name, out_shape=out_shape, in_specs=[_HBM_SPEC] * na,
        out_specs=[_HBM_SPEC] * na + [pl.BlockSpec(memory_space=pltpu.VMEM)],
        scratch_shapes=[pltpu.SemaphoreType.DMA((na * (NDEV - 1),)), pltpu.SemaphoreType.DMA((na * (NDEV - 1),)),
                        pltpu.SemaphoreType.DMA((na,))],
        compiler_params=pltpu.CompilerParams(has_side_effects=True),
    )(*arrays)
    return res[:na], res[na][0, 0]


_SEM_SPEC = pl.BlockSpec(memory_space=pltpu.SEMAPHORE)
_DATAFLOW = pltpu.SideEffectType.DATAFLOW_SIDE_EFFECTING


def _split_copies(srcs, lands, send_sems, recv_sems, scatter, arriving):
    x, y, c, me = _mesh_pos()
    copies = []
    for i in range(len(srcs)):
        for f in range(1, NDEV):
            peer, pidx = _flip(x, y, c, f)
            k = i * (NDEV - 1) + f - 1
            copies.append(pltpu.make_async_remote_copy(
                src_ref=srcs[i].at[pidx] if scatter else srcs[i], dst_ref=lands[i].at[pidx if arriving else me],
                send_sem=send_sems.at[k], recv_sem=recv_sems.at[k], device_id=peer,
                device_id_type=pl.DeviceIdType.MESH))
    return copies


def _exchange_start(srcs, lands, *, scatter, name):
    na = len(srcs)
    nsem = na * (NDEV - 1)

    def body(*refs):
        ins_src, ins_land = refs[:na], refs[na:2 * na]
        send_sems, recv_sems = refs[2 * na], refs[2 * na + 1]
        token = refs[-1]
        for cp in _split_copies(ins_src, ins_land, send_sems, recv_sems, scatter, False):
            cp.start()
        token[...] = jnp.zeros_like(token)

    thru = [pltpu.HBM(a.shape, a.dtype) for a in list(srcs) + list(lands)]
    res = pl.pallas_call(
        body, name=name,
        out_shape=(pltpu.SemaphoreType.DMA((nsem,)), pltpu.SemaphoreType.DMA((nsem,)), *thru,
                   jax.ShapeDtypeStruct((8, 128), F32)),
        in_specs=[_HBM_SPEC] * (2 * na),
        out_specs=(_SEM_SPEC, _SEM_SPEC, *([_HBM_SPEC] * (2 * na)), pl.BlockSpec(memory_space=pltpu.VMEM)),
        input_output_aliases={i: 2 + i for i in range(2 * na)},
        compiler_params=pltpu.CompilerParams(has_side_effects=_DATAFLOW),
    )(*[pltpu.with_memory_space_constraint(a, pltpu.HBM) for a in list(srcs) + list(lands)])
    send_sems, recv_sems = res[0], res[1]
    return send_sems, recv_sems, res[2:2 + na], res[2 + na:2 + 2 * na], res[-1][0, 0]


def _exchange_wait(send_sems, recv_sems, srcs, lands, after, *, scatter, name):
    na = len(srcs)

    def body(*refs):
        ins_src, ins_land = refs[:na], refs[na:2 * na]
        s_sems, r_sems = refs[2 * na], refs[2 * na + 1]
        for cp in _split_copies(ins_src, ins_land, s_sems, r_sems, scatter, False):
            cp.wait_send()
        for cp in _split_copies(ins_src, ins_land, s_sems, r_sems, scatter, True):
            cp.wait_recv()

    thru = [pltpu.HBM(a.shape, a.dtype) for a in list(srcs) + list(lands)]
    res = pl.pallas_call(
        body, name=name, out_shape=tuple(thru),
        in_specs=[_HBM_SPEC] * (2 * na) + [_SEM_SPEC, _SEM_SPEC, pl.BlockSpec(memory_space=pl.ANY)],
        out_specs=tuple([_HBM_SPEC] * (2 * na)),
        input_output_aliases={i: i for i in range(2 * na)},
        compiler_params=pltpu.CompilerParams(has_side_effects=_DATAFLOW),
    )(*srcs, *lands, send_sems, recv_sems, after)
    return res[na:]


def _landing(block, me):
    buf = lax.empty((NDEV,) + block.shape, block.dtype)
    return lax.dynamic_update_slice_in_dim(buf, block[None], me, axis=0)


def _seg_kw(nseg, n_ctx, tm):
    return dict(nseg=nseg, seg_blocks=(n_ctx // tm if nseg == 2 else 0))


def _ffn_fwd(tag, h, gpre, gpost, shift, scale, gate, w, *, nseg, n_ctx, tm):
    n = h.shape[0]
    kw = _seg_kw(nseg, n_ctx, tm)
    (u,) = _rowwise(tag + "_pre", _pre_fwd_fn, n, [h], [("full", gpre), ("seg", shift), ("seg", scale)],
                    [(D_MODEL, BF16)], tm=tm, **kw)
    s, a, b = _mm_glu(u, w["win"], name=tag + "_glu")
    if "late" in w:
        w.update(w.pop("late")(s))
    y, ho = _mm_rows(s, w["wout"], functools.partial(_out_post_fn, 0.5), [h], [("full", gpost), ("seg", gate)],
                     [(D_MODEL, F32), (D_MODEL, F32)], name=tag + "_out", tk=FFN_DIM, n_ctx=n_ctx)
    return ho, dict(h=h, u=u, s=s, a=a, b=b, y=y)


def _ffn_bwd(tag, dho, sv, gpre, gpost, scale, gate, w, put, *, nseg, n_ctx, tm):
    n = dho.shape[0]
    kw = _seg_kw(nseg, n_ctx, tm)
    dy, dgate, dgpost = _rowwise(tag + "_postb", functools.partial(_post_bwd_fn, 0.5), n, [dho, sv["y"]],
                                 [("full", gpost), ("seg", gate)], [(D_MODEL, BF16)], [D_MODEL, D_MODEL], tm=tm, **kw)
    tok = put("w_out", _mm_tn(sv["s"], dy, name=tag + "_dwout", tm=1408, tn=1024))
    ds = _mm(dy, w["wout"], out_dtype=F32, name=tag + "_ds", tn=1408, rhs_t=True)
    (dp,) = _rowwise(tag + "_glub", _glu_bwd_fn, n, [ds, sv["a"], sv["b"]], [], [(2 * FFN_DIM, BF16)], tm=min(tm, 128))
    dwin = _mm_tn(sv["u"], dp, name=tag + "_dwin", tn=1408)
    if tok is not None:
        gpre = gpre + tok
    dh, dshift, dscale, dgpre = _mm_rows(dp, w["win"], _pre_bwd_fn, [sv["h"], dho], [("full", gpre), ("seg", scale)],
                                         [(D_MODEL, F32)], [D_MODEL, D_MODEL, D_MODEL], name=tag + "_du",
                                         rhs_t=True, n_ctx=n_ctx)
    return dh, put("w_in", dwin), dict(shift=dshift, scale=dscale, gate=dgate, gpre=dgpre, gpost=dgpost)


def _local_step(x, ctx, target, mods, norm_g, get_w, small, put_grad):
    t_len, n_ctx = x.shape[0], ctx.shape[0]
    n0 = t_len + n_ctx
    tm0 = _pick(n_ctx, 256, 8)
    tm1 = _pick(t_len, 256, 8)
    ncc = n_ctx // CHUNK
    g = {}

    def modrow(i, k, nseg):
        mc, mx = mods[i]
        if nseg == 2:
            return jnp.stack([mc[k], mx[k]])[:, None, :]
        return mx[k][None, None, :]

    pending = [None]

    def gvec(i, k):
        v = norm_g[i, k][None, :]
        if pending[0] is not None:
            v = v + pending[0]
            pending[0] = None
        return v

    xc = jnp.concatenate([ctx, x], axis=0)
    L0 = dict(nseg=2, n_ctx=n_ctx, tm=tm0)
    wts = dict(get_w("ffn00", xc))
    h1, sv_f01 = _ffn_fwd("l0f1", xc, gvec(0, 0), gvec(0, 1), modrow(0, 0, 2), modrow(0, 1, 2), modrow(0, 2, 2),
                          wts["ffn00"], **L0)
    kw0 = _seg_kw(2, n_ctx, tm0)
    (um0,) = _rowwise("l0m_pre", _pre_fwd_fn, n0, [h1], [("full", gvec(0, 2)), ("seg", modrow(0, 3, 2)),
                                                         ("seg", modrow(0, 4, 2))], [(D_MODEL, BF16)], tm=tm0, **kw0)
    wts.update(get_w("ssd", um0))
    z = _mm(um0, wts["ssd_win"], out_dtype=F32, name="ssd_z", n=SSD_INNER)
    xbc_pre = _mm(um0, wts["ssd_win"], out_dtype=F32, name="ssd_xbc", n=SSD_CONV_DIM, b_off=(0, SSD_INNER // 1024))
    dtr = _mm(um0, wts["ssd_wdt"], out_dtype=F32, name="ssd_dt")
    cpre, xbc = _conv_fwd(xbc_pre, small["conv_w8"], small["conv_b"], n_ctx=n_ctx, name="ssd_conv")
    nh = SSD_HEADS
    dt_dir = [dtr[:, :nh], dtr[:, nh:2 * nh]]
    dtT_dir = [d.T for d in dt_dir]
    bias_r = [small["dt_bias"][d][None, :] for d in range(2)]
    bias_c = [small["dt_bias"][d][:, None] for d in range(2)]
    alog_r = [small["a_log"][d][None, :] for d in range(2)]
    alog_c = [small["a_log"][d][:, None] for d in range(2)]
    ys, hss = [], []
    for d in range(2):
        yd, hsd = _ssd_scan_fwd(xbc, dt_dir[d], dtT_dir[d], bias_r[d], bias_c[d], alog_r[d], alog_c[d],
                                rev=(d == 1), n_ctx_chunks=ncc, name=f"ssd_scan{d}")
        ys.append(yd)
        hss.append(hsd)
    dvec = jnp.repeat(small["ssd_d"], SSD_HEAD_DIM)[None, :]
    ngv = small["ssd_norm_g"][None, :]
    gate_rows = [ys[0], ys[1], (xbc, SSD_INNER, 0, 0), z]
    lat = lambda r: (r[0], r[1], r[2], ncc) if isinstance(r, tuple) else (r, r.shape[1], 0, ncc)
    (yn,) = _rowwise("ssd_gate", _ssdgate_fwd_fn, t_len, [lat(r) for r in gate_rows],
                     [("full", dvec), ("full", ngv)], [(SSD_INNER, BF16)], tm=CHUNK)
    h1x = h1[n_ctx:]
    L1 = dict(nseg=1, n_ctx=0, tm=tm1)
    yo0, h2 = _mm_rows(yn, wts["ssd_wout"], functools.partial(_out_post_fn, 1.0), [h1x],
                       [("full", gvec(0, 3)), ("seg", modrow(0, 5, 1))], [(D_MODEL, F32), (D_MODEL, F32)],
                       name="ssd_out", tk=SSD_INNER)
    wts.update(get_w("ffn01", h2))
    h3, sv_f02 = _ffn_fwd("l0f2", h2, gvec(0, 4), gvec(0, 5), modrow(0, 6, 1), modrow(0, 7, 1), modrow(0, 8, 1),
                          wts["ffn01"], **L1)

    wts.update(get_w("ffn10", h3))
    h4, sv_f11 = _ffn_fwd("l1f1", h3, gvec(1, 0), gvec(1, 1), modrow(1, 0, 1), modrow(1, 1, 1), modrow(1, 2, 1),
                          wts["ffn10"], **L1)
    (um1,) = _rowwise("l1m_pre", _pre_fwd_fn, t_len, [h4], [("full", gvec(1, 2)), ("seg", modrow(1, 3, 1)),
                                                            ("seg", modrow(1, 4, 1))], [(D_MODEL, BF16)], tm=tm1)
    wts.update(get_w("gm", um1))
    p1 = _mm(um1, wts["gm_win"], out_dtype=F32, name="gm_in")
    vg = small["gm_v_g"][None, :]
    vb = small["gm_v_b"][None, :]
    gu, gvn = _rowwise("gm_act", _gm_act_fwd_fn, t_len, [p1], [("full", vg), ("full", vb)],
                       [(GM_INNER, F32), (GM_INNER, BF16)], tm=128)
    ws_bf = small["gm_w_s"].astype(BF16)
    wst_bf = jnp.swapaxes(small["gm_w_s"], 1, 2).astype(BF16)
    bst = small["gm_b_s"].T
    tgm = _gm_spatial_fwd(gu, gvn, ws_bf, bst, name="gm_spatial")
    yo1, h5 = _mm_rows(tgm, wts["gm_wout"], functools.partial(_out_post_fn, 1.0), [h4],
                       [("full", gvec(1, 3)), ("seg", modrow(1, 5, 1))], [(D_MODEL, F32), (D_MODEL, F32)],
                       name="gm_out", tk=GM_INNER)
    wts.update(get_w("ffn11", h5))
    h6, sv_f12 = _ffn_fwd("l1f2", h5, gvec(1, 4), gvec(1, 5), modrow(1, 6, 1), modrow(1, 7, 1), modrow(1, 8, 1),
                          wts["ffn11"], **L1)

    dh, loss_parts = _rowwise("loss", _loss_fn, t_len, [h6, target], [], [(D_MODEL, F32)], [D_MODEL], tm=tm1)

    zero = jnp.zeros((D_MODEL,), F32)
    dmx = [[zero] * N_MOD for _ in range(2)]
    dmc = [[zero] * N_MOD for _ in range(2)]
    dng = [[zero] * 6 for _ in range(2)]

    def put_mod(i, k, acc):
        if acc.shape[0] == 2:
            dmc[i][k] = dmc[i][k] + acc[0, 0]
            dmx[i][k] = dmx[i][k] + acc[1, 0]
        else:
            dmx[i][k] = dmx[i][k] + acc[0, 0]

    def put_g(i, k, acc):
        dng[i][k] = dng[i][k] + jnp.sum(acc[:, 0], axis=0)

    def ffn_back(tag, i, j, dho, sv, w, lay):
        nseg = lay["nseg"]
        base = 0 if j == 0 else 6
        gi = 0 if j == 0 else 4
        dh_in, pending[0], s = _ffn_bwd(tag, dho, sv, gvec(i, gi), gvec(i, gi + 1), modrow(i, base + 1, nseg),
                                        modrow(i, base + 2, nseg), w, functools.partial(put_grad, f"ffn{i}{j}"), **lay)
        put_mod(i, base, s["shift"])
        put_mod(i, base + 1, s["scale"])
        put_mod(i, base + 2, s["gate"])
        put_g(i, gi, s["gpre"])
        put_g(i, gi + 1, s["gpost"])
        return dh_in

    dh = ffn_back("l1f2", 1, 1, dh, sv_f12, wts["ffn11"], L1)
    dyo, dgate, dgp = _rowwise("l1m_postb", functools.partial(_post_bwd_fn, 1.0), t_len, [dh, yo1],
                               [("full", gvec(1, 3)), ("seg", modrow(1, 5, 1))], [(D_MODEL, BF16)],
                               [D_MODEL, D_MODEL], tm=tm1)
    put_mod(1, 5, dgate)
    put_g(1, 3, dgp)
    put_grad("gm", "w_out", _mm_tn(tgm, dyo, name="gm_dwout", tn=1024))
    dtg = _mm(dyo, wts["gm_wout"], out_dtype=F32, name="gm_dt", rhs_t=True)
    dgu, dgvn, dws, dbst = _gm_spatial_bwd(dtg, gu, gvn, ws_bf, wst_bf, bst, name="gm_spatialb")
    g["gm_w_s"] = dws
    g["gm_b_s"] = dbst.T
    dp1, dvg, dvb = _rowwise("gm_actb", _gm_act_bwd_fn, t_len, [p1, dgu, dgvn], [("full", vg)],
                             [(2 * GM_INNER, BF16)], [GM_INNER, GM_INNER], tm=128)
    g["gm_v_g"] = dvg[0, 0]
    g["gm_v_b"] = dvb[0, 0]
    pending[0] = put_grad("gm", "w_in", _mm_tn(um1, dp1, name="gm_dwin", tm=1024))
    dh, dsh, dsc, dgp = _mm_rows(dp1, wts["gm_win"], _pre_bwd_fn, [h4, dh],
                                 [("full", gvec(1, 2)), ("seg", modrow(1, 4, 1))], [(D_MODEL, F32)],
                                 [D_MODEL, D_MODEL, D_MODEL], name="gm_dum", tk=1024, rhs_t=True)
    put_mod(1, 3, dsh)
    put_mod(1, 4, dsc)
    put_g(1, 2, dgp)
    dh = ffn_back("l1f1", 1, 0, dh, sv_f11, wts["ffn10"], L1)

    dh = ffn_back("l0f2", 0, 1, dh, sv_f02, wts["ffn01"], L1)
    dyo, dgate, dgp = _rowwise("l0m_postb", functools.partial(_post_bwd_fn, 1.0), t_len, [dh, yo0],
                               [("full", gvec(0, 3)), ("seg", modrow(0, 5, 1))], [(D_MODEL, BF16)],
                               [D_MODEL, D_MODEL], tm=tm1)
    put_mod(0, 5, dgate)
    put_g(0, 3, dgp)
    put_grad("ssd", "w_out", _mm_tn(yn, dyo, name="ssd_dwout", tn=1024))
    dyn = _mm(dyo, wts["ssd_wout"], out_dtype=F32, name="ssd_dyn", rhs_t=True)
    dy_ssd, dz, dngv, ddv = _rowwise("ssd_gateb", _ssdgate_bwd_fn, n0, [(dyn, SSD_INNER, 0, -ncc)] + gate_rows,
                                     [("full", dvec), ("full", ngv)], [(SSD_INNER, F32), (SSD_INNER, BF16)],
                                     [SSD_INNER, SSD_INNER], tm=128)
    g["ssd_norm_g"] = dngv[0, 0]
    g["ssd_D"] = jnp.sum(ddv[0, 0].reshape(SSD_HEADS, SSD_HEAD_DIM), axis=1)
    dxbcs, ddts, dalogs, dbiases = [], [], [], []
    for d in range(2):
        dxd, ddtd, dal, dbi = _ssd_scan_bwd(dy_ssd, xbc, hss[d], dt_dir[d], dtT_dir[d], bias_r[d], bias_c[d],
                                            alog_r[d], alog_c[d], dvec, rev=(d == 1), n_ctx_chunks=ncc,
                                            direct=(d == 0), name=f"ssd_scanb{d}")
        dxbcs.append(dxd)
        ddts.append(ddtd)
        dalogs.append(dal[0])
        dbiases.append(dbi[0])
    g["ssd_A_log"] = jnp.stack(dalogs)
    g["ssd_dt_bias"] = jnp.stack(dbiases)
    dxbc_pre, dcw8, dcb = _conv_bwd(dxbcs[0], dxbcs[1], cpre, xbc_pre, small["conv_w8"], n_ctx=n_ctx, name="ssd_convb")
    g["ssd_conv_w"] = dcw8[:SSD_CONV]
    g["ssd_conv_b"] = dcb[0]
    ddt_bf = jnp.concatenate([ddts[0], ddts[1], jnp.zeros((n0, 128 - 2 * nh), F32)], axis=1).astype(BF16)
    dw_ssd_in = jnp.concatenate([
        _mm_tn(um0, dz, name="ssd_dwz", tm=1024),
        _mm_tn(um0, dxbc_pre, name="ssd_dwxbc", tm=1024),
        _mm_tn(um0, ddt_bf, name="ssd_dwdt", tm=1024)[:, :2 * nh]], axis=1)
    pending[0] = put_grad("ssd", "w_in", dw_ssd_in)
    win_ssd = wts["ssd_win"]
    dum0 = _mm(dz, win_ssd, out_dtype=F32, name="ssd_dum_z", tk=1024, rhs_t=True, n=D_MODEL)
    dum0 = _mm(dxbc_pre, win_ssd, out_dtype=F32, name="ssd_dum_x", tk=1024, rhs_t=True, n=D_MODEL,
               b_off=(0, SSD_INNER // 1024), add=dum0)
    dum0 = _mm(ddt_bf, wts["ssd_wdt"], out_dtype=F32, name="ssd_dum_dt", rhs_t=True, add=dum0)
    dh0, dsh, dsc, dgp = _rowwise("l0m_preb", _pre_bwd_fn, n0, [dum0, h1, (dh, D_MODEL, 0, -(n_ctx // tm0))],
                                  [("full", gvec(0, 2)), ("seg", modrow(0, 4, 2))], [(D_MODEL, F32)],
                                  [D_MODEL, D_MODEL, D_MODEL], tm=tm0, **kw0)
    put_mod(0, 3, dsh)
    put_mod(0, 4, dsc)
    put_g(0, 2, dgp)
    dh0 = ffn_back("l0f1", 0, 0, dh0, sv_f01, wts["ffn00"], L0)
    grad_x = dh0[n_ctx:]
    g["norm_g"] = jnp.stack([jnp.stack(r) for r in dng])
    g["dmx"] = jnp.stack([jnp.concatenate(r) for r in dmx])
    g["dmc"] = jnp.stack([jnp.concatenate(r) for r in dmc])
    return loss_parts[0], grad_x, g


GROUPS = ("ffn00", "ssd", "ffn01", "ffn10", "gm", "ffn11")


def _mats_in(group, win_l):
    k, nloc = win_l.shape[1], win_l.shape[2]
    win = jnp.transpose(win_l, (1, 0, 2)).reshape(k, NDEV * nloc)
    if group.startswith("ffn"):
        return dict(win=win)
    if group == "gm":
        return dict(gm_win=win)
    assert group == "ssd"
    c1 = SSD_INNER + SSD_CONV_DIM
    return dict(ssd_win=win, ssd_wdt=jnp.pad(win[:, c1:], ((0, 0), (0, 128 - 2 * SSD_HEADS))))


def _mats_out(group, wout_l):
    pre = "" if group.startswith("ffn") else group + "_"
    return {pre + "wout": wout_l.reshape(-1, wout_l.shape[2])}


def _group_mats(group, lands):
    m = {**_mats_in(group, lands[0]), **_mats_out(group, lands[1])}
    return {group: m} if group.startswith("ffn") else m


def _grad_blocks(which, grad):
    if which == "w_in":
        k, n = grad.shape
        return jnp.transpose(grad.reshape(k, NDEV, n // NDEV), (1, 0, 2)).astype(BF16)
    return grad.reshape(NDEV, grad.shape[0] // NDEV, grad.shape[1]).astype(BF16)


def kernel(x, c, ctx, c_ctx, ada_w, ada_b, norm_g, ffn_w_in, ffn_w_out, ssd_w_in, ssd_conv_w, ssd_conv_b, ssd_dt_bias, ssd_A_log, ssd_D, ssd_norm_g, ssd_w_out, gm_w_in, gm_v_g, gm_v_b, gm_w_s, gm_b_s, gm_w_out, loss_target, m_c_ctx, m_ada_w, m_ada_b, m_norm_g, m_ffn_w_in, m_ffn_w_out, m_ssd_w_in, m_ssd_conv_w, m_ssd_conv_b, m_ssd_dt_bias, m_ssd_A_log, m_ssd_D, m_ssd_norm_g, m_ssd_w_out, m_gm_w_in, m_gm_v_g, m_gm_v_b, m_gm_w_s, m_gm_b_s, m_gm_w_out, v_c_ctx, v_ada_w, v_ada_b, v_norm_g, v_ffn_w_in, v_ffn_w_out, v_ssd_w_in, v_ssd_conv_w, v_ssd_conv_b, v_ssd_dt_bias, v_ssd_A_log, v_ssd_D, v_ssd_norm_g, v_ssd_w_out, v_gm_w_in, v_gm_v_g, v_gm_v_b, v_gm_w_s, v_gm_b_s, v_gm_w_out):
    me = 4 * lax.axis_index("x") + 2 * lax.axis_index("y") + lax.axis_index("c")
    d = D_MODEL
    ncol = N_MOD * d // NDEV

    small_pack = jnp.concatenate([c.reshape(-1), norm_g.reshape(-1), ssd_conv_w.reshape(-1),
                                  gm_v_g.reshape(-1), gm_v_b.reshape(-1)])[None, :]
    (sp,), _ = _exchange([small_pack], scatter=False, name="gather_small")
    sp = sp[:, 0]
    o = 0
    c_all = sp[:, o:o + d]; o += d
    ng_all = sp[:, o:o + 2 * 6 * 128].reshape(NDEV, 2, 6, 128); o += 2 * 6 * 128
    cw_all = sp[:, o:o + SSD_CONV * 512].reshape(NDEV, SSD_CONV, 512); o += SSD_CONV * 512
    vg_all = sp[:, o:o + 256]; o += 256
    vb_all = sp[:, o:o + 256]; o += 256
    norm_g_full = jnp.transpose(ng_all, (1, 2, 0, 3)).reshape(2, 6, d)
    conv_w_full = jnp.transpose(cw_all, (1, 0, 2)).reshape(SSD_CONV, SSD_CONV_DIM)
    gm_v_g_full = vg_all.reshape(-1)
    gm_v_b_full = vb_all.reshape(-1)

    c16 = jnp.concatenate([c_all, jnp.broadcast_to(c_ctx[None, :], (NDEV, d))], axis=0)
    ada_b_loc = lax.dynamic_slice_in_dim(ada_b, me * ncol, ncol, axis=1)
    mods_loc = jnp.stack([_mm_f32(c16, ada_w[i], name=f"ada_mod{i}", silu_a=True, bias=ada_b_loc[i][None, :])
                          for i in range(2)])
    (mods_all,), mods_done = _exchange([mods_loc], scatter=False, name="gather_mods")

    shard = {"ssd": (ssd_w_in[0], ssd_w_out[0]), "gm": (gm_w_in[0], gm_w_out[0])}
    moment = {"ssd": ((m_ssd_w_in[0], v_ssd_w_in[0]), (m_ssd_w_out[0], v_ssd_w_out[0])),
              "gm": ((m_gm_w_in[0], v_gm_w_in[0]), (m_gm_w_out[0], v_gm_w_out[0]))}
    for i in range(2):
        for j in range(2):
            shard[f"ffn{i}{j}"] = (ffn_w_in[i, j], ffn_w_out[i, j])
            moment[f"ffn{i}{j}"] = ((m_ffn_w_in[i, j], v_ffn_w_in[i, j]), (m_ffn_w_out[i, j], v_ffn_w_out[i, j]))
    first = GROUPS[0]
    units = [(first + "_in", first, (0,)), (first + "_out", first, (1,))] + [(grp, grp, (0, 1)) for grp in GROUPS[1:]]
    gathers = {}
    for unit, grp, idx in units:
        srcs = [(shard[grp][k] + mods_done).astype(BF16) for k in idx]
        st = _exchange_start(srcs, [_landing(s, me) for s in srcs], scatter=False, name="gather_start_" + unit)
        gathers[unit] = st[:4]

    def fetch(unit, after):
        return _exchange_wait(*gathers[unit], after, scatter=False, name="gather_wait_" + unit)

    def get_w(grp, after):
        if grp != first:
            return _group_mats(grp, fetch(grp, after))
        late = lambda later: _mats_out(grp, fetch(grp + "_out", later)[0])
        return {grp: dict(_mats_in(grp, fetch(grp + "_in", after)[0]), late=late)}

    scatters = {}
    held = {}

    def put_grad(grp, which, grad):
        if grp == first:
            unit, blocks = grp + "_" + which[2:], [_grad_blocks(which, grad)]
        else:
            held[grp, which] = _grad_blocks(which, grad)
            if (grp, "w_in") not in held or (grp, "w_out") not in held:
                return None
            unit, blocks = grp, [held[grp, "w_in"], held[grp, "w_out"]]
        lands = [_landing(lax.dynamic_index_in_dim(b, me, axis=0, keepdims=False), me) for b in blocks]
        st = _exchange_start(blocks, lands, scatter=True, name="scatter_start_" + unit)
        scatters[unit] = st[:4]
        return st[4]

    mods_rows = jnp.transpose(mods_all, (1, 2, 0, 3)).reshape(2, 2 * NDEV, N_MOD * d)
    mx = lax.dynamic_index_in_dim(mods_rows, me, axis=1, keepdims=False).reshape(2, N_MOD, d)
    mc = mods_rows[:, NDEV].reshape(2, N_MOD, d)
    mods = [(mc[i], mx[i]) for i in range(2)]

    small = dict(conv_w8=jnp.pad(conv_w_full, ((0, 8 - SSD_CONV), (0, 0))), conv_b=ssd_conv_b, dt_bias=ssd_dt_bias[0],
                 a_log=ssd_A_log[0], ssd_d=ssd_D[0], ssd_norm_g=ssd_norm_g[0], gm_v_g=gm_v_g_full,
                 gm_v_b=gm_v_b_full, gm_w_s=gm_w_s[0], gm_b_s=gm_b_s[0])
    loss_parts, grad_x, g = _local_step(x[0], ctx[0], loss_target[0], mods, norm_g_full, get_w, small, put_grad)
    loss = lax.psum(0.5 / d * jnp.sum(loss_parts), ("x", "y", "c"))

    upd = {}
    after = grad_x
    for unit, grp, idx in reversed(units):
        parts = _exchange_wait(*scatters[unit], after, scatter=True, name="scatter_wait_" + unit)
        for k, p in zip(idx, parts):
            m_, v_ = moment[grp][k]
            which = ("in", "out")[k]
            upd[grp, which] = _adamw(p, shard[grp][k], m_, v_, name=f"adamw_{grp}_{which}")
            after = upd[grp, which][0]
    res = {}
    for which in ("in", "out"):
        res["ffn_w_" + which] = [jnp.stack([jnp.stack([upd[f"ffn{i}{j}", which][k] for j in range(2)])
                                            for i in range(2)]) for k in range(4)]
        res["ssd_w_" + which] = [upd["ssd", which][k][None] for k in range(4)]
        res["gm_w_" + which] = [upd["gm", which][k][None] for k in range(4)]

    sg_names = ["dmx", "dmc", "norm_g", "ssd_conv_w", "ssd_conv_b", "ssd_dt_bias", "ssd_A_log", "ssd_D", "ssd_norm_g",
                "gm_v_g", "gm_v_b", "gm_w_s", "gm_b_s"]
    sg_shapes = [g[n].shape for n in sg_names]
    flat = jnp.concatenate([g[n].reshape(-1) for n in sg_names])
    npack = flat.shape[0]
    pad = (-npack) % 1024
    flat = jnp.pad(flat, (0, pad)).reshape(-1, 128)
    (sg_all,), _ = _exchange([flat], scatter=False, name="gather_small_grads")
    sg_sum = _sum_slots(sg_all, name="sum_small_grads").reshape(-1)[:npack]
    sums = {}
    o = 0
    for n, shp in zip(sg_names, sg_shapes):
        sz = math.prod(shp)
        sums[n] = sg_sum[o:o + sz].reshape(shp)
        o += sz
    per_dev = sg_all.reshape(NDEV, -1)
    dmx_all = per_dev[:, :2 * N_MOD * d].reshape(NDEV, 2, N_MOD * d)
    dmc_all = per_dev[:, 2 * N_MOD * d:4 * N_MOD * d].reshape(NDEV, 2, N_MOD * d)

    (s16,) = _rowwise("ada_silu", lambda cc: ((_silu(cc),), ()), 2 * NDEV, [c16], [], [(d, F32)], tm=2 * NDEV)
    s16_t = s16.T
    g_ada_w, dcc_parts = [], []
    for i in range(2):
        rhs = jnp.concatenate([lax.dynamic_slice_in_dim(dmx_all[:, i], me * ncol, ncol, axis=1),
                               lax.dynamic_slice_in_dim(dmc_all[:, i], me * ncol, ncol, axis=1)], axis=0)
        g_ada_w.append(_mm_f32(s16_t, rhs, name=f"ada_dw{i}"))
        dmc_loc = lax.dynamic_slice_in_dim(sums["dmc"][i], me * ncol, ncol, axis=0)
        rhs_c = jnp.zeros((ncol, 128), F32).at[:, 0].set(dmc_loc)
        dcc_parts.append(_mm_f32(ada_w[i], rhs_c, name=f"ada_dcc{i}")[:, 0])
    g_ada_w = jnp.stack(g_ada_w)
    dcc_part = (dcc_parts[0] + dcc_parts[1]).reshape(8, 128)
    (dcc_all,), _ = _exchange([dcc_part], scatter=False, name="gather_dcc")
    g_c_ctx = _sum_slots(dcc_all, name="sum_dcc", scale_by=c_ctx.reshape(8, 128)).reshape(d)
    g_ada_b = sums["dmx"] + sums["dmc"]

    outs = _adamw(g_ada_w.reshape(1, -1, ncol), ada_w.reshape(-1, ncol), m_ada_w.reshape(-1, ncol),
                  v_ada_w.reshape(-1, ncol), name="adamw_ada_w")
    res["ada_w"] = [o_.reshape(ada_w.shape) for o_ in outs]

    loc = lambda a, ax, n: lax.dynamic_slice_in_dim(a, me * n, n, axis=ax)
    small_g = dict(c_ctx=g_c_ctx, ada_b=g_ada_b, norm_g=loc(sums["norm_g"], 2, 128),
                   ssd_conv_w=loc(sums["ssd_conv_w"], 1, 512)[None], ssd_conv_b=sums["ssd_conv_b"][None],
                   ssd_dt_bias=sums["ssd_dt_bias"][None], ssd_A_log=sums["ssd_A_log"][None], ssd_D=sums["ssd_D"][None],
                   ssd_norm_g=sums["ssd_norm_g"][None], gm_v_g=loc(sums["gm_v_g"], 0, 256)[None],
                   gm_v_b=loc(sums["gm_v_b"], 0, 256)[None], gm_w_s=sums["gm_w_s"][None], gm_b_s=sums["gm_b_s"][None])
    small_w = dict(c_ctx=(c_ctx, m_c_ctx, v_c_ctx), ada_b=(ada_b, m_ada_b, v_ada_b), norm_g=(norm_g, m_norm_g, v_norm_g),
                   ssd_conv_w=(ssd_conv_w, m_ssd_conv_w, v_ssd_conv_w), ssd_conv_b=(ssd_conv_b, m_ssd_conv_b, v_ssd_conv_b),
                   ssd_dt_bias=(ssd_dt_bias, m_ssd_dt_bias, v_ssd_dt_bias), ssd_A_log=(ssd_A_log, m_ssd_A_log, v_ssd_A_log),
                   ssd_D=(ssd_D, m_ssd_D, v_ssd_D), ssd_norm_g=(ssd_norm_g, m_ssd_norm_g, v_ssd_norm_g),
                   gm_v_g=(gm_v_g, m_gm_v_g, v_gm_v_g), gm_v_b=(gm_v_b, m_gm_v_b, v_gm_v_b),
                   gm_w_s=(gm_w_s, m_gm_w_s, v_gm_w_s), gm_b_s=(gm_b_s, m_gm_b_s, v_gm_b_s))
    sn = list(small_w)

    def pack(arrs):
        f = jnp.concatenate([a.reshape(-1) for a in arrs])
        return jnp.pad(f, (0, (-f.shape[0]) % 1024)).reshape(-1, 128)

    pg = pack([small_g[n].reshape(small_w[n][0].shape) for n in sn])
    outs = _adamw(pg[None], pack([small_w[n][0] for n in sn]), pack([small_w[n][1] for n in sn]),
                  pack([small_w[n][2] for n in sn]), name="adamw_small")
    flat_outs = [o_.reshape(-1) for o_ in outs]
    o = 0
    for n in sn:
        shp = small_w[n][0].shape
        sz = math.prod(shp)
        res[n] = [fo[o:o + sz].reshape(shp) for fo in flat_outs]
        o += sz

    order = ["c_ctx", "ada_w", "ada_b", "norm_g", "ffn_w_in", "ffn_w_out", "ssd_w_in", "ssd_conv_w", "ssd_conv_b",
             "ssd_dt_bias", "ssd_A_log", "ssd_D", "ssd_norm_g", "ssd_w_out", "gm_w_in", "gm_v_g", "gm_v_b", "gm_w_s",
             "gm_b_s", "gm_w_out"]
    result = [loss, grad_x[None]]
    for k in range(4):
        result += [res[n][k] for n in order]
    return tuple(result)
```

```python
import functools
import math

import jax
import jax.numpy as jnp
from jax import lax
from jax.experimental import pallas as pl
from jax.experimental.pallas import tpu as pltpu

F32 = jnp.float32
BF16 = jnp.bfloat16

NDEV = 8
D_MODEL = 1024
FFN_DIM = 2816
N_MOD = 9
EPS = 1e-6
SSD_INNER = 2048
SSD_HEADS = 32
SSD_HEAD_DIM = 64
SSD_GROUPS = 8
SSD_HPG = 4
SSD_STATE = 128
SSD_CONV = 5
SSD_CONV_DIM = 4096
CHUNK = 128
GM_INNER = 2048
GM_GROUPS = 8
GM_GROUP_DIM = 256
ADAM_LR = 0.001
ADAM_B1 = 0.9
ADAM_B2 = 0.999
ADAM_EPS = 1e-08
ADAM_WD = 0.01
ADAM_STEP = 10
NEG_BIG = -1e30
VMEM_LIMIT_BYTES = 56 * 1024 * 1024
HI = lax.Precision.HIGHEST


def _params(*sem):
    return pltpu.CompilerParams(dimension_semantics=sem, vmem_limit_bytes=VMEM_LIMIT_BYTES)


def _pick(n, target, mult=16):
    if n <= target:
        return n
    for t in range(target - target % mult, 0, -mult):
        if n % t == 0:
            return t
    raise ValueError((n, target, mult))


def _sig(x):
    return 0.5 * jnp.tanh(0.5 * x) + 0.5


def _silu(x):
    return x * _sig(x)


def _dsilu(x):
    s = _sig(x)
    return s * (1.0 + x * (1.0 - s))


_GELU_C = math.sqrt(2.0 / math.pi)


def _gelu(x):
    return 0.5 * x * (1.0 + jnp.tanh(_GELU_C * (x + 0.044715 * x * x * x)))


def _dgelu(x):
    t = jnp.tanh(_GELU_C * (x + 0.044715 * x * x * x))
    return 0.5 * (1.0 + t) + 0.5 * x * (1.0 - t * t) * _GELU_C * (1.0 + 3.0 * 0.044715 * x * x)


def _softplus(x):
    return jnp.maximum(x, 0.0) + jnp.log1p(jnp.exp(-jnp.abs(x)))


def _sum0(v):
    return jnp.sum(v, axis=0, keepdims=True)


def _rms(h):
    r = lax.rsqrt(jnp.mean(h * h, axis=-1, keepdims=True) + EPS)
    return h * r, r


def _dot(a, b, dims=((1,), (0,)), precision=None):
    return lax.dot_general(a, b, (dims, ((), ())), preferred_element_type=F32, precision=precision)


_NT = ((1,), (1,))
_TN = ((0,), (0,))


def _rowwise(name, fn, n_rows, rows, consts, outs, accs=(), *, tm, nseg=1, seg_blocks=0):
    assert n_rows % tm == 0
    if nseg == 2:
        assert seg_blocks > 0
        seg = lambda i: jnp.where(i < seg_blocks, 0, 1)
    else:
        seg = lambda i: 0
    in_specs, args, lacking = [], [], []
    for r in rows:
        arr, width, cb, off = r if isinstance(r, tuple) else (r, r.shape[1], 0, 0)
        in_specs.append(pl.BlockSpec((tm, width), lambda i, cb=cb, off=off: (jnp.maximum(i + off, 0), cb)))
        args.append(arr)
        lacking.append(-off if off < 0 else 0)
    for kind, arr in consts:
        if kind == "seg":
            assert arr.shape[0] == nseg and arr.shape[1] == 1, arr.shape
            in_specs.append(pl.BlockSpec((None, 1, arr.shape[2]), lambda i: (seg(i), 0, 0)))
        else:
            in_specs.append(pl.BlockSpec(arr.shape, lambda i: (0, 0)))
        args.append(arr)
    out_shape = [jax.ShapeDtypeStruct((n_rows, w), dt) for w, dt in outs]
    out_specs = [pl.BlockSpec((tm, w), lambda i: (i, 0)) for w, _ in outs]
    out_shape += [jax.ShapeDtypeStruct((nseg, 1, w), F32) for w in accs]
    out_specs += [pl.BlockSpec((None, 1, w), lambda i: (seg(i), 0, 0)) for w in accs]
    n_in, n_out, n_acc = len(args), len(outs), len(accs)

    def kern(*refs):
        i = pl.program_id(0)
        ins = [r[...] for r in refs[:n_in]]
        for k, lack in enumerate(lacking):
            if lack:
                ins[k] = jnp.where(i >= lack, ins[k], jnp.zeros_like(ins[k]))
        res, terms = fn(*ins)
        for ref, v in zip(refs[n_in:n_in + n_out], res):
            ref[...] = v.astype(ref.dtype)
        if n_acc:
            sums = [_sum0(v) for v in terms]
            first = (i == 0) | (i == seg_blocks) if nseg == 2 else (i == 0)
            acc_refs = refs[n_in + n_out:]

            @pl.when(first)
            def _():
                for ref, v in zip(acc_refs, sums):
                    ref[...] = v

            @pl.when(jnp.logical_not(first))
            def _():
                for ref, v in zip(acc_refs, sums):
                    ref[...] += v

    res = pl.pallas_call(
        kern, name=name, grid=(n_rows // tm,), in_specs=in_specs, out_specs=out_specs, out_shape=out_shape,
        compiler_params=_params("arbitrary"),
    )(*args)
    return res


def _pre_fwd_fn(h, g, shift, scale):
    hh, _ = _rms(h)
    return (hh * g * (1.0 + scale) + shift,), ()


def _pre_bwd_fn(du, h, dres, g, scale):
    hh, r = _rms(h)
    n = hh * g
    dn = du * (1.0 + scale)
    dhh = dn * g
    dh = dres + r * (dhh - hh * jnp.mean(dhh * hh, axis=-1, keepdims=True))
    return (dh,), (du, du * n, dn * hh)


def _post_fwd_fn(weight, h, y, g, gate):
    yh, _ = _rms(y)
    return (h + weight * gate * (yh * g),), ()


def _out_post_fn(weight, y, h, g, gate):
    return (y,) + _post_fwd_fn(weight, h, y, g, gate)[0], ()


def _post_bwd_fn(weight, dh, y, g, gate):
    yh, r = _rms(y)
    dr = dh * weight
    dyh = dr * gate * g
    dy = r * (dyh - yh * jnp.mean(dyh * yh, axis=-1, keepdims=True))
    return (dy,), (dr * yh * g, dr * gate * yh)


def _glu_bwd_fn(ds, a, b):
    a = a.astype(F32)
    b = b.astype(F32)
    sg = _sig(a)
    da = ds * b * (sg * (1.0 + a * (1.0 - sg)))
    db = ds * (a * sg)
    return (jnp.concatenate([da, db], axis=1),), ()


def _loss_fn(y, t):
    diff = y - t
    return (diff * (1.0 / D_MODEL),), (diff * diff,)


def _ssd_y(yf, yb, xs, z, dvec):
    y = yf + yb + dvec * xs
    return y, y * _silu(z)


def _ssdgate_fwd_fn(yf, yb, xs, z, dvec, ng):
    _, yg = _ssd_y(yf, yb, xs, z, dvec)
    parts = []
    for g in range(SSD_GROUPS):
        sl = slice(g * 256, (g + 1) * 256)
        parts.append(_rms(yg[:, sl])[0])
    return (jnp.concatenate(parts, axis=1) * ng,), ()


def _ssdgate_bwd_fn(dyn, yf, yb, xs, z, dvec, ng):
    y, yg = _ssd_y(yf, yb, xs, z, dvec)
    dyg_parts, ygh_parts = [], []
    for g in range(SSD_GROUPS):
        sl = slice(g * 256, (g + 1) * 256)
        ygh, r = _rms(yg[:, sl])
        d = dyn[:, sl] * ng[:, sl]
        dyg_parts.append(r * (d - ygh * jnp.mean(d * ygh, axis=-1, keepdims=True)))
        ygh_parts.append(ygh)
    dyg = jnp.concatenate(dyg_parts, axis=1)
    ygh = jnp.concatenate(ygh_parts, axis=1)
    dy = dyg * _silu(z)
    dz = dyg * y * _dsilu(z)
    return (dy, dz), (dyn * ygh, dy * xs)


def _ln_stats(v):
    mu = jnp.mean(v, axis=-1, keepdims=True)
    vc = v - mu
    r = lax.rsqrt(jnp.mean(vc * vc, axis=-1, keepdims=True) + EPS)
    return vc * r, r


def _gm_act_fwd_fn(p, vg, vb):
    gu = _gelu(p[:, :GM_INNER])
    gvh, _ = _ln_stats(_gelu(p[:, GM_INNER:]))
    return (gu, gvh * vg + vb), ()


def _gm_act_bwd_fn(p, dgu, dgvn, vg):
    pu = p[:, :GM_INNER]
    pv = p[:, GM_INNER:]
    gvh, r = _ln_stats(_gelu(pv))
    dgvh = dgvn * vg
    dgv = r * (dgvh - jnp.mean(dgvh, axis=-1, keepdims=True) - gvh * jnp.mean(dgvh * gvh, axis=-1, keepdims=True))
    dp = jnp.concatenate([dgu * _dgelu(pu), dgv * _dgelu(pv)], axis=1)
    return (dp,), (dgvn * gvh, dgvn)


def _mm(a, b, *, out_dtype, name, tm=1088, tn=1024, tk=1408, add=None, rhs_t=False, n=None, b_off=(0, 0)):
    m, k = a.shape
    if n is None:
        n, k2 = b.shape if rhs_t else b.shape[::-1]
        assert k == k2
    tm, tn, tk = _pick(m, tm), _pick(n, tn, 128), _pick(k, tk, 128)
    o0, o1 = b_off
    nk = k // tk
    dims = _NT if rhs_t else ((1,), (0,))

    def kern(*refs):
        a_ref, b_ref = refs[:2]
        add_ref = refs[2] if add is not None else None
        o_ref = refs[3] if add is not None else refs[2]

        def finish(r):
            if add is not None:
                r = r + add_ref[...]
            o_ref[...] = r.astype(o_ref.dtype)

        p = _dot(a_ref[...], b_ref[...], dims)
        if nk == 1:
            finish(p)
            return
        acc_ref = refs[-1]
        kk = pl.program_id(2)

        @pl.when(kk == 0)
        def _():
            acc_ref[...] = p

        @pl.when((kk > 0) & (kk < nk - 1))
        def _():
            acc_ref[...] += p

        @pl.when(kk == nk - 1)
        def _():
            finish(acc_ref[...] + p)

    if rhs_t:
        b_spec = pl.BlockSpec((tn, tk), lambda i, j, kk: (j + o0, kk + o1))
    else:
        b_spec = pl.BlockSpec((tk, tn), lambda i, j, kk: (kk + o0, j + o1))
    in_specs = [pl.BlockSpec((tm, tk), lambda i, j, kk: (i, kk)), b_spec]
    args = [a, b]
    if add is not None:
        in_specs.append(pl.BlockSpec((tm, tn), lambda i, j, kk: (i, j)))
        args.append(add)
    return pl.pallas_call(
        kern, name=name, grid=(m // tm, n // tn, nk), in_specs=in_specs,
        out_specs=pl.BlockSpec((tm, tn), lambda i, j, kk: (i, j)),
        out_shape=jax.ShapeDtypeStruct((m, n), out_dtype),
        scratch_shapes=[pltpu.VMEM((tm, tn), F32)] if nk > 1 else [],
        compiler_params=_params("parallel", "parallel", "arbitrary"),
    )(*args)


def _mm_rows(a, b, fn, rows, consts, outs, accs=(), *, name, tm=544, tk=1408, rhs_t=False, n_ctx=0):
    m, k = a.shape
    n = b.shape[0] if rhs_t else b.shape[1]
    tm, tk = _pick(m, tm), _pick(k, tk, 128)
    nk = k // tk
    dims = _NT if rhs_t else ((1,), (0,))
    n_rows, n_const, n_out, n_acc = len(rows), len(consts), len(outs), len(accs)

    def kern(*refs):
        a_ref, b_ref = refs[:2]
        row_refs = refs[2:2 + n_rows]
        const_refs = refs[2 + n_rows:2 + n_rows + n_const]
        out_refs = refs[2 + n_rows + n_const:2 + n_rows + n_const + n_out]
        acc_refs = refs[2 + n_rows + n_const + n_out:2 + n_rows + n_const + n_out + n_acc]
        i, kk = pl.program_id(0), pl.program_id(1)

        def finish(p):
            is_ctx = (i * tm + lax.broadcasted_iota(jnp.int32, (tm, 1), 0)) < n_ctx
            cvals = []
            for (kind, arr), ref in zip(consts, const_refs):
                if kind == "seg":
                    cvals.append(jnp.where(is_ctx, ref[0], ref[1]) if arr.shape[0] == 2 else ref[0])
                else:
                    cvals.append(ref[...])
            res, terms = fn(p, *[r[...] for r in row_refs], *cvals)
            for ref, v in zip(out_refs, res):
                ref[...] = v.astype(ref.dtype)
            for ref, v in zip(acc_refs, terms):
                s_all = _sum0(v)
                s_ctx = _sum0(jnp.where(is_ctx, v, 0.0)) if n_ctx else jnp.zeros_like(s_all)
                both = jnp.concatenate([s_ctx, s_all - s_ctx], axis=0)[:, None, :]

                @pl.when(i == 0)
                def _():
                    ref[...] = both

                @pl.when(i > 0)
                def _():
                    ref[...] += both

        p = _dot(a_ref[...], b_ref[...], dims)
        if nk == 1:
            finish(p)
            return
        scr = refs[-1]

        @pl.when(kk == 0)
        def _():
            scr[...] = p

        @pl.when((kk > 0) & (kk < nk - 1))
        def _():
            scr[...] += p

        @pl.when(kk == nk - 1)
        def _():
            finish(scr[...] + p)

    b_spec = pl.BlockSpec((n, tk), lambda i, kk: (0, kk)) if rhs_t else pl.BlockSpec((tk, n), lambda i, kk: (kk, 0))
    in_specs = [pl.BlockSpec((tm, tk), lambda i, kk: (i, kk)), b_spec]
    in_specs += [pl.BlockSpec((tm, r.shape[1]), lambda i, kk: (i, 0)) for r in rows]
    for kind, arr in consts:
        in_specs.append(pl.BlockSpec(arr.shape, (lambda i, kk: (0, 0, 0)) if kind == "seg" else (lambda i, kk: (0, 0))))
    out_shape = [jax.ShapeDtypeStruct((m, w), dt) for w, dt in outs]
    out_specs = [pl.BlockSpec((tm, w), lambda i, kk: (i, 0)) for w, _ in outs]
    out_shape += [jax.ShapeDtypeStruct((2, 1, w), F32) for w in accs]
    out_specs += [pl.BlockSpec((2, 1, w), lambda i, kk: (0, 0, 0)) for w in accs]
    return pl.pallas_call(
        kern, name=name, grid=(m // tm, nk), in_specs=in_specs, out_specs=out_specs, out_shape=out_shape,
        scratch_shapes=[pltpu.VMEM((tm, n), F32)] if nk > 1 else [],
        compiler_params=_params("arbitrary", "arbitrary"),
    )(a, b, *rows, *[arr for _, arr in consts])


def _mm_glu(u, win, *, name, tm=2176, tn=256):
    m, k = u.shape
    n = win.shape[1] // 2
    tm, tn = _pick(m, tm), _pick(n, tn, 128)
    nj = n // tn

    def kern(u_ref, wa_ref, wb_ref, s_ref, a_ref, b_ref):
        uu = u_ref[...]
        a = jnp.dot(uu, wa_ref[...], preferred_element_type=F32)
        b = jnp.dot(uu, wb_ref[...], preferred_element_type=F32)
        s_ref[...] = (_silu(a) * b).astype(BF16)
        a_ref[...] = a.astype(BF16)
        b_ref[...] = b.astype(BF16)

    ospec = pl.BlockSpec((tm, tn), lambda i, j: (i, j))
    return pl.pallas_call(
        kern, name=name, grid=(m // tm, nj),
        in_specs=[pl.BlockSpec((tm, k), lambda i, j: (i, 0)), pl.BlockSpec((k, tn), lambda i, j: (0, j)),
                  pl.BlockSpec((k, tn), lambda i, j: (0, nj + j))],
        out_specs=[ospec, ospec, ospec],
        out_shape=[jax.ShapeDtypeStruct((m, n), BF16)] * 3,
        compiler_params=_params("parallel", "parallel"),
    )(u, win, win)


def _mm_tn(a, b, *, name, tm=1024, tn=1024, tk=1088):
    t, m = a.shape
    t2, n = b.shape
    assert t == t2
    tm, tn, tk = _pick(m, tm, 128), _pick(n, tn, 128), _pick(t, tk)
    nk = t // tk

    def kern(a_ref, b_ref, o_ref):
        kk = pl.program_id(2)

        @pl.when(kk == 0)
        def _():
            o_ref[...] = jnp.zeros_like(o_ref)

        o_ref[...] += _dot(a_ref[...], b_ref[...], _TN)

    return pl.pallas_call(
        kern, name=name, grid=(m // tm, n // tn, nk),
        in_specs=[pl.BlockSpec((tk, tm), lambda i, j, kk: (kk, i)), pl.BlockSpec((tk, tn), lambda i, j, kk: (kk, j))],
        out_specs=pl.BlockSpec((tm, tn), lambda i, j, kk: (i, j)),
        out_shape=jax.ShapeDtypeStruct((m, n), F32),
        compiler_params=_params("parallel", "parallel", "arbitrary"),
    )(a, b)


def _mm_f32(a, b, *, name, silu_a=False, bias=None):
    m, k = a.shape
    n = b.shape[1]

    def kern(*refs):
        if bias is None:
            a_ref, b_ref, o_ref = refs
        else:
            a_ref, b_ref, bias_ref, o_ref = refs
        av = a_ref[...]
        if silu_a:
            av = _silu(av)
        r = jnp.dot(av, b_ref[...], preferred_element_type=F32, precision=HI)
        if bias is not None:
            r = r + bias_ref[...]
        o_ref[...] = r

    args = [a, b] + ([] if bias is None else [bias])
    return pl.pallas_call(kern, name=name, out_shape=jax.ShapeDtypeStruct((m, n), F32),
                          compiler_params=pltpu.CompilerParams(vmem_limit_bytes=VMEM_LIMIT_BYTES))(*args)


CONV_WIN = 32


def _conv_windows(n, n_ctx):
    assert n_ctx % CONV_WIN == 0 and n_ctx >= CONV_WIN and n - n_ctx >= CONV_WIN
    return (0, n_ctx - CONV_WIN // 2, n - CONV_WIN)


def _tap_outside(r0, s, n, n_ctx):
    t = r0 + lax.broadcasted_iota(jnp.int32, (CONV_WIN, 1), 0)
    lo = jnp.where(t < n_ctx, 0, n_ctx)
    hi = jnp.where(t < n_ctx, n_ctx, n)
    return jnp.where((t + s >= lo) & (t + s < hi), 0.0, 1.0)


def _rolled(v, s):
    return v if s == 0 else pltpu.roll(v, (-s) % v.shape[0], 0)


def _conv_fwd(xp, w8, b, *, n_ctx, name, cb=256):
    n, c = xp.shape
    half = SSD_CONV // 2

    def kern(x_ref, w_ref, b_ref, cpre_ref, act_ref):
        x = x_ref[...]
        acc = jnp.zeros_like(x) + b_ref[...]
        rolled = {}
        for k in range(SSD_CONV):
            rolled[k] = _rolled(x, k - half)
            acc = acc + rolled[k] * w_ref[k:k + 1, :]
        cpre_ref[...] = acc
        act_ref[...] = _silu(acc)
        for r0 in _conv_windows(n, n_ctx):
            rows = slice(r0, r0 + CONV_WIN)
            fix = acc[rows]
            for k in range(SSD_CONV):
                if k != half:
                    fix = fix - rolled[k][rows] * w_ref[k:k + 1, :] * _tap_outside(r0, k - half, n, n_ctx)
            cpre_ref[rows, :] = fix
            act_ref[rows, :] = _silu(fix)

    spec = pl.BlockSpec((n, cb), lambda j: (0, j))
    return pl.pallas_call(
        kern, name=name, grid=(c // cb,),
        in_specs=[spec, pl.BlockSpec((8, cb), lambda j: (0, j)), pl.BlockSpec((1, cb), lambda j: (0, j))],
        out_specs=[spec, spec], out_shape=[jax.ShapeDtypeStruct((n, c), F32)] * 2,
        compiler_params=_params("parallel"),
    )(xp, w8, b)


def _conv_bwd(d1, d2, cpre, xp, w8, *, n_ctx, name, cb=128):
    n, c = xp.shape
    half = SSD_CONV // 2

    def kern(d1_ref, d2_ref, cpre_ref, x_ref, w_ref, dx_ref, dw_ref, db_ref):
        g = (d1_ref[...] + d2_ref[...]) * _dsilu(cpre_ref[...])
        x = x_ref[...]
        dx = jnp.zeros_like(g)
        dw_ref[...] = jnp.zeros_like(dw_ref)
        g_rolled = {}
        for k in range(SSD_CONV):
            s = k - half
            g_rolled[k] = _rolled(g, -s)
            dx = dx + g_rolled[k] * w_ref[k:k + 1, :]
            xr = _rolled(x, s)
            dw = _sum0(g * xr)
            if s != 0:
                for r0 in _conv_windows(n, n_ctx):
                    rows = slice(r0, r0 + CONV_WIN)
                    dw = dw - _sum0(g[rows] * xr[rows] * _tap_outside(r0, s, n, n_ctx))
            dw_ref[k:k + 1, :] = dw
        dx_ref[...] = dx.astype(BF16)
        for r0 in _conv_windows(n, n_ctx):
            rows = slice(r0, r0 + CONV_WIN)
            fix = dx[rows]
            for k in range(SSD_CONV):
                if k != half:
                    fix = fix - g_rolled[k][rows] * w_ref[k:k + 1, :] * _tap_outside(r0, half - k, n, n_ctx)
            dx_ref[rows, :] = fix.astype(BF16)
        db_ref[...] = _sum0(g)

    spec = pl.BlockSpec((n, cb), lambda j: (0, j))
    return pl.pallas_call(
        kern, name=name, grid=(c // cb,),
        in_specs=[spec, spec, spec, spec, pl.BlockSpec((8, cb), lambda j: (0, j))],
        out_specs=[spec, pl.BlockSpec((8, cb), lambda j: (0, j)), pl.BlockSpec((1, cb), lambda j: (0, j))],
        out_shape=[jax.ShapeDtypeStruct((n, c), BF16), jax.ShapeDtypeStruct((8, c), F32),
                   jax.ShapeDtypeStruct((1, c), F32)],
        compiler_params=_params("parallel"),
    )(d1, d2, cpre, xp, w8)


def _chunk_of(s, nc, n_ctx_chunks, rev):
    if not rev:
        return s
    return jnp.where(s < n_ctx_chunks, n_ctx_chunks - 1 - s, nc - 1 - (s - n_ctx_chunks))


def _scan_common(dt_raw, dtT_raw, bias_r, bias_c, alog_r, alog_c, rev):
    ii = lax.broadcasted_iota(jnp.int32, (CHUNK, CHUNK), 0)
    jj = lax.broadcasted_iota(jnp.int32, (CHUNK, CHUNK), 1)
    tri = (jj >= ii) if rev else (jj <= ii)
    tri_t = (ii >= jj) if rev else (ii <= jj)
    a_r = -jnp.exp(alog_r)
    a_c = -jnp.exp(alog_c)
    dt = _softplus(dt_raw + bias_r)
    dt_t = _softplus(dtT_raw + bias_c)
    al = dt * a_r
    acum = _dot(tri.astype(F32), al, precision=HI)
    acum_t = _dot(dt_t * a_c, tri_t.astype(F32), precision=HI)
    atot = _sum0(al)
    return tri, tri_t, a_r, dt, acum, acum_t, atot


def _head_spread():
    return jnp.repeat(jnp.eye(SSD_HEADS, dtype=BF16), SSD_HEAD_DIM, axis=1)


def _dot_sel(v, sel):
    hi = v.astype(BF16)
    lo = (v - hi.astype(F32)).astype(BF16)
    return _dot(hi, sel) + _dot(lo, sel)


def _ssd_scan_fwd(xbc, dt_raw, dtT_raw, bias_r, bias_c, alog_r, alog_c, *, rev, n_ctx_chunks, name):
    n = xbc.shape[0]
    nc = n // CHUNK
    cidx = functools.partial(_chunk_of, nc=nc, n_ctx_chunks=n_ctx_chunks, rev=rev)

    def kern(xs_ref, b_ref, c_ref, dt_ref, dtT_ref, br_ref, bc_ref, ar_ref, ac_ref, e_ref, y_ref, hs_ref, h_scr):
        @pl.when(pl.program_id(0) == 0)
        def _():
            h_scr[...] = jnp.zeros_like(h_scr)

        tri, _, _, dt, acum, acum_t, atot = _scan_common(
            dt_ref[...], dtT_ref[...], br_ref[...], bc_ref[...], ar_ref[...], ac_ref[...], rev)
        etot = jnp.exp(atot)
        spread = lambda v: _dot_sel(v, e_ref[...])
        xdt_all = xs_ref[...] * spread(dt)
        eax = spread(jnp.exp(acum))
        xdw_all = xdt_all * spread(jnp.exp(atot - acum))
        hs_ref[...] = h_scr[...]
        for g in range(SSD_GROUPS):
            gs = slice(g * 256, (g + 1) * 256)
            bg = b_ref[:, g * SSD_STATE:(g + 1) * SSD_STATE].astype(BF16)
            cg = c_ref[:, g * SSD_STATE:(g + 1) * SSD_STATE].astype(BF16)
            cb = _dot(cg, bg, _NT)
            h4 = h_scr[gs, :]
            ys = []
            for k in range(SSD_HPG):
                h = g * SSD_HPG + k
                lmat = jnp.exp(jnp.where(tri, acum[:, h:h + 1] - acum_t[h:h + 1, :], NEG_BIG))
                xdt_h = xdt_all[:, h * SSD_HEAD_DIM:(h + 1) * SSD_HEAD_DIM].astype(BF16)
                ys.append(_dot((cb * lmat).astype(BF16), xdt_h))
            y_ref[:, gs] = jnp.concatenate(ys, axis=1) + _dot(cg, h4.astype(BF16), _NT) * eax[:, gs]
            s4 = _dot(xdw_all[:, gs].astype(BF16), bg, _TN)
            for k in range(SSD_HPG):
                h = g * SSD_HPG + k
                rs = slice(h * SSD_HEAD_DIM, (h + 1) * SSD_HEAD_DIM)
                h_scr[rs, :] = h4[k * SSD_HEAD_DIM:(k + 1) * SSD_HEAD_DIM] * etot[:, h:h + 1] + \
                    s4[k * SSD_HEAD_DIM:(k + 1) * SSD_HEAD_DIM]

    nh = SSD_HEADS
    small = lambda shape: pl.BlockSpec(shape, lambda s: (0, 0))
    return pl.pallas_call(
        kern, name=name, grid=(nc,),
        in_specs=[pl.BlockSpec((CHUNK, SSD_INNER), lambda s: (cidx(s), 0)),
                  pl.BlockSpec((CHUNK, 1024), lambda s: (cidx(s), 2)),
                  pl.BlockSpec((CHUNK, 1024), lambda s: (cidx(s), 3)),
                  pl.BlockSpec((CHUNK, nh), lambda s: (cidx(s), 0)),
                  pl.BlockSpec((nh, CHUNK), lambda s: (0, cidx(s))),
                  small((1, nh)), small((nh, 1)), small((1, nh)), small((nh, 1)), small((nh, SSD_INNER))],
        out_specs=[pl.BlockSpec((CHUNK, SSD_INNER), lambda s: (cidx(s), 0)),
                   pl.BlockSpec((None, SSD_INNER, SSD_STATE), lambda s: (s, 0, 0))],
        out_shape=[jax.ShapeDtypeStruct((n, SSD_INNER), F32),
                   jax.ShapeDtypeStruct((nc, SSD_INNER, SSD_STATE), F32)],
        scratch_shapes=[pltpu.VMEM((SSD_INNER, SSD_STATE), F32)],
        compiler_params=_params("arbitrary"),
    )(xbc, xbc, xbc, dt_raw, dtT_raw, bias_r, bias_c, alog_r, alog_c, _head_spread())


def _ssd_scan_bwd(dy, xbc, hs, dt_raw, dtT_raw, bias_r, bias_c, alog_r, alog_c, dvec, *, rev, n_ctx_chunks,
                  direct, name):
    n = xbc.shape[0]
    nc = n // CHUNK
    nh = SSD_HEADS
    step_of = lambda r: nc - 1 - r
    cidx = lambda r: _chunk_of(step_of(r), nc, n_ctx_chunks, rev)

    def kern(dy_ref, xs_ref, b_ref, c_ref, hs_ref, dt_ref, dtT_ref, br_ref, bc_ref, ar_ref, ac_ref, dv_ref,
             e_ref, et_ref, dx_ref, ddt_ref, dal_ref, dbias_ref, dh_scr):
        @pl.when(pl.program_id(0) == 0)
        def _():
            dh_scr[...] = jnp.zeros_like(dh_scr)
            dal_ref[...] = jnp.zeros_like(dal_ref)
            dbias_ref[...] = jnp.zeros_like(dbias_ref)

        tri, tri_t, a_r, dt, acum, acum_t, atot = _scan_common(
            dt_ref[...], dtT_ref[...], br_ref[...], bc_ref[...], ar_ref[...], ac_ref[...], rev)
        etot = jnp.exp(atot)
        spread = lambda v: _dot_sel(v, e_ref[...])
        gather = lambda v: _dot_sel(v, et_ref[...])
        xs_all = xs_ref[...]
        dy_all = dy_ref[...]
        dtx = spread(dt)
        eax = spread(jnp.exp(acum))
        decx = spread(jnp.exp(atot - acum))
        xdt_all = xs_all * dtx
        xdw_all = xdt_all * decx
        dyo_all = dy_all * eax
        lane = lax.broadcasted_iota(jnp.int32, (CHUNK, nh), 1)
        lane1 = lax.broadcasted_iota(jnp.int32, (1, nh), 1)
        sub = lax.broadcasted_iota(jnp.int32, (nh, CHUNK), 0)
        g_rows = jnp.zeros((CHUNK, nh), F32)
        g_cols = jnp.zeros((nh, CHUNK), F32)
        dtot = jnp.zeros((1, nh), F32)
        q_col, q_e, q_dt = [], [], []
        for g in range(SSD_GROUPS):
            gs = slice(g * 256, (g + 1) * 256)
            bg = b_ref[:, g * SSD_STATE:(g + 1) * SSD_STATE].astype(BF16)
            cg = c_ref[:, g * SSD_STATE:(g + 1) * SSD_STATE].astype(BF16)
            cb = _dot(cg, bg, _NT)
            hs4 = hs_ref[gs, :]
            dh4 = dh_scr[gs, :]
            hs4_bf = hs4.astype(BF16)
            dh4_bf = dh4.astype(BF16)
            dy4 = dy_all[:, gs]
            dy4_bf = dy4.astype(BF16)
            xdt4_bf = xdt_all[:, gs].astype(BF16)
            xdw4 = xdw_all[:, gs]
            xdw4_bf = xdw4.astype(BF16)
            dyo4_bf = dyo_all[:, gs].astype(BF16)
            yoff4 = _dot(cg, hs4_bf, _NT) * eax[:, gs]
            dcg = _dot(dyo4_bf, hs4_bf)
            dh_new4 = _dot(dyo4_bf, cg, _TN)
            bdh4 = _dot(bg, dh4_bf, _NT)
            dbg = _dot(xdw4_bf, dh4_bf)
            e4 = xdw4 * bdh4
            q_col.append(dy4 * yoff4 - e4)
            q_e.append(e4)
            hsum = jnp.sum(dh4 * hs4, axis=1, keepdims=True)
            dcb = jnp.zeros((CHUNK, CHUNK), F32)
            dxdts = []
            for k in range(SSD_HPG):
                h = g * SSD_HPG + k
                ks = slice(k * SSD_HEAD_DIM, (k + 1) * SSD_HEAD_DIM)
                lmat = jnp.exp(jnp.where(tri, acum[:, h:h + 1] - acum_t[h:h + 1, :], NEG_BIG))
                mf = cb * lmat
                dm = _dot(dy4_bf[:, ks], xdt4_bf[:, ks], _NT)
                dcb = dcb + dm * lmat
                gmat = dm * mf
                g_rows = g_rows + jnp.where(lane == h, jnp.sum(gmat, axis=1, keepdims=True), 0.0)
                g_cols = g_cols + jnp.where(sub == h, _sum0(gmat), 0.0)
                dxdts.append(_dot(mf.astype(BF16), dy4_bf[:, ks], _TN))
                et = etot[:, h:h + 1]
                dtot = dtot + jnp.where(lane1 == h, _sum0(hsum[ks]) * et, 0.0)
                dh_scr[h * SSD_HEAD_DIM:(h + 1) * SSD_HEAD_DIM, :] = dh4[ks] * et + dh_new4[ks]
            dxdt4 = jnp.concatenate(dxdts, axis=1) + bdh4 * decx[:, gs]
            q_dt.append(dxdt4 * xs_all[:, gs])
            dx4 = dxdt4 * dtx[:, gs]
            if direct:
                dx4 = dx4 + dy4 * dv_ref[:, gs]
            dcb_bf = dcb.astype(BF16)
            dx_ref[:, gs] = dx4
            dx_ref[:, SSD_INNER + g * SSD_STATE:SSD_INNER + (g + 1) * SSD_STATE] = dbg + _dot(dcb_bf, cg, _TN)
            dx_ref[:, SSD_INNER + 1024 + g * SSD_STATE:SSD_INNER + 1024 + (g + 1) * SSD_STATE] = \
                dcg + _dot(dcb_bf, bg)
        e_heads = gather(jnp.concatenate(q_e, axis=1))
        dacum = gather(jnp.concatenate(q_col, axis=1)) + g_rows - g_cols.T
        dal = _dot(tri_t.astype(F32), dacum, precision=HI) + dtot + _sum0(e_heads)
        ddt = gather(jnp.concatenate(q_dt, axis=1)) + dal * a_r
        ddt_raw = ddt * _sig(dt_ref[...] + br_ref[...])
        ddt_ref[...] = ddt_raw
        dal_ref[...] += _sum0(dal * dt) * a_r
        dbias_ref[...] += _sum0(ddt_raw)

    small = lambda shape: pl.BlockSpec(shape, lambda r: (0, 0))
    return pl.pallas_call(
        kern, name=name, grid=(nc,),
        in_specs=[pl.BlockSpec((CHUNK, SSD_INNER), lambda r: (cidx(r), 0)),
                  pl.BlockSpec((CHUNK, SSD_INNER), lambda r: (cidx(r), 0)),
                  pl.BlockSpec((CHUNK, 1024), lambda r: (cidx(r), 2)),
                  pl.BlockSpec((CHUNK, 1024), lambda r: (cidx(r), 3)),
                  pl.BlockSpec((None, SSD_INNER, SSD_STATE), lambda r: (step_of(r), 0, 0)),
                  pl.BlockSpec((CHUNK, nh), lambda r: (cidx(r), 0)),
                  pl.BlockSpec((nh, CHUNK), lambda r: (0, cidx(r))),
                  small((1, nh)), small((nh, 1)), small((1, nh)), small((nh, 1)), small((1, SSD_INNER)),
                  small((nh, SSD_INNER)), small((SSD_INNER, nh))],
        out_specs=[pl.BlockSpec((CHUNK, SSD_CONV_DIM), lambda r: (cidx(r), 0)),
                   pl.BlockSpec((CHUNK, nh), lambda r: (cidx(r), 0)),
                   small((1, nh)), small((1, nh))],
        out_shape=[jax.ShapeDtypeStruct((n, SSD_CONV_DIM), F32), jax.ShapeDtypeStruct((n, nh), F32),
                   jax.ShapeDtypeStruct((1, nh), F32), jax.ShapeDtypeStruct((1, nh), F32)],
        scratch_shapes=[pltpu.VMEM((SSD_INNER, SSD_STATE), F32)],
        compiler_params=_params("arbitrary"),
    )(dy, xbc, xbc, xbc, hs, dt_raw, dtT_raw, bias_r, bias_c, alog_r, alog_c, dvec, _head_spread(),
      _head_spread().T)


def _gm_spatial_fwd(gu, gvn, ws, bst, *, name):
    n = gu.shape[0]

    def kern(gu_ref, gv_ref, ws_ref, bs_ref, o_ref):
        for g in range(GM_GROUPS):
            sl = slice(g * GM_GROUP_DIM, (g + 1) * GM_GROUP_DIM)
            s = _dot(ws_ref[g], gv_ref[:, sl]) + bs_ref[:, g:g + 1]
            o_ref[:, sl] = (gu_ref[:, sl] * s).astype(BF16)

    spec = pl.BlockSpec((CHUNK, GM_INNER), lambda i: (i, 0))
    return pl.pallas_call(
        kern, name=name, grid=(n // CHUNK,),
        in_specs=[spec, spec, pl.BlockSpec(ws.shape, lambda i: (0, 0, 0)), pl.BlockSpec(bst.shape, lambda i: (0, 0))],
        out_specs=spec, out_shape=jax.ShapeDtypeStruct((n, GM_INNER), BF16),
        compiler_params=_params("parallel"),
    )(gu, gvn, ws, bst)


def _gm_spatial_bwd(dt, gu, gvn, ws, wst, bst, *, name):
    n = gu.shape[0]

    def kern(dt_ref, gu_ref, gv_ref, ws_ref, wst_ref, bs_ref, dgu_ref, dgv_ref, dws_ref, dbs_ref):
        @pl.when(pl.program_id(0) == 0)
        def _():
            dws_ref[...] = jnp.zeros_like(dws_ref)
            dbs_ref[...] = jnp.zeros_like(dbs_ref)

        lane = lax.broadcasted_iota(jnp.int32, (CHUNK, GM_GROUPS), 1)
        dbs = jnp.zeros((CHUNK, GM_GROUPS), F32)
        for g in range(GM_GROUPS):
            sl = slice(g * GM_GROUP_DIM, (g + 1) * GM_GROUP_DIM)
            gv = gv_ref[:, sl]
            s = _dot(ws_ref[g], gv) + bs_ref[:, g:g + 1]
            d = dt_ref[:, sl]
            dgu_ref[:, sl] = d * s
            ds = d * gu_ref[:, sl]
            ds_bf = ds.astype(BF16)
            dws_ref[g] += _dot(ds_bf, gv, _NT)
            dgv_ref[:, sl] = _dot(wst_ref[g], ds_bf)
            dbs = dbs + jnp.where(lane == g, jnp.sum(ds, axis=1, keepdims=True), 0.0)
        dbs_ref[...] += dbs

    spec = pl.BlockSpec((CHUNK, GM_INNER), lambda i: (i, 0))
    wspec = pl.BlockSpec(ws.shape, lambda i: (0, 0, 0))
    bspec = pl.BlockSpec(bst.shape, lambda i: (0, 0))
    return pl.pallas_call(
        kern, name=name, grid=(n // CHUNK,),
        in_specs=[spec, spec, spec, wspec, wspec, bspec],
        out_specs=[spec, spec, wspec, bspec],
        out_shape=[jax.ShapeDtypeStruct((n, GM_INNER), F32), jax.ShapeDtypeStruct((n, GM_INNER), F32),
                   jax.ShapeDtypeStruct(ws.shape, F32), jax.ShapeDtypeStruct(bst.shape, F32)],
        compiler_params=_params("arbitrary"),
    )(dt, gu, gvn, ws, wst, bst)


def _adamw(parts, w, m, v, *, name, tm=256, sel=(), into=None):
    ns, r, wd = parts.shape
    tm = _pick(r, tm, 8)
    lead = len(sel)
    assert w.shape[lead:] == (r, wd) and lead == w.ndim - 2

    def kern(*refs):
        p_ref, w_ref, m_ref, v_ref = refs[:4]
        g_ref, d_ref, nm_ref, nv_ref = refs[-4:]
        g = p_ref[0].astype(F32)
        for s in range(1, ns):
            g = g + p_ref[s].astype(F32)
        m2 = ADAM_B1 * m_ref[...] + (1.0 - ADAM_B1) * g
        v2 = ADAM_B2 * v_ref[...] + (1.0 - ADAM_B2) * (g * g)
        m_hat = m2 / (1.0 - ADAM_B1 ** ADAM_STEP)
        v_hat = v2 / (1.0 - ADAM_B2 ** ADAM_STEP)
        g_ref[...] = g
        d_ref[...] = -ADAM_LR * (m_hat / (jnp.sqrt(v_hat) + ADAM_EPS) + ADAM_WD * w_ref[...])
        nm_ref[...] = m2
        nv_ref[...] = v2

    spec = pl.BlockSpec((None,) * lead + (tm, wd), lambda i: tuple(sel) + (i, 0))
    chained = any(s > 1 for s in w.shape[:lead])
    extra, aliases = [], {}
    if chained:
        extra = list(into) if into is not None else [lax.empty(w.shape, F32) for _ in range(4)]
        aliases = {4 + k: k for k in range(4)}
    return pl.pallas_call(
        kern, name=name, grid=(r // tm,),
        in_specs=[pl.BlockSpec((ns, tm, wd), lambda i: (0, i, 0)), spec, spec, spec] +
                 [pl.BlockSpec(memory_space=pl.ANY)] * len(extra),
        out_specs=[spec] * 4, out_shape=[jax.ShapeDtypeStruct(w.shape, F32)] * 4,
        input_output_aliases=aliases,
        compiler_params=_params("parallel"),
    )(parts, w, m, v, *extra)


def _sum_slots(parts, *, name, scale_by=None):
    ns, r, wd = parts.shape

    def kern(*refs):
        p_ref, o_ref = refs[0], refs[-1]
        g = p_ref[0]
        for s in range(1, ns):
            g = g + p_ref[s]
        if scale_by is not None:
            g = g * _dsilu(refs[1][...])
        o_ref[...] = g

    args = [parts] + ([] if scale_by is None else [scale_by])
    return pl.pallas_call(kern, name=name, out_shape=jax.ShapeDtypeStruct((r, wd), F32),
                          compiler_params=pltpu.CompilerParams(vmem_limit_bytes=VMEM_LIMIT_BYTES))(*args)


def _mesh_pos():
    x, y, c = lax.axis_index("x"), lax.axis_index("y"), lax.axis_index("c")
    return x, y, c, 4 * x + 2 * y + c


def _flip(x, y, c, f):
    fx, fy, fc = (f >> 2) & 1, (f >> 1) & 1, f & 1
    px = 1 - x if fx else x
    py = 1 - y if fy else y
    pc = 1 - c if fc else c
    return (px, py, pc), 4 * px + 2 * py + pc


_HBM_SPEC = pl.BlockSpec(memory_space=pltpu.HBM)


def _exchange(arrays, *, scatter, name):
    na = len(arrays)
    if scatter:
        out_shape = [jax.ShapeDtypeStruct(a.shape, a.dtype) for a in arrays]
    else:
        out_shape = [jax.ShapeDtypeStruct((NDEV,) + a.shape, a.dtype) for a in arrays]

    out_shape.append(jax.ShapeDtypeStruct((8, 128), F32))

    def body(*refs):
        ins, outs = refs[:na], refs[na:2 * na]
        send_sems, recv_sems, local_sems = refs[2 * na + 1:]
        refs[2 * na][...] = jnp.zeros((8, 128), F32)
        x, y, c, me = _mesh_pos()
        copies = []
        for i in range(na):
            src_own = ins[i].at[me] if scatter else ins[i]
            lc = pltpu.make_async_copy(src_own, outs[i].at[me], local_sems.at[i])
            lc.start()
            copies.append(lc)
        sends = []
        for f in range(1, NDEV):
            peer, pidx = _flip(x, y, c, f)
            for i in range(na):
                k = i * (NDEV - 1) + f - 1
                src = ins[i].at[pidx] if scatter else ins[i]
                cp = pltpu.make_async_remote_copy(
                    src_ref=src, dst_ref=outs[i].at[me], send_sem=send_sems.at[k], recv_sem=recv_sems.at[k],
                    device_id=peer, device_id_type=pl.DeviceIdType.MESH)
                cp.start()
                sends.append(cp)
        for f in range(1, NDEV):
            peer, pidx = _flip(x, y, c, f)
            for i in range(na):
                k = i * (NDEV - 1) + f - 1
                src = ins[i].at[pidx] if scatter else ins[i]
                pltpu.make_async_remote_copy(
                    src_ref=src, dst_ref=outs[i].at[pidx], send_sem=send_sems.at[k], recv_sem=recv_sems.at[k],
                    device_id=peer, device_id_type=pl.DeviceIdType.MESH).wait_recv()
        for cp in sends:
            cp.wait_send()
        for lc in copies:
            lc.wait()

    res = pl.pallas_call(
        body, name=name, out_shape=out_shape, in_specs=[_HBM_SPEC] * na,
        out_specs=[_HBM_SPEC] * na + [pl.BlockSpec(memory_space=pltpu.VMEM)],
        scratch_shapes=[pltpu.SemaphoreType.DMA((na * (NDEV - 1),)), pltpu.SemaphoreType.DMA((na * (NDEV - 1),)),
                        pltpu.SemaphoreType.DMA((na,))],
        compiler_params=pltpu.CompilerParams(has_side_effects=True),
    )(*arrays)
    return res[:na], res[na][0, 0]


_SEM_SPEC = pl.BlockSpec(memory_space=pltpu.SEMAPHORE)
_DATAFLOW = pltpu.SideEffectType.DATAFLOW_SIDE_EFFECTING


def _split_copies(srcs, lands, send_sems, recv_sems, scatter, arriving):
    x, y, c, me = _mesh_pos()
    copies = []
    for i in range(len(srcs)):
        for f in range(1, NDEV):
            peer, pidx = _flip(x, y, c, f)
            k = i * (NDEV - 1) + f - 1
            copies.append(pltpu.make_async_remote_copy(
                src_ref=srcs[i].at[pidx] if scatter else srcs[i], dst_ref=lands[i].at[pidx if arriving else me],
                send_sem=send_sems.at[k], recv_sem=recv_sems.at[k], device_id=peer,
                device_id_type=pl.DeviceIdType.MESH))
    return copies


def _exchange_start(srcs, lands, *, scatter, name):
    na = len(srcs)
    nsem = na * (NDEV - 1)

    def body(*refs):
        ins_src, ins_land = refs[:na], refs[na:2 * na]
        send_sems, recv_sems = refs[2 * na], refs[2 * na + 1]
        token = refs[-1]
        for cp in _split_copies(ins_src, ins_land, send_sems, recv_sems, scatter, False):
            cp.start()
        token[...] = jnp.zeros_like(token)

    thru = [pltpu.HBM(a.shape, a.dtype) for a in list(srcs) + list(lands)]
    res = pl.pallas_call(
        body, name=name,
        out_shape=(pltpu.SemaphoreType.DMA((nsem,)), pltpu.SemaphoreType.DMA((nsem,)), *thru,
                   jax.ShapeDtypeStruct((8, 128), F32)),
        in_specs=[_HBM_SPEC] * (2 * na),
        out_specs=(_SEM_SPEC, _SEM_SPEC, *([_HBM_SPEC] * (2 * na)), pl.BlockSpec(memory_space=pltpu.VMEM)),
        input_output_aliases={i: 2 + i for i in range(2 * na)},
        compiler_params=pltpu.CompilerParams(has_side_effects=_DATAFLOW),
    )(*[pltpu.with_memory_space_constraint(a, pltpu.HBM) for a in list(srcs) + list(lands)])
    send_sems, recv_sems = res[0], res[1]
    return send_sems, recv_sems, res[2:2 + na], res[2 + na:2 + 2 * na], res[-1][0, 0]


def _exchange_wait(send_sems, recv_sems, srcs, lands, after, *, scatter, name):
    na = len(srcs)

    def body(*refs):
        ins_src, ins_land = refs[:na], refs[na:2 * na]
        s_sems, r_sems = refs[2 * na], refs[2 * na + 1]
        for cp in _split_copies(ins_src, ins_land, s_sems, r_sems, scatter, False):
            cp.wait_send()
        for cp in _split_copies(ins_src, ins_land, s_sems, r_sems, scatter, True):
            cp.wait_recv()

    thru = [pltpu.HBM(a.shape, a.dtype) for a in list(srcs) + list(lands)]
    res = pl.pallas_call(
        body, name=name, out_shape=tuple(thru),
        in_specs=[_HBM_SPEC] * (2 * na) + [_SEM_SPEC, _SEM_SPEC, pl.BlockSpec(memory_space=pl.ANY)],
        out_specs=tuple([_HBM_SPEC] * (2 * na)),
        input_output_aliases={i: i for i in range(2 * na)},
        compiler_params=pltpu.CompilerParams(has_side_effects=_DATAFLOW),
    )(*srcs, *lands, send_sems, recv_sems, after)
    return res[na:]


def _landing(block, me):
    buf = lax.empty((NDEV,) + block.shape, block.dtype)
    return lax.dynamic_update_slice_in_dim(buf, block[None], me, axis=0)


def _seg_kw(nseg, n_ctx, tm):
    return dict(nseg=nseg, seg_blocks=(n_ctx // tm if nseg == 2 else 0))


def _ffn_fwd(tag, h, gpre, gpost, shift, scale, gate, w, *, nseg, n_ctx, tm):
    n = h.shape[0]
    kw = _seg_kw(nseg, n_ctx, tm)
    (u,) = _rowwise(tag + "_pre", _pre_fwd_fn, n, [h], [("full", gpre), ("seg", shift), ("seg", scale)],
                    [(D_MODEL, BF16)], tm=tm, **kw)
    s, a, b = _mm_glu(u, w["win"], name=tag + "_glu")
    if "late" in w:
        w.update(w.pop("late")(s))
    y, ho = _mm_rows(s, w["wout"], functools.partial(_out_post_fn, 0.5), [h], [("full", gpost), ("seg", gate)],
                     [(D_MODEL, F32), (D_MODEL, F32)], name=tag + "_out", tk=FFN_DIM, n_ctx=n_ctx)
    return ho, dict(h=h, u=u, s=s, a=a, b=b, y=y)


def _ffn_bwd(tag, dho, sv, gpre, gpost, scale, gate, w, put, *, nseg, n_ctx, tm):
    n = dho.shape[0]
    kw = _seg_kw(nseg, n_ctx, tm)
    dy, dgate, dgpost = _rowwise(tag + "_postb", functools.partial(_post_bwd_fn, 0.5), n, [dho, sv["y"]],
                                 [("full", gpost), ("seg", gate)], [(D_MODEL, BF16)], [D_MODEL, D_MODEL], tm=tm, **kw)
    tok = put("w_out", _mm_tn(sv["s"], dy, name=tag + "_dwout", tm=1408, tn=1024))
    ds = _mm(dy, w["wout"], out_dtype=F32, name=tag + "_ds", tn=1408, rhs_t=True)
    (dp,) = _rowwise(tag + "_glub", _glu_bwd_fn, n, [ds, sv["a"], sv["b"]], [], [(2 * FFN_DIM, BF16)], tm=min(tm, 128))
    tok2 = put("w_in", _mm_tn(sv["u"], dp, name=tag + "_dwin", tn=1408))
    for t in (tok, tok2):
        if t is not None:
            gpre = gpre + t
    dh, dshift, dscale, dgpre = _mm_rows(dp, w["win"], _pre_bwd_fn, [sv["h"], dho], [("full", gpre), ("seg", scale)],
                                         [(D_MODEL, F32)], [D_MODEL, D_MODEL, D_MODEL], name=tag + "_du",
                                         rhs_t=True, n_ctx=n_ctx)
    return dh, None, dict(shift=dshift, scale=dscale, gate=dgate, gpre=dgpre, gpost=dgpost)


def _local_step(x, ctx, target, mods, norm_g, get_w, small, put_grad):
    t_len, n_ctx = x.shape[0], ctx.shape[0]
    n0 = t_len + n_ctx
    tm0 = _pick(n_ctx, 256, 8)
    tm1 = _pick(t_len, 256, 8)
    ncc = n_ctx // CHUNK
    g = {}

    def modrow(i, k, nseg):
        mc, mx = mods[i]
        if nseg == 2:
            return jnp.stack([mc[k], mx[k]])[:, None, :]
        return mx[k][None, None, :]

    pending = [None]

    def gvec(i, k):
        v = norm_g[i, k][None, :]
        if pending[0] is not None:
            v = v + pending[0]
            pending[0] = None
        return v

    xc = jnp.concatenate([ctx, x], axis=0)
    L0 = dict(nseg=2, n_ctx=n_ctx, tm=tm0)
    wts = dict(get_w("ffn00", xc))
    h1, sv_f01 = _ffn_fwd("l0f1", xc, gvec(0, 0), gvec(0, 1), modrow(0, 0, 2), modrow(0, 1, 2), modrow(0, 2, 2),
                          wts["ffn00"], **L0)
    kw0 = _seg_kw(2, n_ctx, tm0)
    (um0,) = _rowwise("l0m_pre", _pre_fwd_fn, n0, [h1], [("full", gvec(0, 2)), ("seg", modrow(0, 3, 2)),
                                                         ("seg", modrow(0, 4, 2))], [(D_MODEL, BF16)], tm=tm0, **kw0)
    wts.update(get_w("ssd", um0))
    z = _mm(um0, wts["ssd_win"], out_dtype=F32, name="ssd_z", n=SSD_INNER)
    xbc_pre = _mm(um0, wts["ssd_win"], out_dtype=F32, name="ssd_xbc", n=SSD_CONV_DIM, b_off=(0, SSD_INNER // 1024))
    dtr = _mm(um0, wts["ssd_wdt"], out_dtype=F32, name="ssd_dt")
    cpre, xbc = _conv_fwd(xbc_pre, small["conv_w8"], small["conv_b"], n_ctx=n_ctx, name="ssd_conv")
    nh = SSD_HEADS
    dt_dir = [dtr[:, :nh], dtr[:, nh:2 * nh]]
    dtT_dir = [d.T for d in dt_dir]
    bias_r = [small["dt_bias"][d][None, :] for d in range(2)]
    bias_c = [small["dt_bias"][d][:, None] for d in range(2)]
    alog_r = [small["a_log"][d][None, :] for d in range(2)]
    alog_c = [small["a_log"][d][:, None] for d in range(2)]
    ys, hss = [], []
    for d in range(2):
        yd, hsd = _ssd_scan_fwd(xbc, dt_dir[d], dtT_dir[d], bias_r[d], bias_c[d], alog_r[d], alog_c[d],
                                rev=(d == 1), n_ctx_chunks=ncc, name=f"ssd_scan{d}")
        ys.append(yd)
        hss.append(hsd)
    dvec = jnp.repeat(small["ssd_d"], SSD_HEAD_DIM)[None, :]
    ngv = small["ssd_norm_g"][None, :]
    gate_rows = [ys[0], ys[1], (xbc, SSD_INNER, 0, 0), z]
    lat = lambda r: (r[0], r[1], r[2], ncc) if isinstance(r, tuple) else (r, r.shape[1], 0, ncc)
    (yn,) = _rowwise("ssd_gate", _ssdgate_fwd_fn, t_len, [lat(r) for r in gate_rows],
                     [("full", dvec), ("full", ngv)], [(SSD_INNER, BF16)], tm=CHUNK)
    h1x = h1[n_ctx:]
    L1 = dict(nseg=1, n_ctx=0, tm=tm1)
    yo0, h2 = _mm_rows(yn, wts["ssd_wout"], functools.partial(_out_post_fn, 1.0), [h1x],
                       [("full", gvec(0, 3)), ("seg", modrow(0, 5, 1))], [(D_MODEL, F32), (D_MODEL, F32)],
                       name="ssd_out", tk=SSD_INNER)
    wts.update(get_w("ffn01", h2))
    h3, sv_f02 = _ffn_fwd("l0f2", h2, gvec(0, 4), gvec(0, 5), modrow(0, 6, 1), modrow(0, 7, 1), modrow(0, 8, 1),
                          wts["ffn01"], **L1)

    wts.update(get_w("ffn10", h3))
    h4, sv_f11 = _ffn_fwd("l1f1", h3, gvec(1, 0), gvec(1, 1), modrow(1, 0, 1), modrow(1, 1, 1), modrow(1, 2, 1),
                          wts["ffn10"], **L1)
    (um1,) = _rowwise("l1m_pre", _pre_fwd_fn, t_len, [h4], [("full", gvec(1, 2)), ("seg", modrow(1, 3, 1)),
                                                            ("seg", modrow(1, 4, 1))], [(D_MODEL, BF16)], tm=tm1)
    wts.update(get_w("gm", um1))
    p1 = _mm(um1, wts["gm_win"], out_dtype=F32, name="gm_in")
    vg = small["gm_v_g"][None, :]
    vb = small["gm_v_b"][None, :]
    gu, gvn = _rowwise("gm_act", _gm_act_fwd_fn, t_len, [p1], [("full", vg), ("full", vb)],
                       [(GM_INNER, F32), (GM_INNER, BF16)], tm=128)
    ws_bf = small["gm_w_s"].astype(BF16)
    wst_bf = jnp.swapaxes(small["gm_w_s"], 1, 2).astype(BF16)
    bst = small["gm_b_s"].T
    tgm = _gm_spatial_fwd(gu, gvn, ws_bf, bst, name="gm_spatial")
    yo1, h5 = _mm_rows(tgm, wts["gm_wout"], functools.partial(_out_post_fn, 1.0), [h4],
                       [("full", gvec(1, 3)), ("seg", modrow(1, 5, 1))], [(D_MODEL, F32), (D_MODEL, F32)],
                       name="gm_out", tk=GM_INNER)
    wts.update(get_w("ffn11", h5))
    h6, sv_f12 = _ffn_fwd("l1f2", h5, gvec(1, 4), gvec(1, 5), modrow(1, 6, 1), modrow(1, 7, 1), modrow(1, 8, 1),
                          wts["ffn11"], **L1)

    dh, loss_parts = _rowwise("loss", _loss_fn, t_len, [h6, target], [], [(D_MODEL, F32)], [D_MODEL], tm=tm1)

    zero = jnp.zeros((D_MODEL,), F32)
    dmx = [[zero] * N_MOD for _ in range(2)]
    dmc = [[zero] * N_MOD for _ in range(2)]
    dng = [[zero] * 6 for _ in range(2)]

    def put_mod(i, k, acc):
        if acc.shape[0] == 2:
            dmc[i][k] = dmc[i][k] + acc[0, 0]
            dmx[i][k] = dmx[i][k] + acc[1, 0]
        else:
            dmx[i][k] = dmx[i][k] + acc[0, 0]

    def put_g(i, k, acc):
        dng[i][k] = dng[i][k] + jnp.sum(acc[:, 0], axis=0)

    def ffn_back(tag, i, j, dho, sv, w, lay):
        nseg = lay["nseg"]
        base = 0 if j == 0 else 6
        gi = 0 if j == 0 else 4
        dh_in, pending[0], s = _ffn_bwd(tag, dho, sv, gvec(i, gi), gvec(i, gi + 1), modrow(i, base + 1, nseg),
                                        modrow(i, base + 2, nseg), w, functools.partial(put_grad, f"ffn{i}{j}"), **lay)
        put_mod(i, base, s["shift"])
        put_mod(i, base + 1, s["scale"])
        put_mod(i, base + 2, s["gate"])
        put_g(i, gi, s["gpre"])
        put_g(i, gi + 1, s["gpost"])
        return dh_in

    dh = ffn_back("l1f2", 1, 1, dh, sv_f12, wts["ffn11"], L1)
    dyo, dgate, dgp = _rowwise("l1m_postb", functools.partial(_post_bwd_fn, 1.0), t_len, [dh, yo1],
                               [("full", gvec(1, 3)), ("seg", modrow(1, 5, 1))], [(D_MODEL, BF16)],
                               [D_MODEL, D_MODEL], tm=tm1)
    put_mod(1, 5, dgate)
    put_g(1, 3, dgp)
    put_grad("gm", "w_out", _mm_tn(tgm, dyo, name="gm_dwout", tn=1024))
    dtg = _mm(dyo, wts["gm_wout"], out_dtype=F32, name="gm_dt", rhs_t=True)
    dgu, dgvn, dws, dbst = _gm_spatial_bwd(dtg, gu, gvn, ws_bf, wst_bf, bst, name="gm_spatialb")
    g["gm_w_s"] = dws
    g["gm_b_s"] = dbst.T
    dp1, dvg, dvb = _rowwise("gm_actb", _gm_act_bwd_fn, t_len, [p1, dgu, dgvn], [("full", vg)],
                             [(2 * GM_INNER, BF16)], [GM_INNER, GM_INNER], tm=128)
    g["gm_v_g"] = dvg[0, 0]
    g["gm_v_b"] = dvb[0, 0]
    pending[0] = put_grad("gm", "w_in", _mm_tn(um1, dp1, name="gm_dwin", tm=1024))
    dh, dsh, dsc, dgp = _mm_rows(dp1, wts["gm_win"], _pre_bwd_fn, [h4, dh],
                                 [("full", gvec(1, 2)), ("seg", modrow(1, 4, 1))], [(D_MODEL, F32)],
                                 [D_MODEL, D_MODEL, D_MODEL], name="gm_dum", tk=1024, rhs_t=True)
    put_mod(1, 3, dsh)
    put_mod(1, 4, dsc)
    put_g(1, 2, dgp)
    dh = ffn_back("l1f1", 1, 0, dh, sv_f11, wts["ffn10"], L1)

    dh = ffn_back("l0f2", 0, 1, dh, sv_f02, wts["ffn01"], L1)
    dyo, dgate, dgp = _rowwise("l0m_postb", functools.partial(_post_bwd_fn, 1.0), t_len, [dh, yo0],
                               [("full", gvec(0, 3)), ("seg", modrow(0, 5, 1))], [(D_MODEL, BF16)],
                               [D_MODEL, D_MODEL], tm=tm1)
    put_mod(0, 5, dgate)
    put_g(0, 3, dgp)
    put_grad("ssd", "w_out", _mm_tn(yn, dyo, name="ssd_dwout", tn=1024))
    dyn = _mm(dyo, wts["ssd_wout"], out_dtype=F32, name="ssd_dyn", rhs_t=True)
    dy_ssd, dz, dngv, ddv = _rowwise("ssd_gateb", _ssdgate_bwd_fn, n0, [(dyn, SSD_INNER, 0, -ncc)] + gate_rows,
                                     [("full", dvec), ("full", ngv)], [(SSD_INNER, F32), (SSD_INNER, BF16)],
                                     [SSD_INNER, SSD_INNER], tm=128)
    g["ssd_norm_g"] = dngv[0, 0]
    g["ssd_D"] = jnp.sum(ddv[0, 0].reshape(SSD_HEADS, SSD_HEAD_DIM), axis=1)
    dxbcs, ddts, dalogs, dbiases = [], [], [], []
    for d in range(2):
        dxd, ddtd, dal, dbi = _ssd_scan_bwd(dy_ssd, xbc, hss[d], dt_dir[d], dtT_dir[d], bias_r[d], bias_c[d],
                                            alog_r[d], alog_c[d], dvec, rev=(d == 1), n_ctx_chunks=ncc,
                                            direct=(d == 0), name=f"ssd_scanb{d}")
        dxbcs.append(dxd)
        ddts.append(ddtd)
        dalogs.append(dal[0])
        dbiases.append(dbi[0])
    g["ssd_A_log"] = jnp.stack(dalogs)
    g["ssd_dt_bias"] = jnp.stack(dbiases)
    dxbc_pre, dcw8, dcb = _conv_bwd(dxbcs[0], dxbcs[1], cpre, xbc_pre, small["conv_w8"], n_ctx=n_ctx, name="ssd_convb")
    g["ssd_conv_w"] = dcw8[:SSD_CONV]
    g["ssd_conv_b"] = dcb[0]
    ddt_bf = jnp.concatenate([ddts[0], ddts[1], jnp.zeros((n0, 128 - 2 * nh), F32)], axis=1).astype(BF16)
    dw_ssd_in = jnp.concatenate([
        _mm_tn(um0, dz, name="ssd_dwz", tm=1024),
        _mm_tn(um0, dxbc_pre, name="ssd_dwxbc", tm=1024),
        _mm_tn(um0, ddt_bf, name="ssd_dwdt", tm=1024)[:, :2 * nh]], axis=1)
    pending[0] = put_grad("ssd", "w_in", dw_ssd_in)
    win_ssd = wts["ssd_win"]
    dum0 = _mm(dz, win_ssd, out_dtype=F32, name="ssd_dum_z", tk=1024, rhs_t=True, n=D_MODEL)
    dum0 = _mm(dxbc_pre, win_ssd, out_dtype=F32, name="ssd_dum_x", tk=1024, rhs_t=True, n=D_MODEL,
               b_off=(0, SSD_INNER // 1024), add=dum0)
    dum0 = _mm(ddt_bf, wts["ssd_wdt"], out_dtype=F32, name="ssd_dum_dt", rhs_t=True, add=dum0)
    dh0, dsh, dsc, dgp = _rowwise("l0m_preb", _pre_bwd_fn, n0, [dum0, h1, (dh, D_MODEL, 0, -(n_ctx // tm0))],
                                  [("full", gvec(0, 2)), ("seg", modrow(0, 4, 2))], [(D_MODEL, F32)],
                                  [D_MODEL, D_MODEL, D_MODEL], tm=tm0, **kw0)
    put_mod(0, 3, dsh)
    put_mod(0, 4, dsc)
    put_g(0, 2, dgp)
    dh0 = ffn_back("l0f1", 0, 0, dh0, sv_f01, wts["ffn00"], L0)
    grad_x = dh0[n_ctx:]
    g["norm_g"] = jnp.stack([jnp.stack(r) for r in dng])
    g["dmx"] = jnp.stack([jnp.concatenate(r) for r in dmx])
    g["dmc"] = jnp.stack([jnp.concatenate(r) for r in dmc])
    return loss_parts[0], grad_x, g


GROUPS = ("ffn00", "ssd", "ffn01", "ffn10", "gm", "ffn11")


def _mats_in(group, win_l):
    k, nloc = win_l.shape[1], win_l.shape[2]
    win = jnp.transpose(win_l, (1, 0, 2)).reshape(k, NDEV * nloc)
    if group.startswith("ffn"):
        return dict(win=win)
    if group == "gm":
        return dict(gm_win=win)
    assert group == "ssd"
    c1 = SSD_INNER + SSD_CONV_DIM
    return dict(ssd_win=win, ssd_wdt=jnp.pad(win[:, c1:], ((0, 0), (0, 128 - 2 * SSD_HEADS))))


def _mats_out(group, wout_l):
    pre = "" if group.startswith("ffn") else group + "_"
    return {pre + "wout": wout_l.reshape(-1, wout_l.shape[2])}


def _group_mats(group, lands):
    m = {**_mats_in(group, lands[0]), **_mats_out(group, lands[1])}
    return {group: m} if group.startswith("ffn") else m


def _grad_blocks(which, grad):
    if which == "w_in":
        k, n = grad.shape
        return jnp.transpose(grad.reshape(k, NDEV, n // NDEV), (1, 0, 2)).astype(BF16)
    return grad.reshape(NDEV, grad.shape[0] // NDEV, grad.shape[1]).astype(BF16)


def kernel(x, c, ctx, c_ctx, ada_w, ada_b, norm_g, ffn_w_in, ffn_w_out, ssd_w_in, ssd_conv_w, ssd_conv_b, ssd_dt_bias, ssd_A_log, ssd_D, ssd_norm_g, ssd_w_out, gm_w_in, gm_v_g, gm_v_b, gm_w_s, gm_b_s, gm_w_out, loss_target, m_c_ctx, m_ada_w, m_ada_b, m_norm_g, m_ffn_w_in, m_ffn_w_out, m_ssd_w_in, m_ssd_conv_w, m_ssd_conv_b, m_ssd_dt_bias, m_ssd_A_log, m_ssd_D, m_ssd_norm_g, m_ssd_w_out, m_gm_w_in, m_gm_v_g, m_gm_v_b, m_gm_w_s, m_gm_b_s, m_gm_w_out, v_c_ctx, v_ada_w, v_ada_b, v_norm_g, v_ffn_w_in, v_ffn_w_out, v_ssd_w_in, v_ssd_conv_w, v_ssd_conv_b, v_ssd_dt_bias, v_ssd_A_log, v_ssd_D, v_ssd_norm_g, v_ssd_w_out, v_gm_w_in, v_gm_v_g, v_gm_v_b, v_gm_w_s, v_gm_b_s, v_gm_w_out):
    me = 4 * lax.axis_index("x") + 2 * lax.axis_index("y") + lax.axis_index("c")
    d = D_MODEL
    ncol = N_MOD * d // NDEV

    small_pack = jnp.concatenate([c.reshape(-1), norm_g.reshape(-1), ssd_conv_w.reshape(-1),
                                  gm_v_g.reshape(-1), gm_v_b.reshape(-1)])[None, :]
    (sp,), _ = _exchange([small_pack], scatter=False, name="gather_small")
    sp = sp[:, 0]
    o = 0
    c_all = sp[:, o:o + d]; o += d
    ng_all = sp[:, o:o + 2 * 6 * 128].reshape(NDEV, 2, 6, 128); o += 2 * 6 * 128
    cw_all = sp[:, o:o + SSD_CONV * 512].reshape(NDEV, SSD_CONV, 512); o += SSD_CONV * 512
    vg_all = sp[:, o:o + 256]; o += 256
    vb_all = sp[:, o:o + 256]; o += 256
    norm_g_full = jnp.transpose(ng_all, (1, 2, 0, 3)).reshape(2, 6, d)
    conv_w_full = jnp.transpose(cw_all, (1, 0, 2)).reshape(SSD_CONV, SSD_CONV_DIM)
    gm_v_g_full = vg_all.reshape(-1)
    gm_v_b_full = vb_all.reshape(-1)

    c16 = jnp.concatenate([c_all, jnp.broadcast_to(c_ctx[None, :], (NDEV, d))], axis=0)
    ada_b_loc = lax.dynamic_slice_in_dim(ada_b, me * ncol, ncol, axis=1)
    mods_loc = jnp.stack([_mm_f32(c16, ada_w[i], name=f"ada_mod{i}", silu_a=True, bias=ada_b_loc[i][None, :])
                          for i in range(2)])
    (mods_all,), mods_done = _exchange([mods_loc], scatter=False, name="gather_mods")

    shard = {"ssd": (ssd_w_in[0], ssd_w_out[0]), "gm": (gm_w_in[0], gm_w_out[0])}
    for i in range(2):
        for j in range(2):
            shard[f"ffn{i}{j}"] = (ffn_w_in[i, j], ffn_w_out[i, j])
    first = GROUPS[0]
    units = [(first + "_in", first, (0,)), (first + "_out", first, (1,))] + [(grp, grp, (0, 1)) for grp in GROUPS[1:]]
    gathers = {}
    started = mods_done
    for unit, grp, idx in units:
        srcs = [(shard[grp][k] + started).astype(BF16) for k in idx]
        st = _exchange_start(srcs, [_landing(s, me) for s in srcs], scatter=False, name="gather_start_" + unit)
        gathers[unit] = st[:4]
        started = st[4]

    def fetch(unit, after):
        return _exchange_wait(*gathers[unit], after, scatter=False, name="gather_wait_" + unit)

    def get_w(grp, after):
        if grp != first:
            return _group_mats(grp, fetch(grp, after))
        late = lambda later: _mats_out(grp, fetch(grp + "_out", later)[0])
        return {grp: dict(_mats_in(grp, fetch(grp + "_in", after)[0]), late=late)}

    scatters = {}
    held = {}

    def put_grad(grp, which, grad):
        if grp == first:
            unit, blocks = grp + "_" + which[2:], [_grad_blocks(which, grad)]
        else:
            held[grp, which] = _grad_blocks(which, grad)
            if (grp, "w_in") not in held or (grp, "w_out") not in held:
                return None
            unit, blocks = grp, [held[grp, "w_in"], held[grp, "w_out"]]
        lands = [_landing(lax.dynamic_index_in_dim(b, me, axis=0, keepdims=False), me) for b in blocks]
        st = _exchange_start(blocks, lands, scatter=True, name="scatter_start_" + unit)
        scatters[unit] = st[:4]
        return st[4]

    mods_rows = jnp.transpose(mods_all, (1, 2, 0, 3)).reshape(2, 2 * NDEV, N_MOD * d) + started
    mx = lax.dynamic_index_in_dim(mods_rows, me, axis=1, keepdims=False).reshape(2, N_MOD, d)
    mc = mods_rows[:, NDEV].reshape(2, N_MOD, d)
    mods = [(mc[i], mx[i]) for i in range(2)]

    small = dict(conv_w8=jnp.pad(conv_w_full, ((0, 8 - SSD_CONV), (0, 0))), conv_b=ssd_conv_b, dt_bias=ssd_dt_bias[0],
                 a_log=ssd_A_log[0], ssd_d=ssd_D[0], ssd_norm_g=ssd_norm_g[0], gm_v_g=gm_v_g_full,
                 gm_v_b=gm_v_b_full, gm_w_s=gm_w_s[0], gm_b_s=gm_b_s[0])
    loss_parts, grad_x, g = _local_step(x[0], ctx[0], loss_target[0], mods, norm_g_full, get_w, small, put_grad)
    loss = lax.psum(0.5 / d * jnp.sum(loss_parts), ("x", "y", "c"))

    whole = {"ffn_w_in": (ffn_w_in, m_ffn_w_in, v_ffn_w_in), "ffn_w_out": (ffn_w_out, m_ffn_w_out, v_ffn_w_out),
             "ssd_w_in": (ssd_w_in, m_ssd_w_in, v_ssd_w_in), "ssd_w_out": (ssd_w_out, m_ssd_w_out, v_ssd_w_out),
             "gm_w_in": (gm_w_in, m_gm_w_in, v_gm_w_in), "gm_w_out": (gm_w_out, m_gm_w_out, v_gm_w_out)}
    res = {}
    after = grad_x
    for unit, grp, idx in reversed(units):
        parts = _exchange_wait(*scatters[unit], after, scatter=True, name="scatter_wait_" + unit)
        for k, p in zip(idx, parts):
            which = ("in", "out")[k]
            nm = ("ffn" if grp.startswith("ffn") else grp) + "_w_" + which
            sel = (int(grp[3]), int(grp[4])) if grp.startswith("ffn") else (0,)
            res[nm] = _adamw(p, *whole[nm], name=f"adamw_{grp}_{which}", sel=sel, into=res.get(nm))
            after = res[nm][0]

    sg_names = ["dmx", "dmc", "norm_g", "ssd_conv_w", "ssd_conv_b", "ssd_dt_bias", "ssd_A_log", "ssd_D", "ssd_norm_g",
                "gm_v_g", "gm_v_b", "gm_w_s", "gm_b_s"]
    sg_shapes = [g[n].shape for n in sg_names]
    flat = jnp.concatenate([g[n].reshape(-1) for n in sg_names])
    npack = flat.shape[0]
    pad = (-npack) % 1024
    flat = jnp.pad(flat, (0, pad)).reshape(-1, 128)
    (sg_all,), _ = _exchange([flat], scatter=False, name="gather_small_grads")
    sg_sum = _sum_slots(sg_all, name="sum_small_grads").reshape(-1)[:npack]
    sums = {}
    o = 0
    for n, shp in zip(sg_names, sg_shapes):
        sz = math.prod(shp)
        sums[n] = sg_sum[o:o + sz].reshape(shp)
        o += sz
    per_dev = sg_all.reshape(NDEV, -1)
    dmx_all = per_dev[:, :2 * N_MOD * d].reshape(NDEV, 2, N_MOD * d)
    dmc_all = per_dev[:, 2 * N_MOD * d:4 * N_MOD * d].reshape(NDEV, 2, N_MOD * d)

    (s16,) = _rowwise("ada_silu", lambda cc: ((_silu(cc),), ()), 2 * NDEV, [c16], [], [(d, F32)], tm=2 * NDEV)
    s16_t = s16.T
    g_ada_w, dcc_parts = [], []
    for i in range(2):
        rhs = jnp.concatenate([lax.dynamic_slice_in_dim(dmx_all[:, i], me * ncol, ncol, axis=1),
                               lax.dynamic_slice_in_dim(dmc_all[:, i], me * ncol, ncol, axis=1)], axis=0)
        g_ada_w.append(_mm_f32(s16_t, rhs, name=f"ada_dw{i}"))
        dmc_loc = lax.dynamic_slice_in_dim(sums["dmc"][i], me * ncol, ncol, axis=0)
        rhs_c = jnp.zeros((ncol, 128), F32).at[:, 0].set(dmc_loc)
        dcc_parts.append(_mm_f32(ada_w[i], rhs_c, name=f"ada_dcc{i}")[:, 0])
    g_ada_w = jnp.stack(g_ada_w)
    dcc_part = (dcc_parts[0] + dcc_parts[1]).reshape(8, 128)
    (dcc_all,), _ = _exchange([dcc_part], scatter=False, name="gather_dcc")
    g_c_ctx = _sum_slots(dcc_all, name="sum_dcc", scale_by=c_ctx.reshape(8, 128)).reshape(d)
    g_ada_b = sums["dmx"] + sums["dmc"]

    outs = _adamw(g_ada_w.reshape(1, -1, ncol), ada_w.reshape(-1, ncol), m_ada_w.reshape(-1, ncol),
                  v_ada_w.reshape(-1, ncol), name="adamw_ada_w")
    res["ada_w"] = [o_.reshape(ada_w.shape) for o_ in outs]

    loc = lambda a, ax, n: lax.dynamic_slice_in_dim(a, me * n, n, axis=ax)
    small_g = dict(c_ctx=g_c_ctx, ada_b=g_ada_b, norm_g=loc(sums["norm_g"], 2, 128),
                   ssd_conv_w=loc(sums["ssd_conv_w"], 1, 512)[None], ssd_conv_b=sums["ssd_conv_b"][None],
                   ssd_dt_bias=sums["ssd_dt_bias"][None], ssd_A_log=sums["ssd_A_log"][None], ssd_D=sums["ssd_D"][None],
                   ssd_norm_g=sums["ssd_norm_g"][None], gm_v_g=loc(sums["gm_v_g"], 0, 256)[None],
                   gm_v_b=loc(sums["gm_v_b"], 0, 256)[None], gm_w_s=sums["gm_w_s"][None], gm_b_s=sums["gm_b_s"][None])
    small_w = dict(c_ctx=(c_ctx, m_c_ctx, v_c_ctx), ada_b=(ada_b, m_ada_b, v_ada_b), norm_g=(norm_g, m_norm_g, v_norm_g),
                   ssd_conv_w=(ssd_conv_w, m_ssd_conv_w, v_ssd_conv_w), ssd_conv_b=(ssd_conv_b, m_ssd_conv_b, v_ssd_conv_b),
                   ssd_dt_bias=(ssd_dt_bias, m_ssd_dt_bias, v_ssd_dt_bias), ssd_A_log=(ssd_A_log, m_ssd_A_log, v_ssd_A_log),
                   ssd_D=(ssd_D, m_ssd_D, v_ssd_D), ssd_norm_g=(ssd_norm_g, m_ssd_norm_g, v_ssd_norm_g),
                   gm_v_g=(gm_v_g, m_gm_v_g, v_gm_v_g), gm_v_b=(gm_v_b, m_gm_v_b, v_gm_v_b),
                   gm_w_s=(gm_w_s, m_gm_w_s, v_gm_w_s), gm_b_s=(gm_b_s, m_gm_b_s, v_gm_b_s))
    sn = list(small_w)

    def pack(arrs):
        f = jnp.concatenate([a.reshape(-1) for a in arrs])
        return jnp.pad(f, (0, (-f.shape[0]) % 1024)).reshape(-1, 128)

    pg = pack([small_g[n].reshape(small_w[n][0].shape) for n in sn])
    outs = _adamw(pg[None], pack([small_w[n][0] for n in sn]), pack([small_w[n][1] for n in sn]),
                  pack([small_w[n][2] for n in sn]), name="adamw_small")
    flat_outs = [o_.reshape(-1) for o_ in outs]
    o = 0
    for n in sn:
        shp = small_w[n][0].shape
        sz = math.prod(shp)
        res[n] = [fo[o:o + sz].reshape(shp) for fo in flat_outs]
        o += sz

    order = ["c_ctx", "ada_w", "ada_b", "norm_g", "ffn_w_in", "ffn_w_out", "ssd_w_in", "ssd_conv_w", "ssd_conv_b",
             "ssd_dt_bias", "ssd_A_log", "ssd_D", "ssd_norm_g", "ssd_w_out", "gm_w_in", "gm_v_g", "gm_v_b", "gm_w_s",
             "gm_b_s", "gm_w_out"]
    result = [loss, grad_x[None]]
    for k in range(4):
        result += [res[n][k] for n in order]
    return tuple(result)
```

```python
import functools
import math

import jax
import jax.numpy as jnp
from jax import lax
from jax.experimental import pallas as pl
from jax.experimental.pallas import tpu as pltpu

F32 = jnp.float32
BF16 = jnp.bfloat16

NDEV = 8
D_MODEL = 1024
FFN_DIM = 2816
N_MOD = 9
EPS = 1e-6
SSD_INNER = 2048
SSD_HEADS = 32
SSD_HEAD_DIM = 64
SSD_GROUPS = 8
SSD_HPG = 4
SSD_STATE = 128
SSD_CONV = 5
SSD_CONV_DIM = 4096
CHUNK = 128
GM_INNER = 2048
GM_GROUPS = 8
GM_GROUP_DIM = 256
ADAM_LR = 0.001
ADAM_B1 = 0.9
ADAM_B2 = 0.999
ADAM_EPS = 1e-08
ADAM_WD = 0.01
ADAM_STEP = 10
NEG_BIG = -1e30
VMEM_LIMIT_BYTES = 56 * 1024 * 1024
HI = lax.Precision.HIGHEST


def _params(*sem):
    return pltpu.CompilerParams(dimension_semantics=sem, vmem_limit_bytes=VMEM_LIMIT_BYTES)


def _pick(n, target, mult=16):
    if n <= target:
        return n
    for t in range(target - target % mult, 0, -mult):
        if n % t == 0:
            return t
    raise ValueError((n, target, mult))


def _sig(x):
    return 0.5 * jnp.tanh(0.5 * x) + 0.5


def _silu(x):
    return x * _sig(x)


def _dsilu(x):
    s = _sig(x)
    return s * (1.0 + x * (1.0 - s))


_GELU_C = math.sqrt(2.0 / math.pi)


def _gelu(x):
    return 0.5 * x * (1.0 + jnp.tanh(_GELU_C * (x + 0.044715 * x * x * x)))


def _dgelu(x):
    t = jnp.tanh(_GELU_C * (x + 0.044715 * x * x * x))
    return 0.5 * (1.0 + t) + 0.5 * x * (1.0 - t * t) * _GELU_C * (1.0 + 3.0 * 0.044715 * x * x)


def _softplus(x):
    return jnp.maximum(x, 0.0) + jnp.log1p(jnp.exp(-jnp.abs(x)))


def _sum0(v):
    return jnp.sum(v, axis=0, keepdims=True)


def _rms(h):
    r = lax.rsqrt(jnp.mean(h * h, axis=-1, keepdims=True) + EPS)
    return h * r, r


def _dot(a, b, dims=((1,), (0,)), precision=None):
    return lax.dot_general(a, b, (dims, ((), ())), preferred_element_type=F32, precision=precision)


_NT = ((1,), (1,))
_TN = ((0,), (0,))


def _rowwise(name, fn, n_rows, rows, consts, outs, accs=(), *, tm, nseg=1, seg_blocks=0):
    assert n_rows % tm == 0
    if nseg == 2:
        assert seg_blocks > 0
        seg = lambda i: jnp.where(i < seg_blocks, 0, 1)
    else:
        seg = lambda i: 0
    in_specs, args, lacking = [], [], []
    for r in rows:
        arr, width, cb, off = r if isinstance(r, tuple) else (r, r.shape[1], 0, 0)
        in_specs.append(pl.BlockSpec((tm, width), lambda i, cb=cb, off=off: (jnp.maximum(i + off, 0), cb)))
        args.append(arr)
        lacking.append(-off if off < 0 else 0)
    for kind, arr in consts:
        if kind == "seg":
            assert arr.shape[0] == nseg and arr.shape[1] == 1, arr.shape
            in_specs.append(pl.BlockSpec((None, 1, arr.shape[2]), lambda i: (seg(i), 0, 0)))
        else:
            in_specs.append(pl.BlockSpec(arr.shape, lambda i: (0, 0)))
        args.append(arr)
    out_shape = [jax.ShapeDtypeStruct((n_rows, w), dt) for w, dt in outs]
    out_specs = [pl.BlockSpec((tm, w), lambda i: (i, 0)) for w, _ in outs]
    out_shape += [jax.ShapeDtypeStruct((nseg, 1, w), F32) for w in accs]
    out_specs += [pl.BlockSpec((None, 1, w), lambda i: (seg(i), 0, 0)) for w in accs]
    n_in, n_out, n_acc = len(args), len(outs), len(accs)

    def kern(*refs):
        i = pl.program_id(0)
        ins = [r[...] for r in refs[:n_in]]
        for k, lack in enumerate(lacking):
            if lack:
                ins[k] = jnp.where(i >= lack, ins[k], jnp.zeros_like(ins[k]))
        res, terms = fn(*ins)
        for ref, v in zip(refs[n_in:n_in + n_out], res):
            ref[...] = v.astype(ref.dtype)
        if n_acc:
            sums = [_sum0(v) for v in terms]
            first = (i == 0) | (i == seg_blocks) if nseg == 2 else (i == 0)
            acc_refs = refs[n_in + n_out:]

            @pl.when(first)
            def _():
                for ref, v in zip(acc_refs, sums):
                    ref[...] = v

            @pl.when(jnp.logical_not(first))
            def _():
                for ref, v in zip(acc_refs, sums):
                    ref[...] += v

    res = pl.pallas_call(
        kern, name=name, grid=(n_rows // tm,), in_specs=in_specs, out_specs=out_specs, out_shape=out_shape,
        compiler_params=_params("arbitrary"),
    )(*args)
    return res


def _pre_fwd_fn(h, g, shift, scale):
    hh, _ = _rms(h)
    return (hh * g * (1.0 + scale) + shift,), ()


def _pre_bwd_fn(du, h, dres, g, scale):
    hh, r = _rms(h)
    n = hh * g
    dn = du * (1.0 + scale)
    dhh = dn * g
    dh = dres + r * (dhh - hh * jnp.mean(dhh * hh, axis=-1, keepdims=True))
    return (dh,), (du, du * n, dn * hh)


def _post_fwd_fn(weight, h, y, g, gate):
    yh, _ = _rms(y)
    return (h + weight * gate * (yh * g),), ()


def _out_post_fn(weight, y, h, g, gate):
    return (y,) + _post_fwd_fn(weight, h, y, g, gate)[0], ()


def _post_bwd_fn(weight, dh, y, g, gate):
    yh, r = _rms(y)
    dr = dh * weight
    dyh = dr * gate * g
    dy = r * (dyh - yh * jnp.mean(dyh * yh, axis=-1, keepdims=True))
    return (dy,), (dr * yh * g, dr * gate * yh)


def _glu_bwd_fn(ds, a, b):
    a = a.astype(F32)
    b = b.astype(F32)
    sg = _sig(a)
    da = ds * b * (sg * (1.0 + a * (1.0 - sg)))
    db = ds * (a * sg)
    return (jnp.concatenate([da, db], axis=1),), ()


def _loss_fn(y, t):
    diff = y - t
    return (diff * (1.0 / D_MODEL),), (diff * diff,)


def _ssd_y(yf, yb, xs, z, dvec):
    y = yf + yb + dvec * xs
    return y, y * _silu(z)


def _ssdgate_fwd_fn(yf, yb, xs, z, dvec, ng):
    _, yg = _ssd_y(yf, yb, xs, z, dvec)
    parts = []
    for g in range(SSD_GROUPS):
        sl = slice(g * 256, (g + 1) * 256)
        parts.append(_rms(yg[:, sl])[0])
    return (jnp.concatenate(parts, axis=1) * ng,), ()


def _ssdgate_bwd_fn(dyn, yf, yb, xs, z, dvec, ng):
    y, yg = _ssd_y(yf, yb, xs, z, dvec)
    dyg_parts, ygh_parts = [], []
    for g in range(SSD_GROUPS):
        sl = slice(g * 256, (g + 1) * 256)
        ygh, r = _rms(yg[:, sl])
        d = dyn[:, sl] * ng[:, sl]
        dyg_parts.append(r * (d - ygh * jnp.mean(d * ygh, axis=-1, keepdims=True)))
        ygh_parts.append(ygh)
    dyg = jnp.concatenate(dyg_parts, axis=1)
    ygh = jnp.concatenate(ygh_parts, axis=1)
    dy = dyg * _silu(z)
    dz = dyg * y * _dsilu(z)
    return (dy, dz), (dyn * ygh, dy * xs)


def _ln_stats(v):
    mu = jnp.mean(v, axis=-1, keepdims=True)
    vc = v - mu
    r = lax.rsqrt(jnp.mean(vc * vc, axis=-1, keepdims=True) + EPS)
    return vc * r, r


def _gm_act_fwd_fn(p, vg, vb):
    gu = _gelu(p[:, :GM_INNER])
    gvh, _ = _ln_stats(_gelu(p[:, GM_INNER:]))
    return (gu, gvh * vg + vb), ()


def _gm_act_bwd_fn(p, dgu, dgvn, vg):
    pu = p[:, :GM_INNER]
    pv = p[:, GM_INNER:]
    gvh, r = _ln_stats(_gelu(pv))
    dgvh = dgvn * vg
    dgv = r * (dgvh - jnp.mean(dgvh, axis=-1, keepdims=True) - gvh * jnp.mean(dgvh * gvh, axis=-1, keepdims=True))
    dp = jnp.concatenate([dgu * _dgelu(pu), dgv * _dgelu(pv)], axis=1)
    return (dp,), (dgvn * gvh, dgvn)


def _mm(a, b, *, out_dtype, name, tm=1088, tn=1024, tk=1408, add=None, rhs_t=False, n=None, b_off=(0, 0)):
    m, k = a.shape
    if n is None:
        n, k2 = b.shape if rhs_t else b.shape[::-1]
        assert k == k2
    tm, tn, tk = _pick(m, tm), _pick(n, tn, 128), _pick(k, tk, 128)
    o0, o1 = b_off
    nk = k // tk
    dims = _NT if rhs_t else ((1,), (0,))

    def kern(*refs):
        a_ref, b_ref = refs[:2]
        add_ref = refs[2] if add is not None else None
        o_ref = refs[3] if add is not None else refs[2]

        def finish(r):
            if add is not None:
                r = r + add_ref[...]
            o_ref[...] = r.astype(o_ref.dtype)

        p = _dot(a_ref[...], b_ref[...], dims)
        if nk == 1:
            finish(p)
            return
        acc_ref = refs[-1]
        kk = pl.program_id(2)

        @pl.when(kk == 0)
        def _():
            acc_ref[...] = p

        @pl.when((kk > 0) & (kk < nk - 1))
        def _():
            acc_ref[...] += p

        @pl.when(kk == nk - 1)
        def _():
            finish(acc_ref[...] + p)

    if rhs_t:
        b_spec = pl.BlockSpec((tn, tk), lambda i, j, kk: (j + o0, kk + o1))
    else:
        b_spec = pl.BlockSpec((tk, tn), lambda i, j, kk: (kk + o0, j + o1))
    in_specs = [pl.BlockSpec((tm, tk), lambda i, j, kk: (i, kk)), b_spec]
    args = [a, b]
    if add is not None:
        in_specs.append(pl.BlockSpec((tm, tn), lambda i, j, kk: (i, j)))
        args.append(add)
    return pl.pallas_call(
        kern, name=name, grid=(m // tm, n // tn, nk), in_specs=in_specs,
        out_specs=pl.BlockSpec((tm, tn), lambda i, j, kk: (i, j)),
        out_shape=jax.ShapeDtypeStruct((m, n), out_dtype),
        scratch_shapes=[pltpu.VMEM((tm, tn), F32)] if nk > 1 else [],
        compiler_params=_params("parallel", "parallel", "arbitrary"),
    )(*args)


def _mm_rows(a, b, fn, rows, consts, outs, accs=(), *, name, tm=544, tk=1408, rhs_t=False, n_ctx=0):
    m, k = a.shape
    n = b.shape[0] if rhs_t else b.shape[1]
    tm, tk = _pick(m, tm), _pick(k, tk, 128)
    nk = k // tk
    dims = _NT if rhs_t else ((1,), (0,))
    n_rows, n_const, n_out, n_acc = len(rows), len(consts), len(outs), len(accs)

    def kern(*refs):
        a_ref, b_ref = refs[:2]
        row_refs = refs[2:2 + n_rows]
        const_refs = refs[2 + n_rows:2 + n_rows + n_const]
        out_refs = refs[2 + n_rows + n_const:2 + n_rows + n_const + n_out]
        acc_refs = refs[2 + n_rows + n_const + n_out:2 + n_rows + n_const + n_out + n_acc]
        i, kk = pl.program_id(0), pl.program_id(1)

        def finish(p):
            is_ctx = (i * tm + lax.broadcasted_iota(jnp.int32, (tm, 1), 0)) < n_ctx
            cvals = []
            for (kind, arr), ref in zip(consts, const_refs):
                if kind == "seg":
                    cvals.append(jnp.where(is_ctx, ref[0], ref[1]) if arr.shape[0] == 2 else ref[0])
                else:
                    cvals.append(ref[...])
            res, terms = fn(p, *[r[...] for r in row_refs], *cvals)
            for ref, v in zip(out_refs, res):
                ref[...] = v.astype(ref.dtype)
            for ref, v in zip(acc_refs, terms):
                s_all = _sum0(v)
                s_ctx = _sum0(jnp.where(is_ctx, v, 0.0)) if n_ctx else jnp.zeros_like(s_all)
                both = jnp.concatenate([s_ctx, s_all - s_ctx], axis=0)[:, None, :]

                @pl.when(i == 0)
                def _():
                    ref[...] = both

                @pl.when(i > 0)
                def _():
                    ref[...] += both

        p = _dot(a_ref[...], b_ref[...], dims)
        if nk == 1:
            finish(p)
            return
        scr = refs[-1]

        @pl.when(kk == 0)
        def _():
            scr[...] = p

        @pl.when((kk > 0) & (kk < nk - 1))
        def _():
            scr[...] += p

        @pl.when(kk == nk - 1)
        def _():
            finish(scr[...] + p)

    b_spec = pl.BlockSpec((n, tk), lambda i, kk: (0, kk)) if rhs_t else pl.BlockSpec((tk, n), lambda i, kk: (kk, 0))
    in_specs = [pl.BlockSpec((tm, tk), lambda i, kk: (i, kk)), b_spec]
    in_specs += [pl.BlockSpec((tm, r.shape[1]), lambda i, kk: (i, 0)) for r in rows]
    for kind, arr in consts:
        in_specs.append(pl.BlockSpec(arr.shape, (lambda i, kk: (0, 0, 0)) if kind == "seg" else (lambda i, kk: (0, 0))))
    out_shape = [jax.ShapeDtypeStruct((m, w), dt) for w, dt in outs]
    out_specs = [pl.BlockSpec((tm, w), lambda i, kk: (i, 0)) for w, _ in outs]
    out_shape += [jax.ShapeDtypeStruct((2, 1, w), F32) for w in accs]
    out_specs += [pl.BlockSpec((2, 1, w), lambda i, kk: (0, 0, 0)) for w in accs]
    return pl.pallas_call(
        kern, name=name, grid=(m // tm, nk), in_specs=in_specs, out_specs=out_specs, out_shape=out_shape,
        scratch_shapes=[pltpu.VMEM((tm, n), F32)] if nk > 1 else [],
        compiler_params=_params("arbitrary", "arbitrary"),
    )(a, b, *rows, *[arr for _, arr in consts])


def _mm_glu(u, win, *, name, tm=2176, tn=256):
    m, k = u.shape
    n = win.shape[1] // 2
    tm, tn = _pick(m, tm), _pick(n, tn, 128)
    nj = n // tn

    def kern(u_ref, wa_ref, wb_ref, s_ref, a_ref, b_ref):
        uu = u_ref[...]
        a = jnp.dot(uu, wa_ref[...], preferred_element_type=F32)
        b = jnp.dot(uu, wb_ref[...], preferred_element_type=F32)
        s_ref[...] = (_silu(a) * b).astype(BF16)
        a_ref[...] = a.astype(BF16)
        b_ref[...] = b.astype(BF16)

    ospec = pl.BlockSpec((tm, tn), lambda i, j: (i, j))
    return pl.pallas_call(
        kern, name=name, grid=(m // tm, nj),
        in_specs=[pl.BlockSpec((tm, k), lambda i, j: (i, 0)), pl.BlockSpec((k, tn), lambda i, j: (0, j)),
                  pl.BlockSpec((k, tn), lambda i, j: (0, nj + j))],
        out_specs=[ospec, ospec, ospec],
        out_shape=[jax.ShapeDtypeStruct((m, n), BF16)] * 3,
        compiler_params=_params("parallel", "parallel"),
    )(u, win, win)


def _mm_tn(a, b, *, name, tm=1024, tn=1024, tk=1088, col_blocks=None):
    t, m = a.shape
    t2, n = b.shape
    assert t == t2
    tm, tn, tk = _pick(m, tm, 128), _pick(n, tn, 128), _pick(t, tk)
    nk = t // tk
    if col_blocks is None:
        def kern(a_ref, b_ref, o_ref):
            kk = pl.program_id(2)

            @pl.when(kk == 0)
            def _():
                o_ref[...] = jnp.zeros_like(o_ref)

            o_ref[...] += _dot(a_ref[...], b_ref[...], _TN)

        out_spec = pl.BlockSpec((tm, tn), lambda i, j, kk: (i, j))
        out_shape = jax.ShapeDtypeStruct((m, n), F32)
        scratch = []
    else:
        wb = n // col_blocks
        per = tn // wb
        assert tn % wb == 0 and wb % 8 == 0

        def kern(a_ref, b_ref, o_ref, acc_ref):
            kk = pl.program_id(2)
            p = _dot(a_ref[...], b_ref[...], _TN)

            @pl.when(kk == 0)
            def _():
                acc_ref[...] = p

            @pl.when((kk > 0) & (kk < nk - 1))
            def _():
                acc_ref[...] += p

            @pl.when(kk == nk - 1)
            def _():
                r = acc_ref[...] + p if nk > 1 else p
                for c in range(per):
                    o_ref[c] = r[:, c * wb:(c + 1) * wb].astype(BF16)

        out_spec = pl.BlockSpec((per, tm, wb), lambda i, j, kk: (j, i, 0))
        out_shape = jax.ShapeDtypeStruct((col_blocks, m, wb), BF16)
        scratch = [pltpu.VMEM((tm, tn), F32)]

    return pl.pallas_call(
        kern, name=name, grid=(m // tm, n // tn, nk),
        in_specs=[pl.BlockSpec((tk, tm), lambda i, j, kk: (kk, i)), pl.BlockSpec((tk, tn), lambda i, j, kk: (kk, j))],
        out_specs=out_spec, out_shape=out_shape, scratch_shapes=scratch,
        compiler_params=_params("parallel", "parallel", "arbitrary"),
    )(a, b)


def _mm_f32(a, b, *, name, silu_a=False, bias=None):
    m, k = a.shape
    n = b.shape[1]

    def kern(*refs):
        if bias is None:
            a_ref, b_ref, o_ref = refs
        else:
            a_ref, b_ref, bias_ref, o_ref = refs
        av = a_ref[...]
        if silu_a:
            av = _silu(av)
        r = jnp.dot(av, b_ref[...], preferred_element_type=F32, precision=HI)
        if bias is not None:
            r = r + bias_ref[...]
        o_ref[...] = r

    args = [a, b] + ([] if bias is None else [bias])
    return pl.pallas_call(kern, name=name, out_shape=jax.ShapeDtypeStruct((m, n), F32),
                          compiler_params=pltpu.CompilerParams(vmem_limit_bytes=VMEM_LIMIT_BYTES))(*args)


CONV_WIN = 32


def _conv_windows(n, n_ctx):
    assert n_ctx % CONV_WIN == 0 and n_ctx >= CONV_WIN and n - n_ctx >= CONV_WIN
    return (0, n_ctx - CONV_WIN // 2, n - CONV_WIN)


def _tap_outside(r0, s, n, n_ctx):
    t = r0 + lax.broadcasted_iota(jnp.int32, (CONV_WIN, 1), 0)
    lo = jnp.where(t < n_ctx, 0, n_ctx)
    hi = jnp.where(t < n_ctx, n_ctx, n)
    return jnp.where((t + s >= lo) & (t + s < hi), 0.0, 1.0)


def _rolled(v, s):
    return v if s == 0 else pltpu.roll(v, (-s) % v.shape[0], 0)


def _conv_fwd(xp, w8, b, *, n_ctx, name, cb=256):
    n, c = xp.shape
    half = SSD_CONV // 2

    def kern(x_ref, w_ref, b_ref, cpre_ref, act_ref):
        x = x_ref[...]
        acc = jnp.zeros_like(x) + b_ref[...]
        rolled = {}
        for k in range(SSD_CONV):
            rolled[k] = _rolled(x, k - half)
            acc = acc + rolled[k] * w_ref[k:k + 1, :]
        cpre_ref[...] = acc
        act_ref[...] = _silu(acc)
        for r0 in _conv_windows(n, n_ctx):
            rows = slice(r0, r0 + CONV_WIN)
            fix = acc[rows]
            for k in range(SSD_CONV):
                if k != half:
                    fix = fix - rolled[k][rows] * w_ref[k:k + 1, :] * _tap_outside(r0, k - half, n, n_ctx)
            cpre_ref[rows, :] = fix
            act_ref[rows, :] = _silu(fix)

    spec = pl.BlockSpec((n, cb), lambda j: (0, j))
    return pl.pallas_call(
        kern, name=name, grid=(c // cb,),
        in_specs=[spec, pl.BlockSpec((8, cb), lambda j: (0, j)), pl.BlockSpec((1, cb), lambda j: (0, j))],
        out_specs=[spec, spec], out_shape=[jax.ShapeDtypeStruct((n, c), F32)] * 2,
        compiler_params=_params("parallel"),
    )(xp, w8, b)


def _conv_bwd(d1, d2, cpre, xp, w8, *, n_ctx, name, cb=128):
    n, c = xp.shape
    half = SSD_CONV // 2

    def kern(d1_ref, d2_ref, cpre_ref, x_ref, w_ref, dx_ref, dw_ref, db_ref):
        g = (d1_ref[...] + d2_ref[...]) * _dsilu(cpre_ref[...])
        x = x_ref[...]
        dx = jnp.zeros_like(g)
        dw_ref[...] = jnp.zeros_like(dw_ref)
        g_rolled = {}
        for k in range(SSD_CONV):
            s = k - half
            g_rolled[k] = _rolled(g, -s)
            dx = dx + g_rolled[k] * w_ref[k:k + 1, :]
            xr = _rolled(x, s)
            dw = _sum0(g * xr)
            if s != 0:
                for r0 in _conv_windows(n, n_ctx):
                    rows = slice(r0, r0 + CONV_WIN)
                    dw = dw - _sum0(g[rows] * xr[rows] * _tap_outside(r0, s, n, n_ctx))
            dw_ref[k:k + 1, :] = dw
        dx_ref[...] = dx.astype(BF16)
        for r0 in _conv_windows(n, n_ctx):
            rows = slice(r0, r0 + CONV_WIN)
            fix = dx[rows]
            for k in range(SSD_CONV):
                if k != half:
                    fix = fix - g_rolled[k][rows] * w_ref[k:k + 1, :] * _tap_outside(r0, half - k, n, n_ctx)
            dx_ref[rows, :] = fix.astype(BF16)
        db_ref[...] = _sum0(g)

    spec = pl.BlockSpec((n, cb), lambda j: (0, j))
    return pl.pallas_call(
        kern, name=name, grid=(c // cb,),
        in_specs=[spec, spec, spec, spec, pl.BlockSpec((8, cb), lambda j: (0, j))],
        out_specs=[spec, pl.BlockSpec((8, cb), lambda j: (0, j)), pl.BlockSpec((1, cb), lambda j: (0, j))],
        out_shape=[jax.ShapeDtypeStruct((n, c), BF16), jax.ShapeDtypeStruct((8, c), F32),
                   jax.ShapeDtypeStruct((1, c), F32)],
        compiler_params=_params("parallel"),
    )(d1, d2, cpre, xp, w8)


def _chunk_of(s, nc, n_ctx_chunks, rev):
    if not rev:
        return s
    return jnp.where(s < n_ctx_chunks, n_ctx_chunks - 1 - s, nc - 1 - (s - n_ctx_chunks))


def _scan_common(dt_raw, dtT_raw, bias_r, bias_c, alog_r, alog_c, rev):
    ii = lax.broadcasted_iota(jnp.int32, (CHUNK, CHUNK), 0)
    jj = lax.broadcasted_iota(jnp.int32, (CHUNK, CHUNK), 1)
    tri = (jj >= ii) if rev else (jj <= ii)
    tri_t = (ii >= jj) if rev else (ii <= jj)
    a_r = -jnp.exp(alog_r)
    a_c = -jnp.exp(alog_c)
    dt = _softplus(dt_raw + bias_r)
    dt_t = _softplus(dtT_raw + bias_c)
    al = dt * a_r
    acum = _dot(tri.astype(F32), al, precision=HI)
    acum_t = _dot(dt_t * a_c, tri_t.astype(F32), precision=HI)
    atot = _sum0(al)
    return tri, tri_t, a_r, dt, acum, acum_t, atot


def _head_spread():
    return jnp.repeat(jnp.eye(SSD_HEADS, dtype=BF16), SSD_HEAD_DIM, axis=1)


def _dot_sel(v, sel):
    hi = v.astype(BF16)
    lo = (v - hi.astype(F32)).astype(BF16)
    return _dot(hi, sel) + _dot(lo, sel)


def _ssd_scan_fwd(xbc, dt_raw, dtT_raw, bias_r, bias_c, alog_r, alog_c, *, rev, n_ctx_chunks, name):
    n = xbc.shape[0]
    nc = n // CHUNK
    cidx = functools.partial(_chunk_of, nc=nc, n_ctx_chunks=n_ctx_chunks, rev=rev)

    def kern(xs_ref, b_ref, c_ref, dt_ref, dtT_ref, br_ref, bc_ref, ar_ref, ac_ref, e_ref, y_ref, hs_ref, h_scr):
        @pl.when(pl.program_id(0) == 0)
        def _():
            h_scr[...] = jnp.zeros_like(h_scr)

        tri, _, _, dt, acum, acum_t, atot = _scan_common(
            dt_ref[...], dtT_ref[...], br_ref[...], bc_ref[...], ar_ref[...], ac_ref[...], rev)
        etot = jnp.exp(atot)
        spread = lambda v: _dot_sel(v, e_ref[...])
        xdt_all = xs_ref[...] * spread(dt)
        eax = spread(jnp.exp(acum))
        xdw_all = xdt_all * spread(jnp.exp(atot - acum))
        hs_ref[...] = h_scr[...]
        for g in range(SSD_GROUPS):
            gs = slice(g * 256, (g + 1) * 256)
            bg = b_ref[:, g * SSD_STATE:(g + 1) * SSD_STATE].astype(BF16)
            cg = c_ref[:, g * SSD_STATE:(g + 1) * SSD_STATE].astype(BF16)
            cb = _dot(cg, bg, _NT)
            h4 = h_scr[gs, :]
            ys = []
            for k in range(SSD_HPG):
                h = g * SSD_HPG + k
                lmat = jnp.exp(jnp.where(tri, acum[:, h:h + 1] - acum_t[h:h + 1, :], NEG_BIG))
                xdt_h = xdt_all[:, h * SSD_HEAD_DIM:(h + 1) * SSD_HEAD_DIM].astype(BF16)
                ys.append(_dot((cb * lmat).astype(BF16), xdt_h))
            y_ref[:, gs] = jnp.concatenate(ys, axis=1) + _dot(cg, h4.astype(BF16), _NT) * eax[:, gs]
            s4 = _dot(xdw_all[:, gs].astype(BF16), bg, _TN)
            for k in range(SSD_HPG):
                h = g * SSD_HPG + k
                rs = slice(h * SSD_HEAD_DIM, (h + 1) * SSD_HEAD_DIM)
                h_scr[rs, :] = h4[k * SSD_HEAD_DIM:(k + 1) * SSD_HEAD_DIM] * etot[:, h:h + 1] + \
                    s4[k * SSD_HEAD_DIM:(k + 1) * SSD_HEAD_DIM]

    nh = SSD_HEADS
    small = lambda shape: pl.BlockSpec(shape, lambda s: (0, 0))
    return pl.pallas_call(
        kern, name=name, grid=(nc,),
        in_specs=[pl.BlockSpec((CHUNK, SSD_INNER), lambda s: (cidx(s), 0)),
                  pl.BlockSpec((CHUNK, 1024), lambda s: (cidx(s), 2)),
                  pl.BlockSpec((CHUNK, 1024), lambda s: (cidx(s), 3)),
                  pl.BlockSpec((CHUNK, nh), lambda s: (cidx(s), 0)),
                  pl.BlockSpec((nh, CHUNK), lambda s: (0, cidx(s))),
                  small((1, nh)), small((nh, 1)), small((1, nh)), small((nh, 1)), small((nh, SSD_INNER))],
        out_specs=[pl.BlockSpec((CHUNK, SSD_INNER), lambda s: (cidx(s), 0)),
                   pl.BlockSpec((None, SSD_INNER, SSD_STATE), lambda s: (s, 0, 0))],
        out_shape=[jax.ShapeDtypeStruct((n, SSD_INNER), F32),
                   jax.ShapeDtypeStruct((nc, SSD_INNER, SSD_STATE), F32)],
        scratch_shapes=[pltpu.VMEM((SSD_INNER, SSD_STATE), F32)],
        compiler_params=_params("arbitrary"),
    )(xbc, xbc, xbc, dt_raw, dtT_raw, bias_r, bias_c, alog_r, alog_c, _head_spread())


def _ssd_scan_bwd(dy, xbc, hs, dt_raw, dtT_raw, bias_r, bias_c, alog_r, alog_c, dvec, *, rev, n_ctx_chunks,
                  direct, name):
    n = xbc.shape[0]
    nc = n // CHUNK
    nh = SSD_HEADS
    step_of = lambda r: nc - 1 - r
    cidx = lambda r: _chunk_of(step_of(r), nc, n_ctx_chunks, rev)

    def kern(dy_ref, xs_ref, b_ref, c_ref, hs_ref, dt_ref, dtT_ref, br_ref, bc_ref, ar_ref, ac_ref, dv_ref,
             e_ref, et_ref, dx_ref, ddt_ref, dal_ref, dbias_ref, dh_scr):
        @pl.when(pl.program_id(0) == 0)
        def _():
            dh_scr[...] = jnp.zeros_like(dh_scr)
            dal_ref[...] = jnp.zeros_like(dal_ref)
            dbias_ref[...] = jnp.zeros_like(dbias_ref)

        tri, tri_t, a_r, dt, acum, acum_t, atot = _scan_common(
            dt_ref[...], dtT_ref[...], br_ref[...], bc_ref[...], ar_ref[...], ac_ref[...], rev)
        etot = jnp.exp(atot)
        spread = lambda v: _dot_sel(v, e_ref[...])
        gather = lambda v: _dot_sel(v, et_ref[...])
        xs_all = xs_ref[...]
        dy_all = dy_ref[...]
        dtx = spread(dt)
        eax = spread(jnp.exp(acum))
        decx = spread(jnp.exp(atot - acum))
        xdt_all = xs_all * dtx
        xdw_all = xdt_all * decx
        dyo_all = dy_all * eax
        lane = lax.broadcasted_iota(jnp.int32, (CHUNK, nh), 1)
        lane1 = lax.broadcasted_iota(jnp.int32, (1, nh), 1)
        sub = lax.broadcasted_iota(jnp.int32, (nh, CHUNK), 0)
        g_rows = jnp.zeros((CHUNK, nh), F32)
        g_cols = jnp.zeros((nh, CHUNK), F32)
        dtot = jnp.zeros((1, nh), F32)
        q_col, q_e, q_dt = [], [], []
        for g in range(SSD_GROUPS):
            gs = slice(g * 256, (g + 1) * 256)
            bg = b_ref[:, g * SSD_STATE:(g + 1) * SSD_STATE].astype(BF16)
            cg = c_ref[:, g * SSD_STATE:(g + 1) * SSD_STATE].astype(BF16)
            cb = _dot(cg, bg, _NT)
            hs4 = hs_ref[gs, :]
            dh4 = dh_scr[gs, :]
            hs4_bf = hs4.astype(BF16)
            dh4_bf = dh4.astype(BF16)
            dy4 = dy_all[:, gs]
            dy4_bf = dy4.astype(BF16)
            xdt4_bf = xdt_all[:, gs].astype(BF16)
            xdw4 = xdw_all[:, gs]
            xdw4_bf = xdw4.astype(BF16)
            dyo4_bf = dyo_all[:, gs].astype(BF16)
            yoff4 = _dot(cg, hs4_bf, _NT) * eax[:, gs]
            dcg = _dot(dyo4_bf, hs4_bf)
            dh_new4 = _dot(dyo4_bf, cg, _TN)
            bdh4 = _dot(bg, dh4_bf, _NT)
            dbg = _dot(xdw4_bf, dh4_bf)
            e4 = xdw4 * bdh4
            q_col.append(dy4 * yoff4 - e4)
            q_e.append(e4)
            hsum = jnp.sum(dh4 * hs4, axis=1, keepdims=True)
            dcb = jnp.zeros((CHUNK, CHUNK), F32)
            dxdts = []
            for k in range(SSD_HPG):
                h = g * SSD_HPG + k
                ks = slice(k * SSD_HEAD_DIM, (k + 1) * SSD_HEAD_DIM)
                lmat = jnp.exp(jnp.where(tri, acum[:, h:h + 1] - acum_t[h:h + 1, :], NEG_BIG))
                mf = cb * lmat
                dm = _dot(dy4_bf[:, ks], xdt4_bf[:, ks], _NT)
                dcb = dcb + dm * lmat
                gmat = dm * mf
                g_rows = g_rows + jnp.where(lane == h, jnp.sum(gmat, axis=1, keepdims=True), 0.0)
                g_cols = g_cols + jnp.where(sub == h, _sum0(gmat), 0.0)
                dxdts.append(_dot(mf.astype(BF16), dy4_bf[:, ks], _TN))
                et = etot[:, h:h + 1]
                dtot = dtot + jnp.where(lane1 == h, _sum0(hsum[ks]) * et, 0.0)
                dh_scr[h * SSD_HEAD_DIM:(h + 1) * SSD_HEAD_DIM, :] = dh4[ks] * et + dh_new4[ks]
            dxdt4 = jnp.concatenate(dxdts, axis=1) + bdh4 * decx[:, gs]
            q_dt.append(dxdt4 * xs_all[:, gs])
            dx4 = dxdt4 * dtx[:, gs]
            if direct:
                dx4 = dx4 + dy4 * dv_ref[:, gs]
            dcb_bf = dcb.astype(BF16)
            dx_ref[:, gs] = dx4
            dx_ref[:, SSD_INNER + g * SSD_STATE:SSD_INNER + (g + 1) * SSD_STATE] = dbg + _dot(dcb_bf, cg, _TN)
            dx_ref[:, SSD_INNER + 1024 + g * SSD_STATE:SSD_INNER + 1024 + (g + 1) * SSD_STATE] = \
                dcg + _dot(dcb_bf, bg)
        e_heads = gather(jnp.concatenate(q_e, axis=1))
        dacum = gather(jnp.concatenate(q_col, axis=1)) + g_rows - g_cols.T
        dal = _dot(tri_t.astype(F32), dacum, precision=HI) + dtot + _sum0(e_heads)
        ddt = gather(jnp.concatenate(q_dt, axis=1)) + dal * a_r
        ddt_raw = ddt * _sig(dt_ref[...] + br_ref[...])
        ddt_ref[...] = ddt_raw
        dal_ref[...] += _sum0(dal * dt) * a_r
        dbias_ref[...] += _sum0(ddt_raw)

    small = lambda shape: pl.BlockSpec(shape, lambda r: (0, 0))
    return pl.pallas_call(
        kern, name=name, grid=(nc,),
        in_specs=[pl.BlockSpec((CHUNK, SSD_INNER), lambda r: (cidx(r), 0)),
                  pl.BlockSpec((CHUNK, SSD_INNER), lambda r: (cidx(r), 0)),
                  pl.BlockSpec((CHUNK, 1024), lambda r: (cidx(r), 2)),
                  pl.BlockSpec((CHUNK, 1024), lambda r: (cidx(r), 3)),
                  pl.BlockSpec((None, SSD_INNER, SSD_STATE), lambda r: (step_of(r), 0, 0)),
                  pl.BlockSpec((CHUNK, nh), lambda r: (cidx(r), 0)),
                  pl.BlockSpec((nh, CHUNK), lambda r: (0, cidx(r))),
                  small((1, nh)), small((nh, 1)), small((1, nh)), small((nh, 1)), small((1, SSD_INNER)),
                  small((nh, SSD_INNER)), small((SSD_INNER, nh))],
        out_specs=[pl.BlockSpec((CHUNK, SSD_CONV_DIM), lambda r: (cidx(r), 0)),
                   pl.BlockSpec((CHUNK, nh), lambda r: (cidx(r), 0)),
                   small((1, nh)), small((1, nh))],
        out_shape=[jax.ShapeDtypeStruct((n, SSD_CONV_DIM), F32), jax.ShapeDtypeStruct((n, nh), F32),
                   jax.ShapeDtypeStruct((1, nh), F32), jax.ShapeDtypeStruct((1, nh), F32)],
        scratch_shapes=[pltpu.VMEM((SSD_INNER, SSD_STATE), F32)],
        compiler_params=_params("arbitrary"),
    )(dy, xbc, xbc, xbc, hs, dt_raw, dtT_raw, bias_r, bias_c, alog_r, alog_c, dvec, _head_spread(),
      _head_spread().T)


def _gm_spatial_fwd(gu, gvn, ws, bst, *, name):
    n = gu.shape[0]

    def kern(gu_ref, gv_ref, ws_ref, bs_ref, o_ref):
        for g in range(GM_GROUPS):
            sl = slice(g * GM_GROUP_DIM, (g + 1) * GM_GROUP_DIM)
            s = _dot(ws_ref[g], gv_ref[:, sl]) + bs_ref[:, g:g + 1]
            o_ref[:, sl] = (gu_ref[:, sl] * s).astype(BF16)

    spec = pl.BlockSpec((CHUNK, GM_INNER), lambda i: (i, 0))
    return pl.pallas_call(
        kern, name=name, grid=(n // CHUNK,),
        in_specs=[spec, spec, pl.BlockSpec(ws.shape, lambda i: (0, 0, 0)), pl.BlockSpec(bst.shape, lambda i: (0, 0))],
        out_specs=spec, out_shape=jax.ShapeDtypeStruct((n, GM_INNER), BF16),
        compiler_params=_params("parallel"),
    )(gu, gvn, ws, bst)


def _gm_spatial_bwd(dt, gu, gvn, ws, wst, bst, *, name):
    n = gu.shape[0]

    def kern(dt_ref, gu_ref, gv_ref, ws_ref, wst_ref, bs_ref, dgu_ref, dgv_ref, dws_ref, dbs_ref):
        @pl.when(pl.program_id(0) == 0)
        def _():
            dws_ref[...] = jnp.zeros_like(dws_ref)
            dbs_ref[...] = jnp.zeros_like(dbs_ref)

        lane = lax.broadcasted_iota(jnp.int32, (CHUNK, GM_GROUPS), 1)
        dbs = jnp.zeros((CHUNK, GM_GROUPS), F32)
        for g in range(GM_GROUPS):
            sl = slice(g * GM_GROUP_DIM, (g + 1) * GM_GROUP_DIM)
            gv = gv_ref[:, sl]
            s = _dot(ws_ref[g], gv) + bs_ref[:, g:g + 1]
            d = dt_ref[:, sl]
            dgu_ref[:, sl] = d * s
            ds = d * gu_ref[:, sl]
            ds_bf = ds.astype(BF16)
            dws_ref[g] += _dot(ds_bf, gv, _NT)
            dgv_ref[:, sl] = _dot(wst_ref[g], ds_bf)
            dbs = dbs + jnp.where(lane == g, jnp.sum(ds, axis=1, keepdims=True), 0.0)
        dbs_ref[...] += dbs

    spec = pl.BlockSpec((CHUNK, GM_INNER), lambda i: (i, 0))
    wspec = pl.BlockSpec(ws.shape, lambda i: (0, 0, 0))
    bspec = pl.BlockSpec(bst.shape, lambda i: (0, 0))
    return pl.pallas_call(
        kern, name=name, grid=(n // CHUNK,),
        in_specs=[spec, spec, spec, wspec, wspec, bspec],
        out_specs=[spec, spec, wspec, bspec],
        out_shape=[jax.ShapeDtypeStruct((n, GM_INNER), F32), jax.ShapeDtypeStruct((n, GM_INNER), F32),
                   jax.ShapeDtypeStruct(ws.shape, F32), jax.ShapeDtypeStruct(bst.shape, F32)],
        compiler_params=_params("arbitrary"),
    )(dt, gu, gvn, ws, wst, bst)


def _adamw(parts, w, m, v, *, name, tm=256, sel=(), into=None):
    ns, r, wd = parts.shape
    tm = _pick(r, tm, 8)
    lead = len(sel)
    assert w.shape[lead:] == (r, wd) and lead == w.ndim - 2

    def kern(*refs):
        p_ref, w_ref, m_ref, v_ref = refs[:4]
        g_ref, d_ref, nm_ref, nv_ref = refs[-4:]
        g = p_ref[0].astype(F32)
        for s in range(1, ns):
            g = g + p_ref[s].astype(F32)
        m2 = ADAM_B1 * m_ref[...] + (1.0 - ADAM_B1) * g
        v2 = ADAM_B2 * v_ref[...] + (1.0 - ADAM_B2) * (g * g)
        m_hat = m2 / (1.0 - ADAM_B1 ** ADAM_STEP)
        v_hat = v2 / (1.0 - ADAM_B2 ** ADAM_STEP)
        g_ref[...] = g
        d_ref[...] = -ADAM_LR * (m_hat / (jnp.sqrt(v_hat) + ADAM_EPS) + ADAM_WD * w_ref[...])
        nm_ref[...] = m2
        nv_ref[...] = v2

    spec = pl.BlockSpec((None,) * lead + (tm, wd), lambda i: tuple(sel) + (i, 0))
    chained = any(s > 1 for s in w.shape[:lead])
    extra, aliases = [], {}
    if chained:
        extra = list(into) if into is not None else [lax.empty(w.shape, F32) for _ in range(4)]
        aliases = {4 + k: k for k in range(4)}
    return pl.pallas_call(
        kern, name=name, grid=(r // tm,),
        in_specs=[pl.BlockSpec((ns, tm, wd), lambda i: (0, i, 0)), spec, spec, spec] +
                 [pl.BlockSpec(memory_space=pl.ANY)] * len(extra),
        out_specs=[spec] * 4, out_shape=[jax.ShapeDtypeStruct(w.shape, F32)] * 4,
        input_output_aliases=aliases,
        compiler_params=_params("parallel"),
    )(parts, w, m, v, *extra)


def _zero_after(x, *, name):
    def kern(x_ref, o_ref):
        o_ref[...] = jnp.zeros_like(o_ref)

    return pl.pallas_call(kern, name=name, out_shape=jax.ShapeDtypeStruct((8, 128), F32),
                          in_specs=[pl.BlockSpec(memory_space=pl.ANY)])(x)[0, 0]


def _sum_slots(parts, *, name, scale_by=None):
    ns, r, wd = parts.shape

    def kern(*refs):
        p_ref, o_ref = refs[0], refs[-1]
        g = p_ref[0]
        for s in range(1, ns):
            g = g + p_ref[s]
        if scale_by is not None:
            g = g * _dsilu(refs[1][...])
        o_ref[...] = g

    args = [parts] + ([] if scale_by is None else [scale_by])
    return pl.pallas_call(kern, name=name, out_shape=jax.ShapeDtypeStruct((r, wd), F32),
                          compiler_params=pltpu.CompilerParams(vmem_limit_bytes=VMEM_LIMIT_BYTES))(*args)


def _mesh_pos():
    x, y, c = lax.axis_index("x"), lax.axis_index("y"), lax.axis_index("c")
    return x, y, c, 4 * x + 2 * y + c


def _flip(x, y, c, f):
    fx, fy, fc = (f >> 2) & 1, (f >> 1) & 1, f & 1
    px = 1 - x if fx else x
    py = 1 - y if fy else y
    pc = 1 - c if fc else c
    return (px, py, pc), 4 * px + 2 * py + pc


_HBM_SPEC = pl.BlockSpec(memory_space=pltpu.HBM)


def _exchange(arrays, *, scatter, name):
    na = len(arrays)
    if scatter:
        out_shape = [jax.ShapeDtypeStruct(a.shape, a.dtype) for a in arrays]
    else:
        out_shape = [jax.ShapeDtypeStruct((NDEV,) + a.shape, a.dtype) for a in arrays]

    out_shape.append(jax.ShapeDtypeStruct((8, 128), F32))

    def body(*refs):
        ins, outs = refs[:na], refs[na:2 * na]
        send_sems, recv_sems, local_sems = refs[2 * na + 1:]
        refs[2 * na][...] = jnp.zeros((8, 128), F32)
        x, y, c, me = _mesh_pos()
        copies = []
        for i in range(na):
            src_own = ins[i].at[me] if scatter else ins[i]
            lc = pltpu.make_async_copy(src_own, outs[i].at[me], local_sems.at[i])
            lc.start()
            copies.append(lc)
        sends = []
        for f in range(1, NDEV):
            peer, pidx = _flip(x, y, c, f)
            for i in range(na):
                k = i * (NDEV - 1) + f - 1
                src = ins[i].at[pidx] if scatter else ins[i]
                cp = pltpu.make_async_remote_copy(
                    src_ref=src, dst_ref=outs[i].at[me], send_sem=send_sems.at[k], recv_sem=recv_sems.at[k],
                    device_id=peer, device_id_type=pl.DeviceIdType.MESH)
                cp.start()
                sends.append(cp)
        for f in range(1, NDEV):
            peer, pidx = _flip(x, y, c, f)
            for i in range(na):
                k = i * (NDEV - 1) + f - 1
                src = ins[i].at[pidx] if scatter else ins[i]
                pltpu.make_async_remote_copy(
                    src_ref=src, dst_ref=outs[i].at[pidx], send_sem=send_sems.at[k], recv_sem=recv_sems.at[k],
                    device_id=peer, device_id_type=pl.DeviceIdType.MESH).wait_recv()
        for cp in sends:
            cp.wait_send()
        for lc in copies:
            lc.wait()

    res = pl.pallas_call(
        body, name=name, out_shape=out_shape, in_specs=[_HBM_SPEC] * na,
        out_specs=[_HBM_SPEC] * na + [pl.BlockSpec(memory_space=pltpu.VMEM)],
        scratch_shapes=[pltpu.SemaphoreType.DMA((na * (NDEV - 1),)), pltpu.SemaphoreType.DMA((na * (NDEV - 1),)),
                        pltpu.SemaphoreType.DMA((na,))],
        compiler_params=pltpu.CompilerParams(has_side_effects=True),
    )(*arrays)
    return res[:na], res[na][0, 0]


_SEM_SPEC = pl.BlockSpec(memory_space=pltpu.SEMAPHORE)
_DATAFLOW = pltpu.SideEffectType.DATAFLOW_SIDE_EFFECTING


def _split_copies(srcs, lands, send_sems, recv_sems, scatter, arriving):
    x, y, c, me = _mesh_pos()
    copies = []
    for i in range(len(srcs)):
        for f in range(1, NDEV):
            peer, pidx = _flip(x, y, c, f)
            k = i * (NDEV - 1) + f - 1
            copies.append(pltpu.make_async_remote_copy(
                src_ref=srcs[i].at[pidx] if scatter else srcs[i], dst_ref=lands[i].at[pidx if arriving else me],
                send_sem=send_sems.at[k], recv_sem=recv_sems.at[k], device_id=peer,
                device_id_type=pl.DeviceIdType.MESH))
    return copies


def _exchange_start(srcs, lands, *, scatter, name):
    na = len(srcs)
    nsem = na * (NDEV - 1)

    def body(*refs):
        ins_src, ins_land = refs[:na], refs[na:2 * na]
        send_sems, recv_sems = refs[2 * na], refs[2 * na + 1]
        token = refs[-1]
        for cp in _split_copies(ins_src, ins_land, send_sems, recv_sems, scatter, False):
            cp.start()
        token[...] = jnp.zeros_like(token)

    thru = [pltpu.HBM(a.shape, a.dtype) for a in list(srcs) + list(lands)]
    res = pl.pallas_call(
        body, name=name,
        out_shape=(pltpu.SemaphoreType.DMA((nsem,)), pltpu.SemaphoreType.DMA((nsem,)), *thru,
                   jax.ShapeDtypeStruct((8, 128), F32)),
        in_specs=[_HBM_SPEC] * (2 * na),
        out_specs=(_SEM_SPEC, _SEM_SPEC, *([_HBM_SPEC] * (2 * na)), pl.BlockSpec(memory_space=pltpu.VMEM)),
        input_output_aliases={i: 2 + i for i in range(2 * na)},
        compiler_params=pltpu.CompilerParams(has_side_effects=_DATAFLOW),
    )(*[pltpu.with_memory_space_constraint(a, pltpu.HBM) for a in list(srcs) + list(lands)])
    send_sems, recv_sems = res[0], res[1]
    return send_sems, recv_sems, res[2:2 + na], res[2 + na:2 + 2 * na], res[-1][0, 0]


def _exchange_wait(send_sems, recv_sems, srcs, lands, after, *, scatter, name):
    na = len(srcs)

    def body(*refs):
        ins_src, ins_land = refs[:na], refs[na:2 * na]
        s_sems, r_sems = refs[2 * na], refs[2 * na + 1]
        for cp in _split_copies(ins_src, ins_land, s_sems, r_sems, scatter, False):
            cp.wait_send()
        for cp in _split_copies(ins_src, ins_land, s_sems, r_sems, scatter, True):
            cp.wait_recv()

    thru = [pltpu.HBM(a.shape, a.dtype) for a in list(srcs) + list(lands)]
    res = pl.pallas_call(
        body, name=name, out_shape=tuple(thru),
        in_specs=[_HBM_SPEC] * (2 * na) + [_SEM_SPEC, _SEM_SPEC, pl.BlockSpec(memory_space=pl.ANY)],
        out_specs=tuple([_HBM_SPEC] * (2 * na)),
        input_output_aliases={i: i for i in range(2 * na)},
        compiler_params=pltpu.CompilerParams(has_side_effects=_DATAFLOW),
    )(*srcs, *lands, send_sems, recv_sems, after)
    return res[na:]


def _landing(block, me):
    buf = lax.empty((NDEV,) + block.shape, block.dtype)
    return lax.dynamic_update_slice_in_dim(buf, block[None], me, axis=0)


def _seg_kw(nseg, n_ctx, tm):
    return dict(nseg=nseg, seg_blocks=(n_ctx // tm if nseg == 2 else 0))


def _ffn_fwd(tag, h, gpre, gpost, shift, scale, gate, w, *, nseg, n_ctx, tm):
    n = h.shape[0]
    kw = _seg_kw(nseg, n_ctx, tm)
    (u,) = _rowwise(tag + "_pre", _pre_fwd_fn, n, [h], [("full", gpre), ("seg", shift), ("seg", scale)],
                    [(D_MODEL, BF16)], tm=tm, **kw)
    s, a, b = _mm_glu(u, w["win"], name=tag + "_glu")
    if "late" in w:
        w.update(w.pop("late")(s))
    y, ho = _mm_rows(s, w["wout"], functools.partial(_out_post_fn, 0.5), [h], [("full", gpost), ("seg", gate)],
                     [(D_MODEL, F32), (D_MODEL, F32)], name=tag + "_out", tk=FFN_DIM, n_ctx=n_ctx)
    return ho, dict(h=h, u=u, s=s, a=a, b=b, y=y)


def _ffn_bwd(tag, dho, sv, gpre, gpost, scale, gate, w, put, *, nseg, n_ctx, tm):
    n = dho.shape[0]
    kw = _seg_kw(nseg, n_ctx, tm)
    dy, dgate, dgpost = _rowwise(tag + "_postb", functools.partial(_post_bwd_fn, 0.5), n, [dho, sv["y"]],
                                 [("full", gpost), ("seg", gate)], [(D_MODEL, BF16)], [D_MODEL, D_MODEL], tm=tm, **kw)
    tok = put("w_out", _mm_tn(sv["s"], dy, name=tag + "_dwout", tm=1408, tn=1024, col_blocks=1))
    ds = _mm(dy, w["wout"], out_dtype=F32, name=tag + "_ds", tn=1408, rhs_t=True)
    (dp,) = _rowwise(tag + "_glub", _glu_bwd_fn, n, [ds, sv["a"], sv["b"]], [], [(2 * FFN_DIM, BF16)], tm=min(tm, 128))
    tok2 = put("w_in", _mm_tn(sv["u"], dp, name=tag + "_dwin", tn=1408, col_blocks=NDEV))
    for t in (tok, tok2):
        if t is not None:
            gpre = gpre + t
    dh, dshift, dscale, dgpre = _mm_rows(dp, w["win"], _pre_bwd_fn, [sv["h"], dho], [("full", gpre), ("seg", scale)],
                                         [(D_MODEL, F32)], [D_MODEL, D_MODEL, D_MODEL], name=tag + "_du",
                                         rhs_t=True, n_ctx=n_ctx)
    return dh, None, dict(shift=dshift, scale=dscale, gate=dgate, gpre=dgpre, gpost=dgpost)


def _local_step(x, ctx, target, mods, norm_g, get_w, small, put_grad):
    t_len, n_ctx = x.shape[0], ctx.shape[0]
    n0 = t_len + n_ctx
    tm0 = _pick(n_ctx, 256, 8)
    tm1 = _pick(t_len, 256, 8)
    ncc = n_ctx // CHUNK
    g = {}

    def modrow(i, k, nseg):
        mc, mx = mods[i]
        if nseg == 2:
            return jnp.stack([mc[k], mx[k]])[:, None, :]
        return mx[k][None, None, :]

    pending = [None]

    def gvec(i, k):
        v = norm_g[i, k][None, :]
        if pending[0] is not None:
            v = v + pending[0]
            pending[0] = None
        return v

    xc = jnp.concatenate([ctx, x], axis=0)
    L0 = dict(nseg=2, n_ctx=n_ctx, tm=tm0)
    wts = dict(get_w("ffn00", xc))
    h1, sv_f01 = _ffn_fwd("l0f1", xc, gvec(0, 0), gvec(0, 1), modrow(0, 0, 2), modrow(0, 1, 2), modrow(0, 2, 2),
                          wts["ffn00"], **L0)
    kw0 = _seg_kw(2, n_ctx, tm0)
    (um0,) = _rowwise("l0m_pre", _pre_fwd_fn, n0, [h1], [("full", gvec(0, 2)), ("seg", modrow(0, 3, 2)),
                                                         ("seg", modrow(0, 4, 2))], [(D_MODEL, BF16)], tm=tm0, **kw0)
    wts.update(get_w("ssd", um0))
    z = _mm(um0, wts["ssd_win"], out_dtype=F32, name="ssd_z", n=SSD_INNER)
    xbc_pre = _mm(um0, wts["ssd_win"], out_dtype=F32, name="ssd_xbc", n=SSD_CONV_DIM, b_off=(0, SSD_INNER // 1024))
    dtr = _mm(um0, wts["ssd_wdt"], out_dtype=F32, name="ssd_dt")
    cpre, xbc = _conv_fwd(xbc_pre, small["conv_w8"], small["conv_b"], n_ctx=n_ctx, name="ssd_conv")
    nh = SSD_HEADS
    dt_dir = [dtr[:, :nh], dtr[:, nh:2 * nh]]
    dtT_dir = [d.T for d in dt_dir]
    bias_r = [small["dt_bias"][d][None, :] for d in range(2)]
    bias_c = [small["dt_bias"][d][:, None] for d in range(2)]
    alog_r = [small["a_log"][d][None, :] for d in range(2)]
    alog_c = [small["a_log"][d][:, None] for d in range(2)]
    ys, hss = [], []
    for d in range(2):
        yd, hsd = _ssd_scan_fwd(xbc, dt_dir[d], dtT_dir[d], bias_r[d], bias_c[d], alog_r[d], alog_c[d],
                                rev=(d == 1), n_ctx_chunks=ncc, name=f"ssd_scan{d}")
        ys.append(yd)
        hss.append(hsd)
    dvec = jnp.repeat(small["ssd_d"], SSD_HEAD_DIM)[None, :]
    ngv = small["ssd_norm_g"][None, :]
    gate_rows = [ys[0], ys[1], (xbc, SSD_INNER, 0, 0), z]
    lat = lambda r: (r[0], r[1], r[2], ncc) if isinstance(r, tuple) else (r, r.shape[1], 0, ncc)
    (yn,) = _rowwise("ssd_gate", _ssdgate_fwd_fn, t_len, [lat(r) for r in gate_rows],
                     [("full", dvec), ("full", ngv)], [(SSD_INNER, BF16)], tm=CHUNK)
    h1x = h1[n_ctx:]
    L1 = dict(nseg=1, n_ctx=0, tm=tm1)
    yo0, h2 = _mm_rows(yn, wts["ssd_wout"], functools.partial(_out_post_fn, 1.0), [h1x],
                       [("full", gvec(0, 3)), ("seg", modrow(0, 5, 1))], [(D_MODEL, F32), (D_MODEL, F32)],
                       name="ssd_out", tk=SSD_INNER)
    wts.update(get_w("ffn01", h2))
    h3, sv_f02 = _ffn_fwd("l0f2", h2, gvec(0, 4), gvec(0, 5), modrow(0, 6, 1), modrow(0, 7, 1), modrow(0, 8, 1),
                          wts["ffn01"], **L1)

    wts.update(get_w("ffn10", h3))
    h4, sv_f11 = _ffn_fwd("l1f1", h3, gvec(1, 0), gvec(1, 1), modrow(1, 0, 1), modrow(1, 1, 1), modrow(1, 2, 1),
                          wts["ffn10"], **L1)
    (um1,) = _rowwise("l1m_pre", _pre_fwd_fn, t_len, [h4], [("full", gvec(1, 2)), ("seg", modrow(1, 3, 1)),
                                                            ("seg", modrow(1, 4, 1))], [(D_MODEL, BF16)], tm=tm1)
    wts.update(get_w("gm", um1))
    p1 = _mm(um1, wts["gm_win"], out_dtype=F32, name="gm_in")
    vg = small["gm_v_g"][None, :]
    vb = small["gm_v_b"][None, :]
    gu, gvn = _rowwise("gm_act", _gm_act_fwd_fn, t_len, [p1], [("full", vg), ("full", vb)],
                       [(GM_INNER, F32), (GM_INNER, BF16)], tm=128)
    ws_bf = small["gm_w_s"].astype(BF16)
    wst_bf = jnp.swapaxes(small["gm_w_s"], 1, 2).astype(BF16)
    bst = small["gm_b_s"].T
    tgm = _gm_spatial_fwd(gu, gvn, ws_bf, bst, name="gm_spatial")
    yo1, h5 = _mm_rows(tgm, wts["gm_wout"], functools.partial(_out_post_fn, 1.0), [h4],
                       [("full", gvec(1, 3)), ("seg", modrow(1, 5, 1))], [(D_MODEL, F32), (D_MODEL, F32)],
                       name="gm_out", tk=GM_INNER)
    wts.update(get_w("ffn11", h5))
    h6, sv_f12 = _ffn_fwd("l1f2", h5, gvec(1, 4), gvec(1, 5), modrow(1, 6, 1), modrow(1, 7, 1), modrow(1, 8, 1),
                          wts["ffn11"], **L1)

    dh, loss_parts = _rowwise("loss", _loss_fn, t_len, [h6, target], [], [(D_MODEL, F32)], [D_MODEL], tm=tm1)

    zero = jnp.zeros((D_MODEL,), F32)
    dmx = [[zero] * N_MOD for _ in range(2)]
    dmc = [[zero] * N_MOD for _ in range(2)]
    dng = [[zero] * 6 for _ in range(2)]

    def put_mod(i, k, acc):
        if acc.shape[0] == 2:
            dmc[i][k] = dmc[i][k] + acc[0, 0]
            dmx[i][k] = dmx[i][k] + acc[1, 0]
        else:
            dmx[i][k] = dmx[i][k] + acc[0, 0]

    def put_g(i, k, acc):
        dng[i][k] = dng[i][k] + jnp.sum(acc[:, 0], axis=0)

    def ffn_back(tag, i, j, dho, sv, w, lay):
        nseg = lay["nseg"]
        base = 0 if j == 0 else 6
        gi = 0 if j == 0 else 4
        dh_in, pending[0], s = _ffn_bwd(tag, dho, sv, gvec(i, gi), gvec(i, gi + 1), modrow(i, base + 1, nseg),
                                        modrow(i, base + 2, nseg), w, functools.partial(put_grad, f"ffn{i}{j}"), **lay)
        put_mod(i, base, s["shift"])
        put_mod(i, base + 1, s["scale"])
        put_mod(i, base + 2, s["gate"])
        put_g(i, gi, s["gpre"])
        put_g(i, gi + 1, s["gpost"])
        return dh_in

    dh = ffn_back("l1f2", 1, 1, dh, sv_f12, wts["ffn11"], L1)
    dyo, dgate, dgp = _rowwise("l1m_postb", functools.partial(_post_bwd_fn, 1.0), t_len, [dh, yo1],
                               [("full", gvec(1, 3)), ("seg", modrow(1, 5, 1))], [(D_MODEL, BF16)],
                               [D_MODEL, D_MODEL], tm=tm1)
    put_mod(1, 5, dgate)
    put_g(1, 3, dgp)
    put_grad("gm", "w_out", _mm_tn(tgm, dyo, name="gm_dwout", tn=1024, col_blocks=1))
    dtg = _mm(dyo, wts["gm_wout"], out_dtype=F32, name="gm_dt", rhs_t=True)
    dgu, dgvn, dws, dbst = _gm_spatial_bwd(dtg, gu, gvn, ws_bf, wst_bf, bst, name="gm_spatialb")
    g["gm_w_s"] = dws
    g["gm_b_s"] = dbst.T
    dp1, dvg, dvb = _rowwise("gm_actb", _gm_act_bwd_fn, t_len, [p1, dgu, dgvn], [("full", vg)],
                             [(2 * GM_INNER, BF16)], [GM_INNER, GM_INNER], tm=128)
    g["gm_v_g"] = dvg[0, 0]
    g["gm_v_b"] = dvb[0, 0]
    pending[0] = put_grad("gm", "w_in", _mm_tn(um1, dp1, name="gm_dwin", tm=1024, col_blocks=NDEV))
    dh, dsh, dsc, dgp = _mm_rows(dp1, wts["gm_win"], _pre_bwd_fn, [h4, dh],
                                 [("full", gvec(1, 2)), ("seg", modrow(1, 4, 1))], [(D_MODEL, F32)],
                                 [D_MODEL, D_MODEL, D_MODEL], name="gm_dum", tk=1024, rhs_t=True)
    put_mod(1, 3, dsh)
    put_mod(1, 4, dsc)
    put_g(1, 2, dgp)
    dh = ffn_back("l1f1", 1, 0, dh, sv_f11, wts["ffn10"], L1)

    dh = ffn_back("l0f2", 0, 1, dh, sv_f02, wts["ffn01"], L1)
    dyo, dgate, dgp = _rowwise("l0m_postb", functools.partial(_post_bwd_fn, 1.0), t_len, [dh, yo0],
                               [("full", gvec(0, 3)), ("seg", modrow(0, 5, 1))], [(D_MODEL, BF16)],
                               [D_MODEL, D_MODEL], tm=tm1)
    put_mod(0, 5, dgate)
    put_g(0, 3, dgp)
    put_grad("ssd", "w_out", _mm_tn(yn, dyo, name="ssd_dwout", tn=1024, col_blocks=1))
    dyn = _mm(dyo, wts["ssd_wout"], out_dtype=F32, name="ssd_dyn", rhs_t=True)
    dy_ssd, dz, dngv, ddv = _rowwise("ssd_gateb", _ssdgate_bwd_fn, n0, [(dyn, SSD_INNER, 0, -ncc)] + gate_rows,
                                     [("full", dvec), ("full", ngv)], [(SSD_INNER, F32), (SSD_INNER, BF16)],
                                     [SSD_INNER, SSD_INNER], tm=128)
    g["ssd_norm_g"] = dngv[0, 0]
    g["ssd_D"] = jnp.sum(ddv[0, 0].reshape(SSD_HEADS, SSD_HEAD_DIM), axis=1)
    dxbcs, ddts, dalogs, dbiases = [], [], [], []
    for d in range(2):
        dxd, ddtd, dal, dbi = _ssd_scan_bwd(dy_ssd, xbc, hss[d], dt_dir[d], dtT_dir[d], bias_r[d], bias_c[d],
                                            alog_r[d], alog_c[d], dvec, rev=(d == 1), n_ctx_chunks=ncc,
                                            direct=(d == 0), name=f"ssd_scanb{d}")
        dxbcs.append(dxd)
        ddts.append(ddtd)
        dalogs.append(dal[0])
        dbiases.append(dbi[0])
    g["ssd_A_log"] = jnp.stack(dalogs)
    g["ssd_dt_bias"] = jnp.stack(dbiases)
    dxbc_pre, dcw8, dcb = _conv_bwd(dxbcs[0], dxbcs[1], cpre, xbc_pre, small["conv_w8"], n_ctx=n_ctx, name="ssd_convb")
    g["ssd_conv_w"] = dcw8[:SSD_CONV]
    g["ssd_conv_b"] = dcb[0]
    ddt_bf = jnp.concatenate([ddts[0], ddts[1], jnp.zeros((n0, 128 - 2 * nh), F32)], axis=1).astype(BF16)
    dw_ssd_in = jnp.concatenate([
        _mm_tn(um0, dz, name="ssd_dwz", tm=1024),
        _mm_tn(um0, dxbc_pre, name="ssd_dwxbc", tm=1024),
        _mm_tn(um0, ddt_bf, name="ssd_dwdt", tm=1024)[:, :2 * nh]], axis=1)
    pending[0] = put_grad("ssd", "w_in", dw_ssd_in)
    win_ssd = wts["ssd_win"]
    dum0 = _mm(dz, win_ssd, out_dtype=F32, name="ssd_dum_z", tk=1024, rhs_t=True, n=D_MODEL)
    dum0 = _mm(dxbc_pre, win_ssd, out_dtype=F32, name="ssd_dum_x", tk=1024, rhs_t=True, n=D_MODEL,
               b_off=(0, SSD_INNER // 1024), add=dum0)
    dum0 = _mm(ddt_bf, wts["ssd_wdt"], out_dtype=F32, name="ssd_dum_dt", rhs_t=True, add=dum0)
    dh0, dsh, dsc, dgp = _rowwise("l0m_preb", _pre_bwd_fn, n0, [dum0, h1, (dh, D_MODEL, 0, -(n_ctx // tm0))],
                                  [("full", gvec(0, 2)), ("seg", modrow(0, 4, 2))], [(D_MODEL, F32)],
                                  [D_MODEL, D_MODEL, D_MODEL], tm=tm0, **kw0)
    put_mod(0, 3, dsh)
    put_mod(0, 4, dsc)
    put_g(0, 2, dgp)
    dh0 = ffn_back("l0f1", 0, 0, dh0, sv_f01, wts["ffn00"], L0)
    grad_x = dh0[n_ctx:]
    g["norm_g"] = jnp.stack([jnp.stack(r) for r in dng])
    g["dmx"] = jnp.stack([jnp.concatenate(r) for r in dmx])
    g["dmc"] = jnp.stack([jnp.concatenate(r) for r in dmc])
    return loss_parts[0], grad_x, g


GROUPS = ("ffn00", "ssd", "ffn01", "ffn10", "gm", "ffn11")


def _mats_in(group, win_l):
    k, nloc = win_l.shape[1], win_l.shape[2]
    win = jnp.transpose(win_l, (1, 0, 2)).reshape(k, NDEV * nloc)
    if group.startswith("ffn"):
        return dict(win=win)
    if group == "gm":
        return dict(gm_win=win)
    assert group == "ssd"
    c1 = SSD_INNER + SSD_CONV_DIM
    return dict(ssd_win=win, ssd_wdt=jnp.pad(win[:, c1:], ((0, 0), (0, 128 - 2 * SSD_HEADS))))


def _mats_out(group, wout_l):
    pre = "" if group.startswith("ffn") else group + "_"
    return {pre + "wout": wout_l.reshape(-1, wout_l.shape[2])}


def _group_mats(group, lands):
    m = {**_mats_in(group, lands[0]), **_mats_out(group, lands[1])}
    return {group: m} if group.startswith("ffn") else m


def _grad_blocks(which, grad):
    if grad.ndim == 3:
        return grad if which == "w_in" else grad.reshape(NDEV, grad.shape[1] // NDEV, grad.shape[2])
    if which == "w_in":
        k, n = grad.shape
        return jnp.transpose(grad.reshape(k, NDEV, n // NDEV), (1, 0, 2)).astype(BF16)
    return grad.reshape(NDEV, grad.shape[0] // NDEV, grad.shape[1]).astype(BF16)


def kernel(x, c, ctx, c_ctx, ada_w, ada_b, norm_g, ffn_w_in, ffn_w_out, ssd_w_in, ssd_conv_w, ssd_conv_b, ssd_dt_bias, ssd_A_log, ssd_D, ssd_norm_g, ssd_w_out, gm_w_in, gm_v_g, gm_v_b, gm_w_s, gm_b_s, gm_w_out, loss_target, m_c_ctx, m_ada_w, m_ada_b, m_norm_g, m_ffn_w_in, m_ffn_w_out, m_ssd_w_in, m_ssd_conv_w, m_ssd_conv_b, m_ssd_dt_bias, m_ssd_A_log, m_ssd_D, m_ssd_norm_g, m_ssd_w_out, m_gm_w_in, m_gm_v_g, m_gm_v_b, m_gm_w_s, m_gm_b_s, m_gm_w_out, v_c_ctx, v_ada_w, v_ada_b, v_norm_g, v_ffn_w_in, v_ffn_w_out, v_ssd_w_in, v_ssd_conv_w, v_ssd_conv_b, v_ssd_dt_bias, v_ssd_A_log, v_ssd_D, v_ssd_norm_g, v_ssd_w_out, v_gm_w_in, v_gm_v_g, v_gm_v_b, v_gm_w_s, v_gm_b_s, v_gm_w_out):
    me = 4 * lax.axis_index("x") + 2 * lax.axis_index("y") + lax.axis_index("c")
    d = D_MODEL
    ncol = N_MOD * d // NDEV

    small_pack = jnp.concatenate([c.reshape(-1), norm_g.reshape(-1), ssd_conv_w.reshape(-1),
                                  gm_v_g.reshape(-1), gm_v_b.reshape(-1)])[None, :]
    (sp,), _ = _exchange([small_pack], scatter=False, name="gather_small")
    sp = sp[:, 0]
    o = 0
    c_all = sp[:, o:o + d]; o += d
    ng_all = sp[:, o:o + 2 * 6 * 128].reshape(NDEV, 2, 6, 128); o += 2 * 6 * 128
    cw_all = sp[:, o:o + SSD_CONV * 512].reshape(NDEV, SSD_CONV, 512); o += SSD_CONV * 512
    vg_all = sp[:, o:o + 256]; o += 256
    vb_all = sp[:, o:o + 256]; o += 256
    norm_g_full = jnp.transpose(ng_all, (1, 2, 0, 3)).reshape(2, 6, d)
    conv_w_full = jnp.transpose(cw_all, (1, 0, 2)).reshape(SSD_CONV, SSD_CONV_DIM)
    gm_v_g_full = vg_all.reshape(-1)
    gm_v_b_full = vb_all.reshape(-1)

    c16 = jnp.concatenate([c_all, jnp.broadcast_to(c_ctx[None, :], (NDEV, d))], axis=0)
    ada_b_loc = lax.dynamic_slice_in_dim(ada_b, me * ncol, ncol, axis=1)
    mods_loc = jnp.stack([_mm_f32(c16, ada_w[i], name=f"ada_mod{i}", silu_a=True, bias=ada_b_loc[i][None, :])
                          for i in range(2)])
    (mods_all,), mods_done = _exchange([mods_loc], scatter=False, name="gather_mods")

    shard = {"ssd": (ssd_w_in[0], ssd_w_out[0]), "gm": (gm_w_in[0], gm_w_out[0])}
    for i in range(2):
        for j in range(2):
            shard[f"ffn{i}{j}"] = (ffn_w_in[i, j], ffn_w_out[i, j])
    first = GROUPS[0]
    units = [(first + "_in", first, (0,)), (first + "_out", first, (1,))] + [(grp, grp, (0, 1)) for grp in GROUPS[1:]]
    gathers = {}
    started = mods_done
    for unit, grp, idx in units:
        srcs = [(shard[grp][k] + started).astype(BF16) for k in idx]
        st = _exchange_start(srcs, [_landing(s, me) for s in srcs], scatter=False, name="gather_start_" + unit)
        gathers[unit] = st[:4]
        started = st[4]

    def fetch(unit, after):
        return _exchange_wait(*gathers[unit], after, scatter=False, name="gather_wait_" + unit)

    def get_w(grp, after):
        if grp != first:
            return _group_mats(grp, fetch(grp, after))
        late = lambda later: _mats_out(grp, fetch(grp + "_out", later)[0])
        return {grp: dict(_mats_in(grp, fetch(grp + "_in", mods_rows)[0]), late=late)}

    scatters = {}
    held = {}

    def put_grad(grp, which, grad):
        if grp == first:
            unit, blocks = grp + "_" + which[2:], [_grad_blocks(which, grad)]
        else:
            held[grp, which] = _grad_blocks(which, grad)
            if (grp, "w_in") not in held or (grp, "w_out") not in held:
                return None
            unit, blocks = grp, [held[grp, "w_in"], held[grp, "w_out"]]
        lands = [_landing(lax.dynamic_index_in_dim(b, me, axis=0, keepdims=False), me) for b in blocks]
        st = _exchange_start(blocks, lands, scatter=True, name="scatter_start_" + unit)
        scatters[unit] = st[:4]
        return st[4]

    mods_rows = jnp.transpose(mods_all, (1, 2, 0, 3)).reshape(2, 2 * NDEV, N_MOD * d) + started
    mx = lax.dynamic_index_in_dim(mods_rows, me, axis=1, keepdims=False).reshape(2, N_MOD, d)
    mc = mods_rows[:, NDEV].reshape(2, N_MOD, d)
    mods = [(mc[i], mx[i]) for i in range(2)]

    small = dict(conv_w8=jnp.pad(conv_w_full, ((0, 8 - SSD_CONV), (0, 0))), conv_b=ssd_conv_b, dt_bias=ssd_dt_bias[0],
                 a_log=ssd_A_log[0], ssd_d=ssd_D[0], ssd_norm_g=ssd_norm_g[0], gm_v_g=gm_v_g_full,
                 gm_v_b=gm_v_b_full, gm_w_s=gm_w_s[0], gm_b_s=gm_b_s[0])
    loss_parts, grad_x, g = _local_step(x[0], ctx[0], loss_target[0], mods, norm_g_full, get_w, small, put_grad)
    loss = lax.psum(0.5 / d * jnp.sum(loss_parts), ("x", "y", "c"))

    whole = {"ffn_w_in": (ffn_w_in, m_ffn_w_in, v_ffn_w_in), "ffn_w_out": (ffn_w_out, m_ffn_w_out, v_ffn_w_out),
             "ssd_w_in": (ssd_w_in, m_ssd_w_in, v_ssd_w_in), "ssd_w_out": (ssd_w_out, m_ssd_w_out, v_ssd_w_out),
             "gm_w_in": (gm_w_in, m_gm_w_in, v_gm_w_in), "gm_w_out": (gm_w_out, m_gm_w_out, v_gm_w_out)}
    res = {}

    def update_units(some, after):
        for unit, grp, idx in some:
            parts = _exchange_wait(*scatters[unit], after, scatter=True, name="scatter_wait_" + unit)
            for k, p in zip(idx, parts):
                which = ("in", "out")[k]
                nm = ("ffn" if grp.startswith("ffn") else grp) + "_w_" + which
                sel = (int(grp[3]), int(grp[4])) if grp.startswith("ffn") else (0,)
                res[nm] = _adamw(p, *whole[nm], name=f"adamw_{grp}_{which}", sel=sel, into=res.get(nm))
                after = res[nm][0]
        return after

    by_send = list(reversed(units))
    early_done = update_units(by_send[:4], grad_x)

    sg_names = ["dmx", "dmc", "norm_g", "ssd_conv_w", "ssd_conv_b", "ssd_dt_bias", "ssd_A_log", "ssd_D", "ssd_norm_g",
                "gm_v_g", "gm_v_b", "gm_w_s", "gm_b_s"]
    sg_shapes = [g[n].shape for n in sg_names]
    flat = jnp.concatenate([g[n].reshape(-1) for n in sg_names])
    npack = flat.shape[0]
    pad = (-npack) % 1024
    flat = jnp.pad(flat, (0, pad)).reshape(-1, 128)
    flat = flat + _zero_after(early_done, name="after_early_updates")
    (sg_all,), _ = _exchange([flat], scatter=False, name="gather_small_grads")
    update_units(by_send[4:], sg_all)
    sg_sum = _sum_slots(sg_all, name="sum_small_grads").reshape(-1)[:npack]
    sums = {}
    o = 0
    for n, shp in zip(sg_names, sg_shapes):
        sz = math.prod(shp)
        sums[n] = sg_sum[o:o + sz].reshape(shp)
        o += sz
    per_dev = sg_all.reshape(NDEV, -1)
    dmx_all = per_dev[:, :2 * N_MOD * d].reshape(NDEV, 2, N_MOD * d)
    dmc_all = per_dev[:, 2 * N_MOD * d:4 * N_MOD * d].reshape(NDEV, 2, N_MOD * d)

    (s16,) = _rowwise("ada_silu", lambda cc: ((_silu(cc),), ()), 2 * NDEV, [c16], [], [(d, F32)], tm=2 * NDEV)
    s16_t = s16.T
    g_ada_w, dcc_parts = [], []
    for i in range(2):
        rhs = jnp.concatenate([lax.dynamic_slice_in_dim(dmx_all[:, i], me * ncol, ncol, axis=1),
                               lax.dynamic_slice_in_dim(dmc_all[:, i], me * ncol, ncol, axis=1)], axis=0)
        g_ada_w.append(_mm_f32(s16_t, rhs, name=f"ada_dw{i}"))
        dmc_loc = lax.dynamic_slice_in_dim(sums["dmc"][i], me * ncol, ncol, axis=0)
        rhs_c = jnp.zeros((ncol, 128), F32).at[:, 0].set(dmc_loc)
        dcc_parts.append(_mm_f32(ada_w[i], rhs_c, name=f"ada_dcc{i}")[:, 0])
    g_ada_w = jnp.stack(g_ada_w)
    dcc_part = (dcc_parts[0] + dcc_parts[1]).reshape(8, 128)
    (dcc_all,), _ = _exchange([dcc_part], scatter=False, name="gather_dcc")
    g_c_ctx = _sum_slots(dcc_all, name="sum_dcc", scale_by=c_ctx.reshape(8, 128)).reshape(d)
    g_ada_b = sums["dmx"] + sums["dmc"]

    outs = _adamw(g_ada_w.reshape(1, -1, ncol), ada_w.reshape(-1, ncol), m_ada_w.reshape(-1, ncol),
                  v_ada_w.reshape(-1, ncol), name="adamw_ada_w")
    res["ada_w"] = [o_.reshape(ada_w.shape) for o_ in outs]

    loc = lambda a, ax, n: lax.dynamic_slice_in_dim(a, me * n, n, axis=ax)
    small_g = dict(c_ctx=g_c_ctx, ada_b=g_ada_b, norm_g=loc(sums["norm_g"], 2, 128),
                   ssd_conv_w=loc(sums["ssd_conv_w"], 1, 512)[None], ssd_conv_b=sums["ssd_conv_b"][None],
                   ssd_dt_bias=sums["ssd_dt_bias"][None], ssd_A_log=sums["ssd_A_log"][None], ssd_D=sums["ssd_D"][None],
                   ssd_norm_g=sums["ssd_norm_g"][None], gm_v_g=loc(sums["gm_v_g"], 0, 256)[None],
                   gm_v_b=loc(sums["gm_v_b"], 0, 256)[None], gm_w_s=sums["gm_w_s"][None], gm_b_s=sums["gm_b_s"][None])
    small_w = dict(c_ctx=(c_ctx, m_c_ctx, v_c_ctx), ada_b=(ada_b, m_ada_b, v_ada_b), norm_g=(norm_g, m_norm_g, v_norm_g),
                   ssd_conv_w=(ssd_conv_w, m_ssd_conv_w, v_ssd_conv_w), ssd_conv_b=(ssd_conv_b, m_ssd_conv_b, v_ssd_conv_b),
                   ssd_dt_bias=(ssd_dt_bias, m_ssd_dt_bias, v_ssd_dt_bias), ssd_A_log=(ssd_A_log, m_ssd_A_log, v_ssd_A_log),
                   ssd_D=(ssd_D, m_ssd_D, v_ssd_D), ssd_norm_g=(ssd_norm_g, m_ssd_norm_g, v_ssd_norm_g),
                   gm_v_g=(gm_v_g, m_gm_v_g, v_gm_v_g), gm_v_b=(gm_v_b, m_gm_v_b, v_gm_v_b),
                   gm_w_s=(gm_w_s, m_gm_w_s, v_gm_w_s), gm_b_s=(gm_b_s, m_gm_b_s, v_gm_b_s))
    sn = list(small_w)

    def pack(arrs):
        f = jnp.concatenate([a.reshape(-1) for a in arrs])
        return jnp.pad(f, (0, (-f.shape[0]) % 1024)).reshape(-1, 128)

    pg = pack([small_g[n].reshape(small_w[n][0].shape) for n in sn])
    outs = _adamw(pg[None], pack([small_w[n][0] for n in sn]), pack([small_w[n][1] for n in sn]),
                  pack([small_w[n][2] for n in sn]), name="adamw_small")
    flat_outs = [o_.reshape(-1) for o_ in outs]
    o = 0
    for n in sn:
        shp = small_w[n][0].shape
        sz = math.prod(shp)
        res[n] = [fo[o:o + sz].reshape(shp) for fo in flat_outs]
        o += sz

    order = ["c_ctx", "ada_w", "ada_b", "norm_g", "ffn_w_in", "ffn_w_out", "ssd_w_in", "ssd_conv_w", "ssd_conv_b",
             "ssd_dt_bias", "ssd_A_log", "ssd_D", "ssd_norm_g", "ssd_w_out", "gm_w_in", "gm_v_g", "gm_v_b", "gm_w_s",
             "gm_b_s", "gm_w_out"]
    result = [loss, grad_x[None]]
    for k in range(4):
        result += [res[n][k] for n in order]
    return tuple(result)
```

```python
import functools
import math

import jax
import jax.numpy as jnp
from jax import lax
from jax.experimental import pallas as pl
from jax.experimental.pallas import tpu as pltpu

F32 = jnp.float32
BF16 = jnp.bfloat16

NDEV = 8
D_MODEL = 1024
FFN_DIM = 2816
N_MOD = 9
EPS = 1e-6
SSD_INNER = 2048
SSD_HEADS = 32
SSD_HEAD_DIM = 64
SSD_GROUPS = 8
SSD_HPG = 4
SSD_STATE = 128
SSD_CONV = 5
SSD_CONV_DIM = 4096
CHUNK = 128
GM_INNER = 2048
GM_GROUPS = 8
GM_GROUP_DIM = 256
ADAM_LR = 0.001
ADAM_B1 = 0.9
ADAM_B2 = 0.999
ADAM_EPS = 1e-08
ADAM_WD = 0.01
ADAM_STEP = 10
NEG_BIG = -1e30
VMEM_LIMIT_BYTES = 56 * 1024 * 1024
HI = lax.Precision.HIGHEST


def _params(*sem):
    return pltpu.CompilerParams(dimension_semantics=sem, vmem_limit_bytes=VMEM_LIMIT_BYTES)


def _pick(n, target, mult=16):
    if n <= target:
        return n
    for t in range(target - target % mult, 0, -mult):
        if n % t == 0:
            return t
    raise ValueError((n, target, mult))


def _sig(x):
    return 0.5 * jnp.tanh(0.5 * x) + 0.5


def _silu(x):
    return x * _sig(x)


def _dsilu(x):
    s = _sig(x)
    return s * (1.0 + x * (1.0 - s))


_GELU_C = math.sqrt(2.0 / math.pi)


def _gelu(x):
    return 0.5 * x * (1.0 + jnp.tanh(_GELU_C * (x + 0.044715 * x * x * x)))


def _dgelu(x):
    t = jnp.tanh(_GELU_C * (x + 0.044715 * x * x * x))
    return 0.5 * (1.0 + t) + 0.5 * x * (1.0 - t * t) * _GELU_C * (1.0 + 3.0 * 0.044715 * x * x)


def _softplus(x):
    return jnp.maximum(x, 0.0) + jnp.log1p(jnp.exp(-jnp.abs(x)))


def _sum0(v):
    return jnp.sum(v, axis=0, keepdims=True)


def _rms(h):
    r = lax.rsqrt(jnp.mean(h * h, axis=-1, keepdims=True) + EPS)
    return h * r, r


def _dot(a, b, dims=((1,), (0,)), precision=None):
    return lax.dot_general(a, b, (dims, ((), ())), preferred_element_type=F32, precision=precision)


_NT = ((1,), (1,))
_TN = ((0,), (0,))


def _rowwise(name, fn, n_rows, rows, consts, outs, accs=(), *, tm, nseg=1, seg_blocks=0):
    assert n_rows % tm == 0
    if nseg == 2:
        assert seg_blocks > 0
        seg = lambda i: jnp.where(i < seg_blocks, 0, 1)
    else:
        seg = lambda i: 0
    in_specs, args, lacking = [], [], []
    for r in rows:
        arr, width, cb, off = r if isinstance(r, tuple) else (r, r.shape[1], 0, 0)
        in_specs.append(pl.BlockSpec((tm, width), lambda i, cb=cb, off=off: (jnp.maximum(i + off, 0), cb)))
        args.append(arr)
        lacking.append(-off if off < 0 else 0)
    for kind, arr in consts:
        if kind == "seg":
            assert arr.shape[0] == nseg and arr.shape[1] == 1, arr.shape
            in_specs.append(pl.BlockSpec((None, 1, arr.shape[2]), lambda i: (seg(i), 0, 0)))
        else:
            in_specs.append(pl.BlockSpec(arr.shape, lambda i: (0, 0)))
        args.append(arr)
    out_shape = [jax.ShapeDtypeStruct((n_rows, w), dt) for w, dt in outs]
    out_specs = [pl.BlockSpec((tm, w), lambda i: (i, 0)) for w, _ in outs]
    out_shape += [jax.ShapeDtypeStruct((nseg, 1, w), F32) for w in accs]
    out_specs += [pl.BlockSpec((None, 1, w), lambda i: (seg(i), 0, 0)) for w in accs]
    n_in, n_out, n_acc = len(args), len(outs), len(accs)

    def kern(*refs):
        i = pl.program_id(0)
        ins = [r[...] for r in refs[:n_in]]
        for k, lack in enumerate(lacking):
            if lack:
                ins[k] = jnp.where(i >= lack, ins[k], jnp.zeros_like(ins[k]))
        res, terms = fn(*ins)
        for ref, v in zip(refs[n_in:n_in + n_out], res):
            ref[...] = v.astype(ref.dtype)
        if n_acc:
            sums = [_sum0(v) for v in terms]
            first = (i == 0) | (i == seg_blocks) if nseg == 2 else (i == 0)
            acc_refs = refs[n_in + n_out:]

            @pl.when(first)
            def _():
                for ref, v in zip(acc_refs, sums):
                    ref[...] = v

            @pl.when(jnp.logical_not(first))
            def _():
                for ref, v in zip(acc_refs, sums):
                    ref[...] += v

    res = pl.pallas_call(
        kern, name=name, grid=(n_rows // tm,), in_specs=in_specs, out_specs=out_specs, out_shape=out_shape,
        compiler_params=_params("arbitrary"),
    )(*args)
    return res


def _pre_fwd_fn(h, g, shift, scale):
    hh, _ = _rms(h)
    return (hh * g * (1.0 + scale) + shift,), ()


def _pre_bwd_fn(du, h, dres, g, scale):
    hh, r = _rms(h)
    n = hh * g
    dn = du * (1.0 + scale)
    dhh = dn * g
    dh = dres + r * (dhh - hh * jnp.mean(dhh * hh, axis=-1, keepdims=True))
    return (dh,), (du, du * n, dn * hh)


def _post_fwd_fn(weight, h, y, g, gate):
    yh, _ = _rms(y)
    return (h + weight * gate * (yh * g),), ()


def _out_post_fn(weight, y, h, g, gate):
    return (y,) + _post_fwd_fn(weight, h, y, g, gate)[0], ()


def _post_bwd_fn(weight, dh, y, g, gate):
    yh, r = _rms(y)
    dr = dh * weight
    dyh = dr * gate * g
    dy = r * (dyh - yh * jnp.mean(dyh * yh, axis=-1, keepdims=True))
    return (dy,), (dr * yh * g, dr * gate * yh)


def _glu_bwd_fn(ds, a, b):
    a = a.astype(F32)
    b = b.astype(F32)
    sg = _sig(a)
    da = ds * b * (sg * (1.0 + a * (1.0 - sg)))
    db = ds * (a * sg)
    return (jnp.concatenate([da, db], axis=1),), ()


def _loss_fn(y, t):
    diff = y - t
    return (diff * (1.0 / D_MODEL),), (diff * diff,)


def _ssd_y(yf, yb, xs, z, dvec):
    y = yf + yb + dvec * xs
    return y, y * _silu(z)


def _ssdgate_fwd_fn(yf, yb, xs, z, dvec, ng):
    _, yg = _ssd_y(yf, yb, xs, z, dvec)
    parts = []
    for g in range(SSD_GROUPS):
        sl = slice(g * 256, (g + 1) * 256)
        parts.append(_rms(yg[:, sl])[0])
    return (jnp.concatenate(parts, axis=1) * ng,), ()


def _ssdgate_bwd_fn(dyn, yf, yb, xs, z, dvec, ng):
    y, yg = _ssd_y(yf, yb, xs, z, dvec)
    dyg_parts, ygh_parts = [], []
    for g in range(SSD_GROUPS):
        sl = slice(g * 256, (g + 1) * 256)
        ygh, r = _rms(yg[:, sl])
        d = dyn[:, sl] * ng[:, sl]
        dyg_parts.append(r * (d - ygh * jnp.mean(d * ygh, axis=-1, keepdims=True)))
        ygh_parts.append(ygh)
    dyg = jnp.concatenate(dyg_parts, axis=1)
    ygh = jnp.concatenate(ygh_parts, axis=1)
    dy = dyg * _silu(z)
    dz = dyg * y * _dsilu(z)
    return (dy, dz), (dyn * ygh, dy * xs)


def _ln_stats(v):
    mu = jnp.mean(v, axis=-1, keepdims=True)
    vc = v - mu
    r = lax.rsqrt(jnp.mean(vc * vc, axis=-1, keepdims=True) + EPS)
    return vc * r, r


def _gm_act_fwd_fn(p, vg, vb):
    gu = _gelu(p[:, :GM_INNER])
    gvh, _ = _ln_stats(_gelu(p[:, GM_INNER:]))
    return (gu, gvh * vg + vb), ()


def _gm_act_bwd_fn(p, dgu, dgvn, vg):
    pu = p[:, :GM_INNER]
    pv = p[:, GM_INNER:]
    gvh, r = _ln_stats(_gelu(pv))
    dgvh = dgvn * vg
    dgv = r * (dgvh - jnp.mean(dgvh, axis=-1, keepdims=True) - gvh * jnp.mean(dgvh * gvh, axis=-1, keepdims=True))
    dp = jnp.concatenate([dgu * _dgelu(pu), dgv * _dgelu(pv)], axis=1)
    return (dp,), (dgvn * gvh, dgvn)


def _mm(a, b, *, out_dtype, name, tm=1088, tn=1024, tk=1408, add=None, rhs_t=False, n=None, b_off=(0, 0)):
    m, k = a.shape
    if n is None:
        n, k2 = b.shape if rhs_t else b.shape[::-1]
        assert k == k2
    tm, tn, tk = _pick(m, tm), _pick(n, tn, 128), _pick(k, tk, 128)
    o0, o1 = b_off
    nk = k // tk
    dims = _NT if rhs_t else ((1,), (0,))

    def kern(*refs):
        a_ref, b_ref = refs[:2]
        add_ref = refs[2] if add is not None else None
        o_ref = refs[3] if add is not None else refs[2]

        def finish(r):
            if add is not None:
                r = r + add_ref[...]
            o_ref[...] = r.astype(o_ref.dtype)

        p = _dot(a_ref[...], b_ref[...], dims)
        if nk == 1:
            finish(p)
            return
        acc_ref = refs[-1]
        kk = pl.program_id(2)

        @pl.when(kk == 0)
        def _():
            acc_ref[...] = p

        @pl.when((kk > 0) & (kk < nk - 1))
        def _():
            acc_ref[...] += p

        @pl.when(kk == nk - 1)
        def _():
            finish(acc_ref[...] + p)

    if rhs_t:
        b_spec = pl.BlockSpec((tn, tk), lambda i, j, kk: (j + o0, kk + o1))
    else:
        b_spec = pl.BlockSpec((tk, tn), lambda i, j, kk: (kk + o0, j + o1))
    in_specs = [pl.BlockSpec((tm, tk), lambda i, j, kk: (i, kk)), b_spec]
    args = [a, b]
    if add is not None:
        in_specs.append(pl.BlockSpec((tm, tn), lambda i, j, kk: (i, j)))
        args.append(add)
    return pl.pallas_call(
        kern, name=name, grid=(m // tm, n // tn, nk), in_specs=in_specs,
        out_specs=pl.BlockSpec((tm, tn), lambda i, j, kk: (i, j)),
        out_shape=jax.ShapeDtypeStruct((m, n), out_dtype),
        scratch_shapes=[pltpu.VMEM((tm, tn), F32)] if nk > 1 else [],
        compiler_params=_params("parallel", "parallel", "arbitrary"),
    )(*args)


def _mm_rows(a, b, fn, rows, consts, outs, accs=(), *, name, tm=544, tk=1408, rhs_t=False, n_ctx=0):
    m, k = a.shape
    n = b.shape[0] if rhs_t else b.shape[1]
    tm, tk = _pick(m, tm), _pick(k, tk, 128)
    nk = k // tk
    dims = _NT if rhs_t else ((1,), (0,))
    n_rows, n_const, n_out, n_acc = len(rows), len(consts), len(outs), len(accs)

    def kern(*refs):
        a_ref, b_ref = refs[:2]
        row_refs = refs[2:2 + n_rows]
        const_refs = refs[2 + n_rows:2 + n_rows + n_const]
        out_refs = refs[2 + n_rows + n_const:2 + n_rows + n_const + n_out]
        acc_refs = refs[2 + n_rows + n_const + n_out:2 + n_rows + n_const + n_out + n_acc]
        i, kk = pl.program_id(0), pl.program_id(1)

        def finish(p):
            is_ctx = (i * tm + lax.broadcasted_iota(jnp.int32, (tm, 1), 0)) < n_ctx
            cvals = []
            for (kind, arr), ref in zip(consts, const_refs):
                if kind == "seg":
                    cvals.append(jnp.where(is_ctx, ref[0], ref[1]) if arr.shape[0] == 2 else ref[0])
                else:
                    cvals.append(ref[...])
            res, terms = fn(p, *[r[...] for r in row_refs], *cvals)
            for ref, v in zip(out_refs, res):
                ref[...] = v.astype(ref.dtype)
            for ref, v in zip(acc_refs, terms):
                s_all = _sum0(v)
                s_ctx = _sum0(jnp.where(is_ctx, v, 0.0)) if n_ctx else jnp.zeros_like(s_all)
                both = jnp.concatenate([s_ctx, s_all - s_ctx], axis=0)[:, None, :]

                @pl.when(i == 0)
                def _():
                    ref[...] = both

                @pl.when(i > 0)
                def _():
                    ref[...] += both

        p = _dot(a_ref[...], b_ref[...], dims)
        if nk == 1:
            finish(p)
            return
        scr = refs[-1]

        @pl.when(kk == 0)
        def _():
            scr[...] = p

        @pl.when((kk > 0) & (kk < nk - 1))
        def _():
            scr[...] += p

        @pl.when(kk == nk - 1)
        def _():
            finish(scr[...] + p)

    b_spec = pl.BlockSpec((n, tk), lambda i, kk: (0, kk)) if rhs_t else pl.BlockSpec((tk, n), lambda i, kk: (kk, 0))
    in_specs = [pl.BlockSpec((tm, tk), lambda i, kk: (i, kk)), b_spec]
    in_specs += [pl.BlockSpec((tm, r.shape[1]), lambda i, kk: (i, 0)) for r in rows]
    for kind, arr in consts:
        in_specs.append(pl.BlockSpec(arr.shape, (lambda i, kk: (0, 0, 0)) if kind == "seg" else (lambda i, kk: (0, 0))))
    out_shape = [jax.ShapeDtypeStruct((m, w), dt) for w, dt in outs]
    out_specs = [pl.BlockSpec((tm, w), lambda i, kk: (i, 0)) for w, _ in outs]
    out_shape += [jax.ShapeDtypeStruct((2, 1, w), F32) for w in accs]
    out_specs += [pl.BlockSpec((2, 1, w), lambda i, kk: (0, 0, 0)) for w in accs]
    return pl.pallas_call(
        kern, name=name, grid=(m // tm, nk), in_specs=in_specs, out_specs=out_specs, out_shape=out_shape,
        scratch_shapes=[pltpu.VMEM((tm, n), F32)] if nk > 1 else [],
        compiler_params=_params("arbitrary", "arbitrary"),
    )(a, b, *rows, *[arr for _, arr in consts])


def _mm_glu(u, win, *, name, tm=2176, tn=256):
    m, k = u.shape
    n = win.shape[1] // 2
    tm, tn = _pick(m, tm), _pick(n, tn, 128)
    nj = n // tn

    def kern(u_ref, wa_ref, wb_ref, s_ref, a_ref, b_ref):
        uu = u_ref[...]
        a = jnp.dot(uu, wa_ref[...], preferred_element_type=F32)
        b = jnp.dot(uu, wb_ref[...], preferred_element_type=F32)
        s_ref[...] = (_silu(a) * b).astype(BF16)
        a_ref[...] = a.astype(BF16)
        b_ref[...] = b.astype(BF16)

    ospec = pl.BlockSpec((tm, tn), lambda i, j: (i, j))
    return pl.pallas_call(
        kern, name=name, grid=(m // tm, nj),
        in_specs=[pl.BlockSpec((tm, k), lambda i, j: (i, 0)), pl.BlockSpec((k, tn), lambda i, j: (0, j)),
                  pl.BlockSpec((k, tn), lambda i, j: (0, nj + j))],
        out_specs=[ospec, ospec, ospec],
        out_shape=[jax.ShapeDtypeStruct((m, n), BF16)] * 3,
        compiler_params=_params("parallel", "parallel"),
    )(u, win, win)


def _mm_tn(a, b, *, name, tm=1024, tn=1024, tk=1088, col_blocks=None):
    t, m = a.shape
    t2, n = b.shape
    assert t == t2
    tm, tn, tk = _pick(m, tm, 128), _pick(n, tn, 128), _pick(t, tk)
    nk = t // tk
    if col_blocks is None:
        def kern(a_ref, b_ref, o_ref):
            kk = pl.program_id(2)

            @pl.when(kk == 0)
            def _():
                o_ref[...] = jnp.zeros_like(o_ref)

            o_ref[...] += _dot(a_ref[...], b_ref[...], _TN)

        out_spec = pl.BlockSpec((tm, tn), lambda i, j, kk: (i, j))
        out_shape = jax.ShapeDtypeStruct((m, n), F32)
        scratch = []
    else:
        wb = n // col_blocks
        per = tn // wb
        assert tn % wb == 0 and wb % 8 == 0

        def kern(a_ref, b_ref, o_ref, acc_ref):
            kk = pl.program_id(2)
            p = _dot(a_ref[...], b_ref[...], _TN)

            @pl.when(kk == 0)
            def _():
                acc_ref[...] = p

            @pl.when((kk > 0) & (kk < nk - 1))
            def _():
                acc_ref[...] += p

            @pl.when(kk == nk - 1)
            def _():
                r = acc_ref[...] + p if nk > 1 else p
                for c in range(per):
                    o_ref[c] = r[:, c * wb:(c + 1) * wb].astype(BF16)

        out_spec = pl.BlockSpec((per, tm, wb), lambda i, j, kk: (j, i, 0))
        out_shape = jax.ShapeDtypeStruct((col_blocks, m, wb), BF16)
        scratch = [pltpu.VMEM((tm, tn), F32)]

    return pl.pallas_call(
        kern, name=name, grid=(m // tm, n // tn, nk),
        in_specs=[pl.BlockSpec((tk, tm), lambda i, j, kk: (kk, i)), pl.BlockSpec((tk, tn), lambda i, j, kk: (kk, j))],
        out_specs=out_spec, out_shape=out_shape, scratch_shapes=scratch,
        compiler_params=_params("parallel", "parallel", "arbitrary"),
    )(a, b)


def _mm_f32(a, b, *, name, silu_a=False, bias=None):
    m, k = a.shape
    n = b.shape[1]

    def kern(*refs):
        if bias is None:
            a_ref, b_ref, o_ref = refs
        else:
            a_ref, b_ref, bias_ref, o_ref = refs
        av = a_ref[...]
        if silu_a:
            av = _silu(av)
        r = jnp.dot(av, b_ref[...], preferred_element_type=F32, precision=HI)
        if bias is not None:
            r = r + bias_ref[...]
        o_ref[...] = r

    args = [a, b] + ([] if bias is None else [bias])
    return pl.pallas_call(kern, name=name, out_shape=jax.ShapeDtypeStruct((m, n), F32),
                          compiler_params=pltpu.CompilerParams(vmem_limit_bytes=VMEM_LIMIT_BYTES))(*args)


CONV_WIN = 32


def _conv_windows(n, n_ctx):
    assert n_ctx % CONV_WIN == 0 and n_ctx >= CONV_WIN and n - n_ctx >= CONV_WIN
    return (0, n_ctx - CONV_WIN // 2, n - CONV_WIN)


def _tap_outside(r0, s, n, n_ctx):
    t = r0 + lax.broadcasted_iota(jnp.int32, (CONV_WIN, 1), 0)
    lo = jnp.where(t < n_ctx, 0, n_ctx)
    hi = jnp.where(t < n_ctx, n_ctx, n)
    return jnp.where((t + s >= lo) & (t + s < hi), 0.0, 1.0)


def _rolled(v, s):
    return v if s == 0 else pltpu.roll(v, (-s) % v.shape[0], 0)


def _conv_fwd(xp, w8, b, *, n_ctx, name, cb=256):
    n, c = xp.shape
    half = SSD_CONV // 2

    def kern(x_ref, w_ref, b_ref, cpre_ref, act_ref):
        x = x_ref[...]
        acc = jnp.zeros_like(x) + b_ref[...]
        rolled = {}
        for k in range(SSD_CONV):
            rolled[k] = _rolled(x, k - half)
            acc = acc + rolled[k] * w_ref[k:k + 1, :]
        cpre_ref[...] = acc
        act_ref[...] = _silu(acc)
        for r0 in _conv_windows(n, n_ctx):
            rows = slice(r0, r0 + CONV_WIN)
            fix = acc[rows]
            for k in range(SSD_CONV):
                if k != half:
                    fix = fix - rolled[k][rows] * w_ref[k:k + 1, :] * _tap_outside(r0, k - half, n, n_ctx)
            cpre_ref[rows, :] = fix
            act_ref[rows, :] = _silu(fix)

    spec = pl.BlockSpec((n, cb), lambda j: (0, j))
    return pl.pallas_call(
        kern, name=name, grid=(c // cb,),
        in_specs=[spec, pl.BlockSpec((8, cb), lambda j: (0, j)), pl.BlockSpec((1, cb), lambda j: (0, j))],
        out_specs=[spec, spec], out_shape=[jax.ShapeDtypeStruct((n, c), F32)] * 2,
        compiler_params=_params("parallel"),
    )(xp, w8, b)


def _conv_bwd(d1, d2, cpre, xp, w8, *, n_ctx, name, cb=128):
    n, c = xp.shape
    half = SSD_CONV // 2

    def kern(d1_ref, d2_ref, cpre_ref, x_ref, w_ref, dx_ref, dw_ref, db_ref):
        g = (d1_ref[...] + d2_ref[...]) * _dsilu(cpre_ref[...])
        x = x_ref[...]
        dx = jnp.zeros_like(g)
        dw_ref[...] = jnp.zeros_like(dw_ref)
        g_rolled = {}
        for k in range(SSD_CONV):
            s = k - half
            g_rolled[k] = _rolled(g, -s)
            dx = dx + g_rolled[k] * w_ref[k:k + 1, :]
            xr = _rolled(x, s)
            dw = _sum0(g * xr)
            if s != 0:
                for r0 in _conv_windows(n, n_ctx):
                    rows = slice(r0, r0 + CONV_WIN)
                    dw = dw - _sum0(g[rows] * xr[rows] * _tap_outside(r0, s, n, n_ctx))
            dw_ref[k:k + 1, :] = dw
        dx_ref[...] = dx.astype(BF16)
        for r0 in _conv_windows(n, n_ctx):
            rows = slice(r0, r0 + CONV_WIN)
            fix = dx[rows]
            for k in range(SSD_CONV):
                if k != half:
                    fix = fix - g_rolled[k][rows] * w_ref[k:k + 1, :] * _tap_outside(r0, half - k, n, n_ctx)
            dx_ref[rows, :] = fix.astype(BF16)
        db_ref[...] = _sum0(g)

    spec = pl.BlockSpec((n, cb), lambda j: (0, j))
    return pl.pallas_call(
        kern, name=name, grid=(c // cb,),
        in_specs=[spec, spec, spec, spec, pl.BlockSpec((8, cb), lambda j: (0, j))],
        out_specs=[spec, pl.BlockSpec((8, cb), lambda j: (0, j)), pl.BlockSpec((1, cb), lambda j: (0, j))],
        out_shape=[jax.ShapeDtypeStruct((n, c), BF16), jax.ShapeDtypeStruct((8, c), F32),
                   jax.ShapeDtypeStruct((1, c), F32)],
        compiler_params=_params("parallel"),
    )(d1, d2, cpre, xp, w8)


def _chunk_of(s, nc, n_ctx_chunks, rev):
    if not rev:
        return s
    return jnp.where(s < n_ctx_chunks, n_ctx_chunks - 1 - s, nc - 1 - (s - n_ctx_chunks))


def _scan_common(dt_raw, dtT_raw, bias_r, bias_c, alog_r, alog_c, rev):
    ii = lax.broadcasted_iota(jnp.int32, (CHUNK, CHUNK), 0)
    jj = lax.broadcasted_iota(jnp.int32, (CHUNK, CHUNK), 1)
    tri = (jj >= ii) if rev else (jj <= ii)
    tri_t = (ii >= jj) if rev else (ii <= jj)
    a_r = -jnp.exp(alog_r)
    a_c = -jnp.exp(alog_c)
    dt = _softplus(dt_raw + bias_r)
    dt_t = _softplus(dtT_raw + bias_c)
    al = dt * a_r
    acum = _dot(tri.astype(F32), al, precision=HI)
    acum_t = _dot(dt_t * a_c, tri_t.astype(F32), precision=HI)
    atot = _sum0(al)
    return tri, tri_t, a_r, dt, acum, acum_t, atot


def _head_spread():
    return jnp.repeat(jnp.eye(SSD_HEADS, dtype=BF16), SSD_HEAD_DIM, axis=1)


def _dot_sel(v, sel):
    hi = v.astype(BF16)
    lo = (v - hi.astype(F32)).astype(BF16)
    return _dot(hi, sel) + _dot(lo, sel)


def _ssd_scan_fwd(xbc, dt_raw, dtT_raw, bias_r, bias_c, alog_r, alog_c, *, rev, n_ctx_chunks, name):
    n = xbc.shape[0]
    nc = n // CHUNK
    cidx = functools.partial(_chunk_of, nc=nc, n_ctx_chunks=n_ctx_chunks, rev=rev)

    def kern(xs_ref, b_ref, c_ref, dt_ref, dtT_ref, br_ref, bc_ref, ar_ref, ac_ref, e_ref, y_ref, hs_ref, h_scr):
        @pl.when(pl.program_id(0) == 0)
        def _():
            h_scr[...] = jnp.zeros_like(h_scr)

        tri, _, _, dt, acum, acum_t, atot = _scan_common(
            dt_ref[...], dtT_ref[...], br_ref[...], bc_ref[...], ar_ref[...], ac_ref[...], rev)
        etot = jnp.exp(atot)
        spread = lambda v: _dot_sel(v, e_ref[...])
        xdt_all = xs_ref[...] * spread(dt)
        eax = spread(jnp.exp(acum))
        xdw_all = xdt_all * spread(jnp.exp(atot - acum))
        hs_ref[...] = h_scr[...]
        for g in range(SSD_GROUPS):
            gs = slice(g * 256, (g + 1) * 256)
            bg = b_ref[:, g * SSD_STATE:(g + 1) * SSD_STATE].astype(BF16)
            cg = c_ref[:, g * SSD_STATE:(g + 1) * SSD_STATE].astype(BF16)
            cb = _dot(cg, bg, _NT)
            h4 = h_scr[gs, :]
            ys = []
            for k in range(SSD_HPG):
                h = g * SSD_HPG + k
                lmat = jnp.exp(jnp.where(tri, acum[:, h:h + 1] - acum_t[h:h + 1, :], NEG_BIG))
                xdt_h = xdt_all[:, h * SSD_HEAD_DIM:(h + 1) * SSD_HEAD_DIM].astype(BF16)
                ys.append(_dot((cb * lmat).astype(BF16), xdt_h))
            y_ref[:, gs] = jnp.concatenate(ys, axis=1) + _dot(cg, h4.astype(BF16), _NT) * eax[:, gs]
            s4 = _dot(xdw_all[:, gs].astype(BF16), bg, _TN)
            for k in range(SSD_HPG):
                h = g * SSD_HPG + k
                rs = slice(h * SSD_HEAD_DIM, (h + 1) * SSD_HEAD_DIM)
                h_scr[rs, :] = h4[k * SSD_HEAD_DIM:(k + 1) * SSD_HEAD_DIM] * etot[:, h:h + 1] + \
                    s4[k * SSD_HEAD_DIM:(k + 1) * SSD_HEAD_DIM]

    nh = SSD_HEADS
    small = lambda shape: pl.BlockSpec(shape, lambda s: (0, 0))
    return pl.pallas_call(
        kern, name=name, grid=(nc,),
        in_specs=[pl.BlockSpec((CHUNK, SSD_INNER), lambda s: (cidx(s), 0)),
                  pl.BlockSpec((CHUNK, 1024), lambda s: (cidx(s), 2)),
                  pl.BlockSpec((CHUNK, 1024), lambda s: (cidx(s), 3)),
                  pl.BlockSpec((CHUNK, nh), lambda s: (cidx(s), 0)),
                  pl.BlockSpec((nh, CHUNK), lambda s: (0, cidx(s))),
                  small((1, nh)), small((nh, 1)), small((1, nh)), small((nh, 1)), small((nh, SSD_INNER))],
        out_specs=[pl.BlockSpec((CHUNK, SSD_INNER), lambda s: (cidx(s), 0)),
                   pl.BlockSpec((None, SSD_INNER, SSD_STATE), lambda s: (s, 0, 0))],
        out_shape=[jax.ShapeDtypeStruct((n, SSD_INNER), F32),
                   jax.ShapeDtypeStruct((nc, SSD_INNER, SSD_STATE), F32)],
        scratch_shapes=[pltpu.VMEM((SSD_INNER, SSD_STATE), F32)],
        compiler_params=_params("arbitrary"),
    )(xbc, xbc, xbc, dt_raw, dtT_raw, bias_r, bias_c, alog_r, alog_c, _head_spread())


def _ssd_scan_bwd(dy, xbc, hs, dt_raw, dtT_raw, bias_r, bias_c, alog_r, alog_c, dvec, *, rev, n_ctx_chunks,
                  direct, name):
    n = xbc.shape[0]
    nc = n // CHUNK
    nh = SSD_HEADS
    step_of = lambda r: nc - 1 - r
    cidx = lambda r: _chunk_of(step_of(r), nc, n_ctx_chunks, rev)

    def kern(dy_ref, xs_ref, b_ref, c_ref, hs_ref, dt_ref, dtT_ref, br_ref, bc_ref, ar_ref, ac_ref, dv_ref,
             e_ref, et_ref, dx_ref, ddt_ref, dal_ref, dbias_ref, dh_scr):
        @pl.when(pl.program_id(0) == 0)
        def _():
            dh_scr[...] = jnp.zeros_like(dh_scr)
            dal_ref[...] = jnp.zeros_like(dal_ref)
            dbias_ref[...] = jnp.zeros_like(dbias_ref)

        tri, tri_t, a_r, dt, acum, acum_t, atot = _scan_common(
            dt_ref[...], dtT_ref[...], br_ref[...], bc_ref[...], ar_ref[...], ac_ref[...], rev)
        etot = jnp.exp(atot)
        spread = lambda v: _dot_sel(v, e_ref[...])
        gather = lambda v: _dot_sel(v, et_ref[...])
        xs_all = xs_ref[...]
        dy_all = dy_ref[...]
        dtx = spread(dt)
        eax = spread(jnp.exp(acum))
        decx = spread(jnp.exp(atot - acum))
        xdt_all = xs_all * dtx
        xdw_all = xdt_all * decx
        dyo_all = dy_all * eax
        lane = lax.broadcasted_iota(jnp.int32, (CHUNK, nh), 1)
        lane1 = lax.broadcasted_iota(jnp.int32, (1, nh), 1)
        sub = lax.broadcasted_iota(jnp.int32, (nh, CHUNK), 0)
        g_rows = jnp.zeros((CHUNK, nh), F32)
        g_cols = jnp.zeros((nh, CHUNK), F32)
        dtot = jnp.zeros((1, nh), F32)
        q_col, q_e, q_dt = [], [], []
        for g in range(SSD_GROUPS):
            gs = slice(g * 256, (g + 1) * 256)
            bg = b_ref[:, g * SSD_STATE:(g + 1) * SSD_STATE].astype(BF16)
            cg = c_ref[:, g * SSD_STATE:(g + 1) * SSD_STATE].astype(BF16)
            cb = _dot(cg, bg, _NT)
            hs4 = hs_ref[gs, :]
            dh4 = dh_scr[gs, :]
            hs4_bf = hs4.astype(BF16)
            dh4_bf = dh4.astype(BF16)
            dy4 = dy_all[:, gs]
            dy4_bf = dy4.astype(BF16)
            xdt4_bf = xdt_all[:, gs].astype(BF16)
            xdw4 = xdw_all[:, gs]
            xdw4_bf = xdw4.astype(BF16)
            dyo4_bf = dyo_all[:, gs].astype(BF16)
            yoff4 = _dot(cg, hs4_bf, _NT) * eax[:, gs]
            dcg = _dot(dyo4_bf, hs4_bf)
            dh_new4 = _dot(dyo4_bf, cg, _TN)
            bdh4 = _dot(bg, dh4_bf, _NT)
            dbg = _dot(xdw4_bf, dh4_bf)
            e4 = xdw4 * bdh4
            q_col.append(dy4 * yoff4 - e4)
            q_e.append(e4)
            hsum = jnp.sum(dh4 * hs4, axis=1, keepdims=True)
            dcb = jnp.zeros((CHUNK, CHUNK), F32)
            dxdts = []
            for k in range(SSD_HPG):
                h = g * SSD_HPG + k
                ks = slice(k * SSD_HEAD_DIM, (k + 1) * SSD_HEAD_DIM)
                lmat = jnp.exp(jnp.where(tri, acum[:, h:h + 1] - acum_t[h:h + 1, :], NEG_BIG))
                mf = cb * lmat
                dm = _dot(dy4_bf[:, ks], xdt4_bf[:, ks], _NT)
                dcb = dcb + dm * lmat
                gmat = dm * mf
                g_rows = g_rows + jnp.where(lane == h, jnp.sum(gmat, axis=1, keepdims=True), 0.0)
                g_cols = g_cols + jnp.where(sub == h, _sum0(gmat), 0.0)
                dxdts.append(_dot(mf.astype(BF16), dy4_bf[:, ks], _TN))
                et = etot[:, h:h + 1]
                dtot = dtot + jnp.where(lane1 == h, _sum0(hsum[ks]) * et, 0.0)
                dh_scr[h * SSD_HEAD_DIM:(h + 1) * SSD_HEAD_DIM, :] = dh4[ks] * et + dh_new4[ks]
            dxdt4 = jnp.concatenate(dxdts, axis=1) + bdh4 * decx[:, gs]
            q_dt.append(dxdt4 * xs_all[:, gs])
            dx4 = dxdt4 * dtx[:, gs]
            if direct:
                dx4 = dx4 + dy4 * dv_ref[:, gs]
            dcb_bf = dcb.astype(BF16)
            dx_ref[:, gs] = dx4
            dx_ref[:, SSD_INNER + g * SSD_STATE:SSD_INNER + (g + 1) * SSD_STATE] = dbg + _dot(dcb_bf, cg, _TN)
            dx_ref[:, SSD_INNER + 1024 + g * SSD_STATE:SSD_INNER + 1024 + (g + 1) * SSD_STATE] = \
                dcg + _dot(dcb_bf, bg)
        e_heads = gather(jnp.concatenate(q_e, axis=1))
        dacum = gather(jnp.concatenate(q_col, axis=1)) + g_rows - g_cols.T
        dal = _dot(tri_t.astype(F32), dacum, precision=HI) + dtot + _sum0(e_heads)
        ddt = gather(jnp.concatenate(q_dt, axis=1)) + dal * a_r
        ddt_raw = ddt * _sig(dt_ref[...] + br_ref[...])
        ddt_ref[...] = ddt_raw
        dal_ref[...] += _sum0(dal * dt) * a_r
        dbias_ref[...] += _sum0(ddt_raw)

    small = lambda shape: pl.BlockSpec(shape, lambda r: (0, 0))
    return pl.pallas_call(
        kern, name=name, grid=(nc,),
        in_specs=[pl.BlockSpec((CHUNK, SSD_INNER), lambda r: (cidx(r), 0)),
                  pl.BlockSpec((CHUNK, SSD_INNER), lambda r: (cidx(r), 0)),
                  pl.BlockSpec((CHUNK, 1024), lambda r: (cidx(r), 2)),
                  pl.BlockSpec((CHUNK, 1024), lambda r: (cidx(r), 3)),
                  pl.BlockSpec((None, SSD_INNER, SSD_STATE), lambda r: (step_of(r), 0, 0)),
                  pl.BlockSpec((CHUNK, nh), lambda r: (cidx(r), 0)),
                  pl.BlockSpec((nh, CHUNK), lambda r: (0, cidx(r))),
                  small((1, nh)), small((nh, 1)), small((1, nh)), small((nh, 1)), small((1, SSD_INNER)),
                  small((nh, SSD_INNER)), small((SSD_INNER, nh))],
        out_specs=[pl.BlockSpec((CHUNK, SSD_CONV_DIM), lambda r: (cidx(r), 0)),
                   pl.BlockSpec((CHUNK, nh), lambda r: (cidx(r), 0)),
                   small((1, nh)), small((1, nh))],
        out_shape=[jax.ShapeDtypeStruct((n, SSD_CONV_DIM), F32), jax.ShapeDtypeStruct((n, nh), F32),
                   jax.ShapeDtypeStruct((1, nh), F32), jax.ShapeDtypeStruct((1, nh), F32)],
        scratch_shapes=[pltpu.VMEM((SSD_INNER, SSD_STATE), F32)],
        compiler_params=_params("arbitrary"),
    )(dy, xbc, xbc, xbc, hs, dt_raw, dtT_raw, bias_r, bias_c, alog_r, alog_c, dvec, _head_spread(),
      _head_spread().T)


def _gm_spatial_fwd(gu, gvn, ws, bst, *, name):
    n = gu.shape[0]

    def kern(gu_ref, gv_ref, ws_ref, bs_ref, o_ref):
        for g in range(GM_GROUPS):
            sl = slice(g * GM_GROUP_DIM, (g + 1) * GM_GROUP_DIM)
            s = _dot(ws_ref[g], gv_ref[:, sl]) + bs_ref[:, g:g + 1]
            o_ref[:, sl] = (gu_ref[:, sl] * s).astype(BF16)

    spec = pl.BlockSpec((CHUNK, GM_INNER), lambda i: (i, 0))
    return pl.pallas_call(
        kern, name=name, grid=(n // CHUNK,),
        in_specs=[spec, spec, pl.BlockSpec(ws.shape, lambda i: (0, 0, 0)), pl.BlockSpec(bst.shape, lambda i: (0, 0))],
        out_specs=spec, out_shape=jax.ShapeDtypeStruct((n, GM_INNER), BF16),
        compiler_params=_params("parallel"),
    )(gu, gvn, ws, bst)


def _gm_spatial_bwd(dt, gu, gvn, ws, wst, bst, *, name):
    n = gu.shape[0]

    def kern(dt_ref, gu_ref, gv_ref, ws_ref, wst_ref, bs_ref, dgu_ref, dgv_ref, dws_ref, dbs_ref):
        @pl.when(pl.program_id(0) == 0)
        def _():
            dws_ref[...] = jnp.zeros_like(dws_ref)
            dbs_ref[...] = jnp.zeros_like(dbs_ref)

        lane = lax.broadcasted_iota(jnp.int32, (CHUNK, GM_GROUPS), 1)
        dbs = jnp.zeros((CHUNK, GM_GROUPS), F32)
        for g in range(GM_GROUPS):
            sl = slice(g * GM_GROUP_DIM, (g + 1) * GM_GROUP_DIM)
            gv = gv_ref[:, sl]
            s = _dot(ws_ref[g], gv) + bs_ref[:, g:g + 1]
            d = dt_ref[:, sl]
            dgu_ref[:, sl] = d * s
            ds = d * gu_ref[:, sl]
            ds_bf = ds.astype(BF16)
            dws_ref[g] += _dot(ds_bf, gv, _NT)
            dgv_ref[:, sl] = _dot(wst_ref[g], ds_bf)
            dbs = dbs + jnp.where(lane == g, jnp.sum(ds, axis=1, keepdims=True), 0.0)
        dbs_ref[...] += dbs

    spec = pl.BlockSpec((CHUNK, GM_INNER), lambda i: (i, 0))
    wspec = pl.BlockSpec(ws.shape, lambda i: (0, 0, 0))
    bspec = pl.BlockSpec(bst.shape, lambda i: (0, 0))
    return pl.pallas_call(
        kern, name=name, grid=(n // CHUNK,),
        in_specs=[spec, spec, spec, wspec, wspec, bspec],
        out_specs=[spec, spec, wspec, bspec],
        out_shape=[jax.ShapeDtypeStruct((n, GM_INNER), F32), jax.ShapeDtypeStruct((n, GM_INNER), F32),
                   jax.ShapeDtypeStruct(ws.shape, F32), jax.ShapeDtypeStruct(bst.shape, F32)],
        compiler_params=_params("arbitrary"),
    )(dt, gu, gvn, ws, wst, bst)


def _adamw(parts, w, m, v, *, name, tm=256, sel=(), into=None):
    ns, r, wd = parts.shape
    tm = _pick(r, tm, 8)
    lead = len(sel)
    assert w.shape[lead:] == (r, wd) and lead == w.ndim - 2

    def kern(*refs):
        p_ref, w_ref, m_ref, v_ref = refs[:4]
        g_ref, d_ref, nm_ref, nv_ref = refs[-4:]
        g = p_ref[0].astype(F32)
        for s in range(1, ns):
            g = g + p_ref[s].astype(F32)
        m2 = ADAM_B1 * m_ref[...] + (1.0 - ADAM_B1) * g
        v2 = ADAM_B2 * v_ref[...] + (1.0 - ADAM_B2) * (g * g)
        m_hat = m2 / (1.0 - ADAM_B1 ** ADAM_STEP)
        v_hat = v2 / (1.0 - ADAM_B2 ** ADAM_STEP)
        g_ref[...] = g
        d_ref[...] = -ADAM_LR * (m_hat / (jnp.sqrt(v_hat) + ADAM_EPS) + ADAM_WD * w_ref[...])
        nm_ref[...] = m2
        nv_ref[...] = v2

    spec = pl.BlockSpec((None,) * lead + (tm, wd), lambda i: tuple(sel) + (i, 0))
    chained = any(s > 1 for s in w.shape[:lead])
    extra, aliases = [], {}
    if chained:
        extra = list(into) if into is not None else [lax.empty(w.shape, F32) for _ in range(4)]
        aliases = {4 + k: k for k in range(4)}
    return pl.pallas_call(
        kern, name=name, grid=(r // tm,),
        in_specs=[pl.BlockSpec((ns, tm, wd), lambda i: (0, i, 0)), spec, spec, spec] +
                 [pl.BlockSpec(memory_space=pl.ANY)] * len(extra),
        out_specs=[spec] * 4, out_shape=[jax.ShapeDtypeStruct(w.shape, F32)] * 4,
        input_output_aliases=aliases,
        compiler_params=_params("parallel"),
    )(parts, w, m, v, *extra)


def _zero_after(x, *, name):
    def kern(x_ref, o_ref):
        o_ref[...] = jnp.zeros_like(o_ref)

    return pl.pallas_call(kern, name=name, out_shape=jax.ShapeDtypeStruct((8, 128), F32),
                          in_specs=[pl.BlockSpec(memory_space=pl.ANY)])(x)[0, 0]


def _sum_slots(parts, *, name, scale_by=None):
    ns, r, wd = parts.shape

    def kern(*refs):
        p_ref, o_ref = refs[0], refs[-1]
        g = p_ref[0]
        for s in range(1, ns):
            g = g + p_ref[s]
        if scale_by is not None:
            g = g * _dsilu(refs[1][...])
        o_ref[...] = g

    args = [parts] + ([] if scale_by is None else [scale_by])
    return pl.pallas_call(kern, name=name, out_shape=jax.ShapeDtypeStruct((r, wd), F32),
                          compiler_params=pltpu.CompilerParams(vmem_limit_bytes=VMEM_LIMIT_BYTES))(*args)


def _mesh_pos():
    x, y, c = lax.axis_index("x"), lax.axis_index("y"), lax.axis_index("c")
    return x, y, c, 4 * x + 2 * y + c


def _flip(x, y, c, f):
    fx, fy, fc = (f >> 2) & 1, (f >> 1) & 1, f & 1
    px = 1 - x if fx else x
    py = 1 - y if fy else y
    pc = 1 - c if fc else c
    return (px, py, pc), 4 * px + 2 * py + pc


_HBM_SPEC = pl.BlockSpec(memory_space=pltpu.HBM)


def _exchange(arrays, *, scatter, name):
    na = len(arrays)
    if scatter:
        out_shape = [jax.ShapeDtypeStruct(a.shape, a.dtype) for a in arrays]
    else:
        out_shape = [jax.ShapeDtypeStruct((NDEV,) + a.shape, a.dtype) for a in arrays]

    out_shape.append(jax.ShapeDtypeStruct((8, 128), F32))

    def body(*refs):
        ins, outs = refs[:na], refs[na:2 * na]
        send_sems, recv_sems, local_sems = refs[2 * na + 1:]
        refs[2 * na][...] = jnp.zeros((8, 128), F32)
        x, y, c, me = _mesh_pos()
        copies = []
        for i in range(na):
            src_own = ins[i].at[me] if scatter else ins[i]
            lc = pltpu.make_async_copy(src_own, outs[i].at[me], local_sems.at[i])
            lc.start()
            copies.append(lc)
        sends = []
        for f in range(1, NDEV):
            peer, pidx = _flip(x, y, c, f)
            for i in range(na):
                k = i * (NDEV - 1) + f - 1
                src = ins[i].at[pidx] if scatter else ins[i]
                cp = pltpu.make_async_remote_copy(
                    src_ref=src, dst_ref=outs[i].at[me], send_sem=send_sems.at[k], recv_sem=recv_sems.at[k],
                    device_id=peer, device_id_type=pl.DeviceIdType.MESH)
                cp.start()
                sends.append(cp)
        for f in range(1, NDEV):
            peer, pidx = _flip(x, y, c, f)
            for i in range(na):
                k = i * (NDEV - 1) + f - 1
                src = ins[i].at[pidx] if scatter else ins[i]
                pltpu.make_async_remote_copy(
                    src_ref=src, dst_ref=outs[i].at[pidx], send_sem=send_sems.at[k], recv_sem=recv_sems.at[k],
                    device_id=peer, device_id_type=pl.DeviceIdType.MESH).wait_recv()
        for cp in sends:
            cp.wait_send()
        for lc in copies:
            lc.wait()

    res = pl.pallas_call(
        body, name=name, out_shape=out_shape, in_specs=[_HBM_SPEC] * na,
        out_specs=[_HBM_SPEC] * na + [pl.BlockSpec(memory_space=pltpu.VMEM)],
        scratch_shapes=[pltpu.SemaphoreType.DMA((na * (NDEV - 1),)), pltpu.SemaphoreType.DMA((na * (NDEV - 1),)),
                        pltpu.SemaphoreType.DMA((na,))],
        compiler_params=pltpu.CompilerParams(has_side_effects=True),
    )(*arrays)
    return res[:na], res[na][0, 0]


_SEM_SPEC = pl.BlockSpec(memory_space=pltpu.SEMAPHORE)
_DATAFLOW = pltpu.SideEffectType.DATAFLOW_SIDE_EFFECTING


def _split_copies(srcs, lands, send_sems, recv_sems, scatter, arriving):
    x, y, c, me = _mesh_pos()
    copies = []
    for i in range(len(srcs)):
        for f in range(1, NDEV):
            peer, pidx = _flip(x, y, c, f)
            k = i * (NDEV - 1) + f - 1
            copies.append(pltpu.make_async_remote_copy(
                src_ref=srcs[i].at[pidx] if scatter else srcs[i], dst_ref=lands[i].at[pidx if arriving else me],
                send_sem=send_sems.at[k], recv_sem=recv_sems.at[k], device_id=peer,
                device_id_type=pl.DeviceIdType.MESH))
    return copies


def _exchange_start(srcs, lands, *, scatter, name):
    na = len(srcs)
    nsem = na * (NDEV - 1)

    def body(*refs):
        ins_src, ins_land = refs[:na], refs[na:2 * na]
        send_sems, recv_sems = refs[2 * na], refs[2 * na + 1]
        token = refs[-1]
        for cp in _split_copies(ins_src, ins_land, send_sems, recv_sems, scatter, False):
            cp.start()
        token[...] = jnp.zeros_like(token)

    thru = [pltpu.HBM(a.shape, a.dtype) for a in list(srcs) + list(lands)]
    res = pl.pallas_call(
        body, name=name,
        out_shape=(pltpu.SemaphoreType.DMA((nsem,)), pltpu.SemaphoreType.DMA((nsem,)), *thru,
                   jax.ShapeDtypeStruct((8, 128), F32)),
        in_specs=[_HBM_SPEC] * (2 * na),
        out_specs=(_SEM_SPEC, _SEM_SPEC, *([_HBM_SPEC] * (2 * na)), pl.BlockSpec(memory_space=pltpu.VMEM)),
        input_output_aliases={i: 2 + i for i in range(2 * na)},
        compiler_params=pltpu.CompilerParams(has_side_effects=_DATAFLOW),
    )(*[pltpu.with_memory_space_constraint(a, pltpu.HBM) for a in list(srcs) + list(lands)])
    send_sems, recv_sems = res[0], res[1]
    return send_sems, recv_sems, res[2:2 + na], res[2 + na:2 + 2 * na], res[-1][0, 0]


def _exchange_wait(send_sems, recv_sems, srcs, lands, after, *, scatter, name):
    na = len(srcs)

    def body(*refs):
        ins_src, ins_land = refs[:na], refs[na:2 * na]
        s_sems, r_sems = refs[2 * na], refs[2 * na + 1]
        for cp in _split_copies(ins_src, ins_land, s_sems, r_sems, scatter, False):
            cp.wait_send()
        for cp in _split_copies(ins_src, ins_land, s_sems, r_sems, scatter, True):
            cp.wait_recv()

    thru = [pltpu.HBM(a.shape, a.dtype) for a in list(srcs) + list(lands)]
    res = pl.pallas_call(
        body, name=name, out_shape=tuple(thru),
        in_specs=[_HBM_SPEC] * (2 * na) + [_SEM_SPEC, _SEM_SPEC, pl.BlockSpec(memory_space=pl.ANY)],
        out_specs=tuple([_HBM_SPEC] * (2 * na)),
        input_output_aliases={i: i for i in range(2 * na)},
        compiler_params=pltpu.CompilerParams(has_side_effects=_DATAFLOW),
    )(*srcs, *lands, send_sems, recv_sems, after)
    return res[na:]


def _landing(block, me):
    buf = lax.empty((NDEV,) + block.shape, block.dtype)
    return lax.dynamic_update_slice_in_dim(buf, block[None], me, axis=0)


def _seg_kw(nseg, n_ctx, tm):
    return dict(nseg=nseg, seg_blocks=(n_ctx // tm if nseg == 2 else 0))


def _ffn_fwd(tag, h, gpre, gpost, shift, scale, gate, w, *, nseg, n_ctx, tm):
    n = h.shape[0]
    kw = _seg_kw(nseg, n_ctx, tm)
    (u,) = _rowwise(tag + "_pre", _pre_fwd_fn, n, [h], [("full", gpre), ("seg", shift), ("seg", scale)],
                    [(D_MODEL, BF16)], tm=tm, **kw)
    if "early" in w:
        w.update(w.pop("early")(u))
    s, a, b = _mm_glu(u, w["win"], name=tag + "_glu")
    if "late" in w:
        w.update(w.pop("late")(s))
    y, ho = _mm_rows(s, w["wout"], functools.partial(_out_post_fn, 0.5), [h], [("full", gpost), ("seg", gate)],
                     [(D_MODEL, F32), (D_MODEL, F32)], name=tag + "_out", tk=FFN_DIM, n_ctx=n_ctx)
    return ho, dict(h=h, u=u, s=s, a=a, b=b, y=y)


def _ffn_bwd(tag, dho, sv, gpre, gpost, scale, gate, w, put, *, nseg, n_ctx, tm):
    n = dho.shape[0]
    kw = _seg_kw(nseg, n_ctx, tm)
    dy, dgate, dgpost = _rowwise(tag + "_postb", functools.partial(_post_bwd_fn, 0.5), n, [dho, sv["y"]],
                                 [("full", gpost), ("seg", gate)], [(D_MODEL, BF16)], [D_MODEL, D_MODEL], tm=tm, **kw)
    tok = put("w_out", _mm_tn(sv["s"], dy, name=tag + "_dwout", tm=1408, tn=1024, col_blocks=1))
    ds = _mm(dy, w["wout"], out_dtype=F32, name=tag + "_ds", tn=1408, rhs_t=True)
    (dp,) = _rowwise(tag + "_glub", _glu_bwd_fn, n, [ds, sv["a"], sv["b"]], [], [(2 * FFN_DIM, BF16)], tm=min(tm, 128))
    tok2 = put("w_in", _mm_tn(sv["u"], dp, name=tag + "_dwin", tn=1408, col_blocks=NDEV))
    for t in (tok, tok2):
        if t is not None:
            gpre = gpre + t
    dh, dshift, dscale, dgpre = _mm_rows(dp, w["win"], _pre_bwd_fn, [sv["h"], dho], [("full", gpre), ("seg", scale)],
                                         [(D_MODEL, F32)], [D_MODEL, D_MODEL, D_MODEL], name=tag + "_du",
                                         rhs_t=True, n_ctx=n_ctx)
    return dh, None, dict(shift=dshift, scale=dscale, gate=dgate, gpre=dgpre, gpost=dgpost)


def _local_step(x, ctx, target, mods, norm_g, get_w, small, put_grad):
    t_len, n_ctx = x.shape[0], ctx.shape[0]
    n0 = t_len + n_ctx
    tm0 = _pick(n_ctx, 256, 8)
    tm1 = _pick(t_len, 256, 8)
    ncc = n_ctx // CHUNK
    g = {}

    def modrow(i, k, nseg):
        mc, mx = mods[i]
        if nseg == 2:
            return jnp.stack([mc[k], mx[k]])[:, None, :]
        return mx[k][None, None, :]

    pending = [None]

    def gvec(i, k):
        v = norm_g[i, k][None, :]
        if pending[0] is not None:
            v = v + pending[0]
            pending[0] = None
        return v

    xc = jnp.concatenate([ctx, x], axis=0)
    L0 = dict(nseg=2, n_ctx=n_ctx, tm=tm0)
    wts = dict(get_w("ffn00", xc))
    h1, sv_f01 = _ffn_fwd("l0f1", xc, gvec(0, 0), gvec(0, 1), modrow(0, 0, 2), modrow(0, 1, 2), modrow(0, 2, 2),
                          wts["ffn00"], **L0)
    kw0 = _seg_kw(2, n_ctx, tm0)
    (um0,) = _rowwise("l0m_pre", _pre_fwd_fn, n0, [h1], [("full", gvec(0, 2)), ("seg", modrow(0, 3, 2)),
                                                         ("seg", modrow(0, 4, 2))], [(D_MODEL, BF16)], tm=tm0, **kw0)
    wts.update(get_w("ssd", um0))
    z = _mm(um0, wts["ssd_win"], out_dtype=F32, name="ssd_z", n=SSD_INNER)
    xbc_pre = _mm(um0, wts["ssd_win"], out_dtype=F32, name="ssd_xbc", n=SSD_CONV_DIM, b_off=(0, SSD_INNER // 1024))
    dtr = _mm(um0, wts["ssd_wdt"], out_dtype=F32, name="ssd_dt")
    cpre, xbc = _conv_fwd(xbc_pre, small["conv_w8"], small["conv_b"], n_ctx=n_ctx, name="ssd_conv")
    nh = SSD_HEADS
    dt_dir = [dtr[:, :nh], dtr[:, nh:2 * nh]]
    dtT_dir = [d.T for d in dt_dir]
    bias_r = [small["dt_bias"][d][None, :] for d in range(2)]
    bias_c = [small["dt_bias"][d][:, None] for d in range(2)]
    alog_r = [small["a_log"][d][None, :] for d in range(2)]
    alog_c = [small["a_log"][d][:, None] for d in range(2)]
    ys, hss = [], []
    for d in range(2):
        yd, hsd = _ssd_scan_fwd(xbc, dt_dir[d], dtT_dir[d], bias_r[d], bias_c[d], alog_r[d], alog_c[d],
                                rev=(d == 1), n_ctx_chunks=ncc, name=f"ssd_scan{d}")
        ys.append(yd)
        hss.append(hsd)
    dvec = jnp.repeat(small["ssd_d"], SSD_HEAD_DIM)[None, :]
    ngv = small["ssd_norm_g"][None, :]
    gate_rows = [ys[0], ys[1], (xbc, SSD_INNER, 0, 0), z]
    lat = lambda r: (r[0], r[1], r[2], ncc) if isinstance(r, tuple) else (r, r.shape[1], 0, ncc)
    (yn,) = _rowwise("ssd_gate", _ssdgate_fwd_fn, t_len, [lat(r) for r in gate_rows],
                     [("full", dvec), ("full", ngv)], [(SSD_INNER, BF16)], tm=CHUNK)
    h1x = h1[n_ctx:]
    L1 = dict(nseg=1, n_ctx=0, tm=tm1)
    if "late" in wts:
        wts.update(wts.pop("late")(yn))
    yo0, h2 = _mm_rows(yn, wts["ssd_wout"], functools.partial(_out_post_fn, 1.0), [h1x],
                       [("full", gvec(0, 3)), ("seg", modrow(0, 5, 1))], [(D_MODEL, F32), (D_MODEL, F32)],
                       name="ssd_out", tk=SSD_INNER)
    wts.update(get_w("ffn01", h2))
    h3, sv_f02 = _ffn_fwd("l0f2", h2, gvec(0, 4), gvec(0, 5), modrow(0, 6, 1), modrow(0, 7, 1), modrow(0, 8, 1),
                          wts["ffn01"], **L1)

    wts.update(get_w("ffn10", h3))
    h4, sv_f11 = _ffn_fwd("l1f1", h3, gvec(1, 0), gvec(1, 1), modrow(1, 0, 1), modrow(1, 1, 1), modrow(1, 2, 1),
                          wts["ffn10"], **L1)
    (um1,) = _rowwise("l1m_pre", _pre_fwd_fn, t_len, [h4], [("full", gvec(1, 2)), ("seg", modrow(1, 3, 1)),
                                                            ("seg", modrow(1, 4, 1))], [(D_MODEL, BF16)], tm=tm1)
    wts.update(get_w("gm", um1))
    p1 = _mm(um1, wts["gm_win"], out_dtype=F32, name="gm_in")
    vg = small["gm_v_g"][None, :]
    vb = small["gm_v_b"][None, :]
    gu, gvn = _rowwise("gm_act", _gm_act_fwd_fn, t_len, [p1], [("full", vg), ("full", vb)],
                       [(GM_INNER, F32), (GM_INNER, BF16)], tm=128)
    ws_bf = small["gm_w_s"].astype(BF16)
    wst_bf = jnp.swapaxes(small["gm_w_s"], 1, 2).astype(BF16)
    bst = small["gm_b_s"].T
    tgm = _gm_spatial_fwd(gu, gvn, ws_bf, bst, name="gm_spatial")
    yo1, h5 = _mm_rows(tgm, wts["gm_wout"], functools.partial(_out_post_fn, 1.0), [h4],
                       [("full", gvec(1, 3)), ("seg", modrow(1, 5, 1))], [(D_MODEL, F32), (D_MODEL, F32)],
                       name="gm_out", tk=GM_INNER)
    wts.update(get_w("ffn11", h5))
    h6, sv_f12 = _ffn_fwd("l1f2", h5, gvec(1, 4), gvec(1, 5), modrow(1, 6, 1), modrow(1, 7, 1), modrow(1, 8, 1),
                          wts["ffn11"], **L1)

    dh, loss_parts = _rowwise("loss", _loss_fn, t_len, [h6, target], [], [(D_MODEL, F32)], [D_MODEL], tm=tm1)

    zero = jnp.zeros((D_MODEL,), F32)
    dmx = [[zero] * N_MOD for _ in range(2)]
    dmc = [[zero] * N_MOD for _ in range(2)]
    dng = [[zero] * 6 for _ in range(2)]

    def put_mod(i, k, acc):
        if acc.shape[0] == 2:
            dmc[i][k] = dmc[i][k] + acc[0, 0]
            dmx[i][k] = dmx[i][k] + acc[1, 0]
        else:
            dmx[i][k] = dmx[i][k] + acc[0, 0]

    def put_g(i, k, acc):
        dng[i][k] = dng[i][k] + jnp.sum(acc[:, 0], axis=0)

    def ffn_back(tag, i, j, dho, sv, w, lay):
        nseg = lay["nseg"]
        base = 0 if j == 0 else 6
        gi = 0 if j == 0 else 4
        dh_in, pending[0], s = _ffn_bwd(tag, dho, sv, gvec(i, gi), gvec(i, gi + 1), modrow(i, base + 1, nseg),
                                        modrow(i, base + 2, nseg), w, functools.partial(put_grad, f"ffn{i}{j}"), **lay)
        put_mod(i, base, s["shift"])
        put_mod(i, base + 1, s["scale"])
        put_mod(i, base + 2, s["gate"])
        put_g(i, gi, s["gpre"])
        put_g(i, gi + 1, s["gpost"])
        return dh_in

    dh = ffn_back("l1f2", 1, 1, dh, sv_f12, wts["ffn11"], L1)
    dyo, dgate, dgp = _rowwise("l1m_postb", functools.partial(_post_bwd_fn, 1.0), t_len, [dh, yo1],
                               [("full", gvec(1, 3)), ("seg", modrow(1, 5, 1))], [(D_MODEL, BF16)],
                               [D_MODEL, D_MODEL], tm=tm1)
    put_mod(1, 5, dgate)
    put_g(1, 3, dgp)
    put_grad("gm", "w_out", _mm_tn(tgm, dyo, name="gm_dwout", tn=1024, col_blocks=1))
    dtg = _mm(dyo, wts["gm_wout"], out_dtype=F32, name="gm_dt", rhs_t=True)
    dgu, dgvn, dws, dbst = _gm_spatial_bwd(dtg, gu, gvn, ws_bf, wst_bf, bst, name="gm_spatialb")
    g["gm_w_s"] = dws
    g["gm_b_s"] = dbst.T
    dp1, dvg, dvb = _rowwise("gm_actb", _gm_act_bwd_fn, t_len, [p1, dgu, dgvn], [("full", vg)],
                             [(2 * GM_INNER, BF16)], [GM_INNER, GM_INNER], tm=128)
    g["gm_v_g"] = dvg[0, 0]
    g["gm_v_b"] = dvb[0, 0]
    pending[0] = put_grad("gm", "w_in", _mm_tn(um1, dp1, name="gm_dwin", tm=1024, col_blocks=NDEV))
    dh, dsh, dsc, dgp = _mm_rows(dp1, wts["gm_win"], _pre_bwd_fn, [h4, dh],
                                 [("full", gvec(1, 2)), ("seg", modrow(1, 4, 1))], [(D_MODEL, F32)],
                                 [D_MODEL, D_MODEL, D_MODEL], name="gm_dum", tk=1024, rhs_t=True)
    put_mod(1, 3, dsh)
    put_mod(1, 4, dsc)
    put_g(1, 2, dgp)
    dh = ffn_back("l1f1", 1, 0, dh, sv_f11, wts["ffn10"], L1)

    dh = ffn_back("l0f2", 0, 1, dh, sv_f02, wts["ffn01"], L1)
    dyo, dgate, dgp = _rowwise("l0m_postb", functools.partial(_post_bwd_fn, 1.0), t_len, [dh, yo0],
                               [("full", gvec(0, 3)), ("seg", modrow(0, 5, 1))], [(D_MODEL, BF16)],
                               [D_MODEL, D_MODEL], tm=tm1)
    put_mod(0, 5, dgate)
    put_g(0, 3, dgp)
    tok = put_grad("ssd", "w_out", _mm_tn(yn, dyo, name="ssd_dwout", tn=1024, col_blocks=1))
    dyn = _mm(dyo, wts["ssd_wout"], out_dtype=F32, name="ssd_dyn", rhs_t=True)
    dy_ssd, dz, dngv, ddv = _rowwise("ssd_gateb", _ssdgate_bwd_fn, n0, [(dyn, SSD_INNER, 0, -ncc)] + gate_rows,
                                     [("full", dvec), ("full", ngv if tok is None else ngv + tok)],
                                     [(SSD_INNER, F32), (SSD_INNER, BF16)],
                                     [SSD_INNER, SSD_INNER], tm=128)
    g["ssd_norm_g"] = dngv[0, 0]
    g["ssd_D"] = jnp.sum(ddv[0, 0].reshape(SSD_HEADS, SSD_HEAD_DIM), axis=1)
    dxbcs, ddts, dalogs, dbiases = [], [], [], []
    for d in range(2):
        dxd, ddtd, dal, dbi = _ssd_scan_bwd(dy_ssd, xbc, hss[d], dt_dir[d], dtT_dir[d], bias_r[d], bias_c[d],
                                            alog_r[d], alog_c[d], dvec, rev=(d == 1), n_ctx_chunks=ncc,
                                            direct=(d == 0), name=f"ssd_scanb{d}")
        dxbcs.append(dxd)
        ddts.append(ddtd)
        dalogs.append(dal[0])
        dbiases.append(dbi[0])
    g["ssd_A_log"] = jnp.stack(dalogs)
    g["ssd_dt_bias"] = jnp.stack(dbiases)
    dxbc_pre, dcw8, dcb = _conv_bwd(dxbcs[0], dxbcs[1], cpre, xbc_pre, small["conv_w8"], n_ctx=n_ctx, name="ssd_convb")
    g["ssd_conv_w"] = dcw8[:SSD_CONV]
    g["ssd_conv_b"] = dcb[0]
    ddt_bf = jnp.concatenate([ddts[0], ddts[1], jnp.zeros((n0, 128 - 2 * nh), F32)], axis=1).astype(BF16)
    dw_ssd_in = jnp.concatenate([
        _mm_tn(um0, dz, name="ssd_dwz", tm=1024),
        _mm_tn(um0, dxbc_pre, name="ssd_dwxbc", tm=1024),
        _mm_tn(um0, ddt_bf, name="ssd_dwdt", tm=1024)[:, :2 * nh]], axis=1)
    pending[0] = put_grad("ssd", "w_in", dw_ssd_in)
    win_ssd = wts["ssd_win"]
    dum0 = _mm(dz, win_ssd, out_dtype=F32, name="ssd_dum_z", tk=1024, rhs_t=True, n=D_MODEL)
    dum0 = _mm(dxbc_pre, win_ssd, out_dtype=F32, name="ssd_dum_x", tk=1024, rhs_t=True, n=D_MODEL,
               b_off=(0, SSD_INNER // 1024), add=dum0)
    dum0 = _mm(ddt_bf, wts["ssd_wdt"], out_dtype=F32, name="ssd_dum_dt", rhs_t=True, add=dum0)
    dh0, dsh, dsc, dgp = _rowwise("l0m_preb", _pre_bwd_fn, n0, [dum0, h1, (dh, D_MODEL, 0, -(n_ctx // tm0))],
                                  [("full", gvec(0, 2)), ("seg", modrow(0, 4, 2))], [(D_MODEL, F32)],
                                  [D_MODEL, D_MODEL, D_MODEL], tm=tm0, **kw0)
    put_mod(0, 3, dsh)
    put_mod(0, 4, dsc)
    put_g(0, 2, dgp)
    dh0 = ffn_back("l0f1", 0, 0, dh0, sv_f01, wts["ffn00"], L0)
    grad_x = dh0[n_ctx:]
    g["norm_g"] = jnp.stack([jnp.stack(r) for r in dng])
    g["dmx"] = jnp.stack([jnp.concatenate(r) for r in dmx])
    g["dmc"] = jnp.stack([jnp.concatenate(r) for r in dmc])
    return loss_parts[0], grad_x, g


GROUPS = ("ffn00", "ssd", "ffn01", "ffn10", "gm", "ffn11")


def _mats_in(group, win_l):
    k, nloc = win_l.shape[1], win_l.shape[2]
    win = jnp.transpose(win_l, (1, 0, 2)).reshape(k, NDEV * nloc)
    if group.startswith("ffn"):
        return dict(win=win)
    if group == "gm":
        return dict(gm_win=win)
    assert group == "ssd"
    c1 = SSD_INNER + SSD_CONV_DIM
    return dict(ssd_win=win, ssd_wdt=jnp.pad(win[:, c1:], ((0, 0), (0, 128 - 2 * SSD_HEADS))))


def _mats_out(group, wout_l):
    pre = "" if group.startswith("ffn") else group + "_"
    return {pre + "wout": wout_l.reshape(-1, wout_l.shape[2])}


def _group_mats(group, lands):
    m = {**_mats_in(group, lands[0]), **_mats_out(group, lands[1])}
    return {group: m} if group.startswith("ffn") else m


def _grad_blocks(which, grad):
    if grad.ndim == 3:
        return grad if which == "w_in" else grad.reshape(NDEV, grad.shape[1] // NDEV, grad.shape[2])
    if which == "w_in":
        k, n = grad.shape
        return jnp.transpose(grad.reshape(k, NDEV, n // NDEV), (1, 0, 2)).astype(BF16)
    return grad.reshape(NDEV, grad.shape[0] // NDEV, grad.shape[1]).astype(BF16)


def kernel(x, c, ctx, c_ctx, ada_w, ada_b, norm_g, ffn_w_in, ffn_w_out, ssd_w_in, ssd_conv_w, ssd_conv_b, ssd_dt_bias, ssd_A_log, ssd_D, ssd_norm_g, ssd_w_out, gm_w_in, gm_v_g, gm_v_b, gm_w_s, gm_b_s, gm_w_out, loss_target, m_c_ctx, m_ada_w, m_ada_b, m_norm_g, m_ffn_w_in, m_ffn_w_out, m_ssd_w_in, m_ssd_conv_w, m_ssd_conv_b, m_ssd_dt_bias, m_ssd_A_log, m_ssd_D, m_ssd_norm_g, m_ssd_w_out, m_gm_w_in, m_gm_v_g, m_gm_v_b, m_gm_w_s, m_gm_b_s, m_gm_w_out, v_c_ctx, v_ada_w, v_ada_b, v_norm_g, v_ffn_w_in, v_ffn_w_out, v_ssd_w_in, v_ssd_conv_w, v_ssd_conv_b, v_ssd_dt_bias, v_ssd_A_log, v_ssd_D, v_ssd_norm_g, v_ssd_w_out, v_gm_w_in, v_gm_v_g, v_gm_v_b, v_gm_w_s, v_gm_b_s, v_gm_w_out):
    me = 4 * lax.axis_index("x") + 2 * lax.axis_index("y") + lax.axis_index("c")
    d = D_MODEL
    ncol = N_MOD * d // NDEV

    small_pack = jnp.concatenate([c.reshape(-1), norm_g.reshape(-1), ssd_conv_w.reshape(-1),
                                  gm_v_g.reshape(-1), gm_v_b.reshape(-1)])[None, :]
    (sp,), _ = _exchange([small_pack], scatter=False, name="gather_small")
    sp = sp[:, 0]
    o = 0
    c_all = sp[:, o:o + d]; o += d
    ng_all = sp[:, o:o + 2 * 6 * 128].reshape(NDEV, 2, 6, 128); o += 2 * 6 * 128
    cw_all = sp[:, o:o + SSD_CONV * 512].reshape(NDEV, SSD_CONV, 512); o += SSD_CONV * 512
    vg_all = sp[:, o:o + 256]; o += 256
    vb_all = sp[:, o:o + 256]; o += 256
    norm_g_full = jnp.transpose(ng_all, (1, 2, 0, 3)).reshape(2, 6, d)
    conv_w_full = jnp.transpose(cw_all, (1, 0, 2)).reshape(SSD_CONV, SSD_CONV_DIM)
    gm_v_g_full = vg_all.reshape(-1)
    gm_v_b_full = vb_all.reshape(-1)

    c16 = jnp.concatenate([c_all, jnp.broadcast_to(c_ctx[None, :], (NDEV, d))], axis=0)
    ada_b_loc = lax.dynamic_slice_in_dim(ada_b, me * ncol, ncol, axis=1)
    mods_loc = jnp.stack([_mm_f32(c16, ada_w[i], name=f"ada_mod{i}", silu_a=True, bias=ada_b_loc[i][None, :])
                          for i in range(2)])
    (mods_all,), mods_done = _exchange([mods_loc], scatter=False, name="gather_mods")

    shard = {"ssd": (ssd_w_in[0], ssd_w_out[0]), "gm": (gm_w_in[0], gm_w_out[0])}
    for i in range(2):
        for j in range(2):
            shard[f"ffn{i}{j}"] = (ffn_w_in[i, j], ffn_w_out[i, j])
    apart = GROUPS[:2]
    units = []
    for grp in GROUPS:
        units += [(grp + "_in", grp, (0,)), (grp + "_out", grp, (1,))] if grp in apart else [(grp, grp, (0, 1))]
    gathers = {}
    started = mods_done
    for unit, grp, idx in units:
        srcs = [(shard[grp][k] + started).astype(BF16) for k in idx]
        st = _exchange_start(srcs, [_landing(s, me) for s in srcs], scatter=False, name="gather_start_" + unit)
        gathers[unit] = st[:4]
        started = st[4]

    def fetch(unit, after):
        return _exchange_wait(*gathers[unit], after, scatter=False, name="gather_wait_" + unit)

    def get_w(grp, after):
        if grp not in apart:
            return _group_mats(grp, fetch(grp, after))
        early = lambda later: _mats_in(grp, fetch(grp + "_in", later)[0])
        late = lambda later: _mats_out(grp, fetch(grp + "_out", later)[0])
        if grp.startswith("ffn"):
            return {grp: dict(early=early, late=late)}
        return dict(early(after), late=late)

    scatters = {}
    held = {}

    def put_grad(grp, which, grad):
        if grp in apart:
            unit, blocks = grp + "_" + which[2:], [_grad_blocks(which, grad)]
        else:
            held[grp, which] = _grad_blocks(which, grad)
            if (grp, "w_in") not in held or (grp, "w_out") not in held:
                return None
            unit, blocks = grp, [held[grp, "w_in"], held[grp, "w_out"]]
        lands = [_landing(lax.dynamic_index_in_dim(b, me, axis=0, keepdims=False), me) for b in blocks]
        st = _exchange_start(blocks, lands, scatter=True, name="scatter_start_" + unit)
        scatters[unit] = st[:4]
        return st[4]

    mods_rows = jnp.transpose(mods_all, (1, 2, 0, 3)).reshape(2, 2 * NDEV, N_MOD * d) + started
    mx = lax.dynamic_index_in_dim(mods_rows, me, axis=1, keepdims=False).reshape(2, N_MOD, d)
    mc = mods_rows[:, NDEV].reshape(2, N_MOD, d)
    mods = [(mc[i], mx[i]) for i in range(2)]

    small = dict(conv_w8=jnp.pad(conv_w_full, ((0, 8 - SSD_CONV), (0, 0))), conv_b=ssd_conv_b, dt_bias=ssd_dt_bias[0],
                 a_log=ssd_A_log[0], ssd_d=ssd_D[0], ssd_norm_g=ssd_norm_g[0], gm_v_g=gm_v_g_full,
                 gm_v_b=gm_v_b_full, gm_w_s=gm_w_s[0], gm_b_s=gm_b_s[0])
    loss_parts, grad_x, g = _local_step(x[0], ctx[0], loss_target[0], mods, norm_g_full, get_w, small, put_grad)
    loss = lax.psum(0.5 / d * jnp.sum(loss_parts), ("x", "y", "c"))

    whole = {"ffn_w_in": (ffn_w_in, m_ffn_w_in, v_ffn_w_in), "ffn_w_out": (ffn_w_out, m_ffn_w_out, v_ffn_w_out),
             "ssd_w_in": (ssd_w_in, m_ssd_w_in, v_ssd_w_in), "ssd_w_out": (ssd_w_out, m_ssd_w_out, v_ssd_w_out),
             "gm_w_in": (gm_w_in, m_gm_w_in, v_gm_w_in), "gm_w_out": (gm_w_out, m_gm_w_out, v_gm_w_out)}
    res = {}

    def update_units(some, after):
        for unit, grp, idx in some:
            parts = _exchange_wait(*scatters[unit], after, scatter=True, name="scatter_wait_" + unit)
            for k, p in zip(idx, parts):
                which = ("in", "out")[k]
                nm = ("ffn" if grp.startswith("ffn") else grp) + "_w_" + which
                sel = (int(grp[3]), int(grp[4])) if grp.startswith("ffn") else (0,)
                res[nm] = _adamw(p, *whole[nm], name=f"adamw_{grp}_{which}", sel=sel, into=res.get(nm))
                after = res[nm][0]
        return after

    by_send = list(reversed(units))
    early_done = update_units(by_send[:4], grad_x)

    sg_names = ["dmx", "dmc", "norm_g", "ssd_conv_w", "ssd_conv_b", "ssd_dt_bias", "ssd_A_log", "ssd_D", "ssd_norm_g",
                "gm_v_g", "gm_v_b", "gm_w_s", "gm_b_s"]
    sg_shapes = [g[n].shape for n in sg_names]
    flat = jnp.concatenate([g[n].reshape(-1) for n in sg_names])
    npack = flat.shape[0]
    pad = (-npack) % 1024
    flat = jnp.pad(flat, (0, pad)).reshape(-1, 128)
    flat = flat + _zero_after(early_done, name="after_early_updates")
    (sg_all,), _ = _exchange([flat], scatter=False, name="gather_small_grads")
    update_units(by_send[4:], sg_all)
    sg_sum = _sum_slots(sg_all, name="sum_small_grads").reshape(-1)[:npack]
    sums = {}
    o = 0
    for n, shp in zip(sg_names, sg_shapes):
        sz = math.prod(shp)
        sums[n] = sg_sum[o:o + sz].reshape(shp)
        o += sz
    per_dev = sg_all.reshape(NDEV, -1)
    dmx_all = per_dev[:, :2 * N_MOD * d].reshape(NDEV, 2, N_MOD * d)
    dmc_all = per_dev[:, 2 * N_MOD * d:4 * N_MOD * d].reshape(NDEV, 2, N_MOD * d)

    (s16,) = _rowwise("ada_silu", lambda cc: ((_silu(cc),), ()), 2 * NDEV, [c16], [], [(d, F32)], tm=2 * NDEV)
    s16_t = s16.T
    g_ada_w, dcc_parts = [], []
    for i in range(2):
        rhs = jnp.concatenate([lax.dynamic_slice_in_dim(dmx_all[:, i], me * ncol, ncol, axis=1),
                               lax.dynamic_slice_in_dim(dmc_all[:, i], me * ncol, ncol, axis=1)], axis=0)
        g_ada_w.append(_mm_f32(s16_t, rhs, name=f"ada_dw{i}"))
        dmc_loc = lax.dynamic_slice_in_dim(sums["dmc"][i], me * ncol, ncol, axis=0)
        rhs_c = jnp.zeros((ncol, 128), F32).at[:, 0].set(dmc_loc)
        dcc_parts.append(_mm_f32(ada_w[i], rhs_c, name=f"ada_dcc{i}")[:, 0])
    g_ada_w = jnp.stack(g_ada_w)
    dcc_part = (dcc_parts[0] + dcc_parts[1]).reshape(8, 128)
    (dcc_all,), _ = _exchange([dcc_part], scatter=False, name="gather_dcc")
    g_c_ctx = _sum_slots(dcc_all, name="sum_dcc", scale_by=c_ctx.reshape(8, 128)).reshape(d)
    g_ada_b = sums["dmx"] + sums["dmc"]

    outs = _adamw(g_ada_w.reshape(1, -1, ncol), ada_w.reshape(-1, ncol), m_ada_w.reshape(-1, ncol),
                  v_ada_w.reshape(-1, ncol), name="adamw_ada_w")
    res["ada_w"] = [o_.reshape(ada_w.shape) for o_ in outs]

    loc = lambda a, ax, n: lax.dynamic_slice_in_dim(a, me * n, n, axis=ax)
    small_g = dict(c_ctx=g_c_ctx, ada_b=g_ada_b, norm_g=loc(sums["norm_g"], 2, 128),
                   ssd_conv_w=loc(sums["ssd_conv_w"], 1, 512)[None], ssd_conv_b=sums["ssd_conv_b"][None],
                   ssd_dt_bias=sums["ssd_dt_bias"][None], ssd_A_log=sums["ssd_A_log"][None], ssd_D=sums["ssd_D"][None],
                   ssd_norm_g=sums["ssd_norm_g"][None], gm_v_g=loc(sums["gm_v_g"], 0, 256)[None],
                   gm_v_b=loc(sums["gm_v_b"], 0, 256)[None], gm_w_s=sums["gm_w_s"][None], gm_b_s=sums["gm_b_s"][None])
    small_w = dict(c_ctx=(c_ctx, m_c_ctx, v_c_ctx), ada_b=(ada_b, m_ada_b, v_ada_b), norm_g=(norm_g, m_norm_g, v_norm_g),
                   ssd_conv_w=(ssd_conv_w, m_ssd_conv_w, v_ssd_conv_w), ssd_conv_b=(ssd_conv_b, m_ssd_conv_b, v_ssd_conv_b),
                   ssd_dt_bias=(ssd_dt_bias, m_ssd_dt_bias, v_ssd_dt_bias), ssd_A_log=(ssd_A_log, m_ssd_A_log, v_ssd_A_log),
                   ssd_D=(ssd_D, m_ssd_D, v_ssd_D), ssd_norm_g=(ssd_norm_g, m_ssd_norm_g, v_ssd_norm_g),
                   gm_v_g=(gm_v_g, m_gm_v_g, v_gm_v_g), gm_v_b=(gm_v_b, m_gm_v_b, v_gm_v_b),
                   gm_w_s=(gm_w_s, m_gm_w_s, v_gm_w_s), gm_b_s=(gm_b_s, m_gm_b_s, v_gm_b_s))
    sn = list(small_w)

    def pack(arrs):
        f = jnp.concatenate([a.reshape(-1) for a in arrs])
        return jnp.pad(f, (0, (-f.shape[0]) % 1024)).reshape(-1, 128)

    pg = pack([small_g[n].reshape(small_w[n][0].shape) for n in sn])
    outs = _adamw(pg[None], pack([small_w[n][0] for n in sn]), pack([small_w[n][1] for n in sn]),
                  pack([small_w[n][2] for n in sn]), name="adamw_small")
    flat_outs = [o_.reshape(-1) for o_ in outs]
    o = 0
    for n in sn:
        shp = small_w[n][0].shape
        sz = math.prod(shp)
        res[n] = [fo[o:o + sz].reshape(shp) for fo in flat_outs]
        o += sz

    order = ["c_ctx", "ada_w", "ada_b", "norm_g", "ffn_w_in", "ffn_w_out", "ssd_w_in", "ssd_conv_w", "ssd_conv_b",
             "ssd_dt_bias", "ssd_A_log", "ssd_D", "ssd_norm_g", "ssd_w_out", "gm_w_in", "gm_v_g", "gm_v_b", "gm_w_s",
             "gm_b_s", "gm_w_out"]
    result = [loss, grad_x[None]]
    for k in range(4):
        result += [res[n][k] for n in order]
    return tuple(result)
```

```python
import functools
import math

import jax
import jax.numpy as jnp
from jax import lax
from jax.experimental import pallas as pl
from jax.experimental.pallas import tpu as pltpu

F32 = jnp.float32
BF16 = jnp.bfloat16

NDEV = 8
D_MODEL = 1024
FFN_DIM = 2816
N_MOD = 9
EPS = 1e-6
SSD_INNER = 2048
SSD_HEADS = 32
SSD_HEAD_DIM = 64
SSD_GROUPS = 8
SSD_HPG = 4
SSD_STATE = 128
SSD_CONV = 5
SSD_CONV_DIM = 4096
CHUNK = 128
GM_INNER = 2048
GM_GROUPS = 8
GM_GROUP_DIM = 256
ADAM_LR = 0.001
ADAM_B1 = 0.9
ADAM_B2 = 0.999
ADAM_EPS = 1e-08
ADAM_WD = 0.01
ADAM_STEP = 10
NEG_BIG = -1e30
VMEM_LIMIT_BYTES = 56 * 1024 * 1024
HI = lax.Precision.HIGHEST


def _params(*sem):
    return pltpu.CompilerParams(dimension_semantics=sem, vmem_limit_bytes=VMEM_LIMIT_BYTES)


def _pick(n, target, mult=16):
    if n <= target:
        return n
    for t in range(target - target % mult, 0, -mult):
        if n % t == 0:
            return t
    raise ValueError((n, target, mult))


def _sig(x):
    return 0.5 * jnp.tanh(0.5 * x) + 0.5


def _silu(x):
    return x * _sig(x)


def _dsilu(x):
    s = _sig(x)
    return s * (1.0 + x * (1.0 - s))


_GELU_C = math.sqrt(2.0 / math.pi)


def _gelu(x):
    return 0.5 * x * (1.0 + jnp.tanh(_GELU_C * (x + 0.044715 * x * x * x)))


def _dgelu(x):
    t = jnp.tanh(_GELU_C * (x + 0.044715 * x * x * x))
    return 0.5 * (1.0 + t) + 0.5 * x * (1.0 - t * t) * _GELU_C * (1.0 + 3.0 * 0.044715 * x * x)


def _softplus(x):
    return jnp.maximum(x, 0.0) + jnp.log1p(jnp.exp(-jnp.abs(x)))


def _sum0(v):
    return jnp.sum(v, axis=0, keepdims=True)


def _rms(h):
    r = lax.rsqrt(jnp.mean(h * h, axis=-1, keepdims=True) + EPS)
    return h * r, r


def _dot(a, b, dims=((1,), (0,)), precision=None):
    return lax.dot_general(a, b, (dims, ((), ())), preferred_element_type=F32, precision=precision)


_NT = ((1,), (1,))
_TN = ((0,), (0,))


def _rowwise(name, fn, n_rows, rows, consts, outs, accs=(), *, tm, nseg=1, seg_blocks=0):
    assert n_rows % tm == 0
    if nseg == 2:
        assert seg_blocks > 0
        seg = lambda i: jnp.where(i < seg_blocks, 0, 1)
    else:
        seg = lambda i: 0
    in_specs, args, lacking = [], [], []
    for r in rows:
        arr, width, cb, off = r if isinstance(r, tuple) else (r, r.shape[1], 0, 0)
        in_specs.append(pl.BlockSpec((tm, width), lambda i, cb=cb, off=off: (jnp.maximum(i + off, 0), cb)))
        args.append(arr)
        lacking.append(-off if off < 0 else 0)
    for kind, arr in consts:
        if kind == "seg":
            assert arr.shape[0] == nseg and arr.shape[1] == 1, arr.shape
            in_specs.append(pl.BlockSpec((None, 1, arr.shape[2]), lambda i: (seg(i), 0, 0)))
        else:
            in_specs.append(pl.BlockSpec(arr.shape, lambda i: (0, 0)))
        args.append(arr)
    out_shape = [jax.ShapeDtypeStruct((n_rows, w), dt) for w, dt in outs]
    out_specs = [pl.BlockSpec((tm, w), lambda i: (i, 0)) for w, _ in outs]
    out_shape += [jax.ShapeDtypeStruct((nseg, 1, w), F32) for w in accs]
    out_specs += [pl.BlockSpec((None, 1, w), lambda i: (seg(i), 0, 0)) for w in accs]
    n_in, n_out, n_acc = len(args), len(outs), len(accs)

    def kern(*refs):
        i = pl.program_id(0)
        ins = [r[...] for r in refs[:n_in]]
        for k, lack in enumerate(lacking):
            if lack:
                ins[k] = jnp.where(i >= lack, ins[k], jnp.zeros_like(ins[k]))
        res, terms = fn(*ins)
        for ref, v in zip(refs[n_in:n_in + n_out], res):
            ref[...] = v.astype(ref.dtype)
        if n_acc:
            sums = [_sum0(v) for v in terms]
            first = (i == 0) | (i == seg_blocks) if nseg == 2 else (i == 0)
            acc_refs = refs[n_in + n_out:]

            @pl.when(first)
            def _():
                for ref, v in zip(acc_refs, sums):
                    ref[...] = v

            @pl.when(jnp.logical_not(first))
            def _():
                for ref, v in zip(acc_refs, sums):
                    ref[...] += v

    res = pl.pallas_call(
        kern, name=name, grid=(n_rows // tm,), in_specs=in_specs, out_specs=out_specs, out_shape=out_shape,
        compiler_params=_params("arbitrary"),
    )(*args)
    return res


def _pre_fwd_fn(h, g, shift, scale):
    hh, _ = _rms(h)
    return (hh * g * (1.0 + scale) + shift,), ()


def _pre_bwd_fn(du, h, dres, g, scale):
    hh, r = _rms(h)
    n = hh * g
    dn = du * (1.0 + scale)
    dhh = dn * g
    dh = dres + r * (dhh - hh * jnp.mean(dhh * hh, axis=-1, keepdims=True))
    return (dh,), (du, du * n, dn * hh)


def _post_fwd_fn(weight, h, y, g, gate):
    yh, _ = _rms(y)
    return (h + weight * gate * (yh * g),), ()


def _out_post_fn(weight, y, h, g, gate):
    return (y,) + _post_fwd_fn(weight, h, y, g, gate)[0], ()


def _post_bwd_fn(weight, dh, y, g, gate):
    yh, r = _rms(y)
    dr = dh * weight
    dyh = dr * gate * g
    dy = r * (dyh - yh * jnp.mean(dyh * yh, axis=-1, keepdims=True))
    return (dy,), (dr * yh * g, dr * gate * yh)


def _glu_bwd_fn(ds, a, b):
    a = a.astype(F32)
    b = b.astype(F32)
    sg = _sig(a)
    da = ds * b * (sg * (1.0 + a * (1.0 - sg)))
    db = ds * (a * sg)
    return (jnp.concatenate([da, db], axis=1),), ()


def _loss_fn(y, t):
    diff = y - t
    return (diff * (1.0 / D_MODEL),), (diff * diff,)


def _ssd_y(yf, yb, xs, z, dvec):
    y = yf + yb + dvec * xs
    return y, y * _silu(z)


def _ssdgate_fwd_fn(yf, yb, xs, z, dvec, ng):
    _, yg = _ssd_y(yf, yb, xs, z, dvec)
    parts = []
    for g in range(SSD_GROUPS):
        sl = slice(g * 256, (g + 1) * 256)
        parts.append(_rms(yg[:, sl])[0])
    return (jnp.concatenate(parts, axis=1) * ng,), ()


def _ssdgate_bwd_fn(dyn, yf, yb, xs, z, dvec, ng):
    y, yg = _ssd_y(yf, yb, xs, z, dvec)
    dyg_parts, ygh_parts = [], []
    for g in range(SSD_GROUPS):
        sl = slice(g * 256, (g + 1) * 256)
        ygh, r = _rms(yg[:, sl])
        d = dyn[:, sl] * ng[:, sl]
        dyg_parts.append(r * (d - ygh * jnp.mean(d * ygh, axis=-1, keepdims=True)))
        ygh_parts.append(ygh)
    dyg = jnp.concatenate(dyg_parts, axis=1)
    ygh = jnp.concatenate(ygh_parts, axis=1)
    dy = dyg * _silu(z)
    dz = dyg * y * _dsilu(z)
    return (dy, dz), (dyn * ygh, dy * xs)


def _ln_stats(v):
    mu = jnp.mean(v, axis=-1, keepdims=True)
    vc = v - mu
    r = lax.rsqrt(jnp.mean(vc * vc, axis=-1, keepdims=True) + EPS)
    return vc * r, r


def _gm_act_fwd_fn(p, vg, vb):
    gu = _gelu(p[:, :GM_INNER])
    gvh, _ = _ln_stats(_gelu(p[:, GM_INNER:]))
    return (gu, gvh * vg + vb), ()


def _gm_act_bwd_fn(p, dgu, dgvn, vg):
    pu = p[:, :GM_INNER]
    pv = p[:, GM_INNER:]
    gvh, r = _ln_stats(_gelu(pv))
    dgvh = dgvn * vg
    dgv = r * (dgvh - jnp.mean(dgvh, axis=-1, keepdims=True) - gvh * jnp.mean(dgvh * gvh, axis=-1, keepdims=True))
    dp = jnp.concatenate([dgu * _dgelu(pu), dgv * _dgelu(pv)], axis=1)
    return (dp,), (dgvn * gvh, dgvn)


def _mm(a, b, *, out_dtype, name, tm=1088, tn=1024, tk=1408, add=None, rhs_t=False, n=None, b_off=(0, 0)):
    m, k = a.shape
    if n is None:
        n, k2 = b.shape if rhs_t else b.shape[::-1]
        assert k == k2
    tm, tn, tk = _pick(m, tm), _pick(n, tn, 128), _pick(k, tk, 128)
    o0, o1 = b_off
    nk = k // tk
    dims = _NT if rhs_t else ((1,), (0,))

    def kern(*refs):
        a_ref, b_ref = refs[:2]
        add_ref = refs[2] if add is not None else None
        o_ref = refs[3] if add is not None else refs[2]

        def finish(r):
            if add is not None:
                r = r + add_ref[...]
            o_ref[...] = r.astype(o_ref.dtype)

        p = _dot(a_ref[...], b_ref[...], dims)
        if nk == 1:
            finish(p)
            return
        acc_ref = refs[-1]
        kk = pl.program_id(2)

        @pl.when(kk == 0)
        def _():
            acc_ref[...] = p

        @pl.when((kk > 0) & (kk < nk - 1))
        def _():
            acc_ref[...] += p

        @pl.when(kk == nk - 1)
        def _():
            finish(acc_ref[...] + p)

    if rhs_t:
        b_spec = pl.BlockSpec((tn, tk), lambda i, j, kk: (j + o0, kk + o1))
    else:
        b_spec = pl.BlockSpec((tk, tn), lambda i, j, kk: (kk + o0, j + o1))
    in_specs = [pl.BlockSpec((tm, tk), lambda i, j, kk: (i, kk)), b_spec]
    args = [a, b]
    if add is not None:
        in_specs.append(pl.BlockSpec((tm, tn), lambda i, j, kk: (i, j)))
        args.append(add)
    return pl.pallas_call(
        kern, name=name, grid=(m // tm, n // tn, nk), in_specs=in_specs,
        out_specs=pl.BlockSpec((tm, tn), lambda i, j, kk: (i, j)),
        out_shape=jax.ShapeDtypeStruct((m, n), out_dtype),
        scratch_shapes=[pltpu.VMEM((tm, tn), F32)] if nk > 1 else [],
        compiler_params=_params("parallel", "parallel", "arbitrary"),
    )(*args)


def _mm_rows(a, b, fn, rows, consts, outs, accs=(), *, name, tm=544, tk=1408, rhs_t=False, n_ctx=0):
    m, k = a.shape
    n = b.shape[0] if rhs_t else b.shape[1]
    tm, tk = _pick(m, tm), _pick(k, tk, 128)
    nk = k // tk
    dims = _NT if rhs_t else ((1,), (0,))
    n_rows, n_const, n_out, n_acc = len(rows), len(consts), len(outs), len(accs)

    def kern(*refs):
        a_ref, b_ref = refs[:2]
        row_refs = refs[2:2 + n_rows]
        const_refs = refs[2 + n_rows:2 + n_rows + n_const]
        out_refs = refs[2 + n_rows + n_const:2 + n_rows + n_const + n_out]
        acc_refs = refs[2 + n_rows + n_const + n_out:2 + n_rows + n_const + n_out + n_acc]
        i, kk = pl.program_id(0), pl.program_id(1)

        def finish(p, rs=slice(None), r0=0):
            nr = p.shape[0]
            is_ctx = (i * tm + r0 + lax.broadcasted_iota(jnp.int32, (nr, 1), 0)) < n_ctx
            cvals = []
            for (kind, arr), ref in zip(consts, const_refs):
                if kind == "seg":
                    cvals.append(jnp.where(is_ctx, ref[0], ref[1]) if arr.shape[0] == 2 else ref[0])
                else:
                    cvals.append(ref[...])
            res, terms = fn(p, *[r[rs, :] for r in row_refs], *cvals)
            for ref, v in zip(out_refs, res):
                ref[rs, :] = v.astype(ref.dtype)
            for ref, v in zip(acc_refs, terms):
                s_all = _sum0(v)
                s_ctx = _sum0(jnp.where(is_ctx, v, 0.0)) if n_ctx else jnp.zeros_like(s_all)
                both = jnp.concatenate([s_ctx, s_all - s_ctx], axis=0)[:, None, :]

                @pl.when(i == 0)
                def _():
                    ref[...] = both

                @pl.when(i > 0)
                def _():
                    ref[...] += both

        if nk == 1 and n_acc == 0:
            nsub = next(s for s in (4, 2, 1) if tm % (16 * s) == 0)
            sub = tm // nsub
            for r in range(nsub):
                rs = slice(r * sub, (r + 1) * sub)
                finish(_dot(a_ref[rs, :], b_ref[...], dims), rs, r * sub)
            return
        p = _dot(a_ref[...], b_ref[...], dims)
        if nk == 1:
            finish(p)
            return
        scr = refs[-1]

        @pl.when(kk == 0)
        def _():
            scr[...] = p

        @pl.when((kk > 0) & (kk < nk - 1))
        def _():
            scr[...] += p

        @pl.when(kk == nk - 1)
        def _():
            finish(scr[...] + p)

    b_spec = pl.BlockSpec((n, tk), lambda i, kk: (0, kk)) if rhs_t else pl.BlockSpec((tk, n), lambda i, kk: (kk, 0))
    in_specs = [pl.BlockSpec((tm, tk), lambda i, kk: (i, kk)), b_spec]
    in_specs += [pl.BlockSpec((tm, r.shape[1]), lambda i, kk: (i, 0)) for r in rows]
    for kind, arr in consts:
        in_specs.append(pl.BlockSpec(arr.shape, (lambda i, kk: (0, 0, 0)) if kind == "seg" else (lambda i, kk: (0, 0))))
    out_shape = [jax.ShapeDtypeStruct((m, w), dt) for w, dt in outs]
    out_specs = [pl.BlockSpec((tm, w), lambda i, kk: (i, 0)) for w, _ in outs]
    out_shape += [jax.ShapeDtypeStruct((2, 1, w), F32) for w in accs]
    out_specs += [pl.BlockSpec((2, 1, w), lambda i, kk: (0, 0, 0)) for w in accs]
    return pl.pallas_call(
        kern, name=name, grid=(m // tm, nk), in_specs=in_specs, out_specs=out_specs, out_shape=out_shape,
        scratch_shapes=[pltpu.VMEM((tm, n), F32)] if nk > 1 else [],
        compiler_params=_params("arbitrary", "arbitrary"),
    )(a, b, *rows, *[arr for _, arr in consts])


def _mm_glu(u, win, *, name, tm=2176, tn=256):
    m, k = u.shape
    n = win.shape[1] // 2
    tm, tn = _pick(m, tm), _pick(n, tn, 128)
    nj = n // tn

    nsub = 4 if tm % 64 == 0 else 1
    sub = tm // nsub

    def kern(u_ref, wa_ref, wb_ref, s_ref, a_ref, b_ref):
        for r in range(nsub):
            rows = slice(r * sub, (r + 1) * sub)
            uu = u_ref[rows, :]
            a = jnp.dot(uu, wa_ref[...], preferred_element_type=F32)
            b = jnp.dot(uu, wb_ref[...], preferred_element_type=F32)
            s_ref[rows, :] = (_silu(a) * b).astype(BF16)
            a_ref[rows, :] = a.astype(BF16)
            b_ref[rows, :] = b.astype(BF16)

    ospec = pl.BlockSpec((tm, tn), lambda i, j: (i, j))
    return pl.pallas_call(
        kern, name=name, grid=(m // tm, nj),
        in_specs=[pl.BlockSpec((tm, k), lambda i, j: (i, 0)), pl.BlockSpec((k, tn), lambda i, j: (0, j)),
                  pl.BlockSpec((k, tn), lambda i, j: (0, nj + j))],
        out_specs=[ospec, ospec, ospec],
        out_shape=[jax.ShapeDtypeStruct((m, n), BF16)] * 3,
        compiler_params=_params("parallel", "parallel"),
    )(u, win, win)


def _mm_tn(a, b, *, name, tm=1024, tn=1024, tk=1088, col_blocks=None):
    t, m = a.shape
    t2, n = b.shape
    assert t == t2
    tm, tn, tk = _pick(m, tm, 128), _pick(n, tn, 128), _pick(t, tk)
    nk = t // tk
    if col_blocks is None:
        def kern(a_ref, b_ref, o_ref):
            kk = pl.program_id(2)

            @pl.when(kk == 0)
            def _():
                o_ref[...] = jnp.zeros_like(o_ref)

            o_ref[...] += _dot(a_ref[...], b_ref[...], _TN)

        out_spec = pl.BlockSpec((tm, tn), lambda i, j, kk: (i, j))
        out_shape = jax.ShapeDtypeStruct((m, n), F32)
        scratch = []
    else:
        wb = n // col_blocks
        per = tn // wb
        assert tn % wb == 0 and wb % 8 == 0

        def kern(a_ref, b_ref, o_ref, acc_ref):
            kk = pl.program_id(2)
            p = _dot(a_ref[...], b_ref[...], _TN)

            @pl.when(kk == 0)
            def _():
                acc_ref[...] = p

            @pl.when((kk > 0) & (kk < nk - 1))
            def _():
                acc_ref[...] += p

            @pl.when(kk == nk - 1)
            def _():
                r = acc_ref[...] + p if nk > 1 else p
                for c in range(per):
                    o_ref[c] = r[:, c * wb:(c + 1) * wb].astype(BF16)

        out_spec = pl.BlockSpec((per, tm, wb), lambda i, j, kk: (j, i, 0))
        out_shape = jax.ShapeDtypeStruct((col_blocks, m, wb), BF16)
        scratch = [pltpu.VMEM((tm, tn), F32)]

    return pl.pallas_call(
        kern, name=name, grid=(m // tm, n // tn, nk),
        in_specs=[pl.BlockSpec((tk, tm), lambda i, j, kk: (kk, i)), pl.BlockSpec((tk, tn), lambda i, j, kk: (kk, j))],
        out_specs=out_spec, out_shape=out_shape, scratch_shapes=scratch,
        compiler_params=_params("parallel", "parallel", "arbitrary"),
    )(a, b)


def _mm_f32(a, b, *, name, silu_a=False, bias=None):
    m, k = a.shape
    n = b.shape[1]

    def kern(*refs):
        if bias is None:
            a_ref, b_ref, o_ref = refs
        else:
            a_ref, b_ref, bias_ref, o_ref = refs
        av = a_ref[...]
        if silu_a:
            av = _silu(av)
        r = jnp.dot(av, b_ref[...], preferred_element_type=F32, precision=HI)
        if bias is not None:
            r = r + bias_ref[...]
        o_ref[...] = r

    args = [a, b] + ([] if bias is None else [bias])
    return pl.pallas_call(kern, name=name, out_shape=jax.ShapeDtypeStruct((m, n), F32),
                          compiler_params=pltpu.CompilerParams(vmem_limit_bytes=VMEM_LIMIT_BYTES))(*args)


CONV_WIN = 32


def _conv_windows(n, n_ctx):
    assert n_ctx % CONV_WIN == 0 and n_ctx >= CONV_WIN and n - n_ctx >= CONV_WIN
    return (0, n_ctx - CONV_WIN // 2, n - CONV_WIN)


def _tap_outside(r0, s, n, n_ctx):
    t = r0 + lax.broadcasted_iota(jnp.int32, (CONV_WIN, 1), 0)
    lo = jnp.where(t < n_ctx, 0, n_ctx)
    hi = jnp.where(t < n_ctx, n_ctx, n)
    return jnp.where((t + s >= lo) & (t + s < hi), 0.0, 1.0)


def _rolled(v, s):
    return v if s == 0 else pltpu.roll(v, (-s) % v.shape[0], 0)


def _conv_fwd(xp, w8, b, *, n_ctx, name, cb=256):
    n, c = xp.shape
    half = SSD_CONV // 2

    def kern(x_ref, w_ref, b_ref, cpre_ref, act_ref):
        x = x_ref[...]
        acc = jnp.zeros_like(x) + b_ref[...]
        rolled = {}
        for k in range(SSD_CONV):
            rolled[k] = _rolled(x, k - half)
            acc = acc + rolled[k] * w_ref[k:k + 1, :]
        cpre_ref[...] = acc
        act_ref[...] = _silu(acc)
        for r0 in _conv_windows(n, n_ctx):
            rows = slice(r0, r0 + CONV_WIN)
            fix = acc[rows]
            for k in range(SSD_CONV):
                if k != half:
                    fix = fix - rolled[k][rows] * w_ref[k:k + 1, :] * _tap_outside(r0, k - half, n, n_ctx)
            cpre_ref[rows, :] = fix
            act_ref[rows, :] = _silu(fix)

    spec = pl.BlockSpec((n, cb), lambda j: (0, j))
    return pl.pallas_call(
        kern, name=name, grid=(c // cb,),
        in_specs=[spec, pl.BlockSpec((8, cb), lambda j: (0, j)), pl.BlockSpec((1, cb), lambda j: (0, j))],
        out_specs=[spec, spec], out_shape=[jax.ShapeDtypeStruct((n, c), F32)] * 2,
        compiler_params=_params("parallel"),
    )(xp, w8, b)


def _conv_bwd(d1, d2, cpre, xp, w8, *, n_ctx, name, cb=128):
    n, c = xp.shape
    half = SSD_CONV // 2

    def kern(d1_ref, d2_ref, cpre_ref, x_ref, w_ref, dx_ref, dw_ref, db_ref):
        g = (d1_ref[...] + d2_ref[...]) * _dsilu(cpre_ref[...])
        x = x_ref[...]
        dx = jnp.zeros_like(g)
        dw_ref[...] = jnp.zeros_like(dw_ref)
        g_rolled = {}
        for k in range(SSD_CONV):
            s = k - half
            g_rolled[k] = _rolled(g, -s)
            dx = dx + g_rolled[k] * w_ref[k:k + 1, :]
            xr = _rolled(x, s)
            dw = _sum0(g * xr)
            if s != 0:
                for r0 in _conv_windows(n, n_ctx):
                    rows = slice(r0, r0 + CONV_WIN)
                    dw = dw - _sum0(g[rows] * xr[rows] * _tap_outside(r0, s, n, n_ctx))
            dw_ref[k:k + 1, :] = dw
        dx_ref[...] = dx.astype(BF16)
        for r0 in _conv_windows(n, n_ctx):
            rows = slice(r0, r0 + CONV_WIN)
            fix = dx[rows]
            for k in range(SSD_CONV):
                if k != half:
                    fix = fix - g_rolled[k][rows] * w_ref[k:k + 1, :] * _tap_outside(r0, half - k, n, n_ctx)
            dx_ref[rows, :] = fix.astype(BF16)
        db_ref[...] = _sum0(g)

    spec = pl.BlockSpec((n, cb), lambda j: (0, j))
    return pl.pallas_call(
        kern, name=name, grid=(c // cb,),
        in_specs=[spec, spec, spec, spec, pl.BlockSpec((8, cb), lambda j: (0, j))],
        out_specs=[spec, pl.BlockSpec((8, cb), lambda j: (0, j)), pl.BlockSpec((1, cb), lambda j: (0, j))],
        out_shape=[jax.ShapeDtypeStruct((n, c), BF16), jax.ShapeDtypeStruct((8, c), F32),
                   jax.ShapeDtypeStruct((1, c), F32)],
        compiler_params=_params("parallel"),
    )(d1, d2, cpre, xp, w8)


def _chunk_of(s, nc, n_ctx_chunks, rev):
    if not rev:
        return s
    return jnp.where(s < n_ctx_chunks, n_ctx_chunks - 1 - s, nc - 1 - (s - n_ctx_chunks))


def _scan_common(dt_raw, dtT_raw, bias_r, bias_c, alog_r, alog_c, rev):
    ii = lax.broadcasted_iota(jnp.int32, (CHUNK, CHUNK), 0)
    jj = lax.broadcasted_iota(jnp.int32, (CHUNK, CHUNK), 1)
    tri = (jj >= ii) if rev else (jj <= ii)
    tri_t = (ii >= jj) if rev else (ii <= jj)
    a_r = -jnp.exp(alog_r)
    a_c = -jnp.exp(alog_c)
    dt = _softplus(dt_raw + bias_r)
    dt_t = _softplus(dtT_raw + bias_c)
    al = dt * a_r
    acum = _dot(tri.astype(F32), al, precision=HI)
    acum_t = _dot(dt_t * a_c, tri_t.astype(F32), precision=HI)
    atot = _sum0(al)
    return tri, tri_t, a_r, dt, acum, acum_t, atot


def _head_spread():
    return jnp.repeat(jnp.eye(SSD_HEADS, dtype=BF16), SSD_HEAD_DIM, axis=1)


def _dot_sel(v, sel):
    hi = v.astype(BF16)
    lo = (v - hi.astype(F32)).astype(BF16)
    return _dot(hi, sel) + _dot(lo, sel)


def _ssd_scan_fwd(xbc, dt_raw, dtT_raw, bias_r, bias_c, alog_r, alog_c, *, rev, n_ctx_chunks, name):
    n = xbc.shape[0]
    nc = n // CHUNK
    cidx = functools.partial(_chunk_of, nc=nc, n_ctx_chunks=n_ctx_chunks, rev=rev)

    def kern(xs_ref, b_ref, c_ref, dt_ref, dtT_ref, br_ref, bc_ref, ar_ref, ac_ref, e_ref, y_ref, hs_ref, h_scr):
        @pl.when(pl.program_id(0) == 0)
        def _():
            h_scr[...] = jnp.zeros_like(h_scr)

        tri, _, _, dt, acum, acum_t, atot = _scan_common(
            dt_ref[...], dtT_ref[...], br_ref[...], bc_ref[...], ar_ref[...], ac_ref[...], rev)
        etot = jnp.exp(atot)
        spread = lambda v: _dot_sel(v, e_ref[...])
        xdt_all = xs_ref[...] * spread(dt)
        eax = spread(jnp.exp(acum))
        xdw_all = xdt_all * spread(jnp.exp(atot - acum))
        hs_ref[...] = h_scr[...]
        for g in range(SSD_GROUPS):
            gs = slice(g * 256, (g + 1) * 256)
            bg = b_ref[:, g * SSD_STATE:(g + 1) * SSD_STATE].astype(BF16)
            cg = c_ref[:, g * SSD_STATE:(g + 1) * SSD_STATE].astype(BF16)
            cb = _dot(cg, bg, _NT)
            h4 = h_scr[gs, :]
            ys = []
            for k in range(SSD_HPG):
                h = g * SSD_HPG + k
                lmat = jnp.exp(jnp.where(tri, acum[:, h:h + 1] - acum_t[h:h + 1, :], NEG_BIG))
                xdt_h = xdt_all[:, h * SSD_HEAD_DIM:(h + 1) * SSD_HEAD_DIM].astype(BF16)
                ys.append(_dot((cb * lmat).astype(BF16), xdt_h))
            y_ref[:, gs] = jnp.concatenate(ys, axis=1) + _dot(cg, h4.astype(BF16), _NT) * eax[:, gs]
            s4 = _dot(xdw_all[:, gs].astype(BF16), bg, _TN)
            for k in range(SSD_HPG):
                h = g * SSD_HPG + k
                rs = slice(h * SSD_HEAD_DIM, (h + 1) * SSD_HEAD_DIM)
                h_scr[rs, :] = h4[k * SSD_HEAD_DIM:(k + 1) * SSD_HEAD_DIM] * etot[:, h:h + 1] + \
                    s4[k * SSD_HEAD_DIM:(k + 1) * SSD_HEAD_DIM]

    nh = SSD_HEADS
    small = lambda shape: pl.BlockSpec(shape, lambda s: (0, 0))
    return pl.pallas_call(
        kern, name=name, grid=(nc,),
        in_specs=[pl.BlockSpec((CHUNK, SSD_INNER), lambda s: (cidx(s), 0)),
                  pl.BlockSpec((CHUNK, 1024), lambda s: (cidx(s), 2)),
                  pl.BlockSpec((CHUNK, 1024), lambda s: (cidx(s), 3)),
                  pl.BlockSpec((CHUNK, nh), lambda s: (cidx(s), 0)),
                  pl.BlockSpec((nh, CHUNK), lambda s: (0, cidx(s))),
                  small((1, nh)), small((nh, 1)), small((1, nh)), small((nh, 1)), small((nh, SSD_INNER))],
        out_specs=[pl.BlockSpec((CHUNK, SSD_INNER), lambda s: (cidx(s), 0)),
                   pl.BlockSpec((None, SSD_INNER, SSD_STATE), lambda s: (s, 0, 0))],
        out_shape=[jax.ShapeDtypeStruct((n, SSD_INNER), F32),
                   jax.ShapeDtypeStruct((nc, SSD_INNER, SSD_STATE), F32)],
        scratch_shapes=[pltpu.VMEM((SSD_INNER, SSD_STATE), F32)],
        compiler_params=_params("arbitrary"),
    )(xbc, xbc, xbc, dt_raw, dtT_raw, bias_r, bias_c, alog_r, alog_c, _head_spread())


def _ssd_scan_bwd(dy, xbc, hs, dt_raw, dtT_raw, bias_r, bias_c, alog_r, alog_c, dvec, *, rev, n_ctx_chunks,
                  direct, name):
    n = xbc.shape[0]
    nc = n // CHUNK
    nh = SSD_HEADS
    step_of = lambda r: nc - 1 - r
    cidx = lambda r: _chunk_of(step_of(r), nc, n_ctx_chunks, rev)

    def kern(dy_ref, xs_ref, b_ref, c_ref, hs_ref, dt_ref, dtT_ref, br_ref, bc_ref, ar_ref, ac_ref, dv_ref,
             e_ref, et_ref, dx_ref, ddt_ref, dal_ref, dbias_ref, dh_scr):
        @pl.when(pl.program_id(0) == 0)
        def _():
            dh_scr[...] = jnp.zeros_like(dh_scr)
            dal_ref[...] = jnp.zeros_like(dal_ref)
            dbias_ref[...] = jnp.zeros_like(dbias_ref)

        tri, tri_t, a_r, dt, acum, acum_t, atot = _scan_common(
            dt_ref[...], dtT_ref[...], br_ref[...], bc_ref[...], ar_ref[...], ac_ref[...], rev)
        etot = jnp.exp(atot)
        spread = lambda v: _dot_sel(v, e_ref[...])
        gather = lambda v: _dot_sel(v, et_ref[...])
        xs_all = xs_ref[...]
        dy_all = dy_ref[...]
        dtx = spread(dt)
        eax = spread(jnp.exp(acum))
        decx = spread(jnp.exp(atot - acum))
        xdt_all = xs_all * dtx
        xdw_all = xdt_all * decx
        dyo_all = dy_all * eax
        lane = lax.broadcasted_iota(jnp.int32, (CHUNK, nh), 1)
        lane1 = lax.broadcasted_iota(jnp.int32, (1, nh), 1)
        sub = lax.broadcasted_iota(jnp.int32, (nh, CHUNK), 0)
        g_rows = jnp.zeros((CHUNK, nh), F32)
        g_cols = jnp.zeros((nh, CHUNK), F32)
        dtot = jnp.zeros((1, nh), F32)
        q_col, q_e, q_dt = [], [], []
        for g in range(SSD_GROUPS):
            gs = slice(g * 256, (g + 1) * 256)
            bg = b_ref[:, g * SSD_STATE:(g + 1) * SSD_STATE].astype(BF16)
            cg = c_ref[:, g * SSD_STATE:(g + 1) * SSD_STATE].astype(BF16)
            cb = _dot(cg, bg, _NT)
            hs4 = hs_ref[gs, :]
            dh4 = dh_scr[gs, :]
            hs4_bf = hs4.astype(BF16)
            dh4_bf = dh4.astype(BF16)
            dy4 = dy_all[:, gs]
            dy4_bf = dy4.astype(BF16)
            xdt4_bf = xdt_all[:, gs].astype(BF16)
            xdw4 = xdw_all[:, gs]
            xdw4_bf = xdw4.astype(BF16)
            dyo4_bf = dyo_all[:, gs].astype(BF16)
            yoff4 = _dot(cg, hs4_bf, _NT) * eax[:, gs]
            dcg = _dot(dyo4_bf, hs4_bf)
            dh_new4 = _dot(dyo4_bf, cg, _TN)
            bdh4 = _dot(bg, dh4_bf, _NT)
            dbg = _dot(xdw4_bf, dh4_bf)
            e4 = xdw4 * bdh4
            q_col.append(dy4 * yoff4 - e4)
            q_e.append(e4)
            hsum = jnp.sum(dh4 * hs4, axis=1, keepdims=True)
            dcb = jnp.zeros((CHUNK, CHUNK), F32)
            dxdts = []
            for k in range(SSD_HPG):
                h = g * SSD_HPG + k
                ks = slice(k * SSD_HEAD_DIM, (k + 1) * SSD_HEAD_DIM)
                lmat = jnp.exp(jnp.where(tri, acum[:, h:h + 1] - acum_t[h:h + 1, :], NEG_BIG))
                mf = cb * lmat
                dm = _dot(dy4_bf[:, ks], xdt4_bf[:, ks], _NT)
                dcb = dcb + dm * lmat
                gmat = dm * mf
                g_rows = g_rows + jnp.where(lane == h, jnp.sum(gmat, axis=1, keepdims=True), 0.0)
                g_cols = g_cols + jnp.where(sub == h, _sum0(gmat), 0.0)
                dxdts.append(_dot(mf.astype(BF16), dy4_bf[:, ks], _TN))
                et = etot[:, h:h + 1]
                dtot = dtot + jnp.where(lane1 == h, _sum0(hsum[ks]) * et, 0.0)
                dh_scr[h * SSD_HEAD_DIM:(h + 1) * SSD_HEAD_DIM, :] = dh4[ks] * et + dh_new4[ks]
            dxdt4 = jnp.concatenate(dxdts, axis=1) + bdh4 * decx[:, gs]
            q_dt.append(dxdt4 * xs_all[:, gs])
            dx4 = dxdt4 * dtx[:, gs]
            if direct:
                dx4 = dx4 + dy4 * dv_ref[:, gs]
            dcb_bf = dcb.astype(BF16)
            dx_ref[:, gs] = dx4
            dx_ref[:, SSD_INNER + g * SSD_STATE:SSD_INNER + (g + 1) * SSD_STATE] = dbg + _dot(dcb_bf, cg, _TN)
            dx_ref[:, SSD_INNER + 1024 + g * SSD_STATE:SSD_INNER + 1024 + (g + 1) * SSD_STATE] = \
                dcg + _dot(dcb_bf, bg)
        e_heads = gather(jnp.concatenate(q_e, axis=1))
        dacum = gather(jnp.concatenate(q_col, axis=1)) + g_rows - g_cols.T
        dal = _dot(tri_t.astype(F32), dacum, precision=HI) + dtot + _sum0(e_heads)
        ddt = gather(jnp.concatenate(q_dt, axis=1)) + dal * a_r
        ddt_raw = ddt * _sig(dt_ref[...] + br_ref[...])
        ddt_ref[...] = ddt_raw
        dal_ref[...] += _sum0(dal * dt) * a_r
        dbias_ref[...] += _sum0(ddt_raw)

    small = lambda shape: pl.BlockSpec(shape, lambda r: (0, 0))
    return pl.pallas_call(
        kern, name=name, grid=(nc,),
        in_specs=[pl.BlockSpec((CHUNK, SSD_INNER), lambda r: (cidx(r), 0)),
                  pl.BlockSpec((CHUNK, SSD_INNER), lambda r: (cidx(r), 0)),
                  pl.BlockSpec((CHUNK, 1024), lambda r: (cidx(r), 2)),
                  pl.BlockSpec((CHUNK, 1024), lambda r: (cidx(r), 3)),
                  pl.BlockSpec((None, SSD_INNER, SSD_STATE), lambda r: (step_of(r), 0, 0)),
                  pl.BlockSpec((CHUNK, nh), lambda r: (cidx(r), 0)),
                  pl.BlockSpec((nh, CHUNK), lambda r: (0, cidx(r))),
                  small((1, nh)), small((nh, 1)), small((1, nh)), small((nh, 1)), small((1, SSD_INNER)),
                  small((nh, SSD_INNER)), small((SSD_INNER, nh))],
        out_specs=[pl.BlockSpec((CHUNK, SSD_CONV_DIM), lambda r: (cidx(r), 0)),
                   pl.BlockSpec((CHUNK, nh), lambda r: (cidx(r), 0)),
                   small((1, nh)), small((1, nh))],
        out_shape=[jax.ShapeDtypeStruct((n, SSD_CONV_DIM), F32), jax.ShapeDtypeStruct((n, nh), F32),
                   jax.ShapeDtypeStruct((1, nh), F32), jax.ShapeDtypeStruct((1, nh), F32)],
        scratch_shapes=[pltpu.VMEM((SSD_INNER, SSD_STATE), F32)],
        compiler_params=_params("arbitrary"),
    )(dy, xbc, xbc, xbc, hs, dt_raw, dtT_raw, bias_r, bias_c, alog_r, alog_c, dvec, _head_spread(),
      _head_spread().T)


def _gm_spatial_fwd(gu, gvn, ws, bst, *, name):
    n = gu.shape[0]

    def kern(gu_ref, gv_ref, ws_ref, bs_ref, o_ref):
        for g in range(GM_GROUPS):
            sl = slice(g * GM_GROUP_DIM, (g + 1) * GM_GROUP_DIM)
            s = _dot(ws_ref[g], gv_ref[:, sl]) + bs_ref[:, g:g + 1]
            o_ref[:, sl] = (gu_ref[:, sl] * s).astype(BF16)

    spec = pl.BlockSpec((CHUNK, GM_INNER), lambda i: (i, 0))
    return pl.pallas_call(
        kern, name=name, grid=(n // CHUNK,),
        in_specs=[spec, spec, pl.BlockSpec(ws.shape, lambda i: (0, 0, 0)), pl.BlockSpec(bst.shape, lambda i: (0, 0))],
        out_specs=spec, out_shape=jax.ShapeDtypeStruct((n, GM_INNER), BF16),
        compiler_params=_params("parallel"),
    )(gu, gvn, ws, bst)


def _gm_spatial_bwd(dt, gu, gvn, ws, wst, bst, *, name):
    n = gu.shape[0]

    def kern(dt_ref, gu_ref, gv_ref, ws_ref, wst_ref, bs_ref, dgu_ref, dgv_ref, dws_ref, dbs_ref):
        @pl.when(pl.program_id(0) == 0)
        def _():
            dws_ref[...] = jnp.zeros_like(dws_ref)
            dbs_ref[...] = jnp.zeros_like(dbs_ref)

        lane = lax.broadcasted_iota(jnp.int32, (CHUNK, GM_GROUPS), 1)
        dbs = jnp.zeros((CHUNK, GM_GROUPS), F32)
        for g in range(GM_GROUPS):
            sl = slice(g * GM_GROUP_DIM, (g + 1) * GM_GROUP_DIM)
            gv = gv_ref[:, sl]
            s = _dot(ws_ref[g], gv) + bs_ref[:, g:g + 1]
            d = dt_ref[:, sl]
            dgu_ref[:, sl] = d * s
            ds = d * gu_ref[:, sl]
            ds_bf = ds.astype(BF16)
            dws_ref[g] += _dot(ds_bf, gv, _NT)
            dgv_ref[:, sl] = _dot(wst_ref[g], ds_bf)
            dbs = dbs + jnp.where(lane == g, jnp.sum(ds, axis=1, keepdims=True), 0.0)
        dbs_ref[...] += dbs

    spec = pl.BlockSpec((CHUNK, GM_INNER), lambda i: (i, 0))
    wspec = pl.BlockSpec(ws.shape, lambda i: (0, 0, 0))
    bspec = pl.BlockSpec(bst.shape, lambda i: (0, 0))
    return pl.pallas_call(
        kern, name=name, grid=(n // CHUNK,),
        in_specs=[spec, spec, spec, wspec, wspec, bspec],
        out_specs=[spec, spec, wspec, bspec],
        out_shape=[jax.ShapeDtypeStruct((n, GM_INNER), F32), jax.ShapeDtypeStruct((n, GM_INNER), F32),
                   jax.ShapeDtypeStruct(ws.shape, F32), jax.ShapeDtypeStruct(bst.shape, F32)],
        compiler_params=_params("arbitrary"),
    )(dt, gu, gvn, ws, wst, bst)


def _adamw(parts, w, m, v, *, name, tm=256, sel=(), into=None):
    ns, r, wd = parts.shape
    tm = _pick(r, tm, 8)
    lead = len(sel)
    assert w.shape[lead:] == (r, wd) and lead == w.ndim - 2

    def kern(*refs):
        p_ref, w_ref, m_ref, v_ref = refs[:4]
        g_ref, d_ref, nm_ref, nv_ref = refs[-4:]
        g = p_ref[0].astype(F32)
        for s in range(1, ns):
            g = g + p_ref[s].astype(F32)
        m2 = ADAM_B1 * m_ref[...] + (1.0 - ADAM_B1) * g
        v2 = ADAM_B2 * v_ref[...] + (1.0 - ADAM_B2) * (g * g)
        m_hat = m2 / (1.0 - ADAM_B1 ** ADAM_STEP)
        v_hat = v2 / (1.0 - ADAM_B2 ** ADAM_STEP)
        g_ref[...] = g
        d_ref[...] = -ADAM_LR * (m_hat / (jnp.sqrt(v_hat) + ADAM_EPS) + ADAM_WD * w_ref[...])
        nm_ref[...] = m2
        nv_ref[...] = v2

    spec = pl.BlockSpec((None,) * lead + (tm, wd), lambda i: tuple(sel) + (i, 0))
    chained = any(s > 1 for s in w.shape[:lead])
    extra, aliases = [], {}
    if chained:
        extra = list(into) if into is not None else [lax.empty(w.shape, F32) for _ in range(4)]
        aliases = {4 + k: k for k in range(4)}
    return pl.pallas_call(
        kern, name=name, grid=(r // tm,),
        in_specs=[pl.BlockSpec((ns, tm, wd), lambda i: (0, i, 0)), spec, spec, spec] +
                 [pl.BlockSpec(memory_space=pl.ANY)] * len(extra),
        out_specs=[spec] * 4, out_shape=[jax.ShapeDtypeStruct(w.shape, F32)] * 4,
        input_output_aliases=aliases,
        compiler_params=_params("parallel"),
    )(parts, w, m, v, *extra)


def _zero_after(x, *, name):
    def kern(x_ref, o_ref):
        o_ref[...] = jnp.zeros_like(o_ref)

    return pl.pallas_call(kern, name=name, out_shape=jax.ShapeDtypeStruct((8, 128), F32),
                          in_specs=[pl.BlockSpec(memory_space=pl.ANY)])(x)[0, 0]


def _sum_slots(parts, *, name, scale_by=None):
    ns, r, wd = parts.shape

    def kern(*refs):
        p_ref, o_ref = refs[0], refs[-1]
        g = p_ref[0]
        for s in range(1, ns):
            g = g + p_ref[s]
        if scale_by is not None:
            g = g * _dsilu(refs[1][...])
        o_ref[...] = g

    args = [parts] + ([] if scale_by is None else [scale_by])
    return pl.pallas_call(kern, name=name, out_shape=jax.ShapeDtypeStruct((r, wd), F32),
                          compiler_params=pltpu.CompilerParams(vmem_limit_bytes=VMEM_LIMIT_BYTES))(*args)


def _mesh_pos():
    x, y, c = lax.axis_index("x"), lax.axis_index("y"), lax.axis_index("c")
    return x, y, c, 4 * x + 2 * y + c


def _flip(x, y, c, f):
    fx, fy, fc = (f >> 2) & 1, (f >> 1) & 1, f & 1
    px = 1 - x if fx else x
    py = 1 - y if fy else y
    pc = 1 - c if fc else c
    return (px, py, pc), 4 * px + 2 * py + pc


_HBM_SPEC = pl.BlockSpec(memory_space=pltpu.HBM)


def _exchange(arrays, *, scatter, name):
    na = len(arrays)
    if scatter:
        out_shape = [jax.ShapeDtypeStruct(a.shape, a.dtype) for a in arrays]
    else:
        out_shape = [jax.ShapeDtypeStruct((NDEV,) + a.shape, a.dtype) for a in arrays]

    out_shape.append(jax.ShapeDtypeStruct((8, 128), F32))

    def body(*refs):
        ins, outs = refs[:na], refs[na:2 * na]
        send_sems, recv_sems, local_sems = refs[2 * na + 1:]
        refs[2 * na][...] = jnp.zeros((8, 128), F32)
        x, y, c, me = _mesh_pos()
        copies = []
        for i in range(na):
            src_own = ins[i].at[me] if scatter else ins[i]
            lc = pltpu.make_async_copy(src_own, outs[i].at[me], local_sems.at[i])
            lc.start()
            copies.append(lc)
        sends = []
        for f in range(1, NDEV):
            peer, pidx = _flip(x, y, c, f)
            for i in range(na):
                k = i * (NDEV - 1) + f - 1
                src = ins[i].at[pidx] if scatter else ins[i]
                cp = pltpu.make_async_remote_copy(
                    src_ref=src, dst_ref=outs[i].at[me], send_sem=send_sems.at[k], recv_sem=recv_sems.at[k],
                    device_id=peer, device_id_type=pl.DeviceIdType.MESH)
                cp.start()
                sends.append(cp)
        for f in range(1, NDEV):
            peer, pidx = _flip(x, y, c, f)
            for i in range(na):
                k = i * (NDEV - 1) + f - 1
                src = ins[i].at[pidx] if scatter else ins[i]
                pltpu.make_async_remote_copy(
                    src_ref=src, dst_ref=outs[i].at[pidx], send_sem=send_sems.at[k], recv_sem=recv_sems.at[k],
                    device_id=peer, device_id_type=pl.DeviceIdType.MESH).wait_recv()
        for cp in sends:
            cp.wait_send()
        for lc in copies:
            lc.wait()

    res = pl.pallas_call(
        body, name=name, out_shape=out_shape, in_specs=[_HBM_SPEC] * na,
        out_specs=[_HBM_SPEC] * na + [pl.BlockSpec(memory_space=pltpu.VMEM)],
        scratch_shapes=[pltpu.SemaphoreType.DMA((na * (NDEV - 1),)), pltpu.SemaphoreType.DMA((na * (NDEV - 1),)),
                        pltpu.SemaphoreType.DMA((na,))],
        compiler_params=pltpu.CompilerParams(has_side_effects=True),
    )(*arrays)
    return res[:na], res[na][0, 0]


_SEM_SPEC = pl.BlockSpec(memory_space=pltpu.SEMAPHORE)
_DATAFLOW = pltpu.SideEffectType.DATAFLOW_SIDE_EFFECTING


def _split_copies(srcs, lands, send_sems, recv_sems, scatter, arriving):
    x, y, c, me = _mesh_pos()
    copies = []
    for i in range(len(srcs)):
        for f in range(1, NDEV):
            peer, pidx = _flip(x, y, c, f)
            k = i * (NDEV - 1) + f - 1
            copies.append(pltpu.make_async_remote_copy(
                src_ref=srcs[i].at[pidx] if scatter else srcs[i], dst_ref=lands[i].at[pidx if arriving else me],
                send_sem=send_sems.at[k], recv_sem=recv_sems.at[k], device_id=peer,
                device_id_type=pl.DeviceIdType.MESH))
    return copies


def _exchange_start(srcs, lands, *, scatter, name):
    na = len(srcs)
    nsem = na * (NDEV - 1)

    def body(*refs):
        ins_src, ins_land = refs[:na], refs[na:2 * na]
        send_sems, recv_sems = refs[2 * na], refs[2 * na + 1]
        token = refs[-1]
        for cp in _split_copies(ins_src, ins_land, send_sems, recv_sems, scatter, False):
            cp.start()
        token[...] = jnp.zeros_like(token)

    thru = [pltpu.HBM(a.shape, a.dtype) for a in list(srcs) + list(lands)]
    res = pl.pallas_call(
        body, name=name,
        out_shape=(pltpu.SemaphoreType.DMA((nsem,)), pltpu.SemaphoreType.DMA((nsem,)), *thru,
                   jax.ShapeDtypeStruct((8, 128), F32)),
        in_specs=[_HBM_SPEC] * (2 * na),
        out_specs=(_SEM_SPEC, _SEM_SPEC, *([_HBM_SPEC] * (2 * na)), pl.BlockSpec(memory_space=pltpu.VMEM)),
        input_output_aliases={i: 2 + i for i in range(2 * na)},
        compiler_params=pltpu.CompilerParams(has_side_effects=_DATAFLOW),
    )(*[pltpu.with_memory_space_constraint(a, pltpu.HBM) for a in list(srcs) + list(lands)])
    send_sems, recv_sems = res[0], res[1]
    return send_sems, recv_sems, res[2:2 + na], res[2 + na:2 + 2 * na], res[-1][0, 0]


def _exchange_wait(send_sems, recv_sems, srcs, lands, after, *, scatter, name):
    na = len(srcs)

    def body(*refs):
        ins_src, ins_land = refs[:na], refs[na:2 * na]
        s_sems, r_sems = refs[2 * na], refs[2 * na + 1]
        for cp in _split_copies(ins_src, ins_land, s_sems, r_sems, scatter, False):
            cp.wait_send()
        for cp in _split_copies(ins_src, ins_land, s_sems, r_sems, scatter, True):
            cp.wait_recv()

    thru = [pltpu.HBM(a.shape, a.dtype) for a in list(srcs) + list(lands)]
    res = pl.pallas_call(
        body, name=name, out_shape=tuple(thru),
        in_specs=[_HBM_SPEC] * (2 * na) + [_SEM_SPEC, _SEM_SPEC, pl.BlockSpec(memory_space=pl.ANY)],
        out_specs=tuple([_HBM_SPEC] * (2 * na)),
        input_output_aliases={i: i for i in range(2 * na)},
        compiler_params=pltpu.CompilerParams(has_side_effects=_DATAFLOW),
    )(*srcs, *lands, send_sems, recv_sems, after)
    return res[na:]


def _landing(block, me):
    buf = lax.empty((NDEV,) + block.shape, block.dtype)
    return lax.dynamic_update_slice_in_dim(buf, block[None], me, axis=0)


def _seg_kw(nseg, n_ctx, tm):
    return dict(nseg=nseg, seg_blocks=(n_ctx // tm if nseg == 2 else 0))


def _ffn_fwd(tag, h, gpre, gpost, shift, scale, gate, w, *, nseg, n_ctx, tm):
    n = h.shape[0]
    kw = _seg_kw(nseg, n_ctx, tm)
    (u,) = _rowwise(tag + "_pre", _pre_fwd_fn, n, [h], [("full", gpre), ("seg", shift), ("seg", scale)],
                    [(D_MODEL, BF16)], tm=tm, **kw)
    if "early" in w:
        w.update(w.pop("early")(u))
    s, a, b = _mm_glu(u, w["win"], name=tag + "_glu")
    if "late" in w:
        w.update(w.pop("late")(s))
    y, ho = _mm_rows(s, w["wout"], functools.partial(_out_post_fn, 0.5), [h], [("full", gpost), ("seg", gate)],
                     [(D_MODEL, F32), (D_MODEL, F32)], name=tag + "_out", tk=FFN_DIM, n_ctx=n_ctx)
    return ho, dict(h=h, u=u, s=s, a=a, b=b, y=y)


def _ffn_bwd(tag, dho, sv, gpre, gpost, scale, gate, w, put, *, nseg, n_ctx, tm):
    n = dho.shape[0]
    kw = _seg_kw(nseg, n_ctx, tm)
    dy, dgate, dgpost = _rowwise(tag + "_postb", functools.partial(_post_bwd_fn, 0.5), n, [dho, sv["y"]],
                                 [("full", gpost), ("seg", gate)], [(D_MODEL, BF16)], [D_MODEL, D_MODEL], tm=tm, **kw)
    tok = put("w_out", _mm_tn(sv["s"], dy, name=tag + "_dwout", tm=1408, tn=1024, col_blocks=1))
    ds = _mm(dy, w["wout"], out_dtype=F32, name=tag + "_ds", tn=1408, rhs_t=True)
    (dp,) = _rowwise(tag + "_glub", _glu_bwd_fn, n, [ds, sv["a"], sv["b"]], [], [(2 * FFN_DIM, BF16)], tm=min(tm, 128))
    tok2 = put("w_in", _mm_tn(sv["u"], dp, name=tag + "_dwin", tn=1408, col_blocks=NDEV))
    for t in (tok, tok2):
        if t is not None:
            gpre = gpre + t
    dh, dshift, dscale, dgpre = _mm_rows(dp, w["win"], _pre_bwd_fn, [sv["h"], dho], [("full", gpre), ("seg", scale)],
                                         [(D_MODEL, F32)], [D_MODEL, D_MODEL, D_MODEL], name=tag + "_du",
                                         rhs_t=True, n_ctx=n_ctx)
    return dh, None, dict(shift=dshift, scale=dscale, gate=dgate, gpre=dgpre, gpost=dgpost)


def _local_step(x, ctx, target, mods, norm_g, get_w, small, put_grad):
    t_len, n_ctx = x.shape[0], ctx.shape[0]
    n0 = t_len + n_ctx
    tm0 = _pick(n_ctx, 256, 8)
    tm1 = _pick(t_len, 256, 8)
    ncc = n_ctx // CHUNK
    g = {}

    def modrow(i, k, nseg):
        mc, mx = mods[i]
        if nseg == 2:
            return jnp.stack([mc[k], mx[k]])[:, None, :]
        return mx[k][None, None, :]

    pending = [None]

    def gvec(i, k):
        v = norm_g[i, k][None, :]
        if pending[0] is not None:
            v = v + pending[0]
            pending[0] = None
        return v

    xc = jnp.concatenate([ctx, x], axis=0)
    L0 = dict(nseg=2, n_ctx=n_ctx, tm=tm0)
    wts = dict(get_w("ffn00", xc))
    h1, sv_f01 = _ffn_fwd("l0f1", xc, gvec(0, 0), gvec(0, 1), modrow(0, 0, 2), modrow(0, 1, 2), modrow(0, 2, 2),
                          wts["ffn00"], **L0)
    kw0 = _seg_kw(2, n_ctx, tm0)
    (um0,) = _rowwise("l0m_pre", _pre_fwd_fn, n0, [h1], [("full", gvec(0, 2)), ("seg", modrow(0, 3, 2)),
                                                         ("seg", modrow(0, 4, 2))], [(D_MODEL, BF16)], tm=tm0, **kw0)
    wts.update(get_w("ssd", um0))
    z = _mm(um0, wts["ssd_win"], out_dtype=F32, name="ssd_z", n=SSD_INNER)
    xbc_pre = _mm(um0, wts["ssd_win"], out_dtype=F32, name="ssd_xbc", n=SSD_CONV_DIM, b_off=(0, SSD_INNER // 1024))
    dtr = _mm(um0, wts["ssd_wdt"], out_dtype=F32, name="ssd_dt")
    cpre, xbc = _conv_fwd(xbc_pre, small["conv_w8"], small["conv_b"], n_ctx=n_ctx, name="ssd_conv")
    nh = SSD_HEADS
    dt_dir = [dtr[:, :nh], dtr[:, nh:2 * nh]]
    dtT_dir = [d.T for d in dt_dir]
    bias_r = [small["dt_bias"][d][None, :] for d in range(2)]
    bias_c = [small["dt_bias"][d][:, None] for d in range(2)]
    alog_r = [small["a_log"][d][None, :] for d in range(2)]
    alog_c = [small["a_log"][d][:, None] for d in range(2)]
    ys, hss = [], []
    for d in range(2):
        yd, hsd = _ssd_scan_fwd(xbc, dt_dir[d], dtT_dir[d], bias_r[d], bias_c[d], alog_r[d], alog_c[d],
                                rev=(d == 1), n_ctx_chunks=ncc, name=f"ssd_scan{d}")
        ys.append(yd)
        hss.append(hsd)
    dvec = jnp.repeat(small["ssd_d"], SSD_HEAD_DIM)[None, :]
    ngv = small["ssd_norm_g"][None, :]
    gate_rows = [ys[0], ys[1], (xbc, SSD_INNER, 0, 0), z]
    lat = lambda r: (r[0], r[1], r[2], ncc) if isinstance(r, tuple) else (r, r.shape[1], 0, ncc)
    (yn,) = _rowwise("ssd_gate", _ssdgate_fwd_fn, t_len, [lat(r) for r in gate_rows],
                     [("full", dvec), ("full", ngv)], [(SSD_INNER, BF16)], tm=CHUNK)
    h1x = h1[n_ctx:]
    L1 = dict(nseg=1, n_ctx=0, tm=tm1)
    if "late" in wts:
        wts.update(wts.pop("late")(yn))
    yo0, h2 = _mm_rows(yn, wts["ssd_wout"], functools.partial(_out_post_fn, 1.0), [h1x],
                       [("full", gvec(0, 3)), ("seg", modrow(0, 5, 1))], [(D_MODEL, F32), (D_MODEL, F32)],
                       name="ssd_out", tk=SSD_INNER)
    wts.update(get_w("ffn01", h2))
    h3, sv_f02 = _ffn_fwd("l0f2", h2, gvec(0, 4), gvec(0, 5), modrow(0, 6, 1), modrow(0, 7, 1), modrow(0, 8, 1),
                          wts["ffn01"], **L1)

    wts.update(get_w("ffn10", h3))
    h4, sv_f11 = _ffn_fwd("l1f1", h3, gvec(1, 0), gvec(1, 1), modrow(1, 0, 1), modrow(1, 1, 1), modrow(1, 2, 1),
                          wts["ffn10"], **L1)
    (um1,) = _rowwise("l1m_pre", _pre_fwd_fn, t_len, [h4], [("full", gvec(1, 2)), ("seg", modrow(1, 3, 1)),
                                                            ("seg", modrow(1, 4, 1))], [(D_MODEL, BF16)], tm=tm1)
    wts.update(get_w("gm", um1))
    p1 = _mm(um1, wts["gm_win"], out_dtype=F32, name="gm_in")
    vg = small["gm_v_g"][None, :]
    vb = small["gm_v_b"][None, :]
    gu, gvn = _rowwise("gm_act", _gm_act_fwd_fn, t_len, [p1], [("full", vg), ("full", vb)],
                       [(GM_INNER, F32), (GM_INNER, BF16)], tm=128)
    ws_bf = small["gm_w_s"].astype(BF16)
    wst_bf = jnp.swapaxes(small["gm_w_s"], 1, 2).astype(BF16)
    bst = small["gm_b_s"].T
    tgm = _gm_spatial_fwd(gu, gvn, ws_bf, bst, name="gm_spatial")
    yo1, h5 = _mm_rows(tgm, wts["gm_wout"], functools.partial(_out_post_fn, 1.0), [h4],
                       [("full", gvec(1, 3)), ("seg", modrow(1, 5, 1))], [(D_MODEL, F32), (D_MODEL, F32)],
                       name="gm_out", tk=GM_INNER)
    wts.update(get_w("ffn11", h5))
    h6, sv_f12 = _ffn_fwd("l1f2", h5, gvec(1, 4), gvec(1, 5), modrow(1, 6, 1), modrow(1, 7, 1), modrow(1, 8, 1),
                          wts["ffn11"], **L1)

    dh, loss_parts = _rowwise("loss", _loss_fn, t_len, [h6, target], [], [(D_MODEL, F32)], [D_MODEL], tm=tm1)

    zero = jnp.zeros((D_MODEL,), F32)
    dmx = [[zero] * N_MOD for _ in range(2)]
    dmc = [[zero] * N_MOD for _ in range(2)]
    dng = [[zero] * 6 for _ in range(2)]

    def put_mod(i, k, acc):
        if acc.shape[0] == 2:
            dmc[i][k] = dmc[i][k] + acc[0, 0]
            dmx[i][k] = dmx[i][k] + acc[1, 0]
        else:
            dmx[i][k] = dmx[i][k] + acc[0, 0]

    def put_g(i, k, acc):
        dng[i][k] = dng[i][k] + jnp.sum(acc[:, 0], axis=0)

    def ffn_back(tag, i, j, dho, sv, w, lay):
        nseg = lay["nseg"]
        base = 0 if j == 0 else 6
        gi = 0 if j == 0 else 4
        dh_in, pending[0], s = _ffn_bwd(tag, dho, sv, gvec(i, gi), gvec(i, gi + 1), modrow(i, base + 1, nseg),
                                        modrow(i, base + 2, nseg), w, functools.partial(put_grad, f"ffn{i}{j}"), **lay)
        put_mod(i, base, s["shift"])
        put_mod(i, base + 1, s["scale"])
        put_mod(i, base + 2, s["gate"])
        put_g(i, gi, s["gpre"])
        put_g(i, gi + 1, s["gpost"])
        return dh_in

    dh = ffn_back("l1f2", 1, 1, dh, sv_f12, wts["ffn11"], L1)
    dyo, dgate, dgp = _rowwise("l1m_postb", functools.partial(_post_bwd_fn, 1.0), t_len, [dh, yo1],
                               [("full", gvec(1, 3)), ("seg", modrow(1, 5, 1))], [(D_MODEL, BF16)],
                               [D_MODEL, D_MODEL], tm=tm1)
    put_mod(1, 5, dgate)
    put_g(1, 3, dgp)
    put_grad("gm", "w_out", _mm_tn(tgm, dyo, name="gm_dwout", tn=1024, col_blocks=1))
    dtg = _mm(dyo, wts["gm_wout"], out_dtype=F32, name="gm_dt", rhs_t=True)
    dgu, dgvn, dws, dbst = _gm_spatial_bwd(dtg, gu, gvn, ws_bf, wst_bf, bst, name="gm_spatialb")
    g["gm_w_s"] = dws
    g["gm_b_s"] = dbst.T
    dp1, dvg, dvb = _rowwise("gm_actb", _gm_act_bwd_fn, t_len, [p1, dgu, dgvn], [("full", vg)],
                             [(2 * GM_INNER, BF16)], [GM_INNER, GM_INNER], tm=128)
    g["gm_v_g"] = dvg[0, 0]
    g["gm_v_b"] = dvb[0, 0]
    pending[0] = put_grad("gm", "w_in", _mm_tn(um1, dp1, name="gm_dwin", tm=1024, col_blocks=NDEV))
    dh, dsh, dsc, dgp = _mm_rows(dp1, wts["gm_win"], _pre_bwd_fn, [h4, dh],
                                 [("full", gvec(1, 2)), ("seg", modrow(1, 4, 1))], [(D_MODEL, F32)],
                                 [D_MODEL, D_MODEL, D_MODEL], name="gm_dum", tk=1024, rhs_t=True)
    put_mod(1, 3, dsh)
    put_mod(1, 4, dsc)
    put_g(1, 2, dgp)
    dh = ffn_back("l1f1", 1, 0, dh, sv_f11, wts["ffn10"], L1)

    dh = ffn_back("l0f2", 0, 1, dh, sv_f02, wts["ffn01"], L1)
    dyo, dgate, dgp = _rowwise("l0m_postb", functools.partial(_post_bwd_fn, 1.0), t_len, [dh, yo0],
                               [("full", gvec(0, 3)), ("seg", modrow(0, 5, 1))], [(D_MODEL, BF16)],
                               [D_MODEL, D_MODEL], tm=tm1)
    put_mod(0, 5, dgate)
    put_g(0, 3, dgp)
    tok = put_grad("ssd", "w_out", _mm_tn(yn, dyo, name="ssd_dwout", tn=1024, col_blocks=1))
    dyn = _mm(dyo, wts["ssd_wout"], out_dtype=F32, name="ssd_dyn", rhs_t=True)
    dy_ssd, dz, dngv, ddv = _rowwise("ssd_gateb", _ssdgate_bwd_fn, n0, [(dyn, SSD_INNER, 0, -ncc)] + gate_rows,
                                     [("full", dvec), ("full", ngv if tok is None else ngv + tok)],
                                     [(SSD_INNER, F32), (SSD_INNER, BF16)],
                                     [SSD_INNER, SSD_INNER], tm=128)
    g["ssd_norm_g"] = dngv[0, 0]
    g["ssd_D"] = jnp.sum(ddv[0, 0].reshape(SSD_HEADS, SSD_HEAD_DIM), axis=1)
    dxbcs, ddts, dalogs, dbiases = [], [], [], []
    for d in range(2):
        dxd, ddtd, dal, dbi = _ssd_scan_bwd(dy_ssd, xbc, hss[d], dt_dir[d], dtT_dir[d], bias_r[d], bias_c[d],
                                            alog_r[d], alog_c[d], dvec, rev=(d == 1), n_ctx_chunks=ncc,
                                            direct=(d == 0), name=f"ssd_scanb{d}")
        dxbcs.append(dxd)
        ddts.append(ddtd)
        dalogs.append(dal[0])
        dbiases.append(dbi[0])
    g["ssd_A_log"] = jnp.stack(dalogs)
    g["ssd_dt_bias"] = jnp.stack(dbiases)
    dxbc_pre, dcw8, dcb = _conv_bwd(dxbcs[0], dxbcs[1], cpre, xbc_pre, small["conv_w8"], n_ctx=n_ctx, name="ssd_convb")
    g["ssd_conv_w"] = dcw8[:SSD_CONV]
    g["ssd_conv_b"] = dcb[0]
    ddt_bf = jnp.concatenate([ddts[0], ddts[1], jnp.zeros((n0, 128 - 2 * nh), F32)], axis=1).astype(BF16)
    dw_ssd_in = jnp.concatenate([
        _mm_tn(um0, dz, name="ssd_dwz", tm=1024),
        _mm_tn(um0, dxbc_pre, name="ssd_dwxbc", tm=1024),
        _mm_tn(um0, ddt_bf, name="ssd_dwdt", tm=1024)[:, :2 * nh]], axis=1)
    pending[0] = put_grad("ssd", "w_in", dw_ssd_in)
    win_ssd = wts["ssd_win"]
    dum0 = _mm(dz, win_ssd, out_dtype=F32, name="ssd_dum_z", tk=1024, rhs_t=True, n=D_MODEL)
    dum0 = _mm(dxbc_pre, win_ssd, out_dtype=F32, name="ssd_dum_x", tk=1024, rhs_t=True, n=D_MODEL,
               b_off=(0, SSD_INNER // 1024), add=dum0)
    dum0 = _mm(ddt_bf, wts["ssd_wdt"], out_dtype=F32, name="ssd_dum_dt", rhs_t=True, add=dum0)
    dh0, dsh, dsc, dgp = _rowwise("l0m_preb", _pre_bwd_fn, n0, [dum0, h1, (dh, D_MODEL, 0, -(n_ctx // tm0))],
                                  [("full", gvec(0, 2)), ("seg", modrow(0, 4, 2))], [(D_MODEL, F32)],
                                  [D_MODEL, D_MODEL, D_MODEL], tm=tm0, **kw0)
    put_mod(0, 3, dsh)
    put_mod(0, 4, dsc)
    put_g(0, 2, dgp)
    dh0 = ffn_back("l0f1", 0, 0, dh0, sv_f01, wts["ffn00"], L0)
    grad_x = dh0[n_ctx:]
    g["norm_g"] = jnp.stack([jnp.stack(r) for r in dng])
    g["dmx"] = jnp.stack([jnp.concatenate(r) for r in dmx])
    g["dmc"] = jnp.stack([jnp.concatenate(r) for r in dmc])
    return loss_parts[0], grad_x, g


GROUPS = ("ffn00", "ssd", "ffn01", "ffn10", "gm", "ffn11")


def _mats_in(group, win_l):
    k, nloc = win_l.shape[1], win_l.shape[2]
    win = jnp.transpose(win_l, (1, 0, 2)).reshape(k, NDEV * nloc)
    if group.startswith("ffn"):
        return dict(win=win)
    if group == "gm":
        return dict(gm_win=win)
    assert group == "ssd"
    c1 = SSD_INNER + SSD_CONV_DIM
    return dict(ssd_win=win, ssd_wdt=jnp.pad(win[:, c1:], ((0, 0), (0, 128 - 2 * SSD_HEADS))))


def _mats_out(group, wout_l):
    pre = "" if group.startswith("ffn") else group + "_"
    return {pre + "wout": wout_l.reshape(-1, wout_l.shape[2])}


def _group_mats(group, lands):
    m = {**_mats_in(group, lands[0]), **_mats_out(group, lands[1])}
    return {group: m} if group.startswith("ffn") else m


def _grad_blocks(which, grad):
    if grad.ndim == 3:
        return grad if which == "w_in" else grad.reshape(NDEV, grad.shape[1] // NDEV, grad.shape[2])
    if which == "w_in":
        k, n = grad.shape
        return jnp.transpose(grad.reshape(k, NDEV, n // NDEV), (1, 0, 2)).astype(BF16)
    return grad.reshape(NDEV, grad.shape[0] // NDEV, grad.shape[1]).astype(BF16)


def kernel(x, c, ctx, c_ctx, ada_w, ada_b, norm_g, ffn_w_in, ffn_w_out, ssd_w_in, ssd_conv_w, ssd_conv_b, ssd_dt_bias, ssd_A_log, ssd_D, ssd_norm_g, ssd_w_out, gm_w_in, gm_v_g, gm_v_b, gm_w_s, gm_b_s, gm_w_out, loss_target, m_c_ctx, m_ada_w, m_ada_b, m_norm_g, m_ffn_w_in, m_ffn_w_out, m_ssd_w_in, m_ssd_conv_w, m_ssd_conv_b, m_ssd_dt_bias, m_ssd_A_log, m_ssd_D, m_ssd_norm_g, m_ssd_w_out, m_gm_w_in, m_gm_v_g, m_gm_v_b, m_gm_w_s, m_gm_b_s, m_gm_w_out, v_c_ctx, v_ada_w, v_ada_b, v_norm_g, v_ffn_w_in, v_ffn_w_out, v_ssd_w_in, v_ssd_conv_w, v_ssd_conv_b, v_ssd_dt_bias, v_ssd_A_log, v_ssd_D, v_ssd_norm_g, v_ssd_w_out, v_gm_w_in, v_gm_v_g, v_gm_v_b, v_gm_w_s, v_gm_b_s, v_gm_w_out):
    me = 4 * lax.axis_index("x") + 2 * lax.axis_index("y") + lax.axis_index("c")
    d = D_MODEL
    ncol = N_MOD * d // NDEV

    small_pack = jnp.concatenate([c.reshape(-1), norm_g.reshape(-1), ssd_conv_w.reshape(-1),
                                  gm_v_g.reshape(-1), gm_v_b.reshape(-1)])[None, :]
    (sp,), _ = _exchange([small_pack], scatter=False, name="gather_small")
    sp = sp[:, 0]
    o = 0
    c_all = sp[:, o:o + d]; o += d
    ng_all = sp[:, o:o + 2 * 6 * 128].reshape(NDEV, 2, 6, 128); o += 2 * 6 * 128
    cw_all = sp[:, o:o + SSD_CONV * 512].reshape(NDEV, SSD_CONV, 512); o += SSD_CONV * 512
    vg_all = sp[:, o:o + 256]; o += 256
    vb_all = sp[:, o:o + 256]; o += 256
    norm_g_full = jnp.transpose(ng_all, (1, 2, 0, 3)).reshape(2, 6, d)
    conv_w_full = jnp.transpose(cw_all, (1, 0, 2)).reshape(SSD_CONV, SSD_CONV_DIM)
    gm_v_g_full = vg_all.reshape(-1)
    gm_v_b_full = vb_all.reshape(-1)

    c16 = jnp.concatenate([c_all, jnp.broadcast_to(c_ctx[None, :], (NDEV, d))], axis=0)
    ada_b_loc = lax.dynamic_slice_in_dim(ada_b, me * ncol, ncol, axis=1)
    mods_loc = jnp.stack([_mm_f32(c16, ada_w[i], name=f"ada_mod{i}", silu_a=True, bias=ada_b_loc[i][None, :])
                          for i in range(2)])
    (mods_all,), mods_done = _exchange([mods_loc], scatter=False, name="gather_mods")

    shard = {"ssd": (ssd_w_in[0], ssd_w_out[0]), "gm": (gm_w_in[0], gm_w_out[0])}
    for i in range(2):
        for j in range(2):
            shard[f"ffn{i}{j}"] = (ffn_w_in[i, j], ffn_w_out[i, j])
    apart = GROUPS[:2]
    units = []
    for grp in GROUPS:
        units += [(grp + "_in", grp, (0,)), (grp + "_out", grp, (1,))] if grp in apart else [(grp, grp, (0, 1))]
    gathers = {}
    started = mods_done
    for unit, grp, idx in units:
        srcs = [(shard[grp][k] + started).astype(BF16) for k in idx]
        st = _exchange_start(srcs, [_landing(s, me) for s in srcs], scatter=False, name="gather_start_" + unit)
        gathers[unit] = st[:4]
        started = st[4]

    def fetch(unit, after):
        return _exchange_wait(*gathers[unit], after, scatter=False, name="gather_wait_" + unit)

    def get_w(grp, after):
        if grp not in apart:
            return _group_mats(grp, fetch(grp, after))
        early = lambda later: _mats_in(grp, fetch(grp + "_in", later)[0])
        late = lambda later: _mats_out(grp, fetch(grp + "_out", later)[0])
        if grp.startswith("ffn"):
            return {grp: dict(early=early, late=late)}
        return dict(early(after), late=late)

    scatters = {}
    held = {}

    def put_grad(grp, which, grad):
        if grp in apart:
            unit, blocks = grp + "_" + which[2:], [_grad_blocks(which, grad)]
        else:
            held[grp, which] = _grad_blocks(which, grad)
            if (grp, "w_in") not in held or (grp, "w_out") not in held:
                return None
            unit, blocks = grp, [held[grp, "w_in"], held[grp, "w_out"]]
        if unit == units[0][0]:
            held[unit] = blocks
            return None
        return send(unit, blocks)

    def send(unit, blocks, follows=None):
        own = [lax.dynamic_index_in_dim(b, me, axis=0, keepdims=False) for b in blocks]
        if follows is not None:
            own = [o_ + follows.astype(o_.dtype) for o_ in own]
        st = _exchange_start(blocks, [_landing(o_, me) for o_ in own], scatter=True, name="scatter_start_" + unit)
        scatters[unit] = st[:4]
        return st[4]

    mods_rows = jnp.transpose(mods_all, (1, 2, 0, 3)).reshape(2, 2 * NDEV, N_MOD * d) + started
    mx = lax.dynamic_index_in_dim(mods_rows, me, axis=1, keepdims=False).reshape(2, N_MOD, d)
    mc = mods_rows[:, NDEV].reshape(2, N_MOD, d)
    mods = [(mc[i], mx[i]) for i in range(2)]

    small = dict(conv_w8=jnp.pad(conv_w_full, ((0, 8 - SSD_CONV), (0, 0))), conv_b=ssd_conv_b, dt_bias=ssd_dt_bias[0],
                 a_log=ssd_A_log[0], ssd_d=ssd_D[0], ssd_norm_g=ssd_norm_g[0], gm_v_g=gm_v_g_full,
                 gm_v_b=gm_v_b_full, gm_w_s=gm_w_s[0], gm_b_s=gm_b_s[0])
    loss_parts, grad_x, g = _local_step(x[0], ctx[0], loss_target[0], mods, norm_g_full, get_w, small, put_grad)
    g["loss"] = (0.5 / d * jnp.sum(loss_parts)).reshape(1)

    whole = {"ffn_w_in": (ffn_w_in, m_ffn_w_in, v_ffn_w_in), "ffn_w_out": (ffn_w_out, m_ffn_w_out, v_ffn_w_out),
             "ssd_w_in": (ssd_w_in, m_ssd_w_in, v_ssd_w_in), "ssd_w_out": (ssd_w_out, m_ssd_w_out, v_ssd_w_out),
             "gm_w_in": (gm_w_in, m_gm_w_in, v_gm_w_in), "gm_w_out": (gm_w_out, m_gm_w_out, v_gm_w_out)}
    res = {}

    def update_units(some, after):
        for unit, grp, idx in some:
            parts = _exchange_wait(*scatters[unit], after, scatter=True, name="scatter_wait_" + unit)
            for k, p in zip(idx, parts):
                which = ("in", "out")[k]
                nm = ("ffn" if grp.startswith("ffn") else grp) + "_w_" + which
                sel = (int(grp[3]), int(grp[4])) if grp.startswith("ffn") else (0,)
                res[nm] = _adamw(p, *whole[nm], name=f"adamw_{grp}_{which}", sel=sel, into=res.get(nm))
                after = res[nm][0]
        return after

    by_send = list(reversed(units))
    early_done = update_units(by_send[:4], grad_x)

    sg_names = ["dmx", "dmc", "norm_g", "ssd_conv_w", "ssd_conv_b", "ssd_dt_bias", "ssd_A_log", "ssd_D", "ssd_norm_g",
                "gm_v_g", "gm_v_b", "gm_w_s", "gm_b_s", "loss"]
    sg_shapes = [g[n].shape for n in sg_names]
    flat = jnp.concatenate([g[n].reshape(-1) for n in sg_names])
    npack = flat.shape[0]
    pad = (-npack) % 1024
    flat = jnp.pad(flat, (0, pad)).reshape(-1, 128)
    flat = flat + _zero_after(early_done, name="after_early_updates")
    (sg_all,), sg_done = _exchange([flat], scatter=False, name="gather_small_grads")
    send(units[0][0], held[units[0][0]], follows=sg_done)
    update_units(by_send[4:], sg_all)
    sg_sum = _sum_slots(sg_all, name="sum_small_grads").reshape(-1)[:npack]
    sums = {}
    o = 0
    for n, shp in zip(sg_names, sg_shapes):
        sz = math.prod(shp)
        sums[n] = sg_sum[o:o + sz].reshape(shp)
        o += sz
    loss = sums["loss"][0]
    per_dev = sg_all.reshape(NDEV, -1)
    dmx_all =per_dev[:, :2 * N_MOD * d].reshape(NDEV, 2, N_MOD * d)
    dmc_all = per_dev[:, 2 * N_MOD * d:4 * N_MOD * d].reshape(NDEV, 2, N_MOD * d)

    (s16,) = _rowwise("ada_silu", lambda cc: ((_silu(cc),), ()), 2 * NDEV, [c16], [], [(d, F32)], tm=2 * NDEV)
    s16_t = s16.T
    g_ada_w, dcc_parts = [], []
    for i in range(2):
        rhs = jnp.concatenate([lax.dynamic_slice_in_dim(dmx_all[:, i], me * ncol, ncol, axis=1),
                               lax.dynamic_slice_in_dim(dmc_all[:, i], me * ncol, ncol, axis=1)], axis=0)
        g_ada_w.append(_mm_f32(s16_t, rhs, name=f"ada_dw{i}"))
        dmc_loc = lax.dynamic_slice_in_dim(sums["dmc"][i], me * ncol, ncol, axis=0)
        rhs_c = jnp.zeros((ncol, 128), F32).at[:, 0].set(dmc_loc)
        dcc_parts.append(_mm_f32(ada_w[i], rhs_c, name=f"ada_dcc{i}")[:, 0])
    g_ada_w = jnp.stack(g_ada_w)
    dcc_part = (dcc_parts[0] + dcc_parts[1]).reshape(8, 128)
    (dcc_all,), _ = _exchange([dcc_part], scatter=False, name="gather_dcc")
    g_c_ctx = _sum_slots(dcc_all, name="sum_dcc", scale_by=c_ctx.reshape(8, 128)).reshape(d)
    g_ada_b = sums["dmx"] + sums["dmc"]

    outs = _adamw(g_ada_w.reshape(1, -1, ncol), ada_w.reshape(-1, ncol), m_ada_w.reshape(-1, ncol),
                  v_ada_w.reshape(-1, ncol), name="adamw_ada_w")
    res["ada_w"] = [o_.reshape(ada_w.shape) for o_ in outs]

    loc = lambda a, ax, n: lax.dynamic_slice_in_dim(a, me * n, n, axis=ax)
    small_g = dict(c_ctx=g_c_ctx, ada_b=g_ada_b, norm_g=loc(sums["norm_g"], 2, 128),
                   ssd_conv_w=loc(sums["ssd_conv_w"], 1, 512)[None], ssd_conv_b=sums["ssd_conv_b"][None],
                   ssd_dt_bias=sums["ssd_dt_bias"][None], ssd_A_log=sums["ssd_A_log"][None], ssd_D=sums["ssd_D"][None],
                   ssd_norm_g=sums["ssd_norm_g"][None], gm_v_g=loc(sums["gm_v_g"], 0, 256)[None],
                   gm_v_b=loc(sums["gm_v_b"], 0, 256)[None], gm_w_s=sums["gm_w_s"][None], gm_b_s=sums["gm_b_s"][None])
    small_w = dict(c_ctx=(c_ctx, m_c_ctx, v_c_ctx), ada_b=(ada_b, m_ada_b, v_ada_b), norm_g=(norm_g, m_norm_g, v_norm_g),
                   ssd_conv_w=(ssd_conv_w, m_ssd_conv_w, v_ssd_conv_w), ssd_conv_b=(ssd_conv_b, m_ssd_conv_b, v_ssd_conv_b),
                   ssd_dt_bias=(ssd_dt_bias, m_ssd_dt_bias, v_ssd_dt_bias), ssd_A_log=(ssd_A_log, m_ssd_A_log, v_ssd_A_log),
                   ssd_D=(ssd_D, m_ssd_D, v_ssd_D), ssd_norm_g=(ssd_norm_g, m_ssd_norm_g, v_ssd_norm_g),
                   gm_v_g=(gm_v_g, m_gm_v_g, v_gm_v_g), gm_v_b=(gm_v_b, m_gm_v_b, v_gm_v_b),
                   gm_w_s=(gm_w_s, m_gm_w_s, v_gm_w_s), gm_b_s=(gm_b_s, m_gm_b_s, v_gm_b_s))
    sn = list(small_w)

    def pack(arrs):
        f = jnp.concatenate([a.reshape(-1) for a in arrs])
        return jnp.pad(f, (0, (-f.shape[0]) % 1024)).reshape(-1, 128)

    pg = pack([small_g[n].reshape(small_w[n][0].shape) for n in sn])
    outs = _adamw(pg[None], pack([small_w[n][0] for n in sn]), pack([small_w[n][1] for n in sn]),
                  pack([small_w[n][2] for n in sn]), name="adamw_small")
    flat_outs = [o_.reshape(-1) for o_ in outs]
    o = 0
    for n in sn:
        shp = small_w[n][0].shape
        sz = math.prod(shp)
        res[n] = [fo[o:o + sz].reshape(shp) for fo in flat_outs]
        o += sz

    order = ["c_ctx", "ada_w", "ada_b", "norm_g", "ffn_w_in", "ffn_w_out", "ssd_w_in", "ssd_conv_w", "ssd_conv_b",
             "ssd_dt_bias", "ssd_A_log", "ssd_D", "ssd_norm_g", "ssd_w_out", "gm_w_in", "gm_v_g", "gm_v_b", "gm_w_s",
             "gm_b_s", "gm_w_out"]
    result = [loss, grad_x[None]]
    for k in range(4):
        result += [res[n][k] for n in order]
    return tuple(result)
```

```python
import functools
import math

import jax
import jax.numpy as jnp
from jax import lax
from jax.experimental import pallas as pl
from jax.experimental.pallas import tpu as pltpu

F32 = jnp.float32
BF16 = jnp.bfloat16

NDEV = 8
D_MODEL = 1024
FFN_DIM = 2816
N_MOD = 9
EPS = 1e-6
SSD_INNER = 2048
SSD_HEADS = 32
SSD_HEAD_DIM = 64
SSD_GROUPS = 8
SSD_HPG = 4
SSD_STATE = 128
SSD_CONV = 5
SSD_CONV_DIM = 4096
CHUNK = 128
GM_INNER = 2048
GM_GROUPS = 8
GM_GROUP_DIM = 256
ADAM_LR = 0.001
ADAM_B1 = 0.9
ADAM_B2 = 0.999
ADAM_EPS = 1e-08
ADAM_WD = 0.01
ADAM_STEP = 10
NEG_BIG = -1e30
VMEM_LIMIT_BYTES = 56 * 1024 * 1024
HI = lax.Precision.HIGHEST


def _params(*sem):
    return pltpu.CompilerParams(dimension_semantics=sem, vmem_limit_bytes=VMEM_LIMIT_BYTES)


def _pick(n, target, mult=16):
    if n <= target:
        return n
    for t in range(target - target % mult, 0, -mult):
        if n % t == 0:
            return t
    raise ValueError((n, target, mult))


def _sig(x):
    return 0.5 * jnp.tanh(0.5 * x) + 0.5


def _silu(x):
    return x * _sig(x)


def _dsilu(x):
    s = _sig(x)
    return s * (1.0 + x * (1.0 - s))


_GELU_C = math.sqrt(2.0 / math.pi)


def _gelu(x):
    return 0.5 * x * (1.0 + jnp.tanh(_GELU_C * (x + 0.044715 * x * x * x)))


def _dgelu(x):
    t = jnp.tanh(_GELU_C * (x + 0.044715 * x * x * x))
    return 0.5 * (1.0 + t) + 0.5 * x * (1.0 - t * t) * _GELU_C * (1.0 + 3.0 * 0.044715 * x * x)


def _softplus(x):
    return jnp.maximum(x, 0.0) + jnp.log1p(jnp.exp(-jnp.abs(x)))


def _sum0(v):
    return jnp.sum(v, axis=0, keepdims=True)


def _rms(h):
    r = lax.rsqrt(jnp.mean(h * h, axis=-1, keepdims=True) + EPS)
    return h * r, r


def _dot(a, b, dims=((1,), (0,)), precision=None):
    return lax.dot_general(a, b, (dims, ((), ())), preferred_element_type=F32, precision=precision)


_NT = ((1,), (1,))
_TN = ((0,), (0,))


def _rowwise(name, fn, n_rows, rows, consts, outs, accs=(), *, tm, nseg=1, seg_blocks=0):
    assert n_rows % tm == 0
    if nseg == 2:
        assert seg_blocks > 0
        seg = lambda i: jnp.where(i < seg_blocks, 0, 1)
    else:
        seg = lambda i: 0
    in_specs, args, lacking = [], [], []
    for r in rows:
        arr, width, cb, off = r if isinstance(r, tuple) else (r, r.shape[1], 0, 0)
        in_specs.append(pl.BlockSpec((tm, width), lambda i, cb=cb, off=off: (jnp.maximum(i + off, 0), cb)))
        args.append(arr)
        lacking.append(-off if off < 0 else 0)
    for kind, arr in consts:
        if kind == "seg":
            assert arr.shape[0] == nseg and arr.shape[1] == 1, arr.shape
            in_specs.append(pl.BlockSpec((None, 1, arr.shape[2]), lambda i: (seg(i), 0, 0)))
        else:
            in_specs.append(pl.BlockSpec(arr.shape, lambda i: (0, 0)))
        args.append(arr)
    out_shape = [jax.ShapeDtypeStruct((n_rows, w), dt) for w, dt in outs]
    out_specs = [pl.BlockSpec((tm, w), lambda i: (i, 0)) for w, _ in outs]
    out_shape += [jax.ShapeDtypeStruct((nseg, 1, w), F32) for w in accs]
    out_specs += [pl.BlockSpec((None, 1, w), lambda i: (seg(i), 0, 0)) for w in accs]
    n_in, n_out, n_acc = len(args), len(outs), len(accs)

    def kern(*refs):
        i = pl.program_id(0)
        ins = [r[...] for r in refs[:n_in]]
        for k, lack in enumerate(lacking):
            if lack:
                ins[k] = jnp.where(i >= lack, ins[k], jnp.zeros_like(ins[k]))
        res, terms = fn(*ins)
        for ref, v in zip(refs[n_in:n_in + n_out], res):
            ref[...] = v.astype(ref.dtype)
        if n_acc:
            sums = [_sum0(v) for v in terms]
            first = (i == 0) | (i == seg_blocks) if nseg == 2 else (i == 0)
            acc_refs = refs[n_in + n_out:]

            @pl.when(first)
            def _():
                for ref, v in zip(acc_refs, sums):
                    ref[...] = v

            @pl.when(jnp.logical_not(first))
            def _():
                for ref, v in zip(acc_refs, sums):
                    ref[...] += v

    res = pl.pallas_call(
        kern, name=name, grid=(n_rows // tm,), in_specs=in_specs, out_specs=out_specs, out_shape=out_shape,
        compiler_params=_params("arbitrary"),
    )(*args)
    return res


def _pre_fwd_fn(h, g, shift, scale):
    hh, _ = _rms(h)
    return (hh * g * (1.0 + scale) + shift,), ()


def _pre_bwd_fn(du, h, dres, g, scale):
    hh, r = _rms(h)
    n = hh * g
    dn = du * (1.0 + scale)
    dhh = dn * g
    dh = dres + r * (dhh - hh * jnp.mean(dhh * hh, axis=-1, keepdims=True))
    return (dh,), (du, du * n, dn * hh)


def _post_fwd_fn(weight, h, y, g, gate):
    yh, _ = _rms(y)
    return (h + weight * gate * (yh * g),), ()


def _out_post_fn(weight, y, h, g, gate):
    return (y,) + _post_fwd_fn(weight, h, y, g, gate)[0], ()


def _post_bwd_fn(weight, dh, y, g, gate):
    yh, r = _rms(y)
    dr = dh * weight
    dyh = dr * gate * g
    dy = r * (dyh - yh * jnp.mean(dyh * yh, axis=-1, keepdims=True))
    return (dy,), (dr * yh * g, dr * gate * yh)


def _glu_bwd_fn(ds, a, b):
    a = a.astype(F32)
    b = b.astype(F32)
    sg = _sig(a)
    da = ds * b * (sg * (1.0 + a * (1.0 - sg)))
    db = ds * (a * sg)
    return (jnp.concatenate([da, db], axis=1),), ()


def _loss_fn(y, t):
    diff = y - t
    return (diff * (1.0 / D_MODEL),), (diff * diff,)


def _ssd_y(yf, yb, xs, z, dvec):
    y = yf + yb + dvec * xs
    return y, y * _silu(z)


def _ssdgate_fwd_fn(yf, yb, xs, z, dvec, ng):
    _, yg = _ssd_y(yf, yb, xs, z, dvec)
    parts = []
    for g in range(SSD_GROUPS):
        sl = slice(g * 256, (g + 1) * 256)
        parts.append(_rms(yg[:, sl])[0])
    return (jnp.concatenate(parts, axis=1) * ng,), ()


def _ssdgate_bwd_fn(dyn, yf, yb, xs, z, dvec, ng):
    y, yg = _ssd_y(yf, yb, xs, z, dvec)
    dyg_parts, ygh_parts = [], []
    for g in range(SSD_GROUPS):
        sl = slice(g * 256, (g + 1) * 256)
        ygh, r = _rms(yg[:, sl])
        d = dyn[:, sl] * ng[:, sl]
        dyg_parts.append(r * (d - ygh * jnp.mean(d * ygh, axis=-1, keepdims=True)))
        ygh_parts.append(ygh)
    dyg = jnp.concatenate(dyg_parts, axis=1)
    ygh = jnp.concatenate(ygh_parts, axis=1)
    dy = dyg * _silu(z)
    dz = dyg * y * _dsilu(z)
    return (dy, dz), (dyn * ygh, dy * xs)


def _ln_stats(v):
    mu = jnp.mean(v, axis=-1, keepdims=True)
    vc = v - mu
    r = lax.rsqrt(jnp.mean(vc * vc, axis=-1, keepdims=True) + EPS)
    return vc * r, r


def _gm_act_fwd_fn(p, vg, vb):
    gu = _gelu(p[:, :GM_INNER])
    gvh, _ = _ln_stats(_gelu(p[:, GM_INNER:]))
    return (gu, gvh * vg + vb), ()


def _gm_act_bwd_fn(p, dgu, dgvn, vg):
    pu = p[:, :GM_INNER]
    pv = p[:, GM_INNER:]
    gvh, r = _ln_stats(_gelu(pv))
    dgvh = dgvn * vg
    dgv = r * (dgvh - jnp.mean(dgvh, axis=-1, keepdims=True) - gvh * jnp.mean(dgvh * gvh, axis=-1, keepdims=True))
    dp = jnp.concatenate([dgu * _dgelu(pu), dgv * _dgelu(pv)], axis=1)
    return (dp,), (dgvn * gvh, dgvn)


def _mm(a, b, *, out_dtype, name, tm=1088, tn=1024, tk=1408, add=None, rhs_t=False, n=None, b_off=(0, 0)):
    m, k = a.shape
    if n is None:
        n, k2 = b.shape if rhs_t else b.shape[::-1]
        assert k == k2
    tm, tn, tk = _pick(m, tm), _pick(n, tn, 128), _pick(k, tk, 128)
    o0, o1 = b_off
    nk = k // tk
    dims = _NT if rhs_t else ((1,), (0,))

    def kern(*refs):
        a_ref, b_ref = refs[:2]
        add_ref = refs[2] if add is not None else None
        o_ref = refs[3] if add is not None else refs[2]

        def finish(r):
            if add is not None:
                r = r + add_ref[...]
            o_ref[...] = r.astype(o_ref.dtype)

        p = _dot(a_ref[...], b_ref[...], dims)
        if nk == 1:
            finish(p)
            return
        acc_ref = refs[-1]
        kk = pl.program_id(2)

        @pl.when(kk == 0)
        def _():
            acc_ref[...] = p

        @pl.when((kk > 0) & (kk < nk - 1))
        def _():
            acc_ref[...] += p

        @pl.when(kk == nk - 1)
        def _():
            finish(acc_ref[...] + p)

    if rhs_t:
        b_spec = pl.BlockSpec((tn, tk), lambda i, j, kk: (j + o0, kk + o1))
    else:
        b_spec = pl.BlockSpec((tk, tn), lambda i, j, kk: (kk + o0, j + o1))
    in_specs = [pl.BlockSpec((tm, tk), lambda i, j, kk: (i, kk)), b_spec]
    args = [a, b]
    if add is not None:
        in_specs.append(pl.BlockSpec((tm, tn), lambda i, j, kk: (i, j)))
        args.append(add)
    return pl.pallas_call(
        kern, name=name, grid=(m // tm, n // tn, nk), in_specs=in_specs,
        out_specs=pl.BlockSpec((tm, tn), lambda i, j, kk: (i, j)),
        out_shape=jax.ShapeDtypeStruct((m, n), out_dtype),
        scratch_shapes=[pltpu.VMEM((tm, tn), F32)] if nk > 1 else [],
        compiler_params=_params("parallel", "parallel", "arbitrary"),
    )(*args)


def _mm_rows(a, b, fn, rows, consts, outs, accs=(), *, name, tm=544, tk=1408, rhs_t=False, n_ctx=0):
    m, k = a.shape
    n = b.shape[0] if rhs_t else b.shape[1]
    tm, tk = _pick(m, tm), _pick(k, tk, 128)
    nk = k // tk
    dims = _NT if rhs_t else ((1,), (0,))
    n_rows, n_const, n_out, n_acc = len(rows), len(consts), len(outs), len(accs)

    def kern(*refs):
        a_ref, b_ref = refs[:2]
        row_refs = refs[2:2 + n_rows]
        const_refs = refs[2 + n_rows:2 + n_rows + n_const]
        out_refs = refs[2 + n_rows + n_const:2 + n_rows + n_const + n_out]
        acc_refs = refs[2 + n_rows + n_const + n_out:2 + n_rows + n_const + n_out + n_acc]
        i, kk = pl.program_id(0), pl.program_id(1)

        def finish(p, rs=slice(None), r0=0):
            nr = p.shape[0]
            is_ctx = (i * tm + r0 + lax.broadcasted_iota(jnp.int32, (nr, 1), 0)) < n_ctx
            cvals = []
            for (kind, arr), ref in zip(consts, const_refs):
                if kind == "seg":
                    cvals.append(jnp.where(is_ctx, ref[0], ref[1]) if arr.shape[0] == 2 else ref[0])
                else:
                    cvals.append(ref[...])
            res, terms = fn(p, *[r[rs, :] for r in row_refs], *cvals)
            for ref, v in zip(out_refs, res):
                ref[rs, :] = v.astype(ref.dtype)
            for ref, v in zip(acc_refs, terms):
                s_all = _sum0(v)
                s_ctx = _sum0(jnp.where(is_ctx, v, 0.0)) if n_ctx else jnp.zeros_like(s_all)
                both = jnp.concatenate([s_ctx, s_all - s_ctx], axis=0)[:, None, :]

                @pl.when(i == 0)
                def _():
                    ref[...] = both

                @pl.when(i > 0)
                def _():
                    ref[...] += both

        if nk == 1 and n_acc == 0:
            nsub = next(s for s in (4, 2, 1) if tm % (16 * s) == 0)
            sub = tm // nsub
            for r in range(nsub):
                rs = slice(r * sub, (r + 1) * sub)
                finish(_dot(a_ref[rs, :], b_ref[...], dims), rs, r * sub)
            return
        p = _dot(a_ref[...], b_ref[...], dims)
        if nk == 1:
            finish(p)
            return
        scr = refs[-1]

        @pl.when(kk == 0)
        def _():
            scr[...] = p

        @pl.when((kk > 0) & (kk < nk - 1))
        def _():
            scr[...] += p

        @pl.when(kk == nk - 1)
        def _():
            finish(scr[...] + p)

    b_spec = pl.BlockSpec((n, tk), lambda i, kk: (0, kk)) if rhs_t else pl.BlockSpec((tk, n), lambda i, kk: (kk, 0))
    in_specs = [pl.BlockSpec((tm, tk), lambda i, kk: (i, kk)), b_spec]
    in_specs += [pl.BlockSpec((tm, r.shape[1]), lambda i, kk: (i, 0)) for r in rows]
    for kind, arr in consts:
        in_specs.append(pl.BlockSpec(arr.shape, (lambda i, kk: (0, 0, 0)) if kind == "seg" else (lambda i, kk: (0, 0))))
    out_shape = [jax.ShapeDtypeStruct((m, w), dt) for w, dt in outs]
    out_specs = [pl.BlockSpec((tm, w), lambda i, kk: (i, 0)) for w, _ in outs]
    out_shape += [jax.ShapeDtypeStruct((2, 1, w), F32) for w in accs]
    out_specs += [pl.BlockSpec((2, 1, w), lambda i, kk: (0, 0, 0)) for w in accs]
    return pl.pallas_call(
        kern, name=name, grid=(m // tm, nk), in_specs=in_specs, out_specs=out_specs, out_shape=out_shape,
        scratch_shapes=[pltpu.VMEM((tm, n), F32)] if nk > 1 else [],
        compiler_params=_params("arbitrary", "arbitrary"),
    )(a, b, *rows, *[arr for _, arr in consts])


def _mm_glu(u, win, *, name, tm=2176, tn=256):
    m, k = u.shape
    n = win.shape[1] // 2
    tm, tn = _pick(m, tm), _pick(n, tn, 128)
    nj = n // tn

    nsub = 4 if tm % 64 == 0 else 1
    sub = tm // nsub

    def kern(u_ref, wa_ref, wb_ref, s_ref, a_ref, b_ref):
        for r in range(nsub):
            rows = slice(r * sub, (r + 1) * sub)
            uu = u_ref[rows, :]
            a = jnp.dot(uu, wa_ref[...], preferred_element_type=F32)
            b = jnp.dot(uu, wb_ref[...], preferred_element_type=F32)
            s_ref[rows, :] = (_silu(a) * b).astype(BF16)
            a_ref[rows, :] = a.astype(BF16)
            b_ref[rows, :] = b.astype(BF16)

    ospec = pl.BlockSpec((tm, tn), lambda i, j: (i, j))
    return pl.pallas_call(
        kern, name=name, grid=(m // tm, nj),
        in_specs=[pl.BlockSpec((tm, k), lambda i, j: (i, 0)), pl.BlockSpec((k, tn), lambda i, j: (0, j)),
                  pl.BlockSpec((k, tn), lambda i, j: (0, nj + j))],
        out_specs=[ospec, ospec, ospec],
        out_shape=[jax.ShapeDtypeStruct((m, n), BF16)] * 3,
        compiler_params=_params("parallel", "parallel"),
    )(u, win, win)


def _mm_tn(a, b, *, name, tm=1024, tn=1024, tk=1088, col_blocks=None):
    t, m = a.shape
    t2, n = b.shape
    assert t == t2
    tm, tn, tk = _pick(m, tm, 128), _pick(n, tn, 128), _pick(t, tk)
    nk = t // tk
    if col_blocks is None:
        def kern(a_ref, b_ref, o_ref):
            kk = pl.program_id(2)

            @pl.when(kk == 0)
            def _():
                o_ref[...] = jnp.zeros_like(o_ref)

            o_ref[...] += _dot(a_ref[...], b_ref[...], _TN)

        out_spec = pl.BlockSpec((tm, tn), lambda i, j, kk: (i, j))
        out_shape = jax.ShapeDtypeStruct((m, n), F32)
        scratch = []
    else:
        wb = n // col_blocks
        per = tn // wb
        assert tn % wb == 0 and wb % 8 == 0

        def kern(a_ref, b_ref, o_ref, acc_ref):
            kk = pl.program_id(2)
            p = _dot(a_ref[...], b_ref[...], _TN)

            @pl.when(kk == 0)
            def _():
                acc_ref[...] = p

            @pl.when((kk > 0) & (kk < nk - 1))
            def _():
                acc_ref[...] += p

            @pl.when(kk == nk - 1)
            def _():
                r = acc_ref[...] + p if nk > 1 else p
                for c in range(per):
                    o_ref[c] = r[:, c * wb:(c + 1) * wb].astype(BF16)

        out_spec = pl.BlockSpec((per, tm, wb), lambda i, j, kk: (j, i, 0))
        out_shape = jax.ShapeDtypeStruct((col_blocks, m, wb), BF16)
        scratch = [pltpu.VMEM((tm, tn), F32)]

    return pl.pallas_call(
        kern, name=name, grid=(m // tm, n // tn, nk),
        in_specs=[pl.BlockSpec((tk, tm), lambda i, j, kk: (kk, i)), pl.BlockSpec((tk, tn), lambda i, j, kk: (kk, j))],
        out_specs=out_spec, out_shape=out_shape, scratch_shapes=scratch,
        compiler_params=_params("parallel", "parallel", "arbitrary"),
    )(a, b)


def _mm_f32(a, b, *, name, silu_a=False, bias=None):
    m, k = a.shape
    n = b.shape[1]

    def kern(*refs):
        if bias is None:
            a_ref, b_ref, o_ref = refs
        else:
            a_ref, b_ref, bias_ref, o_ref = refs
        av = a_ref[...]
        if silu_a:
            av = _silu(av)
        r = jnp.dot(av, b_ref[...], preferred_element_type=F32, precision=HI)
        if bias is not None:
            r = r + bias_ref[...]
        o_ref[...] = r

    args = [a, b] + ([] if bias is None else [bias])
    return pl.pallas_call(kern, name=name, out_shape=jax.ShapeDtypeStruct((m, n), F32),
                          compiler_params=pltpu.CompilerParams(vmem_limit_bytes=VMEM_LIMIT_BYTES))(*args)


CONV_WIN = 32


def _conv_windows(n, n_ctx):
    assert n_ctx % CONV_WIN == 0 and n_ctx >= CONV_WIN and n - n_ctx >= CONV_WIN
    return (0, n_ctx - CONV_WIN // 2, n - CONV_WIN)


def _tap_outside(r0, s, n, n_ctx):
    t = r0 + lax.broadcasted_iota(jnp.int32, (CONV_WIN, 1), 0)
    lo = jnp.where(t < n_ctx, 0, n_ctx)
    hi = jnp.where(t < n_ctx, n_ctx, n)
    return jnp.where((t + s >= lo) & (t + s < hi), 0.0, 1.0)


def _rolled(v, s):
    return v if s == 0 else pltpu.roll(v, (-s) % v.shape[0], 0)


def _conv_fwd(xp, w8, b, *, n_ctx, name, cb=256):
    n, c = xp.shape
    half = SSD_CONV // 2

    def kern(x_ref, w_ref, b_ref, cpre_ref, act_ref):
        x = x_ref[...]
        acc = jnp.zeros_like(x) + b_ref[...]
        rolled = {}
        for k in range(SSD_CONV):
            rolled[k] = _rolled(x, k - half)
            acc = acc + rolled[k] * w_ref[k:k + 1, :]
        cpre_ref[...] = acc
        act_ref[...] = _silu(acc)
        for r0 in _conv_windows(n, n_ctx):
            rows = slice(r0, r0 + CONV_WIN)
            fix = acc[rows]
            for k in range(SSD_CONV):
                if k != half:
                    fix = fix - rolled[k][rows] * w_ref[k:k + 1, :] * _tap_outside(r0, k - half, n, n_ctx)
            cpre_ref[rows, :] = fix
            act_ref[rows, :] = _silu(fix)

    spec = pl.BlockSpec((n, cb), lambda j: (0, j))
    return pl.pallas_call(
        kern, name=name, grid=(c // cb,),
        in_specs=[spec, pl.BlockSpec((8, cb), lambda j: (0, j)), pl.BlockSpec((1, cb), lambda j: (0, j))],
        out_specs=[spec, spec], out_shape=[jax.ShapeDtypeStruct((n, c), F32)] * 2,
        compiler_params=_params("parallel"),
    )(xp, w8, b)


def _conv_bwd(d1, d2, cpre, xp, w8, *, n_ctx, name, cb=128):
    n, c = xp.shape
    half = SSD_CONV // 2

    def kern(d1_ref, d2_ref, cpre_ref, x_ref, w_ref, dx_ref, dw_ref, db_ref):
        g = (d1_ref[...] + d2_ref[...]) * _dsilu(cpre_ref[...])
        x = x_ref[...]
        dx = jnp.zeros_like(g)
        dw_ref[...] = jnp.zeros_like(dw_ref)
        g_rolled = {}
        for k in range(SSD_CONV):
            s = k - half
            g_rolled[k] = _rolled(g, -s)
            dx = dx + g_rolled[k] * w_ref[k:k + 1, :]
            xr = _rolled(x, s)
            dw = _sum0(g * xr)
            if s != 0:
                for r0 in _conv_windows(n, n_ctx):
                    rows = slice(r0, r0 + CONV_WIN)
                    dw = dw - _sum0(g[rows] * xr[rows] * _tap_outside(r0, s, n, n_ctx))
            dw_ref[k:k + 1, :] = dw
        dx_ref[...] = dx.astype(BF16)
        for r0 in _conv_windows(n, n_ctx):
            rows = slice(r0, r0 + CONV_WIN)
            fix = dx[rows]
            for k in range(SSD_CONV):
                if k != half:
                    fix = fix - g_rolled[k][rows] * w_ref[k:k + 1, :] * _tap_outside(r0, half - k, n, n_ctx)
            dx_ref[rows, :] = fix.astype(BF16)
        db_ref[...] = _sum0(g)

    spec = pl.BlockSpec((n, cb), lambda j: (0, j))
    return pl.pallas_call(
        kern, name=name, grid=(c // cb,),
        in_specs=[spec, spec, spec, spec, pl.BlockSpec((8, cb), lambda j: (0, j))],
        out_specs=[spec, pl.BlockSpec((8, cb), lambda j: (0, j)), pl.BlockSpec((1, cb), lambda j: (0, j))],
        out_shape=[jax.ShapeDtypeStruct((n, c), BF16), jax.ShapeDtypeStruct((8, c), F32),
                   jax.ShapeDtypeStruct((1, c), F32)],
        compiler_params=_params("parallel"),
    )(d1, d2, cpre, xp, w8)


def _chunk_of(s, nc, n_ctx_chunks, rev):
    if not rev:
        return s
    return jnp.where(s < n_ctx_chunks, n_ctx_chunks - 1 - s, nc - 1 - (s - n_ctx_chunks))


def _scan_common(dt_raw, dtT_raw, bias_r, bias_c, alog_r, alog_c, rev):
    ii = lax.broadcasted_iota(jnp.int32, (CHUNK, CHUNK), 0)
    jj = lax.broadcasted_iota(jnp.int32, (CHUNK, CHUNK), 1)
    tri = (jj >= ii) if rev else (jj <= ii)
    tri_t = (ii >= jj) if rev else (ii <= jj)
    a_r = -jnp.exp(alog_r)
    a_c = -jnp.exp(alog_c)
    dt = _softplus(dt_raw + bias_r)
    dt_t = _softplus(dtT_raw + bias_c)
    al = dt * a_r
    acum = _dot(tri.astype(F32), al, precision=HI)
    acum_t = _dot(dt_t * a_c, tri_t.astype(F32), precision=HI)
    atot = _sum0(al)
    return tri, tri_t, a_r, dt, acum, acum_t, atot


def _head_spread():
    return jnp.repeat(jnp.eye(SSD_HEADS, dtype=BF16), SSD_HEAD_DIM, axis=1)


def _dot_sel(v, sel):
    hi = v.astype(BF16)
    lo = (v - hi.astype(F32)).astype(BF16)
    return _dot(hi, sel) + _dot(lo, sel)


def _ssd_scan_fwd(xbc, dt_raw, dtT_raw, bias_r, bias_c, alog_r, alog_c, *, rev, n_ctx_chunks, name):
    n = xbc.shape[0]
    nc = n // CHUNK
    cidx = functools.partial(_chunk_of, nc=nc, n_ctx_chunks=n_ctx_chunks, rev=rev)

    def kern(xs_ref, b_ref, c_ref, dt_ref, dtT_ref, br_ref, bc_ref, ar_ref, ac_ref, e_ref, y_ref, hs_ref, h_scr):
        @pl.when(pl.program_id(0) == 0)
        def _():
            h_scr[...] = jnp.zeros_like(h_scr)

        tri, _, _, dt, acum, acum_t, atot = _scan_common(
            dt_ref[...], dtT_ref[...], br_ref[...], bc_ref[...], ar_ref[...], ac_ref[...], rev)
        etot = jnp.exp(atot)
        spread = lambda v: _dot_sel(v, e_ref[...])
        xdt_all = xs_ref[...] * spread(dt)
        eax = spread(jnp.exp(acum))
        xdw_all = xdt_all * spread(jnp.exp(atot - acum))
        hs_ref[...] = h_scr[...]
        for g in range(SSD_GROUPS):
            gs = slice(g * 256, (g + 1) * 256)
            bg = b_ref[:, g * SSD_STATE:(g + 1) * SSD_STATE].astype(BF16)
            cg = c_ref[:, g * SSD_STATE:(g + 1) * SSD_STATE].astype(BF16)
            cb = _dot(cg, bg, _NT)
            h4 = h_scr[gs, :]
            ys = []
            for k in range(SSD_HPG):
                h = g * SSD_HPG + k
                lmat = jnp.exp(jnp.where(tri, acum[:, h:h + 1] - acum_t[h:h + 1, :], NEG_BIG))
                xdt_h = xdt_all[:, h * SSD_HEAD_DIM:(h + 1) * SSD_HEAD_DIM].astype(BF16)
                ys.append(_dot((cb * lmat).astype(BF16), xdt_h))
            y_ref[:, gs] = jnp.concatenate(ys, axis=1) + _dot(cg, h4.astype(BF16), _NT) * eax[:, gs]
            s4 = _dot(xdw_all[:, gs].astype(BF16), bg, _TN)
            for k in range(SSD_HPG):
                h = g * SSD_HPG + k
                rs = slice(h * SSD_HEAD_DIM, (h + 1) * SSD_HEAD_DIM)
                h_scr[rs, :] = h4[k * SSD_HEAD_DIM:(k + 1) * SSD_HEAD_DIM] * etot[:, h:h + 1] + \
                    s4[k * SSD_HEAD_DIM:(k + 1) * SSD_HEAD_DIM]

    nh = SSD_HEADS
    small = lambda shape: pl.BlockSpec(shape, lambda s: (0, 0))
    return pl.pallas_call(
        kern, name=name, grid=(nc,),
        in_specs=[pl.BlockSpec((CHUNK, SSD_INNER), lambda s: (cidx(s), 0)),
                  pl.BlockSpec((CHUNK, 1024), lambda s: (cidx(s), 2)),
                  pl.BlockSpec((CHUNK, 1024), lambda s: (cidx(s), 3)),
                  pl.BlockSpec((CHUNK, nh), lambda s: (cidx(s), 0)),
                  pl.BlockSpec((nh, CHUNK), lambda s: (0, cidx(s))),
                  small((1, nh)), small((nh, 1)), small((1, nh)), small((nh, 1)), small((nh, SSD_INNER))],
        out_specs=[pl.BlockSpec((CHUNK, SSD_INNER), lambda s: (cidx(s), 0)),
                   pl.BlockSpec((None, SSD_INNER, SSD_STATE), lambda s: (s, 0, 0))],
        out_shape=[jax.ShapeDtypeStruct((n, SSD_INNER), F32),
                   jax.ShapeDtypeStruct((nc, SSD_INNER, SSD_STATE), F32)],
        scratch_shapes=[pltpu.VMEM((SSD_INNER, SSD_STATE), F32)],
        compiler_params=_params("arbitrary"),
    )(xbc, xbc, xbc, dt_raw, dtT_raw, bias_r, bias_c, alog_r, alog_c, _head_spread())


def _ssd_scan_bwd(dy, xbc, hs, dt_raw, dtT_raw, bias_r, bias_c, alog_r, alog_c, dvec, *, rev, n_ctx_chunks,
                  direct, name):
    n = xbc.shape[0]
    nc = n // CHUNK
    nh = SSD_HEADS
    step_of = lambda r: nc - 1 - r
    cidx = lambda r: _chunk_of(step_of(r), nc, n_ctx_chunks, rev)

    def kern(dy_ref, xs_ref, b_ref, c_ref, hs_ref, dt_ref, dtT_ref, br_ref, bc_ref, ar_ref, ac_ref, dv_ref,
             e_ref, et_ref, dx_ref, ddt_ref, dal_ref, dbias_ref, dh_scr):
        @pl.when(pl.program_id(0) == 0)
        def _():
            dh_scr[...] = jnp.zeros_like(dh_scr)
            dal_ref[...] = jnp.zeros_like(dal_ref)
            dbias_ref[...] = jnp.zeros_like(dbias_ref)

        tri, tri_t, a_r, dt, acum, acum_t, atot = _scan_common(
            dt_ref[...], dtT_ref[...], br_ref[...], bc_ref[...], ar_ref[...], ac_ref[...], rev)
        etot = jnp.exp(atot)
        spread = lambda v: _dot_sel(v, e_ref[...])
        gather = lambda v: _dot_sel(v, et_ref[...])
        xs_all = xs_ref[...]
        dy_all = dy_ref[...]
        dtx = spread(dt)
        eax = spread(jnp.exp(acum))
        decx = spread(jnp.exp(atot - acum))
        xdt_all = xs_all * dtx
        xdw_all = xdt_all * decx
        dyo_all = dy_all * eax
        lane = lax.broadcasted_iota(jnp.int32, (CHUNK, nh), 1)
        lane1 = lax.broadcasted_iota(jnp.int32, (1, nh), 1)
        sub = lax.broadcasted_iota(jnp.int32, (nh, CHUNK), 0)
        g_rows = jnp.zeros((CHUNK, nh), F32)
        g_cols = jnp.zeros((nh, CHUNK), F32)
        dtot = jnp.zeros((1, nh), F32)
        q_col, q_e, q_dt = [], [], []
        for g in range(SSD_GROUPS):
            gs = slice(g * 256, (g + 1) * 256)
            bg = b_ref[:, g * SSD_STATE:(g + 1) * SSD_STATE].astype(BF16)
            cg = c_ref[:, g * SSD_STATE:(g + 1) * SSD_STATE].astype(BF16)
            cb = _dot(cg, bg, _NT)
            hs4 = hs_ref[gs, :]
            dh4 = dh_scr[gs, :]
            hs4_bf = hs4.astype(BF16)
            dh4_bf = dh4.astype(BF16)
            dy4 = dy_all[:, gs]
            dy4_bf = dy4.astype(BF16)
            xdt4_bf = xdt_all[:, gs].astype(BF16)
            xdw4 = xdw_all[:, gs]
            xdw4_bf = xdw4.astype(BF16)
            dyo4_bf = dyo_all[:, gs].astype(BF16)
            yoff4 = _dot(cg, hs4_bf, _NT) * eax[:, gs]
            dcg = _dot(dyo4_bf, hs4_bf)
            dh_new4 = _dot(dyo4_bf, cg, _TN)
            bdh4 = _dot(bg, dh4_bf, _NT)
            dbg = _dot(xdw4_bf, dh4_bf)
            e4 = xdw4 * bdh4
            q_col.append(dy4 * yoff4 - e4)
            q_e.append(e4)
            hsum = jnp.sum(dh4 * hs4, axis=1, keepdims=True)
            dcb = jnp.zeros((CHUNK, CHUNK), F32)
            dxdts = []
            for k in range(SSD_HPG):
                h = g * SSD_HPG + k
                ks = slice(k * SSD_HEAD_DIM, (k + 1) * SSD_HEAD_DIM)
                lmat = jnp.exp(jnp.where(tri, acum[:, h:h + 1] - acum_t[h:h + 1, :], NEG_BIG))
                mf = cb * lmat
                dm = _dot(dy4_bf[:, ks], xdt4_bf[:, ks], _NT)
                dcb = dcb + dm * lmat
                gmat = dm * mf
                g_rows = g_rows + jnp.where(lane == h, jnp.sum(gmat, axis=1, keepdims=True), 0.0)
                g_cols = g_cols + jnp.where(sub == h, _sum0(gmat), 0.0)
                dxdts.append(_dot(mf.astype(BF16), dy4_bf[:, ks], _TN))
                et = etot[:, h:h + 1]
                dtot = dtot + jnp.where(lane1 == h, _sum0(hsum[ks]) * et, 0.0)
                dh_scr[h * SSD_HEAD_DIM:(h + 1) * SSD_HEAD_DIM, :] = dh4[ks] * et + dh_new4[ks]
            dxdt4 = jnp.concatenate(dxdts, axis=1) + bdh4 * decx[:, gs]
            q_dt.append(dxdt4 * xs_all[:, gs])
            dx4 = dxdt4 * dtx[:, gs]
            if direct:
                dx4 = dx4 + dy4 * dv_ref[:, gs]
            dcb_bf = dcb.astype(BF16)
            dx_ref[:, gs] = dx4
            dx_ref[:, SSD_INNER + g * SSD_STATE:SSD_INNER + (g + 1) * SSD_STATE] = dbg + _dot(dcb_bf, cg, _TN)
            dx_ref[:, SSD_INNER + 1024 + g * SSD_STATE:SSD_INNER + 1024 + (g + 1) * SSD_STATE] = \
                dcg + _dot(dcb_bf, bg)
        e_heads = gather(jnp.concatenate(q_e, axis=1))
        dacum = gather(jnp.concatenate(q_col, axis=1)) + g_rows - g_cols.T
        dal = _dot(tri_t.astype(F32), dacum, precision=HI) + dtot + _sum0(e_heads)
        ddt = gather(jnp.concatenate(q_dt, axis=1)) + dal * a_r
        ddt_raw = ddt * _sig(dt_ref[...] + br_ref[...])
        ddt_ref[...] = ddt_raw
        dal_ref[...] += _sum0(dal * dt) * a_r
        dbias_ref[...] += _sum0(ddt_raw)

    small = lambda shape: pl.BlockSpec(shape, lambda r: (0, 0))
    return pl.pallas_call(
        kern, name=name, grid=(nc,),
        in_specs=[pl.BlockSpec((CHUNK, SSD_INNER), lambda r: (cidx(r), 0)),
                  pl.BlockSpec((CHUNK, SSD_INNER), lambda r: (cidx(r), 0)),
                  pl.BlockSpec((CHUNK, 1024), lambda r: (cidx(r), 2)),
                  pl.BlockSpec((CHUNK, 1024), lambda r: (cidx(r), 3)),
                  pl.BlockSpec((None, SSD_INNER, SSD_STATE), lambda r: (step_of(r), 0, 0)),
                  pl.BlockSpec((CHUNK, nh), lambda r: (cidx(r), 0)),
                  pl.BlockSpec((nh, CHUNK), lambda r: (0, cidx(r))),
                  small((1, nh)), small((nh, 1)), small((1, nh)), small((nh, 1)), small((1, SSD_INNER)),
                  small((nh, SSD_INNER)), small((SSD_INNER, nh))],
        out_specs=[pl.BlockSpec((CHUNK, SSD_CONV_DIM), lambda r: (cidx(r), 0)),
                   pl.BlockSpec((CHUNK, nh), lambda r: (cidx(r), 0)),
                   small((1, nh)), small((1, nh))],
        out_shape=[jax.ShapeDtypeStruct((n, SSD_CONV_DIM), F32), jax.ShapeDtypeStruct((n, nh), F32),
                   jax.ShapeDtypeStruct((1, nh), F32), jax.ShapeDtypeStruct((1, nh), F32)],
        scratch_shapes=[pltpu.VMEM((SSD_INNER, SSD_STATE), F32)],
        compiler_params=_params("arbitrary"),
    )(dy, xbc, xbc, xbc, hs, dt_raw, dtT_raw, bias_r, bias_c, alog_r, alog_c, dvec, _head_spread(),
      _head_spread().T)


def _gm_spatial_fwd(gu, gvn, ws, bst, *, name):
    n = gu.shape[0]

    def kern(gu_ref, gv_ref, ws_ref, bs_ref, o_ref):
        for g in range(GM_GROUPS):
            sl = slice(g * GM_GROUP_DIM, (g + 1) * GM_GROUP_DIM)
            s = _dot(ws_ref[g], gv_ref[:, sl]) + bs_ref[:, g:g + 1]
            o_ref[:, sl] = (gu_ref[:, sl] * s).astype(BF16)

    spec = pl.BlockSpec((CHUNK, GM_INNER), lambda i: (i, 0))
    return pl.pallas_call(
        kern, name=name, grid=(n // CHUNK,),
        in_specs=[spec, spec, pl.BlockSpec(ws.shape, lambda i: (0, 0, 0)), pl.BlockSpec(bst.shape, lambda i: (0, 0))],
        out_specs=spec, out_shape=jax.ShapeDtypeStruct((n, GM_INNER), BF16),
        compiler_params=_params("parallel"),
    )(gu, gvn, ws, bst)


def _gm_spatial_bwd(dt, gu, gvn, ws, wst, bst, *, name):
    n = gu.shape[0]

    def kern(dt_ref, gu_ref, gv_ref, ws_ref, wst_ref, bs_ref, dgu_ref, dgv_ref, dws_ref, dbs_ref):
        @pl.when(pl.program_id(0) == 0)
        def _():
            dws_ref[...] = jnp.zeros_like(dws_ref)
            dbs_ref[...] = jnp.zeros_like(dbs_ref)

        lane = lax.broadcasted_iota(jnp.int32, (CHUNK, GM_GROUPS), 1)
        dbs = jnp.zeros((CHUNK, GM_GROUPS), F32)
        for g in range(GM_GROUPS):
            sl = slice(g * GM_GROUP_DIM, (g + 1) * GM_GROUP_DIM)
            gv = gv_ref[:, sl]
            s = _dot(ws_ref[g], gv) + bs_ref[:, g:g + 1]
            d = dt_ref[:, sl]
            dgu_ref[:, sl] = d * s
            ds = d * gu_ref[:, sl]
            ds_bf = ds.astype(BF16)
            dws_ref[g] += _dot(ds_bf, gv, _NT)
            dgv_ref[:, sl] = _dot(wst_ref[g], ds_bf)
            dbs = dbs + jnp.where(lane == g, jnp.sum(ds, axis=1, keepdims=True), 0.0)
        dbs_ref[...] += dbs

    spec = pl.BlockSpec((CHUNK, GM_INNER), lambda i: (i, 0))
    wspec = pl.BlockSpec(ws.shape, lambda i: (0, 0, 0))
    bspec = pl.BlockSpec(bst.shape, lambda i: (0, 0))
    return pl.pallas_call(
        kern, name=name, grid=(n // CHUNK,),
        in_specs=[spec, spec, spec, wspec, wspec, bspec],
        out_specs=[spec, spec, wspec, bspec],
        out_shape=[jax.ShapeDtypeStruct((n, GM_INNER), F32), jax.ShapeDtypeStruct((n, GM_INNER), F32),
                   jax.ShapeDtypeStruct(ws.shape, F32), jax.ShapeDtypeStruct(bst.shape, F32)],
        compiler_params=_params("arbitrary"),
    )(dt, gu, gvn, ws, wst, bst)


def _adamw(parts, w, m, v, *, name, tm=256, sel=(), into=None):
    ns, r, wd = parts.shape
    tm = _pick(r, tm, 8)
    lead = len(sel)
    assert w.shape[lead:] == (r, wd) and lead == w.ndim - 2

    def kern(*refs):
        p_ref, w_ref, m_ref, v_ref = refs[:4]
        g_ref, d_ref, nm_ref, nv_ref = refs[-4:]
        g = p_ref[0].astype(F32)
        for s in range(1, ns):
            g = g + p_ref[s].astype(F32)
        m2 = ADAM_B1 * m_ref[...] + (1.0 - ADAM_B1) * g
        v2 = ADAM_B2 * v_ref[...] + (1.0 - ADAM_B2) * (g * g)
        m_hat = m2 / (1.0 - ADAM_B1 ** ADAM_STEP)
        v_hat = v2 / (1.0 - ADAM_B2 ** ADAM_STEP)
        g_ref[...] = g
        d_ref[...] = -ADAM_LR * (m_hat / (jnp.sqrt(v_hat) + ADAM_EPS) + ADAM_WD * w_ref[...])
        nm_ref[...] = m2
        nv_ref[...] = v2

    spec = pl.BlockSpec((None,) * lead + (tm, wd), lambda i: tuple(sel) + (i, 0))
    extra, aliases = [], {}
    if into is not None:
        extra = list(into)
        aliases = {4 + k: k for k in range(4)}
    return pl.pallas_call(
        kern, name=name, grid=(r // tm,),
        in_specs=[pl.BlockSpec((ns, tm, wd), lambda i: (0, i, 0)), spec, spec, spec] +
                 [pl.BlockSpec(memory_space=pl.ANY)] * len(extra),
        out_specs=[spec] * 4, out_shape=[jax.ShapeDtypeStruct(w.shape, F32)] * 4,
        input_output_aliases=aliases,
        compiler_params=_params("parallel"),
    )(parts, w, m, v, *extra)


def _zero_after(x, *, name):
    def kern(x_ref, o_ref):
        o_ref[...] = jnp.zeros_like(o_ref)

    return pl.pallas_call(kern, name=name, out_shape=jax.ShapeDtypeStruct((8, 128), F32),
                          in_specs=[pl.BlockSpec(memory_space=pl.ANY)])(x)[0, 0]


def _sum_slots(parts, *, name, scale_by=None, plus=None):
    ns, r, wd = parts.shape

    def kern(*refs):
        p_ref, o_ref = refs[0], refs[-1]
        g = p_ref[0]
        for s in range(1, ns):
            g = g + p_ref[s]
        if scale_by is not None:
            g = g * _dsilu(refs[1][...])
        if plus is not None:
            g = g + refs[-2][...]
        o_ref[...] = g

    args = [parts] + ([] if scale_by is None else [scale_by]) + ([] if plus is None else [plus])
    return pl.pallas_call(kern, name=name, out_shape=jax.ShapeDtypeStruct((r, wd), F32),
                          compiler_params=pltpu.CompilerParams(vmem_limit_bytes=VMEM_LIMIT_BYTES))(*args)


def _mesh_pos():
    x, y, c = lax.axis_index("x"), lax.axis_index("y"), lax.axis_index("c")
    return x, y, c, 4 * x + 2 * y + c


def _flip(x, y, c, f):
    fx, fy, fc = (f >> 2) & 1, (f >> 1) & 1, f & 1
    px = 1 - x if fx else x
    py = 1 - y if fy else y
    pc = 1 - c if fc else c
    return (px, py, pc), 4 * px + 2 * py + pc


_HBM_SPEC = pl.BlockSpec(memory_space=pltpu.HBM)


def _exchange(arrays, *, scatter, name):
    na = len(arrays)
    if scatter:
        out_shape = [jax.ShapeDtypeStruct(a.shape, a.dtype) for a in arrays]
    else:
        out_shape = [jax.ShapeDtypeStruct((NDEV,) + a.shape, a.dtype) for a in arrays]

    out_shape.append(jax.ShapeDtypeStruct((8, 128), F32))

    def body(*refs):
        ins, outs = refs[:na], refs[na:2 * na]
        send_sems, recv_sems, local_sems = refs[2 * na + 1:]
        refs[2 * na][...] = jnp.zeros((8, 128), F32)
        x, y, c, me = _mesh_pos()
        copies = []
        for i in range(na):
            src_own = ins[i].at[me] if scatter else ins[i]
            lc = pltpu.make_async_copy(src_own, outs[i].at[me], local_sems.at[i])
            lc.start()
            copies.append(lc)
        sends = []
        for f in range(1, NDEV):
            peer, pidx = _flip(x, y, c, f)
            for i in range(na):
                k = i * (NDEV - 1) + f - 1
                src = ins[i].at[pidx] if scatter else ins[i]
                cp = pltpu.make_async_remote_copy(
                    src_ref=src, dst_ref=outs[i].at[me], send_sem=send_sems.at[k], recv_sem=recv_sems.at[k],
                    device_id=peer, device_id_type=pl.DeviceIdType.MESH)
                cp.start()
                sends.append(cp)
        for f in range(1, NDEV):
            peer, pidx = _flip(x, y, c, f)
            for i in range(na):
                k = i * (NDEV - 1) + f - 1
                src = ins[i].at[pidx] if scatter else ins[i]
                pltpu.make_async_remote_copy(
                    src_ref=src, dst_ref=outs[i].at[pidx], send_sem=send_sems.at[k], recv_sem=recv_sems.at[k],
                    device_id=peer, device_id_type=pl.DeviceIdType.MESH).wait_recv()
        for cp in sends:
            cp.wait_send()
        for lc in copies:
            lc.wait()

    res = pl.pallas_call(
        body, name=name, out_shape=out_shape, in_specs=[_HBM_SPEC] * na,
        out_specs=[_HBM_SPEC] * na + [pl.BlockSpec(memory_space=pltpu.VMEM)],
        scratch_shapes=[pltpu.SemaphoreType.DMA((na * (NDEV - 1),)), pltpu.SemaphoreType.DMA((na * (NDEV - 1),)),
                        pltpu.SemaphoreType.DMA((na,))],
        compiler_params=pltpu.CompilerParams(has_side_effects=True),
    )(*arrays)
    return res[:na], res[na][0, 0]


_SEM_SPEC = pl.BlockSpec(memory_space=pltpu.SEMAPHORE)
_DATAFLOW = pltpu.SideEffectType.DATAFLOW_SIDE_EFFECTING


def _split_copies(srcs, lands, send_sems, recv_sems, scatter, arriving):
    x, y, c, me = _mesh_pos()
    copies = []
    for i in range(len(srcs)):
        for f in range(1, NDEV):
            peer, pidx = _flip(x, y, c, f)
            k = i * (NDEV - 1) + f - 1
            copies.append(pltpu.make_async_remote_copy(
                src_ref=srcs[i].at[pidx] if scatter else srcs[i], dst_ref=lands[i].at[pidx if arriving else me],
                send_sem=send_sems.at[k], recv_sem=recv_sems.at[k], device_id=peer,
                device_id_type=pl.DeviceIdType.MESH))
    return copies


def _exchange_start(srcs, lands, *, scatter, name):
    na = len(srcs)
    nsem = na * (NDEV - 1)

    def body(*refs):
        ins_src, ins_land = refs[:na], refs[na:2 * na]
        send_sems, recv_sems = refs[2 * na], refs[2 * na + 1]
        token = refs[-1]
        for cp in _split_copies(ins_src, ins_land, send_sems, recv_sems, scatter, False):
            cp.start()
        token[...] = jnp.zeros_like(token)

    thru = [pltpu.HBM(a.shape, a.dtype) for a in list(srcs) + list(lands)]
    res = pl.pallas_call(
        body, name=name,
        out_shape=(pltpu.SemaphoreType.DMA((nsem,)), pltpu.SemaphoreType.DMA((nsem,)), *thru,
                   jax.ShapeDtypeStruct((8, 128), F32)),
        in_specs=[_HBM_SPEC] * (2 * na),
        out_specs=(_SEM_SPEC, _SEM_SPEC, *([_HBM_SPEC] * (2 * na)), pl.BlockSpec(memory_space=pltpu.VMEM)),
        input_output_aliases={i: 2 + i for i in range(2 * na)},
        compiler_params=pltpu.CompilerParams(has_side_effects=_DATAFLOW),
    )(*[pltpu.with_memory_space_constraint(a, pltpu.HBM) for a in list(srcs) + list(lands)])
    send_sems, recv_sems = res[0], res[1]
    return send_sems, recv_sems, res[2:2 + na], res[2 + na:2 + 2 * na], res[-1][0, 0]


def _exchange_wait(send_sems, recv_sems, srcs, lands, after, *, scatter, name):
    na = len(srcs)

    def body(*refs):
        ins_src, ins_land = refs[:na], refs[na:2 * na]
        s_sems, r_sems = refs[2 * na], refs[2 * na + 1]
        for cp in _split_copies(ins_src, ins_land, s_sems, r_sems, scatter, False):
            cp.wait_send()
        for cp in _split_copies(ins_src, ins_land, s_sems, r_sems, scatter, True):
            cp.wait_recv()

    thru = [pltpu.HBM(a.shape, a.dtype) for a in list(srcs) + list(lands)]
    res = pl.pallas_call(
        body, name=name, out_shape=tuple(thru),
        in_specs=[_HBM_SPEC] * (2 * na) + [_SEM_SPEC, _SEM_SPEC, pl.BlockSpec(memory_space=pl.ANY)],
        out_specs=tuple([_HBM_SPEC] * (2 * na)),
        input_output_aliases={i: i for i in range(2 * na)},
        compiler_params=pltpu.CompilerParams(has_side_effects=_DATAFLOW),
    )(*srcs, *lands, send_sems, recv_sems, after)
    return res[na:]


def _landing(block, me):
    buf = lax.empty((NDEV,) + block.shape, block.dtype)
    return lax.dynamic_update_slice_in_dim(buf, block[None], me, axis=0)


def _seg_kw(nseg, n_ctx, tm):
    return dict(nseg=nseg, seg_blocks=(n_ctx // tm if nseg == 2 else 0))


def _ffn_fwd(tag, h, gpre, gpost, shift, scale, gate, w, *, nseg, n_ctx, tm):
    n = h.shape[0]
    kw = _seg_kw(nseg, n_ctx, tm)
    (u,) = _rowwise(tag + "_pre", _pre_fwd_fn, n, [h], [("full", gpre), ("seg", shift), ("seg", scale)],
                    [(D_MODEL, BF16)], tm=tm, **kw)
    if "early" in w:
        w.update(w.pop("early")(u))
    s, a, b = _mm_glu(u, w["win"], name=tag + "_glu")
    if "late" in w:
        w.update(w.pop("late")(s))
    y, ho = _mm_rows(s, w["wout"], functools.partial(_out_post_fn, 0.5), [h], [("full", gpost), ("seg", gate)],
                     [(D_MODEL, F32), (D_MODEL, F32)], name=tag + "_out", tk=FFN_DIM, n_ctx=n_ctx)
    return ho, dict(h=h, u=u, s=s, a=a, b=b, y=y)


def _ffn_bwd(tag, dho, sv, gpre, gpost, scale, gate, w, put, *, nseg, n_ctx, tm):
    n = dho.shape[0]
    kw = _seg_kw(nseg, n_ctx, tm)
    dy, dgate, dgpost = _rowwise(tag + "_postb", functools.partial(_post_bwd_fn, 0.5), n, [dho, sv["y"]],
                                 [("full", gpost), ("seg", gate)], [(D_MODEL, BF16)], [D_MODEL, D_MODEL], tm=tm, **kw)
    tok = put("w_out", _mm_tn(sv["s"], dy, name=tag + "_dwout", tm=1408, tn=1024, col_blocks=1))
    ds = _mm(dy, w["wout"], out_dtype=F32, name=tag + "_ds", tn=1408, rhs_t=True)
    (dp,) = _rowwise(tag + "_glub", _glu_bwd_fn, n, [ds, sv["a"], sv["b"]], [], [(2 * FFN_DIM, BF16)], tm=min(tm, 128))
    tok2 = put("w_in", _mm_tn(sv["u"], dp, name=tag + "_dwin", tn=1408, col_blocks=NDEV))
    for t in (tok, tok2):
        if t is not None:
            gpre = gpre + t
    dh, dshift, dscale, dgpre = _mm_rows(dp, w["win"], _pre_bwd_fn, [sv["h"], dho], [("full", gpre), ("seg", scale)],
                                         [(D_MODEL, F32)], [D_MODEL, D_MODEL, D_MODEL], name=tag + "_du",
                                         rhs_t=True, n_ctx=n_ctx)
    return dh, None, dict(shift=dshift, scale=dscale, gate=dgate, gpre=dgpre, gpost=dgpost)


def _local_step(x, ctx, target, mods, norm_g, get_w, small, put_grad):
    t_len, n_ctx = x.shape[0], ctx.shape[0]
    n0 = t_len + n_ctx
    tm0 = _pick(n_ctx, 256, 8)
    tm1 = _pick(t_len, 256, 8)
    ncc = n_ctx // CHUNK
    g = {}

    def modrow(i, k, nseg):
        mc, mx = mods[i]
        if nseg == 2:
            return jnp.stack([mc[k], mx[k]])[:, None, :]
        return mx[k][None, None, :]

    pending = [None]

    def gvec(i, k):
        v = norm_g[i, k][None, :]
        if pending[0] is not None:
            v = v + pending[0]
            pending[0] = None
        return v

    xc = jnp.concatenate([ctx, x], axis=0)
    L0 = dict(nseg=2, n_ctx=n_ctx, tm=tm0)
    wts = dict(get_w("ffn00", xc))
    h1, sv_f01 = _ffn_fwd("l0f1", xc, gvec(0, 0), gvec(0, 1), modrow(0, 0, 2), modrow(0, 1, 2), modrow(0, 2, 2),
                          wts["ffn00"], **L0)
    kw0 = _seg_kw(2, n_ctx, tm0)
    (um0,) = _rowwise("l0m_pre", _pre_fwd_fn, n0, [h1], [("full", gvec(0, 2)), ("seg", modrow(0, 3, 2)),
                                                         ("seg", modrow(0, 4, 2))], [(D_MODEL, BF16)], tm=tm0, **kw0)
    wts.update(get_w("ssd", um0))
    z = _mm(um0, wts["ssd_win"], out_dtype=F32, name="ssd_z", n=SSD_INNER)
    xbc_pre = _mm(um0, wts["ssd_win"], out_dtype=F32, name="ssd_xbc", n=SSD_CONV_DIM, b_off=(0, SSD_INNER // 1024))
    dtr = _mm(um0, wts["ssd_wdt"], out_dtype=F32, name="ssd_dt")
    cpre, xbc = _conv_fwd(xbc_pre, small["conv_w8"], small["conv_b"], n_ctx=n_ctx, name="ssd_conv")
    nh = SSD_HEADS
    dt_dir = [dtr[:, :nh], dtr[:, nh:2 * nh]]
    dtT_dir = [d.T for d in dt_dir]
    bias_r = [small["dt_bias"][d][None, :] for d in range(2)]
    bias_c = [small["dt_bias"][d][:, None] for d in range(2)]
    alog_r = [small["a_log"][d][None, :] for d in range(2)]
    alog_c = [small["a_log"][d][:, None] for d in range(2)]
    ys, hss = [], []
    for d in range(2):
        yd, hsd = _ssd_scan_fwd(xbc, dt_dir[d], dtT_dir[d], bias_r[d], bias_c[d], alog_r[d], alog_c[d],
                                rev=(d == 1), n_ctx_chunks=ncc, name=f"ssd_scan{d}")
        ys.append(yd)
        hss.append(hsd)
    dvec = jnp.repeat(small["ssd_d"], SSD_HEAD_DIM)[None, :]
    ngv = small["ssd_norm_g"][None, :]
    gate_rows = [ys[0], ys[1], (xbc, SSD_INNER, 0, 0), z]
    lat = lambda r: (r[0], r[1], r[2], ncc) if isinstance(r, tuple) else (r, r.shape[1], 0, ncc)
    (yn,) = _rowwise("ssd_gate", _ssdgate_fwd_fn, t_len, [lat(r) for r in gate_rows],
                     [("full", dvec), ("full", ngv)], [(SSD_INNER, BF16)], tm=CHUNK)
    h1x = h1[n_ctx:]
    L1 = dict(nseg=1, n_ctx=0, tm=tm1)
    if "late" in wts:
        wts.update(wts.pop("late")(yn))
    yo0, h2 = _mm_rows(yn, wts["ssd_wout"], functools.partial(_out_post_fn, 1.0), [h1x],
                       [("full", gvec(0, 3)), ("seg", modrow(0, 5, 1))], [(D_MODEL, F32), (D_MODEL, F32)],
                       name="ssd_out", tk=SSD_INNER)
    wts.update(get_w("ffn01", h2))
    h3, sv_f02 = _ffn_fwd("l0f2", h2, gvec(0, 4), gvec(0, 5), modrow(0, 6, 1), modrow(0, 7, 1), modrow(0, 8, 1),
                          wts["ffn01"], **L1)

    wts.update(get_w("ffn10", h3))
    h4, sv_f11 = _ffn_fwd("l1f1", h3, gvec(1, 0), gvec(1, 1), modrow(1, 0, 1), modrow(1, 1, 1), modrow(1, 2, 1),
                          wts["ffn10"], **L1)
    (um1,) = _rowwise("l1m_pre", _pre_fwd_fn, t_len, [h4], [("full", gvec(1, 2)), ("seg", modrow(1, 3, 1)),
                                                            ("seg", modrow(1, 4, 1))], [(D_MODEL, BF16)], tm=tm1)
    wts.update(get_w("gm", um1))
    p1 = _mm(um1, wts["gm_win"], out_dtype=F32, name="gm_in")
    vg = small["gm_v_g"][None, :]
    vb = small["gm_v_b"][None, :]
    gu, gvn = _rowwise("gm_act", _gm_act_fwd_fn, t_len, [p1], [("full", vg), ("full", vb)],
                       [(GM_INNER, F32), (GM_INNER, BF16)], tm=128)
    ws_bf = small["gm_w_s"].astype(BF16)
    wst_bf = jnp.swapaxes(small["gm_w_s"], 1, 2).astype(BF16)
    bst = small["gm_b_s"].T
    tgm = _gm_spatial_fwd(gu, gvn, ws_bf, bst, name="gm_spatial")
    yo1, h5 = _mm_rows(tgm, wts["gm_wout"], functools.partial(_out_post_fn, 1.0), [h4],
                       [("full", gvec(1, 3)), ("seg", modrow(1, 5, 1))], [(D_MODEL, F32), (D_MODEL, F32)],
                       name="gm_out", tk=GM_INNER)
    wts.update(get_w("ffn11", h5))
    h6, sv_f12 = _ffn_fwd("l1f2", h5, gvec(1, 4), gvec(1, 5), modrow(1, 6, 1), modrow(1, 7, 1), modrow(1, 8, 1),
                          wts["ffn11"], **L1)

    dh, loss_parts = _rowwise("loss", _loss_fn, t_len, [h6, target], [], [(D_MODEL, F32)], [D_MODEL], tm=tm1)

    zero = jnp.zeros((D_MODEL,), F32)
    dmx = [[zero] * N_MOD for _ in range(2)]
    dmc = [[zero] * N_MOD for _ in range(2)]
    dng = [[zero] * 6 for _ in range(2)]

    def put_mod(i, k, acc):
        if acc.shape[0] == 2:
            dmc[i][k] = dmc[i][k] + acc[0, 0]
            dmx[i][k] = dmx[i][k] + acc[1, 0]
        else:
            dmx[i][k] = dmx[i][k] + acc[0, 0]

    def put_g(i, k, acc):
        dng[i][k] = dng[i][k] + jnp.sum(acc[:, 0], axis=0)

    def ffn_back(tag, i, j, dho, sv, w, lay):
        nseg = lay["nseg"]
        base = 0 if j == 0 else 6
        gi = 0 if j == 0 else 4
        dh_in, pending[0], s = _ffn_bwd(tag, dho, sv, gvec(i, gi), gvec(i, gi + 1), modrow(i, base + 1, nseg),
                                        modrow(i, base + 2, nseg), w, functools.partial(put_grad, f"ffn{i}{j}"), **lay)
        put_mod(i, base, s["shift"])
        put_mod(i, base + 1, s["scale"])
        put_mod(i, base + 2, s["gate"])
        put_g(i, gi, s["gpre"])
        put_g(i, gi + 1, s["gpost"])
        return dh_in

    dh = ffn_back("l1f2", 1, 1, dh, sv_f12, wts["ffn11"], L1)
    dyo, dgate, dgp = _rowwise("l1m_postb", functools.partial(_post_bwd_fn, 1.0), t_len, [dh, yo1],
                               [("full", gvec(1, 3)), ("seg", modrow(1, 5, 1))], [(D_MODEL, BF16)],
                               [D_MODEL, D_MODEL], tm=tm1)
    put_mod(1, 5, dgate)
    put_g(1, 3, dgp)
    put_grad("gm", "w_out", _mm_tn(tgm, dyo, name="gm_dwout", tn=1024, col_blocks=1))
    dtg = _mm(dyo, wts["gm_wout"], out_dtype=F32, name="gm_dt", rhs_t=True)
    dgu, dgvn, dws, dbst = _gm_spatial_bwd(dtg, gu, gvn, ws_bf, wst_bf, bst, name="gm_spatialb")
    g["gm_w_s"] = dws
    g["gm_b_s"] = dbst.T
    dp1, dvg, dvb = _rowwise("gm_actb", _gm_act_bwd_fn, t_len, [p1, dgu, dgvn], [("full", vg)],
                             [(2 * GM_INNER, BF16)], [GM_INNER, GM_INNER], tm=128)
    g["gm_v_g"] = dvg[0, 0]
    g["gm_v_b"] = dvb[0, 0]
    pending[0] = put_grad("gm", "w_in", _mm_tn(um1, dp1, name="gm_dwin", tm=1024, col_blocks=NDEV))
    dh, dsh, dsc, dgp = _mm_rows(dp1, wts["gm_win"], _pre_bwd_fn, [h4, dh],
                                 [("full", gvec(1, 2)), ("seg", modrow(1, 4, 1))], [(D_MODEL, F32)],
                                 [D_MODEL, D_MODEL, D_MODEL], name="gm_dum", tk=1024, rhs_t=True)
    put_mod(1, 3, dsh)
    put_mod(1, 4, dsc)
    put_g(1, 2, dgp)
    dh = ffn_back("l1f1", 1, 0, dh, sv_f11, wts["ffn10"], L1)

    dh = ffn_back("l0f2", 0, 1, dh, sv_f02, wts["ffn01"], L1)
    dyo, dgate, dgp = _rowwise("l0m_postb", functools.partial(_post_bwd_fn, 1.0), t_len, [dh, yo0],
                               [("full", gvec(0, 3)), ("seg", modrow(0, 5, 1))], [(D_MODEL, BF16)],
                               [D_MODEL, D_MODEL], tm=tm1)
    put_mod(0, 5, dgate)
    put_g(0, 3, dgp)
    tok = put_grad("ssd", "w_out", _mm_tn(yn, dyo, name="ssd_dwout", tn=1024, col_blocks=1))
    dyn = _mm(dyo, wts["ssd_wout"], out_dtype=F32, name="ssd_dyn", rhs_t=True)
    dy_ssd, dz, dngv, ddv = _rowwise("ssd_gateb", _ssdgate_bwd_fn, n0, [(dyn, SSD_INNER, 0, -ncc)] + gate_rows,
                                     [("full", dvec), ("full", ngv if tok is None else ngv + tok)],
                                     [(SSD_INNER, F32), (SSD_INNER, BF16)],
                                     [SSD_INNER, SSD_INNER], tm=128)
    g["ssd_norm_g"] = dngv[0, 0]
    g["ssd_D"] = jnp.sum(ddv[0, 0].reshape(SSD_HEADS, SSD_HEAD_DIM), axis=1)
    dxbcs, ddts, dalogs, dbiases = [], [], [], []
    for d in range(2):
        dxd, ddtd, dal, dbi = _ssd_scan_bwd(dy_ssd, xbc, hss[d], dt_dir[d], dtT_dir[d], bias_r[d], bias_c[d],
                                            alog_r[d], alog_c[d], dvec, rev=(d == 1), n_ctx_chunks=ncc,
                                            direct=(d == 0), name=f"ssd_scanb{d}")
        dxbcs.append(dxd)
        ddts.append(ddtd)
        dalogs.append(dal[0])
        dbiases.append(dbi[0])
    g["ssd_A_log"] = jnp.stack(dalogs)
    g["ssd_dt_bias"] = jnp.stack(dbiases)
    dxbc_pre, dcw8, dcb = _conv_bwd(dxbcs[0], dxbcs[1], cpre, xbc_pre, small["conv_w8"], n_ctx=n_ctx, name="ssd_convb")
    g["ssd_conv_w"] = dcw8[:SSD_CONV]
    g["ssd_conv_b"] = dcb[0]
    ddt_bf = jnp.concatenate([ddts[0], ddts[1], jnp.zeros((n0, 128 - 2 * nh), F32)], axis=1).astype(BF16)
    dw_ssd_in = jnp.concatenate([
        _mm_tn(um0, dz, name="ssd_dwz", tm=1024),
        _mm_tn(um0, dxbc_pre, name="ssd_dwxbc", tm=1024),
        _mm_tn(um0, ddt_bf, name="ssd_dwdt", tm=1024)[:, :2 * nh]], axis=1)
    pending[0] = put_grad("ssd", "w_in", dw_ssd_in)
    win_ssd = wts["ssd_win"]
    dum0 = _mm(dz, win_ssd, out_dtype=F32, name="ssd_dum_z", tk=1024, rhs_t=True, n=D_MODEL)
    dum0 = _mm(dxbc_pre, win_ssd, out_dtype=F32, name="ssd_dum_x", tk=1024, rhs_t=True, n=D_MODEL,
               b_off=(0, SSD_INNER // 1024), add=dum0)
    dum0 = _mm(ddt_bf, wts["ssd_wdt"], out_dtype=F32, name="ssd_dum_dt", rhs_t=True, add=dum0)
    dh0, dsh, dsc, dgp = _rowwise("l0m_preb", _pre_bwd_fn, n0, [dum0, h1, (dh, D_MODEL, 0, -(n_ctx // tm0))],
                                  [("full", gvec(0, 2)), ("seg", modrow(0, 4, 2))], [(D_MODEL, F32)],
                                  [D_MODEL, D_MODEL, D_MODEL], tm=tm0, **kw0)
    put_mod(0, 3, dsh)
    put_mod(0, 4, dsc)
    put_g(0, 2, dgp)
    dh0 = ffn_back("l0f1", 0, 0, dh0, sv_f01, wts["ffn00"], L0)
    grad_x = dh0[n_ctx:]
    g["norm_g"] = jnp.stack([jnp.stack(r) for r in dng])
    g["dmx"] = jnp.stack([jnp.concatenate(r) for r in dmx])
    g["dmc"] = jnp.stack([jnp.concatenate(r) for r in dmc])
    return loss_parts[0], grad_x, g


GROUPS = ("ffn00", "ssd", "ffn01", "ffn10", "gm", "ffn11")


def _mats_in(group, win_l):
    k, nloc = win_l.shape[1], win_l.shape[2]
    win = jnp.transpose(win_l, (1, 0, 2)).reshape(k, NDEV * nloc)
    if group.startswith("ffn"):
        return dict(win=win)
    if group == "gm":
        return dict(gm_win=win)
    assert group == "ssd"
    c1 = SSD_INNER + SSD_CONV_DIM
    return dict(ssd_win=win, ssd_wdt=jnp.pad(win[:, c1:], ((0, 0), (0, 128 - 2 * SSD_HEADS))))


def _mats_out(group, wout_l):
    pre = "" if group.startswith("ffn") else group + "_"
    return {pre + "wout": wout_l.reshape(-1, wout_l.shape[2])}


def _group_mats(group, lands):
    m = {**_mats_in(group, lands[0]), **_mats_out(group, lands[1])}
    return {group: m} if group.startswith("ffn") else m


def _grad_blocks(which, grad):
    if grad.ndim == 3:
        return grad if which == "w_in" else grad.reshape(NDEV, grad.shape[1] // NDEV, grad.shape[2])
    if which == "w_in":
        k, n = grad.shape
        return jnp.transpose(grad.reshape(k, NDEV, n // NDEV), (1, 0, 2)).astype(BF16)
    return grad.reshape(NDEV, grad.shape[0] // NDEV, grad.shape[1]).astype(BF16)


def kernel(x, c, ctx, c_ctx, ada_w, ada_b, norm_g, ffn_w_in, ffn_w_out, ssd_w_in, ssd_conv_w, ssd_conv_b, ssd_dt_bias, ssd_A_log, ssd_D, ssd_norm_g, ssd_w_out, gm_w_in, gm_v_g, gm_v_b, gm_w_s, gm_b_s, gm_w_out, loss_target, m_c_ctx, m_ada_w, m_ada_b, m_norm_g, m_ffn_w_in, m_ffn_w_out, m_ssd_w_in, m_ssd_conv_w, m_ssd_conv_b, m_ssd_dt_bias, m_ssd_A_log, m_ssd_D, m_ssd_norm_g, m_ssd_w_out, m_gm_w_in, m_gm_v_g, m_gm_v_b, m_gm_w_s, m_gm_b_s, m_gm_w_out, v_c_ctx, v_ada_w, v_ada_b, v_norm_g, v_ffn_w_in, v_ffn_w_out, v_ssd_w_in, v_ssd_conv_w, v_ssd_conv_b, v_ssd_dt_bias, v_ssd_A_log, v_ssd_D, v_ssd_norm_g, v_ssd_w_out, v_gm_w_in, v_gm_v_g, v_gm_v_b, v_gm_w_s, v_gm_b_s, v_gm_w_out):
    me = 4 * lax.axis_index("x") + 2 * lax.axis_index("y") + lax.axis_index("c")
    d = D_MODEL
    ncol = N_MOD * d // NDEV

    small_pack = jnp.concatenate([c.reshape(-1), norm_g.reshape(-1), ssd_conv_w.reshape(-1),
                                  gm_v_g.reshape(-1), gm_v_b.reshape(-1)])[None, :]
    (sp,), _ = _exchange([small_pack], scatter=False, name="gather_small")
    sp = sp[:, 0]
    o = 0
    c_all = sp[:, o:o + d]; o += d
    ng_all = sp[:, o:o + 2 * 6 * 128].reshape(NDEV, 2, 6, 128); o += 2 * 6 * 128
    cw_all = sp[:, o:o + SSD_CONV * 512].reshape(NDEV, SSD_CONV, 512); o += SSD_CONV * 512
    vg_all = sp[:, o:o + 256]; o += 256
    vb_all = sp[:, o:o + 256]; o += 256
    norm_g_full = jnp.transpose(ng_all, (1, 2, 0, 3)).reshape(2, 6, d)
    conv_w_full = jnp.transpose(cw_all, (1, 0, 2)).reshape(SSD_CONV, SSD_CONV_DIM)
    gm_v_g_full = vg_all.reshape(-1)
    gm_v_b_full = vb_all.reshape(-1)

    c16 = jnp.concatenate([c_all, jnp.broadcast_to(c_ctx[None, :], (NDEV, d))], axis=0)
    ada_b_loc = lax.dynamic_slice_in_dim(ada_b, me * ncol, ncol, axis=1)
    mods_loc = jnp.stack([_mm_f32(c16, ada_w[i], name=f"ada_mod{i}", silu_a=True, bias=ada_b_loc[i][None, :])
                          for i in range(2)])
    (mods_all,), mods_done = _exchange([mods_loc], scatter=False, name="gather_mods")

    shard = {"ssd": (ssd_w_in[0], ssd_w_out[0]), "gm": (gm_w_in[0], gm_w_out[0])}
    for i in range(2):
        for j in range(2):
            shard[f"ffn{i}{j}"] = (ffn_w_in[i, j], ffn_w_out[i, j])
    apart = GROUPS[:2]
    units = []
    for grp in GROUPS:
        units += [(grp + "_in", grp, (0,)), (grp + "_out", grp, (1,))] if grp in apart else [(grp, grp, (0, 1))]
    gathers = {}
    started = mods_done
    for unit, grp, idx in units:
        srcs = [(shard[grp][k] + started).astype(BF16) for k in idx]
        st = _exchange_start(srcs, [_landing(s, me) for s in srcs], scatter=False, name="gather_start_" + unit)
        gathers[unit] = st[:4]
        started = st[4]

    def fetch(unit, after):
        return _exchange_wait(*gathers[unit], after, scatter=False, name="gather_wait_" + unit)

    def get_w(grp, after):
        if grp not in apart:
            return _group_mats(grp, fetch(grp, after))
        early = lambda later: _mats_in(grp, fetch(grp + "_in", later)[0])
        late = lambda later: _mats_out(grp, fetch(grp + "_out", later)[0])
        if grp.startswith("ffn"):
            return {grp: dict(early=early, late=late)}
        return dict(early(after), late=late)

    scatters = {}
    held = {}

    def put_grad(grp, which, grad):
        if grp in apart:
            unit, blocks = grp + "_" + which[2:], [_grad_blocks(which, grad)]
        else:
            held[grp, which] = _grad_blocks(which, grad)
            if (grp, "w_in") not in held or (grp, "w_out") not in held:
                return None
            unit, blocks = grp, [held[grp, "w_in"], held[grp, "w_out"]]
        if unit == units[0][0]:
            held[unit] = blocks
            return None
        return send(unit, blocks)

    def send(unit, blocks, follows=None):
        own = [lax.dynamic_index_in_dim(b, me, axis=0, keepdims=False) for b in blocks]
        if follows is not None:
            own = [o_ + follows.astype(o_.dtype) for o_ in own]
        st = _exchange_start(blocks, [_landing(o_, me) for o_ in own], scatter=True, name="scatter_start_" + unit)
        scatters[unit] = st[:4]
        return st[4]

    mods_rows = jnp.transpose(mods_all, (1, 2, 0, 3)).reshape(2, 2 * NDEV, N_MOD * d) + started
    mx = lax.dynamic_index_in_dim(mods_rows, me, axis=1, keepdims=False).reshape(2, N_MOD, d)
    mc = mods_rows[:, NDEV].reshape(2, N_MOD, d)
    mods = [(mc[i], mx[i]) for i in range(2)]

    small = dict(conv_w8=jnp.pad(conv_w_full, ((0, 8 - SSD_CONV), (0, 0))), conv_b=ssd_conv_b, dt_bias=ssd_dt_bias[0],
                 a_log=ssd_A_log[0], ssd_d=ssd_D[0], ssd_norm_g=ssd_norm_g[0], gm_v_g=gm_v_g_full,
                 gm_v_b=gm_v_b_full, gm_w_s=gm_w_s[0], gm_b_s=gm_b_s[0])
    loss_parts, grad_x, g = _local_step(x[0], ctx[0], loss_target[0], mods, norm_g_full, get_w, small, put_grad)
    g["loss"] = (0.5 / d * jnp.sum(loss_parts)).reshape(1)

    whole = {"ffn_w_in": (ffn_w_in, m_ffn_w_in, v_ffn_w_in), "ffn_w_out": (ffn_w_out, m_ffn_w_out, v_ffn_w_out),
             "ssd_w_in": (ssd_w_in, m_ssd_w_in, v_ssd_w_in), "ssd_w_out": (ssd_w_out, m_ssd_w_out, v_ssd_w_out),
             "gm_w_in": (gm_w_in, m_gm_w_in, v_gm_w_in), "gm_w_out": (gm_w_out, m_gm_w_out, v_gm_w_out)}
    res = {}

    def update_units(some, after):
        for unit, grp, idx in some:
            parts = _exchange_wait(*scatters[unit], after, scatter=True, name="scatter_wait_" + unit)
            for k, p in zip(idx, parts):
                which = ("in", "out")[k]
                nm = ("ffn" if grp.startswith("ffn") else grp) + "_w_" + which
                sel = (int(grp[3]), int(grp[4])) if grp.startswith("ffn") else (0,)
                res[nm] = _adamw(p, *whole[nm], name=f"adamw_{grp}_{which}", sel=sel, into=res.get(nm))
                after = res[nm][0]
        return after

    by_send = list(reversed(units))
    early_done = update_units(by_send[:4], grad_x)

    sg_names = ["dmx", "dmc", "norm_g", "ssd_conv_w", "ssd_conv_b", "ssd_dt_bias", "ssd_A_log", "ssd_D", "ssd_norm_g",
                "gm_v_g", "gm_v_b", "gm_w_s", "gm_b_s", "loss"]
    sg_shapes = [g[n].shape for n in sg_names]
    flat = jnp.concatenate([g[n].reshape(-1) for n in sg_names])
    npack = flat.shape[0]
    pad = (-npack) % 1024
    flat = jnp.pad(flat, (0, pad)).reshape(-1, 128)
    flat = flat + _zero_after(early_done, name="after_early_updates")
    (sg_all,), sg_done = _exchange([flat], scatter=False, name="gather_small_grads")
    last_sent = send(units[0][0], held[units[0][0]], follows=sg_done)
    sg_sum = _sum_slots(sg_all, name="sum_small_grads", plus=last_sent.reshape(1, 1)).reshape(-1)[:npack]
    update_units(by_send[4:], sg_sum)
    sums = {}
    o = 0
    for n, shp in zip(sg_names, sg_shapes):
        sz = math.prod(shp)
        sums[n] = sg_sum[o:o + sz].reshape(shp)
        o += sz
    loss = sums["loss"][0]
    per_dev = sg_all.reshape(NDEV, -1)
    dmx_all =per_dev[:, :2 * N_MOD * d].reshape(NDEV, 2, N_MOD * d)
    dmc_all = per_dev[:, 2 * N_MOD * d:4 * N_MOD * d].reshape(NDEV, 2, N_MOD * d)

    (s16,) = _rowwise("ada_silu", lambda cc: ((_silu(cc),), ()), 2 * NDEV, [c16], [], [(d, F32)], tm=2 * NDEV)
    s16_t = s16.T
    g_ada_w, dcc_parts = [], []
    for i in range(2):
        rhs = jnp.concatenate([lax.dynamic_slice_in_dim(dmx_all[:, i], me * ncol, ncol, axis=1),
                               lax.dynamic_slice_in_dim(dmc_all[:, i], me * ncol, ncol, axis=1)], axis=0)
        g_ada_w.append(_mm_f32(s16_t, rhs, name=f"ada_dw{i}"))
        dmc_loc = lax.dynamic_slice_in_dim(sums["dmc"][i], me * ncol, ncol, axis=0)
        rhs_c = jnp.zeros((ncol, 128), F32).at[:, 0].set(dmc_loc)
        dcc_parts.append(_mm_f32(ada_w[i], rhs_c, name=f"ada_dcc{i}")[:, 0])
    g_ada_w = jnp.stack(g_ada_w)
    dcc_part = (dcc_parts[0] + dcc_parts[1]).reshape(8, 128)
    (dcc_all,), _ = _exchange([dcc_part], scatter=False, name="gather_dcc")
    g_c_ctx = _sum_slots(dcc_all, name="sum_dcc", scale_by=c_ctx.reshape(8, 128)).reshape(d)
    g_ada_b = sums["dmx"] + sums["dmc"]

    outs = _adamw(g_ada_w.reshape(1, -1, ncol), ada_w.reshape(-1, ncol), m_ada_w.reshape(-1, ncol),
                  v_ada_w.reshape(-1, ncol), name="adamw_ada_w")
    res["ada_w"] = [o_.reshape(ada_w.shape) for o_ in outs]

    loc = lambda a, ax, n: lax.dynamic_slice_in_dim(a, me * n, n, axis=ax)
    small_g = dict(c_ctx=g_c_ctx, ada_b=g_ada_b, norm_g=loc(sums["norm_g"], 2, 128),
                   ssd_conv_w=loc(sums["ssd_conv_w"], 1, 512)[None], ssd_conv_b=sums["ssd_conv_b"][None],
                   ssd_dt_bias=sums["ssd_dt_bias"][None], ssd_A_log=sums["ssd_A_log"][None], ssd_D=sums["ssd_D"][None],
                   ssd_norm_g=sums["ssd_norm_g"][None], gm_v_g=loc(sums["gm_v_g"], 0, 256)[None],
                   gm_v_b=loc(sums["gm_v_b"], 0, 256)[None], gm_w_s=sums["gm_w_s"][None], gm_b_s=sums["gm_b_s"][None])
    small_w = dict(c_ctx=(c_ctx, m_c_ctx, v_c_ctx), ada_b=(ada_b, m_ada_b, v_ada_b), norm_g=(norm_g, m_norm_g, v_norm_g),
                   ssd_conv_w=(ssd_conv_w, m_ssd_conv_w, v_ssd_conv_w), ssd_conv_b=(ssd_conv_b, m_ssd_conv_b, v_ssd_conv_b),
                   ssd_dt_bias=(ssd_dt_bias, m_ssd_dt_bias, v_ssd_dt_bias), ssd_A_log=(ssd_A_log, m_ssd_A_log, v_ssd_A_log),
                   ssd_D=(ssd_D, m_ssd_D, v_ssd_D), ssd_norm_g=(ssd_norm_g, m_ssd_norm_g, v_ssd_norm_g),
                   gm_v_g=(gm_v_g, m_gm_v_g, v_gm_v_g), gm_v_b=(gm_v_b, m_gm_v_b, v_gm_v_b),
                   gm_w_s=(gm_w_s, m_gm_w_s, v_gm_w_s), gm_b_s=(gm_b_s, m_gm_b_s, v_gm_b_s))
    sn = list(small_w)

    def pack(arrs):
        f = jnp.concatenate([a.reshape(-1) for a in arrs])
        return jnp.pad(f, (0, (-f.shape[0]) % 1024)).reshape(-1, 128)

    pg = pack([small_g[n].reshape(small_w[n][0].shape) for n in sn])
    outs = _adamw(pg[None], pack([small_w[n][0] for n in sn]), pack([small_w[n][1] for n in sn]),
                  pack([small_w[n][2] for n in sn]), name="adamw_small")
    flat_outs = [o_.reshape(-1) for o_ in outs]
    o = 0
    for n in sn:
        shp = small_w[n][0].shape
        sz = math.prod(shp)
        res[n] = [fo[o:o + sz].reshape(shp) for fo in flat_outs]
        o += sz

    order = ["c_ctx", "ada_w", "ada_b", "norm_g", "ffn_w_in", "ffn_w_out", "ssd_w_in", "ssd_conv_w", "ssd_conv_b",
             "ssd_dt_bias", "ssd_A_log", "ssd_D", "ssd_norm_g", "ssd_w_out", "gm_w_in", "gm_v_g", "gm_v_b", "gm_w_s",
             "gm_b_s", "gm_w_out"]
    result = [loss, grad_x[None]]
    for k in range(4):
        result += [res[n][k] for n in order]
    return tuple(result)
```

```python
import functools
import math

import jax
import jax.numpy as jnp
from jax import lax
from jax.experimental import pallas as pl
from jax.experimental.pallas import tpu as pltpu

F32 = jnp.float32
BF16 = jnp.bfloat16

NDEV = 8
D_MODEL = 1024
FFN_DIM = 2816
N_MOD = 9
EPS = 1e-6
SSD_INNER = 2048
SSD_HEADS = 32
SSD_HEAD_DIM = 64
SSD_GROUPS = 8
SSD_HPG = 4
SSD_STATE = 128
SSD_CONV = 5
SSD_CONV_DIM = 4096
CHUNK = 128
GM_INNER = 2048
GM_GROUPS = 8
GM_GROUP_DIM = 256
ADAM_LR = 0.001
ADAM_B1 = 0.9
ADAM_B2 = 0.999
ADAM_EPS = 1e-08
ADAM_WD = 0.01
ADAM_STEP = 10
NEG_BIG = -1e30
VMEM_LIMIT_BYTES = 56 * 1024 * 1024
HI = lax.Precision.HIGHEST


def _params(*sem):
    return pltpu.CompilerParams(dimension_semantics=sem, vmem_limit_bytes=VMEM_LIMIT_BYTES)


def _pick(n, target, mult=16):
    if n <= target:
        return n
    for t in range(target - target % mult, 0, -mult):
        if n % t == 0:
            return t
    raise ValueError((n, target, mult))


def _sig(x):
    return 0.5 * jnp.tanh(0.5 * x) + 0.5


def _silu(x):
    return x * _sig(x)


def _dsilu(x):
    s = _sig(x)
    return s * (1.0 + x * (1.0 - s))


_GELU_C = math.sqrt(2.0 / math.pi)


def _gelu(x):
    return 0.5 * x * (1.0 + jnp.tanh(_GELU_C * (x + 0.044715 * x * x * x)))


def _dgelu(x):
    t = jnp.tanh(_GELU_C * (x + 0.044715 * x * x * x))
    return 0.5 * (1.0 + t) + 0.5 * x * (1.0 - t * t) * _GELU_C * (1.0 + 3.0 * 0.044715 * x * x)


def _softplus(x):
    return jnp.maximum(x, 0.0) + jnp.log1p(jnp.exp(-jnp.abs(x)))


def _sum0(v):
    return jnp.sum(v, axis=0, keepdims=True)


def _rms(h):
    r = lax.rsqrt(jnp.mean(h * h, axis=-1, keepdims=True) + EPS)
    return h * r, r


def _dot(a, b, dims=((1,), (0,)), precision=None):
    return lax.dot_general(a, b, (dims, ((), ())), preferred_element_type=F32, precision=precision)


_NT = ((1,), (1,))
_TN = ((0,), (0,))


def _rowwise(name, fn, n_rows, rows, consts, outs, accs=(), *, tm, nseg=1, seg_blocks=0):
    assert n_rows % tm == 0
    if nseg == 2:
        assert seg_blocks > 0
        seg = lambda i: jnp.where(i < seg_blocks, 0, 1)
    else:
        seg = lambda i: 0
    in_specs, args, lacking = [], [], []
    for r in rows:
        arr, width, cb, off = r if isinstance(r, tuple) else (r, r.shape[1], 0, 0)
        in_specs.append(pl.BlockSpec((tm, width), lambda i, cb=cb, off=off: (jnp.maximum(i + off, 0), cb)))
        args.append(arr)
        lacking.append(-off if off < 0 else 0)
    for kind, arr in consts:
        if kind == "seg":
            assert arr.shape[0] == nseg and arr.shape[1] == 1, arr.shape
            in_specs.append(pl.BlockSpec((None, 1, arr.shape[2]), lambda i: (seg(i), 0, 0)))
        else:
            in_specs.append(pl.BlockSpec(arr.shape, lambda i: (0, 0)))
        args.append(arr)
    out_shape = [jax.ShapeDtypeStruct((n_rows, w), dt) for w, dt in outs]
    out_specs = [pl.BlockSpec((tm, w), lambda i: (i, 0)) for w, _ in outs]
    out_shape += [jax.ShapeDtypeStruct((nseg, 1, w), F32) for w in accs]
    out_specs += [pl.BlockSpec((None, 1, w), lambda i: (seg(i), 0, 0)) for w in accs]
    n_in, n_out, n_acc = len(args), len(outs), len(accs)

    def kern(*refs):
        i = pl.program_id(0)
        ins = [r[...] for r in refs[:n_in]]
        for k, lack in enumerate(lacking):
            if lack:
                ins[k] = jnp.where(i >= lack, ins[k], jnp.zeros_like(ins[k]))
        res, terms = fn(*ins)
        for ref, v in zip(refs[n_in:n_in + n_out], res):
            ref[...] = v.astype(ref.dtype)
        if n_acc:
            sums = [_sum0(v) for v in terms]
            first = (i == 0) | (i == seg_blocks) if nseg == 2 else (i == 0)
            acc_refs = refs[n_in + n_out:]

            @pl.when(first)
            def _():
                for ref, v in zip(acc_refs, sums):
                    ref[...] = v

            @pl.when(jnp.logical_not(first))
            def _():
                for ref, v in zip(acc_refs, sums):
                    ref[...] += v

    res = pl.pallas_call(
        kern, name=name, grid=(n_rows // tm,), in_specs=in_specs, out_specs=out_specs, out_shape=out_shape,
        compiler_params=_params("arbitrary"),
    )(*args)
    return res


def _pre_fwd_fn(h, g, shift, scale):
    hh, _ = _rms(h)
    return (hh * g * (1.0 + scale) + shift,), ()


def _pre_bwd_fn(du, h, dres, g, scale):
    hh, r = _rms(h)
    n = hh * g
    dn = du * (1.0 + scale)
    dhh = dn * g
    dh = dres + r * (dhh - hh * jnp.mean(dhh * hh, axis=-1, keepdims=True))
    return (dh,), (du, du * n, dn * hh)


def _post_fwd_fn(weight, h, y, g, gate):
    yh, _ = _rms(y)
    return (h + weight * gate * (yh * g),), ()


def _out_post_fn(weight, y, h, g, gate):
    return (y,) + _post_fwd_fn(weight, h, y, g, gate)[0], ()


def _post_bwd_fn(weight, dh, y, g, gate):
    yh, r = _rms(y)
    dr = dh * weight
    dyh = dr * gate * g
    dy = r * (dyh - yh * jnp.mean(dyh * yh, axis=-1, keepdims=True))
    return (dy,), (dr * yh * g, dr * gate * yh)


def _glu_bwd_fn(ds, a, b):
    a = a.astype(F32)
    b = b.astype(F32)
    sg = _sig(a)
    da = ds * b * (sg * (1.0 + a * (1.0 - sg)))
    db = ds * (a * sg)
    return (jnp.concatenate([da, db], axis=1),), ()


def _loss_fn(y, t):
    diff = y - t
    return (diff * (1.0 / D_MODEL),), (diff * diff,)


def _ssd_y(yf, yb, xs, z, dvec):
    y = yf + yb + dvec * xs
    return y, y * _silu(z)


def _ssdgate_fwd_fn(yf, yb, xs, z, dvec, ng):
    _, yg = _ssd_y(yf, yb, xs, z, dvec)
    parts = []
    for g in range(SSD_GROUPS):
        sl = slice(g * 256, (g + 1) * 256)
        parts.append(_rms(yg[:, sl])[0])
    return (jnp.concatenate(parts, axis=1) * ng,), ()


def _ssdgate_bwd_fn(dyn, yf, yb, xs, z, dvec, ng):
    y, yg = _ssd_y(yf, yb, xs, z, dvec)
    dyg_parts, ygh_parts = [], []
    for g in range(SSD_GROUPS):
        sl = slice(g * 256, (g + 1) * 256)
        ygh, r = _rms(yg[:, sl])
        d = dyn[:, sl] * ng[:, sl]
        dyg_parts.append(r * (d - ygh * jnp.mean(d * ygh, axis=-1, keepdims=True)))
        ygh_parts.append(ygh)
    dyg = jnp.concatenate(dyg_parts, axis=1)
    ygh = jnp.concatenate(ygh_parts, axis=1)
    dy = dyg * _silu(z)
    dz = dyg * y * _dsilu(z)
    return (dy, dz), (dyn * ygh, dy * xs)


def _ln_stats(v):
    mu = jnp.mean(v, axis=-1, keepdims=True)
    vc = v - mu
    r = lax.rsqrt(jnp.mean(vc * vc, axis=-1, keepdims=True) + EPS)
    return vc * r, r


def _gm_act_fwd_fn(p, vg, vb):
    gu = _gelu(p[:, :GM_INNER])
    gvh, _ = _ln_stats(_gelu(p[:, GM_INNER:]))
    return (gu, gvh * vg + vb), ()


def _gm_act_bwd_fn(p, dgu, dgvn, vg):
    pu = p[:, :GM_INNER]
    pv = p[:, GM_INNER:]
    gvh, r = _ln_stats(_gelu(pv))
    dgvh = dgvn * vg
    dgv = r * (dgvh - jnp.mean(dgvh, axis=-1, keepdims=True) - gvh * jnp.mean(dgvh * gvh, axis=-1, keepdims=True))
    dp = jnp.concatenate([dgu * _dgelu(pu), dgv * _dgelu(pv)], axis=1)
    return (dp,), (dgvn * gvh, dgvn)


def _mm(a, b, *, out_dtype, name, tm=1088, tn=1024, tk=1408, add=None, rhs_t=False, n=None, b_off=(0, 0)):
    m, k = a.shape
    if n is None:
        n, k2 = b.shape if rhs_t else b.shape[::-1]
        assert k == k2
    tm, tn, tk = _pick(m, tm), _pick(n, tn, 128), _pick(k, tk, 128)
    o0, o1 = b_off
    nk = k // tk
    dims = _NT if rhs_t else ((1,), (0,))

    def kern(*refs):
        a_ref, b_ref = refs[:2]
        add_ref = refs[2] if add is not None else None
        o_ref = refs[3] if add is not None else refs[2]

        def finish(r):
            if add is not None:
                r = r + add_ref[...]
            o_ref[...] = r.astype(o_ref.dtype)

        p = _dot(a_ref[...], b_ref[...], dims)
        if nk == 1:
            finish(p)
            return
        acc_ref = refs[-1]
        kk = pl.program_id(2)

        @pl.when(kk == 0)
        def _():
            acc_ref[...] = p

        @pl.when((kk > 0) & (kk < nk - 1))
        def _():
            acc_ref[...] += p

        @pl.when(kk == nk - 1)
        def _():
            finish(acc_ref[...] + p)

    if rhs_t:
        b_spec = pl.BlockSpec((tn, tk), lambda i, j, kk: (j + o0, kk + o1))
    else:
        b_spec = pl.BlockSpec((tk, tn), lambda i, j, kk: (kk + o0, j + o1))
    in_specs = [pl.BlockSpec((tm, tk), lambda i, j, kk: (i, kk)), b_spec]
    args = [a, b]
    if add is not None:
        in_specs.append(pl.BlockSpec((tm, tn), lambda i, j, kk: (i, j)))
        args.append(add)
    return pl.pallas_call(
        kern, name=name, grid=(m // tm, n // tn, nk), in_specs=in_specs,
        out_specs=pl.BlockSpec((tm, tn), lambda i, j, kk: (i, j)),
        out_shape=jax.ShapeDtypeStruct((m, n), out_dtype),
        scratch_shapes=[pltpu.VMEM((tm, tn), F32)] if nk > 1 else [],
        compiler_params=_params("parallel", "parallel", "arbitrary"),
    )(*args)


def _mm_rows(a, b, fn, rows, consts, outs, accs=(), *, name, tm=544, tk=1408, rhs_t=False, n_ctx=0):
    m, k = a.shape
    n = b.shape[0] if rhs_t else b.shape[1]
    tm, tk = _pick(m, tm), _pick(k, tk, 128)
    nk = k // tk
    dims = _NT if rhs_t else ((1,), (0,))
    n_rows, n_const, n_out, n_acc = len(rows), len(consts), len(outs), len(accs)

    def kern(*refs):
        a_ref, b_ref = refs[:2]
        row_refs = refs[2:2 + n_rows]
        const_refs = refs[2 + n_rows:2 + n_rows + n_const]
        out_refs = refs[2 + n_rows + n_const:2 + n_rows + n_const + n_out]
        acc_refs = refs[2 + n_rows + n_const + n_out:2 + n_rows + n_const + n_out + n_acc]
        i, kk = pl.program_id(0), pl.program_id(1)

        def finish(p, rs=slice(None), r0=0):
            nr = p.shape[0]
            is_ctx = (i * tm + r0 + lax.broadcasted_iota(jnp.int32, (nr, 1), 0)) < n_ctx
            cvals = []
            for (kind, arr), ref in zip(consts, const_refs):
                if kind == "seg":
                    cvals.append(jnp.where(is_ctx, ref[0], ref[1]) if arr.shape[0] == 2 else ref[0])
                else:
                    cvals.append(ref[...])
            res, terms = fn(p, *[r[rs, :] for r in row_refs], *cvals)
            for ref, v in zip(out_refs, res):
                ref[rs, :] = v.astype(ref.dtype)
            for ref, v in zip(acc_refs, terms):
                s_all = _sum0(v)
                s_ctx = _sum0(jnp.where(is_ctx, v, 0.0)) if n_ctx else jnp.zeros_like(s_all)
                both = jnp.concatenate([s_ctx, s_all - s_ctx], axis=0)[:, None, :]

                @pl.when(i == 0)
                def _():
                    ref[...] = both

                @pl.when(i > 0)
                def _():
                    ref[...] += both

        if nk == 1 and n_acc == 0:
            nsub = next(s for s in (4, 2, 1) if tm % (16 * s) == 0)
            sub = tm // nsub
            for r in range(nsub):
                rs = slice(r * sub, (r + 1) * sub)
                finish(_dot(a_ref[rs, :], b_ref[...], dims), rs, r * sub)
            return
        p = _dot(a_ref[...], b_ref[...], dims)
        if nk == 1:
            finish(p)
            return
        scr = refs[-1]

        @pl.when(kk == 0)
        def _():
            scr[...] = p

        @pl.when((kk > 0) & (kk < nk - 1))
        def _():
            scr[...] += p

        @pl.when(kk == nk - 1)
        def _():
            finish(scr[...] + p)

    b_spec = pl.BlockSpec((n, tk), lambda i, kk: (0, kk)) if rhs_t else pl.BlockSpec((tk, n), lambda i, kk: (kk, 0))
    in_specs = [pl.BlockSpec((tm, tk), lambda i, kk: (i, kk)), b_spec]
    in_specs += [pl.BlockSpec((tm, r.shape[1]), lambda i, kk: (i, 0)) for r in rows]
    for kind, arr in consts:
        in_specs.append(pl.BlockSpec(arr.shape, (lambda i, kk: (0, 0, 0)) if kind == "seg" else (lambda i, kk: (0, 0))))
    out_shape = [jax.ShapeDtypeStruct((m, w), dt) for w, dt in outs]
    out_specs = [pl.BlockSpec((tm, w), lambda i, kk: (i, 0)) for w, _ in outs]
    out_shape += [jax.ShapeDtypeStruct((2, 1, w), F32) for w in accs]
    out_specs += [pl.BlockSpec((2, 1, w), lambda i, kk: (0, 0, 0)) for w in accs]
    return pl.pallas_call(
        kern, name=name, grid=(m // tm, nk), in_specs=in_specs, out_specs=out_specs, out_shape=out_shape,
        scratch_shapes=[pltpu.VMEM((tm, n), F32)] if nk > 1 else [],
        compiler_params=_params("arbitrary", "arbitrary"),
    )(a, b, *rows, *[arr for _, arr in consts])


def _mm_glu(u, win_t, *, name, tm=2176, tn=256):
    m, k = u.shape
    n = win_t.shape[0] // 2
    tm, tn = _pick(m, tm), _pick(n, tn, 128)
    nj = n // tn

    nsub = 4 if tm % 64 == 0 else 1
    sub = tm // nsub

    def kern(u_ref, wa_ref, wb_ref, s_ref, a_ref, b_ref):
        for r in range(nsub):
            rows = slice(r * sub, (r + 1) * sub)
            uu = u_ref[rows, :]
            a = _dot(uu, wa_ref[...], _NT)
            b = _dot(uu, wb_ref[...], _NT)
            s_ref[rows, :] = (_silu(a) * b).astype(BF16)
            a_ref[rows, :] = a.astype(BF16)
            b_ref[rows, :] = b.astype(BF16)

    ospec = pl.BlockSpec((tm, tn), lambda i, j: (i, j))
    return pl.pallas_call(
        kern, name=name, grid=(m // tm, nj),
        in_specs=[pl.BlockSpec((tm, k), lambda i, j: (i, 0)), pl.BlockSpec((tn, k), lambda i, j: (j, 0)),
                  pl.BlockSpec((tn, k), lambda i, j: (nj + j, 0))],
        out_specs=[ospec, ospec, ospec],
        out_shape=[jax.ShapeDtypeStruct((m, n), BF16)] * 3,
        compiler_params=_params("parallel", "parallel"),
    )(u, win_t, win_t)


def _mm_tn(a, b, *, name, tm=1024, tn=1024, tk=1088, col_blocks=None):
    t, m = a.shape
    t2, n = b.shape
    assert t == t2
    tm, tn, tk = _pick(m, tm, 128), _pick(n, tn, 128), _pick(t, tk)
    nk = t // tk
    if col_blocks is None:
        def kern(a_ref, b_ref, o_ref):
            kk = pl.program_id(2)

            @pl.when(kk == 0)
            def _():
                o_ref[...] = jnp.zeros_like(o_ref)

            o_ref[...] += _dot(a_ref[...], b_ref[...], _TN)

        out_spec = pl.BlockSpec((tm, tn), lambda i, j, kk: (i, j))
        out_shape = jax.ShapeDtypeStruct((m, n), F32)
        scratch = []
    else:
        wb = n // col_blocks
        per = tn // wb
        assert tn % wb == 0 and wb % 8 == 0

        def kern(a_ref, b_ref, o_ref, acc_ref):
            kk = pl.program_id(2)
            p = _dot(a_ref[...], b_ref[...], _TN)

            @pl.when(kk == 0)
            def _():
                acc_ref[...] = p

            @pl.when((kk > 0) & (kk < nk - 1))
            def _():
                acc_ref[...] += p

            @pl.when(kk == nk - 1)
            def _():
                r = acc_ref[...] + p if nk > 1 else p
                for c in range(per):
                    o_ref[c] = r[:, c * wb:(c + 1) * wb].astype(BF16)

        out_spec = pl.BlockSpec((per, tm, wb), lambda i, j, kk: (j, i, 0))
        out_shape = jax.ShapeDtypeStruct((col_blocks, m, wb), BF16)
        scratch = [pltpu.VMEM((tm, tn), F32)]

    return pl.pallas_call(
        kern, name=name, grid=(m // tm, n // tn, nk),
        in_specs=[pl.BlockSpec((tk, tm), lambda i, j, kk: (kk, i)), pl.BlockSpec((tk, tn), lambda i, j, kk: (kk, j))],
        out_specs=out_spec, out_shape=out_shape, scratch_shapes=scratch,
        compiler_params=_params("parallel", "parallel", "arbitrary"),
    )(a, b)


def _mm_f32(a, b, *, name, silu_a=False, bias=None):
    m, k = a.shape
    n = b.shape[1]

    def kern(*refs):
        if bias is None:
            a_ref, b_ref, o_ref = refs
        else:
            a_ref, b_ref, bias_ref, o_ref = refs
        av = a_ref[...]
        if silu_a:
            av = _silu(av)
        r = jnp.dot(av, b_ref[...], preferred_element_type=F32, precision=HI)
        if bias is not None:
            r = r + bias_ref[...]
        o_ref[...] = r

    args = [a, b] + ([] if bias is None else [bias])
    return pl.pallas_call(kern, name=name, out_shape=jax.ShapeDtypeStruct((m, n), F32),
                          compiler_params=pltpu.CompilerParams(vmem_limit_bytes=VMEM_LIMIT_BYTES))(*args)


CONV_WIN = 32


def _conv_windows(n, n_ctx):
    assert n_ctx % CONV_WIN == 0 and n_ctx >= CONV_WIN and n - n_ctx >= CONV_WIN
    return (0, n_ctx - CONV_WIN // 2, n - CONV_WIN)


def _tap_outside(r0, s, n, n_ctx):
    t = r0 + lax.broadcasted_iota(jnp.int32, (CONV_WIN, 1), 0)
    lo = jnp.where(t < n_ctx, 0, n_ctx)
    hi = jnp.where(t < n_ctx, n_ctx, n)
    return jnp.where((t + s >= lo) & (t + s < hi), 0.0, 1.0)


def _rolled(v, s):
    return v if s == 0 else pltpu.roll(v, (-s) % v.shape[0], 0)


def _conv_fwd(xp, w8, b, *, n_ctx, name, cb=256):
    n, c = xp.shape
    half = SSD_CONV // 2

    def kern(x_ref, w_ref, b_ref, cpre_ref, act_ref):
        x = x_ref[...]
        acc = jnp.zeros_like(x) + b_ref[...]
        rolled = {}
        for k in range(SSD_CONV):
            rolled[k] = _rolled(x, k - half)
            acc = acc + rolled[k] * w_ref[k:k + 1, :]
        cpre_ref[...] = acc
        act_ref[...] = _silu(acc)
        for r0 in _conv_windows(n, n_ctx):
            rows = slice(r0, r0 + CONV_WIN)
            fix = acc[rows]
            for k in range(SSD_CONV):
                if k != half:
                    fix = fix - rolled[k][rows] * w_ref[k:k + 1, :] * _tap_outside(r0, k - half, n, n_ctx)
            cpre_ref[rows, :] = fix
            act_ref[rows, :] = _silu(fix)

    spec = pl.BlockSpec((n, cb), lambda j: (0, j))
    return pl.pallas_call(
        kern, name=name, grid=(c // cb,),
        in_specs=[spec, pl.BlockSpec((8, cb), lambda j: (0, j)), pl.BlockSpec((1, cb), lambda j: (0, j))],
        out_specs=[spec, spec], out_shape=[jax.ShapeDtypeStruct((n, c), F32)] * 2,
        compiler_params=_params("parallel"),
    )(xp, w8, b)


def _conv_bwd(d1, d2, cpre, xp, w8, *, n_ctx, name, cb=128):
    n, c = xp.shape
    half = SSD_CONV // 2

    def kern(d1_ref, d2_ref, cpre_ref, x_ref, w_ref, dx_ref, dw_ref, db_ref):
        g = (d1_ref[...] + d2_ref[...]) * _dsilu(cpre_ref[...])
        x = x_ref[...]
        dx = jnp.zeros_like(g)
        dw_ref[...] = jnp.zeros_like(dw_ref)
        g_rolled = {}
        for k in range(SSD_CONV):
            s = k - half
            g_rolled[k] = _rolled(g, -s)
            dx = dx + g_rolled[k] * w_ref[k:k + 1, :]
            xr = _rolled(x, s)
            dw = _sum0(g * xr)
            if s != 0:
                for r0 in _conv_windows(n, n_ctx):
                    rows = slice(r0, r0 + CONV_WIN)
                    dw = dw - _sum0(g[rows] * xr[rows] * _tap_outside(r0, s, n, n_ctx))
            dw_ref[k:k + 1, :] = dw
        dx_ref[...] = dx.astype(BF16)
        for r0 in _conv_windows(n, n_ctx):
            rows = slice(r0, r0 + CONV_WIN)
            fix = dx[rows]
            for k in range(SSD_CONV):
                if k != half:
                    fix = fix - g_rolled[k][rows] * w_ref[k:k + 1, :] * _tap_outside(r0, half - k, n, n_ctx)
            dx_ref[rows, :] = fix.astype(BF16)
        db_ref[...] = _sum0(g)

    spec = pl.BlockSpec((n, cb), lambda j: (0, j))
    return pl.pallas_call(
        kern, name=name, grid=(c // cb,),
        in_specs=[spec, spec, spec, spec, pl.BlockSpec((8, cb), lambda j: (0, j))],
        out_specs=[spec, pl.BlockSpec((8, cb), lambda j: (0, j)), pl.BlockSpec((1, cb), lambda j: (0, j))],
        out_shape=[jax.ShapeDtypeStruct((n, c), BF16), jax.ShapeDtypeStruct((8, c), F32),
                   jax.ShapeDtypeStruct((1, c), F32)],
        compiler_params=_params("parallel"),
    )(d1, d2, cpre, xp, w8)


def _chunk_of(s, nc, n_ctx_chunks, rev):
    if not rev:
        return s
    return jnp.where(s < n_ctx_chunks, n_ctx_chunks - 1 - s, nc - 1 - (s - n_ctx_chunks))


def _scan_common(dt_raw, dtT_raw, bias_r, bias_c, alog_r, alog_c, rev):
    ii = lax.broadcasted_iota(jnp.int32, (CHUNK, CHUNK), 0)
    jj = lax.broadcasted_iota(jnp.int32, (CHUNK, CHUNK), 1)
    tri = (jj >= ii) if rev else (jj <= ii)
    tri_t = (ii >= jj) if rev else (ii <= jj)
    a_r = -jnp.exp(alog_r)
    a_c = -jnp.exp(alog_c)
    dt = _softplus(dt_raw + bias_r)
    dt_t = _softplus(dtT_raw + bias_c)
    al = dt * a_r
    acum = _dot(tri.astype(F32), al, precision=HI)
    acum_t = _dot(dt_t * a_c, tri_t.astype(F32), precision=HI)
    atot = _sum0(al)
    return tri, tri_t, a_r, dt, acum, acum_t, atot


def _head_spread():
    return jnp.repeat(jnp.eye(SSD_HEADS, dtype=BF16), SSD_HEAD_DIM, axis=1)


def _dot_sel(v, sel):
    hi = v.astype(BF16)
    lo = (v - hi.astype(F32)).astype(BF16)
    return _dot(hi, sel) + _dot(lo, sel)


def _ssd_scan_fwd(xbc, dt_raw, dtT_raw, bias_r, bias_c, alog_r, alog_c, *, rev, n_ctx_chunks, name):
    n = xbc.shape[0]
    nc = n // CHUNK
    cidx = functools.partial(_chunk_of, nc=nc, n_ctx_chunks=n_ctx_chunks, rev=rev)

    def kern(xs_ref, b_ref, c_ref, dt_ref, dtT_ref, br_ref, bc_ref, ar_ref, ac_ref, e_ref, y_ref, hs_ref, h_scr):
        @pl.when(pl.program_id(0) == 0)
        def _():
            h_scr[...] = jnp.zeros_like(h_scr)

        tri, _, _, dt, acum, acum_t, atot = _scan_common(
            dt_ref[...], dtT_ref[...], br_ref[...], bc_ref[...], ar_ref[...], ac_ref[...], rev)
        etot = jnp.exp(atot)
        spread = lambda v: _dot_sel(v, e_ref[...])
        xdt_all = xs_ref[...] * spread(dt)
        eax = spread(jnp.exp(acum))
        xdw_all = xdt_all * spread(jnp.exp(atot - acum))
        hs_ref[...] = h_scr[...]
        for g in range(SSD_GROUPS):
            gs = slice(g * 256, (g + 1) * 256)
            bg = b_ref[:, g * SSD_STATE:(g + 1) * SSD_STATE].astype(BF16)
            cg = c_ref[:, g * SSD_STATE:(g + 1) * SSD_STATE].astype(BF16)
            cb = _dot(cg, bg, _NT)
            h4 = h_scr[gs, :]
            ys = []
            for k in range(SSD_HPG):
                h = g * SSD_HPG + k
                lmat = jnp.exp(jnp.where(tri, acum[:, h:h + 1] - acum_t[h:h + 1, :], NEG_BIG))
                xdt_h = xdt_all[:, h * SSD_HEAD_DIM:(h + 1) * SSD_HEAD_DIM].astype(BF16)
                ys.append(_dot((cb * lmat).astype(BF16), xdt_h))
            y_ref[:, gs] = jnp.concatenate(ys, axis=1) + _dot(cg, h4.astype(BF16), _NT) * eax[:, gs]
            s4 = _dot(xdw_all[:, gs].astype(BF16), bg, _TN)
            for k in range(SSD_HPG):
                h = g * SSD_HPG + k
                rs = slice(h * SSD_HEAD_DIM, (h + 1) * SSD_HEAD_DIM)
                h_scr[rs, :] = h4[k * SSD_HEAD_DIM:(k + 1) * SSD_HEAD_DIM] * etot[:, h:h + 1] + \
                    s4[k * SSD_HEAD_DIM:(k + 1) * SSD_HEAD_DIM]

    nh = SSD_HEADS
    small = lambda shape: pl.BlockSpec(shape, lambda s: (0, 0))
    return pl.pallas_call(
        kern, name=name, grid=(nc,),
        in_specs=[pl.BlockSpec((CHUNK, SSD_INNER), lambda s: (cidx(s), 0)),
                  pl.BlockSpec((CHUNK, 1024), lambda s: (cidx(s), 2)),
                  pl.BlockSpec((CHUNK, 1024), lambda s: (cidx(s), 3)),
                  pl.BlockSpec((CHUNK, nh), lambda s: (cidx(s), 0)),
                  pl.BlockSpec((nh, CHUNK), lambda s: (0, cidx(s))),
                  small((1, nh)), small((nh, 1)), small((1, nh)), small((nh, 1)), small((nh, SSD_INNER))],
        out_specs=[pl.BlockSpec((CHUNK, SSD_INNER), lambda s: (cidx(s), 0)),
                   pl.BlockSpec((None, SSD_INNER, SSD_STATE), lambda s: (s, 0, 0))],
        out_shape=[jax.ShapeDtypeStruct((n, SSD_INNER), F32),
                   jax.ShapeDtypeStruct((nc, SSD_INNER, SSD_STATE), F32)],
        scratch_shapes=[pltpu.VMEM((SSD_INNER, SSD_STATE), F32)],
        compiler_params=_params("arbitrary"),
    )(xbc, xbc, xbc, dt_raw, dtT_raw, bias_r, bias_c, alog_r, alog_c, _head_spread())


def _ssd_scan_bwd(dy, xbc, hs, dt_raw, dtT_raw, bias_r, bias_c, alog_r, alog_c, dvec, *, rev, n_ctx_chunks,
                  direct, name):
    n = xbc.shape[0]
    nc = n // CHUNK
    nh = SSD_HEADS
    step_of = lambda r: nc - 1 - r
    cidx = lambda r: _chunk_of(step_of(r), nc, n_ctx_chunks, rev)

    def kern(dy_ref, xs_ref, b_ref, c_ref, hs_ref, dt_ref, dtT_ref, br_ref, bc_ref, ar_ref, ac_ref, dv_ref,
             e_ref, et_ref, dx_ref, ddt_ref, dal_ref, dbias_ref, dh_scr):
        @pl.when(pl.program_id(0) == 0)
        def _():
            dh_scr[...] = jnp.zeros_like(dh_scr)
            dal_ref[...] = jnp.zeros_like(dal_ref)
            dbias_ref[...] = jnp.zeros_like(dbias_ref)

        tri, tri_t, a_r, dt, acum, acum_t, atot = _scan_common(
            dt_ref[...], dtT_ref[...], br_ref[...], bc_ref[...], ar_ref[...], ac_ref[...], rev)
        etot = jnp.exp(atot)
        spread = lambda v: _dot_sel(v, e_ref[...])
        gather = lambda v: _dot_sel(v, et_ref[...])
        xs_all = xs_ref[...]
        dy_all = dy_ref[...]
        dtx = spread(dt)
        eax = spread(jnp.exp(acum))
        decx = spread(jnp.exp(atot - acum))
        xdt_all = xs_all * dtx
        xdw_all = xdt_all * decx
        dyo_all = dy_all * eax
        lane = lax.broadcasted_iota(jnp.int32, (CHUNK, nh), 1)
        lane1 = lax.broadcasted_iota(jnp.int32, (1, nh), 1)
        sub = lax.broadcasted_iota(jnp.int32, (nh, CHUNK), 0)
        g_rows = jnp.zeros((CHUNK, nh), F32)
        g_cols = jnp.zeros((nh, CHUNK), F32)
        dtot = jnp.zeros((1, nh), F32)
        q_col, q_e, q_dt = [], [], []
        for g in range(SSD_GROUPS):
            gs = slice(g * 256, (g + 1) * 256)
            bg = b_ref[:, g * SSD_STATE:(g + 1) * SSD_STATE].astype(BF16)
            cg = c_ref[:, g * SSD_STATE:(g + 1) * SSD_STATE].astype(BF16)
            cb = _dot(cg, bg, _NT)
            hs4 = hs_ref[gs, :]
            dh4 = dh_scr[gs, :]
            hs4_bf = hs4.astype(BF16)
            dh4_bf = dh4.astype(BF16)
            dy4 = dy_all[:, gs]
            dy4_bf = dy4.astype(BF16)
            xdt4_bf = xdt_all[:, gs].astype(BF16)
            xdw4 = xdw_all[:, gs]
            xdw4_bf = xdw4.astype(BF16)
            dyo4_bf = dyo_all[:, gs].astype(BF16)
            yoff4 = _dot(cg, hs4_bf, _NT) * eax[:, gs]
            dcg = _dot(dyo4_bf, hs4_bf)
            dh_new4 = _dot(dyo4_bf, cg, _TN)
            bdh4 = _dot(bg, dh4_bf, _NT)
            dbg = _dot(xdw4_bf, dh4_bf)
            e4 = xdw4 * bdh4
            q_col.append(dy4 * yoff4 - e4)
            q_e.append(e4)
            hsum = jnp.sum(dh4 * hs4, axis=1, keepdims=True)
            dcb = jnp.zeros((CHUNK, CHUNK), F32)
            dxdts = []
            for k in range(SSD_HPG):
                h = g * SSD_HPG + k
                ks = slice(k * SSD_HEAD_DIM, (k + 1) * SSD_HEAD_DIM)
                lmat = jnp.exp(jnp.where(tri, acum[:, h:h + 1] - acum_t[h:h + 1, :], NEG_BIG))
                mf = cb * lmat
                dm = _dot(dy4_bf[:, ks], xdt4_bf[:, ks], _NT)
                dcb = dcb + dm * lmat
                gmat = dm * mf
                g_rows = g_rows + jnp.where(lane == h, jnp.sum(gmat, axis=1, keepdims=True), 0.0)
                g_cols = g_cols + jnp.where(sub == h, _sum0(gmat), 0.0)
                dxdts.append(_dot(mf.astype(BF16), dy4_bf[:, ks], _TN))
                et = etot[:, h:h + 1]
                dtot = dtot + jnp.where(lane1 == h, _sum0(hsum[ks]) * et, 0.0)
                dh_scr[h * SSD_HEAD_DIM:(h + 1) * SSD_HEAD_DIM, :] = dh4[ks] * et + dh_new4[ks]
            dxdt4 = jnp.concatenate(dxdts, axis=1) + bdh4 * decx[:, gs]
            q_dt.append(dxdt4 * xs_all[:, gs])
            dx4 = dxdt4 * dtx[:, gs]
            if direct:
                dx4 = dx4 + dy4 * dv_ref[:, gs]
            dcb_bf = dcb.astype(BF16)
            dx_ref[:, gs] = dx4
            dx_ref[:, SSD_INNER + g * SSD_STATE:SSD_INNER + (g + 1) * SSD_STATE] = dbg + _dot(dcb_bf, cg, _TN)
            dx_ref[:, SSD_INNER + 1024 + g * SSD_STATE:SSD_INNER + 1024 + (g + 1) * SSD_STATE] = \
                dcg + _dot(dcb_bf, bg)
        e_heads = gather(jnp.concatenate(q_e, axis=1))
        dacum = gather(jnp.concatenate(q_col, axis=1)) + g_rows - g_cols.T
        dal = _dot(tri_t.astype(F32), dacum, precision=HI) + dtot + _sum0(e_heads)
        ddt = gather(jnp.concatenate(q_dt, axis=1)) + dal * a_r
        ddt_raw = ddt * _sig(dt_ref[...] + br_ref[...])
        ddt_ref[...] = ddt_raw
        dal_ref[...] += _sum0(dal * dt) * a_r
        dbias_ref[...] += _sum0(ddt_raw)

    small = lambda shape: pl.BlockSpec(shape, lambda r: (0, 0))
    return pl.pallas_call(
        kern, name=name, grid=(nc,),
        in_specs=[pl.BlockSpec((CHUNK, SSD_INNER), lambda r: (cidx(r), 0)),
                  pl.BlockSpec((CHUNK, SSD_INNER), lambda r: (cidx(r), 0)),
                  pl.BlockSpec((CHUNK, 1024), lambda r: (cidx(r), 2)),
                  pl.BlockSpec((CHUNK, 1024), lambda r: (cidx(r), 3)),
                  pl.BlockSpec((None, SSD_INNER, SSD_STATE), lambda r: (step_of(r), 0, 0)),
                  pl.BlockSpec((CHUNK, nh), lambda r: (cidx(r), 0)),
                  pl.BlockSpec((nh, CHUNK), lambda r: (0, cidx(r))),
                  small((1, nh)), small((nh, 1)), small((1, nh)), small((nh, 1)), small((1, SSD_INNER)),
                  small((nh, SSD_INNER)), small((SSD_INNER, nh))],
        out_specs=[pl.BlockSpec((CHUNK, SSD_CONV_DIM), lambda r: (cidx(r), 0)),
                   pl.BlockSpec((CHUNK, nh), lambda r: (cidx(r), 0)),
                   small((1, nh)), small((1, nh))],
        out_shape=[jax.ShapeDtypeStruct((n, SSD_CONV_DIM), F32), jax.ShapeDtypeStruct((n, nh), F32),
                   jax.ShapeDtypeStruct((1, nh), F32), jax.ShapeDtypeStruct((1, nh), F32)],
        scratch_shapes=[pltpu.VMEM((SSD_INNER, SSD_STATE), F32)],
        compiler_params=_params("arbitrary"),
    )(dy, xbc, xbc, xbc, hs, dt_raw, dtT_raw, bias_r, bias_c, alog_r, alog_c, dvec, _head_spread(),
      _head_spread().T)


def _gm_spatial_fwd(gu, gvn, ws, bst, *, name):
    n = gu.shape[0]

    def kern(gu_ref, gv_ref, ws_ref, bs_ref, o_ref):
        for g in range(GM_GROUPS):
            sl = slice(g * GM_GROUP_DIM, (g + 1) * GM_GROUP_DIM)
            s = _dot(ws_ref[g], gv_ref[:, sl]) + bs_ref[:, g:g + 1]
            o_ref[:, sl] = (gu_ref[:, sl] * s).astype(BF16)

    spec = pl.BlockSpec((CHUNK, GM_INNER), lambda i: (i, 0))
    return pl.pallas_call(
        kern, name=name, grid=(n // CHUNK,),
        in_specs=[spec, spec, pl.BlockSpec(ws.shape, lambda i: (0, 0, 0)), pl.BlockSpec(bst.shape, lambda i: (0, 0))],
        out_specs=spec, out_shape=jax.ShapeDtypeStruct((n, GM_INNER), BF16),
        compiler_params=_params("parallel"),
    )(gu, gvn, ws, bst)


def _gm_spatial_bwd(dt, gu, gvn, ws, wst, bst, *, name):
    n = gu.shape[0]

    def kern(dt_ref, gu_ref, gv_ref, ws_ref, wst_ref, bs_ref, dgu_ref, dgv_ref, dws_ref, dbs_ref):
        @pl.when(pl.program_id(0) == 0)
        def _():
            dws_ref[...] = jnp.zeros_like(dws_ref)
            dbs_ref[...] = jnp.zeros_like(dbs_ref)

        lane = lax.broadcasted_iota(jnp.int32, (CHUNK, GM_GROUPS), 1)
        dbs = jnp.zeros((CHUNK, GM_GROUPS), F32)
        for g in range(GM_GROUPS):
            sl = slice(g * GM_GROUP_DIM, (g + 1) * GM_GROUP_DIM)
            gv = gv_ref[:, sl]
            s = _dot(ws_ref[g], gv) + bs_ref[:, g:g + 1]
            d = dt_ref[:, sl]
            dgu_ref[:, sl] = d * s
            ds = d * gu_ref[:, sl]
            ds_bf = ds.astype(BF16)
            dws_ref[g] += _dot(ds_bf, gv, _NT)
            dgv_ref[:, sl] = _dot(wst_ref[g], ds_bf)
            dbs = dbs + jnp.where(lane == g, jnp.sum(ds, axis=1, keepdims=True), 0.0)
        dbs_ref[...] += dbs

    spec = pl.BlockSpec((CHUNK, GM_INNER), lambda i: (i, 0))
    wspec = pl.BlockSpec(ws.shape, lambda i: (0, 0, 0))
    bspec = pl.BlockSpec(bst.shape, lambda i: (0, 0))
    return pl.pallas_call(
        kern, name=name, grid=(n // CHUNK,),
        in_specs=[spec, spec, spec, wspec, wspec, bspec],
        out_specs=[spec, spec, wspec, bspec],
        out_shape=[jax.ShapeDtypeStruct((n, GM_INNER), F32), jax.ShapeDtypeStruct((n, GM_INNER), F32),
                   jax.ShapeDtypeStruct(ws.shape, F32), jax.ShapeDtypeStruct(bst.shape, F32)],
        compiler_params=_params("arbitrary"),
    )(dt, gu, gvn, ws, wst, bst)


def _adamw(parts, w, m, v, *, name, tm=256, sel=(), into=None):
    ns, r, wd = parts.shape
    tm = _pick(r, tm, 8)
    tc = wd
    if tm < 64 and wd % 256 == 0:
        tm, tc = r, 256
    lead = len(sel)
    assert w.shape[lead:] == (r, wd) and lead == w.ndim - 2

    def kern(*refs):
        p_ref, w_ref, m_ref, v_ref = refs[:4]
        g_ref, d_ref, nm_ref, nv_ref = refs[-4:]
        g = p_ref[0].astype(F32)
        for s in range(1, ns):
            g = g + p_ref[s].astype(F32)
        m2 = ADAM_B1 * m_ref[...] + (1.0 - ADAM_B1) * g
        v2 = ADAM_B2 * v_ref[...] + (1.0 - ADAM_B2) * (g * g)
        m_hat = m2 / (1.0 - ADAM_B1 ** ADAM_STEP)
        v_hat = v2 / (1.0 - ADAM_B2 ** ADAM_STEP)
        g_ref[...] = g
        d_ref[...] = -ADAM_LR * (m_hat / (jnp.sqrt(v_hat) + ADAM_EPS) + ADAM_WD * w_ref[...])
        nm_ref[...] = m2
        nv_ref[...] = v2

    spec = pl.BlockSpec((None,) * lead + (tm, tc), lambda i, j: tuple(sel) + (i, j))
    extra, aliases = [], {}
    if into is not None:
        extra = list(into)
        aliases = {4 + k: k for k in range(4)}
    return pl.pallas_call(
        kern, name=name, grid=(r // tm, wd // tc),
        in_specs=[pl.BlockSpec((ns, tm, tc), lambda i, j: (0, i, j)), spec, spec, spec] +
                 [pl.BlockSpec(memory_space=pl.ANY)] * len(extra),
        out_specs=[spec] * 4, out_shape=[jax.ShapeDtypeStruct(w.shape, F32)] * 4,
        input_output_aliases=aliases,
        compiler_params=_params("parallel", "parallel"),
    )(parts, w, m, v, *extra)


def _sum_slots(parts, *, name, scale_by=None):
    ns, r, wd = parts.shape

    def kern(*refs):
        p_ref, o_ref = refs[0], refs[-1]
        g = p_ref[0]
        for s in range(1, ns):
            g = g + p_ref[s]
        if scale_by is not None:
            g = g * _dsilu(refs[1][...])
        o_ref[...] = g

    args = [parts] + ([] if scale_by is None else [scale_by])
    return pl.pallas_call(kern, name=name, out_shape=jax.ShapeDtypeStruct((r, wd), F32),
                          compiler_params=pltpu.CompilerParams(vmem_limit_bytes=VMEM_LIMIT_BYTES))(*args)


def _mesh_pos():
    x, y, c = lax.axis_index("x"), lax.axis_index("y"), lax.axis_index("c")
    return x, y, c, 4 * x + 2 * y + c


def _flip(x, y, c, f):
    fx, fy, fc = (f >> 2) & 1, (f >> 1) & 1, f & 1
    px = 1 - x if fx else x
    py = 1 - y if fy else y
    pc = 1 - c if fc else c
    return (px, py, pc), 4 * px + 2 * py + pc


_HBM_SPEC = pl.BlockSpec(memory_space=pltpu.HBM)


def _exchange(arrays, *, scatter, name):
    na = len(arrays)
    if scatter:
        out_shape = [jax.ShapeDtypeStruct(a.shape, a.dtype) for a in arrays]
    else:
        out_shape = [jax.ShapeDtypeStruct((NDEV,) + a.shape, a.dtype) for a in arrays]

    out_shape.append(jax.ShapeDtypeStruct((8, 128), F32))

    def body(*refs):
        ins, outs = refs[:na], refs[na:2 * na]
        send_sems, recv_sems, local_sems = refs[2 * na + 1:]
        refs[2 * na][...] = jnp.zeros((8, 128), F32)
        x, y, c, me = _mesh_pos()
        copies = []
        for i in range(na):
            src_own = ins[i].at[me] if scatter else ins[i]
            lc = pltpu.make_async_copy(src_own, outs[i].at[me], local_sems.at[i])
            lc.start()
            copies.append(lc)
        sends = []
        for f in range(1, NDEV):
            peer, pidx = _flip(x, y, c, f)
            for i in range(na):
                k = i * (NDEV - 1) + f - 1
                src = ins[i].at[pidx] if scatter else ins[i]
                cp = pltpu.make_async_remote_copy(
                    src_ref=src, dst_ref=outs[i].at[me], send_sem=send_sems.at[k], recv_sem=recv_sems.at[k],
                    device_id=peer, device_id_type=pl.DeviceIdType.MESH)
                cp.start()
                sends.append(cp)
        for f in range(1, NDEV):
            peer, pidx = _flip(x, y, c, f)
            for i in range(na):
                k = i * (NDEV - 1) + f - 1
                src = ins[i].at[pidx] if scatter else ins[i]
                pltpu.make_async_remote_copy(
                    src_ref=src, dst_ref=outs[i].at[pidx], send_sem=send_sems.at[k], recv_sem=recv_sems.at[k],
                    device_id=peer, device_id_type=pl.DeviceIdType.MESH).wait_recv()
        for cp in sends:
            cp.wait_send()
        for lc in copies:
            lc.wait()

    res = pl.pallas_call(
        body, name=name, out_shape=out_shape, in_specs=[_HBM_SPEC] * na,
        out_specs=[_HBM_SPEC] * na + [pl.BlockSpec(memory_space=pltpu.VMEM)],
        scratch_shapes=[pltpu.SemaphoreType.DMA((na * (NDEV - 1),)), pltpu.SemaphoreType.DMA((na * (NDEV - 1),)),
                        pltpu.SemaphoreType.DMA((na,))],
        compiler_params=pltpu.CompilerParams(has_side_effects=True),
    )(*arrays)
    return res[:na], res[na][0, 0]


_SEM_SPEC = pl.BlockSpec(memory_space=pltpu.SEMAPHORE)
_DATAFLOW = pltpu.SideEffectType.DATAFLOW_SIDE_EFFECTING


def _split_copies(srcs, lands, send_sems, recv_sems, scatter, arriving):
    x, y, c, me = _mesh_pos()
    copies = []
    for i in range(len(srcs)):
        for f in range(1, NDEV):
            peer, pidx = _flip(x, y, c, f)
            k = i * (NDEV - 1) + f - 1
            copies.append(pltpu.make_async_remote_copy(
                src_ref=srcs[i].at[pidx] if scatter else srcs[i], dst_ref=lands[i].at[pidx if arriving else me],
                send_sem=send_sems.at[k], recv_sem=recv_sems.at[k], device_id=peer,
                device_id_type=pl.DeviceIdType.MESH))
    return copies


def _exchange_start(srcs, lands, *, scatter, name):
    na = len(srcs)
    nsem = na * (NDEV - 1)

    def body(*refs):
        ins_src, ins_land = refs[:na], refs[na:2 * na]
        send_sems, recv_sems = refs[2 * na], refs[2 * na + 1]
        token = refs[-1]
        for cp in _split_copies(ins_src, ins_land, send_sems, recv_sems, scatter, False):
            cp.start()
        token[...] = jnp.zeros_like(token)

    thru = [pltpu.HBM(a.shape, a.dtype) for a in list(srcs) + list(lands)]
    res = pl.pallas_call(
        body, name=name,
        out_shape=(pltpu.SemaphoreType.DMA((nsem,)), pltpu.SemaphoreType.DMA((nsem,)), *thru,
                   jax.ShapeDtypeStruct((8, 128), F32)),
        in_specs=[_HBM_SPEC] * (2 * na),
        out_specs=(_SEM_SPEC, _SEM_SPEC, *([_HBM_SPEC] * (2 * na)), pl.BlockSpec(memory_space=pltpu.VMEM)),
        input_output_aliases={i: 2 + i for i in range(2 * na)},
        compiler_params=pltpu.CompilerParams(has_side_effects=_DATAFLOW),
    )(*[pltpu.with_memory_space_constraint(a, pltpu.HBM) for a in list(srcs) + list(lands)])
    send_sems, recv_sems = res[0], res[1]
    return send_sems, recv_sems, res[2:2 + na], res[2 + na:2 + 2 * na], res[-1][0, 0]


def _exchange_wait(send_sems, recv_sems, srcs, lands, after, *, scatter, name):
    na = len(srcs)

    def body(*refs):
        ins_src, ins_land = refs[:na], refs[na:2 * na]
        s_sems, r_sems = refs[2 * na], refs[2 * na + 1]
        for cp in _split_copies(ins_src, ins_land, s_sems, r_sems, scatter, False):
            cp.wait_send()
        for cp in _split_copies(ins_src, ins_land, s_sems, r_sems, scatter, True):
            cp.wait_recv()

    thru = [pltpu.HBM(a.shape, a.dtype) for a in list(srcs) + list(lands)]
    res = pl.pallas_call(
        body, name=name, out_shape=tuple(thru),
        in_specs=[_HBM_SPEC] * (2 * na) + [_SEM_SPEC, _SEM_SPEC, pl.BlockSpec(memory_space=pl.ANY)],
        out_specs=tuple([_HBM_SPEC] * (2 * na)),
        input_output_aliases={i: i for i in range(2 * na)},
        compiler_params=pltpu.CompilerParams(has_side_effects=_DATAFLOW),
    )(*srcs, *lands, send_sems, recv_sems, after)
    return res[na:]


def _landing(block, me):
    buf = lax.empty((NDEV,) + block.shape, block.dtype)
    return lax.dynamic_update_slice_in_dim(buf, block[None], me, axis=0)


def _seg_kw(nseg, n_ctx, tm):
    return dict(nseg=nseg, seg_blocks=(n_ctx // tm if nseg == 2 else 0))


def _ffn_fwd(tag, h, gpre, gpost, shift, scale, gate, w, *, nseg, n_ctx, tm):
    n = h.shape[0]
    kw = _seg_kw(nseg, n_ctx, tm)
    (u,) = _rowwise(tag + "_pre", _pre_fwd_fn, n, [h], [("full", gpre), ("seg", shift), ("seg", scale)],
                    [(D_MODEL, BF16)], tm=tm, **kw)
    if "early" in w:
        w.update(w.pop("early")(u))
    s, a, b = _mm_glu(u, w["win_t"], name=tag + "_glu")
    if "late" in w:
        w.update(w.pop("late")(s))
    y, ho = _mm_rows(s, w["wout"], functools.partial(_out_post_fn, 0.5), [h], [("full", gpost), ("seg", gate)],
                     [(D_MODEL, F32), (D_MODEL, F32)], name=tag + "_out", tk=FFN_DIM, n_ctx=n_ctx)
    return ho, dict(h=h, u=u, s=s, a=a, b=b, y=y)


def _ffn_bwd(tag, dho, sv, gpre, gpost, scale, gate, w, put, *, nseg, n_ctx, tm):
    n = dho.shape[0]
    kw = _seg_kw(nseg, n_ctx, tm)
    dy, dgate, dgpost = _rowwise(tag + "_postb", functools.partial(_post_bwd_fn, 0.5), n, [dho, sv["y"]],
                                 [("full", gpost), ("seg", gate)], [(D_MODEL, BF16)], [D_MODEL, D_MODEL], tm=tm, **kw)
    tok = put("w_out", _mm_tn(sv["s"], dy, name=tag + "_dwout", tm=1408, tn=1024, col_blocks=1))
    ds = _mm(dy, w["wout"], out_dtype=F32, name=tag + "_ds", tn=1408, rhs_t=True)
    (dp,) = _rowwise(tag + "_glub", _glu_bwd_fn, n, [ds, sv["a"], sv["b"]], [], [(2 * FFN_DIM, BF16)], tm=min(tm, 128))
    tok2 = put("w_in", _mm_tn(dp, sv["u"], name=tag + "_dwin", tm=1408, tn=1024, col_blocks=1))
    for t in (tok, tok2):
        if t is not None:
            gpre = gpre + t
    dh, dshift, dscale, dgpre = _mm_rows(dp, w["win_t"], _pre_bwd_fn, [sv["h"], dho],
                                         [("full", gpre), ("seg", scale)], [(D_MODEL, F32)],
                                         [D_MODEL, D_MODEL, D_MODEL], name=tag + "_du", n_ctx=n_ctx)
    return dh, None, dict(shift=dshift, scale=dscale, gate=dgate, gpre=dgpre, gpost=dgpost)


def _local_step(x, ctx, target, mods, norm_g, get_w, small, put_grad):
    t_len, n_ctx = x.shape[0], ctx.shape[0]
    n0 = t_len + n_ctx
    tm0 = _pick(n_ctx, 256, 8)
    tm1 = _pick(t_len, 256, 8)
    ncc = n_ctx // CHUNK
    g = {}

    def modrow(i, k, nseg):
        mc, mx = mods[i]
        if nseg == 2:
            return jnp.stack([mc[k], mx[k]])[:, None, :]
        return mx[k][None, None, :]

    pending = [None]

    def gvec(i, k):
        v = norm_g[i, k][None, :]
        if pending[0] is not None:
            v = v + pending[0]
            pending[0] = None
        return v

    xc = jnp.concatenate([ctx, x], axis=0)
    L0 = dict(nseg=2, n_ctx=n_ctx, tm=tm0)
    wts = dict(get_w("ffn00", xc))
    h1, sv_f01 = _ffn_fwd("l0f1", xc, gvec(0, 0), gvec(0, 1), modrow(0, 0, 2), modrow(0, 1, 2), modrow(0, 2, 2),
                          wts["ffn00"], **L0)
    kw0 = _seg_kw(2, n_ctx, tm0)
    (um0,) = _rowwise("l0m_pre", _pre_fwd_fn, n0, [h1], [("full", gvec(0, 2)), ("seg", modrow(0, 3, 2)),
                                                         ("seg", modrow(0, 4, 2))], [(D_MODEL, BF16)], tm=tm0, **kw0)
    wts.update(get_w("ssd", um0))
    win_ssd = wts["ssd_win_t"]
    nh = SSD_HEADS
    dt_blk = (SSD_INNER + SSD_CONV_DIM) // (2 * nh)
    z = _mm(um0, win_ssd, out_dtype=F32, name="ssd_z", rhs_t=True, n=SSD_INNER)
    xbc_pre = _mm(um0, win_ssd, out_dtype=F32, name="ssd_xbc", rhs_t=True, n=SSD_CONV_DIM,
                  b_off=(SSD_INNER // 1024, 0))
    dtr = _mm(um0, win_ssd, out_dtype=F32, name="ssd_dt", rhs_t=True, n=2 * nh, b_off=(dt_blk, 0))
    cpre, xbc = _conv_fwd(xbc_pre, small["conv_w8"], small["conv_b"], n_ctx=n_ctx, name="ssd_conv")
    nh = SSD_HEADS
    dt_dir = [dtr[:, :nh], dtr[:, nh:2 * nh]]
    dtT_dir = [d.T for d in dt_dir]
    bias_r = [small["dt_bias"][d][None, :] for d in range(2)]
    bias_c = [small["dt_bias"][d][:, None] for d in range(2)]
    alog_r = [small["a_log"][d][None, :] for d in range(2)]
    alog_c = [small["a_log"][d][:, None] for d in range(2)]
    ys, hss = [], []
    for d in range(2):
        yd, hsd = _ssd_scan_fwd(xbc, dt_dir[d], dtT_dir[d], bias_r[d], bias_c[d], alog_r[d], alog_c[d],
                                rev=(d == 1), n_ctx_chunks=ncc, name=f"ssd_scan{d}")
        ys.append(yd)
        hss.append(hsd)
    dvec = jnp.repeat(small["ssd_d"], SSD_HEAD_DIM)[None, :]
    ngv = small["ssd_norm_g"][None, :]
    gate_rows = [ys[0], ys[1], (xbc, SSD_INNER, 0, 0), z]
    lat = lambda r: (r[0], r[1], r[2], ncc) if isinstance(r, tuple) else (r, r.shape[1], 0, ncc)
    (yn,) = _rowwise("ssd_gate", _ssdgate_fwd_fn, t_len, [lat(r) for r in gate_rows],
                     [("full", dvec), ("full", ngv)], [(SSD_INNER, BF16)], tm=CHUNK)
    h1x = h1[n_ctx:]
    L1 = dict(nseg=1, n_ctx=0, tm=tm1)
    if "late" in wts:
        wts.update(wts.pop("late")(yn))
    yo0, h2 = _mm_rows(yn, wts["ssd_wout"], functools.partial(_out_post_fn, 1.0), [h1x],
                       [("full", gvec(0, 3)), ("seg", modrow(0, 5, 1))], [(D_MODEL, F32), (D_MODEL, F32)],
                       name="ssd_out", tk=SSD_INNER)
    wts.update(get_w("ffn01", h2))
    h3, sv_f02 = _ffn_fwd("l0f2", h2, gvec(0, 4), gvec(0, 5), modrow(0, 6, 1), modrow(0, 7, 1), modrow(0, 8, 1),
                          wts["ffn01"], **L1)

    wts.update(get_w("ffn10", h3))
    h4, sv_f11 = _ffn_fwd("l1f1", h3, gvec(1, 0), gvec(1, 1), modrow(1, 0, 1), modrow(1, 1, 1), modrow(1, 2, 1),
                          wts["ffn10"], **L1)
    (um1,) = _rowwise("l1m_pre", _pre_fwd_fn, t_len, [h4], [("full", gvec(1, 2)), ("seg", modrow(1, 3, 1)),
                                                            ("seg", modrow(1, 4, 1))], [(D_MODEL, BF16)], tm=tm1)
    wts.update(get_w("gm", um1))
    p1 = _mm(um1, wts["gm_win"], out_dtype=F32, name="gm_in")
    vg = small["gm_v_g"][None, :]
    vb = small["gm_v_b"][None, :]
    gu, gvn = _rowwise("gm_act", _gm_act_fwd_fn, t_len, [p1], [("full", vg), ("full", vb)],
                       [(GM_INNER, F32), (GM_INNER, BF16)], tm=128)
    ws_bf = small["gm_w_s"].astype(BF16)
    wst_bf = jnp.swapaxes(small["gm_w_s"], 1, 2).astype(BF16)
    bst = small["gm_b_s"].T
    tgm = _gm_spatial_fwd(gu, gvn, ws_bf, bst, name="gm_spatial")
    yo1, h5 = _mm_rows(tgm, wts["gm_wout"], functools.partial(_out_post_fn, 1.0), [h4],
                       [("full", gvec(1, 3)), ("seg", modrow(1, 5, 1))], [(D_MODEL, F32), (D_MODEL, F32)],
                       name="gm_out", tk=GM_INNER)
    wts.update(get_w("ffn11", h5))
    h6, sv_f12 = _ffn_fwd("l1f2", h5, gvec(1, 4), gvec(1, 5), modrow(1, 6, 1), modrow(1, 7, 1), modrow(1, 8, 1),
                          wts["ffn11"], **L1)

    dh, loss_parts = _rowwise("loss", _loss_fn, t_len, [h6, target], [], [(D_MODEL, F32)], [D_MODEL], tm=tm1)

    zero = jnp.zeros((D_MODEL,), F32)
    dmx = [[zero] * N_MOD for _ in range(2)]
    dmc = [[zero] * N_MOD for _ in range(2)]
    dng = [[zero] * 6 for _ in range(2)]

    def put_mod(i, k, acc):
        if acc.shape[0] == 2:
            dmc[i][k] = dmc[i][k] + acc[0, 0]
            dmx[i][k] = dmx[i][k] + acc[1, 0]
        else:
            dmx[i][k] = dmx[i][k] + acc[0, 0]

    def put_g(i, k, acc):
        dng[i][k] = dng[i][k] + jnp.sum(acc[:, 0], axis=0)

    def ffn_back(tag, i, j, dho, sv, w, lay):
        nseg = lay["nseg"]
        base = 0 if j == 0 else 6
        gi = 0 if j == 0 else 4
        dh_in, pending[0], s = _ffn_bwd(tag, dho, sv, gvec(i, gi), gvec(i, gi + 1), modrow(i, base + 1, nseg),
                                        modrow(i, base + 2, nseg), w, functools.partial(put_grad, f"ffn{i}{j}"), **lay)
        put_mod(i, base, s["shift"])
        put_mod(i, base + 1, s["scale"])
        put_mod(i, base + 2, s["gate"])
        put_g(i, gi, s["gpre"])
        put_g(i, gi + 1, s["gpost"])
        return dh_in

    dh = ffn_back("l1f2", 1, 1, dh, sv_f12, wts["ffn11"], L1)
    dyo, dgate, dgp = _rowwise("l1m_postb", functools.partial(_post_bwd_fn, 1.0), t_len, [dh, yo1],
                               [("full", gvec(1, 3)), ("seg", modrow(1, 5, 1))], [(D_MODEL, BF16)],
                               [D_MODEL, D_MODEL], tm=tm1)
    put_mod(1, 5, dgate)
    put_g(1, 3, dgp)
    put_grad("gm", "w_out", _mm_tn(tgm, dyo, name="gm_dwout", tn=1024, col_blocks=1))
    dtg = _mm(dyo, wts["gm_wout"], out_dtype=F32, name="gm_dt", rhs_t=True)
    dgu, dgvn, dws, dbst = _gm_spatial_bwd(dtg, gu, gvn, ws_bf, wst_bf, bst, name="gm_spatialb")
    g["gm_w_s"] = dws
    g["gm_b_s"] = dbst.T
    dp1, dvg, dvb = _rowwise("gm_actb", _gm_act_bwd_fn, t_len, [p1, dgu, dgvn], [("full", vg)],
                             [(2 * GM_INNER, BF16)], [GM_INNER, GM_INNER], tm=128)
    g["gm_v_g"] = dvg[0, 0]
    g["gm_v_b"] = dvb[0, 0]
    pending[0] = put_grad("gm", "w_in", _mm_tn(um1, dp1, name="gm_dwin", tm=1024, col_blocks=NDEV))
    dh, dsh, dsc, dgp = _mm_rows(dp1, wts["gm_win"], _pre_bwd_fn, [h4, dh],
                                 [("full", gvec(1, 2)), ("seg", modrow(1, 4, 1))], [(D_MODEL, F32)],
                                 [D_MODEL, D_MODEL, D_MODEL], name="gm_dum", tk=1024, rhs_t=True)
    put_mod(1, 3, dsh)
    put_mod(1, 4, dsc)
    put_g(1, 2, dgp)
    dh = ffn_back("l1f1", 1, 0, dh, sv_f11, wts["ffn10"], L1)

    dh = ffn_back("l0f2", 0, 1, dh, sv_f02, wts["ffn01"], L1)
    dyo, dgate, dgp = _rowwise("l0m_postb", functools.partial(_post_bwd_fn, 1.0), t_len, [dh, yo0],
                               [("full", gvec(0, 3)), ("seg", modrow(0, 5, 1))], [(D_MODEL, BF16)],
                               [D_MODEL, D_MODEL], tm=tm1)
    put_mod(0, 5, dgate)
    put_g(0, 3, dgp)
    tok = put_grad("ssd", "w_out", _mm_tn(yn, dyo, name="ssd_dwout", tn=1024, col_blocks=1))
    dyn = _mm(dyo, wts["ssd_wout"], out_dtype=F32, name="ssd_dyn", rhs_t=True)
    dy_ssd, dz, dngv, ddv = _rowwise("ssd_gateb", _ssdgate_bwd_fn, n0, [(dyn, SSD_INNER, 0, -ncc)] + gate_rows,
                                     [("full", dvec), ("full", ngv if tok is None else ngv + tok)],
                                     [(SSD_INNER, F32), (SSD_INNER, BF16)],
                                     [SSD_INNER, SSD_INNER], tm=128)
    g["ssd_norm_g"] = dngv[0, 0]
    g["ssd_D"] = jnp.sum(ddv[0, 0].reshape(SSD_HEADS, SSD_HEAD_DIM), axis=1)
    dxbcs, ddts, dalogs, dbiases = [], [], [], []
    for d in range(2):
        dxd, ddtd, dal, dbi = _ssd_scan_bwd(dy_ssd, xbc, hss[d], dt_dir[d], dtT_dir[d], bias_r[d], bias_c[d],
                                            alog_r[d], alog_c[d], dvec, rev=(d == 1), n_ctx_chunks=ncc,
                                            direct=(d == 0), name=f"ssd_scanb{d}")
        dxbcs.append(dxd)
        ddts.append(ddtd)
        dalogs.append(dal[0])
        dbiases.append(dbi[0])
    g["ssd_A_log"] = jnp.stack(dalogs)
    g["ssd_dt_bias"] = jnp.stack(dbiases)
    dxbc_pre, dcw8, dcb = _conv_bwd(dxbcs[0], dxbcs[1], cpre, xbc_pre, small["conv_w8"], n_ctx=n_ctx, name="ssd_convb")
    g["ssd_conv_w"] = dcw8[:SSD_CONV]
    g["ssd_conv_b"] = dcb[0]
    ddt_bf = jnp.concatenate([ddts[0], ddts[1]], axis=1).astype(BF16)
    dw_ssd_in_t = jnp.concatenate([
        _mm_tn(dz, um0, name="ssd_dwz", col_blocks=1),
        _mm_tn(dxbc_pre, um0, name="ssd_dwxbc", col_blocks=1),
        _mm_tn(ddt_bf, um0, name="ssd_dwdt", col_blocks=1)], axis=1)
    pending[0] = put_grad("ssd", "w_in", dw_ssd_in_t)
    dum0 = _mm(dz, win_ssd, out_dtype=F32, name="ssd_dum_z", tk=1024, n=D_MODEL)
    dum0 = _mm(dxbc_pre, win_ssd, out_dtype=F32, name="ssd_dum_x", tk=1024, n=D_MODEL,
               b_off=(SSD_INNER // 1024, 0), add=dum0)
    dum0 = _mm(ddt_bf, win_ssd, out_dtype=F32, name="ssd_dum_dt", tk=2 * nh, n=D_MODEL, b_off=(dt_blk, 0), add=dum0)
    dh0, dsh, dsc, dgp = _rowwise("l0m_preb", _pre_bwd_fn, n0, [dum0, h1, (dh, D_MODEL, 0, -(n_ctx // tm0))],
                                  [("full", gvec(0, 2)), ("seg", modrow(0, 4, 2))], [(D_MODEL, F32)],
                                  [D_MODEL, D_MODEL, D_MODEL], tm=tm0, **kw0)
    put_mod(0, 3, dsh)
    put_mod(0, 4, dsc)
    put_g(0, 2, dgp)
    dh0 = ffn_back("l0f1", 0, 0, dh0, sv_f01, wts["ffn00"], L0)
    grad_x = dh0[n_ctx:]
    g["norm_g"] = jnp.stack([jnp.stack(r) for r in dng])
    g["dmx"] = jnp.stack([jnp.concatenate(r) for r in dmx])
    g["dmc"] = jnp.stack([jnp.concatenate(r) for r in dmc])
    return loss_parts[0], grad_x, g


GROUPS = ("ffn00", "ssd", "ffn01", "ffn10", "gm", "ffn11")


TRANSPOSED_IN = ("ffn", "ssd")


def _is_transposed(group):
    return group.startswith(TRANSPOSED_IN)


def _mats_in(group, win_l):
    if _is_transposed(group):
        return {("win_t" if group.startswith("ffn") else group + "_win_t"): win_l.reshape(-1, win_l.shape[2])}
    k, nloc = win_l.shape[1], win_l.shape[2]
    return {group + "_win": jnp.transpose(win_l, (1, 0, 2)).reshape(k, NDEV * nloc)}


def _mats_out(group, wout_l):
    pre = "" if group.startswith("ffn") else group + "_"
    return {pre + "wout": wout_l.reshape(-1, wout_l.shape[2])}


def _group_mats(group, lands):
    m = {**_mats_in(group, lands[0]), **_mats_out(group, lands[1])}
    return {group: m} if group.startswith("ffn") else m


def _grad_blocks(which, grad):
    if grad.ndim == 3:
        return grad if grad.shape[0] == NDEV else grad.reshape(NDEV, grad.shape[1] // NDEV, grad.shape[2])
    if which == "w_in":
        k, n = grad.shape
        return jnp.transpose(grad.reshape(k, NDEV, n // NDEV), (1, 0, 2)).astype(BF16)
    return grad.reshape(NDEV, grad.shape[0] // NDEV, grad.shape[1]).astype(BF16)


def kernel(x, c, ctx, c_ctx, ada_w, ada_b, norm_g, ffn_w_in, ffn_w_out, ssd_w_in, ssd_conv_w, ssd_conv_b, ssd_dt_bias, ssd_A_log, ssd_D, ssd_norm_g, ssd_w_out, gm_w_in, gm_v_g, gm_v_b, gm_w_s, gm_b_s, gm_w_out, loss_target, m_c_ctx, m_ada_w, m_ada_b, m_norm_g, m_ffn_w_in, m_ffn_w_out, m_ssd_w_in, m_ssd_conv_w, m_ssd_conv_b, m_ssd_dt_bias, m_ssd_A_log, m_ssd_D, m_ssd_norm_g, m_ssd_w_out, m_gm_w_in, m_gm_v_g, m_gm_v_b, m_gm_w_s, m_gm_b_s, m_gm_w_out, v_c_ctx, v_ada_w, v_ada_b, v_norm_g, v_ffn_w_in, v_ffn_w_out, v_ssd_w_in, v_ssd_conv_w, v_ssd_conv_b, v_ssd_dt_bias, v_ssd_A_log, v_ssd_D, v_ssd_norm_g, v_ssd_w_out, v_gm_w_in, v_gm_v_g, v_gm_v_b, v_gm_w_s, v_gm_b_s, v_gm_w_out):
    me = 4 * lax.axis_index("x") + 2 * lax.axis_index("y") + lax.axis_index("c")
    d = D_MODEL
    ncol = N_MOD * d // NDEV

    small_pack = jnp.concatenate([c.reshape(-1), norm_g.reshape(-1), ssd_conv_w.reshape(-1),
                                  gm_v_g.reshape(-1), gm_v_b.reshape(-1)])[None, :]
    (sp,), _ = _exchange([small_pack], scatter=False, name="gather_small")
    sp = sp[:, 0]
    o = 0
    c_all = sp[:, o:o + d]; o += d
    ng_all = sp[:, o:o + 2 * 6 * 128].reshape(NDEV, 2, 6, 128); o += 2 * 6 * 128
    cw_all = sp[:, o:o + SSD_CONV * 512].reshape(NDEV, SSD_CONV, 512); o += SSD_CONV * 512
    vg_all = sp[:, o:o + 256]; o += 256
    vb_all = sp[:, o:o + 256]; o += 256
    norm_g_full = jnp.transpose(ng_all, (1, 2, 0, 3)).reshape(2, 6, d)
    conv_w_full = jnp.transpose(cw_all, (1, 0, 2)).reshape(SSD_CONV, SSD_CONV_DIM)
    gm_v_g_full = vg_all.reshape(-1)
    gm_v_b_full = vb_all.reshape(-1)

    c16 = jnp.concatenate([c_all, jnp.broadcast_to(c_ctx[None, :], (NDEV, d))], axis=0)
    ada_b_loc = lax.dynamic_slice_in_dim(ada_b, me * ncol, ncol, axis=1)
    mods_loc = jnp.stack([_mm_f32(c16, ada_w[i], name=f"ada_mod{i}", silu_a=True, bias=ada_b_loc[i][None, :])
                          for i in range(2)])
    (mods_all,), mods_done = _exchange([mods_loc], scatter=False, name="gather_mods")

    tr = lambda a: jnp.swapaxes(a, -1, -2)
    shard = {"ssd": (tr(ssd_w_in)[0], ssd_w_out[0]), "gm": (gm_w_in[0], gm_w_out[0])}
    for i in range(2):
        for j in range(2):
            shard[f"ffn{i}{j}"] = (tr(ffn_w_in)[i, j], ffn_w_out[i, j])
    apart = GROUPS[:2]
    units = []
    for grp in GROUPS:
        units += [(grp + "_in", grp, (0,)), (grp + "_out", grp, (1,))] if grp in apart else [(grp, grp, (0, 1))]
    gathers = {}
    started = mods_done
    for unit, grp, idx in units:
        srcs = [(shard[grp][k] + started).astype(BF16) for k in idx]
        st = _exchange_start(srcs, [_landing(s, me) for s in srcs], scatter=False, name="gather_start_" + unit)
        gathers[unit] = st[:4]
        started = st[4]

    def fetch(unit, after):
        return _exchange_wait(*gathers[unit], after, scatter=False, name="gather_wait_" + unit)

    def get_w(grp, after):
        if grp not in apart:
            return _group_mats(grp, fetch(grp, after))
        early = lambda later: _mats_in(grp, fetch(grp + "_in", later)[0])
        late = lambda later: _mats_out(grp, fetch(grp + "_out", later)[0])
        if grp.startswith("ffn"):
            return {grp: dict(early=early, late=late)}
        return dict(early(after), late=late)

    scatters = {}
    held = {}

    def put_grad(grp, which, grad):
        if grp in apart:
            unit, blocks = grp + "_" + which[2:], [_grad_blocks(which, grad)]
        else:
            held[grp, which] = _grad_blocks(which, grad)
            if (grp, "w_in") not in held or (grp, "w_out") not in held:
                return None
            unit, blocks = grp, [held[grp, "w_in"], held[grp, "w_out"]]
        own = [lax.dynamic_index_in_dim(b, me, axis=0, keepdims=False) for b in blocks]
        st = _exchange_start(blocks, [_landing(o_, me) for o_ in own], scatter=True, name="scatter_start_" + unit)
        scatters[unit] = st[:4]
        return st[4]

    mods_rows = jnp.transpose(mods_all, (1, 2, 0, 3)).reshape(2, 2 * NDEV, N_MOD * d) + started
    mx = lax.dynamic_index_in_dim(mods_rows, me, axis=1, keepdims=False).reshape(2, N_MOD, d)
    mc = mods_rows[:, NDEV].reshape(2, N_MOD, d)
    mods = [(mc[i], mx[i]) for i in range(2)]

    small = dict(conv_w8=jnp.pad(conv_w_full, ((0, 8 - SSD_CONV), (0, 0))), conv_b=ssd_conv_b, dt_bias=ssd_dt_bias[0],
                 a_log=ssd_A_log[0], ssd_d=ssd_D[0], ssd_norm_g=ssd_norm_g[0], gm_v_g=gm_v_g_full,
                 gm_v_b=gm_v_b_full, gm_w_s=gm_w_s[0], gm_b_s=gm_b_s[0])
    loss_parts, grad_x, g = _local_step(x[0], ctx[0], loss_target[0], mods, norm_g_full, get_w, small, put_grad)
    g["loss"] = (0.5 / d * jnp.sum(loss_parts)).reshape(1)

    whole = {"ffn_w_in": (tr(ffn_w_in), tr(m_ffn_w_in), tr(v_ffn_w_in)), "ffn_w_out": (ffn_w_out, m_ffn_w_out, v_ffn_w_out),
             "ssd_w_in": (tr(ssd_w_in), tr(m_ssd_w_in), tr(v_ssd_w_in)), "ssd_w_out": (ssd_w_out, m_ssd_w_out, v_ssd_w_out),
             "gm_w_in": (gm_w_in, m_gm_w_in, v_gm_w_in), "gm_w_out": (gm_w_out, m_gm_w_out, v_gm_w_out)}
    res = {}

    def update_units(some, after):
        for unit, grp, idx in some:
            parts = _exchange_wait(*scatters[unit], after, scatter=True, name="scatter_wait_" + unit)
            for k, p in zip(idx, parts):
                which = ("in", "out")[k]
                nm = ("ffn" if grp.startswith("ffn") else grp) + "_w_" + which
                sel = (int(grp[3]), int(grp[4])) if grp.startswith("ffn") else (0,)
                res[nm] = _adamw(p, *whole[nm], name=f"adamw_{grp}_{which}", sel=sel, into=res.get(nm))
                after = res[nm][0]
        return after

    sg_names = ["dmx", "dmc", "norm_g", "ssd_conv_w", "ssd_conv_b", "ssd_dt_bias", "ssd_A_log", "ssd_D", "ssd_norm_g",
                "gm_v_g", "gm_v_b", "gm_w_s", "gm_b_s", "loss"]
    sg_shapes = [g[n].shape for n in sg_names]
    flat = jnp.concatenate([g[n].reshape(-1) for n in sg_names])
    npack = flat.shape[0]
    pad = (-npack) % 1024
    flat = jnp.pad(flat, (0, pad)).reshape(-1, 128)
    sg_start = _exchange_start([flat], [_landing(flat, me)], scatter=False, name="small_grads_start")
    by_send = list(reversed(units))
    early_done = update_units(by_send[:4], jnp.stack([sg_start[4], grad_x[0, 0]]))
    (sg_all,) = _exchange_wait(*sg_start[:4], early_done, scatter=False, name="small_grads_wait")
    sg_sum = _sum_slots(sg_all, name="sum_small_grads").reshape(-1)[:npack]
    update_units(by_send[4:], sg_sum)
    sums = {}
    o = 0
    for n, shp in zip(sg_names, sg_shapes):
        sz = math.prod(shp)
        sums[n] = sg_sum[o:o + sz].reshape(shp)
        o += sz
    loss = sums["loss"][0]
    per_dev = sg_all.reshape(NDEV, -1)
    dmx_all =per_dev[:, :2 * N_MOD * d].reshape(NDEV, 2, N_MOD * d)
    dmc_all = per_dev[:, 2 * N_MOD * d:4 * N_MOD * d].reshape(NDEV, 2, N_MOD * d)

    (s16,) = _rowwise("ada_silu", lambda cc: ((_silu(cc),), ()), 2 * NDEV, [c16], [], [(d, F32)], tm=2 * NDEV)
    s16_t = s16.T
    g_ada_w, dcc_parts = [], []
    for i in range(2):
        rhs = jnp.concatenate([lax.dynamic_slice_in_dim(dmx_all[:, i], me * ncol, ncol, axis=1),
                               lax.dynamic_slice_in_dim(dmc_all[:, i], me * ncol, ncol, axis=1)], axis=0)
        g_ada_w.append(_mm_f32(s16_t, rhs, name=f"ada_dw{i}"))
        dmc_loc = lax.dynamic_slice_in_dim(sums["dmc"][i], me * ncol, ncol, axis=0)
        rhs_c = jnp.zeros((ncol, 128), F32).at[:, 0].set(dmc_loc)
        dcc_parts.append(_mm_f32(ada_w[i], rhs_c, name=f"ada_dcc{i}")[:, 0])
    g_ada_w = jnp.stack(g_ada_w)
    dcc_part = (dcc_parts[0] + dcc_parts[1]).reshape(8, 128)
    (dcc_all,), _ = _exchange([dcc_part], scatter=False, name="gather_dcc")
    g_c_ctx = _sum_slots(dcc_all, name="sum_dcc", scale_by=c_ctx.reshape(8, 128)).reshape(d)
    g_ada_b = sums["dmx"] + sums["dmc"]

    outs = _adamw(g_ada_w.reshape(1, -1, ncol), ada_w.reshape(-1, ncol), m_ada_w.reshape(-1, ncol),
                  v_ada_w.reshape(-1, ncol), name="adamw_ada_w")
    res["ada_w"] = [o_.reshape(ada_w.shape) for o_ in outs]

    loc = lambda a, ax, n: lax.dynamic_slice_in_dim(a, me * n, n, axis=ax)
    small_g = dict(c_ctx=g_c_ctx, ada_b=g_ada_b, norm_g=loc(sums["norm_g"], 2, 128),
                   ssd_conv_w=loc(sums["ssd_conv_w"], 1, 512)[None], ssd_conv_b=sums["ssd_conv_b"][None],
                   ssd_dt_bias=sums["ssd_dt_bias"][None], ssd_A_log=sums["ssd_A_log"][None], ssd_D=sums["ssd_D"][None],
                   ssd_norm_g=sums["ssd_norm_g"][None], gm_v_g=loc(sums["gm_v_g"], 0, 256)[None],
                   gm_v_b=loc(sums["gm_v_b"], 0, 256)[None], gm_w_s=sums["gm_w_s"][None], gm_b_s=sums["gm_b_s"][None])
    small_w = dict(c_ctx=(c_ctx, m_c_ctx, v_c_ctx), ada_b=(ada_b, m_ada_b, v_ada_b), norm_g=(norm_g, m_norm_g, v_norm_g),
                   ssd_conv_w=(ssd_conv_w, m_ssd_conv_w, v_ssd_conv_w), ssd_conv_b=(ssd_conv_b, m_ssd_conv_b, v_ssd_conv_b),
                   ssd_dt_bias=(ssd_dt_bias, m_ssd_dt_bias, v_ssd_dt_bias), ssd_A_log=(ssd_A_log, m_ssd_A_log, v_ssd_A_log),
                   ssd_D=(ssd_D, m_ssd_D, v_ssd_D), ssd_norm_g=(ssd_norm_g, m_ssd_norm_g, v_ssd_norm_g),
                   gm_v_g=(gm_v_g, m_gm_v_g, v_gm_v_g), gm_v_b=(gm_v_b, m_gm_v_b, v_gm_v_b),
                   gm_w_s=(gm_w_s, m_gm_w_s, v_gm_w_s), gm_b_s=(gm_b_s, m_gm_b_s, v_gm_b_s))
    sn = list(small_w)

    def pack(arrs):
        f = jnp.concatenate([a.reshape(-1) for a in arrs])
        return jnp.pad(f, (0, (-f.shape[0]) % 1024)).reshape(-1, 128)

    pg = pack([small_g[n].reshape(small_w[n][0].shape) for n in sn])
    outs = _adamw(pg[None], pack([small_w[n][0] for n in sn]), pack([small_w[n][1] for n in sn]),
                  pack([small_w[n][2] for n in sn]), name="adamw_small")
    flat_outs = [o_.reshape(-1) for o_ in outs]
    o = 0
    for n in sn:
        shp = small_w[n][0].shape
        sz = math.prod(shp)
        res[n] = [fo[o:o + sz].reshape(shp) for fo in flat_outs]
        o += sz

    order = ["c_ctx", "ada_w", "ada_b", "norm_g", "ffn_w_in", "ffn_w_out", "ssd_w_in", "ssd_conv_w", "ssd_conv_b",
             "ssd_dt_bias", "ssd_A_log", "ssd_D", "ssd_norm_g", "ssd_w_out", "gm_w_in", "gm_v_g", "gm_v_b", "gm_w_s",
             "gm_b_s", "gm_w_out"]
    for nm in ("ffn_w_in", "ssd_w_in"):
        res[nm] = [tr(a) for a in res[nm]]
    result = [loss, grad_x[None]]
    for k in range(4):
        result += [res[n][k] for n in order]
    return tuple(result)
```

```python
import functools
import math

import jax
import jax.numpy as jnp
from jax import lax
from jax.experimental import pallas as pl
from jax.experimental.pallas import tpu as pltpu

F32 = jnp.float32
BF16 = jnp.bfloat16

NDEV = 8
D_MODEL = 1024
FFN_DIM = 2816
N_MOD = 9
EPS = 1e-6
SSD_INNER = 2048
SSD_HEADS = 32
SSD_HEAD_DIM = 64
SSD_GROUPS = 8
SSD_HPG = 4
SSD_STATE = 128
SSD_CONV = 5
SSD_CONV_DIM = 4096
CHUNK = 128
GM_INNER = 2048
GM_GROUPS = 8
GM_GROUP_DIM = 256
ADAM_LR = 0.001
ADAM_B1 = 0.9
ADAM_B2 = 0.999
ADAM_EPS = 1e-08
ADAM_WD = 0.01
ADAM_STEP = 10
NEG_BIG = -1e30
VMEM_LIMIT_BYTES = 56 * 1024 * 1024
HI = lax.Precision.HIGHEST


def _params(*sem):
    return pltpu.CompilerParams(dimension_semantics=sem, vmem_limit_bytes=VMEM_LIMIT_BYTES)


def _pick(n, target, mult=16):
    if n <= target:
        return n
    for t in range(target - target % mult, 0, -mult):
        if n % t == 0:
            return t
    raise ValueError((n, target, mult))


def _sig(x):
    return 0.5 * jnp.tanh(0.5 * x) + 0.5


def _silu(x):
    return x * _sig(x)


def _dsilu(x):
    s = _sig(x)
    return s * (1.0 + x * (1.0 - s))


_GELU_C = math.sqrt(2.0 / math.pi)


def _gelu(x):
    return 0.5 * x * (1.0 + jnp.tanh(_GELU_C * (x + 0.044715 * x * x * x)))


def _dgelu(x):
    t = jnp.tanh(_GELU_C * (x + 0.044715 * x * x * x))
    return 0.5 * (1.0 + t) + 0.5 * x * (1.0 - t * t) * _GELU_C * (1.0 + 3.0 * 0.044715 * x * x)


def _softplus(x):
    return jnp.maximum(x, 0.0) + jnp.log1p(jnp.exp(-jnp.abs(x)))


def _sum0(v):
    return jnp.sum(v, axis=0, keepdims=True)


def _rms(h):
    r = lax.rsqrt(jnp.mean(h * h, axis=-1, keepdims=True) + EPS)
    return h * r, r


def _dot(a, b, dims=((1,), (0,)), precision=None):
    return lax.dot_general(a, b, (dims, ((), ())), preferred_element_type=F32, precision=precision)


_NT = ((1,), (1,))
_TN = ((0,), (0,))


def _rowwise(name, fn, n_rows, rows, consts, outs, accs=(), *, tm, nseg=1, seg_blocks=0):
    assert n_rows % tm == 0
    if nseg == 2:
        assert seg_blocks > 0
        seg = lambda i: jnp.where(i < seg_blocks, 0, 1)
    else:
        seg = lambda i: 0
    in_specs, args, lacking = [], [], []
    for r in rows:
        arr, width, cb, off = r if isinstance(r, tuple) else (r, r.shape[1], 0, 0)
        in_specs.append(pl.BlockSpec((tm, width), lambda i, cb=cb, off=off: (jnp.maximum(i + off, 0), cb)))
        args.append(arr)
        lacking.append(-off if off < 0 else 0)
    for kind, arr in consts:
        if kind == "seg":
            assert arr.shape[0] == nseg and arr.shape[1] == 1, arr.shape
            in_specs.append(pl.BlockSpec((None, 1, arr.shape[2]), lambda i: (seg(i), 0, 0)))
        else:
            in_specs.append(pl.BlockSpec(arr.shape, lambda i: (0, 0)))
        args.append(arr)
    out_shape = [jax.ShapeDtypeStruct((n_rows, w), dt) for w, dt in outs]
    out_specs = [pl.BlockSpec((tm, w), lambda i: (i, 0)) for w, _ in outs]
    out_shape += [jax.ShapeDtypeStruct((nseg, 1, w), F32) for w in accs]
    out_specs += [pl.BlockSpec((None, 1, w), lambda i: (seg(i), 0, 0)) for w in accs]
    n_in, n_out, n_acc = len(args), len(outs), len(accs)

    def kern(*refs):
        i = pl.program_id(0)
        ins = [r[...] for r in refs[:n_in]]
        for k, lack in enumerate(lacking):
            if lack:
                ins[k] = jnp.where(i >= lack, ins[k], jnp.zeros_like(ins[k]))
        res, terms = fn(*ins)
        for ref, v in zip(refs[n_in:n_in + n_out], res):
            ref[...] = v.astype(ref.dtype)
        if n_acc:
            sums = [_sum0(v) for v in terms]
            first = (i == 0) | (i == seg_blocks) if nseg == 2 else (i == 0)
            acc_refs = refs[n_in + n_out:]

            @pl.when(first)
            def _():
                for ref, v in zip(acc_refs, sums):
                    ref[...] = v

            @pl.when(jnp.logical_not(first))
            def _():
                for ref, v in zip(acc_refs, sums):
                    ref[...] += v

    res = pl.pallas_call(
        kern, name=name, grid=(n_rows // tm,), in_specs=in_specs, out_specs=out_specs, out_shape=out_shape,
        compiler_params=_params("arbitrary"),
    )(*args)
    return res


def _pre_fwd_fn(h, g, shift, scale):
    hh, _ = _rms(h)
    return (hh * g * (1.0 + scale) + shift,), ()


def _pre_bwd_fn(du, h, dres, g, scale):
    hh, r = _rms(h)
    n = hh * g
    dn = du * (1.0 + scale)
    dhh = dn * g
    dh = dres + r * (dhh - hh * jnp.mean(dhh * hh, axis=-1, keepdims=True))
    return (dh,), (du, du * n, dn * hh)


def _post_fwd_fn(weight, h, y, g, gate):
    yh, _ = _rms(y)
    return (h + weight * gate * (yh * g),), ()


def _out_post_fn(weight, y, h, g, gate):
    return (y,) + _post_fwd_fn(weight, h, y, g, gate)[0], ()


def _post_bwd_fn(weight, dh, y, g, gate):
    yh, r = _rms(y)
    dr = dh * weight
    dyh = dr * gate * g
    dy = r * (dyh - yh * jnp.mean(dyh * yh, axis=-1, keepdims=True))
    return (dy,), (dr * yh * g, dr * gate * yh)


def _glu_bwd_fn(ds, a, b):
    a = a.astype(F32)
    b = b.astype(F32)
    sg = _sig(a)
    da = ds * b * (sg * (1.0 + a * (1.0 - sg)))
    db = ds * (a * sg)
    return (jnp.concatenate([da, db], axis=1),), ()


def _loss_fn(y, t):
    diff = y - t
    return (diff * (1.0 / D_MODEL),), (diff * diff,)


def _ssd_y(yf, yb, xs, z, dvec):
    y = yf + yb + dvec * xs
    return y, y * _silu(z)


def _ssdgate_fwd_fn(yf, yb, xs, z, dvec, ng):
    _, yg = _ssd_y(yf, yb, xs, z, dvec)
    parts = []
    for g in range(SSD_GROUPS):
        sl = slice(g * 256, (g + 1) * 256)
        parts.append(_rms(yg[:, sl])[0])
    return (jnp.concatenate(parts, axis=1) * ng,), ()


def _ssdgate_bwd_fn(dyn, yf, yb, xs, z, dvec, ng):
    y, yg = _ssd_y(yf, yb, xs, z, dvec)
    dyg_parts, ygh_parts = [], []
    for g in range(SSD_GROUPS):
        sl = slice(g * 256, (g + 1) * 256)
        ygh, r = _rms(yg[:, sl])
        d = dyn[:, sl] * ng[:, sl]
        dyg_parts.append(r * (d - ygh * jnp.mean(d * ygh, axis=-1, keepdims=True)))
        ygh_parts.append(ygh)
    dyg = jnp.concatenate(dyg_parts, axis=1)
    ygh = jnp.concatenate(ygh_parts, axis=1)
    dy = dyg * _silu(z)
    dz = dyg * y * _dsilu(z)
    return (dy, dz), (dyn * ygh, dy * xs)


def _ln_stats(v):
    mu = jnp.mean(v, axis=-1, keepdims=True)
    vc = v - mu
    r = lax.rsqrt(jnp.mean(vc * vc, axis=-1, keepdims=True) + EPS)
    return vc * r, r


def _gm_act_fwd_fn(p, vg, vb):
    gu = _gelu(p[:, :GM_INNER])
    gvh, _ = _ln_stats(_gelu(p[:, GM_INNER:]))
    return (gu, gvh * vg + vb), ()


def _gm_act_bwd_fn(p, dgu, dgvn, vg):
    pu = p[:, :GM_INNER]
    pv = p[:, GM_INNER:]
    gvh, r = _ln_stats(_gelu(pv))
    dgvh = dgvn * vg
    dgv = r * (dgvh - jnp.mean(dgvh, axis=-1, keepdims=True) - gvh * jnp.mean(dgvh * gvh, axis=-1, keepdims=True))
    dp = jnp.concatenate([dgu * _dgelu(pu), dgv * _dgelu(pv)], axis=1)
    return (dp,), (dgvn * gvh, dgvn)


def _mm(a, b, *, out_dtype, name, tm=1088, tn=1024, tk=1408, add=None, rhs_t=False, n=None, b_off=(0, 0)):
    m, k = a.shape
    if n is None:
        n, k2 = b.shape if rhs_t else b.shape[::-1]
        assert k == k2
    tm, tn, tk = _pick(m, tm), _pick(n, tn, 128), _pick(k, tk, 128)
    o0, o1 = b_off
    nk = k // tk
    dims = _NT if rhs_t else ((1,), (0,))

    def kern(*refs):
        a_ref, b_ref = refs[:2]
        add_ref = refs[2] if add is not None else None
        o_ref = refs[3] if add is not None else refs[2]

        def finish(r):
            if add is not None:
                r = r + add_ref[...]
            o_ref[...] = r.astype(o_ref.dtype)

        p = _dot(a_ref[...], b_ref[...], dims)
        if nk == 1:
            finish(p)
            return
        acc_ref = refs[-1]
        kk = pl.program_id(2)

        @pl.when(kk == 0)
        def _():
            acc_ref[...] = p

        @pl.when((kk > 0) & (kk < nk - 1))
        def _():
            acc_ref[...] += p

        @pl.when(kk == nk - 1)
        def _():
            finish(acc_ref[...] + p)

    if rhs_t:
        b_spec = pl.BlockSpec((tn, tk), lambda i, j, kk: (j + o0, kk + o1))
    else:
        b_spec = pl.BlockSpec((tk, tn), lambda i, j, kk: (kk + o0, j + o1))
    in_specs = [pl.BlockSpec((tm, tk), lambda i, j, kk: (i, kk)), b_spec]
    args = [a, b]
    if add is not None:
        in_specs.append(pl.BlockSpec((tm, tn), lambda i, j, kk: (i, j)))
        args.append(add)
    return pl.pallas_call(
        kern, name=name, grid=(m // tm, n // tn, nk), in_specs=in_specs,
        out_specs=pl.BlockSpec((tm, tn), lambda i, j, kk: (i, j)),
        out_shape=jax.ShapeDtypeStruct((m, n), out_dtype),
        scratch_shapes=[pltpu.VMEM((tm, tn), F32)] if nk > 1 else [],
        compiler_params=_params("parallel", "parallel", "arbitrary"),
    )(*args)


def _mm_rows(a, b, fn, rows, consts, outs, accs=(), *, name, tm=544, tk=1408, rhs_t=False, n_ctx=0):
    m, k = a.shape
    n = b.shape[0] if rhs_t else b.shape[1]
    tm, tk = _pick(m, tm), _pick(k, tk, 128)
    nk = k // tk
    dims = _NT if rhs_t else ((1,), (0,))
    n_rows, n_const, n_out, n_acc = len(rows), len(consts), len(outs), len(accs)

    def kern(*refs):
        a_ref, b_ref = refs[:2]
        row_refs = refs[2:2 + n_rows]
        const_refs = refs[2 + n_rows:2 + n_rows + n_const]
        out_refs = refs[2 + n_rows + n_const:2 + n_rows + n_const + n_out]
        acc_refs = refs[2 + n_rows + n_const + n_out:2 + n_rows + n_const + n_out + n_acc]
        i, kk = pl.program_id(0), pl.program_id(1)

        def finish(p, rs=slice(None), r0=0):
            nr = p.shape[0]
            is_ctx = (i * tm + r0 + lax.broadcasted_iota(jnp.int32, (nr, 1), 0)) < n_ctx
            cvals = []
            for (kind, arr), ref in zip(consts, const_refs):
                if kind == "seg":
                    cvals.append(jnp.where(is_ctx, ref[0], ref[1]) if arr.shape[0] == 2 else ref[0])
                else:
                    cvals.append(ref[...])
            res, terms = fn(p, *[r[rs, :] for r in row_refs], *cvals)
            for ref, v in zip(out_refs, res):
                ref[rs, :] = v.astype(ref.dtype)
            for ref, v in zip(acc_refs, terms):
                s_all = _sum0(v)
                s_ctx = _sum0(jnp.where(is_ctx, v, 0.0)) if n_ctx else jnp.zeros_like(s_all)
                both = jnp.concatenate([s_ctx, s_all - s_ctx], axis=0)[:, None, :]

                @pl.when(i == 0)
                def _():
                    ref[...] = both

                @pl.when(i > 0)
                def _():
                    ref[...] += both

        if nk == 1 and n_acc == 0:
            nsub = next(s for s in (4, 2, 1) if tm % (16 * s) == 0)
            sub = tm // nsub
            for r in range(nsub):
                rs = slice(r * sub, (r + 1) * sub)
                finish(_dot(a_ref[rs, :], b_ref[...], dims), rs, r * sub)
            return
        p = _dot(a_ref[...], b_ref[...], dims)
        if nk == 1:
            finish(p)
            return
        scr = refs[-1]

        @pl.when(kk == 0)
        def _():
            scr[...] = p

        @pl.when((kk > 0) & (kk < nk - 1))
        def _():
            scr[...] += p

        @pl.when(kk == nk - 1)
        def _():
            finish(scr[...] + p)

    b_spec = pl.BlockSpec((n, tk), lambda i, kk: (0, kk)) if rhs_t else pl.BlockSpec((tk, n), lambda i, kk: (kk, 0))
    in_specs = [pl.BlockSpec((tm, tk), lambda i, kk: (i, kk)), b_spec]
    in_specs += [pl.BlockSpec((tm, r.shape[1]), lambda i, kk: (i, 0)) for r in rows]
    for kind, arr in consts:
        in_specs.append(pl.BlockSpec(arr.shape, (lambda i, kk: (0, 0, 0)) if kind == "seg" else (lambda i, kk: (0, 0))))
    out_shape = [jax.ShapeDtypeStruct((m, w), dt) for w, dt in outs]
    out_specs = [pl.BlockSpec((tm, w), lambda i, kk: (i, 0)) for w, _ in outs]
    out_shape += [jax.ShapeDtypeStruct((2, 1, w), F32) for w in accs]
    out_specs += [pl.BlockSpec((2, 1, w), lambda i, kk: (0, 0, 0)) for w in accs]
    return pl.pallas_call(
        kern, name=name, grid=(m // tm, nk), in_specs=in_specs, out_specs=out_specs, out_shape=out_shape,
        scratch_shapes=[pltpu.VMEM((tm, n), F32)] if nk > 1 else [],
        compiler_params=_params("arbitrary", "arbitrary"),
    )(a, b, *rows, *[arr for _, arr in consts])


def _mm_glu(u, win_t, *, name, tm=2176, tn=256):
    m, k = u.shape
    n = win_t.shape[0] // 2
    tm, tn = _pick(m, tm), _pick(n, tn, 128)
    nj = n // tn

    nsub = 4 if tm % 64 == 0 else 1
    sub = tm // nsub

    def kern(u_ref, wa_ref, wb_ref, s_ref, a_ref, b_ref):
        for r in range(nsub):
            rows = slice(r * sub, (r + 1) * sub)
            uu = u_ref[rows, :]
            a = _dot(uu, wa_ref[...], _NT)
            b = _dot(uu, wb_ref[...], _NT)
            s_ref[rows, :] = (_silu(a) * b).astype(BF16)
            a_ref[rows, :] = a.astype(BF16)
            b_ref[rows, :] = b.astype(BF16)

    ospec = pl.BlockSpec((tm, tn), lambda i, j: (i, j))
    return pl.pallas_call(
        kern, name=name, grid=(m // tm, nj),
        in_specs=[pl.BlockSpec((tm, k), lambda i, j: (i, 0)), pl.BlockSpec((tn, k), lambda i, j: (j, 0)),
                  pl.BlockSpec((tn, k), lambda i, j: (nj + j, 0))],
        out_specs=[ospec, ospec, ospec],
        out_shape=[jax.ShapeDtypeStruct((m, n), BF16)] * 3,
        compiler_params=_params("parallel", "parallel"),
    )(u, win_t, win_t)


def _mm_tn(a, b, *, name, tm=1024, tn=1024, tk=1088, col_blocks=None):
    t, m = a.shape
    t2, n = b.shape
    assert t == t2
    tm, tn, tk = _pick(m, tm, 128), _pick(n, tn, 128), _pick(t, tk)
    nk = t // tk
    if col_blocks is None:
        def kern(a_ref, b_ref, o_ref):
            kk = pl.program_id(2)

            @pl.when(kk == 0)
            def _():
                o_ref[...] = jnp.zeros_like(o_ref)

            o_ref[...] += _dot(a_ref[...], b_ref[...], _TN)

        out_spec = pl.BlockSpec((tm, tn), lambda i, j, kk: (i, j))
        out_shape = jax.ShapeDtypeStruct((m, n), F32)
        scratch = []
    else:
        wb = n // col_blocks
        per = tn // wb
        assert tn % wb == 0 and wb % 8 == 0

        def kern(a_ref, b_ref, o_ref, acc_ref):
            kk = pl.program_id(2)
            p = _dot(a_ref[...], b_ref[...], _TN)

            @pl.when(kk == 0)
            def _():
                acc_ref[...] = p

            @pl.when((kk > 0) & (kk < nk - 1))
            def _():
                acc_ref[...] += p

            @pl.when(kk == nk - 1)
            def _():
                r = acc_ref[...] + p if nk > 1 else p
                for c in range(per):
                    o_ref[c] = r[:, c * wb:(c + 1) * wb].astype(BF16)

        out_spec = pl.BlockSpec((per, tm, wb), lambda i, j, kk: (j, i, 0))
        out_shape = jax.ShapeDtypeStruct((col_blocks, m, wb), BF16)
        scratch = [pltpu.VMEM((tm, tn), F32)]

    return pl.pallas_call(
        kern, name=name, grid=(m // tm, n // tn, nk),
        in_specs=[pl.BlockSpec((tk, tm), lambda i, j, kk: (kk, i)), pl.BlockSpec((tk, tn), lambda i, j, kk: (kk, j))],
        out_specs=out_spec, out_shape=out_shape, scratch_shapes=scratch,
        compiler_params=_params("parallel", "parallel", "arbitrary"),
    )(a, b)


def _mm_f32(a, b, *, name, silu_a=False, bias=None):
    m, k = a.shape
    n = b.shape[1]

    def kern(*refs):
        if bias is None:
            a_ref, b_ref, o_ref = refs
        else:
            a_ref, b_ref, bias_ref, o_ref = refs
        av = a_ref[...]
        if silu_a:
            av = _silu(av)
        r = jnp.dot(av, b_ref[...], preferred_element_type=F32, precision=HI)
        if bias is not None:
            r = r + bias_ref[...]
        o_ref[...] = r

    args = [a, b] + ([] if bias is None else [bias])
    return pl.pallas_call(kern, name=name, out_shape=jax.ShapeDtypeStruct((m, n), F32),
                          compiler_params=pltpu.CompilerParams(vmem_limit_bytes=VMEM_LIMIT_BYTES))(*args)


CONV_WIN = 32


def _conv_windows(n, n_ctx):
    assert n_ctx % CONV_WIN == 0 and n_ctx >= CONV_WIN and n - n_ctx >= CONV_WIN
    return (0, n_ctx - CONV_WIN // 2, n - CONV_WIN)


def _tap_outside(r0, s, n, n_ctx):
    t = r0 + lax.broadcasted_iota(jnp.int32, (CONV_WIN, 1), 0)
    lo = jnp.where(t < n_ctx, 0, n_ctx)
    hi = jnp.where(t < n_ctx, n_ctx, n)
    return jnp.where((t + s >= lo) & (t + s < hi), 0.0, 1.0)


def _rolled(v, s):
    return v if s == 0 else pltpu.roll(v, (-s) % v.shape[0], 0)


def _conv_fwd(xp, w8, b, *, n_ctx, name, cb=256):
    n, c = xp.shape
    half = SSD_CONV // 2

    def kern(x_ref, w_ref, b_ref, cpre_ref, act_ref):
        x = x_ref[...]
        acc = jnp.zeros_like(x) + b_ref[...]
        rolled = {}
        for k in range(SSD_CONV):
            rolled[k] = _rolled(x, k - half)
            acc = acc + rolled[k] * w_ref[k:k + 1, :]
        cpre_ref[...] = acc
        act_ref[...] = _silu(acc)
        for r0 in _conv_windows(n, n_ctx):
            rows = slice(r0, r0 + CONV_WIN)
            fix = acc[rows]
            for k in range(SSD_CONV):
                if k != half:
                    fix = fix - rolled[k][rows] * w_ref[k:k + 1, :] * _tap_outside(r0, k - half, n, n_ctx)
            cpre_ref[rows, :] = fix
            act_ref[rows, :] = _silu(fix)

    spec = pl.BlockSpec((n, cb), lambda j: (0, j))
    return pl.pallas_call(
        kern, name=name, grid=(c // cb,),
        in_specs=[spec, pl.BlockSpec((8, cb), lambda j: (0, j)), pl.BlockSpec((1, cb), lambda j: (0, j))],
        out_specs=[spec, spec], out_shape=[jax.ShapeDtypeStruct((n, c), F32)] * 2,
        compiler_params=_params("parallel"),
    )(xp, w8, b)


def _conv_bwd(d1, d2, cpre, xp, w8, *, n_ctx, name, cb=128):
    n, c = xp.shape
    half = SSD_CONV // 2

    def kern(d1_ref, d2_ref, cpre_ref, x_ref, w_ref, dx_ref, dw_ref, db_ref):
        g = (d1_ref[...] + d2_ref[...]) * _dsilu(cpre_ref[...])
        x = x_ref[...]
        dx = jnp.zeros_like(g)
        dw_ref[...] = jnp.zeros_like(dw_ref)
        g_rolled = {}
        for k in range(SSD_CONV):
            s = k - half
            g_rolled[k] = _rolled(g, -s)
            dx = dx + g_rolled[k] * w_ref[k:k + 1, :]
            xr = _rolled(x, s)
            dw = _sum0(g * xr)
            if s != 0:
                for r0 in _conv_windows(n, n_ctx):
                    rows = slice(r0, r0 + CONV_WIN)
                    dw = dw - _sum0(g[rows] * xr[rows] * _tap_outside(r0, s, n, n_ctx))
            dw_ref[k:k + 1, :] = dw
        dx_ref[...] = dx.astype(BF16)
        for r0 in _conv_windows(n, n_ctx):
            rows = slice(r0, r0 + CONV_WIN)
            fix = dx[rows]
            for k in range(SSD_CONV):
                if k != half:
                    fix = fix - g_rolled[k][rows] * w_ref[k:k + 1, :] * _tap_outside(r0, half - k, n, n_ctx)
            dx_ref[rows, :] = fix.astype(BF16)
        db_ref[...] = _sum0(g)

    spec = pl.BlockSpec((n, cb), lambda j: (0, j))
    return pl.pallas_call(
        kern, name=name, grid=(c // cb,),
        in_specs=[spec, spec, spec, spec, pl.BlockSpec((8, cb), lambda j: (0, j))],
        out_specs=[spec, pl.BlockSpec((8, cb), lambda j: (0, j)), pl.BlockSpec((1, cb), lambda j: (0, j))],
        out_shape=[jax.ShapeDtypeStruct((n, c), BF16), jax.ShapeDtypeStruct((8, c), F32),
                   jax.ShapeDtypeStruct((1, c), F32)],
        compiler_params=_params("parallel"),
    )(d1, d2, cpre, xp, w8)


def _chunk_of(s, nc, n_ctx_chunks, rev):
    if not rev:
        return s
    return jnp.where(s < n_ctx_chunks, n_ctx_chunks - 1 - s, nc - 1 - (s - n_ctx_chunks))


def _scan_common(dt_raw, dtT_raw, bias_r, bias_c, alog_r, alog_c, rev):
    ii = lax.broadcasted_iota(jnp.int32, (CHUNK, CHUNK), 0)
    jj = lax.broadcasted_iota(jnp.int32, (CHUNK, CHUNK), 1)
    tri = (jj >= ii) if rev else (jj <= ii)
    tri_t = (ii >= jj) if rev else (ii <= jj)
    a_r = -jnp.exp(alog_r)
    a_c = -jnp.exp(alog_c)
    dt = _softplus(dt_raw + bias_r)
    dt_t = _softplus(dtT_raw + bias_c)
    al = dt * a_r
    acum = _dot(tri.astype(F32), al, precision=HI)
    acum_t = _dot(dt_t * a_c, tri_t.astype(F32), precision=HI)
    atot = _sum0(al)
    return tri, tri_t, a_r, dt, acum, acum_t, atot


def _head_spread():
    return jnp.repeat(jnp.eye(SSD_HEADS, dtype=BF16), SSD_HEAD_DIM, axis=1)


def _dot_sel(v, sel):
    hi = v.astype(BF16)
    lo = (v - hi.astype(F32)).astype(BF16)
    return _dot(hi, sel) + _dot(lo, sel)


def _ssd_scan_fwd(xbc, dt_raw, dtT_raw, bias_r, bias_c, alog_r, alog_c, *, rev, n_ctx_chunks, name):
    n = xbc.shape[0]
    nc = n // CHUNK
    cidx = functools.partial(_chunk_of, nc=nc, n_ctx_chunks=n_ctx_chunks, rev=rev)

    def kern(xs_ref, b_ref, c_ref, dt_ref, dtT_ref, br_ref, bc_ref, ar_ref, ac_ref, e_ref, y_ref, hs_ref, h_scr):
        @pl.when(pl.program_id(0) == 0)
        def _():
            h_scr[...] = jnp.zeros_like(h_scr)

        tri, _, _, dt, acum, acum_t, atot = _scan_common(
            dt_ref[...], dtT_ref[...], br_ref[...], bc_ref[...], ar_ref[...], ac_ref[...], rev)
        etot = jnp.exp(atot)
        spread = lambda v: _dot_sel(v, e_ref[...])
        xdt_all = xs_ref[...] * spread(dt)
        eax = spread(jnp.exp(acum))
        xdw_all = xdt_all * spread(jnp.exp(atot - acum))
        hs_ref[...] = h_scr[...]
        for g in range(SSD_GROUPS):
            gs = slice(g * 256, (g + 1) * 256)
            bg = b_ref[:, g * SSD_STATE:(g + 1) * SSD_STATE].astype(BF16)
            cg = c_ref[:, g * SSD_STATE:(g + 1) * SSD_STATE].astype(BF16)
            cb = _dot(cg, bg, _NT)
            h4 = h_scr[gs, :]
            ys = []
            for k in range(SSD_HPG):
                h = g * SSD_HPG + k
                lmat = jnp.exp(jnp.where(tri, acum[:, h:h + 1] - acum_t[h:h + 1, :], NEG_BIG))
                xdt_h = xdt_all[:, h * SSD_HEAD_DIM:(h + 1) * SSD_HEAD_DIM].astype(BF16)
                ys.append(_dot((cb * lmat).astype(BF16), xdt_h))
            y_ref[:, gs] = jnp.concatenate(ys, axis=1) + _dot(cg, h4.astype(BF16), _NT) * eax[:, gs]
            s4 = _dot(xdw_all[:, gs].astype(BF16), bg, _TN)
            for k in range(SSD_HPG):
                h = g * SSD_HPG + k
                rs = slice(h * SSD_HEAD_DIM, (h + 1) * SSD_HEAD_DIM)
                h_scr[rs, :] = h4[k * SSD_HEAD_DIM:(k + 1) * SSD_HEAD_DIM] * etot[:, h:h + 1] + \
                    s4[k * SSD_HEAD_DIM:(k + 1) * SSD_HEAD_DIM]

    nh = SSD_HEADS
    small = lambda shape: pl.BlockSpec(shape, lambda s: (0, 0))
    return pl.pallas_call(
        kern, name=name, grid=(nc,),
        in_specs=[pl.BlockSpec((CHUNK, SSD_INNER), lambda s: (cidx(s), 0)),
                  pl.BlockSpec((CHUNK, 1024), lambda s: (cidx(s), 2)),
                  pl.BlockSpec((CHUNK, 1024), lambda s: (cidx(s), 3)),
                  pl.BlockSpec((CHUNK, nh), lambda s: (cidx(s), 0)),
                  pl.BlockSpec((nh, CHUNK), lambda s: (0, cidx(s))),
                  small((1, nh)), small((nh, 1)), small((1, nh)), small((nh, 1)), small((nh, SSD_INNER))],
        out_specs=[pl.BlockSpec((CHUNK, SSD_INNER), lambda s: (cidx(s), 0)),
                   pl.BlockSpec((None, SSD_INNER, SSD_STATE), lambda s: (s, 0, 0))],
        out_shape=[jax.ShapeDtypeStruct((n, SSD_INNER), F32),
                   jax.ShapeDtypeStruct((nc, SSD_INNER, SSD_STATE), F32)],
        scratch_shapes=[pltpu.VMEM((SSD_INNER, SSD_STATE), F32)],
        compiler_params=_params("arbitrary"),
    )(xbc, xbc, xbc, dt_raw, dtT_raw, bias_r, bias_c, alog_r, alog_c, _head_spread())


def _ssd_scan_bwd(dy, xbc, hs, dt_raw, dtT_raw, bias_r, bias_c, alog_r, alog_c, dvec, *, rev, n_ctx_chunks,
                  direct, name):
    n = xbc.shape[0]
    nc = n // CHUNK
    nh = SSD_HEADS
    step_of = lambda r: nc - 1 - r
    cidx = lambda r: _chunk_of(step_of(r), nc, n_ctx_chunks, rev)

    def kern(dy_ref, xs_ref, b_ref, c_ref, hs_ref, dt_ref, dtT_ref, br_ref, bc_ref, ar_ref, ac_ref, dv_ref,
             e_ref, et_ref, dx_ref, ddt_ref, dal_ref, dbias_ref, dh_scr):
        @pl.when(pl.program_id(0) == 0)
        def _():
            dh_scr[...] = jnp.zeros_like(dh_scr)
            dal_ref[...] = jnp.zeros_like(dal_ref)
            dbias_ref[...] = jnp.zeros_like(dbias_ref)

        tri, tri_t, a_r, dt, acum, acum_t, atot = _scan_common(
            dt_ref[...], dtT_ref[...], br_ref[...], bc_ref[...], ar_ref[...], ac_ref[...], rev)
        etot = jnp.exp(atot)
        spread = lambda v: _dot_sel(v, e_ref[...])
        gather = lambda v: _dot_sel(v, et_ref[...])
        xs_all = xs_ref[...]
        dy_all = dy_ref[...]
        dtx = spread(dt)
        eax = spread(jnp.exp(acum))
        decx = spread(jnp.exp(atot - acum))
        xdt_all = xs_all * dtx
        xdw_all = xdt_all * decx
        dyo_all = dy_all * eax
        lane = lax.broadcasted_iota(jnp.int32, (CHUNK, nh), 1)
        lane1 = lax.broadcasted_iota(jnp.int32, (1, nh), 1)
        sub = lax.broadcasted_iota(jnp.int32, (nh, CHUNK), 0)
        g_rows = jnp.zeros((CHUNK, nh), F32)
        g_cols = jnp.zeros((nh, CHUNK), F32)
        dtot = jnp.zeros((1, nh), F32)
        q_col, q_e, q_dt = [], [], []
        for g in range(SSD_GROUPS):
            gs = slice(g * 256, (g + 1) * 256)
            bg = b_ref[:, g * SSD_STATE:(g + 1) * SSD_STATE].astype(BF16)
            cg = c_ref[:, g * SSD_STATE:(g + 1) * SSD_STATE].astype(BF16)
            cb = _dot(cg, bg, _NT)
            hs4 = hs_ref[gs, :]
            dh4 = dh_scr[gs, :]
            hs4_bf = hs4.astype(BF16)
            dh4_bf = dh4.astype(BF16)
            dy4 = dy_all[:, gs]
            dy4_bf = dy4.astype(BF16)
            xdt4_bf = xdt_all[:, gs].astype(BF16)
            xdw4 = xdw_all[:, gs]
            xdw4_bf = xdw4.astype(BF16)
            dyo4_bf = dyo_all[:, gs].astype(BF16)
            yoff4 = _dot(cg, hs4_bf, _NT) * eax[:, gs]
            dcg = _dot(dyo4_bf, hs4_bf)
            dh_new4 = _dot(dyo4_bf, cg, _TN)
            bdh4 = _dot(bg, dh4_bf, _NT)
            dbg = _dot(xdw4_bf, dh4_bf)
            e4 = xdw4 * bdh4
            q_col.append(dy4 * yoff4 - e4)
            q_e.append(e4)
            hsum = jnp.sum(dh4 * hs4, axis=1, keepdims=True)
            dcb = jnp.zeros((CHUNK, CHUNK), F32)
            dxdts = []
            for k in range(SSD_HPG):
                h = g * SSD_HPG + k
                ks = slice(k * SSD_HEAD_DIM, (k + 1) * SSD_HEAD_DIM)
                lmat = jnp.exp(jnp.where(tri, acum[:, h:h + 1] - acum_t[h:h + 1, :], NEG_BIG))
                mf = cb * lmat
                dm = _dot(dy4_bf[:, ks], xdt4_bf[:, ks], _NT)
                dcb = dcb + dm * lmat
                gmat = dm * mf
                g_rows = g_rows + jnp.where(lane == h, jnp.sum(gmat, axis=1, keepdims=True), 0.0)
                g_cols = g_cols + jnp.where(sub == h, _sum0(gmat), 0.0)
                dxdts.append(_dot(mf.astype(BF16), dy4_bf[:, ks], _TN))
                et = etot[:, h:h + 1]
                dtot = dtot + jnp.where(lane1 == h, _sum0(hsum[ks]) * et, 0.0)
                dh_scr[h * SSD_HEAD_DIM:(h + 1) * SSD_HEAD_DIM, :] = dh4[ks] * et + dh_new4[ks]
            dxdt4 = jnp.concatenate(dxdts, axis=1) + bdh4 * decx[:, gs]
            q_dt.append(dxdt4 * xs_all[:, gs])
            dx4 = dxdt4 * dtx[:, gs]
            if direct:
                dx4 = dx4 + dy4 * dv_ref[:, gs]
            dcb_bf = dcb.astype(BF16)
            dx_ref[:, gs] = dx4
            dx_ref[:, SSD_INNER + g * SSD_STATE:SSD_INNER + (g + 1) * SSD_STATE] = dbg + _dot(dcb_bf, cg, _TN)
            dx_ref[:, SSD_INNER + 1024 + g * SSD_STATE:SSD_INNER + 1024 + (g + 1) * SSD_STATE] = \
                dcg + _dot(dcb_bf, bg)
        e_heads = gather(jnp.concatenate(q_e, axis=1))
        dacum = gather(jnp.concatenate(q_col, axis=1)) + g_rows - g_cols.T
        dal = _dot(tri_t.astype(F32), dacum, precision=HI) + dtot + _sum0(e_heads)
        ddt = gather(jnp.concatenate(q_dt, axis=1)) + dal * a_r
        ddt_raw = ddt * _sig(dt_ref[...] + br_ref[...])
        ddt_ref[...] = ddt_raw
        dal_ref[...] += _sum0(dal * dt) * a_r
        dbias_ref[...] += _sum0(ddt_raw)

    small = lambda shape: pl.BlockSpec(shape, lambda r: (0, 0))
    return pl.pallas_call(
        kern, name=name, grid=(nc,),
        in_specs=[pl.BlockSpec((CHUNK, SSD_INNER), lambda r: (cidx(r), 0)),
                  pl.BlockSpec((CHUNK, SSD_INNER), lambda r: (cidx(r), 0)),
                  pl.BlockSpec((CHUNK, 1024), lambda r: (cidx(r), 2)),
                  pl.BlockSpec((CHUNK, 1024), lambda r: (cidx(r), 3)),
                  pl.BlockSpec((None, SSD_INNER, SSD_STATE), lambda r: (step_of(r), 0, 0)),
                  pl.BlockSpec((CHUNK, nh), lambda r: (cidx(r), 0)),
                  pl.BlockSpec((nh, CHUNK), lambda r: (0, cidx(r))),
                  small((1, nh)), small((nh, 1)), small((1, nh)), small((nh, 1)), small((1, SSD_INNER)),
                  small((nh, SSD_INNER)), small((SSD_INNER, nh))],
        out_specs=[pl.BlockSpec((CHUNK, SSD_CONV_DIM), lambda r: (cidx(r), 0)),
                   pl.BlockSpec((CHUNK, nh), lambda r: (cidx(r), 0)),
                   small((1, nh)), small((1, nh))],
        out_shape=[jax.ShapeDtypeStruct((n, SSD_CONV_DIM), F32), jax.ShapeDtypeStruct((n, nh), F32),
                   jax.ShapeDtypeStruct((1, nh), F32), jax.ShapeDtypeStruct((1, nh), F32)],
        scratch_shapes=[pltpu.VMEM((SSD_INNER, SSD_STATE), F32)],
        compiler_params=_params("arbitrary"),
    )(dy, xbc, xbc, xbc, hs, dt_raw, dtT_raw, bias_r, bias_c, alog_r, alog_c, dvec, _head_spread(),
      _head_spread().T)


def _gm_spatial_fwd(gu, gvn, ws, bst, *, name):
    n = gu.shape[0]

    def kern(gu_ref, gv_ref, ws_ref, bs_ref, o_ref):
        for g in range(GM_GROUPS):
            sl = slice(g * GM_GROUP_DIM, (g + 1) * GM_GROUP_DIM)
            s = _dot(ws_ref[g], gv_ref[:, sl]) + bs_ref[:, g:g + 1]
            o_ref[:, sl] = (gu_ref[:, sl] * s).astype(BF16)

    spec = pl.BlockSpec((CHUNK, GM_INNER), lambda i: (i, 0))
    return pl.pallas_call(
        kern, name=name, grid=(n // CHUNK,),
        in_specs=[spec, spec, pl.BlockSpec(ws.shape, lambda i: (0, 0, 0)), pl.BlockSpec(bst.shape, lambda i: (0, 0))],
        out_specs=spec, out_shape=jax.ShapeDtypeStruct((n, GM_INNER), BF16),
        compiler_params=_params("parallel"),
    )(gu, gvn, ws, bst)


def _gm_spatial_bwd(dt, gu, gvn, ws, wst, bst, *, name):
    n = gu.shape[0]

    def kern(dt_ref, gu_ref, gv_ref, ws_ref, wst_ref, bs_ref, dgu_ref, dgv_ref, dws_ref, dbs_ref):
        @pl.when(pl.program_id(0) == 0)
        def _():
            dws_ref[...] = jnp.zeros_like(dws_ref)
            dbs_ref[...] = jnp.zeros_like(dbs_ref)

        lane = lax.broadcasted_iota(jnp.int32, (CHUNK, GM_GROUPS), 1)
        dbs = jnp.zeros((CHUNK, GM_GROUPS), F32)
        for g in range(GM_GROUPS):
            sl = slice(g * GM_GROUP_DIM, (g + 1) * GM_GROUP_DIM)
            gv = gv_ref[:, sl]
            s = _dot(ws_ref[g], gv) + bs_ref[:, g:g + 1]
            d = dt_ref[:, sl]
            dgu_ref[:, sl] = d * s
            ds = d * gu_ref[:, sl]
            ds_bf = ds.astype(BF16)
            dws_ref[g] += _dot(ds_bf, gv, _NT)
            dgv_ref[:, sl] = _dot(wst_ref[g], ds_bf)
            dbs = dbs + jnp.where(lane == g, jnp.sum(ds, axis=1, keepdims=True), 0.0)
        dbs_ref[...] += dbs

    spec = pl.BlockSpec((CHUNK, GM_INNER), lambda i: (i, 0))
    wspec = pl.BlockSpec(ws.shape, lambda i: (0, 0, 0))
    bspec = pl.BlockSpec(bst.shape, lambda i: (0, 0))
    return pl.pallas_call(
        kern, name=name, grid=(n // CHUNK,),
        in_specs=[spec, spec, spec, wspec, wspec, bspec],
        out_specs=[spec, spec, wspec, bspec],
        out_shape=[jax.ShapeDtypeStruct((n, GM_INNER), F32), jax.ShapeDtypeStruct((n, GM_INNER), F32),
                   jax.ShapeDtypeStruct(ws.shape, F32), jax.ShapeDtypeStruct(bst.shape, F32)],
        compiler_params=_params("arbitrary"),
    )(dt, gu, gvn, ws, wst, bst)


def _adamw(parts, w, m, v, *, name, tm=256, sel=(), into=None):
    ns, r, wd = parts.shape
    tm = _pick(r, tm, 8)
    tc = wd
    if tm < 64 and wd % 256 == 0:
        tm, tc = r, 256
    lead = len(sel)
    assert w.shape[lead:] == (r, wd) and lead == w.ndim - 2

    def kern(*refs):
        p_ref, w_ref, m_ref, v_ref = refs[:4]
        g_ref, d_ref, nm_ref, nv_ref = refs[-4:]
        g = p_ref[0].astype(F32)
        for s in range(1, ns):
            g = g + p_ref[s].astype(F32)
        m2 = ADAM_B1 * m_ref[...] + (1.0 - ADAM_B1) * g
        v2 = ADAM_B2 * v_ref[...] + (1.0 - ADAM_B2) * (g * g)
        m_hat = m2 / (1.0 - ADAM_B1 ** ADAM_STEP)
        v_hat = v2 / (1.0 - ADAM_B2 ** ADAM_STEP)
        g_ref[...] = g
        d_ref[...] = -ADAM_LR * (m_hat / (jnp.sqrt(v_hat) + ADAM_EPS) + ADAM_WD * w_ref[...])
        nm_ref[...] = m2
        nv_ref[...] = v2

    spec = pl.BlockSpec((None,) * lead + (tm, tc), lambda i, j: tuple(sel) + (i, j))
    extra, aliases = [], {}
    if into is not None:
        extra = list(into)
        aliases = {4 + k: k for k in range(4)}
    return pl.pallas_call(
        kern, name=name, grid=(r // tm, wd // tc),
        in_specs=[pl.BlockSpec((ns, tm, tc), lambda i, j: (0, i, j)), spec, spec, spec] +
                 [pl.BlockSpec(memory_space=pl.ANY)] * len(extra),
        out_specs=[spec] * 4, out_shape=[jax.ShapeDtypeStruct(w.shape, F32)] * 4,
        input_output_aliases=aliases,
        compiler_params=_params("parallel", "parallel"),
    )(parts, w, m, v, *extra)


def _sum_slots(parts, *, name, scale_by=None):
    ns, r, wd = parts.shape

    def kern(*refs):
        p_ref, o_ref = refs[0], refs[-1]
        g = p_ref[0]
        for s in range(1, ns):
            g = g + p_ref[s]
        if scale_by is not None:
            g = g * _dsilu(refs[1][...])
        o_ref[...] = g

    args = [parts] + ([] if scale_by is None else [scale_by])
    return pl.pallas_call(kern, name=name, out_shape=jax.ShapeDtypeStruct((r, wd), F32),
                          compiler_params=pltpu.CompilerParams(vmem_limit_bytes=VMEM_LIMIT_BYTES))(*args)


def _mesh_pos():
    x, y, c = lax.axis_index("x"), lax.axis_index("y"), lax.axis_index("c")
    return x, y, c, 4 * x + 2 * y + c


def _flip(x, y, c, f):
    fx, fy, fc = (f >> 2) & 1, (f >> 1) & 1, f & 1
    px = 1 - x if fx else x
    py = 1 - y if fy else y
    pc = 1 - c if fc else c
    return (px, py, pc), 4 * px + 2 * py + pc


_HBM_SPEC = pl.BlockSpec(memory_space=pltpu.HBM)


def _exchange(arrays, *, scatter, name):
    na = len(arrays)
    if scatter:
        out_shape = [jax.ShapeDtypeStruct(a.shape, a.dtype) for a in arrays]
    else:
        out_shape = [jax.ShapeDtypeStruct((NDEV,) + a.shape, a.dtype) for a in arrays]

    out_shape.append(jax.ShapeDtypeStruct((8, 128), F32))

    def body(*refs):
        ins, outs = refs[:na], refs[na:2 * na]
        send_sems, recv_sems, local_sems = refs[2 * na + 1:]
        refs[2 * na][...] = jnp.zeros((8, 128), F32)
        x, y, c, me = _mesh_pos()
        copies = []
        for i in range(na):
            src_own = ins[i].at[me] if scatter else ins[i]
            lc = pltpu.make_async_copy(src_own, outs[i].at[me], local_sems.at[i])
            lc.start()
            copies.append(lc)
        sends = []
        for f in range(1, NDEV):
            peer, pidx = _flip(x, y, c, f)
            for i in range(na):
                k = i * (NDEV - 1) + f - 1
                src = ins[i].at[pidx] if scatter else ins[i]
                cp = pltpu.make_async_remote_copy(
                    src_ref=src, dst_ref=outs[i].at[me], send_sem=send_sems.at[k], recv_sem=recv_sems.at[k],
                    device_id=peer, device_id_type=pl.DeviceIdType.MESH)
                cp.start()
                sends.append(cp)
        for f in range(1, NDEV):
            peer, pidx = _flip(x, y, c, f)
            for i in range(na):
                k = i * (NDEV - 1) + f - 1
                src = ins[i].at[pidx] if scatter else ins[i]
                pltpu.make_async_remote_copy(
                    src_ref=src, dst_ref=outs[i].at[pidx], send_sem=send_sems.at[k], recv_sem=recv_sems.at[k],
                    device_id=peer, device_id_type=pl.DeviceIdType.MESH).wait_recv()
        for cp in sends:
            cp.wait_send()
        for lc in copies:
            lc.wait()

    res = pl.pallas_call(
        body, name=name, out_shape=out_shape, in_specs=[_HBM_SPEC] * na,
        out_specs=[_HBM_SPEC] * na + [pl.BlockSpec(memory_space=pltpu.VMEM)],
        scratch_shapes=[pltpu.SemaphoreType.DMA((na * (NDEV - 1),)), pltpu.SemaphoreType.DMA((na * (NDEV - 1),)),
                        pltpu.SemaphoreType.DMA((na,))],
        compiler_params=pltpu.CompilerParams(has_side_effects=True),
    )(*arrays)
    return res[:na], res[na][0, 0]


_SEM_SPEC = pl.BlockSpec(memory_space=pltpu.SEMAPHORE)
_DATAFLOW = pltpu.SideEffectType.DATAFLOW_SIDE_EFFECTING


def _split_copies(srcs, lands, send_sems, recv_sems, scatter, arriving):
    x, y, c, me = _mesh_pos()
    copies = []
    for i in range(len(srcs)):
        for f in range(1, NDEV):
            peer, pidx = _flip(x, y, c, f)
            k = i * (NDEV - 1) + f - 1
            copies.append(pltpu.make_async_remote_copy(
                src_ref=srcs[i].at[pidx] if scatter else srcs[i], dst_ref=lands[i].at[pidx if arriving else me],
                send_sem=send_sems.at[k], recv_sem=recv_sems.at[k], device_id=peer,
                device_id_type=pl.DeviceIdType.MESH))
    return copies


def _exchange_start(srcs, lands, *, scatter, name):
    na = len(srcs)
    nsem = na * (NDEV - 1)

    def body(*refs):
        ins_src, ins_land = refs[:na], refs[na:2 * na]
        send_sems, recv_sems = refs[2 * na], refs[2 * na + 1]
        token = refs[-1]
        for cp in _split_copies(ins_src, ins_land, send_sems, recv_sems, scatter, False):
            cp.start()
        token[...] = jnp.zeros_like(token)

    thru = [pltpu.HBM(a.shape, a.dtype) for a in list(srcs) + list(lands)]
    res = pl.pallas_call(
        body, name=name,
        out_shape=(pltpu.SemaphoreType.DMA((nsem,)), pltpu.SemaphoreType.DMA((nsem,)), *thru,
                   jax.ShapeDtypeStruct((8, 128), F32)),
        in_specs=[_HBM_SPEC] * (2 * na),
        out_specs=(_SEM_SPEC, _SEM_SPEC, *([_HBM_SPEC] * (2 * na)), pl.BlockSpec(memory_space=pltpu.VMEM)),
        input_output_aliases={i: 2 + i for i in range(2 * na)},
        compiler_params=pltpu.CompilerParams(has_side_effects=_DATAFLOW),
    )(*[pltpu.with_memory_space_constraint(a, pltpu.HBM) for a in list(srcs) + list(lands)])
    send_sems, recv_sems = res[0], res[1]
    return send_sems, recv_sems, res[2:2 + na], res[2 + na:2 + 2 * na], res[-1][0, 0]


def _exchange_wait(send_sems, recv_sems, srcs, lands, after, *, scatter, name):
    na = len(srcs)

    def body(*refs):
        ins_src, ins_land = refs[:na], refs[na:2 * na]
        s_sems, r_sems = refs[2 * na], refs[2 * na + 1]
        for cp in _split_copies(ins_src, ins_land, s_sems, r_sems, scatter, False):
            cp.wait_send()
        for cp in _split_copies(ins_src, ins_land, s_sems, r_sems, scatter, True):
            cp.wait_recv()

    thru = [pltpu.HBM(a.shape, a.dtype) for a in list(srcs) + list(lands)]
    res = pl.pallas_call(
        body, name=name, out_shape=tuple(thru),
        in_specs=[_HBM_SPEC] * (2 * na) + [_SEM_SPEC, _SEM_SPEC, pl.BlockSpec(memory_space=pl.ANY)],
        out_specs=tuple([_HBM_SPEC] * (2 * na)),
        input_output_aliases={i: i for i in range(2 * na)},
        compiler_params=pltpu.CompilerParams(has_side_effects=_DATAFLOW),
    )(*srcs, *lands, send_sems, recv_sems, after)
    return res[na:]


def _landing(block, me):
    buf = lax.empty((NDEV,) + block.shape, block.dtype)
    return lax.dynamic_update_slice_in_dim(buf, block[None], me, axis=0)


def _seg_kw(nseg, n_ctx, tm):
    return dict(nseg=nseg, seg_blocks=(n_ctx // tm if nseg == 2 else 0))


def _ffn_fwd(tag, h, gpre, gpost, shift, scale, gate, w, *, nseg, n_ctx, tm):
    n = h.shape[0]
    kw = _seg_kw(nseg, n_ctx, tm)
    (u,) = _rowwise(tag + "_pre", _pre_fwd_fn, n, [h], [("full", gpre), ("seg", shift), ("seg", scale)],
                    [(D_MODEL, BF16)], tm=tm, **kw)
    if "early" in w:
        w.update(w.pop("early")(u))
    s, a, b = _mm_glu(u, w["win_t"], name=tag + "_glu")
    if "late" in w:
        w.update(w.pop("late")(s))
    y, ho = _mm_rows(s, w["wout"], functools.partial(_out_post_fn, 0.5), [h], [("full", gpost), ("seg", gate)],
                     [(D_MODEL, F32), (D_MODEL, F32)], name=tag + "_out", tk=FFN_DIM, n_ctx=n_ctx)
    return ho, dict(h=h, u=u, s=s, a=a, b=b, y=y)


def _ffn_bwd(tag, dho, sv, gpre, gpost, scale, gate, w, put, *, nseg, n_ctx, tm):
    n = dho.shape[0]
    kw = _seg_kw(nseg, n_ctx, tm)
    dy, dgate, dgpost = _rowwise(tag + "_postb", functools.partial(_post_bwd_fn, 0.5), n, [dho, sv["y"]],
                                 [("full", gpost), ("seg", gate)], [(D_MODEL, BF16)], [D_MODEL, D_MODEL], tm=tm, **kw)
    tok = put("w_out", _mm_tn(sv["s"], dy, name=tag + "_dwout", tm=1408, tn=1024, col_blocks=1))
    ds = _mm(dy, w["wout"], out_dtype=F32, name=tag + "_ds", tn=1408, rhs_t=True)
    (dp,) = _rowwise(tag + "_glub", _glu_bwd_fn, n, [ds, sv["a"], sv["b"]], [], [(2 * FFN_DIM, BF16)], tm=min(tm, 128))
    tok2 = put("w_in", _mm_tn(dp, sv["u"], name=tag + "_dwin", tm=1408, tn=1024, col_blocks=1))
    for t in (tok, tok2):
        if t is not None:
            gpre = gpre + t
    dh, dshift, dscale, dgpre = _mm_rows(dp, w["win_t"], _pre_bwd_fn, [sv["h"], dho],
                                         [("full", gpre), ("seg", scale)], [(D_MODEL, F32)],
                                         [D_MODEL, D_MODEL, D_MODEL], name=tag + "_du", n_ctx=n_ctx)
    return dh, None, dict(shift=dshift, scale=dscale, gate=dgate, gpre=dgpre, gpost=dgpost)


def _local_step(x, ctx, target, mods, norm_g, get_w, small, put_grad):
    t_len, n_ctx = x.shape[0], ctx.shape[0]
    n0 = t_len + n_ctx
    tm0 = _pick(n_ctx, 256, 8)
    tm1 = _pick(t_len, 256, 8)
    ncc = n_ctx // CHUNK
    g = {}

    def modrow(i, k, nseg):
        mc, mx = mods[i]
        if nseg == 2:
            return jnp.stack([mc[k], mx[k]])[:, None, :]
        return mx[k][None, None, :]

    pending = [None]

    def gvec(i, k):
        v = norm_g[i, k][None, :]
        if pending[0] is not None:
            v = v + pending[0]
            pending[0] = None
        return v

    xc = jnp.concatenate([ctx, x], axis=0)
    L0 = dict(nseg=2, n_ctx=n_ctx, tm=tm0)
    wts = dict(get_w("ffn00", xc))
    h1, sv_f01 = _ffn_fwd("l0f1", xc, gvec(0, 0), gvec(0, 1), modrow(0, 0, 2), modrow(0, 1, 2), modrow(0, 2, 2),
                          wts["ffn00"], **L0)
    kw0 = _seg_kw(2, n_ctx, tm0)
    (um0,) = _rowwise("l0m_pre", _pre_fwd_fn, n0, [h1], [("full", gvec(0, 2)), ("seg", modrow(0, 3, 2)),
                                                         ("seg", modrow(0, 4, 2))], [(D_MODEL, BF16)], tm=tm0, **kw0)
    wts.update(get_w("ssd", um0))
    win_ssd = wts["ssd_win_t"]
    nh = SSD_HEADS
    dt_blk = (SSD_INNER + SSD_CONV_DIM) // (2 * nh)
    z = _mm(um0, win_ssd, out_dtype=F32, name="ssd_z", rhs_t=True, n=SSD_INNER)
    xbc_pre = _mm(um0, win_ssd, out_dtype=F32, name="ssd_xbc", rhs_t=True, n=SSD_CONV_DIM,
                  b_off=(SSD_INNER // 1024, 0))
    dtr = _mm(um0, win_ssd, out_dtype=F32, name="ssd_dt", rhs_t=True, n=2 * nh, b_off=(dt_blk, 0))
    cpre, xbc = _conv_fwd(xbc_pre, small["conv_w8"], small["conv_b"], n_ctx=n_ctx, name="ssd_conv")
    nh = SSD_HEADS
    dt_dir = [dtr[:, :nh], dtr[:, nh:2 * nh]]
    dtT_dir = [d.T for d in dt_dir]
    bias_r = [small["dt_bias"][d][None, :] for d in range(2)]
    bias_c = [small["dt_bias"][d][:, None] for d in range(2)]
    alog_r = [small["a_log"][d][None, :] for d in range(2)]
    alog_c = [small["a_log"][d][:, None] for d in range(2)]
    ys, hss = [], []
    for d in range(2):
        yd, hsd = _ssd_scan_fwd(xbc, dt_dir[d], dtT_dir[d], bias_r[d], bias_c[d], alog_r[d], alog_c[d],
                                rev=(d == 1), n_ctx_chunks=ncc, name=f"ssd_scan{d}")
        ys.append(yd)
        hss.append(hsd)
    dvec = jnp.repeat(small["ssd_d"], SSD_HEAD_DIM)[None, :]
    ngv = small["ssd_norm_g"][None, :]
    gate_rows = [ys[0], ys[1], (xbc, SSD_INNER, 0, 0), z]
    lat = lambda r: (r[0], r[1], r[2], ncc) if isinstance(r, tuple) else (r, r.shape[1], 0, ncc)
    (yn,) = _rowwise("ssd_gate", _ssdgate_fwd_fn, t_len, [lat(r) for r in gate_rows],
                     [("full", dvec), ("full", ngv)], [(SSD_INNER, BF16)], tm=CHUNK)
    h1x = h1[n_ctx:]
    L1 = dict(nseg=1, n_ctx=0, tm=tm1)
    if "late" in wts:
        wts.update(wts.pop("late")(yn))
    yo0, h2 = _mm_rows(yn, wts["ssd_wout"], functools.partial(_out_post_fn, 1.0), [h1x],
                       [("full", gvec(0, 3)), ("seg", modrow(0, 5, 1))], [(D_MODEL, F32), (D_MODEL, F32)],
                       name="ssd_out", tk=SSD_INNER)
    wts.update(get_w("ffn01", h2))
    h3, sv_f02 = _ffn_fwd("l0f2", h2, gvec(0, 4), gvec(0, 5), modrow(0, 6, 1), modrow(0, 7, 1), modrow(0, 8, 1),
                          wts["ffn01"], **L1)

    wts.update(get_w("ffn10", h3))
    h4, sv_f11 = _ffn_fwd("l1f1", h3, gvec(1, 0), gvec(1, 1), modrow(1, 0, 1), modrow(1, 1, 1), modrow(1, 2, 1),
                          wts["ffn10"], **L1)
    (um1,) = _rowwise("l1m_pre", _pre_fwd_fn, t_len, [h4], [("full", gvec(1, 2)), ("seg", modrow(1, 3, 1)),
                                                            ("seg", modrow(1, 4, 1))], [(D_MODEL, BF16)], tm=tm1)
    wts.update(get_w("gm", um1))
    p1 = _mm(um1, wts["gm_win"], out_dtype=F32, name="gm_in")
    vg = small["gm_v_g"][None, :]
    vb = small["gm_v_b"][None, :]
    gu, gvn = _rowwise("gm_act", _gm_act_fwd_fn, t_len, [p1], [("full", vg), ("full", vb)],
                       [(GM_INNER, F32), (GM_INNER, BF16)], tm=128)
    ws_bf = small["gm_w_s"].astype(BF16)
    wst_bf = jnp.swapaxes(small["gm_w_s"], 1, 2).astype(BF16)
    bst = small["gm_b_s"].T
    tgm = _gm_spatial_fwd(gu, gvn, ws_bf, bst, name="gm_spatial")
    yo1, h5 = _mm_rows(tgm, wts["gm_wout"], functools.partial(_out_post_fn, 1.0), [h4],
                       [("full", gvec(1, 3)), ("seg", modrow(1, 5, 1))], [(D_MODEL, F32), (D_MODEL, F32)],
                       name="gm_out", tk=GM_INNER)
    wts.update(get_w("ffn11", h5))
    h6, sv_f12 = _ffn_fwd("l1f2", h5, gvec(1, 4), gvec(1, 5), modrow(1, 6, 1), modrow(1, 7, 1), modrow(1, 8, 1),
                          wts["ffn11"], **L1)

    dh, loss_parts = _rowwise("loss", _loss_fn, t_len, [h6, target], [], [(D_MODEL, F32)], [D_MODEL], tm=tm1)

    zero = jnp.zeros((D_MODEL,), F32)
    dmx = [[zero] * N_MOD for _ in range(2)]
    dmc = [[zero] * N_MOD for _ in range(2)]
    dng = [[zero] * 6 for _ in range(2)]

    def put_mod(i, k, acc):
        if acc.shape[0] == 2:
            dmc[i][k] = dmc[i][k] + acc[0, 0]
            dmx[i][k] = dmx[i][k] + acc[1, 0]
        else:
            dmx[i][k] = dmx[i][k] + acc[0, 0]

    def put_g(i, k, acc):
        dng[i][k] = dng[i][k] + jnp.sum(acc[:, 0], axis=0)

    def ffn_back(tag, i, j, dho, sv, w, lay):
        nseg = lay["nseg"]
        base = 0 if j == 0 else 6
        gi = 0 if j == 0 else 4
        dh_in, pending[0], s = _ffn_bwd(tag, dho, sv, gvec(i, gi), gvec(i, gi + 1), modrow(i, base + 1, nseg),
                                        modrow(i, base + 2, nseg), w, functools.partial(put_grad, f"ffn{i}{j}"), **lay)
        put_mod(i, base, s["shift"])
        put_mod(i, base + 1, s["scale"])
        put_mod(i, base + 2, s["gate"])
        put_g(i, gi, s["gpre"])
        put_g(i, gi + 1, s["gpost"])
        return dh_in

    dh = ffn_back("l1f2", 1, 1, dh, sv_f12, wts["ffn11"], L1)
    dyo, dgate, dgp = _rowwise("l1m_postb", functools.partial(_post_bwd_fn, 1.0), t_len, [dh, yo1],
                               [("full", gvec(1, 3)), ("seg", modrow(1, 5, 1))], [(D_MODEL, BF16)],
                               [D_MODEL, D_MODEL], tm=tm1)
    put_mod(1, 5, dgate)
    put_g(1, 3, dgp)
    put_grad("gm", "w_out", _mm_tn(tgm, dyo, name="gm_dwout", tn=1024, col_blocks=1))
    dtg = _mm(dyo, wts["gm_wout"], out_dtype=F32, name="gm_dt", rhs_t=True)
    dgu, dgvn, dws, dbst = _gm_spatial_bwd(dtg, gu, gvn, ws_bf, wst_bf, bst, name="gm_spatialb")
    g["gm_w_s"] = dws
    g["gm_b_s"] = dbst.T
    dp1, dvg, dvb = _rowwise("gm_actb", _gm_act_bwd_fn, t_len, [p1, dgu, dgvn], [("full", vg)],
                             [(2 * GM_INNER, BF16)], [GM_INNER, GM_INNER], tm=128)
    g["gm_v_g"] = dvg[0, 0]
    g["gm_v_b"] = dvb[0, 0]
    pending[0] = put_grad("gm", "w_in", _mm_tn(um1, dp1, name="gm_dwin", tm=1024, col_blocks=NDEV))
    dh, dsh, dsc, dgp = _mm_rows(dp1, wts["gm_win"], _pre_bwd_fn, [h4, dh],
                                 [("full", gvec(1, 2)), ("seg", modrow(1, 4, 1))], [(D_MODEL, F32)],
                                 [D_MODEL, D_MODEL, D_MODEL], name="gm_dum", tk=1024, rhs_t=True)
    put_mod(1, 3, dsh)
    put_mod(1, 4, dsc)
    put_g(1, 2, dgp)
    dh = ffn_back("l1f1", 1, 0, dh, sv_f11, wts["ffn10"], L1)

    dh = ffn_back("l0f2", 0, 1, dh, sv_f02, wts["ffn01"], L1)
    dyo, dgate, dgp = _rowwise("l0m_postb", functools.partial(_post_bwd_fn, 1.0), t_len, [dh, yo0],
                               [("full", gvec(0, 3)), ("seg", modrow(0, 5, 1))], [(D_MODEL, BF16)],
                               [D_MODEL, D_MODEL], tm=tm1)
    put_mod(0, 5, dgate)
    put_g(0, 3, dgp)
    tok = put_grad("ssd", "w_out", _mm_tn(yn, dyo, name="ssd_dwout", tn=1024, col_blocks=1))
    dyn = _mm(dyo, wts["ssd_wout"], out_dtype=F32, name="ssd_dyn", rhs_t=True)
    dy_ssd, dz, dngv, ddv = _rowwise("ssd_gateb", _ssdgate_bwd_fn, n0, [(dyn, SSD_INNER, 0, -ncc)] + gate_rows,
                                     [("full", dvec), ("full", ngv if tok is None else ngv + tok)],
                                     [(SSD_INNER, F32), (SSD_INNER, BF16)],
                                     [SSD_INNER, SSD_INNER], tm=128)
    g["ssd_norm_g"] = dngv[0, 0]
    g["ssd_D"] = jnp.sum(ddv[0, 0].reshape(SSD_HEADS, SSD_HEAD_DIM), axis=1)
    dxbcs, ddts, dalogs, dbiases = [], [], [], []
    for d in range(2):
        dxd, ddtd, dal, dbi = _ssd_scan_bwd(dy_ssd, xbc, hss[d], dt_dir[d], dtT_dir[d], bias_r[d], bias_c[d],
                                            alog_r[d], alog_c[d], dvec, rev=(d == 1), n_ctx_chunks=ncc,
                                            direct=(d == 0), name=f"ssd_scanb{d}")
        dxbcs.append(dxd)
        ddts.append(ddtd)
        dalogs.append(dal[0])
        dbiases.append(dbi[0])
    g["ssd_A_log"] = jnp.stack(dalogs)
    g["ssd_dt_bias"] = jnp.stack(dbiases)
    dxbc_pre, dcw8, dcb = _conv_bwd(dxbcs[0], dxbcs[1], cpre, xbc_pre, small["conv_w8"], n_ctx=n_ctx, name="ssd_convb")
    g["ssd_conv_w"] = dcw8[:SSD_CONV]
    g["ssd_conv_b"] = dcb[0]
    ddt_bf = jnp.concatenate([ddts[0], ddts[1]], axis=1).astype(BF16)
    dw_ssd_in_t = jnp.concatenate([
        _mm_tn(dz, um0, name="ssd_dwz", col_blocks=1),
        _mm_tn(dxbc_pre, um0, name="ssd_dwxbc", col_blocks=1),
        _mm_tn(ddt_bf, um0, name="ssd_dwdt", col_blocks=1)], axis=1)
    pending[0] = put_grad("ssd", "w_in", dw_ssd_in_t)
    dum0 = _mm(dz, win_ssd, out_dtype=F32, name="ssd_dum_z", tk=1024, n=D_MODEL)
    dum0 = _mm(dxbc_pre, win_ssd, out_dtype=F32, name="ssd_dum_x", tk=1024, n=D_MODEL,
               b_off=(SSD_INNER // 1024, 0), add=dum0)
    dum0 = _mm(ddt_bf, win_ssd, out_dtype=F32, name="ssd_dum_dt", tk=2 * nh, n=D_MODEL, b_off=(dt_blk, 0), add=dum0)
    dh0, dsh, dsc, dgp = _rowwise("l0m_preb", _pre_bwd_fn, n0, [dum0, h1, (dh, D_MODEL, 0, -(n_ctx // tm0))],
                                  [("full", gvec(0, 2)), ("seg", modrow(0, 4, 2))], [(D_MODEL, F32)],
                                  [D_MODEL, D_MODEL, D_MODEL], tm=tm0, **kw0)
    put_mod(0, 3, dsh)
    put_mod(0, 4, dsc)
    put_g(0, 2, dgp)
    dh0 = ffn_back("l0f1", 0, 0, dh0, sv_f01, wts["ffn00"], L0)
    grad_x = dh0[n_ctx:]
    g["norm_g"] = jnp.stack([jnp.stack(r) for r in dng])
    g["dmx"] = jnp.stack([jnp.concatenate(r) for r in dmx])
    g["dmc"] = jnp.stack([jnp.concatenate(r) for r in dmc])
    return loss_parts[0], grad_x, g


GROUPS = ("ffn00", "ssd", "ffn01", "ffn10", "gm", "ffn11")


TRANSPOSED_IN = ("ffn", "ssd")


def _is_transposed(group):
    return group.startswith(TRANSPOSED_IN)


def _mats_in(group, win_l):
    if _is_transposed(group):
        return {("win_t" if group.startswith("ffn") else group + "_win_t"): win_l.reshape(-1, win_l.shape[2])}
    k, nloc = win_l.shape[1], win_l.shape[2]
    return {group + "_win": jnp.transpose(win_l, (1, 0, 2)).reshape(k, NDEV * nloc)}


def _mats_out(group, wout_l):
    pre = "" if group.startswith("ffn") else group + "_"
    return {pre + "wout": wout_l.reshape(-1, wout_l.shape[2])}


def _group_mats(group, lands):
    m = {**_mats_in(group, lands[0]), **_mats_out(group, lands[1])}
    return {group: m} if group.startswith("ffn") else m


def _grad_blocks(which, grad):
    if grad.ndim == 3:
        return grad if grad.shape[0] == NDEV else grad.reshape(NDEV, grad.shape[1] // NDEV, grad.shape[2])
    if which == "w_in":
        k, n = grad.shape
        return jnp.transpose(grad.reshape(k, NDEV, n // NDEV), (1, 0, 2)).astype(BF16)
    return grad.reshape(NDEV, grad.shape[0] // NDEV, grad.shape[1]).astype(BF16)


def kernel(x, c, ctx, c_ctx, ada_w, ada_b, norm_g, ffn_w_in, ffn_w_out, ssd_w_in, ssd_conv_w, ssd_conv_b, ssd_dt_bias, ssd_A_log, ssd_D, ssd_norm_g, ssd_w_out, gm_w_in, gm_v_g, gm_v_b, gm_w_s, gm_b_s, gm_w_out, loss_target, m_c_ctx, m_ada_w, m_ada_b, m_norm_g, m_ffn_w_in, m_ffn_w_out, m_ssd_w_in, m_ssd_conv_w, m_ssd_conv_b, m_ssd_dt_bias, m_ssd_A_log, m_ssd_D, m_ssd_norm_g, m_ssd_w_out, m_gm_w_in, m_gm_v_g, m_gm_v_b, m_gm_w_s, m_gm_b_s, m_gm_w_out, v_c_ctx, v_ada_w, v_ada_b, v_norm_g, v_ffn_w_in, v_ffn_w_out, v_ssd_w_in, v_ssd_conv_w, v_ssd_conv_b, v_ssd_dt_bias, v_ssd_A_log, v_ssd_D, v_ssd_norm_g, v_ssd_w_out, v_gm_w_in, v_gm_v_g, v_gm_v_b, v_gm_w_s, v_gm_b_s, v_gm_w_out):
    me = 4 * lax.axis_index("x") + 2 * lax.axis_index("y") + lax.axis_index("c")
    d = D_MODEL
    ncol = N_MOD * d // NDEV

    small_pack = jnp.concatenate([c.reshape(-1), norm_g.reshape(-1), ssd_conv_w.reshape(-1),
                                  gm_v_g.reshape(-1), gm_v_b.reshape(-1)])[None, :]
    (sp,), _ = _exchange([small_pack], scatter=False, name="gather_small")
    sp = sp[:, 0]
    o = 0
    c_all = sp[:, o:o + d]; o += d
    ng_all = sp[:, o:o + 2 * 6 * 128].reshape(NDEV, 2, 6, 128); o += 2 * 6 * 128
    cw_all = sp[:, o:o + SSD_CONV * 512].reshape(NDEV, SSD_CONV, 512); o += SSD_CONV * 512
    vg_all = sp[:, o:o + 256]; o += 256
    vb_all = sp[:, o:o + 256]; o += 256
    norm_g_full = jnp.transpose(ng_all, (1, 2, 0, 3)).reshape(2, 6, d)
    conv_w_full = jnp.transpose(cw_all, (1, 0, 2)).reshape(SSD_CONV, SSD_CONV_DIM)
    gm_v_g_full = vg_all.reshape(-1)
    gm_v_b_full = vb_all.reshape(-1)

    c16 = jnp.concatenate([c_all, jnp.broadcast_to(c_ctx[None, :], (NDEV, d))], axis=0)
    ada_b_loc = lax.dynamic_slice_in_dim(ada_b, me * ncol, ncol, axis=1)
    mods_loc = jnp.stack([_mm_f32(c16, ada_w[i], name=f"ada_mod{i}", silu_a=True, bias=ada_b_loc[i][None, :])
                          for i in range(2)])
    (mods_all,), mods_done = _exchange([mods_loc], scatter=False, name="gather_mods")

    tr = lambda a: jnp.swapaxes(a, -1, -2)
    shard = {"ssd": (tr(ssd_w_in)[0], ssd_w_out[0]), "gm": (gm_w_in[0], gm_w_out[0])}
    for i in range(2):
        for j in range(2):
            shard[f"ffn{i}{j}"] = (tr(ffn_w_in)[i, j], ffn_w_out[i, j])
    apart = GROUPS[:2]
    units = []
    for grp in GROUPS:
        units += [(grp + "_in", grp, (0,)), (grp + "_out", grp, (1,))] if grp in apart else [(grp, grp, (0, 1))]
    gathers = {}
    started = mods_done
    for unit, grp, idx in units:
        srcs = [(shard[grp][k] + started).astype(BF16) for k in idx]
        st = _exchange_start(srcs, [_landing(s, me) for s in srcs], scatter=False, name="gather_start_" + unit)
        gathers[unit] = st[:4]
        started = st[4]

    def fetch(unit, after):
        return _exchange_wait(*gathers[unit], after, scatter=False, name="gather_wait_" + unit)

    def get_w(grp, after):
        if grp not in apart:
            return _group_mats(grp, fetch(grp, after))
        early = lambda later: _mats_in(grp, fetch(grp + "_in", later)[0])
        late = lambda later: _mats_out(grp, fetch(grp + "_out", later)[0])
        if grp.startswith("ffn"):
            return {grp: dict(early=early, late=late)}
        return dict(early(after), late=late)

    scatters = {}
    held = {}

    def put_grad(grp, which, grad):
        if grp in apart:
            unit, blocks = grp + "_" + which[2:], [_grad_blocks(which, grad)]
        else:
            held[grp, which] = _grad_blocks(which, grad)
            if (grp, "w_in") not in held or (grp, "w_out") not in held:
                return None
            unit, blocks = grp, [held[grp, "w_in"], held[grp, "w_out"]]
        own = [lax.dynamic_index_in_dim(b, me, axis=0, keepdims=False) for b in blocks]
        st = _exchange_start(blocks, [_landing(o_, me) for o_ in own], scatter=True, name="scatter_start_" + unit)
        scatters[unit] = st[:4]
        return st[4]

    mods_rows = jnp.transpose(mods_all, (1, 2, 0, 3)).reshape(2, 2 * NDEV, N_MOD * d) + started
    mx = lax.dynamic_index_in_dim(mods_rows, me, axis=1, keepdims=False).reshape(2, N_MOD, d)
    mc = mods_rows[:, NDEV].reshape(2, N_MOD, d)
    mods = [(mc[i], mx[i]) for i in range(2)]

    small = dict(conv_w8=jnp.pad(conv_w_full, ((0, 8 - SSD_CONV), (0, 0))), conv_b=ssd_conv_b, dt_bias=ssd_dt_bias[0],
                 a_log=ssd_A_log[0], ssd_d=ssd_D[0], ssd_norm_g=ssd_norm_g[0], gm_v_g=gm_v_g_full,
                 gm_v_b=gm_v_b_full, gm_w_s=gm_w_s[0], gm_b_s=gm_b_s[0])
    loss_parts, grad_x, g = _local_step(x[0], ctx[0], loss_target[0], mods, norm_g_full, get_w, small, put_grad)
    g["loss"] = (0.5 / d * jnp.sum(loss_parts)).reshape(1)

    whole = {"ffn_w_in": (tr(ffn_w_in), tr(m_ffn_w_in), tr(v_ffn_w_in)), "ffn_w_out": (ffn_w_out, m_ffn_w_out, v_ffn_w_out),
             "ssd_w_in": (tr(ssd_w_in), tr(m_ssd_w_in), tr(v_ssd_w_in)), "ssd_w_out": (ssd_w_out, m_ssd_w_out, v_ssd_w_out),
             "gm_w_in": (gm_w_in, m_gm_w_in, v_gm_w_in), "gm_w_out": (gm_w_out, m_gm_w_out, v_gm_w_out)}
    res = {}

    def update_units(some, after):
        for unit, grp, idx in some:
            parts = _exchange_wait(*scatters[unit], after, scatter=True, name="scatter_wait_" + unit)
            for k, p in zip(idx, parts):
                which = ("in", "out")[k]
                nm = ("ffn" if grp.startswith("ffn") else grp) + "_w_" + which
                sel = (int(grp[3]), int(grp[4])) if grp.startswith("ffn") else (0,)
                res[nm] = _adamw(p, *whole[nm], name=f"adamw_{grp}_{which}", sel=sel, into=res.get(nm))
                after = res[nm][0]
        return after

    sg_names = ["dmx", "dmc", "norm_g", "ssd_conv_w", "ssd_conv_b", "ssd_dt_bias", "ssd_A_log", "ssd_D", "ssd_norm_g",
                "gm_v_g", "gm_v_b", "gm_w_s", "gm_b_s", "loss"]
    sg_shapes = [g[n].shape for n in sg_names]
    flat = jnp.concatenate([g[n].reshape(-1) for n in sg_names])
    npack = flat.shape[0]
    pad = (-npack) % 1024
    flat = jnp.pad(flat, (0, pad)).reshape(-1, 128)
    sg_start = _exchange_start([flat], [_landing(flat, me)], scatter=False, name="small_grads_start")
    by_send = list(reversed(units))
    update_units(by_send[:4], jnp.stack([sg_start[4], grad_x[0, 0]]))
    early_done = jnp.stack([res[nm][0].reshape(-1)[-1] for nm in sorted(res)])
    (sg_all,) = _exchange_wait(*sg_start[:4], early_done, scatter=False, name="small_grads_wait")
    sg_sum = _sum_slots(sg_all, name="sum_small_grads").reshape(-1)[:npack]
    update_units(by_send[4:], sg_sum)
    sums = {}
    o = 0
    for n, shp in zip(sg_names, sg_shapes):
        sz = math.prod(shp)
        sums[n] = sg_sum[o:o + sz].reshape(shp)
        o += sz
    loss = sums["loss"][0]
    per_dev = sg_all.reshape(NDEV, -1)
    dmx_all =per_dev[:, :2 * N_MOD * d].reshape(NDEV, 2, N_MOD * d)
    dmc_all = per_dev[:, 2 * N_MOD * d:4 * N_MOD * d].reshape(NDEV, 2, N_MOD * d)

    (s16,) = _rowwise("ada_silu", lambda cc: ((_silu(cc),), ()), 2 * NDEV, [c16], [], [(d, F32)], tm=2 * NDEV)
    s16_t = s16.T
    g_ada_w, dcc_parts = [], []
    for i in range(2):
        rhs = jnp.concatenate([lax.dynamic_slice_in_dim(dmx_all[:, i], me * ncol, ncol, axis=1),
                               lax.dynamic_slice_in_dim(dmc_all[:, i], me * ncol, ncol, axis=1)], axis=0)
        g_ada_w.append(_mm_f32(s16_t, rhs, name=f"ada_dw{i}"))
        dmc_loc = lax.dynamic_slice_in_dim(sums["dmc"][i], me * ncol, ncol, axis=0)
        rhs_c = jnp.zeros((ncol, 128), F32).at[:, 0].set(dmc_loc)
        dcc_parts.append(_mm_f32(ada_w[i], rhs_c, name=f"ada_dcc{i}")[:, 0])
    g_ada_w = jnp.stack(g_ada_w)
    dcc_part = (dcc_parts[0] + dcc_parts[1]).reshape(8, 128)
    (dcc_all,), _ = _exchange([dcc_part], scatter=False, name="gather_dcc")
    g_c_ctx = _sum_slots(dcc_all, name="sum_dcc", scale_by=c_ctx.reshape(8, 128)).reshape(d)
    g_ada_b = sums["dmx"] + sums["dmc"]

    outs = _adamw(g_ada_w.reshape(1, -1, ncol), ada_w.reshape(-1, ncol), m_ada_w.reshape(-1, ncol),
                  v_ada_w.reshape(-1, ncol), name="adamw_ada_w")
    res["ada_w"] = [o_.reshape(ada_w.shape) for o_ in outs]

    loc = lambda a, ax, n: lax.dynamic_slice_in_dim(a, me * n, n, axis=ax)
    small_g = dict(c_ctx=g_c_ctx, ada_b=g_ada_b, norm_g=loc(sums["norm_g"], 2, 128),
                   ssd_conv_w=loc(sums["ssd_conv_w"], 1, 512)[None], ssd_conv_b=sums["ssd_conv_b"][None],
                   ssd_dt_bias=sums["ssd_dt_bias"][None], ssd_A_log=sums["ssd_A_log"][None], ssd_D=sums["ssd_D"][None],
                   ssd_norm_g=sums["ssd_norm_g"][None], gm_v_g=loc(sums["gm_v_g"], 0, 256)[None],
                   gm_v_b=loc(sums["gm_v_b"], 0, 256)[None], gm_w_s=sums["gm_w_s"][None], gm_b_s=sums["gm_b_s"][None])
    small_w = dict(c_ctx=(c_ctx, m_c_ctx, v_c_ctx), ada_b=(ada_b, m_ada_b, v_ada_b), norm_g=(norm_g, m_norm_g, v_norm_g),
                   ssd_conv_w=(ssd_conv_w, m_ssd_conv_w, v_ssd_conv_w), ssd_conv_b=(ssd_conv_b, m_ssd_conv_b, v_ssd_conv_b),
                   ssd_dt_bias=(ssd_dt_bias, m_ssd_dt_bias, v_ssd_dt_bias), ssd_A_log=(ssd_A_log, m_ssd_A_log, v_ssd_A_log),
                   ssd_D=(ssd_D, m_ssd_D, v_ssd_D), ssd_norm_g=(ssd_norm_g, m_ssd_norm_g, v_ssd_norm_g),
                   gm_v_g=(gm_v_g, m_gm_v_g, v_gm_v_g), gm_v_b=(gm_v_b, m_gm_v_b, v_gm_v_b),
                   gm_w_s=(gm_w_s, m_gm_w_s, v_gm_w_s), gm_b_s=(gm_b_s, m_gm_b_s, v_gm_b_s))
    sn = list(small_w)

    def pack(arrs):
        f = jnp.concatenate([a.reshape(-1) for a in arrs])
        return jnp.pad(f, (0, (-f.shape[0]) % (256 * 128))).reshape(-1, 128)

    pg = pack([small_g[n].reshape(small_w[n][0].shape) for n in sn])
    outs = _adamw(pg[None], pack([small_w[n][0] for n in sn]), pack([small_w[n][1] for n in sn]),
                  pack([small_w[n][2] for n in sn]), name="adamw_small")
    flat_outs = [o_.reshape(-1) for o_ in outs]
    o = 0
    for n in sn:
        shp = small_w[n][0].shape
        sz = math.prod(shp)
        res[n] = [fo[o:o + sz].reshape(shp) for fo in flat_outs]
        o += sz

    order = ["c_ctx", "ada_w", "ada_b", "norm_g", "ffn_w_in", "ffn_w_out", "ssd_w_in", "ssd_conv_w", "ssd_conv_b",
             "ssd_dt_bias", "ssd_A_log", "ssd_D", "ssd_norm_g", "ssd_w_out", "gm_w_in", "gm_v_g", "gm_v_b", "gm_w_s",
             "gm_b_s", "gm_w_out"]
    for nm in ("ffn_w_in", "ssd_w_in"):
        res[nm] = [tr(a) for a in res[nm]]
    result = [loss, grad_x[None]]
    for k in range(4):
        result += [res[n][k] for n in order]
    return tuple(result)
```

```python
import functools
import math

import jax
import jax.numpy as jnp
from jax import lax
from jax.experimental import pallas as pl
from jax.experimental.pallas import tpu as pltpu

F32 = jnp.float32
BF16 = jnp.bfloat16

NDEV = 8
D_MODEL = 1024
FFN_DIM = 2816
N_MOD = 9
EPS = 1e-6
SSD_INNER = 2048
SSD_HEADS = 32
SSD_HEAD_DIM = 64
SSD_GROUPS = 8
SSD_HPG = 4
SSD_STATE = 128
SSD_CONV = 5
SSD_CONV_DIM = 4096
CHUNK = 128
GM_INNER = 2048
GM_GROUPS = 8
GM_GROUP_DIM = 256
ADAM_LR = 0.001
ADAM_B1 = 0.9
ADAM_B2 = 0.999
ADAM_EPS = 1e-08
ADAM_WD = 0.01
ADAM_STEP = 10
NEG_BIG = -1e30
VMEM_LIMIT_BYTES = 56 * 1024 * 1024
HI = lax.Precision.HIGHEST


def _params(*sem):
    return pltpu.CompilerParams(dimension_semantics=sem, vmem_limit_bytes=VMEM_LIMIT_BYTES)


def _pick(n, target, mult=16):
    if n <= target:
        return n
    for t in range(target - target % mult, 0, -mult):
        if n % t == 0:
            return t
    raise ValueError((n, target, mult))


def _sig(x):
    return 0.5 * jnp.tanh(0.5 * x) + 0.5


def _silu(x):
    return x * _sig(x)


def _dsilu(x):
    s = _sig(x)
    return s * (1.0 + x * (1.0 - s))


_GELU_C = math.sqrt(2.0 / math.pi)


def _gelu(x):
    return 0.5 * x * (1.0 + jnp.tanh(_GELU_C * (x + 0.044715 * x * x * x)))


def _dgelu(x):
    t = jnp.tanh(_GELU_C * (x + 0.044715 * x * x * x))
    return 0.5 * (1.0 + t) + 0.5 * x * (1.0 - t * t) * _GELU_C * (1.0 + 3.0 * 0.044715 * x * x)


def _softplus(x):
    return jnp.maximum(x, 0.0) + jnp.log1p(jnp.exp(-jnp.abs(x)))


def _sum0(v):
    return jnp.sum(v, axis=0, keepdims=True)


def _rms(h):
    r = lax.rsqrt(jnp.mean(h * h, axis=-1, keepdims=True) + EPS)
    return h * r, r


def _dot(a, b, dims=((1,), (0,)), precision=None):
    return lax.dot_general(a, b, (dims, ((), ())), preferred_element_type=F32, precision=precision)


_NT = ((1,), (1,))
_TN = ((0,), (0,))


def _rowwise(name, fn, n_rows, rows, consts, outs, accs=(), *, tm, nseg=1, seg_blocks=0):
    assert n_rows % tm == 0
    if nseg == 2:
        assert seg_blocks > 0
        seg = lambda i: jnp.where(i < seg_blocks, 0, 1)
    else:
        seg = lambda i: 0
    in_specs, args, lacking = [], [], []
    for r in rows:
        arr, width, cb, off = r if isinstance(r, tuple) else (r, r.shape[1], 0, 0)
        in_specs.append(pl.BlockSpec((tm, width), lambda i, cb=cb, off=off: (jnp.maximum(i + off, 0), cb)))
        args.append(arr)
        lacking.append(-off if off < 0 else 0)
    for kind, arr in consts:
        if kind == "seg":
            assert arr.shape[0] == nseg and arr.shape[1] == 1, arr.shape
            in_specs.append(pl.BlockSpec((None, 1, arr.shape[2]), lambda i: (seg(i), 0, 0)))
        else:
            in_specs.append(pl.BlockSpec(arr.shape, lambda i: (0, 0)))
        args.append(arr)
    out_shape = [jax.ShapeDtypeStruct((n_rows, w), dt) for w, dt in outs]
    out_specs = [pl.BlockSpec((tm, w), lambda i: (i, 0)) for w, _ in outs]
    out_shape += [jax.ShapeDtypeStruct((nseg, 1, w), F32) for w in accs]
    out_specs += [pl.BlockSpec((None, 1, w), lambda i: (seg(i), 0, 0)) for w in accs]
    n_in, n_out, n_acc = len(args), len(outs), len(accs)

    def kern(*refs):
        i = pl.program_id(0)
        ins = [r[...] for r in refs[:n_in]]
        for k, lack in enumerate(lacking):
            if lack:
                ins[k] = jnp.where(i >= lack, ins[k], jnp.zeros_like(ins[k]))
        res, terms = fn(*ins)
        for ref, v in zip(refs[n_in:n_in + n_out], res):
            ref[...] = v.astype(ref.dtype)
        if n_acc:
            sums = [_sum0(v) for v in terms]
            first = (i == 0) | (i == seg_blocks) if nseg == 2 else (i == 0)
            acc_refs = refs[n_in + n_out:]

            @pl.when(first)
            def _():
                for ref, v in zip(acc_refs, sums):
                    ref[...] = v

            @pl.when(jnp.logical_not(first))
            def _():
                for ref, v in zip(acc_refs, sums):
                    ref[...] += v

    res = pl.pallas_call(
        kern, name=name, grid=(n_rows // tm,), in_specs=in_specs, out_specs=out_specs, out_shape=out_shape,
        compiler_params=_params("arbitrary"),
    )(*args)
    return res


def _pre_fwd_fn(h, g, shift, scale):
    hh, _ = _rms(h)
    return (hh * g * (1.0 + scale) + shift,), ()


def _pre_bwd_fn(du, h, dres, g, scale):
    hh, r = _rms(h)
    n = hh * g
    dn = du * (1.0 + scale)
    dhh = dn * g
    dh = dres + r * (dhh - hh * jnp.mean(dhh * hh, axis=-1, keepdims=True))
    return (dh,), (du, du * n, dn * hh)


def _post_fwd_fn(weight, h, y, g, gate):
    yh, _ = _rms(y)
    return (h + weight * gate * (yh * g),), ()


def _out_post_fn(weight, y, h, g, gate):
    return (y,) + _post_fwd_fn(weight, h, y, g, gate)[0], ()


def _post_bwd_fn(weight, dh, y, g, gate):
    yh, r = _rms(y)
    dr = dh * weight
    dyh = dr * gate * g
    dy = r * (dyh - yh * jnp.mean(dyh * yh, axis=-1, keepdims=True))
    return (dy,), (dr * yh * g, dr * gate * yh)


def _glu_bwd_fn(ds, a, b):
    a = a.astype(F32)
    b = b.astype(F32)
    sg = _sig(a)
    da = ds * b * (sg * (1.0 + a * (1.0 - sg)))
    db = ds * (a * sg)
    return (jnp.concatenate([da, db], axis=1),), ()


def _loss_fn(y, t):
    diff = y - t
    return (diff * (1.0 / D_MODEL),), (diff * diff,)


def _ssd_y(yf, yb, xs, z, dvec):
    y = yf + yb + dvec * xs
    return y, y * _silu(z)


def _ssdgate_fwd_fn(yf, yb, xs, z, dvec, ng):
    _, yg = _ssd_y(yf, yb, xs, z, dvec)
    parts = []
    for g in range(SSD_GROUPS):
        sl = slice(g * 256, (g + 1) * 256)
        parts.append(_rms(yg[:, sl])[0])
    return (jnp.concatenate(parts, axis=1) * ng,), ()


def _ssdgate_bwd_fn(dyn, yf, yb, xs, z, dvec, ng):
    y, yg = _ssd_y(yf, yb, xs, z, dvec)
    dyg_parts, ygh_parts = [], []
    for g in range(SSD_GROUPS):
        sl = slice(g * 256, (g + 1) * 256)
        ygh, r = _rms(yg[:, sl])
        d = dyn[:, sl] * ng[:, sl]
        dyg_parts.append(r * (d - ygh * jnp.mean(d * ygh, axis=-1, keepdims=True)))
        ygh_parts.append(ygh)
    dyg = jnp.concatenate(dyg_parts, axis=1)
    ygh = jnp.concatenate(ygh_parts, axis=1)
    dy = dyg * _silu(z)
    dz = dyg * y * _dsilu(z)
    return (dy, dz), (dyn * ygh, dy * xs)


def _ln_stats(v):
    mu = jnp.mean(v, axis=-1, keepdims=True)
    vc = v - mu
    r = lax.rsqrt(jnp.mean(vc * vc, axis=-1, keepdims=True) + EPS)
    return vc * r, r


def _gm_act_fwd_fn(p, vg, vb):
    gu = _gelu(p[:, :GM_INNER])
    gvh, _ = _ln_stats(_gelu(p[:, GM_INNER:]))
    return (gu, gvh * vg + vb), ()


def _gm_act_bwd_fn(p, dgu, dgvn, vg):
    pu = p[:, :GM_INNER]
    pv = p[:, GM_INNER:]
    gvh, r = _ln_stats(_gelu(pv))
    dgvh = dgvn * vg
    dgv = r * (dgvh - jnp.mean(dgvh, axis=-1, keepdims=True) - gvh * jnp.mean(dgvh * gvh, axis=-1, keepdims=True))
    dp = jnp.concatenate([dgu * _dgelu(pu), dgv * _dgelu(pv)], axis=1)
    return (dp,), (dgvn * gvh, dgvn)


def _mm(a, b, *, out_dtype, name, tm=1088, tn=1024, tk=1408, add=None, rhs_t=False, n=None, b_off=(0, 0)):
    m, k = a.shape
    if n is None:
        n, k2 = b.shape if rhs_t else b.shape[::-1]
        assert k == k2
    tm, tn, tk = _pick(m, tm), _pick(n, tn, 128), _pick(k, tk, 128)
    o0, o1 = b_off
    nk = k // tk
    dims = _NT if rhs_t else ((1,), (0,))

    def kern(*refs):
        a_ref, b_ref = refs[:2]
        add_ref = refs[2] if add is not None else None
        o_ref = refs[3] if add is not None else refs[2]

        def finish(r):
            if add is not None:
                r = r + add_ref[...]
            o_ref[...] = r.astype(o_ref.dtype)

        p = _dot(a_ref[...], b_ref[...], dims)
        if nk == 1:
            finish(p)
            return
        acc_ref = refs[-1]
        kk = pl.program_id(2)

        @pl.when(kk == 0)
        def _():
            acc_ref[...] = p

        @pl.when((kk > 0) & (kk < nk - 1))
        def _():
            acc_ref[...] += p

        @pl.when(kk == nk - 1)
        def _():
            finish(acc_ref[...] + p)

    if rhs_t:
        b_spec = pl.BlockSpec((tn, tk), lambda i, j, kk: (j + o0, kk + o1))
    else:
        b_spec = pl.BlockSpec((tk, tn), lambda i, j, kk: (kk + o0, j + o1))
    in_specs = [pl.BlockSpec((tm, tk), lambda i, j, kk: (i, kk)), b_spec]
    args = [a, b]
    if add is not None:
        in_specs.append(pl.BlockSpec((tm, tn), lambda i, j, kk: (i, j)))
        args.append(add)
    return pl.pallas_call(
        kern, name=name, grid=(m // tm, n // tn, nk), in_specs=in_specs,
        out_specs=pl.BlockSpec((tm, tn), lambda i, j, kk: (i, j)),
        out_shape=jax.ShapeDtypeStruct((m, n), out_dtype),
        scratch_shapes=[pltpu.VMEM((tm, tn), F32)] if nk > 1 else [],
        compiler_params=_params("parallel", "parallel", "arbitrary"),
    )(*args)


def _mm_rows(a, b, fn, rows, consts, outs, accs=(), *, name, tm=544, tk=1408, rhs_t=False, n_ctx=0):
    halves = a.ndim == 3
    m, k = (a.shape[1], 2 * a.shape[2]) if halves else a.shape
    n = b.shape[0] if rhs_t else b.shape[1]
    tm, tk = _pick(m, tm), _pick(k, tk, 128)
    nk = k // tk
    if halves:
        hb = k // 2 // tk
        a_spec = pl.BlockSpec((None, tm, tk), lambda i, kk: (kk // hb, i, kk % hb))
    else:
        a_spec = pl.BlockSpec((tm, tk), lambda i, kk: (i, kk))
    dims = _NT if rhs_t else ((1,), (0,))
    n_rows, n_const, n_out, n_acc = len(rows), len(consts), len(outs), len(accs)

    def kern(*refs):
        a_ref, b_ref = refs[:2]
        row_refs = refs[2:2 + n_rows]
        const_refs = refs[2 + n_rows:2 + n_rows + n_const]
        out_refs = refs[2 + n_rows + n_const:2 + n_rows + n_const + n_out]
        acc_refs = refs[2 + n_rows + n_const + n_out:2 + n_rows + n_const + n_out + n_acc]
        i, kk = pl.program_id(0), pl.program_id(1)

        def finish(p, rs=slice(None), r0=0):
            nr = p.shape[0]
            is_ctx = (i * tm + r0 + lax.broadcasted_iota(jnp.int32, (nr, 1), 0)) < n_ctx
            cvals = []
            for (kind, arr), ref in zip(consts, const_refs):
                if kind == "seg":
                    cvals.append(jnp.where(is_ctx, ref[0], ref[1]) if arr.shape[0] == 2 else ref[0])
                else:
                    cvals.append(ref[...])
            res, terms = fn(p, *[r[rs, :] for r in row_refs], *cvals)
            for ref, v in zip(out_refs, res):
                ref[rs, :] = v.astype(ref.dtype)
            for ref, v in zip(acc_refs, terms):
                s_all = _sum0(v)
                s_ctx = _sum0(jnp.where(is_ctx, v, 0.0)) if n_ctx else jnp.zeros_like(s_all)
                both = jnp.concatenate([s_ctx, s_all - s_ctx], axis=0)[:, None, :]

                @pl.when(i == 0)
                def _():
                    ref[...] = both

                @pl.when(i > 0)
                def _():
                    ref[...] += both

        if nk == 1 and n_acc == 0:
            nsub = next(s for s in (4, 2, 1) if tm % (16 * s) == 0)
            sub = tm // nsub
            for r in range(nsub):
                rs = slice(r * sub, (r + 1) * sub)
                finish(_dot(a_ref[rs, :], b_ref[...], dims), rs, r * sub)
            return
        p = _dot(a_ref[...], b_ref[...], dims)
        if nk == 1:
            finish(p)
            return
        scr = refs[-1]

        @pl.when(kk == 0)
        def _():
            scr[...] = p

        @pl.when((kk > 0) & (kk < nk - 1))
        def _():
            scr[...] += p

        @pl.when(kk == nk - 1)
        def _():
            finish(scr[...] + p)

    b_spec = pl.BlockSpec((n, tk), lambda i, kk: (0, kk)) if rhs_t else pl.BlockSpec((tk, n), lambda i, kk: (kk, 0))
    in_specs = [a_spec, b_spec]
    in_specs += [pl.BlockSpec((tm, r.shape[1]), lambda i, kk: (i, 0)) for r in rows]
    for kind, arr in consts:
        in_specs.append(pl.BlockSpec(arr.shape, (lambda i, kk: (0, 0, 0)) if kind == "seg" else (lambda i, kk: (0, 0))))
    out_shape = [jax.ShapeDtypeStruct((m, w), dt) for w, dt in outs]
    out_specs = [pl.BlockSpec((tm, w), lambda i, kk: (i, 0)) for w, _ in outs]
    out_shape += [jax.ShapeDtypeStruct((2, 1, w), F32) for w in accs]
    out_specs += [pl.BlockSpec((2, 1, w), lambda i, kk: (0, 0, 0)) for w in accs]
    return pl.pallas_call(
        kern, name=name, grid=(m // tm, nk), in_specs=in_specs, out_specs=out_specs, out_shape=out_shape,
        scratch_shapes=[pltpu.VMEM((tm, n), F32)] if nk > 1 else [],
        compiler_params=_params("arbitrary", "arbitrary"),
    )(a, b, *rows, *[arr for _, arr in consts])


def _mm_glu(u, win_t, *, name, tm=2176, tn=256):
    m, k = u.shape
    n = win_t.shape[0] // 2
    tm, tn = _pick(m, tm), _pick(n, tn, 128)
    nj = n // tn

    nsub = 4 if tm % 64 == 0 else 1
    sub = tm // nsub

    def kern(u_ref, wa_ref, wb_ref, s_ref, a_ref, b_ref):
        for r in range(nsub):
            rows = slice(r * sub, (r + 1) * sub)
            uu = u_ref[rows, :]
            a = _dot(uu, wa_ref[...], _NT)
            b = _dot(uu, wb_ref[...], _NT)
            s_ref[rows, :] = (_silu(a) * b).astype(BF16)
            a_ref[rows, :] = a.astype(BF16)
            b_ref[rows, :] = b.astype(BF16)

    ospec = pl.BlockSpec((tm, tn), lambda i, j: (i, j))
    return pl.pallas_call(
        kern, name=name, grid=(m // tm, nj),
        in_specs=[pl.BlockSpec((tm, k), lambda i, j: (i, 0)), pl.BlockSpec((tn, k), lambda i, j: (j, 0)),
                  pl.BlockSpec((tn, k), lambda i, j: (nj + j, 0))],
        out_specs=[ospec, ospec, ospec],
        out_shape=[jax.ShapeDtypeStruct((m, n), BF16)] * 3,
        compiler_params=_params("parallel", "parallel"),
    )(u, win_t, win_t)


def _mm_glu_bwd(dy, wout, a, b, *, name, tm=544, tn=1408):
    m, k = dy.shape
    f = wout.shape[0]
    tm, tn = _pick(m, tm), _pick(f, tn, 128)
    nsub = next(s for s in (4, 2, 1) if tm % (16 * s) == 0)
    sub = tm // nsub

    def kern(dy_ref, w_ref, a_ref, b_ref, o_ref):
        for r in range(nsub):
            rs = slice(r * sub, (r + 1) * sub)
            ds = _dot(dy_ref[rs, :], w_ref[...], _NT)
            (dp,), _ = _glu_bwd_fn(ds, a_ref[rs, :], b_ref[rs, :])
            o_ref[0, rs, :] = dp[:, :tn].astype(BF16)
            o_ref[1, rs, :] = dp[:, tn:].astype(BF16)

    tile = pl.BlockSpec((tm, tn), lambda i, j: (i, j))
    return pl.pallas_call(
        kern, name=name, grid=(m // tm, f // tn),
        in_specs=[pl.BlockSpec((tm, k), lambda i, j: (i, 0)), pl.BlockSpec((tn, k), lambda i, j: (j, 0)), tile, tile],
        out_specs=pl.BlockSpec((2, tm, tn), lambda i, j: (0, i, j)),
        out_shape=jax.ShapeDtypeStruct((2, m, f), BF16),
        compiler_params=_params("parallel", "parallel"),
    )(dy, wout, a, b)


def _mm_tn(a, b, *, name, tm=1024, tn=1024, tk=1088, col_blocks=None):
    halves = a.ndim == 3
    t, m = (a.shape[1], 2 * a.shape[2]) if halves else a.shape
    t2, n = b.shape
    assert t == t2
    tm, tn, tk = _pick(m, tm, 128), _pick(n, tn, 128), _pick(t, tk)
    nk = t // tk
    if halves:
        hb = m // 2 // tm
        a_spec = pl.BlockSpec((None, tk, tm), lambda i, j, kk: (i // hb, kk, i % hb))
    else:
        a_spec = pl.BlockSpec((tk, tm), lambda i, j, kk: (kk, i))
    if col_blocks is None:
        def kern(a_ref, b_ref, o_ref):
            kk = pl.program_id(2)

            @pl.when(kk == 0)
            def _():
                o_ref[...] = jnp.zeros_like(o_ref)

            o_ref[...] += _dot(a_ref[...], b_ref[...], _TN)

        out_spec = pl.BlockSpec((tm, tn), lambda i, j, kk: (i, j))
        out_shape = jax.ShapeDtypeStruct((m, n), F32)
        scratch = []
    else:
        wb = n // col_blocks
        per = tn // wb
        assert tn % wb == 0 and wb % 8 == 0

        def kern(a_ref, b_ref, o_ref, acc_ref):
            kk = pl.program_id(2)
            p = _dot(a_ref[...], b_ref[...], _TN)

            @pl.when(kk == 0)
            def _():
                acc_ref[...] = p

            @pl.when((kk > 0) & (kk < nk - 1))
            def _():
                acc_ref[...] += p

            @pl.when(kk == nk - 1)
            def _():
                r = acc_ref[...] + p if nk > 1 else p
                for c in range(per):
                    o_ref[c] = r[:, c * wb:(c + 1) * wb].astype(BF16)

        out_spec = pl.BlockSpec((per, tm, wb), lambda i, j, kk: (j, i, 0))
        out_shape = jax.ShapeDtypeStruct((col_blocks, m, wb), BF16)
        scratch = [pltpu.VMEM((tm, tn), F32)]

    return pl.pallas_call(
        kern, name=name, grid=(m // tm, n // tn, nk),
        in_specs=[a_spec, pl.BlockSpec((tk, tn), lambda i, j, kk: (kk, j))],
        out_specs=out_spec, out_shape=out_shape, scratch_shapes=scratch,
        compiler_params=_params("parallel", "parallel", "arbitrary"),
    )(a, b)


def _mm_f32(a, b, *, name, silu_a=False, bias=None):
    m, k = a.shape
    n = b.shape[1]

    def kern(*refs):
        if bias is None:
            a_ref, b_ref, o_ref = refs
        else:
            a_ref, b_ref, bias_ref, o_ref = refs
        av = a_ref[...]
        if silu_a:
            av = _silu(av)
        r = jnp.dot(av, b_ref[...], preferred_element_type=F32, precision=HI)
        if bias is not None:
            r = r + bias_ref[...]
        o_ref[...] = r

    args = [a, b] + ([] if bias is None else [bias])
    return pl.pallas_call(kern, name=name, out_shape=jax.ShapeDtypeStruct((m, n), F32),
                          compiler_params=pltpu.CompilerParams(vmem_limit_bytes=VMEM_LIMIT_BYTES))(*args)


CONV_WIN = 32


def _conv_windows(n, n_ctx):
    assert n_ctx % CONV_WIN == 0 and n_ctx >= CONV_WIN and n - n_ctx >= CONV_WIN
    return (0, n_ctx - CONV_WIN // 2, n - CONV_WIN)


def _tap_outside(r0, s, n, n_ctx):
    t = r0 + lax.broadcasted_iota(jnp.int32, (CONV_WIN, 1), 0)
    lo = jnp.where(t < n_ctx, 0, n_ctx)
    hi = jnp.where(t < n_ctx, n_ctx, n)
    return jnp.where((t + s >= lo) & (t + s < hi), 0.0, 1.0)


def _rolled(v, s):
    return v if s == 0 else pltpu.roll(v, (-s) % v.shape[0], 0)


def _conv_fwd(xp, w8, b, *, n_ctx, name, cb=256):
    n, c = xp.shape
    half = SSD_CONV // 2

    def kern(x_ref, w_ref, b_ref, cpre_ref, act_ref):
        x = x_ref[...]
        acc = jnp.zeros_like(x) + b_ref[...]
        rolled = {}
        for k in range(SSD_CONV):
            rolled[k] = _rolled(x, k - half)
            acc = acc + rolled[k] * w_ref[k:k + 1, :]
        cpre_ref[...] = acc
        act_ref[...] = _silu(acc)
        for r0 in _conv_windows(n, n_ctx):
            rows = slice(r0, r0 + CONV_WIN)
            fix = acc[rows]
            for k in range(SSD_CONV):
                if k != half:
                    fix = fix - rolled[k][rows] * w_ref[k:k + 1, :] * _tap_outside(r0, k - half, n, n_ctx)
            cpre_ref[rows, :] = fix
            act_ref[rows, :] = _silu(fix)

    spec = pl.BlockSpec((n, cb), lambda j: (0, j))
    return pl.pallas_call(
        kern, name=name, grid=(c // cb,),
        in_specs=[spec, pl.BlockSpec((8, cb), lambda j: (0, j)), pl.BlockSpec((1, cb), lambda j: (0, j))],
        out_specs=[spec, spec], out_shape=[jax.ShapeDtypeStruct((n, c), F32)] * 2,
        compiler_params=_params("parallel"),
    )(xp, w8, b)


def _conv_bwd(d1, d2, cpre, xp, w8, *, n_ctx, name, cb=128):
    n, c = xp.shape
    half = SSD_CONV // 2

    def kern(d1_ref, d2_ref, cpre_ref, x_ref, w_ref, dx_ref, dw_ref, db_ref):
        g = (d1_ref[...] + d2_ref[...]) * _dsilu(cpre_ref[...])
        x = x_ref[...]
        dx = jnp.zeros_like(g)
        dw_ref[...] = jnp.zeros_like(dw_ref)
        g_rolled = {}
        for k in range(SSD_CONV):
            s = k - half
            g_rolled[k] = _rolled(g, -s)
            dx = dx + g_rolled[k] * w_ref[k:k + 1, :]
            xr = _rolled(x, s)
            dw = _sum0(g * xr)
            if s != 0:
                for r0 in _conv_windows(n, n_ctx):
                    rows = slice(r0, r0 + CONV_WIN)
                    dw = dw - _sum0(g[rows] * xr[rows] * _tap_outside(r0, s, n, n_ctx))
            dw_ref[k:k + 1, :] = dw
        dx_ref[...] = dx.astype(BF16)
        for r0 in _conv_windows(n, n_ctx):
            rows = slice(r0, r0 + CONV_WIN)
            fix = dx[rows]
            for k in range(SSD_CONV):
                if k != half:
                    fix = fix - g_rolled[k][rows] * w_ref[k:k + 1, :] * _tap_outside(r0, half - k, n, n_ctx)
            dx_ref[rows, :] = fix.astype(BF16)
        db_ref[...] = _sum0(g)

    spec = pl.BlockSpec((n, cb), lambda j: (0, j))
    return pl.pallas_call(
        kern, name=name, grid=(c // cb,),
        in_specs=[spec, spec, spec, spec, pl.BlockSpec((8, cb), lambda j: (0, j))],
        out_specs=[spec, pl.BlockSpec((8, cb), lambda j: (0, j)), pl.BlockSpec((1, cb), lambda j: (0, j))],
        out_shape=[jax.ShapeDtypeStruct((n, c), BF16), jax.ShapeDtypeStruct((8, c), F32),
                   jax.ShapeDtypeStruct((1, c), F32)],
        compiler_params=_params("parallel"),
    )(d1, d2, cpre, xp, w8)


def _chunk_of(s, nc, n_ctx_chunks, rev):
    if not rev:
        return s
    return jnp.where(s < n_ctx_chunks, n_ctx_chunks - 1 - s, nc - 1 - (s - n_ctx_chunks))


def _scan_common(dt_raw, dtT_raw, bias_r, bias_c, alog_r, alog_c, rev):
    ii = lax.broadcasted_iota(jnp.int32, (CHUNK, CHUNK), 0)
    jj = lax.broadcasted_iota(jnp.int32, (CHUNK, CHUNK), 1)
    tri = (jj >= ii) if rev else (jj <= ii)
    tri_t = (ii >= jj) if rev else (ii <= jj)
    a_r = -jnp.exp(alog_r)
    a_c = -jnp.exp(alog_c)
    dt = _softplus(dt_raw + bias_r)
    dt_t = _softplus(dtT_raw + bias_c)
    al = dt * a_r
    acum = _dot(tri.astype(F32), al, precision=HI)
    acum_t = _dot(dt_t * a_c, tri_t.astype(F32), precision=HI)
    atot = _sum0(al)
    return tri, tri_t, a_r, dt, acum, acum_t, atot


def _head_spread():
    return jnp.repeat(jnp.eye(SSD_HEADS, dtype=BF16), SSD_HEAD_DIM, axis=1)


def _dot_sel(v, sel):
    hi = v.astype(BF16)
    lo = (v - hi.astype(F32)).astype(BF16)
    return _dot(hi, sel) + _dot(lo, sel)


def _ssd_scan_fwd(xbc, dt_raw, dtT_raw, bias_r, bias_c, alog_r, alog_c, *, rev, n_ctx_chunks, name):
    n = xbc.shape[0]
    nc = n // CHUNK
    cidx = functools.partial(_chunk_of, nc=nc, n_ctx_chunks=n_ctx_chunks, rev=rev)

    def kern(xs_ref, b_ref, c_ref, dt_ref, dtT_ref, br_ref, bc_ref, ar_ref, ac_ref, e_ref, y_ref, hs_ref, h_scr):
        @pl.when(pl.program_id(0) == 0)
        def _():
            h_scr[...] = jnp.zeros_like(h_scr)

        tri, _, _, dt, acum, acum_t, atot = _scan_common(
            dt_ref[...], dtT_ref[...], br_ref[...], bc_ref[...], ar_ref[...], ac_ref[...], rev)
        etot = jnp.exp(atot)
        spread = lambda v: _dot_sel(v, e_ref[...])
        xdt_all = xs_ref[...] * spread(dt)
        eax = spread(jnp.exp(acum))
        xdw_all = xdt_all * spread(jnp.exp(atot - acum))
        hs_ref[...] = h_scr[...]
        for g in range(SSD_GROUPS):
            gs = slice(g * 256, (g + 1) * 256)
            bg = b_ref[:, g * SSD_STATE:(g + 1) * SSD_STATE].astype(BF16)
            cg = c_ref[:, g * SSD_STATE:(g + 1) * SSD_STATE].astype(BF16)
            cb = _dot(cg, bg, _NT)
            h4 = h_scr[gs, :]
            ys = []
            for k in range(SSD_HPG):
                h = g * SSD_HPG + k
                lmat = jnp.exp(jnp.where(tri, acum[:, h:h + 1] - acum_t[h:h + 1, :], NEG_BIG))
                xdt_h = xdt_all[:, h * SSD_HEAD_DIM:(h + 1) * SSD_HEAD_DIM].astype(BF16)
                ys.append(_dot((cb * lmat).astype(BF16), xdt_h))
            y_ref[:, gs] = jnp.concatenate(ys, axis=1) + _dot(cg, h4.astype(BF16), _NT) * eax[:, gs]
            s4 = _dot(xdw_all[:, gs].astype(BF16), bg, _TN)
            for k in range(SSD_HPG):
                h = g * SSD_HPG + k
                rs = slice(h * SSD_HEAD_DIM, (h + 1) * SSD_HEAD_DIM)
                h_scr[rs, :] = h4[k * SSD_HEAD_DIM:(k + 1) * SSD_HEAD_DIM] * etot[:, h:h + 1] + \
                    s4[k * SSD_HEAD_DIM:(k + 1) * SSD_HEAD_DIM]

    nh = SSD_HEADS
    small = lambda shape: pl.BlockSpec(shape, lambda s: (0, 0))
    return pl.pallas_call(
        kern, name=name, grid=(nc,),
        in_specs=[pl.BlockSpec((CHUNK, SSD_INNER), lambda s: (cidx(s), 0)),
                  pl.BlockSpec((CHUNK, 1024), lambda s: (cidx(s), 2)),
                  pl.BlockSpec((CHUNK, 1024), lambda s: (cidx(s), 3)),
                  pl.BlockSpec((CHUNK, nh), lambda s: (cidx(s), 0)),
                  pl.BlockSpec((nh, CHUNK), lambda s: (0, cidx(s))),
                  small((1, nh)), small((nh, 1)), small((1, nh)), small((nh, 1)), small((nh, SSD_INNER))],
        out_specs=[pl.BlockSpec((CHUNK, SSD_INNER), lambda s: (cidx(s), 0)),
                   pl.BlockSpec((None, SSD_INNER, SSD_STATE), lambda s: (s, 0, 0))],
        out_shape=[jax.ShapeDtypeStruct((n, SSD_INNER), F32),
                   jax.ShapeDtypeStruct((nc, SSD_INNER, SSD_STATE), F32)],
        scratch_shapes=[pltpu.VMEM((SSD_INNER, SSD_STATE), F32)],
        compiler_params=_params("arbitrary"),
    )(xbc, xbc, xbc, dt_raw, dtT_raw, bias_r, bias_c, alog_r, alog_c, _head_spread())


def _ssd_scan_bwd(dy, xbc, hs, dt_raw, dtT_raw, bias_r, bias_c, alog_r, alog_c, dvec, *, rev, n_ctx_chunks,
                  direct, name):
    n = xbc.shape[0]
    nc = n // CHUNK
    nh = SSD_HEADS
    step_of = lambda r: nc - 1 - r
    cidx = lambda r: _chunk_of(step_of(r), nc, n_ctx_chunks, rev)

    def kern(dy_ref, xs_ref, b_ref, c_ref, hs_ref, dt_ref, dtT_ref, br_ref, bc_ref, ar_ref, ac_ref, dv_ref,
             e_ref, et_ref, dx_ref, ddt_ref, dal_ref, dbias_ref, dh_scr):
        @pl.when(pl.program_id(0) == 0)
        def _():
            dh_scr[...] = jnp.zeros_like(dh_scr)
            dal_ref[...] = jnp.zeros_like(dal_ref)
            dbias_ref[...] = jnp.zeros_like(dbias_ref)

        tri, tri_t, a_r, dt, acum, acum_t, atot = _scan_common(
            dt_ref[...], dtT_ref[...], br_ref[...], bc_ref[...], ar_ref[...], ac_ref[...], rev)
        etot = jnp.exp(atot)
        spread = lambda v: _dot_sel(v, e_ref[...])
        gather = lambda v: _dot_sel(v, et_ref[...])
        xs_all = xs_ref[...]
        dy_all = dy_ref[...]
        dtx = spread(dt)
        eax = spread(jnp.exp(acum))
        decx = spread(jnp.exp(atot - acum))
        xdt_all = xs_all * dtx
        xdw_all = xdt_all * decx
        dyo_all = dy_all * eax
        lane = lax.broadcasted_iota(jnp.int32, (CHUNK, nh), 1)
        lane1 = lax.broadcasted_iota(jnp.int32, (1, nh), 1)
        sub = lax.broadcasted_iota(jnp.int32, (nh, CHUNK), 0)
        g_rows = jnp.zeros((CHUNK, nh), F32)
        g_cols = jnp.zeros((nh, CHUNK), F32)
        dtot = jnp.zeros((1, nh), F32)
        q_col, q_e, q_dt = [], [], []
        for g in range(SSD_GROUPS):
            gs = slice(g * 256, (g + 1) * 256)
            bg = b_ref[:, g * SSD_STATE:(g + 1) * SSD_STATE].astype(BF16)
            cg = c_ref[:, g * SSD_STATE:(g + 1) * SSD_STATE].astype(BF16)
            cb = _dot(cg, bg, _NT)
            hs4 = hs_ref[gs, :]
            dh4 = dh_scr[gs, :]
            hs4_bf = hs4.astype(BF16)
            dh4_bf = dh4.astype(BF16)
            dy4 = dy_all[:, gs]
            dy4_bf = dy4.astype(BF16)
            xdt4_bf = xdt_all[:, gs].astype(BF16)
            xdw4 = xdw_all[:, gs]
            xdw4_bf = xdw4.astype(BF16)
            dyo4_bf = dyo_all[:, gs].astype(BF16)
            yoff4 = _dot(cg, hs4_bf, _NT) * eax[:, gs]
            dcg = _dot(dyo4_bf, hs4_bf)
            dh_new4 = _dot(dyo4_bf, cg, _TN)
            bdh4 = _dot(bg, dh4_bf, _NT)
            dbg = _dot(xdw4_bf, dh4_bf)
            e4 = xdw4 * bdh4
            q_col.append(dy4 * yoff4 - e4)
            q_e.append(e4)
            hsum = jnp.sum(dh4 * hs4, axis=1, keepdims=True)
            dcb = jnp.zeros((CHUNK, CHUNK), F32)
            dxdts = []
            for k in range(SSD_HPG):
                h = g * SSD_HPG + k
                ks = slice(k * SSD_HEAD_DIM, (k + 1) * SSD_HEAD_DIM)
                lmat = jnp.exp(jnp.where(tri, acum[:, h:h + 1] - acum_t[h:h + 1, :], NEG_BIG))
                mf = cb * lmat
                dm = _dot(dy4_bf[:, ks], xdt4_bf[:, ks], _NT)
                dcb = dcb + dm * lmat
                gmat = dm * mf
                g_rows = g_rows + jnp.where(lane == h, jnp.sum(gmat, axis=1, keepdims=True), 0.0)
                g_cols = g_cols + jnp.where(sub == h, _sum0(gmat), 0.0)
                dxdts.append(_dot(mf.astype(BF16), dy4_bf[:, ks], _TN))
                et = etot[:, h:h + 1]
                dtot = dtot + jnp.where(lane1 == h, _sum0(hsum[ks]) * et, 0.0)
                dh_scr[h * SSD_HEAD_DIM:(h + 1) * SSD_HEAD_DIM, :] = dh4[ks] * et + dh_new4[ks]
            dxdt4 = jnp.concatenate(dxdts, axis=1) + bdh4 * decx[:, gs]
            q_dt.append(dxdt4 * xs_all[:, gs])
            dx4 = dxdt4 * dtx[:, gs]
            if direct:
                dx4 = dx4 + dy4 * dv_ref[:, gs]
            dcb_bf = dcb.astype(BF16)
            dx_ref[:, gs] = dx4
            dx_ref[:, SSD_INNER + g * SSD_STATE:SSD_INNER + (g + 1) * SSD_STATE] = dbg + _dot(dcb_bf, cg, _TN)
            dx_ref[:, SSD_INNER + 1024 + g * SSD_STATE:SSD_INNER + 1024 + (g + 1) * SSD_STATE] = \
                dcg + _dot(dcb_bf, bg)
        e_heads = gather(jnp.concatenate(q_e, axis=1))
        dacum = gather(jnp.concatenate(q_col, axis=1)) + g_rows - g_cols.T
        dal = _dot(tri_t.astype(F32), dacum, precision=HI) + dtot + _sum0(e_heads)
        ddt = gather(jnp.concatenate(q_dt, axis=1)) + dal * a_r
        ddt_raw = ddt * _sig(dt_ref[...] + br_ref[...])
        ddt_ref[...] = ddt_raw
        dal_ref[...] += _sum0(dal * dt) * a_r
        dbias_ref[...] += _sum0(ddt_raw)

    small = lambda shape: pl.BlockSpec(shape, lambda r: (0, 0))
    return pl.pallas_call(
        kern, name=name, grid=(nc,),
        in_specs=[pl.BlockSpec((CHUNK, SSD_INNER), lambda r: (cidx(r), 0)),
                  pl.BlockSpec((CHUNK, SSD_INNER), lambda r: (cidx(r), 0)),
                  pl.BlockSpec((CHUNK, 1024), lambda r: (cidx(r), 2)),
                  pl.BlockSpec((CHUNK, 1024), lambda r: (cidx(r), 3)),
                  pl.BlockSpec((None, SSD_INNER, SSD_STATE), lambda r: (step_of(r), 0, 0)),
                  pl.BlockSpec((CHUNK, nh), lambda r: (cidx(r), 0)),
                  pl.BlockSpec((nh, CHUNK), lambda r: (0, cidx(r))),
                  small((1, nh)), small((nh, 1)), small((1, nh)), small((nh, 1)), small((1, SSD_INNER)),
                  small((nh, SSD_INNER)), small((SSD_INNER, nh))],
        out_specs=[pl.BlockSpec((CHUNK, SSD_CONV_DIM), lambda r: (cidx(r), 0)),
                   pl.BlockSpec((CHUNK, nh), lambda r: (cidx(r), 0)),
                   small((1, nh)), small((1, nh))],
        out_shape=[jax.ShapeDtypeStruct((n, SSD_CONV_DIM), F32), jax.ShapeDtypeStruct((n, nh), F32),
                   jax.ShapeDtypeStruct((1, nh), F32), jax.ShapeDtypeStruct((1, nh), F32)],
        scratch_shapes=[pltpu.VMEM((SSD_INNER, SSD_STATE), F32)],
        compiler_params=_params("arbitrary"),
    )(dy, xbc, xbc, xbc, hs, dt_raw, dtT_raw, bias_r, bias_c, alog_r, alog_c, dvec, _head_spread(),
      _head_spread().T)


def _gm_spatial_fwd(gu, gvn, ws, bst, *, name):
    n = gu.shape[0]

    def kern(gu_ref, gv_ref, ws_ref, bs_ref, o_ref):
        for g in range(GM_GROUPS):
            sl = slice(g * GM_GROUP_DIM, (g + 1) * GM_GROUP_DIM)
            s = _dot(ws_ref[g], gv_ref[:, sl]) + bs_ref[:, g:g + 1]
            o_ref[:, sl] = (gu_ref[:, sl] * s).astype(BF16)

    spec = pl.BlockSpec((CHUNK, GM_INNER), lambda i: (i, 0))
    return pl.pallas_call(
        kern, name=name, grid=(n // CHUNK,),
        in_specs=[spec, spec, pl.BlockSpec(ws.shape, lambda i: (0, 0, 0)), pl.BlockSpec(bst.shape, lambda i: (0, 0))],
        out_specs=spec, out_shape=jax.ShapeDtypeStruct((n, GM_INNER), BF16),
        compiler_params=_params("parallel"),
    )(gu, gvn, ws, bst)


def _gm_spatial_bwd(dt, gu, gvn, ws, wst, bst, *, name):
    n = gu.shape[0]

    def kern(dt_ref, gu_ref, gv_ref, ws_ref, wst_ref, bs_ref, dgu_ref, dgv_ref, dws_ref, dbs_ref):
        @pl.when(pl.program_id(0) == 0)
        def _():
            dws_ref[...] = jnp.zeros_like(dws_ref)
            dbs_ref[...] = jnp.zeros_like(dbs_ref)

        lane = lax.broadcasted_iota(jnp.int32, (CHUNK, GM_GROUPS), 1)
        dbs = jnp.zeros((CHUNK, GM_GROUPS), F32)
        for g in range(GM_GROUPS):
            sl = slice(g * GM_GROUP_DIM, (g + 1) * GM_GROUP_DIM)
            gv = gv_ref[:, sl]
            s = _dot(ws_ref[g], gv) + bs_ref[:, g:g + 1]
            d = dt_ref[:, sl]
            dgu_ref[:, sl] = d * s
            ds = d * gu_ref[:, sl]
            ds_bf = ds.astype(BF16)
            dws_ref[g] += _dot(ds_bf, gv, _NT)
            dgv_ref[:, sl] = _dot(wst_ref[g], ds_bf)
            dbs = dbs + jnp.where(lane == g, jnp.sum(ds, axis=1, keepdims=True), 0.0)
        dbs_ref[...] += dbs

    spec = pl.BlockSpec((CHUNK, GM_INNER), lambda i: (i, 0))
    wspec = pl.BlockSpec(ws.shape, lambda i: (0, 0, 0))
    bspec = pl.BlockSpec(bst.shape, lambda i: (0, 0))
    return pl.pallas_call(
        kern, name=name, grid=(n // CHUNK,),
        in_specs=[spec, spec, spec, wspec, wspec, bspec],
        out_specs=[spec, spec, wspec, bspec],
        out_shape=[jax.ShapeDtypeStruct((n, GM_INNER), F32), jax.ShapeDtypeStruct((n, GM_INNER), F32),
                   jax.ShapeDtypeStruct(ws.shape, F32), jax.ShapeDtypeStruct(bst.shape, F32)],
        compiler_params=_params("arbitrary"),
    )(dt, gu, gvn, ws, wst, bst)


def _adamw(parts, w, m, v, *, name, tm=256, sel=(), into=None):
    ns, r, wd = parts.shape
    tm = _pick(r, tm, 8)
    tc = wd
    if tm < 64 and wd % 256 == 0:
        tm, tc = r, 256
    lead = len(sel)
    assert w.shape[lead:] == (r, wd) and lead == w.ndim - 2

    def kern(*refs):
        p_ref, w_ref, m_ref, v_ref = refs[:4]
        g_ref, d_ref, nm_ref, nv_ref = refs[-4:]
        g = p_ref[0].astype(F32)
        for s in range(1, ns):
            g = g + p_ref[s].astype(F32)
        m2 = ADAM_B1 * m_ref[...] + (1.0 - ADAM_B1) * g
        v2 = ADAM_B2 * v_ref[...] + (1.0 - ADAM_B2) * (g * g)
        m_hat = m2 / (1.0 - ADAM_B1 ** ADAM_STEP)
        v_hat = v2 / (1.0 - ADAM_B2 ** ADAM_STEP)
        g_ref[...] = g
        d_ref[...] = -ADAM_LR * (m_hat / (jnp.sqrt(v_hat) + ADAM_EPS) + ADAM_WD * w_ref[...])
        nm_ref[...] = m2
        nv_ref[...] = v2

    spec = pl.BlockSpec((None,) * lead + (tm, tc), lambda i, j: tuple(sel) + (i, j))
    extra, aliases = [], {}
    if into is not None:
        extra = list(into)
        aliases = {4 + k: k for k in range(4)}
    return pl.pallas_call(
        kern, name=name, grid=(r // tm, wd // tc),
        in_specs=[pl.BlockSpec((ns, tm, tc), lambda i, j: (0, i, j)), spec, spec, spec] +
                 [pl.BlockSpec(memory_space=pl.ANY)] * len(extra),
        out_specs=[spec] * 4, out_shape=[jax.ShapeDtypeStruct(w.shape, F32)] * 4,
        input_output_aliases=aliases,
        compiler_params=_params("parallel", "parallel"),
    )(parts, w, m, v, *extra)


def _sum_slots(parts, *, name, scale_by=None):
    ns, r, wd = parts.shape

    def kern(*refs):
        p_ref, o_ref = refs[0], refs[-1]
        g = p_ref[0]
        for s in range(1, ns):
            g = g + p_ref[s]
        if scale_by is not None:
            g = g * _dsilu(refs[1][...])
        o_ref[...] = g

    args = [parts] + ([] if scale_by is None else [scale_by])
    return pl.pallas_call(kern, name=name, out_shape=jax.ShapeDtypeStruct((r, wd), F32),
                          compiler_params=pltpu.CompilerParams(vmem_limit_bytes=VMEM_LIMIT_BYTES))(*args)


def _mesh_pos():
    x, y, c = lax.axis_index("x"), lax.axis_index("y"), lax.axis_index("c")
    return x, y, c, 4 * x + 2 * y + c


def _flip(x, y, c, f):
    fx, fy, fc = (f >> 2) & 1, (f >> 1) & 1, f & 1
    px = 1 - x if fx else x
    py = 1 - y if fy else y
    pc = 1 - c if fc else c
    return (px, py, pc), 4 * px + 2 * py + pc


_HBM_SPEC = pl.BlockSpec(memory_space=pltpu.HBM)


def _exchange(arrays, *, scatter, name):
    na = len(arrays)
    if scatter:
        out_shape = [jax.ShapeDtypeStruct(a.shape, a.dtype) for a in arrays]
    else:
        out_shape = [jax.ShapeDtypeStruct((NDEV,) + a.shape, a.dtype) for a in arrays]

    out_shape.append(jax.ShapeDtypeStruct((8, 128), F32))

    def body(*refs):
        ins, outs = refs[:na], refs[na:2 * na]
        send_sems, recv_sems, local_sems = refs[2 * na + 1:]
        refs[2 * na][...] = jnp.zeros((8, 128), F32)
        x, y, c, me = _mesh_pos()
        copies = []
        for i in range(na):
            src_own = ins[i].at[me] if scatter else ins[i]
            lc = pltpu.make_async_copy(src_own, outs[i].at[me], local_sems.at[i])
            lc.start()
            copies.append(lc)
        sends = []
        for f in range(1, NDEV):
            peer, pidx = _flip(x, y, c, f)
            for i in range(na):
                k = i * (NDEV - 1) + f - 1
                src = ins[i].at[pidx] if scatter else ins[i]
                cp = pltpu.make_async_remote_copy(
                    src_ref=src, dst_ref=outs[i].at[me], send_sem=send_sems.at[k], recv_sem=recv_sems.at[k],
                    device_id=peer, device_id_type=pl.DeviceIdType.MESH)
                cp.start()
                sends.append(cp)
        for f in range(1, NDEV):
            peer, pidx = _flip(x, y, c, f)
            for i in range(na):
                k = i * (NDEV - 1) + f - 1
                src = ins[i].at[pidx] if scatter else ins[i]
                pltpu.make_async_remote_copy(
                    src_ref=src, dst_ref=outs[i].at[pidx], send_sem=send_sems.at[k], recv_sem=recv_sems.at[k],
                    device_id=peer, device_id_type=pl.DeviceIdType.MESH).wait_recv()
        for cp in sends:
            cp.wait_send()
        for lc in copies:
            lc.wait()

    res = pl.pallas_call(
        body, name=name, out_shape=out_shape, in_specs=[_HBM_SPEC] * na,
        out_specs=[_HBM_SPEC] * na + [pl.BlockSpec(memory_space=pltpu.VMEM)],
        scratch_shapes=[pltpu.SemaphoreType.DMA((na * (NDEV - 1),)), pltpu.SemaphoreType.DMA((na * (NDEV - 1),)),
                        pltpu.SemaphoreType.DMA((na,))],
        compiler_params=pltpu.CompilerParams(has_side_effects=True),
    )(*arrays)
    return res[:na], res[na][0, 0]


_SEM_SPEC = pl.BlockSpec(memory_space=pltpu.SEMAPHORE)
_DATAFLOW = pltpu.SideEffectType.DATAFLOW_SIDE_EFFECTING


def _split_copies(srcs, lands, send_sems, recv_sems, scatter, arriving):
    x, y, c, me = _mesh_pos()
    copies = []
    for i in range(len(srcs)):
        for f in range(1, NDEV):
            peer, pidx = _flip(x, y, c, f)
            k = i * (NDEV - 1) + f - 1
            copies.append(pltpu.make_async_remote_copy(
                src_ref=srcs[i].at[pidx] if scatter else srcs[i], dst_ref=lands[i].at[pidx if arriving else me],
                send_sem=send_sems.at[k], recv_sem=recv_sems.at[k], device_id=peer,
                device_id_type=pl.DeviceIdType.MESH))
    return copies


def _exchange_start(srcs, lands, *, scatter, name):
    na = len(srcs)
    nsem = na * (NDEV - 1)

    def body(*refs):
        ins_src, ins_land = refs[:na], refs[na:2 * na]
        send_sems, recv_sems = refs[2 * na], refs[2 * na + 1]
        token = refs[-1]
        for cp in _split_copies(ins_src, ins_land, send_sems, recv_sems, scatter, False):
            cp.start()
        token[...] = jnp.zeros_like(token)

    thru = [pltpu.HBM(a.shape, a.dtype) for a in list(srcs) + list(lands)]
    res = pl.pallas_call(
        body, name=name,
        out_shape=(pltpu.SemaphoreType.DMA((nsem,)), pltpu.SemaphoreType.DMA((nsem,)), *thru,
                   jax.ShapeDtypeStruct((8, 128), F32)),
        in_specs=[_HBM_SPEC] * (2 * na),
        out_specs=(_SEM_SPEC, _SEM_SPEC, *([_HBM_SPEC] * (2 * na)), pl.BlockSpec(memory_space=pltpu.VMEM)),
        input_output_aliases={i: 2 + i for i in range(2 * na)},
        compiler_params=pltpu.CompilerParams(has_side_effects=_DATAFLOW),
    )(*[pltpu.with_memory_space_constraint(a, pltpu.HBM) for a in list(srcs) + list(lands)])
    send_sems, recv_sems = res[0], res[1]
    return send_sems, recv_sems, res[2:2 + na], res[2 + na:2 + 2 * na], res[-1][0, 0]


def _exchange_wait(send_sems, recv_sems, srcs, lands, after, *, scatter, name):
    na = len(srcs)

    def body(*refs):
        ins_src, ins_land = refs[:na], refs[na:2 * na]
        s_sems, r_sems = refs[2 * na], refs[2 * na + 1]
        for cp in _split_copies(ins_src, ins_land, s_sems, r_sems, scatter, False):
            cp.wait_send()
        for cp in _split_copies(ins_src, ins_land, s_sems, r_sems, scatter, True):
            cp.wait_recv()

    thru = [pltpu.HBM(a.shape, a.dtype) for a in list(srcs) + list(lands)]
    res = pl.pallas_call(
        body, name=name, out_shape=tuple(thru),
        in_specs=[_HBM_SPEC] * (2 * na) + [_SEM_SPEC, _SEM_SPEC, pl.BlockSpec(memory_space=pl.ANY)],
        out_specs=tuple([_HBM_SPEC] * (2 * na)),
        input_output_aliases={i: i for i in range(2 * na)},
        compiler_params=pltpu.CompilerParams(has_side_effects=_DATAFLOW),
    )(*srcs, *lands, send_sems, recv_sems, after)
    return res[na:]


def _landing(block, me):
    buf = lax.empty((NDEV,) + block.shape, block.dtype)
    return lax.dynamic_update_slice_in_dim(buf, block[None], me, axis=0)


def _seg_kw(nseg, n_ctx, tm):
    return dict(nseg=nseg, seg_blocks=(n_ctx // tm if nseg == 2 else 0))


def _ffn_fwd(tag, h, gpre, gpost, shift, scale, gate, w, *, nseg, n_ctx, tm):
    n = h.shape[0]
    kw = _seg_kw(nseg, n_ctx, tm)
    (u,) = _rowwise(tag + "_pre", _pre_fwd_fn, n, [h], [("full", gpre), ("seg", shift), ("seg", scale)],
                    [(D_MODEL, BF16)], tm=tm, **kw)
    if "early" in w:
        w.update(w.pop("early")(u))
    s, a, b = _mm_glu(u, w["win_t"], name=tag + "_glu")
    if "late" in w:
        w.update(w.pop("late")(s))
    y, ho = _mm_rows(s, w["wout"], functools.partial(_out_post_fn, 0.5), [h], [("full", gpost), ("seg", gate)],
                     [(D_MODEL, F32), (D_MODEL, F32)], name=tag + "_out", tk=FFN_DIM, n_ctx=n_ctx)
    return ho, dict(h=h, u=u, s=s, a=a, b=b, y=y)


def _ffn_bwd(tag, dho, sv, gpre, gpost, scale, gate, w, put, *, nseg, n_ctx, tm):
    n = dho.shape[0]
    kw = _seg_kw(nseg, n_ctx, tm)
    dy, dgate, dgpost = _rowwise(tag + "_postb", functools.partial(_post_bwd_fn, 0.5), n, [dho, sv["y"]],
                                 [("full", gpost), ("seg", gate)], [(D_MODEL, BF16)], [D_MODEL, D_MODEL], tm=tm, **kw)
    tok = put("w_out", _mm_tn(sv["s"], dy, name=tag + "_dwout", tm=1408, tn=1024, col_blocks=1))
    dp = _mm_glu_bwd(dy, w["wout"], sv["a"], sv["b"], name=tag + "_ds")
    tok2 = put("w_in", _mm_tn(dp, sv["u"], name=tag + "_dwin", tm=1408, tn=1024, col_blocks=1))
    for t in (tok, tok2):
        if t is not None:
            gpre = gpre + t
    dh, dshift, dscale, dgpre = _mm_rows(dp, w["win_t"], _pre_bwd_fn, [sv["h"], dho],
                                         [("full", gpre), ("seg", scale)], [(D_MODEL, F32)],
                                         [D_MODEL, D_MODEL, D_MODEL], name=tag + "_du", n_ctx=n_ctx)
    return dh, None, dict(shift=dshift, scale=dscale, gate=dgate, gpre=dgpre, gpost=dgpost)


def _local_step(x, ctx, target, mods, norm_g, get_w, small, put_grad):
    t_len, n_ctx = x.shape[0], ctx.shape[0]
    n0 = t_len + n_ctx
    tm0 = _pick(n_ctx, 256, 8)
    tm1 = _pick(t_len, 256, 8)
    ncc = n_ctx // CHUNK
    g = {}

    def modrow(i, k, nseg):
        mc, mx = mods[i]
        if nseg == 2:
            return jnp.stack([mc[k], mx[k]])[:, None, :]
        return mx[k][None, None, :]

    pending = [None]

    def gvec(i, k):
        v = norm_g[i, k][None, :]
        if pending[0] is not None:
            v = v + pending[0]
            pending[0] = None
        return v

    xc = jnp.concatenate([ctx, x], axis=0)
    L0 = dict(nseg=2, n_ctx=n_ctx, tm=tm0)
    wts = dict(get_w("ffn00", xc))
    h1, sv_f01 = _ffn_fwd("l0f1", xc, gvec(0, 0), gvec(0, 1), modrow(0, 0, 2), modrow(0, 1, 2), modrow(0, 2, 2),
                          wts["ffn00"], **L0)
    kw0 = _seg_kw(2, n_ctx, tm0)
    (um0,) = _rowwise("l0m_pre", _pre_fwd_fn, n0, [h1], [("full", gvec(0, 2)), ("seg", modrow(0, 3, 2)),
                                                         ("seg", modrow(0, 4, 2))], [(D_MODEL, BF16)], tm=tm0, **kw0)
    wts.update(get_w("ssd", um0))
    win_ssd = wts["ssd_win_t"]
    nh = SSD_HEADS
    dt_blk = (SSD_INNER + SSD_CONV_DIM) // (2 * nh)
    z = _mm(um0, win_ssd, out_dtype=F32, name="ssd_z", rhs_t=True, n=SSD_INNER)
    xbc_pre = _mm(um0, win_ssd, out_dtype=F32, name="ssd_xbc", rhs_t=True, n=SSD_CONV_DIM,
                  b_off=(SSD_INNER // 1024, 0))
    dtr = _mm(um0, win_ssd, out_dtype=F32, name="ssd_dt", rhs_t=True, n=2 * nh, b_off=(dt_blk, 0))
    cpre, xbc = _conv_fwd(xbc_pre, small["conv_w8"], small["conv_b"], n_ctx=n_ctx, name="ssd_conv")
    nh = SSD_HEADS
    dt_dir = [dtr[:, :nh], dtr[:, nh:2 * nh]]
    dtT_dir = [d.T for d in dt_dir]
    bias_r = [small["dt_bias"][d][None, :] for d in range(2)]
    bias_c = [small["dt_bias"][d][:, None] for d in range(2)]
    alog_r = [small["a_log"][d][None, :] for d in range(2)]
    alog_c = [small["a_log"][d][:, None] for d in range(2)]
    ys, hss = [], []
    for d in range(2):
        yd, hsd = _ssd_scan_fwd(xbc, dt_dir[d], dtT_dir[d], bias_r[d], bias_c[d], alog_r[d], alog_c[d],
                                rev=(d == 1), n_ctx_chunks=ncc, name=f"ssd_scan{d}")
        ys.append(yd)
        hss.append(hsd)
    dvec = jnp.repeat(small["ssd_d"], SSD_HEAD_DIM)[None, :]
    ngv = small["ssd_norm_g"][None, :]
    gate_rows = [ys[0], ys[1], (xbc, SSD_INNER, 0, 0), z]
    lat = lambda r: (r[0], r[1], r[2], ncc) if isinstance(r, tuple) else (r, r.shape[1], 0, ncc)
    (yn,) = _rowwise("ssd_gate", _ssdgate_fwd_fn, t_len, [lat(r) for r in gate_rows],
                     [("full", dvec), ("full", ngv)], [(SSD_INNER, BF16)], tm=CHUNK)
    h1x = h1[n_ctx:]
    L1 = dict(nseg=1, n_ctx=0, tm=tm1)
    if "late" in wts:
        wts.update(wts.pop("late")(yn))
    yo0, h2 = _mm_rows(yn, wts["ssd_wout"], functools.partial(_out_post_fn, 1.0), [h1x],
                       [("full", gvec(0, 3)), ("seg", modrow(0, 5, 1))], [(D_MODEL, F32), (D_MODEL, F32)],
                       name="ssd_out", tk=SSD_INNER)
    wts.update(get_w("ffn01", h2))
    h3, sv_f02 = _ffn_fwd("l0f2", h2, gvec(0, 4), gvec(0, 5), modrow(0, 6, 1), modrow(0, 7, 1), modrow(0, 8, 1),
                          wts["ffn01"], **L1)

    wts.update(get_w("ffn10", h3))
    h4, sv_f11 = _ffn_fwd("l1f1", h3, gvec(1, 0), gvec(1, 1), modrow(1, 0, 1), modrow(1, 1, 1), modrow(1, 2, 1),
                          wts["ffn10"], **L1)
    (um1,) = _rowwise("l1m_pre", _pre_fwd_fn, t_len, [h4], [("full", gvec(1, 2)), ("seg", modrow(1, 3, 1)),
                                                            ("seg", modrow(1, 4, 1))], [(D_MODEL, BF16)], tm=tm1)
    wts.update(get_w("gm", um1))
    p1 = _mm(um1, wts["gm_win"], out_dtype=F32, name="gm_in")
    vg = small["gm_v_g"][None, :]
    vb = small["gm_v_b"][None, :]
    gu, gvn = _rowwise("gm_act", _gm_act_fwd_fn, t_len, [p1], [("full", vg), ("full", vb)],
                       [(GM_INNER, F32), (GM_INNER, BF16)], tm=128)
    ws_bf = small["gm_w_s"].astype(BF16)
    wst_bf = jnp.swapaxes(small["gm_w_s"], 1, 2).astype(BF16)
    bst = small["gm_b_s"].T
    tgm = _gm_spatial_fwd(gu, gvn, ws_bf, bst, name="gm_spatial")
    yo1, h5 = _mm_rows(tgm, wts["gm_wout"], functools.partial(_out_post_fn, 1.0), [h4],
                       [("full", gvec(1, 3)), ("seg", modrow(1, 5, 1))], [(D_MODEL, F32), (D_MODEL, F32)],
                       name="gm_out", tk=GM_INNER)
    wts.update(get_w("ffn11", h5))
    h6, sv_f12 = _ffn_fwd("l1f2", h5, gvec(1, 4), gvec(1, 5), modrow(1, 6, 1), modrow(1, 7, 1), modrow(1, 8, 1),
                          wts["ffn11"], **L1)

    dh, loss_parts = _rowwise("loss", _loss_fn, t_len, [h6, target], [], [(D_MODEL, F32)], [D_MODEL], tm=tm1)

    zero = jnp.zeros((D_MODEL,), F32)
    dmx = [[zero] * N_MOD for _ in range(2)]
    dmc = [[zero] * N_MOD for _ in range(2)]
    dng = [[zero] * 6 for _ in range(2)]

    def put_mod(i, k, acc):
        if acc.shape[0] == 2:
            dmc[i][k] = dmc[i][k] + acc[0, 0]
            dmx[i][k] = dmx[i][k] + acc[1, 0]
        else:
            dmx[i][k] = dmx[i][k] + acc[0, 0]

    def put_g(i, k, acc):
        dng[i][k] = dng[i][k] + jnp.sum(acc[:, 0], axis=0)

    def ffn_back(tag, i, j, dho, sv, w, lay):
        nseg = lay["nseg"]
        base = 0 if j == 0 else 6
        gi = 0 if j == 0 else 4
        dh_in, pending[0], s = _ffn_bwd(tag, dho, sv, gvec(i, gi), gvec(i, gi + 1), modrow(i, base + 1, nseg),
                                        modrow(i, base + 2, nseg), w, functools.partial(put_grad, f"ffn{i}{j}"), **lay)
        put_mod(i, base, s["shift"])
        put_mod(i, base + 1, s["scale"])
        put_mod(i, base + 2, s["gate"])
        put_g(i, gi, s["gpre"])
        put_g(i, gi + 1, s["gpost"])
        return dh_in

    dh = ffn_back("l1f2", 1, 1, dh, sv_f12, wts["ffn11"], L1)
    dyo, dgate, dgp = _rowwise("l1m_postb", functools.partial(_post_bwd_fn, 1.0), t_len, [dh, yo1],
                               [("full", gvec(1, 3)), ("seg", modrow(1, 5, 1))], [(D_MODEL, BF16)],
                               [D_MODEL, D_MODEL], tm=tm1)
    put_mod(1, 5, dgate)
    put_g(1, 3, dgp)
    put_grad("gm", "w_out", _mm_tn(tgm, dyo, name="gm_dwout", tn=1024, col_blocks=1))
    dtg = _mm(dyo, wts["gm_wout"], out_dtype=F32, name="gm_dt", rhs_t=True)
    dgu, dgvn, dws, dbst = _gm_spatial_bwd(dtg, gu, gvn, ws_bf, wst_bf, bst, name="gm_spatialb")
    g["gm_w_s"] = dws
    g["gm_b_s"] = dbst.T
    dp1, dvg, dvb = _rowwise("gm_actb", _gm_act_bwd_fn, t_len, [p1, dgu, dgvn], [("full", vg)],
                             [(2 * GM_INNER, BF16)], [GM_INNER, GM_INNER], tm=128)
    g["gm_v_g"] = dvg[0, 0]
    g["gm_v_b"] = dvb[0, 0]
    pending[0] = put_grad("gm", "w_in", _mm_tn(um1, dp1, name="gm_dwin", tm=1024, col_blocks=NDEV))
    dh, dsh, dsc, dgp = _mm_rows(dp1, wts["gm_win"], _pre_bwd_fn, [h4, dh],
                                 [("full", gvec(1, 2)), ("seg", modrow(1, 4, 1))], [(D_MODEL, F32)],
                                 [D_MODEL, D_MODEL, D_MODEL], name="gm_dum", tk=1024, rhs_t=True)
    put_mod(1, 3, dsh)
    put_mod(1, 4, dsc)
    put_g(1, 2, dgp)
    dh = ffn_back("l1f1", 1, 0, dh, sv_f11, wts["ffn10"], L1)

    dh = ffn_back("l0f2", 0, 1, dh, sv_f02, wts["ffn01"], L1)
    dyo, dgate, dgp = _rowwise("l0m_postb", functools.partial(_post_bwd_fn, 1.0), t_len, [dh, yo0],
                               [("full", gvec(0, 3)), ("seg", modrow(0, 5, 1))], [(D_MODEL, BF16)],
                               [D_MODEL, D_MODEL], tm=tm1)
    put_mod(0, 5, dgate)
    put_g(0, 3, dgp)
    tok = put_grad("ssd", "w_out", _mm_tn(yn, dyo, name="ssd_dwout", tn=1024, col_blocks=1))
    dyn = _mm(dyo, wts["ssd_wout"], out_dtype=F32, name="ssd_dyn", rhs_t=True)
    dy_ssd, dz, dngv, ddv = _rowwise("ssd_gateb", _ssdgate_bwd_fn, n0, [(dyn, SSD_INNER, 0, -ncc)] + gate_rows,
                                     [("full", dvec), ("full", ngv if tok is None else ngv + tok)],
                                     [(SSD_INNER, F32), (SSD_INNER, BF16)],
                                     [SSD_INNER, SSD_INNER], tm=128)
    g["ssd_norm_g"] = dngv[0, 0]
    g["ssd_D"] = jnp.sum(ddv[0, 0].reshape(SSD_HEADS, SSD_HEAD_DIM), axis=1)
    dxbcs, ddts, dalogs, dbiases = [], [], [], []
    for d in range(2):
        dxd, ddtd, dal, dbi = _ssd_scan_bwd(dy_ssd, xbc, hss[d], dt_dir[d], dtT_dir[d], bias_r[d], bias_c[d],
                                            alog_r[d], alog_c[d], dvec, rev=(d == 1), n_ctx_chunks=ncc,
                                            direct=(d == 0), name=f"ssd_scanb{d}")
        dxbcs.append(dxd)
        ddts.append(ddtd)
        dalogs.append(dal[0])
        dbiases.append(dbi[0])
    g["ssd_A_log"] = jnp.stack(dalogs)
    g["ssd_dt_bias"] = jnp.stack(dbiases)
    dxbc_pre, dcw8, dcb = _conv_bwd(dxbcs[0], dxbcs[1], cpre, xbc_pre, small["conv_w8"], n_ctx=n_ctx, name="ssd_convb")
    g["ssd_conv_w"] = dcw8[:SSD_CONV]
    g["ssd_conv_b"] = dcb[0]
    ddt_bf = jnp.concatenate([ddts[0], ddts[1]], axis=1).astype(BF16)
    dw_ssd_in_t = jnp.concatenate([
        _mm_tn(dz, um0, name="ssd_dwz", col_blocks=1),
        _mm_tn(dxbc_pre, um0, name="ssd_dwxbc", col_blocks=1),
        _mm_tn(ddt_bf, um0, name="ssd_dwdt", col_blocks=1)], axis=1)
    pending[0] = put_grad("ssd", "w_in", dw_ssd_in_t)
    dum0 = _mm(dz, win_ssd, out_dtype=F32, name="ssd_dum_z", tk=1024, n=D_MODEL)
    dum0 = _mm(dxbc_pre, win_ssd, out_dtype=F32, name="ssd_dum_x", tk=1024, n=D_MODEL,
               b_off=(SSD_INNER // 1024, 0), add=dum0)
    dum0 = _mm(ddt_bf, win_ssd, out_dtype=F32, name="ssd_dum_dt", tk=2 * nh, n=D_MODEL, b_off=(dt_blk, 0), add=dum0)
    dh0, dsh, dsc, dgp = _rowwise("l0m_preb", _pre_bwd_fn, n0, [dum0, h1, (dh, D_MODEL, 0, -(n_ctx // tm0))],
                                  [("full", gvec(0, 2)), ("seg", modrow(0, 4, 2))], [(D_MODEL, F32)],
                                  [D_MODEL, D_MODEL, D_MODEL], tm=tm0, **kw0)
    put_mod(0, 3, dsh)
    put_mod(0, 4, dsc)
    put_g(0, 2, dgp)
    dh0 = ffn_back("l0f1", 0, 0, dh0, sv_f01, wts["ffn00"], L0)
    grad_x = dh0[n_ctx:]
    g["norm_g"] = jnp.stack([jnp.stack(r) for r in dng])
    g["dmx"] = jnp.stack([jnp.concatenate(r) for r in dmx])
    g["dmc"] = jnp.stack([jnp.concatenate(r) for r in dmc])
    return loss_parts[0], grad_x, g


GROUPS = ("ffn00", "ssd", "ffn01", "ffn10", "gm", "ffn11")


TRANSPOSED_IN = ("ffn", "ssd")


def _is_transposed(group):
    return group.startswith(TRANSPOSED_IN)


def _mats_in(group, win_l):
    if _is_transposed(group):
        return {("win_t" if group.startswith("ffn") else group + "_win_t"): win_l.reshape(-1, win_l.shape[2])}
    k, nloc = win_l.shape[1], win_l.shape[2]
    return {group + "_win": jnp.transpose(win_l, (1, 0, 2)).reshape(k, NDEV * nloc)}


def _mats_out(group, wout_l):
    pre = "" if group.startswith("ffn") else group + "_"
    return {pre + "wout": wout_l.reshape(-1, wout_l.shape[2])}


def _group_mats(group, lands):
    m = {**_mats_in(group, lands[0]), **_mats_out(group, lands[1])}
    return {group: m} if group.startswith("ffn") else m


def _grad_blocks(which, grad):
    if grad.ndim == 3:
        return grad if grad.shape[0] == NDEV else grad.reshape(NDEV, grad.shape[1] // NDEV, grad.shape[2])
    if which == "w_in":
        k, n = grad.shape
        return jnp.transpose(grad.reshape(k, NDEV, n // NDEV), (1, 0, 2)).astype(BF16)
    return grad.reshape(NDEV, grad.shape[0] // NDEV, grad.shape[1]).astype(BF16)


def kernel(x, c, ctx, c_ctx, ada_w, ada_b, norm_g, ffn_w_in, ffn_w_out, ssd_w_in, ssd_conv_w, ssd_conv_b, ssd_dt_bias, ssd_A_log, ssd_D, ssd_norm_g, ssd_w_out, gm_w_in, gm_v_g, gm_v_b, gm_w_s, gm_b_s, gm_w_out, loss_target, m_c_ctx, m_ada_w, m_ada_b, m_norm_g, m_ffn_w_in, m_ffn_w_out, m_ssd_w_in, m_ssd_conv_w, m_ssd_conv_b, m_ssd_dt_bias, m_ssd_A_log, m_ssd_D, m_ssd_norm_g, m_ssd_w_out, m_gm_w_in, m_gm_v_g, m_gm_v_b, m_gm_w_s, m_gm_b_s, m_gm_w_out, v_c_ctx, v_ada_w, v_ada_b, v_norm_g, v_ffn_w_in, v_ffn_w_out, v_ssd_w_in, v_ssd_conv_w, v_ssd_conv_b, v_ssd_dt_bias, v_ssd_A_log, v_ssd_D, v_ssd_norm_g, v_ssd_w_out, v_gm_w_in, v_gm_v_g, v_gm_v_b, v_gm_w_s, v_gm_b_s, v_gm_w_out):
    me = 4 * lax.axis_index("x") + 2 * lax.axis_index("y") + lax.axis_index("c")
    d = D_MODEL
    ncol = N_MOD * d // NDEV

    small_pack = jnp.concatenate([c.reshape(-1), norm_g.reshape(-1), ssd_conv_w.reshape(-1),
                                  gm_v_g.reshape(-1), gm_v_b.reshape(-1)])[None, :]
    (sp,), _ = _exchange([small_pack], scatter=False, name="gather_small")
    sp = sp[:, 0]
    o = 0
    c_all = sp[:, o:o + d]; o += d
    ng_all = sp[:, o:o + 2 * 6 * 128].reshape(NDEV, 2, 6, 128); o += 2 * 6 * 128
    cw_all = sp[:, o:o + SSD_CONV * 512].reshape(NDEV, SSD_CONV, 512); o += SSD_CONV * 512
    vg_all = sp[:, o:o + 256]; o += 256
    vb_all = sp[:, o:o + 256]; o += 256
    norm_g_full = jnp.transpose(ng_all, (1, 2, 0, 3)).reshape(2, 6, d)
    conv_w_full = jnp.transpose(cw_all, (1, 0, 2)).reshape(SSD_CONV, SSD_CONV_DIM)
    gm_v_g_full = vg_all.reshape(-1)
    gm_v_b_full = vb_all.reshape(-1)

    c16 = jnp.concatenate([c_all, jnp.broadcast_to(c_ctx[None, :], (NDEV, d))], axis=0)
    ada_b_loc = lax.dynamic_slice_in_dim(ada_b, me * ncol, ncol, axis=1)
    mods_loc = jnp.stack([_mm_f32(c16, ada_w[i], name=f"ada_mod{i}", silu_a=True, bias=ada_b_loc[i][None, :])
                          for i in range(2)])
    (mods_all,), mods_done = _exchange([mods_loc], scatter=False, name="gather_mods")

    tr = lambda a: jnp.swapaxes(a, -1, -2)
    shard = {"ssd": (tr(ssd_w_in)[0], ssd_w_out[0]), "gm": (gm_w_in[0], gm_w_out[0])}
    for i in range(2):
        for j in range(2):
            shard[f"ffn{i}{j}"] = (tr(ffn_w_in)[i, j], ffn_w_out[i, j])
    apart = GROUPS[:2]
    units = []
    for grp in GROUPS:
        units += [(grp + "_in", grp, (0,)), (grp + "_out", grp, (1,))] if grp in apart else [(grp, grp, (0, 1))]
    gathers = {}
    started = mods_done
    for unit, grp, idx in units:
        srcs = [(shard[grp][k] + started).astype(BF16) for k in idx]
        st = _exchange_start(srcs, [_landing(s, me) for s in srcs], scatter=False, name="gather_start_" + unit)
        gathers[unit] = st[:4]
        started = st[4]

    def fetch(unit, after):
        return _exchange_wait(*gathers[unit], after, scatter=False, name="gather_wait_" + unit)

    def get_w(grp, after):
        if grp not in apart:
            return _group_mats(grp, fetch(grp, after))
        early = lambda later: _mats_in(grp, fetch(grp + "_in", later)[0])
        late = lambda later: _mats_out(grp, fetch(grp + "_out", later)[0])
        if grp.startswith("ffn"):
            return {grp: dict(early=early, late=late)}
        return dict(early(after), late=late)

    scatters = {}
    held = {}

    def put_grad(grp, which, grad):
        if grp in apart:
            unit, blocks = grp + "_" + which[2:], [_grad_blocks(which, grad)]
        else:
            held[grp, which] = _grad_blocks(which, grad)
            if (grp, "w_in") not in held or (grp, "w_out") not in held:
                return None
            unit, blocks = grp, [held[grp, "w_in"], held[grp, "w_out"]]
        own = [lax.dynamic_index_in_dim(b, me, axis=0, keepdims=False) for b in blocks]
        st = _exchange_start(blocks, [_landing(o_, me) for o_ in own], scatter=True, name="scatter_start_" + unit)
        scatters[unit] = st[:4]
        return st[4]

    mods_rows = jnp.transpose(mods_all, (1, 2, 0, 3)).reshape(2, 2 * NDEV, N_MOD * d) + started
    mx = lax.dynamic_index_in_dim(mods_rows, me, axis=1, keepdims=False).reshape(2, N_MOD, d)
    mc = mods_rows[:, NDEV].reshape(2, N_MOD, d)
    mods = [(mc[i], mx[i]) for i in range(2)]

    small = dict(conv_w8=jnp.pad(conv_w_full, ((0, 8 - SSD_CONV), (0, 0))), conv_b=ssd_conv_b, dt_bias=ssd_dt_bias[0],
                 a_log=ssd_A_log[0], ssd_d=ssd_D[0], ssd_norm_g=ssd_norm_g[0], gm_v_g=gm_v_g_full,
                 gm_v_b=gm_v_b_full, gm_w_s=gm_w_s[0], gm_b_s=gm_b_s[0])
    loss_parts, grad_x, g = _local_step(x[0], ctx[0], loss_target[0], mods, norm_g_full, get_w, small, put_grad)
    g["loss"] = (0.5 / d * jnp.sum(loss_parts)).reshape(1)

    whole = {"ffn_w_in": (tr(ffn_w_in), tr(m_ffn_w_in), tr(v_ffn_w_in)), "ffn_w_out": (ffn_w_out, m_ffn_w_out, v_ffn_w_out),
             "ssd_w_in": (tr(ssd_w_in), tr(m_ssd_w_in), tr(v_ssd_w_in)), "ssd_w_out": (ssd_w_out, m_ssd_w_out, v_ssd_w_out),
             "gm_w_in": (gm_w_in, m_gm_w_in, v_gm_w_in), "gm_w_out": (gm_w_out, m_gm_w_out, v_gm_w_out)}
    res = {}

    def update_units(some, after):
        for unit, grp, idx in some:
            parts = _exchange_wait(*scatters[unit], after, scatter=True, name="scatter_wait_" + unit)
            for k, p in zip(idx, parts):
                which = ("in", "out")[k]
                nm = ("ffn" if grp.startswith("ffn") else grp) + "_w_" + which
                sel = (int(grp[3]), int(grp[4])) if grp.startswith("ffn") else (0,)
                res[nm] = _adamw(p, *whole[nm], name=f"adamw_{grp}_{which}", sel=sel, into=res.get(nm))
                after = res[nm][0]
        return after

    sg_names = ["dmx", "dmc", "norm_g", "ssd_conv_w", "ssd_conv_b", "ssd_dt_bias", "ssd_A_log", "ssd_D", "ssd_norm_g",
                "gm_v_g", "gm_v_b", "gm_w_s", "gm_b_s", "loss"]
    sg_shapes = [g[n].shape for n in sg_names]
    flat = jnp.concatenate([g[n].reshape(-1) for n in sg_names])
    npack = flat.shape[0]
    pad = (-npack) % 1024
    flat = jnp.pad(flat, (0, pad)).reshape(-1, 128)
    sg_start = _exchange_start([flat], [_landing(flat, me)], scatter=False, name="small_grads_start")
    by_send = list(reversed(units))
    update_units(by_send[:4], jnp.stack([sg_start[4], grad_x[0, 0]]))
    early_done = jnp.stack([res[nm][0].reshape(-1)[-1] for nm in sorted(res)])
    (sg_all,) = _exchange_wait(*sg_start[:4], early_done, scatter=False, name="small_grads_wait")
    sg_sum = _sum_slots(sg_all, name="sum_small_grads").reshape(-1)[:npack]
    update_units(by_send[4:], sg_sum)
    sums = {}
    o = 0
    for n, shp in zip(sg_names, sg_shapes):
        sz = math.prod(shp)
        sums[n] = sg_sum[o:o + sz].reshape(shp)
        o += sz
    loss = sums["loss"][0]
    per_dev = sg_all.reshape(NDEV, -1)
    dmx_all =per_dev[:, :2 * N_MOD * d].reshape(NDEV, 2, N_MOD * d)
    dmc_all = per_dev[:, 2 * N_MOD * d:4 * N_MOD * d].reshape(NDEV, 2, N_MOD * d)

    (s16,) = _rowwise("ada_silu", lambda cc: ((_silu(cc),), ()), 2 * NDEV, [c16], [], [(d, F32)], tm=2 * NDEV)
    s16_t = s16.T
    g_ada_w, dcc_parts = [], []
    for i in range(2):
        rhs = jnp.concatenate([lax.dynamic_slice_in_dim(dmx_all[:, i], me * ncol, ncol, axis=1),
                               lax.dynamic_slice_in_dim(dmc_all[:, i], me * ncol, ncol, axis=1)], axis=0)
        g_ada_w.append(_mm_f32(s16_t, rhs, name=f"ada_dw{i}"))
        dmc_loc = lax.dynamic_slice_in_dim(sums["dmc"][i], me * ncol, ncol, axis=0)
        rhs_c = jnp.zeros((ncol, 128), F32).at[:, 0].set(dmc_loc)
        dcc_parts.append(_mm_f32(ada_w[i], rhs_c, name=f"ada_dcc{i}")[:, 0])
    g_ada_w = jnp.stack(g_ada_w)
    dcc_part = (dcc_parts[0] + dcc_parts[1]).reshape(8, 128)
    (dcc_all,), _ = _exchange([dcc_part], scatter=False, name="gather_dcc")
    g_c_ctx = _sum_slots(dcc_all, name="sum_dcc", scale_by=c_ctx.reshape(8, 128)).reshape(d)
    g_ada_b = sums["dmx"] + sums["dmc"]

    outs = _adamw(g_ada_w.reshape(1, -1, ncol), ada_w.reshape(-1, ncol), m_ada_w.reshape(-1, ncol),
                  v_ada_w.reshape(-1, ncol), name="adamw_ada_w")
    res["ada_w"] = [o_.reshape(ada_w.shape) for o_ in outs]

    loc = lambda a, ax, n: lax.dynamic_slice_in_dim(a, me * n, n, axis=ax)
    small_g = dict(c_ctx=g_c_ctx, ada_b=g_ada_b, norm_g=loc(sums["norm_g"], 2, 128),
                   ssd_conv_w=loc(sums["ssd_conv_w"], 1, 512)[None], ssd_conv_b=sums["ssd_conv_b"][None],
                   ssd_dt_bias=sums["ssd_dt_bias"][None], ssd_A_log=sums["ssd_A_log"][None], ssd_D=sums["ssd_D"][None],
                   ssd_norm_g=sums["ssd_norm_g"][None], gm_v_g=loc(sums["gm_v_g"], 0, 256)[None],
                   gm_v_b=loc(sums["gm_v_b"], 0, 256)[None], gm_w_s=sums["gm_w_s"][None], gm_b_s=sums["gm_b_s"][None])
    small_w = dict(c_ctx=(c_ctx, m_c_ctx, v_c_ctx), ada_b=(ada_b, m_ada_b, v_ada_b), norm_g=(norm_g, m_norm_g, v_norm_g),
                   ssd_conv_w=(ssd_conv_w, m_ssd_conv_w, v_ssd_conv_w), ssd_conv_b=(ssd_conv_b, m_ssd_conv_b, v_ssd_conv_b),
                   ssd_dt_bias=(ssd_dt_bias, m_ssd_dt_bias, v_ssd_dt_bias), ssd_A_log=(ssd_A_log, m_ssd_A_log, v_ssd_A_log),
                   ssd_D=(ssd_D, m_ssd_D, v_ssd_D), ssd_norm_g=(ssd_norm_g, m_ssd_norm_g, v_ssd_norm_g),
                   gm_v_g=(gm_v_g, m_gm_v_g, v_gm_v_g), gm_v_b=(gm_v_b, m_gm_v_b, v_gm_v_b),
                   gm_w_s=(gm_w_s, m_gm_w_s, v_gm_w_s), gm_b_s=(gm_b_s, m_gm_b_s, v_gm_b_s))
    sn = list(small_w)

    def pack(arrs):
        f = jnp.concatenate([a.reshape(-1) for a in arrs])
        return jnp.pad(f, (0, (-f.shape[0]) % (256 * 128))).reshape(-1, 128)

    pg = pack([small_g[n].reshape(small_w[n][0].shape) for n in sn])
    outs = _adamw(pg[None], pack([small_w[n][0] for n in sn]), pack([small_w[n][1] for n in sn]),
                  pack([small_w[n][2] for n in sn]), name="adamw_small")
    flat_outs = [o_.reshape(-1) for o_ in outs]
    o = 0
    for n in sn:
        shp = small_w[n][0].shape
        sz = math.prod(shp)
        res[n] = [fo[o:o + sz].reshape(shp) for fo in flat_outs]
        o += sz

    order = ["c_ctx", "ada_w", "ada_b", "norm_g", "ffn_w_in", "ffn_w_out", "ssd_w_in", "ssd_conv_w", "ssd_conv_b",
             "ssd_dt_bias", "ssd_A_log", "ssd_D", "ssd_norm_g", "ssd_w_out", "gm_w_in", "gm_v_g", "gm_v_b", "gm_w_s",
             "gm_b_s", "gm_w_out"]
    for nm in ("ffn_w_in", "ssd_w_in"):
        res[nm] = [tr(a) for a in res[nm]]
    result = [loss, grad_x[None]]
    for k in range(4):
        result += [res[n][k] for n in order]
    return tuple(result)
```

```python
import functools
import math

import jax
import jax.numpy as jnp
from jax import lax
from jax.experimental import pallas as pl
from jax.experimental.pallas import tpu as pltpu

F32 = jnp.float32
BF16 = jnp.bfloat16

NDEV = 8
D_MODEL = 1024
FFN_DIM = 2816
N_MOD = 9
EPS = 1e-6
SSD_INNER = 2048
SSD_HEADS = 32
SSD_HEAD_DIM = 64
SSD_GROUPS = 8
SSD_HPG = 4
SSD_STATE = 128
SSD_CONV = 5
SSD_CONV_DIM = 4096
CHUNK = 128
GM_INNER = 2048
GM_GROUPS = 8
GM_GROUP_DIM = 256
ADAM_LR = 0.001
ADAM_B1 = 0.9
ADAM_B2 = 0.999
ADAM_EPS = 1e-08
ADAM_WD = 0.01
ADAM_STEP = 10
NEG_BIG = -1e30
VMEM_LIMIT_BYTES = 56 * 1024 * 1024
HI = lax.Precision.HIGHEST


def _params(*sem):
    return pltpu.CompilerParams(dimension_semantics=sem, vmem_limit_bytes=VMEM_LIMIT_BYTES)


def _pick(n, target, mult=16):
    if n <= target:
        return n
    for t in range(target - target % mult, 0, -mult):
        if n % t == 0:
            return t
    raise ValueError((n, target, mult))


def _sig(x):
    return 0.5 * jnp.tanh(0.5 * x) + 0.5


def _silu(x):
    return x * _sig(x)


def _dsilu(x):
    s = _sig(x)
    return s * (1.0 + x * (1.0 - s))


_GELU_C = math.sqrt(2.0 / math.pi)


def _gelu(x):
    return 0.5 * x * (1.0 + jnp.tanh(_GELU_C * (x + 0.044715 * x * x * x)))


def _dgelu(x):
    t = jnp.tanh(_GELU_C * (x + 0.044715 * x * x * x))
    return 0.5 * (1.0 + t) + 0.5 * x * (1.0 - t * t) * _GELU_C * (1.0 + 3.0 * 0.044715 * x * x)


def _softplus(x):
    return jnp.maximum(x, 0.0) + jnp.log1p(jnp.exp(-jnp.abs(x)))


def _sum0(v):
    return jnp.sum(v, axis=0, keepdims=True)


def _rms(h):
    r = lax.rsqrt(jnp.mean(h * h, axis=-1, keepdims=True) + EPS)
    return h * r, r


def _dot(a, b, dims=((1,), (0,)), precision=None):
    return lax.dot_general(a, b, (dims, ((), ())), preferred_element_type=F32, precision=precision)


_NT = ((1,), (1,))
_TN = ((0,), (0,))


def _rowwise(name, fn, n_rows, rows, consts, outs, accs=(), *, tm, nseg=1, seg_blocks=0):
    assert n_rows % tm == 0
    if nseg == 2:
        assert seg_blocks > 0
        seg = lambda i: jnp.where(i < seg_blocks, 0, 1)
    else:
        seg = lambda i: 0
    in_specs, args, lacking = [], [], []
    for r in rows:
        arr, width, cb, off = r if isinstance(r, tuple) else (r, r.shape[1], 0, 0)
        in_specs.append(pl.BlockSpec((tm, width), lambda i, cb=cb, off=off: (jnp.maximum(i + off, 0), cb)))
        args.append(arr)
        lacking.append(-off if off < 0 else 0)
    for kind, arr in consts:
        if kind == "seg":
            assert arr.shape[0] == nseg and arr.shape[1] == 1, arr.shape
            in_specs.append(pl.BlockSpec((None, 1, arr.shape[2]), lambda i: (seg(i), 0, 0)))
        else:
            in_specs.append(pl.BlockSpec(arr.shape, lambda i: (0, 0)))
        args.append(arr)
    out_shape = [jax.ShapeDtypeStruct((n_rows, w), dt) for w, dt in outs]
    out_specs = [pl.BlockSpec((tm, w), lambda i: (i, 0)) for w, _ in outs]
    out_shape += [jax.ShapeDtypeStruct((nseg, 1, w), F32) for w in accs]
    out_specs += [pl.BlockSpec((None, 1, w), lambda i: (seg(i), 0, 0)) for w in accs]
    n_in, n_out, n_acc = len(args), len(outs), len(accs)

    def kern(*refs):
        i = pl.program_id(0)
        ins = [r[...] for r in refs[:n_in]]
        for k, lack in enumerate(lacking):
            if lack:
                ins[k] = jnp.where(i >= lack, ins[k], jnp.zeros_like(ins[k]))
        res, terms = fn(*ins)
        for ref, v in zip(refs[n_in:n_in + n_out], res):
            ref[...] = v.astype(ref.dtype)
        if n_acc:
            sums = [_sum0(v) for v in terms]
            first = (i == 0) | (i == seg_blocks) if nseg == 2 else (i == 0)
            acc_refs = refs[n_in + n_out:]

            @pl.when(first)
            def _():
                for ref, v in zip(acc_refs, sums):
                    ref[...] = v

            @pl.when(jnp.logical_not(first))
            def _():
                for ref, v in zip(acc_refs, sums):
                    ref[...] += v

    res = pl.pallas_call(
        kern, name=name, grid=(n_rows // tm,), in_specs=in_specs, out_specs=out_specs, out_shape=out_shape,
        compiler_params=_params("arbitrary"),
    )(*args)
    return res


def _pre_fwd_fn(h, g, shift, scale):
    hh, _ = _rms(h)
    return (hh * g * (1.0 + scale) + shift,), ()


def _pre_bwd_fn(du, h, dres, g, scale):
    hh, r = _rms(h)
    n = hh * g
    dn = du * (1.0 + scale)
    dhh = dn * g
    dh = dres + r * (dhh - hh * jnp.mean(dhh * hh, axis=-1, keepdims=True))
    return (dh,), (du, du * n, dn * hh)


def _post_fwd_fn(weight, h, y, g, gate):
    yh, _ = _rms(y)
    return (h + weight * gate * (yh * g),), ()


def _out_post_fn(weight, y, h, g, gate):
    return (y,) + _post_fwd_fn(weight, h, y, g, gate)[0], ()


def _post_bwd_fn(weight, dh, y, g, gate):
    yh, r = _rms(y)
    dr = dh * weight
    dyh = dr * gate * g
    dy = r * (dyh - yh * jnp.mean(dyh * yh, axis=-1, keepdims=True))
    return (dy,), (dr * yh * g, dr * gate * yh)


def _glu_bwd_fn(ds, a, b):
    a = a.astype(F32)
    b = b.astype(F32)
    sg = _sig(a)
    da = ds * b * (sg * (1.0 + a * (1.0 - sg)))
    db = ds * (a * sg)
    return (jnp.concatenate([da, db], axis=1),), ()


def _loss_fn(y, t):
    diff = y - t
    return (diff * (1.0 / D_MODEL),), (diff * diff,)


def _ssd_y(yf, yb, xs, z, dvec):
    y = yf + yb + dvec * xs
    return y, y * _silu(z)


def _ssdgate_fwd_fn(yf, yb, xs, z, dvec, ng):
    _, yg = _ssd_y(yf, yb, xs, z, dvec)
    parts = []
    for g in range(SSD_GROUPS):
        sl = slice(g * 256, (g + 1) * 256)
        parts.append(_rms(yg[:, sl])[0])
    return (jnp.concatenate(parts, axis=1) * ng,), ()


def _ssdgate_bwd_fn(dyn, yf, yb, xs, z, dvec, ng):
    y, yg = _ssd_y(yf, yb, xs, z, dvec)
    dyg_parts, ygh_parts = [], []
    for g in range(SSD_GROUPS):
        sl = slice(g * 256, (g + 1) * 256)
        ygh, r = _rms(yg[:, sl])
        d = dyn[:, sl] * ng[:, sl]
        dyg_parts.append(r * (d - ygh * jnp.mean(d * ygh, axis=-1, keepdims=True)))
        ygh_parts.append(ygh)
    dyg = jnp.concatenate(dyg_parts, axis=1)
    ygh = jnp.concatenate(ygh_parts, axis=1)
    dy = dyg * _silu(z)
    dz = dyg * y * _dsilu(z)
    return (dy, dz), (dyn * ygh, dy * xs)


def _ln_stats(v):
    mu = jnp.mean(v, axis=-1, keepdims=True)
    vc = v - mu
    r = lax.rsqrt(jnp.mean(vc * vc, axis=-1, keepdims=True) + EPS)
    return vc * r, r


def _gm_act_fwd_fn(p, vg, vb):
    gu = _gelu(p[:, :GM_INNER])
    gvh, _ = _ln_stats(_gelu(p[:, GM_INNER:]))
    return (gu, gvh * vg + vb), ()


def _gm_act_bwd_fn(p, dgu, dgvn, vg):
    pu = p[:, :GM_INNER]
    pv = p[:, GM_INNER:]
    gvh, r = _ln_stats(_gelu(pv))
    dgvh = dgvn * vg
    dgv = r * (dgvh - jnp.mean(dgvh, axis=-1, keepdims=True) - gvh * jnp.mean(dgvh * gvh, axis=-1, keepdims=True))
    dp = jnp.concatenate([dgu * _dgelu(pu), dgv * _dgelu(pv)], axis=1)
    return (dp,), (dgvn * gvh, dgvn)


def _mm(a, b, *, out_dtype, name, tm=1088, tn=1024, tk=1408, add=None, rhs_t=False, n=None, b_off=(0, 0)):
    m, k = a.shape
    if n is None:
        n, k2 = b.shape if rhs_t else b.shape[::-1]
        assert k == k2
    tm, tn, tk = _pick(m, tm), _pick(n, tn, 128), _pick(k, tk, 128)
    o0, o1 = b_off
    nk = k // tk
    dims = _NT if rhs_t else ((1,), (0,))

    def kern(*refs):
        a_ref, b_ref = refs[:2]
        add_ref = refs[2] if add is not None else None
        o_ref = refs[3] if add is not None else refs[2]

        def finish(r):
            if add is not None:
                r = r + add_ref[...]
            o_ref[...] = r.astype(o_ref.dtype)

        p = _dot(a_ref[...], b_ref[...], dims)
        if nk == 1:
            finish(p)
            return
        acc_ref = refs[-1]
        kk = pl.program_id(2)

        @pl.when(kk == 0)
        def _():
            acc_ref[...] = p

        @pl.when((kk > 0) & (kk < nk - 1))
        def _():
            acc_ref[...] += p

        @pl.when(kk == nk - 1)
        def _():
            finish(acc_ref[...] + p)

    if rhs_t:
        b_spec = pl.BlockSpec((tn, tk), lambda i, j, kk: (j + o0, kk + o1))
    else:
        b_spec = pl.BlockSpec((tk, tn), lambda i, j, kk: (kk + o0, j + o1))
    in_specs = [pl.BlockSpec((tm, tk), lambda i, j, kk: (i, kk)), b_spec]
    args = [a, b]
    if add is not None:
        in_specs.append(pl.BlockSpec((tm, tn), lambda i, j, kk: (i, j)))
        args.append(add)
    return pl.pallas_call(
        kern, name=name, grid=(m // tm, n // tn, nk), in_specs=in_specs,
        out_specs=pl.BlockSpec((tm, tn), lambda i, j, kk: (i, j)),
        out_shape=jax.ShapeDtypeStruct((m, n), out_dtype),
        scratch_shapes=[pltpu.VMEM((tm, tn), F32)] if nk > 1 else [],
        compiler_params=_params("parallel", "parallel", "arbitrary"),
    )(*args)


def _mm_rows(a, b, fn, rows, consts, outs, accs=(), *, name, tm=544, tk=1408, rhs_t=False, n_ctx=0):
    halves = a.ndim == 3
    m, k = (a.shape[1], 2 * a.shape[2]) if halves else a.shape
    n = b.shape[0] if rhs_t else b.shape[1]
    tm, tk = _pick(m, tm), _pick(k, tk, 128)
    nk = k // tk
    if halves:
        hb = k // 2 // tk
        a_spec = pl.BlockSpec((None, tm, tk), lambda i, kk: (kk // hb, i, kk % hb))
    else:
        a_spec = pl.BlockSpec((tm, tk), lambda i, kk: (i, kk))
    dims = _NT if rhs_t else ((1,), (0,))
    n_rows, n_const, n_out, n_acc = len(rows), len(consts), len(outs), len(accs)

    def kern(*refs):
        a_ref, b_ref = refs[:2]
        row_refs = refs[2:2 + n_rows]
        const_refs = refs[2 + n_rows:2 + n_rows + n_const]
        out_refs = refs[2 + n_rows + n_const:2 + n_rows + n_const + n_out]
        acc_refs = refs[2 + n_rows + n_const + n_out:2 + n_rows + n_const + n_out + n_acc]
        i, kk = pl.program_id(0), pl.program_id(1)

        def finish(p, rs=slice(None), r0=0):
            nr = p.shape[0]
            is_ctx = (i * tm + r0 + lax.broadcasted_iota(jnp.int32, (nr, 1), 0)) < n_ctx
            cvals = []
            for (kind, arr), ref in zip(consts, const_refs):
                if kind == "seg":
                    cvals.append(jnp.where(is_ctx, ref[0], ref[1]) if arr.shape[0] == 2 else ref[0])
                else:
                    cvals.append(ref[...])
            res, terms = fn(p, *[r[rs, :] for r in row_refs], *cvals)
            for ref, v in zip(out_refs, res):
                ref[rs, :] = v.astype(ref.dtype)
            for ref, v in zip(acc_refs, terms):
                s_all = _sum0(v)
                s_ctx = _sum0(jnp.where(is_ctx, v, 0.0)) if n_ctx else jnp.zeros_like(s_all)
                both = jnp.concatenate([s_ctx, s_all - s_ctx], axis=0)[:, None, :]

                @pl.when(i == 0)
                def _():
                    ref[...] = both

                @pl.when(i > 0)
                def _():
                    ref[...] += both

        if nk == 1 and n_acc == 0:
            nsub = 2 if tm % 32 == 0 else 1
            sub = tm // nsub
            for r in range(nsub):
                rs = slice(r * sub, (r + 1) * sub)
                finish(_dot(a_ref[rs, :], b_ref[...], dims), rs, r * sub)
            return
        p = _dot(a_ref[...], b_ref[...], dims)
        if nk == 1:
            finish(p)
            return
        scr = refs[-1]

        @pl.when(kk == 0)
        def _():
            scr[...] = p

        @pl.when((kk > 0) & (kk < nk - 1))
        def _():
            scr[...] += p

        @pl.when(kk == nk - 1)
        def _():
            finish(scr[...] + p)

    b_spec = pl.BlockSpec((n, tk), lambda i, kk: (0, kk)) if rhs_t else pl.BlockSpec((tk, n), lambda i, kk: (kk, 0))
    in_specs = [a_spec, b_spec]
    in_specs += [pl.BlockSpec((tm, r.shape[1]), lambda i, kk: (i, 0)) for r in rows]
    for kind, arr in consts:
        in_specs.append(pl.BlockSpec(arr.shape, (lambda i, kk: (0, 0, 0)) if kind == "seg" else (lambda i, kk: (0, 0))))
    out_shape = [jax.ShapeDtypeStruct((m, w), dt) for w, dt in outs]
    out_specs = [pl.BlockSpec((tm, w), lambda i, kk: (i, 0)) for w, _ in outs]
    out_shape += [jax.ShapeDtypeStruct((2, 1, w), F32) for w in accs]
    out_specs += [pl.BlockSpec((2, 1, w), lambda i, kk: (0, 0, 0)) for w in accs]
    return pl.pallas_call(
        kern, name=name, grid=(m // tm, nk), in_specs=in_specs, out_specs=out_specs, out_shape=out_shape,
        scratch_shapes=[pltpu.VMEM((tm, n), F32)] if nk > 1 else [],
        compiler_params=_params("arbitrary", "arbitrary"),
    )(a, b, *rows, *[arr for _, arr in consts])


def _mm_glu(u, win_t, *, name, tm=2176, tn=256):
    m, k = u.shape
    n = win_t.shape[0] // 2
    tm, tn = _pick(m, tm), _pick(n, tn, 128)
    nj = n // tn

    nsub = 4 if tm % 64 == 0 else 1
    sub = tm // nsub

    def kern(u_ref, wa_ref, wb_ref, s_ref, a_ref, b_ref):
        for r in range(nsub):
            rows = slice(r * sub, (r + 1) * sub)
            uu = u_ref[rows, :]
            a = _dot(uu, wa_ref[...], _NT)
            b = _dot(uu, wb_ref[...], _NT)
            s_ref[rows, :] = (_silu(a) * b).astype(BF16)
            a_ref[rows, :] = a.astype(BF16)
            b_ref[rows, :] = b.astype(BF16)

    ospec = pl.BlockSpec((tm, tn), lambda i, j: (i, j))
    return pl.pallas_call(
        kern, name=name, grid=(m // tm, nj),
        in_specs=[pl.BlockSpec((tm, k), lambda i, j: (i, 0)), pl.BlockSpec((tn, k), lambda i, j: (j, 0)),
                  pl.BlockSpec((tn, k), lambda i, j: (nj + j, 0))],
        out_specs=[ospec, ospec, ospec],
        out_shape=[jax.ShapeDtypeStruct((m, n), BF16)] * 3,
        compiler_params=_params("parallel", "parallel"),
    )(u, win_t, win_t)


def _mm_glu_bwd(dy, wout, a, b, *, name, tm=544, tn=1408):
    m, k = dy.shape
    f = wout.shape[0]
    tm, tn = _pick(m, tm), _pick(f, tn, 128)
    nsub = 2 if tm % 32 == 0 else 1
    sub = tm // nsub

    def kern(dy_ref, w_ref, a_ref, b_ref, o_ref):
        for r in range(nsub):
            rs = slice(r * sub, (r + 1) * sub)
            ds = _dot(dy_ref[rs, :], w_ref[...], _NT)
            (dp,), _ = _glu_bwd_fn(ds, a_ref[rs, :], b_ref[rs, :])
            o_ref[0, rs, :] = dp[:, :tn].astype(BF16)
            o_ref[1, rs, :] = dp[:, tn:].astype(BF16)

    tile = pl.BlockSpec((tm, tn), lambda i, j: (i, j))
    return pl.pallas_call(
        kern, name=name, grid=(m // tm, f // tn),
        in_specs=[pl.BlockSpec((tm, k), lambda i, j: (i, 0)), pl.BlockSpec((tn, k), lambda i, j: (j, 0)), tile, tile],
        out_specs=pl.BlockSpec((2, tm, tn), lambda i, j: (0, i, j)),
        out_shape=jax.ShapeDtypeStruct((2, m, f), BF16),
        compiler_params=_params("parallel", "parallel"),
    )(dy, wout, a, b)


def _mm_tn(a, b, *, name, tm=1024, tn=1024, tk=1088, col_blocks=None, stack=None):
    extra, extra_specs, aliases = [], [], {}
    halves = a.ndim == 3
    t, m = (a.shape[1], 2 * a.shape[2]) if halves else a.shape
    t2, n = b.shape
    assert t == t2
    tm, tn, tk = _pick(m, tm, 128), _pick(n, tn, 128), _pick(t, tk)
    nk = t // tk
    if halves:
        hb = m // 2 // tm
        a_spec = pl.BlockSpec((None, tk, tm), lambda i, j, kk: (i // hb, kk, i % hb))
    else:
        a_spec = pl.BlockSpec((tk, tm), lambda i, j, kk: (kk, i))
    if col_blocks is None:
        def kern(a_ref, b_ref, o_ref):
            kk = pl.program_id(2)

            @pl.when(kk == 0)
            def _():
                o_ref[...] = jnp.zeros_like(o_ref)

            o_ref[...] += _dot(a_ref[...], b_ref[...], _TN)

        out_spec = pl.BlockSpec((tm, tn), lambda i, j, kk: (i, j))
        out_shape = jax.ShapeDtypeStruct((m, n), F32)
        scratch = []
    else:
        wb = n // col_blocks
        per = tn // wb
        assert tn % wb == 0 and wb % 8 == 0

        def kern(a_ref, b_ref, *rest):
            o_ref, acc_ref = rest[-2:]
            kk = pl.program_id(2)
            p = _dot(a_ref[...], b_ref[...], _TN)

            @pl.when(kk == 0)
            def _():
                acc_ref[...] = p

            @pl.when((kk > 0) & (kk < nk - 1))
            def _():
                acc_ref[...] += p

            @pl.when(kk == nk - 1)
            def _():
                r = acc_ref[...] + p if nk > 1 else p
                for c in range(per):
                    o_ref[c] = r[:, c * wb:(c + 1) * wb].astype(BF16)

        rows_total, row0, into = stack if stack is not None else (m, 0, None)
        assert row0 % tm == 0
        out_spec = pl.BlockSpec((per, tm, wb), lambda i, j, kk: (j, i + row0 // tm, 0))
        out_shape = jax.ShapeDtypeStruct((col_blocks, rows_total, wb), BF16)
        scratch = [pltpu.VMEM((tm, tn), F32)]
        if into is not None:
            extra, extra_specs, aliases = [into], [pl.BlockSpec(memory_space=pl.ANY)], {2: 0}

    return pl.pallas_call(
        kern, name=name, grid=(m // tm, n // tn, nk),
        in_specs=[a_spec, pl.BlockSpec((tk, tn), lambda i, j, kk: (kk, j))] + extra_specs,
        out_specs=out_spec, out_shape=out_shape, scratch_shapes=scratch, input_output_aliases=aliases,
        compiler_params=_params("parallel", "parallel", "arbitrary"),
    )(a, b, *extra)


def _mm_f32(a, b, *, name, silu_a=False, bias=None):
    m, k = a.shape
    n = b.shape[1]

    def kern(*refs):
        if bias is None:
            a_ref, b_ref, o_ref = refs
        else:
            a_ref, b_ref, bias_ref, o_ref = refs
        av = a_ref[...]
        if silu_a:
            av = _silu(av)
        r = jnp.dot(av, b_ref[...], preferred_element_type=F32, precision=HI)
        if bias is not None:
            r = r + bias_ref[...]
        o_ref[...] = r

    args = [a, b] + ([] if bias is None else [bias])
    return pl.pallas_call(kern, name=name, out_shape=jax.ShapeDtypeStruct((m, n), F32),
                          compiler_params=pltpu.CompilerParams(vmem_limit_bytes=VMEM_LIMIT_BYTES))(*args)


CONV_WIN = 32


def _conv_windows(n, n_ctx):
    assert n_ctx % CONV_WIN == 0 and n_ctx >= CONV_WIN and n - n_ctx >= CONV_WIN
    return (0, n_ctx - CONV_WIN // 2, n - CONV_WIN)


def _tap_outside(r0, s, n, n_ctx):
    t = r0 + lax.broadcasted_iota(jnp.int32, (CONV_WIN, 1), 0)
    lo = jnp.where(t < n_ctx, 0, n_ctx)
    hi = jnp.where(t < n_ctx, n_ctx, n)
    return jnp.where((t + s >= lo) & (t + s < hi), 0.0, 1.0)


def _rolled(v, s):
    return v if s == 0 else pltpu.roll(v, (-s) % v.shape[0], 0)


def _conv_fwd(xp, w8, b, *, n_ctx, name, cb=256):
    n, c = xp.shape
    half = SSD_CONV // 2

    def kern(x_ref, w_ref, b_ref, cpre_ref, act_ref):
        x = x_ref[...]
        acc = jnp.zeros_like(x) + b_ref[...]
        rolled = {}
        for k in range(SSD_CONV):
            rolled[k] = _rolled(x, k - half)
            acc = acc + rolled[k] * w_ref[k:k + 1, :]
        cpre_ref[...] = acc
        act_ref[...] = _silu(acc)
        for r0 in _conv_windows(n, n_ctx):
            rows = slice(r0, r0 + CONV_WIN)
            fix = acc[rows]
            for k in range(SSD_CONV):
                if k != half:
                    fix = fix - rolled[k][rows] * w_ref[k:k + 1, :] * _tap_outside(r0, k - half, n, n_ctx)
            cpre_ref[rows, :] = fix
            act_ref[rows, :] = _silu(fix)

    spec = pl.BlockSpec((n, cb), lambda j: (0, j))
    return pl.pallas_call(
        kern, name=name, grid=(c // cb,),
        in_specs=[spec, pl.BlockSpec((8, cb), lambda j: (0, j)), pl.BlockSpec((1, cb), lambda j: (0, j))],
        out_specs=[spec, spec], out_shape=[jax.ShapeDtypeStruct((n, c), F32)] * 2,
        compiler_params=_params("parallel"),
    )(xp, w8, b)


def _conv_bwd(d1, d2, cpre, xp, w8, *, n_ctx, name, cb=128):
    n, c = xp.shape
    half = SSD_CONV // 2

    def kern(d1_ref, d2_ref, cpre_ref, x_ref, w_ref, dx_ref, dw_ref, db_ref):
        g = (d1_ref[...] + d2_ref[...]) * _dsilu(cpre_ref[...])
        x = x_ref[...]
        dx = jnp.zeros_like(g)
        dw_ref[...] = jnp.zeros_like(dw_ref)
        g_rolled = {}
        for k in range(SSD_CONV):
            s = k - half
            g_rolled[k] = _rolled(g, -s)
            dx = dx + g_rolled[k] * w_ref[k:k + 1, :]
            xr = _rolled(x, s)
            dw = _sum0(g * xr)
            if s != 0:
                for r0 in _conv_windows(n, n_ctx):
                    rows = slice(r0, r0 + CONV_WIN)
                    dw = dw - _sum0(g[rows] * xr[rows] * _tap_outside(r0, s, n, n_ctx))
            dw_ref[k:k + 1, :] = dw
        dx_ref[...] = dx.astype(BF16)
        for r0 in _conv_windows(n, n_ctx):
            rows = slice(r0, r0 + CONV_WIN)
            fix = dx[rows]
            for k in range(SSD_CONV):
                if k != half:
                    fix = fix - g_rolled[k][rows] * w_ref[k:k + 1, :] * _tap_outside(r0, half - k, n, n_ctx)
            dx_ref[rows, :] = fix.astype(BF16)
        db_ref[...] = _sum0(g)

    spec = pl.BlockSpec((n, cb), lambda j: (0, j))
    return pl.pallas_call(
        kern, name=name, grid=(c // cb,),
        in_specs=[spec, spec, spec, spec, pl.BlockSpec((8, cb), lambda j: (0, j))],
        out_specs=[spec, pl.BlockSpec((8, cb), lambda j: (0, j)), pl.BlockSpec((1, cb), lambda j: (0, j))],
        out_shape=[jax.ShapeDtypeStruct((n, c), BF16), jax.ShapeDtypeStruct((8, c), F32),
                   jax.ShapeDtypeStruct((1, c), F32)],
        compiler_params=_params("parallel"),
    )(d1, d2, cpre, xp, w8)


def _chunk_of(s, nc, n_ctx_chunks, rev):
    if not rev:
        return s
    return jnp.where(s < n_ctx_chunks, n_ctx_chunks - 1 - s, nc - 1 - (s - n_ctx_chunks))


def _scan_common(dt_raw, dtT_raw, bias_r, bias_c, alog_r, alog_c, rev):
    ii = lax.broadcasted_iota(jnp.int32, (CHUNK, CHUNK), 0)
    jj = lax.broadcasted_iota(jnp.int32, (CHUNK, CHUNK), 1)
    tri = (jj >= ii) if rev else (jj <= ii)
    tri_t = (ii >= jj) if rev else (ii <= jj)
    a_r = -jnp.exp(alog_r)
    a_c = -jnp.exp(alog_c)
    dt = _softplus(dt_raw + bias_r)
    dt_t = _softplus(dtT_raw + bias_c)
    al = dt * a_r
    acum = _dot(tri.astype(F32), al, precision=HI)
    acum_t = _dot(dt_t * a_c, tri_t.astype(F32), precision=HI)
    atot = _sum0(al)
    return tri, tri_t, a_r, dt, acum, acum_t, atot


def _head_spread():
    return jnp.repeat(jnp.eye(SSD_HEADS, dtype=BF16), SSD_HEAD_DIM, axis=1)


def _dot_sel(v, sel):
    hi = v.astype(BF16)
    lo = (v - hi.astype(F32)).astype(BF16)
    return _dot(hi, sel) + _dot(lo, sel)


def _ssd_scan_fwd(xbc, dt_raw, dtT_raw, bias_r, bias_c, alog_r, alog_c, *, rev, n_ctx_chunks, name):
    n = xbc.shape[0]
    nc = n // CHUNK
    cidx = functools.partial(_chunk_of, nc=nc, n_ctx_chunks=n_ctx_chunks, rev=rev)

    def kern(xs_ref, b_ref, c_ref, dt_ref, dtT_ref, br_ref, bc_ref, ar_ref, ac_ref, e_ref, y_ref, hs_ref, h_scr):
        @pl.when(pl.program_id(0) == 0)
        def _():
            h_scr[...] = jnp.zeros_like(h_scr)

        tri, _, _, dt, acum, acum_t, atot = _scan_common(
            dt_ref[...], dtT_ref[...], br_ref[...], bc_ref[...], ar_ref[...], ac_ref[...], rev)
        etot = jnp.exp(atot)
        spread = lambda v: _dot_sel(v, e_ref[...])
        xdt_all = xs_ref[...] * spread(dt)
        eax = spread(jnp.exp(acum))
        xdw_all = xdt_all * spread(jnp.exp(atot - acum))
        hs_ref[...] = h_scr[...]
        for g in range(SSD_GROUPS):
            gs = slice(g * 256, (g + 1) * 256)
            bg = b_ref[:, g * SSD_STATE:(g + 1) * SSD_STATE].astype(BF16)
            cg = c_ref[:, g * SSD_STATE:(g + 1) * SSD_STATE].astype(BF16)
            cb = _dot(cg, bg, _NT)
            h4 = h_scr[gs, :]
            ys = []
            for k in range(SSD_HPG):
                h = g * SSD_HPG + k
                lmat = jnp.exp(jnp.where(tri, acum[:, h:h + 1] - acum_t[h:h + 1, :], NEG_BIG))
                xdt_h = xdt_all[:, h * SSD_HEAD_DIM:(h + 1) * SSD_HEAD_DIM].astype(BF16)
                ys.append(_dot((cb * lmat).astype(BF16), xdt_h))
            y_ref[:, gs] = jnp.concatenate(ys, axis=1) + _dot(cg, h4.astype(BF16), _NT) * eax[:, gs]
            s4 = _dot(xdw_all[:, gs].astype(BF16), bg, _TN)
            for k in range(SSD_HPG):
                h = g * SSD_HPG + k
                rs = slice(h * SSD_HEAD_DIM, (h + 1) * SSD_HEAD_DIM)
                h_scr[rs, :] = h4[k * SSD_HEAD_DIM:(k + 1) * SSD_HEAD_DIM] * etot[:, h:h + 1] + \
                    s4[k * SSD_HEAD_DIM:(k + 1) * SSD_HEAD_DIM]

    nh = SSD_HEADS
    small = lambda shape: pl.BlockSpec(shape, lambda s: (0, 0))
    return pl.pallas_call(
        kern, name=name, grid=(nc,),
        in_specs=[pl.BlockSpec((CHUNK, SSD_INNER), lambda s: (cidx(s), 0)),
                  pl.BlockSpec((CHUNK, 1024), lambda s: (cidx(s), 2)),
                  pl.BlockSpec((CHUNK, 1024), lambda s: (cidx(s), 3)),
                  pl.BlockSpec((CHUNK, nh), lambda s: (cidx(s), 0)),
                  pl.BlockSpec((nh, CHUNK), lambda s: (0, cidx(s))),
                  small((1, nh)), small((nh, 1)), small((1, nh)), small((nh, 1)), small((nh, SSD_INNER))],
        out_specs=[pl.BlockSpec((CHUNK, SSD_INNER), lambda s: (cidx(s), 0)),
                   pl.BlockSpec((None, SSD_INNER, SSD_STATE), lambda s: (s, 0, 0))],
        out_shape=[jax.ShapeDtypeStruct((n, SSD_INNER), F32),
                   jax.ShapeDtypeStruct((nc, SSD_INNER, SSD_STATE), F32)],
        scratch_shapes=[pltpu.VMEM((SSD_INNER, SSD_STATE), F32)],
        compiler_params=_params("arbitrary"),
    )(xbc, xbc, xbc, dt_raw, dtT_raw, bias_r, bias_c, alog_r, alog_c, _head_spread())


def _ssd_scan_bwd(dy, xbc, hs, dt_raw, dtT_raw, bias_r, bias_c, alog_r, alog_c, dvec, *, rev, n_ctx_chunks,
                  direct, name):
    n = xbc.shape[0]
    nc = n // CHUNK
    nh = SSD_HEADS
    step_of = lambda r: nc - 1 - r
    cidx = lambda r: _chunk_of(step_of(r), nc, n_ctx_chunks, rev)

    def kern(dy_ref, xs_ref, b_ref, c_ref, hs_ref, dt_ref, dtT_ref, br_ref, bc_ref, ar_ref, ac_ref, dv_ref,
             e_ref, et_ref, dx_ref, ddt_ref, dal_ref, dbias_ref, dh_scr):
        @pl.when(pl.program_id(0) == 0)
        def _():
            dh_scr[...] = jnp.zeros_like(dh_scr)
            dal_ref[...] = jnp.zeros_like(dal_ref)
            dbias_ref[...] = jnp.zeros_like(dbias_ref)

        tri, tri_t, a_r, dt, acum, acum_t, atot = _scan_common(
            dt_ref[...], dtT_ref[...], br_ref[...], bc_ref[...], ar_ref[...], ac_ref[...], rev)
        etot = jnp.exp(atot)
        spread = lambda v: _dot_sel(v, e_ref[...])
        gather = lambda v: _dot_sel(v, et_ref[...])
        xs_all = xs_ref[...]
        dy_all = dy_ref[...]
        dtx = spread(dt)
        eax = spread(jnp.exp(acum))
        decx = spread(jnp.exp(atot - acum))
        xdt_all = xs_all * dtx
        xdw_all = xdt_all * decx
        dyo_all = dy_all * eax
        lane = lax.broadcasted_iota(jnp.int32, (CHUNK, nh), 1)
        lane1 = lax.broadcasted_iota(jnp.int32, (1, nh), 1)
        sub = lax.broadcasted_iota(jnp.int32, (nh, CHUNK), 0)
        g_rows = jnp.zeros((CHUNK, nh), F32)
        g_cols = jnp.zeros((nh, CHUNK), F32)
        dtot = jnp.zeros((1, nh), F32)
        q_col, q_e, q_dt = [], [], []
        for g in range(SSD_GROUPS):
            gs = slice(g * 256, (g + 1) * 256)
            bg = b_ref[:, g * SSD_STATE:(g + 1) * SSD_STATE].astype(BF16)
            cg = c_ref[:, g * SSD_STATE:(g + 1) * SSD_STATE].astype(BF16)
            cb = _dot(cg, bg, _NT)
            hs4 = hs_ref[gs, :]
            dh4 = dh_scr[gs, :]
            hs4_bf = hs4.astype(BF16)
            dh4_bf = dh4.astype(BF16)
            dy4 = dy_all[:, gs]
            dy4_bf = dy4.astype(BF16)
            xdt4_bf = xdt_all[:, gs].astype(BF16)
            xdw4 = xdw_all[:, gs]
            xdw4_bf = xdw4.astype(BF16)
            dyo4_bf = dyo_all[:, gs].astype(BF16)
            yoff4 = _dot(cg, hs4_bf, _NT) * eax[:, gs]
            dcg = _dot(dyo4_bf, hs4_bf)
            dh_new4 = _dot(dyo4_bf, cg, _TN)
            bdh4 = _dot(bg, dh4_bf, _NT)
            dbg = _dot(xdw4_bf, dh4_bf)
            e4 = xdw4 * bdh4
            q_col.append(dy4 * yoff4 - e4)
            q_e.append(e4)
            hsum = jnp.sum(dh4 * hs4, axis=1, keepdims=True)
            dcb = jnp.zeros((CHUNK, CHUNK), F32)
            dxdts = []
            for k in range(SSD_HPG):
                h = g * SSD_HPG + k
                ks = slice(k * SSD_HEAD_DIM, (k + 1) * SSD_HEAD_DIM)
                lmat = jnp.exp(jnp.where(tri, acum[:, h:h + 1] - acum_t[h:h + 1, :], NEG_BIG))
                mf = cb * lmat
                dm = _dot(dy4_bf[:, ks], xdt4_bf[:, ks], _NT)
                dcb = dcb + dm * lmat
                gmat = dm * mf
                g_rows = g_rows + jnp.where(lane == h, jnp.sum(gmat, axis=1, keepdims=True), 0.0)
                g_cols = g_cols + jnp.where(sub == h, _sum0(gmat), 0.0)
                dxdts.append(_dot(mf.astype(BF16), dy4_bf[:, ks], _TN))
                et = etot[:, h:h + 1]
                dtot = dtot + jnp.where(lane1 == h, _sum0(hsum[ks]) * et, 0.0)
                dh_scr[h * SSD_HEAD_DIM:(h + 1) * SSD_HEAD_DIM, :] = dh4[ks] * et + dh_new4[ks]
            dxdt4 = jnp.concatenate(dxdts, axis=1) + bdh4 * decx[:, gs]
            q_dt.append(dxdt4 * xs_all[:, gs])
            dx4 = dxdt4 * dtx[:, gs]
            if direct:
                dx4 = dx4 + dy4 * dv_ref[:, gs]
            dcb_bf = dcb.astype(BF16)
            dx_ref[:, gs] = dx4
            dx_ref[:, SSD_INNER + g * SSD_STATE:SSD_INNER + (g + 1) * SSD_STATE] = dbg + _dot(dcb_bf, cg, _TN)
            dx_ref[:, SSD_INNER + 1024 + g * SSD_STATE:SSD_INNER + 1024 + (g + 1) * SSD_STATE] = \
                dcg + _dot(dcb_bf, bg)
        e_heads = gather(jnp.concatenate(q_e, axis=1))
        dacum = gather(jnp.concatenate(q_col, axis=1)) + g_rows - g_cols.T
        dal = _dot(tri_t.astype(F32), dacum, precision=HI) + dtot + _sum0(e_heads)
        ddt = gather(jnp.concatenate(q_dt, axis=1)) + dal * a_r
        ddt_raw = ddt * _sig(dt_ref[...] + br_ref[...])
        ddt_ref[...] = ddt_raw
        dal_ref[...] += _sum0(dal * dt) * a_r
        dbias_ref[...] += _sum0(ddt_raw)

    small = lambda shape: pl.BlockSpec(shape, lambda r: (0, 0))
    return pl.pallas_call(
        kern, name=name, grid=(nc,),
        in_specs=[pl.BlockSpec((CHUNK, SSD_INNER), lambda r: (cidx(r), 0)),
                  pl.BlockSpec((CHUNK, SSD_INNER), lambda r: (cidx(r), 0)),
                  pl.BlockSpec((CHUNK, 1024), lambda r: (cidx(r), 2)),
                  pl.BlockSpec((CHUNK, 1024), lambda r: (cidx(r), 3)),
                  pl.BlockSpec((None, SSD_INNER, SSD_STATE), lambda r: (step_of(r), 0, 0)),
                  pl.BlockSpec((CHUNK, nh), lambda r: (cidx(r), 0)),
                  pl.BlockSpec((nh, CHUNK), lambda r: (0, cidx(r))),
                  small((1, nh)), small((nh, 1)), small((1, nh)), small((nh, 1)), small((1, SSD_INNER)),
                  small((nh, SSD_INNER)), small((SSD_INNER, nh))],
        out_specs=[pl.BlockSpec((CHUNK, SSD_CONV_DIM), lambda r: (cidx(r), 0)),
                   pl.BlockSpec((CHUNK, nh), lambda r: (cidx(r), 0)),
                   small((1, nh)), small((1, nh))],
        out_shape=[jax.ShapeDtypeStruct((n, SSD_CONV_DIM), F32), jax.ShapeDtypeStruct((n, nh), F32),
                   jax.ShapeDtypeStruct((1, nh), F32), jax.ShapeDtypeStruct((1, nh), F32)],
        scratch_shapes=[pltpu.VMEM((SSD_INNER, SSD_STATE), F32)],
        compiler_params=_params("arbitrary"),
    )(dy, xbc, xbc, xbc, hs, dt_raw, dtT_raw, bias_r, bias_c, alog_r, alog_c, dvec, _head_spread(),
      _head_spread().T)


def _gm_spatial_fwd(gu, gvn, ws, bst, *, name):
    n = gu.shape[0]

    def kern(gu_ref, gv_ref, ws_ref, bs_ref, o_ref):
        for g in range(GM_GROUPS):
            sl = slice(g * GM_GROUP_DIM, (g + 1) * GM_GROUP_DIM)
            s = _dot(ws_ref[g], gv_ref[:, sl]) + bs_ref[:, g:g + 1]
            o_ref[:, sl] = (gu_ref[:, sl] * s).astype(BF16)

    spec = pl.BlockSpec((CHUNK, GM_INNER), lambda i: (i, 0))
    return pl.pallas_call(
        kern, name=name, grid=(n // CHUNK,),
        in_specs=[spec, spec, pl.BlockSpec(ws.shape, lambda i: (0, 0, 0)), pl.BlockSpec(bst.shape, lambda i: (0, 0))],
        out_specs=spec, out_shape=jax.ShapeDtypeStruct((n, GM_INNER), BF16),
        compiler_params=_params("parallel"),
    )(gu, gvn, ws, bst)


def _gm_spatial_bwd(dt, gu, gvn, ws, wst, bst, *, name):
    n = gu.shape[0]

    def kern(dt_ref, gu_ref, gv_ref, ws_ref, wst_ref, bs_ref, dgu_ref, dgv_ref, dws_ref, dbs_ref):
        @pl.when(pl.program_id(0) == 0)
        def _():
            dws_ref[...] = jnp.zeros_like(dws_ref)
            dbs_ref[...] = jnp.zeros_like(dbs_ref)

        lane = lax.broadcasted_iota(jnp.int32, (CHUNK, GM_GROUPS), 1)
        dbs = jnp.zeros((CHUNK, GM_GROUPS), F32)
        for g in range(GM_GROUPS):
            sl = slice(g * GM_GROUP_DIM, (g + 1) * GM_GROUP_DIM)
            gv = gv_ref[:, sl]
            s = _dot(ws_ref[g], gv) + bs_ref[:, g:g + 1]
            d = dt_ref[:, sl]
            dgu_ref[:, sl] = d * s
            ds = d * gu_ref[:, sl]
            ds_bf = ds.astype(BF16)
            dws_ref[g] += _dot(ds_bf, gv, _NT)
            dgv_ref[:, sl] = _dot(wst_ref[g], ds_bf)
            dbs = dbs + jnp.where(lane == g, jnp.sum(ds, axis=1, keepdims=True), 0.0)
        dbs_ref[...] += dbs

    spec = pl.BlockSpec((CHUNK, GM_INNER), lambda i: (i, 0))
    wspec = pl.BlockSpec(ws.shape, lambda i: (0, 0, 0))
    bspec = pl.BlockSpec(bst.shape, lambda i: (0, 0))
    return pl.pallas_call(
        kern, name=name, grid=(n // CHUNK,),
        in_specs=[spec, spec, spec, wspec, wspec, bspec],
        out_specs=[spec, spec, wspec, bspec],
        out_shape=[jax.ShapeDtypeStruct((n, GM_INNER), F32), jax.ShapeDtypeStruct((n, GM_INNER), F32),
                   jax.ShapeDtypeStruct(ws.shape, F32), jax.ShapeDtypeStruct(bst.shape, F32)],
        compiler_params=_params("arbitrary"),
    )(dt, gu, gvn, ws, wst, bst)


def _adamw(parts, w, m, v, *, name, tm=256, sel=(), into=None):
    ns, r, wd = parts.shape
    tm = _pick(r, tm, 8)
    tc = wd
    if tm < 64 and wd % 256 == 0:
        tm, tc = r, 256
    lead = len(sel)
    assert w.shape[lead:] == (r, wd) and lead == w.ndim - 2

    def kern(*refs):
        p_ref, w_ref, m_ref, v_ref = refs[:4]
        g_ref, d_ref, nm_ref, nv_ref = refs[-4:]
        g = p_ref[0].astype(F32)
        for s in range(1, ns):
            g = g + p_ref[s].astype(F32)
        m2 = ADAM_B1 * m_ref[...] + (1.0 - ADAM_B1) * g
        v2 = ADAM_B2 * v_ref[...] + (1.0 - ADAM_B2) * (g * g)
        m_hat = m2 / (1.0 - ADAM_B1 ** ADAM_STEP)
        v_hat = v2 / (1.0 - ADAM_B2 ** ADAM_STEP)
        g_ref[...] = g
        d_ref[...] = -ADAM_LR * (m_hat / (jnp.sqrt(v_hat) + ADAM_EPS) + ADAM_WD * w_ref[...])
        nm_ref[...] = m2
        nv_ref[...] = v2

    spec = pl.BlockSpec((None,) * lead + (tm, tc), lambda i, j: tuple(sel) + (i, j))
    extra, aliases = [], {}
    if into is not None:
        extra = list(into)
        aliases = {4 + k: k for k in range(4)}
    return pl.pallas_call(
        kern, name=name, grid=(r // tm, wd // tc),
        in_specs=[pl.BlockSpec((ns, tm, tc), lambda i, j: (0, i, j)), spec, spec, spec] +
                 [pl.BlockSpec(memory_space=pl.ANY)] * len(extra),
        out_specs=[spec] * 4, out_shape=[jax.ShapeDtypeStruct(w.shape, F32)] * 4,
        input_output_aliases=aliases,
        compiler_params=_params("parallel", "parallel"),
    )(parts, w, m, v, *extra)


def _sum_slots(parts, *, name, scale_by=None):
    ns, r, wd = parts.shape

    def kern(*refs):
        p_ref, o_ref = refs[0], refs[-1]
        g = p_ref[0]
        for s in range(1, ns):
            g = g + p_ref[s]
        if scale_by is not None:
            g = g * _dsilu(refs[1][...])
        o_ref[...] = g

    args = [parts] + ([] if scale_by is None else [scale_by])
    return pl.pallas_call(kern, name=name, out_shape=jax.ShapeDtypeStruct((r, wd), F32),
                          compiler_params=pltpu.CompilerParams(vmem_limit_bytes=VMEM_LIMIT_BYTES))(*args)


def _mesh_pos():
    x, y, c = lax.axis_index("x"), lax.axis_index("y"), lax.axis_index("c")
    return x, y, c, 4 * x + 2 * y + c


def _flip(x, y, c, f):
    fx, fy, fc = (f >> 2) & 1, (f >> 1) & 1, f & 1
    px = 1 - x if fx else x
    py = 1 - y if fy else y
    pc = 1 - c if fc else c
    return (px, py, pc), 4 * px + 2 * py + pc


_HBM_SPEC = pl.BlockSpec(memory_space=pltpu.HBM)


def _exchange(arrays, *, scatter, name):
    na = len(arrays)
    if scatter:
        out_shape = [jax.ShapeDtypeStruct(a.shape, a.dtype) for a in arrays]
    else:
        out_shape = [jax.ShapeDtypeStruct((NDEV,) + a.shape, a.dtype) for a in arrays]

    out_shape.append(jax.ShapeDtypeStruct((8, 128), F32))

    def body(*refs):
        ins, outs = refs[:na], refs[na:2 * na]
        send_sems, recv_sems, local_sems = refs[2 * na + 1:]
        refs[2 * na][...] = jnp.zeros((8, 128), F32)
        x, y, c, me = _mesh_pos()
        copies = []
        for i in range(na):
            src_own = ins[i].at[me] if scatter else ins[i]
            lc = pltpu.make_async_copy(src_own, outs[i].at[me], local_sems.at[i])
            lc.start()
            copies.append(lc)
        sends = []
        for f in range(1, NDEV):
            peer, pidx = _flip(x, y, c, f)
            for i in range(na):
                k = i * (NDEV - 1) + f - 1
                src = ins[i].at[pidx] if scatter else ins[i]
                cp = pltpu.make_async_remote_copy(
                    src_ref=src, dst_ref=outs[i].at[me], send_sem=send_sems.at[k], recv_sem=recv_sems.at[k],
                    device_id=peer, device_id_type=pl.DeviceIdType.MESH)
                cp.start()
                sends.append(cp)
        for f in range(1, NDEV):
            peer, pidx = _flip(x, y, c, f)
            for i in range(na):
                k = i * (NDEV - 1) + f - 1
                src = ins[i].at[pidx] if scatter else ins[i]
                pltpu.make_async_remote_copy(
                    src_ref=src, dst_ref=outs[i].at[pidx], send_sem=send_sems.at[k], recv_sem=recv_sems.at[k],
                    device_id=peer, device_id_type=pl.DeviceIdType.MESH).wait_recv()
        for cp in sends:
            cp.wait_send()
        for lc in copies:
            lc.wait()

    res = pl.pallas_call(
        body, name=name, out_shape=out_shape, in_specs=[_HBM_SPEC] * na,
        out_specs=[_HBM_SPEC] * na + [pl.BlockSpec(memory_space=pltpu.VMEM)],
        scratch_shapes=[pltpu.SemaphoreType.DMA((na * (NDEV - 1),)), pltpu.SemaphoreType.DMA((na * (NDEV - 1),)),
                        pltpu.SemaphoreType.DMA((na,))],
        compiler_params=pltpu.CompilerParams(has_side_effects=True),
    )(*arrays)
    return res[:na], res[na][0, 0]


_SEM_SPEC = pl.BlockSpec(memory_space=pltpu.SEMAPHORE)
_DATAFLOW = pltpu.SideEffectType.DATAFLOW_SIDE_EFFECTING


def _split_copies(srcs, lands, send_sems, recv_sems, scatter, arriving):
    x, y, c, me = _mesh_pos()
    copies = []
    for i in range(len(srcs)):
        for f in range(1, NDEV):
            peer, pidx = _flip(x, y, c, f)
            k = i * (NDEV - 1) + f - 1
            copies.append(pltpu.make_async_remote_copy(
                src_ref=srcs[i].at[pidx] if scatter else srcs[i], dst_ref=lands[i].at[pidx if arriving else me],
                send_sem=send_sems.at[k], recv_sem=recv_sems.at[k], device_id=peer,
                device_id_type=pl.DeviceIdType.MESH))
    return copies


def _exchange_start(srcs, lands, *, scatter, name):
    na = len(srcs)
    nsem = na * (NDEV - 1)

    def body(*refs):
        ins_src, ins_land = refs[:na], refs[na:2 * na]
        send_sems, recv_sems = refs[2 * na], refs[2 * na + 1]
        token = refs[-1]
        for cp in _split_copies(ins_src, ins_land, send_sems, recv_sems, scatter, False):
            cp.start()
        token[...] = jnp.zeros_like(token)

    thru = [pltpu.HBM(a.shape, a.dtype) for a in list(srcs) + list(lands)]
    res = pl.pallas_call(
        body, name=name,
        out_shape=(pltpu.SemaphoreType.DMA((nsem,)), pltpu.SemaphoreType.DMA((nsem,)), *thru,
                   jax.ShapeDtypeStruct((8, 128), F32)),
        in_specs=[_HBM_SPEC] * (2 * na),
        out_specs=(_SEM_SPEC, _SEM_SPEC, *([_HBM_SPEC] * (2 * na)), pl.BlockSpec(memory_space=pltpu.VMEM)),
        input_output_aliases={i: 2 + i for i in range(2 * na)},
        compiler_params=pltpu.CompilerParams(has_side_effects=_DATAFLOW),
    )(*[pltpu.with_memory_space_constraint(a, pltpu.HBM) for a in list(srcs) + list(lands)])
    send_sems, recv_sems = res[0], res[1]
    return send_sems, recv_sems, res[2:2 + na], res[2 + na:2 + 2 * na], res[-1][0, 0]


def _exchange_wait(send_sems, recv_sems, srcs, lands, after, *, scatter, name):
    na = len(srcs)

    def body(*refs):
        ins_src, ins_land = refs[:na], refs[na:2 * na]
        s_sems, r_sems = refs[2 * na], refs[2 * na + 1]
        for cp in _split_copies(ins_src, ins_land, s_sems, r_sems, scatter, False):
            cp.wait_send()
        for cp in _split_copies(ins_src, ins_land, s_sems, r_sems, scatter, True):
            cp.wait_recv()

    thru = [pltpu.HBM(a.shape, a.dtype) for a in list(srcs) + list(lands)]
    res = pl.pallas_call(
        body, name=name, out_shape=tuple(thru),
        in_specs=[_HBM_SPEC] * (2 * na) + [_SEM_SPEC, _SEM_SPEC, pl.BlockSpec(memory_space=pl.ANY)],
        out_specs=tuple([_HBM_SPEC] * (2 * na)),
        input_output_aliases={i: i for i in range(2 * na)},
        compiler_params=pltpu.CompilerParams(has_side_effects=_DATAFLOW),
    )(*srcs, *lands, send_sems, recv_sems, after)
    return res[na:]


def _landing(block, me):
    buf = lax.empty((NDEV,) + block.shape, block.dtype)
    return lax.dynamic_update_slice_in_dim(buf, block[None], me, axis=0)


def _seg_kw(nseg, n_ctx, tm):
    return dict(nseg=nseg, seg_blocks=(n_ctx // tm if nseg == 2 else 0))


def _ffn_fwd(tag, h, gpre, gpost, shift, scale, gate, w, *, nseg, n_ctx, tm):
    n = h.shape[0]
    kw = _seg_kw(nseg, n_ctx, tm)
    (u,) = _rowwise(tag + "_pre", _pre_fwd_fn, n, [h], [("full", gpre), ("seg", shift), ("seg", scale)],
                    [(D_MODEL, BF16)], tm=tm, **kw)
    if "early" in w:
        w.update(w.pop("early")(u))
    s, a, b = _mm_glu(u, w["win_t"], name=tag + "_glu")
    if "late" in w:
        w.update(w.pop("late")(s))
    y, ho = _mm_rows(s, w["wout"], functools.partial(_out_post_fn, 0.5), [h], [("full", gpost), ("seg", gate)],
                     [(D_MODEL, F32), (D_MODEL, F32)], name=tag + "_out", tk=FFN_DIM, n_ctx=n_ctx)
    return ho, dict(h=h, u=u, s=s, a=a, b=b, y=y)


def _ffn_bwd(tag, dho, sv, gpre, gpost, scale, gate, w, put, *, nseg, n_ctx, tm):
    n = dho.shape[0]
    kw = _seg_kw(nseg, n_ctx, tm)
    dy, dgate, dgpost = _rowwise(tag + "_postb", functools.partial(_post_bwd_fn, 0.5), n, [dho, sv["y"]],
                                 [("full", gpost), ("seg", gate)], [(D_MODEL, BF16)], [D_MODEL, D_MODEL], tm=tm, **kw)
    tok = put("w_out", _mm_tn(sv["s"], dy, name=tag + "_dwout", tm=1408, tn=1024, col_blocks=1))
    dp = _mm_glu_bwd(dy, w["wout"], sv["a"], sv["b"], name=tag + "_ds")
    tok2 = put("w_in", _mm_tn(dp, sv["u"], name=tag + "_dwin", tm=1408, tn=1024, col_blocks=1))
    for t in (tok, tok2):
        if t is not None:
            gpre = gpre + t
    dh, dshift, dscale, dgpre = _mm_rows(dp, w["win_t"], _pre_bwd_fn, [sv["h"], dho],
                                         [("full", gpre), ("seg", scale)], [(D_MODEL, F32)],
                                         [D_MODEL, D_MODEL, D_MODEL], name=tag + "_du", n_ctx=n_ctx)
    return dh, None, dict(shift=dshift, scale=dscale, gate=dgate, gpre=dgpre, gpost=dgpost)


def _local_step(x, ctx, target, mods, norm_g, get_w, small, put_grad):
    t_len, n_ctx = x.shape[0], ctx.shape[0]
    n0 = t_len + n_ctx
    tm0 = _pick(n_ctx, 256, 8)
    tm1 = _pick(t_len, 256, 8)
    ncc = n_ctx // CHUNK
    g = {}

    def modrow(i, k, nseg):
        mc, mx = mods[i]
        if nseg == 2:
            return jnp.stack([mc[k], mx[k]])[:, None, :]
        return mx[k][None, None, :]

    pending = [None]

    def gvec(i, k):
        v = norm_g[i, k][None, :]
        if pending[0] is not None:
            v = v + pending[0]
            pending[0] = None
        return v

    xc = jnp.concatenate([ctx, x], axis=0)
    L0 = dict(nseg=2, n_ctx=n_ctx, tm=tm0)
    wts = dict(get_w("ffn00", xc))
    h1, sv_f01 = _ffn_fwd("l0f1", xc, gvec(0, 0), gvec(0, 1), modrow(0, 0, 2), modrow(0, 1, 2), modrow(0, 2, 2),
                          wts["ffn00"], **L0)
    kw0 = _seg_kw(2, n_ctx, tm0)
    (um0,) = _rowwise("l0m_pre", _pre_fwd_fn, n0, [h1], [("full", gvec(0, 2)), ("seg", modrow(0, 3, 2)),
                                                         ("seg", modrow(0, 4, 2))], [(D_MODEL, BF16)], tm=tm0, **kw0)
    wts.update(get_w("ssd", um0))
    win_ssd = wts["ssd_win_t"]
    nh = SSD_HEADS
    dt_blk = (SSD_INNER + SSD_CONV_DIM) // (2 * nh)
    z = _mm(um0, win_ssd, out_dtype=F32, name="ssd_z", rhs_t=True, n=SSD_INNER)
    xbc_pre = _mm(um0, win_ssd, out_dtype=F32, name="ssd_xbc", rhs_t=True, n=SSD_CONV_DIM,
                  b_off=(SSD_INNER // 1024, 0))
    dtr = _mm(um0, win_ssd, out_dtype=F32, name="ssd_dt", rhs_t=True, n=2 * nh, b_off=(dt_blk, 0))
    cpre, xbc = _conv_fwd(xbc_pre, small["conv_w8"], small["conv_b"], n_ctx=n_ctx, name="ssd_conv")
    nh = SSD_HEADS
    dt_dir = [dtr[:, :nh], dtr[:, nh:2 * nh]]
    dtT_dir = [d.T for d in dt_dir]
    bias_r = [small["dt_bias"][d][None, :] for d in range(2)]
    bias_c = [small["dt_bias"][d][:, None] for d in range(2)]
    alog_r = [small["a_log"][d][None, :] for d in range(2)]
    alog_c = [small["a_log"][d][:, None] for d in range(2)]
    ys, hss = [], []
    for d in range(2):
        yd, hsd = _ssd_scan_fwd(xbc, dt_dir[d], dtT_dir[d], bias_r[d], bias_c[d], alog_r[d], alog_c[d],
                                rev=(d == 1), n_ctx_chunks=ncc, name=f"ssd_scan{d}")
        ys.append(yd)
        hss.append(hsd)
    dvec = jnp.repeat(small["ssd_d"], SSD_HEAD_DIM)[None, :]
    ngv = small["ssd_norm_g"][None, :]
    gate_rows = [ys[0], ys[1], (xbc, SSD_INNER, 0, 0), z]
    lat = lambda r: (r[0], r[1], r[2], ncc) if isinstance(r, tuple) else (r, r.shape[1], 0, ncc)
    (yn,) = _rowwise("ssd_gate", _ssdgate_fwd_fn, t_len, [lat(r) for r in gate_rows],
                     [("full", dvec), ("full", ngv)], [(SSD_INNER, BF16)], tm=CHUNK)
    h1x = h1[n_ctx:]
    L1 = dict(nseg=1, n_ctx=0, tm=tm1)
    if "late" in wts:
        wts.update(wts.pop("late")(yn))
    yo0, h2 = _mm_rows(yn, wts["ssd_wout"], functools.partial(_out_post_fn, 1.0), [h1x],
                       [("full", gvec(0, 3)), ("seg", modrow(0, 5, 1))], [(D_MODEL, F32), (D_MODEL, F32)],
                       name="ssd_out", tk=SSD_INNER)
    wts.update(get_w("ffn01", h2))
    h3, sv_f02 = _ffn_fwd("l0f2", h2, gvec(0, 4), gvec(0, 5), modrow(0, 6, 1), modrow(0, 7, 1), modrow(0, 8, 1),
                          wts["ffn01"], **L1)

    wts.update(get_w("ffn10", h3))
    h4, sv_f11 = _ffn_fwd("l1f1", h3, gvec(1, 0), gvec(1, 1), modrow(1, 0, 1), modrow(1, 1, 1), modrow(1, 2, 1),
                          wts["ffn10"], **L1)
    (um1,) = _rowwise("l1m_pre", _pre_fwd_fn, t_len, [h4], [("full", gvec(1, 2)), ("seg", modrow(1, 3, 1)),
                                                            ("seg", modrow(1, 4, 1))], [(D_MODEL, BF16)], tm=tm1)
    wts.update(get_w("gm", um1))
    p1 = _mm(um1, wts["gm_win"], out_dtype=F32, name="gm_in")
    vg = small["gm_v_g"][None, :]
    vb = small["gm_v_b"][None, :]
    gu, gvn = _rowwise("gm_act", _gm_act_fwd_fn, t_len, [p1], [("full", vg), ("full", vb)],
                       [(GM_INNER, F32), (GM_INNER, BF16)], tm=128)
    ws_bf = small["gm_w_s"].astype(BF16)
    wst_bf = jnp.swapaxes(small["gm_w_s"], 1, 2).astype(BF16)
    bst = small["gm_b_s"].T
    tgm = _gm_spatial_fwd(gu, gvn, ws_bf, bst, name="gm_spatial")
    yo1, h5 = _mm_rows(tgm, wts["gm_wout"], functools.partial(_out_post_fn, 1.0), [h4],
                       [("full", gvec(1, 3)), ("seg", modrow(1, 5, 1))], [(D_MODEL, F32), (D_MODEL, F32)],
                       name="gm_out", tk=GM_INNER)
    wts.update(get_w("ffn11", h5))
    h6, sv_f12 = _ffn_fwd("l1f2", h5, gvec(1, 4), gvec(1, 5), modrow(1, 6, 1), modrow(1, 7, 1), modrow(1, 8, 1),
                          wts["ffn11"], **L1)

    dh, loss_parts = _rowwise("loss", _loss_fn, t_len, [h6, target], [], [(D_MODEL, F32)], [D_MODEL], tm=tm1)

    zero = jnp.zeros((D_MODEL,), F32)
    dmx = [[zero] * N_MOD for _ in range(2)]
    dmc = [[zero] * N_MOD for _ in range(2)]
    dng = [[zero] * 6 for _ in range(2)]

    def put_mod(i, k, acc):
        if acc.shape[0] == 2:
            dmc[i][k] = dmc[i][k] + acc[0, 0]
            dmx[i][k] = dmx[i][k] + acc[1, 0]
        else:
            dmx[i][k] = dmx[i][k] + acc[0, 0]

    def put_g(i, k, acc):
        dng[i][k] = dng[i][k] + jnp.sum(acc[:, 0], axis=0)

    def ffn_back(tag, i, j, dho, sv, w, lay):
        nseg = lay["nseg"]
        base = 0 if j == 0 else 6
        gi = 0 if j == 0 else 4
        dh_in, pending[0], s = _ffn_bwd(tag, dho, sv, gvec(i, gi), gvec(i, gi + 1), modrow(i, base + 1, nseg),
                                        modrow(i, base + 2, nseg), w, functools.partial(put_grad, f"ffn{i}{j}"), **lay)
        put_mod(i, base, s["shift"])
        put_mod(i, base + 1, s["scale"])
        put_mod(i, base + 2, s["gate"])
        put_g(i, gi, s["gpre"])
        put_g(i, gi + 1, s["gpost"])
        return dh_in

    dh = ffn_back("l1f2", 1, 1, dh, sv_f12, wts["ffn11"], L1)
    dyo, dgate, dgp = _rowwise("l1m_postb", functools.partial(_post_bwd_fn, 1.0), t_len, [dh, yo1],
                               [("full", gvec(1, 3)), ("seg", modrow(1, 5, 1))], [(D_MODEL, BF16)],
                               [D_MODEL, D_MODEL], tm=tm1)
    put_mod(1, 5, dgate)
    put_g(1, 3, dgp)
    put_grad("gm", "w_out", _mm_tn(tgm, dyo, name="gm_dwout", tn=1024, col_blocks=1))
    dtg = _mm(dyo, wts["gm_wout"], out_dtype=F32, name="gm_dt", rhs_t=True)
    dgu, dgvn, dws, dbst = _gm_spatial_bwd(dtg, gu, gvn, ws_bf, wst_bf, bst, name="gm_spatialb")
    g["gm_w_s"] = dws
    g["gm_b_s"] = dbst.T
    dp1, dvg, dvb = _rowwise("gm_actb", _gm_act_bwd_fn, t_len, [p1, dgu, dgvn], [("full", vg)],
                             [(2 * GM_INNER, BF16)], [GM_INNER, GM_INNER], tm=128)
    g["gm_v_g"] = dvg[0, 0]
    g["gm_v_b"] = dvb[0, 0]
    pending[0] = put_grad("gm", "w_in", _mm_tn(um1, dp1, name="gm_dwin", tm=1024, col_blocks=NDEV))
    dh, dsh, dsc, dgp = _mm_rows(dp1, wts["gm_win"], _pre_bwd_fn, [h4, dh],
                                 [("full", gvec(1, 2)), ("seg", modrow(1, 4, 1))], [(D_MODEL, F32)],
                                 [D_MODEL, D_MODEL, D_MODEL], name="gm_dum", tk=1024, rhs_t=True)
    put_mod(1, 3, dsh)
    put_mod(1, 4, dsc)
    put_g(1, 2, dgp)
    dh = ffn_back("l1f1", 1, 0, dh, sv_f11, wts["ffn10"], L1)

    dh = ffn_back("l0f2", 0, 1, dh, sv_f02, wts["ffn01"], L1)
    dyo, dgate, dgp = _rowwise("l0m_postb", functools.partial(_post_bwd_fn, 1.0), t_len, [dh, yo0],
                               [("full", gvec(0, 3)), ("seg", modrow(0, 5, 1))], [(D_MODEL, BF16)],
                               [D_MODEL, D_MODEL], tm=tm1)
    put_mod(0, 5, dgate)
    put_g(0, 3, dgp)
    tok = put_grad("ssd", "w_out", _mm_tn(yn, dyo, name="ssd_dwout", tn=1024, col_blocks=1))
    dyn = _mm(dyo, wts["ssd_wout"], out_dtype=F32, name="ssd_dyn", rhs_t=True)
    dy_ssd, dz, dngv, ddv = _rowwise("ssd_gateb", _ssdgate_bwd_fn, n0, [(dyn, SSD_INNER, 0, -ncc)] + gate_rows,
                                     [("full", dvec), ("full", ngv if tok is None else ngv + tok)],
                                     [(SSD_INNER, F32), (SSD_INNER, BF16)],
                                     [SSD_INNER, SSD_INNER], tm=128)
    g["ssd_norm_g"] = dngv[0, 0]
    g["ssd_D"] = jnp.sum(ddv[0, 0].reshape(SSD_HEADS, SSD_HEAD_DIM), axis=1)
    dxbcs, ddts, dalogs, dbiases = [], [], [], []
    for d in range(2):
        dxd, ddtd, dal, dbi = _ssd_scan_bwd(dy_ssd, xbc, hss[d], dt_dir[d], dtT_dir[d], bias_r[d], bias_c[d],
                                            alog_r[d], alog_c[d], dvec, rev=(d == 1), n_ctx_chunks=ncc,
                                            direct=(d == 0), name=f"ssd_scanb{d}")
        dxbcs.append(dxd)
        ddts.append(ddtd)
        dalogs.append(dal[0])
        dbiases.append(dbi[0])
    g["ssd_A_log"] = jnp.stack(dalogs)
    g["ssd_dt_bias"] = jnp.stack(dbiases)
    dxbc_pre, dcw8, dcb = _conv_bwd(dxbcs[0], dxbcs[1], cpre, xbc_pre, small["conv_w8"], n_ctx=n_ctx, name="ssd_convb")
    g["ssd_conv_w"] = dcw8[:SSD_CONV]
    g["ssd_conv_b"] = dcb[0]
    ddt_bf = jnp.concatenate([ddts[0], ddts[1]], axis=1).astype(BF16)
    n_in = SSD_INNER + SSD_CONV_DIM + 2 * nh
    dw_t = _mm_tn(dz, um0, name="ssd_dwz", col_blocks=1, stack=(n_in, 0, None))
    dw_t = _mm_tn(dxbc_pre, um0, name="ssd_dwxbc", col_blocks=1, stack=(n_in, SSD_INNER, dw_t))
    dw_t = _mm_tn(ddt_bf, um0, name="ssd_dwdt", col_blocks=1, stack=(n_in, SSD_INNER + SSD_CONV_DIM, dw_t))
    pending[0] = put_grad("ssd", "w_in", dw_t)
    dum0 = _mm(dz, win_ssd, out_dtype=F32, name="ssd_dum_z", tk=1024, n=D_MODEL)
    dum0 = _mm(dxbc_pre, win_ssd, out_dtype=F32, name="ssd_dum_x", tk=1024, n=D_MODEL,
               b_off=(SSD_INNER // 1024, 0), add=dum0)
    dum0 = _mm(ddt_bf, win_ssd, out_dtype=F32, name="ssd_dum_dt", tk=2 * nh, n=D_MODEL, b_off=(dt_blk, 0), add=dum0)
    dh0, dsh, dsc, dgp = _rowwise("l0m_preb", _pre_bwd_fn, n0, [dum0, h1, (dh, D_MODEL, 0, -(n_ctx // tm0))],
                                  [("full", gvec(0, 2)), ("seg", modrow(0, 4, 2))], [(D_MODEL, F32)],
                                  [D_MODEL, D_MODEL, D_MODEL], tm=tm0, **kw0)
    put_mod(0, 3, dsh)
    put_mod(0, 4, dsc)
    put_g(0, 2, dgp)
    dh0 = ffn_back("l0f1", 0, 0, dh0, sv_f01, wts["ffn00"], L0)
    grad_x = dh0[n_ctx:]
    g["norm_g"] = jnp.stack([jnp.stack(r) for r in dng])
    g["dmx"] = jnp.stack([jnp.concatenate(r) for r in dmx])
    g["dmc"] = jnp.stack([jnp.concatenate(r) for r in dmc])
    return loss_parts[0], grad_x, g


GROUPS = ("ffn00", "ssd", "ffn01", "ffn10", "gm", "ffn11")


TRANSPOSED_IN = ("ffn", "ssd")


def _is_transposed(group):
    return group.startswith(TRANSPOSED_IN)


def _mats_in(group, win_l):
    if _is_transposed(group):
        return {("win_t" if group.startswith("ffn") else group + "_win_t"): win_l.reshape(-1, win_l.shape[2])}
    k, nloc = win_l.shape[1], win_l.shape[2]
    return {group + "_win": jnp.transpose(win_l, (1, 0, 2)).reshape(k, NDEV * nloc)}


def _mats_out(group, wout_l):
    pre = "" if group.startswith("ffn") else group + "_"
    return {pre + "wout": wout_l.reshape(-1, wout_l.shape[2])}


def _group_mats(group, lands):
    m = {**_mats_in(group, lands[0]), **_mats_out(group, lands[1])}
    return {group: m} if group.startswith("ffn") else m


def _grad_blocks(which, grad):
    if grad.ndim == 3:
        return grad if grad.shape[0] == NDEV else grad.reshape(NDEV, grad.shape[1] // NDEV, grad.shape[2])
    if which == "w_in":
        k, n = grad.shape
        return jnp.transpose(grad.reshape(k, NDEV, n // NDEV), (1, 0, 2)).astype(BF16)
    return grad.reshape(NDEV, grad.shape[0] // NDEV, grad.shape[1]).astype(BF16)


def kernel(x, c, ctx, c_ctx, ada_w, ada_b, norm_g, ffn_w_in, ffn_w_out, ssd_w_in, ssd_conv_w, ssd_conv_b, ssd_dt_bias, ssd_A_log, ssd_D, ssd_norm_g, ssd_w_out, gm_w_in, gm_v_g, gm_v_b, gm_w_s, gm_b_s, gm_w_out, loss_target, m_c_ctx, m_ada_w, m_ada_b, m_norm_g, m_ffn_w_in, m_ffn_w_out, m_ssd_w_in, m_ssd_conv_w, m_ssd_conv_b, m_ssd_dt_bias, m_ssd_A_log, m_ssd_D, m_ssd_norm_g, m_ssd_w_out, m_gm_w_in, m_gm_v_g, m_gm_v_b, m_gm_w_s, m_gm_b_s, m_gm_w_out, v_c_ctx, v_ada_w, v_ada_b, v_norm_g, v_ffn_w_in, v_ffn_w_out, v_ssd_w_in, v_ssd_conv_w, v_ssd_conv_b, v_ssd_dt_bias, v_ssd_A_log, v_ssd_D, v_ssd_norm_g, v_ssd_w_out, v_gm_w_in, v_gm_v_g, v_gm_v_b, v_gm_w_s, v_gm_b_s, v_gm_w_out):
    me = 4 * lax.axis_index("x") + 2 * lax.axis_index("y") + lax.axis_index("c")
    d = D_MODEL
    ncol = N_MOD * d // NDEV

    small_pack = jnp.concatenate([c.reshape(-1), norm_g.reshape(-1), ssd_conv_w.reshape(-1),
                                  gm_v_g.reshape(-1), gm_v_b.reshape(-1)])[None, :]
    (sp,), _ = _exchange([small_pack], scatter=False, name="gather_small")
    sp = sp[:, 0]
    o = 0
    c_all = sp[:, o:o + d]; o += d
    ng_all = sp[:, o:o + 2 * 6 * 128].reshape(NDEV, 2, 6, 128); o += 2 * 6 * 128
    cw_all = sp[:, o:o + SSD_CONV * 512].reshape(NDEV, SSD_CONV, 512); o += SSD_CONV * 512
    vg_all = sp[:, o:o + 256]; o += 256
    vb_all = sp[:, o:o + 256]; o += 256
    norm_g_full = jnp.transpose(ng_all, (1, 2, 0, 3)).reshape(2, 6, d)
    conv_w_full = jnp.transpose(cw_all, (1, 0, 2)).reshape(SSD_CONV, SSD_CONV_DIM)
    gm_v_g_full = vg_all.reshape(-1)
    gm_v_b_full = vb_all.reshape(-1)

    c16 = jnp.concatenate([c_all, jnp.broadcast_to(c_ctx[None, :], (NDEV, d))], axis=0)
    ada_b_loc = lax.dynamic_slice_in_dim(ada_b, me * ncol, ncol, axis=1)
    mods_loc = jnp.stack([_mm_f32(c16, ada_w[i], name=f"ada_mod{i}", silu_a=True, bias=ada_b_loc[i][None, :])
                          for i in range(2)])
    (mods_all,), mods_done = _exchange([mods_loc], scatter=False, name="gather_mods")

    tr = lambda a: jnp.swapaxes(a, -1, -2)
    shard = {"ssd": (tr(ssd_w_in)[0], ssd_w_out[0]), "gm": (gm_w_in[0], gm_w_out[0])}
    for i in range(2):
        for j in range(2):
            shard[f"ffn{i}{j}"] = (tr(ffn_w_in)[i, j], ffn_w_out[i, j])
    apart = GROUPS[:2]
    units = []
    for grp in GROUPS:
        units += [(grp + "_in", grp, (0,)), (grp + "_out", grp, (1,))] if grp in apart else [(grp, grp, (0, 1))]
    gathers = {}
    started = mods_done
    for unit, grp, idx in units:
        srcs = [(shard[grp][k] + started).astype(BF16) for k in idx]
        st = _exchange_start(srcs, [_landing(s, me) for s in srcs], scatter=False, name="gather_start_" + unit)
        gathers[unit] = st[:4]
        started = st[4]

    def fetch(unit, after):
        return _exchange_wait(*gathers[unit], after, scatter=False, name="gather_wait_" + unit)

    def get_w(grp, after):
        if grp not in apart:
            return _group_mats(grp, fetch(grp, after))
        early = lambda later: _mats_in(grp, fetch(grp + "_in", later)[0])
        late = lambda later: _mats_out(grp, fetch(grp + "_out", later)[0])
        if grp.startswith("ffn"):
            return {grp: dict(early=early, late=late)}
        return dict(early(after), late=late)

    scatters = {}
    held = {}

    def put_grad(grp, which, grad):
        if grp in apart:
            unit, blocks = grp + "_" + which[2:], [_grad_blocks(which, grad)]
        else:
            held[grp, which] = _grad_blocks(which, grad)
            if (grp, "w_in") not in held or (grp, "w_out") not in held:
                return None
            unit, blocks = grp, [held[grp, "w_in"], held[grp, "w_out"]]
        own = [lax.dynamic_index_in_dim(b, me, axis=0, keepdims=False) for b in blocks]
        st = _exchange_start(blocks, [_landing(o_, me) for o_ in own], scatter=True, name="scatter_start_" + unit)
        scatters[unit] = st[:4]
        return st[4]

    mods_rows = jnp.transpose(mods_all, (1, 2, 0, 3)).reshape(2, 2 * NDEV, N_MOD * d) + started
    mx = lax.dynamic_index_in_dim(mods_rows, me, axis=1, keepdims=False).reshape(2, N_MOD, d)
    mc = mods_rows[:, NDEV].reshape(2, N_MOD, d)
    mods = [(mc[i], mx[i]) for i in range(2)]

    small = dict(conv_w8=jnp.pad(conv_w_full, ((0, 8 - SSD_CONV), (0, 0))), conv_b=ssd_conv_b, dt_bias=ssd_dt_bias[0],
                 a_log=ssd_A_log[0], ssd_d=ssd_D[0], ssd_norm_g=ssd_norm_g[0], gm_v_g=gm_v_g_full,
                 gm_v_b=gm_v_b_full, gm_w_s=gm_w_s[0], gm_b_s=gm_b_s[0])
    loss_parts, grad_x, g = _local_step(x[0], ctx[0], loss_target[0], mods, norm_g_full, get_w, small, put_grad)
    g["loss"] = (0.5 / d * jnp.sum(loss_parts)).reshape(1)

    whole = {"ffn_w_in": (tr(ffn_w_in), tr(m_ffn_w_in), tr(v_ffn_w_in)), "ffn_w_out": (ffn_w_out, m_ffn_w_out, v_ffn_w_out),
             "ssd_w_in": (tr(ssd_w_in), tr(m_ssd_w_in), tr(v_ssd_w_in)), "ssd_w_out": (ssd_w_out, m_ssd_w_out, v_ssd_w_out),
             "gm_w_in": (gm_w_in, m_gm_w_in, v_gm_w_in), "gm_w_out": (gm_w_out, m_gm_w_out, v_gm_w_out)}
    res = {}

    def update_units(some, after):
        for unit, grp, idx in some:
            parts = _exchange_wait(*scatters[unit], after, scatter=True, name="scatter_wait_" + unit)
            for k, p in zip(idx, parts):
                which = ("in", "out")[k]
                nm = ("ffn" if grp.startswith("ffn") else grp) + "_w_" + which
                sel = (int(grp[3]), int(grp[4])) if grp.startswith("ffn") else (0,)
                res[nm] = _adamw(p, *whole[nm], name=f"adamw_{grp}_{which}", sel=sel, into=res.get(nm))
                after = res[nm][0]
        return after

    sg_names = ["dmx", "dmc", "norm_g", "ssd_conv_w", "ssd_conv_b", "ssd_dt_bias", "ssd_A_log", "ssd_D", "ssd_norm_g",
                "gm_v_g", "gm_v_b", "gm_w_s", "gm_b_s", "loss"]
    sg_shapes = [g[n].shape for n in sg_names]
    flat = jnp.concatenate([g[n].reshape(-1) for n in sg_names])
    npack = flat.shape[0]
    pad = (-npack) % 1024
    flat = jnp.pad(flat, (0, pad)).reshape(-1, 128)
    sg_start = _exchange_start([flat], [_landing(flat, me)], scatter=False, name="small_grads_start")
    by_send = list(reversed(units))
    update_units(by_send[:4], jnp.stack([sg_start[4], grad_x[0, 0]]))
    early_done = jnp.stack([res[nm][0].reshape(-1)[-1] for nm in sorted(res)])
    (sg_all,) = _exchange_wait(*sg_start[:4], early_done, scatter=False, name="small_grads_wait")
    sg_sum = _sum_slots(sg_all, name="sum_small_grads").reshape(-1)[:npack]
    update_units(by_send[4:], sg_sum)
    sums = {}
    o = 0
    for n, shp in zip(sg_names, sg_shapes):
        sz = math.prod(shp)
        sums[n] = sg_sum[o:o + sz].reshape(shp)
        o += sz
    loss = sums["loss"][0]
    per_dev = sg_all.reshape(NDEV, -1)
    dmx_all =per_dev[:, :2 * N_MOD * d].reshape(NDEV, 2, N_MOD * d)
    dmc_all = per_dev[:, 2 * N_MOD * d:4 * N_MOD * d].reshape(NDEV, 2, N_MOD * d)

    (s16,) = _rowwise("ada_silu", lambda cc: ((_silu(cc),), ()), 2 * NDEV, [c16], [], [(d, F32)], tm=2 * NDEV)
    s16_t = s16.T
    g_ada_w, dcc_parts = [], []
    for i in range(2):
        rhs = jnp.concatenate([lax.dynamic_slice_in_dim(dmx_all[:, i], me * ncol, ncol, axis=1),
                               lax.dynamic_slice_in_dim(dmc_all[:, i], me * ncol, ncol, axis=1)], axis=0)
        g_ada_w.append(_mm_f32(s16_t, rhs, name=f"ada_dw{i}"))
        dmc_loc = lax.dynamic_slice_in_dim(sums["dmc"][i], me * ncol, ncol, axis=0)
        rhs_c = jnp.zeros((ncol, 128), F32).at[:, 0].set(dmc_loc)
        dcc_parts.append(_mm_f32(ada_w[i], rhs_c, name=f"ada_dcc{i}")[:, 0])
    g_ada_w = jnp.stack(g_ada_w)
    dcc_part = (dcc_parts[0] + dcc_parts[1]).reshape(8, 128)
    (dcc_all,), _ = _exchange([dcc_part], scatter=False, name="gather_dcc")
    g_c_ctx = _sum_slots(dcc_all, name="sum_dcc", scale_by=c_ctx.reshape(8, 128)).reshape(d)
    g_ada_b = sums["dmx"] + sums["dmc"]

    outs = _adamw(g_ada_w.reshape(1, -1, ncol), ada_w.reshape(-1, ncol), m_ada_w.reshape(-1, ncol),
                  v_ada_w.reshape(-1, ncol), name="adamw_ada_w")
    res["ada_w"] = [o_.reshape(ada_w.shape) for o_ in outs]

    loc = lambda a, ax, n: lax.dynamic_slice_in_dim(a, me * n, n, axis=ax)
    small_g = dict(c_ctx=g_c_ctx, ada_b=g_ada_b, norm_g=loc(sums["norm_g"], 2, 128),
                   ssd_conv_w=loc(sums["ssd_conv_w"], 1, 512)[None], ssd_conv_b=sums["ssd_conv_b"][None],
                   ssd_dt_bias=sums["ssd_dt_bias"][None], ssd_A_log=sums["ssd_A_log"][None], ssd_D=sums["ssd_D"][None],
                   ssd_norm_g=sums["ssd_norm_g"][None], gm_v_g=loc(sums["gm_v_g"], 0, 256)[None],
                   gm_v_b=loc(sums["gm_v_b"], 0, 256)[None], gm_w_s=sums["gm_w_s"][None], gm_b_s=sums["gm_b_s"][None])
    small_w = dict(c_ctx=(c_ctx, m_c_ctx, v_c_ctx), ada_b=(ada_b, m_ada_b, v_ada_b), norm_g=(norm_g, m_norm_g, v_norm_g),
                   ssd_conv_w=(ssd_conv_w, m_ssd_conv_w, v_ssd_conv_w), ssd_conv_b=(ssd_conv_b, m_ssd_conv_b, v_ssd_conv_b),
                   ssd_dt_bias=(ssd_dt_bias, m_ssd_dt_bias, v_ssd_dt_bias), ssd_A_log=(ssd_A_log, m_ssd_A_log, v_ssd_A_log),
                   ssd_D=(ssd_D, m_ssd_D, v_ssd_D), ssd_norm_g=(ssd_norm_g, m_ssd_norm_g, v_ssd_norm_g),
                   gm_v_g=(gm_v_g, m_gm_v_g, v_gm_v_g), gm_v_b=(gm_v_b, m_gm_v_b, v_gm_v_b),
                   gm_w_s=(gm_w_s, m_gm_w_s, v_gm_w_s), gm_b_s=(gm_b_s, m_gm_b_s, v_gm_b_s))
    sn = list(small_w)

    def pack(arrs):
        f = jnp.concatenate([a.reshape(-1) for a in arrs])
        return jnp.pad(f, (0, (-f.shape[0]) % (256 * 128))).reshape(-1, 128)

    pg = pack([small_g[n].reshape(small_w[n][0].shape) for n in sn])
    outs = _adamw(pg[None], pack([small_w[n][0] for n in sn]), pack([small_w[n][1] for n in sn]),
                  pack([small_w[n][2] for n in sn]), name="adamw_small")
    flat_outs = [o_.reshape(-1) for o_ in outs]
    o = 0
    for n in sn:
        shp = small_w[n][0].shape
        sz = math.prod(shp)
        res[n] = [fo[o:o + sz].reshape(shp) for fo in flat_outs]
        o += sz

    order = ["c_ctx", "ada_w", "ada_b", "norm_g", "ffn_w_in", "ffn_w_out", "ssd_w_in", "ssd_conv_w", "ssd_conv_b",
             "ssd_dt_bias", "ssd_A_log", "ssd_D", "ssd_norm_g", "ssd_w_out", "gm_w_in", "gm_v_g", "gm_v_b", "gm_w_s",
             "gm_b_s", "gm_w_out"]
    for nm in ("ffn_w_in", "ssd_w_in"):
        res[nm] = [tr(a) for a in res[nm]]
    result = [loss, grad_x[None]]
    for k in range(4):
        result += [res[n][k] for n in order]
    return tuple(result)
```

```python
import functools
import math

import jax
import jax.numpy as jnp
from jax import lax
from jax.experimental import pallas as pl
from jax.experimental.pallas import tpu as pltpu

F32 = jnp.float32
BF16 = jnp.bfloat16

NDEV = 8
D_MODEL = 1024
FFN_DIM = 2816
N_MOD = 9
EPS = 1e-6
SSD_INNER = 2048
SSD_HEADS = 32
SSD_HEAD_DIM = 64
SSD_GROUPS = 8
SSD_HPG = 4
SSD_STATE = 128
SSD_CONV = 5
SSD_CONV_DIM = 4096
CHUNK = 128
GM_INNER = 2048
GM_GROUPS = 8
GM_GROUP_DIM = 256
ADAM_LR = 0.001
ADAM_B1 = 0.9
ADAM_B2 = 0.999
ADAM_EPS = 1e-08
ADAM_WD = 0.01
ADAM_STEP = 10
NEG_BIG = -1e30
VMEM_LIMIT_BYTES = 56 * 1024 * 1024
HI = lax.Precision.HIGHEST


def _params(*sem):
    return pltpu.CompilerParams(dimension_semantics=sem, vmem_limit_bytes=VMEM_LIMIT_BYTES)


def _pick(n, target, mult=16):
    if n <= target:
        return n
    for t in range(target - target % mult, 0, -mult):
        if n % t == 0:
            return t
    raise ValueError((n, target, mult))


def _sig(x):
    return 0.5 * jnp.tanh(0.5 * x) + 0.5


def _silu(x):
    return x * _sig(x)


def _dsilu(x):
    s = _sig(x)
    return s * (1.0 + x * (1.0 - s))


_GELU_C = math.sqrt(2.0 / math.pi)


def _gelu(x):
    return 0.5 * x * (1.0 + jnp.tanh(_GELU_C * (x + 0.044715 * x * x * x)))


def _gelu_and_grad(x):
    x2 = x * x
    t = jnp.tanh(_GELU_C * (x + 0.044715 * x2 * x))
    half = 0.5 * (1.0 + t)
    return x * half, half + 0.5 * x * (1.0 - t * t) * _GELU_C * (1.0 + 3.0 * 0.044715 * x2)


def _dgelu(x):
    return _gelu_and_grad(x)[1]


def _softplus(x):
    return jnp.maximum(x, 0.0) + jnp.log1p(jnp.exp(-jnp.abs(x)))


def _sum0(v):
    return jnp.sum(v, axis=0, keepdims=True)


def _rms(h):
    r = lax.rsqrt(jnp.mean(h * h, axis=-1, keepdims=True) + EPS)
    return h * r, r


def _dot(a, b, dims=((1,), (0,)), precision=None):
    return lax.dot_general(a, b, (dims, ((), ())), preferred_element_type=F32, precision=precision)


_NT = ((1,), (1,))
_TN = ((0,), (0,))


def _rowwise(name, fn, n_rows, rows, consts, outs, accs=(), *, tm, nseg=1, seg_blocks=0):
    assert n_rows % tm == 0
    if nseg == 2:
        assert seg_blocks > 0
        seg = lambda i: jnp.where(i < seg_blocks, 0, 1)
    else:
        seg = lambda i: 0
    in_specs, args, lacking = [], [], []
    for r in rows:
        arr, width, cb, off = r if isinstance(r, tuple) else (r, r.shape[1], 0, 0)
        in_specs.append(pl.BlockSpec((tm, width), lambda i, cb=cb, off=off: (jnp.maximum(i + off, 0), cb)))
        args.append(arr)
        lacking.append(-off if off < 0 else 0)
    for kind, arr in consts:
        if kind == "seg":
            assert arr.shape[0] == nseg and arr.shape[1] == 1, arr.shape
            in_specs.append(pl.BlockSpec((None, 1, arr.shape[2]), lambda i: (seg(i), 0, 0)))
        else:
            in_specs.append(pl.BlockSpec(arr.shape, lambda i: (0, 0)))
        args.append(arr)
    out_shape = [jax.ShapeDtypeStruct((n_rows, w), dt) for w, dt in outs]
    out_specs = [pl.BlockSpec((tm, w), lambda i: (i, 0)) for w, _ in outs]
    out_shape += [jax.ShapeDtypeStruct((nseg, 1, w), F32) for w in accs]
    out_specs += [pl.BlockSpec((None, 1, w), lambda i: (seg(i), 0, 0)) for w in accs]
    n_in, n_out, n_acc = len(args), len(outs), len(accs)

    def kern(*refs):
        i = pl.program_id(0)
        ins = [r[...] for r in refs[:n_in]]
        for k, lack in enumerate(lacking):
            if lack:
                ins[k] = jnp.where(i >= lack, ins[k], jnp.zeros_like(ins[k]))
        res, terms = fn(*ins)
        for ref, v in zip(refs[n_in:n_in + n_out], res):
            ref[...] = v.astype(ref.dtype)
        if n_acc:
            sums = [_sum0(v) for v in terms]
            first = (i == 0) | (i == seg_blocks) if nseg == 2 else (i == 0)
            acc_refs = refs[n_in + n_out:]

            @pl.when(first)
            def _():
                for ref, v in zip(acc_refs, sums):
                    ref[...] = v

            @pl.when(jnp.logical_not(first))
            def _():
                for ref, v in zip(acc_refs, sums):
                    ref[...] += v

    res = pl.pallas_call(
        kern, name=name, grid=(n_rows // tm,), in_specs=in_specs, out_specs=out_specs, out_shape=out_shape,
        compiler_params=_params("arbitrary"),
    )(*args)
    return res


def _pre_fwd_fn(h, g, shift, scale):
    hh, _ = _rms(h)
    return (hh * g * (1.0 + scale) + shift,), ()


def _pre_bwd_fn(du, h, dres, g, scale):
    hh, r = _rms(h)
    n = hh * g
    dn = du * (1.0 + scale)
    dhh = dn * g
    dh = dres + r * (dhh - hh * jnp.mean(dhh * hh, axis=-1, keepdims=True))
    return (dh,), (du, du * n, dn * hh)


def _post_fwd_fn(weight, h, y, g, gate):
    yh, _ = _rms(y)
    return (h + weight * gate * (yh * g),), ()


def _out_post_fn(weight, y, h, g, gate):
    return (y,) + _post_fwd_fn(weight, h, y, g, gate)[0], ()


def _post_bwd_fn(weight, dh, y, g, gate):
    yh, r = _rms(y)
    dr = dh * weight
    dyh = dr * gate * g
    dy = r * (dyh - yh * jnp.mean(dyh * yh, axis=-1, keepdims=True))
    return (dy,), (dr * yh * g, dr * gate * yh)


def _glu_bwd_fn(ds, a, b):
    a = a.astype(F32)
    b = b.astype(F32)
    sg = _sig(a)
    da = ds * b * (sg * (1.0 + a * (1.0 - sg)))
    db = ds * (a * sg)
    return (jnp.concatenate([da, db], axis=1),), ()


def _loss_fn(y, t):
    diff = y - t
    return (diff * (1.0 / D_MODEL),), (diff * diff,)


def _ssd_y(yf, yb, xs, z, dvec):
    y = yf + yb + dvec * xs
    return y, y * _silu(z)


def _ssdgate_fwd_fn(yf, yb, xs, z, dvec, ng):
    _, yg = _ssd_y(yf, yb, xs, z, dvec)
    parts = []
    for g in range(SSD_GROUPS):
        sl = slice(g * 256, (g + 1) * 256)
        parts.append(_rms(yg[:, sl])[0])
    return (jnp.concatenate(parts, axis=1) * ng,), ()


def _ssdgate_bwd_fn(dyn, yf, yb, xs, z, dvec, ng):
    y, yg = _ssd_y(yf, yb, xs, z, dvec)
    dyg_parts, ygh_parts = [], []
    for g in range(SSD_GROUPS):
        sl = slice(g * 256, (g + 1) * 256)
        ygh, r = _rms(yg[:, sl])
        d = dyn[:, sl] * ng[:, sl]
        dyg_parts.append(r * (d - ygh * jnp.mean(d * ygh, axis=-1, keepdims=True)))
        ygh_parts.append(ygh)
    dyg = jnp.concatenate(dyg_parts, axis=1)
    ygh = jnp.concatenate(ygh_parts, axis=1)
    dy = dyg * _silu(z)
    dz = dyg * y * _dsilu(z)
    return (dy, dz), (dyn * ygh, dy * xs)


def _ln_stats(v):
    mu = jnp.mean(v, axis=-1, keepdims=True)
    vc = v - mu
    r = lax.rsqrt(jnp.mean(vc * vc, axis=-1, keepdims=True) + EPS)
    return vc * r, r


def _gm_act_fwd_fn(p, vg, vb):
    gu = _gelu(p[:, :GM_INNER])
    gvh, _ = _ln_stats(_gelu(p[:, GM_INNER:]))
    return (gu, gvh * vg + vb), ()


def _gm_act_bwd_fn(p, dgu, dgvn, vg):
    pu = p[:, :GM_INNER]
    pv = p[:, GM_INNER:]
    gv, dgelu_v = _gelu_and_grad(pv)
    gvh, r = _ln_stats(gv)
    dgvh = dgvn * vg
    dgv = r * (dgvh - jnp.mean(dgvh, axis=-1, keepdims=True) - gvh * jnp.mean(dgvh * gvh, axis=-1, keepdims=True))
    dp = jnp.concatenate([dgu * _dgelu(pu), dgv * dgelu_v], axis=1)
    return (dp,), (dgvn * gvh, dgvn)


def _mm(a, b, *, out_dtype, name, tm=1088, tn=1024, tk=1408, add=None, rhs_t=False, n=None, b_off=(0, 0)):
    m, k = a.shape
    col_blocked = b.ndim == 3
    if col_blocked:
        assert not rhs_t and n is None and b.shape[1] == k
        n, tn = b.shape[0] * b.shape[2], b.shape[2]
    elif n is None:
        n, k2 = b.shape if rhs_t else b.shape[::-1]
        assert k == k2
    tm, tn, tk = _pick(m, tm), _pick(n, tn, 128), _pick(k, tk, 128)
    o0, o1 = b_off
    nk = k // tk
    dims = _NT if rhs_t else ((1,), (0,))

    def kern(*refs):
        a_ref, b_ref = refs[:2]
        add_ref = refs[2] if add is not None else None
        o_ref = refs[3] if add is not None else refs[2]

        def finish(r):
            if add is not None:
                r = r + add_ref[...]
            o_ref[...] = r.astype(o_ref.dtype)

        p = _dot(a_ref[...], b_ref[...], dims)
        if nk == 1:
            finish(p)
            return
        acc_ref = refs[-1]
        kk = pl.program_id(2)

        @pl.when(kk == 0)
        def _():
            acc_ref[...] = p

        @pl.when((kk > 0) & (kk < nk - 1))
        def _():
            acc_ref[...] += p

        @pl.when(kk == nk - 1)
        def _():
            finish(acc_ref[...] + p)

    if col_blocked:
        b_spec = pl.BlockSpec((None, tk, tn), lambda i, j, kk: (j, kk, 0))
    elif rhs_t:
        b_spec = pl.BlockSpec((tn, tk), lambda i, j, kk: (j + o0, kk + o1))
    else:
        b_spec = pl.BlockSpec((tk, tn), lambda i, j, kk: (kk + o0, j + o1))
    in_specs = [pl.BlockSpec((tm, tk), lambda i, j, kk: (i, kk)), b_spec]
    args = [a, b]
    if add is not None:
        in_specs.append(pl.BlockSpec((tm, tn), lambda i, j, kk: (i, j)))
        args.append(add)
    return pl.pallas_call(
        kern, name=name, grid=(m // tm, n // tn, nk), in_specs=in_specs,
        out_specs=pl.BlockSpec((tm, tn), lambda i, j, kk: (i, j)),
        out_shape=jax.ShapeDtypeStruct((m, n), out_dtype),
        scratch_shapes=[pltpu.VMEM((tm, tn), F32)] if nk > 1 else [],
        compiler_params=_params("parallel", "parallel", "arbitrary"),
    )(*args)


def _mm_rows(a, b, fn, rows, consts, outs, accs=(), *, name, tm=544, tk=1408, rhs_t=False, n_ctx=0):
    halves = a.ndim == 3
    m, k = (a.shape[1], 2 * a.shape[2]) if halves else a.shape
    col_blocked = b.ndim == 3
    if col_blocked:
        assert rhs_t and b.shape[0] * b.shape[2] == k
        n, tk = b.shape[1], b.shape[2]
    else:
        n = b.shape[0] if rhs_t else b.shape[1]
    tm, tk = _pick(m, tm), _pick(k, tk, 128)
    nk = k // tk
    if halves:
        hb = k // 2 // tk
        a_spec = pl.BlockSpec((None, tm, tk), lambda i, kk: (kk // hb, i, kk % hb))
    else:
        a_spec = pl.BlockSpec((tm, tk), lambda i, kk: (i, kk))
    dims = _NT if rhs_t else ((1,), (0,))
    n_rows, n_const, n_out, n_acc = len(rows), len(consts), len(outs), len(accs)

    def kern(*refs):
        a_ref, b_ref = refs[:2]
        row_refs = refs[2:2 + n_rows]
        const_refs = refs[2 + n_rows:2 + n_rows + n_const]
        out_refs = refs[2 + n_rows + n_const:2 + n_rows + n_const + n_out]
        acc_refs = refs[2 + n_rows + n_const + n_out:2 + n_rows + n_const + n_out + n_acc]
        i, kk = pl.program_id(0), pl.program_id(1)

        def finish(p, rs=slice(None), r0=0):
            nr = p.shape[0]
            is_ctx = (i * tm + r0 + lax.broadcasted_iota(jnp.int32, (nr, 1), 0)) < n_ctx
            cvals = []
            for (kind, arr), ref in zip(consts, const_refs):
                if kind == "seg":
                    cvals.append(jnp.where(is_ctx, ref[0], ref[1]) if arr.shape[0] == 2 else ref[0])
                else:
                    cvals.append(ref[...])
            res, terms = fn(p, *[r[rs, :] for r in row_refs], *cvals)
            for ref, v in zip(out_refs, res):
                ref[rs, :] = v.astype(ref.dtype)
            for ref, v in zip(acc_refs, terms):
                s_all = _sum0(v)
                s_ctx = _sum0(jnp.where(is_ctx, v, 0.0)) if n_ctx else jnp.zeros_like(s_all)
                both = jnp.concatenate([s_ctx, s_all - s_ctx], axis=0)[:, None, :]

                @pl.when(i == 0)
                def _():
                    ref[...] = both

                @pl.when(i > 0)
                def _():
                    ref[...] += both

        if nk == 1 and n_acc == 0:
            nsub = 2 if tm % 32 == 0 else 1
            sub = tm // nsub
            for r in range(nsub):
                rs = slice(r * sub, (r + 1) * sub)
                finish(_dot(a_ref[rs, :], b_ref[...], dims), rs, r * sub)
            return
        p = _dot(a_ref[...], b_ref[...], dims)
        if nk == 1:
            finish(p)
            return
        scr = refs[-1]

        @pl.when(kk == 0)
        def _():
            scr[...] = p

        @pl.when((kk > 0) & (kk < nk - 1))
        def _():
            scr[...] += p

        @pl.when(kk == nk - 1)
        def _():
            finish(scr[...] + p)

    if col_blocked:
        b_spec = pl.BlockSpec((None, n, tk), lambda i, kk: (kk, 0, 0))
    elif rhs_t:
        b_spec = pl.BlockSpec((n, tk), lambda i, kk: (0, kk))
    else:
        b_spec = pl.BlockSpec((tk, n), lambda i, kk: (kk, 0))
    in_specs = [a_spec, b_spec]
    in_specs += [pl.BlockSpec((tm, r.shape[1]), lambda i, kk: (i, 0)) for r in rows]
    for kind, arr in consts:
        in_specs.append(pl.BlockSpec(arr.shape, (lambda i, kk: (0, 0, 0)) if kind == "seg" else (lambda i, kk: (0, 0))))
    out_shape = [jax.ShapeDtypeStruct((m, w), dt) for w, dt in outs]
    out_specs = [pl.BlockSpec((tm, w), lambda i, kk: (i, 0)) for w, _ in outs]
    out_shape += [jax.ShapeDtypeStruct((2, 1, w), F32) for w in accs]
    out_specs += [pl.BlockSpec((2, 1, w), lambda i, kk: (0, 0, 0)) for w in accs]
    return pl.pallas_call(
        kern, name=name, grid=(m // tm, nk), in_specs=in_specs, out_specs=out_specs, out_shape=out_shape,
        scratch_shapes=[pltpu.VMEM((tm, n), F32)] if nk > 1 else [],
        compiler_params=_params("arbitrary", "arbitrary"),
    )(a, b, *rows, *[arr for _, arr in consts])


def _mm_glu(u, win_t, *, name, tm=2176, tn=256):
    m, k = u.shape
    n = win_t.shape[0] // 2
    tm, tn = _pick(m, tm), _pick(n, tn, 128)
    nj = n // tn

    nsub = 4 if tm % 64 == 0 else 1
    sub = tm // nsub

    def kern(u_ref, wa_ref, wb_ref, s_ref, a_ref, b_ref):
        for r in range(nsub):
            rows = slice(r * sub, (r + 1) * sub)
            uu = u_ref[rows, :]
            a = _dot(uu, wa_ref[...], _NT)
            b = _dot(uu, wb_ref[...], _NT)
            s_ref[rows, :] = (_silu(a) * b).astype(BF16)
            a_ref[rows, :] = a.astype(BF16)
            b_ref[rows, :] = b.astype(BF16)

    ospec = pl.BlockSpec((tm, tn), lambda i, j: (i, j))
    return pl.pallas_call(
        kern, name=name, grid=(m // tm, nj),
        in_specs=[pl.BlockSpec((tm, k), lambda i, j: (i, 0)), pl.BlockSpec((tn, k), lambda i, j: (j, 0)),
                  pl.BlockSpec((tn, k), lambda i, j: (nj + j, 0))],
        out_specs=[ospec, ospec, ospec],
        out_shape=[jax.ShapeDtypeStruct((m, n), BF16)] * 3,
        compiler_params=_params("parallel", "parallel"),
    )(u, win_t, win_t)


def _mm_glu_bwd(dy, wout, a, b, *, name, tm=544, tn=1408):
    m, k = dy.shape
    f = wout.shape[0]
    tm, tn = _pick(m, tm), _pick(f, tn, 128)
    nsub = 2 if tm % 32 == 0 else 1
    sub = tm // nsub

    def kern(dy_ref, w_ref, a_ref, b_ref, o_ref):
        for r in range(nsub):
            rs = slice(r * sub, (r + 1) * sub)
            ds = _dot(dy_ref[rs, :], w_ref[...], _NT)
            (dp,), _ = _glu_bwd_fn(ds, a_ref[rs, :], b_ref[rs, :])
            o_ref[0, rs, :] = dp[:, :tn].astype(BF16)
            o_ref[1, rs, :] = dp[:, tn:].astype(BF16)

    tile = pl.BlockSpec((tm, tn), lambda i, j: (i, j))
    return pl.pallas_call(
        kern, name=name, grid=(m // tm, f // tn),
        in_specs=[pl.BlockSpec((tm, k), lambda i, j: (i, 0)), pl.BlockSpec((tn, k), lambda i, j: (j, 0)), tile, tile],
        out_specs=pl.BlockSpec((2, tm, tn), lambda i, j: (0, i, j)),
        out_shape=jax.ShapeDtypeStruct((2, m, f), BF16),
        compiler_params=_params("parallel", "parallel"),
    )(dy, wout, a, b)


def _mm_tn(a, b, *, name, tm=1024, tn=1024, tk=1088, col_blocks=None, stack=None):
    extra, extra_specs, aliases = [], [], {}
    halves = a.ndim == 3
    t, m = (a.shape[1], 2 * a.shape[2]) if halves else a.shape
    t2, n = b.shape
    assert t == t2
    tm, tn, tk = _pick(m, tm, 128), _pick(n, tn, 128), _pick(t, tk)
    nk = t // tk
    if halves:
        hb = m // 2 // tm
        a_spec = pl.BlockSpec((None, tk, tm), lambda i, j, kk: (i // hb, kk, i % hb))
    else:
        a_spec = pl.BlockSpec((tk, tm), lambda i, j, kk: (kk, i))
    if col_blocks is None:
        def kern(a_ref, b_ref, o_ref):
            kk = pl.program_id(2)

            @pl.when(kk == 0)
            def _():
                o_ref[...] = jnp.zeros_like(o_ref)

            o_ref[...] += _dot(a_ref[...], b_ref[...], _TN)

        out_spec = pl.BlockSpec((tm, tn), lambda i, j, kk: (i, j))
        out_shape = jax.ShapeDtypeStruct((m, n), F32)
        scratch = []
    else:
        wb = n // col_blocks
        per = tn // wb
        assert tn % wb == 0 and wb % 8 == 0

        def kern(a_ref, b_ref, *rest):
            o_ref, acc_ref = rest[-2:]
            kk = pl.program_id(2)
            p = _dot(a_ref[...], b_ref[...], _TN)

            @pl.when(kk == 0)
            def _():
                acc_ref[...] = p

            @pl.when((kk > 0) & (kk < nk - 1))
            def _():
                acc_ref[...] += p

            @pl.when(kk == nk - 1)
            def _():
                r = acc_ref[...] + p if nk > 1 else p
                for c in range(per):
                    o_ref[c] = r[:, c * wb:(c + 1) * wb].astype(BF16)

        rows_total, row0, into = stack if stack is not None else (m, 0, None)
        assert row0 % tm == 0
        out_spec = pl.BlockSpec((per, tm, wb), lambda i, j, kk: (j, i + row0 // tm, 0))
        out_shape = jax.ShapeDtypeStruct((col_blocks, rows_total, wb), BF16)
        scratch = [pltpu.VMEM((tm, tn), F32)]
        if into is not None:
            extra, extra_specs, aliases = [into], [pl.BlockSpec(memory_space=pl.ANY)], {2: 0}

    return pl.pallas_call(
        kern, name=name, grid=(m // tm, n // tn, nk),
        in_specs=[a_spec, pl.BlockSpec((tk, tn), lambda i, j, kk: (kk, j))] + extra_specs,
        out_specs=out_spec, out_shape=out_shape, scratch_shapes=scratch, input_output_aliases=aliases,
        compiler_params=_params("parallel", "parallel", "arbitrary"),
    )(a, b, *extra)


def _mm_f32(a, b, *, name, silu_a=False, bias=None):
    m, k = a.shape
    n = b.shape[1]

    def kern(*refs):
        if bias is None:
            a_ref, b_ref, o_ref = refs
        else:
            a_ref, b_ref, bias_ref, o_ref = refs
        av = a_ref[...]
        if silu_a:
            av = _silu(av)
        r = jnp.dot(av, b_ref[...], preferred_element_type=F32, precision=HI)
        if bias is not None:
            r = r + bias_ref[...]
        o_ref[...] = r

    args = [a, b] + ([] if bias is None else [bias])
    return pl.pallas_call(kern, name=name, out_shape=jax.ShapeDtypeStruct((m, n), F32),
                          compiler_params=pltpu.CompilerParams(vmem_limit_bytes=VMEM_LIMIT_BYTES))(*args)


CONV_WIN = 32


def _conv_windows(n, n_ctx):
    assert n_ctx % CONV_WIN == 0 and n_ctx >= CONV_WIN and n - n_ctx >= CONV_WIN
    return (0, n_ctx - CONV_WIN // 2, n - CONV_WIN)


def _tap_outside(r0, s, n, n_ctx):
    t = r0 + lax.broadcasted_iota(jnp.int32, (CONV_WIN, 1), 0)
    lo = jnp.where(t < n_ctx, 0, n_ctx)
    hi = jnp.where(t < n_ctx, n_ctx, n)
    return jnp.where((t + s >= lo) & (t + s < hi), 0.0, 1.0)


def _rolled(v, s):
    return v if s == 0 else pltpu.roll(v, (-s) % v.shape[0], 0)


def _conv_fwd(xp, w8, b, *, n_ctx, name, cb=256):
    n, c = xp.shape
    half = SSD_CONV // 2

    def kern(x_ref, w_ref, b_ref, cpre_ref, act_ref):
        x = x_ref[...]
        acc = jnp.zeros_like(x) + b_ref[...]
        rolled = {}
        for k in range(SSD_CONV):
            rolled[k] = _rolled(x, k - half)
            acc = acc + rolled[k] * w_ref[k:k + 1, :]
        cpre_ref[...] = acc
        act_ref[...] = _silu(acc)
        for r0 in _conv_windows(n, n_ctx):
            rows = slice(r0, r0 + CONV_WIN)
            fix = acc[rows]
            for k in range(SSD_CONV):
                if k != half:
                    fix = fix - rolled[k][rows] * w_ref[k:k + 1, :] * _tap_outside(r0, k - half, n, n_ctx)
            cpre_ref[rows, :] = fix
            act_ref[rows, :] = _silu(fix)

    spec = pl.BlockSpec((n, cb), lambda j: (0, j))
    return pl.pallas_call(
        kern, name=name, grid=(c // cb,),
        in_specs=[spec, pl.BlockSpec((8, cb), lambda j: (0, j)), pl.BlockSpec((1, cb), lambda j: (0, j))],
        out_specs=[spec, spec], out_shape=[jax.ShapeDtypeStruct((n, c), F32)] * 2,
        compiler_params=_params("parallel"),
    )(xp, w8, b)


def _conv_bwd(d1, d2, cpre, xp, w8, *, n_ctx, name, cb=128):
    n, c = xp.shape
    half = SSD_CONV // 2

    def kern(d1_ref, d2_ref, cpre_ref, x_ref, w_ref, dx_ref, dw_ref, db_ref):
        g = (d1_ref[...] + d2_ref[...]) * _dsilu(cpre_ref[...])
        x = x_ref[...]
        dx = jnp.zeros_like(g)
        dw_ref[...] = jnp.zeros_like(dw_ref)
        g_rolled = {}
        for k in range(SSD_CONV):
            s = k - half
            g_rolled[k] = _rolled(g, -s)
            dx = dx + g_rolled[k] * w_ref[k:k + 1, :]
            xr = _rolled(x, s)
            dw = _sum0(g * xr)
            if s != 0:
                for r0 in _conv_windows(n, n_ctx):
                    rows = slice(r0, r0 + CONV_WIN)
                    dw = dw - _sum0(g[rows] * xr[rows] * _tap_outside(r0, s, n, n_ctx))
            dw_ref[k:k + 1, :] = dw
        dx_ref[...] = dx.astype(BF16)
        for r0 in _conv_windows(n, n_ctx):
            rows = slice(r0, r0 + CONV_WIN)
            fix = dx[rows]
            for k in range(SSD_CONV):
                if k != half:
                    fix = fix - g_rolled[k][rows] * w_ref[k:k + 1, :] * _tap_outside(r0, half - k, n, n_ctx)
            dx_ref[rows, :] = fix.astype(BF16)
        db_ref[...] = _sum0(g)

    spec = pl.BlockSpec((n, cb), lambda j: (0, j))
    return pl.pallas_call(
        kern, name=name, grid=(c // cb,),
        in_specs=[spec, spec, spec, spec, pl.BlockSpec((8, cb), lambda j: (0, j))],
        out_specs=[spec, pl.BlockSpec((8, cb), lambda j: (0, j)), pl.BlockSpec((1, cb), lambda j: (0, j))],
        out_shape=[jax.ShapeDtypeStruct((n, c), BF16), jax.ShapeDtypeStruct((8, c), F32),
                   jax.ShapeDtypeStruct((1, c), F32)],
        compiler_params=_params("parallel"),
    )(d1, d2, cpre, xp, w8)


def _chunk_of(s, nc, n_ctx_chunks, rev):
    if not rev:
        return s
    return jnp.where(s < n_ctx_chunks, n_ctx_chunks - 1 - s, nc - 1 - (s - n_ctx_chunks))


def _scan_common(dt_raw, dtT_raw, bias_r, bias_c, alog_r, alog_c, rev):
    ii = lax.broadcasted_iota(jnp.int32, (CHUNK, CHUNK), 0)
    jj = lax.broadcasted_iota(jnp.int32, (CHUNK, CHUNK), 1)
    tri = (jj >= ii) if rev else (jj <= ii)
    tri_t = (ii >= jj) if rev else (ii <= jj)
    a_r = -jnp.exp(alog_r)
    a_c = -jnp.exp(alog_c)
    dt = _softplus(dt_raw + bias_r)
    dt_t = _softplus(dtT_raw + bias_c)
    al = dt * a_r
    acum = _dot(tri.astype(F32), al, precision=HI)
    acum_t = _dot(dt_t * a_c, tri_t.astype(F32), precision=HI)
    atot = _sum0(al)
    return tri, tri_t, a_r, dt, acum, acum_t, atot


def _head_spread():
    return jnp.repeat(jnp.eye(SSD_HEADS, dtype=BF16), SSD_HEAD_DIM, axis=1)


def _dot_sel(v, sel):
    hi = v.astype(BF16)
    lo = (v - hi.astype(F32)).astype(BF16)
    return _dot(hi, sel) + _dot(lo, sel)


def _ssd_scan_fwd(xbc, dt_raw, dtT_raw, bias_r, bias_c, alog_r, alog_c, *, rev, n_ctx_chunks, name):
    n = xbc.shape[0]
    nc = n // CHUNK
    cidx = functools.partial(_chunk_of, nc=nc, n_ctx_chunks=n_ctx_chunks, rev=rev)

    def kern(xs_ref, b_ref, c_ref, dt_ref, dtT_ref, br_ref, bc_ref, ar_ref, ac_ref, e_ref, y_ref, hs_ref, h_scr):
        @pl.when(pl.program_id(0) == 0)
        def _():
            h_scr[...] = jnp.zeros_like(h_scr)

        tri, _, _, dt, acum, acum_t, atot = _scan_common(
            dt_ref[...], dtT_ref[...], br_ref[...], bc_ref[...], ar_ref[...], ac_ref[...], rev)
        etot = jnp.exp(atot)
        spread = lambda v: _dot_sel(v, e_ref[...])
        xdt_all = xs_ref[...] * spread(dt)
        eax = spread(jnp.exp(acum))
        xdw_all = xdt_all * spread(jnp.exp(atot - acum))
        hs_ref[...] = h_scr[...]
        for g in range(SSD_GROUPS):
            gs = slice(g * 256, (g + 1) * 256)
            bg = b_ref[:, g * SSD_STATE:(g + 1) * SSD_STATE].astype(BF16)
            cg = c_ref[:, g * SSD_STATE:(g + 1) * SSD_STATE].astype(BF16)
            cb = _dot(cg, bg, _NT)
            h4 = h_scr[gs, :]
            ys = []
            for k in range(SSD_HPG):
                h = g * SSD_HPG + k
                lmat = jnp.exp(jnp.where(tri, acum[:, h:h + 1] - acum_t[h:h + 1, :], NEG_BIG))
                xdt_h = xdt_all[:, h * SSD_HEAD_DIM:(h + 1) * SSD_HEAD_DIM].astype(BF16)
                ys.append(_dot((cb * lmat).astype(BF16), xdt_h))
            y_ref[:, gs] = jnp.concatenate(ys, axis=1) + _dot(cg, h4.astype(BF16), _NT) * eax[:, gs]
            s4 = _dot(xdw_all[:, gs].astype(BF16), bg, _TN)
            for k in range(SSD_HPG):
                h = g * SSD_HPG + k
                rs = slice(h * SSD_HEAD_DIM, (h + 1) * SSD_HEAD_DIM)
                h_scr[rs, :] = h4[k * SSD_HEAD_DIM:(k + 1) * SSD_HEAD_DIM] * etot[:, h:h + 1] + \
                    s4[k * SSD_HEAD_DIM:(k + 1) * SSD_HEAD_DIM]

    nh = SSD_HEADS
    small = lambda shape: pl.BlockSpec(shape, lambda s: (0, 0))
    return pl.pallas_call(
        kern, name=name, grid=(nc,),
        in_specs=[pl.BlockSpec((CHUNK, SSD_INNER), lambda s: (cidx(s), 0)),
                  pl.BlockSpec((CHUNK, 1024), lambda s: (cidx(s), 2)),
                  pl.BlockSpec((CHUNK, 1024), lambda s: (cidx(s), 3)),
                  pl.BlockSpec((CHUNK, nh), lambda s: (cidx(s), 0)),
                  pl.BlockSpec((nh, CHUNK), lambda s: (0, cidx(s))),
                  small((1, nh)), small((nh, 1)), small((1, nh)), small((nh, 1)), small((nh, SSD_INNER))],
        out_specs=[pl.BlockSpec((CHUNK, SSD_INNER), lambda s: (cidx(s), 0)),
                   pl.BlockSpec((None, SSD_INNER, SSD_STATE), lambda s: (s, 0, 0))],
        out_shape=[jax.ShapeDtypeStruct((n, SSD_INNER), F32),
                   jax.ShapeDtypeStruct((nc, SSD_INNER, SSD_STATE), F32)],
        scratch_shapes=[pltpu.VMEM((SSD_INNER, SSD_STATE), F32)],
        compiler_params=_params("arbitrary"),
    )(xbc, xbc, xbc, dt_raw, dtT_raw, bias_r, bias_c, alog_r, alog_c, _head_spread())


def _ssd_scan_bwd(dy, xbc, hs, dt_raw, dtT_raw, bias_r, bias_c, alog_r, alog_c, dvec, *, rev, n_ctx_chunks,
                  direct, name):
    n = xbc.shape[0]
    nc = n // CHUNK
    nh = SSD_HEADS
    step_of = lambda r: nc - 1 - r
    cidx = lambda r: _chunk_of(step_of(r), nc, n_ctx_chunks, rev)

    def kern(dy_ref, xs_ref, b_ref, c_ref, hs_ref, dt_ref, dtT_ref, br_ref, bc_ref, ar_ref, ac_ref, dv_ref,
             e_ref, et_ref, dx_ref, ddt_ref, dal_ref, dbias_ref, dh_scr):
        @pl.when(pl.program_id(0) == 0)
        def _():
            dh_scr[...] = jnp.zeros_like(dh_scr)
            dal_ref[...] = jnp.zeros_like(dal_ref)
            dbias_ref[...] = jnp.zeros_like(dbias_ref)

        tri, tri_t, a_r, dt, acum, acum_t, atot = _scan_common(
            dt_ref[...], dtT_ref[...], br_ref[...], bc_ref[...], ar_ref[...], ac_ref[...], rev)
        etot = jnp.exp(atot)
        spread = lambda v: _dot_sel(v, e_ref[...])
        gather = lambda v: _dot_sel(v, et_ref[...])
        xs_all = xs_ref[...]
        dy_all = dy_ref[...]
        dtx = spread(dt)
        eax = spread(jnp.exp(acum))
        decx = spread(jnp.exp(atot - acum))
        xdt_all = xs_all * dtx
        xdw_all = xdt_all * decx
        dyo_all = dy_all * eax
        lane = lax.broadcasted_iota(jnp.int32, (CHUNK, nh), 1)
        lane1 = lax.broadcasted_iota(jnp.int32, (1, nh), 1)
        sub = lax.broadcasted_iota(jnp.int32, (nh, CHUNK), 0)
        g_rows = jnp.zeros((CHUNK, nh), F32)
        g_cols = jnp.zeros((nh, CHUNK), F32)
        dtot = jnp.zeros((1, nh), F32)
        q_col, q_e, q_dt = [], [], []
        for g in range(SSD_GROUPS):
            gs = slice(g * 256, (g + 1) * 256)
            bg = b_ref[:, g * SSD_STATE:(g + 1) * SSD_STATE].astype(BF16)
            cg = c_ref[:, g * SSD_STATE:(g + 1) * SSD_STATE].astype(BF16)
            cb = _dot(cg, bg, _NT)
            hs4 = hs_ref[gs, :]
            dh4 = dh_scr[gs, :]
            hs4_bf = hs4.astype(BF16)
            dh4_bf = dh4.astype(BF16)
            dy4 = dy_all[:, gs]
            dy4_bf = dy4.astype(BF16)
            xdt4_bf = xdt_all[:, gs].astype(BF16)
            xdw4 = xdw_all[:, gs]
            xdw4_bf = xdw4.astype(BF16)
            dyo4_bf = dyo_all[:, gs].astype(BF16)
            yoff4 = _dot(cg, hs4_bf, _NT) * eax[:, gs]
            dcg = _dot(dyo4_bf, hs4_bf)
            dh_new4 = _dot(dyo4_bf, cg, _TN)
            bdh4 = _dot(bg, dh4_bf, _NT)
            dbg = _dot(xdw4_bf, dh4_bf)
            e4 = xdw4 * bdh4
            q_col.append(dy4 * yoff4 - e4)
            q_e.append(e4)
            hsum = jnp.sum(dh4 * hs4, axis=1, keepdims=True)
            dcb = jnp.zeros((CHUNK, CHUNK), F32)
            dxdts = []
            for k in range(SSD_HPG):
                h = g * SSD_HPG + k
                ks = slice(k * SSD_HEAD_DIM, (k + 1) * SSD_HEAD_DIM)
                lmat = jnp.exp(jnp.where(tri, acum[:, h:h + 1] - acum_t[h:h + 1, :], NEG_BIG))
                mf = cb * lmat
                dm = _dot(dy4_bf[:, ks], xdt4_bf[:, ks], _NT)
                dcb = dcb + dm * lmat
                gmat = dm * mf
                g_rows = g_rows + jnp.where(lane == h, jnp.sum(gmat, axis=1, keepdims=True), 0.0)
                g_cols = g_cols + jnp.where(sub == h, _sum0(gmat), 0.0)
                dxdts.append(_dot(mf.astype(BF16), dy4_bf[:, ks], _TN))
                et = etot[:, h:h + 1]
                dtot = dtot + jnp.where(lane1 == h, _sum0(hsum[ks]) * et, 0.0)
                dh_scr[h * SSD_HEAD_DIM:(h + 1) * SSD_HEAD_DIM, :] = dh4[ks] * et + dh_new4[ks]
            dxdt4 = jnp.concatenate(dxdts, axis=1) + bdh4 * decx[:, gs]
            q_dt.append(dxdt4 * xs_all[:, gs])
            dx4 = dxdt4 * dtx[:, gs]
            if direct:
                dx4 = dx4 + dy4 * dv_ref[:, gs]
            dcb_bf = dcb.astype(BF16)
            dx_ref[:, gs] = dx4
            dx_ref[:, SSD_INNER + g * SSD_STATE:SSD_INNER + (g + 1) * SSD_STATE] = dbg + _dot(dcb_bf, cg, _TN)
            dx_ref[:, SSD_INNER + 1024 + g * SSD_STATE:SSD_INNER + 1024 + (g + 1) * SSD_STATE] = \
                dcg + _dot(dcb_bf, bg)
        e_heads = gather(jnp.concatenate(q_e, axis=1))
        dacum = gather(jnp.concatenate(q_col, axis=1)) + g_rows - g_cols.T
        dal = _dot(tri_t.astype(F32), dacum, precision=HI) + dtot + _sum0(e_heads)
        ddt = gather(jnp.concatenate(q_dt, axis=1)) + dal * a_r
        ddt_raw = ddt * _sig(dt_ref[...] + br_ref[...])
        ddt_ref[...] = ddt_raw
        dal_ref[...] += _sum0(dal * dt) * a_r
        dbias_ref[...] += _sum0(ddt_raw)

    small = lambda shape: pl.BlockSpec(shape, lambda r: (0, 0))
    return pl.pallas_call(
        kern, name=name, grid=(nc,),
        in_specs=[pl.BlockSpec((CHUNK, SSD_INNER), lambda r: (cidx(r), 0)),
                  pl.BlockSpec((CHUNK, SSD_INNER), lambda r: (cidx(r), 0)),
                  pl.BlockSpec((CHUNK, 1024), lambda r: (cidx(r), 2)),
                  pl.BlockSpec((CHUNK, 1024), lambda r: (cidx(r), 3)),
                  pl.BlockSpec((None, SSD_INNER, SSD_STATE), lambda r: (step_of(r), 0, 0)),
                  pl.BlockSpec((CHUNK, nh), lambda r: (cidx(r), 0)),
                  pl.BlockSpec((nh, CHUNK), lambda r: (0, cidx(r))),
                  small((1, nh)), small((nh, 1)), small((1, nh)), small((nh, 1)), small((1, SSD_INNER)),
                  small((nh, SSD_INNER)), small((SSD_INNER, nh))],
        out_specs=[pl.BlockSpec((CHUNK, SSD_CONV_DIM), lambda r: (cidx(r), 0)),
                   pl.BlockSpec((CHUNK, nh), lambda r: (cidx(r), 0)),
                   small((1, nh)), small((1, nh))],
        out_shape=[jax.ShapeDtypeStruct((n, SSD_CONV_DIM), F32), jax.ShapeDtypeStruct((n, nh), F32),
                   jax.ShapeDtypeStruct((1, nh), F32), jax.ShapeDtypeStruct((1, nh), F32)],
        scratch_shapes=[pltpu.VMEM((SSD_INNER, SSD_STATE), F32)],
        compiler_params=_params("arbitrary"),
    )(dy, xbc, xbc, xbc, hs, dt_raw, dtT_raw, bias_r, bias_c, alog_r, alog_c, dvec, _head_spread(),
      _head_spread().T)


def _gm_spatial_fwd(gu, gvn, ws, bst, *, name):
    n = gu.shape[0]

    def kern(gu_ref, gv_ref, ws_ref, bs_ref, o_ref):
        for g in range(GM_GROUPS):
            sl = slice(g * GM_GROUP_DIM, (g + 1) * GM_GROUP_DIM)
            s = _dot(ws_ref[g], gv_ref[:, sl]) + bs_ref[:, g:g + 1]
            o_ref[:, sl] = (gu_ref[:, sl] * s).astype(BF16)

    spec = pl.BlockSpec((CHUNK, GM_INNER), lambda i: (i, 0))
    return pl.pallas_call(
        kern, name=name, grid=(n // CHUNK,),
        in_specs=[spec, spec, pl.BlockSpec(ws.shape, lambda i: (0, 0, 0)), pl.BlockSpec(bst.shape, lambda i: (0, 0))],
        out_specs=spec, out_shape=jax.ShapeDtypeStruct((n, GM_INNER), BF16),
        compiler_params=_params("parallel"),
    )(gu, gvn, ws, bst)


def _gm_spatial_bwd(dt, gu, gvn, ws, wst, bst, *, name):
    n = gu.shape[0]

    def kern(dt_ref, gu_ref, gv_ref, ws_ref, wst_ref, bs_ref, dgu_ref, dgv_ref, dws_ref, dbs_ref):
        @pl.when(pl.program_id(0) == 0)
        def _():
            dws_ref[...] = jnp.zeros_like(dws_ref)
            dbs_ref[...] = jnp.zeros_like(dbs_ref)

        lane = lax.broadcasted_iota(jnp.int32, (CHUNK, GM_GROUPS), 1)
        dbs = jnp.zeros((CHUNK, GM_GROUPS), F32)
        for g in range(GM_GROUPS):
            sl = slice(g * GM_GROUP_DIM, (g + 1) * GM_GROUP_DIM)
            gv = gv_ref[:, sl]
            s = _dot(ws_ref[g], gv) + bs_ref[:, g:g + 1]
            d = dt_ref[:, sl]
            dgu_ref[:, sl] = d * s
            ds = d * gu_ref[:, sl]
            ds_bf = ds.astype(BF16)
            dws_ref[g] += _dot(ds_bf, gv, _NT)
            dgv_ref[:, sl] = _dot(wst_ref[g], ds_bf)
            dbs = dbs + jnp.where(lane == g, jnp.sum(ds, axis=1, keepdims=True), 0.0)
        dbs_ref[...] += dbs

    spec = pl.BlockSpec((CHUNK, GM_INNER), lambda i: (i, 0))
    wspec = pl.BlockSpec(ws.shape, lambda i: (0, 0, 0))
    bspec = pl.BlockSpec(bst.shape, lambda i: (0, 0))
    return pl.pallas_call(
        kern, name=name, grid=(n // CHUNK,),
        in_specs=[spec, spec, spec, wspec, wspec, bspec],
        out_specs=[spec, spec, wspec, bspec],
        out_shape=[jax.ShapeDtypeStruct((n, GM_INNER), F32), jax.ShapeDtypeStruct((n, GM_INNER), F32),
                   jax.ShapeDtypeStruct(ws.shape, F32), jax.ShapeDtypeStruct(bst.shape, F32)],
        compiler_params=_params("arbitrary"),
    )(dt, gu, gvn, ws, wst, bst)


def _adamw(parts, w, m, v, *, name, tm=256, sel=(), into=None):
    ns, r, wd = parts.shape
    tm = _pick(r, tm, 8)
    tc = wd
    if tm < 64 and wd % 256 == 0:
        tm, tc = r, 256
    lead = len(sel)
    assert w.shape[lead:] == (r, wd) and lead == w.ndim - 2

    def kern(*refs):
        p_ref, w_ref, m_ref, v_ref = refs[:4]
        g_ref, d_ref, nm_ref, nv_ref = refs[-4:]
        g = p_ref[0].astype(F32)
        for s in range(1, ns):
            g = g + p_ref[s].astype(F32)
        m2 = ADAM_B1 * m_ref[...] + (1.0 - ADAM_B1) * g
        v2 = ADAM_B2 * v_ref[...] + (1.0 - ADAM_B2) * (g * g)
        m_hat = m2 / (1.0 - ADAM_B1 ** ADAM_STEP)
        v_hat = v2 / (1.0 - ADAM_B2 ** ADAM_STEP)
        g_ref[...] = g
        d_ref[...] = -ADAM_LR * (m_hat / (jnp.sqrt(v_hat) + ADAM_EPS) + ADAM_WD * w_ref[...])
        nm_ref[...] = m2
        nv_ref[...] = v2

    spec = pl.BlockSpec((None,) * lead + (tm, tc), lambda i, j: tuple(sel) + (i, j))
    extra, aliases = [], {}
    if into is not None:
        extra = list(into)
        aliases = {4 + k: k for k in range(4)}
    return pl.pallas_call(
        kern, name=name, grid=(r // tm, wd // tc),
        in_specs=[pl.BlockSpec((ns, tm, tc), lambda i, j: (0, i, j)), spec, spec, spec] +
                 [pl.BlockSpec(memory_space=pl.ANY)] * len(extra),
        out_specs=[spec] * 4, out_shape=[jax.ShapeDtypeStruct(w.shape, F32)] * 4,
        input_output_aliases=aliases,
        compiler_params=_params("parallel", "parallel"),
    )(parts, w, m, v, *extra)


def _sum_slots(parts, *, name, scale_by=None):
    ns, r, wd = parts.shape

    def kern(*refs):
        p_ref, o_ref = refs[0], refs[-1]
        g = p_ref[0]
        for s in range(1, ns):
            g = g + p_ref[s]
        if scale_by is not None:
            g = g * _dsilu(refs[1][...])
        o_ref[...] = g

    args = [parts] + ([] if scale_by is None else [scale_by])
    return pl.pallas_call(kern, name=name, out_shape=jax.ShapeDtypeStruct((r, wd), F32),
                          compiler_params=pltpu.CompilerParams(vmem_limit_bytes=VMEM_LIMIT_BYTES))(*args)


def _mesh_pos():
    x, y, c = lax.axis_index("x"), lax.axis_index("y"), lax.axis_index("c")
    return x, y, c, 4 * x + 2 * y + c


def _flip(x, y, c, f):
    fx, fy, fc = (f >> 2) & 1, (f >> 1) & 1, f & 1
    px = 1 - x if fx else x
    py = 1 - y if fy else y
    pc = 1 - c if fc else c
    return (px, py, pc), 4 * px + 2 * py + pc


_HBM_SPEC = pl.BlockSpec(memory_space=pltpu.HBM)


def _exchange(arrays, *, scatter, name):
    na = len(arrays)
    if scatter:
        out_shape = [jax.ShapeDtypeStruct(a.shape, a.dtype) for a in arrays]
    else:
        out_shape = [jax.ShapeDtypeStruct((NDEV,) + a.shape, a.dtype) for a in arrays]

    out_shape.append(jax.ShapeDtypeStruct((8, 128), F32))

    def body(*refs):
        ins, outs = refs[:na], refs[na:2 * na]
        send_sems, recv_sems, local_sems = refs[2 * na + 1:]
        refs[2 * na][...] = jnp.zeros((8, 128), F32)
        x, y, c, me = _mesh_pos()
        copies = []
        for i in range(na):
            src_own = ins[i].at[me] if scatter else ins[i]
            lc = pltpu.make_async_copy(src_own, outs[i].at[me], local_sems.at[i])
            lc.start()
            copies.append(lc)
        sends = []
        for f in range(1, NDEV):
            peer, pidx = _flip(x, y, c, f)
            for i in range(na):
                k = i * (NDEV - 1) + f - 1
                src = ins[i].at[pidx] if scatter else ins[i]
                cp = pltpu.make_async_remote_copy(
                    src_ref=src, dst_ref=outs[i].at[me], send_sem=send_sems.at[k], recv_sem=recv_sems.at[k],
                    device_id=peer, device_id_type=pl.DeviceIdType.MESH)
                cp.start()
                sends.append(cp)
        for f in range(1, NDEV):
            peer, pidx = _flip(x, y, c, f)
            for i in range(na):
                k = i * (NDEV - 1) + f - 1
                src = ins[i].at[pidx] if scatter else ins[i]
                pltpu.make_async_remote_copy(
                    src_ref=src, dst_ref=outs[i].at[pidx], send_sem=send_sems.at[k], recv_sem=recv_sems.at[k],
                    device_id=peer, device_id_type=pl.DeviceIdType.MESH).wait_recv()
        for cp in sends:
            cp.wait_send()
        for lc in copies:
            lc.wait()

    res = pl.pallas_call(
        body, name=name, out_shape=out_shape, in_specs=[_HBM_SPEC] * na,
        out_specs=[_HBM_SPEC] * na + [pl.BlockSpec(memory_space=pltpu.VMEM)],
        scratch_shapes=[pltpu.SemaphoreType.DMA((na * (NDEV - 1),)), pltpu.SemaphoreType.DMA((na * (NDEV - 1),)),
                        pltpu.SemaphoreType.DMA((na,))],
        compiler_params=pltpu.CompilerParams(has_side_effects=True),
    )(*arrays)
    return res[:na], res[na][0, 0]


_SEM_SPEC = pl.BlockSpec(memory_space=pltpu.SEMAPHORE)
_DATAFLOW = pltpu.SideEffectType.DATAFLOW_SIDE_EFFECTING


def _split_copies(srcs, lands, send_sems, recv_sems, scatter, arriving):
    x, y, c, me = _mesh_pos()
    copies = []
    for i in range(len(srcs)):
        for f in range(1, NDEV):
            peer, pidx = _flip(x, y, c, f)
            k = i * (NDEV - 1) + f - 1
            copies.append(pltpu.make_async_remote_copy(
                src_ref=srcs[i].at[pidx] if scatter else srcs[i], dst_ref=lands[i].at[pidx if arriving else me],
                send_sem=send_sems.at[k], recv_sem=recv_sems.at[k], device_id=peer,
                device_id_type=pl.DeviceIdType.MESH))
    return copies


def _exchange_start(srcs, lands, *, scatter, name):
    na = len(srcs)
    nsem = na * (NDEV - 1)

    def body(*refs):
        ins_src, ins_land = refs[:na], refs[na:2 * na]
        send_sems, recv_sems = refs[2 * na], refs[2 * na + 1]
        token = refs[-1]
        for cp in _split_copies(ins_src, ins_land, send_sems, recv_sems, scatter, False):
            cp.start()
        token[...] = jnp.zeros_like(token)

    thru = [pltpu.HBM(a.shape, a.dtype) for a in list(srcs) + list(lands)]
    res = pl.pallas_call(
        body, name=name,
        out_shape=(pltpu.SemaphoreType.DMA((nsem,)), pltpu.SemaphoreType.DMA((nsem,)), *thru,
                   jax.ShapeDtypeStruct((8, 128), F32)),
        in_specs=[_HBM_SPEC] * (2 * na),
        out_specs=(_SEM_SPEC, _SEM_SPEC, *([_HBM_SPEC] * (2 * na)), pl.BlockSpec(memory_space=pltpu.VMEM)),
        input_output_aliases={i: 2 + i for i in range(2 * na)},
        compiler_params=pltpu.CompilerParams(has_side_effects=_DATAFLOW),
    )(*[pltpu.with_memory_space_constraint(a, pltpu.HBM) for a in list(srcs) + list(lands)])
    send_sems, recv_sems = res[0], res[1]
    return send_sems, recv_sems, res[2:2 + na], res[2 + na:2 + 2 * na], res[-1][0, 0]


def _exchange_wait(send_sems, recv_sems, srcs, lands, after, *, scatter, name):
    na = len(srcs)

    def body(*refs):
        ins_src, ins_land = refs[:na], refs[na:2 * na]
        s_sems, r_sems = refs[2 * na], refs[2 * na + 1]
        for cp in _split_copies(ins_src, ins_land, s_sems, r_sems, scatter, False):
            cp.wait_send()
        for cp in _split_copies(ins_src, ins_land, s_sems, r_sems, scatter, True):
            cp.wait_recv()

    thru = [pltpu.HBM(a.shape, a.dtype) for a in list(srcs) + list(lands)]
    res = pl.pallas_call(
        body, name=name, out_shape=tuple(thru),
        in_specs=[_HBM_SPEC] * (2 * na) + [_SEM_SPEC, _SEM_SPEC, pl.BlockSpec(memory_space=pl.ANY)],
        out_specs=tuple([_HBM_SPEC] * (2 * na)),
        input_output_aliases={i: i for i in range(2 * na)},
        compiler_params=pltpu.CompilerParams(has_side_effects=_DATAFLOW),
    )(*srcs, *lands, send_sems, recv_sems, after)
    return res[na:]


def _landing(block, me):
    buf = lax.empty((NDEV,) + block.shape, block.dtype)
    return lax.dynamic_update_slice_in_dim(buf, block[None], me, axis=0)


def _seg_kw(nseg, n_ctx, tm):
    return dict(nseg=nseg, seg_blocks=(n_ctx // tm if nseg == 2 else 0))


def _ffn_fwd(tag, h, gpre, gpost, shift, scale, gate, w, *, nseg, n_ctx, tm):
    n = h.shape[0]
    kw = _seg_kw(nseg, n_ctx, tm)
    (u,) = _rowwise(tag + "_pre", _pre_fwd_fn, n, [h], [("full", gpre), ("seg", shift), ("seg", scale)],
                    [(D_MODEL, BF16)], tm=tm, **kw)
    if "early" in w:
        w.update(w.pop("early")(u))
    s, a, b = _mm_glu(u, w["win_t"], name=tag + "_glu")
    if "late" in w:
        w.update(w.pop("late")(s))
    y, ho = _mm_rows(s, w["wout"], functools.partial(_out_post_fn, 0.5), [h], [("full", gpost), ("seg", gate)],
                     [(D_MODEL, F32), (D_MODEL, F32)], name=tag + "_out", tk=FFN_DIM, n_ctx=n_ctx)
    return ho, dict(h=h, u=u, s=s, a=a, b=b, y=y)


def _ffn_bwd(tag, dho, sv, gpre, gpost, scale, gate, w, put, *, nseg, n_ctx, tm):
    n = dho.shape[0]
    kw = _seg_kw(nseg, n_ctx, tm)
    dy, dgate, dgpost = _rowwise(tag + "_postb", functools.partial(_post_bwd_fn, 0.5), n, [dho, sv["y"]],
                                 [("full", gpost), ("seg", gate)], [(D_MODEL, BF16)], [D_MODEL, D_MODEL], tm=tm, **kw)
    tok = put("w_out", _mm_tn(sv["s"], dy, name=tag + "_dwout", tm=1408, tn=1024, col_blocks=1))
    dp = _mm_glu_bwd(dy, w["wout"], sv["a"], sv["b"], name=tag + "_ds")
    tok2 = put("w_in", _mm_tn(dp, sv["u"], name=tag + "_dwin", tm=1408, tn=1024, col_blocks=1))
    for t in (tok, tok2):
        if t is not None:
            gpre = gpre + t
    dh, dshift, dscale, dgpre = _mm_rows(dp, w["win_t"], _pre_bwd_fn, [sv["h"], dho],
                                         [("full", gpre), ("seg", scale)], [(D_MODEL, F32)],
                                         [D_MODEL, D_MODEL, D_MODEL], name=tag + "_du", tk=FFN_DIM, n_ctx=n_ctx)
    return dh, None, dict(shift=dshift, scale=dscale, gate=dgate, gpre=dgpre, gpost=dgpost)


def _local_step(x, ctx, target, mods, norm_g, get_w, small, put_grad):
    t_len, n_ctx = x.shape[0], ctx.shape[0]
    n0 = t_len + n_ctx
    tm0 = _pick(n_ctx, 256, 8)
    tm1 = _pick(t_len, 256, 8)
    ncc = n_ctx // CHUNK
    g = {}

    def modrow(i, k, nseg):
        mc, mx = mods[i]
        if nseg == 2:
            return jnp.stack([mc[k], mx[k]])[:, None, :]
        return mx[k][None, None, :]

    pending = [None]

    def gvec(i, k):
        v = norm_g[i, k][None, :]
        if pending[0] is not None:
            v = v + pending[0]
            pending[0] = None
        return v

    xc = jnp.concatenate([ctx, x], axis=0)
    L0 = dict(nseg=2, n_ctx=n_ctx, tm=tm0)
    wts = dict(get_w("ffn00", xc))
    h1, sv_f01 = _ffn_fwd("l0f1", xc, gvec(0, 0), gvec(0, 1), modrow(0, 0, 2), modrow(0, 1, 2), modrow(0, 2, 2),
                          wts["ffn00"], **L0)
    kw0 = _seg_kw(2, n_ctx, tm0)
    (um0,) = _rowwise("l0m_pre", _pre_fwd_fn, n0, [h1], [("full", gvec(0, 2)), ("seg", modrow(0, 3, 2)),
                                                         ("seg", modrow(0, 4, 2))], [(D_MODEL, BF16)], tm=tm0, **kw0)
    wts.update(get_w("ssd", um0))
    win_ssd = wts["ssd_win_t"]
    nh = SSD_HEADS
    dt_blk = (SSD_INNER + SSD_CONV_DIM) // (2 * nh)
    z = _mm(um0, win_ssd, out_dtype=F32, name="ssd_z", rhs_t=True, n=SSD_INNER)
    xbc_pre = _mm(um0, win_ssd, out_dtype=F32, name="ssd_xbc", rhs_t=True, n=SSD_CONV_DIM,
                  b_off=(SSD_INNER // 1024, 0))
    dtr = _mm(um0, win_ssd, out_dtype=F32, name="ssd_dt", rhs_t=True, n=2 * nh, b_off=(dt_blk, 0))
    cpre, xbc = _conv_fwd(xbc_pre, small["conv_w8"], small["conv_b"], n_ctx=n_ctx, name="ssd_conv")
    nh = SSD_HEADS
    dt_dir = [dtr[:, :nh], dtr[:, nh:2 * nh]]
    dtT_dir = [d.T for d in dt_dir]
    bias_r = [small["dt_bias"][d][None, :] for d in range(2)]
    bias_c = [small["dt_bias"][d][:, None] for d in range(2)]
    alog_r = [small["a_log"][d][None, :] for d in range(2)]
    alog_c = [small["a_log"][d][:, None] for d in range(2)]
    ys, hss = [], []
    for d in range(2):
        yd, hsd = _ssd_scan_fwd(xbc, dt_dir[d], dtT_dir[d], bias_r[d], bias_c[d], alog_r[d], alog_c[d],
                                rev=(d == 1), n_ctx_chunks=ncc, name=f"ssd_scan{d}")
        ys.append(yd)
        hss.append(hsd)
    dvec = jnp.repeat(small["ssd_d"], SSD_HEAD_DIM)[None, :]
    ngv = small["ssd_norm_g"][None, :]
    gate_rows = [ys[0], ys[1], (xbc, SSD_INNER, 0, 0), z]
    lat = lambda r: (r[0], r[1], r[2], ncc) if isinstance(r, tuple) else (r, r.shape[1], 0, ncc)
    (yn,) = _rowwise("ssd_gate", _ssdgate_fwd_fn, t_len, [lat(r) for r in gate_rows],
                     [("full", dvec), ("full", ngv)], [(SSD_INNER, BF16)], tm=CHUNK)
    h1x = h1[n_ctx:]
    L1 = dict(nseg=1, n_ctx=0, tm=tm1)
    if "late" in wts:
        wts.update(wts.pop("late")(yn))
    yo0, h2 = _mm_rows(yn, wts["ssd_wout"], functools.partial(_out_post_fn, 1.0), [h1x],
                       [("full", gvec(0, 3)), ("seg", modrow(0, 5, 1))], [(D_MODEL, F32), (D_MODEL, F32)],
                       name="ssd_out", tk=SSD_INNER)
    wts.update(get_w("ffn01", h2))
    h3, sv_f02 = _ffn_fwd("l0f2", h2, gvec(0, 4), gvec(0, 5), modrow(0, 6, 1), modrow(0, 7, 1), modrow(0, 8, 1),
                          wts["ffn01"], **L1)

    wts.update(get_w("ffn10", h3))
    h4, sv_f11 = _ffn_fwd("l1f1", h3, gvec(1, 0), gvec(1, 1), modrow(1, 0, 1), modrow(1, 1, 1), modrow(1, 2, 1),
                          wts["ffn10"], **L1)
    (um1,) = _rowwise("l1m_pre", _pre_fwd_fn, t_len, [h4], [("full", gvec(1, 2)), ("seg", modrow(1, 3, 1)),
                                                            ("seg", modrow(1, 4, 1))], [(D_MODEL, BF16)], tm=tm1)
    wts.update(get_w("gm", um1))
    p1 = _mm(um1, wts["gm_win"], out_dtype=F32, name="gm_in")
    vg = small["gm_v_g"][None, :]
    vb = small["gm_v_b"][None, :]
    gu, gvn = _rowwise("gm_act", _gm_act_fwd_fn, t_len, [p1], [("full", vg), ("full", vb)],
                       [(GM_INNER, F32), (GM_INNER, BF16)], tm=128)
    ws_bf = small["gm_w_s"].astype(BF16)
    wst_bf = jnp.swapaxes(small["gm_w_s"], 1, 2).astype(BF16)
    bst = small["gm_b_s"].T
    tgm = _gm_spatial_fwd(gu, gvn, ws_bf, bst, name="gm_spatial")
    yo1, h5 = _mm_rows(tgm, wts["gm_wout"], functools.partial(_out_post_fn, 1.0), [h4],
                       [("full", gvec(1, 3)), ("seg", modrow(1, 5, 1))], [(D_MODEL, F32), (D_MODEL, F32)],
                       name="gm_out", tk=GM_INNER)
    wts.update(get_w("ffn11", h5))
    h6, sv_f12 = _ffn_fwd("l1f2", h5, gvec(1, 4), gvec(1, 5), modrow(1, 6, 1), modrow(1, 7, 1), modrow(1, 8, 1),
                          wts["ffn11"], **L1)

    dh, loss_parts = _rowwise("loss", _loss_fn, t_len, [h6, target], [], [(D_MODEL, F32)], [D_MODEL], tm=tm1)

    zero = jnp.zeros((D_MODEL,), F32)
    dmx = [[zero] * N_MOD for _ in range(2)]
    dmc = [[zero] * N_MOD for _ in range(2)]
    dng = [[zero] * 6 for _ in range(2)]

    def put_mod(i, k, acc):
        if acc.shape[0] == 2:
            dmc[i][k] = dmc[i][k] + acc[0, 0]
            dmx[i][k] = dmx[i][k] + acc[1, 0]
        else:
            dmx[i][k] = dmx[i][k] + acc[0, 0]

    def put_g(i, k, acc):
        dng[i][k] = dng[i][k] + jnp.sum(acc[:, 0], axis=0)

    def ffn_back(tag, i, j, dho, sv, w, lay):
        nseg = lay["nseg"]
        base = 0 if j == 0 else 6
        gi = 0 if j == 0 else 4
        dh_in, pending[0], s = _ffn_bwd(tag, dho, sv, gvec(i, gi), gvec(i, gi + 1), modrow(i, base + 1, nseg),
                                        modrow(i, base + 2, nseg), w, functools.partial(put_grad, f"ffn{i}{j}"), **lay)
        put_mod(i, base, s["shift"])
        put_mod(i, base + 1, s["scale"])
        put_mod(i, base + 2, s["gate"])
        put_g(i, gi, s["gpre"])
        put_g(i, gi + 1, s["gpost"])
        return dh_in

    dh = ffn_back("l1f2", 1, 1, dh, sv_f12, wts["ffn11"], L1)
    dyo, dgate, dgp = _rowwise("l1m_postb", functools.partial(_post_bwd_fn, 1.0), t_len, [dh, yo1],
                               [("full", gvec(1, 3)), ("seg", modrow(1, 5, 1))], [(D_MODEL, BF16)],
                               [D_MODEL, D_MODEL], tm=tm1)
    put_mod(1, 5, dgate)
    put_g(1, 3, dgp)
    put_grad("gm", "w_out", _mm_tn(tgm, dyo, name="gm_dwout", tn=1024, col_blocks=1))
    dtg = _mm(dyo, wts["gm_wout"], out_dtype=F32, name="gm_dt", rhs_t=True)
    dgu, dgvn, dws, dbst = _gm_spatial_bwd(dtg, gu, gvn, ws_bf, wst_bf, bst, name="gm_spatialb")
    g["gm_w_s"] = dws
    g["gm_b_s"] = dbst.T
    dp1, dvg, dvb = _rowwise("gm_actb", _gm_act_bwd_fn, t_len, [p1, dgu, dgvn], [("full", vg)],
                             [(2 * GM_INNER, BF16)], [GM_INNER, GM_INNER], tm=128)
    g["gm_v_g"] = dvg[0, 0]
    g["gm_v_b"] = dvb[0, 0]
    pending[0] = put_grad("gm", "w_in", _mm_tn(um1, dp1, name="gm_dwin", tm=1024, col_blocks=NDEV))
    dh, dsh, dsc, dgp = _mm_rows(dp1, wts["gm_win"], _pre_bwd_fn, [h4, dh],
                                 [("full", gvec(1, 2)), ("seg", modrow(1, 4, 1))], [(D_MODEL, F32)],
                                 [D_MODEL, D_MODEL, D_MODEL], name="gm_dum", tk=1024, rhs_t=True)
    put_mod(1, 3, dsh)
    put_mod(1, 4, dsc)
    put_g(1, 2, dgp)
    dh = ffn_back("l1f1", 1, 0, dh, sv_f11, wts["ffn10"], L1)

    dh = ffn_back("l0f2", 0, 1, dh, sv_f02, wts["ffn01"], L1)
    dyo, dgate, dgp = _rowwise("l0m_postb", functools.partial(_post_bwd_fn, 1.0), t_len, [dh, yo0],
                               [("full", gvec(0, 3)), ("seg", modrow(0, 5, 1))], [(D_MODEL, BF16)],
                               [D_MODEL, D_MODEL], tm=tm1)
    put_mod(0, 5, dgate)
    put_g(0, 3, dgp)
    tok = put_grad("ssd", "w_out", _mm_tn(yn, dyo, name="ssd_dwout", tn=1024, col_blocks=1))
    dyn = _mm(dyo, wts["ssd_wout"], out_dtype=F32, name="ssd_dyn", rhs_t=True)
    dy_ssd, dz, dngv, ddv = _rowwise("ssd_gateb", _ssdgate_bwd_fn, n0, [(dyn, SSD_INNER, 0, -ncc)] + gate_rows,
                                     [("full", dvec), ("full", ngv if tok is None else ngv + tok)],
                                     [(SSD_INNER, F32), (SSD_INNER, BF16)],
                                     [SSD_INNER, SSD_INNER], tm=128)
    g["ssd_norm_g"] = dngv[0, 0]
    g["ssd_D"] = jnp.sum(ddv[0, 0].reshape(SSD_HEADS, SSD_HEAD_DIM), axis=1)
    dxbcs, ddts, dalogs, dbiases = [], [], [], []
    for d in range(2):
        dxd, ddtd, dal, dbi = _ssd_scan_bwd(dy_ssd, xbc, hss[d], dt_dir[d], dtT_dir[d], bias_r[d], bias_c[d],
                                            alog_r[d], alog_c[d], dvec, rev=(d == 1), n_ctx_chunks=ncc,
                                            direct=(d == 0), name=f"ssd_scanb{d}")
        dxbcs.append(dxd)
        ddts.append(ddtd)
        dalogs.append(dal[0])
        dbiases.append(dbi[0])
    g["ssd_A_log"] = jnp.stack(dalogs)
    g["ssd_dt_bias"] = jnp.stack(dbiases)
    dxbc_pre, dcw8, dcb = _conv_bwd(dxbcs[0], dxbcs[1], cpre, xbc_pre, small["conv_w8"], n_ctx=n_ctx, name="ssd_convb")
    g["ssd_conv_w"] = dcw8[:SSD_CONV]
    g["ssd_conv_b"] = dcb[0]
    ddt_bf = jnp.concatenate([ddts[0], ddts[1]], axis=1).astype(BF16)
    n_in = SSD_INNER + SSD_CONV_DIM + 2 * nh
    dw_t = _mm_tn(dz, um0, name="ssd_dwz", col_blocks=1, stack=(n_in, 0, None))
    dw_t = _mm_tn(dxbc_pre, um0, name="ssd_dwxbc", col_blocks=1, stack=(n_in, SSD_INNER, dw_t))
    dw_t = _mm_tn(ddt_bf, um0, name="ssd_dwdt", col_blocks=1, stack=(n_in, SSD_INNER + SSD_CONV_DIM, dw_t))
    pending[0] = put_grad("ssd", "w_in", dw_t)
    dum0 = _mm(dz, win_ssd, out_dtype=F32, name="ssd_dum_z", tk=1024, n=D_MODEL)
    dum0 = _mm(dxbc_pre, win_ssd, out_dtype=F32, name="ssd_dum_x", tk=1024, n=D_MODEL,
               b_off=(SSD_INNER // 1024, 0), add=dum0)
    dum0 = _mm(ddt_bf, win_ssd, out_dtype=F32, name="ssd_dum_dt", tk=2 * nh, n=D_MODEL, b_off=(dt_blk, 0), add=dum0)
    dh0, dsh, dsc, dgp = _rowwise("l0m_preb", _pre_bwd_fn, n0, [dum0, h1, (dh, D_MODEL, 0, -(n_ctx // tm0))],
                                  [("full", gvec(0, 2)), ("seg", modrow(0, 4, 2))], [(D_MODEL, F32)],
                                  [D_MODEL, D_MODEL, D_MODEL], tm=tm0, **kw0)
    put_mod(0, 3, dsh)
    put_mod(0, 4, dsc)
    put_g(0, 2, dgp)
    dh0 = ffn_back("l0f1", 0, 0, dh0, sv_f01, wts["ffn00"], L0)
    grad_x = dh0[n_ctx:]
    g["norm_g"] = jnp.stack([jnp.stack(r) for r in dng])
    g["dmx"] = jnp.stack([jnp.concatenate(r) for r in dmx])
    g["dmc"] = jnp.stack([jnp.concatenate(r) for r in dmc])
    return loss_parts[0], grad_x, g


GROUPS = ("ffn00", "ssd", "ffn01", "ffn10", "gm", "ffn11")


TRANSPOSED_IN = ("ffn", "ssd")


def _is_transposed(group):
    return group.startswith(TRANSPOSED_IN)


def _mats_in(group, win_l):
    if _is_transposed(group):
        return {("win_t" if group.startswith("ffn") else group + "_win_t"): win_l.reshape(-1, win_l.shape[2])}
    return {group + "_win": win_l}


def _mats_out(group, wout_l):
    pre = "" if group.startswith("ffn") else group + "_"
    return {pre + "wout": wout_l.reshape(-1, wout_l.shape[2])}


def _group_mats(group, lands):
    m = {**_mats_in(group, lands[0]), **_mats_out(group, lands[1])}
    return {group: m} if group.startswith("ffn") else m


def _grad_blocks(which, grad):
    if grad.ndim == 3:
        return grad if grad.shape[0] == NDEV else grad.reshape(NDEV, grad.shape[1] // NDEV, grad.shape[2])
    if which == "w_in":
        k, n = grad.shape
        return jnp.transpose(grad.reshape(k, NDEV, n // NDEV), (1, 0, 2)).astype(BF16)
    return grad.reshape(NDEV, grad.shape[0] // NDEV, grad.shape[1]).astype(BF16)


def kernel(x, c, ctx, c_ctx, ada_w, ada_b, norm_g, ffn_w_in, ffn_w_out, ssd_w_in, ssd_conv_w, ssd_conv_b, ssd_dt_bias, ssd_A_log, ssd_D, ssd_norm_g, ssd_w_out, gm_w_in, gm_v_g, gm_v_b, gm_w_s, gm_b_s, gm_w_out, loss_target, m_c_ctx, m_ada_w, m_ada_b, m_norm_g, m_ffn_w_in, m_ffn_w_out, m_ssd_w_in, m_ssd_conv_w, m_ssd_conv_b, m_ssd_dt_bias, m_ssd_A_log, m_ssd_D, m_ssd_norm_g, m_ssd_w_out, m_gm_w_in, m_gm_v_g, m_gm_v_b, m_gm_w_s, m_gm_b_s, m_gm_w_out, v_c_ctx, v_ada_w, v_ada_b, v_norm_g, v_ffn_w_in, v_ffn_w_out, v_ssd_w_in, v_ssd_conv_w, v_ssd_conv_b, v_ssd_dt_bias, v_ssd_A_log, v_ssd_D, v_ssd_norm_g, v_ssd_w_out, v_gm_w_in, v_gm_v_g, v_gm_v_b, v_gm_w_s, v_gm_b_s, v_gm_w_out):
    me = 4 * lax.axis_index("x") + 2 * lax.axis_index("y") + lax.axis_index("c")
    d = D_MODEL
    ncol = N_MOD * d // NDEV

    small_pack = jnp.concatenate([c.reshape(-1), norm_g.reshape(-1), ssd_conv_w.reshape(-1),
                                  gm_v_g.reshape(-1), gm_v_b.reshape(-1)])[None, :]
    (sp,), _ = _exchange([small_pack], scatter=False, name="gather_small")
    sp = sp[:, 0]
    o = 0
    c_all = sp[:, o:o + d]; o += d
    ng_all = sp[:, o:o + 2 * 6 * 128].reshape(NDEV, 2, 6, 128); o += 2 * 6 * 128
    cw_all = sp[:, o:o + SSD_CONV * 512].reshape(NDEV, SSD_CONV, 512); o += SSD_CONV * 512
    vg_all = sp[:, o:o + 256]; o += 256
    vb_all = sp[:, o:o + 256]; o += 256
    norm_g_full = jnp.transpose(ng_all, (1, 2, 0, 3)).reshape(2, 6, d)
    conv_w_full = jnp.transpose(cw_all, (1, 0, 2)).reshape(SSD_CONV, SSD_CONV_DIM)
    gm_v_g_full = vg_all.reshape(-1)
    gm_v_b_full = vb_all.reshape(-1)

    c16 = jnp.concatenate([c_all, jnp.broadcast_to(c_ctx[None, :], (NDEV, d))], axis=0)
    ada_b_loc = lax.dynamic_slice_in_dim(ada_b, me * ncol, ncol, axis=1)
    mods_loc = jnp.stack([_mm_f32(c16, ada_w[i], name=f"ada_mod{i}", silu_a=True, bias=ada_b_loc[i][None, :])
                          for i in range(2)])
    (mods_all,), mods_done = _exchange([mods_loc], scatter=False, name="gather_mods")

    tr = lambda a: jnp.swapaxes(a, -1, -2)
    shard = {"ssd": (tr(ssd_w_in)[0], ssd_w_out[0]), "gm": (gm_w_in[0], gm_w_out[0])}
    for i in range(2):
        for j in range(2):
            shard[f"ffn{i}{j}"] = (tr(ffn_w_in)[i, j], ffn_w_out[i, j])
    apart = GROUPS[:2]
    units = []
    for grp in GROUPS:
        units += [(grp + "_in", grp, (0,)), (grp + "_out", grp, (1,))] if grp in apart else [(grp, grp, (0, 1))]
    gathers = {}
    started = mods_done
    for unit, grp, idx in units:
        srcs = [(shard[grp][k] + started).astype(BF16) for k in idx]
        st = _exchange_start(srcs, [_landing(s, me) for s in srcs], scatter=False, name="gather_start_" + unit)
        gathers[unit] = st[:4]
        started = st[4]

    def fetch(unit, after):
        return _exchange_wait(*gathers[unit], after, scatter=False, name="gather_wait_" + unit)

    def get_w(grp, after):
        if grp not in apart:
            return _group_mats(grp, fetch(grp, after))
        early = lambda later: _mats_in(grp, fetch(grp + "_in", later)[0])
        late = lambda later: _mats_out(grp, fetch(grp + "_out", later)[0])
        if grp.startswith("ffn"):
            return {grp: dict(early=early, late=late)}
        return dict(early(after), late=late)

    scatters = {}
    held = {}

    def put_grad(grp, which, grad):
        if grp in apart:
            unit, blocks = grp + "_" + which[2:], [_grad_blocks(which, grad)]
        else:
            held[grp, which] = _grad_blocks(which, grad)
            if (grp, "w_in") not in held or (grp, "w_out") not in held:
                return None
            unit, blocks = grp, [held[grp, "w_in"], held[grp, "w_out"]]
        own = [lax.dynamic_index_in_dim(b, me, axis=0, keepdims=False) for b in blocks]
        st = _exchange_start(blocks, [_landing(o_, me) for o_ in own], scatter=True, name="scatter_start_" + unit)
        scatters[unit] = st[:4]
        return st[4]

    mods_rows = jnp.transpose(mods_all, (1, 2, 0, 3)).reshape(2, 2 * NDEV, N_MOD * d) + started
    mx = lax.dynamic_index_in_dim(mods_rows, me, axis=1, keepdims=False).reshape(2, N_MOD, d)
    mc = mods_rows[:, NDEV].reshape(2, N_MOD, d)
    mods = [(mc[i], mx[i]) for i in range(2)]

    small = dict(conv_w8=jnp.pad(conv_w_full, ((0, 8 - SSD_CONV), (0, 0))), conv_b=ssd_conv_b, dt_bias=ssd_dt_bias[0],
                 a_log=ssd_A_log[0], ssd_d=ssd_D[0], ssd_norm_g=ssd_norm_g[0], gm_v_g=gm_v_g_full,
                 gm_v_b=gm_v_b_full, gm_w_s=gm_w_s[0], gm_b_s=gm_b_s[0])
    loss_parts, grad_x, g = _local_step(x[0], ctx[0], loss_target[0], mods, norm_g_full, get_w, small, put_grad)
    g["loss"] = (0.5 / d * jnp.sum(loss_parts)).reshape(1)

    whole = {"ffn_w_in": (tr(ffn_w_in), tr(m_ffn_w_in), tr(v_ffn_w_in)), "ffn_w_out": (ffn_w_out, m_ffn_w_out, v_ffn_w_out),
             "ssd_w_in": (tr(ssd_w_in), tr(m_ssd_w_in), tr(v_ssd_w_in)), "ssd_w_out": (ssd_w_out, m_ssd_w_out, v_ssd_w_out),
             "gm_w_in": (gm_w_in, m_gm_w_in, v_gm_w_in), "gm_w_out": (gm_w_out, m_gm_w_out, v_gm_w_out)}
    res = {}

    def update_units(some, after):
        for unit, grp, idx in some:
            parts = _exchange_wait(*scatters[unit], after, scatter=True, name="scatter_wait_" + unit)
            for k, p in zip(idx, parts):
                which = ("in", "out")[k]
                nm = ("ffn" if grp.startswith("ffn") else grp) + "_w_" + which
                sel = (int(grp[3]), int(grp[4])) if grp.startswith("ffn") else (0,)
                res[nm] = _adamw(p, *whole[nm], name=f"adamw_{grp}_{which}", sel=sel, into=res.get(nm))
                after = res[nm][0]
        return after

    sg_names = ["dmx", "dmc", "norm_g", "ssd_conv_w", "ssd_conv_b", "ssd_dt_bias", "ssd_A_log", "ssd_D", "ssd_norm_g",
                "gm_v_g", "gm_v_b", "gm_w_s", "gm_b_s", "loss"]
    sg_shapes = [g[n].shape for n in sg_names]
    flat = jnp.concatenate([g[n].reshape(-1) for n in sg_names])
    npack = flat.shape[0]
    pad = (-npack) % 1024
    flat = jnp.pad(flat, (0, pad)).reshape(-1, 128)
    sg_start = _exchange_start([flat], [_landing(flat, me)], scatter=False, name="small_grads_start")
    by_send = list(reversed(units))
    update_units(by_send[:4], jnp.stack([sg_start[4], grad_x[0, 0]]))
    early_done = jnp.stack([res[nm][0].reshape(-1)[-1] for nm in sorted(res)])
    (sg_all,) = _exchange_wait(*sg_start[:4], early_done, scatter=False, name="small_grads_wait")
    sg_sum = _sum_slots(sg_all, name="sum_small_grads").reshape(-1)[:npack]
    update_units(by_send[4:], sg_sum)
    sums = {}
    o = 0
    for n, shp in zip(sg_names, sg_shapes):
        sz = math.prod(shp)
        sums[n] = sg_sum[o:o + sz].reshape(shp)
        o += sz
    loss = sums["loss"][0]
    per_dev = sg_all.reshape(NDEV, -1)
    dmx_all =per_dev[:, :2 * N_MOD * d].reshape(NDEV, 2, N_MOD * d)
    dmc_all = per_dev[:, 2 * N_MOD * d:4 * N_MOD * d].reshape(NDEV, 2, N_MOD * d)

    (s16,) = _rowwise("ada_silu", lambda cc: ((_silu(cc),), ()), 2 * NDEV, [c16], [], [(d, F32)], tm=2 * NDEV)
    s16_t = s16.T
    g_ada_w, dcc_parts = [], []
    for i in range(2):
        rhs = jnp.concatenate([lax.dynamic_slice_in_dim(dmx_all[:, i], me * ncol, ncol, axis=1),
                               lax.dynamic_slice_in_dim(dmc_all[:, i], me * ncol, ncol, axis=1)], axis=0)
        g_ada_w.append(_mm_f32(s16_t, rhs, name=f"ada_dw{i}"))
        dmc_loc = lax.dynamic_slice_in_dim(sums["dmc"][i], me * ncol, ncol, axis=0)
        rhs_c = jnp.zeros((ncol, 128), F32).at[:, 0].set(dmc_loc)
        dcc_parts.append(_mm_f32(ada_w[i], rhs_c, name=f"ada_dcc{i}")[:, 0])
    g_ada_w = jnp.stack(g_ada_w)
    dcc_part = (dcc_parts[0] + dcc_parts[1]).reshape(8, 128)
    (dcc_all,), _ = _exchange([dcc_part], scatter=False, name="gather_dcc")
    g_c_ctx = _sum_slots(dcc_all, name="sum_dcc", scale_by=c_ctx.reshape(8, 128)).reshape(d)
    g_ada_b = sums["dmx"] + sums["dmc"]

    outs = _adamw(g_ada_w.reshape(1, -1, ncol), ada_w.reshape(-1, ncol), m_ada_w.reshape(-1, ncol),
                  v_ada_w.reshape(-1, ncol), name="adamw_ada_w")
    res["ada_w"] = [o_.reshape(ada_w.shape) for o_ in outs]

    loc = lambda a, ax, n: lax.dynamic_slice_in_dim(a, me * n, n, axis=ax)
    small_g = dict(c_ctx=g_c_ctx, ada_b=g_ada_b, norm_g=loc(sums["norm_g"], 2, 128),
                   ssd_conv_w=loc(sums["ssd_conv_w"], 1, 512)[None], ssd_conv_b=sums["ssd_conv_b"][None],
                   ssd_dt_bias=sums["ssd_dt_bias"][None], ssd_A_log=sums["ssd_A_log"][None], ssd_D=sums["ssd_D"][None],
                   ssd_norm_g=sums["ssd_norm_g"][None], gm_v_g=loc(sums["gm_v_g"], 0, 256)[None],
                   gm_v_b=loc(sums["gm_v_b"], 0, 256)[None], gm_w_s=sums["gm_w_s"][None], gm_b_s=sums["gm_b_s"][None])
    small_w = dict(c_ctx=(c_ctx, m_c_ctx, v_c_ctx), ada_b=(ada_b, m_ada_b, v_ada_b), norm_g=(norm_g, m_norm_g, v_norm_g),
                   ssd_conv_w=(ssd_conv_w, m_ssd_conv_w, v_ssd_conv_w), ssd_conv_b=(ssd_conv_b, m_ssd_conv_b, v_ssd_conv_b),
                   ssd_dt_bias=(ssd_dt_bias, m_ssd_dt_bias, v_ssd_dt_bias), ssd_A_log=(ssd_A_log, m_ssd_A_log, v_ssd_A_log),
                   ssd_D=(ssd_D, m_ssd_D, v_ssd_D), ssd_norm_g=(ssd_norm_g, m_ssd_norm_g, v_ssd_norm_g),
                   gm_v_g=(gm_v_g, m_gm_v_g, v_gm_v_g), gm_v_b=(gm_v_b, m_gm_v_b, v_gm_v_b),
                   gm_w_s=(gm_w_s, m_gm_w_s, v_gm_w_s), gm_b_s=(gm_b_s, m_gm_b_s, v_gm_b_s))
    sn = list(small_w)

    def pack(arrs):
        f = jnp.concatenate([a.reshape(-1) for a in arrs])
        return jnp.pad(f, (0, (-f.shape[0]) % (256 * 128))).reshape(-1, 128)

    pg = pack([small_g[n].reshape(small_w[n][0].shape) for n in sn])
    outs = _adamw(pg[None], pack([small_w[n][0] for n in sn]), pack([small_w[n][1] for n in sn]),
                  pack([small_w[n][2] for n in sn]), name="adamw_small")
    flat_outs = [o_.reshape(-1) for o_ in outs]
    o = 0
    for n in sn:
        shp = small_w[n][0].shape
        sz = math.prod(shp)
        res[n] = [fo[o:o + sz].reshape(shp) for fo in flat_outs]
        o += sz

    order = ["c_ctx", "ada_w", "ada_b", "norm_g", "ffn_w_in", "ffn_w_out", "ssd_w_in", "ssd_conv_w", "ssd_conv_b",
             "ssd_dt_bias", "ssd_A_log", "ssd_D", "ssd_norm_g", "ssd_w_out", "gm_w_in", "gm_v_g", "gm_v_b", "gm_w_s",
             "gm_b_s", "gm_w_out"]
    for nm in ("ffn_w_in", "ssd_w_in"):
        res[nm] = [tr(a) for a in res[nm]]
    result = [loss, grad_x[None]]
    for k in range(4):
        result += [res[n][k] for n in order]
    return tuple(result)
```

```python
import functools
import math

import jax
import jax.numpy as jnp
from jax import lax
from jax.experimental import pallas as pl
from jax.experimental.pallas import tpu as pltpu

F32 = jnp.float32
BF16 = jnp.bfloat16

NDEV = 8
D_MODEL = 1024
FFN_DIM = 2816
N_MOD = 9
EPS = 1e-6
SSD_INNER = 2048
SSD_HEADS = 32
SSD_HEAD_DIM = 64
SSD_GROUPS = 8
SSD_HPG = 4
SSD_STATE = 128
SSD_CONV = 5
SSD_CONV_DIM = 4096
CHUNK = 128
GM_INNER = 2048
GM_GROUPS = 8
GM_GROUP_DIM = 256
ADAM_LR = 0.001
ADAM_B1 = 0.9
ADAM_B2 = 0.999
ADAM_EPS = 1e-08
ADAM_WD = 0.01
ADAM_STEP = 10
NEG_BIG = -1e30
VMEM_LIMIT_BYTES = 56 * 1024 * 1024
HI = lax.Precision.HIGHEST


def _params(*sem):
    return pltpu.CompilerParams(dimension_semantics=sem, vmem_limit_bytes=VMEM_LIMIT_BYTES)


def _pick(n, target, mult=16):
    if n <= target:
        return n
    for t in range(target - target % mult, 0, -mult):
        if n % t == 0:
            return t
    raise ValueError((n, target, mult))


def _sig(x):
    return 0.5 * jnp.tanh(0.5 * x) + 0.5


def _silu(x):
    return x * _sig(x)


def _dsilu(x):
    s = _sig(x)
    return s * (1.0 + x * (1.0 - s))


_GELU_C = math.sqrt(2.0 / math.pi)


def _gelu(x):
    return 0.5 * x * (1.0 + jnp.tanh(_GELU_C * (x + 0.044715 * x * x * x)))


def _gelu_and_grad(x):
    x2 = x * x
    t = jnp.tanh(_GELU_C * (x + 0.044715 * x2 * x))
    half = 0.5 * (1.0 + t)
    return x * half, half + 0.5 * x * (1.0 - t * t) * _GELU_C * (1.0 + 3.0 * 0.044715 * x2)


def _dgelu(x):
    return _gelu_and_grad(x)[1]


def _softplus(x):
    return jnp.maximum(x, 0.0) + jnp.log1p(jnp.exp(-jnp.abs(x)))


def _sum0(v):
    return jnp.sum(v, axis=0, keepdims=True)


def _rms(h):
    r = lax.rsqrt(jnp.mean(h * h, axis=-1, keepdims=True) + EPS)
    return h * r, r


def _dot(a, b, dims=((1,), (0,)), precision=None):
    return lax.dot_general(a, b, (dims, ((), ())), preferred_element_type=F32, precision=precision)


_NT = ((1,), (1,))
_TN = ((0,), (0,))


def _rowwise(name, fn, n_rows, rows, consts, outs, accs=(), *, tm, nseg=1, seg_blocks=0):
    assert n_rows % tm == 0
    if nseg == 2:
        assert seg_blocks > 0
        seg = lambda i: jnp.where(i < seg_blocks, 0, 1)
    else:
        seg = lambda i: 0
    in_specs, args, lacking = [], [], []
    for r in rows:
        arr, width, cb, off = r if isinstance(r, tuple) else (r, r.shape[1], 0, 0)
        in_specs.append(pl.BlockSpec((tm, width), lambda i, cb=cb, off=off: (jnp.maximum(i + off, 0), cb)))
        args.append(arr)
        lacking.append(-off if off < 0 else 0)
    for kind, arr in consts:
        if kind == "seg":
            assert arr.shape[0] == nseg and arr.shape[1] == 1, arr.shape
            in_specs.append(pl.BlockSpec((None, 1, arr.shape[2]), lambda i: (seg(i), 0, 0)))
        else:
            in_specs.append(pl.BlockSpec(arr.shape, lambda i: (0, 0)))
        args.append(arr)
    out_shape = [jax.ShapeDtypeStruct((n_rows, w), dt) for w, dt in outs]
    out_specs = [pl.BlockSpec((tm, w), lambda i: (i, 0)) for w, _ in outs]
    out_shape += [jax.ShapeDtypeStruct((nseg, 1, w), F32) for w in accs]
    out_specs += [pl.BlockSpec((None, 1, w), lambda i: (seg(i), 0, 0)) for w in accs]
    n_in, n_out, n_acc = len(args), len(outs), len(accs)

    def kern(*refs):
        i = pl.program_id(0)
        ins = [r[...] for r in refs[:n_in]]
        for k, lack in enumerate(lacking):
            if lack:
                ins[k] = jnp.where(i >= lack, ins[k], jnp.zeros_like(ins[k]))
        res, terms = fn(*ins)
        for ref, v in zip(refs[n_in:n_in + n_out], res):
            ref[...] = v.astype(ref.dtype)
        if n_acc:
            sums = [_sum0(v) for v in terms]
            first = (i == 0) | (i == seg_blocks) if nseg == 2 else (i == 0)
            acc_refs = refs[n_in + n_out:]

            @pl.when(first)
            def _():
                for ref, v in zip(acc_refs, sums):
                    ref[...] = v

            @pl.when(jnp.logical_not(first))
            def _():
                for ref, v in zip(acc_refs, sums):
                    ref[...] += v

    res = pl.pallas_call(
        kern, name=name, grid=(n_rows // tm,), in_specs=in_specs, out_specs=out_specs, out_shape=out_shape,
        compiler_params=_params("arbitrary"),
    )(*args)
    return res


def _pre_fwd_fn(h, g, shift, scale):
    hh, _ = _rms(h)
    return (hh * g * (1.0 + scale) + shift,), ()


def _pre_bwd_fn(du, h, dres, g, scale):
    hh, r = _rms(h)
    n = hh * g
    dn = du * (1.0 + scale)
    dhh = dn * g
    dh = dres + r * (dhh - hh * jnp.mean(dhh * hh, axis=-1, keepdims=True))
    return (dh,), (du, du * n, dn * hh)


def _post_fwd_fn(weight, h, y, g, gate):
    yh, _ = _rms(y)
    return (h + weight * gate * (yh * g),), ()


def _out_post_fn(weight, y, h, g, gate):
    return (y,) + _post_fwd_fn(weight, h, y, g, gate)[0], ()


def _post_bwd_fn(weight, dh, y, g, gate):
    yh, r = _rms(y)
    dr = dh * weight
    dyh = dr * gate * g
    dy = r * (dyh - yh * jnp.mean(dyh * yh, axis=-1, keepdims=True))
    return (dy,), (dr * yh * g, dr * gate * yh)


def _glu_bwd_fn(ds, a, b):
    a = a.astype(F32)
    b = b.astype(F32)
    sg = _sig(a)
    da = ds * b * (sg * (1.0 + a * (1.0 - sg)))
    db = ds * (a * sg)
    return (jnp.concatenate([da, db], axis=1),), ()


def _loss_fn(y, t):
    diff = y - t
    return (diff * (1.0 / D_MODEL),), (diff * diff,)


def _ssd_y(yf, yb, xs, z, dvec):
    y = yf + yb + dvec * xs
    return y, y * _silu(z)


def _ssdgate_fwd_fn(yf, yb, xs, z, dvec, ng):
    _, yg = _ssd_y(yf, yb, xs, z, dvec)
    parts = []
    for g in range(SSD_GROUPS):
        sl = slice(g * 256, (g + 1) * 256)
        parts.append(_rms(yg[:, sl])[0])
    return (jnp.concatenate(parts, axis=1) * ng,), ()


def _ssdgate_bwd_fn(dyn, yf, yb, xs, z, dvec, ng):
    y, yg = _ssd_y(yf, yb, xs, z, dvec)
    dyg_parts, ygh_parts = [], []
    for g in range(SSD_GROUPS):
        sl = slice(g * 256, (g + 1) * 256)
        ygh, r = _rms(yg[:, sl])
        d = dyn[:, sl] * ng[:, sl]
        dyg_parts.append(r * (d - ygh * jnp.mean(d * ygh, axis=-1, keepdims=True)))
        ygh_parts.append(ygh)
    dyg = jnp.concatenate(dyg_parts, axis=1)
    ygh = jnp.concatenate(ygh_parts, axis=1)
    dy = dyg * _silu(z)
    dz = dyg * y * _dsilu(z)
    return (dy, dz), (dyn * ygh, dy * xs)


def _ln_stats(v):
    mu = jnp.mean(v, axis=-1, keepdims=True)
    vc = v - mu
    r = lax.rsqrt(jnp.mean(vc * vc, axis=-1, keepdims=True) + EPS)
    return vc * r, r


def _gm_act_fwd_fn(p, vg, vb):
    gu = _gelu(p[:, :GM_INNER])
    gvh, _ = _ln_stats(_gelu(p[:, GM_INNER:]))
    return (gu, gvh * vg + vb), ()


def _gm_act_bwd_fn(p, dgu, dgvn, vg):
    pu = p[:, :GM_INNER]
    pv = p[:, GM_INNER:]
    gv, dgelu_v = _gelu_and_grad(pv)
    gvh, r = _ln_stats(gv)
    dgvh = dgvn * vg
    dgv = r * (dgvh - jnp.mean(dgvh, axis=-1, keepdims=True) - gvh * jnp.mean(dgvh * gvh, axis=-1, keepdims=True))
    dp = jnp.concatenate([dgu * _dgelu(pu), dgv * dgelu_v], axis=1)
    return (dp,), (dgvn * gvh, dgvn)


def _mm(a, b, *, out_dtype, name, tm=1088, tn=1024, tk=1408, add=None, rhs_t=False, n=None, b_off=(0, 0)):
    m, k = a.shape
    col_blocked = b.ndim == 3
    if col_blocked:
        assert not rhs_t and n is None and b.shape[1] == k
        n, tn = b.shape[0] * b.shape[2], b.shape[2]
    elif n is None:
        n, k2 = b.shape if rhs_t else b.shape[::-1]
        assert k == k2
    tm, tn, tk = _pick(m, tm), _pick(n, tn, 128), _pick(k, tk, 128)
    o0, o1 = b_off
    nk = k // tk
    dims = _NT if rhs_t else ((1,), (0,))

    def kern(*refs):
        a_ref, b_ref = refs[:2]
        add_ref = refs[2] if add is not None else None
        o_ref = refs[3] if add is not None else refs[2]

        def finish(r):
            if add is not None:
                r = r + add_ref[...]
            o_ref[...] = r.astype(o_ref.dtype)

        p = _dot(a_ref[...], b_ref[...], dims)
        if nk == 1:
            finish(p)
            return
        acc_ref = refs[-1]
        kk = pl.program_id(2)

        @pl.when(kk == 0)
        def _():
            acc_ref[...] = p

        @pl.when((kk > 0) & (kk < nk - 1))
        def _():
            acc_ref[...] += p

        @pl.when(kk == nk - 1)
        def _():
            finish(acc_ref[...] + p)

    if col_blocked:
        b_spec = pl.BlockSpec((None, tk, tn), lambda i, j, kk: (j, kk, 0))
    elif rhs_t:
        b_spec = pl.BlockSpec((tn, tk), lambda i, j, kk: (j + o0, kk + o1))
    else:
        b_spec = pl.BlockSpec((tk, tn), lambda i, j, kk: (kk + o0, j + o1))
    in_specs = [pl.BlockSpec((tm, tk), lambda i, j, kk: (i, kk)), b_spec]
    args = [a, b]
    if add is not None:
        in_specs.append(pl.BlockSpec((tm, tn), lambda i, j, kk: (i, j)))
        args.append(add)
    return pl.pallas_call(
        kern, name=name, grid=(m // tm, n // tn, nk), in_specs=in_specs,
        out_specs=pl.BlockSpec((tm, tn), lambda i, j, kk: (i, j)),
        out_shape=jax.ShapeDtypeStruct((m, n), out_dtype),
        scratch_shapes=[pltpu.VMEM((tm, tn), F32)] if nk > 1 else [],
        compiler_params=_params("parallel", "parallel", "arbitrary"),
    )(*args)


def _mm_rows(a, b, fn, rows, consts, outs, accs=(), *, name, tm=544, tk=1408, rhs_t=False, n_ctx=0):
    halves = a.ndim == 3
    m, k = (a.shape[1], 2 * a.shape[2]) if halves else a.shape
    col_blocked = b.ndim == 3
    kb, nb = 1, None
    if col_blocked:
        assert rhs_t and b.shape[0] * b.shape[2] == k
        n, nb = b.shape[1], b.shape[2]
        kb = max(1, tk // nb)
        assert b.shape[0] % kb == 0
        tk = kb * nb
    else:
        n = b.shape[0] if rhs_t else b.shape[1]
    tm, tk = _pick(m, tm), _pick(k, tk, 128)
    nk = k // tk
    if halves:
        hb = k // 2 // tk
        a_spec = pl.BlockSpec((None, tm, tk), lambda i, kk: (kk // hb, i, kk % hb))
    else:
        a_spec = pl.BlockSpec((tm, tk), lambda i, kk: (i, kk))
    dims = _NT if rhs_t else ((1,), (0,))
    n_rows, n_const, n_out, n_acc = len(rows), len(consts), len(outs), len(accs)

    def kern(*refs):
        a_ref, b_ref = refs[:2]
        row_refs = refs[2:2 + n_rows]
        const_refs = refs[2 + n_rows:2 + n_rows + n_const]
        out_refs = refs[2 + n_rows + n_const:2 + n_rows + n_const + n_out]
        acc_refs = refs[2 + n_rows + n_const + n_out:2 + n_rows + n_const + n_out + n_acc]
        i, kk = pl.program_id(0), pl.program_id(1)

        def finish(p, rs=slice(None), r0=0):
            nr = p.shape[0]
            is_ctx = (i * tm + r0 + lax.broadcasted_iota(jnp.int32, (nr, 1), 0)) < n_ctx
            cvals = []
            for (kind, arr), ref in zip(consts, const_refs):
                if kind == "seg":
                    cvals.append(jnp.where(is_ctx, ref[0], ref[1]) if arr.shape[0] == 2 else ref[0])
                else:
                    cvals.append(ref[...])
            res, terms = fn(p, *[r[rs, :] for r in row_refs], *cvals)
            for ref, v in zip(out_refs, res):
                ref[rs, :] = v.astype(ref.dtype)
            for ref, v in zip(acc_refs, terms):
                s_all = _sum0(v)
                s_ctx = _sum0(jnp.where(is_ctx, v, 0.0)) if n_ctx else jnp.zeros_like(s_all)
                both = jnp.concatenate([s_ctx, s_all - s_ctx], axis=0)[:, None, :]

                @pl.when(i == 0)
                def _():
                    ref[...] = both

                @pl.when(i > 0)
                def _():
                    ref[...] += both

        if nk == 1 and n_acc == 0:
            nsub = 2 if tm % 32 == 0 else 1
            sub = tm // nsub
            for r in range(nsub):
                rs = slice(r * sub, (r + 1) * sub)
                finish(_dot(a_ref[rs, :], b_ref[...], dims), rs, r * sub)
            return
        if col_blocked:
            p = sum(_dot(a_ref[:, c * nb:(c + 1) * nb], b_ref[c], dims) for c in range(kb))
        else:
            p = _dot(a_ref[...], b_ref[...], dims)
        if nk == 1:
            finish(p)
            return
        scr = refs[-1]

        @pl.when(kk == 0)
        def _():
            scr[...] = p

        @pl.when((kk > 0) & (kk < nk - 1))
        def _():
            scr[...] += p

        @pl.when(kk == nk - 1)
        def _():
            finish(scr[...] + p)

    if col_blocked:
        b_spec = pl.BlockSpec((kb, n, nb), lambda i, kk: (kk, 0, 0))
    elif rhs_t:
        b_spec = pl.BlockSpec((n, tk), lambda i, kk: (0, kk))
    else:
        b_spec = pl.BlockSpec((tk, n), lambda i, kk: (kk, 0))
    in_specs = [a_spec, b_spec]
    in_specs += [pl.BlockSpec((tm, r.shape[1]), lambda i, kk: (i, 0)) for r in rows]
    for kind, arr in consts:
        in_specs.append(pl.BlockSpec(arr.shape, (lambda i, kk: (0, 0, 0)) if kind == "seg" else (lambda i, kk: (0, 0))))
    out_shape = [jax.ShapeDtypeStruct((m, w), dt) for w, dt in outs]
    out_specs = [pl.BlockSpec((tm, w), lambda i, kk: (i, 0)) for w, _ in outs]
    out_shape += [jax.ShapeDtypeStruct((2, 1, w), F32) for w in accs]
    out_specs += [pl.BlockSpec((2, 1, w), lambda i, kk: (0, 0, 0)) for w in accs]
    return pl.pallas_call(
        kern, name=name, grid=(m // tm, nk), in_specs=in_specs, out_specs=out_specs, out_shape=out_shape,
        scratch_shapes=[pltpu.VMEM((tm, n), F32)] if nk > 1 else [],
        compiler_params=_params("arbitrary", "arbitrary"),
    )(a, b, *rows, *[arr for _, arr in consts])


def _mm_glu(u, win_t, *, name, tm=2176, tn=256):
    m, k = u.shape
    n = win_t.shape[0] // 2
    tm, tn = _pick(m, tm), _pick(n, tn, 128)
    nj = n // tn

    nsub = 4 if tm % 64 == 0 else 1
    sub = tm // nsub

    def kern(u_ref, wa_ref, wb_ref, s_ref, a_ref, b_ref):
        for r in range(nsub):
            rows = slice(r * sub, (r + 1) * sub)
            uu = u_ref[rows, :]
            a = _dot(uu, wa_ref[...], _NT)
            b = _dot(uu, wb_ref[...], _NT)
            s_ref[rows, :] = (_silu(a) * b).astype(BF16)
            a_ref[rows, :] = a.astype(BF16)
            b_ref[rows, :] = b.astype(BF16)

    ospec = pl.BlockSpec((tm, tn), lambda i, j: (i, j))
    return pl.pallas_call(
        kern, name=name, grid=(m // tm, nj),
        in_specs=[pl.BlockSpec((tm, k), lambda i, j: (i, 0)), pl.BlockSpec((tn, k), lambda i, j: (j, 0)),
                  pl.BlockSpec((tn, k), lambda i, j: (nj + j, 0))],
        out_specs=[ospec, ospec, ospec],
        out_shape=[jax.ShapeDtypeStruct((m, n), BF16)] * 3,
        compiler_params=_params("parallel", "parallel"),
    )(u, win_t, win_t)


def _mm_glu_bwd(dy, wout, a, b, *, name, tm=544, tn=1408):
    m, k = dy.shape
    f = wout.shape[0]
    tm, tn = _pick(m, tm), _pick(f, tn, 128)
    nsub = 2 if tm % 32 == 0 else 1
    sub = tm // nsub

    def kern(dy_ref, w_ref, a_ref, b_ref, o_ref):
        for r in range(nsub):
            rs = slice(r * sub, (r + 1) * sub)
            ds = _dot(dy_ref[rs, :], w_ref[...], _NT)
            (dp,), _ = _glu_bwd_fn(ds, a_ref[rs, :], b_ref[rs, :])
            o_ref[0, rs, :] = dp[:, :tn].astype(BF16)
            o_ref[1, rs, :] = dp[:, tn:].astype(BF16)

    tile = pl.BlockSpec((tm, tn), lambda i, j: (i, j))
    return pl.pallas_call(
        kern, name=name, grid=(m // tm, f // tn),
        in_specs=[pl.BlockSpec((tm, k), lambda i, j: (i, 0)), pl.BlockSpec((tn, k), lambda i, j: (j, 0)), tile, tile],
        out_specs=pl.BlockSpec((2, tm, tn), lambda i, j: (0, i, j)),
        out_shape=jax.ShapeDtypeStruct((2, m, f), BF16),
        compiler_params=_params("parallel", "parallel"),
    )(dy, wout, a, b)


def _mm_tn(a, b, *, name, tm=1024, tn=1024, tk=2176, col_blocks=None, stack=None):
    extra, extra_specs, aliases = [], [], {}
    halves = a.ndim == 3
    t, m = (a.shape[1], 2 * a.shape[2]) if halves else a.shape
    t2, n = b.shape
    assert t == t2
    tm, tn, tk = _pick(m, tm, 128), _pick(n, tn, 128), _pick(t, tk)
    nk = t // tk
    if halves:
        hb = m // 2 // tm
        a_spec = pl.BlockSpec((None, tk, tm), lambda i, j, kk: (i // hb, kk, i % hb))
    else:
        a_spec = pl.BlockSpec((tk, tm), lambda i, j, kk: (kk, i))
    if col_blocks is None:
        def kern(a_ref, b_ref, o_ref):
            kk = pl.program_id(2)

            @pl.when(kk == 0)
            def _():
                o_ref[...] = jnp.zeros_like(o_ref)

            o_ref[...] += _dot(a_ref[...], b_ref[...], _TN)

        out_spec = pl.BlockSpec((tm, tn), lambda i, j, kk: (i, j))
        out_shape = jax.ShapeDtypeStruct((m, n), F32)
        scratch = []
    else:
        wb = n // col_blocks
        per = tn // wb
        assert tn % wb == 0 and wb % 8 == 0

        def kern(a_ref, b_ref, *rest):
            o_ref, acc_ref = rest[-2:]
            kk = pl.program_id(2)
            p = _dot(a_ref[...], b_ref[...], _TN)

            @pl.when(kk == 0)
            def _():
                acc_ref[...] = p

            @pl.when((kk > 0) & (kk < nk - 1))
            def _():
                acc_ref[...] += p

            @pl.when(kk == nk - 1)
            def _():
                r = acc_ref[...] + p if nk > 1 else p
                for c in range(per):
                    o_ref[c] = r[:, c * wb:(c + 1) * wb].astype(BF16)

        rows_total, row0, into = stack if stack is not None else (m, 0, None)
        assert row0 % tm == 0
        out_spec = pl.BlockSpec((per, tm, wb), lambda i, j, kk: (j, i + row0 // tm, 0))
        out_shape = jax.ShapeDtypeStruct((col_blocks, rows_total, wb), BF16)
        scratch = [pltpu.VMEM((tm, tn), F32)]
        if into is not None:
            extra, extra_specs, aliases = [into], [pl.BlockSpec(memory_space=pl.ANY)], {2: 0}

    return pl.pallas_call(
        kern, name=name, grid=(m // tm, n // tn, nk),
        in_specs=[a_spec, pl.BlockSpec((tk, tn), lambda i, j, kk: (kk, j))] + extra_specs,
        out_specs=out_spec, out_shape=out_shape, scratch_shapes=scratch, input_output_aliases=aliases,
        compiler_params=_params("parallel", "parallel", "arbitrary"),
    )(a, b, *extra)


def _mm_f32(a, b, *, name, silu_a=False, bias=None):
    m, k = a.shape
    n = b.shape[1]

    def kern(*refs):
        if bias is None:
            a_ref, b_ref, o_ref = refs
        else:
            a_ref, b_ref, bias_ref, o_ref = refs
        av = a_ref[...]
        if silu_a:
            av = _silu(av)
        r = jnp.dot(av, b_ref[...], preferred_element_type=F32, precision=HI)
        if bias is not None:
            r = r + bias_ref[...]
        o_ref[...] = r

    args = [a, b] + ([] if bias is None else [bias])
    return pl.pallas_call(kern, name=name, out_shape=jax.ShapeDtypeStruct((m, n), F32),
                          compiler_params=pltpu.CompilerParams(vmem_limit_bytes=VMEM_LIMIT_BYTES))(*args)


CONV_WIN = 32


def _conv_windows(n, n_ctx):
    assert n_ctx % CONV_WIN == 0 and n_ctx >= CONV_WIN and n - n_ctx >= CONV_WIN
    return (0, n_ctx - CONV_WIN // 2, n - CONV_WIN)


def _tap_outside(r0, s, n, n_ctx):
    t = r0 + lax.broadcasted_iota(jnp.int32, (CONV_WIN, 1), 0)
    lo = jnp.where(t < n_ctx, 0, n_ctx)
    hi = jnp.where(t < n_ctx, n_ctx, n)
    return jnp.where((t + s >= lo) & (t + s < hi), 0.0, 1.0)


def _rolled(v, s):
    return v if s == 0 else pltpu.roll(v, (-s) % v.shape[0], 0)


def _conv_fwd(xp, w8, b, *, n_ctx, name, cb=256):
    n, c = xp.shape
    half = SSD_CONV // 2

    def kern(x_ref, w_ref, b_ref, cpre_ref, act_ref):
        x = x_ref[...]
        acc = jnp.zeros_like(x) + b_ref[...]
        rolled = {}
        for k in range(SSD_CONV):
            rolled[k] = _rolled(x, k - half)
            acc = acc + rolled[k] * w_ref[k:k + 1, :]
        cpre_ref[...] = acc
        act_ref[...] = _silu(acc)
        for r0 in _conv_windows(n, n_ctx):
            rows = slice(r0, r0 + CONV_WIN)
            fix = acc[rows]
            for k in range(SSD_CONV):
                if k != half:
                    fix = fix - rolled[k][rows] * w_ref[k:k + 1, :] * _tap_outside(r0, k - half, n, n_ctx)
            cpre_ref[rows, :] = fix
            act_ref[rows, :] = _silu(fix)

    spec = pl.BlockSpec((n, cb), lambda j: (0, j))
    return pl.pallas_call(
        kern, name=name, grid=(c // cb,),
        in_specs=[spec, pl.BlockSpec((8, cb), lambda j: (0, j)), pl.BlockSpec((1, cb), lambda j: (0, j))],
        out_specs=[spec, spec], out_shape=[jax.ShapeDtypeStruct((n, c), F32)] * 2,
        compiler_params=_params("parallel"),
    )(xp, w8, b)


def _conv_bwd(d1, d2, cpre, xp, w8, *, n_ctx, name, cb=128):
    n, c = xp.shape
    half = SSD_CONV // 2

    def kern(d1_ref, d2_ref, cpre_ref, x_ref, w_ref, dx_ref, dw_ref, db_ref):
        g = (d1_ref[...] + d2_ref[...]) * _dsilu(cpre_ref[...])
        x = x_ref[...]
        dx = jnp.zeros_like(g)
        dw_ref[...] = jnp.zeros_like(dw_ref)
        g_rolled = {}
        for k in range(SSD_CONV):
            s = k - half
            g_rolled[k] = _rolled(g, -s)
            dx = dx + g_rolled[k] * w_ref[k:k + 1, :]
            xr = _rolled(x, s)
            dw = _sum0(g * xr)
            if s != 0:
                for r0 in _conv_windows(n, n_ctx):
                    rows = slice(r0, r0 + CONV_WIN)
                    dw = dw - _sum0(g[rows] * xr[rows] * _tap_outside(r0, s, n, n_ctx))
            dw_ref[k:k + 1, :] = dw
        dx_ref[...] = dx.astype(BF16)
        for r0 in _conv_windows(n, n_ctx):
            rows = slice(r0, r0 + CONV_WIN)
            fix = dx[rows]
            for k in range(SSD_CONV):
                if k != half:
                    fix = fix - g_rolled[k][rows] * w_ref[k:k + 1, :] * _tap_outside(r0, half - k, n, n_ctx)
            dx_ref[rows, :] = fix.astype(BF16)
        db_ref[...] = _sum0(g)

    spec = pl.BlockSpec((n, cb), lambda j: (0, j))
    return pl.pallas_call(
        kern, name=name, grid=(c // cb,),
        in_specs=[spec, spec, spec, spec, pl.BlockSpec((8, cb), lambda j: (0, j))],
        out_specs=[spec, pl.BlockSpec((8, cb), lambda j: (0, j)), pl.BlockSpec((1, cb), lambda j: (0, j))],
        out_shape=[jax.ShapeDtypeStruct((n, c), BF16), jax.ShapeDtypeStruct((8, c), F32),
                   jax.ShapeDtypeStruct((1, c), F32)],
        compiler_params=_params("parallel"),
    )(d1, d2, cpre, xp, w8)


def _chunk_of(s, nc, n_ctx_chunks, rev):
    if not rev:
        return s
    return jnp.where(s < n_ctx_chunks, n_ctx_chunks - 1 - s, nc - 1 - (s - n_ctx_chunks))


def _scan_common(dt_raw, dtT_raw, bias_r, bias_c, alog_r, alog_c, rev):
    ii = lax.broadcasted_iota(jnp.int32, (CHUNK, CHUNK), 0)
    jj = lax.broadcasted_iota(jnp.int32, (CHUNK, CHUNK), 1)
    tri = (jj >= ii) if rev else (jj <= ii)
    tri_t = (ii >= jj) if rev else (ii <= jj)
    a_r = -jnp.exp(alog_r)
    a_c = -jnp.exp(alog_c)
    dt = _softplus(dt_raw + bias_r)
    dt_t = _softplus(dtT_raw + bias_c)
    al = dt * a_r
    acum = _dot(tri.astype(F32), al, precision=HI)
    acum_t = _dot(dt_t * a_c, tri_t.astype(F32), precision=HI)
    atot = _sum0(al)
    return tri, tri_t, a_r, dt, acum, acum_t, atot


def _head_spread():
    return jnp.repeat(jnp.eye(SSD_HEADS, dtype=BF16), SSD_HEAD_DIM, axis=1)


def _dot_sel(v, sel):
    hi = v.astype(BF16)
    lo = (v - hi.astype(F32)).astype(BF16)
    return _dot(hi, sel) + _dot(lo, sel)


def _ssd_scan_fwd(xbc, dt_raw, dtT_raw, bias_r, bias_c, alog_r, alog_c, *, rev, n_ctx_chunks, name):
    n = xbc.shape[0]
    nc = n // CHUNK
    cidx = functools.partial(_chunk_of, nc=nc, n_ctx_chunks=n_ctx_chunks, rev=rev)

    def kern(xs_ref, b_ref, c_ref, dt_ref, dtT_ref, br_ref, bc_ref, ar_ref, ac_ref, e_ref, y_ref, hs_ref, h_scr):
        @pl.when(pl.program_id(0) == 0)
        def _():
            h_scr[...] = jnp.zeros_like(h_scr)

        tri, _, _, dt, acum, acum_t, atot = _scan_common(
            dt_ref[...], dtT_ref[...], br_ref[...], bc_ref[...], ar_ref[...], ac_ref[...], rev)
        etot = jnp.exp(atot)
        spread = lambda v: _dot_sel(v, e_ref[...])
        xdt_all = xs_ref[...] * spread(dt)
        eax = spread(jnp.exp(acum))
        xdw_all = xdt_all * spread(jnp.exp(atot - acum))
        hs_ref[...] = h_scr[...]
        for g in range(SSD_GROUPS):
            gs = slice(g * 256, (g + 1) * 256)
            bg = b_ref[:, g * SSD_STATE:(g + 1) * SSD_STATE].astype(BF16)
            cg = c_ref[:, g * SSD_STATE:(g + 1) * SSD_STATE].astype(BF16)
            cb = _dot(cg, bg, _NT)
            h4 = h_scr[gs, :]
            ys = []
            for k in range(SSD_HPG):
                h = g * SSD_HPG + k
                lmat = jnp.exp(jnp.where(tri, acum[:, h:h + 1] - acum_t[h:h + 1, :], NEG_BIG))
                xdt_h = xdt_all[:, h * SSD_HEAD_DIM:(h + 1) * SSD_HEAD_DIM].astype(BF16)
                ys.append(_dot((cb * lmat).astype(BF16), xdt_h))
            y_ref[:, gs] = jnp.concatenate(ys, axis=1) + _dot(cg, h4.astype(BF16), _NT) * eax[:, gs]
            s4 = _dot(xdw_all[:, gs].astype(BF16), bg, _TN)
            for k in range(SSD_HPG):
                h = g * SSD_HPG + k
                rs = slice(h * SSD_HEAD_DIM, (h + 1) * SSD_HEAD_DIM)
                h_scr[rs, :] = h4[k * SSD_HEAD_DIM:(k + 1) * SSD_HEAD_DIM] * etot[:, h:h + 1] + \
                    s4[k * SSD_HEAD_DIM:(k + 1) * SSD_HEAD_DIM]

    nh = SSD_HEADS
    small = lambda shape: pl.BlockSpec(shape, lambda s: (0, 0))
    return pl.pallas_call(
        kern, name=name, grid=(nc,),
        in_specs=[pl.BlockSpec((CHUNK, SSD_INNER), lambda s: (cidx(s), 0)),
                  pl.BlockSpec((CHUNK, 1024), lambda s: (cidx(s), 2)),
                  pl.BlockSpec((CHUNK, 1024), lambda s: (cidx(s), 3)),
                  pl.BlockSpec((CHUNK, nh), lambda s: (cidx(s), 0)),
                  pl.BlockSpec((nh, CHUNK), lambda s: (0, cidx(s))),
                  small((1, nh)), small((nh, 1)), small((1, nh)), small((nh, 1)), small((nh, SSD_INNER))],
        out_specs=[pl.BlockSpec((CHUNK, SSD_INNER), lambda s: (cidx(s), 0)),
                   pl.BlockSpec((None, SSD_INNER, SSD_STATE), lambda s: (s, 0, 0))],
        out_shape=[jax.ShapeDtypeStruct((n, SSD_INNER), F32),
                   jax.ShapeDtypeStruct((nc, SSD_INNER, SSD_STATE), F32)],
        scratch_shapes=[pltpu.VMEM((SSD_INNER, SSD_STATE), F32)],
        compiler_params=_params("arbitrary"),
    )(xbc, xbc, xbc, dt_raw, dtT_raw, bias_r, bias_c, alog_r, alog_c, _head_spread())


def _ssd_scan_bwd(dy, xbc, hs, dt_raw, dtT_raw, bias_r, bias_c, alog_r, alog_c, dvec, *, rev, n_ctx_chunks,
                  direct, name):
    n = xbc.shape[0]
    nc = n // CHUNK
    nh = SSD_HEADS
    step_of = lambda r: nc - 1 - r
    cidx = lambda r: _chunk_of(step_of(r), nc, n_ctx_chunks, rev)

    def kern(dy_ref, xs_ref, b_ref, c_ref, hs_ref, dt_ref, dtT_ref, br_ref, bc_ref, ar_ref, ac_ref, dv_ref,
             e_ref, et_ref, dx_ref, ddt_ref, dal_ref, dbias_ref, dh_scr):
        @pl.when(pl.program_id(0) == 0)
        def _():
            dh_scr[...] = jnp.zeros_like(dh_scr)
            dal_ref[...] = jnp.zeros_like(dal_ref)
            dbias_ref[...] = jnp.zeros_like(dbias_ref)

        tri, tri_t, a_r, dt, acum, acum_t, atot = _scan_common(
            dt_ref[...], dtT_ref[...], br_ref[...], bc_ref[...], ar_ref[...], ac_ref[...], rev)
        etot = jnp.exp(atot)
        spread = lambda v: _dot_sel(v, e_ref[...])
        gather = lambda v: _dot_sel(v, et_ref[...])
        xs_all = xs_ref[...]
        dy_all = dy_ref[...]
        dtx = spread(dt)
        eax = spread(jnp.exp(acum))
        decx = spread(jnp.exp(atot - acum))
        xdt_all = xs_all * dtx
        xdw_all = xdt_all * decx
        dyo_all = dy_all * eax
        lane = lax.broadcasted_iota(jnp.int32, (CHUNK, nh), 1)
        lane1 = lax.broadcasted_iota(jnp.int32, (1, nh), 1)
        sub = lax.broadcasted_iota(jnp.int32, (nh, CHUNK), 0)
        g_rows = jnp.zeros((CHUNK, nh), F32)
        g_cols = jnp.zeros((nh, CHUNK), F32)
        dtot = jnp.zeros((1, nh), F32)
        q_col, q_e, q_dt = [], [], []
        for g in range(SSD_GROUPS):
            gs = slice(g * 256, (g + 1) * 256)
            bg = b_ref[:, g * SSD_STATE:(g + 1) * SSD_STATE].astype(BF16)
            cg = c_ref[:, g * SSD_STATE:(g + 1) * SSD_STATE].astype(BF16)
            cb = _dot(cg, bg, _NT)
            hs4 = hs_ref[gs, :]
            dh4 = dh_scr[gs, :]
            hs4_bf = hs4.astype(BF16)
            dh4_bf = dh4.astype(BF16)
            dy4 = dy_all[:, gs]
            dy4_bf = dy4.astype(BF16)
            xdt4_bf = xdt_all[:, gs].astype(BF16)
            xdw4 = xdw_all[:, gs]
            xdw4_bf = xdw4.astype(BF16)
            dyo4_bf = dyo_all[:, gs].astype(BF16)
            yoff4 = _dot(cg, hs4_bf, _NT) * eax[:, gs]
            dcg = _dot(dyo4_bf, hs4_bf)
            dh_new4 = _dot(dyo4_bf, cg, _TN)
            bdh4 = _dot(bg, dh4_bf, _NT)
            dbg = _dot(xdw4_bf, dh4_bf)
            e4 = xdw4 * bdh4
            q_col.append(dy4 * yoff4 - e4)
            q_e.append(e4)
            hsum = jnp.sum(dh4 * hs4, axis=1, keepdims=True)
            dcb = jnp.zeros((CHUNK, CHUNK), F32)
            dxdts = []
            for k in range(SSD_HPG):
                h = g * SSD_HPG + k
                ks = slice(k * SSD_HEAD_DIM, (k + 1) * SSD_HEAD_DIM)
                lmat = jnp.exp(jnp.where(tri, acum[:, h:h + 1] - acum_t[h:h + 1, :], NEG_BIG))
                mf = cb * lmat
                dm = _dot(dy4_bf[:, ks], xdt4_bf[:, ks], _NT)
                dcb = dcb + dm * lmat
                gmat = dm * mf
                g_rows = g_rows + jnp.where(lane == h, jnp.sum(gmat, axis=1, keepdims=True), 0.0)
                g_cols = g_cols + jnp.where(sub == h, _sum0(gmat), 0.0)
                dxdts.append(_dot(mf.astype(BF16), dy4_bf[:, ks], _TN))
                et = etot[:, h:h + 1]
                dtot = dtot + jnp.where(lane1 == h, _sum0(hsum[ks]) * et, 0.0)
                dh_scr[h * SSD_HEAD_DIM:(h + 1) * SSD_HEAD_DIM, :] = dh4[ks] * et + dh_new4[ks]
            dxdt4 = jnp.concatenate(dxdts, axis=1) + bdh4 * decx[:, gs]
            q_dt.append(dxdt4 * xs_all[:, gs])
            dx4 = dxdt4 * dtx[:, gs]
            if direct:
                dx4 = dx4 + dy4 * dv_ref[:, gs]
            dcb_bf = dcb.astype(BF16)
            dx_ref[:, gs] = dx4
            dx_ref[:, SSD_INNER + g * SSD_STATE:SSD_INNER + (g + 1) * SSD_STATE] = dbg + _dot(dcb_bf, cg, _TN)
            dx_ref[:, SSD_INNER + 1024 + g * SSD_STATE:SSD_INNER + 1024 + (g + 1) * SSD_STATE] = \
                dcg + _dot(dcb_bf, bg)
        e_heads = gather(jnp.concatenate(q_e, axis=1))
        dacum = gather(jnp.concatenate(q_col, axis=1)) + g_rows - g_cols.T
        dal = _dot(tri_t.astype(F32), dacum, precision=HI) + dtot + _sum0(e_heads)
        ddt = gather(jnp.concatenate(q_dt, axis=1)) + dal * a_r
        ddt_raw = ddt * _sig(dt_ref[...] + br_ref[...])
        ddt_ref[...] = ddt_raw
        dal_ref[...] += _sum0(dal * dt) * a_r
        dbias_ref[...] += _sum0(ddt_raw)

    small = lambda shape: pl.BlockSpec(shape, lambda r: (0, 0))
    return pl.pallas_call(
        kern, name=name, grid=(nc,),
        in_specs=[pl.BlockSpec((CHUNK, SSD_INNER), lambda r: (cidx(r), 0)),
                  pl.BlockSpec((CHUNK, SSD_INNER), lambda r: (cidx(r), 0)),
                  pl.BlockSpec((CHUNK, 1024), lambda r: (cidx(r), 2)),
                  pl.BlockSpec((CHUNK, 1024), lambda r: (cidx(r), 3)),
                  pl.BlockSpec((None, SSD_INNER, SSD_STATE), lambda r: (step_of(r), 0, 0)),
                  pl.BlockSpec((CHUNK, nh), lambda r: (cidx(r), 0)),
                  pl.BlockSpec((nh, CHUNK), lambda r: (0, cidx(r))),
                  small((1, nh)), small((nh, 1)), small((1, nh)), small((nh, 1)), small((1, SSD_INNER)),
                  small((nh, SSD_INNER)), small((SSD_INNER, nh))],
        out_specs=[pl.BlockSpec((CHUNK, SSD_CONV_DIM), lambda r: (cidx(r), 0)),
                   pl.BlockSpec((CHUNK, nh), lambda r: (cidx(r), 0)),
                   small((1, nh)), small((1, nh))],
        out_shape=[jax.ShapeDtypeStruct((n, SSD_CONV_DIM), F32), jax.ShapeDtypeStruct((n, nh), F32),
                   jax.ShapeDtypeStruct((1, nh), F32), jax.ShapeDtypeStruct((1, nh), F32)],
        scratch_shapes=[pltpu.VMEM((SSD_INNER, SSD_STATE), F32)],
        compiler_params=_params("arbitrary"),
    )(dy, xbc, xbc, xbc, hs, dt_raw, dtT_raw, bias_r, bias_c, alog_r, alog_c, dvec, _head_spread(),
      _head_spread().T)


def _gm_spatial_fwd(gu, gvn, ws, bst, *, name):
    n = gu.shape[0]

    def kern(gu_ref, gv_ref, ws_ref, bs_ref, o_ref):
        for g in range(GM_GROUPS):
            sl = slice(g * GM_GROUP_DIM, (g + 1) * GM_GROUP_DIM)
            s = _dot(ws_ref[g], gv_ref[:, sl]) + bs_ref[:, g:g + 1]
            o_ref[:, sl] = (gu_ref[:, sl] * s).astype(BF16)

    spec = pl.BlockSpec((CHUNK, GM_INNER), lambda i: (i, 0))
    return pl.pallas_call(
        kern, name=name, grid=(n // CHUNK,),
        in_specs=[spec, spec, pl.BlockSpec(ws.shape, lambda i: (0, 0, 0)), pl.BlockSpec(bst.shape, lambda i: (0, 0))],
        out_specs=spec, out_shape=jax.ShapeDtypeStruct((n, GM_INNER), BF16),
        compiler_params=_params("parallel"),
    )(gu, gvn, ws, bst)


def _gm_spatial_bwd(dt, gu, gvn, ws, wst, bst, *, name):
    n = gu.shape[0]

    def kern(dt_ref, gu_ref, gv_ref, ws_ref, wst_ref, bs_ref, dgu_ref, dgv_ref, dws_ref, dbs_ref):
        @pl.when(pl.program_id(0) == 0)
        def _():
            dws_ref[...] = jnp.zeros_like(dws_ref)
            dbs_ref[...] = jnp.zeros_like(dbs_ref)

        lane = lax.broadcasted_iota(jnp.int32, (CHUNK, GM_GROUPS), 1)
        dbs = jnp.zeros((CHUNK, GM_GROUPS), F32)
        for g in range(GM_GROUPS):
            sl = slice(g * GM_GROUP_DIM, (g + 1) * GM_GROUP_DIM)
            gv = gv_ref[:, sl]
            s = _dot(ws_ref[g], gv) + bs_ref[:, g:g + 1]
            d = dt_ref[:, sl]
            dgu_ref[:, sl] = d * s
            ds = d * gu_ref[:, sl]
            ds_bf = ds.astype(BF16)
            dws_ref[g] += _dot(ds_bf, gv, _NT)
            dgv_ref[:, sl] = _dot(wst_ref[g], ds_bf)
            dbs = dbs + jnp.where(lane == g, jnp.sum(ds, axis=1, keepdims=True), 0.0)
        dbs_ref[...] += dbs

    spec = pl.BlockSpec((CHUNK, GM_INNER), lambda i: (i, 0))
    wspec = pl.BlockSpec(ws.shape, lambda i: (0, 0, 0))
    bspec = pl.BlockSpec(bst.shape, lambda i: (0, 0))
    return pl.pallas_call(
        kern, name=name, grid=(n // CHUNK,),
        in_specs=[spec, spec, spec, wspec, wspec, bspec],
        out_specs=[spec, spec, wspec, bspec],
        out_shape=[jax.ShapeDtypeStruct((n, GM_INNER), F32), jax.ShapeDtypeStruct((n, GM_INNER), F32),
                   jax.ShapeDtypeStruct(ws.shape, F32), jax.ShapeDtypeStruct(bst.shape, F32)],
        compiler_params=_params("arbitrary"),
    )(dt, gu, gvn, ws, wst, bst)


def _adamw(parts, w, m, v, *, name, tm=256, sel=(), into=None):
    ns, r, wd = parts.shape
    tm = _pick(r, tm, 8)
    tc = wd
    if tm < 64 and wd % 256 == 0:
        tm, tc = r, 256
    lead = len(sel)
    assert w.shape[lead:] == (r, wd) and lead == w.ndim - 2

    def kern(*refs):
        p_ref, w_ref, m_ref, v_ref = refs[:4]
        g_ref, d_ref, nm_ref, nv_ref = refs[-4:]
        g = p_ref[0].astype(F32)
        for s in range(1, ns):
            g = g + p_ref[s].astype(F32)
        m2 = ADAM_B1 * m_ref[...] + (1.0 - ADAM_B1) * g
        v2 = ADAM_B2 * v_ref[...] + (1.0 - ADAM_B2) * (g * g)
        m_hat = m2 / (1.0 - ADAM_B1 ** ADAM_STEP)
        v_hat = v2 / (1.0 - ADAM_B2 ** ADAM_STEP)
        g_ref[...] = g
        d_ref[...] = -ADAM_LR * (m_hat / (jnp.sqrt(v_hat) + ADAM_EPS) + ADAM_WD * w_ref[...])
        nm_ref[...] = m2
        nv_ref[...] = v2

    spec = pl.BlockSpec((None,) * lead + (tm, tc), lambda i, j: tuple(sel) + (i, j))
    extra, aliases = [], {}
    if into is not None:
        extra = list(into)
        aliases = {4 + k: k for k in range(4)}
    return pl.pallas_call(
        kern, name=name, grid=(r // tm, wd // tc),
        in_specs=[pl.BlockSpec((ns, tm, tc), lambda i, j: (0, i, j)), spec, spec, spec] +
                 [pl.BlockSpec(memory_space=pl.ANY)] * len(extra),
        out_specs=[spec] * 4, out_shape=[jax.ShapeDtypeStruct(w.shape, F32)] * 4,
        input_output_aliases=aliases,
        compiler_params=_params("parallel", "parallel"),
    )(parts, w, m, v, *extra)


def _sum_slots(parts, *, name, scale_by=None):
    ns, r, wd = parts.shape

    def kern(*refs):
        p_ref, o_ref = refs[0], refs[-1]
        g = p_ref[0]
        for s in range(1, ns):
            g = g + p_ref[s]
        if scale_by is not None:
            g = g * _dsilu(refs[1][...])
        o_ref[...] = g

    args = [parts] + ([] if scale_by is None else [scale_by])
    return pl.pallas_call(kern, name=name, out_shape=jax.ShapeDtypeStruct((r, wd), F32),
                          compiler_params=pltpu.CompilerParams(vmem_limit_bytes=VMEM_LIMIT_BYTES))(*args)


def _mesh_pos():
    x, y, c = lax.axis_index("x"), lax.axis_index("y"), lax.axis_index("c")
    return x, y, c, 4 * x + 2 * y + c


def _flip(x, y, c, f):
    fx, fy, fc = (f >> 2) & 1, (f >> 1) & 1, f & 1
    px = 1 - x if fx else x
    py = 1 - y if fy else y
    pc = 1 - c if fc else c
    return (px, py, pc), 4 * px + 2 * py + pc


_HBM_SPEC = pl.BlockSpec(memory_space=pltpu.HBM)


def _exchange(arrays, *, scatter, name):
    na = len(arrays)
    if scatter:
        out_shape = [jax.ShapeDtypeStruct(a.shape, a.dtype) for a in arrays]
    else:
        out_shape = [jax.ShapeDtypeStruct((NDEV,) + a.shape, a.dtype) for a in arrays]

    out_shape.append(jax.ShapeDtypeStruct((8, 128), F32))

    def body(*refs):
        ins, outs = refs[:na], refs[na:2 * na]
        send_sems, recv_sems, local_sems = refs[2 * na + 1:]
        refs[2 * na][...] = jnp.zeros((8, 128), F32)
        x, y, c, me = _mesh_pos()
        copies = []
        for i in range(na):
            src_own = ins[i].at[me] if scatter else ins[i]
            lc = pltpu.make_async_copy(src_own, outs[i].at[me], local_sems.at[i])
            lc.start()
            copies.append(lc)
        sends = []
        for f in range(1, NDEV):
            peer, pidx = _flip(x, y, c, f)
            for i in range(na):
                k = i * (NDEV - 1) + f - 1
                src = ins[i].at[pidx] if scatter else ins[i]
                cp = pltpu.make_async_remote_copy(
                    src_ref=src, dst_ref=outs[i].at[me], send_sem=send_sems.at[k], recv_sem=recv_sems.at[k],
                    device_id=peer, device_id_type=pl.DeviceIdType.MESH)
                cp.start()
                sends.append(cp)
        for f in range(1, NDEV):
            peer, pidx = _flip(x, y, c, f)
            for i in range(na):
                k = i * (NDEV - 1) + f - 1
                src = ins[i].at[pidx] if scatter else ins[i]
                pltpu.make_async_remote_copy(
                    src_ref=src, dst_ref=outs[i].at[pidx], send_sem=send_sems.at[k], recv_sem=recv_sems.at[k],
                    device_id=peer, device_id_type=pl.DeviceIdType.MESH).wait_recv()
        for cp in sends:
            cp.wait_send()
        for lc in copies:
            lc.wait()

    res = pl.pallas_call(
        body, name=name, out_shape=out_shape, in_specs=[_HBM_SPEC] * na,
        out_specs=[_HBM_SPEC] * na + [pl.BlockSpec(memory_space=pltpu.VMEM)],
        scratch_shapes=[pltpu.SemaphoreType.DMA((na * (NDEV - 1),)), pltpu.SemaphoreType.DMA((na * (NDEV - 1),)),
                        pltpu.SemaphoreType.DMA((na,))],
        compiler_params=pltpu.CompilerParams(has_side_effects=True),
    )(*arrays)
    return res[:na], res[na][0, 0]


_SEM_SPEC = pl.BlockSpec(memory_space=pltpu.SEMAPHORE)
_DATAFLOW = pltpu.SideEffectType.DATAFLOW_SIDE_EFFECTING


def _split_copies(srcs, lands, send_sems, recv_sems, scatter, arriving):
    x, y, c, me = _mesh_pos()
    copies = []
    for i in range(len(srcs)):
        for f in range(1, NDEV):
            peer, pidx = _flip(x, y, c, f)
            k = i * (NDEV - 1) + f - 1
            copies.append(pltpu.make_async_remote_copy(
                src_ref=srcs[i].at[pidx] if scatter else srcs[i], dst_ref=lands[i].at[pidx if arriving else me],
                send_sem=send_sems.at[k], recv_sem=recv_sems.at[k], device_id=peer,
                device_id_type=pl.DeviceIdType.MESH))
    return copies


def _exchange_start(srcs, lands, *, scatter, name):
    na = len(srcs)
    nsem = na * (NDEV - 1)

    def body(*refs):
        ins_src, ins_land = refs[:na], refs[na:2 * na]
        send_sems, recv_sems = refs[2 * na], refs[2 * na + 1]
        token = refs[-1]
        for cp in _split_copies(ins_src, ins_land, send_sems, recv_sems, scatter, False):
            cp.start()
        token[...] = jnp.zeros_like(token)

    thru = [pltpu.HBM(a.shape, a.dtype) for a in list(srcs) + list(lands)]
    res = pl.pallas_call(
        body, name=name,
        out_shape=(pltpu.SemaphoreType.DMA((nsem,)), pltpu.SemaphoreType.DMA((nsem,)), *thru,
                   jax.ShapeDtypeStruct((8, 128), F32)),
        in_specs=[_HBM_SPEC] * (2 * na),
        out_specs=(_SEM_SPEC, _SEM_SPEC, *([_HBM_SPEC] * (2 * na)), pl.BlockSpec(memory_space=pltpu.VMEM)),
        input_output_aliases={i: 2 + i for i in range(2 * na)},
        compiler_params=pltpu.CompilerParams(has_side_effects=_DATAFLOW),
    )(*[pltpu.with_memory_space_constraint(a, pltpu.HBM) for a in list(srcs) + list(lands)])
    send_sems, recv_sems = res[0], res[1]
    return send_sems, recv_sems, res[2:2 + na], res[2 + na:2 + 2 * na], res[-1][0, 0]


def _exchange_wait(send_sems, recv_sems, srcs, lands, after, *, scatter, name):
    na = len(srcs)

    def body(*refs):
        ins_src, ins_land = refs[:na], refs[na:2 * na]
        s_sems, r_sems = refs[2 * na], refs[2 * na + 1]
        for cp in _split_copies(ins_src, ins_land, s_sems, r_sems, scatter, False):
            cp.wait_send()
        for cp in _split_copies(ins_src, ins_land, s_sems, r_sems, scatter, True):
            cp.wait_recv()

    thru = [pltpu.HBM(a.shape, a.dtype) for a in list(srcs) + list(lands)]
    res = pl.pallas_call(
        body, name=name, out_shape=tuple(thru),
        in_specs=[_HBM_SPEC] * (2 * na) + [_SEM_SPEC, _SEM_SPEC, pl.BlockSpec(memory_space=pl.ANY)],
        out_specs=tuple([_HBM_SPEC] * (2 * na)),
        input_output_aliases={i: i for i in range(2 * na)},
        compiler_params=pltpu.CompilerParams(has_side_effects=_DATAFLOW),
    )(*srcs, *lands, send_sems, recv_sems, after)
    return res[na:]


def _landing(block, me):
    buf = lax.empty((NDEV,) + block.shape, block.dtype)
    return lax.dynamic_update_slice_in_dim(buf, block[None], me, axis=0)


def _seg_kw(nseg, n_ctx, tm):
    return dict(nseg=nseg, seg_blocks=(n_ctx // tm if nseg == 2 else 0))


def _ffn_fwd(tag, h, gpre, gpost, shift, scale, gate, w, *, nseg, n_ctx, tm):
    n = h.shape[0]
    kw = _seg_kw(nseg, n_ctx, tm)
    (u,) = _rowwise(tag + "_pre", _pre_fwd_fn, n, [h], [("full", gpre), ("seg", shift), ("seg", scale)],
                    [(D_MODEL, BF16)], tm=tm, **kw)
    if "early" in w:
        w.update(w.pop("early")(u))
    s, a, b = _mm_glu(u, w["win_t"], name=tag + "_glu")
    if "late" in w:
        w.update(w.pop("late")(s))
    y, ho = _mm_rows(s, w["wout"], functools.partial(_out_post_fn, 0.5), [h], [("full", gpost), ("seg", gate)],
                     [(D_MODEL, F32), (D_MODEL, F32)], name=tag + "_out", tk=FFN_DIM, n_ctx=n_ctx)
    return ho, dict(h=h, u=u, s=s, a=a, b=b, y=y)


def _ffn_bwd(tag, dho, sv, gpre, gpost, scale, gate, w, put, *, nseg, n_ctx, tm):
    n = dho.shape[0]
    kw = _seg_kw(nseg, n_ctx, tm)
    dy, dgate, dgpost = _rowwise(tag + "_postb", functools.partial(_post_bwd_fn, 0.5), n, [dho, sv["y"]],
                                 [("full", gpost), ("seg", gate)], [(D_MODEL, BF16)], [D_MODEL, D_MODEL], tm=tm, **kw)
    tok = put("w_out", _mm_tn(sv["s"], dy, name=tag + "_dwout", tm=1408, tn=1024, col_blocks=1))
    dp = _mm_glu_bwd(dy, w["wout"], sv["a"], sv["b"], name=tag + "_ds")
    tok2 = put("w_in", _mm_tn(dp, sv["u"], name=tag + "_dwin", tm=1408, tn=1024, col_blocks=1))
    for t in (tok, tok2):
        if t is not None:
            gpre = gpre + t
    dh, dshift, dscale, dgpre = _mm_rows(dp, w["win_t"], _pre_bwd_fn, [sv["h"], dho],
                                         [("full", gpre), ("seg", scale)], [(D_MODEL, F32)],
                                         [D_MODEL, D_MODEL, D_MODEL], name=tag + "_du", tk=FFN_DIM, n_ctx=n_ctx)
    return dh, None, dict(shift=dshift, scale=dscale, gate=dgate, gpre=dgpre, gpost=dgpost)


def _local_step(x, ctx, target, mods, norm_g, get_w, small, put_grad):
    t_len, n_ctx = x.shape[0], ctx.shape[0]
    n0 = t_len + n_ctx
    tm0 = _pick(n_ctx, 256, 8)
    tm1 = _pick(t_len, 256, 8)
    ncc = n_ctx // CHUNK
    g = {}

    def modrow(i, k, nseg):
        mc, mx = mods[i]
        if nseg == 2:
            return jnp.stack([mc[k], mx[k]])[:, None, :]
        return mx[k][None, None, :]

    pending = [None]

    def gvec(i, k):
        v = norm_g[i, k][None, :]
        if pending[0] is not None:
            v = v + pending[0]
            pending[0] = None
        return v

    xc = jnp.concatenate([ctx, x], axis=0)
    L0 = dict(nseg=2, n_ctx=n_ctx, tm=tm0)
    wts = dict(get_w("ffn00", xc))
    h1, sv_f01 = _ffn_fwd("l0f1", xc, gvec(0, 0), gvec(0, 1), modrow(0, 0, 2), modrow(0, 1, 2), modrow(0, 2, 2),
                          wts["ffn00"], **L0)
    kw0 = _seg_kw(2, n_ctx, tm0)
    (um0,) = _rowwise("l0m_pre", _pre_fwd_fn, n0, [h1], [("full", gvec(0, 2)), ("seg", modrow(0, 3, 2)),
                                                         ("seg", modrow(0, 4, 2))], [(D_MODEL, BF16)], tm=tm0, **kw0)
    wts.update(get_w("ssd", um0))
    win_ssd = wts["ssd_win_t"]
    nh = SSD_HEADS
    dt_blk = (SSD_INNER + SSD_CONV_DIM) // (2 * nh)
    z = _mm(um0, win_ssd, out_dtype=F32, name="ssd_z", rhs_t=True, n=SSD_INNER)
    xbc_pre = _mm(um0, win_ssd, out_dtype=F32, name="ssd_xbc", rhs_t=True, n=SSD_CONV_DIM,
                  b_off=(SSD_INNER // 1024, 0))
    dtr = _mm(um0, win_ssd, out_dtype=F32, name="ssd_dt", rhs_t=True, n=2 * nh, b_off=(dt_blk, 0))
    cpre, xbc = _conv_fwd(xbc_pre, small["conv_w8"], small["conv_b"], n_ctx=n_ctx, name="ssd_conv")
    nh = SSD_HEADS
    dt_dir = [dtr[:, :nh], dtr[:, nh:2 * nh]]
    dtT_dir = [d.T for d in dt_dir]
    bias_r = [small["dt_bias"][d][None, :] for d in range(2)]
    bias_c = [small["dt_bias"][d][:, None] for d in range(2)]
    alog_r = [small["a_log"][d][None, :] for d in range(2)]
    alog_c = [small["a_log"][d][:, None] for d in range(2)]
    ys, hss = [], []
    for d in range(2):
        yd, hsd = _ssd_scan_fwd(xbc, dt_dir[d], dtT_dir[d], bias_r[d], bias_c[d], alog_r[d], alog_c[d],
                                rev=(d == 1), n_ctx_chunks=ncc, name=f"ssd_scan{d}")
        ys.append(yd)
        hss.append(hsd)
    dvec = jnp.repeat(small["ssd_d"], SSD_HEAD_DIM)[None, :]
    ngv = small["ssd_norm_g"][None, :]
    gate_rows = [ys[0], ys[1], (xbc, SSD_INNER, 0, 0), z]
    lat = lambda r: (r[0], r[1], r[2], ncc) if isinstance(r, tuple) else (r, r.shape[1], 0, ncc)
    (yn,) = _rowwise("ssd_gate", _ssdgate_fwd_fn, t_len, [lat(r) for r in gate_rows],
                     [("full", dvec), ("full", ngv)], [(SSD_INNER, BF16)], tm=CHUNK)
    h1x = h1[n_ctx:]
    L1 = dict(nseg=1, n_ctx=0, tm=tm1)
    if "late" in wts:
        wts.update(wts.pop("late")(yn))
    yo0, h2 = _mm_rows(yn, wts["ssd_wout"], functools.partial(_out_post_fn, 1.0), [h1x],
                       [("full", gvec(0, 3)), ("seg", modrow(0, 5, 1))], [(D_MODEL, F32), (D_MODEL, F32)],
                       name="ssd_out", tk=SSD_INNER)
    wts.update(get_w("ffn01", h2))
    h3, sv_f02 = _ffn_fwd("l0f2", h2, gvec(0, 4), gvec(0, 5), modrow(0, 6, 1), modrow(0, 7, 1), modrow(0, 8, 1),
                          wts["ffn01"], **L1)

    wts.update(get_w("ffn10", h3))
    h4, sv_f11 = _ffn_fwd("l1f1", h3, gvec(1, 0), gvec(1, 1), modrow(1, 0, 1), modrow(1, 1, 1), modrow(1, 2, 1),
                          wts["ffn10"], **L1)
    (um1,) = _rowwise("l1m_pre", _pre_fwd_fn, t_len, [h4], [("full", gvec(1, 2)), ("seg", modrow(1, 3, 1)),
                                                            ("seg", modrow(1, 4, 1))], [(D_MODEL, BF16)], tm=tm1)
    wts.update(get_w("gm", um1))
    p1 = _mm(um1, wts["gm_win"], out_dtype=F32, name="gm_in")
    vg = small["gm_v_g"][None, :]
    vb = small["gm_v_b"][None, :]
    gu, gvn = _rowwise("gm_act", _gm_act_fwd_fn, t_len, [p1], [("full", vg), ("full", vb)],
                       [(GM_INNER, F32), (GM_INNER, BF16)], tm=128)
    ws_bf = small["gm_w_s"].astype(BF16)
    wst_bf = jnp.swapaxes(small["gm_w_s"], 1, 2).astype(BF16)
    bst = small["gm_b_s"].T
    tgm = _gm_spatial_fwd(gu, gvn, ws_bf, bst, name="gm_spatial")
    yo1, h5 = _mm_rows(tgm, wts["gm_wout"], functools.partial(_out_post_fn, 1.0), [h4],
                       [("full", gvec(1, 3)), ("seg", modrow(1, 5, 1))], [(D_MODEL, F32), (D_MODEL, F32)],
                       name="gm_out", tk=GM_INNER)
    wts.update(get_w("ffn11", h5))
    h6, sv_f12 = _ffn_fwd("l1f2", h5, gvec(1, 4), gvec(1, 5), modrow(1, 6, 1), modrow(1, 7, 1), modrow(1, 8, 1),
                          wts["ffn11"], **L1)

    dh, loss_parts = _rowwise("loss", _loss_fn, t_len, [h6, target], [], [(D_MODEL, F32)], [D_MODEL], tm=tm1)

    zero = jnp.zeros((D_MODEL,), F32)
    dmx = [[zero] * N_MOD for _ in range(2)]
    dmc = [[zero] * N_MOD for _ in range(2)]
    dng = [[zero] * 6 for _ in range(2)]

    def put_mod(i, k, acc):
        if acc.shape[0] == 2:
            dmc[i][k] = dmc[i][k] + acc[0, 0]
            dmx[i][k] = dmx[i][k] + acc[1, 0]
        else:
            dmx[i][k] = dmx[i][k] + acc[0, 0]

    def put_g(i, k, acc):
        dng[i][k] = dng[i][k] + jnp.sum(acc[:, 0], axis=0)

    def ffn_back(tag, i, j, dho, sv, w, lay):
        nseg = lay["nseg"]
        base = 0 if j == 0 else 6
        gi = 0 if j == 0 else 4
        dh_in, pending[0], s = _ffn_bwd(tag, dho, sv, gvec(i, gi), gvec(i, gi + 1), modrow(i, base + 1, nseg),
                                        modrow(i, base + 2, nseg), w, functools.partial(put_grad, f"ffn{i}{j}"), **lay)
        put_mod(i, base, s["shift"])
        put_mod(i, base + 1, s["scale"])
        put_mod(i, base + 2, s["gate"])
        put_g(i, gi, s["gpre"])
        put_g(i, gi + 1, s["gpost"])
        return dh_in

    dh = ffn_back("l1f2", 1, 1, dh, sv_f12, wts["ffn11"], L1)
    dyo, dgate, dgp = _rowwise("l1m_postb", functools.partial(_post_bwd_fn, 1.0), t_len, [dh, yo1],
                               [("full", gvec(1, 3)), ("seg", modrow(1, 5, 1))], [(D_MODEL, BF16)],
                               [D_MODEL, D_MODEL], tm=tm1)
    put_mod(1, 5, dgate)
    put_g(1, 3, dgp)
    put_grad("gm", "w_out", _mm_tn(tgm, dyo, name="gm_dwout", tn=1024, col_blocks=1))
    dtg = _mm(dyo, wts["gm_wout"], out_dtype=F32, name="gm_dt", rhs_t=True)
    dgu, dgvn, dws, dbst = _gm_spatial_bwd(dtg, gu, gvn, ws_bf, wst_bf, bst, name="gm_spatialb")
    g["gm_w_s"] = dws
    g["gm_b_s"] = dbst.T
    dp1, dvg, dvb = _rowwise("gm_actb", _gm_act_bwd_fn, t_len, [p1, dgu, dgvn], [("full", vg)],
                             [(2 * GM_INNER, BF16)], [GM_INNER, GM_INNER], tm=128)
    g["gm_v_g"] = dvg[0, 0]
    g["gm_v_b"] = dvb[0, 0]
    pending[0] = put_grad("gm", "w_in", _mm_tn(um1, dp1, name="gm_dwin", tm=1024, col_blocks=NDEV))
    dh, dsh, dsc, dgp = _mm_rows(dp1, wts["gm_win"], _pre_bwd_fn, [h4, dh],
                                 [("full", gvec(1, 2)), ("seg", modrow(1, 4, 1))], [(D_MODEL, F32)],
                                 [D_MODEL, D_MODEL, D_MODEL], name="gm_dum", tk=2048, rhs_t=True)
    put_mod(1, 3, dsh)
    put_mod(1, 4, dsc)
    put_g(1, 2, dgp)
    dh = ffn_back("l1f1", 1, 0, dh, sv_f11, wts["ffn10"], L1)

    dh = ffn_back("l0f2", 0, 1, dh, sv_f02, wts["ffn01"], L1)
    dyo, dgate, dgp = _rowwise("l0m_postb", functools.partial(_post_bwd_fn, 1.0), t_len, [dh, yo0],
                               [("full", gvec(0, 3)), ("seg", modrow(0, 5, 1))], [(D_MODEL, BF16)],
                               [D_MODEL, D_MODEL], tm=tm1)
    put_mod(0, 5, dgate)
    put_g(0, 3, dgp)
    tok = put_grad("ssd", "w_out", _mm_tn(yn, dyo, name="ssd_dwout", tn=1024, col_blocks=1))
    dyn = _mm(dyo, wts["ssd_wout"], out_dtype=F32, name="ssd_dyn", rhs_t=True)
    dy_ssd, dz, dngv, ddv = _rowwise("ssd_gateb", _ssdgate_bwd_fn, n0, [(dyn, SSD_INNER, 0, -ncc)] + gate_rows,
                                     [("full", dvec), ("full", ngv if tok is None else ngv + tok)],
                                     [(SSD_INNER, F32), (SSD_INNER, BF16)],
                                     [SSD_INNER, SSD_INNER], tm=128)
    g["ssd_norm_g"] = dngv[0, 0]
    g["ssd_D"] = jnp.sum(ddv[0, 0].reshape(SSD_HEADS, SSD_HEAD_DIM), axis=1)
    dxbcs, ddts, dalogs, dbiases = [], [], [], []
    for d in range(2):
        dxd, ddtd, dal, dbi = _ssd_scan_bwd(dy_ssd, xbc, hss[d], dt_dir[d], dtT_dir[d], bias_r[d], bias_c[d],
                                            alog_r[d], alog_c[d], dvec, rev=(d == 1), n_ctx_chunks=ncc,
                                            direct=(d == 0), name=f"ssd_scanb{d}")
        dxbcs.append(dxd)
        ddts.append(ddtd)
        dalogs.append(dal[0])
        dbiases.append(dbi[0])
    g["ssd_A_log"] = jnp.stack(dalogs)
    g["ssd_dt_bias"] = jnp.stack(dbiases)
    dxbc_pre, dcw8, dcb = _conv_bwd(dxbcs[0], dxbcs[1], cpre, xbc_pre, small["conv_w8"], n_ctx=n_ctx, name="ssd_convb")
    g["ssd_conv_w"] = dcw8[:SSD_CONV]
    g["ssd_conv_b"] = dcb[0]
    ddt_bf = jnp.concatenate([ddts[0], ddts[1]], axis=1).astype(BF16)
    n_in = SSD_INNER + SSD_CONV_DIM + 2 * nh
    dw_t = _mm_tn(dz, um0, name="ssd_dwz", col_blocks=1, stack=(n_in, 0, None))
    dw_t = _mm_tn(dxbc_pre, um0, name="ssd_dwxbc", col_blocks=1, stack=(n_in, SSD_INNER, dw_t))
    dw_t = _mm_tn(ddt_bf, um0, name="ssd_dwdt", col_blocks=1, stack=(n_in, SSD_INNER + SSD_CONV_DIM, dw_t))
    pending[0] = put_grad("ssd", "w_in", dw_t)
    dum0 = _mm(dz, win_ssd, out_dtype=F32, name="ssd_dum_z", tk=SSD_INNER, n=D_MODEL)
    dum0 = _mm(dxbc_pre, win_ssd, out_dtype=F32, name="ssd_dum_x", tk=SSD_INNER, n=D_MODEL,
               b_off=(SSD_INNER // SSD_INNER, 0), add=dum0)
    dum0 = _mm(ddt_bf, win_ssd, out_dtype=F32, name="ssd_dum_dt", tk=2 * nh, n=D_MODEL, b_off=(dt_blk, 0), add=dum0)
    dh0, dsh, dsc, dgp = _rowwise("l0m_preb", _pre_bwd_fn, n0, [dum0, h1, (dh, D_MODEL, 0, -(n_ctx // tm0))],
                                  [("full", gvec(0, 2)), ("seg", modrow(0, 4, 2))], [(D_MODEL, F32)],
                                  [D_MODEL, D_MODEL, D_MODEL], tm=tm0, **kw0)
    put_mod(0, 3, dsh)
    put_mod(0, 4, dsc)
    put_g(0, 2, dgp)
    dh0 = ffn_back("l0f1", 0, 0, dh0, sv_f01, wts["ffn00"], L0)
    grad_x = dh0[n_ctx:]
    g["norm_g"] = jnp.stack([jnp.stack(r) for r in dng])
    g["dmx"] = jnp.stack([jnp.concatenate(r) for r in dmx])
    g["dmc"] = jnp.stack([jnp.concatenate(r) for r in dmc])
    return loss_parts[0], grad_x, g


GROUPS = ("ffn00", "ssd", "ffn01", "ffn10", "gm", "ffn11")


TRANSPOSED_IN = ("ffn", "ssd")


def _is_transposed(group):
    return group.startswith(TRANSPOSED_IN)


def _mats_in(group, win_l):
    if _is_transposed(group):
        return {("win_t" if group.startswith("ffn") else group + "_win_t"): win_l.reshape(-1, win_l.shape[2])}
    return {group + "_win": win_l}


def _mats_out(group, wout_l):
    pre = "" if group.startswith("ffn") else group + "_"
    return {pre + "wout": wout_l.reshape(-1, wout_l.shape[2])}


def _group_mats(group, lands):
    m = {**_mats_in(group, lands[0]), **_mats_out(group, lands[1])}
    return {group: m} if group.startswith("ffn") else m


def _grad_blocks(which, grad):
    if grad.ndim == 3:
        return grad if grad.shape[0] == NDEV else grad.reshape(NDEV, grad.shape[1] // NDEV, grad.shape[2])
    if which == "w_in":
        k, n = grad.shape
        return jnp.transpose(grad.reshape(k, NDEV, n // NDEV), (1, 0, 2)).astype(BF16)
    return grad.reshape(NDEV, grad.shape[0] // NDEV, grad.shape[1]).astype(BF16)


def kernel(x, c, ctx, c_ctx, ada_w, ada_b, norm_g, ffn_w_in, ffn_w_out, ssd_w_in, ssd_conv_w, ssd_conv_b, ssd_dt_bias, ssd_A_log, ssd_D, ssd_norm_g, ssd_w_out, gm_w_in, gm_v_g, gm_v_b, gm_w_s, gm_b_s, gm_w_out, loss_target, m_c_ctx, m_ada_w, m_ada_b, m_norm_g, m_ffn_w_in, m_ffn_w_out, m_ssd_w_in, m_ssd_conv_w, m_ssd_conv_b, m_ssd_dt_bias, m_ssd_A_log, m_ssd_D, m_ssd_norm_g, m_ssd_w_out, m_gm_w_in, m_gm_v_g, m_gm_v_b, m_gm_w_s, m_gm_b_s, m_gm_w_out, v_c_ctx, v_ada_w, v_ada_b, v_norm_g, v_ffn_w_in, v_ffn_w_out, v_ssd_w_in, v_ssd_conv_w, v_ssd_conv_b, v_ssd_dt_bias, v_ssd_A_log, v_ssd_D, v_ssd_norm_g, v_ssd_w_out, v_gm_w_in, v_gm_v_g, v_gm_v_b, v_gm_w_s, v_gm_b_s, v_gm_w_out):
    me = 4 * lax.axis_index("x") + 2 * lax.axis_index("y") + lax.axis_index("c")
    d = D_MODEL
    ncol = N_MOD * d // NDEV

    small_pack = jnp.concatenate([c.reshape(-1), norm_g.reshape(-1), ssd_conv_w.reshape(-1),
                                  gm_v_g.reshape(-1), gm_v_b.reshape(-1)])[None, :]
    (sp,), _ = _exchange([small_pack], scatter=False, name="gather_small")
    sp = sp[:, 0]
    o = 0
    c_all = sp[:, o:o + d]; o += d
    ng_all = sp[:, o:o + 2 * 6 * 128].reshape(NDEV, 2, 6, 128); o += 2 * 6 * 128
    cw_all = sp[:, o:o + SSD_CONV * 512].reshape(NDEV, SSD_CONV, 512); o += SSD_CONV * 512
    vg_all = sp[:, o:o + 256]; o += 256
    vb_all = sp[:, o:o + 256]; o += 256
    norm_g_full = jnp.transpose(ng_all, (1, 2, 0, 3)).reshape(2, 6, d)
    conv_w_full = jnp.transpose(cw_all, (1, 0, 2)).reshape(SSD_CONV, SSD_CONV_DIM)
    gm_v_g_full = vg_all.reshape(-1)
    gm_v_b_full = vb_all.reshape(-1)

    c16 = jnp.concatenate([c_all, jnp.broadcast_to(c_ctx[None, :], (NDEV, d))], axis=0)
    ada_b_loc = lax.dynamic_slice_in_dim(ada_b, me * ncol, ncol, axis=1)
    mods_loc = jnp.stack([_mm_f32(c16, ada_w[i], name=f"ada_mod{i}", silu_a=True, bias=ada_b_loc[i][None, :])
                          for i in range(2)])
    (mods_all,), mods_done = _exchange([mods_loc], scatter=False, name="gather_mods")

    tr = lambda a: jnp.swapaxes(a, -1, -2)
    shard = {"ssd": (tr(ssd_w_in)[0], ssd_w_out[0]), "gm": (gm_w_in[0], gm_w_out[0])}
    for i in range(2):
        for j in range(2):
            shard[f"ffn{i}{j}"] = (tr(ffn_w_in)[i, j], ffn_w_out[i, j])
    apart = GROUPS[:2]
    units = []
    for grp in GROUPS:
        units += [(grp + "_in", grp, (0,)), (grp + "_out", grp, (1,))] if grp in apart else [(grp, grp, (0, 1))]
    gathers = {}
    started = mods_done
    for unit, grp, idx in units:
        srcs = [(shard[grp][k] + started).astype(BF16) for k in idx]
        st = _exchange_start(srcs, [_landing(s, me) for s in srcs], scatter=False, name="gather_start_" + unit)
        gathers[unit] = st[:4]
        started = st[4]

    def fetch(unit, after):
        return _exchange_wait(*gathers[unit], after, scatter=False, name="gather_wait_" + unit)

    def get_w(grp, after):
        if grp not in apart:
            return _group_mats(grp, fetch(grp, after))
        early = lambda later: _mats_in(grp, fetch(grp + "_in", later)[0])
        late = lambda later: _mats_out(grp, fetch(grp + "_out", later)[0])
        if grp.startswith("ffn"):
            return {grp: dict(early=early, late=late)}
        return dict(early(after), late=late)

    scatters = {}
    held = {}

    def put_grad(grp, which, grad):
        if grp in apart:
            unit, blocks = grp + "_" + which[2:], [_grad_blocks(which, grad)]
        else:
            held[grp, which] = _grad_blocks(which, grad)
            if (grp, "w_in") not in held or (grp, "w_out") not in held:
                return None
            unit, blocks = grp, [held[grp, "w_in"], held[grp, "w_out"]]
        own = [lax.dynamic_index_in_dim(b, me, axis=0, keepdims=False) for b in blocks]
        st = _exchange_start(blocks, [_landing(o_, me) for o_ in own], scatter=True, name="scatter_start_" + unit)
        scatters[unit] = st[:4]
        return st[4]

    mods_rows = jnp.transpose(mods_all, (1, 2, 0, 3)).reshape(2, 2 * NDEV, N_MOD * d) + started
    mx = lax.dynamic_index_in_dim(mods_rows, me, axis=1, keepdims=False).reshape(2, N_MOD, d)
    mc = mods_rows[:, NDEV].reshape(2, N_MOD, d)
    mods = [(mc[i], mx[i]) for i in range(2)]

    small = dict(conv_w8=jnp.pad(conv_w_full, ((0, 8 - SSD_CONV), (0, 0))), conv_b=ssd_conv_b, dt_bias=ssd_dt_bias[0],
                 a_log=ssd_A_log[0], ssd_d=ssd_D[0], ssd_norm_g=ssd_norm_g[0], gm_v_g=gm_v_g_full,
                 gm_v_b=gm_v_b_full, gm_w_s=gm_w_s[0], gm_b_s=gm_b_s[0])
    loss_parts, grad_x, g = _local_step(x[0], ctx[0], loss_target[0], mods, norm_g_full, get_w, small, put_grad)
    g["loss"] = (0.5 / d * jnp.sum(loss_parts)).reshape(1)

    whole = {"ffn_w_in": (tr(ffn_w_in), tr(m_ffn_w_in), tr(v_ffn_w_in)), "ffn_w_out": (ffn_w_out, m_ffn_w_out, v_ffn_w_out),
             "ssd_w_in": (tr(ssd_w_in), tr(m_ssd_w_in), tr(v_ssd_w_in)), "ssd_w_out": (ssd_w_out, m_ssd_w_out, v_ssd_w_out),
             "gm_w_in": (gm_w_in, m_gm_w_in, v_gm_w_in), "gm_w_out": (gm_w_out, m_gm_w_out, v_gm_w_out)}
    res = {}

    def update_units(some, after):
        for unit, grp, idx in some:
            parts = _exchange_wait(*scatters[unit], after, scatter=True, name="scatter_wait_" + unit)
            for k, p in zip(idx, parts):
                which = ("in", "out")[k]
                nm = ("ffn" if grp.startswith("ffn") else grp) + "_w_" + which
                sel = (int(grp[3]), int(grp[4])) if grp.startswith("ffn") else (0,)
                res[nm] = _adamw(p, *whole[nm], name=f"adamw_{grp}_{which}", sel=sel, into=res.get(nm))
                after = res[nm][0]
        return after

    sg_names = ["dmx", "dmc", "norm_g", "ssd_conv_w", "ssd_conv_b", "ssd_dt_bias", "ssd_A_log", "ssd_D", "ssd_norm_g",
                "gm_v_g", "gm_v_b", "gm_w_s", "gm_b_s", "loss"]
    sg_shapes = [g[n].shape for n in sg_names]
    flat = jnp.concatenate([g[n].reshape(-1) for n in sg_names])
    npack = flat.shape[0]
    pad = (-npack) % 1024
    flat = jnp.pad(flat, (0, pad)).reshape(-1, 128)
    sg_start = _exchange_start([flat], [_landing(flat, me)], scatter=False, name="small_grads_start")
    by_send = list(reversed(units))
    update_units(by_send[:4], jnp.stack([sg_start[4], grad_x[0, 0]]))
    early_done = jnp.stack([res[nm][0].reshape(-1)[-1] for nm in sorted(res)])
    (sg_all,) = _exchange_wait(*sg_start[:4], early_done, scatter=False, name="small_grads_wait")
    sg_sum = _sum_slots(sg_all, name="sum_small_grads").reshape(-1)[:npack]
    update_units(by_send[4:], sg_sum)
    sums = {}
    o = 0
    for n, shp in zip(sg_names, sg_shapes):
        sz = math.prod(shp)
        sums[n] = sg_sum[o:o + sz].reshape(shp)
        o += sz
    loss = sums["loss"][0]
    per_dev = sg_all.reshape(NDEV, -1)
    dmx_all =per_dev[:, :2 * N_MOD * d].reshape(NDEV, 2, N_MOD * d)
    dmc_all = per_dev[:, 2 * N_MOD * d:4 * N_MOD * d].reshape(NDEV, 2, N_MOD * d)

    (s16,) = _rowwise("ada_silu", lambda cc: ((_silu(cc),), ()), 2 * NDEV, [c16], [], [(d, F32)], tm=2 * NDEV)
    s16_t = s16.T
    g_ada_w, dcc_parts = [], []
    for i in range(2):
        rhs = jnp.concatenate([lax.dynamic_slice_in_dim(dmx_all[:, i], me * ncol, ncol, axis=1),
                               lax.dynamic_slice_in_dim(dmc_all[:, i], me * ncol, ncol, axis=1)], axis=0)
        g_ada_w.append(_mm_f32(s16_t, rhs, name=f"ada_dw{i}"))
        dmc_loc = lax.dynamic_slice_in_dim(sums["dmc"][i], me * ncol, ncol, axis=0)
        rhs_c = jnp.zeros((ncol, 128), F32).at[:, 0].set(dmc_loc)
        dcc_parts.append(_mm_f32(ada_w[i], rhs_c, name=f"ada_dcc{i}")[:, 0])
    g_ada_w = jnp.stack(g_ada_w)
    dcc_part = (dcc_parts[0] + dcc_parts[1]).reshape(8, 128)
    (dcc_all,), _ = _exchange([dcc_part], scatter=False, name="gather_dcc")
    g_c_ctx = _sum_slots(dcc_all, name="sum_dcc", scale_by=c_ctx.reshape(8, 128)).reshape(d)
    g_ada_b = sums["dmx"] + sums["dmc"]

    outs = _adamw(g_ada_w.reshape(1, -1, ncol), ada_w.reshape(-1, ncol), m_ada_w.reshape(-1, ncol),
                  v_ada_w.reshape(-1, ncol), name="adamw_ada_w")
    res["ada_w"] = [o_.reshape(ada_w.shape) for o_ in outs]

    loc = lambda a, ax, n: lax.dynamic_slice_in_dim(a, me * n, n, axis=ax)
    small_g = dict(c_ctx=g_c_ctx, ada_b=g_ada_b, norm_g=loc(sums["norm_g"], 2, 128),
                   ssd_conv_w=loc(sums["ssd_conv_w"], 1, 512)[None], ssd_conv_b=sums["ssd_conv_b"][None],
                   ssd_dt_bias=sums["ssd_dt_bias"][None], ssd_A_log=sums["ssd_A_log"][None], ssd_D=sums["ssd_D"][None],
                   ssd_norm_g=sums["ssd_norm_g"][None], gm_v_g=loc(sums["gm_v_g"], 0, 256)[None],
                   gm_v_b=loc(sums["gm_v_b"], 0, 256)[None], gm_w_s=sums["gm_w_s"][None], gm_b_s=sums["gm_b_s"][None])
    small_w = dict(c_ctx=(c_ctx, m_c_ctx, v_c_ctx), ada_b=(ada_b, m_ada_b, v_ada_b), norm_g=(norm_g, m_norm_g, v_norm_g),
                   ssd_conv_w=(ssd_conv_w, m_ssd_conv_w, v_ssd_conv_w), ssd_conv_b=(ssd_conv_b, m_ssd_conv_b, v_ssd_conv_b),
                   ssd_dt_bias=(ssd_dt_bias, m_ssd_dt_bias, v_ssd_dt_bias), ssd_A_log=(ssd_A_log, m_ssd_A_log, v_ssd_A_log),
                   ssd_D=(ssd_D, m_ssd_D, v_ssd_D), ssd_norm_g=(ssd_norm_g, m_ssd_norm_g, v_ssd_norm_g),
                   gm_v_g=(gm_v_g, m_gm_v_g, v_gm_v_g), gm_v_b=(gm_v_b, m_gm_v_b, v_gm_v_b),
                   gm_w_s=(gm_w_s, m_gm_w_s, v_gm_w_s), gm_b_s=(gm_b_s, m_gm_b_s, v_gm_b_s))
    sn = list(small_w)

    def pack(arrs):
        f = jnp.concatenate([a.reshape(-1) for a in arrs])
        return jnp.pad(f, (0, (-f.shape[0]) % (256 * 128))).reshape(-1, 128)

    pg = pack([small_g[n].reshape(small_w[n][0].shape) for n in sn])
    outs = _adamw(pg[None], pack([small_w[n][0] for n in sn]), pack([small_w[n][1] for n in sn]),
                  pack([small_w[n][2] for n in sn]), name="adamw_small")
    flat_outs = [o_.reshape(-1) for o_ in outs]
    o = 0
    for n in sn:
        shp = small_w[n][0].shape
        sz = math.prod(shp)
        res[n] = [fo[o:o + sz].reshape(shp) for fo in flat_outs]
        o += sz

    order = ["c_ctx", "ada_w", "ada_b", "norm_g", "ffn_w_in", "ffn_w_out", "ssd_w_in", "ssd_conv_w", "ssd_conv_b",
             "ssd_dt_bias", "ssd_A_log", "ssd_D", "ssd_norm_g", "ssd_w_out", "gm_w_in", "gm_v_g", "gm_v_b", "gm_w_s",
             "gm_b_s", "gm_w_out"]
    for nm in ("ffn_w_in", "ssd_w_in"):
        res[nm] = [tr(a) for a in res[nm]]
    result = [loss, grad_x[None]]
    for k in range(4):
        result += [res[n][k] for n in order]
    return tuple(result)
```

```python
import functools
import math

import jax
import jax.numpy as jnp
from jax import lax
from jax.experimental import pallas as pl
from jax.experimental.pallas import tpu as pltpu

F32 = jnp.float32
BF16 = jnp.bfloat16

NDEV = 8
D_MODEL = 1024
FFN_DIM = 2816
N_MOD = 9
EPS = 1e-6
SSD_INNER = 2048
SSD_HEADS = 32
SSD_HEAD_DIM = 64
SSD_GROUPS = 8
SSD_HPG = 4
SSD_STATE = 128
SSD_CONV = 5
SSD_CONV_DIM = 4096
CHUNK = 128
GM_INNER = 2048
GM_GROUPS = 8
GM_GROUP_DIM = 256
ADAM_LR = 0.001
ADAM_B1 = 0.9
ADAM_B2 = 0.999
ADAM_EPS = 1e-08
ADAM_WD = 0.01
ADAM_STEP = 10
NEG_BIG = -1e30
VMEM_LIMIT_BYTES = 56 * 1024 * 1024
HI = lax.Precision.HIGHEST


def _params(*sem):
    return pltpu.CompilerParams(dimension_semantics=sem, vmem_limit_bytes=VMEM_LIMIT_BYTES)


def _pick(n, target, mult=16):
    if n <= target:
        return n
    for t in range(target - target % mult, 0, -mult):
        if n % t == 0:
            return t
    raise ValueError((n, target, mult))


def _sig(x):
    return 0.5 * jnp.tanh(0.5 * x) + 0.5


def _silu(x):
    return x * _sig(x)


def _dsilu(x):
    s = _sig(x)
    return s * (1.0 + x * (1.0 - s))


_GELU_C = math.sqrt(2.0 / math.pi)


def _gelu(x):
    return 0.5 * x * (1.0 + jnp.tanh(_GELU_C * (x + 0.044715 * x * x * x)))


def _gelu_and_grad(x):
    x2 = x * x
    t = jnp.tanh(_GELU_C * (x + 0.044715 * x2 * x))
    half = 0.5 * (1.0 + t)
    return x * half, half + 0.5 * x * (1.0 - t * t) * _GELU_C * (1.0 + 3.0 * 0.044715 * x2)


def _dgelu(x):
    return _gelu_and_grad(x)[1]


def _softplus(x):
    return jnp.maximum(x, 0.0) + jnp.log1p(jnp.exp(-jnp.abs(x)))


def _sum0(v):
    return jnp.sum(v, axis=0, keepdims=True)


def _rms(h):
    r = lax.rsqrt(jnp.mean(h * h, axis=-1, keepdims=True) + EPS)
    return h * r, r


def _dot(a, b, dims=((1,), (0,)), precision=None):
    return lax.dot_general(a, b, (dims, ((), ())), preferred_element_type=F32, precision=precision)


_NT = ((1,), (1,))
_TN = ((0,), (0,))


def _rowwise(name, fn, n_rows, rows, consts, outs, accs=(), *, tm, nseg=1, seg_blocks=0):
    assert n_rows % tm == 0
    if nseg == 2:
        assert seg_blocks > 0
        seg = lambda i: jnp.where(i < seg_blocks, 0, 1)
    else:
        seg = lambda i: 0
    in_specs, args, lacking = [], [], []
    for r in rows:
        arr, width, cb, off = r if isinstance(r, tuple) else (r, r.shape[1], 0, 0)
        in_specs.append(pl.BlockSpec((tm, width), lambda i, cb=cb, off=off: (jnp.maximum(i + off, 0), cb)))
        args.append(arr)
        lacking.append(-off if off < 0 else 0)
    for kind, arr in consts:
        if kind == "seg":
            assert arr.shape[0] == nseg and arr.shape[1] == 1, arr.shape
            in_specs.append(pl.BlockSpec((None, 1, arr.shape[2]), lambda i: (seg(i), 0, 0)))
        else:
            in_specs.append(pl.BlockSpec(arr.shape, lambda i: (0, 0)))
        args.append(arr)
    out_shape = [jax.ShapeDtypeStruct((n_rows, w), dt) for w, dt in outs]
    out_specs = [pl.BlockSpec((tm, w), lambda i: (i, 0)) for w, _ in outs]
    out_shape += [jax.ShapeDtypeStruct((nseg, 1, w), F32) for w in accs]
    out_specs += [pl.BlockSpec((None, 1, w), lambda i: (seg(i), 0, 0)) for w in accs]
    n_in, n_out, n_acc = len(args), len(outs), len(accs)

    def kern(*refs):
        i = pl.program_id(0)
        ins = [r[...] for r in refs[:n_in]]
        for k, lack in enumerate(lacking):
            if lack:
                ins[k] = jnp.where(i >= lack, ins[k], jnp.zeros_like(ins[k]))
        res, terms = fn(*ins)
        for ref, v in zip(refs[n_in:n_in + n_out], res):
            ref[...] = v.astype(ref.dtype)
        if n_acc:
            sums = [_sum0(v) for v in terms]
            first = (i == 0) | (i == seg_blocks) if nseg == 2 else (i == 0)
            acc_refs = refs[n_in + n_out:]

            @pl.when(first)
            def _():
                for ref, v in zip(acc_refs, sums):
                    ref[...] = v

            @pl.when(jnp.logical_not(first))
            def _():
                for ref, v in zip(acc_refs, sums):
                    ref[...] += v

    res = pl.pallas_call(
        kern, name=name, grid=(n_rows // tm,), in_specs=in_specs, out_specs=out_specs, out_shape=out_shape,
        compiler_params=_params("arbitrary"),
    )(*args)
    return res


def _pre_fwd_fn(h, g, shift, scale):
    hh, _ = _rms(h)
    return (hh * g * (1.0 + scale) + shift,), ()


def _pre_bwd_fn(du, h, dres, g, scale):
    hh, r = _rms(h)
    n = hh * g
    dn = du * (1.0 + scale)
    dhh = dn * g
    dh = dres + r * (dhh - hh * jnp.mean(dhh * hh, axis=-1, keepdims=True))
    return (dh,), (du, du * n, dn * hh)


def _post_fwd_fn(weight, h, y, g, gate):
    yh, _ = _rms(y)
    return (h + weight * gate * (yh * g),), ()


def _out_post_fn(weight, y, h, g, gate):
    return (y,) + _post_fwd_fn(weight, h, y, g, gate)[0], ()


def _post_bwd_fn(weight, dh, y, g, gate):
    yh, r = _rms(y)
    dr = dh * weight
    dyh = dr * gate * g
    dy = r * (dyh - yh * jnp.mean(dyh * yh, axis=-1, keepdims=True))
    return (dy,), (dr * yh * g, dr * gate * yh)


def _glu_bwd_fn(ds, a, b):
    a = a.astype(F32)
    b = b.astype(F32)
    sg = _sig(a)
    da = ds * b * (sg * (1.0 + a * (1.0 - sg)))
    db = ds * (a * sg)
    return (jnp.concatenate([da, db], axis=1),), ()


def _loss_fn(y, t):
    diff = y - t
    return (diff * (1.0 / D_MODEL),), (diff * diff,)


def _ssd_y(yf, yb, xs, z, dvec):
    y = yf + yb + dvec * xs
    return y, y * _silu(z)


def _ssdgate_fwd_fn(yf, yb, xs, z, dvec, ng):
    _, yg = _ssd_y(yf, yb, xs, z, dvec)
    parts = []
    for g in range(SSD_GROUPS):
        sl = slice(g * 256, (g + 1) * 256)
        parts.append(_rms(yg[:, sl])[0])
    return (jnp.concatenate(parts, axis=1) * ng,), ()


def _ssdgate_bwd_fn(dyn, yf, yb, xs, z, dvec, ng):
    y, yg = _ssd_y(yf, yb, xs, z, dvec)
    dyg_parts, ygh_parts = [], []
    for g in range(SSD_GROUPS):
        sl = slice(g * 256, (g + 1) * 256)
        ygh, r = _rms(yg[:, sl])
        d = dyn[:, sl] * ng[:, sl]
        dyg_parts.append(r * (d - ygh * jnp.mean(d * ygh, axis=-1, keepdims=True)))
        ygh_parts.append(ygh)
    dyg = jnp.concatenate(dyg_parts, axis=1)
    ygh = jnp.concatenate(ygh_parts, axis=1)
    dy = dyg * _silu(z)
    dz = dyg * y * _dsilu(z)
    return (dy, dz), (dyn * ygh, dy * xs)


def _ln_stats(v):
    mu = jnp.mean(v, axis=-1, keepdims=True)
    vc = v - mu
    r = lax.rsqrt(jnp.mean(vc * vc, axis=-1, keepdims=True) + EPS)
    return vc * r, r


def _gm_act_fwd_fn(p, vg, vb):
    gu = _gelu(p[:, :GM_INNER])
    gvh, _ = _ln_stats(_gelu(p[:, GM_INNER:]))
    return (gu, gvh * vg + vb), ()


def _gm_act_bwd_fn(p, dgu, dgvn, vg):
    pu = p[:, :GM_INNER]
    pv = p[:, GM_INNER:]
    gv, dgelu_v = _gelu_and_grad(pv)
    gvh, r = _ln_stats(gv)
    dgvh = dgvn * vg
    dgv = r * (dgvh - jnp.mean(dgvh, axis=-1, keepdims=True) - gvh * jnp.mean(dgvh * gvh, axis=-1, keepdims=True))
    dp = jnp.concatenate([dgu * _dgelu(pu), dgv * dgelu_v], axis=1)
    return (dp,), (dgvn * gvh, dgvn)


def _mm(a, b, *, out_dtype, name, tm=1088, tn=1024, tk=1408, add=None, rhs_t=False, n=None, b_off=(0, 0)):
    m, k = a.shape
    col_blocked = b.ndim == 3
    if col_blocked:
        assert not rhs_t and n is None and b.shape[1] == k
        n, tn = b.shape[0] * b.shape[2], b.shape[2]
    elif n is None:
        n, k2 = b.shape if rhs_t else b.shape[::-1]
        assert k == k2
    tm, tn, tk = _pick(m, tm), _pick(n, tn, 128), _pick(k, tk, 128)
    o0, o1 = b_off
    nk = k // tk
    dims = _NT if rhs_t else ((1,), (0,))

    def kern(*refs):
        a_ref, b_ref = refs[:2]
        add_ref = refs[2] if add is not None else None
        o_ref = refs[3] if add is not None else refs[2]

        def finish(r):
            if add is not None:
                r = r + add_ref[...]
            o_ref[...] = r.astype(o_ref.dtype)

        p = _dot(a_ref[...], b_ref[...], dims)
        if nk == 1:
            finish(p)
            return
        acc_ref = refs[-1]
        kk = pl.program_id(2)

        @pl.when(kk == 0)
        def _():
            acc_ref[...] = p

        @pl.when((kk > 0) & (kk < nk - 1))
        def _():
            acc_ref[...] += p

        @pl.when(kk == nk - 1)
        def _():
            finish(acc_ref[...] + p)

    if col_blocked:
        b_spec = pl.BlockSpec((None, tk, tn), lambda i, j, kk: (j, kk, 0))
    elif rhs_t:
        b_spec = pl.BlockSpec((tn, tk), lambda i, j, kk: (j + o0, kk + o1))
    else:
        b_spec = pl.BlockSpec((tk, tn), lambda i, j, kk: (kk + o0, j + o1))
    in_specs = [pl.BlockSpec((tm, tk), lambda i, j, kk: (i, kk)), b_spec]
    args = [a, b]
    if add is not None:
        in_specs.append(pl.BlockSpec((tm, tn), lambda i, j, kk: (i, j)))
        args.append(add)
    return pl.pallas_call(
        kern, name=name, grid=(m // tm, n // tn, nk), in_specs=in_specs,
        out_specs=pl.BlockSpec((tm, tn), lambda i, j, kk: (i, j)),
        out_shape=jax.ShapeDtypeStruct((m, n), out_dtype),
        scratch_shapes=[pltpu.VMEM((tm, tn), F32)] if nk > 1 else [],
        compiler_params=_params("parallel", "parallel", "arbitrary"),
    )(*args)


def _mm_rows(a, b, fn, rows, consts, outs, accs=(), *, name, tm=544, tk=1408, rhs_t=False, n_ctx=0):
    halves = a.ndim == 3
    m, k = (a.shape[1], 2 * a.shape[2]) if halves else a.shape
    col_blocked = b.ndim == 3
    kb, nb = 1, None
    if col_blocked:
        assert rhs_t and b.shape[0] * b.shape[2] == k
        n, nb = b.shape[1], b.shape[2]
        kb = max(1, tk // nb)
        assert b.shape[0] % kb == 0
        tk = kb * nb
    else:
        n = b.shape[0] if rhs_t else b.shape[1]
    tm, tk = _pick(m, tm), _pick(k, tk, 128)
    nk = k // tk
    if halves:
        hb = k // 2 // tk
        a_spec = pl.BlockSpec((None, tm, tk), lambda i, kk: (kk // hb, i, kk % hb))
    else:
        a_spec = pl.BlockSpec((tm, tk), lambda i, kk: (i, kk))
    dims = _NT if rhs_t else ((1,), (0,))
    n_rows, n_const, n_out, n_acc = len(rows), len(consts), len(outs), len(accs)

    def kern(*refs):
        a_ref, b_ref = refs[:2]
        row_refs = refs[2:2 + n_rows]
        const_refs = refs[2 + n_rows:2 + n_rows + n_const]
        out_refs = refs[2 + n_rows + n_const:2 + n_rows + n_const + n_out]
        acc_refs = refs[2 + n_rows + n_const + n_out:2 + n_rows + n_const + n_out + n_acc]
        i, kk = pl.program_id(0), pl.program_id(1)

        def finish(p, rs=slice(None), r0=0):
            nr = p.shape[0]
            is_ctx = (i * tm + r0 + lax.broadcasted_iota(jnp.int32, (nr, 1), 0)) < n_ctx
            cvals = []
            for (kind, arr), ref in zip(consts, const_refs):
                if kind == "seg":
                    cvals.append(jnp.where(is_ctx, ref[0], ref[1]) if arr.shape[0] == 2 else ref[0])
                else:
                    cvals.append(ref[...])
            res, terms = fn(p, *[r[rs, :] for r in row_refs], *cvals)
            for ref, v in zip(out_refs, res):
                ref[rs, :] = v.astype(ref.dtype)
            for ref, v in zip(acc_refs, terms):
                s_all = _sum0(v)
                s_ctx = _sum0(jnp.where(is_ctx, v, 0.0)) if n_ctx else jnp.zeros_like(s_all)
                both = jnp.concatenate([s_ctx, s_all - s_ctx], axis=0)[:, None, :]

                @pl.when(i == 0)
                def _():
                    ref[...] = both

                @pl.when(i > 0)
                def _():
                    ref[...] += both

        if nk == 1 and n_acc == 0:
            nsub = 2 if tm % 32 == 0 else 1
            sub = tm // nsub
            for r in range(nsub):
                rs = slice(r * sub, (r + 1) * sub)
                finish(_dot(a_ref[rs, :], b_ref[...], dims), rs, r * sub)
            return
        if col_blocked:
            p = sum(_dot(a_ref[:, c * nb:(c + 1) * nb], b_ref[c], dims) for c in range(kb))
        else:
            p = _dot(a_ref[...], b_ref[...], dims)
        if nk == 1:
            finish(p)
            return
        scr = refs[-1]

        @pl.when(kk == 0)
        def _():
            scr[...] = p

        @pl.when((kk > 0) & (kk < nk - 1))
        def _():
            scr[...] += p

        @pl.when(kk == nk - 1)
        def _():
            finish(scr[...] + p)

    if col_blocked:
        b_spec = pl.BlockSpec((kb, n, nb), lambda i, kk: (kk, 0, 0))
    elif rhs_t:
        b_spec = pl.BlockSpec((n, tk), lambda i, kk: (0, kk))
    else:
        b_spec = pl.BlockSpec((tk, n), lambda i, kk: (kk, 0))
    in_specs = [a_spec, b_spec]
    in_specs += [pl.BlockSpec((tm, r.shape[1]), lambda i, kk: (i, 0)) for r in rows]
    for kind, arr in consts:
        in_specs.append(pl.BlockSpec(arr.shape, (lambda i, kk: (0, 0, 0)) if kind == "seg" else (lambda i, kk: (0, 0))))
    out_shape = [jax.ShapeDtypeStruct((m, w), dt) for w, dt in outs]
    out_specs = [pl.BlockSpec((tm, w), lambda i, kk: (i, 0)) for w, _ in outs]
    out_shape += [jax.ShapeDtypeStruct((2, 1, w), F32) for w in accs]
    out_specs += [pl.BlockSpec((2, 1, w), lambda i, kk: (0, 0, 0)) for w in accs]
    return pl.pallas_call(
        kern, name=name, grid=(m // tm, nk), in_specs=in_specs, out_specs=out_specs, out_shape=out_shape,
        scratch_shapes=[pltpu.VMEM((tm, n), F32)] if nk > 1 else [],
        compiler_params=_params("arbitrary", "arbitrary"),
    )(a, b, *rows, *[arr for _, arr in consts])


def _mm_glu(u, win_t, *, name, tm=2176, tn=256):
    m, k = u.shape
    n = win_t.shape[0] // 2
    tm, tn = _pick(m, tm), _pick(n, tn, 128)
    nj = n // tn

    nsub = 4 if tm % 64 == 0 else 1
    sub = tm // nsub

    def kern(u_ref, wa_ref, wb_ref, s_ref, a_ref, b_ref):
        for r in range(nsub):
            rows = slice(r * sub, (r + 1) * sub)
            uu = u_ref[rows, :]
            a = _dot(uu, wa_ref[...], _NT)
            b = _dot(uu, wb_ref[...], _NT)
            s_ref[rows, :] = (_silu(a) * b).astype(BF16)
            a_ref[rows, :] = a.astype(BF16)
            b_ref[rows, :] = b.astype(BF16)

    ospec = pl.BlockSpec((tm, tn), lambda i, j: (i, j))
    return pl.pallas_call(
        kern, name=name, grid=(m // tm, nj),
        in_specs=[pl.BlockSpec((tm, k), lambda i, j: (i, 0)), pl.BlockSpec((tn, k), lambda i, j: (j, 0)),
                  pl.BlockSpec((tn, k), lambda i, j: (nj + j, 0))],
        out_specs=[ospec, ospec, ospec],
        out_shape=[jax.ShapeDtypeStruct((m, n), BF16)] * 3,
        compiler_params=_params("parallel", "parallel"),
    )(u, win_t, win_t)


def _mm_glu_bwd(dy, wout, a, b, *, name, tm=544, tn=1408):
    m, k = dy.shape
    f = wout.shape[0]
    tm, tn = _pick(m, tm), _pick(f, tn, 128)
    nsub = 2 if tm % 32 == 0 else 1
    sub = tm // nsub

    def kern(dy_ref, w_ref, a_ref, b_ref, o_ref):
        for r in range(nsub):
            rs = slice(r * sub, (r + 1) * sub)
            ds = _dot(dy_ref[rs, :], w_ref[...], _NT)
            (dp,), _ = _glu_bwd_fn(ds, a_ref[rs, :], b_ref[rs, :])
            o_ref[0, rs, :] = dp[:, :tn].astype(BF16)
            o_ref[1, rs, :] = dp[:, tn:].astype(BF16)

    tile = pl.BlockSpec((tm, tn), lambda i, j: (i, j))
    return pl.pallas_call(
        kern, name=name, grid=(m // tm, f // tn),
        in_specs=[pl.BlockSpec((tm, k), lambda i, j: (i, 0)), pl.BlockSpec((tn, k), lambda i, j: (j, 0)), tile, tile],
        out_specs=pl.BlockSpec((2, tm, tn), lambda i, j: (0, i, j)),
        out_shape=jax.ShapeDtypeStruct((2, m, f), BF16),
        compiler_params=_params("parallel", "parallel"),
    )(dy, wout, a, b)


def _mm_tn(a, b, *, name, tm=1024, tn=1024, tk=2176, col_blocks=None, stack=None):
    extra, extra_specs, aliases = [], [], {}
    halves = a.ndim == 3
    t, m = (a.shape[1], 2 * a.shape[2]) if halves else a.shape
    t2, n = b.shape
    assert t == t2
    tm, tn, tk = _pick(m, tm, 128), _pick(n, tn, 128), _pick(t, tk)
    nk = t // tk
    if halves:
        hb = m // 2 // tm
        a_spec = pl.BlockSpec((None, tk, tm), lambda i, j, kk: (i // hb, kk, i % hb))
    else:
        a_spec = pl.BlockSpec((tk, tm), lambda i, j, kk: (kk, i))
    if col_blocks is None:
        def kern(a_ref, b_ref, o_ref):
            kk = pl.program_id(2)

            @pl.when(kk == 0)
            def _():
                o_ref[...] = jnp.zeros_like(o_ref)

            o_ref[...] += _dot(a_ref[...], b_ref[...], _TN)

        out_spec = pl.BlockSpec((tm, tn), lambda i, j, kk: (i, j))
        out_shape = jax.ShapeDtypeStruct((m, n), F32)
        scratch = []
    else:
        wb = n // col_blocks
        per = tn // wb
        assert tn % wb == 0 and wb % 8 == 0

        def kern(a_ref, b_ref, *rest):
            o_ref, acc_ref = rest[-2:]
            kk = pl.program_id(2)
            p = _dot(a_ref[...], b_ref[...], _TN)

            @pl.when(kk == 0)
            def _():
                acc_ref[...] = p

            @pl.when((kk > 0) & (kk < nk - 1))
            def _():
                acc_ref[...] += p

            @pl.when(kk == nk - 1)
            def _():
                r = acc_ref[...] + p if nk > 1 else p
                for c in range(per):
                    o_ref[c] = r[:, c * wb:(c + 1) * wb].astype(BF16)

        rows_total, row0, into = stack if stack is not None else (m, 0, None)
        assert row0 % tm == 0
        out_spec = pl.BlockSpec((per, tm, wb), lambda i, j, kk: (j, i + row0 // tm, 0))
        out_shape = jax.ShapeDtypeStruct((col_blocks, rows_total, wb), BF16)
        scratch = [pltpu.VMEM((tm, tn), F32)]
        if into is not None:
            extra, extra_specs, aliases = [into], [pl.BlockSpec(memory_space=pl.ANY)], {2: 0}

    return pl.pallas_call(
        kern, name=name, grid=(m // tm, n // tn, nk),
        in_specs=[a_spec, pl.BlockSpec((tk, tn), lambda i, j, kk: (kk, j))] + extra_specs,
        out_specs=out_spec, out_shape=out_shape, scratch_shapes=scratch, input_output_aliases=aliases,
        compiler_params=_params("parallel", "parallel", "arbitrary"),
    )(a, b, *extra)


def _mm_f32(a, b, *, name, silu_a=False, bias=None):
    m, k = a.shape
    n = b.shape[1]

    def kern(*refs):
        if bias is None:
            a_ref, b_ref, o_ref = refs
        else:
            a_ref, b_ref, bias_ref, o_ref = refs
        av = a_ref[...]
        if silu_a:
            av = _silu(av)
        r = jnp.dot(av, b_ref[...], preferred_element_type=F32, precision=HI)
        if bias is not None:
            r = r + bias_ref[...]
        o_ref[...] = r

    args = [a, b] + ([] if bias is None else [bias])
    return pl.pallas_call(kern, name=name, out_shape=jax.ShapeDtypeStruct((m, n), F32),
                          compiler_params=pltpu.CompilerParams(vmem_limit_bytes=VMEM_LIMIT_BYTES))(*args)


CONV_WIN = 32


def _conv_windows(n, n_ctx):
    assert n_ctx % CONV_WIN == 0 and n_ctx >= CONV_WIN and n - n_ctx >= CONV_WIN
    return (0, n_ctx - CONV_WIN // 2, n - CONV_WIN)


def _tap_outside(r0, s, n, n_ctx):
    t = r0 + lax.broadcasted_iota(jnp.int32, (CONV_WIN, 1), 0)
    lo = jnp.where(t < n_ctx, 0, n_ctx)
    hi = jnp.where(t < n_ctx, n_ctx, n)
    return jnp.where((t + s >= lo) & (t + s < hi), 0.0, 1.0)


def _rolled(v, s):
    return v if s == 0 else pltpu.roll(v, (-s) % v.shape[0], 0)


def _conv_fwd(xp, w8, b, *, n_ctx, name, cb=256):
    n, c = xp.shape
    half = SSD_CONV // 2

    def kern(x_ref, w_ref, b_ref, cpre_ref, act_ref):
        x = x_ref[...]
        acc = jnp.zeros_like(x) + b_ref[...]
        rolled = {}
        for k in range(SSD_CONV):
            rolled[k] = _rolled(x, k - half)
            acc = acc + rolled[k] * w_ref[k:k + 1, :]
        cpre_ref[...] = acc
        act_ref[...] = _silu(acc)
        for r0 in _conv_windows(n, n_ctx):
            rows = slice(r0, r0 + CONV_WIN)
            fix = acc[rows]
            for k in range(SSD_CONV):
                if k != half:
                    fix = fix - rolled[k][rows] * w_ref[k:k + 1, :] * _tap_outside(r0, k - half, n, n_ctx)
            cpre_ref[rows, :] = fix
            act_ref[rows, :] = _silu(fix)

    spec = pl.BlockSpec((n, cb), lambda j: (0, j))
    return pl.pallas_call(
        kern, name=name, grid=(c // cb,),
        in_specs=[spec, pl.BlockSpec((8, cb), lambda j: (0, j)), pl.BlockSpec((1, cb), lambda j: (0, j))],
        out_specs=[spec, spec], out_shape=[jax.ShapeDtypeStruct((n, c), F32)] * 2,
        compiler_params=_params("parallel"),
    )(xp, w8, b)


def _conv_bwd(d1, d2, cpre, xp, w8, *, n_ctx, name, cb=128):
    n, c = xp.shape
    half = SSD_CONV // 2

    def kern(d1_ref, d2_ref, cpre_ref, x_ref, w_ref, dx_ref, dw_ref, db_ref):
        g = (d1_ref[...] + d2_ref[...]) * _dsilu(cpre_ref[...])
        x = x_ref[...]
        dx = jnp.zeros_like(g)
        dw_ref[...] = jnp.zeros_like(dw_ref)
        g_rolled = {}
        for k in range(SSD_CONV):
            s = k - half
            g_rolled[k] = _rolled(g, -s)
            dx = dx + g_rolled[k] * w_ref[k:k + 1, :]
            xr = _rolled(x, s)
            dw = _sum0(g * xr)
            if s != 0:
                for r0 in _conv_windows(n, n_ctx):
                    rows = slice(r0, r0 + CONV_WIN)
                    dw = dw - _sum0(g[rows] * xr[rows] * _tap_outside(r0, s, n, n_ctx))
            dw_ref[k:k + 1, :] = dw
        dx_ref[...] = dx.astype(BF16)
        for r0 in _conv_windows(n, n_ctx):
            rows = slice(r0, r0 + CONV_WIN)
            fix = dx[rows]
            for k in range(SSD_CONV):
                if k != half:
                    fix = fix - g_rolled[k][rows] * w_ref[k:k + 1, :] * _tap_outside(r0, half - k, n, n_ctx)
            dx_ref[rows, :] = fix.astype(BF16)
        db_ref[...] = _sum0(g)

    spec = pl.BlockSpec((n, cb), lambda j: (0, j))
    return pl.pallas_call(
        kern, name=name, grid=(c // cb,),
        in_specs=[spec, spec, spec, spec, pl.BlockSpec((8, cb), lambda j: (0, j))],
        out_specs=[spec, pl.BlockSpec((8, cb), lambda j: (0, j)), pl.BlockSpec((1, cb), lambda j: (0, j))],
        out_shape=[jax.ShapeDtypeStruct((n, c), BF16), jax.ShapeDtypeStruct((8, c), F32),
                   jax.ShapeDtypeStruct((1, c), F32)],
        compiler_params=_params("parallel"),
    )(d1, d2, cpre, xp, w8)


def _chunk_of(s, nc, n_ctx_chunks, rev):
    if not rev:
        return s
    return jnp.where(s < n_ctx_chunks, n_ctx_chunks - 1 - s, nc - 1 - (s - n_ctx_chunks))


def _scan_common(dt_raw, dtT_raw, bias_r, bias_c, alog_r, alog_c, rev):
    ii = lax.broadcasted_iota(jnp.int32, (CHUNK, CHUNK), 0)
    jj = lax.broadcasted_iota(jnp.int32, (CHUNK, CHUNK), 1)
    tri = (jj >= ii) if rev else (jj <= ii)
    tri_t = (ii >= jj) if rev else (ii <= jj)
    a_r = -jnp.exp(alog_r)
    a_c = -jnp.exp(alog_c)
    dt = _softplus(dt_raw + bias_r)
    dt_t = _softplus(dtT_raw + bias_c)
    al = dt * a_r
    acum = _dot(tri.astype(F32), al, precision=HI)
    acum_t = _dot(dt_t * a_c, tri_t.astype(F32), precision=HI)
    atot = _sum0(al)
    return tri, tri_t, a_r, dt, acum, acum_t, atot


def _head_spread():
    return jnp.repeat(jnp.eye(SSD_HEADS, dtype=BF16), SSD_HEAD_DIM, axis=1)


def _dot_sel(v, sel):
    hi = v.astype(BF16)
    lo = (v - hi.astype(F32)).astype(BF16)
    return _dot(hi, sel) + _dot(lo, sel)


def _ssd_scan_fwd(xbc, dt_raw, dtT_raw, bias_r, bias_c, alog_r, alog_c, *, rev, n_ctx_chunks, name):
    n = xbc.shape[0]
    nc = n // CHUNK
    cidx = functools.partial(_chunk_of, nc=nc, n_ctx_chunks=n_ctx_chunks, rev=rev)

    def kern(xs_ref, b_ref, c_ref, dt_ref, dtT_ref, br_ref, bc_ref, ar_ref, ac_ref, e_ref, y_ref, hs_ref, h_scr):
        @pl.when(pl.program_id(0) == 0)
        def _():
            h_scr[...] = jnp.zeros_like(h_scr)

        tri, _, _, dt, acum, acum_t, atot = _scan_common(
            dt_ref[...], dtT_ref[...], br_ref[...], bc_ref[...], ar_ref[...], ac_ref[...], rev)
        etot = jnp.exp(atot)
        spread = lambda v: _dot_sel(v, e_ref[...])
        xdt_all = xs_ref[...] * spread(dt)
        eax = spread(jnp.exp(acum))
        xdw_all = xdt_all * spread(jnp.exp(atot - acum))
        hs_ref[...] = h_scr[...]
        for g in range(SSD_GROUPS):
            gs = slice(g * 256, (g + 1) * 256)
            bg = b_ref[:, g * SSD_STATE:(g + 1) * SSD_STATE].astype(BF16)
            cg = c_ref[:, g * SSD_STATE:(g + 1) * SSD_STATE].astype(BF16)
            cb = _dot(cg, bg, _NT)
            h4 = h_scr[gs, :]
            ys = []
            for k in range(SSD_HPG):
                h = g * SSD_HPG + k
                lmat = jnp.exp(jnp.where(tri, acum[:, h:h + 1] - acum_t[h:h + 1, :], NEG_BIG))
                xdt_h = xdt_all[:, h * SSD_HEAD_DIM:(h + 1) * SSD_HEAD_DIM].astype(BF16)
                ys.append(_dot((cb * lmat).astype(BF16), xdt_h))
            y_ref[:, gs] = jnp.concatenate(ys, axis=1) + _dot(cg, h4.astype(BF16), _NT) * eax[:, gs]
            s4 = _dot(xdw_all[:, gs].astype(BF16), bg, _TN)
            for k in range(SSD_HPG):
                h = g * SSD_HPG + k
                rs = slice(h * SSD_HEAD_DIM, (h + 1) * SSD_HEAD_DIM)
                h_scr[rs, :] = h4[k * SSD_HEAD_DIM:(k + 1) * SSD_HEAD_DIM] * etot[:, h:h + 1] + \
                    s4[k * SSD_HEAD_DIM:(k + 1) * SSD_HEAD_DIM]

    nh = SSD_HEADS
    small = lambda shape: pl.BlockSpec(shape, lambda s: (0, 0))
    return pl.pallas_call(
        kern, name=name, grid=(nc,),
        in_specs=[pl.BlockSpec((CHUNK, SSD_INNER), lambda s: (cidx(s), 0)),
                  pl.BlockSpec((CHUNK, 1024), lambda s: (cidx(s), 2)),
                  pl.BlockSpec((CHUNK, 1024), lambda s: (cidx(s), 3)),
                  pl.BlockSpec((CHUNK, nh), lambda s: (cidx(s), 0)),
                  pl.BlockSpec((nh, CHUNK), lambda s: (0, cidx(s))),
                  small((1, nh)), small((nh, 1)), small((1, nh)), small((nh, 1)), small((nh, SSD_INNER))],
        out_specs=[pl.BlockSpec((CHUNK, SSD_INNER), lambda s: (cidx(s), 0)),
                   pl.BlockSpec((None, SSD_INNER, SSD_STATE), lambda s: (s, 0, 0))],
        out_shape=[jax.ShapeDtypeStruct((n, SSD_INNER), F32),
                   jax.ShapeDtypeStruct((nc, SSD_INNER, SSD_STATE), F32)],
        scratch_shapes=[pltpu.VMEM((SSD_INNER, SSD_STATE), F32)],
        compiler_params=_params("arbitrary"),
    )(xbc, xbc, xbc, dt_raw, dtT_raw, bias_r, bias_c, alog_r, alog_c, _head_spread())


def _ssd_scan_bwd(dy, xbc, hs, dt_raw, dtT_raw, bias_r, bias_c, alog_r, alog_c, dvec, *, rev, n_ctx_chunks,
                  direct, name):
    n = xbc.shape[0]
    nc = n // CHUNK
    nh = SSD_HEADS
    step_of = lambda r: nc - 1 - r
    cidx = lambda r: _chunk_of(step_of(r), nc, n_ctx_chunks, rev)

    def kern(dy_ref, xs_ref, b_ref, c_ref, hs_ref, dt_ref, dtT_ref, br_ref, bc_ref, ar_ref, ac_ref, dv_ref,
             e_ref, et_ref, dx_ref, ddt_ref, dal_ref, dbias_ref, dh_scr):
        @pl.when(pl.program_id(0) == 0)
        def _():
            dh_scr[...] = jnp.zeros_like(dh_scr)
            dal_ref[...] = jnp.zeros_like(dal_ref)
            dbias_ref[...] = jnp.zeros_like(dbias_ref)

        tri, tri_t, a_r, dt, acum, acum_t, atot = _scan_common(
            dt_ref[...], dtT_ref[...], br_ref[...], bc_ref[...], ar_ref[...], ac_ref[...], rev)
        etot = jnp.exp(atot)
        spread = lambda v: _dot_sel(v, e_ref[...])
        gather = lambda v: _dot_sel(v, et_ref[...])
        xs_all = xs_ref[...]
        dy_all = dy_ref[...]
        dtx = spread(dt)
        eax = spread(jnp.exp(acum))
        decx = spread(jnp.exp(atot - acum))
        xdt_all = xs_all * dtx
        xdw_all = xdt_all * decx
        dyo_all = dy_all * eax
        lane = lax.broadcasted_iota(jnp.int32, (CHUNK, nh), 1)
        lane1 = lax.broadcasted_iota(jnp.int32, (1, nh), 1)
        sub = lax.broadcasted_iota(jnp.int32, (nh, CHUNK), 0)
        g_rows = jnp.zeros((CHUNK, nh), F32)
        g_cols = jnp.zeros((nh, CHUNK), F32)
        dtot = jnp.zeros((1, nh), F32)
        q_col, q_e, q_dt = [], [], []
        for g in range(SSD_GROUPS):
            gs = slice(g * 256, (g + 1) * 256)
            bg = b_ref[:, g * SSD_STATE:(g + 1) * SSD_STATE].astype(BF16)
            cg = c_ref[:, g * SSD_STATE:(g + 1) * SSD_STATE].astype(BF16)
            cb = _dot(cg, bg, _NT)
            hs4 = hs_ref[gs, :]
            dh4 = dh_scr[gs, :]
            hs4_bf = hs4.astype(BF16)
            dh4_bf = dh4.astype(BF16)
            dy4 = dy_all[:, gs]
            dy4_bf = dy4.astype(BF16)
            xdt4_bf = xdt_all[:, gs].astype(BF16)
            xdw4 = xdw_all[:, gs]
            xdw4_bf = xdw4.astype(BF16)
            dyo4_bf = dyo_all[:, gs].astype(BF16)
            yoff4 = _dot(cg, hs4_bf, _NT) * eax[:, gs]
            dcg = _dot(dyo4_bf, hs4_bf)
            dh_new4 = _dot(dyo4_bf, cg, _TN)
            bdh4 = _dot(bg, dh4_bf, _NT)
            dbg = _dot(xdw4_bf, dh4_bf)
            e4 = xdw4 * bdh4
            q_col.append(dy4 * yoff4 - e4)
            q_e.append(e4)
            hsum = jnp.sum(dh4 * hs4, axis=1, keepdims=True)
            dcb = jnp.zeros((CHUNK, CHUNK), F32)
            dxdts = []
            for k in range(SSD_HPG):
                h = g * SSD_HPG + k
                ks = slice(k * SSD_HEAD_DIM, (k + 1) * SSD_HEAD_DIM)
                lmat = jnp.exp(jnp.where(tri, acum[:, h:h + 1] - acum_t[h:h + 1, :], NEG_BIG))
                mf = cb * lmat
                dm = _dot(dy4_bf[:, ks], xdt4_bf[:, ks], _NT)
                dcb = dcb + dm * lmat
                gmat = dm * mf
                g_rows = g_rows + jnp.where(lane == h, jnp.sum(gmat, axis=1, keepdims=True), 0.0)
                g_cols = g_cols + jnp.where(sub == h, _sum0(gmat), 0.0)
                dxdts.append(_dot(mf.astype(BF16), dy4_bf[:, ks], _TN))
                et = etot[:, h:h + 1]
                dtot = dtot + jnp.where(lane1 == h, _sum0(hsum[ks]) * et, 0.0)
                dh_scr[h * SSD_HEAD_DIM:(h + 1) * SSD_HEAD_DIM, :] = dh4[ks] * et + dh_new4[ks]
            dxdt4 = jnp.concatenate(dxdts, axis=1) + bdh4 * decx[:, gs]
            q_dt.append(dxdt4 * xs_all[:, gs])
            dx4 = dxdt4 * dtx[:, gs]
            if direct:
                dx4 = dx4 + dy4 * dv_ref[:, gs]
            dcb_bf = dcb.astype(BF16)
            dx_ref[:, gs] = dx4
            dx_ref[:, SSD_INNER + g * SSD_STATE:SSD_INNER + (g + 1) * SSD_STATE] = dbg + _dot(dcb_bf, cg, _TN)
            dx_ref[:, SSD_INNER + 1024 + g * SSD_STATE:SSD_INNER + 1024 + (g + 1) * SSD_STATE] = \
                dcg + _dot(dcb_bf, bg)
        e_heads = gather(jnp.concatenate(q_e, axis=1))
        dacum = gather(jnp.concatenate(q_col, axis=1)) + g_rows - g_cols.T
        dal = _dot(tri_t.astype(F32), dacum, precision=HI) + dtot + _sum0(e_heads)
        ddt = gather(jnp.concatenate(q_dt, axis=1)) + dal * a_r
        ddt_raw = ddt * _sig(dt_ref[...] + br_ref[...])
        ddt_ref[...] = ddt_raw
        dal_ref[...] += _sum0(dal * dt) * a_r
        dbias_ref[...] += _sum0(ddt_raw)

    small = lambda shape: pl.BlockSpec(shape, lambda r: (0, 0))
    return pl.pallas_call(
        kern, name=name, grid=(nc,),
        in_specs=[pl.BlockSpec((CHUNK, SSD_INNER), lambda r: (cidx(r), 0)),
                  pl.BlockSpec((CHUNK, SSD_INNER), lambda r: (cidx(r), 0)),
                  pl.BlockSpec((CHUNK, 1024), lambda r: (cidx(r), 2)),
                  pl.BlockSpec((CHUNK, 1024), lambda r: (cidx(r), 3)),
                  pl.BlockSpec((None, SSD_INNER, SSD_STATE), lambda r: (step_of(r), 0, 0)),
                  pl.BlockSpec((CHUNK, nh), lambda r: (cidx(r), 0)),
                  pl.BlockSpec((nh, CHUNK), lambda r: (0, cidx(r))),
                  small((1, nh)), small((nh, 1)), small((1, nh)), small((nh, 1)), small((1, SSD_INNER)),
                  small((nh, SSD_INNER)), small((SSD_INNER, nh))],
        out_specs=[pl.BlockSpec((CHUNK, SSD_CONV_DIM), lambda r: (cidx(r), 0)),
                   pl.BlockSpec((CHUNK, nh), lambda r: (cidx(r), 0)),
                   small((1, nh)), small((1, nh))],
        out_shape=[jax.ShapeDtypeStruct((n, SSD_CONV_DIM), F32), jax.ShapeDtypeStruct((n, nh), F32),
                   jax.ShapeDtypeStruct((1, nh), F32), jax.ShapeDtypeStruct((1, nh), F32)],
        scratch_shapes=[pltpu.VMEM((SSD_INNER, SSD_STATE), F32)],
        compiler_params=_params("arbitrary"),
    )(dy, xbc, xbc, xbc, hs, dt_raw, dtT_raw, bias_r, bias_c, alog_r, alog_c, dvec, _head_spread(),
      _head_spread().T)


def _gm_spatial_fwd(gu, gvn, ws, bst, *, name):
    n = gu.shape[0]

    def kern(gu_ref, gv_ref, ws_ref, bs_ref, o_ref):
        for g in range(GM_GROUPS):
            sl = slice(g * GM_GROUP_DIM, (g + 1) * GM_GROUP_DIM)
            s = _dot(ws_ref[g], gv_ref[:, sl]) + bs_ref[:, g:g + 1]
            o_ref[:, sl] = (gu_ref[:, sl] * s).astype(BF16)

    spec = pl.BlockSpec((CHUNK, GM_INNER), lambda i: (i, 0))
    return pl.pallas_call(
        kern, name=name, grid=(n // CHUNK,),
        in_specs=[spec, spec, pl.BlockSpec(ws.shape, lambda i: (0, 0, 0)), pl.BlockSpec(bst.shape, lambda i: (0, 0))],
        out_specs=spec, out_shape=jax.ShapeDtypeStruct((n, GM_INNER), BF16),
        compiler_params=_params("parallel"),
    )(gu, gvn, ws, bst)


def _gm_spatial_bwd(dt, gu, gvn, ws, wst, bst, *, name):
    n = gu.shape[0]

    def kern(dt_ref, gu_ref, gv_ref, ws_ref, wst_ref, bs_ref, dgu_ref, dgv_ref, dws_ref, dbs_ref):
        @pl.when(pl.program_id(0) == 0)
        def _():
            dws_ref[...] = jnp.zeros_like(dws_ref)
            dbs_ref[...] = jnp.zeros_like(dbs_ref)

        lane = lax.broadcasted_iota(jnp.int32, (CHUNK, GM_GROUPS), 1)
        dbs = jnp.zeros((CHUNK, GM_GROUPS), F32)
        for g in range(GM_GROUPS):
            sl = slice(g * GM_GROUP_DIM, (g + 1) * GM_GROUP_DIM)
            gv = gv_ref[:, sl]
            s = _dot(ws_ref[g], gv) + bs_ref[:, g:g + 1]
            d = dt_ref[:, sl]
            dgu_ref[:, sl] = d * s
            ds = d * gu_ref[:, sl]
            ds_bf = ds.astype(BF16)
            dws_ref[g] += _dot(ds_bf, gv, _NT)
            dgv_ref[:, sl] = _dot(wst_ref[g], ds_bf)
            dbs = dbs + jnp.where(lane == g, jnp.sum(ds, axis=1, keepdims=True), 0.0)
        dbs_ref[...] += dbs

    spec = pl.BlockSpec((CHUNK, GM_INNER), lambda i: (i, 0))
    wspec = pl.BlockSpec(ws.shape, lambda i: (0, 0, 0))
    bspec = pl.BlockSpec(bst.shape, lambda i: (0, 0))
    return pl.pallas_call(
        kern, name=name, grid=(n // CHUNK,),
        in_specs=[spec, spec, spec, wspec, wspec, bspec],
        out_specs=[spec, spec, wspec, bspec],
        out_shape=[jax.ShapeDtypeStruct((n, GM_INNER), F32), jax.ShapeDtypeStruct((n, GM_INNER), F32),
                   jax.ShapeDtypeStruct(ws.shape, F32), jax.ShapeDtypeStruct(bst.shape, F32)],
        compiler_params=_params("arbitrary"),
    )(dt, gu, gvn, ws, wst, bst)


def _adamw(parts, w, m, v, *, name, tm=256, sel=(), into=None):
    ns, r, wd = parts.shape
    tm = _pick(r, tm, 8)
    tc = wd
    if tm < 64 and wd % 256 == 0:
        tm, tc = r, 256
    lead = len(sel)
    assert w.shape[lead:] == (r, wd) and lead == w.ndim - 2

    def kern(*refs):
        p_ref, w_ref, m_ref, v_ref = refs[:4]
        g_ref, d_ref, nm_ref, nv_ref = refs[-4:]
        g = p_ref[0].astype(F32)
        for s in range(1, ns):
            g = g + p_ref[s].astype(F32)
        m2 = ADAM_B1 * m_ref[...] + (1.0 - ADAM_B1) * g
        v2 = ADAM_B2 * v_ref[...] + (1.0 - ADAM_B2) * (g * g)
        m_hat = m2 / (1.0 - ADAM_B1 ** ADAM_STEP)
        v_hat = v2 / (1.0 - ADAM_B2 ** ADAM_STEP)
        g_ref[...] = g
        d_ref[...] = -ADAM_LR * (m_hat / (jnp.sqrt(v_hat) + ADAM_EPS) + ADAM_WD * w_ref[...])
        nm_ref[...] = m2
        nv_ref[...] = v2

    spec = pl.BlockSpec((None,) * lead + (tm, tc), lambda i, j: tuple(sel) + (i, j))
    extra, aliases = [], {}
    if into is not None:
        extra = list(into)
        aliases = {4 + k: k for k in range(4)}
    return pl.pallas_call(
        kern, name=name, grid=(r // tm, wd // tc),
        in_specs=[pl.BlockSpec((ns, tm, tc), lambda i, j: (0, i, j)), spec, spec, spec] +
                 [pl.BlockSpec(memory_space=pl.ANY)] * len(extra),
        out_specs=[spec] * 4, out_shape=[jax.ShapeDtypeStruct(w.shape, F32)] * 4,
        input_output_aliases=aliases,
        compiler_params=_params("parallel", "parallel"),
    )(parts, w, m, v, *extra)


def _sum_slots(parts, *, name, scale_by=None):
    ns, r, wd = parts.shape

    def kern(*refs):
        p_ref, o_ref = refs[0], refs[-1]
        g = p_ref[0]
        for s in range(1, ns):
            g = g + p_ref[s]
        if scale_by is not None:
            g = g * _dsilu(refs[1][...])
        o_ref[...] = g

    args = [parts] + ([] if scale_by is None else [scale_by])
    return pl.pallas_call(kern, name=name, out_shape=jax.ShapeDtypeStruct((r, wd), F32),
                          compiler_params=pltpu.CompilerParams(vmem_limit_bytes=VMEM_LIMIT_BYTES))(*args)


def _mesh_pos():
    x, y, c = lax.axis_index("x"), lax.axis_index("y"), lax.axis_index("c")
    return x, y, c, 4 * x + 2 * y + c


def _flip(x, y, c, f):
    fx, fy, fc = (f >> 2) & 1, (f >> 1) & 1, f & 1
    px = 1 - x if fx else x
    py = 1 - y if fy else y
    pc = 1 - c if fc else c
    return (px, py, pc), 4 * px + 2 * py + pc


_HBM_SPEC = pl.BlockSpec(memory_space=pltpu.HBM)


def _exchange(arrays, *, scatter, name):
    na = len(arrays)
    if scatter:
        out_shape = [jax.ShapeDtypeStruct(a.shape, a.dtype) for a in arrays]
    else:
        out_shape = [jax.ShapeDtypeStruct((NDEV,) + a.shape, a.dtype) for a in arrays]

    out_shape.append(jax.ShapeDtypeStruct((8, 128), F32))

    def body(*refs):
        ins, outs = refs[:na], refs[na:2 * na]
        send_sems, recv_sems, local_sems = refs[2 * na + 1:]
        refs[2 * na][...] = jnp.zeros((8, 128), F32)
        x, y, c, me = _mesh_pos()
        copies = []
        for i in range(na):
            src_own = ins[i].at[me] if scatter else ins[i]
            lc = pltpu.make_async_copy(src_own, outs[i].at[me], local_sems.at[i])
            lc.start()
            copies.append(lc)
        sends = []
        for f in range(1, NDEV):
            peer, pidx = _flip(x, y, c, f)
            for i in range(na):
                k = i * (NDEV - 1) + f - 1
                src = ins[i].at[pidx] if scatter else ins[i]
                cp = pltpu.make_async_remote_copy(
                    src_ref=src, dst_ref=outs[i].at[me], send_sem=send_sems.at[k], recv_sem=recv_sems.at[k],
                    device_id=peer, device_id_type=pl.DeviceIdType.MESH)
                cp.start()
                sends.append(cp)
        for f in range(1, NDEV):
            peer, pidx = _flip(x, y, c, f)
            for i in range(na):
                k = i * (NDEV - 1) + f - 1
                src = ins[i].at[pidx] if scatter else ins[i]
                pltpu.make_async_remote_copy(
                    src_ref=src, dst_ref=outs[i].at[pidx], send_sem=send_sems.at[k], recv_sem=recv_sems.at[k],
                    device_id=peer, device_id_type=pl.DeviceIdType.MESH).wait_recv()
        for cp in sends:
            cp.wait_send()
        for lc in copies:
            lc.wait()

    res = pl.pallas_call(
        body, name=name, out_shape=out_shape, in_specs=[_HBM_SPEC] * na,
        out_specs=[_HBM_SPEC] * na + [pl.BlockSpec(memory_space=pltpu.VMEM)],
        scratch_shapes=[pltpu.SemaphoreType.DMA((na * (NDEV - 1),)), pltpu.SemaphoreType.DMA((na * (NDEV - 1),)),
                        pltpu.SemaphoreType.DMA((na,))],
        compiler_params=pltpu.CompilerParams(has_side_effects=True),
    )(*arrays)
    return res[:na], res[na][0, 0]


_SEM_SPEC = pl.BlockSpec(memory_space=pltpu.SEMAPHORE)
_DATAFLOW = pltpu.SideEffectType.DATAFLOW_SIDE_EFFECTING


def _split_copies(srcs, lands, send_sems, recv_sems, scatter, arriving):
    x, y, c, me = _mesh_pos()
    copies = []
    for i in range(len(srcs)):
        for f in range(1, NDEV):
            peer, pidx = _flip(x, y, c, f)
            k = i * (NDEV - 1) + f - 1
            copies.append(pltpu.make_async_remote_copy(
                src_ref=srcs[i].at[pidx] if scatter else srcs[i], dst_ref=lands[i].at[pidx if arriving else me],
                send_sem=send_sems.at[k], recv_sem=recv_sems.at[k], device_id=peer,
                device_id_type=pl.DeviceIdType.MESH))
    return copies


def _exchange_start(srcs, lands, *, scatter, name):
    na = len(srcs)
    nsem = na * (NDEV - 1)

    def body(*refs):
        ins_src, ins_land = refs[:na], refs[na:2 * na]
        send_sems, recv_sems = refs[2 * na], refs[2 * na + 1]
        token = refs[-1]
        for cp in _split_copies(ins_src, ins_land, send_sems, recv_sems, scatter, False):
            cp.start()
        token[...] = jnp.zeros_like(token)

    thru = [pltpu.HBM(a.shape, a.dtype) for a in list(srcs) + list(lands)]
    res = pl.pallas_call(
        body, name=name,
        out_shape=(pltpu.SemaphoreType.DMA((nsem,)), pltpu.SemaphoreType.DMA((nsem,)), *thru,
                   jax.ShapeDtypeStruct((8, 128), F32)),
        in_specs=[_HBM_SPEC] * (2 * na),
        out_specs=(_SEM_SPEC, _SEM_SPEC, *([_HBM_SPEC] * (2 * na)), pl.BlockSpec(memory_space=pltpu.VMEM)),
        input_output_aliases={i: 2 + i for i in range(2 * na)},
        compiler_params=pltpu.CompilerParams(has_side_effects=_DATAFLOW),
    )(*[pltpu.with_memory_space_constraint(a, pltpu.HBM) for a in list(srcs) + list(lands)])
    send_sems, recv_sems = res[0], res[1]
    return send_sems, recv_sems, res[2:2 + na], res[2 + na:2 + 2 * na], res[-1][0, 0]


def _exchange_wait(send_sems, recv_sems, srcs, lands, after, *, scatter, name):
    na = len(srcs)

    def body(*refs):
        ins_src, ins_land = refs[:na], refs[na:2 * na]
        s_sems, r_sems = refs[2 * na], refs[2 * na + 1]
        for cp in _split_copies(ins_src, ins_land, s_sems, r_sems, scatter, False):
            cp.wait_send()
        for cp in _split_copies(ins_src, ins_land, s_sems, r_sems, scatter, True):
            cp.wait_recv()

    thru = [pltpu.HBM(a.shape, a.dtype) for a in list(srcs) + list(lands)]
    res = pl.pallas_call(
        body, name=name, out_shape=tuple(thru),
        in_specs=[_HBM_SPEC] * (2 * na) + [_SEM_SPEC, _SEM_SPEC, pl.BlockSpec(memory_space=pl.ANY)],
        out_specs=tuple([_HBM_SPEC] * (2 * na)),
        input_output_aliases={i: i for i in range(2 * na)},
        compiler_params=pltpu.CompilerParams(has_side_effects=_DATAFLOW),
    )(*srcs, *lands, send_sems, recv_sems, after)
    return res[na:]


def _landing(block, me):
    buf = lax.empty((NDEV,) + block.shape, block.dtype)
    return lax.dynamic_update_slice_in_dim(buf, block[None], me, axis=0)


def _seg_kw(nseg, n_ctx, tm):
    return dict(nseg=nseg, seg_blocks=(n_ctx // tm if nseg == 2 else 0))


def _ffn_fwd(tag, h, gpre, gpost, shift, scale, gate, w, *, nseg, n_ctx, tm):
    n = h.shape[0]
    kw = _seg_kw(nseg, n_ctx, tm)
    (u,) = _rowwise(tag + "_pre", _pre_fwd_fn, n, [h], [("full", gpre), ("seg", shift), ("seg", scale)],
                    [(D_MODEL, BF16)], tm=tm, **kw)
    if "early" in w:
        w.update(w.pop("early")(u))
    s, a, b = _mm_glu(u, w["win_t"], name=tag + "_glu")
    if "late" in w:
        w.update(w.pop("late")(s))
    y, ho = _mm_rows(s, w["wout"], functools.partial(_out_post_fn, 0.5), [h], [("full", gpost), ("seg", gate)],
                     [(D_MODEL, F32), (D_MODEL, F32)], name=tag + "_out", tk=FFN_DIM, n_ctx=n_ctx)
    return ho, dict(h=h, u=u, s=s, a=a, b=b, y=y)


def _ffn_bwd(tag, dho, sv, gpre, gpost, scale, gate, w, put, *, nseg, n_ctx, tm):
    n = dho.shape[0]
    kw = _seg_kw(nseg, n_ctx, tm)
    dy, dgate, dgpost = _rowwise(tag + "_postb", functools.partial(_post_bwd_fn, 0.5), n, [dho, sv["y"]],
                                 [("full", gpost), ("seg", gate)], [(D_MODEL, BF16)], [D_MODEL, D_MODEL], tm=tm, **kw)
    tok = put("w_out", _mm_tn(sv["s"], dy, name=tag + "_dwout", tm=1408, tn=1024, col_blocks=1))
    dp = _mm_glu_bwd(dy, w["wout"], sv["a"], sv["b"], name=tag + "_ds")
    tok2 = put("w_in", _mm_tn(dp, sv["u"], name=tag + "_dwin", tm=1408, tn=1024, col_blocks=1))
    for t in (tok, tok2):
        if t is not None:
            gpre = gpre + t
    dh, dshift, dscale, dgpre = _mm_rows(dp, w["win_t"], _pre_bwd_fn, [sv["h"], dho],
                                         [("full", gpre), ("seg", scale)], [(D_MODEL, F32)],
                                         [D_MODEL, D_MODEL, D_MODEL], name=tag + "_du", tk=FFN_DIM, n_ctx=n_ctx)
    return dh, None, dict(shift=dshift, scale=dscale, gate=dgate, gpre=dgpre, gpost=dgpost)


def _local_step(x, ctx, target, mods, norm_g, get_w, small, put_grad):
    t_len, n_ctx = x.shape[0], ctx.shape[0]
    n0 = t_len + n_ctx
    tm0 = _pick(n_ctx, 256, 8)
    tm1 = _pick(t_len, 256, 8)
    ncc = n_ctx // CHUNK
    g = {}

    def modrow(i, k, nseg):
        mc, mx = mods[i]
        if nseg == 2:
            return jnp.stack([mc[k], mx[k]])[:, None, :]
        return mx[k][None, None, :]

    pending = [None]

    def gvec(i, k):
        v = norm_g[i, k][None, :]
        if pending[0] is not None:
            v = v + pending[0]
            pending[0] = None
        return v

    xc = jnp.concatenate([ctx, x], axis=0)
    L0 = dict(nseg=2, n_ctx=n_ctx, tm=tm0)
    wts = dict(get_w("ffn00", xc))
    h1, sv_f01 = _ffn_fwd("l0f1", xc, gvec(0, 0), gvec(0, 1), modrow(0, 0, 2), modrow(0, 1, 2), modrow(0, 2, 2),
                          wts["ffn00"], **L0)
    kw0 = _seg_kw(2, n_ctx, tm0)
    (um0,) = _rowwise("l0m_pre", _pre_fwd_fn, n0, [h1], [("full", gvec(0, 2)), ("seg", modrow(0, 3, 2)),
                                                         ("seg", modrow(0, 4, 2))], [(D_MODEL, BF16)], tm=tm0, **kw0)
    wts.update(get_w("ssd", um0))
    win_ssd = wts["ssd_win_t"]
    nh = SSD_HEADS
    dt_blk = (SSD_INNER + SSD_CONV_DIM) // (2 * nh)
    z = _mm(um0, win_ssd, out_dtype=F32, name="ssd_z", rhs_t=True, n=SSD_INNER)
    xbc_pre = _mm(um0, win_ssd, out_dtype=F32, name="ssd_xbc", rhs_t=True, n=SSD_CONV_DIM,
                  b_off=(SSD_INNER // 1024, 0))
    dtr = _mm(um0, win_ssd, out_dtype=F32, name="ssd_dt", rhs_t=True, n=2 * nh, b_off=(dt_blk, 0))
    cpre, xbc = _conv_fwd(xbc_pre, small["conv_w8"], small["conv_b"], n_ctx=n_ctx, name="ssd_conv")
    nh = SSD_HEADS
    dt_dir = [dtr[:, :nh], dtr[:, nh:2 * nh]]
    dtT_dir = [d.T for d in dt_dir]
    bias_r = [small["dt_bias"][d][None, :] for d in range(2)]
    bias_c = [small["dt_bias"][d][:, None] for d in range(2)]
    alog_r = [small["a_log"][d][None, :] for d in range(2)]
    alog_c = [small["a_log"][d][:, None] for d in range(2)]
    ys, hss = [], []
    for d in range(2):
        yd, hsd = _ssd_scan_fwd(xbc, dt_dir[d], dtT_dir[d], bias_r[d], bias_c[d], alog_r[d], alog_c[d],
                                rev=(d == 1), n_ctx_chunks=ncc, name=f"ssd_scan{d}")
        ys.append(yd)
        hss.append(hsd)
    dvec = jnp.repeat(small["ssd_d"], SSD_HEAD_DIM)[None, :]
    ngv = small["ssd_norm_g"][None, :]
    gate_rows = [ys[0], ys[1], (xbc, SSD_INNER, 0, 0), z]
    off = n_ctx // tm1
    lat = lambda r: (r[0], r[1], r[2], off) if isinstance(r, tuple) else (r, r.shape[1], 0, off)
    (yn,) = _rowwise("ssd_gate", _ssdgate_fwd_fn, t_len, [lat(r) for r in gate_rows],
                     [("full", dvec), ("full", ngv)], [(SSD_INNER, BF16)], tm=tm1)
    h1x = h1[n_ctx:]
    L1 = dict(nseg=1, n_ctx=0, tm=tm1)
    if "late" in wts:
        wts.update(wts.pop("late")(yn))
    yo0, h2 = _mm_rows(yn, wts["ssd_wout"], functools.partial(_out_post_fn, 1.0), [h1x],
                       [("full", gvec(0, 3)), ("seg", modrow(0, 5, 1))], [(D_MODEL, F32), (D_MODEL, F32)],
                       name="ssd_out", tk=SSD_INNER)
    wts.update(get_w("ffn01", h2))
    h3, sv_f02 = _ffn_fwd("l0f2", h2, gvec(0, 4), gvec(0, 5), modrow(0, 6, 1), modrow(0, 7, 1), modrow(0, 8, 1),
                          wts["ffn01"], **L1)

    wts.update(get_w("ffn10", h3))
    h4, sv_f11 = _ffn_fwd("l1f1", h3, gvec(1, 0), gvec(1, 1), modrow(1, 0, 1), modrow(1, 1, 1), modrow(1, 2, 1),
                          wts["ffn10"], **L1)
    (um1,) = _rowwise("l1m_pre", _pre_fwd_fn, t_len, [h4], [("full", gvec(1, 2)), ("seg", modrow(1, 3, 1)),
                                                            ("seg", modrow(1, 4, 1))], [(D_MODEL, BF16)], tm=tm1)
    wts.update(get_w("gm", um1))
    p1 = _mm(um1, wts["gm_win"], out_dtype=F32, name="gm_in", tm=2048)
    vg = small["gm_v_g"][None, :]
    vb = small["gm_v_b"][None, :]
    gu, gvn = _rowwise("gm_act", _gm_act_fwd_fn, t_len, [p1], [("full", vg), ("full", vb)],
                       [(GM_INNER, F32), (GM_INNER, BF16)], tm=256)
    ws_bf = small["gm_w_s"].astype(BF16)
    wst_bf = jnp.swapaxes(small["gm_w_s"], 1, 2).astype(BF16)
    bst = small["gm_b_s"].T
    tgm = _gm_spatial_fwd(gu, gvn, ws_bf, bst, name="gm_spatial")
    yo1, h5 = _mm_rows(tgm, wts["gm_wout"], functools.partial(_out_post_fn, 1.0), [h4],
                       [("full", gvec(1, 3)), ("seg", modrow(1, 5, 1))], [(D_MODEL, F32), (D_MODEL, F32)],
                       name="gm_out", tk=GM_INNER)
    wts.update(get_w("ffn11", h5))
    h6, sv_f12 = _ffn_fwd("l1f2", h5, gvec(1, 4), gvec(1, 5), modrow(1, 6, 1), modrow(1, 7, 1), modrow(1, 8, 1),
                          wts["ffn11"], **L1)

    dh, loss_parts = _rowwise("loss", _loss_fn, t_len, [h6, target], [], [(D_MODEL, F32)], [D_MODEL], tm=tm1)

    zero = jnp.zeros((D_MODEL,), F32)
    dmx = [[zero] * N_MOD for _ in range(2)]
    dmc = [[zero] * N_MOD for _ in range(2)]
    dng = [[zero] * 6 for _ in range(2)]

    def put_mod(i, k, acc):
        if acc.shape[0] == 2:
            dmc[i][k] = dmc[i][k] + acc[0, 0]
            dmx[i][k] = dmx[i][k] + acc[1, 0]
        else:
            dmx[i][k] = dmx[i][k] + acc[0, 0]

    def put_g(i, k, acc):
        dng[i][k] = dng[i][k] + jnp.sum(acc[:, 0], axis=0)

    def ffn_back(tag, i, j, dho, sv, w, lay):
        nseg = lay["nseg"]
        base = 0 if j == 0 else 6
        gi = 0 if j == 0 else 4
        dh_in, pending[0], s = _ffn_bwd(tag, dho, sv, gvec(i, gi), gvec(i, gi + 1), modrow(i, base + 1, nseg),
                                        modrow(i, base + 2, nseg), w, functools.partial(put_grad, f"ffn{i}{j}"), **lay)
        put_mod(i, base, s["shift"])
        put_mod(i, base + 1, s["scale"])
        put_mod(i, base + 2, s["gate"])
        put_g(i, gi, s["gpre"])
        put_g(i, gi + 1, s["gpost"])
        return dh_in

    dh = ffn_back("l1f2", 1, 1, dh, sv_f12, wts["ffn11"], L1)
    dyo, dgate, dgp = _rowwise("l1m_postb", functools.partial(_post_bwd_fn, 1.0), t_len, [dh, yo1],
                               [("full", gvec(1, 3)), ("seg", modrow(1, 5, 1))], [(D_MODEL, BF16)],
                               [D_MODEL, D_MODEL], tm=tm1)
    put_mod(1, 5, dgate)
    put_g(1, 3, dgp)
    put_grad("gm", "w_out", _mm_tn(tgm, dyo, name="gm_dwout", tn=1024, col_blocks=1))
    dtg = _mm(dyo, wts["gm_wout"], out_dtype=F32, name="gm_dt", rhs_t=True)
    dgu, dgvn, dws, dbst = _gm_spatial_bwd(dtg, gu, gvn, ws_bf, wst_bf, bst, name="gm_spatialb")
    g["gm_w_s"] = dws
    g["gm_b_s"] = dbst.T
    dp1, dvg, dvb = _rowwise("gm_actb", _gm_act_bwd_fn, t_len, [p1, dgu, dgvn], [("full", vg)],
                             [(2 * GM_INNER, BF16)], [GM_INNER, GM_INNER], tm=128)
    g["gm_v_g"] = dvg[0, 0]
    g["gm_v_b"] = dvb[0, 0]
    pending[0] = put_grad("gm", "w_in", _mm_tn(um1, dp1, name="gm_dwin", tm=1024, col_blocks=NDEV))
    dh, dsh, dsc, dgp = _mm_rows(dp1, wts["gm_win"], _pre_bwd_fn, [h4, dh],
                                 [("full", gvec(1, 2)), ("seg", modrow(1, 4, 1))], [(D_MODEL, F32)],
                                 [D_MODEL, D_MODEL, D_MODEL], name="gm_dum", tk=2048, rhs_t=True)
    put_mod(1, 3, dsh)
    put_mod(1, 4, dsc)
    put_g(1, 2, dgp)
    dh = ffn_back("l1f1", 1, 0, dh, sv_f11, wts["ffn10"], L1)

    dh = ffn_back("l0f2", 0, 1, dh, sv_f02, wts["ffn01"], L1)
    dyo, dgate, dgp = _rowwise("l0m_postb", functools.partial(_post_bwd_fn, 1.0), t_len, [dh, yo0],
                               [("full", gvec(0, 3)), ("seg", modrow(0, 5, 1))], [(D_MODEL, BF16)],
                               [D_MODEL, D_MODEL], tm=tm1)
    put_mod(0, 5, dgate)
    put_g(0, 3, dgp)
    tok = put_grad("ssd", "w_out", _mm_tn(yn, dyo, name="ssd_dwout", tn=1024, col_blocks=1))
    dyn = _mm(dyo, wts["ssd_wout"], out_dtype=F32, name="ssd_dyn", rhs_t=True)
    dy_ssd, dz, dngv, ddv = _rowwise("ssd_gateb", _ssdgate_bwd_fn, n0, [(dyn, SSD_INNER, 0, -ncc)] + gate_rows,
                                     [("full", dvec), ("full", ngv if tok is None else ngv + tok)],
                                     [(SSD_INNER, F32), (SSD_INNER, BF16)],
                                     [SSD_INNER, SSD_INNER], tm=128)
    g["ssd_norm_g"] = dngv[0, 0]
    g["ssd_D"] = jnp.sum(ddv[0, 0].reshape(SSD_HEADS, SSD_HEAD_DIM), axis=1)
    dxbcs, ddts, dalogs, dbiases = [], [], [], []
    for d in range(2):
        dxd, ddtd, dal, dbi = _ssd_scan_bwd(dy_ssd, xbc, hss[d], dt_dir[d], dtT_dir[d], bias_r[d], bias_c[d],
                                            alog_r[d], alog_c[d], dvec, rev=(d == 1), n_ctx_chunks=ncc,
                                            direct=(d == 0), name=f"ssd_scanb{d}")
        dxbcs.append(dxd)
        ddts.append(ddtd)
        dalogs.append(dal[0])
        dbiases.append(dbi[0])
    g["ssd_A_log"] = jnp.stack(dalogs)
    g["ssd_dt_bias"] = jnp.stack(dbiases)
    dxbc_pre, dcw8, dcb = _conv_bwd(dxbcs[0], dxbcs[1], cpre, xbc_pre, small["conv_w8"], n_ctx=n_ctx, name="ssd_convb")
    g["ssd_conv_w"] = dcw8[:SSD_CONV]
    g["ssd_conv_b"] = dcb[0]
    ddt_bf = jnp.concatenate([ddts[0], ddts[1]], axis=1).astype(BF16)
    n_in = SSD_INNER + SSD_CONV_DIM + 2 * nh
    dw_t = _mm_tn(dz, um0, name="ssd_dwz", col_blocks=1, stack=(n_in, 0, None))
    dw_t = _mm_tn(dxbc_pre, um0, name="ssd_dwxbc", col_blocks=1, stack=(n_in, SSD_INNER, dw_t))
    dw_t = _mm_tn(ddt_bf, um0, name="ssd_dwdt", col_blocks=1, stack=(n_in, SSD_INNER + SSD_CONV_DIM, dw_t))
    pending[0] = put_grad("ssd", "w_in", dw_t)
    dum0 = _mm(dz, win_ssd, out_dtype=F32, name="ssd_dum_z", tk=SSD_INNER, n=D_MODEL)
    dum0 = _mm(dxbc_pre, win_ssd, out_dtype=F32, name="ssd_dum_x", tk=SSD_INNER, n=D_MODEL,
               b_off=(SSD_INNER // SSD_INNER, 0), add=dum0)
    dum0 = _mm(ddt_bf, win_ssd, out_dtype=F32, name="ssd_dum_dt", tk=2 * nh, n=D_MODEL, b_off=(dt_blk, 0), add=dum0)
    dh0, dsh, dsc, dgp = _rowwise("l0m_preb", _pre_bwd_fn, n0, [dum0, h1, (dh, D_MODEL, 0, -(n_ctx // tm0))],
                                  [("full", gvec(0, 2)), ("seg", modrow(0, 4, 2))], [(D_MODEL, F32)],
                                  [D_MODEL, D_MODEL, D_MODEL], tm=tm0, **kw0)
    put_mod(0, 3, dsh)
    put_mod(0, 4, dsc)
    put_g(0, 2, dgp)
    dh0 = ffn_back("l0f1", 0, 0, dh0, sv_f01, wts["ffn00"], L0)
    grad_x = dh0[n_ctx:]
    g["norm_g"] = jnp.stack([jnp.stack(r) for r in dng])
    g["dmx"] = jnp.stack([jnp.concatenate(r) for r in dmx])
    g["dmc"] = jnp.stack([jnp.concatenate(r) for r in dmc])
    return loss_parts[0], grad_x, g


GROUPS = ("ffn00", "ssd", "ffn01", "ffn10", "gm", "ffn11")


TRANSPOSED_IN = ("ffn", "ssd")


def _is_transposed(group):
    return group.startswith(TRANSPOSED_IN)


def _mats_in(group, win_l):
    if _is_transposed(group):
        return {("win_t" if group.startswith("ffn") else group + "_win_t"): win_l.reshape(-1, win_l.shape[2])}
    return {group + "_win": win_l}


def _mats_out(group, wout_l):
    pre = "" if group.startswith("ffn") else group + "_"
    return {pre + "wout": wout_l.reshape(-1, wout_l.shape[2])}


def _group_mats(group, lands):
    m = {**_mats_in(group, lands[0]), **_mats_out(group, lands[1])}
    return {group: m} if group.startswith("ffn") else m


def _grad_blocks(which, grad):
    if grad.ndim == 3:
        return grad if grad.shape[0] == NDEV else grad.reshape(NDEV, grad.shape[1] // NDEV, grad.shape[2])
    if which == "w_in":
        k, n = grad.shape
        return jnp.transpose(grad.reshape(k, NDEV, n // NDEV), (1, 0, 2)).astype(BF16)
    return grad.reshape(NDEV, grad.shape[0] // NDEV, grad.shape[1]).astype(BF16)


def kernel(x, c, ctx, c_ctx, ada_w, ada_b, norm_g, ffn_w_in, ffn_w_out, ssd_w_in, ssd_conv_w, ssd_conv_b, ssd_dt_bias, ssd_A_log, ssd_D, ssd_norm_g, ssd_w_out, gm_w_in, gm_v_g, gm_v_b, gm_w_s, gm_b_s, gm_w_out, loss_target, m_c_ctx, m_ada_w, m_ada_b, m_norm_g, m_ffn_w_in, m_ffn_w_out, m_ssd_w_in, m_ssd_conv_w, m_ssd_conv_b, m_ssd_dt_bias, m_ssd_A_log, m_ssd_D, m_ssd_norm_g, m_ssd_w_out, m_gm_w_in, m_gm_v_g, m_gm_v_b, m_gm_w_s, m_gm_b_s, m_gm_w_out, v_c_ctx, v_ada_w, v_ada_b, v_norm_g, v_ffn_w_in, v_ffn_w_out, v_ssd_w_in, v_ssd_conv_w, v_ssd_conv_b, v_ssd_dt_bias, v_ssd_A_log, v_ssd_D, v_ssd_norm_g, v_ssd_w_out, v_gm_w_in, v_gm_v_g, v_gm_v_b, v_gm_w_s, v_gm_b_s, v_gm_w_out):
    me = 4 * lax.axis_index("x") + 2 * lax.axis_index("y") + lax.axis_index("c")
    d = D_MODEL
    ncol = N_MOD * d // NDEV

    small_pack = jnp.concatenate([c.reshape(-1), norm_g.reshape(-1), ssd_conv_w.reshape(-1),
                                  gm_v_g.reshape(-1), gm_v_b.reshape(-1)])[None, :]
    (sp,), _ = _exchange([small_pack], scatter=False, name="gather_small")
    sp = sp[:, 0]
    o = 0
    c_all = sp[:, o:o + d]; o += d
    ng_all = sp[:, o:o + 2 * 6 * 128].reshape(NDEV, 2, 6, 128); o += 2 * 6 * 128
    cw_all = sp[:, o:o + SSD_CONV * 512].reshape(NDEV, SSD_CONV, 512); o += SSD_CONV * 512
    vg_all = sp[:, o:o + 256]; o += 256
    vb_all = sp[:, o:o + 256]; o += 256
    norm_g_full = jnp.transpose(ng_all, (1, 2, 0, 3)).reshape(2, 6, d)
    conv_w_full = jnp.transpose(cw_all, (1, 0, 2)).reshape(SSD_CONV, SSD_CONV_DIM)
    gm_v_g_full = vg_all.reshape(-1)
    gm_v_b_full = vb_all.reshape(-1)

    c16 = jnp.concatenate([c_all, jnp.broadcast_to(c_ctx[None, :], (NDEV, d))], axis=0)
    ada_b_loc = lax.dynamic_slice_in_dim(ada_b, me * ncol, ncol, axis=1)
    mods_loc = jnp.stack([_mm_f32(c16, ada_w[i], name=f"ada_mod{i}", silu_a=True, bias=ada_b_loc[i][None, :])
                          for i in range(2)])
    (mods_all,), mods_done = _exchange([mods_loc], scatter=False, name="gather_mods")

    tr = lambda a: jnp.swapaxes(a, -1, -2)
    shard = {"ssd": (tr(ssd_w_in)[0], ssd_w_out[0]), "gm": (gm_w_in[0], gm_w_out[0])}
    for i in range(2):
        for j in range(2):
            shard[f"ffn{i}{j}"] = (tr(ffn_w_in)[i, j], ffn_w_out[i, j])
    apart = GROUPS[:2]
    units = []
    for grp in GROUPS:
        units += [(grp + "_in", grp, (0,)), (grp + "_out", grp, (1,))] if grp in apart else [(grp, grp, (0, 1))]
    gathers = {}
    started = mods_done
    for unit, grp, idx in units:
        srcs = [(shard[grp][k] + started).astype(BF16) for k in idx]
        st = _exchange_start(srcs, [_landing(s, me) for s in srcs], scatter=False, name="gather_start_" + unit)
        gathers[unit] = st[:4]
        started = st[4]

    def fetch(unit, after):
        return _exchange_wait(*gathers[unit], after, scatter=False, name="gather_wait_" + unit)

    def get_w(grp, after):
        if grp not in apart:
            return _group_mats(grp, fetch(grp, after))
        early = lambda later: _mats_in(grp, fetch(grp + "_in", later)[0])
        late = lambda later: _mats_out(grp, fetch(grp + "_out", later)[0])
        if grp.startswith("ffn"):
            return {grp: dict(early=early, late=late)}
        return dict(early(after), late=late)

    scatters = {}
    held = {}

    def put_grad(grp, which, grad):
        if grp in apart:
            unit, blocks = grp + "_" + which[2:], [_grad_blocks(which, grad)]
        else:
            held[grp, which] = _grad_blocks(which, grad)
            if (grp, "w_in") not in held or (grp, "w_out") not in held:
                return None
            unit, blocks = grp, [held[grp, "w_in"], held[grp, "w_out"]]
        own = [lax.dynamic_index_in_dim(b, me, axis=0, keepdims=False) for b in blocks]
        st = _exchange_start(blocks, [_landing(o_, me) for o_ in own], scatter=True, name="scatter_start_" + unit)
        scatters[unit] = st[:4]
        return st[4]

    mods_rows = jnp.transpose(mods_all, (1, 2, 0, 3)).reshape(2, 2 * NDEV, N_MOD * d) + started
    mx = lax.dynamic_index_in_dim(mods_rows, me, axis=1, keepdims=False).reshape(2, N_MOD, d)
    mc = mods_rows[:, NDEV].reshape(2, N_MOD, d)
    mods = [(mc[i], mx[i]) for i in range(2)]

    small = dict(conv_w8=jnp.pad(conv_w_full, ((0, 8 - SSD_CONV), (0, 0))), conv_b=ssd_conv_b, dt_bias=ssd_dt_bias[0],
                 a_log=ssd_A_log[0], ssd_d=ssd_D[0], ssd_norm_g=ssd_norm_g[0], gm_v_g=gm_v_g_full,
                 gm_v_b=gm_v_b_full, gm_w_s=gm_w_s[0], gm_b_s=gm_b_s[0])
    loss_parts, grad_x, g = _local_step(x[0], ctx[0], loss_target[0], mods, norm_g_full, get_w, small, put_grad)
    g["loss"] = (0.5 / d * jnp.sum(loss_parts)).reshape(1)

    whole = {"ffn_w_in": (tr(ffn_w_in), tr(m_ffn_w_in), tr(v_ffn_w_in)), "ffn_w_out": (ffn_w_out, m_ffn_w_out, v_ffn_w_out),
             "ssd_w_in": (tr(ssd_w_in), tr(m_ssd_w_in), tr(v_ssd_w_in)), "ssd_w_out": (ssd_w_out, m_ssd_w_out, v_ssd_w_out),
             "gm_w_in": (gm_w_in, m_gm_w_in, v_gm_w_in), "gm_w_out": (gm_w_out, m_gm_w_out, v_gm_w_out)}
    res = {}

    def update_units(some, after):
        for unit, grp, idx in some:
            parts = _exchange_wait(*scatters[unit], after, scatter=True, name="scatter_wait_" + unit)
            for k, p in zip(idx, parts):
                which = ("in", "out")[k]
                nm = ("ffn" if grp.startswith("ffn") else grp) + "_w_" + which
                sel = (int(grp[3]), int(grp[4])) if grp.startswith("ffn") else (0,)
                res[nm] = _adamw(p, *whole[nm], name=f"adamw_{grp}_{which}", sel=sel, into=res.get(nm))
                after = res[nm][0]
        return after

    sg_names = ["dmx", "dmc", "norm_g", "ssd_conv_w", "ssd_conv_b", "ssd_dt_bias", "ssd_A_log", "ssd_D", "ssd_norm_g",
                "gm_v_g", "gm_v_b", "gm_w_s", "gm_b_s", "loss"]
    sg_shapes = [g[n].shape for n in sg_names]
    flat = jnp.concatenate([g[n].reshape(-1) for n in sg_names])
    npack = flat.shape[0]
    pad = (-npack) % 1024
    flat = jnp.pad(flat, (0, pad)).reshape(-1, 128)
    sg_start = _exchange_start([flat], [_landing(flat, me)], scatter=False, name="small_grads_start")
    by_send = list(reversed(units))
    update_units(by_send[:4], jnp.stack([sg_start[4], grad_x[0, 0]]))
    early_done = jnp.stack([res[nm][0].reshape(-1)[-1] for nm in sorted(res)])
    (sg_all,) = _exchange_wait(*sg_start[:4], early_done, scatter=False, name="small_grads_wait")
    sg_sum = _sum_slots(sg_all, name="sum_small_grads").reshape(-1)[:npack]
    update_units(by_send[4:], sg_sum)
    sums = {}
    o = 0
    for n, shp in zip(sg_names, sg_shapes):
        sz = math.prod(shp)
        sums[n] = sg_sum[o:o + sz].reshape(shp)
        o += sz
    loss = sums["loss"][0]
    per_dev = sg_all.reshape(NDEV, -1)
    dmx_all =per_dev[:, :2 * N_MOD * d].reshape(NDEV, 2, N_MOD * d)
    dmc_all = per_dev[:, 2 * N_MOD * d:4 * N_MOD * d].reshape(NDEV, 2, N_MOD * d)

    (s16,) = _rowwise("ada_silu", lambda cc: ((_silu(cc),), ()), 2 * NDEV, [c16], [], [(d, F32)], tm=2 * NDEV)
    s16_t = s16.T
    g_ada_w, dcc_parts = [], []
    for i in range(2):
        rhs = jnp.concatenate([lax.dynamic_slice_in_dim(dmx_all[:, i], me * ncol, ncol, axis=1),
                               lax.dynamic_slice_in_dim(dmc_all[:, i], me * ncol, ncol, axis=1)], axis=0)
        g_ada_w.append(_mm_f32(s16_t, rhs, name=f"ada_dw{i}"))
        dmc_loc = lax.dynamic_slice_in_dim(sums["dmc"][i], me * ncol, ncol, axis=0)
        rhs_c = jnp.zeros((ncol, 128), F32).at[:, 0].set(dmc_loc)
        dcc_parts.append(_mm_f32(ada_w[i], rhs_c, name=f"ada_dcc{i}")[:, 0])
    g_ada_w = jnp.stack(g_ada_w)
    dcc_part = (dcc_parts[0] + dcc_parts[1]).reshape(8, 128)
    (dcc_all,), _ = _exchange([dcc_part], scatter=False, name="gather_dcc")
    g_c_ctx = _sum_slots(dcc_all, name="sum_dcc", scale_by=c_ctx.reshape(8, 128)).reshape(d)
    g_ada_b = sums["dmx"] + sums["dmc"]

    outs = _adamw(g_ada_w.reshape(1, -1, ncol), ada_w.reshape(-1, ncol), m_ada_w.reshape(-1, ncol),
                  v_ada_w.reshape(-1, ncol), name="adamw_ada_w")
    res["ada_w"] = [o_.reshape(ada_w.shape) for o_ in outs]

    loc = lambda a, ax, n: lax.dynamic_slice_in_dim(a, me * n, n, axis=ax)
    small_g = dict(c_ctx=g_c_ctx, ada_b=g_ada_b, norm_g=loc(sums["norm_g"], 2, 128),
                   ssd_conv_w=loc(sums["ssd_conv_w"], 1, 512)[None], ssd_conv_b=sums["ssd_conv_b"][None],
                   ssd_dt_bias=sums["ssd_dt_bias"][None], ssd_A_log=sums["ssd_A_log"][None], ssd_D=sums["ssd_D"][None],
                   ssd_norm_g=sums["ssd_norm_g"][None], gm_v_g=loc(sums["gm_v_g"], 0, 256)[None],
                   gm_v_b=loc(sums["gm_v_b"], 0, 256)[None], gm_w_s=sums["gm_w_s"][None], gm_b_s=sums["gm_b_s"][None])
    small_w = dict(c_ctx=(c_ctx, m_c_ctx, v_c_ctx), ada_b=(ada_b, m_ada_b, v_ada_b), norm_g=(norm_g, m_norm_g, v_norm_g),
                   ssd_conv_w=(ssd_conv_w, m_ssd_conv_w, v_ssd_conv_w), ssd_conv_b=(ssd_conv_b, m_ssd_conv_b, v_ssd_conv_b),
                   ssd_dt_bias=(ssd_dt_bias, m_ssd_dt_bias, v_ssd_dt_bias), ssd_A_log=(ssd_A_log, m_ssd_A_log, v_ssd_A_log),
                   ssd_D=(ssd_D, m_ssd_D, v_ssd_D), ssd_norm_g=(ssd_norm_g, m_ssd_norm_g, v_ssd_norm_g),
                   gm_v_g=(gm_v_g, m_gm_v_g, v_gm_v_g), gm_v_b=(gm_v_b, m_gm_v_b, v_gm_v_b),
                   gm_w_s=(gm_w_s, m_gm_w_s, v_gm_w_s), gm_b_s=(gm_b_s, m_gm_b_s, v_gm_b_s))
    sn = list(small_w)

    def pack(arrs):
        f = jnp.concatenate([a.reshape(-1) for a in arrs])
        return jnp.pad(f, (0, (-f.shape[0]) % (256 * 128))).reshape(-1, 128)

    pg = pack([small_g[n].reshape(small_w[n][0].shape) for n in sn])
    outs = _adamw(pg[None], pack([small_w[n][0] for n in sn]), pack([small_w[n][1] for n in sn]),
                  pack([small_w[n][2] for n in sn]), name="adamw_small")
    flat_outs = [o_.reshape(-1) for o_ in outs]
    o = 0
    for n in sn:
        shp = small_w[n][0].shape
        sz = math.prod(shp)
        res[n] = [fo[o:o + sz].reshape(shp) for fo in flat_outs]
        o += sz

    order = ["c_ctx", "ada_w", "ada_b", "norm_g", "ffn_w_in", "ffn_w_out", "ssd_w_in", "ssd_conv_w", "ssd_conv_b",
             "ssd_dt_bias", "ssd_A_log", "ssd_D", "ssd_norm_g", "ssd_w_out", "gm_w_in", "gm_v_g", "gm_v_b", "gm_w_s",
             "gm_b_s", "gm_w_out"]
    for nm in ("ffn_w_in", "ssd_w_in"):
        res[nm] = [tr(a) for a in res[nm]]
    result = [loss, grad_x[None]]
    for k in range(4):
        result += [res[n][k] for n in order]
    return tuple(result)
```

```python
import functools
import math

import jax
import jax.numpy as jnp
from jax import lax
from jax.experimental import pallas as pl
from jax.experimental.pallas import tpu as pltpu

F32 = jnp.float32
BF16 = jnp.bfloat16

NDEV = 8
D_MODEL = 1024
FFN_DIM = 2816
N_MOD = 9
EPS = 1e-6
SSD_INNER = 2048
SSD_HEADS = 32
SSD_HEAD_DIM = 64
SSD_GROUPS = 8
SSD_HPG = 4
SSD_STATE = 128
SSD_CONV = 5
SSD_CONV_DIM = 4096
CHUNK = 128
GM_INNER = 2048
GM_GROUPS = 8
GM_GROUP_DIM = 256
ADAM_LR = 0.001
ADAM_B1 = 0.9
ADAM_B2 = 0.999
ADAM_EPS = 1e-08
ADAM_WD = 0.01
ADAM_STEP = 10
NEG_BIG = -1e30
VMEM_LIMIT_BYTES = 56 * 1024 * 1024
HI = lax.Precision.HIGHEST


def _params(*sem):
    return pltpu.CompilerParams(dimension_semantics=sem, vmem_limit_bytes=VMEM_LIMIT_BYTES)


def _pick(n, target, mult=16):
    if n <= target:
        return n
    for t in range(target - target % mult, 0, -mult):
        if n % t == 0:
            return t
    raise ValueError((n, target, mult))


def _sig(x):
    return 0.5 * jnp.tanh(0.5 * x) + 0.5


def _silu(x):
    return x * _sig(x)


def _dsilu(x):
    s = _sig(x)
    return s * (1.0 + x * (1.0 - s))


_GELU_C = math.sqrt(2.0 / math.pi)


def _gelu(x):
    return 0.5 * x * (1.0 + jnp.tanh(_GELU_C * (x + 0.044715 * x * x * x)))


def _gelu_and_grad(x):
    x2 = x * x
    t = jnp.tanh(_GELU_C * (x + 0.044715 * x2 * x))
    half = 0.5 * (1.0 + t)
    return x * half, half + 0.5 * x * (1.0 - t * t) * _GELU_C * (1.0 + 3.0 * 0.044715 * x2)


def _dgelu(x):
    return _gelu_and_grad(x)[1]


def _softplus(x):
    return jnp.maximum(x, 0.0) + jnp.log1p(jnp.exp(-jnp.abs(x)))


def _sum0(v):
    return jnp.sum(v, axis=0, keepdims=True)


def _rms(h):
    r = lax.rsqrt(jnp.mean(h * h, axis=-1, keepdims=True) + EPS)
    return h * r, r


def _dot(a, b, dims=((1,), (0,)), precision=None):
    return lax.dot_general(a, b, (dims, ((), ())), preferred_element_type=F32, precision=precision)


_NT = ((1,), (1,))
_TN = ((0,), (0,))


def _rowwise(name, fn, n_rows, rows, consts, outs, accs=(), *, tm, nseg=1, seg_blocks=0):
    assert n_rows % tm == 0
    if nseg == 2:
        assert seg_blocks > 0
        seg = lambda i: jnp.where(i < seg_blocks, 0, 1)
    else:
        seg = lambda i: 0
    in_specs, args, lacking = [], [], []
    for r in rows:
        arr, width, cb, off = r if isinstance(r, tuple) else (r, r.shape[1], 0, 0)
        in_specs.append(pl.BlockSpec((tm, width), lambda i, cb=cb, off=off: (jnp.maximum(i + off, 0), cb)))
        args.append(arr)
        lacking.append(-off if off < 0 else 0)
    for kind, arr in consts:
        if kind == "seg":
            assert arr.shape[0] == nseg and arr.shape[1] == 1, arr.shape
            in_specs.append(pl.BlockSpec((None, 1, arr.shape[2]), lambda i: (seg(i), 0, 0)))
        else:
            in_specs.append(pl.BlockSpec(arr.shape, lambda i: (0, 0)))
        args.append(arr)
    out_shape = [jax.ShapeDtypeStruct((n_rows, w), dt) for w, dt in outs]
    out_specs = [pl.BlockSpec((tm, w), lambda i: (i, 0)) for w, _ in outs]
    out_shape += [jax.ShapeDtypeStruct((nseg, 1, w), F32) for w in accs]
    out_specs += [pl.BlockSpec((None, 1, w), lambda i: (seg(i), 0, 0)) for w in accs]
    n_in, n_out, n_acc = len(args), len(outs), len(accs)

    def kern(*refs):
        i = pl.program_id(0)
        ins = [r[...] for r in refs[:n_in]]
        for k, lack in enumerate(lacking):
            if lack:
                ins[k] = jnp.where(i >= lack, ins[k], jnp.zeros_like(ins[k]))
        res, terms = fn(*ins)
        for ref, v in zip(refs[n_in:n_in + n_out], res):
            ref[...] = v.astype(ref.dtype)
        if n_acc:
            sums = [_sum0(v) for v in terms]
            first = (i == 0) | (i == seg_blocks) if nseg == 2 else (i == 0)
            acc_refs = refs[n_in + n_out:]

            @pl.when(first)
            def _():
                for ref, v in zip(acc_refs, sums):
                    ref[...] = v

            @pl.when(jnp.logical_not(first))
            def _():
                for ref, v in zip(acc_refs, sums):
                    ref[...] += v

    res = pl.pallas_call(
        kern, name=name, grid=(n_rows // tm,), in_specs=in_specs, out_specs=out_specs, out_shape=out_shape,
        compiler_params=_params("arbitrary"),
    )(*args)
    return res


def _pre_fwd_fn(h, g, shift, scale):
    hh, _ = _rms(h)
    return (hh * g * (1.0 + scale) + shift,), ()


def _pre_bwd_fn(du, h, dres, g, scale):
    hh, r = _rms(h)
    n = hh * g
    dn = du * (1.0 + scale)
    dhh = dn * g
    dh = dres + r * (dhh - hh * jnp.mean(dhh * hh, axis=-1, keepdims=True))
    return (dh,), (du, du * n, dn * hh)


def _post_fwd_fn(weight, h, y, g, gate):
    yh, _ = _rms(y)
    return (h + weight * gate * (yh * g),), ()


def _out_post_fn(weight, y, h, g, gate):
    return (y,) + _post_fwd_fn(weight, h, y, g, gate)[0], ()


def _post_bwd_fn(weight, dh, y, g, gate):
    yh, r = _rms(y)
    dr = dh * weight
    dyh = dr * gate * g
    dy = r * (dyh - yh * jnp.mean(dyh * yh, axis=-1, keepdims=True))
    return (dy,), (dr * yh * g, dr * gate * yh)


def _glu_bwd_fn(ds, a, b):
    a = a.astype(F32)
    b = b.astype(F32)
    sg = _sig(a)
    da = ds * b * (sg * (1.0 + a * (1.0 - sg)))
    db = ds * (a * sg)
    return (jnp.concatenate([da, db], axis=1),), ()


def _loss_fn(y, t):
    diff = y - t
    return (diff * (1.0 / D_MODEL),), (diff * diff,)


def _ssd_y(yf, yb, xs, z, dvec):
    y = yf + yb + dvec * xs
    return y, y * _silu(z)


def _ssdgate_fwd_fn(yf, yb, xs, z, dvec, ng):
    _, yg = _ssd_y(yf, yb, xs, z, dvec)
    parts = []
    for g in range(SSD_GROUPS):
        sl = slice(g * 256, (g + 1) * 256)
        parts.append(_rms(yg[:, sl])[0])
    return (jnp.concatenate(parts, axis=1) * ng,), ()


def _ssdgate_bwd_fn(dyn, yf, yb, xs, z, dvec, ng):
    y, yg = _ssd_y(yf, yb, xs, z, dvec)
    dyg_parts, ygh_parts = [], []
    for g in range(SSD_GROUPS):
        sl = slice(g * 256, (g + 1) * 256)
        ygh, r = _rms(yg[:, sl])
        d = dyn[:, sl] * ng[:, sl]
        dyg_parts.append(r * (d - ygh * jnp.mean(d * ygh, axis=-1, keepdims=True)))
        ygh_parts.append(ygh)
    dyg = jnp.concatenate(dyg_parts, axis=1)
    ygh = jnp.concatenate(ygh_parts, axis=1)
    dy = dyg * _silu(z)
    dz = dyg * y * _dsilu(z)
    return (dy, dz), (dyn * ygh, dy * xs)


def _ln_stats(v):
    mu = jnp.mean(v, axis=-1, keepdims=True)
    vc = v - mu
    r = lax.rsqrt(jnp.mean(vc * vc, axis=-1, keepdims=True) + EPS)
    return vc * r, r


def _gm_act_fwd_fn(p, vg, vb):
    gu = _gelu(p[:, :GM_INNER])
    gvh, _ = _ln_stats(_gelu(p[:, GM_INNER:]))
    return (gu, gvh * vg + vb), ()


def _gm_act_bwd_fn(p, dgu, dgvn, vg):
    pu = p[:, :GM_INNER]
    pv = p[:, GM_INNER:]
    gv, dgelu_v = _gelu_and_grad(pv)
    gvh, r = _ln_stats(gv)
    dgvh = dgvn * vg
    dgv = r * (dgvh - jnp.mean(dgvh, axis=-1, keepdims=True) - gvh * jnp.mean(dgvh * gvh, axis=-1, keepdims=True))
    dp = jnp.concatenate([dgu * _dgelu(pu), dgv * dgelu_v], axis=1)
    return (dp,), (dgvn * gvh, dgvn)


def _mm(a, b, *, out_dtype, name, tm=1088, tn=1024, tk=1408, add=None, rhs_t=False, n=None, b_off=(0, 0)):
    m, k = a.shape
    col_blocked = b.ndim == 3
    if col_blocked:
        assert not rhs_t and n is None and b.shape[1] == k
        n, tn = b.shape[0] * b.shape[2], b.shape[2]
    elif n is None:
        n, k2 = b.shape if rhs_t else b.shape[::-1]
        assert k == k2
    tm, tn, tk = _pick(m, tm), _pick(n, tn, 128), _pick(k, tk, 128)
    o0, o1 = b_off
    nk = k // tk
    dims = _NT if rhs_t else ((1,), (0,))

    def kern(*refs):
        a_ref, b_ref = refs[:2]
        add_ref = refs[2] if add is not None else None
        o_ref = refs[3] if add is not None else refs[2]

        def finish(r):
            if add is not None:
                r = r + add_ref[...]
            o_ref[...] = r.astype(o_ref.dtype)

        p = _dot(a_ref[...], b_ref[...], dims)
        if nk == 1:
            finish(p)
            return
        acc_ref = refs[-1]
        kk = pl.program_id(2)

        @pl.when(kk == 0)
        def _():
            acc_ref[...] = p

        @pl.when((kk > 0) & (kk < nk - 1))
        def _():
            acc_ref[...] += p

        @pl.when(kk == nk - 1)
        def _():
            finish(acc_ref[...] + p)

    if col_blocked:
        b_spec = pl.BlockSpec((None, tk, tn), lambda i, j, kk: (j, kk, 0))
    elif rhs_t:
        b_spec = pl.BlockSpec((tn, tk), lambda i, j, kk: (j + o0, kk + o1))
    else:
        b_spec = pl.BlockSpec((tk, tn), lambda i, j, kk: (kk + o0, j + o1))
    in_specs = [pl.BlockSpec((tm, tk), lambda i, j, kk: (i, kk)), b_spec]
    args = [a, b]
    if add is not None:
        in_specs.append(pl.BlockSpec((tm, tn), lambda i, j, kk: (i, j)))
        args.append(add)
    return pl.pallas_call(
        kern, name=name, grid=(m // tm, n // tn, nk), in_specs=in_specs,
        out_specs=pl.BlockSpec((tm, tn), lambda i, j, kk: (i, j)),
        out_shape=jax.ShapeDtypeStruct((m, n), out_dtype),
        scratch_shapes=[pltpu.VMEM((tm, tn), F32)] if nk > 1 else [],
        compiler_params=_params("parallel", "parallel", "arbitrary"),
    )(*args)


def _mm_rows(a, b, fn, rows, consts, outs, accs=(), *, name, tm=544, tk=1408, rhs_t=False, n_ctx=0):
    halves = a.ndim == 3
    m, k = (a.shape[1], 2 * a.shape[2]) if halves else a.shape
    col_blocked = b.ndim == 3
    kb, nb = 1, None
    if col_blocked:
        assert rhs_t and b.shape[0] * b.shape[2] == k
        n, nb = b.shape[1], b.shape[2]
        kb = max(1, tk // nb)
        assert b.shape[0] % kb == 0
        tk = kb * nb
    else:
        n = b.shape[0] if rhs_t else b.shape[1]
    tm, tk = _pick(m, tm), _pick(k, tk, 128)
    nk = k // tk
    if halves:
        hb = k // 2 // tk
        a_spec = pl.BlockSpec((None, tm, tk), lambda i, kk: (kk // hb, i, kk % hb))
    else:
        a_spec = pl.BlockSpec((tm, tk), lambda i, kk: (i, kk))
    dims = _NT if rhs_t else ((1,), (0,))
    n_rows, n_const, n_out, n_acc = len(rows), len(consts), len(outs), len(accs)

    def kern(*refs):
        a_ref, b_ref = refs[:2]
        row_refs = refs[2:2 + n_rows]
        const_refs = refs[2 + n_rows:2 + n_rows + n_const]
        out_refs = refs[2 + n_rows + n_const:2 + n_rows + n_const + n_out]
        acc_refs = refs[2 + n_rows + n_const + n_out:2 + n_rows + n_const + n_out + n_acc]
        i, kk = pl.program_id(0), pl.program_id(1)

        def finish(p, rs=slice(None), r0=0):
            nr = p.shape[0]
            is_ctx = (i * tm + r0 + lax.broadcasted_iota(jnp.int32, (nr, 1), 0)) < n_ctx
            cvals = []
            for (kind, arr), ref in zip(consts, const_refs):
                if kind == "seg":
                    cvals.append(jnp.where(is_ctx, ref[0], ref[1]) if arr.shape[0] == 2 else ref[0])
                else:
                    cvals.append(ref[...])
            res, terms = fn(p, *[r[rs, :] for r in row_refs], *cvals)
            for ref, v in zip(out_refs, res):
                ref[rs, :] = v.astype(ref.dtype)
            for ref, v in zip(acc_refs, terms):
                s_all = _sum0(v)
                s_ctx = _sum0(jnp.where(is_ctx, v, 0.0)) if n_ctx else jnp.zeros_like(s_all)
                both = jnp.concatenate([s_ctx, s_all - s_ctx], axis=0)[:, None, :]

                @pl.when(i == 0)
                def _():
                    ref[...] = both

                @pl.when(i > 0)
                def _():
                    ref[...] += both

        if nk == 1 and n_acc == 0:
            nsub = 2 if tm % 32 == 0 else 1
            sub = tm // nsub
            for r in range(nsub):
                rs = slice(r * sub, (r + 1) * sub)
                finish(_dot(a_ref[rs, :], b_ref[...], dims), rs, r * sub)
            return
        if col_blocked:
            p = sum(_dot(a_ref[:, c * nb:(c + 1) * nb], b_ref[c], dims) for c in range(kb))
        else:
            p = _dot(a_ref[...], b_ref[...], dims)
        if nk == 1:
            finish(p)
            return
        scr = refs[-1]

        @pl.when(kk == 0)
        def _():
            scr[...] = p

        @pl.when((kk > 0) & (kk < nk - 1))
        def _():
            scr[...] += p

        @pl.when(kk == nk - 1)
        def _():
            finish(scr[...] + p)

    if col_blocked:
        b_spec = pl.BlockSpec((kb, n, nb), lambda i, kk: (kk, 0, 0))
    elif rhs_t:
        b_spec = pl.BlockSpec((n, tk), lambda i, kk: (0, kk))
    else:
        b_spec = pl.BlockSpec((tk, n), lambda i, kk: (kk, 0))
    in_specs = [a_spec, b_spec]
    in_specs += [pl.BlockSpec((tm, r.shape[1]), lambda i, kk: (i, 0)) for r in rows]
    for kind, arr in consts:
        in_specs.append(pl.BlockSpec(arr.shape, (lambda i, kk: (0, 0, 0)) if kind == "seg" else (lambda i, kk: (0, 0))))
    out_shape = [jax.ShapeDtypeStruct((m, w), dt) for w, dt in outs]
    out_specs = [pl.BlockSpec((tm, w), lambda i, kk: (i, 0)) for w, _ in outs]
    out_shape += [jax.ShapeDtypeStruct((2, 1, w), F32) for w in accs]
    out_specs += [pl.BlockSpec((2, 1, w), lambda i, kk: (0, 0, 0)) for w in accs]
    return pl.pallas_call(
        kern, name=name, grid=(m // tm, nk), in_specs=in_specs, out_specs=out_specs, out_shape=out_shape,
        scratch_shapes=[pltpu.VMEM((tm, n), F32)] if nk > 1 else [],
        compiler_params=_params("arbitrary", "arbitrary"),
    )(a, b, *rows, *[arr for _, arr in consts])


def _mm_glu(u, win_t, *, name, tm=2176, tn=256):
    m, k = u.shape
    n = win_t.shape[0] // 2
    tm, tn = _pick(m, tm), _pick(n, tn, 128)
    nj = n // tn

    nsub = 4 if tm % 64 == 0 else 1
    sub = tm // nsub

    def kern(u_ref, wa_ref, wb_ref, s_ref, a_ref, b_ref):
        for r in range(nsub):
            rows = slice(r * sub, (r + 1) * sub)
            uu = u_ref[rows, :]
            a = _dot(uu, wa_ref[...], _NT)
            b = _dot(uu, wb_ref[...], _NT)
            s_ref[rows, :] = (_silu(a) * b).astype(BF16)
            a_ref[rows, :] = a.astype(BF16)
            b_ref[rows, :] = b.astype(BF16)

    ospec = pl.BlockSpec((tm, tn), lambda i, j: (i, j))
    return pl.pallas_call(
        kern, name=name, grid=(m // tm, nj),
        in_specs=[pl.BlockSpec((tm, k), lambda i, j: (i, 0)), pl.BlockSpec((tn, k), lambda i, j: (j, 0)),
                  pl.BlockSpec((tn, k), lambda i, j: (nj + j, 0))],
        out_specs=[ospec, ospec, ospec],
        out_shape=[jax.ShapeDtypeStruct((m, n), BF16)] * 3,
        compiler_params=_params("parallel", "parallel"),
    )(u, win_t, win_t)


def _mm_glu_bwd(dy, wout, a, b, *, name, tm=544, tn=1408):
    m, k = dy.shape
    f = wout.shape[0]
    tm, tn = _pick(m, tm), _pick(f, tn, 128)
    nsub = 2 if tm % 32 == 0 else 1
    sub = tm // nsub

    def kern(dy_ref, w_ref, a_ref, b_ref, o_ref):
        for r in range(nsub):
            rs = slice(r * sub, (r + 1) * sub)
            ds = _dot(dy_ref[rs, :], w_ref[...], _NT)
            (dp,), _ = _glu_bwd_fn(ds, a_ref[rs, :], b_ref[rs, :])
            o_ref[0, rs, :] = dp[:, :tn].astype(BF16)
            o_ref[1, rs, :] = dp[:, tn:].astype(BF16)

    tile = pl.BlockSpec((tm, tn), lambda i, j: (i, j))
    return pl.pallas_call(
        kern, name=name, grid=(m // tm, f // tn),
        in_specs=[pl.BlockSpec((tm, k), lambda i, j: (i, 0)), pl.BlockSpec((tn, k), lambda i, j: (j, 0)), tile, tile],
        out_specs=pl.BlockSpec((2, tm, tn), lambda i, j: (0, i, j)),
        out_shape=jax.ShapeDtypeStruct((2, m, f), BF16),
        compiler_params=_params("parallel", "parallel"),
    )(dy, wout, a, b)


def _mm_tn(a, b, *, name, tm=1024, tn=1024, tk=2176, col_blocks=None, stack=None):
    extra, extra_specs, aliases = [], [], {}
    halves = a.ndim == 3
    t, m = (a.shape[1], 2 * a.shape[2]) if halves else a.shape
    t2, n = b.shape
    assert t == t2
    tm, tn, tk = _pick(m, tm, 128), _pick(n, tn, 128), _pick(t, tk)
    nk = t // tk
    if halves:
        hb = m // 2 // tm
        a_spec = pl.BlockSpec((None, tk, tm), lambda i, j, kk: (i // hb, kk, i % hb))
    else:
        a_spec = pl.BlockSpec((tk, tm), lambda i, j, kk: (kk, i))
    if col_blocks is None:
        def kern(a_ref, b_ref, o_ref):
            kk = pl.program_id(2)

            @pl.when(kk == 0)
            def _():
                o_ref[...] = jnp.zeros_like(o_ref)

            o_ref[...] += _dot(a_ref[...], b_ref[...], _TN)

        out_spec = pl.BlockSpec((tm, tn), lambda i, j, kk: (i, j))
        out_shape = jax.ShapeDtypeStruct((m, n), F32)
        scratch = []
    else:
        wb = n // col_blocks
        per = tn // wb
        assert tn % wb == 0 and wb % 8 == 0

        def kern(a_ref, b_ref, *rest):
            o_ref, acc_ref = rest[-2:]
            kk = pl.program_id(2)
            p = _dot(a_ref[...], b_ref[...], _TN)

            @pl.when(kk == 0)
            def _():
                acc_ref[...] = p

            @pl.when((kk > 0) & (kk < nk - 1))
            def _():
                acc_ref[...] += p

            @pl.when(kk == nk - 1)
            def _():
                r = acc_ref[...] + p if nk > 1 else p
                for c in range(per):
                    o_ref[c] = r[:, c * wb:(c + 1) * wb].astype(BF16)

        rows_total, row0, into = stack if stack is not None else (m, 0, None)
        assert row0 % tm == 0
        out_spec = pl.BlockSpec((per, tm, wb), lambda i, j, kk: (j, i + row0 // tm, 0))
        out_shape = jax.ShapeDtypeStruct((col_blocks, rows_total, wb), BF16)
        scratch = [pltpu.VMEM((tm, tn), F32)]
        if into is not None:
            extra, extra_specs, aliases = [into], [pl.BlockSpec(memory_space=pl.ANY)], {2: 0}

    return pl.pallas_call(
        kern, name=name, grid=(m // tm, n // tn, nk),
        in_specs=[a_spec, pl.BlockSpec((tk, tn), lambda i, j, kk: (kk, j))] + extra_specs,
        out_specs=out_spec, out_shape=out_shape, scratch_shapes=scratch, input_output_aliases=aliases,
        compiler_params=_params("parallel", "parallel", "arbitrary"),
    )(a, b, *extra)


def _mm_f32(a, b, *, name, silu_a=False, bias=None):
    m, k = a.shape
    n = b.shape[1]

    def kern(*refs):
        if bias is None:
            a_ref, b_ref, o_ref = refs
        else:
            a_ref, b_ref, bias_ref, o_ref = refs
        av = a_ref[...]
        if silu_a:
            av = _silu(av)
        r = jnp.dot(av, b_ref[...], preferred_element_type=F32, precision=HI)
        if bias is not None:
            r = r + bias_ref[...]
        o_ref[...] = r

    args = [a, b] + ([] if bias is None else [bias])
    return pl.pallas_call(kern, name=name, out_shape=jax.ShapeDtypeStruct((m, n), F32),
                          compiler_params=pltpu.CompilerParams(vmem_limit_bytes=VMEM_LIMIT_BYTES))(*args)


CONV_WIN = 32


def _conv_windows(n, n_ctx):
    assert n_ctx % CONV_WIN == 0 and n_ctx >= CONV_WIN and n - n_ctx >= CONV_WIN
    return (0, n_ctx - CONV_WIN // 2, n - CONV_WIN)


def _tap_outside(r0, s, n, n_ctx):
    t = r0 + lax.broadcasted_iota(jnp.int32, (CONV_WIN, 1), 0)
    lo = jnp.where(t < n_ctx, 0, n_ctx)
    hi = jnp.where(t < n_ctx, n_ctx, n)
    return jnp.where((t + s >= lo) & (t + s < hi), 0.0, 1.0)


def _rolled(v, s):
    return v if s == 0 else pltpu.roll(v, (-s) % v.shape[0], 0)


def _conv_fwd(xp, w8, b, *, n_ctx, name, cb=256):
    n, c = xp.shape
    half = SSD_CONV // 2

    def kern(x_ref, w_ref, b_ref, cpre_ref, act_ref):
        x = x_ref[...]
        acc = jnp.zeros_like(x) + b_ref[...]
        rolled = {}
        for k in range(SSD_CONV):
            rolled[k] = _rolled(x, k - half)
            acc = acc + rolled[k] * w_ref[k:k + 1, :]
        cpre_ref[...] = acc
        act_ref[...] = _silu(acc)
        for r0 in _conv_windows(n, n_ctx):
            rows = slice(r0, r0 + CONV_WIN)
            fix = acc[rows]
            for k in range(SSD_CONV):
                if k != half:
                    fix = fix - rolled[k][rows] * w_ref[k:k + 1, :] * _tap_outside(r0, k - half, n, n_ctx)
            cpre_ref[rows, :] = fix
            act_ref[rows, :] = _silu(fix)

    spec = pl.BlockSpec((n, cb), lambda j: (0, j))
    return pl.pallas_call(
        kern, name=name, grid=(c // cb,),
        in_specs=[spec, pl.BlockSpec((8, cb), lambda j: (0, j)), pl.BlockSpec((1, cb), lambda j: (0, j))],
        out_specs=[spec, spec], out_shape=[jax.ShapeDtypeStruct((n, c), F32)] * 2,
        compiler_params=_params("parallel"),
    )(xp, w8, b)


def _conv_bwd(d1, d2, cpre, xp, w8, *, n_ctx, name, cb=128):
    n, c = xp.shape
    half = SSD_CONV // 2

    def kern(d1_ref, d2_ref, cpre_ref, x_ref, w_ref, dx_ref, dw_ref, db_ref):
        g = (d1_ref[...] + d2_ref[...]) * _dsilu(cpre_ref[...])
        x = x_ref[...]
        dx = jnp.zeros_like(g)
        dw_ref[...] = jnp.zeros_like(dw_ref)
        g_rolled = {}
        for k in range(SSD_CONV):
            s = k - half
            g_rolled[k] = _rolled(g, -s)
            dx = dx + g_rolled[k] * w_ref[k:k + 1, :]
            xr = _rolled(x, s)
            dw = _sum0(g * xr)
            if s != 0:
                for r0 in _conv_windows(n, n_ctx):
                    rows = slice(r0, r0 + CONV_WIN)
                    dw = dw - _sum0(g[rows] * xr[rows] * _tap_outside(r0, s, n, n_ctx))
            dw_ref[k:k + 1, :] = dw
        dx_ref[...] = dx.astype(BF16)
        for r0 in _conv_windows(n, n_ctx):
            rows = slice(r0, r0 + CONV_WIN)
            fix = dx[rows]
            for k in range(SSD_CONV):
                if k != half:
                    fix = fix - g_rolled[k][rows] * w_ref[k:k + 1, :] * _tap_outside(r0, half - k, n, n_ctx)
            dx_ref[rows, :] = fix.astype(BF16)
        db_ref[...] = _sum0(g)

    spec = pl.BlockSpec((n, cb), lambda j: (0, j))
    return pl.pallas_call(
        kern, name=name, grid=(c // cb,),
        in_specs=[spec, spec, spec, spec, pl.BlockSpec((8, cb), lambda j: (0, j))],
        out_specs=[spec, pl.BlockSpec((8, cb), lambda j: (0, j)), pl.BlockSpec((1, cb), lambda j: (0, j))],
        out_shape=[jax.ShapeDtypeStruct((n, c), BF16), jax.ShapeDtypeStruct((8, c), F32),
                   jax.ShapeDtypeStruct((1, c), F32)],
        compiler_params=_params("parallel"),
    )(d1, d2, cpre, xp, w8)


def _chunk_of(s, nc, n_ctx_chunks, rev):
    if not rev:
        return s
    return jnp.where(s < n_ctx_chunks, n_ctx_chunks - 1 - s, nc - 1 - (s - n_ctx_chunks))


def _scan_common(dt_raw, dtT_raw, bias_r, bias_c, alog_r, alog_c, rev):
    ii = lax.broadcasted_iota(jnp.int32, (CHUNK, CHUNK), 0)
    jj = lax.broadcasted_iota(jnp.int32, (CHUNK, CHUNK), 1)
    tri = (jj >= ii) if rev else (jj <= ii)
    tri_t = (ii >= jj) if rev else (ii <= jj)
    a_r = -jnp.exp(alog_r)
    a_c = -jnp.exp(alog_c)
    dt = _softplus(dt_raw + bias_r)
    dt_t = _softplus(dtT_raw + bias_c)
    al = dt * a_r
    acum = _dot(tri.astype(F32), al, precision=HI)
    acum_t = _dot(dt_t * a_c, tri_t.astype(F32), precision=HI)
    atot = _sum0(al)
    return tri, tri_t, a_r, dt, acum, acum_t, atot


def _head_spread():
    return jnp.repeat(jnp.eye(SSD_HEADS, dtype=BF16), SSD_HEAD_DIM, axis=1)


def _dot_sel(v, sel):
    hi = v.astype(BF16)
    lo = (v - hi.astype(F32)).astype(BF16)
    return _dot(hi, sel) + _dot(lo, sel)


def _ssd_scan_fwd(xbc, dt_raw, dtT_raw, bias_r, bias_c, alog_r, alog_c, *, rev, n_ctx_chunks, name):
    n = xbc.shape[0]
    nc = n // CHUNK
    cidx = functools.partial(_chunk_of, nc=nc, n_ctx_chunks=n_ctx_chunks, rev=rev)

    def kern(xs_ref, b_ref, c_ref, dt_ref, dtT_ref, br_ref, bc_ref, ar_ref, ac_ref, e_ref, y_ref, hs_ref, h_scr):
        @pl.when(pl.program_id(0) == 0)
        def _():
            h_scr[...] = jnp.zeros_like(h_scr)

        tri, _, _, dt, acum, acum_t, atot = _scan_common(
            dt_ref[...], dtT_ref[...], br_ref[...], bc_ref[...], ar_ref[...], ac_ref[...], rev)
        etot = jnp.exp(atot)
        spread = lambda v: _dot_sel(v, e_ref[...])
        xdt_all = xs_ref[...] * spread(dt)
        eax = spread(jnp.exp(acum))
        xdw_all = xdt_all * spread(jnp.exp(atot - acum))
        hs_ref[...] = h_scr[...]
        for g in range(SSD_GROUPS):
            gs = slice(g * 256, (g + 1) * 256)
            bg = b_ref[:, g * SSD_STATE:(g + 1) * SSD_STATE].astype(BF16)
            cg = c_ref[:, g * SSD_STATE:(g + 1) * SSD_STATE].astype(BF16)
            cb = _dot(cg, bg, _NT)
            h4 = h_scr[gs, :]
            ys = []
            for k in range(SSD_HPG):
                h = g * SSD_HPG + k
                lmat = jnp.exp(jnp.where(tri, acum[:, h:h + 1] - acum_t[h:h + 1, :], NEG_BIG))
                xdt_h = xdt_all[:, h * SSD_HEAD_DIM:(h + 1) * SSD_HEAD_DIM].astype(BF16)
                ys.append(_dot((cb * lmat).astype(BF16), xdt_h))
            y_ref[:, gs] = jnp.concatenate(ys, axis=1) + _dot(cg, h4.astype(BF16), _NT) * eax[:, gs]
            s4 = _dot(xdw_all[:, gs].astype(BF16), bg, _TN)
            for k in range(SSD_HPG):
                h = g * SSD_HPG + k
                rs = slice(h * SSD_HEAD_DIM, (h + 1) * SSD_HEAD_DIM)
                h_scr[rs, :] = h4[k * SSD_HEAD_DIM:(k + 1) * SSD_HEAD_DIM] * etot[:, h:h + 1] + \
                    s4[k * SSD_HEAD_DIM:(k + 1) * SSD_HEAD_DIM]

    nh = SSD_HEADS
    small = lambda shape: pl.BlockSpec(shape, lambda s: (0, 0))
    return pl.pallas_call(
        kern, name=name, grid=(nc,),
        in_specs=[pl.BlockSpec((CHUNK, SSD_INNER), lambda s: (cidx(s), 0)),
                  pl.BlockSpec((CHUNK, 1024), lambda s: (cidx(s), 2)),
                  pl.BlockSpec((CHUNK, 1024), lambda s: (cidx(s), 3)),
                  pl.BlockSpec((CHUNK, nh), lambda s: (cidx(s), 0)),
                  pl.BlockSpec((nh, CHUNK), lambda s: (0, cidx(s))),
                  small((1, nh)), small((nh, 1)), small((1, nh)), small((nh, 1)), small((nh, SSD_INNER))],
        out_specs=[pl.BlockSpec((CHUNK, SSD_INNER), lambda s: (cidx(s), 0)),
                   pl.BlockSpec((None, SSD_INNER, SSD_STATE), lambda s: (s, 0, 0))],
        out_shape=[jax.ShapeDtypeStruct((n, SSD_INNER), F32),
                   jax.ShapeDtypeStruct((nc, SSD_INNER, SSD_STATE), F32)],
        scratch_shapes=[pltpu.VMEM((SSD_INNER, SSD_STATE), F32)],
        compiler_params=_params("arbitrary"),
    )(xbc, xbc, xbc, dt_raw, dtT_raw, bias_r, bias_c, alog_r, alog_c, _head_spread())


def _ssd_scan_bwd(dy, xbc, hs, dt_raw, dtT_raw, bias_r, bias_c, alog_r, alog_c, dvec, *, rev, n_ctx_chunks,
                  direct, name):
    n = xbc.shape[0]
    nc = n // CHUNK
    nh = SSD_HEADS
    step_of = lambda r: nc - 1 - r
    cidx = lambda r: _chunk_of(step_of(r), nc, n_ctx_chunks, rev)

    def kern(dy_ref, xs_ref, b_ref, c_ref, hs_ref, dt_ref, dtT_ref, br_ref, bc_ref, ar_ref, ac_ref, dv_ref,
             e_ref, et_ref, dx_ref, ddt_ref, dal_ref, dbias_ref, dh_scr):
        @pl.when(pl.program_id(0) == 0)
        def _():
            dh_scr[...] = jnp.zeros_like(dh_scr)
            dal_ref[...] = jnp.zeros_like(dal_ref)
            dbias_ref[...] = jnp.zeros_like(dbias_ref)

        tri, tri_t, a_r, dt, acum, acum_t, atot = _scan_common(
            dt_ref[...], dtT_ref[...], br_ref[...], bc_ref[...], ar_ref[...], ac_ref[...], rev)
        etot = jnp.exp(atot)
        spread = lambda v: _dot_sel(v, e_ref[...])
        gather = lambda v: _dot_sel(v, et_ref[...])
        xs_all = xs_ref[...]
        dy_all = dy_ref[...]
        dtx = spread(dt)
        eax = spread(jnp.exp(acum))
        decx = spread(jnp.exp(atot - acum))
        xdt_all = xs_all * dtx
        xdw_all = xdt_all * decx
        dyo_all = dy_all * eax
        lane = lax.broadcasted_iota(jnp.int32, (CHUNK, nh), 1)
        lane1 = lax.broadcasted_iota(jnp.int32, (1, nh), 1)
        sub = lax.broadcasted_iota(jnp.int32, (nh, CHUNK), 0)
        g_rows = jnp.zeros((CHUNK, nh), F32)
        g_cols = jnp.zeros((nh, CHUNK), F32)
        dtot = jnp.zeros((1, nh), F32)
        q_col, q_e, q_dt = [], [], []
        for g in range(SSD_GROUPS):
            gs = slice(g * 256, (g + 1) * 256)
            bg = b_ref[:, g * SSD_STATE:(g + 1) * SSD_STATE].astype(BF16)
            cg = c_ref[:, g * SSD_STATE:(g + 1) * SSD_STATE].astype(BF16)
            cb = _dot(cg, bg, _NT)
            hs4 = hs_ref[gs, :]
            dh4 = dh_scr[gs, :]
            hs4_bf = hs4.astype(BF16)
            dh4_bf = dh4.astype(BF16)
            dy4 = dy_all[:, gs]
            dy4_bf = dy4.astype(BF16)
            xdt4_bf = xdt_all[:, gs].astype(BF16)
            xdw4 = xdw_all[:, gs]
            xdw4_bf = xdw4.astype(BF16)
            dyo4_bf = dyo_all[:, gs].astype(BF16)
            yoff4 = _dot(cg, hs4_bf, _NT) * eax[:, gs]
            dcg = _dot(dyo4_bf, hs4_bf)
            dh_new4 = _dot(dyo4_bf, cg, _TN)
            bdh4 = _dot(bg, dh4_bf, _NT)
            dbg = _dot(xdw4_bf, dh4_bf)
            e4 = xdw4 * bdh4
            q_col.append(dy4 * yoff4 - e4)
            q_e.append(e4)
            hsum = jnp.sum(dh4 * hs4, axis=1, keepdims=True)
            dcb = jnp.zeros((CHUNK, CHUNK), F32)
            dxdts = []
            for k in range(SSD_HPG):
                h = g * SSD_HPG + k
                ks = slice(k * SSD_HEAD_DIM, (k + 1) * SSD_HEAD_DIM)
                lmat = jnp.exp(jnp.where(tri, acum[:, h:h + 1] - acum_t[h:h + 1, :], NEG_BIG))
                mf = cb * lmat
                dm = _dot(dy4_bf[:, ks], xdt4_bf[:, ks], _NT)
                dcb = dcb + dm * lmat
                gmat = dm * mf
                g_rows = g_rows + jnp.where(lane == h, jnp.sum(gmat, axis=1, keepdims=True), 0.0)
                g_cols = g_cols + jnp.where(sub == h, _sum0(gmat), 0.0)
                dxdts.append(_dot(mf.astype(BF16), dy4_bf[:, ks], _TN))
                et = etot[:, h:h + 1]
                dtot = dtot + jnp.where(lane1 == h, _sum0(hsum[ks]) * et, 0.0)
                dh_scr[h * SSD_HEAD_DIM:(h + 1) * SSD_HEAD_DIM, :] = dh4[ks] * et + dh_new4[ks]
            dxdt4 = jnp.concatenate(dxdts, axis=1) + bdh4 * decx[:, gs]
            q_dt.append(dxdt4 * xs_all[:, gs])
            dx4 = dxdt4 * dtx[:, gs]
            if direct:
                dx4 = dx4 + dy4 * dv_ref[:, gs]
            dcb_bf = dcb.astype(BF16)
            dx_ref[:, gs] = dx4
            dx_ref[:, SSD_INNER + g * SSD_STATE:SSD_INNER + (g + 1) * SSD_STATE] = dbg + _dot(dcb_bf, cg, _TN)
            dx_ref[:, SSD_INNER + 1024 + g * SSD_STATE:SSD_INNER + 1024 + (g + 1) * SSD_STATE] = \
                dcg + _dot(dcb_bf, bg)
        e_heads = gather(jnp.concatenate(q_e, axis=1))
        dacum = gather(jnp.concatenate(q_col, axis=1)) + g_rows - g_cols.T
        dal = _dot(tri_t.astype(F32), dacum, precision=HI) + dtot + _sum0(e_heads)
        ddt = gather(jnp.concatenate(q_dt, axis=1)) + dal * a_r
        ddt_raw = ddt * _sig(dt_ref[...] + br_ref[...])
        ddt_ref[...] = ddt_raw
        dal_ref[...] += _sum0(dal * dt) * a_r
        dbias_ref[...] += _sum0(ddt_raw)

    small = lambda shape: pl.BlockSpec(shape, lambda r: (0, 0))
    return pl.pallas_call(
        kern, name=name, grid=(nc,),
        in_specs=[pl.BlockSpec((CHUNK, SSD_INNER), lambda r: (cidx(r), 0)),
                  pl.BlockSpec((CHUNK, SSD_INNER), lambda r: (cidx(r), 0)),
                  pl.BlockSpec((CHUNK, 1024), lambda r: (cidx(r), 2)),
                  pl.BlockSpec((CHUNK, 1024), lambda r: (cidx(r), 3)),
                  pl.BlockSpec((None, SSD_INNER, SSD_STATE), lambda r: (step_of(r), 0, 0)),
                  pl.BlockSpec((CHUNK, nh), lambda r: (cidx(r), 0)),
                  pl.BlockSpec((nh, CHUNK), lambda r: (0, cidx(r))),
                  small((1, nh)), small((nh, 1)), small((1, nh)), small((nh, 1)), small((1, SSD_INNER)),
                  small((nh, SSD_INNER)), small((SSD_INNER, nh))],
        out_specs=[pl.BlockSpec((CHUNK, SSD_CONV_DIM), lambda r: (cidx(r), 0)),
                   pl.BlockSpec((CHUNK, nh), lambda r: (cidx(r), 0)),
                   small((1, nh)), small((1, nh))],
        out_shape=[jax.ShapeDtypeStruct((n, SSD_CONV_DIM), F32), jax.ShapeDtypeStruct((n, nh), F32),
                   jax.ShapeDtypeStruct((1, nh), F32), jax.ShapeDtypeStruct((1, nh), F32)],
        scratch_shapes=[pltpu.VMEM((SSD_INNER, SSD_STATE), F32)],
        compiler_params=_params("arbitrary"),
    )(dy, xbc, xbc, xbc, hs, dt_raw, dtT_raw, bias_r, bias_c, alog_r, alog_c, dvec, _head_spread(),
      _head_spread().T)


def _gm_spatial_fwd(gu, gvn, ws, bst, *, name):
    n = gu.shape[0]

    def kern(gu_ref, gv_ref, ws_ref, bs_ref, o_ref):
        for g in range(GM_GROUPS):
            sl = slice(g * GM_GROUP_DIM, (g + 1) * GM_GROUP_DIM)
            s = _dot(ws_ref[g], gv_ref[:, sl]) + bs_ref[:, g:g + 1]
            o_ref[:, sl] = (gu_ref[:, sl] * s).astype(BF16)

    spec = pl.BlockSpec((CHUNK, GM_INNER), lambda i: (i, 0))
    return pl.pallas_call(
        kern, name=name, grid=(n // CHUNK,),
        in_specs=[spec, spec, pl.BlockSpec(ws.shape, lambda i: (0, 0, 0)), pl.BlockSpec(bst.shape, lambda i: (0, 0))],
        out_specs=spec, out_shape=jax.ShapeDtypeStruct((n, GM_INNER), BF16),
        compiler_params=_params("parallel"),
    )(gu, gvn, ws, bst)


def _gm_spatial_bwd(dt, gu, gvn, ws, wst, bst, *, name):
    n = gu.shape[0]

    def kern(dt_ref, gu_ref, gv_ref, ws_ref, wst_ref, bs_ref, dgu_ref, dgv_ref, dws_ref, dbs_ref):
        @pl.when(pl.program_id(0) == 0)
        def _():
            dws_ref[...] = jnp.zeros_like(dws_ref)
            dbs_ref[...] = jnp.zeros_like(dbs_ref)

        lane = lax.broadcasted_iota(jnp.int32, (CHUNK, GM_GROUPS), 1)
        dbs = jnp.zeros((CHUNK, GM_GROUPS), F32)
        for g in range(GM_GROUPS):
            sl = slice(g * GM_GROUP_DIM, (g + 1) * GM_GROUP_DIM)
            gv = gv_ref[:, sl]
            s = _dot(ws_ref[g], gv) + bs_ref[:, g:g + 1]
            d = dt_ref[:, sl]
            dgu_ref[:, sl] = d * s
            ds = d * gu_ref[:, sl]
            ds_bf = ds.astype(BF16)
            dws_ref[g] += _dot(ds_bf, gv, _NT)
            dgv_ref[:, sl] = _dot(wst_ref[g], ds_bf)
            dbs = dbs + jnp.where(lane == g, jnp.sum(ds, axis=1, keepdims=True), 0.0)
        dbs_ref[...] += dbs

    spec = pl.BlockSpec((CHUNK, GM_INNER), lambda i: (i, 0))
    wspec = pl.BlockSpec(ws.shape, lambda i: (0, 0, 0))
    bspec = pl.BlockSpec(bst.shape, lambda i: (0, 0))
    return pl.pallas_call(
        kern, name=name, grid=(n // CHUNK,),
        in_specs=[spec, spec, spec, wspec, wspec, bspec],
        out_specs=[spec, spec, wspec, bspec],
        out_shape=[jax.ShapeDtypeStruct((n, GM_INNER), F32), jax.ShapeDtypeStruct((n, GM_INNER), F32),
                   jax.ShapeDtypeStruct(ws.shape, F32), jax.ShapeDtypeStruct(bst.shape, F32)],
        compiler_params=_params("arbitrary"),
    )(dt, gu, gvn, ws, wst, bst)


def _adamw(parts, w, m, v, *, name, tm=256, sel=(), into=None):
    ns, r, wd = parts.shape
    tm = _pick(r, tm, 8)
    tc = wd
    if tm < 64 and wd % 256 == 0:
        tm, tc = r, 256
    lead = len(sel)
    assert w.shape[lead:] == (r, wd) and lead == w.ndim - 2

    def kern(*refs):
        p_ref, w_ref, m_ref, v_ref = refs[:4]
        g_ref, d_ref, nm_ref, nv_ref = refs[-4:]
        g = p_ref[0].astype(F32)
        for s in range(1, ns):
            g = g + p_ref[s].astype(F32)
        m2 = ADAM_B1 * m_ref[...] + (1.0 - ADAM_B1) * g
        v2 = ADAM_B2 * v_ref[...] + (1.0 - ADAM_B2) * (g * g)
        m_hat = m2 / (1.0 - ADAM_B1 ** ADAM_STEP)
        v_hat = v2 / (1.0 - ADAM_B2 ** ADAM_STEP)
        g_ref[...] = g
        d_ref[...] = -ADAM_LR * (m_hat / (jnp.sqrt(v_hat) + ADAM_EPS) + ADAM_WD * w_ref[...])
        nm_ref[...] = m2
        nv_ref[...] = v2

    spec = pl.BlockSpec((None,) * lead + (tm, tc), lambda i, j: tuple(sel) + (i, j))
    extra, aliases = [], {}
    if into is not None:
        extra = list(into)
        aliases = {4 + k: k for k in range(4)}
    return pl.pallas_call(
        kern, name=name, grid=(r // tm, wd // tc),
        in_specs=[pl.BlockSpec((ns, tm, tc), lambda i, j: (0, i, j)), spec, spec, spec] +
                 [pl.BlockSpec(memory_space=pl.ANY)] * len(extra),
        out_specs=[spec] * 4, out_shape=[jax.ShapeDtypeStruct(w.shape, F32)] * 4,
        input_output_aliases=aliases,
        compiler_params=_params("parallel", "parallel"),
    )(parts, w, m, v, *extra)


def _sum_slots(parts, *, name, scale_by=None):
    ns, r, wd = parts.shape

    def kern(*refs):
        p_ref, o_ref = refs[0], refs[-1]
        g = p_ref[0]
        for s in range(1, ns):
            g = g + p_ref[s]
        if scale_by is not None:
            g = g * _dsilu(refs[1][...])
        o_ref[...] = g

    args = [parts] + ([] if scale_by is None else [scale_by])
    return pl.pallas_call(kern, name=name, out_shape=jax.ShapeDtypeStruct((r, wd), F32),
                          compiler_params=pltpu.CompilerParams(vmem_limit_bytes=VMEM_LIMIT_BYTES))(*args)


def _mesh_pos():
    x, y, c = lax.axis_index("x"), lax.axis_index("y"), lax.axis_index("c")
    return x, y, c, 4 * x + 2 * y + c


def _flip(x, y, c, f):
    fx, fy, fc = (f >> 2) & 1, (f >> 1) & 1, f & 1
    px = 1 - x if fx else x
    py = 1 - y if fy else y
    pc = 1 - c if fc else c
    return (px, py, pc), 4 * px + 2 * py + pc


_HBM_SPEC = pl.BlockSpec(memory_space=pltpu.HBM)


def _exchange(arrays, *, scatter, name):
    na = len(arrays)
    if scatter:
        out_shape = [jax.ShapeDtypeStruct(a.shape, a.dtype) for a in arrays]
    else:
        out_shape = [jax.ShapeDtypeStruct((NDEV,) + a.shape, a.dtype) for a in arrays]

    out_shape.append(jax.ShapeDtypeStruct((8, 128), F32))

    def body(*refs):
        ins, outs = refs[:na], refs[na:2 * na]
        send_sems, recv_sems, local_sems = refs[2 * na + 1:]
        refs[2 * na][...] = jnp.zeros((8, 128), F32)
        x, y, c, me = _mesh_pos()
        copies = []
        for i in range(na):
            src_own = ins[i].at[me] if scatter else ins[i]
            lc = pltpu.make_async_copy(src_own, outs[i].at[me], local_sems.at[i])
            lc.start()
            copies.append(lc)
        sends = []
        for f in range(1, NDEV):
            peer, pidx = _flip(x, y, c, f)
            for i in range(na):
                k = i * (NDEV - 1) + f - 1
                src = ins[i].at[pidx] if scatter else ins[i]
                cp = pltpu.make_async_remote_copy(
                    src_ref=src, dst_ref=outs[i].at[me], send_sem=send_sems.at[k], recv_sem=recv_sems.at[k],
                    device_id=peer, device_id_type=pl.DeviceIdType.MESH)
                cp.start()
                sends.append(cp)
        for f in range(1, NDEV):
            peer, pidx = _flip(x, y, c, f)
            for i in range(na):
                k = i * (NDEV - 1) + f - 1
                src = ins[i].at[pidx] if scatter else ins[i]
                pltpu.make_async_remote_copy(
                    src_ref=src, dst_ref=outs[i].at[pidx], send_sem=send_sems.at[k], recv_sem=recv_sems.at[k],
                    device_id=peer, device_id_type=pl.DeviceIdType.MESH).wait_recv()
        for cp in sends:
            cp.wait_send()
        for lc in copies:
            lc.wait()

    res = pl.pallas_call(
        body, name=name, out_shape=out_shape, in_specs=[_HBM_SPEC] * na,
        out_specs=[_HBM_SPEC] * na + [pl.BlockSpec(memory_space=pltpu.VMEM)],
        scratch_shapes=[pltpu.SemaphoreType.DMA((na * (NDEV - 1),)), pltpu.SemaphoreType.DMA((na * (NDEV - 1),)),
                        pltpu.SemaphoreType.DMA((na,))],
        compiler_params=pltpu.CompilerParams(has_side_effects=True),
    )(*arrays)
    return res[:na], res[na][0, 0]


_SEM_SPEC = pl.BlockSpec(memory_space=pltpu.SEMAPHORE)
_DATAFLOW = pltpu.SideEffectType.DATAFLOW_SIDE_EFFECTING


def _split_copies(srcs, lands, send_sems, recv_sems, scatter, arriving):
    x, y, c, me = _mesh_pos()
    copies = []
    for i in range(len(srcs)):
        for f in range(1, NDEV):
            peer, pidx = _flip(x, y, c, f)
            k = i * (NDEV - 1) + f - 1
            copies.append(pltpu.make_async_remote_copy(
                src_ref=srcs[i].at[pidx] if scatter else srcs[i], dst_ref=lands[i].at[pidx if arriving else me],
                send_sem=send_sems.at[k], recv_sem=recv_sems.at[k], device_id=peer,
                device_id_type=pl.DeviceIdType.MESH))
    return copies


def _exchange_start(srcs, lands, *, scatter, name):
    na = len(srcs)
    nsem = na * (NDEV - 1)

    def body(*refs):
        ins_src, ins_land = refs[:na], refs[na:2 * na]
        send_sems, recv_sems = refs[2 * na], refs[2 * na + 1]
        token = refs[-1]
        for cp in _split_copies(ins_src, ins_land, send_sems, recv_sems, scatter, False):
            cp.start()
        token[...] = jnp.zeros_like(token)

    thru = [pltpu.HBM(a.shape, a.dtype) for a in list(srcs) + list(lands)]
    res = pl.pallas_call(
        body, name=name,
        out_shape=(pltpu.SemaphoreType.DMA((nsem,)), pltpu.SemaphoreType.DMA((nsem,)), *thru,
                   jax.ShapeDtypeStruct((8, 128), F32)),
        in_specs=[_HBM_SPEC] * (2 * na),
        out_specs=(_SEM_SPEC, _SEM_SPEC, *([_HBM_SPEC] * (2 * na)), pl.BlockSpec(memory_space=pltpu.VMEM)),
        input_output_aliases={i: 2 + i for i in range(2 * na)},
        compiler_params=pltpu.CompilerParams(has_side_effects=_DATAFLOW),
    )(*[pltpu.with_memory_space_constraint(a, pltpu.HBM) for a in list(srcs) + list(lands)])
    send_sems, recv_sems = res[0], res[1]
    return send_sems, recv_sems, res[2:2 + na], res[2 + na:2 + 2 * na], res[-1][0, 0]


def _exchange_wait(send_sems, recv_sems, srcs, lands, after, *, scatter, name):
    na = len(srcs)

    def body(*refs):
        ins_src, ins_land = refs[:na], refs[na:2 * na]
        s_sems, r_sems = refs[2 * na], refs[2 * na + 1]
        for cp in _split_copies(ins_src, ins_land, s_sems, r_sems, scatter, False):
            cp.wait_send()
        for cp in _split_copies(ins_src, ins_land, s_sems, r_sems, scatter, True):
            cp.wait_recv()

    thru = [pltpu.HBM(a.shape, a.dtype) for a in list(srcs) + list(lands)]
    res = pl.pallas_call(
        body, name=name, out_shape=tuple(thru),
        in_specs=[_HBM_SPEC] * (2 * na) + [_SEM_SPEC, _SEM_SPEC, pl.BlockSpec(memory_space=pl.ANY)],
        out_specs=tuple([_HBM_SPEC] * (2 * na)),
        input_output_aliases={i: i for i in range(2 * na)},
        compiler_params=pltpu.CompilerParams(has_side_effects=_DATAFLOW),
    )(*srcs, *lands, send_sems, recv_sems, after)
    return res[na:]


def _landing(block, me):
    buf = lax.empty((NDEV,) + block.shape, block.dtype)
    return lax.dynamic_update_slice_in_dim(buf, block[None], me, axis=0)


def _seg_kw(nseg, n_ctx, tm):
    return dict(nseg=nseg, seg_blocks=(n_ctx // tm if nseg == 2 else 0))


def _ffn_fwd(tag, h, gpre, gpost, shift, scale, gate, w, *, nseg, n_ctx, tm):
    n = h.shape[0]
    kw = _seg_kw(nseg, n_ctx, tm)
    (u,) = _rowwise(tag + "_pre", _pre_fwd_fn, n, [h], [("full", gpre), ("seg", shift), ("seg", scale)],
                    [(D_MODEL, BF16)], tm=tm, **kw)
    if "early" in w:
        w.update(w.pop("early")(u))
    s, a, b = _mm_glu(u, w["win_t"], name=tag + "_glu")
    if "late" in w:
        w.update(w.pop("late")(s))
    y, ho = _mm_rows(s, w["wout"], functools.partial(_out_post_fn, 0.5), [h], [("full", gpost), ("seg", gate)],
                     [(D_MODEL, F32), (D_MODEL, F32)], name=tag + "_out", tk=FFN_DIM, n_ctx=n_ctx)
    return ho, dict(h=h, u=u, s=s, a=a, b=b, y=y)


def _ffn_bwd(tag, dho, sv, gpre, gpost, scale, gate, w, put, *, nseg, n_ctx, tm):
    n = dho.shape[0]
    kw = _seg_kw(nseg, n_ctx, tm)
    dy, dgate, dgpost = _rowwise(tag + "_postb", functools.partial(_post_bwd_fn, 0.5), n, [dho, sv["y"]],
                                 [("full", gpost), ("seg", gate)], [(D_MODEL, BF16)], [D_MODEL, D_MODEL], tm=tm, **kw)
    tok = put("w_out", _mm_tn(sv["s"], dy, name=tag + "_dwout", tm=1408, tn=1024, col_blocks=1))
    dp = _mm_glu_bwd(dy, w["wout"], sv["a"], sv["b"], name=tag + "_ds")
    tok2 = put("w_in", _mm_tn(dp, sv["u"], name=tag + "_dwin", tm=1408, tn=1024, col_blocks=1))
    for t in (tok, tok2):
        if t is not None:
            gpre = gpre + t
    dh, dshift, dscale, dgpre = _mm_rows(dp, w["win_t"], _pre_bwd_fn, [sv["h"], dho],
                                         [("full", gpre), ("seg", scale)], [(D_MODEL, F32)],
                                         [D_MODEL, D_MODEL, D_MODEL], name=tag + "_du", tk=FFN_DIM, n_ctx=n_ctx)
    return dh, None, dict(shift=dshift, scale=dscale, gate=dgate, gpre=dgpre, gpost=dgpost)


def _local_step(x, ctx, target, mods, norm_g, get_w, small, put_grad):
    t_len, n_ctx = x.shape[0], ctx.shape[0]
    n0 = t_len + n_ctx
    tm0 = _pick(n_ctx, 256, 8)
    tm1 = _pick(t_len, 256, 8)
    ncc = n_ctx // CHUNK
    g = {}

    def modrow(i, k, nseg):
        mc, mx = mods[i]
        if nseg == 2:
            return jnp.stack([mc[k], mx[k]])[:, None, :]
        return mx[k][None, None, :]

    pending = [None]

    def gvec(i, k):
        v = norm_g[i, k][None, :]
        if pending[0] is not None:
            v = v + pending[0]
            pending[0] = None
        return v

    xc = jnp.concatenate([ctx, x], axis=0)
    L0 = dict(nseg=2, n_ctx=n_ctx, tm=tm0)
    wts = dict(get_w("ffn00", xc))
    h1, sv_f01 = _ffn_fwd("l0f1", xc, gvec(0, 0), gvec(0, 1), modrow(0, 0, 2), modrow(0, 1, 2), modrow(0, 2, 2),
                          wts["ffn00"], **L0)
    kw0 = _seg_kw(2, n_ctx, tm0)
    (um0,) = _rowwise("l0m_pre", _pre_fwd_fn, n0, [h1], [("full", gvec(0, 2)), ("seg", modrow(0, 3, 2)),
                                                         ("seg", modrow(0, 4, 2))], [(D_MODEL, BF16)], tm=tm0, **kw0)
    wts.update(get_w("ssd", um0))
    win_ssd = wts["ssd_win_t"]
    nh = SSD_HEADS
    dt_blk = (SSD_INNER + SSD_CONV_DIM) // (2 * nh)
    z = _mm(um0, win_ssd, out_dtype=F32, name="ssd_z", rhs_t=True, n=SSD_INNER)
    xbc_pre = _mm(um0, win_ssd, out_dtype=F32, name="ssd_xbc", rhs_t=True, n=SSD_CONV_DIM,
                  b_off=(SSD_INNER // 1024, 0))
    dtr = _mm(um0, win_ssd, out_dtype=F32, name="ssd_dt", rhs_t=True, n=2 * nh, b_off=(dt_blk, 0))
    cpre, xbc = _conv_fwd(xbc_pre, small["conv_w8"], small["conv_b"], n_ctx=n_ctx, name="ssd_conv")
    nh = SSD_HEADS
    dt_dir = [dtr[:, :nh], dtr[:, nh:2 * nh]]
    dtT_dir = [d.T for d in dt_dir]
    bias_r = [small["dt_bias"][d][None, :] for d in range(2)]
    bias_c = [small["dt_bias"][d][:, None] for d in range(2)]
    alog_r = [small["a_log"][d][None, :] for d in range(2)]
    alog_c = [small["a_log"][d][:, None] for d in range(2)]
    ys, hss = [], []
    for d in range(2):
        yd, hsd = _ssd_scan_fwd(xbc, dt_dir[d], dtT_dir[d], bias_r[d], bias_c[d], alog_r[d], alog_c[d],
                                rev=(d == 1), n_ctx_chunks=ncc, name=f"ssd_scan{d}")
        ys.append(yd)
        hss.append(hsd)
    dvec = jnp.repeat(small["ssd_d"], SSD_HEAD_DIM)[None, :]
    ngv = small["ssd_norm_g"][None, :]
    gate_rows = [ys[0], ys[1], (xbc, SSD_INNER, 0, 0), z]
    off = n_ctx // tm1
    lat = lambda r: (r[0], r[1], r[2], off) if isinstance(r, tuple) else (r, r.shape[1], 0, off)
    (yn,) = _rowwise("ssd_gate", _ssdgate_fwd_fn, t_len, [lat(r) for r in gate_rows],
                     [("full", dvec), ("full", ngv)], [(SSD_INNER, BF16)], tm=tm1)
    h1x = h1[n_ctx:]
    L1 = dict(nseg=1, n_ctx=0, tm=tm1)
    if "late" in wts:
        wts.update(wts.pop("late")(yn))
    yo0, h2 = _mm_rows(yn, wts["ssd_wout"], functools.partial(_out_post_fn, 1.0), [h1x],
                       [("full", gvec(0, 3)), ("seg", modrow(0, 5, 1))], [(D_MODEL, F32), (D_MODEL, F32)],
                       name="ssd_out", tk=SSD_INNER)
    wts.update(get_w("ffn01", h2))
    h3, sv_f02 = _ffn_fwd("l0f2", h2, gvec(0, 4), gvec(0, 5), modrow(0, 6, 1), modrow(0, 7, 1), modrow(0, 8, 1),
                          wts["ffn01"], **L1)

    wts.update(get_w("ffn10", h3))
    h4, sv_f11 = _ffn_fwd("l1f1", h3, gvec(1, 0), gvec(1, 1), modrow(1, 0, 1), modrow(1, 1, 1), modrow(1, 2, 1),
                          wts["ffn10"], **L1)
    (um1,) = _rowwise("l1m_pre", _pre_fwd_fn, t_len, [h4], [("full", gvec(1, 2)), ("seg", modrow(1, 3, 1)),
                                                            ("seg", modrow(1, 4, 1))], [(D_MODEL, BF16)], tm=tm1)
    wts.update(get_w("gm", um1))
    p1 = _mm(um1, wts["gm_win"], out_dtype=F32, name="gm_in", tm=2048)
    vg = small["gm_v_g"][None, :]
    vb = small["gm_v_b"][None, :]
    gu, gvn = _rowwise("gm_act", _gm_act_fwd_fn, t_len, [p1], [("full", vg), ("full", vb)],
                       [(GM_INNER, F32), (GM_INNER, BF16)], tm=256)
    ws_bf = small["gm_w_s"].astype(BF16)
    wst_bf = jnp.swapaxes(small["gm_w_s"], 1, 2).astype(BF16)
    bst = small["gm_b_s"].T
    tgm = _gm_spatial_fwd(gu, gvn, ws_bf, bst, name="gm_spatial")
    yo1, h5 = _mm_rows(tgm, wts["gm_wout"], functools.partial(_out_post_fn, 1.0), [h4],
                       [("full", gvec(1, 3)), ("seg", modrow(1, 5, 1))], [(D_MODEL, F32), (D_MODEL, F32)],
                       name="gm_out", tk=GM_INNER)
    wts.update(get_w("ffn11", h5))
    h6, sv_f12 = _ffn_fwd("l1f2", h5, gvec(1, 4), gvec(1, 5), modrow(1, 6, 1), modrow(1, 7, 1), modrow(1, 8, 1),
                          wts["ffn11"], **L1)

    dh, loss_parts = _rowwise("loss", _loss_fn, t_len, [h6, target], [], [(D_MODEL, F32)], [D_MODEL], tm=tm1)

    zero = jnp.zeros((D_MODEL,), F32)
    dmx = [[zero] * N_MOD for _ in range(2)]
    dmc = [[zero] * N_MOD for _ in range(2)]
    dng = [[zero] * 6 for _ in range(2)]

    def put_mod(i, k, acc):
        if acc.shape[0] == 2:
            dmc[i][k] = dmc[i][k] + acc[0, 0]
            dmx[i][k] = dmx[i][k] + acc[1, 0]
        else:
            dmx[i][k] = dmx[i][k] + acc[0, 0]

    def put_g(i, k, acc):
        dng[i][k] = dng[i][k] + jnp.sum(acc[:, 0], axis=0)

    def ffn_back(tag, i, j, dho, sv, w, lay):
        nseg = lay["nseg"]
        base = 0 if j == 0 else 6
        gi = 0 if j == 0 else 4
        dh_in, pending[0], s = _ffn_bwd(tag, dho, sv, gvec(i, gi), gvec(i, gi + 1), modrow(i, base + 1, nseg),
                                        modrow(i, base + 2, nseg), w, functools.partial(put_grad, f"ffn{i}{j}"), **lay)
        put_mod(i, base, s["shift"])
        put_mod(i, base + 1, s["scale"])
        put_mod(i, base + 2, s["gate"])
        put_g(i, gi, s["gpre"])
        put_g(i, gi + 1, s["gpost"])
        return dh_in

    dh = ffn_back("l1f2", 1, 1, dh, sv_f12, wts["ffn11"], L1)
    dyo, dgate, dgp = _rowwise("l1m_postb", functools.partial(_post_bwd_fn, 1.0), t_len, [dh, yo1],
                               [("full", gvec(1, 3)), ("seg", modrow(1, 5, 1))], [(D_MODEL, BF16)],
                               [D_MODEL, D_MODEL], tm=tm1)
    put_mod(1, 5, dgate)
    put_g(1, 3, dgp)
    put_grad("gm", "w_out", _mm_tn(tgm, dyo, name="gm_dwout", tn=1024, col_blocks=1))
    dtg = _mm(dyo, wts["gm_wout"], out_dtype=F32, name="gm_dt", rhs_t=True)
    dgu, dgvn, dws, dbst = _gm_spatial_bwd(dtg, gu, gvn, ws_bf, wst_bf, bst, name="gm_spatialb")
    g["gm_w_s"] = dws
    g["gm_b_s"] = dbst.T
    dp1, dvg, dvb = _rowwise("gm_actb", _gm_act_bwd_fn, t_len, [p1, dgu, dgvn], [("full", vg)],
                             [(2 * GM_INNER, BF16)], [GM_INNER, GM_INNER], tm=256)
    g["gm_v_g"] = dvg[0, 0]
    g["gm_v_b"] = dvb[0, 0]
    pending[0] = put_grad("gm", "w_in", _mm_tn(um1, dp1, name="gm_dwin", tm=1024, col_blocks=NDEV))
    dh, dsh, dsc, dgp = _mm_rows(dp1, wts["gm_win"], _pre_bwd_fn, [h4, dh],
                                 [("full", gvec(1, 2)), ("seg", modrow(1, 4, 1))], [(D_MODEL, F32)],
                                 [D_MODEL, D_MODEL, D_MODEL], name="gm_dum", tk=2048, rhs_t=True)
    put_mod(1, 3, dsh)
    put_mod(1, 4, dsc)
    put_g(1, 2, dgp)
    dh = ffn_back("l1f1", 1, 0, dh, sv_f11, wts["ffn10"], L1)

    dh = ffn_back("l0f2", 0, 1, dh, sv_f02, wts["ffn01"], L1)
    dyo, dgate, dgp = _rowwise("l0m_postb", functools.partial(_post_bwd_fn, 1.0), t_len, [dh, yo0],
                               [("full", gvec(0, 3)), ("seg", modrow(0, 5, 1))], [(D_MODEL, BF16)],
                               [D_MODEL, D_MODEL], tm=tm1)
    put_mod(0, 5, dgate)
    put_g(0, 3, dgp)
    tok = put_grad("ssd", "w_out", _mm_tn(yn, dyo, name="ssd_dwout", tn=1024, col_blocks=1))
    dyn = _mm(dyo, wts["ssd_wout"], out_dtype=F32, name="ssd_dyn", rhs_t=True)
    dy_ssd, dz, dngv, ddv = _rowwise("ssd_gateb", _ssdgate_bwd_fn, n0,
                                     [(dyn, SSD_INNER, 0, -(n_ctx // tm0))] + gate_rows,
                                     [("full", dvec), ("full", ngv if tok is None else ngv + tok)],
                                     [(SSD_INNER, F32), (SSD_INNER, BF16)],
                                     [SSD_INNER, SSD_INNER], tm=tm0)
    g["ssd_norm_g"] = dngv[0, 0]
    g["ssd_D"] = jnp.sum(ddv[0, 0].reshape(SSD_HEADS, SSD_HEAD_DIM), axis=1)
    dxbcs, ddts, dalogs, dbiases = [], [], [], []
    for d in range(2):
        dxd, ddtd, dal, dbi = _ssd_scan_bwd(dy_ssd, xbc, hss[d], dt_dir[d], dtT_dir[d], bias_r[d], bias_c[d],
                                            alog_r[d], alog_c[d], dvec, rev=(d == 1), n_ctx_chunks=ncc,
                                            direct=(d == 0), name=f"ssd_scanb{d}")
        dxbcs.append(dxd)
        ddts.append(ddtd)
        dalogs.append(dal[0])
        dbiases.append(dbi[0])
    g["ssd_A_log"] = jnp.stack(dalogs)
    g["ssd_dt_bias"] = jnp.stack(dbiases)
    dxbc_pre, dcw8, dcb = _conv_bwd(dxbcs[0], dxbcs[1], cpre, xbc_pre, small["conv_w8"], n_ctx=n_ctx, name="ssd_convb")
    g["ssd_conv_w"] = dcw8[:SSD_CONV]
    g["ssd_conv_b"] = dcb[0]
    ddt_bf = jnp.concatenate([ddts[0], ddts[1]], axis=1).astype(BF16)
    n_in = SSD_INNER + SSD_CONV_DIM + 2 * nh
    dw_t = _mm_tn(dz, um0, name="ssd_dwz", col_blocks=1, stack=(n_in, 0, None))
    dw_t = _mm_tn(dxbc_pre, um0, name="ssd_dwxbc", col_blocks=1, stack=(n_in, SSD_INNER, dw_t))
    dw_t = _mm_tn(ddt_bf, um0, name="ssd_dwdt", col_blocks=1, stack=(n_in, SSD_INNER + SSD_CONV_DIM, dw_t))
    pending[0] = put_grad("ssd", "w_in", dw_t)
    dum0 = _mm(dz, win_ssd, out_dtype=F32, name="ssd_dum_z", tk=SSD_INNER, n=D_MODEL)
    dum0 = _mm(dxbc_pre, win_ssd, out_dtype=F32, name="ssd_dum_x", tk=SSD_INNER, n=D_MODEL,
               b_off=(SSD_INNER // SSD_INNER, 0), add=dum0)
    dum0 = _mm(ddt_bf, win_ssd, out_dtype=F32, name="ssd_dum_dt", tk=2 * nh, n=D_MODEL, b_off=(dt_blk, 0), add=dum0)
    dh0, dsh, dsc, dgp = _rowwise("l0m_preb", _pre_bwd_fn, n0, [dum0, h1, (dh, D_MODEL, 0, -(n_ctx // tm0))],
                                  [("full", gvec(0, 2)), ("seg", modrow(0, 4, 2))], [(D_MODEL, F32)],
                                  [D_MODEL, D_MODEL, D_MODEL], tm=tm0, **kw0)
    put_mod(0, 3, dsh)
    put_mod(0, 4, dsc)
    put_g(0, 2, dgp)
    dh0 = ffn_back("l0f1", 0, 0, dh0, sv_f01, wts["ffn00"], L0)
    grad_x = dh0[n_ctx:]
    g["norm_g"] = jnp.stack([jnp.stack(r) for r in dng])
    g["dmx"] = jnp.stack([jnp.concatenate(r) for r in dmx])
    g["dmc"] = jnp.stack([jnp.concatenate(r) for r in dmc])
    return loss_parts[0], grad_x, g


GROUPS = ("ffn00", "ssd", "ffn01", "ffn10", "gm", "ffn11")


TRANSPOSED_IN = ("ffn", "ssd")


def _is_transposed(group):
    return group.startswith(TRANSPOSED_IN)


def _mats_in(group, win_l):
    if _is_transposed(group):
        return {("win_t" if group.startswith("ffn") else group + "_win_t"): win_l.reshape(-1, win_l.shape[2])}
    return {group + "_win": win_l}


def _mats_out(group, wout_l):
    pre = "" if group.startswith("ffn") else group + "_"
    return {pre + "wout": wout_l.reshape(-1, wout_l.shape[2])}


def _group_mats(group, lands):
    m = {**_mats_in(group, lands[0]), **_mats_out(group, lands[1])}
    return {group: m} if group.startswith("ffn") else m


def _grad_blocks(which, grad):
    if grad.ndim == 3:
        return grad if grad.shape[0] == NDEV else grad.reshape(NDEV, grad.shape[1] // NDEV, grad.shape[2])
    if which == "w_in":
        k, n = grad.shape
        return jnp.transpose(grad.reshape(k, NDEV, n // NDEV), (1, 0, 2)).astype(BF16)
    return grad.reshape(NDEV, grad.shape[0] // NDEV, grad.shape[1]).astype(BF16)


def kernel(x, c, ctx, c_ctx, ada_w, ada_b, norm_g, ffn_w_in, ffn_w_out, ssd_w_in, ssd_conv_w, ssd_conv_b, ssd_dt_bias, ssd_A_log, ssd_D, ssd_norm_g, ssd_w_out, gm_w_in, gm_v_g, gm_v_b, gm_w_s, gm_b_s, gm_w_out, loss_target, m_c_ctx, m_ada_w, m_ada_b, m_norm_g, m_ffn_w_in, m_ffn_w_out, m_ssd_w_in, m_ssd_conv_w, m_ssd_conv_b, m_ssd_dt_bias, m_ssd_A_log, m_ssd_D, m_ssd_norm_g, m_ssd_w_out, m_gm_w_in, m_gm_v_g, m_gm_v_b, m_gm_w_s, m_gm_b_s, m_gm_w_out, v_c_ctx, v_ada_w, v_ada_b, v_norm_g, v_ffn_w_in, v_ffn_w_out, v_ssd_w_in, v_ssd_conv_w, v_ssd_conv_b, v_ssd_dt_bias, v_ssd_A_log, v_ssd_D, v_ssd_norm_g, v_ssd_w_out, v_gm_w_in, v_gm_v_g, v_gm_v_b, v_gm_w_s, v_gm_b_s, v_gm_w_out):
    me = 4 * lax.axis_index("x") + 2 * lax.axis_index("y") + lax.axis_index("c")
    d = D_MODEL
    ncol = N_MOD * d // NDEV

    small_pack = jnp.concatenate([c.reshape(-1), norm_g.reshape(-1), ssd_conv_w.reshape(-1),
                                  gm_v_g.reshape(-1), gm_v_b.reshape(-1)])[None, :]
    (sp,), _ = _exchange([small_pack], scatter=False, name="gather_small")
    sp = sp[:, 0]
    o = 0
    c_all = sp[:, o:o + d]; o += d
    ng_all = sp[:, o:o + 2 * 6 * 128].reshape(NDEV, 2, 6, 128); o += 2 * 6 * 128
    cw_all = sp[:, o:o + SSD_CONV * 512].reshape(NDEV, SSD_CONV, 512); o += SSD_CONV * 512
    vg_all = sp[:, o:o + 256]; o += 256
    vb_all = sp[:, o:o + 256]; o += 256
    norm_g_full = jnp.transpose(ng_all, (1, 2, 0, 3)).reshape(2, 6, d)
    conv_w_full = jnp.transpose(cw_all, (1, 0, 2)).reshape(SSD_CONV, SSD_CONV_DIM)
    gm_v_g_full = vg_all.reshape(-1)
    gm_v_b_full = vb_all.reshape(-1)

    c16 = jnp.concatenate([c_all, jnp.broadcast_to(c_ctx[None, :], (NDEV, d))], axis=0)
    ada_b_loc = lax.dynamic_slice_in_dim(ada_b, me * ncol, ncol, axis=1)
    mods_loc = jnp.stack([_mm_f32(c16, ada_w[i], name=f"ada_mod{i}", silu_a=True, bias=ada_b_loc[i][None, :])
                          for i in range(2)])
    (mods_all,), mods_done = _exchange([mods_loc], scatter=False, name="gather_mods")

    tr = lambda a: jnp.swapaxes(a, -1, -2)
    shard = {"ssd": (tr(ssd_w_in)[0], ssd_w_out[0]), "gm": (gm_w_in[0], gm_w_out[0])}
    for i in range(2):
        for j in range(2):
            shard[f"ffn{i}{j}"] = (tr(ffn_w_in)[i, j], ffn_w_out[i, j])
    apart = GROUPS[:2]
    units = []
    for grp in GROUPS:
        units += [(grp + "_in", grp, (0,)), (grp + "_out", grp, (1,))] if grp in apart else [(grp, grp, (0, 1))]
    gathers = {}
    started = mods_done
    for unit, grp, idx in units:
        srcs = [(shard[grp][k] + started).astype(BF16) for k in idx]
        st = _exchange_start(srcs, [_landing(s, me) for s in srcs], scatter=False, name="gather_start_" + unit)
        gathers[unit] = st[:4]
        started = st[4]

    def fetch(unit, after):
        return _exchange_wait(*gathers[unit], after, scatter=False, name="gather_wait_" + unit)

    def get_w(grp, after):
        if grp not in apart:
            return _group_mats(grp, fetch(grp, after))
        early = lambda later: _mats_in(grp, fetch(grp + "_in", later)[0])
        late = lambda later: _mats_out(grp, fetch(grp + "_out", later)[0])
        if grp.startswith("ffn"):
            return {grp: dict(early=early, late=late)}
        return dict(early(after), late=late)

    scatters = {}
    held = {}

    def put_grad(grp, which, grad):
        if grp in apart:
            unit, blocks = grp + "_" + which[2:], [_grad_blocks(which, grad)]
        else:
            held[grp, which] = _grad_blocks(which, grad)
            if (grp, "w_in") not in held or (grp, "w_out") not in held:
                return None
            unit, blocks = grp, [held[grp, "w_in"], held[grp, "w_out"]]
        own = [lax.dynamic_index_in_dim(b, me, axis=0, keepdims=False) for b in blocks]
        st = _exchange_start(blocks, [_landing(o_, me) for o_ in own], scatter=True, name="scatter_start_" + unit)
        scatters[unit] = st[:4]
        return st[4]

    mods_rows = jnp.transpose(mods_all, (1, 2, 0, 3)).reshape(2, 2 * NDEV, N_MOD * d) + started
    mx = lax.dynamic_index_in_dim(mods_rows, me, axis=1, keepdims=False).reshape(2, N_MOD, d)
    mc = mods_rows[:, NDEV].reshape(2, N_MOD, d)
    mods = [(mc[i], mx[i]) for i in range(2)]

    small = dict(conv_w8=jnp.pad(conv_w_full, ((0, 8 - SSD_CONV), (0, 0))), conv_b=ssd_conv_b, dt_bias=ssd_dt_bias[0],
                 a_log=ssd_A_log[0], ssd_d=ssd_D[0], ssd_norm_g=ssd_norm_g[0], gm_v_g=gm_v_g_full,
                 gm_v_b=gm_v_b_full, gm_w_s=gm_w_s[0], gm_b_s=gm_b_s[0])
    loss_parts, grad_x, g = _local_step(x[0], ctx[0], loss_target[0], mods, norm_g_full, get_w, small, put_grad)
    g["loss"] = (0.5 / d * jnp.sum(loss_parts)).reshape(1)

    whole = {"ffn_w_in": (tr(ffn_w_in), tr(m_ffn_w_in), tr(v_ffn_w_in)), "ffn_w_out": (ffn_w_out, m_ffn_w_out, v_ffn_w_out),
             "ssd_w_in": (tr(ssd_w_in), tr(m_ssd_w_in), tr(v_ssd_w_in)), "ssd_w_out": (ssd_w_out, m_ssd_w_out, v_ssd_w_out),
             "gm_w_in": (gm_w_in, m_gm_w_in, v_gm_w_in), "gm_w_out": (gm_w_out, m_gm_w_out, v_gm_w_out)}
    res = {}

    def update_units(some, after):
        for unit, grp, idx in some:
            parts = _exchange_wait(*scatters[unit], after, scatter=True, name="scatter_wait_" + unit)
            for k, p in zip(idx, parts):
                which = ("in", "out")[k]
                nm = ("ffn" if grp.startswith("ffn") else grp) + "_w_" + which
                sel = (int(grp[3]), int(grp[4])) if grp.startswith("ffn") else (0,)
                res[nm] = _adamw(p, *whole[nm], name=f"adamw_{grp}_{which}", sel=sel, into=res.get(nm))
                after = res[nm][0]
        return after

    sg_names = ["dmx", "dmc", "norm_g", "ssd_conv_w", "ssd_conv_b", "ssd_dt_bias", "ssd_A_log", "ssd_D", "ssd_norm_g",
                "gm_v_g", "gm_v_b", "gm_w_s", "gm_b_s", "loss"]
    sg_shapes = [g[n].shape for n in sg_names]
    flat = jnp.concatenate([g[n].reshape(-1) for n in sg_names])
    npack = flat.shape[0]
    pad = (-npack) % 1024
    flat = jnp.pad(flat, (0, pad)).reshape(-1, 128)
    sg_start = _exchange_start([flat], [_landing(flat, me)], scatter=False, name="small_grads_start")
    by_send = list(reversed(units))
    update_units(by_send[:4], jnp.stack([sg_start[4], grad_x[0, 0]]))
    early_done = jnp.stack([res[nm][0].reshape(-1)[-1] for nm in sorted(res)])
    (sg_all,) = _exchange_wait(*sg_start[:4], early_done, scatter=False, name="small_grads_wait")
    sg_sum = _sum_slots(sg_all, name="sum_small_grads").reshape(-1)[:npack]
    update_units(by_send[4:], sg_sum)
    sums = {}
    o = 0
    for n, shp in zip(sg_names, sg_shapes):
        sz = math.prod(shp)
        sums[n] = sg_sum[o:o + sz].reshape(shp)
        o += sz
    loss = sums["loss"][0]
    per_dev = sg_all.reshape(NDEV, -1)
    dmx_all =per_dev[:, :2 * N_MOD * d].reshape(NDEV, 2, N_MOD * d)
    dmc_all = per_dev[:, 2 * N_MOD * d:4 * N_MOD * d].reshape(NDEV, 2, N_MOD * d)

    (s16,) = _rowwise("ada_silu", lambda cc: ((_silu(cc),), ()), 2 * NDEV, [c16], [], [(d, F32)], tm=2 * NDEV)
    s16_t = s16.T
    g_ada_w, dcc_parts = [], []
    for i in range(2):
        rhs = jnp.concatenate([lax.dynamic_slice_in_dim(dmx_all[:, i], me * ncol, ncol, axis=1),
                               lax.dynamic_slice_in_dim(dmc_all[:, i], me * ncol, ncol, axis=1)], axis=0)
        g_ada_w.append(_mm_f32(s16_t, rhs, name=f"ada_dw{i}"))
        dmc_loc = lax.dynamic_slice_in_dim(sums["dmc"][i], me * ncol, ncol, axis=0)
        rhs_c = jnp.zeros((ncol, 128), F32).at[:, 0].set(dmc_loc)
        dcc_parts.append(_mm_f32(ada_w[i], rhs_c, name=f"ada_dcc{i}")[:, 0])
    g_ada_w = jnp.stack(g_ada_w)
    dcc_part = (dcc_parts[0] + dcc_parts[1]).reshape(8, 128)
    (dcc_all,), _ = _exchange([dcc_part], scatter=False, name="gather_dcc")
    g_c_ctx = _sum_slots(dcc_all, name="sum_dcc", scale_by=c_ctx.reshape(8, 128)).reshape(d)
    g_ada_b = sums["dmx"] + sums["dmc"]

    outs = _adamw(g_ada_w.reshape(1, -1, ncol), ada_w.reshape(-1, ncol), m_ada_w.reshape(-1, ncol),
                  v_ada_w.reshape(-1, ncol), name="adamw_ada_w")
    res["ada_w"] = [o_.reshape(ada_w.shape) for o_ in outs]

    loc = lambda a, ax, n: lax.dynamic_slice_in_dim(a, me * n, n, axis=ax)
    small_g = dict(c_ctx=g_c_ctx, ada_b=g_ada_b, norm_g=loc(sums["norm_g"], 2, 128),
                   ssd_conv_w=loc(sums["ssd_conv_w"], 1, 512)[None], ssd_conv_b=sums["ssd_conv_b"][None],
                   ssd_dt_bias=sums["ssd_dt_bias"][None], ssd_A_log=sums["ssd_A_log"][None], ssd_D=sums["ssd_D"][None],
                   ssd_norm_g=sums["ssd_norm_g"][None], gm_v_g=loc(sums["gm_v_g"], 0, 256)[None],
                   gm_v_b=loc(sums["gm_v_b"], 0, 256)[None], gm_w_s=sums["gm_w_s"][None], gm_b_s=sums["gm_b_s"][None])
    small_w = dict(c_ctx=(c_ctx, m_c_ctx, v_c_ctx), ada_b=(ada_b, m_ada_b, v_ada_b), norm_g=(norm_g, m_norm_g, v_norm_g),
                   ssd_conv_w=(ssd_conv_w, m_ssd_conv_w, v_ssd_conv_w), ssd_conv_b=(ssd_conv_b, m_ssd_conv_b, v_ssd_conv_b),
                   ssd_dt_bias=(ssd_dt_bias, m_ssd_dt_bias, v_ssd_dt_bias), ssd_A_log=(ssd_A_log, m_ssd_A_log, v_ssd_A_log),
                   ssd_D=(ssd_D, m_ssd_D, v_ssd_D), ssd_norm_g=(ssd_norm_g, m_ssd_norm_g, v_ssd_norm_g),
                   gm_v_g=(gm_v_g, m_gm_v_g, v_gm_v_g), gm_v_b=(gm_v_b, m_gm_v_b, v_gm_v_b),
                   gm_w_s=(gm_w_s, m_gm_w_s, v_gm_w_s), gm_b_s=(gm_b_s, m_gm_b_s, v_gm_b_s))
    sn = list(small_w)

    def pack(arrs):
        f = jnp.concatenate([a.reshape(-1) for a in arrs])
        return jnp.pad(f, (0, (-f.shape[0]) % (256 * 128))).reshape(-1, 128)

    pg = pack([small_g[n].reshape(small_w[n][0].shape) for n in sn])
    outs = _adamw(pg[None], pack([small_w[n][0] for n in sn]), pack([small_w[n][1] for n in sn]),
                  pack([small_w[n][2] for n in sn]), name="adamw_small")
    flat_outs = [o_.reshape(-1) for o_ in outs]
    o = 0
    for n in sn:
        shp = small_w[n][0].shape
        sz = math.prod(shp)
        res[n] = [fo[o:o + sz].reshape(shp) for fo in flat_outs]
        o += sz

    order = ["c_ctx", "ada_w", "ada_b", "norm_g", "ffn_w_in", "ffn_w_out", "ssd_w_in", "ssd_conv_w", "ssd_conv_b",
             "ssd_dt_bias", "ssd_A_log", "ssd_D", "ssd_norm_g", "ssd_w_out", "gm_w_in", "gm_v_g", "gm_v_b", "gm_w_s",
             "gm_b_s", "gm_w_out"]
    for nm in ("ffn_w_in", "ssd_w_in"):
        res[nm] = [tr(a) for a in res[nm]]
    result = [loss, grad_x[None]]
    for k in range(4):
        result += [res[n][k] for n in order]
    return tuple(result)
```

```python
import functools
import math

import jax
import jax.numpy as jnp
from jax import lax
from jax.experimental import pallas as pl
from jax.experimental.pallas import tpu as pltpu

F32 = jnp.float32
BF16 = jnp.bfloat16

NDEV = 8
D_MODEL = 1024
FFN_DIM = 2816
N_MOD = 9
EPS = 1e-6
SSD_INNER = 2048
SSD_HEADS = 32
SSD_HEAD_DIM = 64
SSD_GROUPS = 8
SSD_HPG = 4
SSD_STATE = 128
SSD_CONV = 5
SSD_CONV_DIM = 4096
CHUNK = 128
GM_INNER = 2048
GM_GROUPS = 8
GM_GROUP_DIM = 256
ADAM_LR = 0.001
ADAM_B1 = 0.9
ADAM_B2 = 0.999
ADAM_EPS = 1e-08
ADAM_WD = 0.01
ADAM_STEP = 10
NEG_BIG = -1e30
VMEM_LIMIT_BYTES = 56 * 1024 * 1024
HI = lax.Precision.HIGHEST


def _params(*sem):
    return pltpu.CompilerParams(dimension_semantics=sem, vmem_limit_bytes=VMEM_LIMIT_BYTES)


def _pick(n, target, mult=16):
    if n <= target:
        return n
    for t in range(target - target % mult, 0, -mult):
        if n % t == 0:
            return t
    raise ValueError((n, target, mult))


def _sig(x):
    return 0.5 * jnp.tanh(0.5 * x) + 0.5


def _silu(x):
    return x * _sig(x)


def _dsilu(x):
    s = _sig(x)
    return s * (1.0 + x * (1.0 - s))


_GELU_C = math.sqrt(2.0 / math.pi)


def _gelu(x):
    return 0.5 * x * (1.0 + jnp.tanh(_GELU_C * (x + 0.044715 * x * x * x)))


def _gelu_and_grad(x):
    x2 = x * x
    t = jnp.tanh(_GELU_C * (x + 0.044715 * x2 * x))
    half = 0.5 * (1.0 + t)
    return x * half, half + 0.5 * x * (1.0 - t * t) * _GELU_C * (1.0 + 3.0 * 0.044715 * x2)


def _dgelu(x):
    return _gelu_and_grad(x)[1]


def _softplus(x):
    return jnp.maximum(x, 0.0) + jnp.log1p(jnp.exp(-jnp.abs(x)))


def _sum0(v):
    return jnp.sum(v, axis=0, keepdims=True)


def _rms(h):
    r = lax.rsqrt(jnp.mean(h * h, axis=-1, keepdims=True) + EPS)
    return h * r, r


def _dot(a, b, dims=((1,), (0,)), precision=None):
    return lax.dot_general(a, b, (dims, ((), ())), preferred_element_type=F32, precision=precision)


_NT = ((1,), (1,))
_TN = ((0,), (0,))


def _rowwise(name, fn, n_rows, rows, consts, outs, accs=(), *, tm, nseg=1, seg_blocks=0):
    assert n_rows % tm == 0
    if nseg == 2:
        assert seg_blocks > 0
        seg = lambda i: jnp.where(i < seg_blocks, 0, 1)
    else:
        seg = lambda i: 0
    in_specs, args, lacking = [], [], []
    for r in rows:
        arr, width, cb, off = r if isinstance(r, tuple) else (r, r.shape[1], 0, 0)
        in_specs.append(pl.BlockSpec((tm, width), lambda i, cb=cb, off=off: (jnp.maximum(i + off, 0), cb)))
        args.append(arr)
        lacking.append(-off if off < 0 else 0)
    for kind, arr in consts:
        if kind == "seg":
            assert arr.shape[0] == nseg and arr.shape[1] == 1, arr.shape
            in_specs.append(pl.BlockSpec((None, 1, arr.shape[2]), lambda i: (seg(i), 0, 0)))
        else:
            in_specs.append(pl.BlockSpec(arr.shape, lambda i: (0, 0)))
        args.append(arr)
    out_shape = [jax.ShapeDtypeStruct((n_rows, w), dt) for w, dt in outs]
    out_specs = [pl.BlockSpec((tm, w), lambda i: (i, 0)) for w, _ in outs]
    out_shape += [jax.ShapeDtypeStruct((nseg, 1, w), F32) for w in accs]
    out_specs += [pl.BlockSpec((None, 1, w), lambda i: (seg(i), 0, 0)) for w in accs]
    n_in, n_out, n_acc = len(args), len(outs), len(accs)

    def kern(*refs):
        i = pl.program_id(0)
        ins = [r[...] for r in refs[:n_in]]
        for k, lack in enumerate(lacking):
            if lack:
                ins[k] = jnp.where(i >= lack, ins[k], jnp.zeros_like(ins[k]))
        res, terms = fn(*ins)
        for ref, v in zip(refs[n_in:n_in + n_out], res):
            ref[...] = v.astype(ref.dtype)
        if n_acc:
            sums = [_sum0(v) for v in terms]
            first = (i == 0) | (i == seg_blocks) if nseg == 2 else (i == 0)
            acc_refs = refs[n_in + n_out:]

            @pl.when(first)
            def _():
                for ref, v in zip(acc_refs, sums):
                    ref[...] = v

            @pl.when(jnp.logical_not(first))
            def _():
                for ref, v in zip(acc_refs, sums):
                    ref[...] += v

    res = pl.pallas_call(
        kern, name=name, grid=(n_rows // tm,), in_specs=in_specs, out_specs=out_specs, out_shape=out_shape,
        compiler_params=_params("arbitrary"),
    )(*args)
    return res


def _pre_fwd_fn(h, g, shift, scale):
    hh, _ = _rms(h)
    return (hh * g * (1.0 + scale) + shift,), ()


def _pre_bwd_fn(du, h, dres, g, scale):
    hh, r = _rms(h)
    n = hh * g
    dn = du * (1.0 + scale)
    dhh = dn * g
    dh = dres + r * (dhh - hh * jnp.mean(dhh * hh, axis=-1, keepdims=True))
    return (dh,), (du, du * n, dn * hh)


def _post_fwd_fn(weight, h, y, g, gate):
    yh, _ = _rms(y)
    return (h + weight * gate * (yh * g),), ()


def _out_post_fn(weight, y, h, g, gate):
    return (y,) + _post_fwd_fn(weight, h, y, g, gate)[0], ()


def _post_bwd_fn(weight, dh, y, g, gate):
    yh, r = _rms(y)
    dr = dh * weight
    dyh = dr * gate * g
    dy = r * (dyh - yh * jnp.mean(dyh * yh, axis=-1, keepdims=True))
    return (dy,), (dr * yh * g, dr * gate * yh)


def _glu_bwd_fn(ds, a, b):
    a = a.astype(F32)
    b = b.astype(F32)
    sg = _sig(a)
    da = ds * b * (sg * (1.0 + a * (1.0 - sg)))
    db = ds * (a * sg)
    return (jnp.concatenate([da, db], axis=1),), ()


def _loss_fn(y, t):
    diff = y - t
    return (diff * (1.0 / D_MODEL),), (diff * diff,)


def _ssd_y(yf, yb, xs, z, dvec):
    y = yf + yb + dvec * xs
    return y, y * _silu(z)


def _ssdgate_fwd_fn(yf, yb, xs, z, dvec, ng):
    _, yg = _ssd_y(yf, yb, xs, z, dvec)
    parts = []
    for g in range(SSD_GROUPS):
        sl = slice(g * 256, (g + 1) * 256)
        parts.append(_rms(yg[:, sl])[0])
    return (jnp.concatenate(parts, axis=1) * ng,), ()


def _ssdgate_bwd_fn(dyn, yf, yb, xs, z, dvec, ng):
    y, yg = _ssd_y(yf, yb, xs, z, dvec)
    dyg_parts, ygh_parts = [], []
    for g in range(SSD_GROUPS):
        sl = slice(g * 256, (g + 1) * 256)
        ygh, r = _rms(yg[:, sl])
        d = dyn[:, sl] * ng[:, sl]
        dyg_parts.append(r * (d - ygh * jnp.mean(d * ygh, axis=-1, keepdims=True)))
        ygh_parts.append(ygh)
    dyg = jnp.concatenate(dyg_parts, axis=1)
    ygh = jnp.concatenate(ygh_parts, axis=1)
    dy = dyg * _silu(z)
    dz = dyg * y * _dsilu(z)
    return (dy, dz), (dyn * ygh, dy * xs)


def _ln_stats(v):
    mu = jnp.mean(v, axis=-1, keepdims=True)
    vc = v - mu
    r = lax.rsqrt(jnp.mean(vc * vc, axis=-1, keepdims=True) + EPS)
    return vc * r, r


def _gm_act_fwd_fn(p, vg, vb):
    gu = _gelu(p[:, :GM_INNER])
    gvh, _ = _ln_stats(_gelu(p[:, GM_INNER:]))
    return (gu, gvh * vg + vb), ()


def _gm_act_bwd_fn(p, dgu, dgvn, vg):
    pu = p[:, :GM_INNER]
    pv = p[:, GM_INNER:]
    gv, dgelu_v = _gelu_and_grad(pv)
    gvh, r = _ln_stats(gv)
    dgvh = dgvn * vg
    dgv = r * (dgvh - jnp.mean(dgvh, axis=-1, keepdims=True) - gvh * jnp.mean(dgvh * gvh, axis=-1, keepdims=True))
    dp = jnp.concatenate([dgu * _dgelu(pu), dgv * dgelu_v], axis=1)
    return (dp,), (dgvn * gvh, dgvn)


def _mm(a, b, *, out_dtype, name, tm=1088, tn=1024, tk=1408, add=None, rhs_t=False, n=None, b_off=(0, 0)):
    m, k = a.shape
    col_blocked = b.ndim == 3
    if col_blocked:
        assert not rhs_t and n is None and b.shape[1] == k
        n, tn = b.shape[0] * b.shape[2], b.shape[2]
    elif n is None:
        n, k2 = b.shape if rhs_t else b.shape[::-1]
        assert k == k2
    tm, tn, tk = _pick(m, tm), _pick(n, tn, 128), _pick(k, tk, 128)
    o0, o1 = b_off
    nk = k // tk
    dims = _NT if rhs_t else ((1,), (0,))

    def kern(*refs):
        a_ref, b_ref = refs[:2]
        add_ref = refs[2] if add is not None else None
        o_ref = refs[3] if add is not None else refs[2]

        def finish(r):
            if add is not None:
                r = r + add_ref[...]
            o_ref[...] = r.astype(o_ref.dtype)

        p = _dot(a_ref[...], b_ref[...], dims)
        if nk == 1:
            finish(p)
            return
        acc_ref = refs[-1]
        kk = pl.program_id(2)

        @pl.when(kk == 0)
        def _():
            acc_ref[...] = p

        @pl.when((kk > 0) & (kk < nk - 1))
        def _():
            acc_ref[...] += p

        @pl.when(kk == nk - 1)
        def _():
            finish(acc_ref[...] + p)

    if col_blocked:
        b_spec = pl.BlockSpec((None, tk, tn), lambda i, j, kk: (j, kk, 0))
    elif rhs_t:
        b_spec = pl.BlockSpec((tn, tk), lambda i, j, kk: (j + o0, kk + o1))
    else:
        b_spec = pl.BlockSpec((tk, tn), lambda i, j, kk: (kk + o0, j + o1))
    in_specs = [pl.BlockSpec((tm, tk), lambda i, j, kk: (i, kk)), b_spec]
    args = [a, b]
    if add is not None:
        in_specs.append(pl.BlockSpec((tm, tn), lambda i, j, kk: (i, j)))
        args.append(add)
    return pl.pallas_call(
        kern, name=name, grid=(m // tm, n // tn, nk), in_specs=in_specs,
        out_specs=pl.BlockSpec((tm, tn), lambda i, j, kk: (i, j)),
        out_shape=jax.ShapeDtypeStruct((m, n), out_dtype),
        scratch_shapes=[pltpu.VMEM((tm, tn), F32)] if nk > 1 else [],
        compiler_params=_params("parallel", "parallel", "arbitrary"),
    )(*args)


def _mm_rows(a, b, fn, rows, consts, outs, accs=(), *, name, tm=544, tk=1408, rhs_t=False, n_ctx=0):
    halves = a.ndim == 3
    m, k = (a.shape[1], 2 * a.shape[2]) if halves else a.shape
    col_blocked = b.ndim == 3
    kb, nb = 1, None
    if col_blocked:
        assert rhs_t and b.shape[0] * b.shape[2] == k
        n, nb = b.shape[1], b.shape[2]
        kb = max(1, tk // nb)
        assert b.shape[0] % kb == 0
        tk = kb * nb
    else:
        n = b.shape[0] if rhs_t else b.shape[1]
    tm, tk = _pick(m, tm), _pick(k, tk, 128)
    nk = k // tk
    if halves:
        hb = k // 2 // tk
        a_spec = pl.BlockSpec((None, tm, tk), lambda i, kk: (kk // hb, i, kk % hb))
    else:
        a_spec = pl.BlockSpec((tm, tk), lambda i, kk: (i, kk))
    dims = _NT if rhs_t else ((1,), (0,))
    n_rows, n_const, n_out, n_acc = len(rows), len(consts), len(outs), len(accs)

    def kern(*refs):
        a_ref, b_ref = refs[:2]
        row_refs = refs[2:2 + n_rows]
        const_refs = refs[2 + n_rows:2 + n_rows + n_const]
        out_refs = refs[2 + n_rows + n_const:2 + n_rows + n_const + n_out]
        acc_refs = refs[2 + n_rows + n_const + n_out:2 + n_rows + n_const + n_out + n_acc]
        i, kk = pl.program_id(0), pl.program_id(1)

        def finish(p, rs=slice(None), r0=0):
            nr = p.shape[0]
            is_ctx = (i * tm + r0 + lax.broadcasted_iota(jnp.int32, (nr, 1), 0)) < n_ctx
            cvals = []
            for (kind, arr), ref in zip(consts, const_refs):
                if kind == "seg":
                    cvals.append(jnp.where(is_ctx, ref[0], ref[1]) if arr.shape[0] == 2 else ref[0])
                else:
                    cvals.append(ref[...])
            res, terms = fn(p, *[r[rs, :] for r in row_refs], *cvals)
            for ref, v in zip(out_refs, res):
                ref[rs, :] = v.astype(ref.dtype)
            for ref, v in zip(acc_refs, terms):
                s_all = _sum0(v)
                s_ctx = _sum0(jnp.where(is_ctx, v, 0.0)) if n_ctx else jnp.zeros_like(s_all)
                both = jnp.concatenate([s_ctx, s_all - s_ctx], axis=0)[:, None, :]

                @pl.when(i == 0)
                def _():
                    ref[...] = both

                @pl.when(i > 0)
                def _():
                    ref[...] += both

        if nk == 1 and n_acc == 0:
            nsub = 2 if tm % 32 == 0 else 1
            sub = tm // nsub
            for r in range(nsub):
                rs = slice(r * sub, (r + 1) * sub)
                finish(_dot(a_ref[rs, :], b_ref[...], dims), rs, r * sub)
            return
        if col_blocked:
            p = sum(_dot(a_ref[:, c * nb:(c + 1) * nb], b_ref[c], dims) for c in range(kb))
        else:
            p = _dot(a_ref[...], b_ref[...], dims)
        if nk == 1:
            finish(p)
            return
        scr = refs[-1]

        @pl.when(kk == 0)
        def _():
            scr[...] = p

        @pl.when((kk > 0) & (kk < nk - 1))
        def _():
            scr[...] += p

        @pl.when(kk == nk - 1)
        def _():
            finish(scr[...] + p)

    if col_blocked:
        b_spec = pl.BlockSpec((kb, n, nb), lambda i, kk: (kk, 0, 0))
    elif rhs_t:
        b_spec = pl.BlockSpec((n, tk), lambda i, kk: (0, kk))
    else:
        b_spec = pl.BlockSpec((tk, n), lambda i, kk: (kk, 0))
    in_specs = [a_spec, b_spec]
    in_specs += [pl.BlockSpec((tm, r.shape[1]), lambda i, kk: (i, 0)) for r in rows]
    for kind, arr in consts:
        in_specs.append(pl.BlockSpec(arr.shape, (lambda i, kk: (0, 0, 0)) if kind == "seg" else (lambda i, kk: (0, 0))))
    out_shape = [jax.ShapeDtypeStruct((m, w), dt) for w, dt in outs]
    out_specs = [pl.BlockSpec((tm, w), lambda i, kk: (i, 0)) for w, _ in outs]
    out_shape += [jax.ShapeDtypeStruct((2, 1, w), F32) for w in accs]
    out_specs += [pl.BlockSpec((2, 1, w), lambda i, kk: (0, 0, 0)) for w in accs]
    return pl.pallas_call(
        kern, name=name, grid=(m // tm, nk), in_specs=in_specs, out_specs=out_specs, out_shape=out_shape,
        scratch_shapes=[pltpu.VMEM((tm, n), F32)] if nk > 1 else [],
        compiler_params=_params("arbitrary", "arbitrary"),
    )(a, b, *rows, *[arr for _, arr in consts])


def _mm_glu(u, win_t, *, name, tm=1088, tn=1408):
    m, k = u.shape
    n = win_t.shape[0] // 2
    tm, tn = _pick(m, tm), _pick(n, tn, 128)
    nj = n // tn

    nsub = 2 if tm % 32 == 0 else 1
    sub = tm // nsub

    def kern(u_ref, wa_ref, wb_ref, s_ref, a_ref, b_ref):
        for r in range(nsub):
            rows = slice(r * sub, (r + 1) * sub)
            uu = u_ref[rows, :]
            a = _dot(uu, wa_ref[...], _NT)
            b = _dot(uu, wb_ref[...], _NT)
            s_ref[rows, :] = (_silu(a) * b).astype(BF16)
            a_ref[rows, :] = a.astype(BF16)
            b_ref[rows, :] = b.astype(BF16)

    ospec = pl.BlockSpec((tm, tn), lambda i, j: (i, j))
    return pl.pallas_call(
        kern, name=name, grid=(m // tm, nj),
        in_specs=[pl.BlockSpec((tm, k), lambda i, j: (i, 0)), pl.BlockSpec((tn, k), lambda i, j: (j, 0)),
                  pl.BlockSpec((tn, k), lambda i, j: (nj + j, 0))],
        out_specs=[ospec, ospec, ospec],
        out_shape=[jax.ShapeDtypeStruct((m, n), BF16)] * 3,
        compiler_params=_params("parallel", "parallel"),
    )(u, win_t, win_t)


def _mm_glu_bwd(dy, wout, a, b, *, name, tm=544, tn=1408):
    m, k = dy.shape
    f = wout.shape[0]
    tm, tn = _pick(m, tm), _pick(f, tn, 128)
    nsub = 2 if tm % 32 == 0 else 1
    sub = tm // nsub

    def kern(dy_ref, w_ref, a_ref, b_ref, o_ref):
        for r in range(nsub):
            rs = slice(r * sub, (r + 1) * sub)
            ds = _dot(dy_ref[rs, :], w_ref[...], _NT)
            (dp,), _ = _glu_bwd_fn(ds, a_ref[rs, :], b_ref[rs, :])
            o_ref[0, rs, :] = dp[:, :tn].astype(BF16)
            o_ref[1, rs, :] = dp[:, tn:].astype(BF16)

    tile = pl.BlockSpec((tm, tn), lambda i, j: (i, j))
    return pl.pallas_call(
        kern, name=name, grid=(m // tm, f // tn),
        in_specs=[pl.BlockSpec((tm, k), lambda i, j: (i, 0)), pl.BlockSpec((tn, k), lambda i, j: (j, 0)), tile, tile],
        out_specs=pl.BlockSpec((2, tm, tn), lambda i, j: (0, i, j)),
        out_shape=jax.ShapeDtypeStruct((2, m, f), BF16),
        compiler_params=_params("parallel", "parallel"),
    )(dy, wout, a, b)


def _mm_tn(a, b, *, name, tm=1024, tn=1024, tk=2176, col_blocks=None, stack=None):
    extra, extra_specs, aliases = [], [], {}
    halves = a.ndim == 3
    t, m = (a.shape[1], 2 * a.shape[2]) if halves else a.shape
    t2, n = b.shape
    assert t == t2
    tm, tn, tk = _pick(m, tm, 128), _pick(n, tn, 128), _pick(t, tk)
    nk = t // tk
    if halves:
        hb = m // 2 // tm
        a_spec = pl.BlockSpec((None, tk, tm), lambda i, j, kk: (i // hb, kk, i % hb))
    else:
        a_spec = pl.BlockSpec((tk, tm), lambda i, j, kk: (kk, i))
    if col_blocks is None:
        def kern(a_ref, b_ref, o_ref):
            kk = pl.program_id(2)

            @pl.when(kk == 0)
            def _():
                o_ref[...] = jnp.zeros_like(o_ref)

            o_ref[...] += _dot(a_ref[...], b_ref[...], _TN)

        out_spec = pl.BlockSpec((tm, tn), lambda i, j, kk: (i, j))
        out_shape = jax.ShapeDtypeStruct((m, n), F32)
        scratch = []
    else:
        wb = n // col_blocks
        per = tn // wb
        assert tn % wb == 0 and wb % 8 == 0

        def kern(a_ref, b_ref, *rest):
            o_ref, acc_ref = rest[-2:]
            kk = pl.program_id(2)
            p = _dot(a_ref[...], b_ref[...], _TN)

            @pl.when(kk == 0)
            def _():
                acc_ref[...] = p

            @pl.when((kk > 0) & (kk < nk - 1))
            def _():
                acc_ref[...] += p

            @pl.when(kk == nk - 1)
            def _():
                r = acc_ref[...] + p if nk > 1 else p
                for c in range(per):
                    o_ref[c] = r[:, c * wb:(c + 1) * wb].astype(BF16)

        rows_total, row0, into = stack if stack is not None else (m, 0, None)
        assert row0 % tm == 0
        out_spec = pl.BlockSpec((per, tm, wb), lambda i, j, kk: (j, i + row0 // tm, 0))
        out_shape = jax.ShapeDtypeStruct((col_blocks, rows_total, wb), BF16)
        scratch = [pltpu.VMEM((tm, tn), F32)]
        if into is not None:
            extra, extra_specs, aliases = [into], [pl.BlockSpec(memory_space=pl.ANY)], {2: 0}

    return pl.pallas_call(
        kern, name=name, grid=(m // tm, n // tn, nk),
        in_specs=[a_spec, pl.BlockSpec((tk, tn), lambda i, j, kk: (kk, j))] + extra_specs,
        out_specs=out_spec, out_shape=out_shape, scratch_shapes=scratch, input_output_aliases=aliases,
        compiler_params=_params("parallel", "parallel", "arbitrary"),
    )(a, b, *extra)


def _mm_f32(a, b, *, name, silu_a=False, bias=None):
    m, k = a.shape
    n = b.shape[1]

    def kern(*refs):
        if bias is None:
            a_ref, b_ref, o_ref = refs
        else:
            a_ref, b_ref, bias_ref, o_ref = refs
        av = a_ref[...]
        if silu_a:
            av = _silu(av)
        r = jnp.dot(av, b_ref[...], preferred_element_type=F32, precision=HI)
        if bias is not None:
            r = r + bias_ref[...]
        o_ref[...] = r

    args = [a, b] + ([] if bias is None else [bias])
    return pl.pallas_call(kern, name=name, out_shape=jax.ShapeDtypeStruct((m, n), F32),
                          compiler_params=pltpu.CompilerParams(vmem_limit_bytes=VMEM_LIMIT_BYTES))(*args)


CONV_WIN = 32


def _conv_windows(n, n_ctx):
    assert n_ctx % CONV_WIN == 0 and n_ctx >= CONV_WIN and n - n_ctx >= CONV_WIN
    return (0, n_ctx - CONV_WIN // 2, n - CONV_WIN)


def _tap_outside(r0, s, n, n_ctx):
    t = r0 + lax.broadcasted_iota(jnp.int32, (CONV_WIN, 1), 0)
    lo = jnp.where(t < n_ctx, 0, n_ctx)
    hi = jnp.where(t < n_ctx, n_ctx, n)
    return jnp.where((t + s >= lo) & (t + s < hi), 0.0, 1.0)


def _rolled(v, s):
    return v if s == 0 else pltpu.roll(v, (-s) % v.shape[0], 0)


def _conv_fwd(xp, w8, b, *, n_ctx, name, cb=256):
    n, c = xp.shape
    half = SSD_CONV // 2

    def kern(x_ref, w_ref, b_ref, cpre_ref, act_ref):
        x = x_ref[...]
        acc = jnp.zeros_like(x) + b_ref[...]
        rolled = {}
        for k in range(SSD_CONV):
            rolled[k] = _rolled(x, k - half)
            acc = acc + rolled[k] * w_ref[k:k + 1, :]
        cpre_ref[...] = acc
        act_ref[...] = _silu(acc)
        for r0 in _conv_windows(n, n_ctx):
            rows = slice(r0, r0 + CONV_WIN)
            fix = acc[rows]
            for k in range(SSD_CONV):
                if k != half:
                    fix = fix - rolled[k][rows] * w_ref[k:k + 1, :] * _tap_outside(r0, k - half, n, n_ctx)
            cpre_ref[rows, :] = fix
            act_ref[rows, :] = _silu(fix)

    spec = pl.BlockSpec((n, cb), lambda j: (0, j))
    return pl.pallas_call(
        kern, name=name, grid=(c // cb,),
        in_specs=[spec, pl.BlockSpec((8, cb), lambda j: (0, j)), pl.BlockSpec((1, cb), lambda j: (0, j))],
        out_specs=[spec, spec], out_shape=[jax.ShapeDtypeStruct((n, c), F32)] * 2,
        compiler_params=_params("parallel"),
    )(xp, w8, b)


def _conv_bwd(d1, d2, cpre, xp, w8, *, n_ctx, name, cb=128):
    n, c = xp.shape
    half = SSD_CONV // 2

    def kern(d1_ref, d2_ref, cpre_ref, x_ref, w_ref, dx_ref, dw_ref, db_ref):
        g = (d1_ref[...] + d2_ref[...]) * _dsilu(cpre_ref[...])
        x = x_ref[...]
        dx = jnp.zeros_like(g)
        dw_ref[...] = jnp.zeros_like(dw_ref)
        g_rolled = {}
        for k in range(SSD_CONV):
            s = k - half
            g_rolled[k] = _rolled(g, -s)
            dx = dx + g_rolled[k] * w_ref[k:k + 1, :]
            xr = _rolled(x, s)
            dw = _sum0(g * xr)
            if s != 0:
                for r0 in _conv_windows(n, n_ctx):
                    rows = slice(r0, r0 + CONV_WIN)
                    dw = dw - _sum0(g[rows] * xr[rows] * _tap_outside(r0, s, n, n_ctx))
            dw_ref[k:k + 1, :] = dw
        dx_ref[...] = dx.astype(BF16)
        for r0 in _conv_windows(n, n_ctx):
            rows = slice(r0, r0 + CONV_WIN)
            fix = dx[rows]
            for k in range(SSD_CONV):
                if k != half:
                    fix = fix - g_rolled[k][rows] * w_ref[k:k + 1, :] * _tap_outside(r0, half - k, n, n_ctx)
            dx_ref[rows, :] = fix.astype(BF16)
        db_ref[...] = _sum0(g)

    spec = pl.BlockSpec((n, cb), lambda j: (0, j))
    return pl.pallas_call(
        kern, name=name, grid=(c // cb,),
        in_specs=[spec, spec, spec, spec, pl.BlockSpec((8, cb), lambda j: (0, j))],
        out_specs=[spec, pl.BlockSpec((8, cb), lambda j: (0, j)), pl.BlockSpec((1, cb), lambda j: (0, j))],
        out_shape=[jax.ShapeDtypeStruct((n, c), BF16), jax.ShapeDtypeStruct((8, c), F32),
                   jax.ShapeDtypeStruct((1, c), F32)],
        compiler_params=_params("parallel"),
    )(d1, d2, cpre, xp, w8)


def _chunk_of(s, nc, n_ctx_chunks, rev):
    if not rev:
        return s
    return jnp.where(s < n_ctx_chunks, n_ctx_chunks - 1 - s, nc - 1 - (s - n_ctx_chunks))


def _scan_common(dt_raw, dtT_raw, bias_r, bias_c, alog_r, alog_c, rev):
    ii = lax.broadcasted_iota(jnp.int32, (CHUNK, CHUNK), 0)
    jj = lax.broadcasted_iota(jnp.int32, (CHUNK, CHUNK), 1)
    tri = (jj >= ii) if rev else (jj <= ii)
    tri_t = (ii >= jj) if rev else (ii <= jj)
    a_r = -jnp.exp(alog_r)
    a_c = -jnp.exp(alog_c)
    dt = _softplus(dt_raw + bias_r)
    dt_t = _softplus(dtT_raw + bias_c)
    al = dt * a_r
    acum = _dot(tri.astype(F32), al, precision=HI)
    acum_t = _dot(dt_t * a_c, tri_t.astype(F32), precision=HI)
    atot = _sum0(al)
    return tri, tri_t, a_r, dt, acum, acum_t, atot


def _head_spread():
    return jnp.repeat(jnp.eye(SSD_HEADS, dtype=BF16), SSD_HEAD_DIM, axis=1)


def _dot_sel(v, sel):
    hi = v.astype(BF16)
    lo = (v - hi.astype(F32)).astype(BF16)
    return _dot(hi, sel) + _dot(lo, sel)


def _ssd_scan_fwd(xbc, dt_raw, dtT_raw, bias_r, bias_c, alog_r, alog_c, *, rev, n_ctx_chunks, name):
    n = xbc.shape[0]
    nc = n // CHUNK
    cidx = functools.partial(_chunk_of, nc=nc, n_ctx_chunks=n_ctx_chunks, rev=rev)

    def kern(xs_ref, b_ref, c_ref, dt_ref, dtT_ref, br_ref, bc_ref, ar_ref, ac_ref, e_ref, y_ref, hs_ref, h_scr):
        @pl.when(pl.program_id(0) == 0)
        def _():
            h_scr[...] = jnp.zeros_like(h_scr)

        tri, _, _, dt, acum, acum_t, atot = _scan_common(
            dt_ref[...], dtT_ref[...], br_ref[...], bc_ref[...], ar_ref[...], ac_ref[...], rev)
        etot = jnp.exp(atot)
        spread = lambda v: _dot_sel(v, e_ref[...])
        xdt_all = xs_ref[...] * spread(dt)
        eax = spread(jnp.exp(acum))
        xdw_all = xdt_all * spread(jnp.exp(atot - acum))
        hs_ref[...] = h_scr[...]
        for g in range(SSD_GROUPS):
            gs = slice(g * 256, (g + 1) * 256)
            bg = b_ref[:, g * SSD_STATE:(g + 1) * SSD_STATE].astype(BF16)
            cg = c_ref[:, g * SSD_STATE:(g + 1) * SSD_STATE].astype(BF16)
            cb = _dot(cg, bg, _NT)
            h4 = h_scr[gs, :]
            ys = []
            for k in range(SSD_HPG):
                h = g * SSD_HPG + k
                lmat = jnp.exp(jnp.where(tri, acum[:, h:h + 1] - acum_t[h:h + 1, :], NEG_BIG))
                xdt_h = xdt_all[:, h * SSD_HEAD_DIM:(h + 1) * SSD_HEAD_DIM].astype(BF16)
                ys.append(_dot((cb * lmat).astype(BF16), xdt_h))
            y_ref[:, gs] = jnp.concatenate(ys, axis=1) + _dot(cg, h4.astype(BF16), _NT) * eax[:, gs]
            s4 = _dot(xdw_all[:, gs].astype(BF16), bg, _TN)
            for k in range(SSD_HPG):
                h = g * SSD_HPG + k
                rs = slice(h * SSD_HEAD_DIM, (h + 1) * SSD_HEAD_DIM)
                h_scr[rs, :] = h4[k * SSD_HEAD_DIM:(k + 1) * SSD_HEAD_DIM] * etot[:, h:h + 1] + \
                    s4[k * SSD_HEAD_DIM:(k + 1) * SSD_HEAD_DIM]

    nh = SSD_HEADS
    small = lambda shape: pl.BlockSpec(shape, lambda s: (0, 0))
    return pl.pallas_call(
        kern, name=name, grid=(nc,),
        in_specs=[pl.BlockSpec((CHUNK, SSD_INNER), lambda s: (cidx(s), 0)),
                  pl.BlockSpec((CHUNK, 1024), lambda s: (cidx(s), 2)),
                  pl.BlockSpec((CHUNK, 1024), lambda s: (cidx(s), 3)),
                  pl.BlockSpec((CHUNK, nh), lambda s: (cidx(s), 0)),
                  pl.BlockSpec((nh, CHUNK), lambda s: (0, cidx(s))),
                  small((1, nh)), small((nh, 1)), small((1, nh)), small((nh, 1)), small((nh, SSD_INNER))],
        out_specs=[pl.BlockSpec((CHUNK, SSD_INNER), lambda s: (cidx(s), 0)),
                   pl.BlockSpec((None, SSD_INNER, SSD_STATE), lambda s: (s, 0, 0))],
        out_shape=[jax.ShapeDtypeStruct((n, SSD_INNER), F32),
                   jax.ShapeDtypeStruct((nc, SSD_INNER, SSD_STATE), F32)],
        scratch_shapes=[pltpu.VMEM((SSD_INNER, SSD_STATE), F32)],
        compiler_params=_params("arbitrary"),
    )(xbc, xbc, xbc, dt_raw, dtT_raw, bias_r, bias_c, alog_r, alog_c, _head_spread())


def _ssd_scan_bwd(dy, xbc, hs, dt_raw, dtT_raw, bias_r, bias_c, alog_r, alog_c, dvec, *, rev, n_ctx_chunks,
                  direct, name):
    n = xbc.shape[0]
    nc = n // CHUNK
    nh = SSD_HEADS
    step_of = lambda r: nc - 1 - r
    cidx = lambda r: _chunk_of(step_of(r), nc, n_ctx_chunks, rev)

    def kern(dy_ref, xs_ref, b_ref, c_ref, hs_ref, dt_ref, dtT_ref, br_ref, bc_ref, ar_ref, ac_ref, dv_ref,
             e_ref, et_ref, dx_ref, ddt_ref, dal_ref, dbias_ref, dh_scr):
        @pl.when(pl.program_id(0) == 0)
        def _():
            dh_scr[...] = jnp.zeros_like(dh_scr)
            dal_ref[...] = jnp.zeros_like(dal_ref)
            dbias_ref[...] = jnp.zeros_like(dbias_ref)

        tri, tri_t, a_r, dt, acum, acum_t, atot = _scan_common(
            dt_ref[...], dtT_ref[...], br_ref[...], bc_ref[...], ar_ref[...], ac_ref[...], rev)
        etot = jnp.exp(atot)
        spread = lambda v: _dot_sel(v, e_ref[...])
        gather = lambda v: _dot_sel(v, et_ref[...])
        xs_all = xs_ref[...]
        dy_all = dy_ref[...]
        dtx = spread(dt)
        eax = spread(jnp.exp(acum))
        decx = spread(jnp.exp(atot - acum))
        xdt_all = xs_all * dtx
        xdw_all = xdt_all * decx
        dyo_all = dy_all * eax
        lane = lax.broadcasted_iota(jnp.int32, (CHUNK, nh), 1)
        lane1 = lax.broadcasted_iota(jnp.int32, (1, nh), 1)
        sub = lax.broadcasted_iota(jnp.int32, (nh, CHUNK), 0)
        g_rows = jnp.zeros((CHUNK, nh), F32)
        g_cols = jnp.zeros((nh, CHUNK), F32)
        dtot = jnp.zeros((1, nh), F32)
        q_col, q_e, q_dt = [], [], []
        for g in range(SSD_GROUPS):
            gs = slice(g * 256, (g + 1) * 256)
            bg = b_ref[:, g * SSD_STATE:(g + 1) * SSD_STATE].astype(BF16)
            cg = c_ref[:, g * SSD_STATE:(g + 1) * SSD_STATE].astype(BF16)
            cb = _dot(cg, bg, _NT)
            hs4 = hs_ref[gs, :]
            dh4 = dh_scr[gs, :]
            hs4_bf = hs4.astype(BF16)
            dh4_bf = dh4.astype(BF16)
            dy4 = dy_all[:, gs]
            dy4_bf = dy4.astype(BF16)
            xdt4_bf = xdt_all[:, gs].astype(BF16)
            xdw4 = xdw_all[:, gs]
            xdw4_bf = xdw4.astype(BF16)
            dyo4_bf = dyo_all[:, gs].astype(BF16)
            yoff4 = _dot(cg, hs4_bf, _NT) * eax[:, gs]
            dcg = _dot(dyo4_bf, hs4_bf)
            dh_new4 = _dot(dyo4_bf, cg, _TN)
            bdh4 = _dot(bg, dh4_bf, _NT)
            dbg = _dot(xdw4_bf, dh4_bf)
            e4 = xdw4 * bdh4
            q_col.append(dy4 * yoff4 - e4)
            q_e.append(e4)
            hsum = jnp.sum(dh4 * hs4, axis=1, keepdims=True)
            dcb = jnp.zeros((CHUNK, CHUNK), F32)
            dxdts = []
            for k in range(SSD_HPG):
                h = g * SSD_HPG + k
                ks = slice(k * SSD_HEAD_DIM, (k + 1) * SSD_HEAD_DIM)
                lmat = jnp.exp(jnp.where(tri, acum[:, h:h + 1] - acum_t[h:h + 1, :], NEG_BIG))
                mf = cb * lmat
                dm = _dot(dy4_bf[:, ks], xdt4_bf[:, ks], _NT)
                dcb = dcb + dm * lmat
                gmat = dm * mf
                g_rows = g_rows + jnp.where(lane == h, jnp.sum(gmat, axis=1, keepdims=True), 0.0)
                g_cols = g_cols + jnp.where(sub == h, _sum0(gmat), 0.0)
                dxdts.append(_dot(mf.astype(BF16), dy4_bf[:, ks], _TN))
                et = etot[:, h:h + 1]
                dtot = dtot + jnp.where(lane1 == h, _sum0(hsum[ks]) * et, 0.0)
                dh_scr[h * SSD_HEAD_DIM:(h + 1) * SSD_HEAD_DIM, :] = dh4[ks] * et + dh_new4[ks]
            dxdt4 = jnp.concatenate(dxdts, axis=1) + bdh4 * decx[:, gs]
            q_dt.append(dxdt4 * xs_all[:, gs])
            dx4 = dxdt4 * dtx[:, gs]
            if direct:
                dx4 = dx4 + dy4 * dv_ref[:, gs]
            dcb_bf = dcb.astype(BF16)
            dx_ref[:, gs] = dx4
            dx_ref[:, SSD_INNER + g * SSD_STATE:SSD_INNER + (g + 1) * SSD_STATE] = dbg + _dot(dcb_bf, cg, _TN)
            dx_ref[:, SSD_INNER + 1024 + g * SSD_STATE:SSD_INNER + 1024 + (g + 1) * SSD_STATE] = \
                dcg + _dot(dcb_bf, bg)
        e_heads = gather(jnp.concatenate(q_e, axis=1))
        dacum = gather(jnp.concatenate(q_col, axis=1)) + g_rows - g_cols.T
        dal = _dot(tri_t.astype(F32), dacum, precision=HI) + dtot + _sum0(e_heads)
        ddt = gather(jnp.concatenate(q_dt, axis=1)) + dal * a_r
        ddt_raw = ddt * _sig(dt_ref[...] + br_ref[...])
        ddt_ref[...] = ddt_raw
        dal_ref[...] += _sum0(dal * dt) * a_r
        dbias_ref[...] += _sum0(ddt_raw)

    small = lambda shape: pl.BlockSpec(shape, lambda r: (0, 0))
    return pl.pallas_call(
        kern, name=name, grid=(nc,),
        in_specs=[pl.BlockSpec((CHUNK, SSD_INNER), lambda r: (cidx(r), 0)),
                  pl.BlockSpec((CHUNK, SSD_INNER), lambda r: (cidx(r), 0)),
                  pl.BlockSpec((CHUNK, 1024), lambda r: (cidx(r), 2)),
                  pl.BlockSpec((CHUNK, 1024), lambda r: (cidx(r), 3)),
                  pl.BlockSpec((None, SSD_INNER, SSD_STATE), lambda r: (step_of(r), 0, 0)),
                  pl.BlockSpec((CHUNK, nh), lambda r: (cidx(r), 0)),
                  pl.BlockSpec((nh, CHUNK), lambda r: (0, cidx(r))),
                  small((1, nh)), small((nh, 1)), small((1, nh)), small((nh, 1)), small((1, SSD_INNER)),
                  small((nh, SSD_INNER)), small((SSD_INNER, nh))],
        out_specs=[pl.BlockSpec((CHUNK, SSD_CONV_DIM), lambda r: (cidx(r), 0)),
                   pl.BlockSpec((CHUNK, nh), lambda r: (cidx(r), 0)),
                   small((1, nh)), small((1, nh))],
        out_shape=[jax.ShapeDtypeStruct((n, SSD_CONV_DIM), F32), jax.ShapeDtypeStruct((n, nh), F32),
                   jax.ShapeDtypeStruct((1, nh), F32), jax.ShapeDtypeStruct((1, nh), F32)],
        scratch_shapes=[pltpu.VMEM((SSD_INNER, SSD_STATE), F32)],
        compiler_params=_params("arbitrary"),
    )(dy, xbc, xbc, xbc, hs, dt_raw, dtT_raw, bias_r, bias_c, alog_r, alog_c, dvec, _head_spread(),
      _head_spread().T)


def _gm_spatial_fwd(gu, gvn, ws, bst, *, name):
    n = gu.shape[0]

    def kern(gu_ref, gv_ref, ws_ref, bs_ref, o_ref):
        for g in range(GM_GROUPS):
            sl = slice(g * GM_GROUP_DIM, (g + 1) * GM_GROUP_DIM)
            s = _dot(ws_ref[g], gv_ref[:, sl]) + bs_ref[:, g:g + 1]
            o_ref[:, sl] = (gu_ref[:, sl] * s).astype(BF16)

    spec = pl.BlockSpec((CHUNK, GM_INNER), lambda i: (i, 0))
    return pl.pallas_call(
        kern, name=name, grid=(n // CHUNK,),
        in_specs=[spec, spec, pl.BlockSpec(ws.shape, lambda i: (0, 0, 0)), pl.BlockSpec(bst.shape, lambda i: (0, 0))],
        out_specs=spec, out_shape=jax.ShapeDtypeStruct((n, GM_INNER), BF16),
        compiler_params=_params("parallel"),
    )(gu, gvn, ws, bst)


def _gm_spatial_bwd(dt, gu, gvn, ws, wst, bst, *, name):
    n = gu.shape[0]

    def kern(dt_ref, gu_ref, gv_ref, ws_ref, wst_ref, bs_ref, dgu_ref, dgv_ref, dws_ref, dbs_ref):
        @pl.when(pl.program_id(0) == 0)
        def _():
            dws_ref[...] = jnp.zeros_like(dws_ref)
            dbs_ref[...] = jnp.zeros_like(dbs_ref)

        lane = lax.broadcasted_iota(jnp.int32, (CHUNK, GM_GROUPS), 1)
        dbs = jnp.zeros((CHUNK, GM_GROUPS), F32)
        for g in range(GM_GROUPS):
            sl = slice(g * GM_GROUP_DIM, (g + 1) * GM_GROUP_DIM)
            gv = gv_ref[:, sl]
            s = _dot(ws_ref[g], gv) + bs_ref[:, g:g + 1]
            d = dt_ref[:, sl]
            dgu_ref[:, sl] = d * s
            ds = d * gu_ref[:, sl]
            ds_bf = ds.astype(BF16)
            dws_ref[g] += _dot(ds_bf, gv, _NT)
            dgv_ref[:, sl] = _dot(wst_ref[g], ds_bf)
            dbs = dbs + jnp.where(lane == g, jnp.sum(ds, axis=1, keepdims=True), 0.0)
        dbs_ref[...] += dbs

    spec = pl.BlockSpec((CHUNK, GM_INNER), lambda i: (i, 0))
    wspec = pl.BlockSpec(ws.shape, lambda i: (0, 0, 0))
    bspec = pl.BlockSpec(bst.shape, lambda i: (0, 0))
    return pl.pallas_call(
        kern, name=name, grid=(n // CHUNK,),
        in_specs=[spec, spec, spec, wspec, wspec, bspec],
        out_specs=[spec, spec, wspec, bspec],
        out_shape=[jax.ShapeDtypeStruct((n, GM_INNER), F32), jax.ShapeDtypeStruct((n, GM_INNER), F32),
                   jax.ShapeDtypeStruct(ws.shape, F32), jax.ShapeDtypeStruct(bst.shape, F32)],
        compiler_params=_params("arbitrary"),
    )(dt, gu, gvn, ws, wst, bst)


def _adamw(parts, w, m, v, *, name, tm=256, sel=(), into=None):
    ns, r, wd = parts.shape
    tm = _pick(r, tm, 8)
    tc = wd
    if tm < 64 and wd % 256 == 0:
        tm, tc = r, 256
    lead = len(sel)
    assert w.shape[lead:] == (r, wd) and lead == w.ndim - 2

    def kern(*refs):
        p_ref, w_ref, m_ref, v_ref = refs[:4]
        g_ref, d_ref, nm_ref, nv_ref = refs[-4:]
        g = p_ref[0].astype(F32)
        for s in range(1, ns):
            g = g + p_ref[s].astype(F32)
        m2 = ADAM_B1 * m_ref[...] + (1.0 - ADAM_B1) * g
        v2 = ADAM_B2 * v_ref[...] + (1.0 - ADAM_B2) * (g * g)
        m_hat = m2 / (1.0 - ADAM_B1 ** ADAM_STEP)
        v_hat = v2 / (1.0 - ADAM_B2 ** ADAM_STEP)
        g_ref[...] = g
        d_ref[...] = -ADAM_LR * (m_hat / (jnp.sqrt(v_hat) + ADAM_EPS) + ADAM_WD * w_ref[...])
        nm_ref[...] = m2
        nv_ref[...] = v2

    spec = pl.BlockSpec((None,) * lead + (tm, tc), lambda i, j: tuple(sel) + (i, j))
    extra, aliases = [], {}
    if into is not None:
        extra = list(into)
        aliases = {4 + k: k for k in range(4)}
    return pl.pallas_call(
        kern, name=name, grid=(r // tm, wd // tc),
        in_specs=[pl.BlockSpec((ns, tm, tc), lambda i, j: (0, i, j)), spec, spec, spec] +
                 [pl.BlockSpec(memory_space=pl.ANY)] * len(extra),
        out_specs=[spec] * 4, out_shape=[jax.ShapeDtypeStruct(w.shape, F32)] * 4,
        input_output_aliases=aliases,
        compiler_params=_params("parallel", "parallel"),
    )(parts, w, m, v, *extra)


def _sum_slots(parts, *, name, scale_by=None):
    ns, r, wd = parts.shape

    def kern(*refs):
        p_ref, o_ref = refs[0], refs[-1]
        g = p_ref[0]
        for s in range(1, ns):
            g = g + p_ref[s]
        if scale_by is not None:
            g = g * _dsilu(refs[1][...])
        o_ref[...] = g

    args = [parts] + ([] if scale_by is None else [scale_by])
    return pl.pallas_call(kern, name=name, out_shape=jax.ShapeDtypeStruct((r, wd), F32),
                          compiler_params=pltpu.CompilerParams(vmem_limit_bytes=VMEM_LIMIT_BYTES))(*args)


def _mesh_pos():
    x, y, c = lax.axis_index("x"), lax.axis_index("y"), lax.axis_index("c")
    return x, y, c, 4 * x + 2 * y + c


def _flip(x, y, c, f):
    fx, fy, fc = (f >> 2) & 1, (f >> 1) & 1, f & 1
    px = 1 - x if fx else x
    py = 1 - y if fy else y
    pc = 1 - c if fc else c
    return (px, py, pc), 4 * px + 2 * py + pc


_HBM_SPEC = pl.BlockSpec(memory_space=pltpu.HBM)


def _exchange(arrays, *, scatter, name):
    na = len(arrays)
    if scatter:
        out_shape = [jax.ShapeDtypeStruct(a.shape, a.dtype) for a in arrays]
    else:
        out_shape = [jax.ShapeDtypeStruct((NDEV,) + a.shape, a.dtype) for a in arrays]

    out_shape.append(jax.ShapeDtypeStruct((8, 128), F32))

    def body(*refs):
        ins, outs = refs[:na], refs[na:2 * na]
        send_sems, recv_sems, local_sems = refs[2 * na + 1:]
        refs[2 * na][...] = jnp.zeros((8, 128), F32)
        x, y, c, me = _mesh_pos()
        copies = []
        for i in range(na):
            src_own = ins[i].at[me] if scatter else ins[i]
            lc = pltpu.make_async_copy(src_own, outs[i].at[me], local_sems.at[i])
            lc.start()
            copies.append(lc)
        sends = []
        for f in range(1, NDEV):
            peer, pidx = _flip(x, y, c, f)
            for i in range(na):
                k = i * (NDEV - 1) + f - 1
                src = ins[i].at[pidx] if scatter else ins[i]
                cp = pltpu.make_async_remote_copy(
                    src_ref=src, dst_ref=outs[i].at[me], send_sem=send_sems.at[k], recv_sem=recv_sems.at[k],
                    device_id=peer, device_id_type=pl.DeviceIdType.MESH)
                cp.start()
                sends.append(cp)
        for f in range(1, NDEV):
            peer, pidx = _flip(x, y, c, f)
            for i in range(na):
                k = i * (NDEV - 1) + f - 1
                src = ins[i].at[pidx] if scatter else ins[i]
                pltpu.make_async_remote_copy(
                    src_ref=src, dst_ref=outs[i].at[pidx], send_sem=send_sems.at[k], recv_sem=recv_sems.at[k],
                    device_id=peer, device_id_type=pl.DeviceIdType.MESH).wait_recv()
        for cp in sends:
            cp.wait_send()
        for lc in copies:
            lc.wait()

    res = pl.pallas_call(
        body, name=name, out_shape=out_shape, in_specs=[_HBM_SPEC] * na,
        out_specs=[_HBM_SPEC] * na + [pl.BlockSpec(memory_space=pltpu.VMEM)],
        scratch_shapes=[pltpu.SemaphoreType.DMA((na * (NDEV - 1),)), pltpu.SemaphoreType.DMA((na * (NDEV - 1),)),
                        pltpu.SemaphoreType.DMA((na,))],
        compiler_params=pltpu.CompilerParams(has_side_effects=True),
    )(*arrays)
    return res[:na], res[na][0, 0]


_SEM_SPEC = pl.BlockSpec(memory_space=pltpu.SEMAPHORE)
_DATAFLOW = pltpu.SideEffectType.DATAFLOW_SIDE_EFFECTING


def _split_copies(srcs, lands, send_sems, recv_sems, scatter, arriving):
    x, y, c, me = _mesh_pos()
    copies = []
    for i in range(len(srcs)):
        for f in range(1, NDEV):
            peer, pidx = _flip(x, y, c, f)
            k = i * (NDEV - 1) + f - 1
            copies.append(pltpu.make_async_remote_copy(
                src_ref=srcs[i].at[pidx] if scatter else srcs[i], dst_ref=lands[i].at[pidx if arriving else me],
                send_sem=send_sems.at[k], recv_sem=recv_sems.at[k], device_id=peer,
                device_id_type=pl.DeviceIdType.MESH))
    return copies


def _exchange_start(srcs, lands, *, scatter, name):
    na = len(srcs)
    nsem = na * (NDEV - 1)

    def body(*refs):
        ins_src, ins_land = refs[:na], refs[na:2 * na]
        send_sems, recv_sems = refs[2 * na], refs[2 * na + 1]
        token = refs[-1]
        for cp in _split_copies(ins_src, ins_land, send_sems, recv_sems, scatter, False):
            cp.start()
        token[...] = jnp.zeros_like(token)

    thru = [pltpu.HBM(a.shape, a.dtype) for a in list(srcs) + list(lands)]
    res = pl.pallas_call(
        body, name=name,
        out_shape=(pltpu.SemaphoreType.DMA((nsem,)), pltpu.SemaphoreType.DMA((nsem,)), *thru,
                   jax.ShapeDtypeStruct((8, 128), F32)),
        in_specs=[_HBM_SPEC] * (2 * na),
        out_specs=(_SEM_SPEC, _SEM_SPEC, *([_HBM_SPEC] * (2 * na)), pl.BlockSpec(memory_space=pltpu.VMEM)),
        input_output_aliases={i: 2 + i for i in range(2 * na)},
        compiler_params=pltpu.CompilerParams(has_side_effects=_DATAFLOW),
    )(*[pltpu.with_memory_space_constraint(a, pltpu.HBM) for a in list(srcs) + list(lands)])
    send_sems, recv_sems = res[0], res[1]
    return send_sems, recv_sems, res[2:2 + na], res[2 + na:2 + 2 * na], res[-1][0, 0]


def _exchange_wait(send_sems, recv_sems, srcs, lands, after, *, scatter, name):
    na = len(srcs)

    def body(*refs):
        ins_src, ins_land = refs[:na], refs[na:2 * na]
        s_sems, r_sems = refs[2 * na], refs[2 * na + 1]
        for cp in _split_copies(ins_src, ins_land, s_sems, r_sems, scatter, False):
            cp.wait_send()
        for cp in _split_copies(ins_src, ins_land, s_sems, r_sems, scatter, True):
            cp.wait_recv()

    thru = [pltpu.HBM(a.shape, a.dtype) for a in list(srcs) + list(lands)]
    res = pl.pallas_call(
        body, name=name, out_shape=tuple(thru),
        in_specs=[_HBM_SPEC] * (2 * na) + [_SEM_SPEC, _SEM_SPEC, pl.BlockSpec(memory_space=pl.ANY)],
        out_specs=tuple([_HBM_SPEC] * (2 * na)),
        input_output_aliases={i: i for i in range(2 * na)},
        compiler_params=pltpu.CompilerParams(has_side_effects=_DATAFLOW),
    )(*srcs, *lands, send_sems, recv_sems, after)
    return res[na:]


def _landing(block, me):
    buf = lax.empty((NDEV,) + block.shape, block.dtype)
    return lax.dynamic_update_slice_in_dim(buf, block[None], me, axis=0)


def _seg_kw(nseg, n_ctx, tm):
    return dict(nseg=nseg, seg_blocks=(n_ctx // tm if nseg == 2 else 0))


def _ffn_fwd(tag, h, gpre, gpost, shift, scale, gate, w, *, nseg, n_ctx, tm):
    n = h.shape[0]
    kw = _seg_kw(nseg, n_ctx, tm)
    (u,) = _rowwise(tag + "_pre", _pre_fwd_fn, n, [h], [("full", gpre), ("seg", shift), ("seg", scale)],
                    [(D_MODEL, BF16)], tm=tm, **kw)
    if "early" in w:
        w.update(w.pop("early")(u))
    s, a, b = _mm_glu(u, w["win_t"], name=tag + "_glu")
    if "late" in w:
        w.update(w.pop("late")(s))
    y, ho = _mm_rows(s, w["wout"], functools.partial(_out_post_fn, 0.5), [h], [("full", gpost), ("seg", gate)],
                     [(D_MODEL, F32), (D_MODEL, F32)], name=tag + "_out", tk=FFN_DIM, n_ctx=n_ctx)
    return ho, dict(h=h, u=u, s=s, a=a, b=b, y=y)


def _ffn_bwd(tag, dho, sv, gpre, gpost, scale, gate, w, put, *, nseg, n_ctx, tm):
    n = dho.shape[0]
    kw = _seg_kw(nseg, n_ctx, tm)
    dy, dgate, dgpost = _rowwise(tag + "_postb", functools.partial(_post_bwd_fn, 0.5), n, [dho, sv["y"]],
                                 [("full", gpost), ("seg", gate)], [(D_MODEL, BF16)], [D_MODEL, D_MODEL], tm=tm, **kw)
    tok = put("w_out", _mm_tn(sv["s"], dy, name=tag + "_dwout", tm=1408, tn=1024, col_blocks=1))
    dp = _mm_glu_bwd(dy, w["wout"], sv["a"], sv["b"], name=tag + "_ds")
    tok2 = put("w_in", _mm_tn(dp, sv["u"], name=tag + "_dwin", tm=1408, tn=1024, col_blocks=1))
    for t in (tok, tok2):
        if t is not None:
            gpre = gpre + t
    dh, dshift, dscale, dgpre = _mm_rows(dp, w["win_t"], _pre_bwd_fn, [sv["h"], dho],
                                         [("full", gpre), ("seg", scale)], [(D_MODEL, F32)],
                                         [D_MODEL, D_MODEL, D_MODEL], name=tag + "_du", tk=FFN_DIM, n_ctx=n_ctx)
    return dh, None, dict(shift=dshift, scale=dscale, gate=dgate, gpre=dgpre, gpost=dgpost)


def _local_step(x, ctx, target, mods, norm_g, get_w, small, put_grad):
    t_len, n_ctx = x.shape[0], ctx.shape[0]
    n0 = t_len + n_ctx
    tm0 = _pick(n_ctx, 256, 8)
    tm1 = _pick(t_len, 256, 8)
    ncc = n_ctx // CHUNK
    g = {}

    def modrow(i, k, nseg):
        mc, mx = mods[i]
        if nseg == 2:
            return jnp.stack([mc[k], mx[k]])[:, None, :]
        return mx[k][None, None, :]

    pending = [None]

    def gvec(i, k):
        v = norm_g[i, k][None, :]
        if pending[0] is not None:
            v = v + pending[0]
            pending[0] = None
        return v

    xc = jnp.concatenate([ctx, x], axis=0)
    L0 = dict(nseg=2, n_ctx=n_ctx, tm=tm0)
    wts = dict(get_w("ffn00", xc))
    h1, sv_f01 = _ffn_fwd("l0f1", xc, gvec(0, 0), gvec(0, 1), modrow(0, 0, 2), modrow(0, 1, 2), modrow(0, 2, 2),
                          wts["ffn00"], **L0)
    kw0 = _seg_kw(2, n_ctx, tm0)
    (um0,) = _rowwise("l0m_pre", _pre_fwd_fn, n0, [h1], [("full", gvec(0, 2)), ("seg", modrow(0, 3, 2)),
                                                         ("seg", modrow(0, 4, 2))], [(D_MODEL, BF16)], tm=tm0, **kw0)
    wts.update(get_w("ssd", um0))
    win_ssd = wts["ssd_win_t"]
    nh = SSD_HEADS
    dt_blk = (SSD_INNER + SSD_CONV_DIM) // (2 * nh)
    z = _mm(um0, win_ssd, out_dtype=F32, name="ssd_z", rhs_t=True, n=SSD_INNER)
    xbc_pre = _mm(um0, win_ssd, out_dtype=F32, name="ssd_xbc", rhs_t=True, n=SSD_CONV_DIM,
                  b_off=(SSD_INNER // 1024, 0))
    dtr = _mm(um0, win_ssd, out_dtype=F32, name="ssd_dt", rhs_t=True, n=2 * nh, b_off=(dt_blk, 0))
    cpre, xbc = _conv_fwd(xbc_pre, small["conv_w8"], small["conv_b"], n_ctx=n_ctx, name="ssd_conv")
    nh = SSD_HEADS
    dt_dir = [dtr[:, :nh], dtr[:, nh:2 * nh]]
    dtT_dir = [d.T for d in dt_dir]
    bias_r = [small["dt_bias"][d][None, :] for d in range(2)]
    bias_c = [small["dt_bias"][d][:, None] for d in range(2)]
    alog_r = [small["a_log"][d][None, :] for d in range(2)]
    alog_c = [small["a_log"][d][:, None] for d in range(2)]
    ys, hss = [], []
    for d in range(2):
        yd, hsd = _ssd_scan_fwd(xbc, dt_dir[d], dtT_dir[d], bias_r[d], bias_c[d], alog_r[d], alog_c[d],
                                rev=(d == 1), n_ctx_chunks=ncc, name=f"ssd_scan{d}")
        ys.append(yd)
        hss.append(hsd)
    dvec = jnp.repeat(small["ssd_d"], SSD_HEAD_DIM)[None, :]
    ngv = small["ssd_norm_g"][None, :]
    gate_rows = [ys[0], ys[1], (xbc, SSD_INNER, 0, 0), z]
    off = n_ctx // tm1
    lat = lambda r: (r[0], r[1], r[2], off) if isinstance(r, tuple) else (r, r.shape[1], 0, off)
    (yn,) = _rowwise("ssd_gate", _ssdgate_fwd_fn, t_len, [lat(r) for r in gate_rows],
                     [("full", dvec), ("full", ngv)], [(SSD_INNER, BF16)], tm=tm1)
    h1x = h1[n_ctx:]
    L1 = dict(nseg=1, n_ctx=0, tm=_pick(t_len, 512, 8))
    if "late" in wts:
        wts.update(wts.pop("late")(yn))
    yo0, h2 = _mm_rows(yn, wts["ssd_wout"], functools.partial(_out_post_fn, 1.0), [h1x],
                       [("full", gvec(0, 3)), ("seg", modrow(0, 5, 1))], [(D_MODEL, F32), (D_MODEL, F32)],
                       name="ssd_out", tk=SSD_INNER)
    wts.update(get_w("ffn01", h2))
    h3, sv_f02 = _ffn_fwd("l0f2", h2, gvec(0, 4), gvec(0, 5), modrow(0, 6, 1), modrow(0, 7, 1), modrow(0, 8, 1),
                          wts["ffn01"], **L1)

    wts.update(get_w("ffn10", h3))
    h4, sv_f11 = _ffn_fwd("l1f1", h3, gvec(1, 0), gvec(1, 1), modrow(1, 0, 1), modrow(1, 1, 1), modrow(1, 2, 1),
                          wts["ffn10"], **L1)
    (um1,) = _rowwise("l1m_pre", _pre_fwd_fn, t_len, [h4], [("full", gvec(1, 2)), ("seg", modrow(1, 3, 1)),
                                                            ("seg", modrow(1, 4, 1))], [(D_MODEL, BF16)], tm=tm1)
    wts.update(get_w("gm", um1))
    p1 = _mm(um1, wts["gm_win"], out_dtype=F32, name="gm_in", tm=2048)
    vg = small["gm_v_g"][None, :]
    vb = small["gm_v_b"][None, :]
    gu, gvn = _rowwise("gm_act", _gm_act_fwd_fn, t_len, [p1], [("full", vg), ("full", vb)],
                       [(GM_INNER, F32), (GM_INNER, BF16)], tm=256)
    ws_bf = small["gm_w_s"].astype(BF16)
    wst_bf = jnp.swapaxes(small["gm_w_s"], 1, 2).astype(BF16)
    bst = small["gm_b_s"].T
    tgm = _gm_spatial_fwd(gu, gvn, ws_bf, bst, name="gm_spatial")
    yo1, h5 = _mm_rows(tgm, wts["gm_wout"], functools.partial(_out_post_fn, 1.0), [h4],
                       [("full", gvec(1, 3)), ("seg", modrow(1, 5, 1))], [(D_MODEL, F32), (D_MODEL, F32)],
                       name="gm_out", tk=GM_INNER)
    wts.update(get_w("ffn11", h5))
    h6, sv_f12 = _ffn_fwd("l1f2", h5, gvec(1, 4), gvec(1, 5), modrow(1, 6, 1), modrow(1, 7, 1), modrow(1, 8, 1),
                          wts["ffn11"], **L1)

    dh, loss_parts = _rowwise("loss", _loss_fn, t_len, [h6, target], [], [(D_MODEL, F32)], [D_MODEL], tm=tm1)

    zero = jnp.zeros((D_MODEL,), F32)
    dmx = [[zero] * N_MOD for _ in range(2)]
    dmc = [[zero] * N_MOD for _ in range(2)]
    dng = [[zero] * 6 for _ in range(2)]

    def put_mod(i, k, acc):
        if acc.shape[0] == 2:
            dmc[i][k] = dmc[i][k] + acc[0, 0]
            dmx[i][k] = dmx[i][k] + acc[1, 0]
        else:
            dmx[i][k] = dmx[i][k] + acc[0, 0]

    def put_g(i, k, acc):
        dng[i][k] = dng[i][k] + jnp.sum(acc[:, 0], axis=0)

    def ffn_back(tag, i, j, dho, sv, w, lay):
        nseg = lay["nseg"]
        base = 0 if j == 0 else 6
        gi = 0 if j == 0 else 4
        dh_in, pending[0], s = _ffn_bwd(tag, dho, sv, gvec(i, gi), gvec(i, gi + 1), modrow(i, base + 1, nseg),
                                        modrow(i, base + 2, nseg), w, functools.partial(put_grad, f"ffn{i}{j}"), **lay)
        put_mod(i, base, s["shift"])
        put_mod(i, base + 1, s["scale"])
        put_mod(i, base + 2, s["gate"])
        put_g(i, gi, s["gpre"])
        put_g(i, gi + 1, s["gpost"])
        return dh_in

    dh = ffn_back("l1f2", 1, 1, dh, sv_f12, wts["ffn11"], L1)
    dyo, dgate, dgp = _rowwise("l1m_postb", functools.partial(_post_bwd_fn, 1.0), t_len, [dh, yo1],
                               [("full", gvec(1, 3)), ("seg", modrow(1, 5, 1))], [(D_MODEL, BF16)],
                               [D_MODEL, D_MODEL], tm=tm1)
    put_mod(1, 5, dgate)
    put_g(1, 3, dgp)
    put_grad("gm", "w_out", _mm_tn(tgm, dyo, name="gm_dwout", tn=1024, col_blocks=1))
    dtg = _mm(dyo, wts["gm_wout"], out_dtype=F32, name="gm_dt", rhs_t=True)
    dgu, dgvn, dws, dbst = _gm_spatial_bwd(dtg, gu, gvn, ws_bf, wst_bf, bst, name="gm_spatialb")
    g["gm_w_s"] = dws
    g["gm_b_s"] = dbst.T
    dp1, dvg, dvb = _rowwise("gm_actb", _gm_act_bwd_fn, t_len, [p1, dgu, dgvn], [("full", vg)],
                             [(2 * GM_INNER, BF16)], [GM_INNER, GM_INNER], tm=256)
    g["gm_v_g"] = dvg[0, 0]
    g["gm_v_b"] = dvb[0, 0]
    pending[0] = put_grad("gm", "w_in", _mm_tn(um1, dp1, name="gm_dwin", tm=1024, col_blocks=NDEV))
    dh, dsh, dsc, dgp = _mm_rows(dp1, wts["gm_win"], _pre_bwd_fn, [h4, dh],
                                 [("full", gvec(1, 2)), ("seg", modrow(1, 4, 1))], [(D_MODEL, F32)],
                                 [D_MODEL, D_MODEL, D_MODEL], name="gm_dum", tk=2048, rhs_t=True)
    put_mod(1, 3, dsh)
    put_mod(1, 4, dsc)
    put_g(1, 2, dgp)
    dh = ffn_back("l1f1", 1, 0, dh, sv_f11, wts["ffn10"], L1)

    dh = ffn_back("l0f2", 0, 1, dh, sv_f02, wts["ffn01"], L1)
    dyo, dgate, dgp = _rowwise("l0m_postb", functools.partial(_post_bwd_fn, 1.0), t_len, [dh, yo0],
                               [("full", gvec(0, 3)), ("seg", modrow(0, 5, 1))], [(D_MODEL, BF16)],
                               [D_MODEL, D_MODEL], tm=tm1)
    put_mod(0, 5, dgate)
    put_g(0, 3, dgp)
    tok = put_grad("ssd", "w_out", _mm_tn(yn, dyo, name="ssd_dwout", tn=1024, col_blocks=1))
    dyn = _mm(dyo, wts["ssd_wout"], out_dtype=F32, name="ssd_dyn", rhs_t=True)
    dy_ssd, dz, dngv, ddv = _rowwise("ssd_gateb", _ssdgate_bwd_fn, n0,
                                     [(dyn, SSD_INNER, 0, -(n_ctx // tm0))] + gate_rows,
                                     [("full", dvec), ("full", ngv if tok is None else ngv + tok)],
                                     [(SSD_INNER, F32), (SSD_INNER, BF16)],
                                     [SSD_INNER, SSD_INNER], tm=tm0)
    g["ssd_norm_g"] = dngv[0, 0]
    g["ssd_D"] = jnp.sum(ddv[0, 0].reshape(SSD_HEADS, SSD_HEAD_DIM), axis=1)
    dxbcs, ddts, dalogs, dbiases = [], [], [], []
    for d in range(2):
        dxd, ddtd, dal, dbi = _ssd_scan_bwd(dy_ssd, xbc, hss[d], dt_dir[d], dtT_dir[d], bias_r[d], bias_c[d],
                                            alog_r[d], alog_c[d], dvec, rev=(d == 1), n_ctx_chunks=ncc,
                                            direct=(d == 0), name=f"ssd_scanb{d}")
        dxbcs.append(dxd)
        ddts.append(ddtd)
        dalogs.append(dal[0])
        dbiases.append(dbi[0])
    g["ssd_A_log"] = jnp.stack(dalogs)
    g["ssd_dt_bias"] = jnp.stack(dbiases)
    dxbc_pre, dcw8, dcb = _conv_bwd(dxbcs[0], dxbcs[1], cpre, xbc_pre, small["conv_w8"], n_ctx=n_ctx, name="ssd_convb")
    g["ssd_conv_w"] = dcw8[:SSD_CONV]
    g["ssd_conv_b"] = dcb[0]
    ddt_bf = jnp.concatenate([ddts[0], ddts[1]], axis=1).astype(BF16)
    n_in = SSD_INNER + SSD_CONV_DIM + 2 * nh
    dw_t = _mm_tn(dz, um0, name="ssd_dwz", col_blocks=1, stack=(n_in, 0, None))
    dw_t = _mm_tn(dxbc_pre, um0, name="ssd_dwxbc", col_blocks=1, stack=(n_in, SSD_INNER, dw_t))
    dw_t = _mm_tn(ddt_bf, um0, name="ssd_dwdt", col_blocks=1, stack=(n_in, SSD_INNER + SSD_CONV_DIM, dw_t))
    pending[0] = put_grad("ssd", "w_in", dw_t)
    dum0 = _mm(dz, win_ssd, out_dtype=F32, name="ssd_dum_z", tk=SSD_INNER, n=D_MODEL)
    dum0 = _mm(dxbc_pre, win_ssd, out_dtype=F32, name="ssd_dum_x", tk=SSD_INNER, n=D_MODEL,
               b_off=(SSD_INNER // SSD_INNER, 0), add=dum0)
    dum0 = _mm(ddt_bf, win_ssd, out_dtype=F32, name="ssd_dum_dt", tk=2 * nh, n=D_MODEL, b_off=(dt_blk, 0), add=dum0)
    dh0, dsh, dsc, dgp = _rowwise("l0m_preb", _pre_bwd_fn, n0, [dum0, h1, (dh, D_MODEL, 0, -(n_ctx // tm0))],
                                  [("full", gvec(0, 2)), ("seg", modrow(0, 4, 2))], [(D_MODEL, F32)],
                                  [D_MODEL, D_MODEL, D_MODEL], tm=tm0, **kw0)
    put_mod(0, 3, dsh)
    put_mod(0, 4, dsc)
    put_g(0, 2, dgp)
    dh0 = ffn_back("l0f1", 0, 0, dh0, sv_f01, wts["ffn00"], L0)
    grad_x = dh0[n_ctx:]
    g["norm_g"] = jnp.stack([jnp.stack(r) for r in dng])
    g["dmx"] = jnp.stack([jnp.concatenate(r) for r in dmx])
    g["dmc"] = jnp.stack([jnp.concatenate(r) for r in dmc])
    return loss_parts[0], grad_x, g


GROUPS = ("ffn00", "ssd", "ffn01", "ffn10", "gm", "ffn11")


TRANSPOSED_IN = ("ffn", "ssd")


def _is_transposed(group):
    return group.startswith(TRANSPOSED_IN)


def _mats_in(group, win_l):
    if _is_transposed(group):
        return {("win_t" if group.startswith("ffn") else group + "_win_t"): win_l.reshape(-1, win_l.shape[2])}
    return {group + "_win": win_l}


def _mats_out(group, wout_l):
    pre = "" if group.startswith("ffn") else group + "_"
    return {pre + "wout": wout_l.reshape(-1, wout_l.shape[2])}


def _group_mats(group, lands):
    m = {**_mats_in(group, lands[0]), **_mats_out(group, lands[1])}
    return {group: m} if group.startswith("ffn") else m


def _grad_blocks(which, grad):
    if grad.ndim == 3:
        return grad if grad.shape[0] == NDEV else grad.reshape(NDEV, grad.shape[1] // NDEV, grad.shape[2])
    if which == "w_in":
        k, n = grad.shape
        return jnp.transpose(grad.reshape(k, NDEV, n // NDEV), (1, 0, 2)).astype(BF16)
    return grad.reshape(NDEV, grad.shape[0] // NDEV, grad.shape[1]).astype(BF16)


def kernel(x, c, ctx, c_ctx, ada_w, ada_b, norm_g, ffn_w_in, ffn_w_out, ssd_w_in, ssd_conv_w, ssd_conv_b, ssd_dt_bias, ssd_A_log, ssd_D, ssd_norm_g, ssd_w_out, gm_w_in, gm_v_g, gm_v_b, gm_w_s, gm_b_s, gm_w_out, loss_target, m_c_ctx, m_ada_w, m_ada_b, m_norm_g, m_ffn_w_in, m_ffn_w_out, m_ssd_w_in, m_ssd_conv_w, m_ssd_conv_b, m_ssd_dt_bias, m_ssd_A_log, m_ssd_D, m_ssd_norm_g, m_ssd_w_out, m_gm_w_in, m_gm_v_g, m_gm_v_b, m_gm_w_s, m_gm_b_s, m_gm_w_out, v_c_ctx, v_ada_w, v_ada_b, v_norm_g, v_ffn_w_in, v_ffn_w_out, v_ssd_w_in, v_ssd_conv_w, v_ssd_conv_b, v_ssd_dt_bias, v_ssd_A_log, v_ssd_D, v_ssd_norm_g, v_ssd_w_out, v_gm_w_in, v_gm_v_g, v_gm_v_b, v_gm_w_s, v_gm_b_s, v_gm_w_out):
    me = 4 * lax.axis_index("x") + 2 * lax.axis_index("y") + lax.axis_index("c")
    d = D_MODEL
    ncol = N_MOD * d // NDEV

    small_pack = jnp.concatenate([c.reshape(-1), norm_g.reshape(-1), ssd_conv_w.reshape(-1),
                                  gm_v_g.reshape(-1), gm_v_b.reshape(-1)])[None, :]
    (sp,), _ = _exchange([small_pack], scatter=False, name="gather_small")
    sp = sp[:, 0]
    o = 0
    c_all = sp[:, o:o + d]; o += d
    ng_all = sp[:, o:o + 2 * 6 * 128].reshape(NDEV, 2, 6, 128); o += 2 * 6 * 128
    cw_all = sp[:, o:o + SSD_CONV * 512].reshape(NDEV, SSD_CONV, 512); o += SSD_CONV * 512
    vg_all = sp[:, o:o + 256]; o += 256
    vb_all = sp[:, o:o + 256]; o += 256
    norm_g_full = jnp.transpose(ng_all, (1, 2, 0, 3)).reshape(2, 6, d)
    conv_w_full = jnp.transpose(cw_all, (1, 0, 2)).reshape(SSD_CONV, SSD_CONV_DIM)
    gm_v_g_full = vg_all.reshape(-1)
    gm_v_b_full = vb_all.reshape(-1)

    c16 = jnp.concatenate([c_all, jnp.broadcast_to(c_ctx[None, :], (NDEV, d))], axis=0)
    ada_b_loc = lax.dynamic_slice_in_dim(ada_b, me * ncol, ncol, axis=1)
    mods_loc = jnp.stack([_mm_f32(c16, ada_w[i], name=f"ada_mod{i}", silu_a=True, bias=ada_b_loc[i][None, :])
                          for i in range(2)])
    (mods_all,), mods_done = _exchange([mods_loc], scatter=False, name="gather_mods")

    tr = lambda a: jnp.swapaxes(a, -1, -2)
    shard = {"ssd": (tr(ssd_w_in)[0], ssd_w_out[0]), "gm": (gm_w_in[0], gm_w_out[0])}
    for i in range(2):
        for j in range(2):
            shard[f"ffn{i}{j}"] = (tr(ffn_w_in)[i, j], ffn_w_out[i, j])
    apart = GROUPS[:2]
    units = []
    for grp in GROUPS:
        units += [(grp + "_in", grp, (0,)), (grp + "_out", grp, (1,))] if grp in apart else [(grp, grp, (0, 1))]
    gathers = {}
    started = mods_done
    for unit, grp, idx in units:
        srcs = [(shard[grp][k] + started).astype(BF16) for k in idx]
        st = _exchange_start(srcs, [_landing(s, me) for s in srcs], scatter=False, name="gather_start_" + unit)
        gathers[unit] = st[:4]
        started = st[4]

    def fetch(unit, after):
        return _exchange_wait(*gathers[unit], after, scatter=False, name="gather_wait_" + unit)

    def get_w(grp, after):
        if grp not in apart:
            return _group_mats(grp, fetch(grp, after))
        early = lambda later: _mats_in(grp, fetch(grp + "_in", later)[0])
        late = lambda later: _mats_out(grp, fetch(grp + "_out", later)[0])
        if grp.startswith("ffn"):
            return {grp: dict(early=early, late=late)}
        return dict(early(after), late=late)

    scatters = {}
    held = {}

    def put_grad(grp, which, grad):
        if grp in apart:
            unit, blocks = grp + "_" + which[2:], [_grad_blocks(which, grad)]
        else:
            held[grp, which] = _grad_blocks(which, grad)
            if (grp, "w_in") not in held or (grp, "w_out") not in held:
                return None
            unit, blocks = grp, [held[grp, "w_in"], held[grp, "w_out"]]
        own = [lax.dynamic_index_in_dim(b, me, axis=0, keepdims=False) for b in blocks]
        st = _exchange_start(blocks, [_landing(o_, me) for o_ in own], scatter=True, name="scatter_start_" + unit)
        scatters[unit] = st[:4]
        return st[4]

    mods_rows = jnp.transpose(mods_all, (1, 2, 0, 3)).reshape(2, 2 * NDEV, N_MOD * d) + started
    mx = lax.dynamic_index_in_dim(mods_rows, me, axis=1, keepdims=False).reshape(2, N_MOD, d)
    mc = mods_rows[:, NDEV].reshape(2, N_MOD, d)
    mods = [(mc[i], mx[i]) for i in range(2)]

    small = dict(conv_w8=jnp.pad(conv_w_full, ((0, 8 - SSD_CONV), (0, 0))), conv_b=ssd_conv_b, dt_bias=ssd_dt_bias[0],
                 a_log=ssd_A_log[0], ssd_d=ssd_D[0], ssd_norm_g=ssd_norm_g[0], gm_v_g=gm_v_g_full,
                 gm_v_b=gm_v_b_full, gm_w_s=gm_w_s[0], gm_b_s=gm_b_s[0])
    loss_parts, grad_x, g = _local_step(x[0], ctx[0], loss_target[0], mods, norm_g_full, get_w, small, put_grad)
    g["loss"] = (0.5 / d * jnp.sum(loss_parts)).reshape(1)

    whole = {"ffn_w_in": (tr(ffn_w_in), tr(m_ffn_w_in), tr(v_ffn_w_in)), "ffn_w_out": (ffn_w_out, m_ffn_w_out, v_ffn_w_out),
             "ssd_w_in": (tr(ssd_w_in), tr(m_ssd_w_in), tr(v_ssd_w_in)), "ssd_w_out": (ssd_w_out, m_ssd_w_out, v_ssd_w_out),
             "gm_w_in": (gm_w_in, m_gm_w_in, v_gm_w_in), "gm_w_out": (gm_w_out, m_gm_w_out, v_gm_w_out)}
    res = {}

    def update_units(some, after):
        for unit, grp, idx in some:
            parts = _exchange_wait(*scatters[unit], after, scatter=True, name="scatter_wait_" + unit)
            for k, p in zip(idx, parts):
                which = ("in", "out")[k]
                nm = ("ffn" if grp.startswith("ffn") else grp) + "_w_" + which
                sel = (int(grp[3]), int(grp[4])) if grp.startswith("ffn") else (0,)
                res[nm] = _adamw(p, *whole[nm], name=f"adamw_{grp}_{which}", sel=sel, into=res.get(nm))
                after = res[nm][0]
        return after

    sg_names = ["dmx", "dmc", "norm_g", "ssd_conv_w", "ssd_conv_b", "ssd_dt_bias", "ssd_A_log", "ssd_D", "ssd_norm_g",
                "gm_v_g", "gm_v_b", "gm_w_s", "gm_b_s", "loss"]
    sg_shapes = [g[n].shape for n in sg_names]
    flat = jnp.concatenate([g[n].reshape(-1) for n in sg_names])
    npack = flat.shape[0]
    pad = (-npack) % 1024
    flat = jnp.pad(flat, (0, pad)).reshape(-1, 128)
    sg_start = _exchange_start([flat], [_landing(flat, me)], scatter=False, name="small_grads_start")
    by_send = list(reversed(units))
    update_units(by_send[:4], jnp.stack([sg_start[4], grad_x[0, 0]]))
    early_done = jnp.stack([res[nm][0].reshape(-1)[-1] for nm in sorted(res)])
    (sg_all,) = _exchange_wait(*sg_start[:4], early_done, scatter=False, name="small_grads_wait")
    sg_sum = _sum_slots(sg_all, name="sum_small_grads").reshape(-1)[:npack]
    update_units(by_send[4:], sg_sum)
    sums = {}
    o = 0
    for n, shp in zip(sg_names, sg_shapes):
        sz = math.prod(shp)
        sums[n] = sg_sum[o:o + sz].reshape(shp)
        o += sz
    loss = sums["loss"][0]
    per_dev = sg_all.reshape(NDEV, -1)
    dmx_all =per_dev[:, :2 * N_MOD * d].reshape(NDEV, 2, N_MOD * d)
    dmc_all = per_dev[:, 2 * N_MOD * d:4 * N_MOD * d].reshape(NDEV, 2, N_MOD * d)

    (s16,) = _rowwise("ada_silu", lambda cc: ((_silu(cc),), ()), 2 * NDEV, [c16], [], [(d, F32)], tm=2 * NDEV)
    s16_t = s16.T
    g_ada_w, dcc_parts = [], []
    for i in range(2):
        rhs = jnp.concatenate([lax.dynamic_slice_in_dim(dmx_all[:, i], me * ncol, ncol, axis=1),
                               lax.dynamic_slice_in_dim(dmc_all[:, i], me * ncol, ncol, axis=1)], axis=0)
        g_ada_w.append(_mm_f32(s16_t, rhs, name=f"ada_dw{i}"))
        dmc_loc = lax.dynamic_slice_in_dim(sums["dmc"][i], me * ncol, ncol, axis=0)
        rhs_c = jnp.zeros((ncol, 128), F32).at[:, 0].set(dmc_loc)
        dcc_parts.append(_mm_f32(ada_w[i], rhs_c, name=f"ada_dcc{i}")[:, 0])
    g_ada_w = jnp.stack(g_ada_w)
    dcc_part = (dcc_parts[0] + dcc_parts[1]).reshape(8, 128)
    (dcc_all,), _ = _exchange([dcc_part], scatter=False, name="gather_dcc")
    g_c_ctx = _sum_slots(dcc_all, name="sum_dcc", scale_by=c_ctx.reshape(8, 128)).reshape(d)
    g_ada_b = sums["dmx"] + sums["dmc"]

    outs = _adamw(g_ada_w.reshape(1, -1, ncol), ada_w.reshape(-1, ncol), m_ada_w.reshape(-1, ncol),
                  v_ada_w.reshape(-1, ncol), name="adamw_ada_w")
    res["ada_w"] = [o_.reshape(ada_w.shape) for o_ in outs]

    loc = lambda a, ax, n: lax.dynamic_slice_in_dim(a, me * n, n, axis=ax)
    small_g = dict(c_ctx=g_c_ctx, ada_b=g_ada_b, norm_g=loc(sums["norm_g"], 2, 128),
                   ssd_conv_w=loc(sums["ssd_conv_w"], 1, 512)[None], ssd_conv_b=sums["ssd_conv_b"][None],
                   ssd_dt_bias=sums["ssd_dt_bias"][None], ssd_A_log=sums["ssd_A_log"][None], ssd_D=sums["ssd_D"][None],
                   ssd_norm_g=sums["ssd_norm_g"][None], gm_v_g=loc(sums["gm_v_g"], 0, 256)[None],
                   gm_v_b=loc(sums["gm_v_b"], 0, 256)[None], gm_w_s=sums["gm_w_s"][None], gm_b_s=sums["gm_b_s"][None])
    small_w = dict(c_ctx=(c_ctx, m_c_ctx, v_c_ctx), ada_b=(ada_b, m_ada_b, v_ada_b), norm_g=(norm_g, m_norm_g, v_norm_g),
                   ssd_conv_w=(ssd_conv_w, m_ssd_conv_w, v_ssd_conv_w), ssd_conv_b=(ssd_conv_b, m_ssd_conv_b, v_ssd_conv_b),
                   ssd_dt_bias=(ssd_dt_bias, m_ssd_dt_bias, v_ssd_dt_bias), ssd_A_log=(ssd_A_log, m_ssd_A_log, v_ssd_A_log),
                   ssd_D=(ssd_D, m_ssd_D, v_ssd_D), ssd_norm_g=(ssd_norm_g, m_ssd_norm_g, v_ssd_norm_g),
                   gm_v_g=(gm_v_g, m_gm_v_g, v_gm_v_g), gm_v_b=(gm_v_b, m_gm_v_b, v_gm_v_b),
                   gm_w_s=(gm_w_s, m_gm_w_s, v_gm_w_s), gm_b_s=(gm_b_s, m_gm_b_s, v_gm_b_s))
    sn = list(small_w)

    def pack(arrs):
        f = jnp.concatenate([a.reshape(-1) for a in arrs])
        return jnp.pad(f, (0, (-f.shape[0]) % (256 * 128))).reshape(-1, 128)

    pg = pack([small_g[n].reshape(small_w[n][0].shape) for n in sn])
    outs = _adamw(pg[None], pack([small_w[n][0] for n in sn]), pack([small_w[n][1] for n in sn]),
                  pack([small_w[n][2] for n in sn]), name="adamw_small")
    flat_outs = [o_.reshape(-1) for o_ in outs]
    o = 0
    for n in sn:
        shp = small_w[n][0].shape
        sz = math.prod(shp)
        res[n] = [fo[o:o + sz].reshape(shp) for fo in flat_outs]
        o += sz

    order = ["c_ctx", "ada_w", "ada_b", "norm_g", "ffn_w_in", "ffn_w_out", "ssd_w_in", "ssd_conv_w", "ssd_conv_b",
             "ssd_dt_bias", "ssd_A_log", "ssd_D", "ssd_norm_g", "ssd_w_out", "gm_w_in", "gm_v_g", "gm_v_b", "gm_w_s",
             "gm_b_s", "gm_w_out"]
    for nm in ("ffn_w_in", "ssd_w_in"):
        res[nm] = [tr(a) for a in res[nm]]
    result = [loss, grad_x[None]]
    for k in range(4):
        result += [res[n][k] for n in order]
    return tuple(result)
```

```python
import functools
import math

import jax
import jax.numpy as jnp
from jax import lax
from jax.experimental import pallas as pl
from jax.experimental.pallas import tpu as pltpu

F32 = jnp.float32
BF16 = jnp.bfloat16

NDEV = 8
D_MODEL = 1024
FFN_DIM = 2816
N_MOD = 9
EPS = 1e-6
SSD_INNER = 2048
SSD_HEADS = 32
SSD_HEAD_DIM = 64
SSD_GROUPS = 8
SSD_HPG = 4
SSD_STATE = 128
SSD_CONV = 5
SSD_CONV_DIM = 4096
CHUNK = 128
GM_INNER = 2048
GM_GROUPS = 8
GM_GROUP_DIM = 256
ADAM_LR = 0.001
ADAM_B1 = 0.9
ADAM_B2 = 0.999
ADAM_EPS = 1e-08
ADAM_WD = 0.01
ADAM_STEP = 10
NEG_BIG = -1e30
VMEM_LIMIT_BYTES = 56 * 1024 * 1024
HI = lax.Precision.HIGHEST


def _params(*sem):
    return pltpu.CompilerParams(dimension_semantics=sem, vmem_limit_bytes=VMEM_LIMIT_BYTES)


def _pick(n, target, mult=16):
    if n <= target:
        return n
    for t in range(target - target % mult, 0, -mult):
        if n % t == 0:
            return t
    raise ValueError((n, target, mult))


def _sig(x):
    return 0.5 * jnp.tanh(0.5 * x) + 0.5


def _silu(x):
    return x * _sig(x)


def _dsilu(x):
    s = _sig(x)
    return s * (1.0 + x * (1.0 - s))


_GELU_C = math.sqrt(2.0 / math.pi)


def _gelu(x):
    return 0.5 * x * (1.0 + jnp.tanh(_GELU_C * (x + 0.044715 * x * x * x)))


def _gelu_and_grad(x):
    x2 = x * x
    t = jnp.tanh(_GELU_C * (x + 0.044715 * x2 * x))
    half = 0.5 * (1.0 + t)
    return x * half, half + 0.5 * x * (1.0 - t * t) * _GELU_C * (1.0 + 3.0 * 0.044715 * x2)


def _dgelu(x):
    return _gelu_and_grad(x)[1]


def _softplus(x):
    return jnp.maximum(x, 0.0) + jnp.log1p(jnp.exp(-jnp.abs(x)))


def _sum0(v):
    return jnp.sum(v, axis=0, keepdims=True)


def _rms(h):
    r = lax.rsqrt(jnp.mean(h * h, axis=-1, keepdims=True) + EPS)
    return h * r, r


def _dot(a, b, dims=((1,), (0,)), precision=None):
    return lax.dot_general(a, b, (dims, ((), ())), preferred_element_type=F32, precision=precision)


_NT = ((1,), (1,))
_TN = ((0,), (0,))


def _rowwise(name, fn, n_rows, rows, consts, outs, accs=(), *, tm, nseg=1, seg_blocks=0):
    assert n_rows % tm == 0
    if nseg == 2:
        assert seg_blocks > 0
        seg = lambda i: jnp.where(i < seg_blocks, 0, 1)
    else:
        seg = lambda i: 0
    in_specs, args, lacking = [], [], []
    for r in rows:
        arr, width, cb, off = r if isinstance(r, tuple) else (r, r.shape[1], 0, 0)
        in_specs.append(pl.BlockSpec((tm, width), lambda i, cb=cb, off=off: (jnp.maximum(i + off, 0), cb)))
        args.append(arr)
        lacking.append(-off if off < 0 else 0)
    for kind, arr in consts:
        if kind == "seg":
            assert arr.shape[0] == nseg and arr.shape[1] == 1, arr.shape
            in_specs.append(pl.BlockSpec((None, 1, arr.shape[2]), lambda i: (seg(i), 0, 0)))
        else:
            in_specs.append(pl.BlockSpec(arr.shape, lambda i: (0, 0)))
        args.append(arr)
    out_shape = [jax.ShapeDtypeStruct((n_rows, w), dt) for w, dt in outs]
    out_specs = [pl.BlockSpec((tm, w), lambda i: (i, 0)) for w, _ in outs]
    out_shape += [jax.ShapeDtypeStruct((nseg, 1, w), F32) for w in accs]
    out_specs += [pl.BlockSpec((None, 1, w), lambda i: (seg(i), 0, 0)) for w in accs]
    n_in, n_out, n_acc = len(args), len(outs), len(accs)

    def kern(*refs):
        i = pl.program_id(0)
        ins = [r[...] for r in refs[:n_in]]
        for k, lack in enumerate(lacking):
            if lack:
                ins[k] = jnp.where(i >= lack, ins[k], jnp.zeros_like(ins[k]))
        res, terms = fn(*ins)
        for ref, v in zip(refs[n_in:n_in + n_out], res):
            ref[...] = v.astype(ref.dtype)
        if n_acc:
            sums = [_sum0(v) for v in terms]
            first = (i == 0) | (i == seg_blocks) if nseg == 2 else (i == 0)
            acc_refs = refs[n_in + n_out:]

            @pl.when(first)
            def _():
                for ref, v in zip(acc_refs, sums):
                    ref[...] = v

            @pl.when(jnp.logical_not(first))
            def _():
                for ref, v in zip(acc_refs, sums):
                    ref[...] += v

    res = pl.pallas_call(
        kern, name=name, grid=(n_rows // tm,), in_specs=in_specs, out_specs=out_specs, out_shape=out_shape,
        compiler_params=_params("arbitrary"),
    )(*args)
    return res


def _pre_fwd_fn(h, g, shift, scale):
    hh, _ = _rms(h)
    return (hh * g * (1.0 + scale) + shift,), ()


def _pre_bwd_fn(du, h, dres, g, scale):
    hh, r = _rms(h)
    n = hh * g
    dn = du * (1.0 + scale)
    dhh = dn * g
    dh = dres + r * (dhh - hh * jnp.mean(dhh * hh, axis=-1, keepdims=True))
    return (dh,), (du, du * n, dn * hh)


def _post_fwd_fn(weight, h, y, g, gate):
    yh, _ = _rms(y)
    return (h + weight * gate * (yh * g),), ()


def _out_post_fn(weight, y, h, g, gate):
    return (y,) + _post_fwd_fn(weight, h, y, g, gate)[0], ()


def _post_bwd_fn(weight, dh, y, g, gate):
    yh, r = _rms(y)
    dr = dh * weight
    dyh = dr * gate * g
    dy = r * (dyh - yh * jnp.mean(dyh * yh, axis=-1, keepdims=True))
    return (dy,), (dr * yh * g, dr * gate * yh)


def _glu_bwd_fn(ds, a, b):
    a = a.astype(F32)
    b = b.astype(F32)
    sg = _sig(a)
    da = ds * b * (sg * (1.0 + a * (1.0 - sg)))
    db = ds * (a * sg)
    return (jnp.concatenate([da, db], axis=1),), ()


def _loss_fn(y, t):
    diff = y - t
    return (diff * (1.0 / D_MODEL),), (diff * diff,)


def _ssd_y(yf, yb, xs, z, dvec):
    y = yf + yb + dvec * xs
    return y, y * _silu(z)


def _ssdgate_fwd_fn(yf, yb, xs, z, dvec, ng):
    _, yg = _ssd_y(yf, yb, xs, z, dvec)
    parts = []
    for g in range(SSD_GROUPS):
        sl = slice(g * 256, (g + 1) * 256)
        parts.append(_rms(yg[:, sl])[0])
    return (jnp.concatenate(parts, axis=1) * ng,), ()


def _ssdgate_bwd_fn(dyn, yf, yb, xs, z, dvec, ng):
    y, yg = _ssd_y(yf, yb, xs, z, dvec)
    dyg_parts, ygh_parts = [], []
    for g in range(SSD_GROUPS):
        sl = slice(g * 256, (g + 1) * 256)
        ygh, r = _rms(yg[:, sl])
        d = dyn[:, sl] * ng[:, sl]
        dyg_parts.append(r * (d - ygh * jnp.mean(d * ygh, axis=-1, keepdims=True)))
        ygh_parts.append(ygh)
    dyg = jnp.concatenate(dyg_parts, axis=1)
    ygh = jnp.concatenate(ygh_parts, axis=1)
    dy = dyg * _silu(z)
    dz = dyg * y * _dsilu(z)
    return (dy, dz), (dyn * ygh, dy * xs)


def _ln_stats(v):
    mu = jnp.mean(v, axis=-1, keepdims=True)
    vc = v - mu
    r = lax.rsqrt(jnp.mean(vc * vc, axis=-1, keepdims=True) + EPS)
    return vc * r, r


def _gm_act_fwd_fn(p, vg, vb):
    gu = _gelu(p[:, :GM_INNER])
    gvh, _ = _ln_stats(_gelu(p[:, GM_INNER:]))
    return (gu, gvh * vg + vb), ()


def _gm_act_bwd_fn(p, dgu, dgvn, vg):
    pu = p[:, :GM_INNER]
    pv = p[:, GM_INNER:]
    gv, dgelu_v = _gelu_and_grad(pv)
    gvh, r = _ln_stats(gv)
    dgvh = dgvn * vg
    dgv = r * (dgvh - jnp.mean(dgvh, axis=-1, keepdims=True) - gvh * jnp.mean(dgvh * gvh, axis=-1, keepdims=True))
    dp = jnp.concatenate([dgu * _dgelu(pu), dgv * dgelu_v], axis=1)
    return (dp,), (dgvn * gvh, dgvn)


def _mm(a, b, *, out_dtype, name, tm=1088, tn=1024, tk=1408, add=None, rhs_t=False, n=None, b_off=(0, 0)):
    m, k = a.shape
    col_blocked = b.ndim == 3
    if col_blocked:
        assert not rhs_t and n is None and b.shape[1] == k
        n, tn = b.shape[0] * b.shape[2], b.shape[2]
    elif n is None:
        n, k2 = b.shape if rhs_t else b.shape[::-1]
        assert k == k2
    tm, tn, tk = _pick(m, tm), _pick(n, tn, 128), _pick(k, tk, 128)
    o0, o1 = b_off
    nk = k // tk
    dims = _NT if rhs_t else ((1,), (0,))

    def kern(*refs):
        a_ref, b_ref = refs[:2]
        add_ref = refs[2] if add is not None else None
        o_ref = refs[3] if add is not None else refs[2]

        def finish(r):
            if add is not None:
                r = r + add_ref[...]
            o_ref[...] = r.astype(o_ref.dtype)

        p = _dot(a_ref[...], b_ref[...], dims)
        if nk == 1:
            finish(p)
            return
        acc_ref = refs[-1]
        kk = pl.program_id(2)

        @pl.when(kk == 0)
        def _():
            acc_ref[...] = p

        @pl.when((kk > 0) & (kk < nk - 1))
        def _():
            acc_ref[...] += p

        @pl.when(kk == nk - 1)
        def _():
            finish(acc_ref[...] + p)

    if col_blocked:
        b_spec = pl.BlockSpec((None, tk, tn), lambda i, j, kk: (j, kk, 0))
    elif rhs_t:
        b_spec = pl.BlockSpec((tn, tk), lambda i, j, kk: (j + o0, kk + o1))
    else:
        b_spec = pl.BlockSpec((tk, tn), lambda i, j, kk: (kk + o0, j + o1))
    in_specs = [pl.BlockSpec((tm, tk), lambda i, j, kk: (i, kk)), b_spec]
    args = [a, b]
    if add is not None:
        in_specs.append(pl.BlockSpec((tm, tn), lambda i, j, kk: (i, j)))
        args.append(add)
    return pl.pallas_call(
        kern, name=name, grid=(m // tm, n // tn, nk), in_specs=in_specs,
        out_specs=pl.BlockSpec((tm, tn), lambda i, j, kk: (i, j)),
        out_shape=jax.ShapeDtypeStruct((m, n), out_dtype),
        scratch_shapes=[pltpu.VMEM((tm, tn), F32)] if nk > 1 else [],
        compiler_params=_params("parallel", "parallel", "arbitrary"),
    )(*args)


def _mm_rows(a, b, fn, rows, consts, outs, accs=(), *, name, tm=544, tk=1408, rhs_t=False, n_ctx=0):
    halves = a.ndim == 3
    m, k = (a.shape[1], 2 * a.shape[2]) if halves else a.shape
    col_blocked = b.ndim == 3
    kb, nb = 1, None
    if col_blocked:
        assert rhs_t and b.shape[0] * b.shape[2] == k
        n, nb = b.shape[1], b.shape[2]
        kb = max(1, tk // nb)
        assert b.shape[0] % kb == 0
        tk = kb * nb
    else:
        n = b.shape[0] if rhs_t else b.shape[1]
    tm, tk = _pick(m, tm), _pick(k, tk, 128)
    nk = k // tk
    if halves:
        hb = k // 2 // tk
        a_spec = pl.BlockSpec((None, tm, tk), lambda i, kk: (kk // hb, i, kk % hb))
    else:
        a_spec = pl.BlockSpec((tm, tk), lambda i, kk: (i, kk))
    dims = _NT if rhs_t else ((1,), (0,))
    n_rows, n_const, n_out, n_acc = len(rows), len(consts), len(outs), len(accs)

    def kern(*refs):
        a_ref, b_ref = refs[:2]
        row_refs = refs[2:2 + n_rows]
        const_refs = refs[2 + n_rows:2 + n_rows + n_const]
        out_refs = refs[2 + n_rows + n_const:2 + n_rows + n_const + n_out]
        acc_refs = refs[2 + n_rows + n_const + n_out:2 + n_rows + n_const + n_out + n_acc]
        i, kk = pl.program_id(0), pl.program_id(1)

        def finish(p, rs=slice(None), r0=0):
            nr = p.shape[0]
            is_ctx = (i * tm + r0 + lax.broadcasted_iota(jnp.int32, (nr, 1), 0)) < n_ctx
            cvals = []
            for (kind, arr), ref in zip(consts, const_refs):
                if kind == "seg":
                    cvals.append(jnp.where(is_ctx, ref[0], ref[1]) if arr.shape[0] == 2 else ref[0])
                else:
                    cvals.append(ref[...])
            res, terms = fn(p, *[r[rs, :] for r in row_refs], *cvals)
            for ref, v in zip(out_refs, res):
                ref[rs, :] = v.astype(ref.dtype)
            for ref, v in zip(acc_refs, terms):
                s_all = _sum0(v)
                s_ctx = _sum0(jnp.where(is_ctx, v, 0.0)) if n_ctx else jnp.zeros_like(s_all)
                both = jnp.concatenate([s_ctx, s_all - s_ctx], axis=0)[:, None, :]

                @pl.when(i == 0)
                def _():
                    ref[...] = both

                @pl.when(i > 0)
                def _():
                    ref[...] += both

        if nk == 1 and n_acc == 0:
            nsub = 2 if tm % 32 == 0 else 1
            sub = tm // nsub
            for r in range(nsub):
                rs = slice(r * sub, (r + 1) * sub)
                finish(_dot(a_ref[rs, :], b_ref[...], dims), rs, r * sub)
            return
        if col_blocked:
            p = sum(_dot(a_ref[:, c * nb:(c + 1) * nb], b_ref[c], dims) for c in range(kb))
        else:
            p = _dot(a_ref[...], b_ref[...], dims)
        if nk == 1:
            finish(p)
            return
        scr = refs[-1]

        @pl.when(kk == 0)
        def _():
            scr[...] = p

        @pl.when((kk > 0) & (kk < nk - 1))
        def _():
            scr[...] += p

        @pl.when(kk == nk - 1)
        def _():
            finish(scr[...] + p)

    if col_blocked:
        b_spec = pl.BlockSpec((kb, n, nb), lambda i, kk: (kk, 0, 0))
    elif rhs_t:
        b_spec = pl.BlockSpec((n, tk), lambda i, kk: (0, kk))
    else:
        b_spec = pl.BlockSpec((tk, n), lambda i, kk: (kk, 0))
    in_specs = [a_spec, b_spec]
    in_specs += [pl.BlockSpec((tm, r.shape[1]), lambda i, kk: (i, 0)) for r in rows]
    for kind, arr in consts:
        in_specs.append(pl.BlockSpec(arr.shape, (lambda i, kk: (0, 0, 0)) if kind == "seg" else (lambda i, kk: (0, 0))))
    out_shape = [jax.ShapeDtypeStruct((m, w), dt) for w, dt in outs]
    out_specs = [pl.BlockSpec((tm, w), lambda i, kk: (i, 0)) for w, _ in outs]
    out_shape += [jax.ShapeDtypeStruct((2, 1, w), F32) for w in accs]
    out_specs += [pl.BlockSpec((2, 1, w), lambda i, kk: (0, 0, 0)) for w in accs]
    return pl.pallas_call(
        kern, name=name, grid=(m // tm, nk), in_specs=in_specs, out_specs=out_specs, out_shape=out_shape,
        scratch_shapes=[pltpu.VMEM((tm, n), F32)] if nk > 1 else [],
        compiler_params=_params("arbitrary", "arbitrary"),
    )(a, b, *rows, *[arr for _, arr in consts])


def _mm_glu(u, win_t, *, name, tm=1088, tn=1408):
    m, k = u.shape
    n = win_t.shape[0] // 2
    tm, tn = _pick(m, tm), _pick(n, tn, 128)
    nj = n // tn

    nsub = 2 if tm % 32 == 0 else 1
    sub = tm // nsub

    def kern(u_ref, wa_ref, wb_ref, s_ref, a_ref, b_ref):
        for r in range(nsub):
            rows = slice(r * sub, (r + 1) * sub)
            uu = u_ref[rows, :]
            a = _dot(uu, wa_ref[...], _NT)
            b = _dot(uu, wb_ref[...], _NT)
            s_ref[rows, :] = (_silu(a) * b).astype(BF16)
            a_ref[rows, :] = a.astype(BF16)
            b_ref[rows, :] = b.astype(BF16)

    ospec = pl.BlockSpec((tm, tn), lambda i, j: (i, j))
    return pl.pallas_call(
        kern, name=name, grid=(m // tm, nj),
        in_specs=[pl.BlockSpec((tm, k), lambda i, j: (i, 0)), pl.BlockSpec((tn, k), lambda i, j: (j, 0)),
                  pl.BlockSpec((tn, k), lambda i, j: (nj + j, 0))],
        out_specs=[ospec, ospec, ospec],
        out_shape=[jax.ShapeDtypeStruct((m, n), BF16)] * 3,
        compiler_params=_params("parallel", "parallel"),
    )(u, win_t, win_t)


def _mm_glu_bwd(dy, wout, a, b, *, name, tm=544, tn=1408):
    m, k = dy.shape
    f = wout.shape[0]
    tm, tn = _pick(m, tm), _pick(f, tn, 128)
    nsub = 2 if tm % 32 == 0 else 1
    sub = tm // nsub

    def kern(dy_ref, w_ref, a_ref, b_ref, o_ref):
        for r in range(nsub):
            rs = slice(r * sub, (r + 1) * sub)
            ds = _dot(dy_ref[rs, :], w_ref[...], _NT)
            (dp,), _ = _glu_bwd_fn(ds, a_ref[rs, :], b_ref[rs, :])
            o_ref[0, rs, :] = dp[:, :tn].astype(BF16)
            o_ref[1, rs, :] = dp[:, tn:].astype(BF16)

    tile = pl.BlockSpec((tm, tn), lambda i, j: (i, j))
    return pl.pallas_call(
        kern, name=name, grid=(m // tm, f // tn),
        in_specs=[pl.BlockSpec((tm, k), lambda i, j: (i, 0)), pl.BlockSpec((tn, k), lambda i, j: (j, 0)), tile, tile],
        out_specs=pl.BlockSpec((2, tm, tn), lambda i, j: (0, i, j)),
        out_shape=jax.ShapeDtypeStruct((2, m, f), BF16),
        compiler_params=_params("parallel", "parallel"),
    )(dy, wout, a, b)


def _mm_tn(a, b, *, name, tm=1024, tn=1024, tk=2176, col_blocks=None, stack=None):
    extra, extra_specs, aliases = [], [], {}
    halves = a.ndim == 3
    t, m = (a.shape[1], 2 * a.shape[2]) if halves else a.shape
    t2, n = b.shape
    assert t == t2
    tm, tn, tk = _pick(m, tm, 128), _pick(n, tn, 128), _pick(t, tk)
    nk = t // tk
    if halves:
        hb = m // 2 // tm
        a_spec = pl.BlockSpec((None, tk, tm), lambda i, j, kk: (i // hb, kk, i % hb))
    else:
        a_spec = pl.BlockSpec((tk, tm), lambda i, j, kk: (kk, i))
    if col_blocks is None:
        def kern(a_ref, b_ref, o_ref):
            kk = pl.program_id(2)

            @pl.when(kk == 0)
            def _():
                o_ref[...] = jnp.zeros_like(o_ref)

            o_ref[...] += _dot(a_ref[...], b_ref[...], _TN)

        out_spec = pl.BlockSpec((tm, tn), lambda i, j, kk: (i, j))
        out_shape = jax.ShapeDtypeStruct((m, n), F32)
        scratch = []
    else:
        wb = n // col_blocks
        per = tn // wb
        assert tn % wb == 0 and wb % 8 == 0

        def kern(a_ref, b_ref, *rest):
            o_ref, acc_ref = rest[-2:]
            kk = pl.program_id(2)
            p = _dot(a_ref[...], b_ref[...], _TN)

            @pl.when(kk == 0)
            def _():
                acc_ref[...] = p

            @pl.when((kk > 0) & (kk < nk - 1))
            def _():
                acc_ref[...] += p

            @pl.when(kk == nk - 1)
            def _():
                r = acc_ref[...] + p if nk > 1 else p
                for c in range(per):
                    o_ref[c] = r[:, c * wb:(c + 1) * wb].astype(BF16)

        rows_total, row0, into = stack if stack is not None else (m, 0, None)
        assert row0 % tm == 0
        out_spec = pl.BlockSpec((per, tm, wb), lambda i, j, kk: (j, i + row0 // tm, 0))
        out_shape = jax.ShapeDtypeStruct((col_blocks, rows_total, wb), BF16)
        scratch = [pltpu.VMEM((tm, tn), F32)]
        if into is not None:
            extra, extra_specs, aliases = [into], [pl.BlockSpec(memory_space=pl.ANY)], {2: 0}

    return pl.pallas_call(
        kern, name=name, grid=(m // tm, n // tn, nk),
        in_specs=[a_spec, pl.BlockSpec((tk, tn), lambda i, j, kk: (kk, j))] + extra_specs,
        out_specs=out_spec, out_shape=out_shape, scratch_shapes=scratch, input_output_aliases=aliases,
        compiler_params=_params("parallel", "parallel", "arbitrary"),
    )(a, b, *extra)


def _mm_f32(a, b, *, name, silu_a=False, bias=None):
    m, k = a.shape
    n = b.shape[1]

    def kern(*refs):
        if bias is None:
            a_ref, b_ref, o_ref = refs
        else:
            a_ref, b_ref, bias_ref, o_ref = refs
        av = a_ref[...]
        if silu_a:
            av = _silu(av)
        r = jnp.dot(av, b_ref[...], preferred_element_type=F32, precision=HI)
        if bias is not None:
            r = r + bias_ref[...]
        o_ref[...] = r

    args = [a, b] + ([] if bias is None else [bias])
    return pl.pallas_call(kern, name=name, out_shape=jax.ShapeDtypeStruct((m, n), F32),
                          compiler_params=pltpu.CompilerParams(vmem_limit_bytes=VMEM_LIMIT_BYTES))(*args)


CONV_WIN = 32


def _conv_windows(n, n_ctx):
    assert n_ctx % CONV_WIN == 0 and n_ctx >= CONV_WIN and n - n_ctx >= CONV_WIN
    return (0, n_ctx - CONV_WIN // 2, n - CONV_WIN)


def _tap_outside(r0, s, n, n_ctx):
    t = r0 + lax.broadcasted_iota(jnp.int32, (CONV_WIN, 1), 0)
    lo = jnp.where(t < n_ctx, 0, n_ctx)
    hi = jnp.where(t < n_ctx, n_ctx, n)
    return jnp.where((t + s >= lo) & (t + s < hi), 0.0, 1.0)


def _rolled(v, s):
    return v if s == 0 else pltpu.roll(v, (-s) % v.shape[0], 0)


def _conv_fwd(xp, w8, b, *, n_ctx, name, cb=256):
    n, c = xp.shape
    half = SSD_CONV // 2

    def kern(x_ref, w_ref, b_ref, cpre_ref, act_ref):
        x = x_ref[...]
        acc = jnp.zeros_like(x) + b_ref[...]
        rolled = {}
        for k in range(SSD_CONV):
            rolled[k] = _rolled(x, k - half)
            acc = acc + rolled[k] * w_ref[k:k + 1, :]
        cpre_ref[...] = acc
        act_ref[...] = _silu(acc)
        for r0 in _conv_windows(n, n_ctx):
            rows = slice(r0, r0 + CONV_WIN)
            fix = acc[rows]
            for k in range(SSD_CONV):
                if k != half:
                    fix = fix - rolled[k][rows] * w_ref[k:k + 1, :] * _tap_outside(r0, k - half, n, n_ctx)
            cpre_ref[rows, :] = fix
            act_ref[rows, :] = _silu(fix)

    spec = pl.BlockSpec((n, cb), lambda j: (0, j))
    return pl.pallas_call(
        kern, name=name, grid=(c // cb,),
        in_specs=[spec, pl.BlockSpec((8, cb), lambda j: (0, j)), pl.BlockSpec((1, cb), lambda j: (0, j))],
        out_specs=[spec, spec], out_shape=[jax.ShapeDtypeStruct((n, c), F32)] * 2,
        compiler_params=_params("parallel"),
    )(xp, w8, b)


def _conv_bwd(d1, d2, cpre, xp, w8, *, n_ctx, name, cb=128):
    n, c = xp.shape
    half = SSD_CONV // 2

    def kern(d1_ref, d2_ref, cpre_ref, x_ref, w_ref, dx_ref, dw_ref, db_ref):
        g = (d1_ref[...] + d2_ref[...]) * _dsilu(cpre_ref[...])
        x = x_ref[...]
        dx = jnp.zeros_like(g)
        dw_ref[...] = jnp.zeros_like(dw_ref)
        g_rolled = {}
        for k in range(SSD_CONV):
            s = k - half
            g_rolled[k] = _rolled(g, -s)
            dx = dx + g_rolled[k] * w_ref[k:k + 1, :]
            xr = _rolled(x, s)
            dw = _sum0(g * xr)
            if s != 0:
                for r0 in _conv_windows(n, n_ctx):
                    rows = slice(r0, r0 + CONV_WIN)
                    dw = dw - _sum0(g[rows] * xr[rows] * _tap_outside(r0, s, n, n_ctx))
            dw_ref[k:k + 1, :] = dw
        dx_ref[...] = dx.astype(BF16)
        for r0 in _conv_windows(n, n_ctx):
            rows = slice(r0, r0 + CONV_WIN)
            fix = dx[rows]
            for k in range(SSD_CONV):
                if k != half:
                    fix = fix - g_rolled[k][rows] * w_ref[k:k + 1, :] * _tap_outside(r0, half - k, n, n_ctx)
            dx_ref[rows, :] = fix.astype(BF16)
        db_ref[...] = _sum0(g)

    spec = pl.BlockSpec((n, cb), lambda j: (0, j))
    return pl.pallas_call(
        kern, name=name, grid=(c // cb,),
        in_specs=[spec, spec, spec, spec, pl.BlockSpec((8, cb), lambda j: (0, j))],
        out_specs=[spec, pl.BlockSpec((8, cb), lambda j: (0, j)), pl.BlockSpec((1, cb), lambda j: (0, j))],
        out_shape=[jax.ShapeDtypeStruct((n, c), BF16), jax.ShapeDtypeStruct((8, c), F32),
                   jax.ShapeDtypeStruct((1, c), F32)],
        compiler_params=_params("parallel"),
    )(d1, d2, cpre, xp, w8)


def _chunk_of(s, nc, n_ctx_chunks, rev):
    if not rev:
        return s
    return jnp.where(s < n_ctx_chunks, n_ctx_chunks - 1 - s, nc - 1 - (s - n_ctx_chunks))


def _scan_common(dt_raw, dtT_raw, bias_r, bias_c, alog_r, alog_c, rev):
    ii = lax.broadcasted_iota(jnp.int32, (CHUNK, CHUNK), 0)
    jj = lax.broadcasted_iota(jnp.int32, (CHUNK, CHUNK), 1)
    tri = (jj >= ii) if rev else (jj <= ii)
    tri_t = (ii >= jj) if rev else (ii <= jj)
    a_r = -jnp.exp(alog_r)
    a_c = -jnp.exp(alog_c)
    dt = _softplus(dt_raw + bias_r)
    dt_t = _softplus(dtT_raw + bias_c)
    al = dt * a_r
    acum = _dot(tri.astype(F32), al, precision=HI)
    acum_t = _dot(dt_t * a_c, tri_t.astype(F32), precision=HI)
    atot = _sum0(al)
    return tri, tri_t, a_r, dt, acum, acum_t, atot


def _head_spread():
    return jnp.repeat(jnp.eye(SSD_HEADS, dtype=BF16), SSD_HEAD_DIM, axis=1)


def _dot_sel(v, sel):
    hi = v.astype(BF16)
    lo = (v - hi.astype(F32)).astype(BF16)
    return _dot(hi, sel) + _dot(lo, sel)


def _ssd_scan_fwd(xbc, dt_raw, dtT_raw, bias_r, bias_c, alog_r, alog_c, *, rev, n_ctx_chunks, name):
    n = xbc.shape[0]
    nc = n // CHUNK
    cidx = functools.partial(_chunk_of, nc=nc, n_ctx_chunks=n_ctx_chunks, rev=rev)

    def kern(xs_ref, b_ref, c_ref, dt_ref, dtT_ref, br_ref, bc_ref, ar_ref, ac_ref, e_ref, y_ref, hs_ref, h_scr):
        @pl.when(pl.program_id(0) == 0)
        def _():
            h_scr[...] = jnp.zeros_like(h_scr)

        tri, _, _, dt, acum, acum_t, atot = _scan_common(
            dt_ref[...], dtT_ref[...], br_ref[...], bc_ref[...], ar_ref[...], ac_ref[...], rev)
        etot = jnp.exp(atot)
        spread = lambda v: _dot_sel(v, e_ref[...])
        xdt_all = xs_ref[...] * spread(dt)
        eax = spread(jnp.exp(acum))
        xdw_all = xdt_all * spread(jnp.exp(atot - acum))
        hs_ref[...] = h_scr[...]
        for g in range(SSD_GROUPS):
            gs = slice(g * 256, (g + 1) * 256)
            bg = b_ref[:, g * SSD_STATE:(g + 1) * SSD_STATE].astype(BF16)
            cg = c_ref[:, g * SSD_STATE:(g + 1) * SSD_STATE].astype(BF16)
            cb = _dot(cg, bg, _NT)
            h4 = h_scr[gs, :]
            ys = []
            for k in range(SSD_HPG):
                h = g * SSD_HPG + k
                lmat = jnp.exp(jnp.where(tri, acum[:, h:h + 1] - acum_t[h:h + 1, :], NEG_BIG))
                xdt_h = xdt_all[:, h * SSD_HEAD_DIM:(h + 1) * SSD_HEAD_DIM].astype(BF16)
                ys.append(_dot((cb * lmat).astype(BF16), xdt_h))
            y_ref[:, gs] = jnp.concatenate(ys, axis=1) + _dot(cg, h4.astype(BF16), _NT) * eax[:, gs]
            s4 = _dot(xdw_all[:, gs].astype(BF16), bg, _TN)
            for k in range(SSD_HPG):
                h = g * SSD_HPG + k
                rs = slice(h * SSD_HEAD_DIM, (h + 1) * SSD_HEAD_DIM)
                h_scr[rs, :] = h4[k * SSD_HEAD_DIM:(k + 1) * SSD_HEAD_DIM] * etot[:, h:h + 1] + \
                    s4[k * SSD_HEAD_DIM:(k + 1) * SSD_HEAD_DIM]

    nh = SSD_HEADS
    small = lambda shape: pl.BlockSpec(shape, lambda s: (0, 0))
    return pl.pallas_call(
        kern, name=name, grid=(nc,),
        in_specs=[pl.BlockSpec((CHUNK, SSD_INNER), lambda s: (cidx(s), 0)),
                  pl.BlockSpec((CHUNK, 1024), lambda s: (cidx(s), 2)),
                  pl.BlockSpec((CHUNK, 1024), lambda s: (cidx(s), 3)),
                  pl.BlockSpec((CHUNK, nh), lambda s: (cidx(s), 0)),
                  pl.BlockSpec((nh, CHUNK), lambda s: (0, cidx(s))),
                  small((1, nh)), small((nh, 1)), small((1, nh)), small((nh, 1)), small((nh, SSD_INNER))],
        out_specs=[pl.BlockSpec((CHUNK, SSD_INNER), lambda s: (cidx(s), 0)),
                   pl.BlockSpec((None, SSD_INNER, SSD_STATE), lambda s: (s, 0, 0))],
        out_shape=[jax.ShapeDtypeStruct((n, SSD_INNER), F32),
                   jax.ShapeDtypeStruct((nc, SSD_INNER, SSD_STATE), F32)],
        scratch_shapes=[pltpu.VMEM((SSD_INNER, SSD_STATE), F32)],
        compiler_params=_params("arbitrary"),
    )(xbc, xbc, xbc, dt_raw, dtT_raw, bias_r, bias_c, alog_r, alog_c, _head_spread())


def _ssd_scan_bwd(dy, xbc, hs, dt_raw, dtT_raw, bias_r, bias_c, alog_r, alog_c, dvec, *, rev, n_ctx_chunks,
                  direct, name):
    n = xbc.shape[0]
    nc = n // CHUNK
    nh = SSD_HEADS
    step_of = lambda r: nc - 1 - r
    cidx = lambda r: _chunk_of(step_of(r), nc, n_ctx_chunks, rev)

    def kern(dy_ref, xs_ref, b_ref, c_ref, hs_ref, dt_ref, dtT_ref, br_ref, bc_ref, ar_ref, ac_ref, dv_ref,
             e_ref, et_ref, dx_ref, ddt_ref, dal_ref, dbias_ref, dh_scr):
        @pl.when(pl.program_id(0) == 0)
        def _():
            dh_scr[...] = jnp.zeros_like(dh_scr)
            dal_ref[...] = jnp.zeros_like(dal_ref)
            dbias_ref[...] = jnp.zeros_like(dbias_ref)

        tri, tri_t, a_r, dt, acum, acum_t, atot = _scan_common(
            dt_ref[...], dtT_ref[...], br_ref[...], bc_ref[...], ar_ref[...], ac_ref[...], rev)
        etot = jnp.exp(atot)
        spread = lambda v: _dot_sel(v, e_ref[...])
        gather = lambda v: _dot_sel(v, et_ref[...])
        xs_all = xs_ref[...]
        dy_all = dy_ref[...]
        dtx = spread(dt)
        eax = spread(jnp.exp(acum))
        decx = spread(jnp.exp(atot - acum))
        xdt_all = xs_all * dtx
        xdw_all = xdt_all * decx
        dyo_all = dy_all * eax
        lane = lax.broadcasted_iota(jnp.int32, (CHUNK, nh), 1)
        lane1 = lax.broadcasted_iota(jnp.int32, (1, nh), 1)
        sub = lax.broadcasted_iota(jnp.int32, (nh, CHUNK), 0)
        g_rows = jnp.zeros((CHUNK, nh), F32)
        g_cols = jnp.zeros((nh, CHUNK), F32)
        dtot = jnp.zeros((1, nh), F32)
        q_col, q_e, q_dt = [], [], []
        for g in range(SSD_GROUPS):
            gs = slice(g * 256, (g + 1) * 256)
            bg = b_ref[:, g * SSD_STATE:(g + 1) * SSD_STATE].astype(BF16)
            cg = c_ref[:, g * SSD_STATE:(g + 1) * SSD_STATE].astype(BF16)
            cb = _dot(cg, bg, _NT)
            hs4 = hs_ref[gs, :]
            dh4 = dh_scr[gs, :]
            hs4_bf = hs4.astype(BF16)
            dh4_bf = dh4.astype(BF16)
            dy4 = dy_all[:, gs]
            dy4_bf = dy4.astype(BF16)
            xdt4_bf = xdt_all[:, gs].astype(BF16)
            xdw4 = xdw_all[:, gs]
            xdw4_bf = xdw4.astype(BF16)
            dyo4_bf = dyo_all[:, gs].astype(BF16)
            yoff4 = _dot(cg, hs4_bf, _NT) * eax[:, gs]
            dcg = _dot(dyo4_bf, hs4_bf)
            dh_new4 = _dot(dyo4_bf, cg, _TN)
            bdh4 = _dot(bg, dh4_bf, _NT)
            dbg = _dot(xdw4_bf, dh4_bf)
            e4 = xdw4 * bdh4
            q_col.append(dy4 * yoff4 - e4)
            q_e.append(e4)
            hsum = jnp.sum(dh4 * hs4, axis=1, keepdims=True)
            dcb = jnp.zeros((CHUNK, CHUNK), F32)
            dxdts = []
            for k in range(SSD_HPG):
                h = g * SSD_HPG + k
                ks = slice(k * SSD_HEAD_DIM, (k + 1) * SSD_HEAD_DIM)
                lmat = jnp.exp(jnp.where(tri, acum[:, h:h + 1] - acum_t[h:h + 1, :], NEG_BIG))
                mf = cb * lmat
                dm = _dot(dy4_bf[:, ks], xdt4_bf[:, ks], _NT)
                dcb = dcb + dm * lmat
                gmat = dm * mf
                g_rows = g_rows + jnp.where(lane == h, jnp.sum(gmat, axis=1, keepdims=True), 0.0)
                g_cols = g_cols + jnp.where(sub == h, _sum0(gmat), 0.0)
                dxdts.append(_dot(mf.astype(BF16), dy4_bf[:, ks], _TN))
                et = etot[:, h:h + 1]
                dtot = dtot + jnp.where(lane1 == h, _sum0(hsum[ks]) * et, 0.0)
                dh_scr[h * SSD_HEAD_DIM:(h + 1) * SSD_HEAD_DIM, :] = dh4[ks] * et + dh_new4[ks]
            dxdt4 = jnp.concatenate(dxdts, axis=1) + bdh4 * decx[:, gs]
            q_dt.append(dxdt4 * xs_all[:, gs])
            dx4 = dxdt4 * dtx[:, gs]
            if direct:
                dx4 = dx4 + dy4 * dv_ref[:, gs]
            dcb_bf = dcb.astype(BF16)
            dx_ref[:, gs] = dx4
            dx_ref[:, SSD_INNER + g * SSD_STATE:SSD_INNER + (g + 1) * SSD_STATE] = dbg + _dot(dcb_bf, cg, _TN)
            dx_ref[:, SSD_INNER + 1024 + g * SSD_STATE:SSD_INNER + 1024 + (g + 1) * SSD_STATE] = \
                dcg + _dot(dcb_bf, bg)
        e_heads = gather(jnp.concatenate(q_e, axis=1))
        dacum = gather(jnp.concatenate(q_col, axis=1)) + g_rows - g_cols.T
        dal = _dot(tri_t.astype(F32), dacum, precision=HI) + dtot + _sum0(e_heads)
        ddt = gather(jnp.concatenate(q_dt, axis=1)) + dal * a_r
        ddt_raw = ddt * _sig(dt_ref[...] + br_ref[...])
        ddt_ref[...] = ddt_raw
        dal_ref[...] += _sum0(dal * dt) * a_r
        dbias_ref[...] += _sum0(ddt_raw)

    small = lambda shape: pl.BlockSpec(shape, lambda r: (0, 0))
    return pl.pallas_call(
        kern, name=name, grid=(nc,),
        in_specs=[pl.BlockSpec((CHUNK, SSD_INNER), lambda r: (cidx(r), 0)),
                  pl.BlockSpec((CHUNK, SSD_INNER), lambda r: (cidx(r), 0)),
                  pl.BlockSpec((CHUNK, 1024), lambda r: (cidx(r), 2)),
                  pl.BlockSpec((CHUNK, 1024), lambda r: (cidx(r), 3)),
                  pl.BlockSpec((None, SSD_INNER, SSD_STATE), lambda r: (step_of(r), 0, 0)),
                  pl.BlockSpec((CHUNK, nh), lambda r: (cidx(r), 0)),
                  pl.BlockSpec((nh, CHUNK), lambda r: (0, cidx(r))),
                  small((1, nh)), small((nh, 1)), small((1, nh)), small((nh, 1)), small((1, SSD_INNER)),
                  small((nh, SSD_INNER)), small((SSD_INNER, nh))],
        out_specs=[pl.BlockSpec((CHUNK, SSD_CONV_DIM), lambda r: (cidx(r), 0)),
                   pl.BlockSpec((CHUNK, nh), lambda r: (cidx(r), 0)),
                   small((1, nh)), small((1, nh))],
        out_shape=[jax.ShapeDtypeStruct((n, SSD_CONV_DIM), F32), jax.ShapeDtypeStruct((n, nh), F32),
                   jax.ShapeDtypeStruct((1, nh), F32), jax.ShapeDtypeStruct((1, nh), F32)],
        scratch_shapes=[pltpu.VMEM((SSD_INNER, SSD_STATE), F32)],
        compiler_params=_params("arbitrary"),
    )(dy, xbc, xbc, xbc, hs, dt_raw, dtT_raw, bias_r, bias_c, alog_r, alog_c, dvec, _head_spread(),
      _head_spread().T)


def _gm_spatial_fwd(gu, gvn, ws, bst, *, name):
    n = gu.shape[0]

    def kern(gu_ref, gv_ref, ws_ref, bs_ref, o_ref):
        for g in range(GM_GROUPS):
            sl = slice(g * GM_GROUP_DIM, (g + 1) * GM_GROUP_DIM)
            s = _dot(ws_ref[g], gv_ref[:, sl]) + bs_ref[:, g:g + 1]
            o_ref[:, sl] = (gu_ref[:, sl] * s).astype(BF16)

    spec = pl.BlockSpec((CHUNK, GM_INNER), lambda i: (i, 0))
    return pl.pallas_call(
        kern, name=name, grid=(n // CHUNK,),
        in_specs=[spec, spec, pl.BlockSpec(ws.shape, lambda i: (0, 0, 0)), pl.BlockSpec(bst.shape, lambda i: (0, 0))],
        out_specs=spec, out_shape=jax.ShapeDtypeStruct((n, GM_INNER), BF16),
        compiler_params=_params("parallel"),
    )(gu, gvn, ws, bst)


def _gm_spatial_bwd(dt, gu, gvn, ws, wst, bst, *, name):
    n = gu.shape[0]

    def kern(dt_ref, gu_ref, gv_ref, ws_ref, wst_ref, bs_ref, dgu_ref, dgv_ref, dws_ref, dbs_ref):
        @pl.when(pl.program_id(0) == 0)
        def _():
            dws_ref[...] = jnp.zeros_like(dws_ref)
            dbs_ref[...] = jnp.zeros_like(dbs_ref)

        lane = lax.broadcasted_iota(jnp.int32, (CHUNK, GM_GROUPS), 1)
        dbs = jnp.zeros((CHUNK, GM_GROUPS), F32)
        for g in range(GM_GROUPS):
            sl = slice(g * GM_GROUP_DIM, (g + 1) * GM_GROUP_DIM)
            gv = gv_ref[:, sl]
            s = _dot(ws_ref[g], gv) + bs_ref[:, g:g + 1]
            d = dt_ref[:, sl]
            dgu_ref[:, sl] = d * s
            ds = d * gu_ref[:, sl]
            ds_bf = ds.astype(BF16)
            dws_ref[g] += _dot(ds_bf, gv, _NT)
            dgv_ref[:, sl] = _dot(wst_ref[g], ds_bf)
            dbs = dbs + jnp.where(lane == g, jnp.sum(ds, axis=1, keepdims=True), 0.0)
        dbs_ref[...] += dbs

    spec = pl.BlockSpec((CHUNK, GM_INNER), lambda i: (i, 0))
    wspec = pl.BlockSpec(ws.shape, lambda i: (0, 0, 0))
    bspec = pl.BlockSpec(bst.shape, lambda i: (0, 0))
    return pl.pallas_call(
        kern, name=name, grid=(n // CHUNK,),
        in_specs=[spec, spec, spec, wspec, wspec, bspec],
        out_specs=[spec, spec, wspec, bspec],
        out_shape=[jax.ShapeDtypeStruct((n, GM_INNER), F32), jax.ShapeDtypeStruct((n, GM_INNER), F32),
                   jax.ShapeDtypeStruct(ws.shape, F32), jax.ShapeDtypeStruct(bst.shape, F32)],
        compiler_params=_params("arbitrary"),
    )(dt, gu, gvn, ws, wst, bst)


def _adamw(parts, w, m, v, *, name, tm=256, sel=(), into=None):
    ns, r, wd = parts.shape
    tm = _pick(r, tm, 8)
    tc = wd
    if tm < 64 and wd % 256 == 0:
        tm, tc = r, 256
    lead = len(sel)
    assert w.shape[lead:] == (r, wd) and lead == w.ndim - 2

    def kern(*refs):
        p_ref, w_ref, m_ref, v_ref = refs[:4]
        g_ref, d_ref, nm_ref, nv_ref = refs[-4:]
        g = p_ref[0].astype(F32)
        for s in range(1, ns):
            g = g + p_ref[s].astype(F32)
        m2 = ADAM_B1 * m_ref[...] + (1.0 - ADAM_B1) * g
        v2 = ADAM_B2 * v_ref[...] + (1.0 - ADAM_B2) * (g * g)
        m_hat = m2 / (1.0 - ADAM_B1 ** ADAM_STEP)
        v_hat = v2 / (1.0 - ADAM_B2 ** ADAM_STEP)
        g_ref[...] = g
        d_ref[...] = -ADAM_LR * (m_hat / (jnp.sqrt(v_hat) + ADAM_EPS) + ADAM_WD * w_ref[...])
        nm_ref[...] = m2
        nv_ref[...] = v2

    spec = pl.BlockSpec((None,) * lead + (tm, tc), lambda i, j: tuple(sel) + (i, j))
    extra, aliases = [], {}
    if into is not None:
        extra = list(into)
        aliases = {4 + k: k for k in range(4)}
    return pl.pallas_call(
        kern, name=name, grid=(r // tm, wd // tc),
        in_specs=[pl.BlockSpec((ns, tm, tc), lambda i, j: (0, i, j)), spec, spec, spec] +
                 [pl.BlockSpec(memory_space=pl.ANY)] * len(extra),
        out_specs=[spec] * 4, out_shape=[jax.ShapeDtypeStruct(w.shape, F32)] * 4,
        input_output_aliases=aliases,
        compiler_params=_params("parallel", "parallel"),
    )(parts, w, m, v, *extra)


def _sum_slots(parts, *, name, scale_by=None):
    ns, r, wd = parts.shape

    def kern(*refs):
        p_ref, o_ref = refs[0], refs[-1]
        g = p_ref[0]
        for s in range(1, ns):
            g = g + p_ref[s]
        if scale_by is not None:
            g = g * _dsilu(refs[1][...])
        o_ref[...] = g

    args = [parts] + ([] if scale_by is None else [scale_by])
    return pl.pallas_call(kern, name=name, out_shape=jax.ShapeDtypeStruct((r, wd), F32),
                          compiler_params=pltpu.CompilerParams(vmem_limit_bytes=VMEM_LIMIT_BYTES))(*args)


def _mesh_pos():
    x, y, c = lax.axis_index("x"), lax.axis_index("y"), lax.axis_index("c")
    return x, y, c, 4 * x + 2 * y + c


def _flip(x, y, c, f):
    fx, fy, fc = (f >> 2) & 1, (f >> 1) & 1, f & 1
    px = 1 - x if fx else x
    py = 1 - y if fy else y
    pc = 1 - c if fc else c
    return (px, py, pc), 4 * px + 2 * py + pc


_HBM_SPEC = pl.BlockSpec(memory_space=pltpu.HBM)


def _exchange(arrays, *, scatter, name):
    na = len(arrays)
    if scatter:
        out_shape = [jax.ShapeDtypeStruct(a.shape, a.dtype) for a in arrays]
    else:
        out_shape = [jax.ShapeDtypeStruct((NDEV,) + a.shape, a.dtype) for a in arrays]

    out_shape.append(jax.ShapeDtypeStruct((8, 128), F32))

    def body(*refs):
        ins, outs = refs[:na], refs[na:2 * na]
        send_sems, recv_sems, local_sems = refs[2 * na + 1:]
        refs[2 * na][...] = jnp.zeros((8, 128), F32)
        x, y, c, me = _mesh_pos()
        copies = []
        for i in range(na):
            src_own = ins[i].at[me] if scatter else ins[i]
            lc = pltpu.make_async_copy(src_own, outs[i].at[me], local_sems.at[i])
            lc.start()
            copies.append(lc)
        sends = []
        for f in range(1, NDEV):
            peer, pidx = _flip(x, y, c, f)
            for i in range(na):
                k = i * (NDEV - 1) + f - 1
                src = ins[i].at[pidx] if scatter else ins[i]
                cp = pltpu.make_async_remote_copy(
                    src_ref=src, dst_ref=outs[i].at[me], send_sem=send_sems.at[k], recv_sem=recv_sems.at[k],
                    device_id=peer, device_id_type=pl.DeviceIdType.MESH)
                cp.start()
                sends.append(cp)
        for f in range(1, NDEV):
            peer, pidx = _flip(x, y, c, f)
            for i in range(na):
                k = i * (NDEV - 1) + f - 1
                src = ins[i].at[pidx] if scatter else ins[i]
                pltpu.make_async_remote_copy(
                    src_ref=src, dst_ref=outs[i].at[pidx], send_sem=send_sems.at[k], recv_sem=recv_sems.at[k],
                    device_id=peer, device_id_type=pl.DeviceIdType.MESH).wait_recv()
        for cp in sends:
            cp.wait_send()
        for lc in copies:
            lc.wait()

    res = pl.pallas_call(
        body, name=name, out_shape=out_shape, in_specs=[_HBM_SPEC] * na,
        out_specs=[_HBM_SPEC] * na + [pl.BlockSpec(memory_space=pltpu.VMEM)],
        scratch_shapes=[pltpu.SemaphoreType.DMA((na * (NDEV - 1),)), pltpu.SemaphoreType.DMA((na * (NDEV - 1),)),
                        pltpu.SemaphoreType.DMA((na,))],
        compiler_params=pltpu.CompilerParams(has_side_effects=True),
    )(*arrays)
    return res[:na], res[na][0, 0]


_SEM_SPEC = pl.BlockSpec(memory_space=pltpu.SEMAPHORE)
_DATAFLOW = pltpu.SideEffectType.DATAFLOW_SIDE_EFFECTING


def _split_copies(srcs, lands, send_sems, recv_sems, scatter, arriving):
    x, y, c, me = _mesh_pos()
    copies = []
    for i in range(len(srcs)):
        for f in range(1, NDEV):
            peer, pidx = _flip(x, y, c, f)
            k = i * (NDEV - 1) + f - 1
            copies.append(pltpu.make_async_remote_copy(
                src_ref=srcs[i].at[pidx] if scatter else srcs[i], dst_ref=lands[i].at[pidx if arriving else me],
                send_sem=send_sems.at[k], recv_sem=recv_sems.at[k], device_id=peer,
                device_id_type=pl.DeviceIdType.MESH))
    return copies


def _exchange_start(srcs, lands, *, scatter, name):
    na = len(srcs)
    nsem = na * (NDEV - 1)

    def body(*refs):
        ins_src, ins_land = refs[:na], refs[na:2 * na]
        send_sems, recv_sems = refs[2 * na], refs[2 * na + 1]
        token = refs[-1]
        for cp in _split_copies(ins_src, ins_land, send_sems, recv_sems, scatter, False):
            cp.start()
        token[...] = jnp.zeros_like(token)

    thru = [pltpu.HBM(a.shape, a.dtype) for a in list(srcs) + list(lands)]
    res = pl.pallas_call(
        body, name=name,
        out_shape=(pltpu.SemaphoreType.DMA((nsem,)), pltpu.SemaphoreType.DMA((nsem,)), *thru,
                   jax.ShapeDtypeStruct((8, 128), F32)),
        in_specs=[_HBM_SPEC] * (2 * na),
        out_specs=(_SEM_SPEC, _SEM_SPEC, *([_HBM_SPEC] * (2 * na)), pl.BlockSpec(memory_space=pltpu.VMEM)),
        input_output_aliases={i: 2 + i for i in range(2 * na)},
        compiler_params=pltpu.CompilerParams(has_side_effects=_DATAFLOW),
    )(*[pltpu.with_memory_space_constraint(a, pltpu.HBM) for a in list(srcs) + list(lands)])
    send_sems, recv_sems = res[0], res[1]
    return send_sems, recv_sems, res[2:2 + na], res[2 + na:2 + 2 * na], res[-1][0, 0]


def _exchange_wait(send_sems, recv_sems, srcs, lands, after, *, scatter, name):
    na = len(srcs)

    def body(*refs):
        ins_src, ins_land = refs[:na], refs[na:2 * na]
        s_sems, r_sems = refs[2 * na], refs[2 * na + 1]
        for cp in _split_copies(ins_src, ins_land, s_sems, r_sems, scatter, False):
            cp.wait_send()
        for cp in _split_copies(ins_src, ins_land, s_sems, r_sems, scatter, True):
            cp.wait_recv()

    thru = [pltpu.HBM(a.shape, a.dtype) for a in list(srcs) + list(lands)]
    res = pl.pallas_call(
        body, name=name, out_shape=tuple(thru),
        in_specs=[_HBM_SPEC] * (2 * na) + [_SEM_SPEC, _SEM_SPEC, pl.BlockSpec(memory_space=pl.ANY)],
        out_specs=tuple([_HBM_SPEC] * (2 * na)),
        input_output_aliases={i: i for i in range(2 * na)},
        compiler_params=pltpu.CompilerParams(has_side_effects=_DATAFLOW),
    )(*srcs, *lands, send_sems, recv_sems, after)
    return res[na:]


def _landing(block, me):
    buf = lax.empty((NDEV,) + block.shape, block.dtype)
    return lax.dynamic_update_slice_in_dim(buf, block[None], me, axis=0)


def _seg_kw(nseg, n_ctx, tm):
    return dict(nseg=nseg, seg_blocks=(n_ctx // tm if nseg == 2 else 0))


def _ffn_fwd(tag, h, gpre, gpost, shift, scale, gate, w, *, nseg, n_ctx, tm):
    n = h.shape[0]
    kw = _seg_kw(nseg, n_ctx, tm)
    (u,) = _rowwise(tag + "_pre", _pre_fwd_fn, n, [h], [("full", gpre), ("seg", shift), ("seg", scale)],
                    [(D_MODEL, BF16)], tm=tm, **kw)
    if "early" in w:
        w.update(w.pop("early")(u))
    s, a, b = _mm_glu(u, w["win_t"], name=tag + "_glu")
    if "late" in w:
        w.update(w.pop("late")(s))
    y, ho = _mm_rows(s, w["wout"], functools.partial(_out_post_fn, 0.5), [h], [("full", gpost), ("seg", gate)],
                     [(D_MODEL, F32), (D_MODEL, F32)], name=tag + "_out", tk=FFN_DIM, n_ctx=n_ctx)
    return ho, dict(h=h, u=u, s=s, a=a, b=b, y=y)


def _ffn_bwd(tag, dho, sv, gpre, gpost, scale, gate, w, put, *, nseg, n_ctx, tm):
    n = dho.shape[0]
    kw = _seg_kw(nseg, n_ctx, tm)
    dy, dgate, dgpost = _rowwise(tag + "_postb", functools.partial(_post_bwd_fn, 0.5), n, [dho, sv["y"]],
                                 [("full", gpost), ("seg", gate)], [(D_MODEL, BF16)], [D_MODEL, D_MODEL], tm=tm, **kw)
    tok = put("w_out", _mm_tn(sv["s"], dy, name=tag + "_dwout", tm=1408, tn=1024, col_blocks=1))
    dp = _mm_glu_bwd(dy, w["wout"], sv["a"], sv["b"], name=tag + "_ds")
    tok2 = put("w_in", _mm_tn(dp, sv["u"], name=tag + "_dwin", tm=1408, tn=1024, col_blocks=1))
    for t in (tok, tok2):
        if t is not None:
            gpre = gpre + t
    dh, dshift, dscale, dgpre = _mm_rows(dp, w["win_t"], _pre_bwd_fn, [sv["h"], dho],
                                         [("full", gpre), ("seg", scale)], [(D_MODEL, F32)],
                                         [D_MODEL, D_MODEL, D_MODEL], name=tag + "_du", tk=FFN_DIM, n_ctx=n_ctx)
    return dh, None, dict(shift=dshift, scale=dscale, gate=dgate, gpre=dgpre, gpost=dgpost)


def _local_step(x, ctx, target, mods, norm_g, get_w, small, put_grad):
    t_len, n_ctx = x.shape[0], ctx.shape[0]
    n0 = t_len + n_ctx
    tm0 = _pick(n_ctx, 256, 8)
    tm1 = _pick(t_len, 512, 8)
    ncc = n_ctx // CHUNK
    g = {}

    def modrow(i, k, nseg):
        mc, mx = mods[i]
        if nseg == 2:
            return jnp.stack([mc[k], mx[k]])[:, None, :]
        return mx[k][None, None, :]

    pending = [None]

    def gvec(i, k):
        v = norm_g[i, k][None, :]
        if pending[0] is not None:
            v = v + pending[0]
            pending[0] = None
        return v

    xc = jnp.concatenate([ctx, x], axis=0)
    L0 = dict(nseg=2, n_ctx=n_ctx, tm=tm0)
    wts = dict(get_w("ffn00", xc))
    h1, sv_f01 = _ffn_fwd("l0f1", xc, gvec(0, 0), gvec(0, 1), modrow(0, 0, 2), modrow(0, 1, 2), modrow(0, 2, 2),
                          wts["ffn00"], **L0)
    kw0 = _seg_kw(2, n_ctx, tm0)
    (um0,) = _rowwise("l0m_pre", _pre_fwd_fn, n0, [h1], [("full", gvec(0, 2)), ("seg", modrow(0, 3, 2)),
                                                         ("seg", modrow(0, 4, 2))], [(D_MODEL, BF16)], tm=tm0, **kw0)
    wts.update(get_w("ssd", um0))
    win_ssd = wts["ssd_win_t"]
    nh = SSD_HEADS
    dt_blk = (SSD_INNER + SSD_CONV_DIM) // (2 * nh)
    z = _mm(um0, win_ssd, out_dtype=F32, name="ssd_z", rhs_t=True, n=SSD_INNER)
    xbc_pre = _mm(um0, win_ssd, out_dtype=F32, name="ssd_xbc", rhs_t=True, n=SSD_CONV_DIM,
                  b_off=(SSD_INNER // 1024, 0))
    dtr = _mm(um0, win_ssd, out_dtype=F32, name="ssd_dt", rhs_t=True, n=2 * nh, b_off=(dt_blk, 0))
    cpre, xbc = _conv_fwd(xbc_pre, small["conv_w8"], small["conv_b"], n_ctx=n_ctx, name="ssd_conv")
    nh = SSD_HEADS
    dt_dir = [dtr[:, :nh], dtr[:, nh:2 * nh]]
    dtT_dir = [d.T for d in dt_dir]
    bias_r = [small["dt_bias"][d][None, :] for d in range(2)]
    bias_c = [small["dt_bias"][d][:, None] for d in range(2)]
    alog_r = [small["a_log"][d][None, :] for d in range(2)]
    alog_c = [small["a_log"][d][:, None] for d in range(2)]
    ys, hss = [], []
    for d in range(2):
        yd, hsd = _ssd_scan_fwd(xbc, dt_dir[d], dtT_dir[d], bias_r[d], bias_c[d], alog_r[d], alog_c[d],
                                rev=(d == 1), n_ctx_chunks=ncc, name=f"ssd_scan{d}")
        ys.append(yd)
        hss.append(hsd)
    dvec = jnp.repeat(small["ssd_d"], SSD_HEAD_DIM)[None, :]
    ngv = small["ssd_norm_g"][None, :]
    gate_rows = [ys[0], ys[1], (xbc, SSD_INNER, 0, 0), z]
    off = n_ctx // tm0
    lat = lambda r: (r[0], r[1], r[2], off) if isinstance(r, tuple) else (r, r.shape[1], 0, off)
    (yn,) = _rowwise("ssd_gate", _ssdgate_fwd_fn, t_len, [lat(r) for r in gate_rows],
                     [("full", dvec), ("full", ngv)], [(SSD_INNER, BF16)], tm=tm0)
    h1x = h1[n_ctx:]
    L1 = dict(nseg=1, n_ctx=0, tm=_pick(t_len, 512, 8))
    if "late" in wts:
        wts.update(wts.pop("late")(yn))
    yo0, h2 = _mm_rows(yn, wts["ssd_wout"], functools.partial(_out_post_fn, 1.0), [h1x],
                       [("full", gvec(0, 3)), ("seg", modrow(0, 5, 1))], [(D_MODEL, F32), (D_MODEL, F32)],
                       name="ssd_out", tk=SSD_INNER)
    wts.update(get_w("ffn01", h2))
    h3, sv_f02 = _ffn_fwd("l0f2", h2, gvec(0, 4), gvec(0, 5), modrow(0, 6, 1), modrow(0, 7, 1), modrow(0, 8, 1),
                          wts["ffn01"], **L1)

    wts.update(get_w("ffn10", h3))
    h4, sv_f11 = _ffn_fwd("l1f1", h3, gvec(1, 0), gvec(1, 1), modrow(1, 0, 1), modrow(1, 1, 1), modrow(1, 2, 1),
                          wts["ffn10"], **L1)
    (um1,) = _rowwise("l1m_pre", _pre_fwd_fn, t_len, [h4], [("full", gvec(1, 2)), ("seg", modrow(1, 3, 1)),
                                                            ("seg", modrow(1, 4, 1))], [(D_MODEL, BF16)], tm=tm1)
    wts.update(get_w("gm", um1))
    p1 = _mm(um1, wts["gm_win"], out_dtype=F32, name="gm_in", tm=2048)
    vg = small["gm_v_g"][None, :]
    vb = small["gm_v_b"][None, :]
    gu, gvn = _rowwise("gm_act", _gm_act_fwd_fn, t_len, [p1], [("full", vg), ("full", vb)],
                       [(GM_INNER, F32), (GM_INNER, BF16)], tm=256)
    ws_bf = small["gm_w_s"].astype(BF16)
    wst_bf = jnp.swapaxes(small["gm_w_s"], 1, 2).astype(BF16)
    bst = small["gm_b_s"].T
    tgm = _gm_spatial_fwd(gu, gvn, ws_bf, bst, name="gm_spatial")
    yo1, h5 = _mm_rows(tgm, wts["gm_wout"], functools.partial(_out_post_fn, 1.0), [h4],
                       [("full", gvec(1, 3)), ("seg", modrow(1, 5, 1))], [(D_MODEL, F32), (D_MODEL, F32)],
                       name="gm_out", tk=GM_INNER)
    wts.update(get_w("ffn11", h5))
    h6, sv_f12 = _ffn_fwd("l1f2", h5, gvec(1, 4), gvec(1, 5), modrow(1, 6, 1), modrow(1, 7, 1), modrow(1, 8, 1),
                          wts["ffn11"], **L1)

    dh, loss_parts = _rowwise("loss", _loss_fn, t_len, [h6, target], [], [(D_MODEL, F32)], [D_MODEL], tm=tm1)

    zero = jnp.zeros((D_MODEL,), F32)
    dmx = [[zero] * N_MOD for _ in range(2)]
    dmc = [[zero] * N_MOD for _ in range(2)]
    dng = [[zero] * 6 for _ in range(2)]

    def put_mod(i, k, acc):
        if acc.shape[0] == 2:
            dmc[i][k] = dmc[i][k] + acc[0, 0]
            dmx[i][k] = dmx[i][k] + acc[1, 0]
        else:
            dmx[i][k] = dmx[i][k] + acc[0, 0]

    def put_g(i, k, acc):
        dng[i][k] = dng[i][k] + jnp.sum(acc[:, 0], axis=0)

    def ffn_back(tag, i, j, dho, sv, w, lay):
        nseg = lay["nseg"]
        base = 0 if j == 0 else 6
        gi = 0 if j == 0 else 4
        dh_in, pending[0], s = _ffn_bwd(tag, dho, sv, gvec(i, gi), gvec(i, gi + 1), modrow(i, base + 1, nseg),
                                        modrow(i, base + 2, nseg), w, functools.partial(put_grad, f"ffn{i}{j}"), **lay)
        put_mod(i, base, s["shift"])
        put_mod(i, base + 1, s["scale"])
        put_mod(i, base + 2, s["gate"])
        put_g(i, gi, s["gpre"])
        put_g(i, gi + 1, s["gpost"])
        return dh_in

    dh = ffn_back("l1f2", 1, 1, dh, sv_f12, wts["ffn11"], L1)
    dyo, dgate, dgp = _rowwise("l1m_postb", functools.partial(_post_bwd_fn, 1.0), t_len, [dh, yo1],
                               [("full", gvec(1, 3)), ("seg", modrow(1, 5, 1))], [(D_MODEL, BF16)],
                               [D_MODEL, D_MODEL], tm=tm1)
    put_mod(1, 5, dgate)
    put_g(1, 3, dgp)
    put_grad("gm", "w_out", _mm_tn(tgm, dyo, name="gm_dwout", tn=1024, col_blocks=1))
    dtg = _mm(dyo, wts["gm_wout"], out_dtype=F32, name="gm_dt", rhs_t=True)
    dgu, dgvn, dws, dbst = _gm_spatial_bwd(dtg, gu, gvn, ws_bf, wst_bf, bst, name="gm_spatialb")
    g["gm_w_s"] = dws
    g["gm_b_s"] = dbst.T
    dp1, dvg, dvb = _rowwise("gm_actb", _gm_act_bwd_fn, t_len, [p1, dgu, dgvn], [("full", vg)],
                             [(2 * GM_INNER, BF16)], [GM_INNER, GM_INNER], tm=256)
    g["gm_v_g"] = dvg[0, 0]
    g["gm_v_b"] = dvb[0, 0]
    pending[0] = put_grad("gm", "w_in", _mm_tn(um1, dp1, name="gm_dwin", tm=1024, col_blocks=NDEV))
    dh, dsh, dsc, dgp = _mm_rows(dp1, wts["gm_win"], _pre_bwd_fn, [h4, dh],
                                 [("full", gvec(1, 2)), ("seg", modrow(1, 4, 1))], [(D_MODEL, F32)],
                                 [D_MODEL, D_MODEL, D_MODEL], name="gm_dum", tk=2048, rhs_t=True)
    put_mod(1, 3, dsh)
    put_mod(1, 4, dsc)
    put_g(1, 2, dgp)
    dh = ffn_back("l1f1", 1, 0, dh, sv_f11, wts["ffn10"], L1)

    dh = ffn_back("l0f2", 0, 1, dh, sv_f02, wts["ffn01"], L1)
    dyo, dgate, dgp = _rowwise("l0m_postb", functools.partial(_post_bwd_fn, 1.0), t_len, [dh, yo0],
                               [("full", gvec(0, 3)), ("seg", modrow(0, 5, 1))], [(D_MODEL, BF16)],
                               [D_MODEL, D_MODEL], tm=tm1)
    put_mod(0, 5, dgate)
    put_g(0, 3, dgp)
    tok = put_grad("ssd", "w_out", _mm_tn(yn, dyo, name="ssd_dwout", tn=1024, col_blocks=1))
    dyn = _mm(dyo, wts["ssd_wout"], out_dtype=F32, name="ssd_dyn", rhs_t=True)
    dy_ssd, dz, dngv, ddv = _rowwise("ssd_gateb", _ssdgate_bwd_fn, n0,
                                     [(dyn, SSD_INNER, 0, -(n_ctx // tm0))] + gate_rows,
                                     [("full", dvec), ("full", ngv if tok is None else ngv + tok)],
                                     [(SSD_INNER, F32), (SSD_INNER, BF16)],
                                     [SSD_INNER, SSD_INNER], tm=tm0)
    g["ssd_norm_g"] = dngv[0, 0]
    g["ssd_D"] = jnp.sum(ddv[0, 0].reshape(SSD_HEADS, SSD_HEAD_DIM), axis=1)
    dxbcs, ddts, dalogs, dbiases = [], [], [], []
    for d in range(2):
        dxd, ddtd, dal, dbi = _ssd_scan_bwd(dy_ssd, xbc, hss[d], dt_dir[d], dtT_dir[d], bias_r[d], bias_c[d],
                                            alog_r[d], alog_c[d], dvec, rev=(d == 1), n_ctx_chunks=ncc,
                                            direct=(d == 0), name=f"ssd_scanb{d}")
        dxbcs.append(dxd)
        ddts.append(ddtd)
        dalogs.append(dal[0])
        dbiases.append(dbi[0])
    g["ssd_A_log"] = jnp.stack(dalogs)
    g["ssd_dt_bias"] = jnp.stack(dbiases)
    dxbc_pre, dcw8, dcb = _conv_bwd(dxbcs[0], dxbcs[1], cpre, xbc_pre, small["conv_w8"], n_ctx=n_ctx, name="ssd_convb")
    g["ssd_conv_w"] = dcw8[:SSD_CONV]
    g["ssd_conv_b"] = dcb[0]
    ddt_bf = jnp.concatenate([ddts[0], ddts[1]], axis=1).astype(BF16)
    n_in = SSD_INNER + SSD_CONV_DIM + 2 * nh
    dw_t = _mm_tn(dz, um0, name="ssd_dwz", col_blocks=1, stack=(n_in, 0, None))
    dw_t = _mm_tn(dxbc_pre, um0, name="ssd_dwxbc", col_blocks=1, stack=(n_in, SSD_INNER, dw_t))
    dw_t = _mm_tn(ddt_bf, um0, name="ssd_dwdt", col_blocks=1, stack=(n_in, SSD_INNER + SSD_CONV_DIM, dw_t))
    pending[0] = put_grad("ssd", "w_in", dw_t)
    dum0 = _mm(dz, win_ssd, out_dtype=F32, name="ssd_dum_z", tk=SSD_INNER, n=D_MODEL)
    dum0 = _mm(dxbc_pre, win_ssd, out_dtype=F32, name="ssd_dum_x", tk=SSD_INNER, n=D_MODEL,
               b_off=(SSD_INNER // SSD_INNER, 0), add=dum0)
    dum0 = _mm(ddt_bf, win_ssd, out_dtype=F32, name="ssd_dum_dt", tk=2 * nh, n=D_MODEL, b_off=(dt_blk, 0), add=dum0)
    dh0, dsh, dsc, dgp = _rowwise("l0m_preb", _pre_bwd_fn, n0, [dum0, h1, (dh, D_MODEL, 0, -(n_ctx // tm0))],
                                  [("full", gvec(0, 2)), ("seg", modrow(0, 4, 2))], [(D_MODEL, F32)],
                                  [D_MODEL, D_MODEL, D_MODEL], tm=tm0, **kw0)
    put_mod(0, 3, dsh)
    put_mod(0, 4, dsc)
    put_g(0, 2, dgp)
    dh0 = ffn_back("l0f1", 0, 0, dh0, sv_f01, wts["ffn00"], L0)
    grad_x = dh0[n_ctx:]
    g["norm_g"] = jnp.stack([jnp.stack(r) for r in dng])
    g["dmx"] = jnp.stack([jnp.concatenate(r) for r in dmx])
    g["dmc"] = jnp.stack([jnp.concatenate(r) for r in dmc])
    return loss_parts[0], grad_x, g


GROUPS = ("ffn00", "ssd", "ffn01", "ffn10", "gm", "ffn11")


TRANSPOSED_IN = ("ffn", "ssd")


def _is_transposed(group):
    return group.startswith(TRANSPOSED_IN)


def _mats_in(group, win_l):
    if _is_transposed(group):
        return {("win_t" if group.startswith("ffn") else group + "_win_t"): win_l.reshape(-1, win_l.shape[2])}
    return {group + "_win": win_l}


def _mats_out(group, wout_l):
    pre = "" if group.startswith("ffn") else group + "_"
    return {pre + "wout": wout_l.reshape(-1, wout_l.shape[2])}


def _group_mats(group, lands):
    m = {**_mats_in(group, lands[0]), **_mats_out(group, lands[1])}
    return {group: m} if group.startswith("ffn") else m


def _grad_blocks(which, grad):
    if grad.ndim == 3:
        return grad if grad.shape[0] == NDEV else grad.reshape(NDEV, grad.shape[1] // NDEV, grad.shape[2])
    if which == "w_in":
        k, n = grad.shape
        return jnp.transpose(grad.reshape(k, NDEV, n // NDEV), (1, 0, 2)).astype(BF16)
    return grad.reshape(NDEV, grad.shape[0] // NDEV, grad.shape[1]).astype(BF16)


def kernel(x, c, ctx, c_ctx, ada_w, ada_b, norm_g, ffn_w_in, ffn_w_out, ssd_w_in, ssd_conv_w, ssd_conv_b, ssd_dt_bias, ssd_A_log, ssd_D, ssd_norm_g, ssd_w_out, gm_w_in, gm_v_g, gm_v_b, gm_w_s, gm_b_s, gm_w_out, loss_target, m_c_ctx, m_ada_w, m_ada_b, m_norm_g, m_ffn_w_in, m_ffn_w_out, m_ssd_w_in, m_ssd_conv_w, m_ssd_conv_b, m_ssd_dt_bias, m_ssd_A_log, m_ssd_D, m_ssd_norm_g, m_ssd_w_out, m_gm_w_in, m_gm_v_g, m_gm_v_b, m_gm_w_s, m_gm_b_s, m_gm_w_out, v_c_ctx, v_ada_w, v_ada_b, v_norm_g, v_ffn_w_in, v_ffn_w_out, v_ssd_w_in, v_ssd_conv_w, v_ssd_conv_b, v_ssd_dt_bias, v_ssd_A_log, v_ssd_D, v_ssd_norm_g, v_ssd_w_out, v_gm_w_in, v_gm_v_g, v_gm_v_b, v_gm_w_s, v_gm_b_s, v_gm_w_out):
    me = 4 * lax.axis_index("x") + 2 * lax.axis_index("y") + lax.axis_index("c")
    d = D_MODEL
    ncol = N_MOD * d // NDEV

    small_pack = jnp.concatenate([c.reshape(-1), norm_g.reshape(-1), ssd_conv_w.reshape(-1),
                                  gm_v_g.reshape(-1), gm_v_b.reshape(-1)])[None, :]
    (sp,), _ = _exchange([small_pack], scatter=False, name="gather_small")
    sp = sp[:, 0]
    o = 0
    c_all = sp[:, o:o + d]; o += d
    ng_all = sp[:, o:o + 2 * 6 * 128].reshape(NDEV, 2, 6, 128); o += 2 * 6 * 128
    cw_all = sp[:, o:o + SSD_CONV * 512].reshape(NDEV, SSD_CONV, 512); o += SSD_CONV * 512
    vg_all = sp[:, o:o + 256]; o += 256
    vb_all = sp[:, o:o + 256]; o += 256
    norm_g_full = jnp.transpose(ng_all, (1, 2, 0, 3)).reshape(2, 6, d)
    conv_w_full = jnp.transpose(cw_all, (1, 0, 2)).reshape(SSD_CONV, SSD_CONV_DIM)
    gm_v_g_full = vg_all.reshape(-1)
    gm_v_b_full = vb_all.reshape(-1)

    c16 = jnp.concatenate([c_all, jnp.broadcast_to(c_ctx[None, :], (NDEV, d))], axis=0)
    ada_b_loc = lax.dynamic_slice_in_dim(ada_b, me * ncol, ncol, axis=1)
    mods_loc = jnp.stack([_mm_f32(c16, ada_w[i], name=f"ada_mod{i}", silu_a=True, bias=ada_b_loc[i][None, :])
                          for i in range(2)])
    (mods_all,), mods_done = _exchange([mods_loc], scatter=False, name="gather_mods")

    tr = lambda a: jnp.swapaxes(a, -1, -2)
    shard = {"ssd": (tr(ssd_w_in)[0], ssd_w_out[0]), "gm": (gm_w_in[0], gm_w_out[0])}
    for i in range(2):
        for j in range(2):
            shard[f"ffn{i}{j}"] = (tr(ffn_w_in)[i, j], ffn_w_out[i, j])
    apart = GROUPS[:2]
    units = []
    for grp in GROUPS:
        units += [(grp + "_in", grp, (0,)), (grp + "_out", grp, (1,))] if grp in apart else [(grp, grp, (0, 1))]
    gathers = {}
    started = mods_done
    for unit, grp, idx in units:
        srcs = [(shard[grp][k] + started).astype(BF16) for k in idx]
        st = _exchange_start(srcs, [_landing(s, me) for s in srcs], scatter=False, name="gather_start_" + unit)
        gathers[unit] = st[:4]
        started = st[4]

    def fetch(unit, after):
        return _exchange_wait(*gathers[unit], after, scatter=False, name="gather_wait_" + unit)

    def get_w(grp, after):
        if grp not in apart:
            return _group_mats(grp, fetch(grp, after))
        early = lambda later: _mats_in(grp, fetch(grp + "_in", later)[0])
        late = lambda later: _mats_out(grp, fetch(grp + "_out", later)[0])
        if grp.startswith("ffn"):
            return {grp: dict(early=early, late=late)}
        return dict(early(after), late=late)

    scatters = {}
    held = {}

    def put_grad(grp, which, grad):
        if grp in apart:
            unit, blocks = grp + "_" + which[2:], [_grad_blocks(which, grad)]
        else:
            held[grp, which] = _grad_blocks(which, grad)
            if (grp, "w_in") not in held or (grp, "w_out") not in held:
                return None
            unit, blocks = grp, [held[grp, "w_in"], held[grp, "w_out"]]
        own = [lax.dynamic_index_in_dim(b, me, axis=0, keepdims=False) for b in blocks]
        st = _exchange_start(blocks, [_landing(o_, me) for o_ in own], scatter=True, name="scatter_start_" + unit)
        scatters[unit] = st[:4]
        return st[4]

    mods_rows = jnp.transpose(mods_all, (1, 2, 0, 3)).reshape(2, 2 * NDEV, N_MOD * d) + started
    mx = lax.dynamic_index_in_dim(mods_rows, me, axis=1, keepdims=False).reshape(2, N_MOD, d)
    mc = mods_rows[:, NDEV].reshape(2, N_MOD, d)
    mods = [(mc[i], mx[i]) for i in range(2)]

    small = dict(conv_w8=jnp.pad(conv_w_full, ((0, 8 - SSD_CONV), (0, 0))), conv_b=ssd_conv_b, dt_bias=ssd_dt_bias[0],
                 a_log=ssd_A_log[0], ssd_d=ssd_D[0], ssd_norm_g=ssd_norm_g[0], gm_v_g=gm_v_g_full,
                 gm_v_b=gm_v_b_full, gm_w_s=gm_w_s[0], gm_b_s=gm_b_s[0])
    loss_parts, grad_x, g = _local_step(x[0], ctx[0], loss_target[0], mods, norm_g_full, get_w, small, put_grad)
    g["loss"] = (0.5 / d * jnp.sum(loss_parts)).reshape(1)

    whole = {"ffn_w_in": (tr(ffn_w_in), tr(m_ffn_w_in), tr(v_ffn_w_in)), "ffn_w_out": (ffn_w_out, m_ffn_w_out, v_ffn_w_out),
             "ssd_w_in": (tr(ssd_w_in), tr(m_ssd_w_in), tr(v_ssd_w_in)), "ssd_w_out": (ssd_w_out, m_ssd_w_out, v_ssd_w_out),
             "gm_w_in": (gm_w_in, m_gm_w_in, v_gm_w_in), "gm_w_out": (gm_w_out, m_gm_w_out, v_gm_w_out)}
    res = {}

    def update_units(some, after):
        for unit, grp, idx in some:
            parts = _exchange_wait(*scatters[unit], after, scatter=True, name="scatter_wait_" + unit)
            for k, p in zip(idx, parts):
                which = ("in", "out")[k]
                nm = ("ffn" if grp.startswith("ffn") else grp) + "_w_" + which
                sel = (int(grp[3]), int(grp[4])) if grp.startswith("ffn") else (0,)
                res[nm] = _adamw(p, *whole[nm], name=f"adamw_{grp}_{which}", sel=sel, into=res.get(nm))
                after = res[nm][0]
        return after

    sg_names = ["dmx", "dmc", "norm_g", "ssd_conv_w", "ssd_conv_b", "ssd_dt_bias", "ssd_A_log", "ssd_D", "ssd_norm_g",
                "gm_v_g", "gm_v_b", "gm_w_s", "gm_b_s", "loss"]
    sg_shapes = [g[n].shape for n in sg_names]
    flat = jnp.concatenate([g[n].reshape(-1) for n in sg_names])
    npack = flat.shape[0]
    pad = (-npack) % 1024
    flat = jnp.pad(flat, (0, pad)).reshape(-1, 128)
    sg_start = _exchange_start([flat], [_landing(flat, me)], scatter=False, name="small_grads_start")
    by_send = list(reversed(units))
    update_units(by_send[:4], jnp.stack([sg_start[4], grad_x[0, 0]]))
    early_done = jnp.stack([res[nm][0].reshape(-1)[-1] for nm in sorted(res)])
    (sg_all,) = _exchange_wait(*sg_start[:4], early_done, scatter=False, name="small_grads_wait")
    sg_sum = _sum_slots(sg_all, name="sum_small_grads").reshape(-1)[:npack]
    update_units(by_send[4:], sg_sum)
    sums = {}
    o = 0
    for n, shp in zip(sg_names, sg_shapes):
        sz = math.prod(shp)
        sums[n] = sg_sum[o:o + sz].reshape(shp)
        o += sz
    loss = sums["loss"][0]
    per_dev = sg_all.reshape(NDEV, -1)
    dmx_all =per_dev[:, :2 * N_MOD * d].reshape(NDEV, 2, N_MOD * d)
    dmc_all = per_dev[:, 2 * N_MOD * d:4 * N_MOD * d].reshape(NDEV, 2, N_MOD * d)

    (s16,) = _rowwise("ada_silu", lambda cc: ((_silu(cc),), ()), 2 * NDEV, [c16], [], [(d, F32)], tm=2 * NDEV)
    s16_t = s16.T
    g_ada_w, dcc_parts = [], []
    for i in range(2):
        rhs = jnp.concatenate([lax.dynamic_slice_in_dim(dmx_all[:, i], me * ncol, ncol, axis=1),
                               lax.dynamic_slice_in_dim(dmc_all[:, i], me * ncol, ncol, axis=1)], axis=0)
        g_ada_w.append(_mm_f32(s16_t, rhs, name=f"ada_dw{i}"))
        dmc_loc = lax.dynamic_slice_in_dim(sums["dmc"][i], me * ncol, ncol, axis=0)
        rhs_c = jnp.zeros((ncol, 128), F32).at[:, 0].set(dmc_loc)
        dcc_parts.append(_mm_f32(ada_w[i], rhs_c, name=f"ada_dcc{i}")[:, 0])
    g_ada_w = jnp.stack(g_ada_w)
    dcc_part = (dcc_parts[0] + dcc_parts[1]).reshape(8, 128)
    (dcc_all,), _ = _exchange([dcc_part], scatter=False, name="gather_dcc")
    g_c_ctx = _sum_slots(dcc_all, name="sum_dcc", scale_by=c_ctx.reshape(8, 128)).reshape(d)
    g_ada_b = sums["dmx"] + sums["dmc"]

    outs = _adamw(g_ada_w.reshape(1, -1, ncol), ada_w.reshape(-1, ncol), m_ada_w.reshape(-1, ncol),
                  v_ada_w.reshape(-1, ncol), name="adamw_ada_w")
    res["ada_w"] = [o_.reshape(ada_w.shape) for o_ in outs]

    loc = lambda a, ax, n: lax.dynamic_slice_in_dim(a, me * n, n, axis=ax)
    small_g = dict(c_ctx=g_c_ctx, ada_b=g_ada_b, norm_g=loc(sums["norm_g"], 2, 128),
                   ssd_conv_w=loc(sums["ssd_conv_w"], 1, 512)[None], ssd_conv_b=sums["ssd_conv_b"][None],
                   ssd_dt_bias=sums["ssd_dt_bias"][None], ssd_A_log=sums["ssd_A_log"][None], ssd_D=sums["ssd_D"][None],
                   ssd_norm_g=sums["ssd_norm_g"][None], gm_v_g=loc(sums["gm_v_g"], 0, 256)[None],
                   gm_v_b=loc(sums["gm_v_b"], 0, 256)[None], gm_w_s=sums["gm_w_s"][None], gm_b_s=sums["gm_b_s"][None])
    small_w = dict(c_ctx=(c_ctx, m_c_ctx, v_c_ctx), ada_b=(ada_b, m_ada_b, v_ada_b), norm_g=(norm_g, m_norm_g, v_norm_g),
                   ssd_conv_w=(ssd_conv_w, m_ssd_conv_w, v_ssd_conv_w), ssd_conv_b=(ssd_conv_b, m_ssd_conv_b, v_ssd_conv_b),
                   ssd_dt_bias=(ssd_dt_bias, m_ssd_dt_bias, v_ssd_dt_bias), ssd_A_log=(ssd_A_log, m_ssd_A_log, v_ssd_A_log),
                   ssd_D=(ssd_D, m_ssd_D, v_ssd_D), ssd_norm_g=(ssd_norm_g, m_ssd_norm_g, v_ssd_norm_g),
                   gm_v_g=(gm_v_g, m_gm_v_g, v_gm_v_g), gm_v_b=(gm_v_b, m_gm_v_b, v_gm_v_b),
                   gm_w_s=(gm_w_s, m_gm_w_s, v_gm_w_s), gm_b_s=(gm_b_s, m_gm_b_s, v_gm_b_s))
    sn = list(small_w)

    def pack(arrs):
        f = jnp.concatenate([a.reshape(-1) for a in arrs])
        return jnp.pad(f, (0, (-f.shape[0]) % (256 * 128))).reshape(-1, 128)

    pg = pack([small_g[n].reshape(small_w[n][0].shape) for n in sn])
    outs = _adamw(pg[None], pack([small_w[n][0] for n in sn]), pack([small_w[n][1] for n in sn]),
                  pack([small_w[n][2] for n in sn]), name="adamw_small")
    flat_outs = [o_.reshape(-1) for o_ in outs]
    o = 0
    for n in sn:
        shp = small_w[n][0].shape
        sz = math.prod(shp)
        res[n] = [fo[o:o + sz].reshape(shp) for fo in flat_outs]
        o += sz

    order = ["c_ctx", "ada_w", "ada_b", "norm_g", "ffn_w_in", "ffn_w_out", "ssd_w_in", "ssd_conv_w", "ssd_conv_b",
             "ssd_dt_bias", "ssd_A_log", "ssd_D", "ssd_norm_g", "ssd_w_out", "gm_w_in", "gm_v_g", "gm_v_b", "gm_w_s",
             "gm_b_s", "gm_w_out"]
    for nm in ("ffn_w_in", "ssd_w_in"):
        res[nm] = [tr(a) for a in res[nm]]
    result = [loss, grad_x[None]]
    for k in range(4):
        result += [res[n][k] for n in order]
    return tuple(result)
```

```python
import functools
import math

import jax
import jax.numpy as jnp
from jax import lax
from jax.experimental import pallas as pl
from jax.experimental.pallas import tpu as pltpu

F32 = jnp.float32
BF16 = jnp.bfloat16

NDEV = 8
D_MODEL = 1024
FFN_DIM = 2816
N_MOD = 9
EPS = 1e-6
SSD_INNER = 2048
SSD_HEADS = 32
SSD_HEAD_DIM = 64
SSD_GROUPS = 8
SSD_HPG = 4
SSD_STATE = 128
SSD_CONV = 5
SSD_CONV_DIM = 4096
CHUNK = 128
GM_INNER = 2048
GM_GROUPS = 8
GM_GROUP_DIM = 256
ADAM_LR = 0.001
ADAM_B1 = 0.9
ADAM_B2 = 0.999
ADAM_EPS = 1e-08
ADAM_WD = 0.01
ADAM_STEP = 10
NEG_BIG = -1e30
VMEM_LIMIT_BYTES = 56 * 1024 * 1024
HI = lax.Precision.HIGHEST


def _params(*sem):
    return pltpu.CompilerParams(dimension_semantics=sem, vmem_limit_bytes=VMEM_LIMIT_BYTES)


def _pick(n, target, mult=16):
    if n <= target:
        return n
    for t in range(target - target % mult, 0, -mult):
        if n % t == 0:
            return t
    raise ValueError((n, target, mult))


def _sig(x):
    return 0.5 * jnp.tanh(0.5 * x) + 0.5


def _silu(x):
    return x * _sig(x)


def _dsilu(x):
    s = _sig(x)
    return s * (1.0 + x * (1.0 - s))


_GELU_C = math.sqrt(2.0 / math.pi)


def _gelu(x):
    return 0.5 * x * (1.0 + jnp.tanh(_GELU_C * (x + 0.044715 * x * x * x)))


def _gelu_and_grad(x):
    x2 = x * x
    t = jnp.tanh(_GELU_C * (x + 0.044715 * x2 * x))
    half = 0.5 * (1.0 + t)
    return x * half, half + 0.5 * x * (1.0 - t * t) * _GELU_C * (1.0 + 3.0 * 0.044715 * x2)


def _dgelu(x):
    return _gelu_and_grad(x)[1]


def _softplus(x):
    return jnp.maximum(x, 0.0) + jnp.log1p(jnp.exp(-jnp.abs(x)))


def _sum0(v):
    return jnp.sum(v, axis=0, keepdims=True)


def _rms(h):
    r = lax.rsqrt(jnp.mean(h * h, axis=-1, keepdims=True) + EPS)
    return h * r, r


def _dot(a, b, dims=((1,), (0,)), precision=None):
    return lax.dot_general(a, b, (dims, ((), ())), preferred_element_type=F32, precision=precision)


_NT = ((1,), (1,))
_TN = ((0,), (0,))


def _rowwise(name, fn, n_rows, rows, consts, outs, accs=(), *, tm, nseg=1, seg_blocks=0):
    assert n_rows % tm == 0
    if nseg == 2:
        assert seg_blocks > 0
        seg = lambda i: jnp.where(i < seg_blocks, 0, 1)
    else:
        seg = lambda i: 0
    in_specs, args, lacking = [], [], []
    for r in rows:
        arr, width, cb, off = r if isinstance(r, tuple) else (r, r.shape[1], 0, 0)
        in_specs.append(pl.BlockSpec((tm, width), lambda i, cb=cb, off=off: (jnp.maximum(i + off, 0), cb)))
        args.append(arr)
        lacking.append(-off if off < 0 else 0)
    for kind, arr in consts:
        if kind == "seg":
            assert arr.shape[0] == nseg and arr.shape[1] == 1, arr.shape
            in_specs.append(pl.BlockSpec((None, 1, arr.shape[2]), lambda i: (seg(i), 0, 0)))
        else:
            in_specs.append(pl.BlockSpec(arr.shape, lambda i: (0, 0)))
        args.append(arr)
    out_shape = [jax.ShapeDtypeStruct((n_rows, w), dt) for w, dt in outs]
    out_specs = [pl.BlockSpec((tm, w), lambda i: (i, 0)) for w, _ in outs]
    out_shape += [jax.ShapeDtypeStruct((nseg, 1, w), F32) for w in accs]
    out_specs += [pl.BlockSpec((None, 1, w), lambda i: (seg(i), 0, 0)) for w in accs]
    n_in, n_out, n_acc = len(args), len(outs), len(accs)

    def kern(*refs):
        i = pl.program_id(0)
        ins = [r[...] for r in refs[:n_in]]
        for k, lack in enumerate(lacking):
            if lack:
                ins[k] = jnp.where(i >= lack, ins[k], jnp.zeros_like(ins[k]))
        res, terms = fn(*ins)
        for ref, v in zip(refs[n_in:n_in + n_out], res):
            ref[...] = v.astype(ref.dtype)
        if n_acc:
            sums = [_sum0(v) for v in terms]
            first = (i == 0) | (i == seg_blocks) if nseg == 2 else (i == 0)
            acc_refs = refs[n_in + n_out:]

            @pl.when(first)
            def _():
                for ref, v in zip(acc_refs, sums):
                    ref[...] = v

            @pl.when(jnp.logical_not(first))
            def _():
                for ref, v in zip(acc_refs, sums):
                    ref[...] += v

    res = pl.pallas_call(
        kern, name=name, grid=(n_rows // tm,), in_specs=in_specs, out_specs=out_specs, out_shape=out_shape,
        compiler_params=_params("arbitrary"),
    )(*args)
    return res


def _pre_fwd_fn(h, g, shift, scale):
    hh, _ = _rms(h)
    return (hh * g * (1.0 + scale) + shift,), ()


def _pre_bwd_fn(du, h, dres, g, scale):
    hh, r = _rms(h)
    n = hh * g
    dn = du * (1.0 + scale)
    dhh = dn * g
    dh = dres + r * (dhh - hh * jnp.mean(dhh * hh, axis=-1, keepdims=True))
    return (dh,), (du, du * n, dn * hh)


def _post_fwd_fn(weight, h, y, g, gate):
    yh, _ = _rms(y)
    return (h + weight * gate * (yh * g),), ()


def _out_post_fn(weight, y, h, g, gate):
    return (y,) + _post_fwd_fn(weight, h, y, g, gate)[0], ()


def _post_bwd_fn(weight, dh, y, g, gate):
    yh, r = _rms(y)
    dr = dh * weight
    dyh = dr * gate * g
    dy = r * (dyh - yh * jnp.mean(dyh * yh, axis=-1, keepdims=True))
    return (dy,), (dr * yh * g, dr * gate * yh)


def _glu_bwd_fn(ds, a, b):
    a = a.astype(F32)
    b = b.astype(F32)
    sg = _sig(a)
    da = ds * b * (sg * (1.0 + a * (1.0 - sg)))
    db = ds * (a * sg)
    return (jnp.concatenate([da, db], axis=1),), ()


def _loss_fn(y, t):
    diff = y - t
    return (diff * (1.0 / D_MODEL),), (diff * diff,)


def _ssd_y(yf, yb, xs, z, dvec):
    y = yf + yb + dvec * xs
    return y, y * _silu(z)


def _ssdgate_fwd_fn(yf, yb, xs, z, dvec, ng):
    _, yg = _ssd_y(yf, yb, xs, z, dvec)
    parts = []
    for g in range(SSD_GROUPS):
        sl = slice(g * 256, (g + 1) * 256)
        parts.append(_rms(yg[:, sl])[0])
    return (jnp.concatenate(parts, axis=1) * ng,), ()


def _ssdgate_bwd_fn(dyn, yf, yb, xs, z, dvec, ng):
    y, yg = _ssd_y(yf, yb, xs, z, dvec)
    dyg_parts, ygh_parts = [], []
    for g in range(SSD_GROUPS):
        sl = slice(g * 256, (g + 1) * 256)
        ygh, r = _rms(yg[:, sl])
        d = dyn[:, sl] * ng[:, sl]
        dyg_parts.append(r * (d - ygh * jnp.mean(d * ygh, axis=-1, keepdims=True)))
        ygh_parts.append(ygh)
    dyg = jnp.concatenate(dyg_parts, axis=1)
    ygh = jnp.concatenate(ygh_parts, axis=1)
    dy = dyg * _silu(z)
    dz = dyg * y * _dsilu(z)
    return (dy, dz), (dyn * ygh, dy * xs)


def _ln_stats(v):
    mu = jnp.mean(v, axis=-1, keepdims=True)
    vc = v - mu
    r = lax.rsqrt(jnp.mean(vc * vc, axis=-1, keepdims=True) + EPS)
    return vc * r, r


def _gm_act_fwd_fn(p, vg, vb):
    gu = _gelu(p[:, :GM_INNER])
    gvh, _ = _ln_stats(_gelu(p[:, GM_INNER:]))
    return (gu, gvh * vg + vb), ()


def _gm_act_bwd_fn(p, dgu, dgvn, vg):
    pu = p[:, :GM_INNER]
    pv = p[:, GM_INNER:]
    gv, dgelu_v = _gelu_and_grad(pv)
    gvh, r = _ln_stats(gv)
    dgvh = dgvn * vg
    dgv = r * (dgvh - jnp.mean(dgvh, axis=-1, keepdims=True) - gvh * jnp.mean(dgvh * gvh, axis=-1, keepdims=True))
    dp = jnp.concatenate([dgu * _dgelu(pu), dgv * dgelu_v], axis=1)
    return (dp,), (dgvn * gvh, dgvn)


def _mm(a, b, *, out_dtype, name, tm=1088, tn=1024, tk=1408, add=None, rhs_t=False, n=None, b_off=(0, 0)):
    m, k = a.shape
    col_blocked = b.ndim == 3
    if col_blocked:
        assert not rhs_t and n is None and b.shape[1] == k
        n, tn = b.shape[0] * b.shape[2], b.shape[2]
    elif n is None:
        n, k2 = b.shape if rhs_t else b.shape[::-1]
        assert k == k2
    tm, tn, tk = _pick(m, tm), _pick(n, tn, 128), _pick(k, tk, 128)
    o0, o1 = b_off
    nk = k // tk
    dims = _NT if rhs_t else ((1,), (0,))

    def kern(*refs):
        a_ref, b_ref = refs[:2]
        add_ref = refs[2] if add is not None else None
        o_ref = refs[3] if add is not None else refs[2]

        def finish(r):
            if add is not None:
                r = r + add_ref[...]
            o_ref[...] = r.astype(o_ref.dtype)

        p = _dot(a_ref[...], b_ref[...], dims)
        if nk == 1:
            finish(p)
            return
        acc_ref = refs[-1]
        kk = pl.program_id(2)

        @pl.when(kk == 0)
        def _():
            acc_ref[...] = p

        @pl.when((kk > 0) & (kk < nk - 1))
        def _():
            acc_ref[...] += p

        @pl.when(kk == nk - 1)
        def _():
            finish(acc_ref[...] + p)

    if col_blocked:
        b_spec = pl.BlockSpec((None, tk, tn), lambda i, j, kk: (j, kk, 0))
    elif rhs_t:
        b_spec = pl.BlockSpec((tn, tk), lambda i, j, kk: (j + o0, kk + o1))
    else:
        b_spec = pl.BlockSpec((tk, tn), lambda i, j, kk: (kk + o0, j + o1))
    in_specs = [pl.BlockSpec((tm, tk), lambda i, j, kk: (i, kk)), b_spec]
    args = [a, b]
    if add is not None:
        in_specs.append(pl.BlockSpec((tm, tn), lambda i, j, kk: (i, j)))
        args.append(add)
    return pl.pallas_call(
        kern, name=name, grid=(m // tm, n // tn, nk), in_specs=in_specs,
        out_specs=pl.BlockSpec((tm, tn), lambda i, j, kk: (i, j)),
        out_shape=jax.ShapeDtypeStruct((m, n), out_dtype),
        scratch_shapes=[pltpu.VMEM((tm, tn), F32)] if nk > 1 else [],
        compiler_params=_params("parallel", "parallel", "arbitrary"),
    )(*args)


def _mm_rows(a, b, fn, rows, consts, outs, accs=(), *, name, tm=544, tk=1408, rhs_t=False, n_ctx=0):
    halves = a.ndim == 3
    m, k = (a.shape[1], 2 * a.shape[2]) if halves else a.shape
    col_blocked = b.ndim == 3
    kb, nb = 1, None
    if col_blocked:
        assert rhs_t and b.shape[0] * b.shape[2] == k
        n, nb = b.shape[1], b.shape[2]
        kb = max(1, tk // nb)
        assert b.shape[0] % kb == 0
        tk = kb * nb
    else:
        n = b.shape[0] if rhs_t else b.shape[1]
    tm, tk = _pick(m, tm), _pick(k, tk, 128)
    nk = k // tk
    if halves:
        hb = k // 2 // tk
        a_spec = pl.BlockSpec((None, tm, tk), lambda i, kk: (kk // hb, i, kk % hb))
    else:
        a_spec = pl.BlockSpec((tm, tk), lambda i, kk: (i, kk))
    dims = _NT if rhs_t else ((1,), (0,))
    n_rows, n_const, n_out, n_acc = len(rows), len(consts), len(outs), len(accs)

    def kern(*refs):
        a_ref, b_ref = refs[:2]
        row_refs = refs[2:2 + n_rows]
        const_refs = refs[2 + n_rows:2 + n_rows + n_const]
        out_refs = refs[2 + n_rows + n_const:2 + n_rows + n_const + n_out]
        acc_refs = refs[2 + n_rows + n_const + n_out:2 + n_rows + n_const + n_out + n_acc]
        i, kk = pl.program_id(0), pl.program_id(1)

        def finish(p, rs=slice(None), r0=0):
            nr = p.shape[0]
            is_ctx = (i * tm + r0 + lax.broadcasted_iota(jnp.int32, (nr, 1), 0)) < n_ctx
            cvals = []
            for (kind, arr), ref in zip(consts, const_refs):
                if kind == "seg":
                    cvals.append(jnp.where(is_ctx, ref[0], ref[1]) if arr.shape[0] == 2 else ref[0])
                else:
                    cvals.append(ref[...])
            res, terms = fn(p, *[r[rs, :] for r in row_refs], *cvals)
            for ref, v in zip(out_refs, res):
                ref[rs, :] = v.astype(ref.dtype)
            for ref, v in zip(acc_refs, terms):
                s_all = _sum0(v)
                s_ctx = _sum0(jnp.where(is_ctx, v, 0.0)) if n_ctx else jnp.zeros_like(s_all)
                both = jnp.concatenate([s_ctx, s_all - s_ctx], axis=0)[:, None, :]

                @pl.when(i == 0)
                def _():
                    ref[...] = both

                @pl.when(i > 0)
                def _():
                    ref[...] += both

        if nk == 1 and n_acc == 0:
            nsub = 2 if tm % 32 == 0 else 1
            sub = tm // nsub
            for r in range(nsub):
                rs = slice(r * sub, (r + 1) * sub)
                finish(_dot(a_ref[rs, :], b_ref[...], dims), rs, r * sub)
            return
        if col_blocked:
            p = sum(_dot(a_ref[:, c * nb:(c + 1) * nb], b_ref[c], dims) for c in range(kb))
        else:
            p = _dot(a_ref[...], b_ref[...], dims)
        if nk == 1:
            finish(p)
            return
        scr = refs[-1]

        @pl.when(kk == 0)
        def _():
            scr[...] = p

        @pl.when((kk > 0) & (kk < nk - 1))
        def _():
            scr[...] += p

        @pl.when(kk == nk - 1)
        def _():
            finish(scr[...] + p)

    if col_blocked:
        b_spec = pl.BlockSpec((kb, n, nb), lambda i, kk: (kk, 0, 0))
    elif rhs_t:
        b_spec = pl.BlockSpec((n, tk), lambda i, kk: (0, kk))
    else:
        b_spec = pl.BlockSpec((tk, n), lambda i, kk: (kk, 0))
    in_specs = [a_spec, b_spec]
    in_specs += [pl.BlockSpec((tm, r.shape[1]), lambda i, kk: (i, 0)) for r in rows]
    for kind, arr in consts:
        in_specs.append(pl.BlockSpec(arr.shape, (lambda i, kk: (0, 0, 0)) if kind == "seg" else (lambda i, kk: (0, 0))))
    out_shape = [jax.ShapeDtypeStruct((m, w), dt) for w, dt in outs]
    out_specs = [pl.BlockSpec((tm, w), lambda i, kk: (i, 0)) for w, _ in outs]
    out_shape += [jax.ShapeDtypeStruct((2, 1, w), F32) for w in accs]
    out_specs += [pl.BlockSpec((2, 1, w), lambda i, kk: (0, 0, 0)) for w in accs]
    return pl.pallas_call(
        kern, name=name, grid=(m // tm, nk), in_specs=in_specs, out_specs=out_specs, out_shape=out_shape,
        scratch_shapes=[pltpu.VMEM((tm, n), F32)] if nk > 1 else [],
        compiler_params=_params("arbitrary", "arbitrary"),
    )(a, b, *rows, *[arr for _, arr in consts])


def _mm_glu(u, win_t, *, name, tm=1088, tn=1408):
    m, k = u.shape
    n = win_t.shape[0] // 2
    tm, tn = _pick(m, tm), _pick(n, tn, 128)
    nj = n // tn

    nsub = 2 if tm % 32 == 0 else 1
    sub = tm // nsub

    def kern(u_ref, wa_ref, wb_ref, s_ref, a_ref, b_ref):
        for r in range(nsub):
            rows = slice(r * sub, (r + 1) * sub)
            uu = u_ref[rows, :]
            a = _dot(uu, wa_ref[...], _NT)
            b = _dot(uu, wb_ref[...], _NT)
            s_ref[rows, :] = (_silu(a) * b).astype(BF16)
            a_ref[rows, :] = a.astype(BF16)
            b_ref[rows, :] = b.astype(BF16)

    ospec = pl.BlockSpec((tm, tn), lambda i, j: (i, j))
    return pl.pallas_call(
        kern, name=name, grid=(m // tm, nj),
        in_specs=[pl.BlockSpec((tm, k), lambda i, j: (i, 0)), pl.BlockSpec((tn, k), lambda i, j: (j, 0)),
                  pl.BlockSpec((tn, k), lambda i, j: (nj + j, 0))],
        out_specs=[ospec, ospec, ospec],
        out_shape=[jax.ShapeDtypeStruct((m, n), BF16)] * 3,
        compiler_params=_params("parallel", "parallel"),
    )(u, win_t, win_t)


def _mm_glu_bwd(dy, wout, a, b, *, name, tm=544, tn=1408):
    m, k = dy.shape
    f = wout.shape[0]
    tm, tn = _pick(m, tm), _pick(f, tn, 128)
    nsub = 2 if tm % 32 == 0 else 1
    sub = tm // nsub

    def kern(dy_ref, w_ref, a_ref, b_ref, o_ref):
        for r in range(nsub):
            rs = slice(r * sub, (r + 1) * sub)
            ds = _dot(dy_ref[rs, :], w_ref[...], _NT)
            (dp,), _ = _glu_bwd_fn(ds, a_ref[rs, :], b_ref[rs, :])
            o_ref[0, rs, :] = dp[:, :tn].astype(BF16)
            o_ref[1, rs, :] = dp[:, tn:].astype(BF16)

    tile = pl.BlockSpec((tm, tn), lambda i, j: (i, j))
    return pl.pallas_call(
        kern, name=name, grid=(m // tm, f // tn),
        in_specs=[pl.BlockSpec((tm, k), lambda i, j: (i, 0)), pl.BlockSpec((tn, k), lambda i, j: (j, 0)), tile, tile],
        out_specs=pl.BlockSpec((2, tm, tn), lambda i, j: (0, i, j)),
        out_shape=jax.ShapeDtypeStruct((2, m, f), BF16),
        compiler_params=_params("parallel", "parallel"),
    )(dy, wout, a, b)


def _mm_tn(a, b, *, name, tm=1024, tn=1024, tk=2176, col_blocks=None, stack=None):
    extra, extra_specs, aliases = [], [], {}
    halves = a.ndim == 3
    t, m = (a.shape[1], 2 * a.shape[2]) if halves else a.shape
    t2, n = b.shape
    assert t == t2
    tm, tn, tk = _pick(m, tm, 128), _pick(n, tn, 128), _pick(t, tk)
    nk = t // tk
    if halves:
        hb = m // 2 // tm
        a_spec = pl.BlockSpec((None, tk, tm), lambda i, j, kk: (i // hb, kk, i % hb))
    else:
        a_spec = pl.BlockSpec((tk, tm), lambda i, j, kk: (kk, i))
    if col_blocks is None:
        def kern(a_ref, b_ref, o_ref):
            kk = pl.program_id(2)

            @pl.when(kk == 0)
            def _():
                o_ref[...] = jnp.zeros_like(o_ref)

            o_ref[...] += _dot(a_ref[...], b_ref[...], _TN)

        out_spec = pl.BlockSpec((tm, tn), lambda i, j, kk: (i, j))
        out_shape = jax.ShapeDtypeStruct((m, n), F32)
        scratch = []
    else:
        wb = n // col_blocks
        per = tn // wb
        assert tn % wb == 0 and wb % 8 == 0

        def kern(a_ref, b_ref, *rest):
            o_ref, acc_ref = rest[-2:]
            kk = pl.program_id(2)
            p = _dot(a_ref[...], b_ref[...], _TN)

            @pl.when(kk == 0)
            def _():
                acc_ref[...] = p

            @pl.when((kk > 0) & (kk < nk - 1))
            def _():
                acc_ref[...] += p

            @pl.when(kk == nk - 1)
            def _():
                r = acc_ref[...] + p if nk > 1 else p
                for c in range(per):
                    o_ref[c] = r[:, c * wb:(c + 1) * wb].astype(BF16)

        rows_total, row0, into = stack if stack is not None else (m, 0, None)
        assert row0 % tm == 0
        out_spec = pl.BlockSpec((per, tm, wb), lambda i, j, kk: (j, i + row0 // tm, 0))
        out_shape = jax.ShapeDtypeStruct((col_blocks, rows_total, wb), BF16)
        scratch = [pltpu.VMEM((tm, tn), F32)]
        if into is not None:
            extra, extra_specs, aliases = [into], [pl.BlockSpec(memory_space=pl.ANY)], {2: 0}

    return pl.pallas_call(
        kern, name=name, grid=(m // tm, n // tn, nk),
        in_specs=[a_spec, pl.BlockSpec((tk, tn), lambda i, j, kk: (kk, j))] + extra_specs,
        out_specs=out_spec, out_shape=out_shape, scratch_shapes=scratch, input_output_aliases=aliases,
        compiler_params=_params("parallel", "parallel", "arbitrary"),
    )(a, b, *extra)


def _mm_f32(a, b, *, name, silu_a=False, bias=None):
    m, k = a.shape
    n = b.shape[1]

    def kern(*refs):
        if bias is None:
            a_ref, b_ref, o_ref = refs
        else:
            a_ref, b_ref, bias_ref, o_ref = refs
        av = a_ref[...]
        if silu_a:
            av = _silu(av)
        r = jnp.dot(av, b_ref[...], preferred_element_type=F32, precision=HI)
        if bias is not None:
            r = r + bias_ref[...]
        o_ref[...] = r

    args = [a, b] + ([] if bias is None else [bias])
    return pl.pallas_call(kern, name=name, out_shape=jax.ShapeDtypeStruct((m, n), F32),
                          compiler_params=pltpu.CompilerParams(vmem_limit_bytes=VMEM_LIMIT_BYTES))(*args)


CONV_WIN = 32


def _conv_windows(n, n_ctx):
    assert n_ctx % CONV_WIN == 0 and n_ctx >= CONV_WIN and n - n_ctx >= CONV_WIN
    return (0, n_ctx - CONV_WIN // 2, n - CONV_WIN)


def _tap_outside(r0, s, n, n_ctx):
    t = r0 + lax.broadcasted_iota(jnp.int32, (CONV_WIN, 1), 0)
    lo = jnp.where(t < n_ctx, 0, n_ctx)
    hi = jnp.where(t < n_ctx, n_ctx, n)
    return jnp.where((t + s >= lo) & (t + s < hi), 0.0, 1.0)


def _rolled(v, s):
    return v if s == 0 else pltpu.roll(v, (-s) % v.shape[0], 0)


def _conv_fwd(xp, w8, b, *, n_ctx, name, cb=256):
    n, c = xp.shape
    half = SSD_CONV // 2

    def kern(x_ref, w_ref, b_ref, cpre_ref, act_ref):
        x = x_ref[...]
        acc = jnp.zeros_like(x) + b_ref[...]
        rolled = {}
        for k in range(SSD_CONV):
            rolled[k] = _rolled(x, k - half)
            acc = acc + rolled[k] * w_ref[k:k + 1, :]
        cpre_ref[...] = acc
        act_ref[...] = _silu(acc)
        for r0 in _conv_windows(n, n_ctx):
            rows = slice(r0, r0 + CONV_WIN)
            fix = acc[rows]
            for k in range(SSD_CONV):
                if k != half:
                    fix = fix - rolled[k][rows] * w_ref[k:k + 1, :] * _tap_outside(r0, k - half, n, n_ctx)
            cpre_ref[rows, :] = fix
            act_ref[rows, :] = _silu(fix)

    spec = pl.BlockSpec((n, cb), lambda j: (0, j))
    return pl.pallas_call(
        kern, name=name, grid=(c // cb,),
        in_specs=[spec, pl.BlockSpec((8, cb), lambda j: (0, j)), pl.BlockSpec((1, cb), lambda j: (0, j))],
        out_specs=[spec, spec], out_shape=[jax.ShapeDtypeStruct((n, c), F32)] * 2,
        compiler_params=_params("parallel"),
    )(xp, w8, b)


def _conv_bwd(d1, d2, cpre, xp, w8, *, n_ctx, name, cb=128):
    n, c = xp.shape
    half = SSD_CONV // 2

    def kern(d1_ref, d2_ref, cpre_ref, x_ref, w_ref, dx_ref, dw_ref, db_ref):
        g = (d1_ref[...] + d2_ref[...]) * _dsilu(cpre_ref[...])
        x = x_ref[...]
        dx = jnp.zeros_like(g)
        dw_ref[...] = jnp.zeros_like(dw_ref)
        g_rolled = {}
        for k in range(SSD_CONV):
            s = k - half
            g_rolled[k] = _rolled(g, -s)
            dx = dx + g_rolled[k] * w_ref[k:k + 1, :]
            xr = _rolled(x, s)
            dw = _sum0(g * xr)
            if s != 0:
                for r0 in _conv_windows(n, n_ctx):
                    rows = slice(r0, r0 + CONV_WIN)
                    dw = dw - _sum0(g[rows] * xr[rows] * _tap_outside(r0, s, n, n_ctx))
            dw_ref[k:k + 1, :] = dw
        dx_ref[...] = dx.astype(BF16)
        for r0 in _conv_windows(n, n_ctx):
            rows = slice(r0, r0 + CONV_WIN)
            fix = dx[rows]
            for k in range(SSD_CONV):
                if k != half:
                    fix = fix - g_rolled[k][rows] * w_ref[k:k + 1, :] * _tap_outside(r0, half - k, n, n_ctx)
            dx_ref[rows, :] = fix.astype(BF16)
        db_ref[...] = _sum0(g)

    spec = pl.BlockSpec((n, cb), lambda j: (0, j))
    return pl.pallas_call(
        kern, name=name, grid=(c // cb,),
        in_specs=[spec, spec, spec, spec, pl.BlockSpec((8, cb), lambda j: (0, j))],
        out_specs=[spec, pl.BlockSpec((8, cb), lambda j: (0, j)), pl.BlockSpec((1, cb), lambda j: (0, j))],
        out_shape=[jax.ShapeDtypeStruct((n, c), BF16), jax.ShapeDtypeStruct((8, c), F32),
                   jax.ShapeDtypeStruct((1, c), F32)],
        compiler_params=_params("parallel"),
    )(d1, d2, cpre, xp, w8)


def _chunk_of(s, nc, n_ctx_chunks, rev):
    if not rev:
        return s
    return jnp.where(s < n_ctx_chunks, n_ctx_chunks - 1 - s, nc - 1 - (s - n_ctx_chunks))


def _scan_common(dt_raw, dtT_raw, bias_r, bias_c, alog_r, alog_c, rev):
    ii = lax.broadcasted_iota(jnp.int32, (CHUNK, CHUNK), 0)
    jj = lax.broadcasted_iota(jnp.int32, (CHUNK, CHUNK), 1)
    tri = (jj >= ii) if rev else (jj <= ii)
    tri_t = (ii >= jj) if rev else (ii <= jj)
    a_r = -jnp.exp(alog_r)
    a_c = -jnp.exp(alog_c)
    dt = _softplus(dt_raw + bias_r)
    dt_t = _softplus(dtT_raw + bias_c)
    al = dt * a_r
    acum = _dot(tri.astype(F32), al, precision=HI)
    acum_t = _dot(dt_t * a_c, tri_t.astype(F32), precision=HI)
    atot = _sum0(al)
    return tri, tri_t, a_r, dt, acum, acum_t, atot


def _head_spread():
    return jnp.repeat(jnp.eye(SSD_HEADS, dtype=BF16), SSD_HEAD_DIM, axis=1)


def _dot_sel(v, sel):
    hi = v.astype(BF16)
    lo = (v - hi.astype(F32)).astype(BF16)
    return _dot(hi, sel) + _dot(lo, sel)


def _ssd_scan_fwd(xbc, dt_raw, dtT_raw, bias_r, bias_c, alog_r, alog_c, *, rev, n_ctx_chunks, name):
    n = xbc.shape[0]
    nc = n // CHUNK
    cidx = functools.partial(_chunk_of, nc=nc, n_ctx_chunks=n_ctx_chunks, rev=rev)

    def kern(xs_ref, b_ref, c_ref, dt_ref, dtT_ref, br_ref, bc_ref, ar_ref, ac_ref, e_ref, y_ref, hs_ref, h_scr):
        @pl.when(pl.program_id(0) == 0)
        def _():
            h_scr[...] = jnp.zeros_like(h_scr)

        tri, _, _, dt, acum, acum_t, atot = _scan_common(
            dt_ref[...], dtT_ref[...], br_ref[...], bc_ref[...], ar_ref[...], ac_ref[...], rev)
        etot = jnp.exp(atot)
        spread = lambda v: _dot_sel(v, e_ref[...])
        xdt_all = xs_ref[...] * spread(dt)
        eax = spread(jnp.exp(acum))
        xdw_all = xdt_all * spread(jnp.exp(atot - acum))
        hs_ref[...] = h_scr[...]
        for g in range(SSD_GROUPS):
            gs = slice(g * 256, (g + 1) * 256)
            bg = b_ref[:, g * SSD_STATE:(g + 1) * SSD_STATE].astype(BF16)
            cg = c_ref[:, g * SSD_STATE:(g + 1) * SSD_STATE].astype(BF16)
            cb = _dot(cg, bg, _NT)
            h4 = h_scr[gs, :]
            ys = []
            for k in range(SSD_HPG):
                h = g * SSD_HPG + k
                lmat = jnp.exp(jnp.where(tri, acum[:, h:h + 1] - acum_t[h:h + 1, :], NEG_BIG))
                xdt_h = xdt_all[:, h * SSD_HEAD_DIM:(h + 1) * SSD_HEAD_DIM].astype(BF16)
                ys.append(_dot((cb * lmat).astype(BF16), xdt_h))
            y_ref[:, gs] = jnp.concatenate(ys, axis=1) + _dot(cg, h4.astype(BF16), _NT) * eax[:, gs]
            s4 = _dot(xdw_all[:, gs].astype(BF16), bg, _TN)
            for k in range(SSD_HPG):
                h = g * SSD_HPG + k
                rs = slice(h * SSD_HEAD_DIM, (h + 1) * SSD_HEAD_DIM)
                h_scr[rs, :] = h4[k * SSD_HEAD_DIM:(k + 1) * SSD_HEAD_DIM] * etot[:, h:h + 1] + \
                    s4[k * SSD_HEAD_DIM:(k + 1) * SSD_HEAD_DIM]

    nh = SSD_HEADS
    small = lambda shape: pl.BlockSpec(shape, lambda s: (0, 0))
    return pl.pallas_call(
        kern, name=name, grid=(nc,),
        in_specs=[pl.BlockSpec((CHUNK, SSD_INNER), lambda s: (cidx(s), 0)),
                  pl.BlockSpec((CHUNK, 1024), lambda s: (cidx(s), 2)),
                  pl.BlockSpec((CHUNK, 1024), lambda s: (cidx(s), 3)),
                  pl.BlockSpec((CHUNK, nh), lambda s: (cidx(s), 0)),
                  pl.BlockSpec((nh, CHUNK), lambda s: (0, cidx(s))),
                  small((1, nh)), small((nh, 1)), small((1, nh)), small((nh, 1)), small((nh, SSD_INNER))],
        out_specs=[pl.BlockSpec((CHUNK, SSD_INNER), lambda s: (cidx(s), 0)),
                   pl.BlockSpec((None, SSD_INNER, SSD_STATE), lambda s: (s, 0, 0))],
        out_shape=[jax.ShapeDtypeStruct((n, SSD_INNER), F32),
                   jax.ShapeDtypeStruct((nc, SSD_INNER, SSD_STATE), F32)],
        scratch_shapes=[pltpu.VMEM((SSD_INNER, SSD_STATE), F32)],
        compiler_params=_params("arbitrary"),
    )(xbc, xbc, xbc, dt_raw, dtT_raw, bias_r, bias_c, alog_r, alog_c, _head_spread())


def _ssd_scan_bwd(dy, xbc, hs, dt_raw, dtT_raw, bias_r, bias_c, alog_r, alog_c, dvec, *, rev, n_ctx_chunks,
                  direct, name):
    n = xbc.shape[0]
    nc = n // CHUNK
    nh = SSD_HEADS
    step_of = lambda r: nc - 1 - r
    cidx = lambda r: _chunk_of(step_of(r), nc, n_ctx_chunks, rev)

    def kern(dy_ref, xs_ref, b_ref, c_ref, hs_ref, dt_ref, dtT_ref, br_ref, bc_ref, ar_ref, ac_ref, dv_ref,
             e_ref, et_ref, dx_ref, ddt_ref, dal_ref, dbias_ref, dh_scr):
        @pl.when(pl.program_id(0) == 0)
        def _():
            dh_scr[...] = jnp.zeros_like(dh_scr)
            dal_ref[...] = jnp.zeros_like(dal_ref)
            dbias_ref[...] = jnp.zeros_like(dbias_ref)

        tri, tri_t, a_r, dt, acum, acum_t, atot = _scan_common(
            dt_ref[...], dtT_ref[...], br_ref[...], bc_ref[...], ar_ref[...], ac_ref[...], rev)
        etot = jnp.exp(atot)
        spread = lambda v: _dot_sel(v, e_ref[...])
        gather = lambda v: _dot_sel(v, et_ref[...])
        xs_all = xs_ref[...]
        dy_all = dy_ref[...]
        dtx = spread(dt)
        eax = spread(jnp.exp(acum))
        decx = spread(jnp.exp(atot - acum))
        xdt_all = xs_all * dtx
        xdw_all = xdt_all * decx
        dyo_all = dy_all * eax
        lane = lax.broadcasted_iota(jnp.int32, (CHUNK, nh), 1)
        lane1 = lax.broadcasted_iota(jnp.int32, (1, nh), 1)
        sub = lax.broadcasted_iota(jnp.int32, (nh, CHUNK), 0)
        g_rows = jnp.zeros((CHUNK, nh), F32)
        g_cols = jnp.zeros((nh, CHUNK), F32)
        dtot = jnp.zeros((1, nh), F32)
        q_col, q_e, q_dt = [], [], []
        for g in range(SSD_GROUPS):
            gs = slice(g * 256, (g + 1) * 256)
            bg = b_ref[:, g * SSD_STATE:(g + 1) * SSD_STATE].astype(BF16)
            cg = c_ref[:, g * SSD_STATE:(g + 1) * SSD_STATE].astype(BF16)
            cb = _dot(cg, bg, _NT)
            hs4 = hs_ref[gs, :]
            dh4 = dh_scr[gs, :]
            hs4_bf = hs4.astype(BF16)
            dh4_bf = dh4.astype(BF16)
            dy4 = dy_all[:, gs]
            dy4_bf = dy4.astype(BF16)
            xdt4_bf = xdt_all[:, gs].astype(BF16)
            xdw4 = xdw_all[:, gs]
            xdw4_bf = xdw4.astype(BF16)
            dyo4_bf = dyo_all[:, gs].astype(BF16)
            yoff4 = _dot(cg, hs4_bf, _NT) * eax[:, gs]
            dcg = _dot(dyo4_bf, hs4_bf)
            dh_new4 = _dot(dyo4_bf, cg, _TN)
            bdh4 = _dot(bg, dh4_bf, _NT)
            dbg = _dot(xdw4_bf, dh4_bf)
            e4 = xdw4 * bdh4
            q_col.append(dy4 * yoff4 - e4)
            q_e.append(e4)
            hsum = jnp.sum(dh4 * hs4, axis=1, keepdims=True)
            dcb = jnp.zeros((CHUNK, CHUNK), F32)
            dxdts = []
            for k in range(SSD_HPG):
                h = g * SSD_HPG + k
                ks = slice(k * SSD_HEAD_DIM, (k + 1) * SSD_HEAD_DIM)
                lmat = jnp.exp(jnp.where(tri, acum[:, h:h + 1] - acum_t[h:h + 1, :], NEG_BIG))
                mf = cb * lmat
                dm = _dot(dy4_bf[:, ks], xdt4_bf[:, ks], _NT)
                dcb = dcb + dm * lmat
                gmat = dm * mf
                g_rows = g_rows + jnp.where(lane == h, jnp.sum(gmat, axis=1, keepdims=True), 0.0)
                g_cols = g_cols + jnp.where(sub == h, _sum0(gmat), 0.0)
                dxdts.append(_dot(mf.astype(BF16), dy4_bf[:, ks], _TN))
                et = etot[:, h:h + 1]
                dtot = dtot + jnp.where(lane1 == h, _sum0(hsum[ks]) * et, 0.0)
                dh_scr[h * SSD_HEAD_DIM:(h + 1) * SSD_HEAD_DIM, :] = dh4[ks] * et + dh_new4[ks]
            dxdt4 = jnp.concatenate(dxdts, axis=1) + bdh4 * decx[:, gs]
            q_dt.append(dxdt4 * xs_all[:, gs])
            dx4 = dxdt4 * dtx[:, gs]
            if direct:
                dx4 = dx4 + dy4 * dv_ref[:, gs]
            dcb_bf = dcb.astype(BF16)
            dx_ref[:, gs] = dx4
            dx_ref[:, SSD_INNER + g * SSD_STATE:SSD_INNER + (g + 1) * SSD_STATE] = dbg + _dot(dcb_bf, cg, _TN)
            dx_ref[:, SSD_INNER + 1024 + g * SSD_STATE:SSD_INNER + 1024 + (g + 1) * SSD_STATE] = \
                dcg + _dot(dcb_bf, bg)
        e_heads = gather(jnp.concatenate(q_e, axis=1))
        dacum = gather(jnp.concatenate(q_col, axis=1)) + g_rows - g_cols.T
        dal = _dot(tri_t.astype(F32), dacum, precision=HI) + dtot + _sum0(e_heads)
        ddt = gather(jnp.concatenate(q_dt, axis=1)) + dal * a_r
        ddt_raw = ddt * _sig(dt_ref[...] + br_ref[...])
        ddt_ref[...] = ddt_raw
        dal_ref[...] += _sum0(dal * dt) * a_r
        dbias_ref[...] += _sum0(ddt_raw)

    small = lambda shape: pl.BlockSpec(shape, lambda r: (0, 0))
    return pl.pallas_call(
        kern, name=name, grid=(nc,),
        in_specs=[pl.BlockSpec((CHUNK, SSD_INNER), lambda r: (cidx(r), 0)),
                  pl.BlockSpec((CHUNK, SSD_INNER), lambda r: (cidx(r), 0)),
                  pl.BlockSpec((CHUNK, 1024), lambda r: (cidx(r), 2)),
                  pl.BlockSpec((CHUNK, 1024), lambda r: (cidx(r), 3)),
                  pl.BlockSpec((None, SSD_INNER, SSD_STATE), lambda r: (step_of(r), 0, 0)),
                  pl.BlockSpec((CHUNK, nh), lambda r: (cidx(r), 0)),
                  pl.BlockSpec((nh, CHUNK), lambda r: (0, cidx(r))),
                  small((1, nh)), small((nh, 1)), small((1, nh)), small((nh, 1)), small((1, SSD_INNER)),
                  small((nh, SSD_INNER)), small((SSD_INNER, nh))],
        out_specs=[pl.BlockSpec((CHUNK, SSD_CONV_DIM), lambda r: (cidx(r), 0)),
                   pl.BlockSpec((CHUNK, nh), lambda r: (cidx(r), 0)),
                   small((1, nh)), small((1, nh))],
        out_shape=[jax.ShapeDtypeStruct((n, SSD_CONV_DIM), F32), jax.ShapeDtypeStruct((n, nh), F32),
                   jax.ShapeDtypeStruct((1, nh), F32), jax.ShapeDtypeStruct((1, nh), F32)],
        scratch_shapes=[pltpu.VMEM((SSD_INNER, SSD_STATE), F32)],
        compiler_params=_params("arbitrary"),
    )(dy, xbc, xbc, xbc, hs, dt_raw, dtT_raw, bias_r, bias_c, alog_r, alog_c, dvec, _head_spread(),
      _head_spread().T)


def _gm_spatial_fwd(gu, gvn, ws, bst, *, name):
    n = gu.shape[0]

    def kern(gu_ref, gv_ref, ws_ref, bs_ref, o_ref):
        for g in range(GM_GROUPS):
            sl = slice(g * GM_GROUP_DIM, (g + 1) * GM_GROUP_DIM)
            s = _dot(ws_ref[g], gv_ref[:, sl]) + bs_ref[:, g:g + 1]
            o_ref[:, sl] = (gu_ref[:, sl] * s).astype(BF16)

    spec = pl.BlockSpec((CHUNK, GM_INNER), lambda i: (i, 0))
    return pl.pallas_call(
        kern, name=name, grid=(n // CHUNK,),
        in_specs=[spec, spec, pl.BlockSpec(ws.shape, lambda i: (0, 0, 0)), pl.BlockSpec(bst.shape, lambda i: (0, 0))],
        out_specs=spec, out_shape=jax.ShapeDtypeStruct((n, GM_INNER), BF16),
        compiler_params=_params("parallel"),
    )(gu, gvn, ws, bst)


def _gm_spatial_bwd(dt, gu, gvn, ws, wst, bst, *, name):
    n = gu.shape[0]

    def kern(dt_ref, gu_ref, gv_ref, ws_ref, wst_ref, bs_ref, dgu_ref, dgv_ref, dws_ref, dbs_ref):
        @pl.when(pl.program_id(0) == 0)
        def _():
            dws_ref[...] = jnp.zeros_like(dws_ref)
            dbs_ref[...] = jnp.zeros_like(dbs_ref)

        lane = lax.broadcasted_iota(jnp.int32, (CHUNK, GM_GROUPS), 1)
        dbs = jnp.zeros((CHUNK, GM_GROUPS), F32)
        for g in range(GM_GROUPS):
            sl = slice(g * GM_GROUP_DIM, (g + 1) * GM_GROUP_DIM)
            gv = gv_ref[:, sl]
            s = _dot(ws_ref[g], gv) + bs_ref[:, g:g + 1]
            d = dt_ref[:, sl]
            dgu_ref[:, sl] = d * s
            ds = d * gu_ref[:, sl]
            ds_bf = ds.astype(BF16)
            dws_ref[g] += _dot(ds_bf, gv, _NT)
            dgv_ref[:, sl] = _dot(wst_ref[g], ds_bf)
            dbs = dbs + jnp.where(lane == g, jnp.sum(ds, axis=1, keepdims=True), 0.0)
        dbs_ref[...] += dbs

    spec = pl.BlockSpec((CHUNK, GM_INNER), lambda i: (i, 0))
    wspec = pl.BlockSpec(ws.shape, lambda i: (0, 0, 0))
    bspec = pl.BlockSpec(bst.shape, lambda i: (0, 0))
    return pl.pallas_call(
        kern, name=name, grid=(n // CHUNK,),
        in_specs=[spec, spec, spec, wspec, wspec, bspec],
        out_specs=[spec, spec, wspec, bspec],
        out_shape=[jax.ShapeDtypeStruct((n, GM_INNER), F32), jax.ShapeDtypeStruct((n, GM_INNER), F32),
                   jax.ShapeDtypeStruct(ws.shape, F32), jax.ShapeDtypeStruct(bst.shape, F32)],
        compiler_params=_params("arbitrary"),
    )(dt, gu, gvn, ws, wst, bst)


def _adamw(parts, w, m, v, *, name, tm=256, sel=(), into=None):
    ns, r, wd = parts.shape
    tm = _pick(r, tm, 8)
    tc = wd
    if tm < 64 and wd % 256 == 0:
        tm, tc = r, 256
    lead = len(sel)
    assert w.shape[lead:] == (r, wd) and lead == w.ndim - 2

    def kern(*refs):
        p_ref, w_ref, m_ref, v_ref = refs[:4]
        g_ref, d_ref, nm_ref, nv_ref = refs[-4:]
        g = p_ref[0].astype(F32)
        for s in range(1, ns):
            g = g + p_ref[s].astype(F32)
        m2 = ADAM_B1 * m_ref[...] + (1.0 - ADAM_B1) * g
        v2 = ADAM_B2 * v_ref[...] + (1.0 - ADAM_B2) * (g * g)
        m_hat = m2 / (1.0 - ADAM_B1 ** ADAM_STEP)
        v_hat = v2 / (1.0 - ADAM_B2 ** ADAM_STEP)
        g_ref[...] = g
        d_ref[...] = -ADAM_LR * (m_hat / (jnp.sqrt(v_hat) + ADAM_EPS) + ADAM_WD * w_ref[...])
        nm_ref[...] = m2
        nv_ref[...] = v2

    spec = pl.BlockSpec((None,) * lead + (tm, tc), lambda i, j: tuple(sel) + (i, j))
    extra, aliases = [], {}
    if into is not None:
        extra = list(into)
        aliases = {4 + k: k for k in range(4)}
    return pl.pallas_call(
        kern, name=name, grid=(r // tm, wd // tc),
        in_specs=[pl.BlockSpec((ns, tm, tc), lambda i, j: (0, i, j)), spec, spec, spec] +
                 [pl.BlockSpec(memory_space=pl.ANY)] * len(extra),
        out_specs=[spec] * 4, out_shape=[jax.ShapeDtypeStruct(w.shape, F32)] * 4,
        input_output_aliases=aliases,
        compiler_params=_params("parallel", "parallel"),
    )(parts, w, m, v, *extra)


def _sum_slots(parts, *, name, scale_by=None):
    ns, r, wd = parts.shape

    def kern(*refs):
        p_ref, o_ref = refs[0], refs[-1]
        g = p_ref[0]
        for s in range(1, ns):
            g = g + p_ref[s]
        if scale_by is not None:
            g = g * _dsilu(refs[1][...])
        o_ref[...] = g

    args = [parts] + ([] if scale_by is None else [scale_by])
    return pl.pallas_call(kern, name=name, out_shape=jax.ShapeDtypeStruct((r, wd), F32),
                          compiler_params=pltpu.CompilerParams(vmem_limit_bytes=VMEM_LIMIT_BYTES))(*args)


def _mesh_pos():
    x, y, c = lax.axis_index("x"), lax.axis_index("y"), lax.axis_index("c")
    return x, y, c, 4 * x + 2 * y + c


def _flip(x, y, c, f):
    fx, fy, fc = (f >> 2) & 1, (f >> 1) & 1, f & 1
    px = 1 - x if fx else x
    py = 1 - y if fy else y
    pc = 1 - c if fc else c
    return (px, py, pc), 4 * px + 2 * py + pc


_HBM_SPEC = pl.BlockSpec(memory_space=pltpu.HBM)


def _exchange(arrays, *, scatter, name):
    na = len(arrays)
    if scatter:
        out_shape = [jax.ShapeDtypeStruct(a.shape, a.dtype) for a in arrays]
    else:
        out_shape = [jax.ShapeDtypeStruct((NDEV,) + a.shape, a.dtype) for a in arrays]

    out_shape.append(jax.ShapeDtypeStruct((8, 128), F32))

    def body(*refs):
        ins, outs = refs[:na], refs[na:2 * na]
        send_sems, recv_sems, local_sems = refs[2 * na + 1:]
        refs[2 * na][...] = jnp.zeros((8, 128), F32)
        x, y, c, me = _mesh_pos()
        copies = []
        for i in range(na):
            src_own = ins[i].at[me] if scatter else ins[i]
            lc = pltpu.make_async_copy(src_own, outs[i].at[me], local_sems.at[i])
            lc.start()
            copies.append(lc)
        sends = []
        for f in range(1, NDEV):
            peer, pidx = _flip(x, y, c, f)
            for i in range(na):
                k = i * (NDEV - 1) + f - 1
                src = ins[i].at[pidx] if scatter else ins[i]
                cp = pltpu.make_async_remote_copy(
                    src_ref=src, dst_ref=outs[i].at[me], send_sem=send_sems.at[k], recv_sem=recv_sems.at[k],
                    device_id=peer, device_id_type=pl.DeviceIdType.MESH)
                cp.start()
                sends.append(cp)
        for f in range(1, NDEV):
            peer, pidx = _flip(x, y, c, f)
            for i in range(na):
                k = i * (NDEV - 1) + f - 1
                src = ins[i].at[pidx] if scatter else ins[i]
                pltpu.make_async_remote_copy(
                    src_ref=src, dst_ref=outs[i].at[pidx], send_sem=send_sems.at[k], recv_sem=recv_sems.at[k],
                    device_id=peer, device_id_type=pl.DeviceIdType.MESH).wait_recv()
        for cp in sends:
            cp.wait_send()
        for lc in copies:
            lc.wait()

    res = pl.pallas_call(
        body, name=name, out_shape=out_shape, in_specs=[_HBM_SPEC] * na,
        out_specs=[_HBM_SPEC] * na + [pl.BlockSpec(memory_space=pltpu.VMEM)],
        scratch_shapes=[pltpu.SemaphoreType.DMA((na * (NDEV - 1),)), pltpu.SemaphoreType.DMA((na * (NDEV - 1),)),
                        pltpu.SemaphoreType.DMA((na,))],
        compiler_params=pltpu.CompilerParams(has_side_effects=True),
    )(*arrays)
    return res[:na], res[na][0, 0]


_SEM_SPEC = pl.BlockSpec(memory_space=pltpu.SEMAPHORE)
_DATAFLOW = pltpu.SideEffectType.DATAFLOW_SIDE_EFFECTING


def _split_copies(srcs, lands, send_sems, recv_sems, scatter, arriving):
    x, y, c, me = _mesh_pos()
    copies = []
    for i in range(len(srcs)):
        for f in range(1, NDEV):
            peer, pidx = _flip(x, y, c, f)
            k = i * (NDEV - 1) + f - 1
            copies.append(pltpu.make_async_remote_copy(
                src_ref=srcs[i].at[pidx] if scatter else srcs[i], dst_ref=lands[i].at[pidx if arriving else me],
                send_sem=send_sems.at[k], recv_sem=recv_sems.at[k], device_id=peer,
                device_id_type=pl.DeviceIdType.MESH))
    return copies


def _exchange_start(srcs, lands, *, scatter, name):
    na = len(srcs)
    nsem = na * (NDEV - 1)

    def body(*refs):
        ins_src, ins_land = refs[:na], refs[na:2 * na]
        send_sems, recv_sems = refs[2 * na], refs[2 * na + 1]
        token = refs[-1]
        for cp in _split_copies(ins_src, ins_land, send_sems, recv_sems, scatter, False):
            cp.start()
        token[...] = jnp.zeros_like(token)

    thru = [pltpu.HBM(a.shape, a.dtype) for a in list(srcs) + list(lands)]
    res = pl.pallas_call(
        body, name=name,
        out_shape=(pltpu.SemaphoreType.DMA((nsem,)), pltpu.SemaphoreType.DMA((nsem,)), *thru,
                   jax.ShapeDtypeStruct((8, 128), F32)),
        in_specs=[_HBM_SPEC] * (2 * na),
        out_specs=(_SEM_SPEC, _SEM_SPEC, *([_HBM_SPEC] * (2 * na)), pl.BlockSpec(memory_space=pltpu.VMEM)),
        input_output_aliases={i: 2 + i for i in range(2 * na)},
        compiler_params=pltpu.CompilerParams(has_side_effects=_DATAFLOW),
    )(*[pltpu.with_memory_space_constraint(a, pltpu.HBM) for a in list(srcs) + list(lands)])
    send_sems, recv_sems = res[0], res[1]
    return send_sems, recv_sems, res[2:2 + na], res[2 + na:2 + 2 * na], res[-1][0, 0]


def _exchange_wait(send_sems, recv_sems, srcs, lands, after, *, scatter, name):
    na = len(srcs)

    def body(*refs):
        ins_src, ins_land = refs[:na], refs[na:2 * na]
        s_sems, r_sems = refs[2 * na], refs[2 * na + 1]
        for cp in _split_copies(ins_src, ins_land, s_sems, r_sems, scatter, False):
            cp.wait_send()
        for cp in _split_copies(ins_src, ins_land, s_sems, r_sems, scatter, True):
            cp.wait_recv()

    thru = [pltpu.HBM(a.shape, a.dtype) for a in list(srcs) + list(lands)]
    res = pl.pallas_call(
        body, name=name, out_shape=tuple(thru),
        in_specs=[_HBM_SPEC] * (2 * na) + [_SEM_SPEC, _SEM_SPEC, pl.BlockSpec(memory_space=pl.ANY)],
        out_specs=tuple([_HBM_SPEC] * (2 * na)),
        input_output_aliases={i: i for i in range(2 * na)},
        compiler_params=pltpu.CompilerParams(has_side_effects=_DATAFLOW),
    )(*srcs, *lands, send_sems, recv_sems, after)
    return res[na:]


def _landing(block, me):
    buf = lax.empty((NDEV,) + block.shape, block.dtype)
    return lax.dynamic_update_slice_in_dim(buf, block[None], me, axis=0)


def _seg_kw(nseg, n_ctx, tm):
    return dict(nseg=nseg, seg_blocks=(n_ctx // tm if nseg == 2 else 0))


def _ffn_fwd(tag, h, gpre, gpost, shift, scale, gate, w, *, nseg, n_ctx, tm):
    n = h.shape[0]
    kw = _seg_kw(nseg, n_ctx, tm)
    (u,) = _rowwise(tag + "_pre", _pre_fwd_fn, n, [h], [("full", gpre), ("seg", shift), ("seg", scale)],
                    [(D_MODEL, BF16)], tm=tm, **kw)
    if "early" in w:
        w.update(w.pop("early")(u))
    s, a, b = _mm_glu(u, w["win_t"], name=tag + "_glu")
    if "late" in w:
        w.update(w.pop("late")(s))
    y, ho = _mm_rows(s, w["wout"], functools.partial(_out_post_fn, 0.5), [h], [("full", gpost), ("seg", gate)],
                     [(D_MODEL, F32), (D_MODEL, F32)], name=tag + "_out", tk=FFN_DIM, n_ctx=n_ctx)
    return ho, dict(h=h, u=u, s=s, a=a, b=b, y=y)


def _ffn_bwd(tag, dho, sv, gpre, gpost, scale, gate, w, put, *, nseg, n_ctx, tm):
    n = dho.shape[0]
    kw = _seg_kw(nseg, n_ctx, tm)
    dy, dgate, dgpost = _rowwise(tag + "_postb", functools.partial(_post_bwd_fn, 0.5), n, [dho, sv["y"]],
                                 [("full", gpost), ("seg", gate)], [(D_MODEL, BF16)], [D_MODEL, D_MODEL], tm=tm, **kw)
    tok = put("w_out", _mm_tn(sv["s"], dy, name=tag + "_dwout", tm=1408, tn=1024, col_blocks=1))
    dp = _mm_glu_bwd(dy, w["wout"], sv["a"], sv["b"], name=tag + "_ds")
    tok2 = put("w_in", _mm_tn(dp, sv["u"], name=tag + "_dwin", tm=1408, tn=1024, col_blocks=1))
    for t in (tok, tok2):
        if t is not None:
            gpre = gpre + t
    dh, dshift, dscale, dgpre = _mm_rows(dp, w["win_t"], _pre_bwd_fn, [sv["h"], dho],
                                         [("full", gpre), ("seg", scale)], [(D_MODEL, F32)],
                                         [D_MODEL, D_MODEL, D_MODEL], name=tag + "_du", tk=FFN_DIM, n_ctx=n_ctx)
    return dh, None, dict(shift=dshift, scale=dscale, gate=dgate, gpre=dgpre, gpost=dgpost)


def _local_step(x, ctx, target, mods, norm_g, get_w, small, put_grad):
    t_len, n_ctx = x.shape[0], ctx.shape[0]
    n0 = t_len + n_ctx
    tm0 = _pick(n_ctx, 256, 8)
    tm1 = _pick(t_len, 512, 8)
    ncc = n_ctx // CHUNK
    g = {}

    def modrow(i, k, nseg):
        mc, mx = mods[i]
        if nseg == 2:
            return jnp.stack([mc[k], mx[k]])[:, None, :]
        return mx[k][None, None, :]

    pending = [None]

    def gvec(i, k):
        v = norm_g[i, k][None, :]
        if pending[0] is not None:
            v = v + pending[0]
            pending[0] = None
        return v

    xc = jnp.concatenate([ctx, x], axis=0)
    L0 = dict(nseg=2, n_ctx=n_ctx, tm=tm0)
    wts = dict(get_w("ffn00", xc))
    h1, sv_f01 = _ffn_fwd("l0f1", xc, gvec(0, 0), gvec(0, 1), modrow(0, 0, 2), modrow(0, 1, 2), modrow(0, 2, 2),
                          wts["ffn00"], **L0)
    kw0 = _seg_kw(2, n_ctx, tm0)
    (um0,) = _rowwise("l0m_pre", _pre_fwd_fn, n0, [h1], [("full", gvec(0, 2)), ("seg", modrow(0, 3, 2)),
                                                         ("seg", modrow(0, 4, 2))], [(D_MODEL, BF16)], tm=tm0, **kw0)
    wts.update(get_w("ssd", um0))
    win_ssd = wts["ssd_win_t"]
    nh = SSD_HEADS
    dt_blk = (SSD_INNER + SSD_CONV_DIM) // (2 * nh)
    z = _mm(um0, win_ssd, out_dtype=F32, name="ssd_z", rhs_t=True, n=SSD_INNER)
    xbc_pre = _mm(um0, win_ssd, out_dtype=F32, name="ssd_xbc", rhs_t=True, n=SSD_CONV_DIM,
                  b_off=(SSD_INNER // 1024, 0))
    dtr = _mm(um0, win_ssd, out_dtype=F32, name="ssd_dt", rhs_t=True, n=2 * nh, b_off=(dt_blk, 0))
    cpre, xbc = _conv_fwd(xbc_pre, small["conv_w8"], small["conv_b"], n_ctx=n_ctx, name="ssd_conv")
    nh = SSD_HEADS
    dt_dir = [dtr[:, :nh], dtr[:, nh:2 * nh]]
    dtT_dir = [d.T for d in dt_dir]
    bias_r = [small["dt_bias"][d][None, :] for d in range(2)]
    bias_c = [small["dt_bias"][d][:, None] for d in range(2)]
    alog_r = [small["a_log"][d][None, :] for d in range(2)]
    alog_c = [small["a_log"][d][:, None] for d in range(2)]
    ys, hss = [], []
    for d in range(2):
        yd, hsd = _ssd_scan_fwd(xbc, dt_dir[d], dtT_dir[d], bias_r[d], bias_c[d], alog_r[d], alog_c[d],
                                rev=(d == 1), n_ctx_chunks=ncc, name=f"ssd_scan{d}")
        ys.append(yd)
        hss.append(hsd)
    dvec = jnp.repeat(small["ssd_d"], SSD_HEAD_DIM)[None, :]
    ngv = small["ssd_norm_g"][None, :]
    gate_rows = [ys[0], ys[1], (xbc, SSD_INNER, 0, 0), z]
    off = n_ctx // tm0
    lat = lambda r: (r[0], r[1], r[2], off) if isinstance(r, tuple) else (r, r.shape[1], 0, off)
    (yn,) = _rowwise("ssd_gate", _ssdgate_fwd_fn, t_len, [lat(r) for r in gate_rows],
                     [("full", dvec), ("full", ngv)], [(SSD_INNER, BF16)], tm=tm0)
    h1x = h1[n_ctx:]
    L1 = dict(nseg=1, n_ctx=0, tm=_pick(t_len, 512, 8))
    if "late" in wts:
        wts.update(wts.pop("late")(yn))
    yo0, h2 = _mm_rows(yn, wts["ssd_wout"], functools.partial(_out_post_fn, 1.0), [h1x],
                       [("full", gvec(0, 3)), ("seg", modrow(0, 5, 1))], [(D_MODEL, F32), (D_MODEL, F32)],
                       name="ssd_out", tk=SSD_INNER)
    wts.update(get_w("ffn01", h2))
    h3, sv_f02 = _ffn_fwd("l0f2", h2, gvec(0, 4), gvec(0, 5), modrow(0, 6, 1), modrow(0, 7, 1), modrow(0, 8, 1),
                          wts["ffn01"], **L1)

    wts.update(get_w("ffn10", h3))
    h4, sv_f11 = _ffn_fwd("l1f1", h3, gvec(1, 0), gvec(1, 1), modrow(1, 0, 1), modrow(1, 1, 1), modrow(1, 2, 1),
                          wts["ffn10"], **L1)
    (um1,) = _rowwise("l1m_pre", _pre_fwd_fn, t_len, [h4], [("full", gvec(1, 2)), ("seg", modrow(1, 3, 1)),
                                                            ("seg", modrow(1, 4, 1))], [(D_MODEL, BF16)], tm=tm1)
    wts.update(get_w("gm", um1))
    p1 = _mm(um1, wts["gm_win"], out_dtype=F32, name="gm_in", tm=2048)
    vg = small["gm_v_g"][None, :]
    vb = small["gm_v_b"][None, :]
    gu, gvn = _rowwise("gm_act", _gm_act_fwd_fn, t_len, [p1], [("full", vg), ("full", vb)],
                       [(GM_INNER, F32), (GM_INNER, BF16)], tm=256)
    ws_bf = small["gm_w_s"].astype(BF16)
    wst_bf = jnp.swapaxes(small["gm_w_s"], 1, 2).astype(BF16)
    bst = small["gm_b_s"].T
    tgm = _gm_spatial_fwd(gu, gvn, ws_bf, bst, name="gm_spatial")
    yo1, h5 = _mm_rows(tgm, wts["gm_wout"], functools.partial(_out_post_fn, 1.0), [h4],
                       [("full", gvec(1, 3)), ("seg", modrow(1, 5, 1))], [(D_MODEL, F32), (D_MODEL, F32)],
                       name="gm_out", tk=GM_INNER)
    wts.update(get_w("ffn11", h5))
    h6, sv_f12 = _ffn_fwd("l1f2", h5, gvec(1, 4), gvec(1, 5), modrow(1, 6, 1), modrow(1, 7, 1), modrow(1, 8, 1),
                          wts["ffn11"], **L1)

    dh, loss_parts = _rowwise("loss", _loss_fn, t_len, [h6, target], [], [(D_MODEL, F32)], [D_MODEL], tm=tm1)

    zero = jnp.zeros((D_MODEL,), F32)
    dmx = [[zero] * N_MOD for _ in range(2)]
    dmc = [[zero] * N_MOD for _ in range(2)]
    dng = [[zero] * 6 for _ in range(2)]

    def put_mod(i, k, acc):
        if acc.shape[0] == 2:
            dmc[i][k] = dmc[i][k] + acc[0, 0]
            dmx[i][k] = dmx[i][k] + acc[1, 0]
        else:
            dmx[i][k] = dmx[i][k] + acc[0, 0]

    def put_g(i, k, acc):
        dng[i][k] = dng[i][k] + jnp.sum(acc[:, 0], axis=0)

    def ffn_back(tag, i, j, dho, sv, w, lay):
        nseg = lay["nseg"]
        base = 0 if j == 0 else 6
        gi = 0 if j == 0 else 4
        dh_in, pending[0], s = _ffn_bwd(tag, dho, sv, gvec(i, gi), gvec(i, gi + 1), modrow(i, base + 1, nseg),
                                        modrow(i, base + 2, nseg), w, functools.partial(put_grad, f"ffn{i}{j}"), **lay)
        put_mod(i, base, s["shift"])
        put_mod(i, base + 1, s["scale"])
        put_mod(i, base + 2, s["gate"])
        put_g(i, gi, s["gpre"])
        put_g(i, gi + 1, s["gpost"])
        return dh_in

    dh = ffn_back("l1f2", 1, 1, dh, sv_f12, wts["ffn11"], L1)
    dyo, dgate, dgp = _rowwise("l1m_postb", functools.partial(_post_bwd_fn, 1.0), t_len, [dh, yo1],
                               [("full", gvec(1, 3)), ("seg", modrow(1, 5, 1))], [(D_MODEL, BF16)],
                               [D_MODEL, D_MODEL], tm=tm1)
    put_mod(1, 5, dgate)
    put_g(1, 3, dgp)
    put_grad("gm", "w_out", _mm_tn(tgm, dyo, name="gm_dwout", tn=1024, col_blocks=1))
    dtg = _mm(dyo, wts["gm_wout"], out_dtype=F32, name="gm_dt", rhs_t=True)
    dgu, dgvn, dws, dbst = _gm_spatial_bwd(dtg, gu, gvn, ws_bf, wst_bf, bst, name="gm_spatialb")
    g["gm_w_s"] = dws
    g["gm_b_s"] = dbst.T
    dp1, dvg, dvb = _rowwise("gm_actb", _gm_act_bwd_fn, t_len, [p1, dgu, dgvn], [("full", vg)],
                             [(2 * GM_INNER, BF16)], [GM_INNER, GM_INNER], tm=256)
    g["gm_v_g"] = dvg[0, 0]
    g["gm_v_b"] = dvb[0, 0]
    pending[0] = put_grad("gm", "w_in", _mm_tn(um1, dp1, name="gm_dwin", tm=1024, col_blocks=NDEV))
    dh, dsh, dsc, dgp = _mm_rows(dp1, wts["gm_win"], _pre_bwd_fn, [h4, dh],
                                 [("full", gvec(1, 2)), ("seg", modrow(1, 4, 1))], [(D_MODEL, F32)],
                                 [D_MODEL, D_MODEL, D_MODEL], name="gm_dum", tk=2048, rhs_t=True)
    put_mod(1, 3, dsh)
    put_mod(1, 4, dsc)
    put_g(1, 2, dgp)
    dh = ffn_back("l1f1", 1, 0, dh, sv_f11, wts["ffn10"], L1)

    dh = ffn_back("l0f2", 0, 1, dh, sv_f02, wts["ffn01"], L1)
    dyo, dgate, dgp = _rowwise("l0m_postb", functools.partial(_post_bwd_fn, 1.0), t_len, [dh, yo0],
                               [("full", gvec(0, 3)), ("seg", modrow(0, 5, 1))], [(D_MODEL, BF16)],
                               [D_MODEL, D_MODEL], tm=tm1)
    put_mod(0, 5, dgate)
    put_g(0, 3, dgp)
    tok = put_grad("ssd", "w_out", _mm_tn(yn, dyo, name="ssd_dwout", tn=1024, col_blocks=1))
    dyn = _mm(dyo, wts["ssd_wout"], out_dtype=F32, name="ssd_dyn", rhs_t=True)
    dy_ssd, dz, dngv, ddv = _rowwise("ssd_gateb", _ssdgate_bwd_fn, n0,
                                     [(dyn, SSD_INNER, 0, -(n_ctx // tm0))] + gate_rows,
                                     [("full", dvec), ("full", ngv if tok is None else ngv + tok)],
                                     [(SSD_INNER, F32), (SSD_INNER, BF16)],
                                     [SSD_INNER, SSD_INNER], tm=tm0)
    g["ssd_norm_g"] = dngv[0, 0]
    g["ssd_D"] = jnp.sum(ddv[0, 0].reshape(SSD_HEADS, SSD_HEAD_DIM), axis=1)
    dxbcs, ddts, dalogs, dbiases = [], [], [], []
    for d in range(2):
        dxd, ddtd, dal, dbi = _ssd_scan_bwd(dy_ssd, xbc, hss[d], dt_dir[d], dtT_dir[d], bias_r[d], bias_c[d],
                                            alog_r[d], alog_c[d], dvec, rev=(d == 1), n_ctx_chunks=ncc,
                                            direct=(d == 0), name=f"ssd_scanb{d}")
        dxbcs.append(dxd)
        ddts.append(ddtd)
        dalogs.append(dal[0])
        dbiases.append(dbi[0])
    g["ssd_A_log"] = jnp.stack(dalogs)
    g["ssd_dt_bias"] = jnp.stack(dbiases)
    dxbc_pre, dcw8, dcb = _conv_bwd(dxbcs[0], dxbcs[1], cpre, xbc_pre, small["conv_w8"], n_ctx=n_ctx, name="ssd_convb")
    g["ssd_conv_w"] = dcw8[:SSD_CONV]
    g["ssd_conv_b"] = dcb[0]
    ddt_bf = jnp.concatenate([ddts[0], ddts[1]], axis=1).astype(BF16)
    n_in = SSD_INNER + SSD_CONV_DIM + 2 * nh
    dw_t = _mm_tn(dz, um0, name="ssd_dwz", col_blocks=1, stack=(n_in, 0, None))
    dw_t = _mm_tn(dxbc_pre, um0, name="ssd_dwxbc", col_blocks=1, stack=(n_in, SSD_INNER, dw_t))
    dw_t = _mm_tn(ddt_bf, um0, name="ssd_dwdt", col_blocks=1, stack=(n_in, SSD_INNER + SSD_CONV_DIM, dw_t))
    pending[0] = put_grad("ssd", "w_in", dw_t)
    dum0 = _mm(dz, win_ssd, out_dtype=F32, name="ssd_dum_z", tk=SSD_INNER, n=D_MODEL)
    dum0 = _mm(dxbc_pre, win_ssd, out_dtype=F32, name="ssd_dum_x", tk=SSD_INNER, n=D_MODEL,
               b_off=(SSD_INNER // SSD_INNER, 0), add=dum0)
    dum0 = _mm(ddt_bf, win_ssd, out_dtype=F32, name="ssd_dum_dt", tk=2 * nh, n=D_MODEL, b_off=(dt_blk, 0), add=dum0)
    dh0, dsh, dsc, dgp = _rowwise("l0m_preb", _pre_bwd_fn, n0, [dum0, h1, (dh, D_MODEL, 0, -(n_ctx // tm0))],
                                  [("full", gvec(0, 2)), ("seg", modrow(0, 4, 2))], [(D_MODEL, F32)],
                                  [D_MODEL, D_MODEL, D_MODEL], tm=tm0, **kw0)
    put_mod(0, 3, dsh)
    put_mod(0, 4, dsc)
    put_g(0, 2, dgp)
    dh0 = ffn_back("l0f1", 0, 0, dh0, sv_f01, wts["ffn00"], L0)
    grad_x = dh0[n_ctx:]
    g["norm_g"] = jnp.stack([jnp.stack(r) for r in dng])
    g["dmx"] = jnp.stack([jnp.concatenate(r) for r in dmx])
    g["dmc"] = jnp.stack([jnp.concatenate(r) for r in dmc])
    return loss_parts[0], grad_x, g


GROUPS = ("ffn00", "ssd", "ffn01", "ffn10", "gm", "ffn11")


TRANSPOSED_IN = ("ffn", "ssd")


def _is_transposed(group):
    return group.startswith(TRANSPOSED_IN)


def _mats_in(group, win_l):
    if _is_transposed(group):
        return {("win_t" if group.startswith("ffn") else group + "_win_t"): win_l.reshape(-1, win_l.shape[2])}
    return {group + "_win": win_l}


def _mats_out(group, wout_l):
    pre = "" if group.startswith("ffn") else group + "_"
    return {pre + "wout": wout_l.reshape(-1, wout_l.shape[2])}


def _group_mats(group, lands):
    m = {**_mats_in(group, lands[0]), **_mats_out(group, lands[1])}
    return {group: m} if group.startswith("ffn") else m


def _grad_blocks(which, grad):
    if grad.ndim == 3:
        return grad if grad.shape[0] == NDEV else grad.reshape(NDEV, grad.shape[1] // NDEV, grad.shape[2])
    if which == "w_in":
        k, n = grad.shape
        return jnp.transpose(grad.reshape(k, NDEV, n // NDEV), (1, 0, 2)).astype(BF16)
    return grad.reshape(NDEV, grad.shape[0] // NDEV, grad.shape[1]).astype(BF16)


def kernel(x, c, ctx, c_ctx, ada_w, ada_b, norm_g, ffn_w_in, ffn_w_out, ssd_w_in, ssd_conv_w, ssd_conv_b, ssd_dt_bias, ssd_A_log, ssd_D, ssd_norm_g, ssd_w_out, gm_w_in, gm_v_g, gm_v_b, gm_w_s, gm_b_s, gm_w_out, loss_target, m_c_ctx, m_ada_w, m_ada_b, m_norm_g, m_ffn_w_in, m_ffn_w_out, m_ssd_w_in, m_ssd_conv_w, m_ssd_conv_b, m_ssd_dt_bias, m_ssd_A_log, m_ssd_D, m_ssd_norm_g, m_ssd_w_out, m_gm_w_in, m_gm_v_g, m_gm_v_b, m_gm_w_s, m_gm_b_s, m_gm_w_out, v_c_ctx, v_ada_w, v_ada_b, v_norm_g, v_ffn_w_in, v_ffn_w_out, v_ssd_w_in, v_ssd_conv_w, v_ssd_conv_b, v_ssd_dt_bias, v_ssd_A_log, v_ssd_D, v_ssd_norm_g, v_ssd_w_out, v_gm_w_in, v_gm_v_g, v_gm_v_b, v_gm_w_s, v_gm_b_s, v_gm_w_out):
    me = 4 * lax.axis_index("x") + 2 * lax.axis_index("y") + lax.axis_index("c")
    d = D_MODEL
    ncol = N_MOD * d // NDEV

    small_pack = jnp.concatenate([c.reshape(-1), norm_g.reshape(-1), ssd_conv_w.reshape(-1),
                                  gm_v_g.reshape(-1), gm_v_b.reshape(-1)])[None, :]
    (sp,), _ = _exchange([small_pack], scatter=False, name="gather_small")
    sp = sp[:, 0]
    o = 0
    c_all = sp[:, o:o + d]; o += d
    ng_all = sp[:, o:o + 2 * 6 * 128].reshape(NDEV, 2, 6, 128); o += 2 * 6 * 128
    cw_all = sp[:, o:o + SSD_CONV * 512].reshape(NDEV, SSD_CONV, 512); o += SSD_CONV * 512
    vg_all = sp[:, o:o + 256]; o += 256
    vb_all = sp[:, o:o + 256]; o += 256
    norm_g_full = jnp.transpose(ng_all, (1, 2, 0, 3)).reshape(2, 6, d)
    conv_w_full = jnp.transpose(cw_all, (1, 0, 2)).reshape(SSD_CONV, SSD_CONV_DIM)
    gm_v_g_full = vg_all.reshape(-1)
    gm_v_b_full = vb_all.reshape(-1)

    c16 = jnp.concatenate([c_all, jnp.broadcast_to(c_ctx[None, :], (NDEV, d))], axis=0)
    ada_b_loc = lax.dynamic_slice_in_dim(ada_b, me * ncol, ncol, axis=1)
    mods_loc = jnp.stack([_mm_f32(c16, ada_w[i], name=f"ada_mod{i}", silu_a=True, bias=ada_b_loc[i][None, :])
                          for i in range(2)])
    (mods_all,), mods_done = _exchange([mods_loc], scatter=False, name="gather_mods")

    tr = lambda a: jnp.swapaxes(a, -1, -2)
    shard = {"ssd": (tr(ssd_w_in)[0], ssd_w_out[0]), "gm": (gm_w_in[0], gm_w_out[0])}
    for i in range(2):
        for j in range(2):
            shard[f"ffn{i}{j}"] = (tr(ffn_w_in)[i, j], ffn_w_out[i, j])
    apart = GROUPS[:2]
    units = []
    for grp in GROUPS:
        units += [(grp + "_in", grp, (0,)), (grp + "_out", grp, (1,))] if grp in apart else [(grp, grp, (0, 1))]
    gathers = {}
    started = mods_done
    for unit, grp, idx in units:
        srcs = [(shard[grp][k] + started).astype(BF16) for k in idx]
        st = _exchange_start(srcs, [_landing(s, me) for s in srcs], scatter=False, name="gather_start_" + unit)
        gathers[unit] = st[:4]
        started = st[4]

    def fetch(unit, after):
        return _exchange_wait(*gathers[unit], after, scatter=False, name="gather_wait_" + unit)

    def get_w(grp, after):
        if grp not in apart:
            return _group_mats(grp, fetch(grp, after))
        early = lambda later: _mats_in(grp, fetch(grp + "_in", later)[0])
        late = lambda later: _mats_out(grp, fetch(grp + "_out", later)[0])
        if grp.startswith("ffn"):
            return {grp: dict(early=early, late=late)}
        return dict(early(after), late=late)

    scatters = {}
    held = {}

    def put_grad(grp, which, grad):
        if grp in apart:
            unit, blocks = grp + "_" + which[2:], [_grad_blocks(which, grad)]
        else:
            held[grp, which] = _grad_blocks(which, grad)
            if (grp, "w_in") not in held or (grp, "w_out") not in held:
                return None
            unit, blocks = grp, [held[grp, "w_in"], held[grp, "w_out"]]
        own = [lax.dynamic_index_in_dim(b, me, axis=0, keepdims=False) for b in blocks]
        st = _exchange_start(blocks, [_landing(o_, me) for o_ in own], scatter=True, name="scatter_start_" + unit)
        scatters[unit] = st[:4]
        return st[4]

    mods_rows = jnp.transpose(mods_all, (1, 2, 0, 3)).reshape(2, 2 * NDEV, N_MOD * d) + started
    mx = lax.dynamic_index_in_dim(mods_rows, me, axis=1, keepdims=False).reshape(2, N_MOD, d)
    mc = mods_rows[:, NDEV].reshape(2, N_MOD, d)
    mods = [(mc[i], mx[i]) for i in range(2)]

    small = dict(conv_w8=jnp.pad(conv_w_full, ((0, 8 - SSD_CONV), (0, 0))), conv_b=ssd_conv_b, dt_bias=ssd_dt_bias[0],
                 a_log=ssd_A_log[0], ssd_d=ssd_D[0], ssd_norm_g=ssd_norm_g[0], gm_v_g=gm_v_g_full,
                 gm_v_b=gm_v_b_full, gm_w_s=gm_w_s[0], gm_b_s=gm_b_s[0])
    loss_parts, grad_x, g = _local_step(x[0], ctx[0], loss_target[0], mods, norm_g_full, get_w, small, put_grad)
    g["loss"] = (0.5 / d * jnp.sum(loss_parts)).reshape(1)

    whole = {"ffn_w_in": (tr(ffn_w_in), tr(m_ffn_w_in), tr(v_ffn_w_in)), "ffn_w_out": (ffn_w_out, m_ffn_w_out, v_ffn_w_out),
             "ssd_w_in": (tr(ssd_w_in), tr(m_ssd_w_in), tr(v_ssd_w_in)), "ssd_w_out": (ssd_w_out, m_ssd_w_out, v_ssd_w_out),
             "gm_w_in": (gm_w_in, m_gm_w_in, v_gm_w_in), "gm_w_out": (gm_w_out, m_gm_w_out, v_gm_w_out)}
    res = {}

    def update_units(some, after):
        for unit, grp, idx in some:
            parts = _exchange_wait(*scatters[unit], after, scatter=True, name="scatter_wait_" + unit)
            for k, p in zip(idx, parts):
                which = ("in", "out")[k]
                nm = ("ffn" if grp.startswith("ffn") else grp) + "_w_" + which
                sel = (int(grp[3]), int(grp[4])) if grp.startswith("ffn") else (0,)
                res[nm] = _adamw(p, *whole[nm], name=f"adamw_{grp}_{which}", sel=sel, into=res.get(nm))
                after = res[nm][0]
        return after

    sg_names = ["dmx", "dmc", "norm_g", "ssd_conv_w", "ssd_conv_b", "ssd_dt_bias", "ssd_A_log", "ssd_D", "ssd_norm_g",
                "gm_v_g", "gm_v_b", "gm_w_s", "gm_b_s", "loss"]
    sg_shapes = [g[n].shape for n in sg_names]
    flat = jnp.concatenate([g[n].reshape(-1) for n in sg_names])
    npack = flat.shape[0]
    pad = (-npack) % 1024
    flat = jnp.pad(flat, (0, pad)).reshape(-1, 128)
    sg_start = _exchange_start([flat], [_landing(flat, me)], scatter=False, name="small_grads_start")
    by_send = list(reversed(units))
    update_units(by_send[:4], jnp.stack([sg_start[4], grad_x[0, 0]]))
    early_done = jnp.stack([res[nm][0].reshape(-1)[-1] for nm in sorted(res)])
    (sg_all,) = _exchange_wait(*sg_start[:4], early_done, scatter=False, name="small_grads_wait")
    sg_sum = _sum_slots(sg_all, name="sum_small_grads").reshape(-1)[:npack]
    update_units(by_send[4:], sg_sum)
    sums = {}
    o = 0
    for n, shp in zip(sg_names, sg_shapes):
        sz = math.prod(shp)
        sums[n] = sg_sum[o:o + sz].reshape(shp)
        o += sz
    loss = sums["loss"][0]
    per_dev = sg_all.reshape(NDEV, -1)
    dmx_all =per_dev[:, :2 * N_MOD * d].reshape(NDEV, 2, N_MOD * d)
    dmc_all = per_dev[:, 2 * N_MOD * d:4 * N_MOD * d].reshape(NDEV, 2, N_MOD * d)

    (s16,) = _rowwise("ada_silu", lambda cc: ((_silu(cc),), ()), 2 * NDEV, [c16], [], [(d, F32)], tm=2 * NDEV)
    s16_t = s16.T
    g_ada_w, dcc_parts = [], []
    for i in range(2):
        rhs = jnp.concatenate([lax.dynamic_slice_in_dim(dmx_all[:, i], me * ncol, ncol, axis=1),
                               lax.dynamic_slice_in_dim(dmc_all[:, i], me * ncol, ncol, axis=1)], axis=0)
        g_ada_w.append(_mm_f32(s16_t, rhs, name=f"ada_dw{i}"))
        dmc_loc = lax.dynamic_slice_in_dim(sums["dmc"][i], me * ncol, ncol, axis=0)
        rhs_c = jnp.zeros((ncol, 128), F32).at[:, 0].set(dmc_loc)
        dcc_parts.append(_mm_f32(ada_w[i], rhs_c, name=f"ada_dcc{i}")[:, 0])
    g_ada_w = jnp.stack(g_ada_w)
    dcc_part = (dcc_parts[0] + dcc_parts[1]).reshape(8, 128)
    (dcc_all,), _ = _exchange([dcc_part], scatter=False, name="gather_dcc")
    g_c_ctx = _sum_slots(dcc_all, name="sum_dcc", scale_by=c_ctx.reshape(8, 128)).reshape(d)
    g_ada_b = sums["dmx"] + sums["dmc"]

    outs = _adamw(g_ada_w.reshape(1, -1, ncol), ada_w.reshape(-1, ncol), m_ada_w.reshape(-1, ncol),
                  v_ada_w.reshape(-1, ncol), name="adamw_ada_w")
    res["ada_w"] = [o_.reshape(ada_w.shape) for o_ in outs]

    loc = lambda a, ax, n: lax.dynamic_slice_in_dim(a, me * n, n, axis=ax)
    small_g = dict(c_ctx=g_c_ctx, ada_b=g_ada_b, norm_g=loc(sums["norm_g"], 2, 128),
                   ssd_conv_w=loc(sums["ssd_conv_w"], 1, 512)[None], ssd_conv_b=sums["ssd_conv_b"][None],
                   ssd_dt_bias=sums["ssd_dt_bias"][None], ssd_A_log=sums["ssd_A_log"][None], ssd_D=sums["ssd_D"][None],
                   ssd_norm_g=sums["ssd_norm_g"][None], gm_v_g=loc(sums["gm_v_g"], 0, 256)[None],
                   gm_v_b=loc(sums["gm_v_b"], 0, 256)[None], gm_b_s=sums["gm_b_s"][None])
    small_w = dict(c_ctx=(c_ctx, m_c_ctx, v_c_ctx), ada_b=(ada_b, m_ada_b, v_ada_b), norm_g=(norm_g, m_norm_g, v_norm_g),
                   ssd_conv_w=(ssd_conv_w, m_ssd_conv_w, v_ssd_conv_w), ssd_conv_b=(ssd_conv_b, m_ssd_conv_b, v_ssd_conv_b),
                   ssd_dt_bias=(ssd_dt_bias, m_ssd_dt_bias, v_ssd_dt_bias), ssd_A_log=(ssd_A_log, m_ssd_A_log, v_ssd_A_log),
                   ssd_D=(ssd_D, m_ssd_D, v_ssd_D), ssd_norm_g=(ssd_norm_g, m_ssd_norm_g, v_ssd_norm_g),
                   gm_v_g=(gm_v_g, m_gm_v_g, v_gm_v_g), gm_v_b=(gm_v_b, m_gm_v_b, v_gm_v_b),
                   gm_b_s=(gm_b_s, m_gm_b_s, v_gm_b_s))
    sn = list(small_w)
    flat2 = lambda a: a.reshape(-1, CHUNK)
    res["gm_w_s"] = [o_.reshape(gm_w_s.shape) for o_ in _adamw(
        flat2(sums["gm_w_s"])[None], flat2(gm_w_s), flat2(m_gm_w_s), flat2(v_gm_w_s), name="adamw_gm_w_s")]

    def pack(arrs):
        f = jnp.concatenate([a.reshape(-1) for a in arrs])
        return jnp.pad(f, (0, (-f.shape[0]) % (256 * 128))).reshape(-1, 128)

    pg = pack([small_g[n].reshape(small_w[n][0].shape) for n in sn])
    outs = _adamw(pg[None], pack([small_w[n][0] for n in sn]), pack([small_w[n][1] for n in sn]),
                  pack([small_w[n][2] for n in sn]), name="adamw_small")
    flat_outs = [o_.reshape(-1) for o_ in outs]
    o = 0
    for n in sn:
        shp = small_w[n][0].shape
        sz = math.prod(shp)
        res[n] = [fo[o:o + sz].reshape(shp) for fo in flat_outs]
        o += sz

    order = ["c_ctx", "ada_w", "ada_b", "norm_g", "ffn_w_in", "ffn_w_out", "ssd_w_in", "ssd_conv_w", "ssd_conv_b",
             "ssd_dt_bias", "ssd_A_log", "ssd_D", "ssd_norm_g", "ssd_w_out", "gm_w_in", "gm_v_g", "gm_v_b", "gm_w_s",
             "gm_b_s", "gm_w_out"]
    for nm in ("ffn_w_in", "ssd_w_in"):
        res[nm] = [tr(a) for a in res[nm]]
    result = [loss, grad_x[None]]
    for k in range(4):
        result += [res[n][k] for n in order]
    return tuple(result)
```

```python
import functools
import math

import jax
import jax.numpy as jnp
from jax import lax
from jax.experimental import pallas as pl
from jax.experimental.pallas import tpu as pltpu

F32 = jnp.float32
BF16 = jnp.bfloat16

NDEV = 8
D_MODEL = 1024
FFN_DIM = 2816
N_MOD = 9
EPS = 1e-6
SSD_INNER = 2048
SSD_HEADS = 32
SSD_HEAD_DIM = 64
SSD_GROUPS = 8
SSD_HPG = 4
SSD_STATE = 128
SSD_CONV = 5
SSD_CONV_DIM = 4096
CHUNK = 128
GM_INNER = 2048
GM_GROUPS = 8
GM_GROUP_DIM = 256
ADAM_LR = 0.001
ADAM_B1 = 0.9
ADAM_B2 = 0.999
ADAM_EPS = 1e-08
ADAM_WD = 0.01
ADAM_STEP = 10
NEG_BIG = -1e30
VMEM_LIMIT_BYTES = 56 * 1024 * 1024
HI = lax.Precision.HIGHEST


def _params(*sem):
    return pltpu.CompilerParams(dimension_semantics=sem, vmem_limit_bytes=VMEM_LIMIT_BYTES)


def _pick(n, target, mult=16):
    if n <= target:
        return n
    for t in range(target - target % mult, 0, -mult):
        if n % t == 0:
            return t
    raise ValueError((n, target, mult))


def _sig(x):
    return 0.5 * jnp.tanh(0.5 * x) + 0.5


def _silu(x):
    return x * _sig(x)


def _dsilu(x):
    s = _sig(x)
    return s * (1.0 + x * (1.0 - s))


_GELU_C = math.sqrt(2.0 / math.pi)


def _gelu(x):
    return 0.5 * x * (1.0 + jnp.tanh(_GELU_C * (x + 0.044715 * x * x * x)))


def _gelu_and_grad(x):
    x2 = x * x
    t = jnp.tanh(_GELU_C * (x + 0.044715 * x2 * x))
    half = 0.5 * (1.0 + t)
    return x * half, half + 0.5 * x * (1.0 - t * t) * _GELU_C * (1.0 + 3.0 * 0.044715 * x2)


def _dgelu(x):
    return _gelu_and_grad(x)[1]


def _softplus(x):
    return jnp.maximum(x, 0.0) + jnp.log1p(jnp.exp(-jnp.abs(x)))


def _sum0(v):
    return jnp.sum(v, axis=0, keepdims=True)


def _rms(h):
    r = lax.rsqrt(jnp.mean(h * h, axis=-1, keepdims=True) + EPS)
    return h * r, r


def _dot(a, b, dims=((1,), (0,)), precision=None):
    return lax.dot_general(a, b, (dims, ((), ())), preferred_element_type=F32, precision=precision)


_NT = ((1,), (1,))
_TN = ((0,), (0,))


def _rowwise(name, fn, n_rows, rows, consts, outs, accs=(), *, tm, nseg=1, seg_blocks=0):
    assert n_rows % tm == 0
    if nseg == 2:
        assert seg_blocks > 0
        seg = lambda i: jnp.where(i < seg_blocks, 0, 1)
    else:
        seg = lambda i: 0
    in_specs, args, lacking = [], [], []
    for r in rows:
        arr, width, cb, off = r if isinstance(r, tuple) else (r, r.shape[1], 0, 0)
        in_specs.append(pl.BlockSpec((tm, width), lambda i, cb=cb, off=off: (jnp.maximum(i + off, 0), cb)))
        args.append(arr)
        lacking.append(-off if off < 0 else 0)
    for kind, arr in consts:
        if kind == "seg":
            assert arr.shape[0] == nseg and arr.shape[1] == 1, arr.shape
            in_specs.append(pl.BlockSpec((None, 1, arr.shape[2]), lambda i: (seg(i), 0, 0)))
        else:
            in_specs.append(pl.BlockSpec(arr.shape, lambda i: (0, 0)))
        args.append(arr)
    out_shape = [jax.ShapeDtypeStruct((n_rows, w), dt) for w, dt in outs]
    out_specs = [pl.BlockSpec((tm, w), lambda i: (i, 0)) for w, _ in outs]
    out_shape += [jax.ShapeDtypeStruct((nseg, 1, w), F32) for w in accs]
    out_specs += [pl.BlockSpec((None, 1, w), lambda i: (seg(i), 0, 0)) for w in accs]
    n_in, n_out, n_acc = len(args), len(outs), len(accs)

    def kern(*refs):
        i = pl.program_id(0)
        ins = [r[...] for r in refs[:n_in]]
        for k, lack in enumerate(lacking):
            if lack:
                ins[k] = jnp.where(i >= lack, ins[k], jnp.zeros_like(ins[k]))
        res, terms = fn(*ins)
        for ref, v in zip(refs[n_in:n_in + n_out], res):
            ref[...] = v.astype(ref.dtype)
        if n_acc:
            sums = [_sum0(v) for v in terms]
            first = (i == 0) | (i == seg_blocks) if nseg == 2 else (i == 0)
            acc_refs = refs[n_in + n_out:]

            @pl.when(first)
            def _():
                for ref, v in zip(acc_refs, sums):
                    ref[...] = v

            @pl.when(jnp.logical_not(first))
            def _():
                for ref, v in zip(acc_refs, sums):
                    ref[...] += v

    res = pl.pallas_call(
        kern, name=name, grid=(n_rows // tm,), in_specs=in_specs, out_specs=out_specs, out_shape=out_shape,
        compiler_params=_params("arbitrary"),
    )(*args)
    return res


def _pre_fwd_fn(h, g, shift, scale):
    hh, _ = _rms(h)
    return (hh * g * (1.0 + scale) + shift,), ()


def _pre_bwd_fn(du, h, dres, g, scale):
    hh, r = _rms(h)
    n = hh * g
    dn = du * (1.0 + scale)
    dhh = dn * g
    dh = dres + r * (dhh - hh * jnp.mean(dhh * hh, axis=-1, keepdims=True))
    return (dh,), (du, du * n, dn * hh)


def _post_fwd_fn(weight, h, y, g, gate):
    yh, _ = _rms(y)
    return (h + weight * gate * (yh * g),), ()


def _out_post_fn(weight, y, h, g, gate):
    return (y,) + _post_fwd_fn(weight, h, y, g, gate)[0], ()


def _post_bwd_fn(weight, dh, y, g, gate):
    yh, r = _rms(y)
    dr = dh * weight
    dyh = dr * gate * g
    dy = r * (dyh - yh * jnp.mean(dyh * yh, axis=-1, keepdims=True))
    return (dy,), (dr * yh * g, dr * gate * yh)


def _glu_bwd_fn(ds, a, b):
    a = a.astype(F32)
    b = b.astype(F32)
    sg = _sig(a)
    da = ds * b * (sg * (1.0 + a * (1.0 - sg)))
    db = ds * (a * sg)
    return (jnp.concatenate([da, db], axis=1),), ()


def _loss_fn(y, t):
    diff = y - t
    return (diff * (1.0 / D_MODEL),), (diff * diff,)


def _ssd_y(yf, yb, xs, z, dvec):
    y = yf + yb + dvec * xs
    return y, y * _silu(z)


def _ssdgate_fwd_fn(yf, yb, xs, z, dvec, ng):
    _, yg = _ssd_y(yf, yb, xs, z, dvec)
    parts = []
    for g in range(SSD_GROUPS):
        sl = slice(g * 256, (g + 1) * 256)
        parts.append(_rms(yg[:, sl])[0])
    return (jnp.concatenate(parts, axis=1) * ng,), ()


def _ssdgate_bwd_fn(dyn, yf, yb, xs, z, dvec, ng):
    y, yg = _ssd_y(yf, yb, xs, z, dvec)
    dyg_parts, ygh_parts = [], []
    for g in range(SSD_GROUPS):
        sl = slice(g * 256, (g + 1) * 256)
        ygh, r = _rms(yg[:, sl])
        d = dyn[:, sl] * ng[:, sl]
        dyg_parts.append(r * (d - ygh * jnp.mean(d * ygh, axis=-1, keepdims=True)))
        ygh_parts.append(ygh)
    dyg = jnp.concatenate(dyg_parts, axis=1)
    ygh = jnp.concatenate(ygh_parts, axis=1)
    dy = dyg * _silu(z)
    dz = dyg * y * _dsilu(z)
    return (dy, dz), (dyn * ygh, dy * xs)


def _ln_stats(v):
    mu = jnp.mean(v, axis=-1, keepdims=True)
    vc = v - mu
    r = lax.rsqrt(jnp.mean(vc * vc, axis=-1, keepdims=True) + EPS)
    return vc * r, r


def _gm_act_fwd_fn(p, vg, vb):
    gu = _gelu(p[:, :GM_INNER])
    gvh, _ = _ln_stats(_gelu(p[:, GM_INNER:]))
    return (gu, gvh * vg + vb), ()


def _gm_act_bwd_fn(p, dgu, dgvn, vg):
    pu = p[:, :GM_INNER]
    pv = p[:, GM_INNER:]
    gv, dgelu_v = _gelu_and_grad(pv)
    gvh, r = _ln_stats(gv)
    dgvh = dgvn * vg
    dgv = r * (dgvh - jnp.mean(dgvh, axis=-1, keepdims=True) - gvh * jnp.mean(dgvh * gvh, axis=-1, keepdims=True))
    dp = jnp.concatenate([dgu * _dgelu(pu), dgv * dgelu_v], axis=1)
    return (dp,), (dgvn * gvh, dgvn)


def _mm(a, b, *, out_dtype, name, tm=1088, tn=1024, tk=1408, add=None, rhs_t=False, n=None, b_off=(0, 0)):
    m, k = a.shape
    col_blocked = b.ndim == 3
    if col_blocked:
        assert not rhs_t and n is None and b.shape[1] == k
        n, tn = b.shape[0] * b.shape[2], b.shape[2]
    elif n is None:
        n, k2 = b.shape if rhs_t else b.shape[::-1]
        assert k == k2
    tm, tn, tk = _pick(m, tm), _pick(n, tn, 128), _pick(k, tk, 128)
    o0, o1 = b_off
    nk = k // tk
    dims = _NT if rhs_t else ((1,), (0,))

    def kern(*refs):
        a_ref, b_ref = refs[:2]
        add_ref = refs[2] if add is not None else None
        o_ref = refs[3] if add is not None else refs[2]

        def finish(r):
            if add is not None:
                r = r + add_ref[...]
            o_ref[...] = r.astype(o_ref.dtype)

        p = _dot(a_ref[...], b_ref[...], dims)
        if nk == 1:
            finish(p)
            return
        acc_ref = refs[-1]
        kk = pl.program_id(2)

        @pl.when(kk == 0)
        def _():
            acc_ref[...] = p

        @pl.when((kk > 0) & (kk < nk - 1))
        def _():
            acc_ref[...] += p

        @pl.when(kk == nk - 1)
        def _():
            finish(acc_ref[...] + p)

    if col_blocked:
        b_spec = pl.BlockSpec((None, tk, tn), lambda i, j, kk: (j, kk, 0))
    elif rhs_t:
        b_spec = pl.BlockSpec((tn, tk), lambda i, j, kk: (j + o0, kk + o1))
    else:
        b_spec = pl.BlockSpec((tk, tn), lambda i, j, kk: (kk + o0, j + o1))
    in_specs = [pl.BlockSpec((tm, tk), lambda i, j, kk: (i, kk)), b_spec]
    args = [a, b]
    if add is not None:
        in_specs.append(pl.BlockSpec((tm, tn), lambda i, j, kk: (i, j)))
        args.append(add)
    return pl.pallas_call(
        kern, name=name, grid=(m // tm, n // tn, nk), in_specs=in_specs,
        out_specs=pl.BlockSpec((tm, tn), lambda i, j, kk: (i, j)),
        out_shape=jax.ShapeDtypeStruct((m, n), out_dtype),
        scratch_shapes=[pltpu.VMEM((tm, tn), F32)] if nk > 1 else [],
        compiler_params=_params("parallel", "parallel", "arbitrary"),
    )(*args)


def _mm_rows(a, b, fn, rows, consts, outs, accs=(), *, name, tm=544, tk=1408, rhs_t=False, n_ctx=0):
    halves = a.ndim == 3
    m, k = (a.shape[1], 2 * a.shape[2]) if halves else a.shape
    col_blocked = b.ndim == 3
    kb, nb = 1, None
    if col_blocked:
        assert rhs_t and b.shape[0] * b.shape[2] == k
        n, nb = b.shape[1], b.shape[2]
        kb = max(1, tk // nb)
        assert b.shape[0] % kb == 0
        tk = kb * nb
    else:
        n = b.shape[0] if rhs_t else b.shape[1]
    tm, tk = _pick(m, tm), _pick(k, tk, 128)
    nk = k // tk
    if halves:
        hb = k // 2 // tk
        a_spec = pl.BlockSpec((None, tm, tk), lambda i, kk: (kk // hb, i, kk % hb))
    else:
        a_spec = pl.BlockSpec((tm, tk), lambda i, kk: (i, kk))
    dims = _NT if rhs_t else ((1,), (0,))
    n_rows, n_const, n_out, n_acc = len(rows), len(consts), len(outs), len(accs)

    def kern(*refs):
        a_ref, b_ref = refs[:2]
        row_refs = refs[2:2 + n_rows]
        const_refs = refs[2 + n_rows:2 + n_rows + n_const]
        out_refs = refs[2 + n_rows + n_const:2 + n_rows + n_const + n_out]
        acc_refs = refs[2 + n_rows + n_const + n_out:2 + n_rows + n_const + n_out + n_acc]
        i, kk = pl.program_id(0), pl.program_id(1)

        def finish(p, rs=slice(None), r0=0):
            nr = p.shape[0]
            is_ctx = (i * tm + r0 + lax.broadcasted_iota(jnp.int32, (nr, 1), 0)) < n_ctx
            cvals = []
            for (kind, arr), ref in zip(consts, const_refs):
                if kind == "seg":
                    cvals.append(jnp.where(is_ctx, ref[0], ref[1]) if arr.shape[0] == 2 else ref[0])
                else:
                    cvals.append(ref[...])
            res, terms = fn(p, *[r[rs, :] for r in row_refs], *cvals)
            for ref, v in zip(out_refs, res):
                ref[rs, :] = v.astype(ref.dtype)
            for ref, v in zip(acc_refs, terms):
                s_all = _sum0(v)
                s_ctx = _sum0(jnp.where(is_ctx, v, 0.0)) if n_ctx else jnp.zeros_like(s_all)
                both = jnp.concatenate([s_ctx, s_all - s_ctx], axis=0)[:, None, :]

                @pl.when(i == 0)
                def _():
                    ref[...] = both

                @pl.when(i > 0)
                def _():
                    ref[...] += both

        if nk == 1 and n_acc == 0:
            nsub = 2 if tm % 32 == 0 else 1
            sub = tm // nsub
            for r in range(nsub):
                rs = slice(r * sub, (r + 1) * sub)
                finish(_dot(a_ref[rs, :], b_ref[...], dims), rs, r * sub)
            return
        if col_blocked:
            p = sum(_dot(a_ref[:, c * nb:(c + 1) * nb], b_ref[c], dims) for c in range(kb))
        else:
            p = _dot(a_ref[...], b_ref[...], dims)
        if nk == 1:
            finish(p)
            return
        scr = refs[-1]

        @pl.when(kk == 0)
        def _():
            scr[...] = p

        @pl.when((kk > 0) & (kk < nk - 1))
        def _():
            scr[...] += p

        @pl.when(kk == nk - 1)
        def _():
            finish(scr[...] + p)

    if col_blocked:
        b_spec = pl.BlockSpec((kb, n, nb), lambda i, kk: (kk, 0, 0))
    elif rhs_t:
        b_spec = pl.BlockSpec((n, tk), lambda i, kk: (0, kk))
    else:
        b_spec = pl.BlockSpec((tk, n), lambda i, kk: (kk, 0))
    in_specs = [a_spec, b_spec]
    in_specs += [pl.BlockSpec((tm, r.shape[1]), lambda i, kk: (i, 0)) for r in rows]
    for kind, arr in consts:
        in_specs.append(pl.BlockSpec(arr.shape, (lambda i, kk: (0, 0, 0)) if kind == "seg" else (lambda i, kk: (0, 0))))
    out_shape = [jax.ShapeDtypeStruct((m, w), dt) for w, dt in outs]
    out_specs = [pl.BlockSpec((tm, w), lambda i, kk: (i, 0)) for w, _ in outs]
    out_shape += [jax.ShapeDtypeStruct((2, 1, w), F32) for w in accs]
    out_specs += [pl.BlockSpec((2, 1, w), lambda i, kk: (0, 0, 0)) for w in accs]
    return pl.pallas_call(
        kern, name=name, grid=(m // tm, nk), in_specs=in_specs, out_specs=out_specs, out_shape=out_shape,
        scratch_shapes=[pltpu.VMEM((tm, n), F32)] if nk > 1 else [],
        compiler_params=_params("arbitrary", "arbitrary"),
    )(a, b, *rows, *[arr for _, arr in consts])


def _mm_glu(u, win_t, *, name, tm=1088, tn=1408):
    m, k = u.shape
    n = win_t.shape[0] // 2
    tm, tn = _pick(m, tm), _pick(n, tn, 128)
    nj = n // tn

    nsub = 2 if tm % 32 == 0 else 1
    sub = tm // nsub

    def kern(u_ref, wa_ref, wb_ref, s_ref, a_ref, b_ref):
        for r in range(nsub):
            rows = slice(r * sub, (r + 1) * sub)
            uu = u_ref[rows, :]
            a = _dot(uu, wa_ref[...], _NT)
            b = _dot(uu, wb_ref[...], _NT)
            s_ref[rows, :] = (_silu(a) * b).astype(BF16)
            a_ref[rows, :] = a.astype(BF16)
            b_ref[rows, :] = b.astype(BF16)

    ospec = pl.BlockSpec((tm, tn), lambda i, j: (i, j))
    return pl.pallas_call(
        kern, name=name, grid=(m // tm, nj),
        in_specs=[pl.BlockSpec((tm, k), lambda i, j: (i, 0)), pl.BlockSpec((tn, k), lambda i, j: (j, 0)),
                  pl.BlockSpec((tn, k), lambda i, j: (nj + j, 0))],
        out_specs=[ospec, ospec, ospec],
        out_shape=[jax.ShapeDtypeStruct((m, n), BF16)] * 3,
        compiler_params=_params("parallel", "parallel"),
    )(u, win_t, win_t)


def _mm_glu_bwd(dy, wout, a, b, *, name, tm=544, tn=1408):
    m, k = dy.shape
    f = wout.shape[0]
    tm, tn = _pick(m, tm), _pick(f, tn, 128)
    nsub = 2 if tm % 32 == 0 else 1
    sub = tm // nsub

    def kern(dy_ref, w_ref, a_ref, b_ref, o_ref):
        for r in range(nsub):
            rs = slice(r * sub, (r + 1) * sub)
            ds = _dot(dy_ref[rs, :], w_ref[...], _NT)
            (dp,), _ = _glu_bwd_fn(ds, a_ref[rs, :], b_ref[rs, :])
            o_ref[0, rs, :] = dp[:, :tn].astype(BF16)
            o_ref[1, rs, :] = dp[:, tn:].astype(BF16)

    tile = pl.BlockSpec((tm, tn), lambda i, j: (i, j))
    return pl.pallas_call(
        kern, name=name, grid=(m // tm, f // tn),
        in_specs=[pl.BlockSpec((tm, k), lambda i, j: (i, 0)), pl.BlockSpec((tn, k), lambda i, j: (j, 0)), tile, tile],
        out_specs=pl.BlockSpec((2, tm, tn), lambda i, j: (0, i, j)),
        out_shape=jax.ShapeDtypeStruct((2, m, f), BF16),
        compiler_params=_params("parallel", "parallel"),
    )(dy, wout, a, b)


def _mm_tn(a, b, *, name, tm=1024, tn=1024, tk=2176, col_blocks=None, stack=None):
    extra, extra_specs, aliases = [], [], {}
    halves = a.ndim == 3
    t, m = (a.shape[1], 2 * a.shape[2]) if halves else a.shape
    t2, n = b.shape
    assert t == t2
    tm, tn, tk = _pick(m, tm, 128), _pick(n, tn, 128), _pick(t, tk)
    nk = t // tk
    if halves:
        hb = m // 2 // tm
        a_spec = pl.BlockSpec((None, tk, tm), lambda i, j, kk: (i // hb, kk, i % hb))
    else:
        a_spec = pl.BlockSpec((tk, tm), lambda i, j, kk: (kk, i))
    if col_blocks is None:
        def kern(a_ref, b_ref, o_ref):
            kk = pl.program_id(2)

            @pl.when(kk == 0)
            def _():
                o_ref[...] = jnp.zeros_like(o_ref)

            o_ref[...] += _dot(a_ref[...], b_ref[...], _TN)

        out_spec = pl.BlockSpec((tm, tn), lambda i, j, kk: (i, j))
        out_shape = jax.ShapeDtypeStruct((m, n), F32)
        scratch = []
    else:
        wb = n // col_blocks
        per = tn // wb
        assert tn % wb == 0 and wb % 8 == 0

        def kern(a_ref, b_ref, *rest):
            o_ref, acc_ref = rest[-2:]
            kk = pl.program_id(2)
            p = _dot(a_ref[...], b_ref[...], _TN)

            @pl.when(kk == 0)
            def _():
                acc_ref[...] = p

            @pl.when((kk > 0) & (kk < nk - 1))
            def _():
                acc_ref[...] += p

            @pl.when(kk == nk - 1)
            def _():
                r = acc_ref[...] + p if nk > 1 else p
                for c in range(per):
                    o_ref[c] = r[:, c * wb:(c + 1) * wb].astype(BF16)

        rows_total, row0, into = stack if stack is not None else (m, 0, None)
        assert row0 % tm == 0
        out_spec = pl.BlockSpec((per, tm, wb), lambda i, j, kk: (j, i + row0 // tm, 0))
        out_shape = jax.ShapeDtypeStruct((col_blocks, rows_total, wb), BF16)
        scratch = [pltpu.VMEM((tm, tn), F32)]
        if into is not None:
            extra, extra_specs, aliases = [into], [pl.BlockSpec(memory_space=pl.ANY)], {2: 0}

    return pl.pallas_call(
        kern, name=name, grid=(m // tm, n // tn, nk),
        in_specs=[a_spec, pl.BlockSpec((tk, tn), lambda i, j, kk: (kk, j))] + extra_specs,
        out_specs=out_spec, out_shape=out_shape, scratch_shapes=scratch, input_output_aliases=aliases,
        compiler_params=_params("parallel", "parallel", "arbitrary"),
    )(a, b, *extra)


def _mm_f32(a, b, *, name, silu_a=False, bias=None):
    m, k = a.shape
    n = b.shape[1]

    def kern(*refs):
        if bias is None:
            a_ref, b_ref, o_ref = refs
        else:
            a_ref, b_ref, bias_ref, o_ref = refs
        av = a_ref[...]
        if silu_a:
            av = _silu(av)
        r = jnp.dot(av, b_ref[...], preferred_element_type=F32, precision=HI)
        if bias is not None:
            r = r + bias_ref[...]
        o_ref[...] = r

    args = [a, b] + ([] if bias is None else [bias])
    return pl.pallas_call(kern, name=name, out_shape=jax.ShapeDtypeStruct((m, n), F32),
                          compiler_params=pltpu.CompilerParams(vmem_limit_bytes=VMEM_LIMIT_BYTES))(*args)


CONV_WIN = 32


def _conv_windows(n, n_ctx):
    assert n_ctx % CONV_WIN == 0 and n_ctx >= CONV_WIN and n - n_ctx >= CONV_WIN
    return (0, n_ctx - CONV_WIN // 2, n - CONV_WIN)


def _tap_outside(r0, s, n, n_ctx):
    t = r0 + lax.broadcasted_iota(jnp.int32, (CONV_WIN, 1), 0)
    lo = jnp.where(t < n_ctx, 0, n_ctx)
    hi = jnp.where(t < n_ctx, n_ctx, n)
    return jnp.where((t + s >= lo) & (t + s < hi), 0.0, 1.0)


def _rolled(v, s):
    return v if s == 0 else pltpu.roll(v, (-s) % v.shape[0], 0)


def _conv_fwd(xp, w8, b, *, n_ctx, name, cb=256):
    n, c = xp.shape
    half = SSD_CONV // 2

    def kern(x_ref, w_ref, b_ref, cpre_ref, act_ref):
        x = x_ref[...]
        acc = jnp.zeros_like(x) + b_ref[...]
        rolled = {}
        for k in range(SSD_CONV):
            rolled[k] = _rolled(x, k - half)
            acc = acc + rolled[k] * w_ref[k:k + 1, :]
        cpre_ref[...] = acc
        act_ref[...] = _silu(acc)
        for r0 in _conv_windows(n, n_ctx):
            rows = slice(r0, r0 + CONV_WIN)
            fix = acc[rows]
            for k in range(SSD_CONV):
                if k != half:
                    fix = fix - rolled[k][rows] * w_ref[k:k + 1, :] * _tap_outside(r0, k - half, n, n_ctx)
            cpre_ref[rows, :] = fix
            act_ref[rows, :] = _silu(fix)

    spec = pl.BlockSpec((n, cb), lambda j: (0, j))
    return pl.pallas_call(
        kern, name=name, grid=(c // cb,),
        in_specs=[spec, pl.BlockSpec((8, cb), lambda j: (0, j)), pl.BlockSpec((1, cb), lambda j: (0, j))],
        out_specs=[spec, spec], out_shape=[jax.ShapeDtypeStruct((n, c), F32)] * 2,
        compiler_params=_params("parallel"),
    )(xp, w8, b)


def _conv_bwd(d1, d2, cpre, xp, w8, *, n_ctx, name, cb=128):
    n, c = xp.shape
    half = SSD_CONV // 2

    def kern(d1_ref, d2_ref, cpre_ref, x_ref, w_ref, dx_ref, dw_ref, db_ref):
        g = (d1_ref[...] + d2_ref[...]) * _dsilu(cpre_ref[...])
        x = x_ref[...]
        dx = jnp.zeros_like(g)
        dw_ref[...] = jnp.zeros_like(dw_ref)
        g_rolled = {}
        for k in range(SSD_CONV):
            s = k - half
            g_rolled[k] = _rolled(g, -s)
            dx = dx + g_rolled[k] * w_ref[k:k + 1, :]
            xr = _rolled(x, s)
            dw = _sum0(g * xr)
            if s != 0:
                for r0 in _conv_windows(n, n_ctx):
                    rows = slice(r0, r0 + CONV_WIN)
                    dw = dw - _sum0(g[rows] * xr[rows] * _tap_outside(r0, s, n, n_ctx))
            dw_ref[k:k + 1, :] = dw
        dx_ref[...] = dx.astype(BF16)
        for r0 in _conv_windows(n, n_ctx):
            rows = slice(r0, r0 + CONV_WIN)
            fix = dx[rows]
            for k in range(SSD_CONV):
                if k != half:
                    fix = fix - g_rolled[k][rows] * w_ref[k:k + 1, :] * _tap_outside(r0, half - k, n, n_ctx)
            dx_ref[rows, :] = fix.astype(BF16)
        db_ref[...] = _sum0(g)

    spec = pl.BlockSpec((n, cb), lambda j: (0, j))
    return pl.pallas_call(
        kern, name=name, grid=(c // cb,),
        in_specs=[spec, spec, spec, spec, pl.BlockSpec((8, cb), lambda j: (0, j))],
        out_specs=[spec, pl.BlockSpec((8, cb), lambda j: (0, j)), pl.BlockSpec((1, cb), lambda j: (0, j))],
        out_shape=[jax.ShapeDtypeStruct((n, c), BF16), jax.ShapeDtypeStruct((8, c), F32),
                   jax.ShapeDtypeStruct((1, c), F32)],
        compiler_params=_params("parallel"),
    )(d1, d2, cpre, xp, w8)


def _chunk_of(s, nc, n_ctx_chunks, rev):
    if not rev:
        return s
    return jnp.where(s < n_ctx_chunks, n_ctx_chunks - 1 - s, nc - 1 - (s - n_ctx_chunks))


def _scan_common(dt_raw, dtT_raw, bias_r, bias_c, alog_r, alog_c, rev):
    ii = lax.broadcasted_iota(jnp.int32, (CHUNK, CHUNK), 0)
    jj = lax.broadcasted_iota(jnp.int32, (CHUNK, CHUNK), 1)
    tri = (jj >= ii) if rev else (jj <= ii)
    tri_t = (ii >= jj) if rev else (ii <= jj)
    a_r = -jnp.exp(alog_r)
    a_c = -jnp.exp(alog_c)
    dt = _softplus(dt_raw + bias_r)
    dt_t = _softplus(dtT_raw + bias_c)
    al = dt * a_r
    acum = _dot(tri.astype(F32), al, precision=HI)
    acum_t = _dot(dt_t * a_c, tri_t.astype(F32), precision=HI)
    atot = _sum0(al)
    return tri, tri_t, a_r, dt, acum, acum_t, atot


def _head_spread():
    return jnp.repeat(jnp.eye(SSD_HEADS, dtype=BF16), SSD_HEAD_DIM, axis=1)


def _dot_sel(v, sel):
    hi = v.astype(BF16)
    lo = (v - hi.astype(F32)).astype(BF16)
    return _dot(hi, sel) + _dot(lo, sel)


def _ssd_scan_fwd(xbc, dt_raw, dtT_raw, bias_r, bias_c, alog_r, alog_c, *, rev, n_ctx_chunks, name):
    n = xbc.shape[0]
    nc = n // CHUNK
    cidx = functools.partial(_chunk_of, nc=nc, n_ctx_chunks=n_ctx_chunks, rev=rev)

    def kern(xs_ref, b_ref, c_ref, dt_ref, dtT_ref, br_ref, bc_ref, ar_ref, ac_ref, e_ref, y_ref, hs_ref, h_scr):
        @pl.when(pl.program_id(0) == 0)
        def _():
            h_scr[...] = jnp.zeros_like(h_scr)

        tri, _, _, dt, acum, acum_t, atot = _scan_common(
            dt_ref[...], dtT_ref[...], br_ref[...], bc_ref[...], ar_ref[...], ac_ref[...], rev)
        etot = jnp.exp(atot)
        spread = lambda v: _dot_sel(v, e_ref[...])
        xdt_all = xs_ref[...] * spread(dt)
        eax = spread(jnp.exp(acum))
        xdw_all = xdt_all * spread(jnp.exp(atot - acum))
        hs_ref[...] = h_scr[...]
        for g in range(SSD_GROUPS):
            gs = slice(g * 256, (g + 1) * 256)
            bg = b_ref[:, g * SSD_STATE:(g + 1) * SSD_STATE].astype(BF16)
            cg = c_ref[:, g * SSD_STATE:(g + 1) * SSD_STATE].astype(BF16)
            cb = _dot(cg, bg, _NT)
            h4 = h_scr[gs, :]
            ys = []
            for k in range(SSD_HPG):
                h = g * SSD_HPG + k
                lmat = jnp.exp(jnp.where(tri, acum[:, h:h + 1] - acum_t[h:h + 1, :], NEG_BIG))
                xdt_h = xdt_all[:, h * SSD_HEAD_DIM:(h + 1) * SSD_HEAD_DIM].astype(BF16)
                ys.append(_dot((cb * lmat).astype(BF16), xdt_h))
            y_ref[:, gs] = jnp.concatenate(ys, axis=1) + _dot(cg, h4.astype(BF16), _NT) * eax[:, gs]
            s4 = _dot(xdw_all[:, gs].astype(BF16), bg, _TN)
            for k in range(SSD_HPG):
                h = g * SSD_HPG + k
                rs = slice(h * SSD_HEAD_DIM, (h + 1) * SSD_HEAD_DIM)
                h_scr[rs, :] = h4[k * SSD_HEAD_DIM:(k + 1) * SSD_HEAD_DIM] * etot[:, h:h + 1] + \
                    s4[k * SSD_HEAD_DIM:(k + 1) * SSD_HEAD_DIM]

    nh = SSD_HEADS
    small = lambda shape: pl.BlockSpec(shape, lambda s: (0, 0))
    return pl.pallas_call(
        kern, name=name, grid=(nc,),
        in_specs=[pl.BlockSpec((CHUNK, SSD_INNER), lambda s: (cidx(s), 0)),
                  pl.BlockSpec((CHUNK, 1024), lambda s: (cidx(s), 2)),
                  pl.BlockSpec((CHUNK, 1024), lambda s: (cidx(s), 3)),
                  pl.BlockSpec((CHUNK, nh), lambda s: (cidx(s), 0)),
                  pl.BlockSpec((nh, CHUNK), lambda s: (0, cidx(s))),
                  small((1, nh)), small((nh, 1)), small((1, nh)), small((nh, 1)), small((nh, SSD_INNER))],
        out_specs=[pl.BlockSpec((CHUNK, SSD_INNER), lambda s: (cidx(s), 0)),
                   pl.BlockSpec((None, SSD_INNER, SSD_STATE), lambda s: (s, 0, 0))],
        out_shape=[jax.ShapeDtypeStruct((n, SSD_INNER), F32),
                   jax.ShapeDtypeStruct((nc, SSD_INNER, SSD_STATE), F32)],
        scratch_shapes=[pltpu.VMEM((SSD_INNER, SSD_STATE), F32)],
        compiler_params=_params("arbitrary"),
    )(xbc, xbc, xbc, dt_raw, dtT_raw, bias_r, bias_c, alog_r, alog_c, _head_spread())


def _ssd_scan_bwd(dy, xbc, hs, dt_raw, dtT_raw, bias_r, bias_c, alog_r, alog_c, dvec, *, rev, n_ctx_chunks,
                  direct, name):
    n = xbc.shape[0]
    nc = n // CHUNK
    nh = SSD_HEADS
    step_of = lambda r: nc - 1 - r
    cidx = lambda r: _chunk_of(step_of(r), nc, n_ctx_chunks, rev)

    def kern(dy_ref, xs_ref, b_ref, c_ref, hs_ref, dt_ref, dtT_ref, br_ref, bc_ref, ar_ref, ac_ref, dv_ref,
             e_ref, et_ref, dx_ref, ddt_ref, dal_ref, dbias_ref, dh_scr):
        @pl.when(pl.program_id(0) == 0)
        def _():
            dh_scr[...] = jnp.zeros_like(dh_scr)
            dal_ref[...] = jnp.zeros_like(dal_ref)
            dbias_ref[...] = jnp.zeros_like(dbias_ref)

        tri, tri_t, a_r, dt, acum, acum_t, atot = _scan_common(
            dt_ref[...], dtT_ref[...], br_ref[...], bc_ref[...], ar_ref[...], ac_ref[...], rev)
        etot = jnp.exp(atot)
        spread = lambda v: _dot_sel(v, e_ref[...])
        gather = lambda v: _dot_sel(v, et_ref[...])
        xs_all = xs_ref[...]
        dy_all = dy_ref[...]
        dtx = spread(dt)
        eax = spread(jnp.exp(acum))
        decx = spread(jnp.exp(atot - acum))
        xdt_all = xs_all * dtx
        xdw_all = xdt_all * decx
        dyo_all = dy_all * eax
        lane = lax.broadcasted_iota(jnp.int32, (CHUNK, nh), 1)
        lane1 = lax.broadcasted_iota(jnp.int32, (1, nh), 1)
        sub = lax.broadcasted_iota(jnp.int32, (nh, CHUNK), 0)
        g_rows = jnp.zeros((CHUNK, nh), F32)
        g_cols = jnp.zeros((nh, CHUNK), F32)
        dtot = jnp.zeros((1, nh), F32)
        q_col, q_e, q_dt = [], [], []
        for g in range(SSD_GROUPS):
            gs = slice(g * 256, (g + 1) * 256)
            bg = b_ref[:, g * SSD_STATE:(g + 1) * SSD_STATE].astype(BF16)
            cg = c_ref[:, g * SSD_STATE:(g + 1) * SSD_STATE].astype(BF16)
            cb = _dot(cg, bg, _NT)
            hs4 = hs_ref[gs, :]
            dh4 = dh_scr[gs, :]
            hs4_bf = hs4.astype(BF16)
            dh4_bf = dh4.astype(BF16)
            dy4 = dy_all[:, gs]
            dy4_bf = dy4.astype(BF16)
            xdt4_bf = xdt_all[:, gs].astype(BF16)
            xdw4 = xdw_all[:, gs]
            xdw4_bf = xdw4.astype(BF16)
            dyo4_bf = dyo_all[:, gs].astype(BF16)
            yoff4 = _dot(cg, hs4_bf, _NT) * eax[:, gs]
            dcg = _dot(dyo4_bf, hs4_bf)
            dh_new4 = _dot(dyo4_bf, cg, _TN)
            bdh4 = _dot(bg, dh4_bf, _NT)
            dbg = _dot(xdw4_bf, dh4_bf)
            e4 = xdw4 * bdh4
            q_col.append(dy4 * yoff4 - e4)
            q_e.append(e4)
            hsum = jnp.sum(dh4 * hs4, axis=1, keepdims=True)
            dcb = jnp.zeros((CHUNK, CHUNK), F32)
            dxdts = []
            for k in range(SSD_HPG):
                h = g * SSD_HPG + k
                ks = slice(k * SSD_HEAD_DIM, (k + 1) * SSD_HEAD_DIM)
                lmat = jnp.exp(jnp.where(tri, acum[:, h:h + 1] - acum_t[h:h + 1, :], NEG_BIG))
                mf = cb * lmat
                dm = _dot(dy4_bf[:, ks], xdt4_bf[:, ks], _NT)
                dcb = dcb + dm * lmat
                gmat = dm * mf
                g_rows = g_rows + jnp.where(lane == h, jnp.sum(gmat, axis=1, keepdims=True), 0.0)
                g_cols = g_cols + jnp.where(sub == h, _sum0(gmat), 0.0)
                dxdts.append(_dot(mf.astype(BF16), dy4_bf[:, ks], _TN))
                et = etot[:, h:h + 1]
                dtot = dtot + jnp.where(lane1 == h, _sum0(hsum[ks]) * et, 0.0)
                dh_scr[h * SSD_HEAD_DIM:(h + 1) * SSD_HEAD_DIM, :] = dh4[ks] * et + dh_new4[ks]
            dxdt4 = jnp.concatenate(dxdts, axis=1) + bdh4 * decx[:, gs]
            q_dt.append(dxdt4 * xs_all[:, gs])
            dx4 = dxdt4 * dtx[:, gs]
            if direct:
                dx4 = dx4 + dy4 * dv_ref[:, gs]
            dcb_bf = dcb.astype(BF16)
            dx_ref[:, gs] = dx4
            dx_ref[:, SSD_INNER + g * SSD_STATE:SSD_INNER + (g + 1) * SSD_STATE] = dbg + _dot(dcb_bf, cg, _TN)
            dx_ref[:, SSD_INNER + 1024 + g * SSD_STATE:SSD_INNER + 1024 + (g + 1) * SSD_STATE] = \
                dcg + _dot(dcb_bf, bg)
        e_heads = gather(jnp.concatenate(q_e, axis=1))
        dacum = gather(jnp.concatenate(q_col, axis=1)) + g_rows - g_cols.T
        dal = _dot(tri_t.astype(F32), dacum, precision=HI) + dtot + _sum0(e_heads)
        ddt = gather(jnp.concatenate(q_dt, axis=1)) + dal * a_r
        ddt_raw = ddt * _sig(dt_ref[...] + br_ref[...])
        ddt_ref[...] = ddt_raw
        dal_ref[...] += _sum0(dal * dt) * a_r
        dbias_ref[...] += _sum0(ddt_raw)

    small = lambda shape: pl.BlockSpec(shape, lambda r: (0, 0))
    return pl.pallas_call(
        kern, name=name, grid=(nc,),
        in_specs=[pl.BlockSpec((CHUNK, SSD_INNER), lambda r: (cidx(r), 0)),
                  pl.BlockSpec((CHUNK, SSD_INNER), lambda r: (cidx(r), 0)),
                  pl.BlockSpec((CHUNK, 1024), lambda r: (cidx(r), 2)),
                  pl.BlockSpec((CHUNK, 1024), lambda r: (cidx(r), 3)),
                  pl.BlockSpec((None, SSD_INNER, SSD_STATE), lambda r: (step_of(r), 0, 0)),
                  pl.BlockSpec((CHUNK, nh), lambda r: (cidx(r), 0)),
                  pl.BlockSpec((nh, CHUNK), lambda r: (0, cidx(r))),
                  small((1, nh)), small((nh, 1)), small((1, nh)), small((nh, 1)), small((1, SSD_INNER)),
                  small((nh, SSD_INNER)), small((SSD_INNER, nh))],
        out_specs=[pl.BlockSpec((CHUNK, SSD_CONV_DIM), lambda r: (cidx(r), 0)),
                   pl.BlockSpec((CHUNK, nh), lambda r: (cidx(r), 0)),
                   small((1, nh)), small((1, nh))],
        out_shape=[jax.ShapeDtypeStruct((n, SSD_CONV_DIM), F32), jax.ShapeDtypeStruct((n, nh), F32),
                   jax.ShapeDtypeStruct((1, nh), F32), jax.ShapeDtypeStruct((1, nh), F32)],
        scratch_shapes=[pltpu.VMEM((SSD_INNER, SSD_STATE), F32)],
        compiler_params=_params("arbitrary"),
    )(dy, xbc, xbc, xbc, hs, dt_raw, dtT_raw, bias_r, bias_c, alog_r, alog_c, dvec, _head_spread(),
      _head_spread().T)


def _gm_spatial_fwd(gu, gvn, ws, bst, *, name):
    n = gu.shape[0]

    def kern(gu_ref, gv_ref, ws_ref, bs_ref, o_ref):
        for g in range(GM_GROUPS):
            sl = slice(g * GM_GROUP_DIM, (g + 1) * GM_GROUP_DIM)
            s = _dot(ws_ref[g], gv_ref[:, sl]) + bs_ref[:, g:g + 1]
            o_ref[:, sl] = (gu_ref[:, sl] * s).astype(BF16)

    spec = pl.BlockSpec((CHUNK, GM_INNER), lambda i: (i, 0))
    return pl.pallas_call(
        kern, name=name, grid=(n // CHUNK,),
        in_specs=[spec, spec, pl.BlockSpec(ws.shape, lambda i: (0, 0, 0)), pl.BlockSpec(bst.shape, lambda i: (0, 0))],
        out_specs=spec, out_shape=jax.ShapeDtypeStruct((n, GM_INNER), BF16),
        compiler_params=_params("parallel"),
    )(gu, gvn, ws, bst)


def _gm_spatial_bwd(dt, gu, gvn, ws, wst, bst, *, name):
    n = gu.shape[0]

    def kern(dt_ref, gu_ref, gv_ref, ws_ref, wst_ref, bs_ref, dgu_ref, dgv_ref, dws_ref, dbs_ref):
        @pl.when(pl.program_id(0) == 0)
        def _():
            dws_ref[...] = jnp.zeros_like(dws_ref)
            dbs_ref[...] = jnp.zeros_like(dbs_ref)

        lane = lax.broadcasted_iota(jnp.int32, (CHUNK, GM_GROUPS), 1)
        dbs = jnp.zeros((CHUNK, GM_GROUPS), F32)
        for g in range(GM_GROUPS):
            sl = slice(g * GM_GROUP_DIM, (g + 1) * GM_GROUP_DIM)
            gv = gv_ref[:, sl]
            s = _dot(ws_ref[g], gv) + bs_ref[:, g:g + 1]
            d = dt_ref[:, sl]
            dgu_ref[:, sl] = d * s
            ds = d * gu_ref[:, sl]
            ds_bf = ds.astype(BF16)
            dws_ref[g] += _dot(ds_bf, gv, _NT)
            dgv_ref[:, sl] = _dot(wst_ref[g], ds_bf)
            dbs = dbs + jnp.where(lane == g, jnp.sum(ds, axis=1, keepdims=True), 0.0)
        dbs_ref[...] += dbs

    spec = pl.BlockSpec((CHUNK, GM_INNER), lambda i: (i, 0))
    wspec = pl.BlockSpec(ws.shape, lambda i: (0, 0, 0))
    bspec = pl.BlockSpec(bst.shape, lambda i: (0, 0))
    return pl.pallas_call(
        kern, name=name, grid=(n // CHUNK,),
        in_specs=[spec, spec, spec, wspec, wspec, bspec],
        out_specs=[spec, spec, wspec, bspec],
        out_shape=[jax.ShapeDtypeStruct((n, GM_INNER), F32), jax.ShapeDtypeStruct((n, GM_INNER), F32),
                   jax.ShapeDtypeStruct(ws.shape, F32), jax.ShapeDtypeStruct(bst.shape, F32)],
        compiler_params=_params("arbitrary"),
    )(dt, gu, gvn, ws, wst, bst)


def _adamw(parts, w, m, v, *, name, tm=256, sel=(), into=None):
    ns, r, wd = parts.shape
    tm = _pick(r, tm, 8)
    tc = wd
    if tm < 64 and wd % 256 == 0:
        tm, tc = r, 256
    lead = len(sel)
    assert w.shape[lead:] == (r, wd) and lead == w.ndim - 2

    def kern(*refs):
        p_ref, w_ref, m_ref, v_ref = refs[:4]
        g_ref, d_ref, nm_ref, nv_ref = refs[-4:]
        g = p_ref[0].astype(F32)
        for s in range(1, ns):
            g = g + p_ref[s].astype(F32)
        m2 = ADAM_B1 * m_ref[...] + (1.0 - ADAM_B1) * g
        v2 = ADAM_B2 * v_ref[...] + (1.0 - ADAM_B2) * (g * g)
        m_hat = m2 / (1.0 - ADAM_B1 ** ADAM_STEP)
        v_hat = v2 / (1.0 - ADAM_B2 ** ADAM_STEP)
        g_ref[...] = g
        d_ref[...] = -ADAM_LR * (m_hat / (jnp.sqrt(v_hat) + ADAM_EPS) + ADAM_WD * w_ref[...])
        nm_ref[...] = m2
        nv_ref[...] = v2

    spec = pl.BlockSpec((None,) * lead + (tm, tc), lambda i, j: tuple(sel) + (i, j))
    extra, aliases = [], {}
    if into is not None:
        extra = list(into)
        aliases = {4 + k: k for k in range(4)}
    return pl.pallas_call(
        kern, name=name, grid=(r // tm, wd // tc),
        in_specs=[pl.BlockSpec((ns, tm, tc), lambda i, j: (0, i, j)), spec, spec, spec] +
                 [pl.BlockSpec(memory_space=pl.ANY)] * len(extra),
        out_specs=[spec] * 4, out_shape=[jax.ShapeDtypeStruct(w.shape, F32)] * 4,
        input_output_aliases=aliases,
        compiler_params=_params("parallel", "parallel"),
    )(parts, w, m, v, *extra)


def _sum_slots(parts, *, name, scale_by=None):
    ns, r, wd = parts.shape

    def kern(*refs):
        p_ref, o_ref = refs[0], refs[-1]
        g = p_ref[0]
        for s in range(1, ns):
            g = g + p_ref[s]
        if scale_by is not None:
            g = g * _dsilu(refs[1][...])
        o_ref[...] = g

    args = [parts] + ([] if scale_by is None else [scale_by])
    return pl.pallas_call(kern, name=name, out_shape=jax.ShapeDtypeStruct((r, wd), F32),
                          compiler_params=pltpu.CompilerParams(vmem_limit_bytes=VMEM_LIMIT_BYTES))(*args)


def _mesh_pos():
    x, y, c = lax.axis_index("x"), lax.axis_index("y"), lax.axis_index("c")
    return x, y, c, 4 * x + 2 * y + c


def _flip(x, y, c, f):
    fx, fy, fc = (f >> 2) & 1, (f >> 1) & 1, f & 1
    px = 1 - x if fx else x
    py = 1 - y if fy else y
    pc = 1 - c if fc else c
    return (px, py, pc), 4 * px + 2 * py + pc


_HBM_SPEC = pl.BlockSpec(memory_space=pltpu.HBM)


def _exchange(arrays, *, scatter, name):
    na = len(arrays)
    if scatter:
        out_shape = [jax.ShapeDtypeStruct(a.shape, a.dtype) for a in arrays]
    else:
        out_shape = [jax.ShapeDtypeStruct((NDEV,) + a.shape, a.dtype) for a in arrays]

    out_shape.append(jax.ShapeDtypeStruct((8, 128), F32))

    def body(*refs):
        ins, outs = refs[:na], refs[na:2 * na]
        send_sems, recv_sems, local_sems = refs[2 * na + 1:]
        refs[2 * na][...] = jnp.zeros((8, 128), F32)
        x, y, c, me = _mesh_pos()
        copies = []
        for i in range(na):
            src_own = ins[i].at[me] if scatter else ins[i]
            lc = pltpu.make_async_copy(src_own, outs[i].at[me], local_sems.at[i])
            lc.start()
            copies.append(lc)
        sends = []
        for f in range(1, NDEV):
            peer, pidx = _flip(x, y, c, f)
            for i in range(na):
                k = i * (NDEV - 1) + f - 1
                src = ins[i].at[pidx] if scatter else ins[i]
                cp = pltpu.make_async_remote_copy(
                    src_ref=src, dst_ref=outs[i].at[me], send_sem=send_sems.at[k], recv_sem=recv_sems.at[k],
                    device_id=peer, device_id_type=pl.DeviceIdType.MESH)
                cp.start()
                sends.append(cp)
        for f in range(1, NDEV):
            peer, pidx = _flip(x, y, c, f)
            for i in range(na):
                k = i * (NDEV - 1) + f - 1
                src = ins[i].at[pidx] if scatter else ins[i]
                pltpu.make_async_remote_copy(
                    src_ref=src, dst_ref=outs[i].at[pidx], send_sem=send_sems.at[k], recv_sem=recv_sems.at[k],
                    device_id=peer, device_id_type=pl.DeviceIdType.MESH).wait_recv()
        for cp in sends:
            cp.wait_send()
        for lc in copies:
            lc.wait()

    res = pl.pallas_call(
        body, name=name, out_shape=out_shape, in_specs=[_HBM_SPEC] * na,
        out_specs=[_HBM_SPEC] * na + [pl.BlockSpec(memory_space=pltpu.VMEM)],
        scratch_shapes=[pltpu.SemaphoreType.DMA((na * (NDEV - 1),)), pltpu.SemaphoreType.DMA((na * (NDEV - 1),)),
                        pltpu.SemaphoreType.DMA((na,))],
        compiler_params=pltpu.CompilerParams(has_side_effects=True),
    )(*arrays)
    return res[:na], res[na][0, 0]


_SEM_SPEC = pl.BlockSpec(memory_space=pltpu.SEMAPHORE)
_DATAFLOW = pltpu.SideEffectType.DATAFLOW_SIDE_EFFECTING


def _split_copies(srcs, lands, send_sems, recv_sems, scatter, arriving):
    x, y, c, me = _mesh_pos()
    copies = []
    for i in range(len(srcs)):
        for f in range(1, NDEV):
            peer, pidx = _flip(x, y, c, f)
            k = i * (NDEV - 1) + f - 1
            copies.append(pltpu.make_async_remote_copy(
                src_ref=srcs[i].at[pidx] if scatter else srcs[i], dst_ref=lands[i].at[pidx if arriving else me],
                send_sem=send_sems.at[k], recv_sem=recv_sems.at[k], device_id=peer,
                device_id_type=pl.DeviceIdType.MESH))
    return copies


def _exchange_start(srcs, lands, *, scatter, name):
    na = len(srcs)
    nsem = na * (NDEV - 1)

    def body(*refs):
        ins_src, ins_land = refs[:na], refs[na:2 * na]
        send_sems, recv_sems = refs[2 * na], refs[2 * na + 1]
        token = refs[-1]
        for cp in _split_copies(ins_src, ins_land, send_sems, recv_sems, scatter, False):
            cp.start()
        token[...] = jnp.zeros_like(token)

    thru = [pltpu.HBM(a.shape, a.dtype) for a in list(srcs) + list(lands)]
    res = pl.pallas_call(
        body, name=name,
        out_shape=(pltpu.SemaphoreType.DMA((nsem,)), pltpu.SemaphoreType.DMA((nsem,)), *thru,
                   jax.ShapeDtypeStruct((8, 128), F32)),
        in_specs=[_HBM_SPEC] * (2 * na),
        out_specs=(_SEM_SPEC, _SEM_SPEC, *([_HBM_SPEC] * (2 * na)), pl.BlockSpec(memory_space=pltpu.VMEM)),
        input_output_aliases={i: 2 + i for i in range(2 * na)},
        compiler_params=pltpu.CompilerParams(has_side_effects=_DATAFLOW),
    )(*[pltpu.with_memory_space_constraint(a, pltpu.HBM) for a in list(srcs) + list(lands)])
    send_sems, recv_sems = res[0], res[1]
    return send_sems, recv_sems, res[2:2 + na], res[2 + na:2 + 2 * na], res[-1][0, 0]


def _exchange_wait(send_sems, recv_sems, srcs, lands, after, *, scatter, name):
    na = len(srcs)

    def body(*refs):
        ins_src, ins_land = refs[:na], refs[na:2 * na]
        s_sems, r_sems = refs[2 * na], refs[2 * na + 1]
        for cp in _split_copies(ins_src, ins_land, s_sems, r_sems, scatter, False):
            cp.wait_send()
        for cp in _split_copies(ins_src, ins_land, s_sems, r_sems, scatter, True):
            cp.wait_recv()

    thru = [pltpu.HBM(a.shape, a.dtype) for a in list(srcs) + list(lands)]
    res = pl.pallas_call(
        body, name=name, out_shape=tuple(thru),
        in_specs=[_HBM_SPEC] * (2 * na) + [_SEM_SPEC, _SEM_SPEC, pl.BlockSpec(memory_space=pl.ANY)],
        out_specs=tuple([_HBM_SPEC] * (2 * na)),
        input_output_aliases={i: i for i in range(2 * na)},
        compiler_params=pltpu.CompilerParams(has_side_effects=_DATAFLOW),
    )(*srcs, *lands, send_sems, recv_sems, after)
    return res[na:]


def _landing(block, me):
    buf = lax.empty((NDEV,) + block.shape, block.dtype)
    return lax.dynamic_update_slice_in_dim(buf, block[None], me, axis=0)


def _seg_kw(nseg, n_ctx, tm):
    return dict(nseg=nseg, seg_blocks=(n_ctx // tm if nseg == 2 else 0))


def _ffn_fwd(tag, h, gpre, gpost, shift, scale, gate, w, *, nseg, n_ctx, tm):
    n = h.shape[0]
    kw = _seg_kw(nseg, n_ctx, tm)
    (u,) = _rowwise(tag + "_pre", _pre_fwd_fn, n, [h], [("full", gpre), ("seg", shift), ("seg", scale)],
                    [(D_MODEL, BF16)], tm=tm, **kw)
    if "early" in w:
        w.update(w.pop("early")(u))
    s, a, b = _mm_glu(u, w["win_t"], name=tag + "_glu")
    if "late" in w:
        w.update(w.pop("late")(s))
    y, ho = _mm_rows(s, w["wout"], functools.partial(_out_post_fn, 0.5), [h], [("full", gpost), ("seg", gate)],
                     [(D_MODEL, F32), (D_MODEL, F32)], name=tag + "_out", tk=FFN_DIM, n_ctx=n_ctx)
    return ho, dict(h=h, u=u, s=s, a=a, b=b, y=y)


def _ffn_bwd(tag, dho, sv, gpre, gpost, scale, gate, w, put, *, nseg, n_ctx, tm):
    n = dho.shape[0]
    kw = _seg_kw(nseg, n_ctx, tm)
    dy, dgate, dgpost = _rowwise(tag + "_postb", functools.partial(_post_bwd_fn, 0.5), n, [dho, sv["y"]],
                                 [("full", gpost), ("seg", gate)], [(D_MODEL, BF16)], [D_MODEL, D_MODEL], tm=tm, **kw)
    tok = put("w_out", _mm_tn(sv["s"], dy, name=tag + "_dwout", tm=1408, tn=1024, col_blocks=1))
    dp = _mm_glu_bwd(dy, w["wout"], sv["a"], sv["b"], name=tag + "_ds")
    tok2 = put("w_in", _mm_tn(dp, sv["u"], name=tag + "_dwin", tm=1408, tn=1024, col_blocks=1))
    for t in (tok, tok2):
        if t is not None:
            gpre = gpre + t
    dh, dshift, dscale, dgpre = _mm_rows(dp, w["win_t"], _pre_bwd_fn, [sv["h"], dho],
                                         [("full", gpre), ("seg", scale)], [(D_MODEL, F32)],
                                         [D_MODEL, D_MODEL, D_MODEL], name=tag + "_du", tk=FFN_DIM, n_ctx=n_ctx)
    return dh, None, dict(shift=dshift, scale=dscale, gate=dgate, gpre=dgpre, gpost=dgpost)


def _local_step(x, ctx, target, mods, norm_g, get_w, small, put_grad):
    t_len, n_ctx = x.shape[0], ctx.shape[0]
    n0 = t_len + n_ctx
    tm0 = _pick(n_ctx, 256, 8)
    tm1 = _pick(t_len, 512, 8)
    ncc = n_ctx // CHUNK
    g = {}

    def modrow(i, k, nseg):
        mc, mx = mods[i]
        if nseg == 2:
            return jnp.stack([mc[k], mx[k]])[:, None, :]
        return mx[k][None, None, :]

    pending = [None]

    def gvec(i, k):
        v = norm_g[i, k][None, :]
        if pending[0] is not None:
            v = v + pending[0]
            pending[0] = None
        return v

    xc = jnp.concatenate([ctx, x], axis=0)
    L0 = dict(nseg=2, n_ctx=n_ctx, tm=tm0)
    wts = dict(get_w("ffn00", xc))
    h1, sv_f01 = _ffn_fwd("l0f1", xc, gvec(0, 0), gvec(0, 1), modrow(0, 0, 2), modrow(0, 1, 2), modrow(0, 2, 2),
                          wts["ffn00"], **L0)
    kw0 = _seg_kw(2, n_ctx, tm0)
    (um0,) = _rowwise("l0m_pre", _pre_fwd_fn, n0, [h1], [("full", gvec(0, 2)), ("seg", modrow(0, 3, 2)),
                                                         ("seg", modrow(0, 4, 2))], [(D_MODEL, BF16)], tm=tm0, **kw0)
    wts.update(get_w("ssd", um0))
    win_ssd = wts["ssd_win_t"]
    nh = SSD_HEADS
    dt_blk = (SSD_INNER + SSD_CONV_DIM) // (2 * nh)
    z = _mm(um0, win_ssd, out_dtype=F32, name="ssd_z", rhs_t=True, n=SSD_INNER)
    xbc_pre = _mm(um0, win_ssd, out_dtype=F32, name="ssd_xbc", rhs_t=True, n=SSD_CONV_DIM,
                  b_off=(SSD_INNER // 1024, 0))
    dtr = _mm(um0, win_ssd, out_dtype=F32, name="ssd_dt", rhs_t=True, n=2 * nh, b_off=(dt_blk, 0))
    cpre, xbc = _conv_fwd(xbc_pre, small["conv_w8"], small["conv_b"], n_ctx=n_ctx, name="ssd_conv")
    nh = SSD_HEADS
    dt_dir = [dtr[:, :nh], dtr[:, nh:2 * nh]]
    dtT_dir = [d.T for d in dt_dir]
    bias_r = [small["dt_bias"][d][None, :] for d in range(2)]
    bias_c = [small["dt_bias"][d][:, None] for d in range(2)]
    alog_r = [small["a_log"][d][None, :] for d in range(2)]
    alog_c = [small["a_log"][d][:, None] for d in range(2)]
    ys, hss = [], []
    for d in range(2):
        yd, hsd = _ssd_scan_fwd(xbc, dt_dir[d], dtT_dir[d], bias_r[d], bias_c[d], alog_r[d], alog_c[d],
                                rev=(d == 1), n_ctx_chunks=ncc, name=f"ssd_scan{d}")
        ys.append(yd)
        hss.append(hsd)
    dvec = jnp.repeat(small["ssd_d"], SSD_HEAD_DIM)[None, :]
    ngv = small["ssd_norm_g"][None, :]
    gate_rows = [ys[0], ys[1], (xbc, SSD_INNER, 0, 0), z]
    off = n_ctx // tm0
    lat = lambda r: (r[0], r[1], r[2], off) if isinstance(r, tuple) else (r, r.shape[1], 0, off)
    (yn,) = _rowwise("ssd_gate", _ssdgate_fwd_fn, t_len, [lat(r) for r in gate_rows],
                     [("full", dvec), ("full", ngv)], [(SSD_INNER, BF16)], tm=tm0)
    h1x = h1[n_ctx:]
    L1 = dict(nseg=1, n_ctx=0, tm=_pick(t_len, 512, 8))
    if "late" in wts:
        wts.update(wts.pop("late")(yn))
    yo0, h2 = _mm_rows(yn, wts["ssd_wout"], functools.partial(_out_post_fn, 1.0), [h1x],
                       [("full", gvec(0, 3)), ("seg", modrow(0, 5, 1))], [(D_MODEL, F32), (D_MODEL, F32)],
                       name="ssd_out", tk=SSD_INNER)
    wts.update(get_w("ffn01", h2))
    h3, sv_f02 = _ffn_fwd("l0f2", h2, gvec(0, 4), gvec(0, 5), modrow(0, 6, 1), modrow(0, 7, 1), modrow(0, 8, 1),
                          wts["ffn01"], **L1)

    wts.update(get_w("ffn10", h3))
    h4, sv_f11 = _ffn_fwd("l1f1", h3, gvec(1, 0), gvec(1, 1), modrow(1, 0, 1), modrow(1, 1, 1), modrow(1, 2, 1),
                          wts["ffn10"], **L1)
    (um1,) = _rowwise("l1m_pre", _pre_fwd_fn, t_len, [h4], [("full", gvec(1, 2)), ("seg", modrow(1, 3, 1)),
                                                            ("seg", modrow(1, 4, 1))], [(D_MODEL, BF16)], tm=tm1)
    wts.update(get_w("gm", um1))
    p1 = _mm(um1, wts["gm_win"], out_dtype=F32, name="gm_in", tm=2048)
    vg = small["gm_v_g"][None, :]
    vb = small["gm_v_b"][None, :]
    gu, gvn = _rowwise("gm_act", _gm_act_fwd_fn, t_len, [p1], [("full", vg), ("full", vb)],
                       [(GM_INNER, F32), (GM_INNER, BF16)], tm=256)
    ws_bf = small["gm_w_s"].astype(BF16)
    wst_bf = jnp.swapaxes(small["gm_w_s"], 1, 2).astype(BF16)
    bst = small["gm_b_s"].T
    tgm = _gm_spatial_fwd(gu, gvn, ws_bf, bst, name="gm_spatial")
    yo1, h5 = _mm_rows(tgm, wts["gm_wout"], functools.partial(_out_post_fn, 1.0), [h4],
                       [("full", gvec(1, 3)), ("seg", modrow(1, 5, 1))], [(D_MODEL, F32), (D_MODEL, F32)],
                       name="gm_out", tk=GM_INNER)
    wts.update(get_w("ffn11", h5))
    h6, sv_f12 = _ffn_fwd("l1f2", h5, gvec(1, 4), gvec(1, 5), modrow(1, 6, 1), modrow(1, 7, 1), modrow(1, 8, 1),
                          wts["ffn11"], **L1)

    dh, loss_parts = _rowwise("loss", _loss_fn, t_len, [h6, target], [], [(D_MODEL, F32)], [D_MODEL], tm=tm1)

    zero = jnp.zeros((D_MODEL,), F32)
    dmx = [[zero] * N_MOD for _ in range(2)]
    dmc = [[zero] * N_MOD for _ in range(2)]
    dng = [[zero] * 6 for _ in range(2)]

    def put_mod(i, k, acc):
        if acc.shape[0] == 2:
            dmc[i][k] = dmc[i][k] + acc[0, 0]
            dmx[i][k] = dmx[i][k] + acc[1, 0]
        else:
            dmx[i][k] = dmx[i][k] + acc[0, 0]

    def put_g(i, k, acc):
        dng[i][k] = dng[i][k] + jnp.sum(acc[:, 0], axis=0)

    def ffn_back(tag, i, j, dho, sv, w, lay):
        nseg = lay["nseg"]
        base = 0 if j == 0 else 6
        gi = 0 if j == 0 else 4
        dh_in, pending[0], s = _ffn_bwd(tag, dho, sv, gvec(i, gi), gvec(i, gi + 1), modrow(i, base + 1, nseg),
                                        modrow(i, base + 2, nseg), w, functools.partial(put_grad, f"ffn{i}{j}"), **lay)
        put_mod(i, base, s["shift"])
        put_mod(i, base + 1, s["scale"])
        put_mod(i, base + 2, s["gate"])
        put_g(i, gi, s["gpre"])
        put_g(i, gi + 1, s["gpost"])
        return dh_in

    dh = ffn_back("l1f2", 1, 1, dh, sv_f12, wts["ffn11"], L1)
    dyo, dgate, dgp = _rowwise("l1m_postb", functools.partial(_post_bwd_fn, 1.0), t_len, [dh, yo1],
                               [("full", gvec(1, 3)), ("seg", modrow(1, 5, 1))], [(D_MODEL, BF16)],
                               [D_MODEL, D_MODEL], tm=tm1)
    put_mod(1, 5, dgate)
    put_g(1, 3, dgp)
    put_grad("gm", "w_out", _mm_tn(tgm, dyo, name="gm_dwout", tn=1024, col_blocks=1))
    dtg = _mm(dyo, wts["gm_wout"], out_dtype=F32, name="gm_dt", rhs_t=True)
    dgu, dgvn, dws, dbst = _gm_spatial_bwd(dtg, gu, gvn, ws_bf, wst_bf, bst, name="gm_spatialb")
    g["gm_w_s"] = dws
    g["gm_b_s"] = dbst.T
    dp1, dvg, dvb = _rowwise("gm_actb", _gm_act_bwd_fn, t_len, [p1, dgu, dgvn], [("full", vg)],
                             [(2 * GM_INNER, BF16)], [GM_INNER, GM_INNER], tm=256)
    g["gm_v_g"] = dvg[0, 0]
    g["gm_v_b"] = dvb[0, 0]
    pending[0] = put_grad("gm", "w_in", _mm_tn(um1, dp1, name="gm_dwin", tm=1024, col_blocks=NDEV))
    dh, dsh, dsc, dgp = _mm_rows(dp1, wts["gm_win"], _pre_bwd_fn, [h4, dh],
                                 [("full", gvec(1, 2)), ("seg", modrow(1, 4, 1))], [(D_MODEL, F32)],
                                 [D_MODEL, D_MODEL, D_MODEL], name="gm_dum", tk=2048, rhs_t=True)
    put_mod(1, 3, dsh)
    put_mod(1, 4, dsc)
    put_g(1, 2, dgp)
    dh = ffn_back("l1f1", 1, 0, dh, sv_f11, wts["ffn10"], L1)

    dh = ffn_back("l0f2", 0, 1, dh, sv_f02, wts["ffn01"], L1)
    dyo, dgate, dgp = _rowwise("l0m_postb", functools.partial(_post_bwd_fn, 1.0), t_len, [dh, yo0],
                               [("full", gvec(0, 3)), ("seg", modrow(0, 5, 1))], [(D_MODEL, BF16)],
                               [D_MODEL, D_MODEL], tm=tm1)
    put_mod(0, 5, dgate)
    put_g(0, 3, dgp)
    tok = put_grad("ssd", "w_out", _mm_tn(yn, dyo, name="ssd_dwout", tn=1024, col_blocks=1))
    dyn = _mm(dyo, wts["ssd_wout"], out_dtype=F32, name="ssd_dyn", rhs_t=True)
    dy_ssd, dz, dngv, ddv = _rowwise("ssd_gateb", _ssdgate_bwd_fn, n0,
                                     [(dyn, SSD_INNER, 0, -(n_ctx // tm0))] + gate_rows,
                                     [("full", dvec), ("full", ngv if tok is None else ngv + tok)],
                                     [(SSD_INNER, F32), (SSD_INNER, BF16)],
                                     [SSD_INNER, SSD_INNER], tm=tm0)
    g["ssd_norm_g"] = dngv[0, 0]
    g["ssd_D"] = jnp.sum(ddv[0, 0].reshape(SSD_HEADS, SSD_HEAD_DIM), axis=1)
    dxbcs, ddts, dalogs, dbiases = [], [], [], []
    for d in range(2):
        dxd, ddtd, dal, dbi = _ssd_scan_bwd(dy_ssd, xbc, hss[d], dt_dir[d], dtT_dir[d], bias_r[d], bias_c[d],
                                            alog_r[d], alog_c[d], dvec, rev=(d == 1), n_ctx_chunks=ncc,
                                            direct=(d == 0), name=f"ssd_scanb{d}")
        dxbcs.append(dxd)
        ddts.append(ddtd)
        dalogs.append(dal[0])
        dbiases.append(dbi[0])
    g["ssd_A_log"] = jnp.stack(dalogs)
    g["ssd_dt_bias"] = jnp.stack(dbiases)
    dxbc_pre, dcw8, dcb = _conv_bwd(dxbcs[0], dxbcs[1], cpre, xbc_pre, small["conv_w8"], n_ctx=n_ctx, name="ssd_convb")
    g["ssd_conv_w"] = dcw8[:SSD_CONV]
    g["ssd_conv_b"] = dcb[0]
    ddt_bf = jnp.concatenate([ddts[0], ddts[1]], axis=1).astype(BF16)
    n_in = SSD_INNER + SSD_CONV_DIM + 2 * nh
    dw_t = _mm_tn(dz, um0, name="ssd_dwz", col_blocks=1, stack=(n_in, 0, None))
    dw_t = _mm_tn(dxbc_pre, um0, name="ssd_dwxbc", col_blocks=1, stack=(n_in, SSD_INNER, dw_t))
    dw_t = _mm_tn(ddt_bf, um0, name="ssd_dwdt", col_blocks=1, stack=(n_in, SSD_INNER + SSD_CONV_DIM, dw_t))
    pending[0] = put_grad("ssd", "w_in", dw_t)
    dum0 = _mm(dz, win_ssd, out_dtype=F32, name="ssd_dum_z", tk=SSD_INNER, n=D_MODEL)
    dum0 = _mm(dxbc_pre, win_ssd, out_dtype=F32, name="ssd_dum_x", tk=SSD_INNER, n=D_MODEL,
               b_off=(SSD_INNER // SSD_INNER, 0), add=dum0)
    dum0 = _mm(ddt_bf, win_ssd, out_dtype=F32, name="ssd_dum_dt", tk=2 * nh, n=D_MODEL, b_off=(dt_blk, 0), add=dum0)
    dh0, dsh, dsc, dgp = _rowwise("l0m_preb", _pre_bwd_fn, n0, [dum0, h1, (dh, D_MODEL, 0, -(n_ctx // tm0))],
                                  [("full", gvec(0, 2)), ("seg", modrow(0, 4, 2))], [(D_MODEL, F32)],
                                  [D_MODEL, D_MODEL, D_MODEL], tm=tm0, **kw0)
    put_mod(0, 3, dsh)
    put_mod(0, 4, dsc)
    put_g(0, 2, dgp)
    dh0 = ffn_back("l0f1", 0, 0, dh0, sv_f01, wts["ffn00"], L0)
    grad_x = dh0[n_ctx:]
    g["norm_g"] = jnp.stack([jnp.stack(r) for r in dng])
    g["dmx"] = jnp.stack([jnp.concatenate(r) for r in dmx])
    g["dmc"] = jnp.stack([jnp.concatenate(r) for r in dmc])
    return loss_parts[0], grad_x, g


GROUPS = ("ffn00", "ssd", "ffn01", "ffn10", "gm", "ffn11")


TRANSPOSED_IN = ("ffn", "ssd")


def _is_transposed(group):
    return group.startswith(TRANSPOSED_IN)


def _mats_in(group, win_l):
    if _is_transposed(group):
        return {("win_t" if group.startswith("ffn") else group + "_win_t"): win_l.reshape(-1, win_l.shape[2])}
    return {group + "_win": win_l}


def _mats_out(group, wout_l):
    pre = "" if group.startswith("ffn") else group + "_"
    return {pre + "wout": wout_l.reshape(-1, wout_l.shape[2])}


def _group_mats(group, lands):
    m = {**_mats_in(group, lands[0]), **_mats_out(group, lands[1])}
    return {group: m} if group.startswith("ffn") else m


def _grad_blocks(which, grad):
    if grad.ndim == 3:
        return grad if grad.shape[0] == NDEV else grad.reshape(NDEV, grad.shape[1] // NDEV, grad.shape[2])
    if which == "w_in":
        k, n = grad.shape
        return jnp.transpose(grad.reshape(k, NDEV, n // NDEV), (1, 0, 2)).astype(BF16)
    return grad.reshape(NDEV, grad.shape[0] // NDEV, grad.shape[1]).astype(BF16)


def kernel(x, c, ctx, c_ctx, ada_w, ada_b, norm_g, ffn_w_in, ffn_w_out, ssd_w_in, ssd_conv_w, ssd_conv_b, ssd_dt_bias, ssd_A_log, ssd_D, ssd_norm_g, ssd_w_out, gm_w_in, gm_v_g, gm_v_b, gm_w_s, gm_b_s, gm_w_out, loss_target, m_c_ctx, m_ada_w, m_ada_b, m_norm_g, m_ffn_w_in, m_ffn_w_out, m_ssd_w_in, m_ssd_conv_w, m_ssd_conv_b, m_ssd_dt_bias, m_ssd_A_log, m_ssd_D, m_ssd_norm_g, m_ssd_w_out, m_gm_w_in, m_gm_v_g, m_gm_v_b, m_gm_w_s, m_gm_b_s, m_gm_w_out, v_c_ctx, v_ada_w, v_ada_b, v_norm_g, v_ffn_w_in, v_ffn_w_out, v_ssd_w_in, v_ssd_conv_w, v_ssd_conv_b, v_ssd_dt_bias, v_ssd_A_log, v_ssd_D, v_ssd_norm_g, v_ssd_w_out, v_gm_w_in, v_gm_v_g, v_gm_v_b, v_gm_w_s, v_gm_b_s, v_gm_w_out):
    me = 4 * lax.axis_index("x") + 2 * lax.axis_index("y") + lax.axis_index("c")
    d = D_MODEL
    ncol = N_MOD * d // NDEV

    small_pack = jnp.concatenate([c.reshape(-1), norm_g.reshape(-1), ssd_conv_w.reshape(-1),
                                  gm_v_g.reshape(-1), gm_v_b.reshape(-1)])[None, :]
    (sp,), _ = _exchange([small_pack], scatter=False, name="gather_small")
    sp = sp[:, 0]
    o = 0
    c_all = sp[:, o:o + d]; o += d
    ng_all = sp[:, o:o + 2 * 6 * 128].reshape(NDEV, 2, 6, 128); o += 2 * 6 * 128
    cw_all = sp[:, o:o + SSD_CONV * 512].reshape(NDEV, SSD_CONV, 512); o += SSD_CONV * 512
    vg_all = sp[:, o:o + 256]; o += 256
    vb_all = sp[:, o:o + 256]; o += 256
    norm_g_full = jnp.transpose(ng_all, (1, 2, 0, 3)).reshape(2, 6, d)
    conv_w_full = jnp.transpose(cw_all, (1, 0, 2)).reshape(SSD_CONV, SSD_CONV_DIM)
    gm_v_g_full = vg_all.reshape(-1)
    gm_v_b_full = vb_all.reshape(-1)

    c16 = jnp.concatenate([c_all, jnp.broadcast_to(c_ctx[None, :], (NDEV, d))], axis=0)
    ada_b_loc = lax.dynamic_slice_in_dim(ada_b, me * ncol, ncol, axis=1)
    mods_loc = jnp.stack([_mm_f32(c16, ada_w[i], name=f"ada_mod{i}", silu_a=True, bias=ada_b_loc[i][None, :])
                          for i in range(2)])
    (mods_all,), mods_done = _exchange([mods_loc], scatter=False, name="gather_mods")

    tr = lambda a: jnp.swapaxes(a, -1, -2)
    shard = {"ssd": (tr(ssd_w_in)[0], ssd_w_out[0]), "gm": (gm_w_in[0], gm_w_out[0])}
    for i in range(2):
        for j in range(2):
            shard[f"ffn{i}{j}"] = (tr(ffn_w_in)[i, j], ffn_w_out[i, j])
    apart = GROUPS[:2]
    units = []
    for grp in GROUPS:
        units += [(grp + "_in", grp, (0,)), (grp + "_out", grp, (1,))] if grp in apart else [(grp, grp, (0, 1))]
    gathers = {}
    started = mods_done
    for unit, grp, idx in units:
        srcs = [(shard[grp][k] + started).astype(BF16) for k in idx]
        st = _exchange_start(srcs, [_landing(s, me) for s in srcs], scatter=False, name="gather_start_" + unit)
        gathers[unit] = st[:4]
        started = st[4]

    def fetch(unit, after):
        return _exchange_wait(*gathers[unit], after, scatter=False, name="gather_wait_" + unit)

    def get_w(grp, after):
        if grp not in apart:
            return _group_mats(grp, fetch(grp, after))
        early = lambda later: _mats_in(grp, fetch(grp + "_in", later)[0])
        late = lambda later: _mats_out(grp, fetch(grp + "_out", later)[0])
        if grp.startswith("ffn"):
            return {grp: dict(early=early, late=late)}
        return dict(early(after), late=late)

    scatters = {}
    held = {}

    def put_grad(grp, which, grad):
        if grp in apart:
            unit, blocks = grp + "_" + which[2:], [_grad_blocks(which, grad)]
        else:
            held[grp, which] = _grad_blocks(which, grad)
            if (grp, "w_in") not in held or (grp, "w_out") not in held:
                return None
            unit, blocks = grp, [held[grp, "w_in"], held[grp, "w_out"]]
        own = [lax.dynamic_index_in_dim(b, me, axis=0, keepdims=False) for b in blocks]
        st = _exchange_start(blocks, [_landing(o_, me) for o_ in own], scatter=True, name="scatter_start_" + unit)
        scatters[unit] = st[:4]
        return st[4]

    mods_rows = jnp.transpose(mods_all, (1, 2, 0, 3)).reshape(2, 2 * NDEV, N_MOD * d) + started
    mx = lax.dynamic_index_in_dim(mods_rows, me, axis=1, keepdims=False).reshape(2, N_MOD, d)
    mc = mods_rows[:, NDEV].reshape(2, N_MOD, d)
    mods = [(mc[i], mx[i]) for i in range(2)]

    small = dict(conv_w8=jnp.pad(conv_w_full, ((0, 8 - SSD_CONV), (0, 0))), conv_b=ssd_conv_b, dt_bias=ssd_dt_bias[0],
                 a_log=ssd_A_log[0], ssd_d=ssd_D[0], ssd_norm_g=ssd_norm_g[0], gm_v_g=gm_v_g_full,
                 gm_v_b=gm_v_b_full, gm_w_s=gm_w_s[0], gm_b_s=gm_b_s[0])
    loss_parts, grad_x, g = _local_step(x[0], ctx[0], loss_target[0], mods, norm_g_full, get_w, small, put_grad)
    g["loss"] = (0.5 / d * jnp.sum(loss_parts)).reshape(1)

    whole = {"ffn_w_in": (tr(ffn_w_in), tr(m_ffn_w_in), tr(v_ffn_w_in)), "ffn_w_out": (ffn_w_out, m_ffn_w_out, v_ffn_w_out),
             "ssd_w_in": (tr(ssd_w_in), tr(m_ssd_w_in), tr(v_ssd_w_in)), "ssd_w_out": (ssd_w_out, m_ssd_w_out, v_ssd_w_out),
             "gm_w_in": (gm_w_in, m_gm_w_in, v_gm_w_in), "gm_w_out": (gm_w_out, m_gm_w_out, v_gm_w_out)}
    res = {}

    def update_units(some, after):
        for unit, grp, idx in some:
            parts = _exchange_wait(*scatters[unit], after, scatter=True, name="scatter_wait_" + unit)
            for k, p in zip(idx, parts):
                which = ("in", "out")[k]
                nm = ("ffn" if grp.startswith("ffn") else grp) + "_w_" + which
                sel = (int(grp[3]), int(grp[4])) if grp.startswith("ffn") else (0,)
                res[nm] = _adamw(p, *whole[nm], name=f"adamw_{grp}_{which}", sel=sel, into=res.get(nm))
                after = res[nm][0]
        return after

    sg_names = ["dmx", "dmc", "norm_g", "ssd_conv_w", "ssd_conv_b", "ssd_dt_bias", "ssd_A_log", "ssd_D", "ssd_norm_g",
                "gm_v_g", "gm_v_b", "gm_b_s", "loss"]
    sg_shapes = [g[n].shape for n in sg_names]
    flat = jnp.concatenate([g[n].reshape(-1) for n in sg_names])
    npack = flat.shape[0]
    pad = (-npack) % 1024
    flat = jnp.pad(flat, (0, pad)).reshape(-1, 128)
    gws = g["gm_w_s"].reshape(-1, CHUNK)
    sg_start = _exchange_start([flat, gws], [_landing(flat, me), _landing(gws, me)], scatter=False,
                               name="small_grads_start")
    by_send = list(reversed(units))
    update_units(by_send[:4], jnp.stack([sg_start[4], grad_x[0, 0]]))
    early_done = jnp.stack([res[nm][0].reshape(-1)[-1] for nm in sorted(res)])
    sg_all, gws_all = _exchange_wait(*sg_start[:4], early_done, scatter=False, name="small_grads_wait")
    sg_sum = _sum_slots(sg_all, name="sum_small_grads").reshape(-1)[:npack]
    update_units(by_send[4:], sg_sum)
    sums = {}
    o = 0
    for n, shp in zip(sg_names, sg_shapes):
        sz = math.prod(shp)
        sums[n] = sg_sum[o:o + sz].reshape(shp)
        o += sz
    loss = sums["loss"][0]
    per_dev = sg_all.reshape(NDEV, -1)
    dmx_all =per_dev[:, :2 * N_MOD * d].reshape(NDEV, 2, N_MOD * d)
    dmc_all = per_dev[:, 2 * N_MOD * d:4 * N_MOD * d].reshape(NDEV, 2, N_MOD * d)

    (s16,) = _rowwise("ada_silu", lambda cc: ((_silu(cc),), ()), 2 * NDEV, [c16], [], [(d, F32)], tm=2 * NDEV)
    s16_t = s16.T
    g_ada_w, dcc_parts = [], []
    for i in range(2):
        rhs = jnp.concatenate([lax.dynamic_slice_in_dim(dmx_all[:, i], me * ncol, ncol, axis=1),
                               lax.dynamic_slice_in_dim(dmc_all[:, i], me * ncol, ncol, axis=1)], axis=0)
        g_ada_w.append(_mm_f32(s16_t, rhs, name=f"ada_dw{i}"))
        dmc_loc = lax.dynamic_slice_in_dim(sums["dmc"][i], me * ncol, ncol, axis=0)
        rhs_c = jnp.zeros((ncol, 128), F32).at[:, 0].set(dmc_loc)
        dcc_parts.append(_mm_f32(ada_w[i], rhs_c, name=f"ada_dcc{i}")[:, 0])
    g_ada_w = jnp.stack(g_ada_w)
    dcc_part = (dcc_parts[0] + dcc_parts[1]).reshape(8, 128)
    (dcc_all,), _ = _exchange([dcc_part], scatter=False, name="gather_dcc")
    g_c_ctx = _sum_slots(dcc_all, name="sum_dcc", scale_by=c_ctx.reshape(8, 128)).reshape(d)
    g_ada_b = sums["dmx"] + sums["dmc"]

    outs = _adamw(g_ada_w.reshape(1, -1, ncol), ada_w.reshape(-1, ncol), m_ada_w.reshape(-1, ncol),
                  v_ada_w.reshape(-1, ncol), name="adamw_ada_w")
    res["ada_w"] = [o_.reshape(ada_w.shape) for o_ in outs]

    loc = lambda a, ax, n: lax.dynamic_slice_in_dim(a, me * n, n, axis=ax)
    small_g = dict(c_ctx=g_c_ctx, ada_b=g_ada_b, norm_g=loc(sums["norm_g"], 2, 128),
                   ssd_conv_w=loc(sums["ssd_conv_w"], 1, 512)[None], ssd_conv_b=sums["ssd_conv_b"][None],
                   ssd_dt_bias=sums["ssd_dt_bias"][None], ssd_A_log=sums["ssd_A_log"][None], ssd_D=sums["ssd_D"][None],
                   ssd_norm_g=sums["ssd_norm_g"][None], gm_v_g=loc(sums["gm_v_g"], 0, 256)[None],
                   gm_v_b=loc(sums["gm_v_b"], 0, 256)[None], gm_b_s=sums["gm_b_s"][None])
    small_w = dict(c_ctx=(c_ctx, m_c_ctx, v_c_ctx), ada_b=(ada_b, m_ada_b, v_ada_b), norm_g=(norm_g, m_norm_g, v_norm_g),
                   ssd_conv_w=(ssd_conv_w, m_ssd_conv_w, v_ssd_conv_w), ssd_conv_b=(ssd_conv_b, m_ssd_conv_b, v_ssd_conv_b),
                   ssd_dt_bias=(ssd_dt_bias, m_ssd_dt_bias, v_ssd_dt_bias), ssd_A_log=(ssd_A_log, m_ssd_A_log, v_ssd_A_log),
                   ssd_D=(ssd_D, m_ssd_D, v_ssd_D), ssd_norm_g=(ssd_norm_g, m_ssd_norm_g, v_ssd_norm_g),
                   gm_v_g=(gm_v_g, m_gm_v_g, v_gm_v_g), gm_v_b=(gm_v_b, m_gm_v_b, v_gm_v_b),
                   gm_b_s=(gm_b_s, m_gm_b_s, v_gm_b_s))
    sn = list(small_w)
    flat2 = lambda a: a.reshape(-1, CHUNK)
    res["gm_w_s"] = [o_.reshape(gm_w_s.shape) for o_ in _adamw(
        gws_all, flat2(gm_w_s), flat2(m_gm_w_s), flat2(v_gm_w_s), name="adamw_gm_w_s")]

    def pack(arrs):
        f = jnp.concatenate([a.reshape(-1) for a in arrs])
        return jnp.pad(f, (0, (-f.shape[0]) % (256 * 128))).reshape(-1, 128)

    pg = pack([small_g[n].reshape(small_w[n][0].shape) for n in sn])
    outs = _adamw(pg[None], pack([small_w[n][0] for n in sn]), pack([small_w[n][1] for n in sn]),
                  pack([small_w[n][2] for n in sn]), name="adamw_small")
    flat_outs = [o_.reshape(-1) for o_ in outs]
    o = 0
    for n in sn:
        shp = small_w[n][0].shape
        sz = math.prod(shp)
        res[n] = [fo[o:o + sz].reshape(shp) for fo in flat_outs]
        o += sz

    order = ["c_ctx", "ada_w", "ada_b", "norm_g", "ffn_w_in", "ffn_w_out", "ssd_w_in", "ssd_conv_w", "ssd_conv_b",
             "ssd_dt_bias", "ssd_A_log", "ssd_D", "ssd_norm_g", "ssd_w_out", "gm_w_in", "gm_v_g", "gm_v_b", "gm_w_s",
             "gm_b_s", "gm_w_out"]
    for nm in ("ffn_w_in", "ssd_w_in"):
        res[nm] = [tr(a) for a in res[nm]]
    result = [loss, grad_x[None]]
    for k in range(4):
        result += [res[n][k] for n in order]
    return tuple(result)
```

```python
import functools
import math

import jax
import jax.numpy as jnp
from jax import lax
from jax.experimental import pallas as pl
from jax.experimental.pallas import tpu as pltpu

F32 = jnp.float32
BF16 = jnp.bfloat16

NDEV = 8
D_MODEL = 1024
FFN_DIM = 2816
N_MOD = 9
EPS = 1e-6
SSD_INNER = 2048
SSD_HEADS = 32
SSD_HEAD_DIM = 64
SSD_GROUPS = 8
SSD_HPG = 4
SSD_STATE = 128
SSD_CONV = 5
SSD_CONV_DIM = 4096
CHUNK = 128
GM_INNER = 2048
GM_GROUPS = 8
GM_GROUP_DIM = 256
ADAM_LR = 0.001
ADAM_B1 = 0.9
ADAM_B2 = 0.999
ADAM_EPS = 1e-08
ADAM_WD = 0.01
ADAM_STEP = 10
NEG_BIG = -1e30
VMEM_LIMIT_BYTES = 56 * 1024 * 1024
HI = lax.Precision.HIGHEST


def _params(*sem):
    return pltpu.CompilerParams(dimension_semantics=sem, vmem_limit_bytes=VMEM_LIMIT_BYTES)


def _pick(n, target, mult=16):
    if n <= target:
        return n
    for t in range(target - target % mult, 0, -mult):
        if n % t == 0:
            return t
    raise ValueError((n, target, mult))


def _sig(x):
    return 0.5 * jnp.tanh(0.5 * x) + 0.5


def _silu(x):
    return x * _sig(x)


def _dsilu(x):
    s = _sig(x)
    return s * (1.0 + x * (1.0 - s))


_GELU_C = math.sqrt(2.0 / math.pi)


def _gelu(x):
    return 0.5 * x * (1.0 + jnp.tanh(_GELU_C * (x + 0.044715 * x * x * x)))


def _gelu_and_grad(x):
    x2 = x * x
    t = jnp.tanh(_GELU_C * (x + 0.044715 * x2 * x))
    half = 0.5 * (1.0 + t)
    return x * half, half + 0.5 * x * (1.0 - t * t) * _GELU_C * (1.0 + 3.0 * 0.044715 * x2)


def _dgelu(x):
    return _gelu_and_grad(x)[1]


def _softplus(x):
    return jnp.maximum(x, 0.0) + jnp.log1p(jnp.exp(-jnp.abs(x)))


def _sum0(v):
    return jnp.sum(v, axis=0, keepdims=True)


def _rms(h):
    r = lax.rsqrt(jnp.mean(h * h, axis=-1, keepdims=True) + EPS)
    return h * r, r


def _dot(a, b, dims=((1,), (0,)), precision=None):
    return lax.dot_general(a, b, (dims, ((), ())), preferred_element_type=F32, precision=precision)


_NT = ((1,), (1,))
_TN = ((0,), (0,))


def _rowwise(name, fn, n_rows, rows, consts, outs, accs=(), *, tm, nseg=1, seg_blocks=0):
    assert n_rows % tm == 0
    if nseg == 2:
        assert seg_blocks > 0
        seg = lambda i: jnp.where(i < seg_blocks, 0, 1)
    else:
        seg = lambda i: 0
    in_specs, args, lacking = [], [], []
    for r in rows:
        arr, width, cb, off = r if isinstance(r, tuple) else (r, r.shape[1], 0, 0)
        in_specs.append(pl.BlockSpec((tm, width), lambda i, cb=cb, off=off: (jnp.maximum(i + off, 0), cb)))
        args.append(arr)
        lacking.append(-off if off < 0 else 0)
    for kind, arr in consts:
        if kind == "seg":
            assert arr.shape[0] == nseg and arr.shape[1] == 1, arr.shape
            in_specs.append(pl.BlockSpec((None, 1, arr.shape[2]), lambda i: (seg(i), 0, 0)))
        else:
            in_specs.append(pl.BlockSpec(arr.shape, lambda i: (0, 0)))
        args.append(arr)
    out_shape = [jax.ShapeDtypeStruct((n_rows, w), dt) for w, dt in outs]
    out_specs = [pl.BlockSpec((tm, w), lambda i: (i, 0)) for w, _ in outs]
    out_shape += [jax.ShapeDtypeStruct((nseg, 1, w), F32) for w in accs]
    out_specs += [pl.BlockSpec((None, 1, w), lambda i: (seg(i), 0, 0)) for w in accs]
    n_in, n_out, n_acc = len(args), len(outs), len(accs)

    def kern(*refs):
        i = pl.program_id(0)
        ins = [r[...] for r in refs[:n_in]]
        for k, lack in enumerate(lacking):
            if lack:
                ins[k] = jnp.where(i >= lack, ins[k], jnp.zeros_like(ins[k]))
        res, terms = fn(*ins)
        for ref, v in zip(refs[n_in:n_in + n_out], res):
            ref[...] = v.astype(ref.dtype)
        if n_acc:
            sums = [_sum0(v) for v in terms]
            first = (i == 0) | (i == seg_blocks) if nseg == 2 else (i == 0)
            acc_refs = refs[n_in + n_out:]

            @pl.when(first)
            def _():
                for ref, v in zip(acc_refs, sums):
                    ref[...] = v

            @pl.when(jnp.logical_not(first))
            def _():
                for ref, v in zip(acc_refs, sums):
                    ref[...] += v

    res = pl.pallas_call(
        kern, name=name, grid=(n_rows // tm,), in_specs=in_specs, out_specs=out_specs, out_shape=out_shape,
        compiler_params=_params("arbitrary"),
    )(*args)
    return res


def _pre_fwd_fn(h, g, shift, scale):
    hh, _ = _rms(h)
    return (hh * g * (1.0 + scale) + shift,), ()


def _pre_bwd_fn(du, h, dres, g, scale):
    hh, r = _rms(h)
    n = hh * g
    dn = du * (1.0 + scale)
    dhh = dn * g
    dh = dres + r * (dhh - hh * jnp.mean(dhh * hh, axis=-1, keepdims=True))
    return (dh,), (du, du * n, dn * hh)


def _post_fwd_fn(weight, h, y, g, gate):
    yh, _ = _rms(y)
    return (h + weight * gate * (yh * g),), ()


def _out_post_fn(weight, y, h, g, gate):
    return (y,) + _post_fwd_fn(weight, h, y, g, gate)[0], ()


def _post_bwd_fn(weight, dh, y, g, gate):
    yh, r = _rms(y)
    dr = dh * weight
    dyh = dr * gate * g
    dy = r * (dyh - yh * jnp.mean(dyh * yh, axis=-1, keepdims=True))
    return (dy,), (dr * yh * g, dr * gate * yh)


def _glu_bwd_fn(ds, a, b):
    a = a.astype(F32)
    b = b.astype(F32)
    sg = _sig(a)
    da = ds * b * (sg * (1.0 + a * (1.0 - sg)))
    db = ds * (a * sg)
    return (jnp.concatenate([da, db], axis=1),), ()


def _loss_fn(y, t):
    diff = y - t
    return (diff * (1.0 / D_MODEL),), (diff * diff,)


def _ssd_y(yf, yb, xs, z, dvec):
    y = yf + yb + dvec * xs
    return y, y * _silu(z)


def _ssdgate_fwd_fn(yf, yb, xs, z, dvec, ng):
    _, yg = _ssd_y(yf, yb, xs, z, dvec)
    parts = []
    for g in range(SSD_GROUPS):
        sl = slice(g * 256, (g + 1) * 256)
        parts.append(_rms(yg[:, sl])[0])
    return (jnp.concatenate(parts, axis=1) * ng,), ()


def _ssdgate_bwd_fn(dyn, yf, yb, xs, z, dvec, ng):
    y, yg = _ssd_y(yf, yb, xs, z, dvec)
    dyg_parts, ygh_parts = [], []
    for g in range(SSD_GROUPS):
        sl = slice(g * 256, (g + 1) * 256)
        ygh, r = _rms(yg[:, sl])
        d = dyn[:, sl] * ng[:, sl]
        dyg_parts.append(r * (d - ygh * jnp.mean(d * ygh, axis=-1, keepdims=True)))
        ygh_parts.append(ygh)
    dyg = jnp.concatenate(dyg_parts, axis=1)
    ygh = jnp.concatenate(ygh_parts, axis=1)
    dy = dyg * _silu(z)
    dz = dyg * y * _dsilu(z)
    return (dy, dz), (dyn * ygh, dy * xs)


def _ln_stats(v):
    mu = jnp.mean(v, axis=-1, keepdims=True)
    vc = v - mu
    r = lax.rsqrt(jnp.mean(vc * vc, axis=-1, keepdims=True) + EPS)
    return vc * r, r


def _gm_act_fwd_fn(p, vg, vb):
    gu = _gelu(p[:, :GM_INNER])
    gvh, _ = _ln_stats(_gelu(p[:, GM_INNER:]))
    return (gu, gvh * vg + vb), ()


def _gm_act_bwd_fn(p, dgu, dgvn, vg):
    pu = p[:, :GM_INNER]
    pv = p[:, GM_INNER:]
    gv, dgelu_v = _gelu_and_grad(pv)
    gvh, r = _ln_stats(gv)
    dgvh = dgvn * vg
    dgv = r * (dgvh - jnp.mean(dgvh, axis=-1, keepdims=True) - gvh * jnp.mean(dgvh * gvh, axis=-1, keepdims=True))
    dp = jnp.concatenate([dgu * _dgelu(pu), dgv * dgelu_v], axis=1)
    return (dp,), (dgvn * gvh, dgvn)


def _mm(a, b, *, out_dtype, name, tm=1088, tn=1024, tk=1408, add=None, rhs_t=False, n=None, b_off=(0, 0)):
    m, k = a.shape
    col_blocked = b.ndim == 3
    if col_blocked:
        assert not rhs_t and n is None and b.shape[1] == k
        n, tn = b.shape[0] * b.shape[2], b.shape[2]
    elif n is None:
        n, k2 = b.shape if rhs_t else b.shape[::-1]
        assert k == k2
    tm, tn, tk = _pick(m, tm), _pick(n, tn, 128), _pick(k, tk, 128)
    o0, o1 = b_off
    nk = k // tk
    dims = _NT if rhs_t else ((1,), (0,))

    def kern(*refs):
        a_ref, b_ref = refs[:2]
        add_ref = refs[2] if add is not None else None
        o_ref = refs[3] if add is not None else refs[2]

        def finish(r):
            if add is not None:
                r = r + add_ref[...]
            o_ref[...] = r.astype(o_ref.dtype)

        p = _dot(a_ref[...], b_ref[...], dims)
        if nk == 1:
            finish(p)
            return
        acc_ref = refs[-1]
        kk = pl.program_id(2)

        @pl.when(kk == 0)
        def _():
            acc_ref[...] = p

        @pl.when((kk > 0) & (kk < nk - 1))
        def _():
            acc_ref[...] += p

        @pl.when(kk == nk - 1)
        def _():
            finish(acc_ref[...] + p)

    if col_blocked:
        b_spec = pl.BlockSpec((None, tk, tn), lambda i, j, kk: (j, kk, 0))
    elif rhs_t:
        b_spec = pl.BlockSpec((tn, tk), lambda i, j, kk: (j + o0, kk + o1))
    else:
        b_spec = pl.BlockSpec((tk, tn), lambda i, j, kk: (kk + o0, j + o1))
    in_specs = [pl.BlockSpec((tm, tk), lambda i, j, kk: (i, kk)), b_spec]
    args = [a, b]
    if add is not None:
        in_specs.append(pl.BlockSpec((tm, tn), lambda i, j, kk: (i, j)))
        args.append(add)
    return pl.pallas_call(
        kern, name=name, grid=(m // tm, n // tn, nk), in_specs=in_specs,
        out_specs=pl.BlockSpec((tm, tn), lambda i, j, kk: (i, j)),
        out_shape=jax.ShapeDtypeStruct((m, n), out_dtype),
        scratch_shapes=[pltpu.VMEM((tm, tn), F32)] if nk > 1 else [],
        compiler_params=_params("parallel", "parallel", "arbitrary"),
    )(*args)


def _mm_rows(a, b, fn, rows, consts, outs, accs=(), *, name, tm=544, tk=1408, rhs_t=False, n_ctx=0):
    halves = a.ndim == 3
    m, k = (a.shape[1], 2 * a.shape[2]) if halves else a.shape
    col_blocked = b.ndim == 3
    kb, nb = 1, None
    if col_blocked:
        assert rhs_t and b.shape[0] * b.shape[2] == k
        n, nb = b.shape[1], b.shape[2]
        kb = max(1, tk // nb)
        assert b.shape[0] % kb == 0
        tk = kb * nb
    else:
        n = b.shape[0] if rhs_t else b.shape[1]
    tm, tk = _pick(m, tm), _pick(k, tk, 128)
    nk = k // tk
    if halves:
        hb = k // 2 // tk
        a_spec = pl.BlockSpec((None, tm, tk), lambda i, kk: (kk // hb, i, kk % hb))
    else:
        a_spec = pl.BlockSpec((tm, tk), lambda i, kk: (i, kk))
    dims = _NT if rhs_t else ((1,), (0,))
    n_rows, n_const, n_out, n_acc = len(rows), len(consts), len(outs), len(accs)

    def kern(*refs):
        a_ref, b_ref = refs[:2]
        row_refs = refs[2:2 + n_rows]
        const_refs = refs[2 + n_rows:2 + n_rows + n_const]
        out_refs = refs[2 + n_rows + n_const:2 + n_rows + n_const + n_out]
        acc_refs = refs[2 + n_rows + n_const + n_out:2 + n_rows + n_const + n_out + n_acc]
        i, kk = pl.program_id(0), pl.program_id(1)

        def finish(p, rs=slice(None), r0=0):
            nr = p.shape[0]
            is_ctx = (i * tm + r0 + lax.broadcasted_iota(jnp.int32, (nr, 1), 0)) < n_ctx
            cvals = []
            for (kind, arr), ref in zip(consts, const_refs):
                if kind == "seg":
                    cvals.append(jnp.where(is_ctx, ref[0], ref[1]) if arr.shape[0] == 2 else ref[0])
                else:
                    cvals.append(ref[...])
            res, terms = fn(p, *[r[rs, :] for r in row_refs], *cvals)
            for ref, v in zip(out_refs, res):
                ref[rs, :] = v.astype(ref.dtype)
            for ref, v in zip(acc_refs, terms):
                s_all = _sum0(v)
                s_ctx = _sum0(jnp.where(is_ctx, v, 0.0)) if n_ctx else jnp.zeros_like(s_all)
                both = jnp.concatenate([s_ctx, s_all - s_ctx], axis=0)[:, None, :]

                @pl.when(i == 0)
                def _():
                    ref[...] = both

                @pl.when(i > 0)
                def _():
                    ref[...] += both

        if nk == 1 and n_acc == 0:
            nsub = 2 if tm % 32 == 0 else 1
            sub = tm // nsub
            for r in range(nsub):
                rs = slice(r * sub, (r + 1) * sub)
                finish(_dot(a_ref[rs, :], b_ref[...], dims), rs, r * sub)
            return
        if col_blocked:
            p = sum(_dot(a_ref[:, c * nb:(c + 1) * nb], b_ref[c], dims) for c in range(kb))
        else:
            p = _dot(a_ref[...], b_ref[...], dims)
        if nk == 1:
            finish(p)
            return
        scr = refs[-1]

        @pl.when(kk == 0)
        def _():
            scr[...] = p

        @pl.when((kk > 0) & (kk < nk - 1))
        def _():
            scr[...] += p

        @pl.when(kk == nk - 1)
        def _():
            finish(scr[...] + p)

    if col_blocked:
        b_spec = pl.BlockSpec((kb, n, nb), lambda i, kk: (kk, 0, 0))
    elif rhs_t:
        b_spec = pl.BlockSpec((n, tk), lambda i, kk: (0, kk))
    else:
        b_spec = pl.BlockSpec((tk, n), lambda i, kk: (kk, 0))
    in_specs = [a_spec, b_spec]
    in_specs += [pl.BlockSpec((tm, r.shape[1]), lambda i, kk: (i, 0)) for r in rows]
    for kind, arr in consts:
        in_specs.append(pl.BlockSpec(arr.shape, (lambda i, kk: (0, 0, 0)) if kind == "seg" else (lambda i, kk: (0, 0))))
    out_shape = [jax.ShapeDtypeStruct((m, w), dt) for w, dt in outs]
    out_specs = [pl.BlockSpec((tm, w), lambda i, kk: (i, 0)) for w, _ in outs]
    out_shape += [jax.ShapeDtypeStruct((2, 1, w), F32) for w in accs]
    out_specs += [pl.BlockSpec((2, 1, w), lambda i, kk: (0, 0, 0)) for w in accs]
    return pl.pallas_call(
        kern, name=name, grid=(m // tm, nk), in_specs=in_specs, out_specs=out_specs, out_shape=out_shape,
        scratch_shapes=[pltpu.VMEM((tm, n), F32)] if nk > 1 else [],
        compiler_params=_params("arbitrary", "arbitrary"),
    )(a, b, *rows, *[arr for _, arr in consts])


def _mm_glu(u, win_t, *, name, tm=1088, tn=1408):
    m, k = u.shape
    n = win_t.shape[0] // 2
    tm, tn = _pick(m, tm), _pick(n, tn, 128)
    nj = n // tn

    nsub = 2 if tm % 32 == 0 else 1
    sub = tm // nsub

    def kern(u_ref, wa_ref, wb_ref, s_ref, a_ref, b_ref):
        for r in range(nsub):
            rows = slice(r * sub, (r + 1) * sub)
            uu = u_ref[rows, :]
            a = _dot(uu, wa_ref[...], _NT)
            b = _dot(uu, wb_ref[...], _NT)
            s_ref[rows, :] = (_silu(a) * b).astype(BF16)
            a_ref[rows, :] = a.astype(BF16)
            b_ref[rows, :] = b.astype(BF16)

    ospec = pl.BlockSpec((tm, tn), lambda i, j: (i, j))
    return pl.pallas_call(
        kern, name=name, grid=(m // tm, nj),
        in_specs=[pl.BlockSpec((tm, k), lambda i, j: (i, 0)), pl.BlockSpec((tn, k), lambda i, j: (j, 0)),
                  pl.BlockSpec((tn, k), lambda i, j: (nj + j, 0))],
        out_specs=[ospec, ospec, ospec],
        out_shape=[jax.ShapeDtypeStruct((m, n), BF16)] * 3,
        compiler_params=_params("parallel", "parallel"),
    )(u, win_t, win_t)


def _mm_glu_bwd(dy, wout, a, b, *, name, tm=1088, tn=1408):
    m, k = dy.shape
    f = wout.shape[0]
    tm, tn = _pick(m, tm), _pick(f, tn, 128)
    nsub = 2 if tm % 32 == 0 else 1
    sub = tm // nsub

    def kern(dy_ref, w_ref, a_ref, b_ref, o_ref):
        for r in range(nsub):
            rs = slice(r * sub, (r + 1) * sub)
            ds = _dot(dy_ref[rs, :], w_ref[...], _NT)
            (dp,), _ = _glu_bwd_fn(ds, a_ref[rs, :], b_ref[rs, :])
            o_ref[0, rs, :] = dp[:, :tn].astype(BF16)
            o_ref[1, rs, :] = dp[:, tn:].astype(BF16)

    tile = pl.BlockSpec((tm, tn), lambda i, j: (i, j))
    return pl.pallas_call(
        kern, name=name, grid=(m // tm, f // tn),
        in_specs=[pl.BlockSpec((tm, k), lambda i, j: (i, 0)), pl.BlockSpec((tn, k), lambda i, j: (j, 0)), tile, tile],
        out_specs=pl.BlockSpec((2, tm, tn), lambda i, j: (0, i, j)),
        out_shape=jax.ShapeDtypeStruct((2, m, f), BF16),
        compiler_params=_params("parallel", "parallel"),
    )(dy, wout, a, b)


def _mm_tn(a, b, *, name, tm=1024, tn=1024, tk=2176, col_blocks=None, stack=None):
    extra, extra_specs, aliases = [], [], {}
    halves = a.ndim == 3
    t, m = (a.shape[1], 2 * a.shape[2]) if halves else a.shape
    t2, n = b.shape
    assert t == t2
    tm, tn, tk = _pick(m, tm, 128), _pick(n, tn, 128), _pick(t, tk)
    nk = t // tk
    if halves:
        hb = m // 2 // tm
        a_spec = pl.BlockSpec((None, tk, tm), lambda i, j, kk: (i // hb, kk, i % hb))
    else:
        a_spec = pl.BlockSpec((tk, tm), lambda i, j, kk: (kk, i))
    if col_blocks is None:
        def kern(a_ref, b_ref, o_ref):
            kk = pl.program_id(2)

            @pl.when(kk == 0)
            def _():
                o_ref[...] = jnp.zeros_like(o_ref)

            o_ref[...] += _dot(a_ref[...], b_ref[...], _TN)

        out_spec = pl.BlockSpec((tm, tn), lambda i, j, kk: (i, j))
        out_shape = jax.ShapeDtypeStruct((m, n), F32)
        scratch = []
    else:
        wb = n // col_blocks
        per = tn // wb
        assert tn % wb == 0 and wb % 8 == 0

        def kern(a_ref, b_ref, *rest):
            o_ref, acc_ref = rest[-2:]
            kk = pl.program_id(2)
            p = _dot(a_ref[...], b_ref[...], _TN)

            @pl.when(kk == 0)
            def _():
                acc_ref[...] = p

            @pl.when((kk > 0) & (kk < nk - 1))
            def _():
                acc_ref[...] += p

            @pl.when(kk == nk - 1)
            def _():
                r = acc_ref[...] + p if nk > 1 else p
                for c in range(per):
                    o_ref[c] = r[:, c * wb:(c + 1) * wb].astype(BF16)

        rows_total, row0, into = stack if stack is not None else (m, 0, None)
        assert row0 % tm == 0
        out_spec = pl.BlockSpec((per, tm, wb), lambda i, j, kk: (j, i + row0 // tm, 0))
        out_shape = jax.ShapeDtypeStruct((col_blocks, rows_total, wb), BF16)
        scratch = [pltpu.VMEM((tm, tn), F32)]
        if into is not None:
            extra, extra_specs, aliases = [into], [pl.BlockSpec(memory_space=pl.ANY)], {2: 0}

    return pl.pallas_call(
        kern, name=name, grid=(m // tm, n // tn, nk),
        in_specs=[a_spec, pl.BlockSpec((tk, tn), lambda i, j, kk: (kk, j))] + extra_specs,
        out_specs=out_spec, out_shape=out_shape, scratch_shapes=scratch, input_output_aliases=aliases,
        compiler_params=_params("parallel", "parallel", "arbitrary"),
    )(a, b, *extra)


def _mm_f32(a, b, *, name, silu_a=False, bias=None):
    m, k = a.shape
    n = b.shape[1]

    def kern(*refs):
        if bias is None:
            a_ref, b_ref, o_ref = refs
        else:
            a_ref, b_ref, bias_ref, o_ref = refs
        av = a_ref[...]
        if silu_a:
            av = _silu(av)
        r = jnp.dot(av, b_ref[...], preferred_element_type=F32, precision=HI)
        if bias is not None:
            r = r + bias_ref[...]
        o_ref[...] = r

    args = [a, b] + ([] if bias is None else [bias])
    return pl.pallas_call(kern, name=name, out_shape=jax.ShapeDtypeStruct((m, n), F32),
                          compiler_params=pltpu.CompilerParams(vmem_limit_bytes=VMEM_LIMIT_BYTES))(*args)


CONV_WIN = 32


def _conv_windows(n, n_ctx):
    assert n_ctx % CONV_WIN == 0 and n_ctx >= CONV_WIN and n - n_ctx >= CONV_WIN
    return (0, n_ctx - CONV_WIN // 2, n - CONV_WIN)


def _tap_outside(r0, s, n, n_ctx):
    t = r0 + lax.broadcasted_iota(jnp.int32, (CONV_WIN, 1), 0)
    lo = jnp.where(t < n_ctx, 0, n_ctx)
    hi = jnp.where(t < n_ctx, n_ctx, n)
    return jnp.where((t + s >= lo) & (t + s < hi), 0.0, 1.0)


def _rolled(v, s):
    return v if s == 0 else pltpu.roll(v, (-s) % v.shape[0], 0)


def _conv_fwd(xp, w8, b, *, n_ctx, name, cb=256):
    n, c = xp.shape
    half = SSD_CONV // 2

    def kern(x_ref, w_ref, b_ref, cpre_ref, act_ref):
        x = x_ref[...]
        acc = jnp.zeros_like(x) + b_ref[...]
        rolled = {}
        for k in range(SSD_CONV):
            rolled[k] = _rolled(x, k - half)
            acc = acc + rolled[k] * w_ref[k:k + 1, :]
        cpre_ref[...] = acc
        act_ref[...] = _silu(acc)
        for r0 in _conv_windows(n, n_ctx):
            rows = slice(r0, r0 + CONV_WIN)
            fix = acc[rows]
            for k in range(SSD_CONV):
                if k != half:
                    fix = fix - rolled[k][rows] * w_ref[k:k + 1, :] * _tap_outside(r0, k - half, n, n_ctx)
            cpre_ref[rows, :] = fix
            act_ref[rows, :] = _silu(fix)

    spec = pl.BlockSpec((n, cb), lambda j: (0, j))
    return pl.pallas_call(
        kern, name=name, grid=(c // cb,),
        in_specs=[spec, pl.BlockSpec((8, cb), lambda j: (0, j)), pl.BlockSpec((1, cb), lambda j: (0, j))],
        out_specs=[spec, spec], out_shape=[jax.ShapeDtypeStruct((n, c), F32)] * 2,
        compiler_params=_params("parallel"),
    )(xp, w8, b)


def _conv_bwd(d1, d2, cpre, xp, w8, *, n_ctx, name, cb=128):
    n, c = xp.shape
    half = SSD_CONV // 2

    def kern(d1_ref, d2_ref, cpre_ref, x_ref, w_ref, dx_ref, dw_ref, db_ref):
        g = (d1_ref[...] + d2_ref[...]) * _dsilu(cpre_ref[...])
        x = x_ref[...]
        dx = jnp.zeros_like(g)
        dw_ref[...] = jnp.zeros_like(dw_ref)
        g_rolled = {}
        for k in range(SSD_CONV):
            s = k - half
            g_rolled[k] = _rolled(g, -s)
            dx = dx + g_rolled[k] * w_ref[k:k + 1, :]
            xr = _rolled(x, s)
            dw = _sum0(g * xr)
            if s != 0:
                for r0 in _conv_windows(n, n_ctx):
                    rows = slice(r0, r0 + CONV_WIN)
                    dw = dw - _sum0(g[rows] * xr[rows] * _tap_outside(r0, s, n, n_ctx))
            dw_ref[k:k + 1, :] = dw
        dx_ref[...] = dx.astype(BF16)
        for r0 in _conv_windows(n, n_ctx):
            rows = slice(r0, r0 + CONV_WIN)
            fix = dx[rows]
            for k in range(SSD_CONV):
                if k != half:
                    fix = fix - g_rolled[k][rows] * w_ref[k:k + 1, :] * _tap_outside(r0, half - k, n, n_ctx)
            dx_ref[rows, :] = fix.astype(BF16)
        db_ref[...] = _sum0(g)

    spec = pl.BlockSpec((n, cb), lambda j: (0, j))
    return pl.pallas_call(
        kern, name=name, grid=(c // cb,),
        in_specs=[spec, spec, spec, spec, pl.BlockSpec((8, cb), lambda j: (0, j))],
        out_specs=[spec, pl.BlockSpec((8, cb), lambda j: (0, j)), pl.BlockSpec((1, cb), lambda j: (0, j))],
        out_shape=[jax.ShapeDtypeStruct((n, c), BF16), jax.ShapeDtypeStruct((8, c), F32),
                   jax.ShapeDtypeStruct((1, c), F32)],
        compiler_params=_params("parallel"),
    )(d1, d2, cpre, xp, w8)


def _chunk_of(s, nc, n_ctx_chunks, rev):
    if not rev:
        return s
    return jnp.where(s < n_ctx_chunks, n_ctx_chunks - 1 - s, nc - 1 - (s - n_ctx_chunks))


def _scan_common(dt_raw, dtT_raw, bias_r, bias_c, alog_r, alog_c, rev):
    ii = lax.broadcasted_iota(jnp.int32, (CHUNK, CHUNK), 0)
    jj = lax.broadcasted_iota(jnp.int32, (CHUNK, CHUNK), 1)
    tri = (jj >= ii) if rev else (jj <= ii)
    tri_t = (ii >= jj) if rev else (ii <= jj)
    a_r = -jnp.exp(alog_r)
    a_c = -jnp.exp(alog_c)
    dt = _softplus(dt_raw + bias_r)
    dt_t = _softplus(dtT_raw + bias_c)
    al = dt * a_r
    acum = _dot(tri.astype(F32), al, precision=HI)
    acum_t = _dot(dt_t * a_c, tri_t.astype(F32), precision=HI)
    atot = _sum0(al)
    return tri, tri_t, a_r, dt, acum, acum_t, atot


def _head_spread():
    return jnp.repeat(jnp.eye(SSD_HEADS, dtype=BF16), SSD_HEAD_DIM, axis=1)


def _dot_sel(v, sel):
    hi = v.astype(BF16)
    lo = (v - hi.astype(F32)).astype(BF16)
    return _dot(hi, sel) + _dot(lo, sel)


def _ssd_scan_fwd(xbc, dt_raw, dtT_raw, bias_r, bias_c, alog_r, alog_c, *, rev, n_ctx_chunks, name):
    n = xbc.shape[0]
    nc = n // CHUNK
    cidx = functools.partial(_chunk_of, nc=nc, n_ctx_chunks=n_ctx_chunks, rev=rev)

    def kern(xs_ref, b_ref, c_ref, dt_ref, dtT_ref, br_ref, bc_ref, ar_ref, ac_ref, e_ref, y_ref, hs_ref, h_scr):
        @pl.when(pl.program_id(0) == 0)
        def _():
            h_scr[...] = jnp.zeros_like(h_scr)

        tri, _, _, dt, acum, acum_t, atot = _scan_common(
            dt_ref[...], dtT_ref[...], br_ref[...], bc_ref[...], ar_ref[...], ac_ref[...], rev)
        etot = jnp.exp(atot)
        spread = lambda v: _dot_sel(v, e_ref[...])
        xdt_all = xs_ref[...] * spread(dt)
        eax = spread(jnp.exp(acum))
        xdw_all = xdt_all * spread(jnp.exp(atot - acum))
        hs_ref[...] = h_scr[...]
        for g in range(SSD_GROUPS):
            gs = slice(g * 256, (g + 1) * 256)
            bg = b_ref[:, g * SSD_STATE:(g + 1) * SSD_STATE].astype(BF16)
            cg = c_ref[:, g * SSD_STATE:(g + 1) * SSD_STATE].astype(BF16)
            cb = _dot(cg, bg, _NT)
            h4 = h_scr[gs, :]
            ys = []
            for k in range(SSD_HPG):
                h = g * SSD_HPG + k
                lmat = jnp.exp(jnp.where(tri, acum[:, h:h + 1] - acum_t[h:h + 1, :], NEG_BIG))
                xdt_h = xdt_all[:, h * SSD_HEAD_DIM:(h + 1) * SSD_HEAD_DIM].astype(BF16)
                ys.append(_dot((cb * lmat).astype(BF16), xdt_h))
            y_ref[:, gs] = jnp.concatenate(ys, axis=1) + _dot(cg, h4.astype(BF16), _NT) * eax[:, gs]
            s4 = _dot(xdw_all[:, gs].astype(BF16), bg, _TN)
            for k in range(SSD_HPG):
                h = g * SSD_HPG + k
                rs = slice(h * SSD_HEAD_DIM, (h + 1) * SSD_HEAD_DIM)
                h_scr[rs, :] = h4[k * SSD_HEAD_DIM:(k + 1) * SSD_HEAD_DIM] * etot[:, h:h + 1] + \
                    s4[k * SSD_HEAD_DIM:(k + 1) * SSD_HEAD_DIM]

    nh = SSD_HEADS
    small = lambda shape: pl.BlockSpec(shape, lambda s: (0, 0))
    return pl.pallas_call(
        kern, name=name, grid=(nc,),
        in_specs=[pl.BlockSpec((CHUNK, SSD_INNER), lambda s: (cidx(s), 0)),
                  pl.BlockSpec((CHUNK, 1024), lambda s: (cidx(s), 2)),
                  pl.BlockSpec((CHUNK, 1024), lambda s: (cidx(s), 3)),
                  pl.BlockSpec((CHUNK, nh), lambda s: (cidx(s), 0)),
                  pl.BlockSpec((nh, CHUNK), lambda s: (0, cidx(s))),
                  small((1, nh)), small((nh, 1)), small((1, nh)), small((nh, 1)), small((nh, SSD_INNER))],
        out_specs=[pl.BlockSpec((CHUNK, SSD_INNER), lambda s: (cidx(s), 0)),
                   pl.BlockSpec((None, SSD_INNER, SSD_STATE), lambda s: (s, 0, 0))],
        out_shape=[jax.ShapeDtypeStruct((n, SSD_INNER), F32),
                   jax.ShapeDtypeStruct((nc, SSD_INNER, SSD_STATE), F32)],
        scratch_shapes=[pltpu.VMEM((SSD_INNER, SSD_STATE), F32)],
        compiler_params=_params("arbitrary"),
    )(xbc, xbc, xbc, dt_raw, dtT_raw, bias_r, bias_c, alog_r, alog_c, _head_spread())


def _ssd_scan_bwd(dy, xbc, hs, dt_raw, dtT_raw, bias_r, bias_c, alog_r, alog_c, dvec, *, rev, n_ctx_chunks,
                  direct, name):
    n = xbc.shape[0]
    nc = n // CHUNK
    nh = SSD_HEADS
    step_of = lambda r: nc - 1 - r
    cidx = lambda r: _chunk_of(step_of(r), nc, n_ctx_chunks, rev)

    def kern(dy_ref, xs_ref, b_ref, c_ref, hs_ref, dt_ref, dtT_ref, br_ref, bc_ref, ar_ref, ac_ref, dv_ref,
             e_ref, et_ref, dx_ref, ddt_ref, dal_ref, dbias_ref, dh_scr):
        @pl.when(pl.program_id(0) == 0)
        def _():
            dh_scr[...] = jnp.zeros_like(dh_scr)
            dal_ref[...] = jnp.zeros_like(dal_ref)
            dbias_ref[...] = jnp.zeros_like(dbias_ref)

        tri, tri_t, a_r, dt, acum, acum_t, atot = _scan_common(
            dt_ref[...], dtT_ref[...], br_ref[...], bc_ref[...], ar_ref[...], ac_ref[...], rev)
        etot = jnp.exp(atot)
        spread = lambda v: _dot_sel(v, e_ref[...])
        gather = lambda v: _dot_sel(v, et_ref[...])
        xs_all = xs_ref[...]
        dy_all = dy_ref[...]
        dtx = spread(dt)
        eax = spread(jnp.exp(acum))
        decx = spread(jnp.exp(atot - acum))
        xdt_all = xs_all * dtx
        xdw_all = xdt_all * decx
        dyo_all = dy_all * eax
        lane = lax.broadcasted_iota(jnp.int32, (CHUNK, nh), 1)
        lane1 = lax.broadcasted_iota(jnp.int32, (1, nh), 1)
        sub = lax.broadcasted_iota(jnp.int32, (nh, CHUNK), 0)
        g_rows = jnp.zeros((CHUNK, nh), F32)
        g_cols = jnp.zeros((nh, CHUNK), F32)
        dtot = jnp.zeros((1, nh), F32)
        q_col, q_e, q_dt = [], [], []
        for g in range(SSD_GROUPS):
            gs = slice(g * 256, (g + 1) * 256)
            bg = b_ref[:, g * SSD_STATE:(g + 1) * SSD_STATE].astype(BF16)
            cg = c_ref[:, g * SSD_STATE:(g + 1) * SSD_STATE].astype(BF16)
            cb = _dot(cg, bg, _NT)
            hs4 = hs_ref[gs, :]
            dh4 = dh_scr[gs, :]
            hs4_bf = hs4.astype(BF16)
            dh4_bf = dh4.astype(BF16)
            dy4 = dy_all[:, gs]
            dy4_bf = dy4.astype(BF16)
            xdt4_bf = xdt_all[:, gs].astype(BF16)
            xdw4 = xdw_all[:, gs]
            xdw4_bf = xdw4.astype(BF16)
            dyo4_bf = dyo_all[:, gs].astype(BF16)
            yoff4 = _dot(cg, hs4_bf, _NT) * eax[:, gs]
            dcg = _dot(dyo4_bf, hs4_bf)
            dh_new4 = _dot(dyo4_bf, cg, _TN)
            bdh4 = _dot(bg, dh4_bf, _NT)
            dbg = _dot(xdw4_bf, dh4_bf)
            e4 = xdw4 * bdh4
            q_col.append(dy4 * yoff4 - e4)
            q_e.append(e4)
            hsum = jnp.sum(dh4 * hs4, axis=1, keepdims=True)
            dcb = jnp.zeros((CHUNK, CHUNK), F32)
            dxdts = []
            for k in range(SSD_HPG):
                h = g * SSD_HPG + k
                ks = slice(k * SSD_HEAD_DIM, (k + 1) * SSD_HEAD_DIM)
                lmat = jnp.exp(jnp.where(tri, acum[:, h:h + 1] - acum_t[h:h + 1, :], NEG_BIG))
                mf = cb * lmat
                dm = _dot(dy4_bf[:, ks], xdt4_bf[:, ks], _NT)
                dcb = dcb + dm * lmat
                gmat = dm * mf
                g_rows = g_rows + jnp.where(lane == h, jnp.sum(gmat, axis=1, keepdims=True), 0.0)
                g_cols = g_cols + jnp.where(sub == h, _sum0(gmat), 0.0)
                dxdts.append(_dot(mf.astype(BF16), dy4_bf[:, ks], _TN))
                et = etot[:, h:h + 1]
                dtot = dtot + jnp.where(lane1 == h, _sum0(hsum[ks]) * et, 0.0)
                dh_scr[h * SSD_HEAD_DIM:(h + 1) * SSD_HEAD_DIM, :] = dh4[ks] * et + dh_new4[ks]
            dxdt4 = jnp.concatenate(dxdts, axis=1) + bdh4 * decx[:, gs]
            q_dt.append(dxdt4 * xs_all[:, gs])
            dx4 = dxdt4 * dtx[:, gs]
            if direct:
                dx4 = dx4 + dy4 * dv_ref[:, gs]
            dcb_bf = dcb.astype(BF16)
            dx_ref[:, gs] = dx4
            dx_ref[:, SSD_INNER + g * SSD_STATE:SSD_INNER + (g + 1) * SSD_STATE] = dbg + _dot(dcb_bf, cg, _TN)
            dx_ref[:, SSD_INNER + 1024 + g * SSD_STATE:SSD_INNER + 1024 + (g + 1) * SSD_STATE] = \
                dcg + _dot(dcb_bf, bg)
        e_heads = gather(jnp.concatenate(q_e, axis=1))
        dacum = gather(jnp.concatenate(q_col, axis=1)) + g_rows - g_cols.T
        dal = _dot(tri_t.astype(F32), dacum, precision=HI) + dtot + _sum0(e_heads)
        ddt = gather(jnp.concatenate(q_dt, axis=1)) + dal * a_r
        ddt_raw = ddt * _sig(dt_ref[...] + br_ref[...])
        ddt_ref[...] = ddt_raw
        dal_ref[...] += _sum0(dal * dt) * a_r
        dbias_ref[...] += _sum0(ddt_raw)

    small = lambda shape: pl.BlockSpec(shape, lambda r: (0, 0))
    return pl.pallas_call(
        kern, name=name, grid=(nc,),
        in_specs=[pl.BlockSpec((CHUNK, SSD_INNER), lambda r: (cidx(r), 0)),
                  pl.BlockSpec((CHUNK, SSD_INNER), lambda r: (cidx(r), 0)),
                  pl.BlockSpec((CHUNK, 1024), lambda r: (cidx(r), 2)),
                  pl.BlockSpec((CHUNK, 1024), lambda r: (cidx(r), 3)),
                  pl.BlockSpec((None, SSD_INNER, SSD_STATE), lambda r: (step_of(r), 0, 0)),
                  pl.BlockSpec((CHUNK, nh), lambda r: (cidx(r), 0)),
                  pl.BlockSpec((nh, CHUNK), lambda r: (0, cidx(r))),
                  small((1, nh)), small((nh, 1)), small((1, nh)), small((nh, 1)), small((1, SSD_INNER)),
                  small((nh, SSD_INNER)), small((SSD_INNER, nh))],
        out_specs=[pl.BlockSpec((CHUNK, SSD_CONV_DIM), lambda r: (cidx(r), 0)),
                   pl.BlockSpec((CHUNK, nh), lambda r: (cidx(r), 0)),
                   small((1, nh)), small((1, nh))],
        out_shape=[jax.ShapeDtypeStruct((n, SSD_CONV_DIM), F32), jax.ShapeDtypeStruct((n, nh), F32),
                   jax.ShapeDtypeStruct((1, nh), F32), jax.ShapeDtypeStruct((1, nh), F32)],
        scratch_shapes=[pltpu.VMEM((SSD_INNER, SSD_STATE), F32)],
        compiler_params=_params("arbitrary"),
    )(dy, xbc, xbc, xbc, hs, dt_raw, dtT_raw, bias_r, bias_c, alog_r, alog_c, dvec, _head_spread(),
      _head_spread().T)


def _gm_spatial_fwd(gu, gvn, ws, bst, *, name):
    n = gu.shape[0]

    def kern(gu_ref, gv_ref, ws_ref, bs_ref, o_ref):
        for g in range(GM_GROUPS):
            sl = slice(g * GM_GROUP_DIM, (g + 1) * GM_GROUP_DIM)
            s = _dot(ws_ref[g], gv_ref[:, sl]) + bs_ref[:, g:g + 1]
            o_ref[:, sl] = (gu_ref[:, sl] * s).astype(BF16)

    spec = pl.BlockSpec((CHUNK, GM_INNER), lambda i: (i, 0))
    return pl.pallas_call(
        kern, name=name, grid=(n // CHUNK,),
        in_specs=[spec, spec, pl.BlockSpec(ws.shape, lambda i: (0, 0, 0)), pl.BlockSpec(bst.shape, lambda i: (0, 0))],
        out_specs=spec, out_shape=jax.ShapeDtypeStruct((n, GM_INNER), BF16),
        compiler_params=_params("parallel"),
    )(gu, gvn, ws, bst)


def _gm_spatial_bwd(dt, gu, gvn, ws, wst, bst, *, name):
    n = gu.shape[0]

    def kern(dt_ref, gu_ref, gv_ref, ws_ref, wst_ref, bs_ref, dgu_ref, dgv_ref, dws_ref, dbs_ref):
        @pl.when(pl.program_id(0) == 0)
        def _():
            dws_ref[...] = jnp.zeros_like(dws_ref)
            dbs_ref[...] = jnp.zeros_like(dbs_ref)

        lane = lax.broadcasted_iota(jnp.int32, (CHUNK, GM_GROUPS), 1)
        dbs = jnp.zeros((CHUNK, GM_GROUPS), F32)
        for g in range(GM_GROUPS):
            sl = slice(g * GM_GROUP_DIM, (g + 1) * GM_GROUP_DIM)
            gv = gv_ref[:, sl]
            s = _dot(ws_ref[g], gv) + bs_ref[:, g:g + 1]
            d = dt_ref[:, sl]
            dgu_ref[:, sl] = d * s
            ds = d * gu_ref[:, sl]
            ds_bf = ds.astype(BF16)
            dws_ref[g] += _dot(ds_bf, gv, _NT)
            dgv_ref[:, sl] = _dot(wst_ref[g], ds_bf)
            dbs = dbs + jnp.where(lane == g, jnp.sum(ds, axis=1, keepdims=True), 0.0)
        dbs_ref[...] += dbs

    spec = pl.BlockSpec((CHUNK, GM_INNER), lambda i: (i, 0))
    wspec = pl.BlockSpec(ws.shape, lambda i: (0, 0, 0))
    bspec = pl.BlockSpec(bst.shape, lambda i: (0, 0))
    return pl.pallas_call(
        kern, name=name, grid=(n // CHUNK,),
        in_specs=[spec, spec, spec, wspec, wspec, bspec],
        out_specs=[spec, spec, wspec, bspec],
        out_shape=[jax.ShapeDtypeStruct((n, GM_INNER), F32), jax.ShapeDtypeStruct((n, GM_INNER), F32),
                   jax.ShapeDtypeStruct(ws.shape, F32), jax.ShapeDtypeStruct(bst.shape, F32)],
        compiler_params=_params("arbitrary"),
    )(dt, gu, gvn, ws, wst, bst)


def _adamw(parts, w, m, v, *, name, tm=256, sel=(), into=None):
    ns, r, wd = parts.shape
    tm = _pick(r, tm, 8)
    tc = wd
    if tm < 64 and wd % 256 == 0:
        tm, tc = r, 256
    lead = len(sel)
    assert w.shape[lead:] == (r, wd) and lead == w.ndim - 2

    def kern(*refs):
        p_ref, w_ref, m_ref, v_ref = refs[:4]
        g_ref, d_ref, nm_ref, nv_ref = refs[-4:]
        g = p_ref[0].astype(F32)
        for s in range(1, ns):
            g = g + p_ref[s].astype(F32)
        m2 = ADAM_B1 * m_ref[...] + (1.0 - ADAM_B1) * g
        v2 = ADAM_B2 * v_ref[...] + (1.0 - ADAM_B2) * (g * g)
        m_hat = m2 / (1.0 - ADAM_B1 ** ADAM_STEP)
        v_hat = v2 / (1.0 - ADAM_B2 ** ADAM_STEP)
        g_ref[...] = g
        d_ref[...] = -ADAM_LR * (m_hat / (jnp.sqrt(v_hat) + ADAM_EPS) + ADAM_WD * w_ref[...])
        nm_ref[...] = m2
        nv_ref[...] = v2

    spec = pl.BlockSpec((None,) * lead + (tm, tc), lambda i, j: tuple(sel) + (i, j))
    extra, aliases = [], {}
    if into is not None:
        extra = list(into)
        aliases = {4 + k: k for k in range(4)}
    return pl.pallas_call(
        kern, name=name, grid=(r // tm, wd // tc),
        in_specs=[pl.BlockSpec((ns, tm, tc), lambda i, j: (0, i, j)), spec, spec, spec] +
                 [pl.BlockSpec(memory_space=pl.ANY)] * len(extra),
        out_specs=[spec] * 4, out_shape=[jax.ShapeDtypeStruct(w.shape, F32)] * 4,
        input_output_aliases=aliases,
        compiler_params=_params("parallel", "parallel"),
    )(parts, w, m, v, *extra)


def _sum_slots(parts, *, name, scale_by=None):
    ns, r, wd = parts.shape

    def kern(*refs):
        p_ref, o_ref = refs[0], refs[-1]
        g = p_ref[0]
        for s in range(1, ns):
            g = g + p_ref[s]
        if scale_by is not None:
            g = g * _dsilu(refs[1][...])
        o_ref[...] = g

    args = [parts] + ([] if scale_by is None else [scale_by])
    return pl.pallas_call(kern, name=name, out_shape=jax.ShapeDtypeStruct((r, wd), F32),
                          compiler_params=pltpu.CompilerParams(vmem_limit_bytes=VMEM_LIMIT_BYTES))(*args)


def _mesh_pos():
    x, y, c = lax.axis_index("x"), lax.axis_index("y"), lax.axis_index("c")
    return x, y, c, 4 * x + 2 * y + c


def _flip(x, y, c, f):
    fx, fy, fc = (f >> 2) & 1, (f >> 1) & 1, f & 1
    px = 1 - x if fx else x
    py = 1 - y if fy else y
    pc = 1 - c if fc else c
    return (px, py, pc), 4 * px + 2 * py + pc


_HBM_SPEC = pl.BlockSpec(memory_space=pltpu.HBM)


def _exchange(arrays, *, scatter, name):
    na = len(arrays)
    if scatter:
        out_shape = [jax.ShapeDtypeStruct(a.shape, a.dtype) for a in arrays]
    else:
        out_shape = [jax.ShapeDtypeStruct((NDEV,) + a.shape, a.dtype) for a in arrays]

    out_shape.append(jax.ShapeDtypeStruct((8, 128), F32))

    def body(*refs):
        ins, outs = refs[:na], refs[na:2 * na]
        send_sems, recv_sems, local_sems = refs[2 * na + 1:]
        refs[2 * na][...] = jnp.zeros((8, 128), F32)
        x, y, c, me = _mesh_pos()
        copies = []
        for i in range(na):
            src_own = ins[i].at[me] if scatter else ins[i]
            lc = pltpu.make_async_copy(src_own, outs[i].at[me], local_sems.at[i])
            lc.start()
            copies.append(lc)
        sends = []
        for f in range(1, NDEV):
            peer, pidx = _flip(x, y, c, f)
            for i in range(na):
                k = i * (NDEV - 1) + f - 1
                src = ins[i].at[pidx] if scatter else ins[i]
                cp = pltpu.make_async_remote_copy(
                    src_ref=src, dst_ref=outs[i].at[me], send_sem=send_sems.at[k], recv_sem=recv_sems.at[k],
                    device_id=peer, device_id_type=pl.DeviceIdType.MESH)
                cp.start()
                sends.append(cp)
        for f in range(1, NDEV):
            peer, pidx = _flip(x, y, c, f)
            for i in range(na):
                k = i * (NDEV - 1) + f - 1
                src = ins[i].at[pidx] if scatter else ins[i]
                pltpu.make_async_remote_copy(
                    src_ref=src, dst_ref=outs[i].at[pidx], send_sem=send_sems.at[k], recv_sem=recv_sems.at[k],
                    device_id=peer, device_id_type=pl.DeviceIdType.MESH).wait_recv()
        for cp in sends:
            cp.wait_send()
        for lc in copies:
            lc.wait()

    res = pl.pallas_call(
        body, name=name, out_shape=out_shape, in_specs=[_HBM_SPEC] * na,
        out_specs=[_HBM_SPEC] * na + [pl.BlockSpec(memory_space=pltpu.VMEM)],
        scratch_shapes=[pltpu.SemaphoreType.DMA((na * (NDEV - 1),)), pltpu.SemaphoreType.DMA((na * (NDEV - 1),)),
                        pltpu.SemaphoreType.DMA((na,))],
        compiler_params=pltpu.CompilerParams(has_side_effects=True),
    )(*arrays)
    return res[:na], res[na][0, 0]


_SEM_SPEC = pl.BlockSpec(memory_space=pltpu.SEMAPHORE)
_DATAFLOW = pltpu.SideEffectType.DATAFLOW_SIDE_EFFECTING


def _split_copies(srcs, lands, send_sems, recv_sems, scatter, arriving):
    x, y, c, me = _mesh_pos()
    copies = []
    for i in range(len(srcs)):
        for f in range(1, NDEV):
            peer, pidx = _flip(x, y, c, f)
            k = i * (NDEV - 1) + f - 1
            copies.append(pltpu.make_async_remote_copy(
                src_ref=srcs[i].at[pidx] if scatter else srcs[i], dst_ref=lands[i].at[pidx if arriving else me],
                send_sem=send_sems.at[k], recv_sem=recv_sems.at[k], device_id=peer,
                device_id_type=pl.DeviceIdType.MESH))
    return copies


def _exchange_start(srcs, lands, *, scatter, name):
    na = len(srcs)
    nsem = na * (NDEV - 1)

    def body(*refs):
        ins_src, ins_land = refs[:na], refs[na:2 * na]
        send_sems, recv_sems = refs[2 * na], refs[2 * na + 1]
        token = refs[-1]
        for cp in _split_copies(ins_src, ins_land, send_sems, recv_sems, scatter, False):
            cp.start()
        token[...] = jnp.zeros_like(token)

    thru = [pltpu.HBM(a.shape, a.dtype) for a in list(srcs) + list(lands)]
    res = pl.pallas_call(
        body, name=name,
        out_shape=(pltpu.SemaphoreType.DMA((nsem,)), pltpu.SemaphoreType.DMA((nsem,)), *thru,
                   jax.ShapeDtypeStruct((8, 128), F32)),
        in_specs=[_HBM_SPEC] * (2 * na),
        out_specs=(_SEM_SPEC, _SEM_SPEC, *([_HBM_SPEC] * (2 * na)), pl.BlockSpec(memory_space=pltpu.VMEM)),
        input_output_aliases={i: 2 + i for i in range(2 * na)},
        compiler_params=pltpu.CompilerParams(has_side_effects=_DATAFLOW),
    )(*[pltpu.with_memory_space_constraint(a, pltpu.HBM) for a in list(srcs) + list(lands)])
    send_sems, recv_sems = res[0], res[1]
    return send_sems, recv_sems, res[2:2 + na], res[2 + na:2 + 2 * na], res[-1][0, 0]


def _exchange_wait(send_sems, recv_sems, srcs, lands, after, *, scatter, name):
    na = len(srcs)

    def body(*refs):
        ins_src, ins_land = refs[:na], refs[na:2 * na]
        s_sems, r_sems = refs[2 * na], refs[2 * na + 1]
        for cp in _split_copies(ins_src, ins_land, s_sems, r_sems, scatter, False):
            cp.wait_send()
        for cp in _split_copies(ins_src, ins_land, s_sems, r_sems, scatter, True):
            cp.wait_recv()

    thru = [pltpu.HBM(a.shape, a.dtype) for a in list(srcs) + list(lands)]
    res = pl.pallas_call(
        body, name=name, out_shape=tuple(thru),
        in_specs=[_HBM_SPEC] * (2 * na) + [_SEM_SPEC, _SEM_SPEC, pl.BlockSpec(memory_space=pl.ANY)],
        out_specs=tuple([_HBM_SPEC] * (2 * na)),
        input_output_aliases={i: i for i in range(2 * na)},
        compiler_params=pltpu.CompilerParams(has_side_effects=_DATAFLOW),
    )(*srcs, *lands, send_sems, recv_sems, after)
    return res[na:]


def _landing(block, me):
    buf = lax.empty((NDEV,) + block.shape, block.dtype)
    return lax.dynamic_update_slice_in_dim(buf, block[None], me, axis=0)


def _seg_kw(nseg, n_ctx, tm):
    return dict(nseg=nseg, seg_blocks=(n_ctx // tm if nseg == 2 else 0))


def _ffn_fwd(tag, h, gpre, gpost, shift, scale, gate, w, *, nseg, n_ctx, tm):
    n = h.shape[0]
    kw = _seg_kw(nseg, n_ctx, tm)
    (u,) = _rowwise(tag + "_pre", _pre_fwd_fn, n, [h], [("full", gpre), ("seg", shift), ("seg", scale)],
                    [(D_MODEL, BF16)], tm=tm, **kw)
    if "early" in w:
        w.update(w.pop("early")(u))
    s, a, b = _mm_glu(u, w["win_t"], name=tag + "_glu")
    if "late" in w:
        w.update(w.pop("late")(s))
    y, ho = _mm_rows(s, w["wout"], functools.partial(_out_post_fn, 0.5), [h], [("full", gpost), ("seg", gate)],
                     [(D_MODEL, F32), (D_MODEL, F32)], name=tag + "_out", tk=FFN_DIM, n_ctx=n_ctx)
    return ho, dict(h=h, u=u, s=s, a=a, b=b, y=y)


def _ffn_bwd(tag, dho, sv, gpre, gpost, scale, gate, w, put, *, nseg, n_ctx, tm):
    n = dho.shape[0]
    kw = _seg_kw(nseg, n_ctx, tm)
    dy, dgate, dgpost = _rowwise(tag + "_postb", functools.partial(_post_bwd_fn, 0.5), n, [dho, sv["y"]],
                                 [("full", gpost), ("seg", gate)], [(D_MODEL, BF16)], [D_MODEL, D_MODEL], tm=tm, **kw)
    tok = put("w_out", _mm_tn(sv["s"], dy, name=tag + "_dwout", tm=1408, tn=1024, col_blocks=1))
    dp = _mm_glu_bwd(dy, w["wout"], sv["a"], sv["b"], name=tag + "_ds")
    tok2 = put("w_in", _mm_tn(dp, sv["u"], name=tag + "_dwin", tm=1408, tn=1024, col_blocks=1))
    for t in (tok, tok2):
        if t is not None:
            gpre = gpre + t
    dh, dshift, dscale, dgpre = _mm_rows(dp, w["win_t"], _pre_bwd_fn, [sv["h"], dho],
                                         [("full", gpre), ("seg", scale)], [(D_MODEL, F32)],
                                         [D_MODEL, D_MODEL, D_MODEL], name=tag + "_du", tk=FFN_DIM, n_ctx=n_ctx)
    return dh, None, dict(shift=dshift, scale=dscale, gate=dgate, gpre=dgpre, gpost=dgpost)


def _local_step(x, ctx, target, mods, norm_g, get_w, small, put_grad):
    t_len, n_ctx = x.shape[0], ctx.shape[0]
    n0 = t_len + n_ctx
    tm0 = _pick(n_ctx, 256, 8)
    tm1 = _pick(t_len, 512, 8)
    ncc = n_ctx // CHUNK
    g = {}

    def modrow(i, k, nseg):
        mc, mx = mods[i]
        if nseg == 2:
            return jnp.stack([mc[k], mx[k]])[:, None, :]
        return mx[k][None, None, :]

    pending = [None]

    def gvec(i, k):
        v = norm_g[i, k][None, :]
        if pending[0] is not None:
            v = v + pending[0]
            pending[0] = None
        return v

    xc = jnp.concatenate([ctx, x], axis=0)
    L0 = dict(nseg=2, n_ctx=n_ctx, tm=tm0)
    wts = dict(get_w("ffn00", xc))
    h1, sv_f01 = _ffn_fwd("l0f1", xc, gvec(0, 0), gvec(0, 1), modrow(0, 0, 2), modrow(0, 1, 2), modrow(0, 2, 2),
                          wts["ffn00"], **L0)
    kw0 = _seg_kw(2, n_ctx, tm0)
    (um0,) = _rowwise("l0m_pre", _pre_fwd_fn, n0, [h1], [("full", gvec(0, 2)), ("seg", modrow(0, 3, 2)),
                                                         ("seg", modrow(0, 4, 2))], [(D_MODEL, BF16)], tm=tm0, **kw0)
    wts.update(get_w("ssd", um0))
    win_ssd = wts["ssd_win_t"]
    nh = SSD_HEADS
    dt_blk = (SSD_INNER + SSD_CONV_DIM) // (2 * nh)
    z = _mm(um0, win_ssd, out_dtype=F32, name="ssd_z", rhs_t=True, n=SSD_INNER)
    xbc_pre = _mm(um0, win_ssd, out_dtype=F32, name="ssd_xbc", rhs_t=True, n=SSD_CONV_DIM,
                  b_off=(SSD_INNER // 1024, 0))
    dtr = _mm(um0, win_ssd, out_dtype=F32, name="ssd_dt", rhs_t=True, n=2 * nh, b_off=(dt_blk, 0))
    cpre, xbc = _conv_fwd(xbc_pre, small["conv_w8"], small["conv_b"], n_ctx=n_ctx, name="ssd_conv")
    nh = SSD_HEADS
    dt_dir = [dtr[:, :nh], dtr[:, nh:2 * nh]]
    dtT_dir = [d.T for d in dt_dir]
    bias_r = [small["dt_bias"][d][None, :] for d in range(2)]
    bias_c = [small["dt_bias"][d][:, None] for d in range(2)]
    alog_r = [small["a_log"][d][None, :] for d in range(2)]
    alog_c = [small["a_log"][d][:, None] for d in range(2)]
    ys, hss = [], []
    for d in range(2):
        yd, hsd = _ssd_scan_fwd(xbc, dt_dir[d], dtT_dir[d], bias_r[d], bias_c[d], alog_r[d], alog_c[d],
                                rev=(d == 1), n_ctx_chunks=ncc, name=f"ssd_scan{d}")
        ys.append(yd)
        hss.append(hsd)
    dvec = jnp.repeat(small["ssd_d"], SSD_HEAD_DIM)[None, :]
    ngv = small["ssd_norm_g"][None, :]
    gate_rows = [ys[0], ys[1], (xbc, SSD_INNER, 0, 0), z]
    off = n_ctx // tm0
    lat = lambda r: (r[0], r[1], r[2], off) if isinstance(r, tuple) else (r, r.shape[1], 0, off)
    (yn,) = _rowwise("ssd_gate", _ssdgate_fwd_fn, t_len, [lat(r) for r in gate_rows],
                     [("full", dvec), ("full", ngv)], [(SSD_INNER, BF16)], tm=tm0)
    h1x = h1[n_ctx:]
    L1 = dict(nseg=1, n_ctx=0, tm=_pick(t_len, 512, 8))
    if "late" in wts:
        wts.update(wts.pop("late")(yn))
    yo0, h2 = _mm_rows(yn, wts["ssd_wout"], functools.partial(_out_post_fn, 1.0), [h1x],
                       [("full", gvec(0, 3)), ("seg", modrow(0, 5, 1))], [(D_MODEL, F32), (D_MODEL, F32)],
                       name="ssd_out", tk=SSD_INNER)
    wts.update(get_w("ffn01", h2))
    h3, sv_f02 = _ffn_fwd("l0f2", h2, gvec(0, 4), gvec(0, 5), modrow(0, 6, 1), modrow(0, 7, 1), modrow(0, 8, 1),
                          wts["ffn01"], **L1)

    wts.update(get_w("ffn10", h3))
    h4, sv_f11 = _ffn_fwd("l1f1", h3, gvec(1, 0), gvec(1, 1), modrow(1, 0, 1), modrow(1, 1, 1), modrow(1, 2, 1),
                          wts["ffn10"], **L1)
    (um1,) = _rowwise("l1m_pre", _pre_fwd_fn, t_len, [h4], [("full", gvec(1, 2)), ("seg", modrow(1, 3, 1)),
                                                            ("seg", modrow(1, 4, 1))], [(D_MODEL, BF16)], tm=tm1)
    wts.update(get_w("gm", um1))
    p1 = _mm(um1, wts["gm_win"], out_dtype=F32, name="gm_in", tm=2048)
    vg = small["gm_v_g"][None, :]
    vb = small["gm_v_b"][None, :]
    gu, gvn = _rowwise("gm_act", _gm_act_fwd_fn, t_len, [p1], [("full", vg), ("full", vb)],
                       [(GM_INNER, F32), (GM_INNER, BF16)], tm=256)
    ws_bf = small["gm_w_s"].astype(BF16)
    wst_bf = jnp.swapaxes(small["gm_w_s"], 1, 2).astype(BF16)
    bst = small["gm_b_s"].T
    tgm = _gm_spatial_fwd(gu, gvn, ws_bf, bst, name="gm_spatial")
    yo1, h5 = _mm_rows(tgm, wts["gm_wout"], functools.partial(_out_post_fn, 1.0), [h4],
                       [("full", gvec(1, 3)), ("seg", modrow(1, 5, 1))], [(D_MODEL, F32), (D_MODEL, F32)],
                       name="gm_out", tk=GM_INNER)
    wts.update(get_w("ffn11", h5))
    h6, sv_f12 = _ffn_fwd("l1f2", h5, gvec(1, 4), gvec(1, 5), modrow(1, 6, 1), modrow(1, 7, 1), modrow(1, 8, 1),
                          wts["ffn11"], **L1)

    dh, loss_parts = _rowwise("loss", _loss_fn, t_len, [h6, target], [], [(D_MODEL, F32)], [D_MODEL], tm=tm1)

    zero = jnp.zeros((D_MODEL,), F32)
    dmx = [[zero] * N_MOD for _ in range(2)]
    dmc = [[zero] * N_MOD for _ in range(2)]
    dng = [[zero] * 6 for _ in range(2)]

    def put_mod(i, k, acc):
        if acc.shape[0] == 2:
            dmc[i][k] = dmc[i][k] + acc[0, 0]
            dmx[i][k] = dmx[i][k] + acc[1, 0]
        else:
            dmx[i][k] = dmx[i][k] + acc[0, 0]

    def put_g(i, k, acc):
        dng[i][k] = dng[i][k] + jnp.sum(acc[:, 0], axis=0)

    def ffn_back(tag, i, j, dho, sv, w, lay):
        nseg = lay["nseg"]
        base = 0 if j == 0 else 6
        gi = 0 if j == 0 else 4
        dh_in, pending[0], s = _ffn_bwd(tag, dho, sv, gvec(i, gi), gvec(i, gi + 1), modrow(i, base + 1, nseg),
                                        modrow(i, base + 2, nseg), w, functools.partial(put_grad, f"ffn{i}{j}"), **lay)
        put_mod(i, base, s["shift"])
        put_mod(i, base + 1, s["scale"])
        put_mod(i, base + 2, s["gate"])
        put_g(i, gi, s["gpre"])
        put_g(i, gi + 1, s["gpost"])
        return dh_in

    dh = ffn_back("l1f2", 1, 1, dh, sv_f12, wts["ffn11"], L1)
    dyo, dgate, dgp = _rowwise("l1m_postb", functools.partial(_post_bwd_fn, 1.0), t_len, [dh, yo1],
                               [("full", gvec(1, 3)), ("seg", modrow(1, 5, 1))], [(D_MODEL, BF16)],
                               [D_MODEL, D_MODEL], tm=tm1)
    put_mod(1, 5, dgate)
    put_g(1, 3, dgp)
    put_grad("gm", "w_out", _mm_tn(tgm, dyo, name="gm_dwout", tn=1024, col_blocks=1))
    dtg = _mm(dyo, wts["gm_wout"], out_dtype=F32, name="gm_dt", rhs_t=True)
    dgu, dgvn, dws, dbst = _gm_spatial_bwd(dtg, gu, gvn, ws_bf, wst_bf, bst, name="gm_spatialb")
    g["gm_w_s"] = dws
    g["gm_b_s"] = dbst.T
    dp1, dvg, dvb = _rowwise("gm_actb", _gm_act_bwd_fn, t_len, [p1, dgu, dgvn], [("full", vg)],
                             [(2 * GM_INNER, BF16)], [GM_INNER, GM_INNER], tm=256)
    g["gm_v_g"] = dvg[0, 0]
    g["gm_v_b"] = dvb[0, 0]
    pending[0] = put_grad("gm", "w_in", _mm_tn(um1, dp1, name="gm_dwin", tm=1024, col_blocks=NDEV))
    dh, dsh, dsc, dgp = _mm_rows(dp1, wts["gm_win"], _pre_bwd_fn, [h4, dh],
                                 [("full", gvec(1, 2)), ("seg", modrow(1, 4, 1))], [(D_MODEL, F32)],
                                 [D_MODEL, D_MODEL, D_MODEL], name="gm_dum", tk=2048, rhs_t=True)
    put_mod(1, 3, dsh)
    put_mod(1, 4, dsc)
    put_g(1, 2, dgp)
    dh = ffn_back("l1f1", 1, 0, dh, sv_f11, wts["ffn10"], L1)

    dh = ffn_back("l0f2", 0, 1, dh, sv_f02, wts["ffn01"], L1)
    dyo, dgate, dgp = _rowwise("l0m_postb", functools.partial(_post_bwd_fn, 1.0), t_len, [dh, yo0],
                               [("full", gvec(0, 3)), ("seg", modrow(0, 5, 1))], [(D_MODEL, BF16)],
                               [D_MODEL, D_MODEL], tm=tm1)
    put_mod(0, 5, dgate)
    put_g(0, 3, dgp)
    tok = put_grad("ssd", "w_out", _mm_tn(yn, dyo, name="ssd_dwout", tn=1024, col_blocks=1))
    dyn = _mm(dyo, wts["ssd_wout"], out_dtype=F32, name="ssd_dyn", rhs_t=True)
    dy_ssd, dz, dngv, ddv = _rowwise("ssd_gateb", _ssdgate_bwd_fn, n0,
                                     [(dyn, SSD_INNER, 0, -(n_ctx // tm0))] + gate_rows,
                                     [("full", dvec), ("full", ngv if tok is None else ngv + tok)],
                                     [(SSD_INNER, F32), (SSD_INNER, BF16)],
                                     [SSD_INNER, SSD_INNER], tm=tm0)
    g["ssd_norm_g"] = dngv[0, 0]
    g["ssd_D"] = jnp.sum(ddv[0, 0].reshape(SSD_HEADS, SSD_HEAD_DIM), axis=1)
    dxbcs, ddts, dalogs, dbiases = [], [], [], []
    for d in range(2):
        dxd, ddtd, dal, dbi = _ssd_scan_bwd(dy_ssd, xbc, hss[d], dt_dir[d], dtT_dir[d], bias_r[d], bias_c[d],
                                            alog_r[d], alog_c[d], dvec, rev=(d == 1), n_ctx_chunks=ncc,
                                            direct=(d == 0), name=f"ssd_scanb{d}")
        dxbcs.append(dxd)
        ddts.append(ddtd)
        dalogs.append(dal[0])
        dbiases.append(dbi[0])
    g["ssd_A_log"] = jnp.stack(dalogs)
    g["ssd_dt_bias"] = jnp.stack(dbiases)
    dxbc_pre, dcw8, dcb = _conv_bwd(dxbcs[0], dxbcs[1], cpre, xbc_pre, small["conv_w8"], n_ctx=n_ctx, name="ssd_convb")
    g["ssd_conv_w"] = dcw8[:SSD_CONV]
    g["ssd_conv_b"] = dcb[0]
    ddt_bf = jnp.concatenate([ddts[0], ddts[1]], axis=1).astype(BF16)
    n_in = SSD_INNER + SSD_CONV_DIM + 2 * nh
    dw_t = _mm_tn(dz, um0, name="ssd_dwz", col_blocks=1, stack=(n_in, 0, None))
    dw_t = _mm_tn(dxbc_pre, um0, name="ssd_dwxbc", col_blocks=1, stack=(n_in, SSD_INNER, dw_t))
    dw_t = _mm_tn(ddt_bf, um0, name="ssd_dwdt", col_blocks=1, stack=(n_in, SSD_INNER + SSD_CONV_DIM, dw_t))
    pending[0] = put_grad("ssd", "w_in", dw_t)
    dum0 = _mm(dz, win_ssd, out_dtype=F32, name="ssd_dum_z", tk=SSD_INNER, n=D_MODEL)
    dum0 = _mm(dxbc_pre, win_ssd, out_dtype=F32, name="ssd_dum_x", tk=SSD_INNER, n=D_MODEL,
               b_off=(SSD_INNER // SSD_INNER, 0), add=dum0)
    dum0 = _mm(ddt_bf, win_ssd, out_dtype=F32, name="ssd_dum_dt", tk=2 * nh, n=D_MODEL, b_off=(dt_blk, 0), add=dum0)
    dh0, dsh, dsc, dgp = _rowwise("l0m_preb", _pre_bwd_fn, n0, [dum0, h1, (dh, D_MODEL, 0, -(n_ctx // tm0))],
                                  [("full", gvec(0, 2)), ("seg", modrow(0, 4, 2))], [(D_MODEL, F32)],
                                  [D_MODEL, D_MODEL, D_MODEL], tm=tm0, **kw0)
    put_mod(0, 3, dsh)
    put_mod(0, 4, dsc)
    put_g(0, 2, dgp)
    dh0 = ffn_back("l0f1", 0, 0, dh0, sv_f01, wts["ffn00"], L0)
    grad_x = dh0[n_ctx:]
    g["norm_g"] = jnp.stack([jnp.stack(r) for r in dng])
    g["dmx"] = jnp.stack([jnp.concatenate(r) for r in dmx])
    g["dmc"] = jnp.stack([jnp.concatenate(r) for r in dmc])
    return loss_parts[0], grad_x, g


GROUPS = ("ffn00", "ssd", "ffn01", "ffn10", "gm", "ffn11")


TRANSPOSED_IN = ("ffn", "ssd")


def _is_transposed(group):
    return group.startswith(TRANSPOSED_IN)


def _mats_in(group, win_l):
    if _is_transposed(group):
        return {("win_t" if group.startswith("ffn") else group + "_win_t"): win_l.reshape(-1, win_l.shape[2])}
    return {group + "_win": win_l}


def _mats_out(group, wout_l):
    pre = "" if group.startswith("ffn") else group + "_"
    return {pre + "wout": wout_l.reshape(-1, wout_l.shape[2])}


def _group_mats(group, lands):
    m = {**_mats_in(group, lands[0]), **_mats_out(group, lands[1])}
    return {group: m} if group.startswith("ffn") else m


def _grad_blocks(which, grad):
    if grad.ndim == 3:
        return grad if grad.shape[0] == NDEV else grad.reshape(NDEV, grad.shape[1] // NDEV, grad.shape[2])
    if which == "w_in":
        k, n = grad.shape
        return jnp.transpose(grad.reshape(k, NDEV, n // NDEV), (1, 0, 2)).astype(BF16)
    return grad.reshape(NDEV, grad.shape[0] // NDEV, grad.shape[1]).astype(BF16)


def kernel(x, c, ctx, c_ctx, ada_w, ada_b, norm_g, ffn_w_in, ffn_w_out, ssd_w_in, ssd_conv_w, ssd_conv_b, ssd_dt_bias, ssd_A_log, ssd_D, ssd_norm_g, ssd_w_out, gm_w_in, gm_v_g, gm_v_b, gm_w_s, gm_b_s, gm_w_out, loss_target, m_c_ctx, m_ada_w, m_ada_b, m_norm_g, m_ffn_w_in, m_ffn_w_out, m_ssd_w_in, m_ssd_conv_w, m_ssd_conv_b, m_ssd_dt_bias, m_ssd_A_log, m_ssd_D, m_ssd_norm_g, m_ssd_w_out, m_gm_w_in, m_gm_v_g, m_gm_v_b, m_gm_w_s, m_gm_b_s, m_gm_w_out, v_c_ctx, v_ada_w, v_ada_b, v_norm_g, v_ffn_w_in, v_ffn_w_out, v_ssd_w_in, v_ssd_conv_w, v_ssd_conv_b, v_ssd_dt_bias, v_ssd_A_log, v_ssd_D, v_ssd_norm_g, v_ssd_w_out, v_gm_w_in, v_gm_v_g, v_gm_v_b, v_gm_w_s, v_gm_b_s, v_gm_w_out):
    me = 4 * lax.axis_index("x") + 2 * lax.axis_index("y") + lax.axis_index("c")
    d = D_MODEL
    ncol = N_MOD * d // NDEV

    small_pack = jnp.concatenate([c.reshape(-1), norm_g.reshape(-1), ssd_conv_w.reshape(-1),
                                  gm_v_g.reshape(-1), gm_v_b.reshape(-1)])[None, :]
    (sp,), _ = _exchange([small_pack], scatter=False, name="gather_small")
    sp = sp[:, 0]
    o = 0
    c_all = sp[:, o:o + d]; o += d
    ng_all = sp[:, o:o + 2 * 6 * 128].reshape(NDEV, 2, 6, 128); o += 2 * 6 * 128
    cw_all = sp[:, o:o + SSD_CONV * 512].reshape(NDEV, SSD_CONV, 512); o += SSD_CONV * 512
    vg_all = sp[:, o:o + 256]; o += 256
    vb_all = sp[:, o:o + 256]; o += 256
    norm_g_full = jnp.transpose(ng_all, (1, 2, 0, 3)).reshape(2, 6, d)
    conv_w_full = jnp.transpose(cw_all, (1, 0, 2)).reshape(SSD_CONV, SSD_CONV_DIM)
    gm_v_g_full = vg_all.reshape(-1)
    gm_v_b_full = vb_all.reshape(-1)

    c16 = jnp.concatenate([c_all, jnp.broadcast_to(c_ctx[None, :], (NDEV, d))], axis=0)
    ada_b_loc = lax.dynamic_slice_in_dim(ada_b, me * ncol, ncol, axis=1)
    mods_loc = jnp.stack([_mm_f32(c16, ada_w[i], name=f"ada_mod{i}", silu_a=True, bias=ada_b_loc[i][None, :])
                          for i in range(2)])
    (mods_all,), mods_done = _exchange([mods_loc], scatter=False, name="gather_mods")

    tr = lambda a: jnp.swapaxes(a, -1, -2)
    shard = {"ssd": (tr(ssd_w_in)[0], ssd_w_out[0]), "gm": (gm_w_in[0], gm_w_out[0])}
    for i in range(2):
        for j in range(2):
            shard[f"ffn{i}{j}"] = (tr(ffn_w_in)[i, j], ffn_w_out[i, j])
    apart = GROUPS[:2]
    units = []
    for grp in GROUPS:
        units += [(grp + "_in", grp, (0,)), (grp + "_out", grp, (1,))] if grp in apart else [(grp, grp, (0, 1))]
    gathers = {}
    started = mods_done
    for unit, grp, idx in units:
        srcs = [(shard[grp][k] + started).astype(BF16) for k in idx]
        st = _exchange_start(srcs, [_landing(s, me) for s in srcs], scatter=False, name="gather_start_" + unit)
        gathers[unit] = st[:4]
        started = st[4]

    def fetch(unit, after):
        return _exchange_wait(*gathers[unit], after, scatter=False, name="gather_wait_" + unit)

    def get_w(grp, after):
        if grp not in apart:
            return _group_mats(grp, fetch(grp, after))
        early = lambda later: _mats_in(grp, fetch(grp + "_in", later)[0])
        late = lambda later: _mats_out(grp, fetch(grp + "_out", later)[0])
        if grp.startswith("ffn"):
            return {grp: dict(early=early, late=late)}
        return dict(early(after), late=late)

    scatters = {}
    held = {}

    def put_grad(grp, which, grad):
        if grp in apart:
            unit, blocks = grp + "_" + which[2:], [_grad_blocks(which, grad)]
        else:
            held[grp, which] = _grad_blocks(which, grad)
            if (grp, "w_in") not in held or (grp, "w_out") not in held:
                return None
            unit, blocks = grp, [held[grp, "w_in"], held[grp, "w_out"]]
        own = [lax.dynamic_index_in_dim(b, me, axis=0, keepdims=False) for b in blocks]
        st = _exchange_start(blocks, [_landing(o_, me) for o_ in own], scatter=True, name="scatter_start_" + unit)
        scatters[unit] = st[:4]
        return st[4]

    mods_rows = jnp.transpose(mods_all, (1, 2, 0, 3)).reshape(2, 2 * NDEV, N_MOD * d) + started
    mx = lax.dynamic_index_in_dim(mods_rows, me, axis=1, keepdims=False).reshape(2, N_MOD, d)
    mc = mods_rows[:, NDEV].reshape(2, N_MOD, d)
    mods = [(mc[i], mx[i]) for i in range(2)]

    small = dict(conv_w8=jnp.pad(conv_w_full, ((0, 8 - SSD_CONV), (0, 0))), conv_b=ssd_conv_b, dt_bias=ssd_dt_bias[0],
                 a_log=ssd_A_log[0], ssd_d=ssd_D[0], ssd_norm_g=ssd_norm_g[0], gm_v_g=gm_v_g_full,
                 gm_v_b=gm_v_b_full, gm_w_s=gm_w_s[0], gm_b_s=gm_b_s[0])
    loss_parts, grad_x, g = _local_step(x[0], ctx[0], loss_target[0], mods, norm_g_full, get_w, small, put_grad)
    g["loss"] = (0.5 / d * jnp.sum(loss_parts)).reshape(1)

    whole = {"ffn_w_in": (tr(ffn_w_in), tr(m_ffn_w_in), tr(v_ffn_w_in)), "ffn_w_out": (ffn_w_out, m_ffn_w_out, v_ffn_w_out),
             "ssd_w_in": (tr(ssd_w_in), tr(m_ssd_w_in), tr(v_ssd_w_in)), "ssd_w_out": (ssd_w_out, m_ssd_w_out, v_ssd_w_out),
             "gm_w_in": (gm_w_in, m_gm_w_in, v_gm_w_in), "gm_w_out": (gm_w_out, m_gm_w_out, v_gm_w_out)}
    res = {}

    def update_units(some, after):
        for unit, grp, idx in some:
            parts = _exchange_wait(*scatters[unit], after, scatter=True, name="scatter_wait_" + unit)
            for k, p in zip(idx, parts):
                which = ("in", "out")[k]
                nm = ("ffn" if grp.startswith("ffn") else grp) + "_w_" + which
                sel = (int(grp[3]), int(grp[4])) if grp.startswith("ffn") else (0,)
                res[nm] = _adamw(p, *whole[nm], name=f"adamw_{grp}_{which}", sel=sel, into=res.get(nm))
                after = res[nm][0]
        return after

    sg_names = ["dmx", "dmc", "norm_g", "ssd_conv_w", "ssd_conv_b", "ssd_dt_bias", "ssd_A_log", "ssd_D", "ssd_norm_g",
                "gm_v_g", "gm_v_b", "gm_w_s", "gm_b_s", "loss"]
    sg_shapes = [g[n].shape for n in sg_names]
    flat = jnp.concatenate([g[n].reshape(-1) for n in sg_names])
    npack = flat.shape[0]
    pad = (-npack) % 1024
    flat = jnp.pad(flat, (0, pad)).reshape(-1, 128)
    sg_start = _exchange_start([flat], [_landing(flat, me)], scatter=False, name="small_grads_start")
    by_send = list(reversed(units))
    update_units(by_send[:4], jnp.stack([sg_start[4], grad_x[0, 0]]))
    early_done = jnp.stack([res[nm][0].reshape(-1)[-1] for nm in sorted(res)])
    (sg_all,) = _exchange_wait(*sg_start[:4], early_done, scatter=False, name="small_grads_wait")
    sg_sum = _sum_slots(sg_all, name="sum_small_grads").reshape(-1)[:npack]
    update_units(by_send[4:], sg_sum)
    sums = {}
    o = 0
    for n, shp in zip(sg_names, sg_shapes):
        sz = math.prod(shp)
        sums[n] = sg_sum[o:o + sz].reshape(shp)
        o += sz
    loss = sums["loss"][0]
    per_dev = sg_all.reshape(NDEV, -1)
    dmx_all =per_dev[:, :2 * N_MOD * d].reshape(NDEV, 2, N_MOD * d)
    dmc_all = per_dev[:, 2 * N_MOD * d:4 * N_MOD * d].reshape(NDEV, 2, N_MOD * d)

    (s16,) = _rowwise("ada_silu", lambda cc: ((_silu(cc),), ()), 2 * NDEV, [c16], [], [(d, F32)], tm=2 * NDEV)
    s16_t = s16.T
    g_ada_w, dcc_parts = [], []
    for i in range(2):
        rhs = jnp.concatenate([lax.dynamic_slice_in_dim(dmx_all[:, i], me * ncol, ncol, axis=1),
                               lax.dynamic_slice_in_dim(dmc_all[:, i], me * ncol, ncol, axis=1)], axis=0)
        g_ada_w.append(_mm_f32(s16_t, rhs, name=f"ada_dw{i}"))
        dmc_loc = lax.dynamic_slice_in_dim(sums["dmc"][i], me * ncol, ncol, axis=0)
        rhs_c = jnp.zeros((ncol, 128), F32).at[:, 0].set(dmc_loc)
        dcc_parts.append(_mm_f32(ada_w[i], rhs_c, name=f"ada_dcc{i}")[:, 0])
    g_ada_w = jnp.stack(g_ada_w)
    dcc_part = (dcc_parts[0] + dcc_parts[1]).reshape(8, 128)
    (dcc_all,), _ = _exchange([dcc_part], scatter=False, name="gather_dcc")
    g_c_ctx = _sum_slots(dcc_all, name="sum_dcc", scale_by=c_ctx.reshape(8, 128)).reshape(d)
    g_ada_b = sums["dmx"] + sums["dmc"]

    outs = _adamw(g_ada_w.reshape(1, -1, ncol), ada_w.reshape(-1, ncol), m_ada_w.reshape(-1, ncol),
                  v_ada_w.reshape(-1, ncol), name="adamw_ada_w")
    res["ada_w"] = [o_.reshape(ada_w.shape) for o_ in outs]

    loc = lambda a, ax, n: lax.dynamic_slice_in_dim(a, me * n, n, axis=ax)
    small_g = dict(c_ctx=g_c_ctx, ada_b=g_ada_b, norm_g=loc(sums["norm_g"], 2, 128),
                   ssd_conv_w=loc(sums["ssd_conv_w"], 1, 512)[None], ssd_conv_b=sums["ssd_conv_b"][None],
                   ssd_dt_bias=sums["ssd_dt_bias"][None], ssd_A_log=sums["ssd_A_log"][None], ssd_D=sums["ssd_D"][None],
                   ssd_norm_g=sums["ssd_norm_g"][None], gm_v_g=loc(sums["gm_v_g"], 0, 256)[None],
                   gm_v_b=loc(sums["gm_v_b"], 0, 256)[None], gm_b_s=sums["gm_b_s"][None])
    small_w = dict(c_ctx=(c_ctx, m_c_ctx, v_c_ctx), ada_b=(ada_b, m_ada_b, v_ada_b), norm_g=(norm_g, m_norm_g, v_norm_g),
                   ssd_conv_w=(ssd_conv_w, m_ssd_conv_w, v_ssd_conv_w), ssd_conv_b=(ssd_conv_b, m_ssd_conv_b, v_ssd_conv_b),
                   ssd_dt_bias=(ssd_dt_bias, m_ssd_dt_bias, v_ssd_dt_bias), ssd_A_log=(ssd_A_log, m_ssd_A_log, v_ssd_A_log),
                   ssd_D=(ssd_D, m_ssd_D, v_ssd_D), ssd_norm_g=(ssd_norm_g, m_ssd_norm_g, v_ssd_norm_g),
                   gm_v_g=(gm_v_g, m_gm_v_g, v_gm_v_g), gm_v_b=(gm_v_b, m_gm_v_b, v_gm_v_b),
                   gm_b_s=(gm_b_s, m_gm_b_s, v_gm_b_s))
    sn = list(small_w)
    flat2 = lambda a: a.reshape(-1, CHUNK)
    res["gm_w_s"] = [o_.reshape(gm_w_s.shape) for o_ in _adamw(
        flat2(sums["gm_w_s"])[None], flat2(gm_w_s), flat2(m_gm_w_s), flat2(v_gm_w_s), name="adamw_gm_w_s")]

    def pack(arrs):
        f = jnp.concatenate([a.reshape(-1) for a in arrs])
        return jnp.pad(f, (0, (-f.shape[0]) % (256 * 128))).reshape(-1, 128)

    pg = pack([small_g[n].reshape(small_w[n][0].shape) for n in sn])
    outs = _adamw(pg[None], pack([small_w[n][0] for n in sn]), pack([small_w[n][1] for n in sn]),
                  pack([small_w[n][2] for n in sn]), name="adamw_small")
    flat_outs = [o_.reshape(-1) for o_ in outs]
    o = 0
    for n in sn:
        shp = small_w[n][0].shape
        sz = math.prod(shp)
        res[n] = [fo[o:o + sz].reshape(shp) for fo in flat_outs]
        o += sz

    order = ["c_ctx", "ada_w", "ada_b", "norm_g", "ffn_w_in", "ffn_w_out", "ssd_w_in", "ssd_conv_w", "ssd_conv_b",
             "ssd_dt_bias", "ssd_A_log", "ssd_D", "ssd_norm_g", "ssd_w_out", "gm_w_in", "gm_v_g", "gm_v_b", "gm_w_s",
             "gm_b_s", "gm_w_out"]
    for nm in ("ffn_w_in", "ssd_w_in"):
        res[nm] = [tr(a) for a in res[nm]]
    result = [loss, grad_x[None]]
    for k in range(4):
        result += [res[n][k] for n in order]
    return tuple(result)
```

```python
import functools
import math

import jax
import jax.numpy as jnp
from jax import lax
from jax.experimental import pallas as pl
from jax.experimental.pallas import tpu as pltpu

F32 = jnp.float32
BF16 = jnp.bfloat16

NDEV = 8
D_MODEL = 1024
FFN_DIM = 2816
N_MOD = 9
EPS = 1e-6
SSD_INNER = 2048
SSD_HEADS = 32
SSD_HEAD_DIM = 64
SSD_GROUPS = 8
SSD_HPG = 4
SSD_STATE = 128
SSD_CONV = 5
SSD_CONV_DIM = 4096
CHUNK = 128
GM_INNER = 2048
GM_GROUPS = 8
GM_GROUP_DIM = 256
ADAM_LR = 0.001
ADAM_B1 = 0.9
ADAM_B2 = 0.999
ADAM_EPS = 1e-08
ADAM_WD = 0.01
ADAM_STEP = 10
NEG_BIG = -1e30
VMEM_LIMIT_BYTES = 56 * 1024 * 1024
HI = lax.Precision.HIGHEST


def _params(*sem):
    return pltpu.CompilerParams(dimension_semantics=sem, vmem_limit_bytes=VMEM_LIMIT_BYTES)


def _pick(n, target, mult=16):
    if n <= target:
        return n
    for t in range(target - target % mult, 0, -mult):
        if n % t == 0:
            return t
    raise ValueError((n, target, mult))


def _sig(x):
    return 0.5 * jnp.tanh(0.5 * x) + 0.5


def _silu(x):
    return x * _sig(x)


def _dsilu(x):
    s = _sig(x)
    return s * (1.0 + x * (1.0 - s))


_GELU_C = math.sqrt(2.0 / math.pi)


def _gelu(x):
    return 0.5 * x * (1.0 + jnp.tanh(_GELU_C * (x + 0.044715 * x * x * x)))


def _gelu_and_grad(x):
    x2 = x * x
    t = jnp.tanh(_GELU_C * (x + 0.044715 * x2 * x))
    half = 0.5 * (1.0 + t)
    return x * half, half + 0.5 * x * (1.0 - t * t) * _GELU_C * (1.0 + 3.0 * 0.044715 * x2)


def _dgelu(x):
    return _gelu_and_grad(x)[1]


def _softplus(x):
    return jnp.maximum(x, 0.0) + jnp.log1p(jnp.exp(-jnp.abs(x)))


def _sum0(v):
    return jnp.sum(v, axis=0, keepdims=True)


def _rms(h):
    r = lax.rsqrt(jnp.mean(h * h, axis=-1, keepdims=True) + EPS)
    return h * r, r


def _dot(a, b, dims=((1,), (0,)), precision=None):
    return lax.dot_general(a, b, (dims, ((), ())), preferred_element_type=F32, precision=precision)


_NT = ((1,), (1,))
_TN = ((0,), (0,))


def _rowwise(name, fn, n_rows, rows, consts, outs, accs=(), *, tm, nseg=1, seg_blocks=0):
    assert n_rows % tm == 0
    if nseg == 2:
        assert seg_blocks > 0
        seg = lambda i: jnp.where(i < seg_blocks, 0, 1)
    else:
        seg = lambda i: 0
    in_specs, args, lacking = [], [], []
    for r in rows:
        arr, width, cb, off = r if isinstance(r, tuple) else (r, r.shape[1], 0, 0)
        in_specs.append(pl.BlockSpec((tm, width), lambda i, cb=cb, off=off: (jnp.maximum(i + off, 0), cb)))
        args.append(arr)
        lacking.append(-off if off < 0 else 0)
    for kind, arr in consts:
        if kind == "seg":
            assert arr.shape[0] == nseg and arr.shape[1] == 1, arr.shape
            in_specs.append(pl.BlockSpec((None, 1, arr.shape[2]), lambda i: (seg(i), 0, 0)))
        else:
            in_specs.append(pl.BlockSpec(arr.shape, lambda i: (0, 0)))
        args.append(arr)
    out_shape = [jax.ShapeDtypeStruct((n_rows, w), dt) for w, dt in outs]
    out_specs = [pl.BlockSpec((tm, w), lambda i: (i, 0)) for w, _ in outs]
    out_shape += [jax.ShapeDtypeStruct((nseg, 1, w), F32) for w in accs]
    out_specs += [pl.BlockSpec((None, 1, w), lambda i: (seg(i), 0, 0)) for w in accs]
    n_in, n_out, n_acc = len(args), len(outs), len(accs)

    def kern(*refs):
        i = pl.program_id(0)
        ins = [r[...] for r in refs[:n_in]]
        for k, lack in enumerate(lacking):
            if lack:
                ins[k] = jnp.where(i >= lack, ins[k], jnp.zeros_like(ins[k]))
        res, terms = fn(*ins)
        for ref, v in zip(refs[n_in:n_in + n_out], res):
            ref[...] = v.astype(ref.dtype)
        if n_acc:
            sums = [_sum0(v) for v in terms]
            first = (i == 0) | (i == seg_blocks) if nseg == 2 else (i == 0)
            acc_refs = refs[n_in + n_out:]

            @pl.when(first)
            def _():
                for ref, v in zip(acc_refs, sums):
                    ref[...] = v

            @pl.when(jnp.logical_not(first))
            def _():
                for ref, v in zip(acc_refs, sums):
                    ref[...] += v

    res = pl.pallas_call(
        kern, name=name, grid=(n_rows // tm,), in_specs=in_specs, out_specs=out_specs, out_shape=out_shape,
        compiler_params=_params("arbitrary"),
    )(*args)
    return res


def _pre_fwd_fn(h, g, shift, scale):
    hh, _ = _rms(h)
    return (hh * g * (1.0 + scale) + shift,), ()


def _pre_bwd_fn(du, h, dres, g, scale):
    hh, r = _rms(h)
    n = hh * g
    dn = du * (1.0 + scale)
    dhh = dn * g
    dh = dres + r * (dhh - hh * jnp.mean(dhh * hh, axis=-1, keepdims=True))
    return (dh,), (du, du * n, dn * hh)


def _post_fwd_fn(weight, h, y, g, gate):
    yh, _ = _rms(y)
    return (h + weight * gate * (yh * g),), ()


def _out_post_fn(weight, y, h, g, gate):
    return (y,) + _post_fwd_fn(weight, h, y, g, gate)[0], ()


def _post_bwd_fn(weight, dh, y, g, gate):
    yh, r = _rms(y)
    dr = dh * weight
    dyh = dr * gate * g
    dy = r * (dyh - yh * jnp.mean(dyh * yh, axis=-1, keepdims=True))
    return (dy,), (dr * yh * g, dr * gate * yh)


def _glu_bwd_fn(ds, a, b):
    a = a.astype(F32)
    b = b.astype(F32)
    sg = _sig(a)
    da = ds * b * (sg * (1.0 + a * (1.0 - sg)))
    db = ds * (a * sg)
    return (jnp.concatenate([da, db], axis=1),), ()


def _loss_fn(y, t):
    diff = y - t
    return (diff * (1.0 / D_MODEL),), (diff * diff,)


def _ssd_y(yf, yb, xs, z, dvec):
    y = yf + yb + dvec * xs
    return y, y * _silu(z)


def _ssdgate_fwd_fn(yf, yb, xs, z, dvec, ng):
    _, yg = _ssd_y(yf, yb, xs, z, dvec)
    parts = []
    for g in range(SSD_GROUPS):
        sl = slice(g * 256, (g + 1) * 256)
        parts.append(_rms(yg[:, sl])[0])
    return (jnp.concatenate(parts, axis=1) * ng,), ()


def _ssdgate_bwd_fn(dyn, yf, yb, xs, z, dvec, ng):
    y, yg = _ssd_y(yf, yb, xs, z, dvec)
    dyg_parts, ygh_parts = [], []
    for g in range(SSD_GROUPS):
        sl = slice(g * 256, (g + 1) * 256)
        ygh, r = _rms(yg[:, sl])
        d = dyn[:, sl] * ng[:, sl]
        dyg_parts.append(r * (d - ygh * jnp.mean(d * ygh, axis=-1, keepdims=True)))
        ygh_parts.append(ygh)
    dyg = jnp.concatenate(dyg_parts, axis=1)
    ygh = jnp.concatenate(ygh_parts, axis=1)
    dy = dyg * _silu(z)
    dz = dyg * y * _dsilu(z)
    return (dy, dz), (dyn * ygh, dy * xs)


def _ln_stats(v):
    mu = jnp.mean(v, axis=-1, keepdims=True)
    vc = v - mu
    r = lax.rsqrt(jnp.mean(vc * vc, axis=-1, keepdims=True) + EPS)
    return vc * r, r


def _gm_act_fwd_fn(p, vg, vb):
    gu = _gelu(p[:, :GM_INNER])
    gvh, _ = _ln_stats(_gelu(p[:, GM_INNER:]))
    return (gu, gvh * vg + vb), ()


def _gm_act_bwd_fn(p, dgu, dgvn, vg):
    pu = p[:, :GM_INNER]
    pv = p[:, GM_INNER:]
    gv, dgelu_v = _gelu_and_grad(pv)
    gvh, r = _ln_stats(gv)
    dgvh = dgvn * vg
    dgv = r * (dgvh - jnp.mean(dgvh, axis=-1, keepdims=True) - gvh * jnp.mean(dgvh * gvh, axis=-1, keepdims=True))
    dp = jnp.concatenate([dgu * _dgelu(pu), dgv * dgelu_v], axis=1)
    return (dp,), (dgvn * gvh, dgvn)


def _mm(a, b, *, out_dtype, name, tm=1088, tn=1024, tk=1408, add=None, rhs_t=False, n=None, b_off=(0, 0)):
    m, k = a.shape
    col_blocked = b.ndim == 3
    if col_blocked:
        assert not rhs_t and n is None and b.shape[1] == k
        n, tn = b.shape[0] * b.shape[2], b.shape[2]
    elif n is None:
        n, k2 = b.shape if rhs_t else b.shape[::-1]
        assert k == k2
    tm, tn, tk = _pick(m, tm), _pick(n, tn, 128), _pick(k, tk, 128)
    o0, o1 = b_off
    nk = k // tk
    dims = _NT if rhs_t else ((1,), (0,))

    def kern(*refs):
        a_ref, b_ref = refs[:2]
        add_ref = refs[2] if add is not None else None
        o_ref = refs[3] if add is not None else refs[2]

        def finish(r):
            if add is not None:
                r = r + add_ref[...]
            o_ref[...] = r.astype(o_ref.dtype)

        p = _dot(a_ref[...], b_ref[...], dims)
        if nk == 1:
            finish(p)
            return
        acc_ref = refs[-1]
        kk = pl.program_id(2)

        @pl.when(kk == 0)
        def _():
            acc_ref[...] = p

        @pl.when((kk > 0) & (kk < nk - 1))
        def _():
            acc_ref[...] += p

        @pl.when(kk == nk - 1)
        def _():
            finish(acc_ref[...] + p)

    if col_blocked:
        b_spec = pl.BlockSpec((None, tk, tn), lambda i, j, kk: (j, kk, 0))
    elif rhs_t:
        b_spec = pl.BlockSpec((tn, tk), lambda i, j, kk: (j + o0, kk + o1))
    else:
        b_spec = pl.BlockSpec((tk, tn), lambda i, j, kk: (kk + o0, j + o1))
    in_specs = [pl.BlockSpec((tm, tk), lambda i, j, kk: (i, kk)), b_spec]
    args = [a, b]
    if add is not None:
        in_specs.append(pl.BlockSpec((tm, tn), lambda i, j, kk: (i, j)))
        args.append(add)
    return pl.pallas_call(
        kern, name=name, grid=(m // tm, n // tn, nk), in_specs=in_specs,
        out_specs=pl.BlockSpec((tm, tn), lambda i, j, kk: (i, j)),
        out_shape=jax.ShapeDtypeStruct((m, n), out_dtype),
        scratch_shapes=[pltpu.VMEM((tm, tn), F32)] if nk > 1 else [],
        compiler_params=_params("parallel", "parallel", "arbitrary"),
    )(*args)


def _mm_rows(a, b, fn, rows, consts, outs, accs=(), *, name, tm=544, tk=1408, rhs_t=False, n_ctx=0):
    halves = a.ndim == 3
    m, k = (a.shape[1], 2 * a.shape[2]) if halves else a.shape
    col_blocked = b.ndim == 3
    kb, nb = 1, None
    if col_blocked:
        assert rhs_t and b.shape[0] * b.shape[2] == k
        n, nb = b.shape[1], b.shape[2]
        kb = max(1, tk // nb)
        assert b.shape[0] % kb == 0
        tk = kb * nb
    else:
        n = b.shape[0] if rhs_t else b.shape[1]
    tm, tk = _pick(m, tm), _pick(k, tk, 128)
    nk = k // tk
    if halves:
        hb = k // 2 // tk
        a_spec = pl.BlockSpec((None, tm, tk), lambda i, kk: (kk // hb, i, kk % hb))
    else:
        a_spec = pl.BlockSpec((tm, tk), lambda i, kk: (i, kk))
    dims = _NT if rhs_t else ((1,), (0,))
    n_rows, n_const, n_out, n_acc = len(rows), len(consts), len(outs), len(accs)

    def kern(*refs):
        a_ref, b_ref = refs[:2]
        row_refs = refs[2:2 + n_rows]
        const_refs = refs[2 + n_rows:2 + n_rows + n_const]
        out_refs = refs[2 + n_rows + n_const:2 + n_rows + n_const + n_out]
        acc_refs = refs[2 + n_rows + n_const + n_out:2 + n_rows + n_const + n_out + n_acc]
        i, kk = pl.program_id(0), pl.program_id(1)

        def finish(p, rs=slice(None), r0=0):
            nr = p.shape[0]
            is_ctx = (i * tm + r0 + lax.broadcasted_iota(jnp.int32, (nr, 1), 0)) < n_ctx
            cvals = []
            for (kind, arr), ref in zip(consts, const_refs):
                if kind == "seg":
                    cvals.append(jnp.where(is_ctx, ref[0], ref[1]) if arr.shape[0] == 2 else ref[0])
                else:
                    cvals.append(ref[...])
            res, terms = fn(p, *[r[rs, :] for r in row_refs], *cvals)
            for ref, v in zip(out_refs, res):
                ref[rs, :] = v.astype(ref.dtype)
            for ref, v in zip(acc_refs, terms):
                s_all = _sum0(v)
                s_ctx = _sum0(jnp.where(is_ctx, v, 0.0)) if n_ctx else jnp.zeros_like(s_all)
                both = jnp.concatenate([s_ctx, s_all - s_ctx], axis=0)[:, None, :]

                @pl.when(i == 0)
                def _():
                    ref[...] = both

                @pl.when(i > 0)
                def _():
                    ref[...] += both

        if nk == 1 and n_acc == 0:
            nsub = 2 if tm % 32 == 0 else 1
            sub = tm // nsub
            for r in range(nsub):
                rs = slice(r * sub, (r + 1) * sub)
                finish(_dot(a_ref[rs, :], b_ref[...], dims), rs, r * sub)
            return
        if col_blocked:
            p = sum(_dot(a_ref[:, c * nb:(c + 1) * nb], b_ref[c], dims) for c in range(kb))
        else:
            p = _dot(a_ref[...], b_ref[...], dims)
        if nk == 1:
            finish(p)
            return
        scr = refs[-1]

        @pl.when(kk == 0)
        def _():
            scr[...] = p

        @pl.when((kk > 0) & (kk < nk - 1))
        def _():
            scr[...] += p

        @pl.when(kk == nk - 1)
        def _():
            finish(scr[...] + p)

    if col_blocked:
        b_spec = pl.BlockSpec((kb, n, nb), lambda i, kk: (kk, 0, 0))
    elif rhs_t:
        b_spec = pl.BlockSpec((n, tk), lambda i, kk: (0, kk))
    else:
        b_spec = pl.BlockSpec((tk, n), lambda i, kk: (kk, 0))
    in_specs = [a_spec, b_spec]
    in_specs += [pl.BlockSpec((tm, r.shape[1]), lambda i, kk: (i, 0)) for r in rows]
    for kind, arr in consts:
        in_specs.append(pl.BlockSpec(arr.shape, (lambda i, kk: (0, 0, 0)) if kind == "seg" else (lambda i, kk: (0, 0))))
    out_shape = [jax.ShapeDtypeStruct((m, w), dt) for w, dt in outs]
    out_specs = [pl.BlockSpec((tm, w), lambda i, kk: (i, 0)) for w, _ in outs]
    out_shape += [jax.ShapeDtypeStruct((2, 1, w), F32) for w in accs]
    out_specs += [pl.BlockSpec((2, 1, w), lambda i, kk: (0, 0, 0)) for w in accs]
    return pl.pallas_call(
        kern, name=name, grid=(m // tm, nk), in_specs=in_specs, out_specs=out_specs, out_shape=out_shape,
        scratch_shapes=[pltpu.VMEM((tm, n), F32)] if nk > 1 else [],
        compiler_params=_params("arbitrary", "arbitrary"),
    )(a, b, *rows, *[arr for _, arr in consts])


def _mm_glu(u, win_t, *, name, tm=1088, tn=1408):
    m, k = u.shape
    n = win_t.shape[0] // 2
    tm, tn = _pick(m, tm), _pick(n, tn, 128)
    nj = n // tn

    nsub = 2 if tm % 32 == 0 else 1
    sub = tm // nsub

    def kern(u_ref, wa_ref, wb_ref, s_ref, a_ref, b_ref):
        for r in range(nsub):
            rows = slice(r * sub, (r + 1) * sub)
            uu = u_ref[rows, :]
            a = _dot(uu, wa_ref[...], _NT)
            b = _dot(uu, wb_ref[...], _NT)
            s_ref[rows, :] = (_silu(a) * b).astype(BF16)
            a_ref[rows, :] = a.astype(BF16)
            b_ref[rows, :] = b.astype(BF16)

    ospec = pl.BlockSpec((tm, tn), lambda i, j: (i, j))
    return pl.pallas_call(
        kern, name=name, grid=(m // tm, nj),
        in_specs=[pl.BlockSpec((tm, k), lambda i, j: (i, 0)), pl.BlockSpec((tn, k), lambda i, j: (j, 0)),
                  pl.BlockSpec((tn, k), lambda i, j: (nj + j, 0))],
        out_specs=[ospec, ospec, ospec],
        out_shape=[jax.ShapeDtypeStruct((m, n), BF16)] * 3,
        compiler_params=_params("parallel", "parallel"),
    )(u, win_t, win_t)


def _mm_glu_bwd(dy, wout, a, b, *, name, tm=1088, tn=1408):
    m, k = dy.shape
    f = wout.shape[0]
    tm, tn = _pick(m, tm), _pick(f, tn, 128)
    nsub = 2 if tm % 32 == 0 else 1
    sub = tm // nsub

    def kern(dy_ref, w_ref, a_ref, b_ref, o_ref):
        for r in range(nsub):
            rs = slice(r * sub, (r + 1) * sub)
            ds = _dot(dy_ref[rs, :], w_ref[...], _NT)
            (dp,), _ = _glu_bwd_fn(ds, a_ref[rs, :], b_ref[rs, :])
            o_ref[0, rs, :] = dp[:, :tn].astype(BF16)
            o_ref[1, rs, :] = dp[:, tn:].astype(BF16)

    tile = pl.BlockSpec((tm, tn), lambda i, j: (i, j))
    return pl.pallas_call(
        kern, name=name, grid=(m // tm, f // tn),
        in_specs=[pl.BlockSpec((tm, k), lambda i, j: (i, 0)), pl.BlockSpec((tn, k), lambda i, j: (j, 0)), tile, tile],
        out_specs=pl.BlockSpec((2, tm, tn), lambda i, j: (0, i, j)),
        out_shape=jax.ShapeDtypeStruct((2, m, f), BF16),
        compiler_params=_params("parallel", "parallel"),
    )(dy, wout, a, b)


def _mm_tn(a, b, *, name, tm=1024, tn=1024, tk=2176, col_blocks=None, stack=None):
    extra, extra_specs, aliases = [], [], {}
    halves = a.ndim == 3
    t, m = (a.shape[1], 2 * a.shape[2]) if halves else a.shape
    t2, n = b.shape
    assert t == t2
    tm, tn, tk = _pick(m, tm, 128), _pick(n, tn, 128), _pick(t, tk)
    nk = t // tk
    if halves:
        hb = m // 2 // tm
        a_spec = pl.BlockSpec((None, tk, tm), lambda i, j, kk: (i // hb, kk, i % hb))
    else:
        a_spec = pl.BlockSpec((tk, tm), lambda i, j, kk: (kk, i))
    if col_blocks is None:
        def kern(a_ref, b_ref, o_ref):
            kk = pl.program_id(2)

            @pl.when(kk == 0)
            def _():
                o_ref[...] = jnp.zeros_like(o_ref)

            o_ref[...] += _dot(a_ref[...], b_ref[...], _TN)

        out_spec = pl.BlockSpec((tm, tn), lambda i, j, kk: (i, j))
        out_shape = jax.ShapeDtypeStruct((m, n), F32)
        scratch = []
    else:
        wb = n // col_blocks
        per = tn // wb
        assert tn % wb == 0 and wb % 8 == 0

        def kern(a_ref, b_ref, *rest):
            o_ref, acc_ref = rest[-2:]
            kk = pl.program_id(2)
            p = _dot(a_ref[...], b_ref[...], _TN)

            @pl.when(kk == 0)
            def _():
                acc_ref[...] = p

            @pl.when((kk > 0) & (kk < nk - 1))
            def _():
                acc_ref[...] += p

            @pl.when(kk == nk - 1)
            def _():
                r = acc_ref[...] + p if nk > 1 else p
                for c in range(per):
                    o_ref[c] = r[:, c * wb:(c + 1) * wb].astype(BF16)

        rows_total, row0, into = stack if stack is not None else (m, 0, None)
        assert row0 % tm == 0
        out_spec = pl.BlockSpec((per, tm, wb), lambda i, j, kk: (j, i + row0 // tm, 0))
        out_shape = jax.ShapeDtypeStruct((col_blocks, rows_total, wb), BF16)
        scratch = [pltpu.VMEM((tm, tn), F32)]
        if into is not None:
            extra, extra_specs, aliases = [into], [pl.BlockSpec(memory_space=pl.ANY)], {2: 0}

    return pl.pallas_call(
        kern, name=name, grid=(m // tm, n // tn, nk),
        in_specs=[a_spec, pl.BlockSpec((tk, tn), lambda i, j, kk: (kk, j))] + extra_specs,
        out_specs=out_spec, out_shape=out_shape, scratch_shapes=scratch, input_output_aliases=aliases,
        compiler_params=_params("parallel", "parallel", "arbitrary"),
    )(a, b, *extra)


def _mm_f32(a, b, *, name, silu_a=False, bias=None):
    m, k = a.shape
    n = b.shape[1]

    def kern(*refs):
        if bias is None:
            a_ref, b_ref, o_ref = refs
        else:
            a_ref, b_ref, bias_ref, o_ref = refs
        av = a_ref[...]
        if silu_a:
            av = _silu(av)
        r = jnp.dot(av, b_ref[...], preferred_element_type=F32, precision=HI)
        if bias is not None:
            r = r + bias_ref[...]
        o_ref[...] = r

    args = [a, b] + ([] if bias is None else [bias])
    return pl.pallas_call(kern, name=name, out_shape=jax.ShapeDtypeStruct((m, n), F32),
                          compiler_params=pltpu.CompilerParams(vmem_limit_bytes=VMEM_LIMIT_BYTES))(*args)


CONV_WIN = 32


def _conv_windows(n, n_ctx):
    assert n_ctx % CONV_WIN == 0 and n_ctx >= CONV_WIN and n - n_ctx >= CONV_WIN
    return (0, n_ctx - CONV_WIN // 2, n - CONV_WIN)


def _tap_outside(r0, s, n, n_ctx):
    t = r0 + lax.broadcasted_iota(jnp.int32, (CONV_WIN, 1), 0)
    lo = jnp.where(t < n_ctx, 0, n_ctx)
    hi = jnp.where(t < n_ctx, n_ctx, n)
    return jnp.where((t + s >= lo) & (t + s < hi), 0.0, 1.0)


def _rolled(v, s):
    return v if s == 0 else pltpu.roll(v, (-s) % v.shape[0], 0)


def _conv_fwd(xp, w8, b, *, n_ctx, name, cb=256):
    n, c = xp.shape
    half = SSD_CONV // 2

    def kern(x_ref, w_ref, b_ref, cpre_ref, act_ref):
        x = x_ref[...]
        acc = jnp.zeros_like(x) + b_ref[...]
        rolled = {}
        for k in range(SSD_CONV):
            rolled[k] = _rolled(x, k - half)
            acc = acc + rolled[k] * w_ref[k:k + 1, :]
        cpre_ref[...] = acc
        act_ref[...] = _silu(acc)
        for r0 in _conv_windows(n, n_ctx):
            rows = slice(r0, r0 + CONV_WIN)
            fix = acc[rows]
            for k in range(SSD_CONV):
                if k != half:
                    fix = fix - rolled[k][rows] * w_ref[k:k + 1, :] * _tap_outside(r0, k - half, n, n_ctx)
            cpre_ref[rows, :] = fix
            act_ref[rows, :] = _silu(fix)

    spec = pl.BlockSpec((n, cb), lambda j: (0, j))
    return pl.pallas_call(
        kern, name=name, grid=(c // cb,),
        in_specs=[spec, pl.BlockSpec((8, cb), lambda j: (0, j)), pl.BlockSpec((1, cb), lambda j: (0, j))],
        out_specs=[spec, spec], out_shape=[jax.ShapeDtypeStruct((n, c), F32)] * 2,
        compiler_params=_params("parallel"),
    )(xp, w8, b)


def _conv_bwd(d1, d2, cpre, xp, w8, *, n_ctx, name, cb=128):
    n, c = xp.shape
    half = SSD_CONV // 2

    def kern(d1_ref, d2_ref, cpre_ref, x_ref, w_ref, dx_ref, dw_ref, db_ref):
        g = (d1_ref[...] + d2_ref[...]) * _dsilu(cpre_ref[...])
        x = x_ref[...]
        dx = jnp.zeros_like(g)
        dw_ref[...] = jnp.zeros_like(dw_ref)
        g_rolled = {}
        for k in range(SSD_CONV):
            s = k - half
            g_rolled[k] = _rolled(g, -s)
            dx = dx + g_rolled[k] * w_ref[k:k + 1, :]
            xr = _rolled(x, s)
            dw = _sum0(g * xr)
            if s != 0:
                for r0 in _conv_windows(n, n_ctx):
                    rows = slice(r0, r0 + CONV_WIN)
                    dw = dw - _sum0(g[rows] * xr[rows] * _tap_outside(r0, s, n, n_ctx))
            dw_ref[k:k + 1, :] = dw
        dx_ref[...] = dx.astype(BF16)
        for r0 in _conv_windows(n, n_ctx):
            rows = slice(r0, r0 + CONV_WIN)
            fix = dx[rows]
            for k in range(SSD_CONV):
                if k != half:
                    fix = fix - g_rolled[k][rows] * w_ref[k:k + 1, :] * _tap_outside(r0, half - k, n, n_ctx)
            dx_ref[rows, :] = fix.astype(BF16)
        db_ref[...] = _sum0(g)

    spec = pl.BlockSpec((n, cb), lambda j: (0, j))
    return pl.pallas_call(
        kern, name=name, grid=(c // cb,),
        in_specs=[spec, spec, spec, spec, pl.BlockSpec((8, cb), lambda j: (0, j))],
        out_specs=[spec, pl.BlockSpec((8, cb), lambda j: (0, j)), pl.BlockSpec((1, cb), lambda j: (0, j))],
        out_shape=[jax.ShapeDtypeStruct((n, c), BF16), jax.ShapeDtypeStruct((8, c), F32),
                   jax.ShapeDtypeStruct((1, c), F32)],
        compiler_params=_params("parallel"),
    )(d1, d2, cpre, xp, w8)


def _chunk_of(s, nc, n_ctx_chunks, rev):
    if not rev:
        return s
    return jnp.where(s < n_ctx_chunks, n_ctx_chunks - 1 - s, nc - 1 - (s - n_ctx_chunks))


def _scan_common(dt_raw, dtT_raw, bias_r, bias_c, alog_r, alog_c, rev):
    ii = lax.broadcasted_iota(jnp.int32, (CHUNK, CHUNK), 0)
    jj = lax.broadcasted_iota(jnp.int32, (CHUNK, CHUNK), 1)
    tri = (jj >= ii) if rev else (jj <= ii)
    tri_t = (ii >= jj) if rev else (ii <= jj)
    a_r = -jnp.exp(alog_r)
    a_c = -jnp.exp(alog_c)
    dt = _softplus(dt_raw + bias_r)
    dt_t = _softplus(dtT_raw + bias_c)
    al = dt * a_r
    acum = _dot(tri.astype(F32), al, precision=HI)
    acum_t = _dot(dt_t * a_c, tri_t.astype(F32), precision=HI)
    atot = _sum0(al)
    return tri, tri_t, a_r, dt, acum, acum_t, atot


def _head_spread():
    return jnp.repeat(jnp.eye(SSD_HEADS, dtype=BF16), SSD_HEAD_DIM, axis=1)


def _dot_sel(v, sel):
    hi = v.astype(BF16)
    lo = (v - hi.astype(F32)).astype(BF16)
    return _dot(hi, sel) + _dot(lo, sel)


def _ssd_scan_fwd(xbc, dt_raw, dtT_raw, bias_r, bias_c, alog_r, alog_c, *, rev, n_ctx_chunks, name):
    n = xbc.shape[0]
    nc = n // CHUNK
    cidx = functools.partial(_chunk_of, nc=nc, n_ctx_chunks=n_ctx_chunks, rev=rev)

    def kern(xs_ref, b_ref, c_ref, dt_ref, dtT_ref, br_ref, bc_ref, ar_ref, ac_ref, e_ref, y_ref, hs_ref, h_scr):
        @pl.when(pl.program_id(0) == 0)
        def _():
            h_scr[...] = jnp.zeros_like(h_scr)

        tri, _, _, dt, acum, acum_t, atot = _scan_common(
            dt_ref[...], dtT_ref[...], br_ref[...], bc_ref[...], ar_ref[...], ac_ref[...], rev)
        etot = jnp.exp(atot)
        spread = lambda v: _dot_sel(v, e_ref[...])
        xdt_all = xs_ref[...] * spread(dt)
        eax = spread(jnp.exp(acum))
        xdw_all = xdt_all * spread(jnp.exp(atot - acum))
        hs_ref[...] = h_scr[...]
        for g in range(SSD_GROUPS):
            gs = slice(g * 256, (g + 1) * 256)
            bg = b_ref[:, g * SSD_STATE:(g + 1) * SSD_STATE].astype(BF16)
            cg = c_ref[:, g * SSD_STATE:(g + 1) * SSD_STATE].astype(BF16)
            cb = _dot(cg, bg, _NT)
            h4 = h_scr[gs, :]
            ys = []
            for k in range(SSD_HPG):
                h = g * SSD_HPG + k
                lmat = jnp.exp(jnp.where(tri, acum[:, h:h + 1] - acum_t[h:h + 1, :], NEG_BIG))
                xdt_h = xdt_all[:, h * SSD_HEAD_DIM:(h + 1) * SSD_HEAD_DIM].astype(BF16)
                ys.append(_dot((cb * lmat).astype(BF16), xdt_h))
            y_ref[:, gs] = jnp.concatenate(ys, axis=1) + _dot(cg, h4.astype(BF16), _NT) * eax[:, gs]
            s4 = _dot(xdw_all[:, gs].astype(BF16), bg, _TN)
            for k in range(SSD_HPG):
                h = g * SSD_HPG + k
                rs = slice(h * SSD_HEAD_DIM, (h + 1) * SSD_HEAD_DIM)
                h_scr[rs, :] = h4[k * SSD_HEAD_DIM:(k + 1) * SSD_HEAD_DIM] * etot[:, h:h + 1] + \
                    s4[k * SSD_HEAD_DIM:(k + 1) * SSD_HEAD_DIM]

    nh = SSD_HEADS
    small = lambda shape: pl.BlockSpec(shape, lambda s: (0, 0))
    return pl.pallas_call(
        kern, name=name, grid=(nc,),
        in_specs=[pl.BlockSpec((CHUNK, SSD_INNER), lambda s: (cidx(s), 0)),
                  pl.BlockSpec((CHUNK, 1024), lambda s: (cidx(s), 2)),
                  pl.BlockSpec((CHUNK, 1024), lambda s: (cidx(s), 3)),
                  pl.BlockSpec((CHUNK, nh), lambda s: (cidx(s), 0)),
                  pl.BlockSpec((nh, CHUNK), lambda s: (0, cidx(s))),
                  small((1, nh)), small((nh, 1)), small((1, nh)), small((nh, 1)), small((nh, SSD_INNER))],
        out_specs=[pl.BlockSpec((CHUNK, SSD_INNER), lambda s: (cidx(s), 0)),
                   pl.BlockSpec((None, SSD_INNER, SSD_STATE), lambda s: (s, 0, 0))],
        out_shape=[jax.ShapeDtypeStruct((n, SSD_INNER), F32),
                   jax.ShapeDtypeStruct((nc, SSD_INNER, SSD_STATE), F32)],
        scratch_shapes=[pltpu.VMEM((SSD_INNER, SSD_STATE), F32)],
        compiler_params=_params("arbitrary"),
    )(xbc, xbc, xbc, dt_raw, dtT_raw, bias_r, bias_c, alog_r, alog_c, _head_spread())


def _ssd_scan_bwd(dy, xbc, hs, dt_raw, dtT_raw, bias_r, bias_c, alog_r, alog_c, dvec, *, rev, n_ctx_chunks,
                  direct, name):
    n = xbc.shape[0]
    nc = n // CHUNK
    nh = SSD_HEADS
    step_of = lambda r: nc - 1 - r
    cidx = lambda r: _chunk_of(step_of(r), nc, n_ctx_chunks, rev)

    def kern(dy_ref, xs_ref, b_ref, c_ref, hs_ref, dt_ref, dtT_ref, br_ref, bc_ref, ar_ref, ac_ref, dv_ref,
             e_ref, et_ref, dx_ref, ddt_ref, dal_ref, dbias_ref, dh_scr):
        @pl.when(pl.program_id(0) == 0)
        def _():
            dh_scr[...] = jnp.zeros_like(dh_scr)
            dal_ref[...] = jnp.zeros_like(dal_ref)
            dbias_ref[...] = jnp.zeros_like(dbias_ref)

        tri, tri_t, a_r, dt, acum, acum_t, atot = _scan_common(
            dt_ref[...], dtT_ref[...], br_ref[...], bc_ref[...], ar_ref[...], ac_ref[...], rev)
        etot = jnp.exp(atot)
        spread = lambda v: _dot_sel(v, e_ref[...])
        gather = lambda v: _dot_sel(v, et_ref[...])
        xs_all = xs_ref[...]
        dy_all = dy_ref[...]
        dtx = spread(dt)
        eax = spread(jnp.exp(acum))
        decx = spread(jnp.exp(atot - acum))
        xdt_all = xs_all * dtx
        xdw_all = xdt_all * decx
        dyo_all = dy_all * eax
        lane = lax.broadcasted_iota(jnp.int32, (CHUNK, nh), 1)
        lane1 = lax.broadcasted_iota(jnp.int32, (1, nh), 1)
        sub = lax.broadcasted_iota(jnp.int32, (nh, CHUNK), 0)
        g_rows = jnp.zeros((CHUNK, nh), F32)
        g_cols = jnp.zeros((nh, CHUNK), F32)
        dtot = jnp.zeros((1, nh), F32)
        q_col, q_e, q_dt = [], [], []
        for g in range(SSD_GROUPS):
            gs = slice(g * 256, (g + 1) * 256)
            bg = b_ref[:, g * SSD_STATE:(g + 1) * SSD_STATE].astype(BF16)
            cg = c_ref[:, g * SSD_STATE:(g + 1) * SSD_STATE].astype(BF16)
            cb = _dot(cg, bg, _NT)
            hs4 = hs_ref[gs, :]
            dh4 = dh_scr[gs, :]
            hs4_bf = hs4.astype(BF16)
            dh4_bf = dh4.astype(BF16)
            dy4 = dy_all[:, gs]
            dy4_bf = dy4.astype(BF16)
            xdt4_bf = xdt_all[:, gs].astype(BF16)
            xdw4 = xdw_all[:, gs]
            xdw4_bf = xdw4.astype(BF16)
            dyo4_bf = dyo_all[:, gs].astype(BF16)
            yoff4 = _dot(cg, hs4_bf, _NT) * eax[:, gs]
            dcg = _dot(dyo4_bf, hs4_bf)
            dh_new4 = _dot(dyo4_bf, cg, _TN)
            bdh4 = _dot(bg, dh4_bf, _NT)
            dbg = _dot(xdw4_bf, dh4_bf)
            e4 = xdw4 * bdh4
            q_col.append(dy4 * yoff4 - e4)
            q_e.append(e4)
            hsum = jnp.sum(dh4 * hs4, axis=1, keepdims=True)
            dcb = jnp.zeros((CHUNK, CHUNK), F32)
            dxdts = []
            for k in range(SSD_HPG):
                h = g * SSD_HPG + k
                ks = slice(k * SSD_HEAD_DIM, (k + 1) * SSD_HEAD_DIM)
                lmat = jnp.exp(jnp.where(tri, acum[:, h:h + 1] - acum_t[h:h + 1, :], NEG_BIG))
                mf = cb * lmat
                dm = _dot(dy4_bf[:, ks], xdt4_bf[:, ks], _NT)
                dcb = dcb + dm * lmat
                gmat = dm * mf
                g_rows = g_rows + jnp.where(lane == h, jnp.sum(gmat, axis=1, keepdims=True), 0.0)
                g_cols = g_cols + jnp.where(sub == h, _sum0(gmat), 0.0)
                dxdts.append(_dot(mf.astype(BF16), dy4_bf[:, ks], _TN))
                et = etot[:, h:h + 1]
                dtot = dtot + jnp.where(lane1 == h, _sum0(hsum[ks]) * et, 0.0)
                dh_scr[h * SSD_HEAD_DIM:(h + 1) * SSD_HEAD_DIM, :] = dh4[ks] * et + dh_new4[ks]
            dxdt4 = jnp.concatenate(dxdts, axis=1) + bdh4 * decx[:, gs]
            q_dt.append(dxdt4 * xs_all[:, gs])
            dx4 = dxdt4 * dtx[:, gs]
            if direct:
                dx4 = dx4 + dy4 * dv_ref[:, gs]
            dcb_bf = dcb.astype(BF16)
            dx_ref[:, gs] = dx4
            dx_ref[:, SSD_INNER + g * SSD_STATE:SSD_INNER + (g + 1) * SSD_STATE] = dbg + _dot(dcb_bf, cg, _TN)
            dx_ref[:, SSD_INNER + 1024 + g * SSD_STATE:SSD_INNER + 1024 + (g + 1) * SSD_STATE] = \
                dcg + _dot(dcb_bf, bg)
        e_heads = gather(jnp.concatenate(q_e, axis=1))
        dacum = gather(jnp.concatenate(q_col, axis=1)) + g_rows - g_cols.T
        dal = _dot(tri_t.astype(F32), dacum, precision=HI) + dtot + _sum0(e_heads)
        ddt = gather(jnp.concatenate(q_dt, axis=1)) + dal * a_r
        ddt_raw = ddt * _sig(dt_ref[...] + br_ref[...])
        ddt_ref[...] = ddt_raw
        dal_ref[...] += _sum0(dal * dt) * a_r
        dbias_ref[...] += _sum0(ddt_raw)

    small = lambda shape: pl.BlockSpec(shape, lambda r: (0, 0))
    return pl.pallas_call(
        kern, name=name, grid=(nc,),
        in_specs=[pl.BlockSpec((CHUNK, SSD_INNER), lambda r: (cidx(r), 0)),
                  pl.BlockSpec((CHUNK, SSD_INNER), lambda r: (cidx(r), 0)),
                  pl.BlockSpec((CHUNK, 1024), lambda r: (cidx(r), 2)),
                  pl.BlockSpec((CHUNK, 1024), lambda r: (cidx(r), 3)),
                  pl.BlockSpec((None, SSD_INNER, SSD_STATE), lambda r: (step_of(r), 0, 0)),
                  pl.BlockSpec((CHUNK, nh), lambda r: (cidx(r), 0)),
                  pl.BlockSpec((nh, CHUNK), lambda r: (0, cidx(r))),
                  small((1, nh)), small((nh, 1)), small((1, nh)), small((nh, 1)), small((1, SSD_INNER)),
                  small((nh, SSD_INNER)), small((SSD_INNER, nh))],
        out_specs=[pl.BlockSpec((CHUNK, SSD_CONV_DIM), lambda r: (cidx(r), 0)),
                   pl.BlockSpec((CHUNK, nh), lambda r: (cidx(r), 0)),
                   small((1, nh)), small((1, nh))],
        out_shape=[jax.ShapeDtypeStruct((n, SSD_CONV_DIM), F32), jax.ShapeDtypeStruct((n, nh), F32),
                   jax.ShapeDtypeStruct((1, nh), F32), jax.ShapeDtypeStruct((1, nh), F32)],
        scratch_shapes=[pltpu.VMEM((SSD_INNER, SSD_STATE), F32)],
        compiler_params=_params("arbitrary"),
    )(dy, xbc, xbc, xbc, hs, dt_raw, dtT_raw, bias_r, bias_c, alog_r, alog_c, dvec, _head_spread(),
      _head_spread().T)


def _gm_spatial_fwd(gu, gvn, ws, bst, *, name):
    n = gu.shape[0]

    def kern(gu_ref, gv_ref, ws_ref, bs_ref, o_ref):
        for g in range(GM_GROUPS):
            sl = slice(g * GM_GROUP_DIM, (g + 1) * GM_GROUP_DIM)
            s = _dot(ws_ref[g], gv_ref[:, sl]) + bs_ref[:, g:g + 1]
            o_ref[:, sl] = (gu_ref[:, sl] * s).astype(BF16)

    spec = pl.BlockSpec((CHUNK, GM_INNER), lambda i: (i, 0))
    return pl.pallas_call(
        kern, name=name, grid=(n // CHUNK,),
        in_specs=[spec, spec, pl.BlockSpec(ws.shape, lambda i: (0, 0, 0)), pl.BlockSpec(bst.shape, lambda i: (0, 0))],
        out_specs=spec, out_shape=jax.ShapeDtypeStruct((n, GM_INNER), BF16),
        compiler_params=_params("parallel"),
    )(gu, gvn, ws, bst)


def _gm_spatial_bwd(dt, gu, gvn, ws, wst, bst, *, name):
    n = gu.shape[0]

    def kern(dt_ref, gu_ref, gv_ref, ws_ref, wst_ref, bs_ref, dgu_ref, dgv_ref, dws_ref, dbs_ref):
        @pl.when(pl.program_id(0) == 0)
        def _():
            dws_ref[...] = jnp.zeros_like(dws_ref)
            dbs_ref[...] = jnp.zeros_like(dbs_ref)

        lane = lax.broadcasted_iota(jnp.int32, (CHUNK, GM_GROUPS), 1)
        dbs = jnp.zeros((CHUNK, GM_GROUPS), F32)
        for g in range(GM_GROUPS):
            sl = slice(g * GM_GROUP_DIM, (g + 1) * GM_GROUP_DIM)
            gv = gv_ref[:, sl]
            s = _dot(ws_ref[g], gv) + bs_ref[:, g:g + 1]
            d = dt_ref[:, sl]
            dgu_ref[:, sl] = d * s
            ds = d * gu_ref[:, sl]
            ds_bf = ds.astype(BF16)
            dws_ref[g] += _dot(ds_bf, gv, _NT)
            dgv_ref[:, sl] = _dot(wst_ref[g], ds_bf)
            dbs = dbs + jnp.where(lane == g, jnp.sum(ds, axis=1, keepdims=True), 0.0)
        dbs_ref[...] += dbs

    spec = pl.BlockSpec((CHUNK, GM_INNER), lambda i: (i, 0))
    wspec = pl.BlockSpec(ws.shape, lambda i: (0, 0, 0))
    bspec = pl.BlockSpec(bst.shape, lambda i: (0, 0))
    return pl.pallas_call(
        kern, name=name, grid=(n // CHUNK,),
        in_specs=[spec, spec, spec, wspec, wspec, bspec],
        out_specs=[spec, spec, wspec, bspec],
        out_shape=[jax.ShapeDtypeStruct((n, GM_INNER), F32), jax.ShapeDtypeStruct((n, GM_INNER), F32),
                   jax.ShapeDtypeStruct(ws.shape, F32), jax.ShapeDtypeStruct(bst.shape, F32)],
        compiler_params=_params("arbitrary"),
    )(dt, gu, gvn, ws, wst, bst)


def _adamw(parts, w, m, v, *, name, tm=256, sel=(), into=None):
    ns, r, wd = parts.shape
    tm = _pick(r, tm, 8)
    tc = wd
    if tm < 64 and wd % 256 == 0:
        tm, tc = r, 256
    lead = len(sel)
    assert w.shape[lead:] == (r, wd) and lead == w.ndim - 2

    def kern(*refs):
        p_ref, w_ref, m_ref, v_ref = refs[:4]
        g_ref, d_ref, nm_ref, nv_ref = refs[-4:]
        g = p_ref[0].astype(F32)
        for s in range(1, ns):
            g = g + p_ref[s].astype(F32)
        m2 = ADAM_B1 * m_ref[...] + (1.0 - ADAM_B1) * g
        v2 = ADAM_B2 * v_ref[...] + (1.0 - ADAM_B2) * (g * g)
        m_hat = m2 / (1.0 - ADAM_B1 ** ADAM_STEP)
        v_hat = v2 / (1.0 - ADAM_B2 ** ADAM_STEP)
        g_ref[...] = g
        d_ref[...] = -ADAM_LR * (m_hat / (jnp.sqrt(v_hat) + ADAM_EPS) + ADAM_WD * w_ref[...])
        nm_ref[...] = m2
        nv_ref[...] = v2

    spec = pl.BlockSpec((None,) * lead + (tm, tc), lambda i, j: tuple(sel) + (i, j))
    extra, aliases = [], {}
    if into is not None:
        extra = list(into)
        aliases = {4 + k: k for k in range(4)}
    return pl.pallas_call(
        kern, name=name, grid=(r // tm, wd // tc),
        in_specs=[pl.BlockSpec((ns, tm, tc), lambda i, j: (0, i, j)), spec, spec, spec] +
                 [pl.BlockSpec(memory_space=pl.ANY)] * len(extra),
        out_specs=[spec] * 4, out_shape=[jax.ShapeDtypeStruct(w.shape, F32)] * 4,
        input_output_aliases=aliases,
        compiler_params=_params("parallel", "parallel"),
    )(parts, w, m, v, *extra)


def _sum_slots(parts, *, name, scale_by=None):
    ns, r, wd = parts.shape

    def kern(*refs):
        p_ref, o_ref = refs[0], refs[-1]
        g = p_ref[0]
        for s in range(1, ns):
            g = g + p_ref[s]
        if scale_by is not None:
            g = g * _dsilu(refs[1][...])
        o_ref[...] = g

    args = [parts] + ([] if scale_by is None else [scale_by])
    return pl.pallas_call(kern, name=name, out_shape=jax.ShapeDtypeStruct((r, wd), F32),
                          compiler_params=pltpu.CompilerParams(vmem_limit_bytes=VMEM_LIMIT_BYTES))(*args)


def _mesh_pos():
    x, y, c = lax.axis_index("x"), lax.axis_index("y"), lax.axis_index("c")
    return x, y, c, 4 * x + 2 * y + c


def _flip(x, y, c, f):
    fx, fy, fc = (f >> 2) & 1, (f >> 1) & 1, f & 1
    px = 1 - x if fx else x
    py = 1 - y if fy else y
    pc = 1 - c if fc else c
    return (px, py, pc), 4 * px + 2 * py + pc


_HBM_SPEC = pl.BlockSpec(memory_space=pltpu.HBM)


def _exchange(arrays, *, scatter, name):
    na = len(arrays)
    if scatter:
        out_shape = [jax.ShapeDtypeStruct(a.shape, a.dtype) for a in arrays]
    else:
        out_shape = [jax.ShapeDtypeStruct((NDEV,) + a.shape, a.dtype) for a in arrays]

    out_shape.append(jax.ShapeDtypeStruct((8, 128), F32))

    def body(*refs):
        ins, outs = refs[:na], refs[na:2 * na]
        send_sems, recv_sems, local_sems = refs[2 * na + 1:]
        refs[2 * na][...] = jnp.zeros((8, 128), F32)
        x, y, c, me = _mesh_pos()
        copies = []
        for i in range(na):
            src_own = ins[i].at[me] if scatter else ins[i]
            lc = pltpu.make_async_copy(src_own, outs[i].at[me], local_sems.at[i])
            lc.start()
            copies.append(lc)
        sends = []
        for f in range(1, NDEV):
            peer, pidx = _flip(x, y, c, f)
            for i in range(na):
                k = i * (NDEV - 1) + f - 1
                src = ins[i].at[pidx] if scatter else ins[i]
                cp = pltpu.make_async_remote_copy(
                    src_ref=src, dst_ref=outs[i].at[me], send_sem=send_sems.at[k], recv_sem=recv_sems.at[k],
                    device_id=peer, device_id_type=pl.DeviceIdType.MESH)
                cp.start()
                sends.append(cp)
        for f in range(1, NDEV):
            peer, pidx = _flip(x, y, c, f)
            for i in range(na):
                k = i * (NDEV - 1) + f - 1
                src = ins[i].at[pidx] if scatter else ins[i]
                pltpu.make_async_remote_copy(
                    src_ref=src, dst_ref=outs[i].at[pidx], send_sem=send_sems.at[k], recv_sem=recv_sems.at[k],
                    device_id=peer, device_id_type=pl.DeviceIdType.MESH).wait_recv()
        for cp in sends:
            cp.wait_send()
        for lc in copies:
            lc.wait()

    res = pl.pallas_call(
        body, name=name, out_shape=out_shape, in_specs=[_HBM_SPEC] * na,
        out_specs=[_HBM_SPEC] * na + [pl.BlockSpec(memory_space=pltpu.VMEM)],
        scratch_shapes=[pltpu.SemaphoreType.DMA((na * (NDEV - 1),)), pltpu.SemaphoreType.DMA((na * (NDEV - 1),)),
                        pltpu.SemaphoreType.DMA((na,))],
        compiler_params=pltpu.CompilerParams(has_side_effects=True),
    )(*arrays)
    return res[:na], res[na][0, 0]


_SEM_SPEC = pl.BlockSpec(memory_space=pltpu.SEMAPHORE)
_DATAFLOW = pltpu.SideEffectType.DATAFLOW_SIDE_EFFECTING


def _split_copies(srcs, lands, send_sems, recv_sems, scatter, arriving):
    x, y, c, me = _mesh_pos()
    copies = []
    for i in range(len(srcs)):
        for f in range(1, NDEV):
            peer, pidx = _flip(x, y, c, f)
            k = i * (NDEV - 1) + f - 1
            copies.append(pltpu.make_async_remote_copy(
                src_ref=srcs[i].at[pidx] if scatter else srcs[i], dst_ref=lands[i].at[pidx if arriving else me],
                send_sem=send_sems.at[k], recv_sem=recv_sems.at[k], device_id=peer,
                device_id_type=pl.DeviceIdType.MESH))
    return copies


def _exchange_start(srcs, lands, *, scatter, name):
    na = len(srcs)
    nsem = na * (NDEV - 1)

    def body(*refs):
        ins_src, ins_land = refs[:na], refs[na:2 * na]
        send_sems, recv_sems = refs[2 * na], refs[2 * na + 1]
        token = refs[-1]
        for cp in _split_copies(ins_src, ins_land, send_sems, recv_sems, scatter, False):
            cp.start()
        token[...] = jnp.zeros_like(token)

    thru = [pltpu.HBM(a.shape, a.dtype) for a in list(srcs) + list(lands)]
    res = pl.pallas_call(
        body, name=name,
        out_shape=(pltpu.SemaphoreType.DMA((nsem,)), pltpu.SemaphoreType.DMA((nsem,)), *thru,
                   jax.ShapeDtypeStruct((8, 128), F32)),
        in_specs=[_HBM_SPEC] * (2 * na),
        out_specs=(_SEM_SPEC, _SEM_SPEC, *([_HBM_SPEC] * (2 * na)), pl.BlockSpec(memory_space=pltpu.VMEM)),
        input_output_aliases={i: 2 + i for i in range(2 * na)},
        compiler_params=pltpu.CompilerParams(has_side_effects=_DATAFLOW),
    )(*[pltpu.with_memory_space_constraint(a, pltpu.HBM) for a in list(srcs) + list(lands)])
    send_sems, recv_sems = res[0], res[1]
    return send_sems, recv_sems, res[2:2 + na], res[2 + na:2 + 2 * na], res[-1][0, 0]


def _exchange_wait(send_sems, recv_sems, srcs, lands, after, *, scatter, name):
    na = len(srcs)

    def body(*refs):
        ins_src, ins_land = refs[:na], refs[na:2 * na]
        s_sems, r_sems = refs[2 * na], refs[2 * na + 1]
        for cp in _split_copies(ins_src, ins_land, s_sems, r_sems, scatter, False):
            cp.wait_send()
        for cp in _split_copies(ins_src, ins_land, s_sems, r_sems, scatter, True):
            cp.wait_recv()

    thru = [pltpu.HBM(a.shape, a.dtype) for a in list(srcs) + list(lands)]
    res = pl.pallas_call(
        body, name=name, out_shape=tuple(thru),
        in_specs=[_HBM_SPEC] * (2 * na) + [_SEM_SPEC, _SEM_SPEC, pl.BlockSpec(memory_space=pl.ANY)],
        out_specs=tuple([_HBM_SPEC] * (2 * na)),
        input_output_aliases={i: i for i in range(2 * na)},
        compiler_params=pltpu.CompilerParams(has_side_effects=_DATAFLOW),
    )(*srcs, *lands, send_sems, recv_sems, after)
    return res[na:]


def _landing(block, me):
    buf = lax.empty((NDEV,) + block.shape, block.dtype)
    return lax.dynamic_update_slice_in_dim(buf, block[None], me, axis=0)


def _seg_kw(nseg, n_ctx, tm):
    return dict(nseg=nseg, seg_blocks=(n_ctx // tm if nseg == 2 else 0))


def _ffn_fwd(tag, h, gpre, gpost, shift, scale, gate, w, *, nseg, n_ctx, tm):
    n = h.shape[0]
    kw = _seg_kw(nseg, n_ctx, tm)
    (u,) = _rowwise(tag + "_pre", _pre_fwd_fn, n, [h], [("full", gpre), ("seg", shift), ("seg", scale)],
                    [(D_MODEL, BF16)], tm=tm, **kw)
    if "early" in w:
        w.update(w.pop("early")(u))
    s, a, b = _mm_glu(u, w["win_t"], name=tag + "_glu")
    if "late" in w:
        w.update(w.pop("late")(s))
    y, ho = _mm_rows(s, w["wout"], functools.partial(_out_post_fn, 0.5), [h], [("full", gpost), ("seg", gate)],
                     [(D_MODEL, F32), (D_MODEL, F32)], name=tag + "_out", tk=FFN_DIM, n_ctx=n_ctx)
    return ho, dict(h=h, u=u, s=s, a=a, b=b, y=y)


def _ffn_bwd(tag, dho, sv, gpre, gpost, scale, gate, w, put, *, nseg, n_ctx, tm):
    n = dho.shape[0]
    kw = _seg_kw(nseg, n_ctx, tm)
    dy, dgate, dgpost = _rowwise(tag + "_postb", functools.partial(_post_bwd_fn, 0.5), n, [dho, sv["y"]],
                                 [("full", gpost), ("seg", gate)], [(D_MODEL, BF16)], [D_MODEL, D_MODEL], tm=tm, **kw)
    tok = put("w_out", _mm_tn(sv["s"], dy, name=tag + "_dwout", tm=1408, tn=1024, col_blocks=1))
    dp = _mm_glu_bwd(dy, w["wout"], sv["a"], sv["b"], name=tag + "_ds")
    tok2 = put("w_in", _mm_tn(dp, sv["u"], name=tag + "_dwin", tm=1408, tn=1024, col_blocks=1))
    for t in (tok, tok2):
        if t is not None:
            gpre = gpre + t
    dh, dshift, dscale, dgpre = _mm_rows(dp, w["win_t"], _pre_bwd_fn, [sv["h"], dho],
                                         [("full", gpre), ("seg", scale)], [(D_MODEL, F32)],
                                         [D_MODEL, D_MODEL, D_MODEL], name=tag + "_du", tk=FFN_DIM, n_ctx=n_ctx)
    return dh, None, dict(shift=dshift, scale=dscale, gate=dgate, gpre=dgpre, gpost=dgpost)


def _local_step(x, ctx, target, mods, norm_g, get_w, small, put_grad):
    t_len, n_ctx = x.shape[0], ctx.shape[0]
    n0 = t_len + n_ctx
    tm0 = _pick(n_ctx, 256, 8)
    tm1 = _pick(t_len, 512, 8)
    ncc = n_ctx // CHUNK
    g = {}

    def modrow(i, k, nseg):
        mc, mx = mods[i]
        if nseg == 2:
            return jnp.stack([mc[k], mx[k]])[:, None, :]
        return mx[k][None, None, :]

    pending = [None]

    def gvec(i, k):
        v = norm_g[i, k][None, :]
        if pending[0] is not None:
            v = v + pending[0]
            pending[0] = None
        return v

    xc = jnp.concatenate([ctx, x], axis=0)
    L0 = dict(nseg=2, n_ctx=n_ctx, tm=tm0)
    wts = dict(get_w("ffn00", xc))
    h1, sv_f01 = _ffn_fwd("l0f1", xc, gvec(0, 0), gvec(0, 1), modrow(0, 0, 2), modrow(0, 1, 2), modrow(0, 2, 2),
                          wts["ffn00"], **L0)
    kw0 = _seg_kw(2, n_ctx, tm0)
    (um0,) = _rowwise("l0m_pre", _pre_fwd_fn, n0, [h1], [("full", gvec(0, 2)), ("seg", modrow(0, 3, 2)),
                                                         ("seg", modrow(0, 4, 2))], [(D_MODEL, BF16)], tm=tm0, **kw0)
    wts.update(get_w("ssd", um0))
    win_ssd = wts["ssd_win_t"]
    nh = SSD_HEADS
    dt_blk = (SSD_INNER + SSD_CONV_DIM) // (2 * nh)
    z = _mm(um0, win_ssd, out_dtype=F32, name="ssd_z", rhs_t=True, n=SSD_INNER, tn=2048)
    xbc_pre = _mm(um0, win_ssd, out_dtype=F32, name="ssd_xbc", rhs_t=True, n=SSD_CONV_DIM, tn=2048,
                  b_off=(SSD_INNER // 2048, 0))
    dtr = _mm(um0, win_ssd, out_dtype=F32, name="ssd_dt", rhs_t=True, n=2 * nh, b_off=(dt_blk, 0))
    cpre, xbc = _conv_fwd(xbc_pre, small["conv_w8"], small["conv_b"], n_ctx=n_ctx, name="ssd_conv")
    nh = SSD_HEADS
    dt_dir = [dtr[:, :nh], dtr[:, nh:2 * nh]]
    dtT_dir = [d.T for d in dt_dir]
    bias_r = [small["dt_bias"][d][None, :] for d in range(2)]
    bias_c = [small["dt_bias"][d][:, None] for d in range(2)]
    alog_r = [small["a_log"][d][None, :] for d in range(2)]
    alog_c = [small["a_log"][d][:, None] for d in range(2)]
    ys, hss = [], []
    for d in range(2):
        yd, hsd = _ssd_scan_fwd(xbc, dt_dir[d], dtT_dir[d], bias_r[d], bias_c[d], alog_r[d], alog_c[d],
                                rev=(d == 1), n_ctx_chunks=ncc, name=f"ssd_scan{d}")
        ys.append(yd)
        hss.append(hsd)
    dvec = jnp.repeat(small["ssd_d"], SSD_HEAD_DIM)[None, :]
    ngv = small["ssd_norm_g"][None, :]
    gate_rows = [ys[0], ys[1], (xbc, SSD_INNER, 0, 0), z]
    off = n_ctx // tm0
    lat = lambda r: (r[0], r[1], r[2], off) if isinstance(r, tuple) else (r, r.shape[1], 0, off)
    (yn,) = _rowwise("ssd_gate", _ssdgate_fwd_fn, t_len, [lat(r) for r in gate_rows],
                     [("full", dvec), ("full", ngv)], [(SSD_INNER, BF16)], tm=tm0)
    h1x = h1[n_ctx:]
    L1 = dict(nseg=1, n_ctx=0, tm=_pick(t_len, 512, 8))
    if "late" in wts:
        wts.update(wts.pop("late")(yn))
    yo0, h2 = _mm_rows(yn, wts["ssd_wout"], functools.partial(_out_post_fn, 1.0), [h1x],
                       [("full", gvec(0, 3)), ("seg", modrow(0, 5, 1))], [(D_MODEL, F32), (D_MODEL, F32)],
                       name="ssd_out", tk=SSD_INNER)
    wts.update(get_w("ffn01", h2))
    h3, sv_f02 = _ffn_fwd("l0f2", h2, gvec(0, 4), gvec(0, 5), modrow(0, 6, 1), modrow(0, 7, 1), modrow(0, 8, 1),
                          wts["ffn01"], **L1)

    wts.update(get_w("ffn10", h3))
    h4, sv_f11 = _ffn_fwd("l1f1", h3, gvec(1, 0), gvec(1, 1), modrow(1, 0, 1), modrow(1, 1, 1), modrow(1, 2, 1),
                          wts["ffn10"], **L1)
    (um1,) = _rowwise("l1m_pre", _pre_fwd_fn, t_len, [h4], [("full", gvec(1, 2)), ("seg", modrow(1, 3, 1)),
                                                            ("seg", modrow(1, 4, 1))], [(D_MODEL, BF16)], tm=tm1)
    wts.update(get_w("gm", um1))
    p1 = _mm(um1, wts["gm_win"], out_dtype=F32, name="gm_in", tm=2048)
    vg = small["gm_v_g"][None, :]
    vb = small["gm_v_b"][None, :]
    gu, gvn = _rowwise("gm_act", _gm_act_fwd_fn, t_len, [p1], [("full", vg), ("full", vb)],
                       [(GM_INNER, F32), (GM_INNER, BF16)], tm=256)
    ws_bf = small["gm_w_s"].astype(BF16)
    wst_bf = jnp.swapaxes(small["gm_w_s"], 1, 2).astype(BF16)
    bst = small["gm_b_s"].T
    tgm = _gm_spatial_fwd(gu, gvn, ws_bf, bst, name="gm_spatial")
    yo1, h5 = _mm_rows(tgm, wts["gm_wout"], functools.partial(_out_post_fn, 1.0), [h4],
                       [("full", gvec(1, 3)), ("seg", modrow(1, 5, 1))], [(D_MODEL, F32), (D_MODEL, F32)],
                       name="gm_out", tk=GM_INNER)
    wts.update(get_w("ffn11", h5))
    h6, sv_f12 = _ffn_fwd("l1f2", h5, gvec(1, 4), gvec(1, 5), modrow(1, 6, 1), modrow(1, 7, 1), modrow(1, 8, 1),
                          wts["ffn11"], **L1)

    dh, loss_parts = _rowwise("loss", _loss_fn, t_len, [h6, target], [], [(D_MODEL, F32)], [D_MODEL], tm=tm1)

    zero = jnp.zeros((D_MODEL,), F32)
    dmx = [[zero] * N_MOD for _ in range(2)]
    dmc = [[zero] * N_MOD for _ in range(2)]
    dng = [[zero] * 6 for _ in range(2)]

    def put_mod(i, k, acc):
        if acc.shape[0] == 2:
            dmc[i][k] = dmc[i][k] + acc[0, 0]
            dmx[i][k] = dmx[i][k] + acc[1, 0]
        else:
            dmx[i][k] = dmx[i][k] + acc[0, 0]

    def put_g(i, k, acc):
        dng[i][k] = dng[i][k] + jnp.sum(acc[:, 0], axis=0)

    def ffn_back(tag, i, j, dho, sv, w, lay):
        nseg = lay["nseg"]
        base = 0 if j == 0 else 6
        gi = 0 if j == 0 else 4
        dh_in, pending[0], s = _ffn_bwd(tag, dho, sv, gvec(i, gi), gvec(i, gi + 1), modrow(i, base + 1, nseg),
                                        modrow(i, base + 2, nseg), w, functools.partial(put_grad, f"ffn{i}{j}"), **lay)
        put_mod(i, base, s["shift"])
        put_mod(i, base + 1, s["scale"])
        put_mod(i, base + 2, s["gate"])
        put_g(i, gi, s["gpre"])
        put_g(i, gi + 1, s["gpost"])
        return dh_in

    dh = ffn_back("l1f2", 1, 1, dh, sv_f12, wts["ffn11"], L1)
    dyo, dgate, dgp = _rowwise("l1m_postb", functools.partial(_post_bwd_fn, 1.0), t_len, [dh, yo1],
                               [("full", gvec(1, 3)), ("seg", modrow(1, 5, 1))], [(D_MODEL, BF16)],
                               [D_MODEL, D_MODEL], tm=tm1)
    put_mod(1, 5, dgate)
    put_g(1, 3, dgp)
    put_grad("gm", "w_out", _mm_tn(tgm, dyo, name="gm_dwout", tn=1024, col_blocks=1))
    dtg = _mm(dyo, wts["gm_wout"], out_dtype=F32, name="gm_dt", rhs_t=True)
    dgu, dgvn, dws, dbst = _gm_spatial_bwd(dtg, gu, gvn, ws_bf, wst_bf, bst, name="gm_spatialb")
    g["gm_w_s"] = dws
    g["gm_b_s"] = dbst.T
    dp1, dvg, dvb = _rowwise("gm_actb", _gm_act_bwd_fn, t_len, [p1, dgu, dgvn], [("full", vg)],
                             [(2 * GM_INNER, BF16)], [GM_INNER, GM_INNER], tm=256)
    g["gm_v_g"] = dvg[0, 0]
    g["gm_v_b"] = dvb[0, 0]
    pending[0] = put_grad("gm", "w_in", _mm_tn(um1, dp1, name="gm_dwin", tm=1024, col_blocks=NDEV))
    dh, dsh, dsc, dgp = _mm_rows(dp1, wts["gm_win"], _pre_bwd_fn, [h4, dh],
                                 [("full", gvec(1, 2)), ("seg", modrow(1, 4, 1))], [(D_MODEL, F32)],
                                 [D_MODEL, D_MODEL, D_MODEL], name="gm_dum", tk=2048, rhs_t=True)
    put_mod(1, 3, dsh)
    put_mod(1, 4, dsc)
    put_g(1, 2, dgp)
    dh = ffn_back("l1f1", 1, 0, dh, sv_f11, wts["ffn10"], L1)

    dh = ffn_back("l0f2", 0, 1, dh, sv_f02, wts["ffn01"], L1)
    dyo, dgate, dgp = _rowwise("l0m_postb", functools.partial(_post_bwd_fn, 1.0), t_len, [dh, yo0],
                               [("full", gvec(0, 3)), ("seg", modrow(0, 5, 1))], [(D_MODEL, BF16)],
                               [D_MODEL, D_MODEL], tm=tm1)
    put_mod(0, 5, dgate)
    put_g(0, 3, dgp)
    tok = put_grad("ssd", "w_out", _mm_tn(yn, dyo, name="ssd_dwout", tn=1024, col_blocks=1))
    dyn = _mm(dyo, wts["ssd_wout"], out_dtype=F32, name="ssd_dyn", rhs_t=True)
    dy_ssd, dz, dngv, ddv = _rowwise("ssd_gateb", _ssdgate_bwd_fn, n0,
                                     [(dyn, SSD_INNER, 0, -(n_ctx // tm0))] + gate_rows,
                                     [("full", dvec), ("full", ngv if tok is None else ngv + tok)],
                                     [(SSD_INNER, F32), (SSD_INNER, BF16)],
                                     [SSD_INNER, SSD_INNER], tm=tm0)
    g["ssd_norm_g"] = dngv[0, 0]
    g["ssd_D"] = jnp.sum(ddv[0, 0].reshape(SSD_HEADS, SSD_HEAD_DIM), axis=1)
    dxbcs, ddts, dalogs, dbiases = [], [], [], []
    for d in range(2):
        dxd, ddtd, dal, dbi = _ssd_scan_bwd(dy_ssd, xbc, hss[d], dt_dir[d], dtT_dir[d], bias_r[d], bias_c[d],
                                            alog_r[d], alog_c[d], dvec, rev=(d == 1), n_ctx_chunks=ncc,
                                            direct=(d == 0), name=f"ssd_scanb{d}")
        dxbcs.append(dxd)
        ddts.append(ddtd)
        dalogs.append(dal[0])
        dbiases.append(dbi[0])
    g["ssd_A_log"] = jnp.stack(dalogs)
    g["ssd_dt_bias"] = jnp.stack(dbiases)
    dxbc_pre, dcw8, dcb = _conv_bwd(dxbcs[0], dxbcs[1], cpre, xbc_pre, small["conv_w8"], n_ctx=n_ctx, name="ssd_convb")
    g["ssd_conv_w"] = dcw8[:SSD_CONV]
    g["ssd_conv_b"] = dcb[0]
    ddt_bf = jnp.concatenate([ddts[0], ddts[1]], axis=1).astype(BF16)
    n_in = SSD_INNER + SSD_CONV_DIM + 2 * nh
    dw_t = _mm_tn(dz, um0, name="ssd_dwz", col_blocks=1, stack=(n_in, 0, None))
    dw_t = _mm_tn(dxbc_pre, um0, name="ssd_dwxbc", col_blocks=1, stack=(n_in, SSD_INNER, dw_t))
    dw_t = _mm_tn(ddt_bf, um0, name="ssd_dwdt", col_blocks=1, stack=(n_in, SSD_INNER + SSD_CONV_DIM, dw_t))
    pending[0] = put_grad("ssd", "w_in", dw_t)
    dum0 = _mm(dz, win_ssd, out_dtype=F32, name="ssd_dum_z", tk=SSD_INNER, n=D_MODEL)
    dum0 = _mm(dxbc_pre, win_ssd, out_dtype=F32, name="ssd_dum_x", tk=SSD_INNER, n=D_MODEL,
               b_off=(SSD_INNER // SSD_INNER, 0), add=dum0)
    dum0 = _mm(ddt_bf, win_ssd, out_dtype=F32, name="ssd_dum_dt", tk=2 * nh, n=D_MODEL, b_off=(dt_blk, 0), add=dum0)
    dh0, dsh, dsc, dgp = _rowwise("l0m_preb", _pre_bwd_fn, n0, [dum0, h1, (dh, D_MODEL, 0, -(n_ctx // tm0))],
                                  [("full", gvec(0, 2)), ("seg", modrow(0, 4, 2))], [(D_MODEL, F32)],
                                  [D_MODEL, D_MODEL, D_MODEL], tm=tm0, **kw0)
    put_mod(0, 3, dsh)
    put_mod(0, 4, dsc)
    put_g(0, 2, dgp)
    dh0 = ffn_back("l0f1", 0, 0, dh0, sv_f01, wts["ffn00"], L0)
    grad_x = dh0[n_ctx:]
    g["norm_g"] = jnp.stack([jnp.stack(r) for r in dng])
    g["dmx"] = jnp.stack([jnp.concatenate(r) for r in dmx])
    g["dmc"] = jnp.stack([jnp.concatenate(r) for r in dmc])
    return loss_parts[0], grad_x, g


GROUPS = ("ffn00", "ssd", "ffn01", "ffn10", "gm", "ffn11")


TRANSPOSED_IN = ("ffn", "ssd")


def _is_transposed(group):
    return group.startswith(TRANSPOSED_IN)


def _mats_in(group, win_l):
    if _is_transposed(group):
        return {("win_t" if group.startswith("ffn") else group + "_win_t"): win_l.reshape(-1, win_l.shape[2])}
    return {group + "_win": win_l}


def _mats_out(group, wout_l):
    pre = "" if group.startswith("ffn") else group + "_"
    return {pre + "wout": wout_l.reshape(-1, wout_l.shape[2])}


def _group_mats(group, lands):
    m = {**_mats_in(group, lands[0]), **_mats_out(group, lands[1])}
    return {group: m} if group.startswith("ffn") else m


def _grad_blocks(which, grad):
    if grad.ndim == 3:
        return grad if grad.shape[0] == NDEV else grad.reshape(NDEV, grad.shape[1] // NDEV, grad.shape[2])
    if which == "w_in":
        k, n = grad.shape
        return jnp.transpose(grad.reshape(k, NDEV, n // NDEV), (1, 0, 2)).astype(BF16)
    return grad.reshape(NDEV, grad.shape[0] // NDEV, grad.shape[1]).astype(BF16)


def kernel(x, c, ctx, c_ctx, ada_w, ada_b, norm_g, ffn_w_in, ffn_w_out, ssd_w_in, ssd_conv_w, ssd_conv_b, ssd_dt_bias, ssd_A_log, ssd_D, ssd_norm_g, ssd_w_out, gm_w_in, gm_v_g, gm_v_b, gm_w_s, gm_b_s, gm_w_out, loss_target, m_c_ctx, m_ada_w, m_ada_b, m_norm_g, m_ffn_w_in, m_ffn_w_out, m_ssd_w_in, m_ssd_conv_w, m_ssd_conv_b, m_ssd_dt_bias, m_ssd_A_log, m_ssd_D, m_ssd_norm_g, m_ssd_w_out, m_gm_w_in, m_gm_v_g, m_gm_v_b, m_gm_w_s, m_gm_b_s, m_gm_w_out, v_c_ctx, v_ada_w, v_ada_b, v_norm_g, v_ffn_w_in, v_ffn_w_out, v_ssd_w_in, v_ssd_conv_w, v_ssd_conv_b, v_ssd_dt_bias, v_ssd_A_log, v_ssd_D, v_ssd_norm_g, v_ssd_w_out, v_gm_w_in, v_gm_v_g, v_gm_v_b, v_gm_w_s, v_gm_b_s, v_gm_w_out):
    me = 4 * lax.axis_index("x") + 2 * lax.axis_index("y") + lax.axis_index("c")
    d = D_MODEL
    ncol = N_MOD * d // NDEV

    small_pack = jnp.concatenate([c.reshape(-1), norm_g.reshape(-1), ssd_conv_w.reshape(-1),
                                  gm_v_g.reshape(-1), gm_v_b.reshape(-1)])[None, :]
    (sp,), _ = _exchange([small_pack], scatter=False, name="gather_small")
    sp = sp[:, 0]
    o = 0
    c_all = sp[:, o:o + d]; o += d
    ng_all = sp[:, o:o + 2 * 6 * 128].reshape(NDEV, 2, 6, 128); o += 2 * 6 * 128
    cw_all = sp[:, o:o + SSD_CONV * 512].reshape(NDEV, SSD_CONV, 512); o += SSD_CONV * 512
    vg_all = sp[:, o:o + 256]; o += 256
    vb_all = sp[:, o:o + 256]; o += 256
    norm_g_full = jnp.transpose(ng_all, (1, 2, 0, 3)).reshape(2, 6, d)
    conv_w_full = jnp.transpose(cw_all, (1, 0, 2)).reshape(SSD_CONV, SSD_CONV_DIM)
    gm_v_g_full = vg_all.reshape(-1)
    gm_v_b_full = vb_all.reshape(-1)

    c16 = jnp.concatenate([c_all, jnp.broadcast_to(c_ctx[None, :], (NDEV, d))], axis=0)
    ada_b_loc = lax.dynamic_slice_in_dim(ada_b, me * ncol, ncol, axis=1)
    mods_loc = jnp.stack([_mm_f32(c16, ada_w[i], name=f"ada_mod{i}", silu_a=True, bias=ada_b_loc[i][None, :])
                          for i in range(2)])
    (mods_all,), mods_done = _exchange([mods_loc], scatter=False, name="gather_mods")

    tr = lambda a: jnp.swapaxes(a, -1, -2)
    shard = {"ssd": (tr(ssd_w_in)[0], ssd_w_out[0]), "gm": (gm_w_in[0], gm_w_out[0])}
    for i in range(2):
        for j in range(2):
            shard[f"ffn{i}{j}"] = (tr(ffn_w_in)[i, j], ffn_w_out[i, j])
    apart = GROUPS[:2]
    units = []
    for grp in GROUPS:
        units += [(grp + "_in", grp, (0,)), (grp + "_out", grp, (1,))] if grp in apart else [(grp, grp, (0, 1))]
    gathers = {}
    started = mods_done
    for unit, grp, idx in units:
        srcs = [(shard[grp][k] + started).astype(BF16) for k in idx]
        st = _exchange_start(srcs, [_landing(s, me) for s in srcs], scatter=False, name="gather_start_" + unit)
        gathers[unit] = st[:4]
        started = st[4]

    def fetch(unit, after):
        return _exchange_wait(*gathers[unit], after, scatter=False, name="gather_wait_" + unit)

    def get_w(grp, after):
        if grp not in apart:
            return _group_mats(grp, fetch(grp, after))
        early = lambda later: _mats_in(grp, fetch(grp + "_in", later)[0])
        late = lambda later: _mats_out(grp, fetch(grp + "_out", later)[0])
        if grp.startswith("ffn"):
            return {grp: dict(early=early, late=late)}
        return dict(early(after), late=late)

    scatters = {}
    held = {}

    def put_grad(grp, which, grad):
        if grp in apart:
            unit, blocks = grp + "_" + which[2:], [_grad_blocks(which, grad)]
        else:
            held[grp, which] = _grad_blocks(which, grad)
            if (grp, "w_in") not in held or (grp, "w_out") not in held:
                return None
            unit, blocks = grp, [held[grp, "w_in"], held[grp, "w_out"]]
        own = [lax.dynamic_index_in_dim(b, me, axis=0, keepdims=False) for b in blocks]
        st = _exchange_start(blocks, [_landing(o_, me) for o_ in own], scatter=True, name="scatter_start_" + unit)
        scatters[unit] = st[:4]
        return st[4]

    mods_rows = jnp.transpose(mods_all, (1, 2, 0, 3)).reshape(2, 2 * NDEV, N_MOD * d) + started
    mx = lax.dynamic_index_in_dim(mods_rows, me, axis=1, keepdims=False).reshape(2, N_MOD, d)
    mc = mods_rows[:, NDEV].reshape(2, N_MOD, d)
    mods = [(mc[i], mx[i]) for i in range(2)]

    small = dict(conv_w8=jnp.pad(conv_w_full, ((0, 8 - SSD_CONV), (0, 0))), conv_b=ssd_conv_b, dt_bias=ssd_dt_bias[0],
                 a_log=ssd_A_log[0], ssd_d=ssd_D[0], ssd_norm_g=ssd_norm_g[0], gm_v_g=gm_v_g_full,
                 gm_v_b=gm_v_b_full, gm_w_s=gm_w_s[0], gm_b_s=gm_b_s[0])
    loss_parts, grad_x, g = _local_step(x[0], ctx[0], loss_target[0], mods, norm_g_full, get_w, small, put_grad)
    g["loss"] = (0.5 / d * jnp.sum(loss_parts)).reshape(1)

    whole = {"ffn_w_in": (tr(ffn_w_in), tr(m_ffn_w_in), tr(v_ffn_w_in)), "ffn_w_out": (ffn_w_out, m_ffn_w_out, v_ffn_w_out),
             "ssd_w_in": (tr(ssd_w_in), tr(m_ssd_w_in), tr(v_ssd_w_in)), "ssd_w_out": (ssd_w_out, m_ssd_w_out, v_ssd_w_out),
             "gm_w_in": (gm_w_in, m_gm_w_in, v_gm_w_in), "gm_w_out": (gm_w_out, m_gm_w_out, v_gm_w_out)}
    res = {}

    def update_units(some, after):
        for unit, grp, idx in some:
            parts = _exchange_wait(*scatters[unit], after, scatter=True, name="scatter_wait_" + unit)
            for k, p in zip(idx, parts):
                which = ("in", "out")[k]
                nm = ("ffn" if grp.startswith("ffn") else grp) + "_w_" + which
                sel = (int(grp[3]), int(grp[4])) if grp.startswith("ffn") else (0,)
                res[nm] = _adamw(p, *whole[nm], name=f"adamw_{grp}_{which}", sel=sel, into=res.get(nm))
                after = res[nm][0]
        return after

    sg_names = ["dmx", "dmc", "norm_g", "ssd_conv_w", "ssd_conv_b", "ssd_dt_bias", "ssd_A_log", "ssd_D", "ssd_norm_g",
                "gm_v_g", "gm_v_b", "gm_w_s", "gm_b_s", "loss"]
    sg_shapes = [g[n].shape for n in sg_names]
    flat = jnp.concatenate([g[n].reshape(-1) for n in sg_names])
    npack = flat.shape[0]
    pad = (-npack) % 1024
    flat = jnp.pad(flat, (0, pad)).reshape(-1, 128)
    sg_start = _exchange_start([flat], [_landing(flat, me)], scatter=False, name="small_grads_start")
    by_send = list(reversed(units))
    update_units(by_send[:4], jnp.stack([sg_start[4], grad_x[0, 0]]))
    early_done = jnp.stack([res[nm][0].reshape(-1)[-1] for nm in sorted(res)])
    (sg_all,) = _exchange_wait(*sg_start[:4], early_done, scatter=False, name="small_grads_wait")
    sg_sum = _sum_slots(sg_all, name="sum_small_grads").reshape(-1)[:npack]
    update_units(by_send[4:], sg_sum)
    sums = {}
    o = 0
    for n, shp in zip(sg_names, sg_shapes):
        sz = math.prod(shp)
        sums[n] = sg_sum[o:o + sz].reshape(shp)
        o += sz
    loss = sums["loss"][0]
    per_dev = sg_all.reshape(NDEV, -1)
    dmx_all =per_dev[:, :2 * N_MOD * d].reshape(NDEV, 2, N_MOD * d)
    dmc_all = per_dev[:, 2 * N_MOD * d:4 * N_MOD * d].reshape(NDEV, 2, N_MOD * d)

    (s16,) = _rowwise("ada_silu", lambda cc: ((_silu(cc),), ()), 2 * NDEV, [c16], [], [(d, F32)], tm=2 * NDEV)
    s16_t = s16.T
    g_ada_w, dcc_parts = [], []
    for i in range(2):
        rhs = jnp.concatenate([lax.dynamic_slice_in_dim(dmx_all[:, i], me * ncol, ncol, axis=1),
                               lax.dynamic_slice_in_dim(dmc_all[:, i], me * ncol, ncol, axis=1)], axis=0)
        g_ada_w.append(_mm_f32(s16_t, rhs, name=f"ada_dw{i}"))
        dmc_loc = lax.dynamic_slice_in_dim(sums["dmc"][i], me * ncol, ncol, axis=0)
        rhs_c = jnp.zeros((ncol, 128), F32).at[:, 0].set(dmc_loc)
        dcc_parts.append(_mm_f32(ada_w[i], rhs_c, name=f"ada_dcc{i}")[:, 0])
    g_ada_w = jnp.stack(g_ada_w)
    dcc_part = (dcc_parts[0] + dcc_parts[1]).reshape(8, 128)
    (dcc_all,), _ = _exchange([dcc_part], scatter=False, name="gather_dcc")
    g_c_ctx = _sum_slots(dcc_all, name="sum_dcc", scale_by=c_ctx.reshape(8, 128)).reshape(d)
    g_ada_b = sums["dmx"] + sums["dmc"]

    outs = _adamw(g_ada_w.reshape(1, -1, ncol), ada_w.reshape(-1, ncol), m_ada_w.reshape(-1, ncol),
                  v_ada_w.reshape(-1, ncol), name="adamw_ada_w")
    res["ada_w"] = [o_.reshape(ada_w.shape) for o_ in outs]

    loc = lambda a, ax, n: lax.dynamic_slice_in_dim(a, me * n, n, axis=ax)
    small_g = dict(c_ctx=g_c_ctx, ada_b=g_ada_b, norm_g=loc(sums["norm_g"], 2, 128),
                   ssd_conv_w=loc(sums["ssd_conv_w"], 1, 512)[None], ssd_conv_b=sums["ssd_conv_b"][None],
                   ssd_dt_bias=sums["ssd_dt_bias"][None], ssd_A_log=sums["ssd_A_log"][None], ssd_D=sums["ssd_D"][None],
                   ssd_norm_g=sums["ssd_norm_g"][None], gm_v_g=loc(sums["gm_v_g"], 0, 256)[None],
                   gm_v_b=loc(sums["gm_v_b"], 0, 256)[None], gm_b_s=sums["gm_b_s"][None])
    small_w = dict(c_ctx=(c_ctx, m_c_ctx, v_c_ctx), ada_b=(ada_b, m_ada_b, v_ada_b), norm_g=(norm_g, m_norm_g, v_norm_g),
                   ssd_conv_w=(ssd_conv_w, m_ssd_conv_w, v_ssd_conv_w), ssd_conv_b=(ssd_conv_b, m_ssd_conv_b, v_ssd_conv_b),
                   ssd_dt_bias=(ssd_dt_bias, m_ssd_dt_bias, v_ssd_dt_bias), ssd_A_log=(ssd_A_log, m_ssd_A_log, v_ssd_A_log),
                   ssd_D=(ssd_D, m_ssd_D, v_ssd_D), ssd_norm_g=(ssd_norm_g, m_ssd_norm_g, v_ssd_norm_g),
                   gm_v_g=(gm_v_g, m_gm_v_g, v_gm_v_g), gm_v_b=(gm_v_b, m_gm_v_b, v_gm_v_b),
                   gm_b_s=(gm_b_s, m_gm_b_s, v_gm_b_s))
    sn = list(small_w)
    flat2 = lambda a: a.reshape(-1, CHUNK)
    res["gm_w_s"] = [o_.reshape(gm_w_s.shape) for o_ in _adamw(
        flat2(sums["gm_w_s"])[None], flat2(gm_w_s), flat2(m_gm_w_s), flat2(v_gm_w_s), name="adamw_gm_w_s")]

    def pack(arrs):
        f = jnp.concatenate([a.reshape(-1) for a in arrs])
        return jnp.pad(f, (0, (-f.shape[0]) % (256 * 128))).reshape(-1, 128)

    pg = pack([small_g[n].reshape(small_w[n][0].shape) for n in sn])
    outs = _adamw(pg[None], pack([small_w[n][0] for n in sn]), pack([small_w[n][1] for n in sn]),
                  pack([small_w[n][2] for n in sn]), name="adamw_small")
    flat_outs = [o_.reshape(-1) for o_ in outs]
    o = 0
    for n in sn:
        shp = small_w[n][0].shape
        sz = math.prod(shp)
        res[n] = [fo[o:o + sz].reshape(shp) for fo in flat_outs]
        o += sz

    order = ["c_ctx", "ada_w", "ada_b", "norm_g", "ffn_w_in", "ffn_w_out", "ssd_w_in", "ssd_conv_w", "ssd_conv_b",
             "ssd_dt_bias", "ssd_A_log", "ssd_D", "ssd_norm_g", "ssd_w_out", "gm_w_in", "gm_v_g", "gm_v_b", "gm_w_s",
             "gm_b_s", "gm_w_out"]
    for nm in ("ffn_w_in", "ssd_w_in"):
        res[nm] = [tr(a) for a in res[nm]]
    result = [loss, grad_x[None]]
    for k in range(4):
        result += [res[n][k] for n in order]
    return tuple(result)
```
